```python
import math
import jax, jax.numpy as jnp
from jax import lax
import numpy as np

D_MODEL = 1024
BATCH = 32
SEQ = 2048
DEPTH = 1

GDN_HEADS = 8
GDN_DK = 128
GDN_DV = 128
SSD_HEADS = 16
SSD_HEADDIM = 64
SSD_GROUPS = 2
SSD_STATE = 128
CONV_K = 4
CHUNK = 64
D_FF = 2816
FFN_CONV_K = 3
EPS = 1e-6

GDN_QK = GDN_HEADS * GDN_DK
GDN_V = GDN_HEADS * GDN_DV
SSD_D = SSD_HEADS * SSD_HEADDIM
SSD_BC = SSD_GROUPS * SSD_STATE
SSD_HPG = SSD_HEADS // SSD_GROUPS
MIX_WIDTH = GDN_V + SSD_D
GDN_CONV_CH = 2 * GDN_QK + GDN_V
SSD_CONV_CH = SSD_D + 2 * SSD_BC
IN_SPLITS = (GDN_QK, GDN_QK, GDN_V, GDN_V, GDN_HEADS, GDN_HEADS,
             SSD_D, SSD_D, SSD_BC, SSD_BC, SSD_HEADS)
D_IN_PROJ = sum(IN_SPLITS)

kernel_name = "hybrid_gdn_ssd_parallel_heads_convffn"


def rms_norm(x, w):
    xf = x.astype(jnp.float32)
    y = xf * lax.rsqrt(jnp.mean(xf * xf, axis=-1, keepdims=True) + EPS)
    return (y * w.astype(jnp.float32)).astype(x.dtype)


def l2_normalize(x):
    xf = x.astype(jnp.float32)
    return xf * lax.rsqrt(jnp.sum(xf * xf, axis=-1, keepdims=True) + EPS)


def causal_dwconv(x, w, b=None):
    k_w, ch = w.shape
    y = lax.conv_general_dilated(
        x, w[:, None, :].astype(x.dtype), window_strides=(1,), padding=[(k_w - 1, 0)],
        dimension_numbers=("NWC", "WIO", "NWC"), feature_group_count=ch)
    if b is not None:
        y = y + b.astype(x.dtype)
    return y


def gdn_chunked(q, k, v, g, beta):
    bsz, s, h, dk = q.shape
    dv = v.shape[-1]
    n = s // CHUNK

    def chunk(t):
        return jnp.moveaxis(t.reshape(bsz, n, CHUNK, h, *t.shape[3:]), 3, 1)

    q, k, v, g, beta = (chunk(t) for t in (q, k, v, g, beta))
    gc = jnp.cumsum(g, axis=-1)
    causal = jnp.tril(jnp.ones((CHUNK, CHUNK), dtype=bool))
    strict = jnp.tril(jnp.ones((CHUNK, CHUNK), dtype=bool), -1)
    decay = jnp.exp(jnp.where(causal, gc[..., :, None] - gc[..., None, :], -jnp.inf))
    kb = k * beta[..., None]
    a_in = jnp.einsum("bhnid,bhnjd->bhnij", kb, k) * decay
    lmat = jnp.where(strict, a_in, 0.0) + jnp.eye(CHUNK, dtype=a_in.dtype)
    rhs = jnp.concatenate([v * beta[..., None], kb * jnp.exp(gc)[..., None]], axis=-1)
    sol = lax.linalg.triangular_solve(lmat, rhs, left_side=True, lower=True, unit_diagonal=True)
    u, w = sol[..., :dv], sol[..., dv:]
    qk = jnp.einsum("bhnid,bhnjd->bhnij", q, k) * decay
    q_dec = q * jnp.exp(gc)[..., None]
    k_dec = k * jnp.exp(gc[..., -1:] - gc)[..., None]
    g_last = jnp.exp(gc[..., -1])

    def step(state, xs):
        u_i, w_i, qk_i, q_i, k_i, gl = xs
        v_new = u_i - jnp.einsum("bhcd,bhde->bhce", w_i, state)
        o = jnp.einsum("bhcd,bhde->bhce", q_i, state) + jnp.einsum("bhij,bhje->bhie", qk_i, v_new)
        state = state * gl[..., None, None] + jnp.einsum("bhcd,bhce->bhde", k_i, v_new)
        return state, o

    xs = tuple(jnp.moveaxis(t, 2, 0) for t in (u, w, qk, q_dec, k_dec, g_last))
    s0 = jnp.zeros((bsz, h, dk, dv), dtype=q.dtype)
    _, o = lax.scan(step, s0, xs)
    return jnp.transpose(o, (1, 0, 3, 2, 4)).reshape(bsz, s, h, dv)


def ssd_chunked(x, dt, a_neg, bmat, cmat):
    bsz, s, grp, hg, p = x.shape
    nst = bmat.shape[-1]
    n = s // CHUNK
    xdt = x * dt[..., None]
    adt = dt * a_neg

    def chunk(t):
        return jnp.moveaxis(t.reshape(bsz, n, CHUNK, *t.shape[2:]), 1, 0)

    xs = tuple(chunk(t) for t in (xdt, adt, bmat, cmat))
    causal = jnp.tril(jnp.ones((CHUNK, CHUNK), dtype=bool))

    def step(state, xs):
        xdt_i, adt_i, b_i, c_i = xs
        acs = jnp.cumsum(jnp.moveaxis(adt_i, 1, -1), axis=-1)
        lmat = jnp.exp(jnp.where(causal, acs[..., :, None] - acs[..., None, :], -jnp.inf))
        cb = jnp.einsum("blgn,bsgn->bgls", c_i, b_i)
        y_diag = jnp.einsum("bgls,bghls,bsghp->blghp", cb, lmat, xdt_i)
        y_off = jnp.einsum("blgn,bghpn,bghl->blghp", c_i, state, jnp.exp(acs))
        decay_s = jnp.exp(acs[..., -1:] - acs)
        state = state * jnp.exp(acs[..., -1])[..., None, None] + jnp.einsum(
            "bsgn,bghs,bsghp->bghpn", b_i, decay_s, xdt_i)
        return state, y_diag + y_off

    s0 = jnp.zeros((bsz, grp, hg, p, nst), dtype=x.dtype)
    _, y = lax.scan(step, s0, xs)
    return jnp.moveaxis(y, 0, 1).reshape(bsz, s, grp, hg, p)


def hybrid_layer(x, pre_mix_norm, w_in, gdn_conv_w, gdn_a_log, gdn_dt_bias, gdn_norm_w,
                 ssd_conv_w, ssd_conv_b, ssd_a_log, ssd_dt_bias, ssd_d, ssd_norm_w,
                 w_out, post_mix_norm, pre_ffn_norm, w_up, ffn_conv_w, ffn_conv_b,
                 w_down, post_ffn_norm):
    f32 = jnp.float32
    bsz, s, _ = x.shape
    h = rms_norm(x, pre_mix_norm)
    proj = h @ w_in
    offsets = np.cumsum(IN_SPLITS)[:-1].tolist()
    q, k, v, z_a, b_a, a_a, z_s, x_s, b_s, c_s, dt_s = jnp.split(proj, offsets, axis=-1)

    qkv = jax.nn.silu(causal_dwconv(jnp.concatenate([q, k, v], axis=-1), gdn_conv_w))
    q, k, v = jnp.split(qkv, [GDN_QK, 2 * GDN_QK], axis=-1)
    q = l2_normalize(q.reshape(bsz, s, GDN_HEADS, GDN_DK)) * (GDN_DK ** -0.5)
    k = l2_normalize(k.reshape(bsz, s, GDN_HEADS, GDN_DK))
    v = v.reshape(bsz, s, GDN_HEADS, GDN_DV).astype(f32)
    beta = jax.nn.sigmoid(b_a.astype(f32))
    g = -jnp.exp(gdn_a_log.astype(f32)) * jax.nn.softplus(a_a.astype(f32) + gdn_dt_bias.astype(f32))
    o_a = gdn_chunked(q, k, v, g, beta)
    o_a = rms_norm(o_a, gdn_norm_w) * jax.nn.silu(z_a.reshape(bsz, s, GDN_HEADS, GDN_DV).astype(f32))
    o_a = o_a.reshape(bsz, s, GDN_V).astype(x.dtype)

    xbc = jax.nn.silu(causal_dwconv(jnp.concatenate([x_s, b_s, c_s], axis=-1), ssd_conv_w, ssd_conv_b))
    xs_, bm, cm = jnp.split(xbc, [SSD_D, SSD_D + SSD_BC], axis=-1)
    xs_ = xs_.reshape(bsz, s, SSD_GROUPS, SSD_HPG, SSD_HEADDIM).astype(f32)
    bm = bm.reshape(bsz, s, SSD_GROUPS, SSD_STATE).astype(f32)
    cm = cm.reshape(bsz, s, SSD_GROUPS, SSD_STATE).astype(f32)
    dt = jax.nn.softplus(dt_s.astype(f32) + ssd_dt_bias.astype(f32)).reshape(bsz, s, SSD_GROUPS, SSD_HPG)
    a_neg = -jnp.exp(ssd_a_log.astype(f32)).reshape(SSD_GROUPS, SSD_HPG)
    y = ssd_chunked(xs_, dt, a_neg, bm, cm)
    y = y + ssd_d.astype(f32).reshape(SSD_GROUPS, SSD_HPG)[..., None] * xs_
    y = y.reshape(bsz, s, SSD_D) * jax.nn.silu(z_s.astype(f32))
    y = rms_norm(y.reshape(bsz, s, SSD_GROUPS, SSD_D // SSD_GROUPS),
                 ssd_norm_w.reshape(SSD_GROUPS, SSD_D // SSD_GROUPS))
    o_s = y.reshape(bsz, s, SSD_D).astype(x.dtype)

    mix = jnp.concatenate([o_a, o_s], axis=-1) @ w_out
    x = x + rms_norm(mix, post_mix_norm)

    h = rms_norm(x, pre_ffn_norm)
    u = causal_dwconv(h @ w_up, ffn_conv_w, ffn_conv_b)
    gate, up = jnp.split(u, 2, axis=-1)
    f = (jax.nn.silu(gate) * up) @ w_down
    return x + rms_norm(f, post_ffn_norm)


def _fwd_setup_inputs(seed: int = 0) -> dict:
    key = jax.random.key(seed)
    ks = jax.random.split(key, 24)
    f32 = jnp.float32
    L = DEPTH

    def nrm(k, shape, scale):
        return jax.random.normal(k, shape, f32) * scale

    def gain(k, shape):
        return 1.0 + 0.05 * jax.random.normal(k, shape, f32)

    def dt_bias(k, shape):
        dt = jnp.exp(jax.random.uniform(k, shape, f32, math.log(1e-3), math.log(1e-1)))
        return dt + jnp.log(-jnp.expm1(-dt))

    def a_log(k, shape):
        return jnp.log(jax.random.uniform(k, shape, f32, 1.0, 16.0))

    return {
        "x": nrm(ks[0], (BATCH, SEQ, D_MODEL), 1.0),
        "pre_mix_norm": gain(ks[1], (L, D_MODEL)),
        "w_in": nrm(ks[2], (L, D_MODEL, D_IN_PROJ), D_MODEL ** -0.5),
        "gdn_conv_w": nrm(ks[3], (L, CONV_K, GDN_CONV_CH), CONV_K ** -0.5),
        "gdn_a_log": a_log(ks[4], (L, GDN_HEADS)),
        "gdn_dt_bias": dt_bias(ks[5], (L, GDN_HEADS)),
        "gdn_norm_w": gain(ks[6], (L, GDN_DV)),
        "ssd_conv_w": nrm(ks[7], (L, CONV_K, SSD_CONV_CH), CONV_K ** -0.5),
        "ssd_conv_b": nrm(ks[8], (L, SSD_CONV_CH), 0.02),
        "ssd_a_log": a_log(ks[9], (L, SSD_HEADS)),
        "ssd_dt_bias": dt_bias(ks[10], (L, SSD_HEADS)),
        "ssd_d": gain(ks[11], (L, SSD_HEADS)),
        "ssd_norm_w": gain(ks[12], (L, SSD_D)),
        "w_out": nrm(ks[13], (L, MIX_WIDTH, D_MODEL), MIX_WIDTH ** -0.5),
        "post_mix_norm": gain(ks[14], (L, D_MODEL)),
        "pre_ffn_norm": gain(ks[15], (L, D_MODEL)),
        "w_up": nrm(ks[16], (L, D_MODEL, 2 * D_FF), D_MODEL ** -0.5),
        "ffn_conv_w": nrm(ks[17], (L, FFN_CONV_K, 2 * D_FF), FFN_CONV_K ** -0.5),
        "ffn_conv_b": nrm(ks[18], (L, 2 * D_FF), 0.02),
        "w_down": nrm(ks[19], (L, D_FF, D_MODEL), D_FF ** -0.5),
        "post_ffn_norm": gain(ks[20], (L, D_MODEL)),
    }


def _fwd_reference(x, pre_mix_norm, w_in, gdn_conv_w, gdn_a_log, gdn_dt_bias, gdn_norm_w,
              ssd_conv_w, ssd_conv_b, ssd_a_log, ssd_dt_bias, ssd_d, ssd_norm_w,
              w_out, post_mix_norm, pre_ffn_norm, w_up, ffn_conv_w, ffn_conv_b,
              w_down, post_ffn_norm):
    for l in range(DEPTH):
        x = hybrid_layer(x, pre_mix_norm[l], w_in[l], gdn_conv_w[l], gdn_a_log[l], gdn_dt_bias[l],
                         gdn_norm_w[l], ssd_conv_w[l], ssd_conv_b[l], ssd_a_log[l], ssd_dt_bias[l],
                         ssd_d[l], ssd_norm_w[l], w_out[l], post_mix_norm[l], pre_ffn_norm[l],
                         w_up[l], ffn_conv_w[l], ffn_conv_b[l], w_down[l], post_ffn_norm[l])
    return x


import jax as _jax
import jax.numpy as _jnp

TWIN_FORMAT = 'train_step'
FWD_PARAMS = ['x', 'pre_mix_norm', 'w_in', 'gdn_conv_w', 'gdn_a_log', 'gdn_dt_bias', 'gdn_norm_w', 'ssd_conv_w', 'ssd_conv_b', 'ssd_a_log', 'ssd_dt_bias', 'ssd_d', 'ssd_norm_w', 'w_out', 'post_mix_norm', 'pre_ffn_norm', 'w_up', 'ffn_conv_w', 'ffn_conv_b', 'w_down', 'post_ffn_norm']
TWIN_WEIGHTS = ['pre_mix_norm', 'w_in', 'gdn_conv_w', 'gdn_a_log', 'gdn_dt_bias', 'gdn_norm_w', 'ssd_conv_w', 'ssd_conv_b', 'ssd_a_log', 'ssd_dt_bias', 'ssd_d', 'ssd_norm_w', 'w_out', 'post_mix_norm', 'pre_ffn_norm', 'w_up', 'ffn_conv_w', 'ffn_conv_b', 'w_down', 'post_ffn_norm']
TWIN_DIFF_INPUT = 'x'
TWIN_INPUTS = ['x', 'pre_mix_norm', 'w_in', 'gdn_conv_w', 'gdn_a_log', 'gdn_dt_bias', 'gdn_norm_w', 'ssd_conv_w', 'ssd_conv_b', 'ssd_a_log', 'ssd_dt_bias', 'ssd_d', 'ssd_norm_w', 'w_out', 'post_mix_norm', 'pre_ffn_norm', 'w_up', 'ffn_conv_w', 'ffn_conv_b', 'w_down', 'post_ffn_norm', 'loss_target', 'm_pre_mix_norm', 'm_w_in', 'm_gdn_conv_w', 'm_gdn_a_log', 'm_gdn_dt_bias', 'm_gdn_norm_w', 'm_ssd_conv_w', 'm_ssd_conv_b', 'm_ssd_a_log', 'm_ssd_dt_bias', 'm_ssd_d', 'm_ssd_norm_w', 'm_w_out', 'm_post_mix_norm', 'm_pre_ffn_norm', 'm_w_up', 'm_ffn_conv_w', 'm_ffn_conv_b', 'm_w_down', 'm_post_ffn_norm', 'v_pre_mix_norm', 'v_w_in', 'v_gdn_conv_w', 'v_gdn_a_log', 'v_gdn_dt_bias', 'v_gdn_norm_w', 'v_ssd_conv_w', 'v_ssd_conv_b', 'v_ssd_a_log', 'v_ssd_dt_bias', 'v_ssd_d', 'v_ssd_norm_w', 'v_w_out', 'v_post_mix_norm', 'v_pre_ffn_norm', 'v_w_up', 'v_ffn_conv_w', 'v_ffn_conv_b', 'v_w_down', 'v_post_ffn_norm']
TWIN_OUTPUTS = ['loss', 'grad_x', 'grad_pre_mix_norm', 'grad_w_in', 'grad_gdn_conv_w', 'grad_gdn_a_log', 'grad_gdn_dt_bias', 'grad_gdn_norm_w', 'grad_ssd_conv_w', 'grad_ssd_conv_b', 'grad_ssd_a_log', 'grad_ssd_dt_bias', 'grad_ssd_d', 'grad_ssd_norm_w', 'grad_w_out', 'grad_post_mix_norm', 'grad_pre_ffn_norm', 'grad_w_up', 'grad_ffn_conv_w', 'grad_ffn_conv_b', 'grad_w_down', 'grad_post_ffn_norm', 'delta_pre_mix_norm', 'delta_w_in', 'delta_gdn_conv_w', 'delta_gdn_a_log', 'delta_gdn_dt_bias', 'delta_gdn_norm_w', 'delta_ssd_conv_w', 'delta_ssd_conv_b', 'delta_ssd_a_log', 'delta_ssd_dt_bias', 'delta_ssd_d', 'delta_ssd_norm_w', 'delta_w_out', 'delta_post_mix_norm', 'delta_pre_ffn_norm', 'delta_w_up', 'delta_ffn_conv_w', 'delta_ffn_conv_b', 'delta_w_down', 'delta_post_ffn_norm', 'new_m_pre_mix_norm', 'new_m_w_in', 'new_m_gdn_conv_w', 'new_m_gdn_a_log', 'new_m_gdn_dt_bias', 'new_m_gdn_norm_w', 'new_m_ssd_conv_w', 'new_m_ssd_conv_b', 'new_m_ssd_a_log', 'new_m_ssd_dt_bias', 'new_m_ssd_d', 'new_m_ssd_norm_w', 'new_m_w_out', 'new_m_post_mix_norm', 'new_m_pre_ffn_norm', 'new_m_w_up', 'new_m_ffn_conv_w', 'new_m_ffn_conv_b', 'new_m_w_down', 'new_m_post_ffn_norm', 'new_v_pre_mix_norm', 'new_v_w_in', 'new_v_gdn_conv_w', 'new_v_gdn_a_log', 'new_v_gdn_dt_bias', 'new_v_gdn_norm_w', 'new_v_ssd_conv_w', 'new_v_ssd_conv_b', 'new_v_ssd_a_log', 'new_v_ssd_dt_bias', 'new_v_ssd_d', 'new_v_ssd_norm_w', 'new_v_w_out', 'new_v_post_mix_norm', 'new_v_pre_ffn_norm', 'new_v_w_up', 'new_v_ffn_conv_w', 'new_v_ffn_conv_b', 'new_v_w_down', 'new_v_post_ffn_norm']
TWIN_LEAF_KINDS = {'loss': 'loss', 'grad_x': 'grad_x', 'grad_pre_mix_norm': 'grad_w', 'grad_w_in': 'grad_w', 'grad_gdn_conv_w': 'grad_w', 'grad_gdn_a_log': 'grad_w', 'grad_gdn_dt_bias': 'grad_w', 'grad_gdn_norm_w': 'grad_w', 'grad_ssd_conv_w': 'grad_w', 'grad_ssd_conv_b': 'grad_w', 'grad_ssd_a_log': 'grad_w', 'grad_ssd_dt_bias': 'grad_w', 'grad_ssd_d': 'grad_w', 'grad_ssd_norm_w': 'grad_w', 'grad_w_out': 'grad_w', 'grad_post_mix_norm': 'grad_w', 'grad_pre_ffn_norm': 'grad_w', 'grad_w_up': 'grad_w', 'grad_ffn_conv_w': 'grad_w', 'grad_ffn_conv_b': 'grad_w', 'grad_w_down': 'grad_w', 'grad_post_ffn_norm': 'grad_w', 'delta_pre_mix_norm': 'delta_w', 'delta_w_in': 'delta_w', 'delta_gdn_conv_w': 'delta_w', 'delta_gdn_a_log': 'delta_w', 'delta_gdn_dt_bias': 'delta_w', 'delta_gdn_norm_w': 'delta_w', 'delta_ssd_conv_w': 'delta_w', 'delta_ssd_conv_b': 'delta_w', 'delta_ssd_a_log': 'delta_w', 'delta_ssd_dt_bias': 'delta_w', 'delta_ssd_d': 'delta_w', 'delta_ssd_norm_w': 'delta_w', 'delta_w_out': 'delta_w', 'delta_post_mix_norm': 'delta_w', 'delta_pre_ffn_norm': 'delta_w', 'delta_w_up': 'delta_w', 'delta_ffn_conv_w': 'delta_w', 'delta_ffn_conv_b': 'delta_w', 'delta_w_down': 'delta_w', 'delta_post_ffn_norm': 'delta_w', 'new_m_pre_mix_norm': 'new_m', 'new_m_w_in': 'new_m', 'new_m_gdn_conv_w': 'new_m', 'new_m_gdn_a_log': 'new_m', 'new_m_gdn_dt_bias': 'new_m', 'new_m_gdn_norm_w': 'new_m', 'new_m_ssd_conv_w': 'new_m', 'new_m_ssd_conv_b': 'new_m', 'new_m_ssd_a_log': 'new_m', 'new_m_ssd_dt_bias': 'new_m', 'new_m_ssd_d': 'new_m', 'new_m_ssd_norm_w': 'new_m', 'new_m_w_out': 'new_m', 'new_m_post_mix_norm': 'new_m', 'new_m_pre_ffn_norm': 'new_m', 'new_m_w_up': 'new_m', 'new_m_ffn_conv_w': 'new_m', 'new_m_ffn_conv_b': 'new_m', 'new_m_w_down': 'new_m', 'new_m_post_ffn_norm': 'new_m', 'new_v_pre_mix_norm': 'new_v', 'new_v_w_in': 'new_v', 'new_v_gdn_conv_w': 'new_v', 'new_v_gdn_a_log': 'new_v', 'new_v_gdn_dt_bias': 'new_v', 'new_v_gdn_norm_w': 'new_v', 'new_v_ssd_conv_w': 'new_v', 'new_v_ssd_conv_b': 'new_v', 'new_v_ssd_a_log': 'new_v', 'new_v_ssd_dt_bias': 'new_v', 'new_v_ssd_d': 'new_v', 'new_v_ssd_norm_w': 'new_v', 'new_v_w_out': 'new_v', 'new_v_post_mix_norm': 'new_v', 'new_v_pre_ffn_norm': 'new_v', 'new_v_w_up': 'new_v', 'new_v_ffn_conv_w': 'new_v', 'new_v_ffn_conv_b': 'new_v', 'new_v_w_down': 'new_v', 'new_v_post_ffn_norm': 'new_v'}


def _forward(args):
    return _fwd_reference(*[args[k] for k in FWD_PARAMS])


def _output_shape():
    out = _jax.eval_shape(lambda: _forward(_fwd_setup_inputs(0)))
    return out.shape, out.dtype

N_MICROBATCH = 1
ADAM_LR = 0.001
ADAM_B1 = 0.9
ADAM_B2 = 0.999
ADAM_EPS = 1e-08
ADAM_WD = 0.01
ADAM_STEP = 10
PER_EXAMPLE_BATCH_AXIS = {'x': 0, 'loss_target': 0}
SHARED_INPUTS = []
_WEIGHT_DTYPES = {'pre_mix_norm': _jnp.float32, 'w_in': _jnp.float32, 'gdn_conv_w': _jnp.float32, 'gdn_a_log': _jnp.float32, 'gdn_dt_bias': _jnp.float32, 'gdn_norm_w': _jnp.float32, 'ssd_conv_w': _jnp.float32, 'ssd_conv_b': _jnp.float32, 'ssd_a_log': _jnp.float32, 'ssd_dt_bias': _jnp.float32, 'ssd_d': _jnp.float32, 'ssd_norm_w': _jnp.float32, 'w_out': _jnp.float32, 'post_mix_norm': _jnp.float32, 'pre_ffn_norm': _jnp.float32, 'w_up': _jnp.float32, 'ffn_conv_w': _jnp.float32, 'ffn_conv_b': _jnp.float32, 'w_down': _jnp.float32, 'post_ffn_norm': _jnp.float32}
MOMENT_SCALE = {'pre_mix_norm': 9.940586e-01, 'w_in': 4.043219e-01, 'gdn_conv_w': 2.823989e-01, 'gdn_a_log': 4.587088e+00, 'gdn_dt_bias': 4.530855e+00, 'gdn_norm_w': 1.551994e+00, 'ssd_conv_w': 5.633119e-01, 'ssd_conv_b': 1.410764e+00, 'ssd_a_log': 5.971761e+00, 'ssd_dt_bias': 1.423195e+00, 'ssd_d': 3.033066e+00, 'ssd_norm_w': 8.196592e-01, 'w_out': 9.359603e-01, 'post_mix_norm': 6.398618e+01, 'pre_ffn_norm': 8.510617e-01, 'w_up': 3.443318e-01, 'ffn_conv_w': 3.824397e-01, 'ffn_conv_b': 1.927220e+00, 'w_down': 6.566915e-01, 'post_ffn_norm': 6.421972e+01}


def _to_microbatches(a, axis):
    t = _jnp.moveaxis(a, axis, 0)
    t = t.reshape((N_MICROBATCH, t.shape[0] // N_MICROBATCH) + t.shape[1:])
    return _jnp.moveaxis(t, 1, axis + 1)


def setup_inputs(seed: int = 0) -> dict:
    inp = _fwd_setup_inputs(seed)
    key = _jax.random.fold_in(_jax.random.key(seed), 7919)
    shape, _ = _output_shape()
    out = dict(inp)
    out["loss_target"] = _jax.random.normal(_jax.random.fold_in(key, 0), shape, _jnp.float32)
    for i, name in enumerate(TWIN_WEIGHTS):
        w = inp[name].astype(_jnp.float32)
        if MOMENT_SCALE is None:
            s = _jnp.sqrt(_jnp.mean(_jnp.square(w)) + 1e-30)
        else:
            s = MOMENT_SCALE[name]
        km, kv = _jax.random.split(_jax.random.fold_in(key, i + 1))
        out[name] = w
        out["m_" + name] = s * _jax.random.normal(km, w.shape, _jnp.float32)
        out["v_" + name] = (s * s) * _jax.random.uniform(kv, w.shape, _jnp.float32, 0.5, 1.5)
    if N_MICROBATCH > 1:
        for name, axis in PER_EXAMPLE_BATCH_AXIS.items():
            out[name] = _to_microbatches(out[name], axis)
    return {'x': out['x'], 'pre_mix_norm': out['pre_mix_norm'], 'w_in': out['w_in'], 'gdn_conv_w': out['gdn_conv_w'], 'gdn_a_log': out['gdn_a_log'], 'gdn_dt_bias': out['gdn_dt_bias'], 'gdn_norm_w': out['gdn_norm_w'], 'ssd_conv_w': out['ssd_conv_w'], 'ssd_conv_b': out['ssd_conv_b'], 'ssd_a_log': out['ssd_a_log'], 'ssd_dt_bias': out['ssd_dt_bias'], 'ssd_d': out['ssd_d'], 'ssd_norm_w': out['ssd_norm_w'], 'w_out': out['w_out'], 'post_mix_norm': out['post_mix_norm'], 'pre_ffn_norm': out['pre_ffn_norm'], 'w_up': out['w_up'], 'ffn_conv_w': out['ffn_conv_w'], 'ffn_conv_b': out['ffn_conv_b'], 'w_down': out['w_down'], 'post_ffn_norm': out['post_ffn_norm'], 'loss_target': out['loss_target'], 'm_pre_mix_norm': out['m_pre_mix_norm'], 'm_w_in': out['m_w_in'], 'm_gdn_conv_w': out['m_gdn_conv_w'], 'm_gdn_a_log': out['m_gdn_a_log'], 'm_gdn_dt_bias': out['m_gdn_dt_bias'], 'm_gdn_norm_w': out['m_gdn_norm_w'], 'm_ssd_conv_w': out['m_ssd_conv_w'], 'm_ssd_conv_b': out['m_ssd_conv_b'], 'm_ssd_a_log': out['m_ssd_a_log'], 'm_ssd_dt_bias': out['m_ssd_dt_bias'], 'm_ssd_d': out['m_ssd_d'], 'm_ssd_norm_w': out['m_ssd_norm_w'], 'm_w_out': out['m_w_out'], 'm_post_mix_norm': out['m_post_mix_norm'], 'm_pre_ffn_norm': out['m_pre_ffn_norm'], 'm_w_up': out['m_w_up'], 'm_ffn_conv_w': out['m_ffn_conv_w'], 'm_ffn_conv_b': out['m_ffn_conv_b'], 'm_w_down': out['m_w_down'], 'm_post_ffn_norm': out['m_post_ffn_norm'], 'v_pre_mix_norm': out['v_pre_mix_norm'], 'v_w_in': out['v_w_in'], 'v_gdn_conv_w': out['v_gdn_conv_w'], 'v_gdn_a_log': out['v_gdn_a_log'], 'v_gdn_dt_bias': out['v_gdn_dt_bias'], 'v_gdn_norm_w': out['v_gdn_norm_w'], 'v_ssd_conv_w': out['v_ssd_conv_w'], 'v_ssd_conv_b': out['v_ssd_conv_b'], 'v_ssd_a_log': out['v_ssd_a_log'], 'v_ssd_dt_bias': out['v_ssd_dt_bias'], 'v_ssd_d': out['v_ssd_d'], 'v_ssd_norm_w': out['v_ssd_norm_w'], 'v_w_out': out['v_w_out'], 'v_post_mix_norm': out['v_post_mix_norm'], 'v_pre_ffn_norm': out['v_pre_ffn_norm'], 'v_w_up': out['v_w_up'], 'v_ffn_conv_w': out['v_ffn_conv_w'], 'v_ffn_conv_b': out['v_ffn_conv_b'], 'v_w_down': out['v_w_down'], 'v_post_ffn_norm': out['v_post_ffn_norm']}


def _loss(weights, diff, rest, loss_target):
    with _jax.named_scope("forward"):
        args = {**rest, TWIN_DIFF_INPUT: diff, **{k: w.astype(_WEIGHT_DTYPES[k]) for k, w in weights.items()}}
        y = _forward(args)
    with _jax.named_scope("loss_head"):
        err = _jnp.square(y.astype(_jnp.float32) - loss_target)
        return 0.5 * _jnp.sum(_jnp.mean(err, axis=-1)) if err.ndim else 0.5 * err


def _adamw(w, g, m, v):
    m = ADAM_B1 * m + (1.0 - ADAM_B1) * g
    v = ADAM_B2 * v + (1.0 - ADAM_B2) * _jnp.square(g)
    m_hat = m / (1.0 - ADAM_B1 ** ADAM_STEP)
    v_hat = v / (1.0 - ADAM_B2 ** ADAM_STEP)
    delta = -ADAM_LR * (m_hat / (_jnp.sqrt(v_hat) + ADAM_EPS) + ADAM_WD * w)
    return delta, m, v


def reference(x, pre_mix_norm, w_in, gdn_conv_w, gdn_a_log, gdn_dt_bias, gdn_norm_w, ssd_conv_w, ssd_conv_b, ssd_a_log, ssd_dt_bias, ssd_d, ssd_norm_w, w_out, post_mix_norm, pre_ffn_norm, w_up, ffn_conv_w, ffn_conv_b, w_down, post_ffn_norm, loss_target, m_pre_mix_norm, m_w_in, m_gdn_conv_w, m_gdn_a_log, m_gdn_dt_bias, m_gdn_norm_w, m_ssd_conv_w, m_ssd_conv_b, m_ssd_a_log, m_ssd_dt_bias, m_ssd_d, m_ssd_norm_w, m_w_out, m_post_mix_norm, m_pre_ffn_norm, m_w_up, m_ffn_conv_w, m_ffn_conv_b, m_w_down, m_post_ffn_norm, v_pre_mix_norm, v_w_in, v_gdn_conv_w, v_gdn_a_log, v_gdn_dt_bias, v_gdn_norm_w, v_ssd_conv_w, v_ssd_conv_b, v_ssd_a_log, v_ssd_dt_bias, v_ssd_d, v_ssd_norm_w, v_w_out, v_post_mix_norm, v_pre_ffn_norm, v_w_up, v_ffn_conv_w, v_ffn_conv_b, v_w_down, v_post_ffn_norm):
    given = dict(x=x, pre_mix_norm=pre_mix_norm, w_in=w_in, gdn_conv_w=gdn_conv_w, gdn_a_log=gdn_a_log, gdn_dt_bias=gdn_dt_bias, gdn_norm_w=gdn_norm_w, ssd_conv_w=ssd_conv_w, ssd_conv_b=ssd_conv_b, ssd_a_log=ssd_a_log, ssd_dt_bias=ssd_dt_bias, ssd_d=ssd_d, ssd_norm_w=ssd_norm_w, w_out=w_out, post_mix_norm=post_mix_norm, pre_ffn_norm=pre_ffn_norm, w_up=w_up, ffn_conv_w=ffn_conv_w, ffn_conv_b=ffn_conv_b, w_down=w_down, post_ffn_norm=post_ffn_norm, loss_target=loss_target, m_pre_mix_norm=m_pre_mix_norm, m_w_in=m_w_in, m_gdn_conv_w=m_gdn_conv_w, m_gdn_a_log=m_gdn_a_log, m_gdn_dt_bias=m_gdn_dt_bias, m_gdn_norm_w=m_gdn_norm_w, m_ssd_conv_w=m_ssd_conv_w, m_ssd_conv_b=m_ssd_conv_b, m_ssd_a_log=m_ssd_a_log, m_ssd_dt_bias=m_ssd_dt_bias, m_ssd_d=m_ssd_d, m_ssd_norm_w=m_ssd_norm_w, m_w_out=m_w_out, m_post_mix_norm=m_post_mix_norm, m_pre_ffn_norm=m_pre_ffn_norm, m_w_up=m_w_up, m_ffn_conv_w=m_ffn_conv_w, m_ffn_conv_b=m_ffn_conv_b, m_w_down=m_w_down, m_post_ffn_norm=m_post_ffn_norm, v_pre_mix_norm=v_pre_mix_norm, v_w_in=v_w_in, v_gdn_conv_w=v_gdn_conv_w, v_gdn_a_log=v_gdn_a_log, v_gdn_dt_bias=v_gdn_dt_bias, v_gdn_norm_w=v_gdn_norm_w, v_ssd_conv_w=v_ssd_conv_w, v_ssd_conv_b=v_ssd_conv_b, v_ssd_a_log=v_ssd_a_log, v_ssd_dt_bias=v_ssd_dt_bias, v_ssd_d=v_ssd_d, v_ssd_norm_w=v_ssd_norm_w, v_w_out=v_w_out, v_post_mix_norm=v_post_mix_norm, v_pre_ffn_norm=v_pre_ffn_norm, v_w_up=v_w_up, v_ffn_conv_w=v_ffn_conv_w, v_ffn_conv_b=v_ffn_conv_b, v_w_down=v_w_down, v_post_ffn_norm=v_post_ffn_norm)
    weights = {n: given[n] for n in TWIN_WEIGHTS}
    shared = {n: given[n] for n in SHARED_INPUTS}
    per_example = {n: given[n] for n in ['x']}
    grad_fn = _jax.value_and_grad(_loss, argnums=(0, 1))

    def one_microbatch(ex, loss_target):
        ex = dict(ex)
        diff = ex.pop(TWIN_DIFF_INPUT)
        return grad_fn(weights, diff, {**shared, **ex}, loss_target)

    if N_MICROBATCH == 1:
        loss, (grad_w, grad_x) = one_microbatch(per_example, given["loss_target"])
    else:
        def body(carry, xs):
            loss_sum, grad_sum = carry
            l_k, (gw_k, gx_k) = one_microbatch(xs[0], xs[1])
            with _jax.named_scope("update"):
                return (loss_sum + l_k, _jax.tree.map(_jnp.add, grad_sum, gw_k)), gx_k

        init = (_jnp.zeros((), _jnp.float32), _jax.tree.map(_jnp.zeros_like, weights))
        (loss, grad_w), grad_x = _jax.lax.scan(body, init, (per_example, given["loss_target"]))
    with _jax.named_scope("update"):
        delta_w, new_m, new_v = {}, {}, {}
        for n in TWIN_WEIGHTS:
            delta_w[n], new_m[n], new_v[n] = _adamw(weights[n], grad_w[n], given["m_" + n], given["v_" + n])
    return (loss, grad_x, *[grad_w[n] for n in TWIN_WEIGHTS], *[delta_w[n] for n in TWIN_WEIGHTS],
            *[new_m[n] for n in TWIN_WEIGHTS], *[new_v[n] for n in TWIN_WEIGHTS])
```

```python
import functools

import jax
import jax.numpy as jnp
from jax import lax
from jax.experimental import pallas as pl
from jax.experimental.pallas import tpu as pltpu

F32 = jnp.float32
BF16 = jnp.bfloat16
MXU_DTYPE = jnp.bfloat16
HIGHEST = lax.Precision.HIGHEST
VMEM_LIMIT_V7X = 48 * 1024 * 1024
SUBLANES = 8
LANES = 128

D_MODEL = 1024
GDN_HEADS = 8
GDN_DK = 128
SSD_HEADS = 16
SSD_P = 64
SSD_GROUPS = 2
SSD_HPG = 8
SSD_N = 128
CHUNK = 64
D_FF = 2816
EPS = 1e-6
N_DEV = 8
PROJ_W = 7168
SMALL_CB = 52
D_IN = 6688

ADAM_LR = 0.001
ADAM_B1 = 0.9
ADAM_B2 = 0.999
ADAM_EPS = 1e-08
ADAM_WD = 0.01
ADAM_STEP = 10

NN = (((1,), (0,)), ((), ()))
NT = (((1,), (1,)), ((), ()))
TN = (((0,), (0,)), ((), ()))


def _pcall(body, **kw):
    return pl.pallas_call(body, **kw)


def _mm(a, b, dims=NN):
    return lax.dot_general(a.astype(MXU_DTYPE), b.astype(MXU_DTYPE), dims, preferred_element_type=F32)


def _mmx(a, b, dims=NN):
    return lax.dot_general(a, b, dims, precision=HIGHEST, preferred_element_type=F32)


def _sigmoid(x):
    return 1.0 / (1.0 + jnp.exp(-x))


def _softplus(x):
    return jnp.maximum(x, 0.0) + jnp.log(1.0 + jnp.exp(-jnp.abs(x)))


def _dsilu(x, s):
    return s * (1.0 + x * (1.0 - s))


def _rowsum(x):
    return jnp.sum(x, axis=1, keepdims=True)


def _colsum(x):
    return jnp.sum(x, axis=0, keepdims=True)


def _pick(dim, pref):
    if dim <= pref:
        return dim
    best = None
    t = LANES
    while t <= pref:
        if dim % t == 0:
            best = t
        t += LANES
    return dim if best is None else best


def _params(sem):
    return pltpu.CompilerParams(dimension_semantics=sem, vmem_limit_bytes=VMEM_LIMIT_V7X)


def _matmul(name, a, b, mode, out_dtype, tm=1024, tn=1024, tk=1024):
    if mode == "nn":
        (m, k), (_, n) = a.shape, b.shape
    elif mode == "nt":
        (m, k), (n, _) = a.shape, b.shape
    else:
        (k, m), (_, n) = a.shape, b.shape
    tm, tn, tk = _pick(m, tm), _pick(n, tn), _pick(k, tk)
    nk = k // tk
    if mode == "tn":
        a_spec = pl.BlockSpec((tk, tm), lambda i, j, kk: (kk, i))
    else:
        a_spec = pl.BlockSpec((tm, tk), lambda i, j, kk: (i, kk))
    if mode == "nt":
        b_spec = pl.BlockSpec((tn, tk), lambda i, j, kk: (j, kk))
    else:
        b_spec = pl.BlockSpec((tk, tn), lambda i, j, kk: (kk, j))
    dims = {"nn": NN, "nt": NT, "tn": TN}[mode]

    def body(a_ref, b_ref, o_ref, acc):
        kk = pl.program_id(2)

        @pl.when(kk == 0)
        def _():
            acc[...] = jnp.zeros_like(acc)

        acc[...] += _mm(a_ref[...], b_ref[...], dims)

        @pl.when(kk == nk - 1)
        def _():
            o_ref[...] = acc[...].astype(out_dtype)

    return _pcall(
        body, name=name, grid=(m // tm, n // tn, nk),
        in_specs=[a_spec, b_spec],
        out_specs=pl.BlockSpec((tm, tn), lambda i, j, kk: (i, j)),
        out_shape=jax.ShapeDtypeStruct((m, n), out_dtype),
        scratch_shapes=[pltpu.VMEM((tm, tn), F32)],
        compiler_params=_params(("parallel", "parallel", "arbitrary")),
    )(a, b)


def _rowwise(name, body, n_rows, tm, ins, outs, accs=()):
    arrays, in_specs = [], []
    last8 = n_rows // SUBLANES - 1
    per = tm // SUBLANES
    for spec in ins:
        kind, arr = spec[0], spec[1]
        if kind == "full":
            in_specs.append(pl.BlockSpec(arr.shape, lambda i, nd=arr.ndim: (0,) * nd))
        else:
            w, cb = spec[2], spec[3]
            if kind == "row":
                in_specs.append(pl.BlockSpec((tm, w), lambda i, cb=cb: (i, cb)))
            elif kind == "prev":
                in_specs.append(pl.BlockSpec((SUBLANES, w), lambda i, cb=cb: (jnp.maximum(i * per - 1, 0), cb)))
            else:
                in_specs.append(pl.BlockSpec((SUBLANES, w), lambda i, cb=cb: (jnp.minimum((i + 1) * per, last8), cb)))
        arrays.append(arr)
    out_shape = [jax.ShapeDtypeStruct((n_rows, w), dt) for (w, dt) in outs]
    out_shape += [jax.ShapeDtypeStruct(s, F32) for s in accs]
    out_specs = [pl.BlockSpec((tm, w), lambda i: (i, 0)) for (w, _) in outs]
    out_specs += [pl.BlockSpec(s, lambda i: (0, 0)) for s in accs]
    n_io = len(ins) + len(outs)

    def kern(*refs):
        i = pl.program_id(0)
        if accs:
            @pl.when(i == 0)
            def _():
                for r in refs[n_io:]:
                    r[...] = jnp.zeros_like(r)
        body(i, *refs)

    res = _pcall(
        kern, name=name, grid=(n_rows // tm,), in_specs=in_specs, out_specs=out_specs, out_shape=out_shape,
        compiler_params=_params(("arbitrary",)),
    )(*arrays)
    return res


def _shift_down(x, halo, j):
    r = pltpu.roll(x, j, 0)
    hr = pltpu.roll(halo, j, 0)
    rows = lax.broadcasted_iota(jnp.int32, (SUBLANES, x.shape[1]), 0)
    top = jnp.where(rows < j, hr, r[0:SUBLANES])
    return jnp.concatenate([top, r[SUBLANES:]], axis=0)


def _shift_up(x, halo, j):
    tm = x.shape[0]
    r = pltpu.roll(x, tm - j, 0)
    hr = pltpu.roll(halo, SUBLANES - j, 0)
    rows = lax.broadcasted_iota(jnp.int32, (SUBLANES, x.shape[1]), 0)
    bot = jnp.where(rows >= SUBLANES - j, hr, r[tm - SUBLANES:])
    return jnp.concatenate([r[:tm - SUBLANES], bot], axis=0)


def _conv_taps(x, halo, kw):
    return [x if kw - 1 - k == 0 else _shift_down(x, halo, kw - 1 - k) for k in range(kw)]


def _conv(taps, w):
    y = taps[0] * w[0:1]
    for k in range(1, len(taps)):
        y = y + taps[k] * w[k:k + 1]
    return y


def _rms(x, width):
    r = lax.rsqrt(jnp.sum(x * x, axis=-1, keepdims=True) * (1.0 / width) + EPS)
    return x * r, r


def _rms_bwd(xh, r, dxh, width):
    return r * (dxh - xh * (jnp.sum(dxh * xh, axis=-1, keepdims=True) * (1.0 / width)))


def _seq_flags(i, seq, tm):
    nps = seq // tm
    pos = i % nps
    return jnp.where(pos == 0, 0.0, 1.0), jnp.where(pos == nps - 1, 0.0, 1.0)


def _norm_cast(name, x, w, tm):
    t, d = x.shape

    def body(i, x_ref, w_ref, h_ref):
        xh, _ = _rms(x_ref[...], d)
        h_ref[...] = (xh * w_ref[...]).astype(BF16)

    return _rowwise(name, body, t, tm, [("row", x, d, 0), ("full", w)], [(d, BF16)])[0]


def _gdn_prep(proj, cw, gp, seq, tm):
    t = proj.shape[0]
    d = D_MODEL

    def body(i, q_ref, qh_ref, k_ref, kh_ref, v_ref, vh_ref, sm_ref, cw_ref, gp_ref, qn_ref, kn_ref, vv_ref, gs_ref):
        keep, _ = _seq_flags(i, seq, tm)
        for x_ref, h_ref, o_ref, off, scale in ((q_ref, qh_ref, qn_ref, 0, GDN_DK ** -0.5),
                                               (k_ref, kh_ref, kn_ref, d, 1.0), (v_ref, vh_ref, vv_ref, 2 * d, None)):
            y = _conv(_conv_taps(x_ref[...], h_ref[...] * keep, 4), cw_ref[:, off:off + d])
            a = y * _sigmoid(y)
            if scale is None:
                o_ref[...] = a
            else:
                for hh in range(GDN_HEADS):
                    s = a[:, hh * GDN_DK:(hh + 1) * GDN_DK]
                    n = lax.rsqrt(_rowsum(s * s) + EPS)
                    o_ref[:, hh * GDN_DK:(hh + 1) * GDN_DK] = s * (n * scale)
        sm = sm_ref[...]
        lane = lax.broadcasted_iota(jnp.int32, sm.shape, 1)
        beta = _sigmoid(sm)
        g = -jnp.exp(gp_ref[0:1, :]) * _softplus(sm + gp_ref[1:2, :])
        gs_ref[...] = jnp.where(lane < 8, beta, jnp.where(lane < 16, g, 0.0))

    ins = []
    for cb in range(3):
        ins += [("row", proj, d, cb), ("prev", proj, d, cb)]
    ins += [("row", proj, LANES, SMALL_CB), ("full", cw), ("full", gp)]
    return _rowwise("gdn_prep", body, t, tm, ins, [(d, F32), (d, F32), (d, F32), (LANES, F32)])


def _chunk_consts():
    row = lax.broadcasted_iota(jnp.int32, (CHUNK, CHUNK), 0)
    col = lax.broadcasted_iota(jnp.int32, (CHUNK, CHUNK), 1)
    lane = lax.broadcasted_iota(jnp.int32, (CHUNK, LANES), 1)
    row128 = lax.broadcasted_iota(jnp.int32, (CHUNK, LANES), 0)
    return dict(
        tril=row >= col, strict=row > col, trilf=(row >= col).astype(F32), triuf=(row <= col).astype(F32),
        eye=(row == col).astype(F32), lane=lane, row128=row128, e0=(lane == 0).astype(F32),
        ones=jnp.ones((CHUNK, LANES), F32))


def _tri_inv(nmat, eye):
    x = eye - nmat
    p = _mmx(nmat, nmat)
    for lvl in range(5):
        x = x + _mmx(x, p)
        if lvl < 4:
            p = _mmx(p, p)
    return x


def _gdn_gates(gs, h, c):
    beta = _rowsum(jnp.where(c["lane"] == h, gs, 0.0))
    g = _rowsum(jnp.where(c["lane"] == h + 8, gs, 0.0))
    gcb = _mmx(c["trilf"], jnp.broadcast_to(g, (CHUNK, LANES)))
    gcr = _mmx(c["e0"], gcb, NT)
    dc = jnp.exp(jnp.where(c["tril"], gcb[:, :CHUNK] - gcr, -1e30))
    gl = gcb[CHUNK - 1:CHUNK, :]
    return beta, gcb, dc, jnp.exp(gcb), jnp.exp(gl), jnp.exp(gl - gcb)


def _gdn_chunk_fwd(qn, kn, vv, gs, bsz, seq, sb):
    t = qn.shape[0]
    nsb = seq // sb
    ncb = sb // CHUNK
    nbh = bsz * GDN_HEADS

    def body(q_ref, k_ref, v_ref, gs_ref, o_ref, st_ref, ti_ref, s_scr):
        h = pl.program_id(1)

        @pl.when(pl.program_id(2) == 0)
        def _():
            s_scr[...] = jnp.zeros_like(s_scr)

        c = _chunk_consts()

        def chunk(n, carry):
            r = pl.ds(pl.multiple_of(n * CHUNK, CHUNK), CHUNK)
            q, k, v = q_ref[r, :], k_ref[r, :], v_ref[r, :]
            beta, gcb, dc, eg, egl, ekd = _gdn_gates(gs_ref[r, :], h, c)
            kb = k * beta
            amat = jnp.where(c["strict"], _mm(kb, k, NT) * dc, 0.0)
            tinv = _tri_inv(amat, c["eye"])
            u = _mmx(tinv, v * beta)
            w = _mmx(tinv, kb * eg)
            qk = _mm(q, k, NT) * dc
            s = s_scr[...]
            v_new = u - _mm(w, s)
            o_ref[r, :] = _mm(q * eg, s) + _mm(qk, v_new)
            st_ref[pl.ds(pl.multiple_of(n * GDN_DK, GDN_DK), GDN_DK), :] = s
            ti_ref[r, :] = tinv
            s_scr[...] = s * egl + _mm(k * ekd, v_new, TN)
            return carry

        lax.fori_loop(0, ncb, chunk, 0)

    blk = lambda: pl.BlockSpec((sb, GDN_DK), lambda b, h, j: (b * nsb + j, h))
    return _pcall(
        body, name="gdn_chunk_fwd", grid=(bsz, GDN_HEADS, nsb),
        in_specs=[blk(), blk(), blk(), pl.BlockSpec((sb, LANES), lambda b, h, j: (b * nsb + j, 0))],
        out_specs=[blk(),
                   pl.BlockSpec((None, ncb * GDN_DK, GDN_DK), lambda b, h, j: (b * GDN_HEADS + h, j, 0)),
                   pl.BlockSpec((None, sb, CHUNK), lambda b, h, j: (b * GDN_HEADS + h, j, 0))],
        out_shape=[jax.ShapeDtypeStruct((t, D_MODEL), F32),
                   jax.ShapeDtypeStruct((nbh, (seq // CHUNK) * GDN_DK, GDN_DK), F32),
                   jax.ShapeDtypeStruct((nbh, seq, CHUNK), F32)],
        scratch_shapes=[pltpu.VMEM((GDN_DK, GDN_DK), F32)],
        compiler_params=_params(("parallel", "parallel", "arbitrary")),
    )(qn, kn, vv, gs)


def _ssd_prep(proj, cw, cb, sp, seq, tm):
    t = proj.shape[0]
    d = D_MODEL
    ssd_w = SSD_HEADS * SSD_P

    def body(i, x_ref, xh_ref, bc_ref, bch_ref, sm_ref, cw_ref, cb_ref, sp_ref, xs_ref, bco_ref, dtx_ref, acsx_ref, acsb_ref):
        keep, _ = _seq_flags(i, seq, tm)
        y = _conv(_conv_taps(x_ref[...], xh_ref[...] * keep, 4), cw_ref[:, 0:d]) + cb_ref[:, 0:d]
        xs_ref[...] = y * _sigmoid(y)
        y = _conv(_conv_taps(bc_ref[...], bch_ref[...] * keep, 4), cw_ref[:, d:d + 512]) + cb_ref[:, d:d + 512]
        bco_ref[...] = y * _sigmoid(y)
        sm = sm_ref[...]
        lane = lax.broadcasted_iota(jnp.int32, sm.shape, 1)
        valid = (lane >= 16) & (lane < 32)
        dt = jnp.where(valid, _softplus(sm + sp_ref[1:2, :]), 0.0)
        adt = dt * (-jnp.exp(sp_ref[0:1, :]))
        ri = lax.broadcasted_iota(jnp.int32, (tm, tm), 0)
        ci = lax.broadcasted_iota(jnp.int32, (tm, tm), 1)
        blocktril = ((ri >= ci) & ((ri // CHUNK) == (ci // CHUNK))).astype(F32)
        acs = _mmx(blocktril, adt)
        l64 = lax.broadcasted_iota(jnp.int32, (LANES, ssd_w), 0)
        d64 = lax.broadcasted_iota(jnp.int32, (LANES, ssd_w), 1)
        e64 = (l64 - 16 == d64 // SSD_P).astype(F32)
        l128 = lax.broadcasted_iota(jnp.int32, (LANES, SSD_HEADS * LANES), 0)
        d128 = lax.broadcasted_iota(jnp.int32, (LANES, SSD_HEADS * LANES), 1)
        e128 = (l128 - 16 == d128 // LANES).astype(F32)
        dtx_ref[...] = _mmx(dt, e64)
        acsx_ref[...] = _mmx(acs, e64)
        acsb_ref[...] = _mmx(acs, e128)

    ins = [("row", proj, d, 5), ("prev", proj, d, 5), ("row", proj, 512, 12), ("prev", proj, 512, 12),
           ("row", proj, LANES, SMALL_CB), ("full", cw), ("full", cb), ("full", sp)]
    return _rowwise("ssd_prep", body, t, tm, ins,
                    [(d, F32), (512, F32), (ssd_w, F32), (ssd_w, F32), (SSD_HEADS * LANES, F32)])


def _ssd_head(acsb_ref, r, hh, cbm, c):
    ab = acsb_ref[r, hh * LANES:(hh + 1) * LANES]
    ar = _mmx(c["e0"], ab, NT)
    lm = jnp.exp(jnp.where(c["tril"], ab[:, :CHUNK] - ar, -1e30))
    return lm, cbm * lm


def _ssd_chunk_fwd(xs, bc, dtx, acsx, acsb, bsz, seq, sb):
    t = xs.shape[0]
    nsb = seq // sb
    ncb = sb // CHUNK
    gw = SSD_HPG * SSD_P

    def body(x_ref, dtx_ref, ax_ref, ab_ref, b_ref, c_ref, y_ref, sts_ref, st_scr):
        @pl.when(pl.program_id(2) == 0)
        def _():
            st_scr[...] = jnp.zeros_like(st_scr)

        c = _chunk_consts()
        lane5 = lax.broadcasted_iota(jnp.int32, (CHUNK, gw), 1) // SSD_P

        def chunk(n, carry):
            r = pl.ds(pl.multiple_of(n * CHUNK, CHUNK), CHUNK)
            x, dt, ax, bm, cm = x_ref[r, :], dtx_ref[r, :], ax_ref[r, :], b_ref[r, :], c_ref[r, :]
            xdt = x * dt
            cbm = _mm(cm, bm, NT)
            al = ax[CHUNK - 1:CHUNK, :]
            st = st_scr[...]
            y = _mm(cm, st) * jnp.exp(ax)
            for hh in range(SSD_HPG):
                _, g = _ssd_head(ab_ref, r, hh, cbm, c)
                y = y + _mm(g, jnp.where(lane5 == hh, xdt, 0.0))
            y_ref[r, :] = y
            sts_ref[pl.ds(pl.multiple_of(n * SSD_N, SSD_N), SSD_N), :] = st
            st_scr[...] = st * jnp.exp(al) + _mm(bm, xdt * jnp.exp(al - ax), TN)
            return carry

        lax.fori_loop(0, ncb, chunk, 0)

    wide = lambda: pl.BlockSpec((sb, gw), lambda b, g, j: (b * nsb + j, g))
    return _pcall(
        body, name="ssd_chunk_fwd", grid=(bsz, SSD_GROUPS, nsb),
        in_specs=[wide(), wide(), wide(),
                  pl.BlockSpec((sb, SSD_HPG * LANES), lambda b, g, j: (b * nsb + j, g)),
                  pl.BlockSpec((sb, SSD_N), lambda b, g, j: (b * nsb + j, g)),
                  pl.BlockSpec((sb, SSD_N), lambda b, g, j: (b * nsb + j, SSD_GROUPS + g))],
        out_specs=[wide(), pl.BlockSpec((None, ncb * SSD_N, gw), lambda b, g, j: (b * SSD_GROUPS + g, j, 0))],
        out_shape=[jax.ShapeDtypeStruct((t, SSD_HEADS * SSD_P), F32),
                   jax.ShapeDtypeStruct((bsz * SSD_GROUPS, (seq // CHUNK) * SSD_N, gw), F32)],
        scratch_shapes=[pltpu.VMEM((SSD_N, gw), F32)],
        compiler_params=_params(("parallel", "parallel", "arbitrary")),
    )(xs, dtx, acsx, acsb, bc, bc)


def _gate_norm(o_gdn, y_ssd, xs, proj, gnw, snw, dvec, tm):
    t = o_gdn.shape[0]
    d = D_MODEL

    def body(i, o_ref, za_ref, y_ref, xs_ref, zs_ref, gnw_ref, snw_ref, dv_ref, out_ref):
        for hh in range(GDN_HEADS):
            sl = slice(hh * GDN_DK, (hh + 1) * GDN_DK)
            oh, _ = _rms(o_ref[:, sl], GDN_DK)
            z = za_ref[:, sl]
            out_ref[:, sl] = (oh * gnw_ref[...] * (z * _sigmoid(z))).astype(BF16)
        zs = zs_ref[...]
        yg = (y_ref[...] + dv_ref[...] * xs_ref[...]) * (zs * _sigmoid(zs))
        for g in range(SSD_GROUPS):
            sl = slice(g * 512, (g + 1) * 512)
            yh, _ = _rms(yg[:, sl], 512)
            out_ref[:, d + g * 512:d + (g + 1) * 512] = (yh * snw_ref[:, sl]).astype(BF16)

    ins = [("row", o_gdn, d, 0), ("row", proj, d, 3), ("row", y_ssd, d, 0), ("row", xs, d, 0), ("row", proj, d, 4),
           ("full", gnw), ("full", snw), ("full", dvec)]
    return _rowwise("gate_norm", body, t, tm, ins, [(2 * d, BF16)])[0]


def _mid(x, mix, pmw, pfw, tm):
    t, d = x.shape

    def body(i, x_ref, mix_ref, pmw_ref, pfw_ref, x1_ref, h2_ref):
        mh, _ = _rms(mix_ref[...], d)
        x1 = x_ref[...] + mh * pmw_ref[...]
        x1_ref[...] = x1
        xh, _ = _rms(x1, d)
        h2_ref[...] = (xh * pfw_ref[...]).astype(BF16)

    return _rowwise("mid", body, t, tm, [("row", x, d, 0), ("row", mix, d, 0), ("full", pmw), ("full", pfw)],
                    [(d, F32), (d, BF16)])


def _ffn_gate_up(ug_ref, ugh_ref, uu_ref, uuh_ref, cw_ref, cb_ref, keep):
    tg = _conv_taps(ug_ref[...], ugh_ref[...] * keep, 3)
    tu = _conv_taps(uu_ref[...], uuh_ref[...] * keep, 3)
    gate = _conv(tg, cw_ref[:, 0:D_FF]) + cb_ref[:, 0:D_FF]
    up = _conv(tu, cw_ref[:, D_FF:2 * D_FF]) + cb_ref[:, D_FF:2 * D_FF]
    return tg, tu, gate, up


def _ffn_act(u_pre, cw, cb, seq, tm):
    t = u_pre.shape[0]

    def body(i, ug_ref, ugh_ref, uu_ref, uuh_ref, cw_ref, cb_ref, act_ref):
        keep, _ = _seq_flags(i, seq, tm)
        _, _, gate, up = _ffn_gate_up(ug_ref, ugh_ref, uu_ref, uuh_ref, cw_ref, cb_ref, keep)
        act_ref[...] = (gate * _sigmoid(gate) * up).astype(BF16)

    ins = [("row", u_pre, D_FF, 0), ("prev", u_pre, D_FF, 0), ("row", u_pre, D_FF, 1), ("prev", u_pre, D_FF, 1),
           ("full", cw), ("full", cb)]
    return _rowwise("ffn_act", body, t, tm, ins, [(D_FF, BF16)])[0]


def _final(x1, f, tgt, w, tm):
    t, d = x1.shape

    def body(i, x1_ref, f_ref, t_ref, w_ref, dy_ref, df_ref, loss_ref, dw_ref):
        fh, r = _rms(f_ref[...], d)
        e = x1_ref[...] + fh * w_ref[...] - t_ref[...]
        loss_ref[...] += _colsum(e * e) * (0.5 / d)
        dy = e * (1.0 / d)
        dy_ref[...] = dy
        dw_ref[...] += _colsum(dy * fh)
        df_ref[...] = _rms_bwd(fh, r, dy * w_ref[...], d).astype(BF16)

    return _rowwise("final", body, t, tm, [("row", x1, d, 0), ("row", f, d, 0), ("row", tgt, d, 0), ("full", w)],
                    [(d, F32), (d, BF16)], accs=[(1, d), (1, d)])


def _ffn_bwd(u_pre, dact, cw, cb, seq, tm):
    t = u_pre.shape[0]

    def body(i, ug_ref, ugh_ref, uu_ref, uuh_ref, da_ref, cw_ref, cb_ref, du_ref, dcw_ref, dcb_ref):
        keep, _ = _seq_flags(i, seq, tm)
        tg, tu, gate, up = _ffn_gate_up(ug_ref, ugh_ref, uu_ref, uuh_ref, cw_ref, cb_ref, keep)
        sg = _sigmoid(gate)
        da = da_ref[...]
        dgate = da * up * _dsilu(gate, sg)
        dup = da * gate * sg
        du_ref[:, 0:D_FF] = dgate
        du_ref[:, D_FF:2 * D_FF] = dup
        dcb_ref[:, 0:D_FF] += _colsum(dgate)
        dcb_ref[:, D_FF:2 * D_FF] += _colsum(dup)
        for k in range(3):
            dcw_ref[k:k + 1, 0:D_FF] += _colsum(dgate * tg[k])
            dcw_ref[k:k + 1, D_FF:2 * D_FF] += _colsum(dup * tu[k])

    ins = [("row", u_pre, D_FF, 0), ("prev", u_pre, D_FF, 0), ("row", u_pre, D_FF, 1), ("prev", u_pre, D_FF, 1),
           ("row", dact, D_FF, 0), ("full", cw), ("full", cb)]
    return _rowwise("ffn_bwd", body, t, tm, ins, [(2 * D_FF, F32)], accs=[(SUBLANES, 2 * D_FF), (1, 2 * D_FF)])


def _conv_t(name, dy, cw, kw, seq, tm):
    t, width = dy.shape

    def body(i, d_ref, dn_ref, cw_ref, o_ref):
        _, keep = _seq_flags(i, seq, tm)
        d = d_ref[...]
        halo = dn_ref[...] * keep
        acc = d * cw_ref[kw - 1:kw, :]
        for j in range(1, kw):
            acc = acc + _shift_up(d, halo, j) * cw_ref[kw - 1 - j:kw - j, :]
        o_ref[...] = acc.astype(BF16)

    return _rowwise(name, body, t, tm, [("row", dy, width, 0), ("next", dy, width, 0), ("full", cw)], [(width, BF16)])[0]


def _mid_bwd(x1, mix, dy, dh2, pmw, pfw, tm):
    t, d = x1.shape

    def body(i, x1_ref, mix_ref, dy_ref, dh2_ref, pmw_ref, pfw_ref, dx1_ref, dmix_ref, dpm_ref, dpf_ref):
        xh, r2 = _rms(x1_ref[...], d)
        dh2 = dh2_ref[...]
        dpf_ref[...] += _colsum(dh2 * xh)
        dx1 = dy_ref[...] + _rms_bwd(xh, r2, dh2 * pfw_ref[...], d)
        dx1_ref[...] = dx1
        mh, r = _rms(mix_ref[...], d)
        dpm_ref[...] += _colsum(dx1 * mh)
        dmix_ref[...] = _rms_bwd(mh, r, dx1 * pmw_ref[...], d).astype(BF16)

    ins = [("row", x1, d, 0), ("row", mix, d, 0), ("row", dy, d, 0), ("row", dh2, d, 0), ("full", pmw), ("full", pfw)]
    return _rowwise("mid_bwd", body, t, tm, ins, [(d, F32), (d, BF16)], accs=[(1, d), (1, d)])


def _gate_norm_bwd(o_gdn, y_ssd, xs, proj, dmixin, gnw, snw, dvec, tm):
    t = o_gdn.shape[0]
    d = D_MODEL

    def body(i, o_ref, za_ref, y_ref, xs_ref, zs_ref, dma_ref, dms_ref, gnw_ref, snw_ref, dv_ref,
             do_ref, dza_ref, dy_ref, dxs_ref, dzs_ref, dgnw_ref, dsnw_ref, dd_ref):
        for hh in range(GDN_HEADS):
            sl = slice(hh * GDN_DK, (hh + 1) * GDN_DK)
            oh, r = _rms(o_ref[:, sl], GDN_DK)
            z = za_ref[:, sl]
            sz = _sigmoid(z)
            dm = dma_ref[:, sl]
            don = dm * (z * sz)
            dza_ref[:, sl] = (dm * oh * gnw_ref[...] * _dsilu(z, sz)).astype(BF16)
            dgnw_ref[...] += _colsum(don * oh)
            do_ref[:, sl] = _rms_bwd(oh, r, don * gnw_ref[...], GDN_DK)
        zs = zs_ref[...]
        sz = _sigmoid(zs)
        sil = zs * sz
        x = xs_ref[...]
        y0 = y_ref[...] + dv_ref[...] * x
        yg = y0 * sil
        dms = dms_ref[...]
        for g in range(SSD_GROUPS):
            sl = slice(g * 512, (g + 1) * 512)
            yh, r = _rms(yg[:, sl], 512)
            dsnw_ref[:, sl] += _colsum(dms[:, sl] * yh)
            dyg = _rms_bwd(yh, r, dms[:, sl] * snw_ref[:, sl], 512)
            dy0 = dyg * sil[:, sl]
            dzs_ref[:, sl] = (dyg * y0[:, sl] * _dsilu(zs[:, sl], sz[:, sl])).astype(BF16)
            dy_ref[:, sl] = dy0
            dxs_ref[:, sl] = dy0 * dv_ref[:, sl]
            dd_ref[:, sl] += _colsum(dy0 * x[:, sl])

    ins = [("row", o_gdn, d, 0), ("row", proj, d, 3), ("row", y_ssd, d, 0), ("row", xs, d, 0), ("row", proj, d, 4),
           ("row", dmixin, d, 0), ("row", dmixin, d, 1), ("full", gnw), ("full", snw), ("full", dvec)]
    return _rowwise("gate_norm_bwd", body, t, tm, ins, [(d, F32), (d, BF16), (d, F32), (d, F32), (d, BF16)],
                    accs=[(1, GDN_DK), (1, d), (1, d)])


def _ssd_chunk_bwd(xs, bc, dtx, acsx, acsb, dy, sts, bsz, seq, sb):
    t = xs.shape[0]
    nsb = seq // sb
    ncb = sb // CHUNK
    gw = SSD_HPG * SSD_P

    def body(x_ref, dtx_ref, ax_ref, ab_ref, b_ref, c_ref, dy_ref, sts_ref, dx_ref, db_ref, dc_ref, ddt_ref, dadt_ref, dst_scr):
        @pl.when(pl.program_id(2) == 0)
        def _():
            dst_scr[...] = jnp.zeros_like(dst_scr)

        c = _chunk_consts()
        lane5 = lax.broadcasted_iota(jnp.int32, (CHUNK, gw), 1) // SSD_P
        row5 = lax.broadcasted_iota(jnp.int32, (CHUNK, gw), 0)
        rsel = (lax.broadcasted_iota(jnp.int32, (gw, LANES), 0) // SSD_P
                == lax.broadcasted_iota(jnp.int32, (gw, LANES), 1)).astype(F32)

        def chunk(nn, carry):
            n = ncb - 1 - nn
            r = pl.ds(pl.multiple_of(n * CHUNK, CHUNK), CHUNK)
            x, dt, ax, bm, cm, dyv = x_ref[r, :], dtx_ref[r, :], ax_ref[r, :], b_ref[r, :], c_ref[r, :], dy_ref[r, :]
            st = sts_ref[pl.ds(pl.multiple_of(n * SSD_N, SSD_N), SSD_N), :]
            dst = dst_scr[...]
            xdt = x * dt
            cbm = _mm(cm, bm, NT)
            al = ax[CHUNK - 1:CHUNK, :]
            ex, el = jnp.exp(ax), jnp.exp(al)
            dec = jnp.exp(al - ax)
            xd = xdt * dec
            dye = dyv * ex
            dxd = _mm(bm, dst)
            dxdt = dec * dxd
            dcm = _mm(dye, st, NT)
            dbm = _mm(xd, dst, NT)
            z = dye * _mm(cm, st) - dxd * xd
            zl = _colsum(dst * st) * el + _colsum(dxd * xd)
            z = z + jnp.where(row5 == CHUNK - 1, zl, 0.0)
            dcb = jnp.zeros((CHUNK, CHUNK), F32)
            rc = jnp.zeros((CHUNK, LANES), F32)
            for hh in range(SSD_HPG):
                lm, g = _ssd_head(ab_ref, r, hh, cbm, c)
                dym = jnp.where(lane5 == hh, dyv, 0.0)
                dxdt = dxdt + _mm(g, dym, TN)
                dg = _mm(dym, xdt, NT)
                dcb = dcb + dg * lm
                pm = dg * g
                rc = rc + jnp.where(c["lane"] == hh, _mmx(pm, c["ones"]) - _mmx(pm, c["ones"], TN), 0.0)
            dc_ref[r, :] = dcm + _mm(dcb, bm)
            db_ref[r, :] = dbm + _mm(dcb, cm, TN)
            dadt_ref[r, :] = _mmx(c["triuf"], _mmx(z, rsel) + rc)
            dx_ref[r, :] = dxdt * dt
            ddt_ref[r, :] = _mmx(dxdt * x, rsel)
            dst_scr[...] = dst * el + _mm(cm, dye, TN)
            return carry

        lax.fori_loop(0, ncb, chunk, 0)

    rev = lambda b, j: b * nsb + (nsb - 1 - j)
    wide = lambda: pl.BlockSpec((sb, gw), lambda b, g, j: (rev(b, j), g))
    narrow = lambda: pl.BlockSpec((sb, SSD_N), lambda b, g, j: (rev(b, j), g))
    return _pcall(
        body, name="ssd_chunk_bwd", grid=(bsz, SSD_GROUPS, nsb),
        in_specs=[wide(), wide(), wide(),
                  pl.BlockSpec((sb, SSD_HPG * LANES), lambda b, g, j: (rev(b, j), g)),
                  narrow(), pl.BlockSpec((sb, SSD_N), lambda b, g, j: (rev(b, j), SSD_GROUPS + g)), wide(),
                  pl.BlockSpec((None, ncb * SSD_N, gw), lambda b, g, j: (b * SSD_GROUPS + g, nsb - 1 - j, 0))],
        out_specs=[wide(), narrow(), narrow(), narrow(), narrow()],
        out_shape=[jax.ShapeDtypeStruct((t, SSD_HEADS * SSD_P), F32)] + [jax.ShapeDtypeStruct((t, 2 * SSD_N), F32)] * 4,
        scratch_shapes=[pltpu.VMEM((SSD_N, gw), F32)],
        compiler_params=_params(("parallel", "parallel", "arbitrary")),
    )(xs, dtx, acsx, acsb, bc, bc, dy, sts)


def _ssd_prep_bwd(proj, dxs_c, dxs_d, dbm, dcm, ddtp, dadt, dsm_gdn, cw, cb, sp, seq, tm):
    t = proj.shape[0]
    d = D_MODEL

    def body(i, x_ref, xh_ref, bc_ref, bch_ref, sm_ref, dxc_ref, dxd_ref, db_ref, dc_ref, ddt_ref, dadt_ref, dsg_ref,
             cw_ref, cb_ref, sp_ref, dpre_ref, dsm_ref, dcw_ref, dcb_ref, dsp_ref):
        keep, _ = _seq_flags(i, seq, tm)
        parts = ((x_ref, xh_ref, 0, d, (dxc_ref[...] + dxd_ref[...],)),
                 (bc_ref, bch_ref, d, 512, (db_ref[...], dc_ref[...])))
        for xr, hr, off, w, grads in parts:
            taps = _conv_taps(xr[...], hr[...] * keep, 4)
            y = _conv(taps, cw_ref[:, off:off + w]) + cb_ref[:, off:off + w]
            ds_ = _dsilu(y, _sigmoid(y))
            o = 0
            for gr in grads:
                wg = gr.shape[1]
                dpre = gr * ds_[:, o:o + wg]
                dpre_ref[:, off + o:off + o + wg] = dpre
                dcb_ref[:, off + o:off + o + wg] += _colsum(dpre)
                for k in range(4):
                    dcw_ref[k:k + 1, off + o:off + o + wg] += _colsum(dpre * taps[k][:, o:o + wg])
                o += wg
        sm = sm_ref[...]
        lane = lax.broadcasted_iota(jnp.int32, sm.shape, 1)
        valid = (lane >= 16) & (lane < 32)
        xb = sm + sp_ref[1:2, :]
        dt = jnp.where(valid, _softplus(xb), 0.0)
        a_neg = -jnp.exp(sp_ref[0:1, :])
        li = lax.broadcasted_iota(jnp.int32, (2 * SSD_N, LANES), 0)
        lo = lax.broadcasted_iota(jnp.int32, (2 * SSD_N, LANES), 1)
        perm = (((li < 8) & (lo == li + 16)) | ((li >= SSD_N) & (li < SSD_N + 8) & (lo == li - SSD_N + 24))).astype(F32)
        ddtp = _mmx(ddt_ref[...], perm)
        dadt_s = _mmx(dadt_ref[...], perm)
        dxb = jnp.where(valid, (ddtp + dadt_s * a_neg) * _sigmoid(xb), 0.0)
        dsm_ref[...] = (dsg_ref[...] + dxb).astype(BF16)
        dsp_ref[1:2, :] += _colsum(dxb)
        dsp_ref[0:1, :] += jnp.where(valid[0:1, :], _colsum(dadt_s * dt) * a_neg, 0.0)

    ins = [("row", proj, d, 5), ("prev", proj, d, 5), ("row", proj, 512, 12), ("prev", proj, 512, 12),
           ("row", proj, LANES, SMALL_CB), ("row", dxs_c, d, 0), ("row", dxs_d, d, 0), ("row", dbm, 256, 0),
           ("row", dcm, 256, 0), ("row", ddtp, 256, 0), ("row", dadt, 256, 0), ("row", dsm_gdn, LANES, 0),
           ("full", cw), ("full", cb), ("full", sp)]
    return _rowwise("ssd_prep_bwd", body, t, tm, ins, [(d + 512, F32), (LANES, BF16)],
                    accs=[(SUBLANES, d + 512), (1, d + 512), (SUBLANES, LANES)])


def _gdn_chunk_bwd(qn, kn, vv, gs, do, sts, tis, bsz, seq, sb):
    t = qn.shape[0]
    nsb = seq // sb
    ncb = sb // CHUNK

    def body(q_ref, k_ref, v_ref, gs_ref, do_ref, st_ref, ti_ref, dq_ref, dk_ref, dv_ref, dgb_ref, ds_scr):
        h = pl.program_id(1)

        @pl.when(pl.program_id(2) == 0)
        def _():
            ds_scr[...] = jnp.zeros_like(ds_scr)

        c = _chunk_consts()

        def chunk(nn, carry):
            n = ncb - 1 - nn
            r = pl.ds(pl.multiple_of(n * CHUNK, CHUNK), CHUNK)
            q, k, v, do_ = q_ref[r, :], k_ref[r, :], v_ref[r, :], do_ref[r, :]
            s = st_ref[pl.ds(pl.multiple_of(n * GDN_DK, GDN_DK), GDN_DK), :]
            tinv = ti_ref[r, :]
            dsn = ds_scr[...]
            beta, gcb, dc, eg, egl, ekd = _gdn_gates(gs_ref[r, :], h, c)
            kb = k * beta
            rhs_w = kb * eg
            u = _mmx(tinv, v * beta)
            w = _mmx(tinv, rhs_w)
            amat = jnp.where(c["strict"], _mm(kb, k, NT) * dc, 0.0)
            qk = _mm(q, k, NT) * dc
            qd, kd = q * eg, k * ekd
            v_new = u - _mm(w, s)
            dv_new = _mm(qk, do_, TN) + _mm(kd, dsn)
            dqk = _mm(do_, v_new, NT)
            dqd = _mm(do_, s, NT)
            ds_scr[...] = _mm(qd, do_, TN) + dsn * egl - _mm(w, dv_new, TN)
            dkd = _mm(v_new, dsn, NT)
            dgl = _colsum(_rowsum(s * dsn)) * egl
            dw = -_mm(dv_new, s, NT)
            dru = _mmx(tinv, dv_new, TN)
            drw = _mmx(tinv, dw, TN)
            da = jnp.where(c["strict"], -(_mm(dru, u, NT) + _mm(drw, w, NT)), 0.0)
            dv_ref[r, :] = dru * beta
            dbeta = _rowsum(dru * v)
            dkb = drw * eg
            m = da * dc
            dkb = dkb + _mm(m, k)
            dk = _mm(m, kb, TN)
            mq = dqk * dc
            dq_ref[r, :] = _mm(mq, k) + dqd * eg
            dk = dk + _mm(mq, q, TN) + dkd * ekd + dkb * beta
            dk_ref[r, :] = dk
            dbeta = dbeta + _rowsum(dkb * k)
            pq = da * amat + dqk * qk
            ekk = _rowsum(dkd * kd)
            dgc = _mmx(pq, c["ones"]) - _mmx(pq, c["ones"], TN) + (_rowsum(drw * rhs_w) + _rowsum(dqd * qd) - ekk)
            dgc = dgc + jnp.where(c["row128"] == CHUNK - 1, _colsum(ekk) + dgl, 0.0)
            dg = _mmx(c["triuf"], dgc)
            dgb_ref[r, :] = jnp.where(c["lane"] == 0, dbeta, jnp.where(c["lane"] == 1, dg, 0.0))
            return carry

        lax.fori_loop(0, ncb, chunk, 0)

    rev = lambda b, j: b * nsb + (nsb - 1 - j)
    blk = lambda: pl.BlockSpec((sb, GDN_DK), lambda b, h, j: (rev(b, j), h))
    return _pcall(
        body, name="gdn_chunk_bwd", grid=(bsz, GDN_HEADS, nsb),
        in_specs=[blk(), blk(), blk(), pl.BlockSpec((sb, LANES), lambda b, h, j: (rev(b, j), 0)), blk(),
                  pl.BlockSpec((None, ncb * GDN_DK, GDN_DK), lambda b, h, j: (b * GDN_HEADS + h, nsb - 1 - j, 0)),
                  pl.BlockSpec((None, sb, CHUNK), lambda b, h, j: (b * GDN_HEADS + h, nsb - 1 - j, 0))],
        out_specs=[blk(), blk(), blk(), blk()],
        out_shape=[jax.ShapeDtypeStruct((t, D_MODEL), F32)] * 4,
        scratch_shapes=[pltpu.VMEM((GDN_DK, GDN_DK), F32)],
        compiler_params=_params(("parallel", "parallel", "arbitrary")),
    )(qn, kn, vv, gs, do, sts, tis)


def _gdn_prep_bwd(proj, dqn, dkn, dvv, dgb, cw, gp, seq, tm):
    t = proj.shape[0]
    d = D_MODEL

    def body(i, q_ref, qh_ref, k_ref, kh_ref, v_ref, vh_ref, sm_ref, dq_ref, dk_ref, dv_ref, dgb_ref, cw_ref, gp_ref,
             dpre_ref, dsm_ref, dcw_ref, dgp_ref):
        keep, _ = _seq_flags(i, seq, tm)
        for x_ref, h_ref, g_ref, off, scale in ((q_ref, qh_ref, dq_ref, 0, GDN_DK ** -0.5),
                                               (k_ref, kh_ref, dk_ref, d, 1.0), (v_ref, vh_ref, dv_ref, 2 * d, None)):
            taps = _conv_taps(x_ref[...], h_ref[...] * keep, 4)
            y = _conv(taps, cw_ref[:, off:off + d])
            sy = _sigmoid(y)
            ds_ = _dsilu(y, sy)
            if scale is None:
                dpre = g_ref[...] * ds_
                dpre_ref[:, off:off + d] = dpre
                for k in range(4):
                    dcw_ref[k:k + 1, off:off + d] += _colsum(dpre * taps[k])
            else:
                a = y * sy
                for hh in range(GDN_HEADS):
                    sl = slice(hh * GDN_DK, (hh + 1) * GDN_DK)
                    s = a[:, sl]
                    n = lax.rsqrt(_rowsum(s * s) + EPS)
                    ah = s * n
                    gq = g_ref[:, sl]
                    dpre = (scale * n) * (gq - ah * _rowsum(gq * ah)) * ds_[:, sl]
                    dpre_ref[:, off + hh * GDN_DK:off + (hh + 1) * GDN_DK] = dpre
                    for k in range(4):
                        dcw_ref[k:k + 1, off + hh * GDN_DK:off + (hh + 1) * GDN_DK] += _colsum(dpre * taps[k][:, sl])
        sm = sm_ref[...]
        lane = lax.broadcasted_iota(jnp.int32, sm.shape, 1)
        si = lax.broadcasted_iota(jnp.int32, (d, LANES), 0)
        so = lax.broadcasted_iota(jnp.int32, (d, LANES), 1)
        sel = (((si % GDN_DK == 0) & (so == si // GDN_DK)) | ((si % GDN_DK == 1) & (so == si // GDN_DK + 8))).astype(F32)
        dsel = _mmx(dgb_ref[...], sel)
        beta = _sigmoid(sm)
        xb = sm + gp_ref[1:2, :]
        a_neg = -jnp.exp(gp_ref[0:1, :])
        sp = _softplus(xb)
        is_g = (lane >= 8) & (lane < 16)
        dxb = jnp.where(is_g, dsel * a_neg * _sigmoid(xb), 0.0)
        dsm_ref[...] = jnp.where(lane < 8, dsel * beta * (1.0 - beta), dxb)
        dgp_ref[1:2, :] += _colsum(dxb)
        dgp_ref[0:1, :] += _colsum(jnp.where(is_g, dsel * a_neg * sp, 0.0))

    ins = []
    for cb in range(3):
        ins += [("row", proj, d, cb), ("prev", proj, d, cb)]
    ins += [("row", proj, LANES, SMALL_CB), ("row", dqn, d, 0), ("row", dkn, d, 0), ("row", dvv, d, 0), ("row", dgb, d, 0),
            ("full", cw), ("full", gp)]
    return _rowwise("gdn_prep_bwd", body, t, tm, ins, [(3 * d, F32), (LANES, F32)],
                    accs=[(SUBLANES, 3 * d), (SUBLANES, LANES)])


def _first_bwd(x, dh1, dx1, w, tm):
    t, d = x.shape

    def body(i, x_ref, dh_ref, dx1_ref, w_ref, dx_ref, dw_ref):
        xh, r = _rms(x_ref[...], d)
        dh = dh_ref[...]
        dw_ref[...] += _colsum(dh * xh)
        dx_ref[...] = dx1_ref[...] + _rms_bwd(xh, r, dh * w_ref[...], d)

    return _rowwise("first_bwd", body, t, tm, [("row", x, d, 0), ("row", dh1, d, 0), ("row", dx1, d, 0), ("full", w)],
                    [(d, F32)], accs=[(1, d)])


def _exchange(name, arrays, scatter):
    n = len(arrays)

    def body(*refs):
        ins, outs = refs[:n], refs[n:2 * n]
        send_sems, recv_sems, loc_sems = refs[2 * n:]
        x, y, c = lax.axis_index("x"), lax.axis_index("y"), lax.axis_index("c")
        me = 4 * x + 2 * y + c
        copies = []
        for t in range(n):
            src_me = ins[t].at[me] if scatter else ins[t]
            loc = pltpu.make_async_copy(src_me, outs[t].at[me], loc_sems.at[t])
            loc.start()
            copies.append((loc, None))
            for k in range(N_DEV - 1):
                bx, by, bc = ((k + 1) >> 2) & 1, ((k + 1) >> 1) & 1, (k + 1) & 1
                px = 1 - x if bx else x
                py = 1 - y if by else y
                pc = 1 - c if bc else c
                peer = 4 * px + 2 * py + pc
                src = ins[t].at[peer] if scatter else ins[t]
                send = pltpu.make_async_remote_copy(
                    src_ref=src, dst_ref=outs[t].at[me], send_sem=send_sems.at[t, k], recv_sem=recv_sems.at[t, k],
                    device_id=(px, py, pc), device_id_type=pl.DeviceIdType.MESH)
                send.start()
                recv = pltpu.make_async_remote_copy(
                    src_ref=src, dst_ref=outs[t].at[peer], send_sem=send_sems.at[t, k], recv_sem=recv_sems.at[t, k],
                    device_id=(px, py, pc), device_id_type=pl.DeviceIdType.MESH)
                copies.append((send, recv))
        for first, second in copies:
            if second is None:
                first.wait()
            else:
                first.wait_send()
                second.wait_recv()

    out_shape = []
    for a in arrays:
        shp = a.shape if scatter else (N_DEV,) + a.shape
        out_shape.append(jax.ShapeDtypeStruct(shp, a.dtype))
    return _pcall(
        body, name=name,
        in_specs=[pl.BlockSpec(memory_space=pl.ANY)] * n,
        out_specs=[pl.BlockSpec(memory_space=pl.ANY)] * n,
        out_shape=out_shape,
        scratch_shapes=[pltpu.SemaphoreType.DMA((n, N_DEV - 1)), pltpu.SemaphoreType.DMA((n, N_DEV - 1)),
                        pltpu.SemaphoreType.DMA((n,))],
        compiler_params=pltpu.CompilerParams(has_side_effects=True),
    )(*arrays)


def _adam_math(w, g, m, v):
    m = ADAM_B1 * m + (1.0 - ADAM_B1) * g
    v = ADAM_B2 * v + (1.0 - ADAM_B2) * (g * g)
    m_hat = m / (1.0 - ADAM_B1 ** ADAM_STEP)
    v_hat = v / (1.0 - ADAM_B2 ** ADAM_STEP)
    delta = -ADAM_LR * (m_hat / (jnp.sqrt(v_hat) + ADAM_EPS) + ADAM_WD * w)
    return delta, m, v


def _adam_big(name, parts, w, m, v, tm):
    r, c = w.shape
    tm = tm if r % tm == 0 else r

    def body(p_ref, w_ref, m_ref, v_ref, g_ref, d_ref, nm_ref, nv_ref):
        g = p_ref[0].astype(F32)
        for s in range(1, N_DEV):
            g = g + p_ref[s].astype(F32)
        g_ref[...] = g
        d_ref[...], nm_ref[...], nv_ref[...] = _adam_math(w_ref[...], g, m_ref[...], v_ref[...])

    blk = lambda: pl.BlockSpec((tm, c), lambda i: (i, 0))
    return _pcall(
        body, name=name, grid=(r // tm,),
        in_specs=[pl.BlockSpec((N_DEV, tm, c), lambda i: (0, i, 0)), blk(), blk(), blk()],
        out_specs=[blk(), blk(), blk(), blk()],
        out_shape=[jax.ShapeDtypeStruct((r, c), F32)] * 4,
        compiler_params=_params(("parallel",)),
    )(parts, w, m, v)


SMALL_ROWS = 56
ROW_DD, ROW_LOSS = 5, 6


def _small_sum(gathered):
    def body(g_ref, o_ref, x_ref):
        s = g_ref[0]
        for dev in range(1, N_DEV):
            s = s + g_ref[dev]
        o_ref[...] = s
        ri = lax.broadcasted_iota(jnp.int32, (D_MODEL, LANES), 0)
        ro = lax.broadcasted_iota(jnp.int32, (D_MODEL, LANES), 1)
        heads = _mmx(jnp.broadcast_to(s[ROW_DD:ROW_DD + 1, :], (SUBLANES, D_MODEL)), (ri // SSD_P == ro).astype(F32))
        loss = _rowsum(jnp.broadcast_to(s[ROW_LOSS:ROW_LOSS + 1, :], (SUBLANES, D_MODEL)))
        row = lax.broadcasted_iota(jnp.int32, (SUBLANES, LANES), 0)
        x_ref[...] = jnp.where(row == 0, heads, jnp.broadcast_to(loss, (SUBLANES, LANES)))

    return _pcall(
        body, name="small_sum",
        out_shape=[jax.ShapeDtypeStruct((SMALL_ROWS, D_MODEL), F32), jax.ShapeDtypeStruct((SUBLANES, LANES), F32)],
        compiler_params=_params(None),
    )(gathered)


def _adam_small(g, w, m, v):
    def body(g_ref, w_ref, m_ref, v_ref, d_ref, nm_ref, nv_ref):
        d_ref[...], nm_ref[...], nv_ref[...] = _adam_math(w_ref[...], g_ref[...], m_ref[...], v_ref[...])

    return _pcall(body, name="adam_small", out_shape=[jax.ShapeDtypeStruct(g.shape, F32)] * 3,
                  compiler_params=_params(None))(g, w, m, v)


def _pack(pieces, rows):
    flat = jnp.concatenate([p.reshape(-1).astype(F32) for p in pieces])
    return jnp.pad(flat, (0, rows * D_MODEL - flat.shape[0])).reshape(rows, D_MODEL)


def _unpack(packed, shapes):
    flat = packed.reshape(-1)
    out, off = [], 0
    for shp in shapes:
        size = 1
        for s in shp:
            size *= s
        out.append(flat[off:off + size].reshape(shp))
        off += size
    return out


def _permute_in(w):
    pad = jnp.zeros((w.shape[0], PROJ_W - D_IN), w.dtype)
    return jnp.concatenate([w[:, 0:4096], w[:, 4112:6672], w[:, 4096:4112], w[:, 6672:6688], pad], axis=1)


def _unpermute_in(g):
    return jnp.concatenate([g[:, 0:4096], g[:, 6656:6672], g[:, 4096:6656], g[:, 6672:6688]], axis=1)


def _lane_row(vec, start):
    return jnp.zeros((LANES,), F32).at[start:start + vec.shape[0]].set(vec)


def _local_step(x, tgt, wp_in, w_out, w_up, w_down, p):
    bsz, seq, d = x.shape
    t = bsz * seq
    x2 = x.reshape(t, d)
    tgt2 = tgt.reshape(t, d)
    tm = min(256, seq)
    tm_wide = min(128, seq)
    sb = min(512, seq)

    gp = jnp.zeros((SUBLANES, LANES), F32).at[0].set(_lane_row(p["gdn_a_log"], 8)).at[1].set(_lane_row(p["gdn_dt_bias"], 8))
    sp = jnp.zeros((SUBLANES, LANES), F32).at[0].set(_lane_row(p["ssd_a_log"], 16)).at[1].set(_lane_row(p["ssd_dt_bias"], 16))
    dvec = jnp.repeat(p["ssd_d"], SSD_P).reshape(1, d)
    row = lambda v: v.reshape(1, -1)
    pre_mix, post_mix, pre_ffn, post_ffn = (row(p[k]) for k in ("pre_mix_norm", "post_mix_norm", "pre_ffn_norm", "post_ffn_norm"))
    gnw, snw = row(p["gdn_norm_w"]), row(p["ssd_norm_w"])
    gcw, scw, scb, fcw, fcb = p["gdn_conv_w"], p["ssd_conv_w"], row(p["ssd_conv_b"]), p["ffn_conv_w"], row(p["ffn_conv_b"])

    h1 = _norm_cast("norm_in", x2, pre_mix, tm)
    proj = _matmul("mm_proj", h1, wp_in, "nn", F32)
    qn, kn, vv, gs = _gdn_prep(proj, gcw, gp, seq, tm)
    o_gdn, gdn_st, gdn_ti = _gdn_chunk_fwd(qn, kn, vv, gs, bsz, seq, sb)
    xs, bc, dtx, acsx, acsb = _ssd_prep(proj, scw, scb, sp, seq, tm)
    y_ssd, ssd_st = _ssd_chunk_fwd(xs, bc, dtx, acsx, acsb, bsz, seq, sb)
    mixin = _gate_norm(o_gdn, y_ssd, xs, proj, gnw, snw, dvec, tm)
    mix = _matmul("mm_out", mixin, w_out, "nn", F32)
    x1, h2 = _mid(x2, mix, post_mix, pre_ffn, tm)
    u_pre = _matmul("mm_up", h2, w_up, "nn", F32)
    act = _ffn_act(u_pre, fcw, fcb, seq, tm_wide)
    f = _matmul("mm_down", act, w_down, "nn", F32, tk=1408)
    dy, df, loss_lanes, d_post_ffn = _final(x1, f, tgt2, post_ffn, tm)

    g_down = _matmul("mm_dw_down", act, df, "tn", F32, tm=1408)
    dact = _matmul("mm_dact", df, w_down, "nt", F32, tn=1408)
    du, d_fcw, d_fcb = _ffn_bwd(u_pre, dact, fcw, fcb, seq, tm_wide)
    du_pre = _conv_t("ffn_conv_t", du, fcw, 3, seq, tm_wide)
    g_up = _matmul("mm_dw_up", h2, du_pre, "tn", F32)
    dh2 = _matmul("mm_dh2", du_pre, w_up, "nt", F32)
    dx1, dmix, d_post_mix, d_pre_ffn = _mid_bwd(x1, mix, dy, dh2, post_mix, pre_ffn, tm)
    g_out = _matmul("mm_dw_out", mixin, dmix, "tn", F32)
    dmixin = _matmul("mm_dmixin", dmix, w_out, "nt", F32)
    do_gdn, dza, dy_ssd, dxs_d, dzs, d_gnw, d_snw, d_dd = _gate_norm_bwd(o_gdn, y_ssd, xs, proj, dmixin, gnw, snw, dvec, tm)
    dxs_c, dbm, dcm, ddtp, dadt = _ssd_chunk_bwd(xs, bc, dtx, acsx, acsb, dy_ssd, ssd_st, bsz, seq, sb)
    dqn, dkn, dvv, dgb = _gdn_chunk_bwd(qn, kn, vv, gs, do_gdn, gdn_st, gdn_ti, bsz, seq, sb)
    dpre_gdn, dsm_gdn, d_gcw, d_gp = _gdn_prep_bwd(proj, dqn, dkn, dvv, dgb, gcw, gp, seq, tm)
    dpre_ssd, dsm, d_scw, d_scb, d_sp = _ssd_prep_bwd(proj, dxs_c, dxs_d, dbm, dcm, ddtp, dadt, dsm_gdn, scw, scb, sp, seq, tm)
    dqkv = _conv_t("gdn_conv_t", dpre_gdn, gcw, 4, seq, tm)
    dxbc = _conv_t("ssd_conv_t", dpre_ssd, scw, 4, seq, tm)
    dproj = jnp.concatenate([dqkv, dza, dzs, dxbc, dsm, jnp.zeros((t, PROJ_W - 6784), BF16)], axis=1)
    g_in = _matmul("mm_dw_in", h1, dproj, "tn", F32)
    dh1 = _matmul("mm_dh1", dproj, wp_in, "nt", F32)
    dx, d_pre_mix = _first_bwd(x2, dh1, dx1, pre_mix, tm)

    small = dict(pre_mix_norm=d_pre_mix, ssd_norm_w=d_snw, post_mix_norm=d_post_mix, pre_ffn_norm=d_pre_ffn,
                 post_ffn_norm=d_post_ffn, dd_lanes=d_dd, loss_lanes=loss_lanes, gdn_gates=d_gp, ssd_gates=d_sp,
                 gdn_norm_w=d_gnw, gdn_conv_w=d_gcw[0:4], ssd_conv_w=d_scw[0:4], ssd_conv_b=d_scb,
                 ffn_conv_w=d_fcw[0:3], ffn_conv_b=d_fcb)
    return dx.reshape(bsz, seq, d), g_in, g_out, g_up, g_down, small


def kernel(x, pre_mix_norm, w_in, gdn_conv_w, gdn_a_log, gdn_dt_bias, gdn_norm_w, ssd_conv_w, ssd_conv_b, ssd_a_log, ssd_dt_bias, ssd_d, ssd_norm_w, w_out, post_mix_norm, pre_ffn_norm, w_up, ffn_conv_w, ffn_conv_b, w_down, post_ffn_norm, loss_target, m_pre_mix_norm, m_w_in, m_gdn_conv_w, m_gdn_a_log, m_gdn_dt_bias, m_gdn_norm_w, m_ssd_conv_w, m_ssd_conv_b, m_ssd_a_log, m_ssd_dt_bias, m_ssd_d, m_ssd_norm_w, m_w_out, m_post_mix_norm, m_pre_ffn_norm, m_w_up, m_ffn_conv_w, m_ffn_conv_b, m_w_down, m_post_ffn_norm, v_pre_mix_norm, v_w_in, v_gdn_conv_w, v_gdn_a_log, v_gdn_dt_bias, v_gdn_norm_w, v_ssd_conv_w, v_ssd_conv_b, v_ssd_a_log, v_ssd_dt_bias, v_ssd_d, v_ssd_norm_w, v_w_out, v_post_mix_norm, v_pre_ffn_norm, v_w_up, v_ffn_conv_w, v_ffn_conv_b, v_w_down, v_post_ffn_norm):
    names = ["pre_mix_norm", "w_in", "gdn_conv_w", "gdn_a_log", "gdn_dt_bias", "gdn_norm_w", "ssd_conv_w", "ssd_conv_b",
             "ssd_a_log", "ssd_dt_bias", "ssd_d", "ssd_norm_w", "w_out", "post_mix_norm", "pre_ffn_norm", "w_up",
             "ffn_conv_w", "ffn_conv_b", "w_down", "post_ffn_norm"]
    w_args = [pre_mix_norm, w_in, gdn_conv_w, gdn_a_log, gdn_dt_bias, gdn_norm_w, ssd_conv_w, ssd_conv_b, ssd_a_log, ssd_dt_bias, ssd_d, ssd_norm_w, w_out, post_mix_norm, pre_ffn_norm, w_up, ffn_conv_w, ffn_conv_b, w_down, post_ffn_norm]
    m_args = [m_pre_mix_norm, m_w_in, m_gdn_conv_w, m_gdn_a_log, m_gdn_dt_bias, m_gdn_norm_w, m_ssd_conv_w, m_ssd_conv_b, m_ssd_a_log, m_ssd_dt_bias, m_ssd_d, m_ssd_norm_w, m_w_out, m_post_mix_norm, m_pre_ffn_norm, m_w_up, m_ffn_conv_w, m_ffn_conv_b, m_w_down, m_post_ffn_norm]
    v_args = [v_pre_mix_norm, v_w_in, v_gdn_conv_w, v_gdn_a_log, v_gdn_dt_bias, v_gdn_norm_w, v_ssd_conv_w, v_ssd_conv_b, v_ssd_a_log, v_ssd_dt_bias, v_ssd_d, v_ssd_norm_w, v_w_out, v_post_mix_norm, v_pre_ffn_norm, v_w_up, v_ffn_conv_w, v_ffn_conv_b, v_w_down, v_post_ffn_norm]
    w = {k: a[0] for k, a in zip(names, w_args)}
    m = {k: a[0] for k, a in zip(names, m_args)}
    v = {k: a[0] for k, a in zip(names, v_args)}
    idx = 4 * lax.axis_index("x") + 2 * lax.axis_index("y") + lax.axis_index("c")
    big = ("w_in", "w_out", "w_up", "w_down")
    conv = ("gdn_conv_w", "ssd_conv_w", "ffn_conv_w")

    conv_local = jnp.concatenate([jnp.pad(w[k], ((0, 4 - w[k].shape[0]), (0, 0))) for k in conv], axis=1)
    g_in, g_out, g_up, g_down, g_conv = _exchange(
        "gather_weights", [w[k].astype(BF16) for k in big] + [conv_local], scatter=False)
    cin, cup = w["w_in"].shape[1], w["w_up"].shape[1]
    wp_in = _permute_in(jnp.transpose(g_in, (1, 0, 2)).reshape(D_MODEL, N_DEV * cin))
    full_up = jnp.transpose(g_up, (1, 0, 2)).reshape(D_MODEL, N_DEV * cup)
    full_out = g_out.reshape(-1, D_MODEL)
    full_down = g_down.reshape(-1, D_MODEL)
    p = {k: w[k] for k in names if k not in big and k not in conv}
    off = 0
    for k in conv:
        cw = w[k].shape[1]
        p[k] = jnp.transpose(g_conv[:, :w[k].shape[0], off:off + cw], (1, 0, 2)).reshape(w[k].shape[0], N_DEV * cw)
        off += cw

    dx, d_in, d_out, d_up, d_down, small = _local_step(x, loss_target, wp_in, full_out, full_up, full_down, p)

    d_in = _unpermute_in(d_in)
    to_peers = [jnp.transpose(d_in.astype(BF16).reshape(D_MODEL, N_DEV, cin), (1, 0, 2)),
                d_out.astype(BF16).reshape(N_DEV, -1, D_MODEL),
                jnp.transpose(d_up.astype(BF16).reshape(D_MODEL, N_DEV, cup), (1, 0, 2)),
                d_down.astype(BF16).reshape(N_DEV, -1, D_MODEL)]
    gate_row = jnp.concatenate([small["gdn_gates"][0], small["gdn_gates"][1], small["ssd_gates"][0], small["ssd_gates"][1],
                                small["gdn_norm_w"][0], jnp.zeros((D_MODEL - 5 * LANES,), F32)]).reshape(1, D_MODEL)
    pack = _pack([small["pre_mix_norm"], small["ssd_norm_w"], small["post_mix_norm"], small["pre_ffn_norm"],
                  small["post_ffn_norm"], small["dd_lanes"], small["loss_lanes"], gate_row,
                  small["gdn_conv_w"], small["ssd_conv_w"], jnp.pad(small["ssd_conv_b"], ((0, 0), (0, 512))),
                  jnp.pad(small["ffn_conv_w"].reshape(-1), (0, 17 * D_MODEL - 3 * 2 * D_FF)),
                  jnp.pad(small["ffn_conv_b"], ((0, 0), (0, 512)))], SMALL_ROWS)
    p_in, p_out, p_up, p_down = _exchange("scatter_grads", to_peers, scatter=True)
    (pack_all,) = _exchange("gather_small", [pack], scatter=False)
    ssum, extra = _small_sum(pack_all)

    grads, deltas, new_m, new_v = {}, {}, {}, {}
    for k, parts in (("w_in", p_in), ("w_out", p_out), ("w_up", p_up), ("w_down", p_down)):
        grads[k], deltas[k], new_m[k], new_v[k] = _adam_big("adam_" + k, parts, w[k], m[k], v[k], 256)

    flat = ssum.reshape(-1)
    gate = ssum[7]
    sg = dict(pre_mix_norm=ssum[0], ssd_norm_w=ssum[1], post_mix_norm=ssum[2], pre_ffn_norm=ssum[3], post_ffn_norm=ssum[4],
              gdn_a_log=gate[8:16], gdn_dt_bias=gate[LANES + 8:LANES + 16], ssd_a_log=gate[2 * LANES + 16:2 * LANES + 32],
              ssd_dt_bias=gate[3 * LANES + 16:3 * LANES + 32], gdn_norm_w=gate[4 * LANES:5 * LANES], ssd_d=extra[0, 0:SSD_HEADS])
    o = 8 * D_MODEL
    full_gcw = flat[o:o + 4 * 3072].reshape(4, 3072)
    o += 12 * D_MODEL
    full_scw = flat[o:o + 4 * 1536].reshape(4, 1536)
    o += 6 * D_MODEL
    sg["ssd_conv_b"] = flat[o:o + 1536]
    o += 2 * D_MODEL
    full_fcw = flat[o:o + 3 * 2 * D_FF].reshape(3, 2 * D_FF)
    o += 17 * D_MODEL
    sg["ffn_conv_b"] = flat[o:o + 2 * D_FF]
    for k, full in (("gdn_conv_w", full_gcw), ("ssd_conv_w", full_scw), ("ffn_conv_w", full_fcw)):
        cw = w[k].shape[1]
        sg[k] = lax.dynamic_slice_in_dim(full, idx * cw, cw, axis=1)
    small_names = [k for k in names if k not in big]
    rows = 24
    gpk = _pack([sg[k] for k in small_names], rows)
    dpk, mpk, vpk = _adam_small(gpk, _pack([w[k] for k in small_names], rows), _pack([m[k] for k in small_names], rows),
                                _pack([v[k] for k in small_names], rows))
    shapes = [w[k].shape for k in small_names]
    for k, g_, d_, m_, v_ in zip(small_names, _unpack(gpk, shapes), _unpack(dpk, shapes), _unpack(mpk, shapes), _unpack(vpk, shapes)):
        grads[k], deltas[k], new_m[k], new_v[k] = g_, d_, m_, v_

    loss = extra[1, 0]
    lead = lambda a: a[None]
    return (loss, dx, *[lead(grads[k]) for k in names], *[lead(deltas[k]) for k in names],
            *[lead(new_m[k]) for k in names], *[lead(new_v[k]) for k in names])
```

```python
import functools

import jax
import jax.numpy as jnp
from jax import lax
from jax.experimental import pallas as pl
from jax.experimental.pallas import tpu as pltpu

F32 = jnp.float32
BF16 = jnp.bfloat16
MXU_DTYPE = jnp.bfloat16
HIGHEST = lax.Precision.HIGHEST
VMEM_LIMIT_V7X = 48 * 1024 * 1024
SUBLANES = 8
LANES = 128

D_MODEL = 1024
GDN_HEADS = 8
GDN_DK = 128
SSD_HEADS = 16
SSD_P = 64
SSD_GROUPS = 2
SSD_HPG = 8
SSD_N = 128
CHUNK = 64
D_FF = 2816
EPS = 1e-6
N_DEV = 8
PROJ_W = 7168
SMALL_CB = 52
D_IN = 6688

ADAM_LR = 0.001
ADAM_B1 = 0.9
ADAM_B2 = 0.999
ADAM_EPS = 1e-08
ADAM_WD = 0.01
ADAM_STEP = 10

NN = (((1,), (0,)), ((), ()))
NT = (((1,), (1,)), ((), ()))
TN = (((0,), (0,)), ((), ()))


def _pcall(body, **kw):
    return pl.pallas_call(body, **kw)


def _mm(a, b, dims=NN):
    return lax.dot_general(a.astype(MXU_DTYPE), b.astype(MXU_DTYPE), dims, preferred_element_type=F32)


def _mmx(a, b, dims=NN):
    return lax.dot_general(a, b, dims, precision=HIGHEST, preferred_element_type=F32)


def _split(a):
    hi = a.astype(MXU_DTYPE)
    return hi, (a - hi.astype(F32)).astype(MXU_DTYPE)


def _mm3(a, b, dims=NN):
    (ah, al), (bh, bl) = _split(a), _split(b)
    dot = lambda p, q: lax.dot_general(p, q, dims, preferred_element_type=F32)
    return dot(ah, bh) + (dot(ah, bl) + dot(al, bh))


def _mmsel(a, sel, dims=NN):
    ah, al = _split(a)
    s = sel.astype(MXU_DTYPE)
    return (lax.dot_general(ah, s, dims, preferred_element_type=F32)
            + lax.dot_general(al, s, dims, preferred_element_type=F32))


def _sigmoid(x):
    return 1.0 / (1.0 + jnp.exp(-x))


def _softplus(x):
    return jnp.maximum(x, 0.0) + jnp.log(1.0 + jnp.exp(-jnp.abs(x)))


def _dsilu(x, s):
    return s * (1.0 + x * (1.0 - s))


def _rowsum(x):
    return jnp.sum(x, axis=1, keepdims=True)


def _colsum(x):
    return jnp.sum(x, axis=0, keepdims=True)


def _pick(dim, pref):
    if dim <= pref:
        return dim
    best = None
    t = LANES
    while t <= pref:
        if dim % t == 0:
            best = t
        t += LANES
    return dim if best is None else best


def _params(sem):
    return pltpu.CompilerParams(dimension_semantics=sem, vmem_limit_bytes=VMEM_LIMIT_V7X)


def _matmul(name, a, b, mode, out_dtype, tm=1024, tn=1024, tk=1024):
    if mode == "nn":
        (m, k), (_, n) = a.shape, b.shape
    elif mode == "nt":
        (m, k), (n, _) = a.shape, b.shape
    else:
        (k, m), (_, n) = a.shape, b.shape
    tm, tn, tk = _pick(m, tm), _pick(n, tn), _pick(k, tk)
    nk = k // tk
    if mode == "tn":
        a_spec = pl.BlockSpec((tk, tm), lambda i, j, kk: (kk, i))
    else:
        a_spec = pl.BlockSpec((tm, tk), lambda i, j, kk: (i, kk))
    if mode == "nt":
        b_spec = pl.BlockSpec((tn, tk), lambda i, j, kk: (j, kk))
    else:
        b_spec = pl.BlockSpec((tk, tn), lambda i, j, kk: (kk, j))
    dims = {"nn": NN, "nt": NT, "tn": TN}[mode]

    def body(a_ref, b_ref, o_ref, acc):
        kk = pl.program_id(2)

        @pl.when(kk == 0)
        def _():
            acc[...] = jnp.zeros_like(acc)

        acc[...] += _mm(a_ref[...], b_ref[...], dims)

        @pl.when(kk == nk - 1)
        def _():
            o_ref[...] = acc[...].astype(out_dtype)

    return _pcall(
        body, name=name, grid=(m // tm, n // tn, nk),
        in_specs=[a_spec, b_spec],
        out_specs=pl.BlockSpec((tm, tn), lambda i, j, kk: (i, j)),
        out_shape=jax.ShapeDtypeStruct((m, n), out_dtype),
        scratch_shapes=[pltpu.VMEM((tm, tn), F32)],
        compiler_params=_params(("parallel", "parallel", "arbitrary")),
    )(a, b)


def _rowwise(name, body, n_rows, tm, ins, outs, accs=()):
    arrays, in_specs = [], []
    last8 = n_rows // SUBLANES - 1
    per = tm // SUBLANES
    for spec in ins:
        kind, arr = spec[0], spec[1]
        if kind == "full":
            in_specs.append(pl.BlockSpec(arr.shape, lambda i, nd=arr.ndim: (0,) * nd))
        else:
            w, cb = spec[2], spec[3]
            if kind == "row":
                in_specs.append(pl.BlockSpec((tm, w), lambda i, cb=cb: (i, cb)))
            elif kind == "prev":
                in_specs.append(pl.BlockSpec((SUBLANES, w), lambda i, cb=cb: (jnp.maximum(i * per - 1, 0), cb)))
            else:
                in_specs.append(pl.BlockSpec((SUBLANES, w), lambda i, cb=cb: (jnp.minimum((i + 1) * per, last8), cb)))
        arrays.append(arr)
    out_shape = [jax.ShapeDtypeStruct((n_rows, w), dt) for (w, dt) in outs]
    out_shape += [jax.ShapeDtypeStruct(s, F32) for s in accs]
    out_specs = [pl.BlockSpec((tm, w), lambda i: (i, 0)) for (w, _) in outs]
    out_specs += [pl.BlockSpec(s, lambda i: (0, 0)) for s in accs]
    n_io = len(ins) + len(outs)

    def kern(*refs):
        i = pl.program_id(0)
        if accs:
            @pl.when(i == 0)
            def _():
                for r in refs[n_io:]:
                    r[...] = jnp.zeros_like(r)
        body(i, *refs)

    res = _pcall(
        kern, name=name, grid=(n_rows // tm,), in_specs=in_specs, out_specs=out_specs, out_shape=out_shape,
        compiler_params=_params(("arbitrary",)),
    )(*arrays)
    return res


def _shift_down(x, halo, j):
    r = pltpu.roll(x, j, 0)
    hr = pltpu.roll(halo, j, 0)
    rows = lax.broadcasted_iota(jnp.int32, (SUBLANES, x.shape[1]), 0)
    top = jnp.where(rows < j, hr, r[0:SUBLANES])
    return jnp.concatenate([top, r[SUBLANES:]], axis=0)


def _shift_up(x, halo, j):
    tm = x.shape[0]
    r = pltpu.roll(x, tm - j, 0)
    hr = pltpu.roll(halo, SUBLANES - j, 0)
    rows = lax.broadcasted_iota(jnp.int32, (SUBLANES, x.shape[1]), 0)
    bot = jnp.where(rows >= SUBLANES - j, hr, r[tm - SUBLANES:])
    return jnp.concatenate([r[:tm - SUBLANES], bot], axis=0)


def _conv_taps(x, halo, kw):
    return [x if kw - 1 - k == 0 else _shift_down(x, halo, kw - 1 - k) for k in range(kw)]


def _conv(taps, w):
    y = taps[0] * w[0:1]
    for k in range(1, len(taps)):
        y = y + taps[k] * w[k:k + 1]
    return y


def _rms(x, width):
    r = lax.rsqrt(jnp.sum(x * x, axis=-1, keepdims=True) * (1.0 / width) + EPS)
    return x * r, r


def _rms_bwd(xh, r, dxh, width):
    return r * (dxh - xh * (jnp.sum(dxh * xh, axis=-1, keepdims=True) * (1.0 / width)))


def _seq_flags(i, seq, tm):
    nps = seq // tm
    pos = i % nps
    return jnp.where(pos == 0, 0.0, 1.0), jnp.where(pos == nps - 1, 0.0, 1.0)


def _norm_cast(name, x, w, tm):
    t, d = x.shape

    def body(i, x_ref, w_ref, h_ref):
        xh, _ = _rms(x_ref[...], d)
        h_ref[...] = (xh * w_ref[...]).astype(BF16)

    return _rowwise(name, body, t, tm, [("row", x, d, 0), ("full", w)], [(d, BF16)])[0]


def _gdn_prep(proj, cw, gp, seq, tm):
    t = proj.shape[0]
    d = D_MODEL

    def body(i, q_ref, qh_ref, k_ref, kh_ref, v_ref, vh_ref, sm_ref, cw_ref, gp_ref, qn_ref, kn_ref, vv_ref, gs_ref):
        keep, _ = _seq_flags(i, seq, tm)
        for x_ref, h_ref, o_ref, off, scale in ((q_ref, qh_ref, qn_ref, 0, GDN_DK ** -0.5),
                                               (k_ref, kh_ref, kn_ref, d, 1.0), (v_ref, vh_ref, vv_ref, 2 * d, None)):
            y = _conv(_conv_taps(x_ref[...], h_ref[...] * keep, 4), cw_ref[:, off:off + d])
            a = y * _sigmoid(y)
            if scale is None:
                o_ref[...] = a
            else:
                for hh in range(GDN_HEADS):
                    s = a[:, hh * GDN_DK:(hh + 1) * GDN_DK]
                    n = lax.rsqrt(_rowsum(s * s) + EPS)
                    o_ref[:, hh * GDN_DK:(hh + 1) * GDN_DK] = s * (n * scale)
        sm = sm_ref[...]
        lane = lax.broadcasted_iota(jnp.int32, sm.shape, 1)
        beta = _sigmoid(sm)
        g = jnp.where((lane >= 8) & (lane < 16), -jnp.exp(gp_ref[0:1, :]) * _softplus(sm + gp_ref[1:2, :]), 0.0)
        gs_ref[...] = jnp.where(lane < 8, beta, _mmx(_block_tri(tm, False), g))

    ins = []
    for cb in range(3):
        ins += [("row", proj, d, cb), ("prev", proj, d, cb)]
    ins += [("row", proj, LANES, SMALL_CB), ("full", cw), ("full", gp)]
    return _rowwise("gdn_prep", body, t, tm, ins, [(d, F32), (d, F32), (d, F32), (LANES, F32)])


def _block_tri(tm, upper):
    ri = lax.broadcasted_iota(jnp.int32, (tm, tm), 0)
    ci = lax.broadcasted_iota(jnp.int32, (tm, tm), 1)
    tri = (ri <= ci) if upper else (ri >= ci)
    return (tri & ((ri // CHUNK) == (ci // CHUNK))).astype(F32)


def _chunk_consts():
    row = lax.broadcasted_iota(jnp.int32, (CHUNK, CHUNK), 0)
    col = lax.broadcasted_iota(jnp.int32, (CHUNK, CHUNK), 1)
    return dict(
        tril=row >= col, strict=row > col, eye=(row == col).astype(F32),
        lane=lax.broadcasted_iota(jnp.int32, (CHUNK, LANES), 1),
        row1=lax.broadcasted_iota(jnp.int32, (CHUNK, 1), 0),
        ones=jnp.ones((CHUNK, LANES), F32))


def _hmap(fn, *lists):
    return [fn(*a) for a in zip(*lists)]


def _tri_inv(nmats, eye):
    x = [eye - n for n in nmats]
    p = _hmap(_mm3, nmats, nmats)
    for lvl in range(5):
        x = _hmap(lambda xi, pi: xi + _mm3(xi, pi), x, p)
        if lvl < 4:
            p = _hmap(_mm3, p, p)
    return x


def _gdn_gates(gs, gc_row, h, c):
    beta = _rowsum(jnp.where(c["lane"] == h, gs, 0.0))
    gc = _rowsum(jnp.where(c["lane"] == h + 8, gs, 0.0))
    dc = jnp.exp(jnp.where(c["tril"], gc - gc_row, -1e30))
    gl = gc[CHUNK - 1:CHUNK, :]
    return beta, dc, jnp.exp(gc), jnp.exp(gl), jnp.exp(gl - gc)


GDN_HB = 4


def _gdn_specs(bsz, seq, sb):
    nsb = seq // sb
    ncb = sb // CHUNK
    fwd = lambda j: j
    bwd = lambda j: nsb - 1 - j
    def specs(order):
        return dict(
            wide=lambda: pl.BlockSpec((1, sb, GDN_HB * GDN_DK), lambda b, h, j: (b, order(j), h)),
            gs=pl.BlockSpec((1, sb, LANES), lambda b, h, j: (b, order(j), 0)),
            gr=pl.BlockSpec((1, GDN_HB, ncb, CHUNK), lambda b, h, j: (b, h, order(j), 0)),
            st=pl.BlockSpec((1, GDN_HB, ncb * GDN_DK, GDN_DK), lambda b, h, j: (b, h, order(j), 0)),
            ti=pl.BlockSpec((1, GDN_HB, sb, CHUNK), lambda b, h, j: (b, h, order(j), 0)))
    return nsb, ncb, specs(fwd), specs(bwd)


def _gdn_chunk_fwd(qn, kn, vv, gs, gr, bsz, seq, sb):
    nsb, ncb, sp, _ = _gdn_specs(bsz, seq, sb)

    def body(q_ref, k_ref, v_ref, gs_ref, gr_ref, o_ref, st_ref, ti_ref, s_scr):
        hg = pl.program_id(1)

        @pl.when(pl.program_id(2) == 0)
        def _():
            s_scr[...] = jnp.zeros_like(s_scr)

        c = _chunk_consts()

        def chunk(n, carry):
            r = pl.ds(pl.multiple_of(n * CHUNK, CHUNK), CHUNK)
            rs = pl.ds(pl.multiple_of(n * GDN_DK, GDN_DK), GDN_DK)
            gsv = gs_ref[0, r, :]
            heads = list(range(GDN_HB))
            sls = [slice(ih * GDN_DK, (ih + 1) * GDN_DK) for ih in heads]
            q = [q_ref[0, r, sl] for sl in sls]
            k = [k_ref[0, r, sl] for sl in sls]
            v = [v_ref[0, r, sl] for sl in sls]
            beta, dc, eg, egl, ekd = zip(*[
                _gdn_gates(gsv, gr_ref[0, ih, pl.ds(n, 1), :], hg * GDN_HB + ih, c) for ih in heads])
            kb = _hmap(lambda a, b: a * b, k, beta)
            amat = _hmap(lambda a, b, d_: jnp.where(c["strict"], _mm(a, b, NT) * d_, 0.0), kb, k, dc)
            tinv = _tri_inv(amat, c["eye"])
            u = _hmap(lambda t_, a, b: _mm3(t_, a * b), tinv, v, beta)
            w = _hmap(lambda t_, a, b: _mm3(t_, a * b), tinv, kb, eg)
            qk = _hmap(lambda a, b, d_: _mm(a, b, NT) * d_, q, k, dc)
            s = [s_scr[ih] for ih in heads]
            v_new = _hmap(lambda a, b, s_: a - _mm(b, s_), u, w, s)
            o = _hmap(lambda a, e, s_, qk_, vn: _mm(a * e, s_) + _mm(qk_, vn), q, eg, s, qk, v_new)
            s_new = _hmap(lambda s_, e, a, f, vn: s_ * e + _mm(a * f, vn, TN), s, egl, k, ekd, v_new)
            for ih in heads:
                o_ref[0, r, sls[ih]] = o[ih]
                st_ref[0, ih, rs, :] = s[ih]
                ti_ref[0, ih, r, :] = tinv[ih]
                s_scr[ih] = s_new[ih]
            return carry

        lax.fori_loop(0, ncb, chunk, 0)

    t3 = (bsz, seq, D_MODEL)
    return _pcall(
        body, name="gdn_chunk_fwd", grid=(bsz, GDN_HEADS // GDN_HB, nsb),
        in_specs=[sp["wide"](), sp["wide"](), sp["wide"](), sp["gs"], sp["gr"]],
        out_specs=[sp["wide"](), sp["st"], sp["ti"]],
        out_shape=[jax.ShapeDtypeStruct(t3, F32),
                   jax.ShapeDtypeStruct((bsz, GDN_HEADS, (seq // CHUNK) * GDN_DK, GDN_DK), F32),
                   jax.ShapeDtypeStruct((bsz, GDN_HEADS, seq, CHUNK), F32)],
        scratch_shapes=[pltpu.VMEM((GDN_HB, GDN_DK, GDN_DK), F32)],
        compiler_params=_params(("parallel", "parallel", "arbitrary")),
    )(qn, kn, vv, gs, gr)


def _ssd_prep(proj, cw, cb, sp, seq, tm):
    t = proj.shape[0]
    d = D_MODEL
    ssd_w = SSD_HEADS * SSD_P

    def body(i, x_ref, xh_ref, bc_ref, bch_ref, sm_ref, cw_ref, cb_ref, sp_ref, xs_ref, bco_ref, dtx_ref, acsx_ref, acs_ref):
        keep, _ = _seq_flags(i, seq, tm)
        y = _conv(_conv_taps(x_ref[...], xh_ref[...] * keep, 4), cw_ref[:, 0:d]) + cb_ref[:, 0:d]
        xs_ref[...] = y * _sigmoid(y)
        y = _conv(_conv_taps(bc_ref[...], bch_ref[...] * keep, 4), cw_ref[:, d:d + 512]) + cb_ref[:, d:d + 512]
        bco_ref[...] = y * _sigmoid(y)
        sm = sm_ref[...]
        lane = lax.broadcasted_iota(jnp.int32, sm.shape, 1)
        valid = (lane >= 16) & (lane < 32)
        dt = jnp.where(valid, _softplus(sm + sp_ref[1:2, :]), 0.0)
        adt = dt * (-jnp.exp(sp_ref[0:1, :]))
        acs = _mmx(_block_tri(tm, False), adt)
        l64 = lax.broadcasted_iota(jnp.int32, (LANES, ssd_w), 0)
        d64 = lax.broadcasted_iota(jnp.int32, (LANES, ssd_w), 1)
        e64 = (l64 - 16 == d64 // SSD_P).astype(F32)
        dtx_ref[...] = _mmx(dt, e64)
        acsx_ref[...] = _mmx(acs, e64)
        acs_ref[...] = acs

    ins = [("row", proj, d, 5), ("prev", proj, d, 5), ("row", proj, 512, 12), ("prev", proj, 512, 12),
           ("row", proj, LANES, SMALL_CB), ("full", cw), ("full", cb), ("full", sp)]
    return _rowwise("ssd_prep", body, t, tm, ins, [(d, F32), (512, F32), (ssd_w, F32), (ssd_w, F32), (LANES, F32)])


SSD_GW = SSD_HPG * SSD_P


def _ssd_head(acs, ar_ref, n, head, cbm, c):
    col = _rowsum(jnp.where(c["lane"] == head + 16, acs, 0.0))
    lm = jnp.exp(jnp.where(c["tril"], col - ar_ref[0, head, pl.ds(n, 1), :], -1e30))
    return lm, cbm * lm


def _ssd_specs(seq, sb):
    nsb = seq // sb
    ncb = sb // CHUNK
    def specs(order):
        return dict(
            wide=lambda: pl.BlockSpec((1, sb, SSD_HEADS * SSD_P), lambda b, j: (b, order(j), 0)),
            bc=lambda: pl.BlockSpec((1, sb, 2 * SSD_GROUPS * SSD_N), lambda b, j: (b, order(j), 0)),
            half=lambda: pl.BlockSpec((1, sb, SSD_GROUPS * SSD_N), lambda b, j: (b, order(j), 0)),
            small=lambda: pl.BlockSpec((1, sb, LANES), lambda b, j: (b, order(j), 0)),
            ar=pl.BlockSpec((1, SSD_HEADS, ncb, CHUNK), lambda b, j: (b, 0, order(j), 0)),
            st=pl.BlockSpec((1, ncb * SSD_N, SSD_HEADS * SSD_P), lambda b, j: (b, order(j), 0)))
    return nsb, ncb, specs(lambda j: j), specs(lambda j: nsb - 1 - j)


def _ssd_chunk_fwd(xs, bc, dtx, acsx, acs, ar, bsz, seq, sb):
    nsb, ncb, sp, _ = _ssd_specs(seq, sb)

    def body(x_ref, dtx_ref, ax_ref, bc_ref, acs_ref, ar_ref, y_ref, sts_ref, st_scr):
        @pl.when(pl.program_id(1) == 0)
        def _():
            st_scr[...] = jnp.zeros_like(st_scr)

        c = _chunk_consts()
        lane5 = lax.broadcasted_iota(jnp.int32, (CHUNK, SSD_GW), 1) // SSD_P

        def chunk(n, carry):
            r = pl.ds(pl.multiple_of(n * CHUNK, CHUNK), CHUNK)
            rs = pl.ds(pl.multiple_of(n * SSD_N, SSD_N), SSD_N)
            acsv = acs_ref[0, r, :]
            for g in range(SSD_GROUPS):
                gl = slice(g * SSD_GW, (g + 1) * SSD_GW)
                x, dt, ax = x_ref[0, r, gl], dtx_ref[0, r, gl], ax_ref[0, r, gl]
                bm = bc_ref[0, r, g * SSD_N:(g + 1) * SSD_N]
                cm = bc_ref[0, r, (SSD_GROUPS + g) * SSD_N:(SSD_GROUPS + g + 1) * SSD_N]
                xdt = x * dt
                cbm = _mm(cm, bm, NT)
                al = ax[CHUNK - 1:CHUNK, :]
                st = st_scr[:, gl]
                y = _mm(cm, st) * jnp.exp(ax)
                for hh in range(SSD_HPG):
                    _, gm = _ssd_head(acsv, ar_ref, n, g * SSD_HPG + hh, cbm, c)
                    y = y + _mm(gm, jnp.where(lane5 == hh, xdt, 0.0))
                y_ref[0, r, gl] = y
                sts_ref[0, rs, gl] = st
                st_scr[:, gl] = st * jnp.exp(al) + _mm(bm, xdt * jnp.exp(al - ax), TN)
            return carry

        lax.fori_loop(0, ncb, chunk, 0)

    return _pcall(
        body, name="ssd_chunk_fwd", grid=(bsz, nsb),
        in_specs=[sp["wide"](), sp["wide"](), sp["wide"](), sp["bc"](), sp["small"](), sp["ar"]],
        out_specs=[sp["wide"](), sp["st"]],
        out_shape=[jax.ShapeDtypeStruct((bsz, seq, SSD_HEADS * SSD_P), F32),
                   jax.ShapeDtypeStruct((bsz, (seq // CHUNK) * SSD_N, SSD_HEADS * SSD_P), F32)],
        scratch_shapes=[pltpu.VMEM((SSD_N, SSD_HEADS * SSD_P), F32)],
        compiler_params=_params(("parallel", "arbitrary")),
    )(xs, dtx, acsx, bc, acs, ar)


def _gate_norm(o_gdn, y_ssd, xs, proj, gnw, snw, dvec, tm):
    t = o_gdn.shape[0]
    d = D_MODEL

    def body(i, o_ref, za_ref, y_ref, xs_ref, zs_ref, gnw_ref, snw_ref, dv_ref, out_ref):
        for hh in range(GDN_HEADS):
            sl = slice(hh * GDN_DK, (hh + 1) * GDN_DK)
            oh, _ = _rms(o_ref[:, sl], GDN_DK)
            z = za_ref[:, sl]
            out_ref[:, sl] = (oh * gnw_ref[...] * (z * _sigmoid(z))).astype(BF16)
        zs = zs_ref[...]
        yg = (y_ref[...] + dv_ref[...] * xs_ref[...]) * (zs * _sigmoid(zs))
        for g in range(SSD_GROUPS):
            sl = slice(g * 512, (g + 1) * 512)
            yh, _ = _rms(yg[:, sl], 512)
            out_ref[:, d + g * 512:d + (g + 1) * 512] = (yh * snw_ref[:, sl]).astype(BF16)

    ins = [("row", o_gdn, d, 0), ("row", proj, d, 3), ("row", y_ssd, d, 0), ("row", xs, d, 0), ("row", proj, d, 4),
           ("full", gnw), ("full", snw), ("full", dvec)]
    return _rowwise("gate_norm", body, t, tm, ins, [(2 * d, BF16)])[0]


def _mid(x, mix, pmw, pfw, tm):
    t, d = x.shape

    def body(i, x_ref, mix_ref, pmw_ref, pfw_ref, x1_ref, h2_ref):
        mh, _ = _rms(mix_ref[...], d)
        x1 = x_ref[...] + mh * pmw_ref[...]
        x1_ref[...] = x1
        xh, _ = _rms(x1, d)
        h2_ref[...] = (xh * pfw_ref[...]).astype(BF16)

    return _rowwise("mid", body, t, tm, [("row", x, d, 0), ("row", mix, d, 0), ("full", pmw), ("full", pfw)],
                    [(d, F32), (d, BF16)])


def _ffn_gate_up(ug_ref, ugh_ref, uu_ref, uuh_ref, cw_ref, cb_ref, keep):
    tg = _conv_taps(ug_ref[...], ugh_ref[...] * keep, 3)
    tu = _conv_taps(uu_ref[...], uuh_ref[...] * keep, 3)
    gate = _conv(tg, cw_ref[:, 0:D_FF]) + cb_ref[:, 0:D_FF]
    up = _conv(tu, cw_ref[:, D_FF:2 * D_FF]) + cb_ref[:, D_FF:2 * D_FF]
    return tg, tu, gate, up


def _ffn_act(u_pre, cw, cb, seq, tm):
    t = u_pre.shape[0]

    def body(i, ug_ref, ugh_ref, uu_ref, uuh_ref, cw_ref, cb_ref, act_ref):
        keep, _ = _seq_flags(i, seq, tm)
        _, _, gate, up = _ffn_gate_up(ug_ref, ugh_ref, uu_ref, uuh_ref, cw_ref, cb_ref, keep)
        act_ref[...] = (gate * _sigmoid(gate) * up).astype(BF16)

    ins = [("row", u_pre, D_FF, 0), ("prev", u_pre, D_FF, 0), ("row", u_pre, D_FF, 1), ("prev", u_pre, D_FF, 1),
           ("full", cw), ("full", cb)]
    return _rowwise("ffn_act", body, t, tm, ins, [(D_FF, BF16)])[0]


def _final(x1, f, tgt, w, tm):
    t, d = x1.shape

    def body(i, x1_ref, f_ref, t_ref, w_ref, dy_ref, df_ref, loss_ref, dw_ref):
        fh, r = _rms(f_ref[...], d)
        e = x1_ref[...] + fh * w_ref[...] - t_ref[...]
        loss_ref[...] += _colsum(e * e) * (0.5 / d)
        dy = e * (1.0 / d)
        dy_ref[...] = dy
        dw_ref[...] += _colsum(dy * fh)
        df_ref[...] = _rms_bwd(fh, r, dy * w_ref[...], d).astype(BF16)

    return _rowwise("final", body, t, tm, [("row", x1, d, 0), ("row", f, d, 0), ("row", tgt, d, 0), ("full", w)],
                    [(d, F32), (d, BF16)], accs=[(1, d), (1, d)])


def _ffn_bwd(u_pre, dact, cw, cb, seq, tm):
    t = u_pre.shape[0]

    def body(i, ug_ref, ugh_ref, uu_ref, uuh_ref, da_ref, cw_ref, cb_ref, du_ref, dcw_ref, dcb_ref):
        keep, _ = _seq_flags(i, seq, tm)
        tg, tu, gate, up = _ffn_gate_up(ug_ref, ugh_ref, uu_ref, uuh_ref, cw_ref, cb_ref, keep)
        sg = _sigmoid(gate)
        da = da_ref[...]
        dgate = da * up * _dsilu(gate, sg)
        dup = da * gate * sg
        du_ref[:, 0:D_FF] = dgate
        du_ref[:, D_FF:2 * D_FF] = dup
        dcb_ref[:, 0:D_FF] += _colsum(dgate)
        dcb_ref[:, D_FF:2 * D_FF] += _colsum(dup)
        for k in range(3):
            dcw_ref[k:k + 1, 0:D_FF] += _colsum(dgate * tg[k])
            dcw_ref[k:k + 1, D_FF:2 * D_FF] += _colsum(dup * tu[k])

    ins = [("row", u_pre, D_FF, 0), ("prev", u_pre, D_FF, 0), ("row", u_pre, D_FF, 1), ("prev", u_pre, D_FF, 1),
           ("row", dact, D_FF, 0), ("full", cw), ("full", cb)]
    return _rowwise("ffn_bwd", body, t, tm, ins, [(2 * D_FF, F32)], accs=[(SUBLANES, 2 * D_FF), (1, 2 * D_FF)])


def _conv_t(name, dy, cw, kw, seq, tm):
    t, width = dy.shape

    def body(i, d_ref, dn_ref, cw_ref, o_ref):
        _, keep = _seq_flags(i, seq, tm)
        d = d_ref[...]
        halo = dn_ref[...] * keep
        acc = d * cw_ref[kw - 1:kw, :]
        for j in range(1, kw):
            acc = acc + _shift_up(d, halo, j) * cw_ref[kw - 1 - j:kw - j, :]
        o_ref[...] = acc.astype(BF16)

    return _rowwise(name, body, t, tm, [("row", dy, width, 0), ("next", dy, width, 0), ("full", cw)], [(width, BF16)])[0]


def _mid_bwd(x1, mix, dy, dh2, pmw, pfw, tm):
    t, d = x1.shape

    def body(i, x1_ref, mix_ref, dy_ref, dh2_ref, pmw_ref, pfw_ref, dx1_ref, dmix_ref, dpm_ref, dpf_ref):
        xh, r2 = _rms(x1_ref[...], d)
        dh2 = dh2_ref[...]
        dpf_ref[...] += _colsum(dh2 * xh)
        dx1 = dy_ref[...] + _rms_bwd(xh, r2, dh2 * pfw_ref[...], d)
        dx1_ref[...] = dx1
        mh, r = _rms(mix_ref[...], d)
        dpm_ref[...] += _colsum(dx1 * mh)
        dmix_ref[...] = _rms_bwd(mh, r, dx1 * pmw_ref[...], d).astype(BF16)

    ins = [("row", x1, d, 0), ("row", mix, d, 0), ("row", dy, d, 0), ("row", dh2, d, 0), ("full", pmw), ("full", pfw)]
    return _rowwise("mid_bwd", body, t, tm, ins, [(d, F32), (d, BF16)], accs=[(1, d), (1, d)])


def _gate_norm_bwd(o_gdn, y_ssd, xs, proj, dmixin, gnw, snw, dvec, tm):
    t = o_gdn.shape[0]
    d = D_MODEL

    def body(i, o_ref, za_ref, y_ref, xs_ref, zs_ref, dma_ref, dms_ref, gnw_ref, snw_ref, dv_ref,
             do_ref, dza_ref, dy_ref, dxs_ref, dzs_ref, dgnw_ref, dsnw_ref, dd_ref):
        for hh in range(GDN_HEADS):
            sl = slice(hh * GDN_DK, (hh + 1) * GDN_DK)
            oh, r = _rms(o_ref[:, sl], GDN_DK)
            z = za_ref[:, sl]
            sz = _sigmoid(z)
            dm = dma_ref[:, sl]
            don = dm * (z * sz)
            dza_ref[:, sl] = (dm * oh * gnw_ref[...] * _dsilu(z, sz)).astype(BF16)
            dgnw_ref[...] += _colsum(don * oh)
            do_ref[:, sl] = _rms_bwd(oh, r, don * gnw_ref[...], GDN_DK)
        zs = zs_ref[...]
        sz = _sigmoid(zs)
        sil = zs * sz
        x = xs_ref[...]
        y0 = y_ref[...] + dv_ref[...] * x
        yg = y0 * sil
        dms = dms_ref[...]
        for g in range(SSD_GROUPS):
            sl = slice(g * 512, (g + 1) * 512)
            yh, r = _rms(yg[:, sl], 512)
            dsnw_ref[:, sl] += _colsum(dms[:, sl] * yh)
            dyg = _rms_bwd(yh, r, dms[:, sl] * snw_ref[:, sl], 512)
            dy0 = dyg * sil[:, sl]
            dzs_ref[:, sl] = (dyg * y0[:, sl] * _dsilu(zs[:, sl], sz[:, sl])).astype(BF16)
            dy_ref[:, sl] = dy0
            dxs_ref[:, sl] = dy0 * dv_ref[:, sl]
            dd_ref[:, sl] += _colsum(dy0 * x[:, sl])

    ins = [("row", o_gdn, d, 0), ("row", proj, d, 3), ("row", y_ssd, d, 0), ("row", xs, d, 0), ("row", proj, d, 4),
           ("row", dmixin, d, 0), ("row", dmixin, d, 1), ("full", gnw), ("full", snw), ("full", dvec)]
    return _rowwise("gate_norm_bwd", body, t, tm, ins, [(d, F32), (d, BF16), (d, F32), (d, F32), (d, BF16)],
                    accs=[(1, GDN_DK), (1, d), (1, d)])


def _ssd_chunk_bwd(xs, bc, dtx, acsx, acs, ar, dy, sts, bsz, seq, sb):
    nsb, ncb, _, sp = _ssd_specs(seq, sb)

    def body(x_ref, dtx_ref, ax_ref, bc_ref, acs_ref, ar_ref, dy_ref, sts_ref, dx_ref, dbc_ref, ddt_ref, dacs_ref, dst_scr):
        @pl.when(pl.program_id(1) == 0)
        def _():
            dst_scr[...] = jnp.zeros_like(dst_scr)

        c = _chunk_consts()
        lane5 = lax.broadcasted_iota(jnp.int32, (CHUNK, SSD_GW), 1) // SSD_P
        row5 = lax.broadcasted_iota(jnp.int32, (CHUNK, SSD_GW), 0)
        sel_in = lax.broadcasted_iota(jnp.int32, (SSD_GW, LANES), 0) // SSD_P
        sel_out = lax.broadcasted_iota(jnp.int32, (SSD_GW, LANES), 1)

        def chunk(nn, carry):
            n = ncb - 1 - nn
            r = pl.ds(pl.multiple_of(n * CHUNK, CHUNK), CHUNK)
            rs = pl.ds(pl.multiple_of(n * SSD_N, SSD_N), SSD_N)
            acsv = acs_ref[0, r, :]
            ddt = jnp.zeros((CHUNK, LANES), F32)
            dacs = jnp.zeros((CHUNK, LANES), F32)
            for g in range(SSD_GROUPS):
                gl = slice(g * SSD_GW, (g + 1) * SSD_GW)
                x, dt, ax, dyv = x_ref[0, r, gl], dtx_ref[0, r, gl], ax_ref[0, r, gl], dy_ref[0, r, gl]
                bm = bc_ref[0, r, g * SSD_N:(g + 1) * SSD_N]
                cm = bc_ref[0, r, (SSD_GROUPS + g) * SSD_N:(SSD_GROUPS + g + 1) * SSD_N]
                st = sts_ref[0, rs, gl]
                dst = dst_scr[:, gl]
                rsel = (sel_in + (16 + g * SSD_HPG) == sel_out).astype(F32)
                xdt = x * dt
                cbm = _mm(cm, bm, NT)
                al = ax[CHUNK - 1:CHUNK, :]
                ex, el = jnp.exp(ax), jnp.exp(al)
                dec = jnp.exp(al - ax)
                xd = xdt * dec
                dye = dyv * ex
                dxd = _mm(bm, dst)
                dxdt = dec * dxd
                dcm = _mm(dye, st, NT)
                dbm = _mm(xd, dst, NT)
                z = dye * _mm(cm, st) - dxd * xd
                zl = _colsum(dst * st) * el + _colsum(dxd * xd)
                z = z + jnp.where(row5 == CHUNK - 1, zl, 0.0)
                dcb = jnp.zeros((CHUNK, CHUNK), F32)
                for hh in range(SSD_HPG):
                    head = g * SSD_HPG + hh
                    lm, gm = _ssd_head(acsv, ar_ref, n, head, cbm, c)
                    dym = jnp.where(lane5 == hh, dyv, 0.0)
                    dxdt = dxdt + _mm(gm, dym, TN)
                    dg = _mm(dym, xdt, NT)
                    dcb = dcb + dg * lm
                    pm = dg * gm
                    dacs = dacs + jnp.where(c["lane"] == head + 16, _rowsum(pm) - _mmsel(pm, c["ones"], TN), 0.0)
                dbc_ref[0, r, (SSD_GROUPS + g) * SSD_N:(SSD_GROUPS + g + 1) * SSD_N] = dcm + _mm(dcb, bm)
                dbc_ref[0, r, g * SSD_N:(g + 1) * SSD_N] = dbm + _mm(dcb, cm, TN)
                dacs = dacs + _mmsel(z, rsel)
                ddt = ddt + _mmsel(dxdt * x, rsel)
                dx_ref[0, r, gl] = dxdt * dt
                dst_scr[:, gl] = dst * el + _mm(cm, dye, TN)
            ddt_ref[0, r, :] = ddt
            dacs_ref[0, r, :] = dacs
            return carry

        lax.fori_loop(0, ncb, chunk, 0)

    return _pcall(
        body, name="ssd_chunk_bwd", grid=(bsz, nsb),
        in_specs=[sp["wide"](), sp["wide"](), sp["wide"](), sp["bc"](), sp["small"](), sp["ar"], sp["wide"](), sp["st"]],
        out_specs=[sp["wide"](), sp["bc"](), sp["small"](), sp["small"]()],
        out_shape=[jax.ShapeDtypeStruct((bsz, seq, SSD_HEADS * SSD_P), F32),
                   jax.ShapeDtypeStruct((bsz, seq, 2 * SSD_GROUPS * SSD_N), F32),
                   jax.ShapeDtypeStruct((bsz, seq, LANES), F32), jax.ShapeDtypeStruct((bsz, seq, LANES), F32)],
        scratch_shapes=[pltpu.VMEM((SSD_N, SSD_HEADS * SSD_P), F32)],
        compiler_params=_params(("parallel", "arbitrary")),
    )(xs, dtx, acsx, bc, acs, ar, dy, sts)


def _ssd_prep_bwd(proj, dxs_c, dxs_d, dbc, ddt, dacs, dsm_gdn, cw, cb, sp, seq, tm):
    t = proj.shape[0]
    d = D_MODEL

    def body(i, x_ref, xh_ref, bc_ref, bch_ref, sm_ref, dxc_ref, dxd_ref, dbc_ref, ddt_ref, dacs_ref, dsg_ref,
             cw_ref, cb_ref, sp_ref, dpre_ref, dsm_ref, dcw_ref, dcb_ref, dsp_ref):
        keep, _ = _seq_flags(i, seq, tm)
        parts = ((x_ref, xh_ref, 0, d, (dxc_ref[...] + dxd_ref[...],)),
                 (bc_ref, bch_ref, d, 512, (dbc_ref[...],)))
        for xr, hr, off, w, grads in parts:
            taps = _conv_taps(xr[...], hr[...] * keep, 4)
            y = _conv(taps, cw_ref[:, off:off + w]) + cb_ref[:, off:off + w]
            ds_ = _dsilu(y, _sigmoid(y))
            o = 0
            for gr in grads:
                wg = gr.shape[1]
                dpre = gr * ds_[:, o:o + wg]
                dpre_ref[:, off + o:off + o + wg] = dpre
                dcb_ref[:, off + o:off + o + wg] += _colsum(dpre)
                for k in range(4):
                    dcw_ref[k:k + 1, off + o:off + o + wg] += _colsum(dpre * taps[k][:, o:o + wg])
                o += wg
        sm = sm_ref[...]
        lane = lax.broadcasted_iota(jnp.int32, sm.shape, 1)
        valid = (lane >= 16) & (lane < 32)
        xb = sm + sp_ref[1:2, :]
        dt = jnp.where(valid, _softplus(xb), 0.0)
        a_neg = -jnp.exp(sp_ref[0:1, :])
        dadt_s = _mmx(_block_tri(tm, True), dacs_ref[...])
        dxb = jnp.where(valid, (ddt_ref[...] + dadt_s * a_neg) * _sigmoid(xb), 0.0)
        dsm_ref[...] = (dsg_ref[...] + dxb).astype(BF16)
        dsp_ref[1:2, :] += _colsum(dxb)
        dsp_ref[0:1, :] += jnp.where(valid[0:1, :], _colsum(dadt_s * dt) * a_neg, 0.0)

    ins = [("row", proj, d, 5), ("prev", proj, d, 5), ("row", proj, 512, 12), ("prev", proj, 512, 12),
           ("row", proj, LANES, SMALL_CB), ("row", dxs_c, d, 0), ("row", dxs_d, d, 0), ("row", dbc, 512, 0),
           ("row", ddt, LANES, 0), ("row", dacs, LANES, 0), ("row", dsm_gdn, LANES, 0),
           ("full", cw), ("full", cb), ("full", sp)]
    return _rowwise("ssd_prep_bwd", body, t, tm, ins, [(d + 512, F32), (LANES, BF16)],
                    accs=[(SUBLANES, d + 512), (1, d + 512), (SUBLANES, LANES)])


def _gdn_chunk_bwd(qn, kn, vv, gs, gr, do, sts, tis, bsz, seq, sb):
    nsb, ncb, _, sp = _gdn_specs(bsz, seq, sb)

    def body(q_ref, k_ref, v_ref, gs_ref, gr_ref, do_ref, st_ref, ti_ref, dq_ref, dk_ref, dv_ref, dgb_ref, ds_scr):
        hg = pl.program_id(1)

        @pl.when(pl.program_id(2) == 0)
        def _():
            ds_scr[...] = jnp.zeros_like(ds_scr)

        c = _chunk_consts()

        def chunk(nn, carry):
            n = ncb - 1 - nn
            r = pl.ds(pl.multiple_of(n * CHUNK, CHUNK), CHUNK)
            rs = pl.ds(pl.multiple_of(n * GDN_DK, GDN_DK), GDN_DK)
            gsv = gs_ref[0, r, :]
            heads = list(range(GDN_HB))
            sls = [slice(ih * GDN_DK, (ih + 1) * GDN_DK) for ih in heads]
            q = [q_ref[0, r, sl] for sl in sls]
            k = [k_ref[0, r, sl] for sl in sls]
            v = [v_ref[0, r, sl] for sl in sls]
            do_ = [do_ref[0, r, sl] for sl in sls]
            s = [st_ref[0, ih, rs, :] for ih in heads]
            tinv = [ti_ref[0, ih, r, :] for ih in heads]
            dsn = [ds_scr[ih] for ih in heads]
            beta, dc, eg, egl, ekd = zip(*[
                _gdn_gates(gsv, gr_ref[0, ih, pl.ds(n, 1), :], hg * GDN_HB + ih, c) for ih in heads])
            mul = lambda a, b: a * b
            kb = _hmap(mul, k, beta)
            rhs_w = _hmap(mul, kb, eg)
            u = _hmap(lambda t_, a, b: _mm3(t_, a * b), tinv, v, beta)
            w = _hmap(_mm3, tinv, rhs_w)
            amat = _hmap(lambda a, b, d_: jnp.where(c["strict"], _mm(a, b, NT) * d_, 0.0), kb, k, dc)
            qk = _hmap(lambda a, b, d_: _mm(a, b, NT) * d_, q, k, dc)
            qd = _hmap(mul, q, eg)
            kd = _hmap(mul, k, ekd)
            v_new = _hmap(lambda a, b, s_: a - _mm(b, s_), u, w, s)
            dv_new = _hmap(lambda qk_, d_, kd_, dn: _mm(qk_, d_, TN) + _mm(kd_, dn), qk, do_, kd, dsn)
            dqk = _hmap(lambda d_, vn: _mm(d_, vn, NT), do_, v_new)
            dqd = _hmap(lambda d_, s_: _mm(d_, s_, NT), do_, s)
            ds_new = _hmap(lambda qd_, d_, dn, e, w_, dvn: _mm(qd_, d_, TN) + dn * e - _mm(w_, dvn, TN),
                           qd, do_, dsn, egl, w, dv_new)
            dkd = _hmap(lambda vn, dn: _mm(vn, dn, NT), v_new, dsn)
            dgl = _hmap(lambda s_, dn, e: _colsum(_rowsum(s_ * dn)) * e, s, dsn, egl)
            dw = _hmap(lambda dvn, s_: -_mm(dvn, s_, NT), dv_new, s)
            dru = _hmap(lambda t_, a: _mm3(t_, a, TN), tinv, dv_new)
            drw = _hmap(lambda t_, a: _mm3(t_, a, TN), tinv, dw)
            da = _hmap(lambda a, u_, b, w_: jnp.where(c["strict"], -(_mm(a, u_, NT) + _mm(b, w_, NT)), 0.0), dru, u, drw, w)
            m = _hmap(mul, da, dc)
            dkb = _hmap(lambda a, e, m_, k_: a * e + _mm(m_, k_), drw, eg, m, k)
            mq = _hmap(mul, dqk, dc)
            dq = _hmap(lambda mq_, k_, a, e: _mm(mq_, k_) + a * e, mq, k, dqd, eg)
            dk = _hmap(lambda m_, kb_, mq_, q_, a, e, b, be: _mm(m_, kb_, TN) + _mm(mq_, q_, TN) + a * e + b * be,
                       m, kb, mq, q, dkd, ekd, dkb, beta)
            dbeta = _hmap(lambda a, v_, b, k_: _rowsum(a * v_) + _rowsum(b * k_), dru, v, dkb, k)
            pq = _hmap(lambda a, am, b, qk_: a * am + b * qk_, da, amat, dqk, qk)
            ekk = _hmap(lambda a, b: _rowsum(a * b), dkd, kd)
            dgc = _hmap(lambda pq_, a, rw, b, qd_, e, gl_: (
                _rowsum(pq_) - _mmsel(pq_, c["ones"], TN) + (_rowsum(a * rw) + _rowsum(b * qd_) - e)
                + jnp.where(c["row1"] == CHUNK - 1, _colsum(e) + gl_, 0.0)), pq, drw, rhs_w, dqd, qd, ekk, dgl)
            for ih in heads:
                ds_scr[ih] = ds_new[ih]
                dv_ref[0, r, sls[ih]] = dru[ih] * beta[ih]
                dq_ref[0, r, sls[ih]] = dq[ih]
                dk_ref[0, r, sls[ih]] = dk[ih]
                dgb_ref[0, r, sls[ih]] = jnp.where(c["lane"] == 0, dbeta[ih], jnp.where(c["lane"] == 1, dgc[ih], 0.0))
            return carry

        lax.fori_loop(0, ncb, chunk, 0)

    return _pcall(
        body, name="gdn_chunk_bwd", grid=(bsz, GDN_HEADS // GDN_HB, nsb),
        in_specs=[sp["wide"](), sp["wide"](), sp["wide"](), sp["gs"], sp["gr"], sp["wide"](), sp["st"], sp["ti"]],
        out_specs=[sp["wide"](), sp["wide"](), sp["wide"](), sp["wide"]()],
        out_shape=[jax.ShapeDtypeStruct((bsz, seq, D_MODEL), F32)] * 4,
        scratch_shapes=[pltpu.VMEM((GDN_HB, GDN_DK, GDN_DK), F32)],
        compiler_params=_params(("parallel", "parallel", "arbitrary")),
    )(qn, kn, vv, gs, gr, do, sts, tis)


def _gdn_prep_bwd(proj, dqn, dkn, dvv, dgb, cw, gp, seq, tm):
    t = proj.shape[0]
    d = D_MODEL

    def body(i, q_ref, qh_ref, k_ref, kh_ref, v_ref, vh_ref, sm_ref, dq_ref, dk_ref, dv_ref, dgb_ref, cw_ref, gp_ref,
             dpre_ref, dsm_ref, dcw_ref, dgp_ref):
        keep, _ = _seq_flags(i, seq, tm)
        for x_ref, h_ref, g_ref, off, scale in ((q_ref, qh_ref, dq_ref, 0, GDN_DK ** -0.5),
                                               (k_ref, kh_ref, dk_ref, d, 1.0), (v_ref, vh_ref, dv_ref, 2 * d, None)):
            taps = _conv_taps(x_ref[...], h_ref[...] * keep, 4)
            y = _conv(taps, cw_ref[:, off:off + d])
            sy = _sigmoid(y)
            ds_ = _dsilu(y, sy)
            if scale is None:
                dpre = g_ref[...] * ds_
                dpre_ref[:, off:off + d] = dpre
                for k in range(4):
                    dcw_ref[k:k + 1, off:off + d] += _colsum(dpre * taps[k])
            else:
                a = y * sy
                for hh in range(GDN_HEADS):
                    sl = slice(hh * GDN_DK, (hh + 1) * GDN_DK)
                    s = a[:, sl]
                    n = lax.rsqrt(_rowsum(s * s) + EPS)
                    ah = s * n
                    gq = g_ref[:, sl]
                    dpre = (scale * n) * (gq - ah * _rowsum(gq * ah)) * ds_[:, sl]
                    dpre_ref[:, off + hh * GDN_DK:off + (hh + 1) * GDN_DK] = dpre
                    for k in range(4):
                        dcw_ref[k:k + 1, off + hh * GDN_DK:off + (hh + 1) * GDN_DK] += _colsum(dpre * taps[k][:, sl])
        sm = sm_ref[...]
        lane = lax.broadcasted_iota(jnp.int32, sm.shape, 1)
        si = lax.broadcasted_iota(jnp.int32, (d, LANES), 0)
        so = lax.broadcasted_iota(jnp.int32, (d, LANES), 1)
        sel = (((si % GDN_DK == 0) & (so == si // GDN_DK)) | ((si % GDN_DK == 1) & (so == si // GDN_DK + 8))).astype(F32)
        dsel = _mmx(dgb_ref[...], sel)
        is_g = (lane >= 8) & (lane < 16)
        dsel = jnp.where(is_g, _mmx(_block_tri(tm, True), dsel), dsel)
        beta = _sigmoid(sm)
        xb = sm + gp_ref[1:2, :]
        a_neg = -jnp.exp(gp_ref[0:1, :])
        sp = _softplus(xb)
        dxb = jnp.where(is_g, dsel * a_neg * _sigmoid(xb), 0.0)
        dsm_ref[...] = jnp.where(lane < 8, dsel * beta * (1.0 - beta), dxb)
        dgp_ref[1:2, :] += _colsum(dxb)
        dgp_ref[0:1, :] += _colsum(jnp.where(is_g, dsel * a_neg * sp, 0.0))

    ins = []
    for cb in range(3):
        ins += [("row", proj, d, cb), ("prev", proj, d, cb)]
    ins += [("row", proj, LANES, SMALL_CB), ("row", dqn, d, 0), ("row", dkn, d, 0), ("row", dvv, d, 0), ("row", dgb, d, 0),
            ("full", cw), ("full", gp)]
    return _rowwise("gdn_prep_bwd", body, t, tm, ins, [(3 * d, F32), (LANES, F32)],
                    accs=[(SUBLANES, 3 * d), (SUBLANES, LANES)])


def _first_bwd(x, dh1, dx1, w, tm):
    t, d = x.shape

    def body(i, x_ref, dh_ref, dx1_ref, w_ref, dx_ref, dw_ref):
        xh, r = _rms(x_ref[...], d)
        dh = dh_ref[...]
        dw_ref[...] += _colsum(dh * xh)
        dx_ref[...] = dx1_ref[...] + _rms_bwd(xh, r, dh * w_ref[...], d)

    return _rowwise("first_bwd", body, t, tm, [("row", x, d, 0), ("row", dh1, d, 0), ("row", dx1, d, 0), ("full", w)],
                    [(d, F32)], accs=[(1, d)])


def _exchange(name, arrays, scatter):
    n = len(arrays)

    def body(*refs):
        ins, outs = refs[:n], refs[n:2 * n]
        send_sems, recv_sems, loc_sems = refs[2 * n:]
        x, y, c = lax.axis_index("x"), lax.axis_index("y"), lax.axis_index("c")
        me = 4 * x + 2 * y + c
        copies = []
        for t in range(n):
            src_me = ins[t].at[me] if scatter else ins[t]
            loc = pltpu.make_async_copy(src_me, outs[t].at[me], loc_sems.at[t])
            loc.start()
            copies.append((loc, None))
            for k in range(N_DEV - 1):
                bx, by, bc = ((k + 1) >> 2) & 1, ((k + 1) >> 1) & 1, (k + 1) & 1
                px = 1 - x if bx else x
                py = 1 - y if by else y
                pc = 1 - c if bc else c
                peer = 4 * px + 2 * py + pc
                src = ins[t].at[peer] if scatter else ins[t]
                send = pltpu.make_async_remote_copy(
                    src_ref=src, dst_ref=outs[t].at[me], send_sem=send_sems.at[t, k], recv_sem=recv_sems.at[t, k],
                    device_id=(px, py, pc), device_id_type=pl.DeviceIdType.MESH)
                send.start()
                recv = pltpu.make_async_remote_copy(
                    src_ref=src, dst_ref=outs[t].at[peer], send_sem=send_sems.at[t, k], recv_sem=recv_sems.at[t, k],
                    device_id=(px, py, pc), device_id_type=pl.DeviceIdType.MESH)
                copies.append((send, recv))
        for first, second in copies:
            if second is None:
                first.wait()
            else:
                first.wait_send()
                second.wait_recv()

    out_shape = []
    for a in arrays:
        shp = a.shape if scatter else (N_DEV,) + a.shape
        out_shape.append(jax.ShapeDtypeStruct(shp, a.dtype))
    return _pcall(
        body, name=name,
        in_specs=[pl.BlockSpec(memory_space=pl.ANY)] * n,
        out_specs=[pl.BlockSpec(memory_space=pl.ANY)] * n,
        out_shape=out_shape,
        scratch_shapes=[pltpu.SemaphoreType.DMA((n, N_DEV - 1)), pltpu.SemaphoreType.DMA((n, N_DEV - 1)),
                        pltpu.SemaphoreType.DMA((n,))],
        compiler_params=pltpu.CompilerParams(has_side_effects=True),
    )(*arrays)


def _adam_math(w, g, m, v):
    m = ADAM_B1 * m + (1.0 - ADAM_B1) * g
    v = ADAM_B2 * v + (1.0 - ADAM_B2) * (g * g)
    m_hat = m / (1.0 - ADAM_B1 ** ADAM_STEP)
    v_hat = v / (1.0 - ADAM_B2 ** ADAM_STEP)
    delta = -ADAM_LR * (m_hat / (jnp.sqrt(v_hat) + ADAM_EPS) + ADAM_WD * w)
    return delta, m, v


def _adam_big(name, parts, w, m, v, tm):
    r, c = w.shape
    tm = tm if r % tm == 0 else r

    def body(p_ref, w_ref, m_ref, v_ref, g_ref, d_ref, nm_ref, nv_ref):
        g = p_ref[0].astype(F32)
        for s in range(1, N_DEV):
            g = g + p_ref[s].astype(F32)
        g_ref[...] = g
        d_ref[...], nm_ref[...], nv_ref[...] = _adam_math(w_ref[...], g, m_ref[...], v_ref[...])

    blk = lambda: pl.BlockSpec((tm, c), lambda i: (i, 0))
    return _pcall(
        body, name=name, grid=(r // tm,),
        in_specs=[pl.BlockSpec((N_DEV, tm, c), lambda i: (0, i, 0)), blk(), blk(), blk()],
        out_specs=[blk(), blk(), blk(), blk()],
        out_shape=[jax.ShapeDtypeStruct((r, c), F32)] * 4,
        compiler_params=_params(("parallel",)),
    )(parts, w, m, v)


SMALL_ROWS = 56
ROW_DD, ROW_LOSS = 5, 6


def _small_sum(gathered):
    def body(g_ref, o_ref, x_ref):
        s = g_ref[0]
        for dev in range(1, N_DEV):
            s = s + g_ref[dev]
        o_ref[...] = s
        ri = lax.broadcasted_iota(jnp.int32, (D_MODEL, LANES), 0)
        ro = lax.broadcasted_iota(jnp.int32, (D_MODEL, LANES), 1)
        heads = _mmx(jnp.broadcast_to(s[ROW_DD:ROW_DD + 1, :], (SUBLANES, D_MODEL)), (ri // SSD_P == ro).astype(F32))
        loss = _rowsum(jnp.broadcast_to(s[ROW_LOSS:ROW_LOSS + 1, :], (SUBLANES, D_MODEL)))
        row = lax.broadcasted_iota(jnp.int32, (SUBLANES, LANES), 0)
        x_ref[...] = jnp.where(row == 0, heads, jnp.broadcast_to(loss, (SUBLANES, LANES)))

    return _pcall(
        body, name="small_sum",
        out_shape=[jax.ShapeDtypeStruct((SMALL_ROWS, D_MODEL), F32), jax.ShapeDtypeStruct((SUBLANES, LANES), F32)],
        compiler_params=_params(None),
    )(gathered)


def _adam_small(g, w, m, v):
    def body(g_ref, w_ref, m_ref, v_ref, d_ref, nm_ref, nv_ref):
        d_ref[...], nm_ref[...], nv_ref[...] = _adam_math(w_ref[...], g_ref[...], m_ref[...], v_ref[...])

    return _pcall(body, name="adam_small", out_shape=[jax.ShapeDtypeStruct(g.shape, F32)] * 3,
                  compiler_params=_params(None))(g, w, m, v)


def _pack(pieces, rows):
    flat = jnp.concatenate([p.reshape(-1).astype(F32) for p in pieces])
    return jnp.pad(flat, (0, rows * D_MODEL - flat.shape[0])).reshape(rows, D_MODEL)


def _unpack(packed, shapes):
    flat = packed.reshape(-1)
    out, off = [], 0
    for shp in shapes:
        size = 1
        for s in shp:
            size *= s
        out.append(flat[off:off + size].reshape(shp))
        off += size
    return out


def _permute_in(w):
    pad = jnp.zeros((w.shape[0], PROJ_W - D_IN), w.dtype)
    return jnp.concatenate([w[:, 0:4096], w[:, 4112:6672], w[:, 4096:4112], w[:, 6672:6688], pad], axis=1)


def _unpermute_in(g):
    return jnp.concatenate([g[:, 0:4096], g[:, 6656:6672], g[:, 4096:6656], g[:, 6672:6688]], axis=1)


def _lane_row(vec, start):
    return jnp.zeros((LANES,), F32).at[start:start + vec.shape[0]].set(vec)


def _local_step(x, tgt, wp_in, w_out, w_up, w_down, p):
    bsz, seq, d = x.shape
    t = bsz * seq
    x2 = x.reshape(t, d)
    tgt2 = tgt.reshape(t, d)
    tm = min(256, seq)
    tm_wide = min(128, seq)
    sb = min(512, seq)

    gp = jnp.zeros((SUBLANES, LANES), F32).at[0].set(_lane_row(p["gdn_a_log"], 8)).at[1].set(_lane_row(p["gdn_dt_bias"], 8))
    sp = jnp.zeros((SUBLANES, LANES), F32).at[0].set(_lane_row(p["ssd_a_log"], 16)).at[1].set(_lane_row(p["ssd_dt_bias"], 16))
    dvec = jnp.repeat(p["ssd_d"], SSD_P).reshape(1, d)
    row = lambda v: v.reshape(1, -1)
    pre_mix, post_mix, pre_ffn, post_ffn = (row(p[k]) for k in ("pre_mix_norm", "post_mix_norm", "pre_ffn_norm", "post_ffn_norm"))
    gnw, snw = row(p["gdn_norm_w"]), row(p["ssd_norm_w"])
    gcw, scw, scb, fcw, fcb = p["gdn_conv_w"], p["ssd_conv_w"], row(p["ssd_conv_b"]), p["ffn_conv_w"], row(p["ffn_conv_b"])

    h1 = _norm_cast("norm_in", x2, pre_mix, tm)
    proj = _matmul("mm_proj", h1, wp_in, "nn", F32)
    b3 = lambda a: a.reshape(bsz, seq, a.shape[-1])
    b2 = lambda a: a.reshape(t, a.shape[-1])
    rows_of = lambda a, lo, n: jnp.transpose(a[:, lo:lo + n].reshape(bsz, seq // CHUNK, CHUNK, n), (0, 3, 1, 2))
    qn, kn, vv, gs = (b3(a) for a in _gdn_prep(proj, gcw, gp, seq, tm))
    gr = rows_of(b2(gs), 8, GDN_HEADS)
    o_gdn, gdn_st, gdn_ti = _gdn_chunk_fwd(qn, kn, vv, gs, gr, bsz, seq, sb)
    o_gdn = b2(o_gdn)
    xs, bc, dtx, acsx, acs = _ssd_prep(proj, scw, scb, sp, seq, tm)
    ar = rows_of(acs, 16, SSD_HEADS)
    y_ssd, ssd_st = _ssd_chunk_fwd(b3(xs), b3(bc), b3(dtx), b3(acsx), b3(acs), ar, bsz, seq, sb)
    y_ssd = b2(y_ssd)
    mixin = _gate_norm(o_gdn, y_ssd, xs, proj, gnw, snw, dvec, tm)
    mix = _matmul("mm_out", mixin, w_out, "nn", F32)
    x1, h2 = _mid(x2, mix, post_mix, pre_ffn, tm)
    u_pre = _matmul("mm_up", h2, w_up, "nn", F32)
    act = _ffn_act(u_pre, fcw, fcb, seq, tm_wide)
    f = _matmul("mm_down", act, w_down, "nn", F32, tk=1408)
    dy, df, loss_lanes, d_post_ffn = _final(x1, f, tgt2, post_ffn, tm)

    g_down = _matmul("mm_dw_down", act, df, "tn", F32, tm=1408)
    dact = _matmul("mm_dact", df, w_down, "nt", F32, tn=1408)
    du, d_fcw, d_fcb = _ffn_bwd(u_pre, dact, fcw, fcb, seq, tm_wide)
    du_pre = _conv_t("ffn_conv_t", du, fcw, 3, seq, tm_wide)
    g_up = _matmul("mm_dw_up", h2, du_pre, "tn", F32)
    dh2 = _matmul("mm_dh2", du_pre, w_up, "nt", F32)
    dx1, dmix, d_post_mix, d_pre_ffn = _mid_bwd(x1, mix, dy, dh2, post_mix, pre_ffn, tm)
    g_out = _matmul("mm_dw_out", mixin, dmix, "tn", F32)
    dmixin = _matmul("mm_dmixin", dmix, w_out, "nt", F32)
    do_gdn, dza, dy_ssd, dxs_d, dzs, d_gnw, d_snw, d_dd = _gate_norm_bwd(o_gdn, y_ssd, xs, proj, dmixin, gnw, snw, dvec, tm)
    dxs_c, dbc, ddt, dacs = (b2(a) for a in _ssd_chunk_bwd(
        b3(xs), b3(bc), b3(dtx), b3(acsx), b3(acs), ar, b3(dy_ssd), ssd_st, bsz, seq, sb))
    dqn, dkn, dvv, dgb = (b2(a) for a in _gdn_chunk_bwd(qn, kn, vv, gs, gr, b3(do_gdn), gdn_st, gdn_ti, bsz, seq, sb))
    dpre_gdn, dsm_gdn, d_gcw, d_gp = _gdn_prep_bwd(proj, dqn, dkn, dvv, dgb, gcw, gp, seq, tm)
    dpre_ssd, dsm, d_scw, d_scb, d_sp = _ssd_prep_bwd(proj, dxs_c, dxs_d, dbc, ddt, dacs, dsm_gdn, scw, scb, sp, seq, tm)
    dqkv = _conv_t("gdn_conv_t", dpre_gdn, gcw, 4, seq, tm)
    dxbc = _conv_t("ssd_conv_t", dpre_ssd, scw, 4, seq, tm)
    dproj = jnp.concatenate([dqkv, dza, dzs, dxbc, dsm, jnp.zeros((t, PROJ_W - 6784), BF16)], axis=1)
    g_in = _matmul("mm_dw_in", h1, dproj, "tn", F32)
    dh1 = _matmul("mm_dh1", dproj, wp_in, "nt", F32)
    dx, d_pre_mix = _first_bwd(x2, dh1, dx1, pre_mix, tm)

    small = dict(pre_mix_norm=d_pre_mix, ssd_norm_w=d_snw, post_mix_norm=d_post_mix, pre_ffn_norm=d_pre_ffn,
                 post_ffn_norm=d_post_ffn, dd_lanes=d_dd, loss_lanes=loss_lanes, gdn_gates=d_gp, ssd_gates=d_sp,
                 gdn_norm_w=d_gnw, gdn_conv_w=d_gcw[0:4], ssd_conv_w=d_scw[0:4], ssd_conv_b=d_scb,
                 ffn_conv_w=d_fcw[0:3], ffn_conv_b=d_fcb)
    return dx.reshape(bsz, seq, d), g_in, g_out, g_up, g_down, small


def kernel(x, pre_mix_norm, w_in, gdn_conv_w, gdn_a_log, gdn_dt_bias, gdn_norm_w, ssd_conv_w, ssd_conv_b, ssd_a_log, ssd_dt_bias, ssd_d, ssd_norm_w, w_out, post_mix_norm, pre_ffn_norm, w_up, ffn_conv_w, ffn_conv_b, w_down, post_ffn_norm, loss_target, m_pre_mix_norm, m_w_in, m_gdn_conv_w, m_gdn_a_log, m_gdn_dt_bias, m_gdn_norm_w, m_ssd_conv_w, m_ssd_conv_b, m_ssd_a_log, m_ssd_dt_bias, m_ssd_d, m_ssd_norm_w, m_w_out, m_post_mix_norm, m_pre_ffn_norm, m_w_up, m_ffn_conv_w, m_ffn_conv_b, m_w_down, m_post_ffn_norm, v_pre_mix_norm, v_w_in, v_gdn_conv_w, v_gdn_a_log, v_gdn_dt_bias, v_gdn_norm_w, v_ssd_conv_w, v_ssd_conv_b, v_ssd_a_log, v_ssd_dt_bias, v_ssd_d, v_ssd_norm_w, v_w_out, v_post_mix_norm, v_pre_ffn_norm, v_w_up, v_ffn_conv_w, v_ffn_conv_b, v_w_down, v_post_ffn_norm):
    names = ["pre_mix_norm", "w_in", "gdn_conv_w", "gdn_a_log", "gdn_dt_bias", "gdn_norm_w", "ssd_conv_w", "ssd_conv_b",
             "ssd_a_log", "ssd_dt_bias", "ssd_d", "ssd_norm_w", "w_out", "post_mix_norm", "pre_ffn_norm", "w_up",
             "ffn_conv_w", "ffn_conv_b", "w_down", "post_ffn_norm"]
    w_args = [pre_mix_norm, w_in, gdn_conv_w, gdn_a_log, gdn_dt_bias, gdn_norm_w, ssd_conv_w, ssd_conv_b, ssd_a_log, ssd_dt_bias, ssd_d, ssd_norm_w, w_out, post_mix_norm, pre_ffn_norm, w_up, ffn_conv_w, ffn_conv_b, w_down, post_ffn_norm]
    m_args = [m_pre_mix_norm, m_w_in, m_gdn_conv_w, m_gdn_a_log, m_gdn_dt_bias, m_gdn_norm_w, m_ssd_conv_w, m_ssd_conv_b, m_ssd_a_log, m_ssd_dt_bias, m_ssd_d, m_ssd_norm_w, m_w_out, m_post_mix_norm, m_pre_ffn_norm, m_w_up, m_ffn_conv_w, m_ffn_conv_b, m_w_down, m_post_ffn_norm]
    v_args = [v_pre_mix_norm, v_w_in, v_gdn_conv_w, v_gdn_a_log, v_gdn_dt_bias, v_gdn_norm_w, v_ssd_conv_w, v_ssd_conv_b, v_ssd_a_log, v_ssd_dt_bias, v_ssd_d, v_ssd_norm_w, v_w_out, v_post_mix_norm, v_pre_ffn_norm, v_w_up, v_ffn_conv_w, v_ffn_conv_b, v_w_down, v_post_ffn_norm]
    w = {k: a[0] for k, a in zip(names, w_args)}
    m = {k: a[0] for k, a in zip(names, m_args)}
    v = {k: a[0] for k, a in zip(names, v_args)}
    idx = 4 * lax.axis_index("x") + 2 * lax.axis_index("y") + lax.axis_index("c")
    big = ("w_in", "w_out", "w_up", "w_down")
    conv = ("gdn_conv_w", "ssd_conv_w", "ffn_conv_w")

    conv_local = jnp.concatenate([jnp.pad(w[k], ((0, 4 - w[k].shape[0]), (0, 0))) for k in conv], axis=1)
    g_in, g_out, g_up, g_down, g_conv = _exchange(
        "gather_weights", [w[k].astype(BF16) for k in big] + [conv_local], scatter=False)
    cin, cup = w["w_in"].shape[1], w["w_up"].shape[1]
    wp_in = _permute_in(jnp.transpose(g_in, (1, 0, 2)).reshape(D_MODEL, N_DEV * cin))
    full_up = jnp.transpose(g_up, (1, 0, 2)).reshape(D_MODEL, N_DEV * cup)
    full_out = g_out.reshape(-1, D_MODEL)
    full_down = g_down.reshape(-1, D_MODEL)
    p = {k: w[k] for k in names if k not in big and k not in conv}
    off = 0
    for k in conv:
        cw = w[k].shape[1]
        p[k] = jnp.transpose(g_conv[:, :w[k].shape[0], off:off + cw], (1, 0, 2)).reshape(w[k].shape[0], N_DEV * cw)
        off += cw

    dx, d_in, d_out, d_up, d_down, small = _local_step(x, loss_target, wp_in, full_out, full_up, full_down, p)

    d_in = _unpermute_in(d_in)
    to_peers = [jnp.transpose(d_in.astype(BF16).reshape(D_MODEL, N_DEV, cin), (1, 0, 2)),
                d_out.astype(BF16).reshape(N_DEV, -1, D_MODEL),
                jnp.transpose(d_up.astype(BF16).reshape(D_MODEL, N_DEV, cup), (1, 0, 2)),
                d_down.astype(BF16).reshape(N_DEV, -1, D_MODEL)]
    gate_row = jnp.concatenate([small["gdn_gates"][0], small["gdn_gates"][1], small["ssd_gates"][0], small["ssd_gates"][1],
                                small["gdn_norm_w"][0], jnp.zeros((D_MODEL - 5 * LANES,), F32)]).reshape(1, D_MODEL)
    pack = _pack([small["pre_mix_norm"], small["ssd_norm_w"], small["post_mix_norm"], small["pre_ffn_norm"],
                  small["post_ffn_norm"], small["dd_lanes"], small["loss_lanes"], gate_row,
                  small["gdn_conv_w"], small["ssd_conv_w"], jnp.pad(small["ssd_conv_b"], ((0, 0), (0, 512))),
                  jnp.pad(small["ffn_conv_w"].reshape(-1), (0, 17 * D_MODEL - 3 * 2 * D_FF)),
                  jnp.pad(small["ffn_conv_b"], ((0, 0), (0, 512)))], SMALL_ROWS)
    p_in, p_out, p_up, p_down = _exchange("scatter_grads", to_peers, scatter=True)
    (pack_all,) = _exchange("gather_small", [pack], scatter=False)
    ssum, extra = _small_sum(pack_all)

    grads, deltas, new_m, new_v = {}, {}, {}, {}
    for k, parts in (("w_in", p_in), ("w_out", p_out), ("w_up", p_up), ("w_down", p_down)):
        grads[k], deltas[k], new_m[k], new_v[k] = _adam_big("adam_" + k, parts, w[k], m[k], v[k], 256)

    flat = ssum.reshape(-1)
    gate = ssum[7]
    sg = dict(pre_mix_norm=ssum[0], ssd_norm_w=ssum[1], post_mix_norm=ssum[2], pre_ffn_norm=ssum[3], post_ffn_norm=ssum[4],
              gdn_a_log=gate[8:16], gdn_dt_bias=gate[LANES + 8:LANES + 16], ssd_a_log=gate[2 * LANES + 16:2 * LANES + 32],
              ssd_dt_bias=gate[3 * LANES + 16:3 * LANES + 32], gdn_norm_w=gate[4 * LANES:5 * LANES], ssd_d=extra[0, 0:SSD_HEADS])
    o = 8 * D_MODEL
    full_gcw = flat[o:o + 4 * 3072].reshape(4, 3072)
    o += 12 * D_MODEL
    full_scw = flat[o:o + 4 * 1536].reshape(4, 1536)
    o += 6 * D_MODEL
    sg["ssd_conv_b"] = flat[o:o + 1536]
    o += 2 * D_MODEL
    full_fcw = flat[o:o + 3 * 2 * D_FF].reshape(3, 2 * D_FF)
    o += 17 * D_MODEL
    sg["ffn_conv_b"] = flat[o:o + 2 * D_FF]
    for k, full in (("gdn_conv_w", full_gcw), ("ssd_conv_w", full_scw), ("ffn_conv_w", full_fcw)):
        cw = w[k].shape[1]
        sg[k] = lax.dynamic_slice_in_dim(full, idx * cw, cw, axis=1)
    small_names = [k for k in names if k not in big]
    rows = 24
    gpk = _pack([sg[k] for k in small_names], rows)
    dpk, mpk, vpk = _adam_small(gpk, _pack([w[k] for k in small_names], rows), _pack([m[k] for k in small_names], rows),
                                _pack([v[k] for k in small_names], rows))
    shapes = [w[k].shape for k in small_names]
    for k, g_, d_, m_, v_ in zip(small_names, _unpack(gpk, shapes), _unpack(dpk, shapes), _unpack(mpk, shapes), _unpack(vpk, shapes)):
        grads[k], deltas[k], new_m[k], new_v[k] = g_, d_, m_, v_

    loss = extra[1, 0]
    lead = lambda a: a[None]
    return (loss, dx, *[lead(grads[k]) for k in names], *[lead(deltas[k]) for k in names],
            *[lead(new_m[k]) for k in names], *[lead(new_v[k]) for k in names])
```

```python
import functools

import jax
import jax.numpy as jnp
from jax import lax
from jax.experimental import pallas as pl
from jax.experimental.pallas import tpu as pltpu

F32 = jnp.float32
BF16 = jnp.bfloat16
MXU_DTYPE = jnp.bfloat16
HIGHEST = lax.Precision.HIGHEST
VMEM_LIMIT_V7X = 48 * 1024 * 1024
SUBLANES = 8
LANES = 128

D_MODEL = 1024
GDN_HEADS = 8
GDN_DK = 128
SSD_HEADS = 16
SSD_P = 64
SSD_GROUPS = 2
SSD_HPG = 8
SSD_N = 128
CHUNK = 64
D_FF = 2816
EPS = 1e-6
N_DEV = 8
PROJ_W = 7168
SMALL_CB = 52
D_IN = 6688

ADAM_LR = 0.001
ADAM_B1 = 0.9
ADAM_B2 = 0.999
ADAM_EPS = 1e-08
ADAM_WD = 0.01
ADAM_STEP = 10

NN = (((1,), (0,)), ((), ()))
NT = (((1,), (1,)), ((), ()))
TN = (((0,), (0,)), ((), ()))


def _pcall(body, **kw):
    return pl.pallas_call(body, **kw)


def _mm(a, b, dims=NN):
    return lax.dot_general(a.astype(MXU_DTYPE), b.astype(MXU_DTYPE), dims, preferred_element_type=F32)


def _mmx(a, b, dims=NN):
    return lax.dot_general(a, b, dims, precision=HIGHEST, preferred_element_type=F32)


def _split(a):
    hi = a.astype(MXU_DTYPE)
    return hi, (a - hi.astype(F32)).astype(MXU_DTYPE)


def _mm3(a, b, dims=NN):
    (ah, al), (bh, bl) = _split(a), _split(b)
    dot = lambda p, q: lax.dot_general(p, q, dims, preferred_element_type=F32)
    return dot(ah, bh) + (dot(ah, bl) + dot(al, bh))


def _mmsel(a, sel, dims=NN):
    ah, al = _split(a)
    s = sel.astype(MXU_DTYPE)
    return (lax.dot_general(ah, s, dims, preferred_element_type=F32)
            + lax.dot_general(al, s, dims, preferred_element_type=F32))


def _sigmoid(x):
    return 1.0 / (1.0 + jnp.exp(-x))


def _softplus(x):
    return jnp.maximum(x, 0.0) + jnp.log(1.0 + jnp.exp(-jnp.abs(x)))


def _dsilu(x, s):
    return s * (1.0 + x * (1.0 - s))


def _rowsum(x):
    return jnp.sum(x, axis=1, keepdims=True)


def _colsum(x):
    return jnp.sum(x, axis=0, keepdims=True)


def _pick(dim, pref):
    if dim <= pref:
        return dim
    best = None
    t = LANES
    while t <= pref:
        if dim % t == 0:
            best = t
        t += LANES
    return dim if best is None else best


def _params(sem):
    return pltpu.CompilerParams(dimension_semantics=sem, vmem_limit_bytes=VMEM_LIMIT_V7X)


def _matmul(name, a, b, mode, out_dtype, tm=1024, tn=1024, tk=1024):
    if mode == "nn":
        (m, k), (_, n) = a.shape, b.shape
    elif mode == "nt":
        (m, k), (n, _) = a.shape, b.shape
    else:
        (k, m), (_, n) = a.shape, b.shape
    tm, tn, tk = _pick(m, tm), _pick(n, tn), _pick(k, tk)
    nk = k // tk
    if mode == "tn":
        a_spec = pl.BlockSpec((tk, tm), lambda i, j, kk: (kk, i))
    else:
        a_spec = pl.BlockSpec((tm, tk), lambda i, j, kk: (i, kk))
    if mode == "nt":
        b_spec = pl.BlockSpec((tn, tk), lambda i, j, kk: (j, kk))
    else:
        b_spec = pl.BlockSpec((tk, tn), lambda i, j, kk: (kk, j))
    dims = {"nn": NN, "nt": NT, "tn": TN}[mode]

    def body(a_ref, b_ref, o_ref, acc):
        kk = pl.program_id(2)

        @pl.when(kk == 0)
        def _():
            acc[...] = jnp.zeros_like(acc)

        acc[...] += _mm(a_ref[...], b_ref[...], dims)

        @pl.when(kk == nk - 1)
        def _():
            o_ref[...] = acc[...].astype(out_dtype)

    return _pcall(
        body, name=name, grid=(m // tm, n // tn, nk),
        in_specs=[a_spec, b_spec],
        out_specs=pl.BlockSpec((tm, tn), lambda i, j, kk: (i, j)),
        out_shape=jax.ShapeDtypeStruct((m, n), out_dtype),
        scratch_shapes=[pltpu.VMEM((tm, tn), F32)],
        compiler_params=_params(("parallel", "parallel", "arbitrary")),
    )(a, b)


def _rowwise(name, body, n_rows, tm, ins, outs, accs=()):
    arrays, in_specs = [], []
    last8 = n_rows // SUBLANES - 1
    per = tm // SUBLANES
    for spec in ins:
        kind, arr = spec[0], spec[1]
        if kind == "full":
            in_specs.append(pl.BlockSpec(arr.shape, lambda i, nd=arr.ndim: (0,) * nd))
        else:
            w, cb = spec[2], spec[3]
            if kind == "row":
                in_specs.append(pl.BlockSpec((tm, w), lambda i, cb=cb: (i, cb)))
            elif kind == "prev":
                in_specs.append(pl.BlockSpec((SUBLANES, w), lambda i, cb=cb: (jnp.maximum(i * per - 1, 0), cb)))
            else:
                in_specs.append(pl.BlockSpec((SUBLANES, w), lambda i, cb=cb: (jnp.minimum((i + 1) * per, last8), cb)))
        arrays.append(arr)
    out_shape = [jax.ShapeDtypeStruct((n_rows, w), dt) for (w, dt) in outs]
    out_shape += [jax.ShapeDtypeStruct(s, F32) for s in accs]
    out_specs = [pl.BlockSpec((tm, w), lambda i: (i, 0)) for (w, _) in outs]
    out_specs += [pl.BlockSpec(s, lambda i: (0, 0)) for s in accs]
    n_io = len(ins) + len(outs)

    def kern(*refs):
        i = pl.program_id(0)
        if accs:
            @pl.when(i == 0)
            def _():
                for r in refs[n_io:]:
                    r[...] = jnp.zeros_like(r)
        body(i, *refs)

    res = _pcall(
        kern, name=name, grid=(n_rows // tm,), in_specs=in_specs, out_specs=out_specs, out_shape=out_shape,
        compiler_params=_params(("arbitrary",)),
    )(*arrays)
    return res


def _shift_down(x, halo, j):
    r = pltpu.roll(x, j, 0)
    hr = pltpu.roll(halo, j, 0)
    rows = lax.broadcasted_iota(jnp.int32, (SUBLANES, x.shape[1]), 0)
    top = jnp.where(rows < j, hr, r[0:SUBLANES])
    return jnp.concatenate([top, r[SUBLANES:]], axis=0)


def _shift_up(x, halo, j):
    tm = x.shape[0]
    r = pltpu.roll(x, tm - j, 0)
    hr = pltpu.roll(halo, SUBLANES - j, 0)
    rows = lax.broadcasted_iota(jnp.int32, (SUBLANES, x.shape[1]), 0)
    bot = jnp.where(rows >= SUBLANES - j, hr, r[tm - SUBLANES:])
    return jnp.concatenate([r[:tm - SUBLANES], bot], axis=0)


def _conv_taps(x, halo, kw):
    return [x if kw - 1 - k == 0 else _shift_down(x, halo, kw - 1 - k) for k in range(kw)]


def _conv(taps, w):
    y = taps[0] * w[0:1]
    for k in range(1, len(taps)):
        y = y + taps[k] * w[k:k + 1]
    return y


def _rms(x, width):
    r = lax.rsqrt(jnp.sum(x * x, axis=-1, keepdims=True) * (1.0 / width) + EPS)
    return x * r, r


def _rms_bwd(xh, r, dxh, width):
    return r * (dxh - xh * (jnp.sum(dxh * xh, axis=-1, keepdims=True) * (1.0 / width)))


def _seq_flags(i, seq, tm):
    nps = seq // tm
    pos = i % nps
    return jnp.where(pos == 0, 0.0, 1.0), jnp.where(pos == nps - 1, 0.0, 1.0)


def _norm_cast(name, x, w, tm):
    t, d = x.shape

    def body(i, x_ref, w_ref, h_ref):
        xh, _ = _rms(x_ref[...], d)
        h_ref[...] = (xh * w_ref[...]).astype(BF16)

    return _rowwise(name, body, t, tm, [("row", x, d, 0), ("full", w)], [(d, BF16)])[0]


def _gdn_prep(proj, cw, gp, seq, tm):
    t = proj.shape[0]
    d = D_MODEL

    def body(i, q_ref, qh_ref, k_ref, kh_ref, v_ref, vh_ref, sm_ref, cw_ref, gp_ref, qn_ref, kn_ref, vv_ref, gs_ref):
        keep, _ = _seq_flags(i, seq, tm)
        for x_ref, h_ref, o_ref, off, scale in ((q_ref, qh_ref, qn_ref, 0, GDN_DK ** -0.5),
                                               (k_ref, kh_ref, kn_ref, d, 1.0), (v_ref, vh_ref, vv_ref, 2 * d, None)):
            y = _conv(_conv_taps(x_ref[...], h_ref[...] * keep, 4), cw_ref[:, off:off + d])
            a = y * _sigmoid(y)
            if scale is None:
                o_ref[...] = a
            else:
                for hh in range(GDN_HEADS):
                    s = a[:, hh * GDN_DK:(hh + 1) * GDN_DK]
                    n = lax.rsqrt(_rowsum(s * s) + EPS)
                    o_ref[:, hh * GDN_DK:(hh + 1) * GDN_DK] = s * (n * scale)
        sm = sm_ref[...]
        lane = lax.broadcasted_iota(jnp.int32, sm.shape, 1)
        beta = _sigmoid(sm)
        g = jnp.where((lane >= 8) & (lane < 16), -jnp.exp(gp_ref[0:1, :]) * _softplus(sm + gp_ref[1:2, :]), 0.0)
        gs_ref[...] = jnp.where(lane < 8, beta, _mmx(_block_tri(tm, False), g))

    ins = []
    for cb in range(3):
        ins += [("row", proj, d, cb), ("prev", proj, d, cb)]
    ins += [("row", proj, LANES, SMALL_CB), ("full", cw), ("full", gp)]
    return _rowwise("gdn_prep", body, t, tm, ins, [(d, F32), (d, F32), (d, F32), (LANES, F32)])


def _block_tri(tm, upper):
    ri = lax.broadcasted_iota(jnp.int32, (tm, tm), 0)
    ci = lax.broadcasted_iota(jnp.int32, (tm, tm), 1)
    tri = (ri <= ci) if upper else (ri >= ci)
    return (tri & ((ri // CHUNK) == (ci // CHUNK))).astype(F32)


def _chunk_consts():
    row = lax.broadcasted_iota(jnp.int32, (CHUNK, CHUNK), 0)
    col = lax.broadcasted_iota(jnp.int32, (CHUNK, CHUNK), 1)
    return dict(
        tril=row >= col, strict=row > col, eye=(row == col).astype(F32),
        lane=lax.broadcasted_iota(jnp.int32, (CHUNK, LANES), 1),
        row1=lax.broadcasted_iota(jnp.int32, (CHUNK, 1), 0),
        ones=jnp.ones((CHUNK, LANES), F32))


def _hmap(fn, *lists):
    return [fn(*a) for a in zip(*lists)]


def _tri_inv(nmats, eye):
    x = [eye - n for n in nmats]
    p = _hmap(_mm3, nmats, nmats)
    for lvl in range(5):
        x = _hmap(lambda xi, pi: xi + _mm3(xi, pi), x, p)
        if lvl < 4:
            p = _hmap(_mm3, p, p)
    return x


def _gdn_gates(gs, gc_row, h, c):
    beta = _rowsum(jnp.where(c["lane"] == h, gs, 0.0))
    gc = _rowsum(jnp.where(c["lane"] == h + 8, gs, 0.0))
    dc = jnp.exp(jnp.where(c["tril"], gc - gc_row, -1e30))
    gl = gc[CHUNK - 1:CHUNK, :]
    return beta, dc, jnp.exp(gc), jnp.exp(gl), jnp.exp(gl - gc)


GDN_HB = 4


def _gdn_specs(bsz, seq, sb):
    nsb = seq // sb
    ncb = sb // CHUNK
    fwd = lambda j: j
    bwd = lambda j: nsb - 1 - j
    def specs(order):
        return dict(
            wide=lambda: pl.BlockSpec((1, sb, GDN_HB * GDN_DK), lambda b, h, j: (b, order(j), h)),
            gs=pl.BlockSpec((1, sb, LANES), lambda b, h, j: (b, order(j), 0)),
            gr=pl.BlockSpec((1, GDN_HB, ncb, CHUNK), lambda b, h, j: (b, h, order(j), 0)),
            st=pl.BlockSpec((1, GDN_HB, ncb * GDN_DK, GDN_DK), lambda b, h, j: (b, h, order(j), 0)),
            ti=pl.BlockSpec((1, GDN_HB, sb, CHUNK), lambda b, h, j: (b, h, order(j), 0)))
    return nsb, ncb, specs(fwd), specs(bwd)


def _riding_exchange(arrays, scatter, n_in, n_out, grid):
    n = len(arrays)
    if n == 0:
        return [], [], [], lambda body: body
    any_spec = [pl.BlockSpec(memory_space=pl.ANY)] * n

    def wrap(body):
        def wrapped(*refs):
            ins = refs[n_in:n_in + n]
            outs = refs[n_in + n + n_out:n_in + 2 * n + n_out]
            sems = refs[len(refs) - 3:]
            pid = [pl.program_id(a) for a in range(len(grid))]
            first = functools.reduce(lambda a, b: a & b, [p == 0 for p in pid])
            last = functools.reduce(lambda a, b: a & b, [p == g - 1 for p, g in zip(pid, grid)])

            @pl.when(first)
            def _():
                _exchange_phase(ins, outs, sems, scatter, start=True)

            body(*refs[:n_in], *refs[n_in + n:n_in + n + n_out], *refs[n_in + 2 * n + n_out:len(refs) - 3])

            @pl.when(last)
            def _():
                _exchange_phase(ins, outs, sems, scatter, start=False)

        return wrapped

    return any_spec, _exchange_out_shapes(arrays, scatter), _exchange_sems(n), wrap


def _gdn_chunk_fwd(qn, kn, vv, gs, gr, bsz, seq, sb, riders):
    nsb, ncb, sp, _ = _gdn_specs(bsz, seq, sb)
    grid = (bsz, GDN_HEADS // GDN_HB, nsb)
    any_spec, rider_shapes, rider_sems, wrap = _riding_exchange(riders, False, 5, 3, grid)

    def body(q_ref, k_ref, v_ref, gs_ref, gr_ref, o_ref, st_ref, ti_ref, s_scr):
        hg = pl.program_id(1)

        @pl.when(pl.program_id(2) == 0)
        def _():
            s_scr[...] = jnp.zeros_like(s_scr)

        c = _chunk_consts()

        def chunk(n, carry):
            r = pl.ds(pl.multiple_of(n * CHUNK, CHUNK), CHUNK)
            rs = pl.ds(pl.multiple_of(n * GDN_DK, GDN_DK), GDN_DK)
            gsv = gs_ref[0, r, :]
            heads = list(range(GDN_HB))
            sls = [slice(ih * GDN_DK, (ih + 1) * GDN_DK) for ih in heads]
            q = [q_ref[0, r, sl] for sl in sls]
            k = [k_ref[0, r, sl] for sl in sls]
            v = [v_ref[0, r, sl] for sl in sls]
            beta, dc, eg, egl, ekd = zip(*[
                _gdn_gates(gsv, gr_ref[0, ih, pl.ds(n, 1), :], hg * GDN_HB + ih, c) for ih in heads])
            kb = _hmap(lambda a, b: a * b, k, beta)
            amat = _hmap(lambda a, b, d_: jnp.where(c["strict"], _mm(a, b, NT) * d_, 0.0), kb, k, dc)
            tinv = _tri_inv(amat, c["eye"])
            u = _hmap(lambda t_, a, b: _mm3(t_, a * b), tinv, v, beta)
            w = _hmap(lambda t_, a, b: _mm3(t_, a * b), tinv, kb, eg)
            qk = _hmap(lambda a, b, d_: _mm(a, b, NT) * d_, q, k, dc)
            s = [s_scr[ih] for ih in heads]
            v_new = _hmap(lambda a, b, s_: a - _mm(b, s_), u, w, s)
            o = _hmap(lambda a, e, s_, qk_, vn: _mm(a * e, s_) + _mm(qk_, vn), q, eg, s, qk, v_new)
            s_new = _hmap(lambda s_, e, a, f, vn: s_ * e + _mm(a * f, vn, TN), s, egl, k, ekd, v_new)
            for ih in heads:
                o_ref[0, r, sls[ih]] = o[ih]
                st_ref[0, ih, rs, :] = s[ih]
                ti_ref[0, ih, r, :] = tinv[ih]
                s_scr[ih] = s_new[ih]
            return carry

        lax.fori_loop(0, ncb, chunk, 0)

    t3 = (bsz, seq, D_MODEL)
    res = _pcall(
        wrap(body), name="gdn_chunk_fwd", grid=grid,
        in_specs=[sp["wide"](), sp["wide"](), sp["wide"](), sp["gs"], sp["gr"]] + any_spec,
        out_specs=[sp["wide"](), sp["st"], sp["ti"]] + any_spec,
        out_shape=[jax.ShapeDtypeStruct(t3, F32),
                   jax.ShapeDtypeStruct((bsz, GDN_HEADS, (seq // CHUNK) * GDN_DK, GDN_DK), F32),
                   jax.ShapeDtypeStruct((bsz, GDN_HEADS, seq, CHUNK), F32)] + rider_shapes,
        scratch_shapes=[pltpu.VMEM((GDN_HB, GDN_DK, GDN_DK), F32)] + rider_sems,
        compiler_params=_params(("arbitrary", "arbitrary", "arbitrary")),
    )(qn, kn, vv, gs, gr, *riders)
    return res[:3], res[3:]


def _ssd_prep(proj, cw, cb, sp, seq, tm):
    t = proj.shape[0]
    d = D_MODEL
    ssd_w = SSD_HEADS * SSD_P

    def body(i, x_ref, xh_ref, bc_ref, bch_ref, sm_ref, cw_ref, cb_ref, sp_ref, xs_ref, bco_ref, dtx_ref, acsx_ref, acs_ref):
        keep, _ = _seq_flags(i, seq, tm)
        y = _conv(_conv_taps(x_ref[...], xh_ref[...] * keep, 4), cw_ref[:, 0:d]) + cb_ref[:, 0:d]
        xs_ref[...] = y * _sigmoid(y)
        y = _conv(_conv_taps(bc_ref[...], bch_ref[...] * keep, 4), cw_ref[:, d:d + 512]) + cb_ref[:, d:d + 512]
        bco_ref[...] = y * _sigmoid(y)
        sm = sm_ref[...]
        lane = lax.broadcasted_iota(jnp.int32, sm.shape, 1)
        valid = (lane >= 16) & (lane < 32)
        dt = jnp.where(valid, _softplus(sm + sp_ref[1:2, :]), 0.0)
        adt = dt * (-jnp.exp(sp_ref[0:1, :]))
        acs = _mmx(_block_tri(tm, False), adt)
        l64 = lax.broadcasted_iota(jnp.int32, (LANES, ssd_w), 0)
        d64 = lax.broadcasted_iota(jnp.int32, (LANES, ssd_w), 1)
        e64 = (l64 - 16 == d64 // SSD_P).astype(F32)
        dtx_ref[...] = _mmx(dt, e64)
        acsx_ref[...] = _mmx(acs, e64)
        acs_ref[...] = acs

    ins = [("row", proj, d, 5), ("prev", proj, d, 5), ("row", proj, 512, 12), ("prev", proj, 512, 12),
           ("row", proj, LANES, SMALL_CB), ("full", cw), ("full", cb), ("full", sp)]
    return _rowwise("ssd_prep", body, t, tm, ins, [(d, F32), (512, F32), (ssd_w, F32), (ssd_w, F32), (LANES, F32)])


SSD_GW = SSD_HPG * SSD_P


def _ssd_head(acs, ar_ref, n, head, cbm, c):
    col = _rowsum(jnp.where(c["lane"] == head + 16, acs, 0.0))
    lm = jnp.exp(jnp.where(c["tril"], col - ar_ref[0, head, pl.ds(n, 1), :], -1e30))
    return lm, cbm * lm


def _ssd_specs(seq, sb):
    nsb = seq // sb
    ncb = sb // CHUNK
    def specs(order):
        return dict(
            wide=lambda: pl.BlockSpec((1, sb, SSD_HEADS * SSD_P), lambda b, j: (b, order(j), 0)),
            bc=lambda: pl.BlockSpec((1, sb, 2 * SSD_GROUPS * SSD_N), lambda b, j: (b, order(j), 0)),
            half=lambda: pl.BlockSpec((1, sb, SSD_GROUPS * SSD_N), lambda b, j: (b, order(j), 0)),
            small=lambda: pl.BlockSpec((1, sb, LANES), lambda b, j: (b, order(j), 0)),
            ar=pl.BlockSpec((1, SSD_HEADS, ncb, CHUNK), lambda b, j: (b, 0, order(j), 0)),
            st=pl.BlockSpec((1, ncb * SSD_N, SSD_HEADS * SSD_P), lambda b, j: (b, order(j), 0)))
    return nsb, ncb, specs(lambda j: j), specs(lambda j: nsb - 1 - j)


def _ssd_chunk_fwd(xs, bc, dtx, acsx, acs, ar, bsz, seq, sb):
    nsb, ncb, sp, _ = _ssd_specs(seq, sb)

    def body(x_ref, dtx_ref, ax_ref, bc_ref, acs_ref, ar_ref, y_ref, sts_ref, st_scr):
        @pl.when(pl.program_id(1) == 0)
        def _():
            st_scr[...] = jnp.zeros_like(st_scr)

        c = _chunk_consts()
        lane5 = lax.broadcasted_iota(jnp.int32, (CHUNK, SSD_GW), 1) // SSD_P

        def chunk(n, carry):
            r = pl.ds(pl.multiple_of(n * CHUNK, CHUNK), CHUNK)
            rs = pl.ds(pl.multiple_of(n * SSD_N, SSD_N), SSD_N)
            acsv = acs_ref[0, r, :]
            for g in range(SSD_GROUPS):
                gl = slice(g * SSD_GW, (g + 1) * SSD_GW)
                x, dt, ax = x_ref[0, r, gl], dtx_ref[0, r, gl], ax_ref[0, r, gl]
                bm = bc_ref[0, r, g * SSD_N:(g + 1) * SSD_N]
                cm = bc_ref[0, r, (SSD_GROUPS + g) * SSD_N:(SSD_GROUPS + g + 1) * SSD_N]
                xdt = x * dt
                cbm = _mm(cm, bm, NT)
                al = ax[CHUNK - 1:CHUNK, :]
                st = st_scr[:, gl]
                y = _mm(cm, st) * jnp.exp(ax)
                for hh in range(SSD_HPG):
                    _, gm = _ssd_head(acsv, ar_ref, n, g * SSD_HPG + hh, cbm, c)
                    y = y + _mm(gm, jnp.where(lane5 == hh, xdt, 0.0))
                y_ref[0, r, gl] = y
                sts_ref[0, rs, gl] = st
                st_scr[:, gl] = st * jnp.exp(al) + _mm(bm, xdt * jnp.exp(al - ax), TN)
            return carry

        lax.fori_loop(0, ncb, chunk, 0)

    return _pcall(
        body, name="ssd_chunk_fwd", grid=(bsz, nsb),
        in_specs=[sp["wide"](), sp["wide"](), sp["wide"](), sp["bc"](), sp["small"](), sp["ar"]],
        out_specs=[sp["wide"](), sp["st"]],
        out_shape=[jax.ShapeDtypeStruct((bsz, seq, SSD_HEADS * SSD_P), F32),
                   jax.ShapeDtypeStruct((bsz, (seq // CHUNK) * SSD_N, SSD_HEADS * SSD_P), F32)],
        scratch_shapes=[pltpu.VMEM((SSD_N, SSD_HEADS * SSD_P), F32)],
        compiler_params=_params(("parallel", "arbitrary")),
    )(xs, dtx, acsx, bc, acs, ar)


def _gate_norm(o_gdn, y_ssd, xs, proj, gnw, snw, dvec, tm):
    t = o_gdn.shape[0]
    d = D_MODEL

    def body(i, o_ref, za_ref, y_ref, xs_ref, zs_ref, gnw_ref, snw_ref, dv_ref, out_ref):
        for hh in range(GDN_HEADS):
            sl = slice(hh * GDN_DK, (hh + 1) * GDN_DK)
            oh, _ = _rms(o_ref[:, sl], GDN_DK)
            z = za_ref[:, sl]
            out_ref[:, sl] = (oh * gnw_ref[...] * (z * _sigmoid(z))).astype(BF16)
        zs = zs_ref[...]
        yg = (y_ref[...] + dv_ref[...] * xs_ref[...]) * (zs * _sigmoid(zs))
        for g in range(SSD_GROUPS):
            sl = slice(g * 512, (g + 1) * 512)
            yh, _ = _rms(yg[:, sl], 512)
            out_ref[:, d + g * 512:d + (g + 1) * 512] = (yh * snw_ref[:, sl]).astype(BF16)

    ins = [("row", o_gdn, d, 0), ("row", proj, d, 3), ("row", y_ssd, d, 0), ("row", xs, d, 0), ("row", proj, d, 4),
           ("full", gnw), ("full", snw), ("full", dvec)]
    return _rowwise("gate_norm", body, t, tm, ins, [(2 * d, BF16)])[0]


def _mid(x, mix, pmw, pfw, tm):
    t, d = x.shape

    def body(i, x_ref, mix_ref, pmw_ref, pfw_ref, x1_ref, h2_ref):
        mh, _ = _rms(mix_ref[...], d)
        x1 = x_ref[...] + mh * pmw_ref[...]
        x1_ref[...] = x1
        xh, _ = _rms(x1, d)
        h2_ref[...] = (xh * pfw_ref[...]).astype(BF16)

    return _rowwise("mid", body, t, tm, [("row", x, d, 0), ("row", mix, d, 0), ("full", pmw), ("full", pfw)],
                    [(d, F32), (d, BF16)])


def _ffn_gate_up(ug_ref, ugh_ref, uu_ref, uuh_ref, cw_ref, cb_ref, keep):
    tg = _conv_taps(ug_ref[...], ugh_ref[...] * keep, 3)
    tu = _conv_taps(uu_ref[...], uuh_ref[...] * keep, 3)
    gate = _conv(tg, cw_ref[:, 0:D_FF]) + cb_ref[:, 0:D_FF]
    up = _conv(tu, cw_ref[:, D_FF:2 * D_FF]) + cb_ref[:, D_FF:2 * D_FF]
    return tg, tu, gate, up


def _ffn_act(u_pre, cw, cb, seq, tm):
    t = u_pre.shape[0]

    def body(i, ug_ref, ugh_ref, uu_ref, uuh_ref, cw_ref, cb_ref, act_ref):
        keep, _ = _seq_flags(i, seq, tm)
        _, _, gate, up = _ffn_gate_up(ug_ref, ugh_ref, uu_ref, uuh_ref, cw_ref, cb_ref, keep)
        act_ref[...] = (gate * _sigmoid(gate) * up).astype(BF16)

    ins = [("row", u_pre, D_FF, 0), ("prev", u_pre, D_FF, 0), ("row", u_pre, D_FF, 1), ("prev", u_pre, D_FF, 1),
           ("full", cw), ("full", cb)]
    return _rowwise("ffn_act", body, t, tm, ins, [(D_FF, BF16)])[0]


def _final(x1, f, tgt, w, tm):
    t, d = x1.shape

    def body(i, x1_ref, f_ref, t_ref, w_ref, dy_ref, df_ref, loss_ref, dw_ref):
        fh, r = _rms(f_ref[...], d)
        e = x1_ref[...] + fh * w_ref[...] - t_ref[...]
        loss_ref[...] += _colsum(e * e) * (0.5 / d)
        dy = e * (1.0 / d)
        dy_ref[...] = dy
        dw_ref[...] += _colsum(dy * fh)
        df_ref[...] = _rms_bwd(fh, r, dy * w_ref[...], d).astype(BF16)

    return _rowwise("final", body, t, tm, [("row", x1, d, 0), ("row", f, d, 0), ("row", tgt, d, 0), ("full", w)],
                    [(d, F32), (d, BF16)], accs=[(1, d), (1, d)])


def _ffn_bwd(u_pre, dact, cw, cb, seq, tm):
    t = u_pre.shape[0]

    def body(i, ug_ref, ugh_ref, uu_ref, uuh_ref, da_ref, cw_ref, cb_ref, du_ref, dcw_ref, dcb_ref):
        keep, _ = _seq_flags(i, seq, tm)
        tg, tu, gate, up = _ffn_gate_up(ug_ref, ugh_ref, uu_ref, uuh_ref, cw_ref, cb_ref, keep)
        sg = _sigmoid(gate)
        da = da_ref[...]
        dgate = da * up * _dsilu(gate, sg)
        dup = da * gate * sg
        du_ref[:, 0:D_FF] = dgate
        du_ref[:, D_FF:2 * D_FF] = dup
        dcb_ref[:, 0:D_FF] += _colsum(dgate)
        dcb_ref[:, D_FF:2 * D_FF] += _colsum(dup)
        for k in range(3):
            dcw_ref[k:k + 1, 0:D_FF] += _colsum(dgate * tg[k])
            dcw_ref[k:k + 1, D_FF:2 * D_FF] += _colsum(dup * tu[k])

    ins = [("row", u_pre, D_FF, 0), ("prev", u_pre, D_FF, 0), ("row", u_pre, D_FF, 1), ("prev", u_pre, D_FF, 1),
           ("row", dact, D_FF, 0), ("full", cw), ("full", cb)]
    return _rowwise("ffn_bwd", body, t, tm, ins, [(2 * D_FF, F32)], accs=[(SUBLANES, 2 * D_FF), (1, 2 * D_FF)])


def _conv_t(name, dy, cw, kw, seq, tm):
    t, width = dy.shape

    def body(i, d_ref, dn_ref, cw_ref, o_ref):
        _, keep = _seq_flags(i, seq, tm)
        d = d_ref[...]
        halo = dn_ref[...] * keep
        acc = d * cw_ref[kw - 1:kw, :]
        for j in range(1, kw):
            acc = acc + _shift_up(d, halo, j) * cw_ref[kw - 1 - j:kw - j, :]
        o_ref[...] = acc.astype(BF16)

    return _rowwise(name, body, t, tm, [("row", dy, width, 0), ("next", dy, width, 0), ("full", cw)], [(width, BF16)])[0]


def _mid_bwd(x1, mix, dy, dh2, pmw, pfw, tm):
    t, d = x1.shape

    def body(i, x1_ref, mix_ref, dy_ref, dh2_ref, pmw_ref, pfw_ref, dx1_ref, dmix_ref, dpm_ref, dpf_ref):
        xh, r2 = _rms(x1_ref[...], d)
        dh2 = dh2_ref[...]
        dpf_ref[...] += _colsum(dh2 * xh)
        dx1 = dy_ref[...] + _rms_bwd(xh, r2, dh2 * pfw_ref[...], d)
        dx1_ref[...] = dx1
        mh, r = _rms(mix_ref[...], d)
        dpm_ref[...] += _colsum(dx1 * mh)
        dmix_ref[...] = _rms_bwd(mh, r, dx1 * pmw_ref[...], d).astype(BF16)

    ins = [("row", x1, d, 0), ("row", mix, d, 0), ("row", dy, d, 0), ("row", dh2, d, 0), ("full", pmw), ("full", pfw)]
    return _rowwise("mid_bwd", body, t, tm, ins, [(d, F32), (d, BF16)], accs=[(1, d), (1, d)])


def _gate_norm_bwd(o_gdn, y_ssd, xs, proj, dmixin, gnw, snw, dvec, tm):
    t = o_gdn.shape[0]
    d = D_MODEL

    def body(i, o_ref, za_ref, y_ref, xs_ref, zs_ref, dma_ref, dms_ref, gnw_ref, snw_ref, dv_ref,
             do_ref, dza_ref, dy_ref, dxs_ref, dzs_ref, dgnw_ref, dsnw_ref, dd_ref):
        for hh in range(GDN_HEADS):
            sl = slice(hh * GDN_DK, (hh + 1) * GDN_DK)
            oh, r = _rms(o_ref[:, sl], GDN_DK)
            z = za_ref[:, sl]
            sz = _sigmoid(z)
            dm = dma_ref[:, sl]
            don = dm * (z * sz)
            dza_ref[:, sl] = (dm * oh * gnw_ref[...] * _dsilu(z, sz)).astype(BF16)
            dgnw_ref[...] += _colsum(don * oh)
            do_ref[:, sl] = _rms_bwd(oh, r, don * gnw_ref[...], GDN_DK)
        zs = zs_ref[...]
        sz = _sigmoid(zs)
        sil = zs * sz
        x = xs_ref[...]
        y0 = y_ref[...] + dv_ref[...] * x
        yg = y0 * sil
        dms = dms_ref[...]
        for g in range(SSD_GROUPS):
            sl = slice(g * 512, (g + 1) * 512)
            yh, r = _rms(yg[:, sl], 512)
            dsnw_ref[:, sl] += _colsum(dms[:, sl] * yh)
            dyg = _rms_bwd(yh, r, dms[:, sl] * snw_ref[:, sl], 512)
            dy0 = dyg * sil[:, sl]
            dzs_ref[:, sl] = (dyg * y0[:, sl] * _dsilu(zs[:, sl], sz[:, sl])).astype(BF16)
            dy_ref[:, sl] = dy0
            dxs_ref[:, sl] = dy0 * dv_ref[:, sl]
            dd_ref[:, sl] += _colsum(dy0 * x[:, sl])

    ins = [("row", o_gdn, d, 0), ("row", proj, d, 3), ("row", y_ssd, d, 0), ("row", xs, d, 0), ("row", proj, d, 4),
           ("row", dmixin, d, 0), ("row", dmixin, d, 1), ("full", gnw), ("full", snw), ("full", dvec)]
    return _rowwise("gate_norm_bwd", body, t, tm, ins, [(d, F32), (d, BF16), (d, F32), (d, F32), (d, BF16)],
                    accs=[(1, GDN_DK), (1, d), (1, d)])


def _ssd_chunk_bwd(xs, bc, dtx, acsx, acs, ar, dy, sts, bsz, seq, sb):
    nsb, ncb, _, sp = _ssd_specs(seq, sb)

    def body(x_ref, dtx_ref, ax_ref, bc_ref, acs_ref, ar_ref, dy_ref, sts_ref, dx_ref, dbc_ref, ddt_ref, dacs_ref, dst_scr):
        @pl.when(pl.program_id(1) == 0)
        def _():
            dst_scr[...] = jnp.zeros_like(dst_scr)

        c = _chunk_consts()
        lane5 = lax.broadcasted_iota(jnp.int32, (CHUNK, SSD_GW), 1) // SSD_P
        row5 = lax.broadcasted_iota(jnp.int32, (CHUNK, SSD_GW), 0)
        sel_in = lax.broadcasted_iota(jnp.int32, (SSD_GW, LANES), 0) // SSD_P
        sel_out = lax.broadcasted_iota(jnp.int32, (SSD_GW, LANES), 1)

        def chunk(nn, carry):
            n = ncb - 1 - nn
            r = pl.ds(pl.multiple_of(n * CHUNK, CHUNK), CHUNK)
            rs = pl.ds(pl.multiple_of(n * SSD_N, SSD_N), SSD_N)
            acsv = acs_ref[0, r, :]
            ddt = jnp.zeros((CHUNK, LANES), F32)
            dacs = jnp.zeros((CHUNK, LANES), F32)
            for g in range(SSD_GROUPS):
                gl = slice(g * SSD_GW, (g + 1) * SSD_GW)
                x, dt, ax, dyv = x_ref[0, r, gl], dtx_ref[0, r, gl], ax_ref[0, r, gl], dy_ref[0, r, gl]
                bm = bc_ref[0, r, g * SSD_N:(g + 1) * SSD_N]
                cm = bc_ref[0, r, (SSD_GROUPS + g) * SSD_N:(SSD_GROUPS + g + 1) * SSD_N]
                st = sts_ref[0, rs, gl]
                dst = dst_scr[:, gl]
                rsel = (sel_in + (16 + g * SSD_HPG) == sel_out).astype(F32)
                xdt = x * dt
                cbm = _mm(cm, bm, NT)
                al = ax[CHUNK - 1:CHUNK, :]
                ex, el = jnp.exp(ax), jnp.exp(al)
                dec = jnp.exp(al - ax)
                xd = xdt * dec
                dye = dyv * ex
                dxd = _mm(bm, dst)
                dxdt = dec * dxd
                dcm = _mm(dye, st, NT)
                dbm = _mm(xd, dst, NT)
                z = dye * _mm(cm, st) - dxd * xd
                zl = _colsum(dst * st) * el + _colsum(dxd * xd)
                z = z + jnp.where(row5 == CHUNK - 1, zl, 0.0)
                dcb = jnp.zeros((CHUNK, CHUNK), F32)
                for hh in range(SSD_HPG):
                    head = g * SSD_HPG + hh
                    lm, gm = _ssd_head(acsv, ar_ref, n, head, cbm, c)
                    dym = jnp.where(lane5 == hh, dyv, 0.0)
                    dxdt = dxdt + _mm(gm, dym, TN)
                    dg = _mm(dym, xdt, NT)
                    dcb = dcb + dg * lm
                    pm = dg * gm
                    dacs = dacs + jnp.where(c["lane"] == head + 16, _rowsum(pm) - _mmsel(pm, c["ones"], TN), 0.0)
                dbc_ref[0, r, (SSD_GROUPS + g) * SSD_N:(SSD_GROUPS + g + 1) * SSD_N] = dcm + _mm(dcb, bm)
                dbc_ref[0, r, g * SSD_N:(g + 1) * SSD_N] = dbm + _mm(dcb, cm, TN)
                dacs = dacs + _mmsel(z, rsel)
                ddt = ddt + _mmsel(dxdt * x, rsel)
                dx_ref[0, r, gl] = dxdt * dt
                dst_scr[:, gl] = dst * el + _mm(cm, dye, TN)
            ddt_ref[0, r, :] = ddt
            dacs_ref[0, r, :] = dacs
            return carry

        lax.fori_loop(0, ncb, chunk, 0)

    return _pcall(
        body, name="ssd_chunk_bwd", grid=(bsz, nsb),
        in_specs=[sp["wide"](), sp["wide"](), sp["wide"](), sp["bc"](), sp["small"](), sp["ar"], sp["wide"](), sp["st"]],
        out_specs=[sp["wide"](), sp["bc"](), sp["small"](), sp["small"]()],
        out_shape=[jax.ShapeDtypeStruct((bsz, seq, SSD_HEADS * SSD_P), F32),
                   jax.ShapeDtypeStruct((bsz, seq, 2 * SSD_GROUPS * SSD_N), F32),
                   jax.ShapeDtypeStruct((bsz, seq, LANES), F32), jax.ShapeDtypeStruct((bsz, seq, LANES), F32)],
        scratch_shapes=[pltpu.VMEM((SSD_N, SSD_HEADS * SSD_P), F32)],
        compiler_params=_params(("parallel", "arbitrary")),
    )(xs, dtx, acsx, bc, acs, ar, dy, sts)


def _ssd_prep_bwd(proj, dxs_c, dxs_d, dbc, ddt, dacs, dsm_gdn, cw, cb, sp, seq, tm):
    t = proj.shape[0]
    d = D_MODEL

    def body(i, x_ref, xh_ref, bc_ref, bch_ref, sm_ref, dxc_ref, dxd_ref, dbc_ref, ddt_ref, dacs_ref, dsg_ref,
             cw_ref, cb_ref, sp_ref, dpre_ref, dsm_ref, dcw_ref, dcb_ref, dsp_ref):
        keep, _ = _seq_flags(i, seq, tm)
        parts = ((x_ref, xh_ref, 0, d, (dxc_ref[...] + dxd_ref[...],)),
                 (bc_ref, bch_ref, d, 512, (dbc_ref[...],)))
        for xr, hr, off, w, grads in parts:
            taps = _conv_taps(xr[...], hr[...] * keep, 4)
            y = _conv(taps, cw_ref[:, off:off + w]) + cb_ref[:, off:off + w]
            ds_ = _dsilu(y, _sigmoid(y))
            o = 0
            for gr in grads:
                wg = gr.shape[1]
                dpre = gr * ds_[:, o:o + wg]
                dpre_ref[:, off + o:off + o + wg] = dpre
                dcb_ref[:, off + o:off + o + wg] += _colsum(dpre)
                for k in range(4):
                    dcw_ref[k:k + 1, off + o:off + o + wg] += _colsum(dpre * taps[k][:, o:o + wg])
                o += wg
        sm = sm_ref[...]
        lane = lax.broadcasted_iota(jnp.int32, sm.shape, 1)
        valid = (lane >= 16) & (lane < 32)
        xb = sm + sp_ref[1:2, :]
        dt = jnp.where(valid, _softplus(xb), 0.0)
        a_neg = -jnp.exp(sp_ref[0:1, :])
        dadt_s = _mmx(_block_tri(tm, True), dacs_ref[...])
        dxb = jnp.where(valid, (ddt_ref[...] + dadt_s * a_neg) * _sigmoid(xb), 0.0)
        dsm_ref[...] = (dsg_ref[...] + dxb).astype(BF16)
        dsp_ref[1:2, :] += _colsum(dxb)
        dsp_ref[0:1, :] += jnp.where(valid[0:1, :], _colsum(dadt_s * dt) * a_neg, 0.0)

    ins = [("row", proj, d, 5), ("prev", proj, d, 5), ("row", proj, 512, 12), ("prev", proj, 512, 12),
           ("row", proj, LANES, SMALL_CB), ("row", dxs_c, d, 0), ("row", dxs_d, d, 0), ("row", dbc, 512, 0),
           ("row", ddt, LANES, 0), ("row", dacs, LANES, 0), ("row", dsm_gdn, LANES, 0),
           ("full", cw), ("full", cb), ("full", sp)]
    return _rowwise("ssd_prep_bwd", body, t, tm, ins, [(d + 512, F32), (LANES, BF16)],
                    accs=[(SUBLANES, d + 512), (1, d + 512), (SUBLANES, LANES)])


def _gdn_chunk_bwd(qn, kn, vv, gs, gr, do, sts, tis, bsz, seq, sb, riders):
    nsb, ncb, _, sp = _gdn_specs(bsz, seq, sb)
    grid = (bsz, GDN_HEADS // GDN_HB, nsb)
    any_spec, rider_shapes, rider_sems, wrap = _riding_exchange(riders, True, 8, 4, grid)

    def body(q_ref, k_ref, v_ref, gs_ref, gr_ref, do_ref, st_ref, ti_ref, dq_ref, dk_ref, dv_ref, dgb_ref, ds_scr):
        hg = pl.program_id(1)

        @pl.when(pl.program_id(2) == 0)
        def _():
            ds_scr[...] = jnp.zeros_like(ds_scr)

        c = _chunk_consts()

        def chunk(nn, carry):
            n = ncb - 1 - nn
            r = pl.ds(pl.multiple_of(n * CHUNK, CHUNK), CHUNK)
            rs = pl.ds(pl.multiple_of(n * GDN_DK, GDN_DK), GDN_DK)
            gsv = gs_ref[0, r, :]
            heads = list(range(GDN_HB))
            sls = [slice(ih * GDN_DK, (ih + 1) * GDN_DK) for ih in heads]
            q = [q_ref[0, r, sl] for sl in sls]
            k = [k_ref[0, r, sl] for sl in sls]
            v = [v_ref[0, r, sl] for sl in sls]
            do_ = [do_ref[0, r, sl] for sl in sls]
            s = [st_ref[0, ih, rs, :] for ih in heads]
            tinv = [ti_ref[0, ih, r, :] for ih in heads]
            dsn = [ds_scr[ih] for ih in heads]
            beta, dc, eg, egl, ekd = zip(*[
                _gdn_gates(gsv, gr_ref[0, ih, pl.ds(n, 1), :], hg * GDN_HB + ih, c) for ih in heads])
            mul = lambda a, b: a * b
            kb = _hmap(mul, k, beta)
            rhs_w = _hmap(mul, kb, eg)
            u = _hmap(lambda t_, a, b: _mm3(t_, a * b), tinv, v, beta)
            w = _hmap(_mm3, tinv, rhs_w)
            amat = _hmap(lambda a, b, d_: jnp.where(c["strict"], _mm(a, b, NT) * d_, 0.0), kb, k, dc)
            qk = _hmap(lambda a, b, d_: _mm(a, b, NT) * d_, q, k, dc)
            qd = _hmap(mul, q, eg)
            kd = _hmap(mul, k, ekd)
            v_new = _hmap(lambda a, b, s_: a - _mm(b, s_), u, w, s)
            dv_new = _hmap(lambda qk_, d_, kd_, dn: _mm(qk_, d_, TN) + _mm(kd_, dn), qk, do_, kd, dsn)
            dqk = _hmap(lambda d_, vn: _mm(d_, vn, NT), do_, v_new)
            dqd = _hmap(lambda d_, s_: _mm(d_, s_, NT), do_, s)
            ds_new = _hmap(lambda qd_, d_, dn, e, w_, dvn: _mm(qd_, d_, TN) + dn * e - _mm(w_, dvn, TN),
                           qd, do_, dsn, egl, w, dv_new)
            dkd = _hmap(lambda vn, dn: _mm(vn, dn, NT), v_new, dsn)
            dgl = _hmap(lambda s_, dn, e: _colsum(_rowsum(s_ * dn)) * e, s, dsn, egl)
            dw = _hmap(lambda dvn, s_: -_mm(dvn, s_, NT), dv_new, s)
            dru = _hmap(lambda t_, a: _mm3(t_, a, TN), tinv, dv_new)
            drw = _hmap(lambda t_, a: _mm3(t_, a, TN), tinv, dw)
            da = _hmap(lambda a, u_, b, w_: jnp.where(c["strict"], -(_mm(a, u_, NT) + _mm(b, w_, NT)), 0.0), dru, u, drw, w)
            m = _hmap(mul, da, dc)
            dkb = _hmap(lambda a, e, m_, k_: a * e + _mm(m_, k_), drw, eg, m, k)
            mq = _hmap(mul, dqk, dc)
            dq = _hmap(lambda mq_, k_, a, e: _mm(mq_, k_) + a * e, mq, k, dqd, eg)
            dk = _hmap(lambda m_, kb_, mq_, q_, a, e, b, be: _mm(m_, kb_, TN) + _mm(mq_, q_, TN) + a * e + b * be,
                       m, kb, mq, q, dkd, ekd, dkb, beta)
            dbeta = _hmap(lambda a, v_, b, k_: _rowsum(a * v_) + _rowsum(b * k_), dru, v, dkb, k)
            pq = _hmap(lambda a, am, b, qk_: a * am + b * qk_, da, amat, dqk, qk)
            ekk = _hmap(lambda a, b: _rowsum(a * b), dkd, kd)
            dgc = _hmap(lambda pq_, a, rw, b, qd_, e, gl_: (
                _rowsum(pq_) - _mmsel(pq_, c["ones"], TN) + (_rowsum(a * rw) + _rowsum(b * qd_) - e)
                + jnp.where(c["row1"] == CHUNK - 1, _colsum(e) + gl_, 0.0)), pq, drw, rhs_w, dqd, qd, ekk, dgl)
            for ih in heads:
                ds_scr[ih] = ds_new[ih]
                dv_ref[0, r, sls[ih]] = dru[ih] * beta[ih]
                dq_ref[0, r, sls[ih]] = dq[ih]
                dk_ref[0, r, sls[ih]] = dk[ih]
                dgb_ref[0, r, sls[ih]] = jnp.where(c["lane"] == 0, dbeta[ih], jnp.where(c["lane"] == 1, dgc[ih], 0.0))
            return carry

        lax.fori_loop(0, ncb, chunk, 0)

    res = _pcall(
        wrap(body), name="gdn_chunk_bwd", grid=grid,
        in_specs=[sp["wide"](), sp["wide"](), sp["wide"](), sp["gs"], sp["gr"], sp["wide"](), sp["st"], sp["ti"]] + any_spec,
        out_specs=[sp["wide"](), sp["wide"](), sp["wide"](), sp["wide"]()] + any_spec,
        out_shape=[jax.ShapeDtypeStruct((bsz, seq, D_MODEL), F32)] * 4 + rider_shapes,
        scratch_shapes=[pltpu.VMEM((GDN_HB, GDN_DK, GDN_DK), F32)] + rider_sems,
        compiler_params=_params(("arbitrary", "arbitrary", "arbitrary")),
    )(qn, kn, vv, gs, gr, do, sts, tis, *riders)
    return res[:4], res[4:]


def _gdn_prep_bwd(proj, dqn, dkn, dvv, dgb, cw, gp, seq, tm):
    t = proj.shape[0]
    d = D_MODEL

    def body(i, q_ref, qh_ref, k_ref, kh_ref, v_ref, vh_ref, sm_ref, dq_ref, dk_ref, dv_ref, dgb_ref, cw_ref, gp_ref,
             dpre_ref, dsm_ref, dcw_ref, dgp_ref):
        keep, _ = _seq_flags(i, seq, tm)
        for x_ref, h_ref, g_ref, off, scale in ((q_ref, qh_ref, dq_ref, 0, GDN_DK ** -0.5),
                                               (k_ref, kh_ref, dk_ref, d, 1.0), (v_ref, vh_ref, dv_ref, 2 * d, None)):
            taps = _conv_taps(x_ref[...], h_ref[...] * keep, 4)
            y = _conv(taps, cw_ref[:, off:off + d])
            sy = _sigmoid(y)
            ds_ = _dsilu(y, sy)
            if scale is None:
                dpre = g_ref[...] * ds_
                dpre_ref[:, off:off + d] = dpre
                for k in range(4):
                    dcw_ref[k:k + 1, off:off + d] += _colsum(dpre * taps[k])
            else:
                a = y * sy
                for hh in range(GDN_HEADS):
                    sl = slice(hh * GDN_DK, (hh + 1) * GDN_DK)
                    s = a[:, sl]
                    n = lax.rsqrt(_rowsum(s * s) + EPS)
                    ah = s * n
                    gq = g_ref[:, sl]
                    dpre = (scale * n) * (gq - ah * _rowsum(gq * ah)) * ds_[:, sl]
                    dpre_ref[:, off + hh * GDN_DK:off + (hh + 1) * GDN_DK] = dpre
                    for k in range(4):
                        dcw_ref[k:k + 1, off + hh * GDN_DK:off + (hh + 1) * GDN_DK] += _colsum(dpre * taps[k][:, sl])
        sm = sm_ref[...]
        lane = lax.broadcasted_iota(jnp.int32, sm.shape, 1)
        si = lax.broadcasted_iota(jnp.int32, (d, LANES), 0)
        so = lax.broadcasted_iota(jnp.int32, (d, LANES), 1)
        sel = (((si % GDN_DK == 0) & (so == si // GDN_DK)) | ((si % GDN_DK == 1) & (so == si // GDN_DK + 8))).astype(F32)
        dsel = _mmx(dgb_ref[...], sel)
        is_g = (lane >= 8) & (lane < 16)
        dsel = jnp.where(is_g, _mmx(_block_tri(tm, True), dsel), dsel)
        beta = _sigmoid(sm)
        xb = sm + gp_ref[1:2, :]
        a_neg = -jnp.exp(gp_ref[0:1, :])
        sp = _softplus(xb)
        dxb = jnp.where(is_g, dsel * a_neg * _sigmoid(xb), 0.0)
        dsm_ref[...] = jnp.where(lane < 8, dsel * beta * (1.0 - beta), dxb)
        dgp_ref[1:2, :] += _colsum(dxb)
        dgp_ref[0:1, :] += _colsum(jnp.where(is_g, dsel * a_neg * sp, 0.0))

    ins = []
    for cb in range(3):
        ins += [("row", proj, d, cb), ("prev", proj, d, cb)]
    ins += [("row", proj, LANES, SMALL_CB), ("row", dqn, d, 0), ("row", dkn, d, 0), ("row", dvv, d, 0), ("row", dgb, d, 0),
            ("full", cw), ("full", gp)]
    return _rowwise("gdn_prep_bwd", body, t, tm, ins, [(3 * d, F32), (LANES, F32)],
                    accs=[(SUBLANES, 3 * d), (SUBLANES, LANES)])


def _first_bwd(x, dh1, dx1, w, tm):
    t, d = x.shape

    def body(i, x_ref, dh_ref, dx1_ref, w_ref, dx_ref, dw_ref):
        xh, r = _rms(x_ref[...], d)
        dh = dh_ref[...]
        dw_ref[...] += _colsum(dh * xh)
        dx_ref[...] = dx1_ref[...] + _rms_bwd(xh, r, dh * w_ref[...], d)

    return _rowwise("first_bwd", body, t, tm, [("row", x, d, 0), ("row", dh1, d, 0), ("row", dx1, d, 0), ("full", w)],
                    [(d, F32)], accs=[(1, d)])


def _exchange(name, arrays, scatter):
    n = len(arrays)

    def body(*refs):
        ins, outs, sems = refs[:n], refs[n:2 * n], refs[2 * n:]
        _exchange_phase(ins, outs, sems, scatter, start=True)
        _exchange_phase(ins, outs, sems, scatter, start=False)

    return _pcall(
        body, name=name,
        in_specs=[pl.BlockSpec(memory_space=pl.ANY)] * n,
        out_specs=[pl.BlockSpec(memory_space=pl.ANY)] * n,
        out_shape=_exchange_out_shapes(arrays, scatter),
        scratch_shapes=_exchange_sems(n),
    )(*arrays)


def _exchange_out_shapes(arrays, scatter):
    return [jax.ShapeDtypeStruct(a.shape if scatter else (N_DEV,) + a.shape, a.dtype) for a in arrays]


def _exchange_sems(n):
    return [pltpu.SemaphoreType.DMA((n, N_DEV - 1)), pltpu.SemaphoreType.DMA((n, N_DEV - 1)), pltpu.SemaphoreType.DMA((n,))]


def _exchange_phase(ins, outs, sems, scatter, start):
    send_sems, recv_sems, loc_sems = sems
    x, y, c = lax.axis_index("x"), lax.axis_index("y"), lax.axis_index("c")
    me = 4 * x + 2 * y + c
    for t in range(len(ins)):
        loc = pltpu.make_async_copy(ins[t].at[me] if scatter else ins[t], outs[t].at[me], loc_sems.at[t])
        if start:
            loc.start()
        else:
            loc.wait()
        for k in range(N_DEV - 1):
            bx, by, bc = ((k + 1) >> 2) & 1, ((k + 1) >> 1) & 1, (k + 1) & 1
            px = 1 - x if bx else x
            py = 1 - y if by else y
            pc = 1 - c if bc else c
            peer = 4 * px + 2 * py + pc
            src = ins[t].at[peer] if scatter else ins[t]
            copy = lambda dst: pltpu.make_async_remote_copy(
                src_ref=src, dst_ref=dst, send_sem=send_sems.at[t, k], recv_sem=recv_sems.at[t, k],
                device_id=(px, py, pc), device_id_type=pl.DeviceIdType.MESH)
            if start:
                copy(outs[t].at[me]).start()
            else:
                copy(outs[t].at[me]).wait_send()
                copy(outs[t].at[peer]).wait_recv()


def _adam_math(w, g, m, v):
    m = ADAM_B1 * m + (1.0 - ADAM_B1) * g
    v = ADAM_B2 * v + (1.0 - ADAM_B2) * (g * g)
    m_hat = m / (1.0 - ADAM_B1 ** ADAM_STEP)
    v_hat = v / (1.0 - ADAM_B2 ** ADAM_STEP)
    delta = -ADAM_LR * (m_hat / (jnp.sqrt(v_hat) + ADAM_EPS) + ADAM_WD * w)
    return delta, m, v


def _adam_big(name, parts, w, m, v, tm):
    r, c = w.shape
    tm = tm if r % tm == 0 else r

    def body(p_ref, w_ref, m_ref, v_ref, g_ref, d_ref, nm_ref, nv_ref):
        g = p_ref[0].astype(F32)
        for s in range(1, N_DEV):
            g = g + p_ref[s].astype(F32)
        g_ref[...] = g
        d_ref[...], nm_ref[...], nv_ref[...] = _adam_math(w_ref[...], g, m_ref[...], v_ref[...])

    blk = lambda: pl.BlockSpec((tm, c), lambda i: (i, 0))
    return _pcall(
        body, name=name, grid=(r // tm,),
        in_specs=[pl.BlockSpec((N_DEV, tm, c), lambda i: (0, i, 0)), blk(), blk(), blk()],
        out_specs=[blk(), blk(), blk(), blk()],
        out_shape=[jax.ShapeDtypeStruct((r, c), F32)] * 4,
        compiler_params=_params(("parallel",)),
    )(parts, w, m, v)


SMALL_ROWS = 56
ROW_DD, ROW_LOSS = 5, 6


def _small_sum(gathered):
    def body(g_ref, o_ref, x_ref):
        s = g_ref[0]
        for dev in range(1, N_DEV):
            s = s + g_ref[dev]
        o_ref[...] = s
        ri = lax.broadcasted_iota(jnp.int32, (D_MODEL, LANES), 0)
        ro = lax.broadcasted_iota(jnp.int32, (D_MODEL, LANES), 1)
        heads = _mmx(jnp.broadcast_to(s[ROW_DD:ROW_DD + 1, :], (SUBLANES, D_MODEL)), (ri // SSD_P == ro).astype(F32))
        loss = _rowsum(jnp.broadcast_to(s[ROW_LOSS:ROW_LOSS + 1, :], (SUBLANES, D_MODEL)))
        row = lax.broadcasted_iota(jnp.int32, (SUBLANES, LANES), 0)
        x_ref[...] = jnp.where(row == 0, heads, jnp.broadcast_to(loss, (SUBLANES, LANES)))

    return _pcall(
        body, name="small_sum",
        out_shape=[jax.ShapeDtypeStruct((SMALL_ROWS, D_MODEL), F32), jax.ShapeDtypeStruct((SUBLANES, LANES), F32)],
        compiler_params=_params(None),
    )(gathered)


def _adam_small(g, w, m, v):
    def body(g_ref, w_ref, m_ref, v_ref, d_ref, nm_ref, nv_ref):
        d_ref[...], nm_ref[...], nv_ref[...] = _adam_math(w_ref[...], g_ref[...], m_ref[...], v_ref[...])

    return _pcall(body, name="adam_small", out_shape=[jax.ShapeDtypeStruct(g.shape, F32)] * 3,
                  compiler_params=_params(None))(g, w, m, v)


def _pack(pieces, rows):
    flat = jnp.concatenate([p.reshape(-1).astype(F32) for p in pieces])
    return jnp.pad(flat, (0, rows * D_MODEL - flat.shape[0])).reshape(rows, D_MODEL)


def _unpack(packed, shapes):
    flat = packed.reshape(-1)
    out, off = [], 0
    for shp in shapes:
        size = 1
        for s in shp:
            size *= s
        out.append(flat[off:off + size].reshape(shp))
        off += size
    return out


def _permute_in(w):
    pad = jnp.zeros((w.shape[0], PROJ_W - D_IN), w.dtype)
    return jnp.concatenate([w[:, 0:4096], w[:, 4112:6672], w[:, 4096:4112], w[:, 6672:6688], pad], axis=1)


def _unpermute_in(g):
    return jnp.concatenate([g[:, 0:4096], g[:, 6656:6672], g[:, 4096:6656], g[:, 6672:6688]], axis=1)


def _lane_row(vec, start):
    return jnp.zeros((LANES,), F32).at[start:start + vec.shape[0]].set(vec)


def _cols_from_shards(g):
    return jnp.transpose(g, (1, 0, 2)).reshape(g.shape[1], N_DEV * g.shape[2])


def _cols_to_shards(a):
    return jnp.transpose(a.astype(BF16).reshape(a.shape[0], N_DEV, a.shape[1] // N_DEV), (1, 0, 2))


def _rows_to_shards(a):
    return a.astype(BF16).reshape(N_DEV, a.shape[0] // N_DEV, a.shape[1])


def _local_step(x, tgt, wp_in, rest, p, rest_is_sharded):
    bsz, seq, d = x.shape
    t = bsz * seq
    x2 = x.reshape(t, d)
    tgt2 = tgt.reshape(t, d)
    tm = min(256, seq)
    tm_wide = min(128, seq)
    sb = min(512, seq)

    gp = jnp.zeros((SUBLANES, LANES), F32).at[0].set(_lane_row(p["gdn_a_log"], 8)).at[1].set(_lane_row(p["gdn_dt_bias"], 8))
    sp = jnp.zeros((SUBLANES, LANES), F32).at[0].set(_lane_row(p["ssd_a_log"], 16)).at[1].set(_lane_row(p["ssd_dt_bias"], 16))
    dvec = jnp.repeat(p["ssd_d"], SSD_P).reshape(1, d)
    row = lambda v: v.reshape(1, -1)
    pre_mix, post_mix, pre_ffn, post_ffn = (row(p[k]) for k in ("pre_mix_norm", "post_mix_norm", "pre_ffn_norm", "post_ffn_norm"))
    gnw, snw = row(p["gdn_norm_w"]), row(p["ssd_norm_w"])
    gcw, scw, scb, fcw, fcb = p["gdn_conv_w"], p["ssd_conv_w"], row(p["ssd_conv_b"]), p["ffn_conv_w"], row(p["ffn_conv_b"])

    h1 = _norm_cast("norm_in", x2, pre_mix, tm)
    proj = _matmul("mm_proj", h1, wp_in, "nn", F32)
    b3 = lambda a: a.reshape(bsz, seq, a.shape[-1])
    b2 = lambda a: a.reshape(t, a.shape[-1])
    rows_of = lambda a, lo, n: jnp.transpose(a[:, lo:lo + n].reshape(bsz, seq // CHUNK, CHUNK, n), (0, 3, 1, 2))
    qn, kn, vv, gs = (b3(a) for a in _gdn_prep(proj, gcw, gp, seq, tm))
    gr = rows_of(b2(gs), 8, GDN_HEADS)
    (o_gdn, gdn_st, gdn_ti), gathered = _gdn_chunk_fwd(qn, kn, vv, gs, gr, bsz, seq, sb, list(rest) if rest_is_sharded else [])
    if rest_is_sharded:
        w_out, w_up, w_down = gathered[0].reshape(-1, d), _cols_from_shards(gathered[1]), gathered[2].reshape(-1, d)
    else:
        w_out, w_up, w_down = rest
    o_gdn = b2(o_gdn)
    xs, bc, dtx, acsx, acs = _ssd_prep(proj, scw, scb, sp, seq, tm)
    ar = rows_of(acs, 16, SSD_HEADS)
    y_ssd, ssd_st = _ssd_chunk_fwd(b3(xs), b3(bc), b3(dtx), b3(acsx), b3(acs), ar, bsz, seq, sb)
    y_ssd = b2(y_ssd)
    mixin = _gate_norm(o_gdn, y_ssd, xs, proj, gnw, snw, dvec, tm)
    mix = _matmul("mm_out", mixin, w_out, "nn", F32)
    x1, h2 = _mid(x2, mix, post_mix, pre_ffn, tm)
    u_pre = _matmul("mm_up", h2, w_up, "nn", F32)
    act = _ffn_act(u_pre, fcw, fcb, seq, tm_wide)
    f = _matmul("mm_down", act, w_down, "nn", F32, tk=1408)
    dy, df, loss_lanes, d_post_ffn = _final(x1, f, tgt2, post_ffn, tm)

    g_down = _matmul("mm_dw_down", act, df, "tn", F32, tm=1408)
    dact = _matmul("mm_dact", df, w_down, "nt", F32, tn=1408)
    du, d_fcw, d_fcb = _ffn_bwd(u_pre, dact, fcw, fcb, seq, tm_wide)
    du_pre = _conv_t("ffn_conv_t", du, fcw, 3, seq, tm_wide)
    g_up = _matmul("mm_dw_up", h2, du_pre, "tn", F32)
    dh2 = _matmul("mm_dh2", du_pre, w_up, "nt", F32)
    dx1, dmix, d_post_mix, d_pre_ffn = _mid_bwd(x1, mix, dy, dh2, post_mix, pre_ffn, tm)
    g_out = _matmul("mm_dw_out", mixin, dmix, "tn", F32)
    dmixin = _matmul("mm_dmixin", dmix, w_out, "nt", F32)
    do_gdn, dza, dy_ssd, dxs_d, dzs, d_gnw, d_snw, d_dd = _gate_norm_bwd(o_gdn, y_ssd, xs, proj, dmixin, gnw, snw, dvec, tm)
    dxs_c, dbc, ddt, dacs = (b2(a) for a in _ssd_chunk_bwd(
        b3(xs), b3(bc), b3(dtx), b3(acsx), b3(acs), ar, b3(dy_ssd), ssd_st, bsz, seq, sb))
    riders = [_rows_to_shards(g_out), _cols_to_shards(g_up), _rows_to_shards(g_down)] if rest_is_sharded else []
    dgdn, received = _gdn_chunk_bwd(qn, kn, vv, gs, gr, b3(do_gdn), gdn_st, gdn_ti, bsz, seq, sb, riders)
    if rest_is_sharded:
        g_out, g_up, g_down = received
    dqn, dkn, dvv, dgb = (b2(a) for a in dgdn)
    dpre_gdn, dsm_gdn, d_gcw, d_gp = _gdn_prep_bwd(proj, dqn, dkn, dvv, dgb, gcw, gp, seq, tm)
    dpre_ssd, dsm, d_scw, d_scb, d_sp = _ssd_prep_bwd(proj, dxs_c, dxs_d, dbc, ddt, dacs, dsm_gdn, scw, scb, sp, seq, tm)
    dqkv = _conv_t("gdn_conv_t", dpre_gdn, gcw, 4, seq, tm)
    dxbc = _conv_t("ssd_conv_t", dpre_ssd, scw, 4, seq, tm)
    dproj = jnp.concatenate([dqkv, dza, dzs, dxbc, dsm, jnp.zeros((t, PROJ_W - 6784), BF16)], axis=1)
    g_in = _matmul("mm_dw_in", h1, dproj, "tn", F32)
    dh1 = _matmul("mm_dh1", dproj, wp_in, "nt", F32)
    dx, d_pre_mix = _first_bwd(x2, dh1, dx1, pre_mix, tm)

    small = dict(pre_mix_norm=d_pre_mix, ssd_norm_w=d_snw, post_mix_norm=d_post_mix, pre_ffn_norm=d_pre_ffn,
                 post_ffn_norm=d_post_ffn, dd_lanes=d_dd, loss_lanes=loss_lanes, gdn_gates=d_gp, ssd_gates=d_sp,
                 gdn_norm_w=d_gnw, gdn_conv_w=d_gcw[0:4], ssd_conv_w=d_scw[0:4], ssd_conv_b=d_scb,
                 ffn_conv_w=d_fcw[0:3], ffn_conv_b=d_fcb)
    return dx.reshape(bsz, seq, d), g_in, g_out, g_up, g_down, small


def kernel(x, pre_mix_norm, w_in, gdn_conv_w, gdn_a_log, gdn_dt_bias, gdn_norm_w, ssd_conv_w, ssd_conv_b, ssd_a_log, ssd_dt_bias, ssd_d, ssd_norm_w, w_out, post_mix_norm, pre_ffn_norm, w_up, ffn_conv_w, ffn_conv_b, w_down, post_ffn_norm, loss_target, m_pre_mix_norm, m_w_in, m_gdn_conv_w, m_gdn_a_log, m_gdn_dt_bias, m_gdn_norm_w, m_ssd_conv_w, m_ssd_conv_b, m_ssd_a_log, m_ssd_dt_bias, m_ssd_d, m_ssd_norm_w, m_w_out, m_post_mix_norm, m_pre_ffn_norm, m_w_up, m_ffn_conv_w, m_ffn_conv_b, m_w_down, m_post_ffn_norm, v_pre_mix_norm, v_w_in, v_gdn_conv_w, v_gdn_a_log, v_gdn_dt_bias, v_gdn_norm_w, v_ssd_conv_w, v_ssd_conv_b, v_ssd_a_log, v_ssd_dt_bias, v_ssd_d, v_ssd_norm_w, v_w_out, v_post_mix_norm, v_pre_ffn_norm, v_w_up, v_ffn_conv_w, v_ffn_conv_b, v_w_down, v_post_ffn_norm):
    names = ["pre_mix_norm", "w_in", "gdn_conv_w", "gdn_a_log", "gdn_dt_bias", "gdn_norm_w", "ssd_conv_w", "ssd_conv_b",
             "ssd_a_log", "ssd_dt_bias", "ssd_d", "ssd_norm_w", "w_out", "post_mix_norm", "pre_ffn_norm", "w_up",
             "ffn_conv_w", "ffn_conv_b", "w_down", "post_ffn_norm"]
    w_args = [pre_mix_norm, w_in, gdn_conv_w, gdn_a_log, gdn_dt_bias, gdn_norm_w, ssd_conv_w, ssd_conv_b, ssd_a_log, ssd_dt_bias, ssd_d, ssd_norm_w, w_out, post_mix_norm, pre_ffn_norm, w_up, ffn_conv_w, ffn_conv_b, w_down, post_ffn_norm]
    m_args = [m_pre_mix_norm, m_w_in, m_gdn_conv_w, m_gdn_a_log, m_gdn_dt_bias, m_gdn_norm_w, m_ssd_conv_w, m_ssd_conv_b, m_ssd_a_log, m_ssd_dt_bias, m_ssd_d, m_ssd_norm_w, m_w_out, m_post_mix_norm, m_pre_ffn_norm, m_w_up, m_ffn_conv_w, m_ffn_conv_b, m_w_down, m_post_ffn_norm]
    v_args = [v_pre_mix_norm, v_w_in, v_gdn_conv_w, v_gdn_a_log, v_gdn_dt_bias, v_gdn_norm_w, v_ssd_conv_w, v_ssd_conv_b, v_ssd_a_log, v_ssd_dt_bias, v_ssd_d, v_ssd_norm_w, v_w_out, v_post_mix_norm, v_pre_ffn_norm, v_w_up, v_ffn_conv_w, v_ffn_conv_b, v_w_down, v_post_ffn_norm]
    w = {k: a[0] for k, a in zip(names, w_args)}
    m = {k: a[0] for k, a in zip(names, m_args)}
    v = {k: a[0] for k, a in zip(names, v_args)}
    idx = 4 * lax.axis_index("x") + 2 * lax.axis_index("y") + lax.axis_index("c")
    big = ("w_in", "w_out", "w_up", "w_down")
    conv = ("gdn_conv_w", "ssd_conv_w", "ffn_conv_w")

    conv_local = jnp.concatenate([jnp.pad(w[k], ((0, 4 - w[k].shape[0]), (0, 0))) for k in conv], axis=1)
    g_in, g_conv = _exchange("gather_weights", [w["w_in"].astype(BF16), conv_local], scatter=False)
    wp_in = _permute_in(_cols_from_shards(g_in))
    p = {k: w[k] for k in names if k not in big and k not in conv}
    off = 0
    for k in conv:
        cw = w[k].shape[1]
        p[k] = jnp.transpose(g_conv[:, :w[k].shape[0], off:off + cw], (1, 0, 2)).reshape(w[k].shape[0], N_DEV * cw)
        off += cw

    rest = tuple(w[k].astype(BF16) for k in ("w_out", "w_up", "w_down"))
    dx, d_in, p_out, p_up, p_down, small = _local_step(x, loss_target, wp_in, rest, p, True)

    gate_row = jnp.concatenate([small["gdn_gates"][0], small["gdn_gates"][1], small["ssd_gates"][0], small["ssd_gates"][1],
                                small["gdn_norm_w"][0], jnp.zeros((D_MODEL - 5 * LANES,), F32)]).reshape(1, D_MODEL)
    pack = _pack([small["pre_mix_norm"], small["ssd_norm_w"], small["post_mix_norm"], small["pre_ffn_norm"],
                  small["post_ffn_norm"], small["dd_lanes"], small["loss_lanes"], gate_row,
                  small["gdn_conv_w"], small["ssd_conv_w"], jnp.pad(small["ssd_conv_b"], ((0, 0), (0, 512))),
                  jnp.pad(small["ffn_conv_w"].reshape(-1), (0, 17 * D_MODEL - 3 * 2 * D_FF)),
                  jnp.pad(small["ffn_conv_b"], ((0, 0), (0, 512)))], SMALL_ROWS)
    (p_in,) = _exchange("scatter_grads", [_cols_to_shards(_unpermute_in(d_in))], scatter=True)
    (pack_all,) = _exchange("gather_small", [pack], scatter=False)
    ssum, extra = _small_sum(pack_all)

    grads, deltas, new_m, new_v = {}, {}, {}, {}
    for k, parts in (("w_in", p_in), ("w_out", p_out), ("w_up", p_up), ("w_down", p_down)):
        grads[k], deltas[k], new_m[k], new_v[k] = _adam_big("adam_" + k, parts, w[k], m[k], v[k], 256)

    flat = ssum.reshape(-1)
    gate = ssum[7]
    sg = dict(pre_mix_norm=ssum[0], ssd_norm_w=ssum[1], post_mix_norm=ssum[2], pre_ffn_norm=ssum[3], post_ffn_norm=ssum[4],
              gdn_a_log=gate[8:16], gdn_dt_bias=gate[LANES + 8:LANES + 16], ssd_a_log=gate[2 * LANES + 16:2 * LANES + 32],
              ssd_dt_bias=gate[3 * LANES + 16:3 * LANES + 32], gdn_norm_w=gate[4 * LANES:5 * LANES], ssd_d=extra[0, 0:SSD_HEADS])
    o = 8 * D_MODEL
    full_gcw = flat[o:o + 4 * 3072].reshape(4, 3072)
    o += 12 * D_MODEL
    full_scw = flat[o:o + 4 * 1536].reshape(4, 1536)
    o += 6 * D_MODEL
    sg["ssd_conv_b"] = flat[o:o + 1536]
    o += 2 * D_MODEL
    full_fcw = flat[o:o + 3 * 2 * D_FF].reshape(3, 2 * D_FF)
    o += 17 * D_MODEL
    sg["ffn_conv_b"] = flat[o:o + 2 * D_FF]
    for k, full in (("gdn_conv_w", full_gcw), ("ssd_conv_w", full_scw), ("ffn_conv_w", full_fcw)):
        cw = w[k].shape[1]
        sg[k] = lax.dynamic_slice_in_dim(full, idx * cw, cw, axis=1)
    small_names = [k for k in names if k not in big]
    rows = 24
    gpk = _pack([sg[k] for k in small_names], rows)
    dpk, mpk, vpk = _adam_small(gpk, _pack([w[k] for k in small_names], rows), _pack([m[k] for k in small_names], rows),
                                _pack([v[k] for k in small_names], rows))
    shapes = [w[k].shape for k in small_names]
    for k, g_, d_, m_, v_ in zip(small_names, _unpack(gpk, shapes), _unpack(dpk, shapes), _unpack(mpk, shapes), _unpack(vpk, shapes)):
        grads[k], deltas[k], new_m[k], new_v[k] = g_, d_, m_, v_

    loss = extra[1, 0]
    lead = lambda a: a[None]
    return (loss, dx, *[lead(grads[k]) for k in names], *[lead(deltas[k]) for k in names],
            *[lead(new_m[k]) for k in names], *[lead(new_v[k]) for k in names])
```

```python
import functools

import jax
import jax.numpy as jnp
from jax import lax
from jax.experimental import pallas as pl
from jax.experimental.pallas import tpu as pltpu

F32 = jnp.float32
BF16 = jnp.bfloat16
MXU_DTYPE = jnp.bfloat16
HIGHEST = lax.Precision.HIGHEST
VMEM_LIMIT_V7X = 48 * 1024 * 1024
SUBLANES = 8
LANES = 128

D_MODEL = 1024
GDN_HEADS = 8
GDN_DK = 128
SSD_HEADS = 16
SSD_P = 64
SSD_GROUPS = 2
SSD_HPG = 8
SSD_N = 128
CHUNK = 64
D_FF = 2816
EPS = 1e-6
N_DEV = 8
PROJ_W = 7168
SMALL_CB = 52
D_IN = 6688

ADAM_LR = 0.001
ADAM_B1 = 0.9
ADAM_B2 = 0.999
ADAM_EPS = 1e-08
ADAM_WD = 0.01
ADAM_STEP = 10

NN = (((1,), (0,)), ((), ()))
NT = (((1,), (1,)), ((), ()))
TN = (((0,), (0,)), ((), ()))


def _pcall(body, **kw):
    return pl.pallas_call(body, **kw)


def _mm(a, b, dims=NN):
    return lax.dot_general(a.astype(MXU_DTYPE), b.astype(MXU_DTYPE), dims, preferred_element_type=F32)


def _mmx(a, b, dims=NN):
    return lax.dot_general(a, b, dims, precision=HIGHEST, preferred_element_type=F32)


def _split(a):
    hi = a.astype(MXU_DTYPE)
    return hi, (a - hi.astype(F32)).astype(MXU_DTYPE)


def _mm3(a, b, dims=NN):
    (ah, al), (bh, bl) = _split(a), _split(b)
    dot = lambda p, q: lax.dot_general(p, q, dims, preferred_element_type=F32)
    return dot(ah, bh) + (dot(ah, bl) + dot(al, bh))


def _mmsel(a, sel, dims=NN):
    ah, al = _split(a)
    s = sel.astype(MXU_DTYPE)
    return (lax.dot_general(ah, s, dims, preferred_element_type=F32)
            + lax.dot_general(al, s, dims, preferred_element_type=F32))


def _sigmoid(x):
    return 1.0 / (1.0 + jnp.exp(-x))


def _softplus(x):
    return jnp.maximum(x, 0.0) + jnp.log(1.0 + jnp.exp(-jnp.abs(x)))


def _dsilu(x, s):
    return s * (1.0 + x * (1.0 - s))


def _rowsum(x):
    return jnp.sum(x, axis=1, keepdims=True)


def _colsum(x):
    return jnp.sum(x, axis=0, keepdims=True)


def _pick(dim, pref):
    if dim <= pref:
        return dim
    best = None
    t = LANES
    while t <= pref:
        if dim % t == 0:
            best = t
        t += LANES
    return dim if best is None else best


def _params(sem):
    return pltpu.CompilerParams(dimension_semantics=sem, vmem_limit_bytes=VMEM_LIMIT_V7X)


def _matmul(name, a, b, mode, out_dtype, tm=1024, tn=1024, tk=1024, scatter_riders=()):
    if mode == "nn":
        (m, k), (_, n) = a.shape, b.shape
    elif mode == "nt":
        (m, k), (n, _) = a.shape, b.shape
    else:
        (k, m), (_, n) = a.shape, b.shape
    tm, tn, tk = _pick(m, tm), _pick(n, tn), _pick(k, tk)
    nk = k // tk
    if mode == "tn":
        a_spec = pl.BlockSpec((tk, tm), lambda i, j, kk: (kk, i))
    else:
        a_spec = pl.BlockSpec((tm, tk), lambda i, j, kk: (i, kk))
    if mode == "nt":
        b_spec = pl.BlockSpec((tn, tk), lambda i, j, kk: (j, kk))
    else:
        b_spec = pl.BlockSpec((tk, tn), lambda i, j, kk: (kk, j))
    dims = {"nn": NN, "nt": NT, "tn": TN}[mode]

    def body(a_ref, b_ref, o_ref, acc):
        kk = pl.program_id(2)

        @pl.when(kk == 0)
        def _():
            acc[...] = jnp.zeros_like(acc)

        acc[...] += _mm(a_ref[...], b_ref[...], dims)

        @pl.when(kk == nk - 1)
        def _():
            o_ref[...] = acc[...].astype(out_dtype)

    grid = (m // tm, n // tn, nk)
    riders = list(scatter_riders)
    any_spec, rider_shapes, rider_sems, wrap = _riding_exchange(riders, True, 2, 1, grid)
    res = _pcall(
        wrap(body), name=name, grid=grid,
        in_specs=[a_spec, b_spec] + any_spec,
        out_specs=[pl.BlockSpec((tm, tn), lambda i, j, kk: (i, j))] + any_spec,
        out_shape=[jax.ShapeDtypeStruct((m, n), out_dtype)] + rider_shapes,
        scratch_shapes=[pltpu.VMEM((tm, tn), F32)] + rider_sems,
        compiler_params=_params(("arbitrary", "arbitrary", "arbitrary") if riders else ("parallel", "parallel", "arbitrary")),
    )(a, b, *riders)
    return (res[0], res[1:]) if riders else res[0]


def _rowwise(name, body, n_rows, tm, ins, outs, accs=()):
    arrays, in_specs = [], []
    last8 = n_rows // SUBLANES - 1
    per = tm // SUBLANES
    for spec in ins:
        kind, arr = spec[0], spec[1]
        if kind == "full":
            in_specs.append(pl.BlockSpec(arr.shape, lambda i, nd=arr.ndim: (0,) * nd))
        else:
            w, cb = spec[2], spec[3]
            if kind == "row":
                in_specs.append(pl.BlockSpec((tm, w), lambda i, cb=cb: (i, cb)))
            elif kind == "prev":
                in_specs.append(pl.BlockSpec((SUBLANES, w), lambda i, cb=cb: (jnp.maximum(i * per - 1, 0), cb)))
            else:
                in_specs.append(pl.BlockSpec((SUBLANES, w), lambda i, cb=cb: (jnp.minimum((i + 1) * per, last8), cb)))
        arrays.append(arr)
    out_shape = [jax.ShapeDtypeStruct((n_rows, w), dt) for (w, dt) in outs]
    out_shape += [jax.ShapeDtypeStruct(s, F32) for s in accs]
    out_specs = [pl.BlockSpec((tm, w), lambda i: (i, 0)) for (w, _) in outs]
    out_specs += [pl.BlockSpec(s, lambda i: (0, 0)) for s in accs]
    n_io = len(ins) + len(outs)

    def kern(*refs):
        i = pl.program_id(0)
        if accs:
            @pl.when(i == 0)
            def _():
                for r in refs[n_io:]:
                    r[...] = jnp.zeros_like(r)
        body(i, *refs)

    res = _pcall(
        kern, name=name, grid=(n_rows // tm,), in_specs=in_specs, out_specs=out_specs, out_shape=out_shape,
        compiler_params=_params(("arbitrary",)),
    )(*arrays)
    return res


def _shift_down(x, halo, j):
    r = pltpu.roll(x, j, 0)
    hr = pltpu.roll(halo, j, 0)
    rows = lax.broadcasted_iota(jnp.int32, (SUBLANES, x.shape[1]), 0)
    top = jnp.where(rows < j, hr, r[0:SUBLANES])
    return jnp.concatenate([top, r[SUBLANES:]], axis=0)


def _shift_up(x, halo, j):
    tm = x.shape[0]
    r = pltpu.roll(x, tm - j, 0)
    hr = pltpu.roll(halo, SUBLANES - j, 0)
    rows = lax.broadcasted_iota(jnp.int32, (SUBLANES, x.shape[1]), 0)
    bot = jnp.where(rows >= SUBLANES - j, hr, r[tm - SUBLANES:])
    return jnp.concatenate([r[:tm - SUBLANES], bot], axis=0)


def _conv_taps(x, halo, kw):
    return [x if kw - 1 - k == 0 else _shift_down(x, halo, kw - 1 - k) for k in range(kw)]


def _conv(taps, w):
    y = taps[0] * w[0:1]
    for k in range(1, len(taps)):
        y = y + taps[k] * w[k:k + 1]
    return y


def _rms(x, width):
    r = lax.rsqrt(jnp.sum(x * x, axis=-1, keepdims=True) * (1.0 / width) + EPS)
    return x * r, r


def _rms_bwd(xh, r, dxh, width):
    return r * (dxh - xh * (jnp.sum(dxh * xh, axis=-1, keepdims=True) * (1.0 / width)))


def _seq_flags(i, seq, tm):
    nps = seq // tm
    pos = i % nps
    return jnp.where(pos == 0, 0.0, 1.0), jnp.where(pos == nps - 1, 0.0, 1.0)


def _norm_cast(name, x, w, tm):
    t, d = x.shape

    def body(i, x_ref, w_ref, h_ref):
        xh, _ = _rms(x_ref[...], d)
        h_ref[...] = (xh * w_ref[...]).astype(BF16)

    return _rowwise(name, body, t, tm, [("row", x, d, 0), ("full", w)], [(d, BF16)])[0]


def _gdn_prep(proj, cw, gp, seq, tm):
    t = proj.shape[0]
    d = D_MODEL

    def body(i, q_ref, qh_ref, k_ref, kh_ref, v_ref, vh_ref, sm_ref, cw_ref, gp_ref, qn_ref, kn_ref, vv_ref, gs_ref):
        keep, _ = _seq_flags(i, seq, tm)
        for x_ref, h_ref, o_ref, off, scale in ((q_ref, qh_ref, qn_ref, 0, GDN_DK ** -0.5),
                                               (k_ref, kh_ref, kn_ref, d, 1.0), (v_ref, vh_ref, vv_ref, 2 * d, None)):
            y = _conv(_conv_taps(x_ref[...], h_ref[...] * keep, 4), cw_ref[:, off:off + d])
            a = y * _sigmoid(y)
            if scale is None:
                o_ref[...] = a
            else:
                for hh in range(GDN_HEADS):
                    s = a[:, hh * GDN_DK:(hh + 1) * GDN_DK]
                    n = lax.rsqrt(_rowsum(s * s) + EPS)
                    o_ref[:, hh * GDN_DK:(hh + 1) * GDN_DK] = s * (n * scale)
        sm = sm_ref[...]
        lane = lax.broadcasted_iota(jnp.int32, sm.shape, 1)
        beta = _sigmoid(sm)
        g = jnp.where((lane >= 8) & (lane < 16), -jnp.exp(gp_ref[0:1, :]) * _softplus(sm + gp_ref[1:2, :]), 0.0)
        gs_ref[...] = jnp.where(lane < 8, beta, _mmx(_block_tri(tm, False), g))

    ins = []
    for cb in range(3):
        ins += [("row", proj, d, cb), ("prev", proj, d, cb)]
    ins += [("row", proj, LANES, SMALL_CB), ("full", cw), ("full", gp)]
    return _rowwise("gdn_prep", body, t, tm, ins, [(d, F32), (d, F32), (d, F32), (LANES, F32)])


def _block_tri(tm, upper):
    ri = lax.broadcasted_iota(jnp.int32, (tm, tm), 0)
    ci = lax.broadcasted_iota(jnp.int32, (tm, tm), 1)
    tri = (ri <= ci) if upper else (ri >= ci)
    return (tri & ((ri // CHUNK) == (ci // CHUNK))).astype(F32)


def _chunk_consts():
    row = lax.broadcasted_iota(jnp.int32, (CHUNK, CHUNK), 0)
    col = lax.broadcasted_iota(jnp.int32, (CHUNK, CHUNK), 1)
    return dict(
        tril=row >= col, strict=row > col, eye=(row == col).astype(F32),
        lane=lax.broadcasted_iota(jnp.int32, (CHUNK, LANES), 1),
        row1=lax.broadcasted_iota(jnp.int32, (CHUNK, 1), 0),
        ones=jnp.ones((CHUNK, LANES), F32))


def _hmap(fn, *lists):
    return [fn(*a) for a in zip(*lists)]


def _tri_inv(nmats, eye):
    x = [eye - n for n in nmats]
    p = _hmap(_mm3, nmats, nmats)
    for lvl in range(5):
        x = _hmap(lambda xi, pi: xi + _mm3(xi, pi), x, p)
        if lvl < 4:
            p = _hmap(_mm3, p, p)
    return x


def _gdn_gates(gs, gc_row, h, c):
    beta = _rowsum(jnp.where(c["lane"] == h, gs, 0.0))
    gc = _rowsum(jnp.where(c["lane"] == h + 8, gs, 0.0))
    dc = jnp.exp(jnp.where(c["tril"], gc - gc_row, -1e30))
    gl = gc[CHUNK - 1:CHUNK, :]
    return beta, dc, jnp.exp(gc), jnp.exp(gl), jnp.exp(gl - gc)


GDN_HB_FWD = 8
GDN_HB_BWD = 4


def _gdn_specs(seq, sb, hb, backward):
    nsb = seq // sb
    ncb = sb // CHUNK
    order = (lambda j: nsb - 1 - j) if backward else (lambda j: j)
    specs = dict(
        wide=lambda: pl.BlockSpec((1, sb, hb * GDN_DK), lambda b, h, j: (b, order(j), h)),
        gs=pl.BlockSpec((1, sb, LANES), lambda b, h, j: (b, order(j), 0)),
        gr=pl.BlockSpec((1, hb, ncb, CHUNK), lambda b, h, j: (b, h, order(j), 0)),
        st=pl.BlockSpec((1, hb, ncb * GDN_DK, GDN_DK), lambda b, h, j: (b, h, order(j), 0)),
        ti=pl.BlockSpec((1, hb, sb, CHUNK), lambda b, h, j: (b, h, order(j), 0)))
    return nsb, ncb, specs


def _riding_exchange(arrays, scatter, n_in, n_out, grid):
    n = len(arrays)
    if n == 0:
        return [], [], [], lambda body: body
    any_spec = [pl.BlockSpec(memory_space=pl.ANY)] * n

    def wrap(body):
        def wrapped(*refs):
            ins = refs[n_in:n_in + n]
            outs = refs[n_in + n + n_out:n_in + 2 * n + n_out]
            sems = refs[len(refs) - 3:]
            pid = [pl.program_id(a) for a in range(len(grid))]
            first = functools.reduce(lambda a, b: a & b, [p == 0 for p in pid])
            last = functools.reduce(lambda a, b: a & b, [p == g - 1 for p, g in zip(pid, grid)])

            @pl.when(first)
            def _():
                _exchange_phase(ins, outs, sems, scatter, start=True)

            body(*refs[:n_in], *refs[n_in + n:n_in + n + n_out], *refs[n_in + 2 * n + n_out:len(refs) - 3])

            @pl.when(last)
            def _():
                _exchange_phase(ins, outs, sems, scatter, start=False)

        return wrapped

    return any_spec, _exchange_out_shapes(arrays, scatter), _exchange_sems(n), wrap


def _gdn_chunk_fwd(qn, kn, vv, gs, gr, bsz, seq, sb, riders):
    hb = GDN_HB_FWD
    nsb, ncb, sp = _gdn_specs(seq, sb, hb, False)
    grid = (bsz, GDN_HEADS // hb, nsb)
    any_spec, rider_shapes, rider_sems, wrap = _riding_exchange(riders, False, 5, 3, grid)

    def body(q_ref, k_ref, v_ref, gs_ref, gr_ref, o_ref, st_ref, ti_ref, s_scr):
        hg = pl.program_id(1)

        @pl.when(pl.program_id(2) == 0)
        def _():
            s_scr[...] = jnp.zeros_like(s_scr)

        c = _chunk_consts()

        def chunk(n, carry):
            r = pl.ds(pl.multiple_of(n * CHUNK, CHUNK), CHUNK)
            rs = pl.ds(pl.multiple_of(n * GDN_DK, GDN_DK), GDN_DK)
            gsv = gs_ref[0, r, :]
            heads = list(range(hb))
            sls = [slice(ih * GDN_DK, (ih + 1) * GDN_DK) for ih in heads]
            q = [q_ref[0, r, sl] for sl in sls]
            k = [k_ref[0, r, sl] for sl in sls]
            v = [v_ref[0, r, sl] for sl in sls]
            beta, dc, eg, egl, ekd = zip(*[
                _gdn_gates(gsv, gr_ref[0, ih, pl.ds(n, 1), :], hg * hb + ih, c) for ih in heads])
            kb = _hmap(lambda a, b: a * b, k, beta)
            amat = _hmap(lambda a, b, d_: jnp.where(c["strict"], _mm(a, b, NT) * d_, 0.0), kb, k, dc)
            tinv = _tri_inv(amat, c["eye"])
            u = _hmap(lambda t_, a, b: _mm3(t_, a * b), tinv, v, beta)
            w = _hmap(lambda t_, a, b: _mm3(t_, a * b), tinv, kb, eg)
            qk = _hmap(lambda a, b, d_: _mm(a, b, NT) * d_, q, k, dc)
            s = [s_scr[ih] for ih in heads]
            v_new = _hmap(lambda a, b, s_: a - _mm(b, s_), u, w, s)
            o = _hmap(lambda a, e, s_, qk_, vn: _mm(a * e, s_) + _mm(qk_, vn), q, eg, s, qk, v_new)
            s_new = _hmap(lambda s_, e, a, f, vn: s_ * e + _mm(a * f, vn, TN), s, egl, k, ekd, v_new)
            for ih in heads:
                o_ref[0, r, sls[ih]] = o[ih]
                st_ref[0, ih, rs, :] = s[ih]
                ti_ref[0, ih, r, :] = tinv[ih]
                s_scr[ih] = s_new[ih]
            return carry

        lax.fori_loop(0, ncb, chunk, 0)

    t3 = (bsz, seq, D_MODEL)
    res = _pcall(
        wrap(body), name="gdn_chunk_fwd", grid=grid,
        in_specs=[sp["wide"](), sp["wide"](), sp["wide"](), sp["gs"], sp["gr"]] + any_spec,
        out_specs=[sp["wide"](), sp["st"], sp["ti"]] + any_spec,
        out_shape=[jax.ShapeDtypeStruct(t3, F32),
                   jax.ShapeDtypeStruct((bsz, GDN_HEADS, (seq // CHUNK) * GDN_DK, GDN_DK), F32),
                   jax.ShapeDtypeStruct((bsz, GDN_HEADS, seq, CHUNK), F32)] + rider_shapes,
        scratch_shapes=[pltpu.VMEM((hb, GDN_DK, GDN_DK), F32)] + rider_sems,
        compiler_params=_params(("arbitrary", "arbitrary", "arbitrary")),
    )(qn, kn, vv, gs, gr, *riders)
    return res[:3], res[3:]


def _ssd_prep(proj, cw, cb, sp, seq, tm):
    t = proj.shape[0]
    d = D_MODEL
    ssd_w = SSD_HEADS * SSD_P

    def body(i, x_ref, xh_ref, bc_ref, bch_ref, sm_ref, cw_ref, cb_ref, sp_ref, xs_ref, bco_ref, dtx_ref, acsx_ref, acs_ref):
        keep, _ = _seq_flags(i, seq, tm)
        y = _conv(_conv_taps(x_ref[...], xh_ref[...] * keep, 4), cw_ref[:, 0:d]) + cb_ref[:, 0:d]
        xs_ref[...] = y * _sigmoid(y)
        y = _conv(_conv_taps(bc_ref[...], bch_ref[...] * keep, 4), cw_ref[:, d:d + 512]) + cb_ref[:, d:d + 512]
        bco_ref[...] = y * _sigmoid(y)
        sm = sm_ref[...]
        lane = lax.broadcasted_iota(jnp.int32, sm.shape, 1)
        valid = (lane >= 16) & (lane < 32)
        dt = jnp.where(valid, _softplus(sm + sp_ref[1:2, :]), 0.0)
        adt = dt * (-jnp.exp(sp_ref[0:1, :]))
        acs = _mmx(_block_tri(tm, False), adt)
        l64 = lax.broadcasted_iota(jnp.int32, (LANES, ssd_w), 0)
        d64 = lax.broadcasted_iota(jnp.int32, (LANES, ssd_w), 1)
        e64 = (l64 - 16 == d64 // SSD_P).astype(F32)
        dtx_ref[...] = _mmx(dt, e64)
        acsx_ref[...] = _mmx(acs, e64)
        acs_ref[...] = acs

    ins = [("row", proj, d, 5), ("prev", proj, d, 5), ("row", proj, 512, 12), ("prev", proj, 512, 12),
           ("row", proj, LANES, SMALL_CB), ("full", cw), ("full", cb), ("full", sp)]
    return _rowwise("ssd_prep", body, t, tm, ins, [(d, F32), (512, F32), (ssd_w, F32), (ssd_w, F32), (LANES, F32)])


SSD_GW = SSD_HPG * SSD_P


def _ssd_head(acs, ar_ref, n, head, cbm, c):
    col = _rowsum(jnp.where(c["lane"] == head + 16, acs, 0.0))
    lm = jnp.exp(jnp.where(c["tril"], col - ar_ref[0, head, pl.ds(n, 1), :], -1e30))
    return lm, cbm * lm


def _ssd_specs(seq, sb):
    nsb = seq // sb
    ncb = sb // CHUNK
    def specs(order):
        return dict(
            wide=lambda: pl.BlockSpec((1, sb, SSD_HEADS * SSD_P), lambda b, j: (b, order(j), 0)),
            bc=lambda: pl.BlockSpec((1, sb, 2 * SSD_GROUPS * SSD_N), lambda b, j: (b, order(j), 0)),
            half=lambda: pl.BlockSpec((1, sb, SSD_GROUPS * SSD_N), lambda b, j: (b, order(j), 0)),
            small=lambda: pl.BlockSpec((1, sb, LANES), lambda b, j: (b, order(j), 0)),
            ar=pl.BlockSpec((1, SSD_HEADS, ncb, CHUNK), lambda b, j: (b, 0, order(j), 0)),
            st=pl.BlockSpec((1, ncb * SSD_N, SSD_HEADS * SSD_P), lambda b, j: (b, order(j), 0)))
    return nsb, ncb, specs(lambda j: j), specs(lambda j: nsb - 1 - j)


def _ssd_chunk_fwd(xs, bc, dtx, acsx, acs, ar, bsz, seq, sb):
    nsb, ncb, sp, _ = _ssd_specs(seq, sb)

    def body(x_ref, dtx_ref, ax_ref, bc_ref, acs_ref, ar_ref, y_ref, sts_ref, st_scr):
        @pl.when(pl.program_id(1) == 0)
        def _():
            st_scr[...] = jnp.zeros_like(st_scr)

        c = _chunk_consts()
        lane5 = lax.broadcasted_iota(jnp.int32, (CHUNK, SSD_GW), 1) // SSD_P

        def chunk(n, carry):
            r = pl.ds(pl.multiple_of(n * CHUNK, CHUNK), CHUNK)
            rs = pl.ds(pl.multiple_of(n * SSD_N, SSD_N), SSD_N)
            acsv = acs_ref[0, r, :]
            for g in range(SSD_GROUPS):
                gl = slice(g * SSD_GW, (g + 1) * SSD_GW)
                x, dt, ax = x_ref[0, r, gl], dtx_ref[0, r, gl], ax_ref[0, r, gl]
                bm = bc_ref[0, r, g * SSD_N:(g + 1) * SSD_N]
                cm = bc_ref[0, r, (SSD_GROUPS + g) * SSD_N:(SSD_GROUPS + g + 1) * SSD_N]
                xdt = x * dt
                cbm = _mm(cm, bm, NT)
                al = ax[CHUNK - 1:CHUNK, :]
                st = st_scr[:, gl]
                y = _mm(cm, st) * jnp.exp(ax)
                for hh in range(SSD_HPG):
                    _, gm = _ssd_head(acsv, ar_ref, n, g * SSD_HPG + hh, cbm, c)
                    y = y + _mm(gm, jnp.where(lane5 == hh, xdt, 0.0))
                y_ref[0, r, gl] = y
                sts_ref[0, rs, gl] = st
                st_scr[:, gl] = st * jnp.exp(al) + _mm(bm, xdt * jnp.exp(al - ax), TN)
            return carry

        lax.fori_loop(0, ncb, chunk, 0)

    return _pcall(
        body, name="ssd_chunk_fwd", grid=(bsz, nsb),
        in_specs=[sp["wide"](), sp["wide"](), sp["wide"](), sp["bc"](), sp["small"](), sp["ar"]],
        out_specs=[sp["wide"](), sp["st"]],
        out_shape=[jax.ShapeDtypeStruct((bsz, seq, SSD_HEADS * SSD_P), F32),
                   jax.ShapeDtypeStruct((bsz, (seq // CHUNK) * SSD_N, SSD_HEADS * SSD_P), F32)],
        scratch_shapes=[pltpu.VMEM((SSD_N, SSD_HEADS * SSD_P), F32)],
        compiler_params=_params(("parallel", "arbitrary")),
    )(xs, dtx, acsx, bc, acs, ar)


def _gate_norm(o_gdn, y_ssd, xs, proj, gnw, snw, dvec, tm):
    t = o_gdn.shape[0]
    d = D_MODEL

    def body(i, o_ref, za_ref, y_ref, xs_ref, zs_ref, gnw_ref, snw_ref, dv_ref, out_ref):
        for hh in range(GDN_HEADS):
            sl = slice(hh * GDN_DK, (hh + 1) * GDN_DK)
            oh, _ = _rms(o_ref[:, sl], GDN_DK)
            z = za_ref[:, sl]
            out_ref[:, sl] = (oh * gnw_ref[...] * (z * _sigmoid(z))).astype(BF16)
        zs = zs_ref[...]
        yg = (y_ref[...] + dv_ref[...] * xs_ref[...]) * (zs * _sigmoid(zs))
        for g in range(SSD_GROUPS):
            sl = slice(g * 512, (g + 1) * 512)
            yh, _ = _rms(yg[:, sl], 512)
            out_ref[:, d + g * 512:d + (g + 1) * 512] = (yh * snw_ref[:, sl]).astype(BF16)

    ins = [("row", o_gdn, d, 0), ("row", proj, d, 3), ("row", y_ssd, d, 0), ("row", xs, d, 0), ("row", proj, d, 4),
           ("full", gnw), ("full", snw), ("full", dvec)]
    return _rowwise("gate_norm", body, t, tm, ins, [(2 * d, BF16)])[0]


def _mid(x, mix, pmw, pfw, tm):
    t, d = x.shape

    def body(i, x_ref, mix_ref, pmw_ref, pfw_ref, x1_ref, h2_ref):
        mh, _ = _rms(mix_ref[...], d)
        x1 = x_ref[...] + mh * pmw_ref[...]
        x1_ref[...] = x1
        xh, _ = _rms(x1, d)
        h2_ref[...] = (xh * pfw_ref[...]).astype(BF16)

    return _rowwise("mid", body, t, tm, [("row", x, d, 0), ("row", mix, d, 0), ("full", pmw), ("full", pfw)],
                    [(d, F32), (d, BF16)])


def _ffn_gate_up(ug_ref, ugh_ref, uu_ref, uuh_ref, cw_ref, cb_ref, keep):
    tg = _conv_taps(ug_ref[...], ugh_ref[...] * keep, 3)
    tu = _conv_taps(uu_ref[...], uuh_ref[...] * keep, 3)
    gate = _conv(tg, cw_ref[:, 0:D_FF]) + cb_ref[:, 0:D_FF]
    up = _conv(tu, cw_ref[:, D_FF:2 * D_FF]) + cb_ref[:, D_FF:2 * D_FF]
    return tg, tu, gate, up


def _ffn_act(u_pre, cw, cb, seq, tm):
    t = u_pre.shape[0]

    def body(i, ug_ref, ugh_ref, uu_ref, uuh_ref, cw_ref, cb_ref, act_ref):
        keep, _ = _seq_flags(i, seq, tm)
        _, _, gate, up = _ffn_gate_up(ug_ref, ugh_ref, uu_ref, uuh_ref, cw_ref, cb_ref, keep)
        act_ref[...] = (gate * _sigmoid(gate) * up).astype(BF16)

    ins = [("row", u_pre, D_FF, 0), ("prev", u_pre, D_FF, 0), ("row", u_pre, D_FF, 1), ("prev", u_pre, D_FF, 1),
           ("full", cw), ("full", cb)]
    return _rowwise("ffn_act", body, t, tm, ins, [(D_FF, BF16)])[0]


def _final(x1, f, tgt, w, tm):
    t, d = x1.shape

    def body(i, x1_ref, f_ref, t_ref, w_ref, dy_ref, df_ref, loss_ref, dw_ref):
        fh, r = _rms(f_ref[...], d)
        e = x1_ref[...] + fh * w_ref[...] - t_ref[...]
        loss_ref[...] += _colsum(e * e) * (0.5 / d)
        dy = e * (1.0 / d)
        dy_ref[...] = dy
        dw_ref[...] += _colsum(dy * fh)
        df_ref[...] = _rms_bwd(fh, r, dy * w_ref[...], d).astype(BF16)

    return _rowwise("final", body, t, tm, [("row", x1, d, 0), ("row", f, d, 0), ("row", tgt, d, 0), ("full", w)],
                    [(d, F32), (d, BF16)], accs=[(1, d), (1, d)])


def _ffn_bwd(u_pre, dact, cw, cb, seq, tm):
    t = u_pre.shape[0]

    def body(i, ug_ref, ugh_ref, ugn_ref, uu_ref, uuh_ref, uun_ref, da_ref, dan_ref, cw_ref, cb_ref, dpre_ref, dcw_ref, dcb_ref):
        keep, keep_next = _seq_flags(i, seq, tm)
        ext = lambda a_ref, n_ref: jnp.concatenate([a_ref[...], n_ref[...]], axis=0)
        rows = tm + SUBLANES
        tg = _conv_taps(ext(ug_ref, ugn_ref), ugh_ref[...] * keep, 3)
        tu = _conv_taps(ext(uu_ref, uun_ref), uuh_ref[...] * keep, 3)
        gate = _conv(tg, cw_ref[:, 0:D_FF]) + cb_ref[:, 0:D_FF]
        up = _conv(tu, cw_ref[:, D_FF:2 * D_FF]) + cb_ref[:, D_FF:2 * D_FF]
        sg = _sigmoid(gate)
        da = jnp.concatenate([da_ref[...], dan_ref[...] * keep_next], axis=0)
        for off, grad, taps in ((0, da * up * _dsilu(gate, sg), tg), (D_FF, da * gate * sg, tu)):
            own = grad[0:tm]
            acc = own * cw_ref[2:3, off:off + D_FF]
            for j in (1, 2):
                acc = acc + pltpu.roll(grad, rows - j, 0)[0:tm] * cw_ref[2 - j:3 - j, off:off + D_FF]
            dpre_ref[:, off:off + D_FF] = acc.astype(BF16)
            dcb_ref[:, off:off + D_FF] += _colsum(own)
            for k in range(3):
                dcw_ref[k:k + 1, off:off + D_FF] += _colsum(own * taps[k][0:tm])

    ins = []
    for cb_ in range(2):
        ins += [("row", u_pre, D_FF, cb_), ("prev", u_pre, D_FF, cb_), ("next", u_pre, D_FF, cb_)]
    ins += [("row", dact, D_FF, 0), ("next", dact, D_FF, 0), ("full", cw), ("full", cb)]
    return _rowwise("ffn_bwd", body, t, tm, ins, [(2 * D_FF, BF16)], accs=[(SUBLANES, 2 * D_FF), (1, 2 * D_FF)])


def _conv_t(name, dy, cw, kw, seq, tm):
    t, width = dy.shape

    def body(i, d_ref, dn_ref, cw_ref, o_ref):
        _, keep = _seq_flags(i, seq, tm)
        d = d_ref[...]
        halo = dn_ref[...] * keep
        acc = d * cw_ref[kw - 1:kw, :]
        for j in range(1, kw):
            acc = acc + _shift_up(d, halo, j) * cw_ref[kw - 1 - j:kw - j, :]
        o_ref[...] = acc.astype(BF16)

    return _rowwise(name, body, t, tm, [("row", dy, width, 0), ("next", dy, width, 0), ("full", cw)], [(width, BF16)])[0]


def _mid_bwd(x1, mix, dy, dh2, pmw, pfw, tm):
    t, d = x1.shape

    def body(i, x1_ref, mix_ref, dy_ref, dh2_ref, pmw_ref, pfw_ref, dx1_ref, dmix_ref, dpm_ref, dpf_ref):
        xh, r2 = _rms(x1_ref[...], d)
        dh2 = dh2_ref[...]
        dpf_ref[...] += _colsum(dh2 * xh)
        dx1 = dy_ref[...] + _rms_bwd(xh, r2, dh2 * pfw_ref[...], d)
        dx1_ref[...] = dx1
        mh, r = _rms(mix_ref[...], d)
        dpm_ref[...] += _colsum(dx1 * mh)
        dmix_ref[...] = _rms_bwd(mh, r, dx1 * pmw_ref[...], d).astype(BF16)

    ins = [("row", x1, d, 0), ("row", mix, d, 0), ("row", dy, d, 0), ("row", dh2, d, 0), ("full", pmw), ("full", pfw)]
    return _rowwise("mid_bwd", body, t, tm, ins, [(d, F32), (d, BF16)], accs=[(1, d), (1, d)])


def _gate_norm_bwd(o_gdn, y_ssd, xs, proj, dmixin, gnw, snw, dvec, tm):
    t = o_gdn.shape[0]
    d = D_MODEL

    def body(i, o_ref, za_ref, y_ref, xs_ref, zs_ref, dma_ref, dms_ref, gnw_ref, snw_ref, dv_ref,
             do_ref, dza_ref, dy_ref, dxs_ref, dzs_ref, dgnw_ref, dsnw_ref, dd_ref):
        for hh in range(GDN_HEADS):
            sl = slice(hh * GDN_DK, (hh + 1) * GDN_DK)
            oh, r = _rms(o_ref[:, sl], GDN_DK)
            z = za_ref[:, sl]
            sz = _sigmoid(z)
            dm = dma_ref[:, sl]
            don = dm * (z * sz)
            dza_ref[:, sl] = (dm * oh * gnw_ref[...] * _dsilu(z, sz)).astype(BF16)
            dgnw_ref[...] += _colsum(don * oh)
            do_ref[:, sl] = _rms_bwd(oh, r, don * gnw_ref[...], GDN_DK)
        zs = zs_ref[...]
        sz = _sigmoid(zs)
        sil = zs * sz
        x = xs_ref[...]
        y0 = y_ref[...] + dv_ref[...] * x
        yg = y0 * sil
        dms = dms_ref[...]
        for g in range(SSD_GROUPS):
            sl = slice(g * 512, (g + 1) * 512)
            yh, r = _rms(yg[:, sl], 512)
            dsnw_ref[:, sl] += _colsum(dms[:, sl] * yh)
            dyg = _rms_bwd(yh, r, dms[:, sl] * snw_ref[:, sl], 512)
            dy0 = dyg * sil[:, sl]
            dzs_ref[:, sl] = (dyg * y0[:, sl] * _dsilu(zs[:, sl], sz[:, sl])).astype(BF16)
            dy_ref[:, sl] = dy0
            dxs_ref[:, sl] = dy0 * dv_ref[:, sl]
            dd_ref[:, sl] += _colsum(dy0 * x[:, sl])

    ins = [("row", o_gdn, d, 0), ("row", proj, d, 3), ("row", y_ssd, d, 0), ("row", xs, d, 0), ("row", proj, d, 4),
           ("row", dmixin, d, 0), ("row", dmixin, d, 1), ("full", gnw), ("full", snw), ("full", dvec)]
    return _rowwise("gate_norm_bwd", body, t, tm, ins, [(d, F32), (d, BF16), (d, F32), (d, F32), (d, BF16)],
                    accs=[(1, GDN_DK), (1, d), (1, d)])


def _ssd_chunk_bwd(xs, bc, dtx, acsx, acs, ar, dy, sts, bsz, seq, sb):
    nsb, ncb, _, sp = _ssd_specs(seq, sb)

    def body(x_ref, dtx_ref, ax_ref, bc_ref, acs_ref, ar_ref, dy_ref, sts_ref, dx_ref, dbc_ref, ddt_ref, dacs_ref, dst_scr):
        @pl.when(pl.program_id(1) == 0)
        def _():
            dst_scr[...] = jnp.zeros_like(dst_scr)

        c = _chunk_consts()
        lane5 = lax.broadcasted_iota(jnp.int32, (CHUNK, SSD_GW), 1) // SSD_P
        row5 = lax.broadcasted_iota(jnp.int32, (CHUNK, SSD_GW), 0)
        sel_in = lax.broadcasted_iota(jnp.int32, (SSD_GW, LANES), 0) // SSD_P
        sel_out = lax.broadcasted_iota(jnp.int32, (SSD_GW, LANES), 1)

        def chunk(nn, carry):
            n = ncb - 1 - nn
            r = pl.ds(pl.multiple_of(n * CHUNK, CHUNK), CHUNK)
            rs = pl.ds(pl.multiple_of(n * SSD_N, SSD_N), SSD_N)
            acsv = acs_ref[0, r, :]
            ddt = jnp.zeros((CHUNK, LANES), F32)
            dacs = jnp.zeros((CHUNK, LANES), F32)
            for g in range(SSD_GROUPS):
                gl = slice(g * SSD_GW, (g + 1) * SSD_GW)
                x, dt, ax, dyv = x_ref[0, r, gl], dtx_ref[0, r, gl], ax_ref[0, r, gl], dy_ref[0, r, gl]
                bm = bc_ref[0, r, g * SSD_N:(g + 1) * SSD_N]
                cm = bc_ref[0, r, (SSD_GROUPS + g) * SSD_N:(SSD_GROUPS + g + 1) * SSD_N]
                st = sts_ref[0, rs, gl]
                dst = dst_scr[:, gl]
                rsel = (sel_in + (16 + g * SSD_HPG) == sel_out).astype(F32)
                xdt = x * dt
                cbm = _mm(cm, bm, NT)
                al = ax[CHUNK - 1:CHUNK, :]
                ex, el = jnp.exp(ax), jnp.exp(al)
                dec = jnp.exp(al - ax)
                xd = xdt * dec
                dye = dyv * ex
                dxd = _mm(bm, dst)
                dxdt = dec * dxd
                dcm = _mm(dye, st, NT)
                dbm = _mm(xd, dst, NT)
                z = dye * _mm(cm, st) - dxd * xd
                zl = _colsum(dst * st) * el + _colsum(dxd * xd)
                z = z + jnp.where(row5 == CHUNK - 1, zl, 0.0)
                dcb = jnp.zeros((CHUNK, CHUNK), F32)
                for hh in range(SSD_HPG):
                    head = g * SSD_HPG + hh
                    lm, gm = _ssd_head(acsv, ar_ref, n, head, cbm, c)
                    dym = jnp.where(lane5 == hh, dyv, 0.0)
                    dxdt = dxdt + _mm(gm, dym, TN)
                    dg = _mm(dym, xdt, NT)
                    dcb = dcb + dg * lm
                    pm = dg * gm
                    dacs = dacs + jnp.where(c["lane"] == head + 16, _rowsum(pm) - _mmsel(pm, c["ones"], TN), 0.0)
                dbc_ref[0, r, (SSD_GROUPS + g) * SSD_N:(SSD_GROUPS + g + 1) * SSD_N] = dcm + _mm(dcb, bm)
                dbc_ref[0, r, g * SSD_N:(g + 1) * SSD_N] = dbm + _mm(dcb, cm, TN)
                dacs = dacs + _mmsel(z, rsel)
                ddt = ddt + _mmsel(dxdt * x, rsel)
                dx_ref[0, r, gl] = dxdt * dt
                dst_scr[:, gl] = dst * el + _mm(cm, dye, TN)
            ddt_ref[0, r, :] = ddt
            dacs_ref[0, r, :] = dacs
            return carry

        lax.fori_loop(0, ncb, chunk, 0)

    return _pcall(
        body, name="ssd_chunk_bwd", grid=(bsz, nsb),
        in_specs=[sp["wide"](), sp["wide"](), sp["wide"](), sp["bc"](), sp["small"](), sp["ar"], sp["wide"](), sp["st"]],
        out_specs=[sp["wide"](), sp["bc"](), sp["small"](), sp["small"]()],
        out_shape=[jax.ShapeDtypeStruct((bsz, seq, SSD_HEADS * SSD_P), F32),
                   jax.ShapeDtypeStruct((bsz, seq, 2 * SSD_GROUPS * SSD_N), F32),
                   jax.ShapeDtypeStruct((bsz, seq, LANES), F32), jax.ShapeDtypeStruct((bsz, seq, LANES), F32)],
        scratch_shapes=[pltpu.VMEM((SSD_N, SSD_HEADS * SSD_P), F32)],
        compiler_params=_params(("parallel", "arbitrary")),
    )(xs, dtx, acsx, bc, acs, ar, dy, sts)


def _ssd_prep_bwd(proj, dxs_c, dxs_d, dbc, ddt, dacs, dsm_gdn, cw, cb, sp, seq, tm):
    t = proj.shape[0]
    d = D_MODEL

    def body(i, x_ref, xh_ref, bc_ref, bch_ref, sm_ref, dxc_ref, dxd_ref, dbc_ref, ddt_ref, dacs_ref, dsg_ref,
             cw_ref, cb_ref, sp_ref, dpre_ref, dsm_ref, dcw_ref, dcb_ref, dsp_ref):
        keep, _ = _seq_flags(i, seq, tm)
        parts = ((x_ref, xh_ref, 0, d, (dxc_ref[...] + dxd_ref[...],)),
                 (bc_ref, bch_ref, d, 512, (dbc_ref[...],)))
        for xr, hr, off, w, grads in parts:
            taps = _conv_taps(xr[...], hr[...] * keep, 4)
            y = _conv(taps, cw_ref[:, off:off + w]) + cb_ref[:, off:off + w]
            ds_ = _dsilu(y, _sigmoid(y))
            o = 0
            for gr in grads:
                wg = gr.shape[1]
                dpre = gr * ds_[:, o:o + wg]
                dpre_ref[:, off + o:off + o + wg] = dpre
                dcb_ref[:, off + o:off + o + wg] += _colsum(dpre)
                for k in range(4):
                    dcw_ref[k:k + 1, off + o:off + o + wg] += _colsum(dpre * taps[k][:, o:o + wg])
                o += wg
        sm = sm_ref[...]
        lane = lax.broadcasted_iota(jnp.int32, sm.shape, 1)
        valid = (lane >= 16) & (lane < 32)
        xb = sm + sp_ref[1:2, :]
        dt = jnp.where(valid, _softplus(xb), 0.0)
        a_neg = -jnp.exp(sp_ref[0:1, :])
        dadt_s = _mmx(_block_tri(tm, True), dacs_ref[...])
        dxb = jnp.where(valid, (ddt_ref[...] + dadt_s * a_neg) * _sigmoid(xb), 0.0)
        dsm_ref[...] = (dsg_ref[...] + dxb).astype(BF16)
        dsp_ref[1:2, :] += _colsum(dxb)
        dsp_ref[0:1, :] += jnp.where(valid[0:1, :], _colsum(dadt_s * dt) * a_neg, 0.0)

    ins = [("row", proj, d, 5), ("prev", proj, d, 5), ("row", proj, 512, 12), ("prev", proj, 512, 12),
           ("row", proj, LANES, SMALL_CB), ("row", dxs_c, d, 0), ("row", dxs_d, d, 0), ("row", dbc, 512, 0),
           ("row", ddt, LANES, 0), ("row", dacs, LANES, 0), ("row", dsm_gdn, LANES, 0),
           ("full", cw), ("full", cb), ("full", sp)]
    return _rowwise("ssd_prep_bwd", body, t, tm, ins, [(d + 512, F32), (LANES, BF16)],
                    accs=[(SUBLANES, d + 512), (1, d + 512), (SUBLANES, LANES)])


def _gdn_chunk_bwd(qn, kn, vv, gs, gr, do, sts, tis, bsz, seq, sb, riders):
    hb = GDN_HB_BWD
    nsb, ncb, sp = _gdn_specs(seq, sb, hb, True)
    grid = (bsz, GDN_HEADS // hb, nsb)
    any_spec, rider_shapes, rider_sems, wrap = _riding_exchange(riders, True, 8, 4, grid)

    def body(q_ref, k_ref, v_ref, gs_ref, gr_ref, do_ref, st_ref, ti_ref, dq_ref, dk_ref, dv_ref, dgb_ref, ds_scr):
        hg = pl.program_id(1)

        @pl.when(pl.program_id(2) == 0)
        def _():
            ds_scr[...] = jnp.zeros_like(ds_scr)

        c = _chunk_consts()

        def chunk(nn, carry):
            n = ncb - 1 - nn
            r = pl.ds(pl.multiple_of(n * CHUNK, CHUNK), CHUNK)
            rs = pl.ds(pl.multiple_of(n * GDN_DK, GDN_DK), GDN_DK)
            gsv = gs_ref[0, r, :]
            heads = list(range(hb))
            sls = [slice(ih * GDN_DK, (ih + 1) * GDN_DK) for ih in heads]
            q = [q_ref[0, r, sl] for sl in sls]
            k = [k_ref[0, r, sl] for sl in sls]
            v = [v_ref[0, r, sl] for sl in sls]
            do_ = [do_ref[0, r, sl] for sl in sls]
            s = [st_ref[0, ih, rs, :] for ih in heads]
            tinv = [ti_ref[0, ih, r, :] for ih in heads]
            dsn = [ds_scr[ih] for ih in heads]
            beta, dc, eg, egl, ekd = zip(*[
                _gdn_gates(gsv, gr_ref[0, ih, pl.ds(n, 1), :], hg * hb + ih, c) for ih in heads])
            mul = lambda a, b: a * b
            kb = _hmap(mul, k, beta)
            rhs_w = _hmap(mul, kb, eg)
            u = _hmap(lambda t_, a, b: _mm3(t_, a * b), tinv, v, beta)
            w = _hmap(_mm3, tinv, rhs_w)
            amat = _hmap(lambda a, b, d_: jnp.where(c["strict"], _mm(a, b, NT) * d_, 0.0), kb, k, dc)
            qk = _hmap(lambda a, b, d_: _mm(a, b, NT) * d_, q, k, dc)
            qd = _hmap(mul, q, eg)
            kd = _hmap(mul, k, ekd)
            v_new = _hmap(lambda a, b, s_: a - _mm(b, s_), u, w, s)
            dv_new = _hmap(lambda qk_, d_, kd_, dn: _mm(qk_, d_, TN) + _mm(kd_, dn), qk, do_, kd, dsn)
            dqk = _hmap(lambda d_, vn: _mm(d_, vn, NT), do_, v_new)
            dqd = _hmap(lambda d_, s_: _mm(d_, s_, NT), do_, s)
            ds_new = _hmap(lambda qd_, d_, dn, e, w_, dvn: _mm(qd_, d_, TN) + dn * e - _mm(w_, dvn, TN),
                           qd, do_, dsn, egl, w, dv_new)
            dkd = _hmap(lambda vn, dn: _mm(vn, dn, NT), v_new, dsn)
            dgl = _hmap(lambda s_, dn, e: _colsum(_rowsum(s_ * dn)) * e, s, dsn, egl)
            dw = _hmap(lambda dvn, s_: -_mm(dvn, s_, NT), dv_new, s)
            dru = _hmap(lambda t_, a: _mm3(t_, a, TN), tinv, dv_new)
            drw = _hmap(lambda t_, a: _mm3(t_, a, TN), tinv, dw)
            da = _hmap(lambda a, u_, b, w_: jnp.where(c["strict"], -(_mm(a, u_, NT) + _mm(b, w_, NT)), 0.0), dru, u, drw, w)
            m = _hmap(mul, da, dc)
            dkb = _hmap(lambda a, e, m_, k_: a * e + _mm(m_, k_), drw, eg, m, k)
            mq = _hmap(mul, dqk, dc)
            dq = _hmap(lambda mq_, k_, a, e: _mm(mq_, k_) + a * e, mq, k, dqd, eg)
            dk = _hmap(lambda m_, kb_, mq_, q_, a, e, b, be: _mm(m_, kb_, TN) + _mm(mq_, q_, TN) + a * e + b * be,
                       m, kb, mq, q, dkd, ekd, dkb, beta)
            dbeta = _hmap(lambda a, v_, b, k_: _rowsum(a * v_) + _rowsum(b * k_), dru, v, dkb, k)
            pq = _hmap(lambda a, am, b, qk_: a * am + b * qk_, da, amat, dqk, qk)
            ekk = _hmap(lambda a, b: _rowsum(a * b), dkd, kd)
            dgc = _hmap(lambda pq_, a, rw, b, qd_, e, gl_: (
                _rowsum(pq_) - _mmsel(pq_, c["ones"], TN) + (_rowsum(a * rw) + _rowsum(b * qd_) - e)
                + jnp.where(c["row1"] == CHUNK - 1, _colsum(e) + gl_, 0.0)), pq, drw, rhs_w, dqd, qd, ekk, dgl)
            for ih in heads:
                ds_scr[ih] = ds_new[ih]
                dv_ref[0, r, sls[ih]] = dru[ih] * beta[ih]
                dq_ref[0, r, sls[ih]] = dq[ih]
                dk_ref[0, r, sls[ih]] = dk[ih]
                dgb_ref[0, r, sls[ih]] = jnp.where(c["lane"] == 0, dbeta[ih], jnp.where(c["lane"] == 1, dgc[ih], 0.0))
            return carry

        lax.fori_loop(0, ncb, chunk, 0)

    res = _pcall(
        wrap(body), name="gdn_chunk_bwd", grid=grid,
        in_specs=[sp["wide"](), sp["wide"](), sp["wide"](), sp["gs"], sp["gr"], sp["wide"](), sp["st"], sp["ti"]] + any_spec,
        out_specs=[sp["wide"](), sp["wide"](), sp["wide"](), sp["wide"]()] + any_spec,
        out_shape=[jax.ShapeDtypeStruct((bsz, seq, D_MODEL), F32)] * 4 + rider_shapes,
        scratch_shapes=[pltpu.VMEM((hb, GDN_DK, GDN_DK), F32)] + rider_sems,
        compiler_params=_params(("arbitrary", "arbitrary", "arbitrary")),
    )(qn, kn, vv, gs, gr, do, sts, tis, *riders)
    return res[:4], res[4:]


def _gdn_prep_bwd(proj, dqn, dkn, dvv, dgb, cw, gp, seq, tm):
    t = proj.shape[0]
    d = D_MODEL

    def body(i, q_ref, qh_ref, k_ref, kh_ref, v_ref, vh_ref, sm_ref, dq_ref, dk_ref, dv_ref, dgb_ref, cw_ref, gp_ref,
             dpre_ref, dsm_ref, dcw_ref, dgp_ref):
        keep, _ = _seq_flags(i, seq, tm)
        for x_ref, h_ref, g_ref, off, scale in ((q_ref, qh_ref, dq_ref, 0, GDN_DK ** -0.5),
                                               (k_ref, kh_ref, dk_ref, d, 1.0), (v_ref, vh_ref, dv_ref, 2 * d, None)):
            taps = _conv_taps(x_ref[...], h_ref[...] * keep, 4)
            y = _conv(taps, cw_ref[:, off:off + d])
            sy = _sigmoid(y)
            ds_ = _dsilu(y, sy)
            if scale is None:
                dpre = g_ref[...] * ds_
                dpre_ref[:, off:off + d] = dpre
                for k in range(4):
                    dcw_ref[k:k + 1, off:off + d] += _colsum(dpre * taps[k])
            else:
                a = y * sy
                for hh in range(GDN_HEADS):
                    sl = slice(hh * GDN_DK, (hh + 1) * GDN_DK)
                    s = a[:, sl]
                    n = lax.rsqrt(_rowsum(s * s) + EPS)
                    ah = s * n
                    gq = g_ref[:, sl]
                    dpre = (scale * n) * (gq - ah * _rowsum(gq * ah)) * ds_[:, sl]
                    dpre_ref[:, off + hh * GDN_DK:off + (hh + 1) * GDN_DK] = dpre
                    for k in range(4):
                        dcw_ref[k:k + 1, off + hh * GDN_DK:off + (hh + 1) * GDN_DK] += _colsum(dpre * taps[k][:, sl])
        sm = sm_ref[...]
        lane = lax.broadcasted_iota(jnp.int32, sm.shape, 1)
        si = lax.broadcasted_iota(jnp.int32, (d, LANES), 0)
        so = lax.broadcasted_iota(jnp.int32, (d, LANES), 1)
        sel = (((si % GDN_DK == 0) & (so == si // GDN_DK)) | ((si % GDN_DK == 1) & (so == si // GDN_DK + 8))).astype(F32)
        dsel = _mmx(dgb_ref[...], sel)
        is_g = (lane >= 8) & (lane < 16)
        dsel = jnp.where(is_g, _mmx(_block_tri(tm, True), dsel), dsel)
        beta = _sigmoid(sm)
        xb = sm + gp_ref[1:2, :]
        a_neg = -jnp.exp(gp_ref[0:1, :])
        sp = _softplus(xb)
        dxb = jnp.where(is_g, dsel * a_neg * _sigmoid(xb), 0.0)
        dsm_ref[...] = jnp.where(lane < 8, dsel * beta * (1.0 - beta), dxb)
        dgp_ref[1:2, :] += _colsum(dxb)
        dgp_ref[0:1, :] += _colsum(jnp.where(is_g, dsel * a_neg * sp, 0.0))

    ins = []
    for cb in range(3):
        ins += [("row", proj, d, cb), ("prev", proj, d, cb)]
    ins += [("row", proj, LANES, SMALL_CB), ("row", dqn, d, 0), ("row", dkn, d, 0), ("row", dvv, d, 0), ("row", dgb, d, 0),
            ("full", cw), ("full", gp)]
    return _rowwise("gdn_prep_bwd", body, t, tm, ins, [(3 * d, F32), (LANES, F32)],
                    accs=[(SUBLANES, 3 * d), (SUBLANES, LANES)])


def _first_bwd(x, dh1, dx1, w, tm):
    t, d = x.shape

    def body(i, x_ref, dh_ref, dx1_ref, w_ref, dx_ref, dw_ref):
        xh, r = _rms(x_ref[...], d)
        dh = dh_ref[...]
        dw_ref[...] += _colsum(dh * xh)
        dx_ref[...] = dx1_ref[...] + _rms_bwd(xh, r, dh * w_ref[...], d)

    return _rowwise("first_bwd", body, t, tm, [("row", x, d, 0), ("row", dh1, d, 0), ("row", dx1, d, 0), ("full", w)],
                    [(d, F32)], accs=[(1, d)])


def _exchange(name, arrays, scatter):
    n = len(arrays)

    def body(*refs):
        ins, outs, sems = refs[:n], refs[n:2 * n], refs[2 * n:]
        _exchange_phase(ins, outs, sems, scatter, start=True)
        _exchange_phase(ins, outs, sems, scatter, start=False)

    return _pcall(
        body, name=name,
        in_specs=[pl.BlockSpec(memory_space=pl.ANY)] * n,
        out_specs=[pl.BlockSpec(memory_space=pl.ANY)] * n,
        out_shape=_exchange_out_shapes(arrays, scatter),
        scratch_shapes=_exchange_sems(n),
    )(*arrays)


def _exchange_out_shapes(arrays, scatter):
    return [jax.ShapeDtypeStruct(a.shape if scatter else (N_DEV,) + a.shape, a.dtype) for a in arrays]


def _exchange_sems(n):
    return [pltpu.SemaphoreType.DMA((n, N_DEV - 1)), pltpu.SemaphoreType.DMA((n, N_DEV - 1)), pltpu.SemaphoreType.DMA((n,))]


def _exchange_phase(ins, outs, sems, scatter, start):
    send_sems, recv_sems, loc_sems = sems
    x, y, c = lax.axis_index("x"), lax.axis_index("y"), lax.axis_index("c")
    me = 4 * x + 2 * y + c
    for t in range(len(ins)):
        loc = pltpu.make_async_copy(ins[t].at[me] if scatter else ins[t], outs[t].at[me], loc_sems.at[t])
        if start:
            loc.start()
        else:
            loc.wait()
        for k in range(N_DEV - 1):
            bx, by, bc = ((k + 1) >> 2) & 1, ((k + 1) >> 1) & 1, (k + 1) & 1
            px = 1 - x if bx else x
            py = 1 - y if by else y
            pc = 1 - c if bc else c
            peer = 4 * px + 2 * py + pc
            src = ins[t].at[peer] if scatter else ins[t]
            copy = lambda dst: pltpu.make_async_remote_copy(
                src_ref=src, dst_ref=dst, send_sem=send_sems.at[t, k], recv_sem=recv_sems.at[t, k],
                device_id=(px, py, pc), device_id_type=pl.DeviceIdType.MESH)
            if start:
                copy(outs[t].at[me]).start()
            else:
                copy(outs[t].at[me]).wait_send()
                copy(outs[t].at[peer]).wait_recv()


def _adam_math(w, g, m, v):
    m = ADAM_B1 * m + (1.0 - ADAM_B1) * g
    v = ADAM_B2 * v + (1.0 - ADAM_B2) * (g * g)
    m_hat = m / (1.0 - ADAM_B1 ** ADAM_STEP)
    v_hat = v / (1.0 - ADAM_B2 ** ADAM_STEP)
    delta = -ADAM_LR * (m_hat / (jnp.sqrt(v_hat) + ADAM_EPS) + ADAM_WD * w)
    return delta, m, v


def _adam_big(name, parts, w, m, v, tm):
    r, c = w.shape
    tm = tm if r % tm == 0 else r

    def body(p_ref, w_ref, m_ref, v_ref, g_ref, d_ref, nm_ref, nv_ref):
        g = p_ref[0].astype(F32)
        for s in range(1, N_DEV):
            g = g + p_ref[s].astype(F32)
        g_ref[...] = g
        d_ref[...], nm_ref[...], nv_ref[...] = _adam_math(w_ref[...], g, m_ref[...], v_ref[...])

    blk = lambda: pl.BlockSpec((tm, c), lambda i: (i, 0))
    return _pcall(
        body, name=name, grid=(r // tm,),
        in_specs=[pl.BlockSpec((N_DEV, tm, c), lambda i: (0, i, 0)), blk(), blk(), blk()],
        out_specs=[blk(), blk(), blk(), blk()],
        out_shape=[jax.ShapeDtypeStruct((r, c), F32)] * 4,
        compiler_params=_params(("parallel",)),
    )(parts, w, m, v)


SMALL_ROWS = 56
ROW_DD, ROW_LOSS = 5, 6


def _small_sum(gathered):
    def body(g_ref, o_ref, x_ref):
        s = g_ref[0]
        for dev in range(1, N_DEV):
            s = s + g_ref[dev]
        o_ref[...] = s
        ri = lax.broadcasted_iota(jnp.int32, (D_MODEL, LANES), 0)
        ro = lax.broadcasted_iota(jnp.int32, (D_MODEL, LANES), 1)
        heads = _mmx(jnp.broadcast_to(s[ROW_DD:ROW_DD + 1, :], (SUBLANES, D_MODEL)), (ri // SSD_P == ro).astype(F32))
        loss = _rowsum(jnp.broadcast_to(s[ROW_LOSS:ROW_LOSS + 1, :], (SUBLANES, D_MODEL)))
        row = lax.broadcasted_iota(jnp.int32, (SUBLANES, LANES), 0)
        x_ref[...] = jnp.where(row == 0, heads, jnp.broadcast_to(loss, (SUBLANES, LANES)))

    return _pcall(
        body, name="small_sum",
        out_shape=[jax.ShapeDtypeStruct((SMALL_ROWS, D_MODEL), F32), jax.ShapeDtypeStruct((SUBLANES, LANES), F32)],
        compiler_params=_params(None),
    )(gathered)


def _adam_small(g, w, m, v):
    def body(g_ref, w_ref, m_ref, v_ref, d_ref, nm_ref, nv_ref):
        d_ref[...], nm_ref[...], nv_ref[...] = _adam_math(w_ref[...], g_ref[...], m_ref[...], v_ref[...])

    return _pcall(body, name="adam_small", out_shape=[jax.ShapeDtypeStruct(g.shape, F32)] * 3,
                  compiler_params=_params(None))(g, w, m, v)


def _pack(pieces, rows):
    flat = jnp.concatenate([p.reshape(-1).astype(F32) for p in pieces])
    return jnp.pad(flat, (0, rows * D_MODEL - flat.shape[0])).reshape(rows, D_MODEL)


def _unpack(packed, shapes):
    flat = packed.reshape(-1)
    out, off = [], 0
    for shp in shapes:
        size = 1
        for s in shp:
            size *= s
        out.append(flat[off:off + size].reshape(shp))
        off += size
    return out


def _permute_in(w):
    pad = jnp.zeros((w.shape[0], PROJ_W - D_IN), w.dtype)
    return jnp.concatenate([w[:, 0:4096], w[:, 4112:6672], w[:, 4096:4112], w[:, 6672:6688], pad], axis=1)


def _unpermute_in(g):
    return jnp.concatenate([g[:, 0:4096], g[:, 6656:6672], g[:, 4096:6656], g[:, 6672:6688]], axis=1)


def _lane_row(vec, start):
    return jnp.zeros((LANES,), F32).at[start:start + vec.shape[0]].set(vec)


def _cols_from_shards(g):
    return jnp.transpose(g, (1, 0, 2)).reshape(g.shape[1], N_DEV * g.shape[2])


def _cols_to_shards(a):
    return jnp.transpose(a.astype(BF16).reshape(a.shape[0], N_DEV, a.shape[1] // N_DEV), (1, 0, 2))


def _rows_to_shards(a):
    return a.astype(BF16).reshape(N_DEV, a.shape[0] // N_DEV, a.shape[1])


def _local_step(x, tgt, wp_in, rest, p, rest_is_sharded):
    bsz, seq, d = x.shape
    t = bsz * seq
    x2 = x.reshape(t, d)
    tgt2 = tgt.reshape(t, d)
    tm = min(256, seq)
    tm_wide = min(128, seq)
    sb = min(512, seq)

    gp = jnp.zeros((SUBLANES, LANES), F32).at[0].set(_lane_row(p["gdn_a_log"], 8)).at[1].set(_lane_row(p["gdn_dt_bias"], 8))
    sp = jnp.zeros((SUBLANES, LANES), F32).at[0].set(_lane_row(p["ssd_a_log"], 16)).at[1].set(_lane_row(p["ssd_dt_bias"], 16))
    dvec = jnp.repeat(p["ssd_d"], SSD_P).reshape(1, d)
    row = lambda v: v.reshape(1, -1)
    pre_mix, post_mix, pre_ffn, post_ffn = (row(p[k]) for k in ("pre_mix_norm", "post_mix_norm", "pre_ffn_norm", "post_ffn_norm"))
    gnw, snw = row(p["gdn_norm_w"]), row(p["ssd_norm_w"])
    gcw, scw, scb, fcw, fcb = p["gdn_conv_w"], p["ssd_conv_w"], row(p["ssd_conv_b"]), p["ffn_conv_w"], row(p["ffn_conv_b"])

    h1 = _norm_cast("norm_in", x2, pre_mix, tm)
    proj = _matmul("mm_proj", h1, wp_in, "nn", F32)
    b3 = lambda a: a.reshape(bsz, seq, a.shape[-1])
    b2 = lambda a: a.reshape(t, a.shape[-1])
    rows_of = lambda a, lo, n: jnp.transpose(a[:, lo:lo + n].reshape(bsz, seq // CHUNK, CHUNK, n), (0, 3, 1, 2))
    qn, kn, vv, gs = (b3(a) for a in _gdn_prep(proj, gcw, gp, seq, tm))
    gr = rows_of(b2(gs), 8, GDN_HEADS)
    (o_gdn, gdn_st, gdn_ti), gathered = _gdn_chunk_fwd(qn, kn, vv, gs, gr, bsz, seq, sb, list(rest) if rest_is_sharded else [])
    if rest_is_sharded:
        w_out, w_up, w_down = gathered[0].reshape(-1, d), _cols_from_shards(gathered[1]), gathered[2].reshape(-1, d)
    else:
        w_out, w_up, w_down = rest
    o_gdn = b2(o_gdn)
    xs, bc, dtx, acsx, acs = _ssd_prep(proj, scw, scb, sp, seq, tm)
    ar = rows_of(acs, 16, SSD_HEADS)
    y_ssd, ssd_st = _ssd_chunk_fwd(b3(xs), b3(bc), b3(dtx), b3(acsx), b3(acs), ar, bsz, seq, sb)
    y_ssd = b2(y_ssd)
    mixin = _gate_norm(o_gdn, y_ssd, xs, proj, gnw, snw, dvec, tm)
    mix = _matmul("mm_out", mixin, w_out, "nn", F32)
    x1, h2 = _mid(x2, mix, post_mix, pre_ffn, tm)
    u_pre = _matmul("mm_up", h2, w_up, "nn", F32)
    act = _ffn_act(u_pre, fcw, fcb, seq, tm_wide)
    f = _matmul("mm_down", act, w_down, "nn", F32, tk=1408)
    dy, df, loss_lanes, d_post_ffn = _final(x1, f, tgt2, post_ffn, tm)

    g_down = _matmul("mm_dw_down", act, df, "tn", F32, tm=1408)
    dact = _matmul("mm_dact", df, w_down, "nt", F32, tn=1408)
    du_pre, d_fcw, d_fcb = _ffn_bwd(u_pre, dact, fcw, fcb, seq, tm_wide)
    g_up = _matmul("mm_dw_up", h2, du_pre, "tn", F32)
    dh2 = _matmul("mm_dh2", du_pre, w_up, "nt", F32)
    dx1, dmix, d_post_mix, d_pre_ffn = _mid_bwd(x1, mix, dy, dh2, post_mix, pre_ffn, tm)
    g_out = _matmul("mm_dw_out", mixin, dmix, "tn", F32)
    dmixin = _matmul("mm_dmixin", dmix, w_out, "nt", F32)
    do_gdn, dza, dy_ssd, dxs_d, dzs, d_gnw, d_snw, d_dd = _gate_norm_bwd(o_gdn, y_ssd, xs, proj, dmixin, gnw, snw, dvec, tm)
    dxs_c, dbc, ddt, dacs = (b2(a) for a in _ssd_chunk_bwd(
        b3(xs), b3(bc), b3(dtx), b3(acsx), b3(acs), ar, b3(dy_ssd), ssd_st, bsz, seq, sb))
    riders = [_rows_to_shards(g_out), _cols_to_shards(g_up), _rows_to_shards(g_down)] if rest_is_sharded else []
    dgdn, received = _gdn_chunk_bwd(qn, kn, vv, gs, gr, b3(do_gdn), gdn_st, gdn_ti, bsz, seq, sb, riders)
    if rest_is_sharded:
        g_out, g_up, g_down = received
    dqn, dkn, dvv, dgb = (b2(a) for a in dgdn)
    dpre_gdn, dsm_gdn, d_gcw, d_gp = _gdn_prep_bwd(proj, dqn, dkn, dvv, dgb, gcw, gp, seq, tm)
    dpre_ssd, dsm, d_scw, d_scb, d_sp = _ssd_prep_bwd(proj, dxs_c, dxs_d, dbc, ddt, dacs, dsm_gdn, scw, scb, sp, seq, tm)
    dqkv = _conv_t("gdn_conv_t", dpre_gdn, gcw, 4, seq, tm)
    dxbc = _conv_t("ssd_conv_t", dpre_ssd, scw, 4, seq, tm)
    dproj = jnp.concatenate([dqkv, dza, dzs, dxbc, dsm, jnp.zeros((t, PROJ_W - 6784), BF16)], axis=1)
    g_in = _matmul("mm_dw_in", h1, dproj, "tn", F32)
    if rest_is_sharded:
        dh1, (g_in,) = _matmul("mm_dh1", dproj, wp_in, "nt", F32, scatter_riders=[_cols_to_shards(_unpermute_in(g_in))])
    else:
        dh1 = _matmul("mm_dh1", dproj, wp_in, "nt", F32)
    dx, d_pre_mix = _first_bwd(x2, dh1, dx1, pre_mix, tm)

    small = dict(pre_mix_norm=d_pre_mix, ssd_norm_w=d_snw, post_mix_norm=d_post_mix, pre_ffn_norm=d_pre_ffn,
                 post_ffn_norm=d_post_ffn, dd_lanes=d_dd, loss_lanes=loss_lanes, gdn_gates=d_gp, ssd_gates=d_sp,
                 gdn_norm_w=d_gnw, gdn_conv_w=d_gcw[0:4], ssd_conv_w=d_scw[0:4], ssd_conv_b=d_scb,
                 ffn_conv_w=d_fcw[0:3], ffn_conv_b=d_fcb)
    return dx.reshape(bsz, seq, d), g_in, g_out, g_up, g_down, small


def kernel(x, pre_mix_norm, w_in, gdn_conv_w, gdn_a_log, gdn_dt_bias, gdn_norm_w, ssd_conv_w, ssd_conv_b, ssd_a_log, ssd_dt_bias, ssd_d, ssd_norm_w, w_out, post_mix_norm, pre_ffn_norm, w_up, ffn_conv_w, ffn_conv_b, w_down, post_ffn_norm, loss_target, m_pre_mix_norm, m_w_in, m_gdn_conv_w, m_gdn_a_log, m_gdn_dt_bias, m_gdn_norm_w, m_ssd_conv_w, m_ssd_conv_b, m_ssd_a_log, m_ssd_dt_bias, m_ssd_d, m_ssd_norm_w, m_w_out, m_post_mix_norm, m_pre_ffn_norm, m_w_up, m_ffn_conv_w, m_ffn_conv_b, m_w_down, m_post_ffn_norm, v_pre_mix_norm, v_w_in, v_gdn_conv_w, v_gdn_a_log, v_gdn_dt_bias, v_gdn_norm_w, v_ssd_conv_w, v_ssd_conv_b, v_ssd_a_log, v_ssd_dt_bias, v_ssd_d, v_ssd_norm_w, v_w_out, v_post_mix_norm, v_pre_ffn_norm, v_w_up, v_ffn_conv_w, v_ffn_conv_b, v_w_down, v_post_ffn_norm):
    names = ["pre_mix_norm", "w_in", "gdn_conv_w", "gdn_a_log", "gdn_dt_bias", "gdn_norm_w", "ssd_conv_w", "ssd_conv_b",
             "ssd_a_log", "ssd_dt_bias", "ssd_d", "ssd_norm_w", "w_out", "post_mix_norm", "pre_ffn_norm", "w_up",
             "ffn_conv_w", "ffn_conv_b", "w_down", "post_ffn_norm"]
    w_args = [pre_mix_norm, w_in, gdn_conv_w, gdn_a_log, gdn_dt_bias, gdn_norm_w, ssd_conv_w, ssd_conv_b, ssd_a_log, ssd_dt_bias, ssd_d, ssd_norm_w, w_out, post_mix_norm, pre_ffn_norm, w_up, ffn_conv_w, ffn_conv_b, w_down, post_ffn_norm]
    m_args = [m_pre_mix_norm, m_w_in, m_gdn_conv_w, m_gdn_a_log, m_gdn_dt_bias, m_gdn_norm_w, m_ssd_conv_w, m_ssd_conv_b, m_ssd_a_log, m_ssd_dt_bias, m_ssd_d, m_ssd_norm_w, m_w_out, m_post_mix_norm, m_pre_ffn_norm, m_w_up, m_ffn_conv_w, m_ffn_conv_b, m_w_down, m_post_ffn_norm]
    v_args = [v_pre_mix_norm, v_w_in, v_gdn_conv_w, v_gdn_a_log, v_gdn_dt_bias, v_gdn_norm_w, v_ssd_conv_w, v_ssd_conv_b, v_ssd_a_log, v_ssd_dt_bias, v_ssd_d, v_ssd_norm_w, v_w_out, v_post_mix_norm, v_pre_ffn_norm, v_w_up, v_ffn_conv_w, v_ffn_conv_b, v_w_down, v_post_ffn_norm]
    w = {k: a[0] for k, a in zip(names, w_args)}
    m = {k: a[0] for k, a in zip(names, m_args)}
    v = {k: a[0] for k, a in zip(names, v_args)}
    idx = 4 * lax.axis_index("x") + 2 * lax.axis_index("y") + lax.axis_index("c")
    big = ("w_in", "w_out", "w_up", "w_down")
    conv = ("gdn_conv_w", "ssd_conv_w", "ffn_conv_w")

    conv_local = jnp.concatenate([jnp.pad(w[k], ((0, 4 - w[k].shape[0]), (0, 0))) for k in conv], axis=1)
    g_in, g_conv = _exchange("gather_weights", [w["w_in"].astype(BF16), conv_local], scatter=False)
    wp_in = _permute_in(_cols_from_shards(g_in))
    p = {k: w[k] for k in names if k not in big and k not in conv}
    off = 0
    for k in conv:
        cw = w[k].shape[1]
        p[k] = jnp.transpose(g_conv[:, :w[k].shape[0], off:off + cw], (1, 0, 2)).reshape(w[k].shape[0], N_DEV * cw)
        off += cw

    rest = tuple(w[k].astype(BF16) for k in ("w_out", "w_up", "w_down"))
    dx, p_in, p_out, p_up, p_down, small = _local_step(x, loss_target, wp_in, rest, p, True)

    gate_row = jnp.concatenate([small["gdn_gates"][0], small["gdn_gates"][1], small["ssd_gates"][0], small["ssd_gates"][1],
                                small["gdn_norm_w"][0], jnp.zeros((D_MODEL - 5 * LANES,), F32)]).reshape(1, D_MODEL)
    pack = _pack([small["pre_mix_norm"], small["ssd_norm_w"], small["post_mix_norm"], small["pre_ffn_norm"],
                  small["post_ffn_norm"], small["dd_lanes"], small["loss_lanes"], gate_row,
                  small["gdn_conv_w"], small["ssd_conv_w"], jnp.pad(small["ssd_conv_b"], ((0, 0), (0, 512))),
                  jnp.pad(small["ffn_conv_w"].reshape(-1), (0, 17 * D_MODEL - 3 * 2 * D_FF)),
                  jnp.pad(small["ffn_conv_b"], ((0, 0), (0, 512)))], SMALL_ROWS)
    (pack_all,) = _exchange("gather_small", [pack], scatter=False)
    ssum, extra = _small_sum(pack_all)

    grads, deltas, new_m, new_v = {}, {}, {}, {}
    for k, parts in (("w_in", p_in), ("w_out", p_out), ("w_up", p_up), ("w_down", p_down)):
        grads[k], deltas[k], new_m[k], new_v[k] = _adam_big("adam_" + k, parts, w[k], m[k], v[k], 256)

    flat = ssum.reshape(-1)
    gate = ssum[7]
    sg = dict(pre_mix_norm=ssum[0], ssd_norm_w=ssum[1], post_mix_norm=ssum[2], pre_ffn_norm=ssum[3], post_ffn_norm=ssum[4],
              gdn_a_log=gate[8:16], gdn_dt_bias=gate[LANES + 8:LANES + 16], ssd_a_log=gate[2 * LANES + 16:2 * LANES + 32],
              ssd_dt_bias=gate[3 * LANES + 16:3 * LANES + 32], gdn_norm_w=gate[4 * LANES:5 * LANES], ssd_d=extra[0, 0:SSD_HEADS])
    o = 8 * D_MODEL
    full_gcw = flat[o:o + 4 * 3072].reshape(4, 3072)
    o += 12 * D_MODEL
    full_scw = flat[o:o + 4 * 1536].reshape(4, 1536)
    o += 6 * D_MODEL
    sg["ssd_conv_b"] = flat[o:o + 1536]
    o += 2 * D_MODEL
    full_fcw = flat[o:o + 3 * 2 * D_FF].reshape(3, 2 * D_FF)
    o += 17 * D_MODEL
    sg["ffn_conv_b"] = flat[o:o + 2 * D_FF]
    for k, full in (("gdn_conv_w", full_gcw), ("ssd_conv_w", full_scw), ("ffn_conv_w", full_fcw)):
        cw = w[k].shape[1]
        sg[k] = lax.dynamic_slice_in_dim(full, idx * cw, cw, axis=1)
    small_names = [k for k in names if k not in big]
    rows = 24
    gpk = _pack([sg[k] for k in small_names], rows)
    dpk, mpk, vpk = _adam_small(gpk, _pack([w[k] for k in small_names], rows), _pack([m[k] for k in small_names], rows),
                                _pack([v[k] for k in small_names], rows))
    shapes = [w[k].shape for k in small_names]
    for k, g_, d_, m_, v_ in zip(small_names, _unpack(gpk, shapes), _unpack(dpk, shapes), _unpack(mpk, shapes), _unpack(vpk, shapes)):
        grads[k], deltas[k], new_m[k], new_v[k] = g_, d_, m_, v_

    loss = extra[1, 0]
    lead = lambda a: a[None]
    return (loss, dx, *[lead(grads[k]) for k in names], *[lead(deltas[k]) for k in names],
            *[lead(new_m[k]) for k in names], *[lead(new_v[k]) for k in names])
```

```python
import functools

import jax
import jax.numpy as jnp
from jax import lax
from jax.experimental import pallas as pl
from jax.experimental.pallas import tpu as pltpu

F32 = jnp.float32
BF16 = jnp.bfloat16
MXU_DTYPE = jnp.bfloat16
HIGHEST = lax.Precision.HIGHEST
VMEM_LIMIT_V7X = 48 * 1024 * 1024
SUBLANES = 8
LANES = 128

D_MODEL = 1024
GDN_HEADS = 8
GDN_DK = 128
SSD_HEADS = 16
SSD_P = 64
SSD_GROUPS = 2
SSD_HPG = 8
SSD_N = 128
CHUNK = 64
D_FF = 2816
EPS = 1e-6
N_DEV = 8
PROJ_W = 7168
SMALL_CB = 52
D_IN = 6688

ADAM_LR = 0.001
ADAM_B1 = 0.9
ADAM_B2 = 0.999
ADAM_EPS = 1e-08
ADAM_WD = 0.01
ADAM_STEP = 10

NN = (((1,), (0,)), ((), ()))
NT = (((1,), (1,)), ((), ()))
TN = (((0,), (0,)), ((), ()))


def _pcall(body, **kw):
    return pl.pallas_call(body, **kw)


def _mm(a, b, dims=NN):
    return lax.dot_general(a.astype(MXU_DTYPE), b.astype(MXU_DTYPE), dims, preferred_element_type=F32)


def _mmx(a, b, dims=NN):
    return lax.dot_general(a, b, dims, precision=HIGHEST, preferred_element_type=F32)


def _split(a):
    hi = a.astype(MXU_DTYPE)
    return hi, (a - hi.astype(F32)).astype(MXU_DTYPE)


def _mm3(a, b, dims=NN):
    (ah, al), (bh, bl) = _split(a), _split(b)
    dot = lambda p, q: lax.dot_general(p, q, dims, preferred_element_type=F32)
    return dot(ah, bh) + (dot(ah, bl) + dot(al, bh))


def _mmsel(a, sel, dims=NN):
    ah, al = _split(a)
    s = sel.astype(MXU_DTYPE)
    return (lax.dot_general(ah, s, dims, preferred_element_type=F32)
            + lax.dot_general(al, s, dims, preferred_element_type=F32))


def _sigmoid(x):
    return 0.5 * jnp.tanh(0.5 * x) + 0.5


def _softplus(x):
    return jnp.maximum(x, 0.0) + jnp.log(1.0 + jnp.exp(-jnp.abs(x)))


def _dsilu(x, s):
    return s * (1.0 + x * (1.0 - s))


def _rowsum(x):
    return jnp.sum(x, axis=1, keepdims=True)


def _colsum(x):
    return jnp.sum(x, axis=0, keepdims=True)


def _pick(dim, pref):
    if dim <= pref:
        return dim
    best = None
    t = LANES
    while t <= pref:
        if dim % t == 0:
            best = t
        t += LANES
    return dim if best is None else best


def _params(sem):
    return pltpu.CompilerParams(dimension_semantics=sem, vmem_limit_bytes=VMEM_LIMIT_V7X)


def _matmul(name, a, b, mode, out_dtype, tm=1024, tn=1024, tk=1024, scatter_riders=()):
    if mode == "nn":
        (m, k), (_, n) = a.shape, b.shape
    elif mode == "nt":
        (m, k), (n, _) = a.shape, b.shape
    else:
        (k, m), (_, n) = a.shape, b.shape
    tm, tn, tk = _pick(m, tm), _pick(n, tn), _pick(k, tk)
    nk = k // tk
    if mode == "tn":
        a_spec = pl.BlockSpec((tk, tm), lambda i, j, kk: (kk, i))
    else:
        a_spec = pl.BlockSpec((tm, tk), lambda i, j, kk: (i, kk))
    if mode == "nt":
        b_spec = pl.BlockSpec((tn, tk), lambda i, j, kk: (j, kk))
    else:
        b_spec = pl.BlockSpec((tk, tn), lambda i, j, kk: (kk, j))
    dims = {"nn": NN, "nt": NT, "tn": TN}[mode]

    assert nk == 1 or out_dtype == F32

    def body(a_ref, b_ref, o_ref):
        prod = _mm(a_ref[...], b_ref[...], dims)
        if nk == 1:
            o_ref[...] = prod.astype(out_dtype)
        else:
            kk = pl.program_id(2)

            @pl.when(kk == 0)
            def _():
                o_ref[...] = prod

            @pl.when(kk > 0)
            def _():
                o_ref[...] += prod

    grid = (m // tm, n // tn, nk)
    riders = list(scatter_riders)
    any_spec, rider_shapes, rider_sems, wrap = _riding_exchange(riders, True, 2, 1, grid)
    res = _pcall(
        wrap(body), name=name, grid=grid,
        in_specs=[a_spec, b_spec] + any_spec,
        out_specs=[pl.BlockSpec((tm, tn), lambda i, j, kk: (i, j))] + any_spec,
        out_shape=[jax.ShapeDtypeStruct((m, n), out_dtype)] + rider_shapes,
        scratch_shapes=rider_sems,
        compiler_params=_params(("arbitrary", "arbitrary", "arbitrary") if riders else ("parallel", "parallel", "arbitrary")),
    )(a, b, *riders)
    return (res[0], res[1:]) if riders else res[0]


def _rowwise(name, body, n_rows, tm, ins, outs, accs=()):
    arrays, in_specs = [], []
    last8 = n_rows // SUBLANES - 1
    per = tm // SUBLANES
    for spec in ins:
        kind, arr = spec[0], spec[1]
        if kind == "full":
            in_specs.append(pl.BlockSpec(arr.shape, lambda i, nd=arr.ndim: (0,) * nd))
        else:
            w, cb = spec[2], spec[3]
            if kind == "row":
                in_specs.append(pl.BlockSpec((tm, w), lambda i, cb=cb: (i, cb)))
            elif kind == "prev":
                in_specs.append(pl.BlockSpec((SUBLANES, w), lambda i, cb=cb: (jnp.maximum(i * per - 1, 0), cb)))
            else:
                in_specs.append(pl.BlockSpec((SUBLANES, w), lambda i, cb=cb: (jnp.minimum((i + 1) * per, last8), cb)))
        arrays.append(arr)
    out_shape = [jax.ShapeDtypeStruct((n_rows, w), dt) for (w, dt) in outs]
    out_shape += [jax.ShapeDtypeStruct(s, F32) for s in accs]
    out_specs = [pl.BlockSpec((tm, w), lambda i: (i, 0)) for (w, _) in outs]
    out_specs += [pl.BlockSpec(s, lambda i: (0, 0)) for s in accs]
    n_io = len(ins) + len(outs)

    def kern(*refs):
        i = pl.program_id(0)
        if accs:
            @pl.when(i == 0)
            def _():
                for r in refs[n_io:]:
                    r[...] = jnp.zeros_like(r)
        body(i, *refs)

    res = _pcall(
        kern, name=name, grid=(n_rows // tm,), in_specs=in_specs, out_specs=out_specs, out_shape=out_shape,
        compiler_params=_params(("arbitrary",)),
    )(*arrays)
    return res


def _shift_down(x, halo, j):
    r = pltpu.roll(x, j, 0)
    hr = pltpu.roll(halo, j, 0)
    rows = lax.broadcasted_iota(jnp.int32, (SUBLANES, x.shape[1]), 0)
    top = jnp.where(rows < j, hr, r[0:SUBLANES])
    return jnp.concatenate([top, r[SUBLANES:]], axis=0)


def _shift_up(x, halo, j):
    tm = x.shape[0]
    r = pltpu.roll(x, tm - j, 0)
    hr = pltpu.roll(halo, SUBLANES - j, 0)
    rows = lax.broadcasted_iota(jnp.int32, (SUBLANES, x.shape[1]), 0)
    bot = jnp.where(rows >= SUBLANES - j, hr, r[tm - SUBLANES:])
    return jnp.concatenate([r[:tm - SUBLANES], bot], axis=0)


def _conv_taps(x, halo, kw):
    return [x if kw - 1 - k == 0 else _shift_down(x, halo, kw - 1 - k) for k in range(kw)]


def _conv(taps, w):
    y = taps[0] * w[0:1]
    for k in range(1, len(taps)):
        y = y + taps[k] * w[k:k + 1]
    return y


def _rms(x, width):
    r = lax.rsqrt(jnp.sum(x * x, axis=-1, keepdims=True) * (1.0 / width) + EPS)
    return x * r, r


def _rms_bwd(xh, r, dxh, width):
    return r * (dxh - xh * (jnp.sum(dxh * xh, axis=-1, keepdims=True) * (1.0 / width)))


def _seq_flags(i, seq, tm):
    nps = seq // tm
    pos = i % nps
    return jnp.where(pos == 0, 0.0, 1.0), jnp.where(pos == nps - 1, 0.0, 1.0)


def _norm_cast(name, x, w, tm):
    t, d = x.shape

    def body(i, x_ref, w_ref, h_ref):
        xh, _ = _rms(x_ref[...], d)
        h_ref[...] = (xh * w_ref[...]).astype(BF16)

    return _rowwise(name, body, t, tm, [("row", x, d, 0), ("full", w)], [(d, BF16)])[0]


def _gdn_prep(proj, cw, gp, seq, tm):
    t = proj.shape[0]
    d = D_MODEL

    def body(i, q_ref, qh_ref, k_ref, kh_ref, v_ref, vh_ref, sm_ref, cw_ref, gp_ref, qn_ref, kn_ref, vv_ref, gs_ref):
        keep, _ = _seq_flags(i, seq, tm)
        for x_ref, h_ref, o_ref, off, scale in ((q_ref, qh_ref, qn_ref, 0, GDN_DK ** -0.5),
                                               (k_ref, kh_ref, kn_ref, d, 1.0), (v_ref, vh_ref, vv_ref, 2 * d, None)):
            y = _conv(_conv_taps(x_ref[...], h_ref[...] * keep, 4), cw_ref[:, off:off + d])
            a = y * _sigmoid(y)
            if scale is None:
                o_ref[...] = a
            else:
                for hh in range(GDN_HEADS):
                    s = a[:, hh * GDN_DK:(hh + 1) * GDN_DK]
                    n = lax.rsqrt(_rowsum(s * s) + EPS)
                    o_ref[:, hh * GDN_DK:(hh + 1) * GDN_DK] = s * (n * scale)
        sm = sm_ref[...]
        lane = lax.broadcasted_iota(jnp.int32, sm.shape, 1)
        beta = _sigmoid(sm)
        g = jnp.where((lane >= 8) & (lane < 16), -jnp.exp(gp_ref[0:1, :]) * _softplus(sm + gp_ref[1:2, :]), 0.0)
        gs_ref[...] = jnp.where(lane < 8, beta, _mmx(_block_tri(tm, False), g))

    ins = []
    for cb in range(3):
        ins += [("row", proj, d, cb), ("prev", proj, d, cb)]
    ins += [("row", proj, LANES, SMALL_CB), ("full", cw), ("full", gp)]
    return _rowwise("gdn_prep", body, t, tm, ins, [(d, F32), (d, F32), (d, F32), (LANES, F32)])


def _block_tri(tm, upper):
    ri = lax.broadcasted_iota(jnp.int32, (tm, tm), 0)
    ci = lax.broadcasted_iota(jnp.int32, (tm, tm), 1)
    tri = (ri <= ci) if upper else (ri >= ci)
    return (tri & ((ri // CHUNK) == (ci // CHUNK))).astype(F32)


def _chunk_consts():
    row = lax.broadcasted_iota(jnp.int32, (CHUNK, CHUNK), 0)
    col = lax.broadcasted_iota(jnp.int32, (CHUNK, CHUNK), 1)
    return dict(
        tril=row >= col, strict=row > col, eye=(row == col).astype(F32),
        lane=lax.broadcasted_iota(jnp.int32, (CHUNK, LANES), 1),
        row1=lax.broadcasted_iota(jnp.int32, (CHUNK, 1), 0),
        ones=jnp.ones((CHUNK, LANES), F32))


def _hmap(fn, *lists):
    return [fn(*a) for a in zip(*lists)]


def _tri_inv(nmats, eye):
    x = [eye - n for n in nmats]
    p = _hmap(_mm3, nmats, nmats)
    for lvl in range(5):
        x = _hmap(lambda xi, pi: xi + _mm3(xi, pi), x, p)
        if lvl < 4:
            p = _hmap(_mm3, p, p)
    return x


def _gdn_gates(gs, gc_row, h, c):
    beta = _rowsum(jnp.where(c["lane"] == h, gs, 0.0))
    gc = _rowsum(jnp.where(c["lane"] == h + 8, gs, 0.0))
    dc = jnp.exp(jnp.where(c["tril"], gc - gc_row, -1e30))
    gl = gc[CHUNK - 1:CHUNK, :]
    return beta, dc, jnp.exp(gc), jnp.exp(gl), jnp.exp(gl - gc)


GDN_HB_FWD = 8
GDN_HB_BWD = 4


def _gdn_specs(seq, sb, hb, backward):
    nsb = seq // sb
    ncb = sb // CHUNK
    order = (lambda j: nsb - 1 - j) if backward else (lambda j: j)
    specs = dict(
        wide=lambda: pl.BlockSpec((1, sb, hb * GDN_DK), lambda b, h, j: (b, order(j), h)),
        gs=pl.BlockSpec((1, sb, LANES), lambda b, h, j: (b, order(j), 0)),
        gr=pl.BlockSpec((1, hb, ncb, CHUNK), lambda b, h, j: (b, h, order(j), 0)),
        st=pl.BlockSpec((1, hb, ncb * GDN_DK, GDN_DK), lambda b, h, j: (b, h, order(j), 0)),
        ti=pl.BlockSpec((1, hb, sb, CHUNK), lambda b, h, j: (b, h, order(j), 0)))
    return nsb, ncb, specs


def _riding_exchange(arrays, scatter, n_in, n_out, grid):
    n = len(arrays)
    if n == 0:
        return [], [], [], lambda body: body
    any_spec = [pl.BlockSpec(memory_space=pl.ANY)] * n

    def wrap(body):
        def wrapped(*refs):
            ins = refs[n_in:n_in + n]
            outs = refs[n_in + n + n_out:n_in + 2 * n + n_out]
            sems = refs[len(refs) - 3:]
            pid = [pl.program_id(a) for a in range(len(grid))]
            first = functools.reduce(lambda a, b: a & b, [p == 0 for p in pid])
            last = functools.reduce(lambda a, b: a & b, [p == g - 1 for p, g in zip(pid, grid)])

            @pl.when(first)
            def _():
                _exchange_phase(ins, outs, sems, scatter, start=True)

            body(*refs[:n_in], *refs[n_in + n:n_in + n + n_out], *refs[n_in + 2 * n + n_out:len(refs) - 3])

            @pl.when(last)
            def _():
                _exchange_phase(ins, outs, sems, scatter, start=False)

        return wrapped

    return any_spec, _exchange_out_shapes(arrays, scatter), _exchange_sems(n), wrap


def _gdn_chunk_fwd(qn, kn, vv, gs, gr, bsz, seq, sb, riders):
    hb = GDN_HB_FWD
    nsb, ncb, sp = _gdn_specs(seq, sb, hb, False)
    grid = (bsz, GDN_HEADS // hb, nsb)
    any_spec, rider_shapes, rider_sems, wrap = _riding_exchange(riders, False, 5, 3, grid)

    def body(q_ref, k_ref, v_ref, gs_ref, gr_ref, o_ref, st_ref, ti_ref, s_scr):
        hg = pl.program_id(1)

        @pl.when(pl.program_id(2) == 0)
        def _():
            s_scr[...] = jnp.zeros_like(s_scr)

        c = _chunk_consts()

        def chunk(n, carry):
            r = pl.ds(pl.multiple_of(n * CHUNK, CHUNK), CHUNK)
            rs = pl.ds(pl.multiple_of(n * GDN_DK, GDN_DK), GDN_DK)
            gsv = gs_ref[0, r, :]
            heads = list(range(hb))
            sls = [slice(ih * GDN_DK, (ih + 1) * GDN_DK) for ih in heads]
            q = [q_ref[0, r, sl] for sl in sls]
            k = [k_ref[0, r, sl] for sl in sls]
            v = [v_ref[0, r, sl] for sl in sls]
            beta, dc, eg, egl, ekd = zip(*[
                _gdn_gates(gsv, gr_ref[0, ih, pl.ds(n, 1), :], hg * hb + ih, c) for ih in heads])
            kb = _hmap(lambda a, b: a * b, k, beta)
            amat = _hmap(lambda a, b, d_: jnp.where(c["strict"], _mm(a, b, NT) * d_, 0.0), kb, k, dc)
            tinv = _tri_inv(amat, c["eye"])
            u = _hmap(lambda t_, a, b: _mm3(t_, a * b), tinv, v, beta)
            w = _hmap(lambda t_, a, b: _mm3(t_, a * b), tinv, kb, eg)
            qk = _hmap(lambda a, b, d_: _mm(a, b, NT) * d_, q, k, dc)
            s = [s_scr[ih] for ih in heads]
            v_new = _hmap(lambda a, b, s_: a - _mm(b, s_), u, w, s)
            o = _hmap(lambda a, e, s_, qk_, vn: _mm(a * e, s_) + _mm(qk_, vn), q, eg, s, qk, v_new)
            s_new = _hmap(lambda s_, e, a, f, vn: s_ * e + _mm(a * f, vn, TN), s, egl, k, ekd, v_new)
            for ih in heads:
                o_ref[0, r, sls[ih]] = o[ih]
                st_ref[0, ih, rs, :] = s[ih]
                ti_ref[0, ih, r, :] = tinv[ih]
                s_scr[ih] = s_new[ih]
            return carry

        lax.fori_loop(0, ncb, chunk, 0)

    t3 = (bsz, seq, D_MODEL)
    res = _pcall(
        wrap(body), name="gdn_chunk_fwd", grid=grid,
        in_specs=[sp["wide"](), sp["wide"](), sp["wide"](), sp["gs"], sp["gr"]] + any_spec,
        out_specs=[sp["wide"](), sp["st"], sp["ti"]] + any_spec,
        out_shape=[jax.ShapeDtypeStruct(t3, F32),
                   jax.ShapeDtypeStruct((bsz, GDN_HEADS, (seq // CHUNK) * GDN_DK, GDN_DK), F32),
                   jax.ShapeDtypeStruct((bsz, GDN_HEADS, seq, CHUNK), F32)] + rider_shapes,
        scratch_shapes=[pltpu.VMEM((hb, GDN_DK, GDN_DK), F32)] + rider_sems,
        compiler_params=_params(("arbitrary", "arbitrary", "arbitrary")),
    )(qn, kn, vv, gs, gr, *riders)
    return res[:3], res[3:]


def _ssd_prep(proj, cw, cb, sp, seq, tm):
    t = proj.shape[0]
    d = D_MODEL
    ssd_w = SSD_HEADS * SSD_P

    def body(i, x_ref, xh_ref, bc_ref, bch_ref, sm_ref, cw_ref, cb_ref, sp_ref, xs_ref, bco_ref, dtx_ref, acsx_ref, acs_ref):
        keep, _ = _seq_flags(i, seq, tm)
        y = _conv(_conv_taps(x_ref[...], xh_ref[...] * keep, 4), cw_ref[:, 0:d]) + cb_ref[:, 0:d]
        xs_ref[...] = y * _sigmoid(y)
        y = _conv(_conv_taps(bc_ref[...], bch_ref[...] * keep, 4), cw_ref[:, d:d + 512]) + cb_ref[:, d:d + 512]
        bco_ref[...] = y * _sigmoid(y)
        sm = sm_ref[...]
        lane = lax.broadcasted_iota(jnp.int32, sm.shape, 1)
        valid = (lane >= 16) & (lane < 32)
        dt = jnp.where(valid, _softplus(sm + sp_ref[1:2, :]), 0.0)
        adt = dt * (-jnp.exp(sp_ref[0:1, :]))
        acs = _mmx(_block_tri(tm, False), adt)
        l64 = lax.broadcasted_iota(jnp.int32, (LANES, ssd_w), 0)
        d64 = lax.broadcasted_iota(jnp.int32, (LANES, ssd_w), 1)
        e64 = (l64 - 16 == d64 // SSD_P).astype(F32)
        dtx_ref[...] = _mmx(dt, e64)
        acsx_ref[...] = _mmx(acs, e64)
        acs_ref[...] = acs

    ins = [("row", proj, d, 5), ("prev", proj, d, 5), ("row", proj, 512, 12), ("prev", proj, 512, 12),
           ("row", proj, LANES, SMALL_CB), ("full", cw), ("full", cb), ("full", sp)]
    return _rowwise("ssd_prep", body, t, tm, ins, [(d, F32), (512, F32), (ssd_w, F32), (ssd_w, F32), (LANES, F32)])


SSD_GW = SSD_HPG * SSD_P


def _ssd_head(acs, ar_ref, n, head, cbm, c):
    col = _rowsum(jnp.where(c["lane"] == head + 16, acs, 0.0))
    lm = jnp.exp(jnp.where(c["tril"], col - ar_ref[0, head, pl.ds(n, 1), :], -1e30))
    return lm, cbm * lm


def _ssd_specs(seq, sb):
    nsb = seq // sb
    ncb = sb // CHUNK
    def specs(order):
        return dict(
            wide=lambda: pl.BlockSpec((1, sb, SSD_HEADS * SSD_P), lambda b, j: (b, order(j), 0)),
            bc=lambda: pl.BlockSpec((1, sb, 2 * SSD_GROUPS * SSD_N), lambda b, j: (b, order(j), 0)),
            half=lambda: pl.BlockSpec((1, sb, SSD_GROUPS * SSD_N), lambda b, j: (b, order(j), 0)),
            small=lambda: pl.BlockSpec((1, sb, LANES), lambda b, j: (b, order(j), 0)),
            ar=pl.BlockSpec((1, SSD_HEADS, ncb, CHUNK), lambda b, j: (b, 0, order(j), 0)),
            st=pl.BlockSpec((1, ncb * SSD_N, SSD_HEADS * SSD_P), lambda b, j: (b, order(j), 0)))
    return nsb, ncb, specs(lambda j: j), specs(lambda j: nsb - 1 - j)


def _ssd_chunk_fwd(xs, bc, dtx, acsx, acs, ar, bsz, seq, sb):
    nsb, ncb, sp, _ = _ssd_specs(seq, sb)

    def body(x_ref, dtx_ref, ax_ref, bc_ref, acs_ref, ar_ref, y_ref, sts_ref, st_scr):
        @pl.when(pl.program_id(1) == 0)
        def _():
            st_scr[...] = jnp.zeros_like(st_scr)

        c = _chunk_consts()
        lane5 = lax.broadcasted_iota(jnp.int32, (CHUNK, SSD_GW), 1) // SSD_P

        def chunk(n, carry):
            r = pl.ds(pl.multiple_of(n * CHUNK, CHUNK), CHUNK)
            rs = pl.ds(pl.multiple_of(n * SSD_N, SSD_N), SSD_N)
            acsv = acs_ref[0, r, :]
            for g in range(SSD_GROUPS):
                gl = slice(g * SSD_GW, (g + 1) * SSD_GW)
                x, dt, ax = x_ref[0, r, gl], dtx_ref[0, r, gl], ax_ref[0, r, gl]
                bm = bc_ref[0, r, g * SSD_N:(g + 1) * SSD_N]
                cm = bc_ref[0, r, (SSD_GROUPS + g) * SSD_N:(SSD_GROUPS + g + 1) * SSD_N]
                xdt = x * dt
                cbm = _mm(cm, bm, NT)
                al = ax[CHUNK - 1:CHUNK, :]
                st = st_scr[:, gl]
                y = _mm(cm, st) * jnp.exp(ax)
                for hh in range(SSD_HPG):
                    _, gm = _ssd_head(acsv, ar_ref, n, g * SSD_HPG + hh, cbm, c)
                    y = y + _mm(gm, jnp.where(lane5 == hh, xdt, 0.0))
                y_ref[0, r, gl] = y
                sts_ref[0, rs, gl] = st
                st_scr[:, gl] = st * jnp.exp(al) + _mm(bm, xdt * jnp.exp(al - ax), TN)
            return carry

        lax.fori_loop(0, ncb, chunk, 0)

    return _pcall(
        body, name="ssd_chunk_fwd", grid=(bsz, nsb),
        in_specs=[sp["wide"](), sp["wide"](), sp["wide"](), sp["bc"](), sp["small"](), sp["ar"]],
        out_specs=[sp["wide"](), sp["st"]],
        out_shape=[jax.ShapeDtypeStruct((bsz, seq, SSD_HEADS * SSD_P), F32),
                   jax.ShapeDtypeStruct((bsz, (seq // CHUNK) * SSD_N, SSD_HEADS * SSD_P), F32)],
        scratch_shapes=[pltpu.VMEM((SSD_N, SSD_HEADS * SSD_P), F32)],
        compiler_params=_params(("parallel", "arbitrary")),
    )(xs, dtx, acsx, bc, acs, ar)


def _gate_norm(o_gdn, y_ssd, xs, proj, gnw, snw, dvec, tm):
    t = o_gdn.shape[0]
    d = D_MODEL

    def body(i, o_ref, za_ref, y_ref, xs_ref, zs_ref, gnw_ref, snw_ref, dv_ref, out_ref):
        for hh in range(GDN_HEADS):
            sl = slice(hh * GDN_DK, (hh + 1) * GDN_DK)
            oh, _ = _rms(o_ref[:, sl], GDN_DK)
            z = za_ref[:, sl]
            out_ref[:, sl] = (oh * gnw_ref[...] * (z * _sigmoid(z))).astype(BF16)
        zs = zs_ref[...]
        yg = (y_ref[...] + dv_ref[...] * xs_ref[...]) * (zs * _sigmoid(zs))
        for g in range(SSD_GROUPS):
            sl = slice(g * 512, (g + 1) * 512)
            yh, _ = _rms(yg[:, sl], 512)
            out_ref[:, d + g * 512:d + (g + 1) * 512] = (yh * snw_ref[:, sl]).astype(BF16)

    ins = [("row", o_gdn, d, 0), ("row", proj, d, 3), ("row", y_ssd, d, 0), ("row", xs, d, 0), ("row", proj, d, 4),
           ("full", gnw), ("full", snw), ("full", dvec)]
    return _rowwise("gate_norm", body, t, tm, ins, [(2 * d, BF16)])[0]


def _mid(x, mix, pmw, pfw, tm):
    t, d = x.shape

    def body(i, x_ref, mix_ref, pmw_ref, pfw_ref, x1_ref, h2_ref):
        mh, _ = _rms(mix_ref[...], d)
        x1 = x_ref[...] + mh * pmw_ref[...]
        x1_ref[...] = x1
        xh, _ = _rms(x1, d)
        h2_ref[...] = (xh * pfw_ref[...]).astype(BF16)

    return _rowwise("mid", body, t, tm, [("row", x, d, 0), ("row", mix, d, 0), ("full", pmw), ("full", pfw)],
                    [(d, F32), (d, BF16)])


def _ffn_gate_up(ug_ref, ugh_ref, uu_ref, uuh_ref, cw_ref, cb_ref, keep):
    tg = _conv_taps(ug_ref[...], ugh_ref[...] * keep, 3)
    tu = _conv_taps(uu_ref[...], uuh_ref[...] * keep, 3)
    gate = _conv(tg, cw_ref[:, 0:D_FF]) + cb_ref[:, 0:D_FF]
    up = _conv(tu, cw_ref[:, D_FF:2 * D_FF]) + cb_ref[:, D_FF:2 * D_FF]
    return tg, tu, gate, up


def _ffn_act(u_pre, cw, cb, seq, tm):
    t = u_pre.shape[0]

    def body(i, ug_ref, ugh_ref, uu_ref, uuh_ref, cw_ref, cb_ref, act_ref):
        keep, _ = _seq_flags(i, seq, tm)
        _, _, gate, up = _ffn_gate_up(ug_ref, ugh_ref, uu_ref, uuh_ref, cw_ref, cb_ref, keep)
        act_ref[...] = (gate * _sigmoid(gate) * up).astype(BF16)

    ins = [("row", u_pre, D_FF, 0), ("prev", u_pre, D_FF, 0), ("row", u_pre, D_FF, 1), ("prev", u_pre, D_FF, 1),
           ("full", cw), ("full", cb)]
    return _rowwise("ffn_act", body, t, tm, ins, [(D_FF, BF16)])[0]


def _final(x1, f, tgt, w, tm):
    t, d = x1.shape

    def body(i, x1_ref, f_ref, t_ref, w_ref, dy_ref, df_ref, loss_ref, dw_ref):
        fh, r = _rms(f_ref[...], d)
        e = x1_ref[...] + fh * w_ref[...] - t_ref[...]
        loss_ref[...] += _colsum(e * e) * (0.5 / d)
        dy = e * (1.0 / d)
        dy_ref[...] = dy
        dw_ref[...] += _colsum(dy * fh)
        df_ref[...] = _rms_bwd(fh, r, dy * w_ref[...], d).astype(BF16)

    return _rowwise("final", body, t, tm, [("row", x1, d, 0), ("row", f, d, 0), ("row", tgt, d, 0), ("full", w)],
                    [(d, F32), (d, BF16)], accs=[(1, d), (1, d)])


def _ffn_bwd(u_pre, dact, cw, cb, seq, tm):
    t = u_pre.shape[0]

    def body(i, ug_ref, ugh_ref, ugn_ref, uu_ref, uuh_ref, uun_ref, da_ref, dan_ref, cw_ref, cb_ref, dpre_ref, dcw_ref, dcb_ref):
        keep, keep_next = _seq_flags(i, seq, tm)
        ext = lambda a_ref, n_ref: jnp.concatenate([a_ref[...], n_ref[...]], axis=0)
        rows = tm + SUBLANES
        tg = _conv_taps(ext(ug_ref, ugn_ref), ugh_ref[...] * keep, 3)
        tu = _conv_taps(ext(uu_ref, uun_ref), uuh_ref[...] * keep, 3)
        gate = _conv(tg, cw_ref[:, 0:D_FF]) + cb_ref[:, 0:D_FF]
        up = _conv(tu, cw_ref[:, D_FF:2 * D_FF]) + cb_ref[:, D_FF:2 * D_FF]
        sg = _sigmoid(gate)
        da = jnp.concatenate([da_ref[...], dan_ref[...] * keep_next], axis=0)
        for off, grad, taps in ((0, da * up * _dsilu(gate, sg), tg), (D_FF, da * gate * sg, tu)):
            own = grad[0:tm]
            acc = own * cw_ref[2:3, off:off + D_FF]
            for j in (1, 2):
                acc = acc + pltpu.roll(grad, rows - j, 0)[0:tm] * cw_ref[2 - j:3 - j, off:off + D_FF]
            dpre_ref[:, off:off + D_FF] = acc.astype(BF16)
            dcb_ref[:, off:off + D_FF] += _colsum(own)
            for k in range(3):
                dcw_ref[k:k + 1, off:off + D_FF] += _colsum(own * taps[k][0:tm])

    ins = []
    for cb_ in range(2):
        ins += [("row", u_pre, D_FF, cb_), ("prev", u_pre, D_FF, cb_), ("next", u_pre, D_FF, cb_)]
    ins += [("row", dact, D_FF, 0), ("next", dact, D_FF, 0), ("full", cw), ("full", cb)]
    return _rowwise("ffn_bwd", body, t, tm, ins, [(2 * D_FF, BF16)], accs=[(SUBLANES, 2 * D_FF), (1, 2 * D_FF)])


def _assemble_dproj(dpre_gdn, dza, dzs, dpre_ssd, dsm, gcw, scw, seq, tm):
    t = dpre_gdn.shape[0]
    d = D_MODEL

    def body(i, dg_ref, dgn_ref, dza_ref, dzs_ref, ds_ref, dsn_ref, dsm_ref, gcw_ref, scw_ref, o_ref):
        _, keep = _seq_flags(i, seq, tm)
        pieces = [(dg_ref, dgn_ref, gcw_ref, 0, c0) for c0 in range(0, 3 * d, d)]
        pieces += [(ds_ref, dsn_ref, scw_ref, 5 * d, c0) for c0 in (0, d)]
        for d_ref, n_ref, cw_ref, base, c0 in pieces:
            w = min(d, d_ref.shape[1] - c0)
            x = d_ref[:, c0:c0 + w]
            halo = n_ref[:, c0:c0 + w] * keep
            acc = x * cw_ref[3:4, c0:c0 + w]
            for j in range(1, 4):
                acc = acc + _shift_up(x, halo, j) * cw_ref[3 - j:4 - j, c0:c0 + w]
            o_ref[:, base + c0:base + c0 + w] = acc.astype(BF16)
        o_ref[:, 3 * d:4 * d] = dza_ref[...]
        o_ref[:, 4 * d:5 * d] = dzs_ref[...]
        o_ref[:, 6 * d + 512:6 * d + 512 + LANES] = dsm_ref[...]
        o_ref[:, 6 * d + 512 + LANES:PROJ_W] = jnp.zeros((tm, PROJ_W - (6 * d + 512 + LANES)), BF16)

    ins = [("row", dpre_gdn, 3 * d, 0), ("next", dpre_gdn, 3 * d, 0), ("row", dza, d, 0), ("row", dzs, d, 0),
           ("row", dpre_ssd, d + 512, 0), ("next", dpre_ssd, d + 512, 0), ("row", dsm, LANES, 0), ("full", gcw), ("full", scw)]
    return _rowwise("assemble_dproj", body, t, tm, ins, [(PROJ_W, BF16)])[0]


def _mid_bwd(x1, mix, dy, dh2, pmw, pfw, tm):
    t, d = x1.shape

    def body(i, x1_ref, mix_ref, dy_ref, dh2_ref, pmw_ref, pfw_ref, dx1_ref, dmix_ref, dpm_ref, dpf_ref):
        xh, r2 = _rms(x1_ref[...], d)
        dh2 = dh2_ref[...]
        dpf_ref[...] += _colsum(dh2 * xh)
        dx1 = dy_ref[...] + _rms_bwd(xh, r2, dh2 * pfw_ref[...], d)
        dx1_ref[...] = dx1
        mh, r = _rms(mix_ref[...], d)
        dpm_ref[...] += _colsum(dx1 * mh)
        dmix_ref[...] = _rms_bwd(mh, r, dx1 * pmw_ref[...], d).astype(BF16)

    ins = [("row", x1, d, 0), ("row", mix, d, 0), ("row", dy, d, 0), ("row", dh2, d, 0), ("full", pmw), ("full", pfw)]
    return _rowwise("mid_bwd", body, t, tm, ins, [(d, F32), (d, BF16)], accs=[(1, d), (1, d)])


def _gate_norm_bwd(o_gdn, y_ssd, xs, proj, dmixin, gnw, snw, dvec, tm):
    t = o_gdn.shape[0]
    d = D_MODEL

    def body(i, o_ref, za_ref, y_ref, xs_ref, zs_ref, dma_ref, dms_ref, gnw_ref, snw_ref, dv_ref,
             do_ref, dza_ref, dy_ref, dxs_ref, dzs_ref, dgnw_ref, dsnw_ref, dd_ref):
        for hh in range(GDN_HEADS):
            sl = slice(hh * GDN_DK, (hh + 1) * GDN_DK)
            oh, r = _rms(o_ref[:, sl], GDN_DK)
            z = za_ref[:, sl]
            sz = _sigmoid(z)
            dm = dma_ref[:, sl]
            don = dm * (z * sz)
            dza_ref[:, sl] = (dm * oh * gnw_ref[...] * _dsilu(z, sz)).astype(BF16)
            dgnw_ref[...] += _colsum(don * oh)
            do_ref[:, sl] = _rms_bwd(oh, r, don * gnw_ref[...], GDN_DK)
        zs = zs_ref[...]
        sz = _sigmoid(zs)
        sil = zs * sz
        x = xs_ref[...]
        y0 = y_ref[...] + dv_ref[...] * x
        yg = y0 * sil
        dms = dms_ref[...]
        for g in range(SSD_GROUPS):
            sl = slice(g * 512, (g + 1) * 512)
            yh, r = _rms(yg[:, sl], 512)
            dsnw_ref[:, sl] += _colsum(dms[:, sl] * yh)
            dyg = _rms_bwd(yh, r, dms[:, sl] * snw_ref[:, sl], 512)
            dy0 = dyg * sil[:, sl]
            dzs_ref[:, sl] = (dyg * y0[:, sl] * _dsilu(zs[:, sl], sz[:, sl])).astype(BF16)
            dy_ref[:, sl] = dy0
            dxs_ref[:, sl] = dy0 * dv_ref[:, sl]
            dd_ref[:, sl] += _colsum(dy0 * x[:, sl])

    ins = [("row", o_gdn, d, 0), ("row", proj, d, 3), ("row", y_ssd, d, 0), ("row", xs, d, 0), ("row", proj, d, 4),
           ("row", dmixin, d, 0), ("row", dmixin, d, 1), ("full", gnw), ("full", snw), ("full", dvec)]
    return _rowwise("gate_norm_bwd", body, t, tm, ins, [(d, F32), (d, BF16), (d, F32), (d, F32), (d, BF16)],
                    accs=[(1, GDN_DK), (1, d), (1, d)])


def _ssd_chunk_bwd(xs, bc, dtx, acsx, acs, ar, dy, sts, bsz, seq, sb):
    nsb, ncb, _, sp = _ssd_specs(seq, sb)

    def body(x_ref, dtx_ref, ax_ref, bc_ref, acs_ref, ar_ref, dy_ref, sts_ref, dx_ref, dbc_ref, ddt_ref, dacs_ref, dst_scr):
        @pl.when(pl.program_id(1) == 0)
        def _():
            dst_scr[...] = jnp.zeros_like(dst_scr)

        c = _chunk_consts()
        lane5 = lax.broadcasted_iota(jnp.int32, (CHUNK, SSD_GW), 1) // SSD_P
        row5 = lax.broadcasted_iota(jnp.int32, (CHUNK, SSD_GW), 0)
        sel_in = lax.broadcasted_iota(jnp.int32, (SSD_GW, LANES), 0) // SSD_P
        sel_out = lax.broadcasted_iota(jnp.int32, (SSD_GW, LANES), 1)

        def chunk(nn, carry):
            n = ncb - 1 - nn
            r = pl.ds(pl.multiple_of(n * CHUNK, CHUNK), CHUNK)
            rs = pl.ds(pl.multiple_of(n * SSD_N, SSD_N), SSD_N)
            acsv = acs_ref[0, r, :]
            ddt = jnp.zeros((CHUNK, LANES), F32)
            dacs = jnp.zeros((CHUNK, LANES), F32)
            for g in range(SSD_GROUPS):
                gl = slice(g * SSD_GW, (g + 1) * SSD_GW)
                x, dt, ax, dyv = x_ref[0, r, gl], dtx_ref[0, r, gl], ax_ref[0, r, gl], dy_ref[0, r, gl]
                bm = bc_ref[0, r, g * SSD_N:(g + 1) * SSD_N]
                cm = bc_ref[0, r, (SSD_GROUPS + g) * SSD_N:(SSD_GROUPS + g + 1) * SSD_N]
                st = sts_ref[0, rs, gl]
                dst = dst_scr[:, gl]
                rsel = (sel_in + (16 + g * SSD_HPG) == sel_out).astype(F32)
                xdt = x * dt
                cbm = _mm(cm, bm, NT)
                al = ax[CHUNK - 1:CHUNK, :]
                ex, el = jnp.exp(ax), jnp.exp(al)
                dec = jnp.exp(al - ax)
                xd = xdt * dec
                dye = dyv * ex
                dxd = _mm(bm, dst)
                dxdt = dec * dxd
                dcm = _mm(dye, st, NT)
                dbm = _mm(xd, dst, NT)
                z = dye * _mm(cm, st) - dxd * xd
                zl = _colsum(dst * st) * el + _colsum(dxd * xd)
                z = z + jnp.where(row5 == CHUNK - 1, zl, 0.0)
                dcb = jnp.zeros((CHUNK, CHUNK), F32)
                for hh in range(SSD_HPG):
                    head = g * SSD_HPG + hh
                    lm, gm = _ssd_head(acsv, ar_ref, n, head, cbm, c)
                    dym = jnp.where(lane5 == hh, dyv, 0.0)
                    dxdt = dxdt + _mm(gm, dym, TN)
                    dg = _mm(dym, xdt, NT)
                    dcb = dcb + dg * lm
                    pm = dg * gm
                    dacs = dacs + jnp.where(c["lane"] == head + 16, _rowsum(pm) - _mmsel(pm, c["ones"], TN), 0.0)
                dbc_ref[0, r, (SSD_GROUPS + g) * SSD_N:(SSD_GROUPS + g + 1) * SSD_N] = dcm + _mm(dcb, bm)
                dbc_ref[0, r, g * SSD_N:(g + 1) * SSD_N] = dbm + _mm(dcb, cm, TN)
                dacs = dacs + _mmsel(z, rsel)
                ddt = ddt + _mmsel(dxdt * x, rsel)
                dx_ref[0, r, gl] = dxdt * dt
                dst_scr[:, gl] = dst * el + _mm(cm, dye, TN)
            ddt_ref[0, r, :] = ddt
            dacs_ref[0, r, :] = dacs
            return carry

        lax.fori_loop(0, ncb, chunk, 0)

    return _pcall(
        body, name="ssd_chunk_bwd", grid=(bsz, nsb),
        in_specs=[sp["wide"](), sp["wide"](), sp["wide"](), sp["bc"](), sp["small"](), sp["ar"], sp["wide"](), sp["st"]],
        out_specs=[sp["wide"](), sp["bc"](), sp["small"](), sp["small"]()],
        out_shape=[jax.ShapeDtypeStruct((bsz, seq, SSD_HEADS * SSD_P), F32),
                   jax.ShapeDtypeStruct((bsz, seq, 2 * SSD_GROUPS * SSD_N), F32),
                   jax.ShapeDtypeStruct((bsz, seq, LANES), F32), jax.ShapeDtypeStruct((bsz, seq, LANES), F32)],
        scratch_shapes=[pltpu.VMEM((SSD_N, SSD_HEADS * SSD_P), F32)],
        compiler_params=_params(("parallel", "arbitrary")),
    )(xs, dtx, acsx, bc, acs, ar, dy, sts)


def _ssd_prep_bwd(proj, dxs_c, dxs_d, dbc, ddt, dacs, dsm_gdn, cw, cb, sp, seq, tm):
    t = proj.shape[0]
    d = D_MODEL

    def body(i, x_ref, xh_ref, bc_ref, bch_ref, sm_ref, dxc_ref, dxd_ref, dbc_ref, ddt_ref, dacs_ref, dsg_ref,
             cw_ref, cb_ref, sp_ref, dpre_ref, dsm_ref, dcw_ref, dcb_ref, dsp_ref):
        keep, _ = _seq_flags(i, seq, tm)
        parts = ((x_ref, xh_ref, 0, d, (dxc_ref[...] + dxd_ref[...],)),
                 (bc_ref, bch_ref, d, 512, (dbc_ref[...],)))
        for xr, hr, off, w, grads in parts:
            taps = _conv_taps(xr[...], hr[...] * keep, 4)
            y = _conv(taps, cw_ref[:, off:off + w]) + cb_ref[:, off:off + w]
            ds_ = _dsilu(y, _sigmoid(y))
            o = 0
            for gr in grads:
                wg = gr.shape[1]
                dpre = gr * ds_[:, o:o + wg]
                dpre_ref[:, off + o:off + o + wg] = dpre
                dcb_ref[:, off + o:off + o + wg] += _colsum(dpre)
                for k in range(4):
                    dcw_ref[k:k + 1, off + o:off + o + wg] += _colsum(dpre * taps[k][:, o:o + wg])
                o += wg
        sm = sm_ref[...]
        lane = lax.broadcasted_iota(jnp.int32, sm.shape, 1)
        valid = (lane >= 16) & (lane < 32)
        xb = sm + sp_ref[1:2, :]
        dt = jnp.where(valid, _softplus(xb), 0.0)
        a_neg = -jnp.exp(sp_ref[0:1, :])
        dadt_s = _mmx(_block_tri(tm, True), dacs_ref[...])
        dxb = jnp.where(valid, (ddt_ref[...] + dadt_s * a_neg) * _sigmoid(xb), 0.0)
        dsm_ref[...] = (dsg_ref[...] + dxb).astype(BF16)
        dsp_ref[1:2, :] += _colsum(dxb)
        dsp_ref[0:1, :] += jnp.where(valid[0:1, :], _colsum(dadt_s * dt) * a_neg, 0.0)

    ins = [("row", proj, d, 5), ("prev", proj, d, 5), ("row", proj, 512, 12), ("prev", proj, 512, 12),
           ("row", proj, LANES, SMALL_CB), ("row", dxs_c, d, 0), ("row", dxs_d, d, 0), ("row", dbc, 512, 0),
           ("row", ddt, LANES, 0), ("row", dacs, LANES, 0), ("row", dsm_gdn, LANES, 0),
           ("full", cw), ("full", cb), ("full", sp)]
    return _rowwise("ssd_prep_bwd", body, t, tm, ins, [(d + 512, F32), (LANES, BF16)],
                    accs=[(SUBLANES, d + 512), (1, d + 512), (SUBLANES, LANES)])


def _gdn_chunk_bwd(qn, kn, vv, gs, gr, do, sts, tis, bsz, seq, sb, riders):
    hb = GDN_HB_BWD
    nsb, ncb, sp = _gdn_specs(seq, sb, hb, True)
    grid = (bsz, GDN_HEADS // hb, nsb)
    any_spec, rider_shapes, rider_sems, wrap = _riding_exchange(riders, True, 8, 4, grid)

    def body(q_ref, k_ref, v_ref, gs_ref, gr_ref, do_ref, st_ref, ti_ref, dq_ref, dk_ref, dv_ref, dgb_ref, ds_scr):
        hg = pl.program_id(1)

        @pl.when(pl.program_id(2) == 0)
        def _():
            ds_scr[...] = jnp.zeros_like(ds_scr)

        c = _chunk_consts()

        def chunk(nn, carry):
            n = ncb - 1 - nn
            r = pl.ds(pl.multiple_of(n * CHUNK, CHUNK), CHUNK)
            rs = pl.ds(pl.multiple_of(n * GDN_DK, GDN_DK), GDN_DK)
            gsv = gs_ref[0, r, :]
            heads = list(range(hb))
            sls = [slice(ih * GDN_DK, (ih + 1) * GDN_DK) for ih in heads]
            q = [q_ref[0, r, sl] for sl in sls]
            k = [k_ref[0, r, sl] for sl in sls]
            v = [v_ref[0, r, sl] for sl in sls]
            do_ = [do_ref[0, r, sl] for sl in sls]
            s = [st_ref[0, ih, rs, :] for ih in heads]
            tinv = [ti_ref[0, ih, r, :] for ih in heads]
            dsn = [ds_scr[ih] for ih in heads]
            beta, dc, eg, egl, ekd = zip(*[
                _gdn_gates(gsv, gr_ref[0, ih, pl.ds(n, 1), :], hg * hb + ih, c) for ih in heads])
            mul = lambda a, b: a * b
            kb = _hmap(mul, k, beta)
            rhs_w = _hmap(mul, kb, eg)
            u = _hmap(lambda t_, a, b: _mm3(t_, a * b), tinv, v, beta)
            w = _hmap(_mm3, tinv, rhs_w)
            amat = _hmap(lambda a, b, d_: jnp.where(c["strict"], _mm(a, b, NT) * d_, 0.0), kb, k, dc)
            qk = _hmap(lambda a, b, d_: _mm(a, b, NT) * d_, q, k, dc)
            qd = _hmap(mul, q, eg)
            kd = _hmap(mul, k, ekd)
            v_new = _hmap(lambda a, b, s_: a - _mm(b, s_), u, w, s)
            dv_new = _hmap(lambda qk_, d_, kd_, dn: _mm(qk_, d_, TN) + _mm(kd_, dn), qk, do_, kd, dsn)
            dqk = _hmap(lambda d_, vn: _mm(d_, vn, NT), do_, v_new)
            dqd = _hmap(lambda d_, s_: _mm(d_, s_, NT), do_, s)
            ds_new = _hmap(lambda qd_, d_, dn, e, w_, dvn: _mm(qd_, d_, TN) + dn * e - _mm(w_, dvn, TN),
                           qd, do_, dsn, egl, w, dv_new)
            dkd = _hmap(lambda vn, dn: _mm(vn, dn, NT), v_new, dsn)
            dgl = _hmap(lambda s_, dn, e: _colsum(_rowsum(s_ * dn)) * e, s, dsn, egl)
            dw = _hmap(lambda dvn, s_: -_mm(dvn, s_, NT), dv_new, s)
            dru = _hmap(lambda t_, a: _mm3(t_, a, TN), tinv, dv_new)
            drw = _hmap(lambda t_, a: _mm3(t_, a, TN), tinv, dw)
            da = _hmap(lambda a, u_, b, w_: jnp.where(c["strict"], -(_mm(a, u_, NT) + _mm(b, w_, NT)), 0.0), dru, u, drw, w)
            m = _hmap(mul, da, dc)
            dkb = _hmap(lambda a, e, m_, k_: a * e + _mm(m_, k_), drw, eg, m, k)
            mq = _hmap(mul, dqk, dc)
            dq = _hmap(lambda mq_, k_, a, e: _mm(mq_, k_) + a * e, mq, k, dqd, eg)
            dk = _hmap(lambda m_, kb_, mq_, q_, a, e, b, be: _mm(m_, kb_, TN) + _mm(mq_, q_, TN) + a * e + b * be,
                       m, kb, mq, q, dkd, ekd, dkb, beta)
            dbeta = _hmap(lambda a, v_, b, k_: _rowsum(a * v_) + _rowsum(b * k_), dru, v, dkb, k)
            pq = _hmap(lambda a, am, b, qk_: a * am + b * qk_, da, amat, dqk, qk)
            ekk = _hmap(lambda a, b: _rowsum(a * b), dkd, kd)
            dgc = _hmap(lambda pq_, a, rw, b, qd_, e, gl_: (
                _rowsum(pq_) - _mmsel(pq_, c["ones"], TN) + (_rowsum(a * rw) + _rowsum(b * qd_) - e)
                + jnp.where(c["row1"] == CHUNK - 1, _colsum(e) + gl_, 0.0)), pq, drw, rhs_w, dqd, qd, ekk, dgl)
            for ih in heads:
                ds_scr[ih] = ds_new[ih]
                dv_ref[0, r, sls[ih]] = dru[ih] * beta[ih]
                dq_ref[0, r, sls[ih]] = dq[ih]
                dk_ref[0, r, sls[ih]] = dk[ih]
                dgb_ref[0, r, sls[ih]] = jnp.where(c["lane"] == 0, dbeta[ih], jnp.where(c["lane"] == 1, dgc[ih], 0.0))
            return carry

        lax.fori_loop(0, ncb, chunk, 0)

    res = _pcall(
        wrap(body), name="gdn_chunk_bwd", grid=grid,
        in_specs=[sp["wide"](), sp["wide"](), sp["wide"](), sp["gs"], sp["gr"], sp["wide"](), sp["st"], sp["ti"]] + any_spec,
        out_specs=[sp["wide"](), sp["wide"](), sp["wide"](), sp["wide"]()] + any_spec,
        out_shape=[jax.ShapeDtypeStruct((bsz, seq, D_MODEL), F32)] * 4 + rider_shapes,
        scratch_shapes=[pltpu.VMEM((hb, GDN_DK, GDN_DK), F32)] + rider_sems,
        compiler_params=_params(("arbitrary", "arbitrary", "arbitrary")),
    )(qn, kn, vv, gs, gr, do, sts, tis, *riders)
    return res[:4], res[4:]


def _gdn_prep_bwd(proj, dqn, dkn, dvv, dgb, cw, gp, seq, tm):
    t = proj.shape[0]
    d = D_MODEL

    def body(i, q_ref, qh_ref, k_ref, kh_ref, v_ref, vh_ref, sm_ref, dq_ref, dk_ref, dv_ref, dgb_ref, cw_ref, gp_ref,
             dpre_ref, dsm_ref, dcw_ref, dgp_ref):
        keep, _ = _seq_flags(i, seq, tm)
        for x_ref, h_ref, g_ref, off, scale in ((q_ref, qh_ref, dq_ref, 0, GDN_DK ** -0.5),
                                               (k_ref, kh_ref, dk_ref, d, 1.0), (v_ref, vh_ref, dv_ref, 2 * d, None)):
            taps = _conv_taps(x_ref[...], h_ref[...] * keep, 4)
            y = _conv(taps, cw_ref[:, off:off + d])
            sy = _sigmoid(y)
            ds_ = _dsilu(y, sy)
            if scale is None:
                dpre = g_ref[...] * ds_
                dpre_ref[:, off:off + d] = dpre
                for k in range(4):
                    dcw_ref[k:k + 1, off:off + d] += _colsum(dpre * taps[k])
            else:
                a = y * sy
                for hh in range(GDN_HEADS):
                    sl = slice(hh * GDN_DK, (hh + 1) * GDN_DK)
                    s = a[:, sl]
                    n = lax.rsqrt(_rowsum(s * s) + EPS)
                    ah = s * n
                    gq = g_ref[:, sl]
                    dpre = (scale * n) * (gq - ah * _rowsum(gq * ah)) * ds_[:, sl]
                    dpre_ref[:, off + hh * GDN_DK:off + (hh + 1) * GDN_DK] = dpre
                    for k in range(4):
                        dcw_ref[k:k + 1, off + hh * GDN_DK:off + (hh + 1) * GDN_DK] += _colsum(dpre * taps[k][:, sl])
        sm = sm_ref[...]
        lane = lax.broadcasted_iota(jnp.int32, sm.shape, 1)
        si = lax.broadcasted_iota(jnp.int32, (d, LANES), 0)
        so = lax.broadcasted_iota(jnp.int32, (d, LANES), 1)
        sel = (((si % GDN_DK == 0) & (so == si // GDN_DK)) | ((si % GDN_DK == 1) & (so == si // GDN_DK + 8))).astype(F32)
        dsel = _mmx(dgb_ref[...], sel)
        is_g = (lane >= 8) & (lane < 16)
        dsel = jnp.where(is_g, _mmx(_block_tri(tm, True), dsel), dsel)
        beta = _sigmoid(sm)
        xb = sm + gp_ref[1:2, :]
        a_neg = -jnp.exp(gp_ref[0:1, :])
        sp = _softplus(xb)
        dxb = jnp.where(is_g, dsel * a_neg * _sigmoid(xb), 0.0)
        dsm_ref[...] = jnp.where(lane < 8, dsel * beta * (1.0 - beta), dxb)
        dgp_ref[1:2, :] += _colsum(dxb)
        dgp_ref[0:1, :] += _colsum(jnp.where(is_g, dsel * a_neg * sp, 0.0))

    ins = []
    for cb in range(3):
        ins += [("row", proj, d, cb), ("prev", proj, d, cb)]
    ins += [("row", proj, LANES, SMALL_CB), ("row", dqn, d, 0), ("row", dkn, d, 0), ("row", dvv, d, 0), ("row", dgb, d, 0),
            ("full", cw), ("full", gp)]
    return _rowwise("gdn_prep_bwd", body, t, tm, ins, [(3 * d, F32), (LANES, F32)],
                    accs=[(SUBLANES, 3 * d), (SUBLANES, LANES)])


def _first_bwd(x, dh1, dx1, w, tm):
    t, d = x.shape

    def body(i, x_ref, dh_ref, dx1_ref, w_ref, dx_ref, dw_ref):
        xh, r = _rms(x_ref[...], d)
        dh = dh_ref[...]
        dw_ref[...] += _colsum(dh * xh)
        dx_ref[...] = dx1_ref[...] + _rms_bwd(xh, r, dh * w_ref[...], d)

    return _rowwise("first_bwd", body, t, tm, [("row", x, d, 0), ("row", dh1, d, 0), ("row", dx1, d, 0), ("full", w)],
                    [(d, F32)], accs=[(1, d)])


def _exchange(name, arrays, scatter):
    n = len(arrays)

    def body(*refs):
        ins, outs, sems = refs[:n], refs[n:2 * n], refs[2 * n:]
        _exchange_phase(ins, outs, sems, scatter, start=True)
        _exchange_phase(ins, outs, sems, scatter, start=False)

    return _pcall(
        body, name=name,
        in_specs=[pl.BlockSpec(memory_space=pl.ANY)] * n,
        out_specs=[pl.BlockSpec(memory_space=pl.ANY)] * n,
        out_shape=_exchange_out_shapes(arrays, scatter),
        scratch_shapes=_exchange_sems(n),
    )(*arrays)


def _exchange_out_shapes(arrays, scatter):
    return [jax.ShapeDtypeStruct(a.shape if scatter else (N_DEV,) + a.shape, a.dtype) for a in arrays]


def _exchange_sems(n):
    return [pltpu.SemaphoreType.DMA((n, N_DEV - 1)), pltpu.SemaphoreType.DMA((n, N_DEV - 1)), pltpu.SemaphoreType.DMA((n,))]


def _exchange_phase(ins, outs, sems, scatter, start):
    send_sems, recv_sems, loc_sems = sems
    x, y, c = lax.axis_index("x"), lax.axis_index("y"), lax.axis_index("c")
    me = 4 * x + 2 * y + c
    for t in range(len(ins)):
        loc = pltpu.make_async_copy(ins[t].at[me] if scatter else ins[t], outs[t].at[me], loc_sems.at[t])
        if start:
            loc.start()
        else:
            loc.wait()
        for k in range(N_DEV - 1):
            bx, by, bc = ((k + 1) >> 2) & 1, ((k + 1) >> 1) & 1, (k + 1) & 1
            px = 1 - x if bx else x
            py = 1 - y if by else y
            pc = 1 - c if bc else c
            peer = 4 * px + 2 * py + pc
            src = ins[t].at[peer] if scatter else ins[t]
            copy = lambda dst: pltpu.make_async_remote_copy(
                src_ref=src, dst_ref=dst, send_sem=send_sems.at[t, k], recv_sem=recv_sems.at[t, k],
                device_id=(px, py, pc), device_id_type=pl.DeviceIdType.MESH)
            if start:
                copy(outs[t].at[me]).start()
            else:
                copy(outs[t].at[me]).wait_send()
                copy(outs[t].at[peer]).wait_recv()


def _adam_math(w, g, m, v):
    m = ADAM_B1 * m + (1.0 - ADAM_B1) * g
    v = ADAM_B2 * v + (1.0 - ADAM_B2) * (g * g)
    m_hat = m / (1.0 - ADAM_B1 ** ADAM_STEP)
    v_hat = v / (1.0 - ADAM_B2 ** ADAM_STEP)
    delta = -ADAM_LR * (m_hat / (jnp.sqrt(v_hat) + ADAM_EPS) + ADAM_WD * w)
    return delta, m, v


def _adam_big(name, parts, w, m, v, tm):
    r, c = w.shape
    tm = tm if r % tm == 0 else r

    def body(p_ref, w_ref, m_ref, v_ref, g_ref, d_ref, nm_ref, nv_ref):
        g = p_ref[0].astype(F32)
        for s in range(1, N_DEV):
            g = g + p_ref[s].astype(F32)
        g_ref[...] = g
        d_ref[...], nm_ref[...], nv_ref[...] = _adam_math(w_ref[...], g, m_ref[...], v_ref[...])

    blk = lambda: pl.BlockSpec((tm, c), lambda i: (i, 0))
    return _pcall(
        body, name=name, grid=(r // tm,),
        in_specs=[pl.BlockSpec((N_DEV, tm, c), lambda i: (0, i, 0)), blk(), blk(), blk()],
        out_specs=[blk(), blk(), blk(), blk()],
        out_shape=[jax.ShapeDtypeStruct((r, c), F32)] * 4,
        compiler_params=_params(("parallel",)),
    )(parts, w, m, v)


SMALL_ROWS = 56
ROW_DD, ROW_LOSS = 5, 6


def _small_sum(gathered):
    def body(g_ref, o_ref, x_ref):
        s = g_ref[0]
        for dev in range(1, N_DEV):
            s = s + g_ref[dev]
        o_ref[...] = s
        ri = lax.broadcasted_iota(jnp.int32, (D_MODEL, LANES), 0)
        ro = lax.broadcasted_iota(jnp.int32, (D_MODEL, LANES), 1)
        heads = _mmx(jnp.broadcast_to(s[ROW_DD:ROW_DD + 1, :], (SUBLANES, D_MODEL)), (ri // SSD_P == ro).astype(F32))
        loss = _rowsum(jnp.broadcast_to(s[ROW_LOSS:ROW_LOSS + 1, :], (SUBLANES, D_MODEL)))
        row = lax.broadcasted_iota(jnp.int32, (SUBLANES, LANES), 0)
        x_ref[...] = jnp.where(row == 0, heads, jnp.broadcast_to(loss, (SUBLANES, LANES)))

    return _pcall(
        body, name="small_sum",
        out_shape=[jax.ShapeDtypeStruct((SMALL_ROWS, D_MODEL), F32), jax.ShapeDtypeStruct((SUBLANES, LANES), F32)],
        compiler_params=_params(None),
    )(gathered)


def _adam_small(g, w, m, v):
    def body(g_ref, w_ref, m_ref, v_ref, d_ref, nm_ref, nv_ref):
        d_ref[...], nm_ref[...], nv_ref[...] = _adam_math(w_ref[...], g_ref[...], m_ref[...], v_ref[...])

    return _pcall(body, name="adam_small", out_shape=[jax.ShapeDtypeStruct(g.shape, F32)] * 3,
                  compiler_params=_params(None))(g, w, m, v)


def _pack(pieces, rows):
    flat = jnp.concatenate([p.reshape(-1).astype(F32) for p in pieces])
    return jnp.pad(flat, (0, rows * D_MODEL - flat.shape[0])).reshape(rows, D_MODEL)


def _unpack(packed, shapes):
    flat = packed.reshape(-1)
    out, off = [], 0
    for shp in shapes:
        size = 1
        for s in shp:
            size *= s
        out.append(flat[off:off + size].reshape(shp))
        off += size
    return out


def _permute_in(w):
    pad = jnp.zeros((w.shape[0], PROJ_W - D_IN), w.dtype)
    return jnp.concatenate([w[:, 0:4096], w[:, 4112:6672], w[:, 4096:4112], w[:, 6672:6688], pad], axis=1)


def _unpermute_in(g):
    return jnp.concatenate([g[:, 0:4096], g[:, 6656:6672], g[:, 4096:6656], g[:, 6672:6688]], axis=1)


def _lane_row(vec, start):
    return jnp.zeros((LANES,), F32).at[start:start + vec.shape[0]].set(vec)


def _cols_from_shards(g):
    return jnp.transpose(g, (1, 0, 2)).reshape(g.shape[1], N_DEV * g.shape[2])


def _cols_to_shards(a):
    return jnp.transpose(a.astype(BF16).reshape(a.shape[0], N_DEV, a.shape[1] // N_DEV), (1, 0, 2))


def _rows_to_shards(a):
    return a.astype(BF16).reshape(N_DEV, a.shape[0] // N_DEV, a.shape[1])


def _local_step(x, tgt, wp_in, rest, p, rest_is_sharded):
    bsz, seq, d = x.shape
    t = bsz * seq
    x2 = x.reshape(t, d)
    tgt2 = tgt.reshape(t, d)
    tm = min(256, seq)
    tm_wide = min(128, seq)
    sb = min(512, seq)

    gp = jnp.zeros((SUBLANES, LANES), F32).at[0].set(_lane_row(p["gdn_a_log"], 8)).at[1].set(_lane_row(p["gdn_dt_bias"], 8))
    sp = jnp.zeros((SUBLANES, LANES), F32).at[0].set(_lane_row(p["ssd_a_log"], 16)).at[1].set(_lane_row(p["ssd_dt_bias"], 16))
    dvec = jnp.repeat(p["ssd_d"], SSD_P).reshape(1, d)
    row = lambda v: v.reshape(1, -1)
    pre_mix, post_mix, pre_ffn, post_ffn = (row(p[k]) for k in ("pre_mix_norm", "post_mix_norm", "pre_ffn_norm", "post_ffn_norm"))
    gnw, snw = row(p["gdn_norm_w"]), row(p["ssd_norm_w"])
    gcw, scw, scb, fcw, fcb = p["gdn_conv_w"], p["ssd_conv_w"], row(p["ssd_conv_b"]), p["ffn_conv_w"], row(p["ffn_conv_b"])

    h1 = _norm_cast("norm_in", x2, pre_mix, tm)
    proj = _matmul("mm_proj", h1, wp_in, "nn", F32)
    b3 = lambda a: a.reshape(bsz, seq, a.shape[-1])
    b2 = lambda a: a.reshape(t, a.shape[-1])
    rows_of = lambda a, lo, n: jnp.transpose(a[:, lo:lo + n].reshape(bsz, seq // CHUNK, CHUNK, n), (0, 3, 1, 2))
    qn, kn, vv, gs = (b3(a) for a in _gdn_prep(proj, gcw, gp, seq, tm))
    gr = rows_of(b2(gs), 8, GDN_HEADS)
    (o_gdn, gdn_st, gdn_ti), gathered = _gdn_chunk_fwd(qn, kn, vv, gs, gr, bsz, seq, sb, list(rest) if rest_is_sharded else [])
    if rest_is_sharded:
        w_out, w_up, w_down = gathered[0].reshape(-1, d), _cols_from_shards(gathered[1]), gathered[2].reshape(-1, d)
    else:
        w_out, w_up, w_down = rest
    o_gdn = b2(o_gdn)
    xs, bc, dtx, acsx, acs = _ssd_prep(proj, scw, scb, sp, seq, tm)
    ar = rows_of(acs, 16, SSD_HEADS)
    y_ssd, ssd_st = _ssd_chunk_fwd(b3(xs), b3(bc), b3(dtx), b3(acsx), b3(acs), ar, bsz, seq, sb)
    y_ssd = b2(y_ssd)
    mixin = _gate_norm(o_gdn, y_ssd, xs, proj, gnw, snw, dvec, tm)
    mix = _matmul("mm_out", mixin, w_out, "nn", F32)
    x1, h2 = _mid(x2, mix, post_mix, pre_ffn, tm)
    u_pre = _matmul("mm_up", h2, w_up, "nn", F32)
    act = _ffn_act(u_pre, fcw, fcb, seq, tm_wide)
    f = _matmul("mm_down", act, w_down, "nn", F32, tk=1408)
    dy, df, loss_lanes, d_post_ffn = _final(x1, f, tgt2, post_ffn, tm)

    g_down = _matmul("mm_dw_down", act, df, "tn", F32, tm=1408)
    dact = _matmul("mm_dact", df, w_down, "nt", F32, tn=1408)
    du_pre, d_fcw, d_fcb = _ffn_bwd(u_pre, dact, fcw, fcb, seq, tm_wide)
    g_up = _matmul("mm_dw_up", h2, du_pre, "tn", F32)
    dh2 = _matmul("mm_dh2", du_pre, w_up, "nt", F32)
    dx1, dmix, d_post_mix, d_pre_ffn = _mid_bwd(x1, mix, dy, dh2, post_mix, pre_ffn, tm)
    g_out = _matmul("mm_dw_out", mixin, dmix, "tn", F32)
    dmixin = _matmul("mm_dmixin", dmix, w_out, "nt", F32)
    do_gdn, dza, dy_ssd, dxs_d, dzs, d_gnw, d_snw, d_dd = _gate_norm_bwd(o_gdn, y_ssd, xs, proj, dmixin, gnw, snw, dvec, tm)
    dxs_c, dbc, ddt, dacs = (b2(a) for a in _ssd_chunk_bwd(
        b3(xs), b3(bc), b3(dtx), b3(acsx), b3(acs), ar, b3(dy_ssd), ssd_st, bsz, seq, sb))
    riders = [_rows_to_shards(g_out), _cols_to_shards(g_up), _rows_to_shards(g_down)] if rest_is_sharded else []
    dgdn, received = _gdn_chunk_bwd(qn, kn, vv, gs, gr, b3(do_gdn), gdn_st, gdn_ti, bsz, seq, sb, riders)
    if rest_is_sharded:
        g_out, g_up, g_down = received
    dqn, dkn, dvv, dgb = (b2(a) for a in dgdn)
    dpre_gdn, dsm_gdn, d_gcw, d_gp = _gdn_prep_bwd(proj, dqn, dkn, dvv, dgb, gcw, gp, seq, tm)
    dpre_ssd, dsm, d_scw, d_scb, d_sp = _ssd_prep_bwd(proj, dxs_c, dxs_d, dbc, ddt, dacs, dsm_gdn, scw, scb, sp, seq, tm)
    dproj = _assemble_dproj(dpre_gdn, dza, dzs, dpre_ssd, dsm, gcw, scw, seq, tm)
    g_in = _matmul("mm_dw_in", h1, dproj, "tn", F32)
    if rest_is_sharded:
        dh1, (g_in,) = _matmul("mm_dh1", dproj, wp_in, "nt", F32, scatter_riders=[_cols_to_shards(_unpermute_in(g_in))])
    else:
        dh1 = _matmul("mm_dh1", dproj, wp_in, "nt", F32)
    dx, d_pre_mix = _first_bwd(x2, dh1, dx1, pre_mix, tm)

    small = dict(pre_mix_norm=d_pre_mix, ssd_norm_w=d_snw, post_mix_norm=d_post_mix, pre_ffn_norm=d_pre_ffn,
                 post_ffn_norm=d_post_ffn, dd_lanes=d_dd, loss_lanes=loss_lanes, gdn_gates=d_gp, ssd_gates=d_sp,
                 gdn_norm_w=d_gnw, gdn_conv_w=d_gcw[0:4], ssd_conv_w=d_scw[0:4], ssd_conv_b=d_scb,
                 ffn_conv_w=d_fcw[0:3], ffn_conv_b=d_fcb)
    return dx.reshape(bsz, seq, d), g_in, g_out, g_up, g_down, small


def kernel(x, pre_mix_norm, w_in, gdn_conv_w, gdn_a_log, gdn_dt_bias, gdn_norm_w, ssd_conv_w, ssd_conv_b, ssd_a_log, ssd_dt_bias, ssd_d, ssd_norm_w, w_out, post_mix_norm, pre_ffn_norm, w_up, ffn_conv_w, ffn_conv_b, w_down, post_ffn_norm, loss_target, m_pre_mix_norm, m_w_in, m_gdn_conv_w, m_gdn_a_log, m_gdn_dt_bias, m_gdn_norm_w, m_ssd_conv_w, m_ssd_conv_b, m_ssd_a_log, m_ssd_dt_bias, m_ssd_d, m_ssd_norm_w, m_w_out, m_post_mix_norm, m_pre_ffn_norm, m_w_up, m_ffn_conv_w, m_ffn_conv_b, m_w_down, m_post_ffn_norm, v_pre_mix_norm, v_w_in, v_gdn_conv_w, v_gdn_a_log, v_gdn_dt_bias, v_gdn_norm_w, v_ssd_conv_w, v_ssd_conv_b, v_ssd_a_log, v_ssd_dt_bias, v_ssd_d, v_ssd_norm_w, v_w_out, v_post_mix_norm, v_pre_ffn_norm, v_w_up, v_ffn_conv_w, v_ffn_conv_b, v_w_down, v_post_ffn_norm):
    names = ["pre_mix_norm", "w_in", "gdn_conv_w", "gdn_a_log", "gdn_dt_bias", "gdn_norm_w", "ssd_conv_w", "ssd_conv_b",
             "ssd_a_log", "ssd_dt_bias", "ssd_d", "ssd_norm_w", "w_out", "post_mix_norm", "pre_ffn_norm", "w_up",
             "ffn_conv_w", "ffn_conv_b", "w_down", "post_ffn_norm"]
    w_args = [pre_mix_norm, w_in, gdn_conv_w, gdn_a_log, gdn_dt_bias, gdn_norm_w, ssd_conv_w, ssd_conv_b, ssd_a_log, ssd_dt_bias, ssd_d, ssd_norm_w, w_out, post_mix_norm, pre_ffn_norm, w_up, ffn_conv_w, ffn_conv_b, w_down, post_ffn_norm]
    m_args = [m_pre_mix_norm, m_w_in, m_gdn_conv_w, m_gdn_a_log, m_gdn_dt_bias, m_gdn_norm_w, m_ssd_conv_w, m_ssd_conv_b, m_ssd_a_log, m_ssd_dt_bias, m_ssd_d, m_ssd_norm_w, m_w_out, m_post_mix_norm, m_pre_ffn_norm, m_w_up, m_ffn_conv_w, m_ffn_conv_b, m_w_down, m_post_ffn_norm]
    v_args = [v_pre_mix_norm, v_w_in, v_gdn_conv_w, v_gdn_a_log, v_gdn_dt_bias, v_gdn_norm_w, v_ssd_conv_w, v_ssd_conv_b, v_ssd_a_log, v_ssd_dt_bias, v_ssd_d, v_ssd_norm_w, v_w_out, v_post_mix_norm, v_pre_ffn_norm, v_w_up, v_ffn_conv_w, v_ffn_conv_b, v_w_down, v_post_ffn_norm]
    w = {k: a[0] for k, a in zip(names, w_args)}
    m = {k: a[0] for k, a in zip(names, m_args)}
    v = {k: a[0] for k, a in zip(names, v_args)}
    idx = 4 * lax.axis_index("x") + 2 * lax.axis_index("y") + lax.axis_index("c")
    big = ("w_in", "w_out", "w_up", "w_down")
    conv = ("gdn_conv_w", "ssd_conv_w", "ffn_conv_w")

    conv_local = jnp.concatenate([jnp.pad(w[k], ((0, 4 - w[k].shape[0]), (0, 0))) for k in conv], axis=1)
    g_in, g_conv = _exchange("gather_weights", [w["w_in"].astype(BF16), conv_local], scatter=False)
    wp_in = _permute_in(_cols_from_shards(g_in))
    p = {k: w[k] for k in names if k not in big and k not in conv}
    off = 0
    for k in conv:
        cw = w[k].shape[1]
        p[k] = jnp.transpose(g_conv[:, :w[k].shape[0], off:off + cw], (1, 0, 2)).reshape(w[k].shape[0], N_DEV * cw)
        off += cw

    rest = tuple(w[k].astype(BF16) for k in ("w_out", "w_up", "w_down"))
    dx, p_in, p_out, p_up, p_down, small = _local_step(x, loss_target, wp_in, rest, p, True)

    gate_row = jnp.concatenate([small["gdn_gates"][0], small["gdn_gates"][1], small["ssd_gates"][0], small["ssd_gates"][1],
                                small["gdn_norm_w"][0], jnp.zeros((D_MODEL - 5 * LANES,), F32)]).reshape(1, D_MODEL)
    pack = _pack([small["pre_mix_norm"], small["ssd_norm_w"], small["post_mix_norm"], small["pre_ffn_norm"],
                  small["post_ffn_norm"], small["dd_lanes"], small["loss_lanes"], gate_row,
                  small["gdn_conv_w"], small["ssd_conv_w"], jnp.pad(small["ssd_conv_b"], ((0, 0), (0, 512))),
                  jnp.pad(small["ffn_conv_w"].reshape(-1), (0, 17 * D_MODEL - 3 * 2 * D_FF)),
                  jnp.pad(small["ffn_conv_b"], ((0, 0), (0, 512)))], SMALL_ROWS)
    (pack_all,) = _exchange("gather_small", [pack], scatter=False)
    ssum, extra = _small_sum(pack_all)

    grads, deltas, new_m, new_v = {}, {}, {}, {}
    for k, parts in (("w_in", p_in), ("w_out", p_out), ("w_up", p_up), ("w_down", p_down)):
        grads[k], deltas[k], new_m[k], new_v[k] = _adam_big("adam_" + k, parts, w[k], m[k], v[k], 256)

    flat = ssum.reshape(-1)
    gate = ssum[7]
    sg = dict(pre_mix_norm=ssum[0], ssd_norm_w=ssum[1], post_mix_norm=ssum[2], pre_ffn_norm=ssum[3], post_ffn_norm=ssum[4],
              gdn_a_log=gate[8:16], gdn_dt_bias=gate[LANES + 8:LANES + 16], ssd_a_log=gate[2 * LANES + 16:2 * LANES + 32],
              ssd_dt_bias=gate[3 * LANES + 16:3 * LANES + 32], gdn_norm_w=gate[4 * LANES:5 * LANES], ssd_d=extra[0, 0:SSD_HEADS])
    o = 8 * D_MODEL
    full_gcw = flat[o:o + 4 * 3072].reshape(4, 3072)
    o += 12 * D_MODEL
    full_scw = flat[o:o + 4 * 1536].reshape(4, 1536)
    o += 6 * D_MODEL
    sg["ssd_conv_b"] = flat[o:o + 1536]
    o += 2 * D_MODEL
    full_fcw = flat[o:o + 3 * 2 * D_FF].reshape(3, 2 * D_FF)
    o += 17 * D_MODEL
    sg["ffn_conv_b"] = flat[o:o + 2 * D_FF]
    for k, full in (("gdn_conv_w", full_gcw), ("ssd_conv_w", full_scw), ("ffn_conv_w", full_fcw)):
        cw = w[k].shape[1]
        sg[k] = lax.dynamic_slice_in_dim(full, idx * cw, cw, axis=1)
    small_names = [k for k in names if k not in big]
    rows = 24
    gpk = _pack([sg[k] for k in small_names], rows)
    dpk, mpk, vpk = _adam_small(gpk, _pack([w[k] for k in small_names], rows), _pack([m[k] for k in small_names], rows),
                                _pack([v[k] for k in small_names], rows))
    shapes = [w[k].shape for k in small_names]
    for k, g_, d_, m_, v_ in zip(small_names, _unpack(gpk, shapes), _unpack(dpk, shapes), _unpack(mpk, shapes), _unpack(vpk, shapes)):
        grads[k], deltas[k], new_m[k], new_v[k] = g_, d_, m_, v_

    loss = extra[1, 0]
    lead = lambda a: a[None]
    return (loss, dx, *[lead(grads[k]) for k in names], *[lead(deltas[k]) for k in names],
            *[lead(new_m[k]) for k in names], *[lead(new_v[k]) for k in names])
```

```python
import functools

import jax
import jax.numpy as jnp
from jax import lax
from jax.experimental import pallas as pl
from jax.experimental.pallas import tpu as pltpu

F32 = jnp.float32
BF16 = jnp.bfloat16
MXU_DTYPE = jnp.bfloat16
HIGHEST = lax.Precision.HIGHEST
VMEM_LIMIT_V7X = 48 * 1024 * 1024
SUBLANES = 8
LANES = 128

D_MODEL = 1024
GDN_HEADS = 8
GDN_DK = 128
SSD_HEADS = 16
SSD_P = 64
SSD_GROUPS = 2
SSD_HPG = 8
SSD_N = 128
CHUNK = 64
D_FF = 2816
EPS = 1e-6
N_DEV = 8
PROJ_W = 7168
SMALL_CB = 52
D_IN = 6688

ADAM_LR = 0.001
ADAM_B1 = 0.9
ADAM_B2 = 0.999
ADAM_EPS = 1e-08
ADAM_WD = 0.01
ADAM_STEP = 10

NN = (((1,), (0,)), ((), ()))
NT = (((1,), (1,)), ((), ()))
TN = (((0,), (0,)), ((), ()))


def _pcall(body, **kw):
    return pl.pallas_call(body, **kw)


def _mm(a, b, dims=NN):
    return lax.dot_general(a.astype(MXU_DTYPE), b.astype(MXU_DTYPE), dims, preferred_element_type=F32)


def _mmx(a, b, dims=NN):
    return lax.dot_general(a, b, dims, precision=HIGHEST, preferred_element_type=F32)


def _split(a):
    hi = a.astype(MXU_DTYPE)
    return hi, (a - hi.astype(F32)).astype(MXU_DTYPE)


def _mm3(a, b, dims=NN):
    (ah, al), (bh, bl) = _split(a), _split(b)
    dot = lambda p, q: lax.dot_general(p, q, dims, preferred_element_type=F32)
    return dot(ah, bh) + (dot(ah, bl) + dot(al, bh))


def _mmsel(a, sel, dims=NN):
    ah, al = _split(a)
    s = sel.astype(MXU_DTYPE)
    return (lax.dot_general(ah, s, dims, preferred_element_type=F32)
            + lax.dot_general(al, s, dims, preferred_element_type=F32))


def _sigmoid(x):
    return 0.5 * jnp.tanh(0.5 * x) + 0.5


def _softplus(x):
    return jnp.maximum(x, 0.0) + jnp.log(1.0 + jnp.exp(-jnp.abs(x)))


def _dsilu(x, s):
    return s * (1.0 + x * (1.0 - s))


def _rowsum(x):
    return jnp.sum(x, axis=1, keepdims=True)


def _colsum(x):
    return jnp.sum(x, axis=0, keepdims=True)


def _pick(dim, pref):
    if dim <= pref:
        return dim
    best = None
    t = LANES
    while t <= pref:
        if dim % t == 0:
            best = t
        t += LANES
    return dim if best is None else best


def _params(sem):
    return pltpu.CompilerParams(dimension_semantics=sem, vmem_limit_bytes=VMEM_LIMIT_V7X)


def _matmul(name, a, b, mode, out_dtype, tm=1024, tn=1024, tk=1024, scatter_riders=()):
    if mode == "nn":
        (m, k), (_, n) = a.shape, b.shape
    elif mode == "nt":
        (m, k), (n, _) = a.shape, b.shape
    else:
        (k, m), (_, n) = a.shape, b.shape
    tm, tn, tk = _pick(m, tm), _pick(n, tn), _pick(k, tk)
    nk = k // tk
    if mode == "tn":
        a_spec = pl.BlockSpec((tk, tm), lambda i, j, kk: (kk, i))
    else:
        a_spec = pl.BlockSpec((tm, tk), lambda i, j, kk: (i, kk))
    if mode == "nt":
        b_spec = pl.BlockSpec((tn, tk), lambda i, j, kk: (j, kk))
    else:
        b_spec = pl.BlockSpec((tk, tn), lambda i, j, kk: (kk, j))
    dims = {"nn": NN, "nt": NT, "tn": TN}[mode]

    def body(a_ref, b_ref, o_ref, *acc):
        if nk == 1:
            o_ref[...] = _mm(a_ref[...], b_ref[...], dims).astype(out_dtype)
            return
        kk = pl.program_id(2)

        @pl.when(kk == 0)
        def _():
            acc[0][...] = jnp.zeros_like(acc[0])

        acc[0][...] += _mm(a_ref[...], b_ref[...], dims)

        @pl.when(kk == nk - 1)
        def _():
            o_ref[...] = acc[0][...].astype(out_dtype)

    grid = (m // tm, n // tn, nk)
    riders = list(scatter_riders)
    any_spec, rider_shapes, rider_sems, wrap = _riding_exchange(riders, True, 2, 1, grid)
    res = _pcall(
        wrap(body), name=name, grid=grid,
        in_specs=[a_spec, b_spec] + any_spec,
        out_specs=[pl.BlockSpec((tm, tn), lambda i, j, kk: (i, j))] + any_spec,
        out_shape=[jax.ShapeDtypeStruct((m, n), out_dtype)] + rider_shapes,
        scratch_shapes=([pltpu.VMEM((tm, tn), F32)] if nk > 1 else []) + rider_sems,
        compiler_params=_params(("arbitrary", "arbitrary", "arbitrary") if riders else ("parallel", "parallel", "arbitrary")),
    )(a, b, *riders)
    return (res[0], res[1:]) if riders else res[0]


def _rowwise(name, body, n_rows, tm, ins, outs, accs=()):
    arrays, in_specs = [], []
    last8 = n_rows // SUBLANES - 1
    per = tm // SUBLANES
    for spec in ins:
        kind, arr = spec[0], spec[1]
        if kind == "full":
            in_specs.append(pl.BlockSpec(arr.shape, lambda i, nd=arr.ndim: (0,) * nd))
        else:
            w, cb = spec[2], spec[3]
            if kind == "row":
                in_specs.append(pl.BlockSpec((tm, w), lambda i, cb=cb: (i, cb)))
            elif kind == "prev":
                in_specs.append(pl.BlockSpec((SUBLANES, w), lambda i, cb=cb: (jnp.maximum(i * per - 1, 0), cb)))
            else:
                in_specs.append(pl.BlockSpec((SUBLANES, w), lambda i, cb=cb: (jnp.minimum((i + 1) * per, last8), cb)))
        arrays.append(arr)
    out_shape = [jax.ShapeDtypeStruct((n_rows, w), dt) for (w, dt) in outs]
    out_shape += [jax.ShapeDtypeStruct(s, F32) for s in accs]
    out_specs = [pl.BlockSpec((tm, w), lambda i: (i, 0)) for (w, _) in outs]
    out_specs += [pl.BlockSpec(s, lambda i: (0, 0)) for s in accs]
    n_io = len(ins) + len(outs)

    def kern(*refs):
        i = pl.program_id(0)
        if accs:
            @pl.when(i == 0)
            def _():
                for r in refs[n_io:]:
                    r[...] = jnp.zeros_like(r)
        body(i, *refs)

    res = _pcall(
        kern, name=name, grid=(n_rows // tm,), in_specs=in_specs, out_specs=out_specs, out_shape=out_shape,
        compiler_params=_params(("arbitrary",)),
    )(*arrays)
    return res


def _shift_down(x, halo, j):
    r = pltpu.roll(x, j, 0)
    hr = pltpu.roll(halo, j, 0)
    rows = lax.broadcasted_iota(jnp.int32, (SUBLANES, x.shape[1]), 0)
    top = jnp.where(rows < j, hr, r[0:SUBLANES])
    return jnp.concatenate([top, r[SUBLANES:]], axis=0)


def _shift_up(x, halo, j):
    tm = x.shape[0]
    r = pltpu.roll(x, tm - j, 0)
    hr = pltpu.roll(halo, SUBLANES - j, 0)
    rows = lax.broadcasted_iota(jnp.int32, (SUBLANES, x.shape[1]), 0)
    bot = jnp.where(rows >= SUBLANES - j, hr, r[tm - SUBLANES:])
    return jnp.concatenate([r[:tm - SUBLANES], bot], axis=0)


def _conv_taps(x, halo, kw):
    return [x if kw - 1 - k == 0 else _shift_down(x, halo, kw - 1 - k) for k in range(kw)]


def _conv(taps, w):
    y = taps[0] * w[0:1]
    for k in range(1, len(taps)):
        y = y + taps[k] * w[k:k + 1]
    return y


def _rms(x, width):
    r = lax.rsqrt(jnp.sum(x * x, axis=-1, keepdims=True) * (1.0 / width) + EPS)
    return x * r, r


def _rms_bwd(xh, r, dxh, width):
    return r * (dxh - xh * (jnp.sum(dxh * xh, axis=-1, keepdims=True) * (1.0 / width)))


def _seq_flags(i, seq, tm):
    nps = seq // tm
    pos = i % nps
    return jnp.where(pos == 0, 0.0, 1.0), jnp.where(pos == nps - 1, 0.0, 1.0)


def _norm_cast(name, x, w, tm):
    t, d = x.shape

    def body(i, x_ref, w_ref, h_ref):
        xh, _ = _rms(x_ref[...], d)
        h_ref[...] = (xh * w_ref[...]).astype(BF16)

    return _rowwise(name, body, t, tm, [("row", x, d, 0), ("full", w)], [(d, BF16)])[0]


def _gdn_prep(proj, cw, gp, seq, tm):
    t = proj.shape[0]
    d = D_MODEL

    def body(i, q_ref, qh_ref, k_ref, kh_ref, v_ref, vh_ref, sm_ref, cw_ref, gp_ref, qn_ref, kn_ref, vv_ref, gs_ref):
        keep, _ = _seq_flags(i, seq, tm)
        for x_ref, h_ref, o_ref, off, scale in ((q_ref, qh_ref, qn_ref, 0, GDN_DK ** -0.5),
                                               (k_ref, kh_ref, kn_ref, d, 1.0), (v_ref, vh_ref, vv_ref, 2 * d, None)):
            y = _conv(_conv_taps(x_ref[...], h_ref[...] * keep, 4), cw_ref[:, off:off + d])
            a = y * _sigmoid(y)
            if scale is None:
                o_ref[...] = a
            else:
                for hh in range(GDN_HEADS):
                    s = a[:, hh * GDN_DK:(hh + 1) * GDN_DK]
                    n = lax.rsqrt(_rowsum(s * s) + EPS)
                    o_ref[:, hh * GDN_DK:(hh + 1) * GDN_DK] = s * (n * scale)
        sm = sm_ref[...]
        lane = lax.broadcasted_iota(jnp.int32, sm.shape, 1)
        beta = _sigmoid(sm)
        g = jnp.where((lane >= 8) & (lane < 16), -jnp.exp(gp_ref[0:1, :]) * _softplus(sm + gp_ref[1:2, :]), 0.0)
        gs_ref[...] = jnp.where(lane < 8, beta, _mmx(_block_tri(tm, False), g))

    ins = []
    for cb in range(3):
        ins += [("row", proj, d, cb), ("prev", proj, d, cb)]
    ins += [("row", proj, LANES, SMALL_CB), ("full", cw), ("full", gp)]
    return _rowwise("gdn_prep", body, t, tm, ins, [(d, F32), (d, F32), (d, F32), (LANES, F32)])


def _block_tri(tm, upper):
    ri = lax.broadcasted_iota(jnp.int32, (tm, tm), 0)
    ci = lax.broadcasted_iota(jnp.int32, (tm, tm), 1)
    tri = (ri <= ci) if upper else (ri >= ci)
    return (tri & ((ri // CHUNK) == (ci // CHUNK))).astype(F32)


def _chunk_consts():
    row = lax.broadcasted_iota(jnp.int32, (CHUNK, CHUNK), 0)
    col = lax.broadcasted_iota(jnp.int32, (CHUNK, CHUNK), 1)
    return dict(
        tril=row >= col, strict=row > col, eye=(row == col).astype(F32),
        lane=lax.broadcasted_iota(jnp.int32, (CHUNK, LANES), 1),
        row1=lax.broadcasted_iota(jnp.int32, (CHUNK, 1), 0),
        ones=jnp.ones((CHUNK, LANES), F32))


def _hmap(fn, *lists):
    return [fn(*a) for a in zip(*lists)]


def _tri_inv(nmats, eye):
    x = [eye - n for n in nmats]
    p = _hmap(_mm3, nmats, nmats)
    for lvl in range(5):
        x = _hmap(lambda xi, pi: xi + _mm3(xi, pi), x, p)
        if lvl < 4:
            p = _hmap(_mm3, p, p)
    return x


def _gdn_gates(gs, gc_row, h, c):
    beta = _rowsum(jnp.where(c["lane"] == h, gs, 0.0))
    gc = _rowsum(jnp.where(c["lane"] == h + 8, gs, 0.0))
    dc = jnp.exp(jnp.where(c["tril"], gc - gc_row, -1e30))
    gl = gc[CHUNK - 1:CHUNK, :]
    return beta, dc, jnp.exp(gc), jnp.exp(gl), jnp.exp(gl - gc)


GDN_HB = GDN_HEADS


def _gdn_specs(seq, sb, hb, backward):
    assert hb == GDN_HEADS
    nsb = seq // sb
    ncb = sb // CHUNK
    order = (lambda j: nsb - 1 - j) if backward else (lambda j: j)
    specs = dict(
        wide=lambda: pl.BlockSpec((1, sb, hb * GDN_DK), lambda b, h, j: (b, order(j), h)),
        gs=pl.BlockSpec((1, sb, LANES), lambda b, h, j: (b, order(j), 0)),
        gr=pl.BlockSpec((1, ncb, GDN_HEADS, CHUNK), lambda b, h, j: (b, order(j), 0, 0)),
        st=pl.BlockSpec((1, hb, ncb * GDN_DK, GDN_DK), lambda b, h, j: (b, h, order(j), 0)),
        ti=pl.BlockSpec((1, hb, sb, CHUNK), lambda b, h, j: (b, h, order(j), 0)))
    return nsb, ncb, specs


def _riding_exchange(arrays, scatter, n_in, n_out, grid):
    n = len(arrays)
    if n == 0:
        return [], [], [], lambda body: body
    any_spec = [pl.BlockSpec(memory_space=pl.ANY)] * n

    def wrap(body):
        def wrapped(*refs):
            ins = refs[n_in:n_in + n]
            outs = refs[n_in + n + n_out:n_in + 2 * n + n_out]
            sems = refs[len(refs) - 3:]
            pid = [pl.program_id(a) for a in range(len(grid))]
            first = functools.reduce(lambda a, b: a & b, [p == 0 for p in pid])
            last = functools.reduce(lambda a, b: a & b, [p == g - 1 for p, g in zip(pid, grid)])

            @pl.when(first)
            def _():
                _exchange_phase(ins, outs, sems, scatter, start=True)

            body(*refs[:n_in], *refs[n_in + n:n_in + n + n_out], *refs[n_in + 2 * n + n_out:len(refs) - 3])

            @pl.when(last)
            def _():
                _exchange_phase(ins, outs, sems, scatter, start=False)

        return wrapped

    return any_spec, _exchange_out_shapes(arrays, scatter), _exchange_sems(n), wrap


def _gdn_chunk_fwd(qn, kn, vv, gs, gr, bsz, seq, sb, riders):
    hb = GDN_HB
    nsb, ncb, sp = _gdn_specs(seq, sb, hb, False)
    grid = (bsz, GDN_HEADS // hb, nsb)
    any_spec, rider_shapes, rider_sems, wrap = _riding_exchange(riders, False, 5, 3, grid)

    def body(q_ref, k_ref, v_ref, gs_ref, gr_ref, o_ref, st_ref, ti_ref, s_scr):
        hg = pl.program_id(1)

        @pl.when(pl.program_id(2) == 0)
        def _():
            s_scr[...] = jnp.zeros_like(s_scr)

        c = _chunk_consts()

        def chunk(n, carry):
            r = pl.ds(pl.multiple_of(n * CHUNK, CHUNK), CHUNK)
            rs = pl.ds(pl.multiple_of(n * GDN_DK, GDN_DK), GDN_DK)
            gsv = gs_ref[0, r, :]
            heads = list(range(hb))
            sls = [slice(ih * GDN_DK, (ih + 1) * GDN_DK) for ih in heads]
            q = [q_ref[0, r, sl] for sl in sls]
            k = [k_ref[0, r, sl] for sl in sls]
            v = [v_ref[0, r, sl] for sl in sls]
            beta, dc, eg, egl, ekd = zip(*[
                _gdn_gates(gsv, gr_ref[0, n, pl.ds(ih, 1), :], ih, c) for ih in heads])
            kb = _hmap(lambda a, b: a * b, k, beta)
            amat = _hmap(lambda a, b, d_: jnp.where(c["strict"], _mm(a, b, NT) * d_, 0.0), kb, k, dc)
            tinv = _tri_inv(amat, c["eye"])
            u = _hmap(lambda t_, a, b: _mm3(t_, a * b), tinv, v, beta)
            w = _hmap(lambda t_, a, b: _mm3(t_, a * b), tinv, kb, eg)
            qk = _hmap(lambda a, b, d_: _mm(a, b, NT) * d_, q, k, dc)
            s = [s_scr[ih] for ih in heads]
            v_new = _hmap(lambda a, b, s_: a - _mm(b, s_), u, w, s)
            o = _hmap(lambda a, e, s_, qk_, vn: _mm(a * e, s_) + _mm(qk_, vn), q, eg, s, qk, v_new)
            s_new = _hmap(lambda s_, e, a, f, vn: s_ * e + _mm(a * f, vn, TN), s, egl, k, ekd, v_new)
            for ih in heads:
                o_ref[0, r, sls[ih]] = o[ih]
                st_ref[0, ih, rs, :] = s[ih]
                ti_ref[0, ih, r, :] = tinv[ih]
                s_scr[ih] = s_new[ih]
            return carry

        lax.fori_loop(0, ncb, chunk, 0)

    t3 = (bsz, seq, D_MODEL)
    res = _pcall(
        wrap(body), name="gdn_chunk_fwd", grid=grid,
        in_specs=[sp["wide"](), sp["wide"](), sp["wide"](), sp["gs"], sp["gr"]] + any_spec,
        out_specs=[sp["wide"](), sp["st"], sp["ti"]] + any_spec,
        out_shape=[jax.ShapeDtypeStruct(t3, F32),
                   jax.ShapeDtypeStruct((bsz, GDN_HEADS, (seq // CHUNK) * GDN_DK, GDN_DK), F32),
                   jax.ShapeDtypeStruct((bsz, GDN_HEADS, seq, CHUNK), F32)] + rider_shapes,
        scratch_shapes=[pltpu.VMEM((hb, GDN_DK, GDN_DK), F32)] + rider_sems,
        compiler_params=_params(("arbitrary", "arbitrary", "arbitrary")),
    )(qn, kn, vv, gs, gr, *riders)
    return res[:3], res[3:]


def _ssd_prep(proj, cw, cb, sp, seq, tm):
    t = proj.shape[0]
    d = D_MODEL
    ssd_w = SSD_HEADS * SSD_P

    def body(i, x_ref, xh_ref, bc_ref, bch_ref, sm_ref, cw_ref, cb_ref, sp_ref, xs_ref, bco_ref, dtx_ref, acsx_ref, acs_ref):
        keep, _ = _seq_flags(i, seq, tm)
        y = _conv(_conv_taps(x_ref[...], xh_ref[...] * keep, 4), cw_ref[:, 0:d]) + cb_ref[:, 0:d]
        xs_ref[...] = y * _sigmoid(y)
        y = _conv(_conv_taps(bc_ref[...], bch_ref[...] * keep, 4), cw_ref[:, d:d + 512]) + cb_ref[:, d:d + 512]
        bco_ref[...] = y * _sigmoid(y)
        sm = sm_ref[...]
        lane = lax.broadcasted_iota(jnp.int32, sm.shape, 1)
        valid = (lane >= 16) & (lane < 32)
        dt = jnp.where(valid, _softplus(sm + sp_ref[1:2, :]), 0.0)
        adt = dt * (-jnp.exp(sp_ref[0:1, :]))
        acs = _mmx(_block_tri(tm, False), adt)
        l64 = lax.broadcasted_iota(jnp.int32, (LANES, ssd_w), 0)
        d64 = lax.broadcasted_iota(jnp.int32, (LANES, ssd_w), 1)
        e64 = (l64 - 16 == d64 // SSD_P).astype(F32)
        dtx_ref[...] = _mmx(dt, e64)
        acsx_ref[...] = _mmx(acs, e64)
        acs_ref[...] = acs

    ins = [("row", proj, d, 5), ("prev", proj, d, 5), ("row", proj, 512, 12), ("prev", proj, 512, 12),
           ("row", proj, LANES, SMALL_CB), ("full", cw), ("full", cb), ("full", sp)]
    return _rowwise("ssd_prep", body, t, tm, ins, [(d, F32), (512, F32), (ssd_w, F32), (ssd_w, F32), (LANES, F32)])


SSD_GW = SSD_HPG * SSD_P


def _ssd_head(acs, ar_ref, n, head, cbm, c):
    col = _rowsum(jnp.where(c["lane"] == head + 16, acs, 0.0))
    lm = jnp.exp(jnp.where(c["tril"], col - ar_ref[0, head, pl.ds(n, 1), :], -1e30))
    return lm, cbm * lm


def _ssd_specs(seq, sb):
    nsb = seq // sb
    ncb = sb // CHUNK
    def specs(order):
        return dict(
            wide=lambda: pl.BlockSpec((1, sb, SSD_HEADS * SSD_P), lambda b, j: (b, order(j), 0)),
            bc=lambda: pl.BlockSpec((1, sb, 2 * SSD_GROUPS * SSD_N), lambda b, j: (b, order(j), 0)),
            half=lambda: pl.BlockSpec((1, sb, SSD_GROUPS * SSD_N), lambda b, j: (b, order(j), 0)),
            small=lambda: pl.BlockSpec((1, sb, LANES), lambda b, j: (b, order(j), 0)),
            ar=pl.BlockSpec((1, SSD_HEADS, ncb, CHUNK), lambda b, j: (b, 0, order(j), 0)),
            st=pl.BlockSpec((1, ncb * SSD_N, SSD_HEADS * SSD_P), lambda b, j: (b, order(j), 0)))
    return nsb, ncb, specs(lambda j: j), specs(lambda j: nsb - 1 - j)


def _ssd_chunk_fwd(xs, bc, dtx, acsx, acs, ar, bsz, seq, sb):
    nsb, ncb, sp, _ = _ssd_specs(seq, sb)

    def body(x_ref, dtx_ref, ax_ref, bc_ref, acs_ref, ar_ref, y_ref, sts_ref, st_scr):
        @pl.when(pl.program_id(1) == 0)
        def _():
            st_scr[...] = jnp.zeros_like(st_scr)

        c = _chunk_consts()
        lane5 = lax.broadcasted_iota(jnp.int32, (CHUNK, SSD_GW), 1) // SSD_P

        def chunk(n, carry):
            r = pl.ds(pl.multiple_of(n * CHUNK, CHUNK), CHUNK)
            rs = pl.ds(pl.multiple_of(n * SSD_N, SSD_N), SSD_N)
            acsv = acs_ref[0, r, :]
            for g in range(SSD_GROUPS):
                gl = slice(g * SSD_GW, (g + 1) * SSD_GW)
                x, dt, ax = x_ref[0, r, gl], dtx_ref[0, r, gl], ax_ref[0, r, gl]
                bm = bc_ref[0, r, g * SSD_N:(g + 1) * SSD_N]
                cm = bc_ref[0, r, (SSD_GROUPS + g) * SSD_N:(SSD_GROUPS + g + 1) * SSD_N]
                xdt = x * dt
                cbm = _mm(cm, bm, NT)
                al = ax[CHUNK - 1:CHUNK, :]
                st = st_scr[:, gl]
                y = _mm(cm, st) * jnp.exp(ax)
                for hh in range(SSD_HPG):
                    _, gm = _ssd_head(acsv, ar_ref, n, g * SSD_HPG + hh, cbm, c)
                    y = y + _mm(gm, jnp.where(lane5 == hh, xdt, 0.0))
                y_ref[0, r, gl] = y
                sts_ref[0, rs, gl] = st
                st_scr[:, gl] = st * jnp.exp(al) + _mm(bm, xdt * jnp.exp(al - ax), TN)
            return carry

        lax.fori_loop(0, ncb, chunk, 0)

    return _pcall(
        body, name="ssd_chunk_fwd", grid=(bsz, nsb),
        in_specs=[sp["wide"](), sp["wide"](), sp["wide"](), sp["bc"](), sp["small"](), sp["ar"]],
        out_specs=[sp["wide"](), sp["st"]],
        out_shape=[jax.ShapeDtypeStruct((bsz, seq, SSD_HEADS * SSD_P), F32),
                   jax.ShapeDtypeStruct((bsz, (seq // CHUNK) * SSD_N, SSD_HEADS * SSD_P), F32)],
        scratch_shapes=[pltpu.VMEM((SSD_N, SSD_HEADS * SSD_P), F32)],
        compiler_params=_params(("parallel", "arbitrary")),
    )(xs, dtx, acsx, bc, acs, ar)


def _gate_norm(o_gdn, y_ssd, xs, proj, gnw, snw, dvec, tm):
    t = o_gdn.shape[0]
    d = D_MODEL

    def body(i, o_ref, za_ref, y_ref, xs_ref, zs_ref, gnw_ref, snw_ref, dv_ref, out_ref):
        for hh in range(GDN_HEADS):
            sl = slice(hh * GDN_DK, (hh + 1) * GDN_DK)
            oh, _ = _rms(o_ref[:, sl], GDN_DK)
            z = za_ref[:, sl]
            out_ref[:, sl] = (oh * gnw_ref[...] * (z * _sigmoid(z))).astype(BF16)
        zs = zs_ref[...]
        yg = (y_ref[...] + dv_ref[...] * xs_ref[...]) * (zs * _sigmoid(zs))
        for g in range(SSD_GROUPS):
            sl = slice(g * 512, (g + 1) * 512)
            yh, _ = _rms(yg[:, sl], 512)
            out_ref[:, d + g * 512:d + (g + 1) * 512] = (yh * snw_ref[:, sl]).astype(BF16)

    ins = [("row", o_gdn, d, 0), ("row", proj, d, 3), ("row", y_ssd, d, 0), ("row", xs, d, 0), ("row", proj, d, 4),
           ("full", gnw), ("full", snw), ("full", dvec)]
    return _rowwise("gate_norm", body, t, tm, ins, [(2 * d, BF16)])[0]


def _mid(x, mix, pmw, pfw, tm):
    t, d = x.shape

    def body(i, x_ref, mix_ref, pmw_ref, pfw_ref, x1_ref, h2_ref):
        mh, _ = _rms(mix_ref[...], d)
        x1 = x_ref[...] + mh * pmw_ref[...]
        x1_ref[...] = x1
        xh, _ = _rms(x1, d)
        h2_ref[...] = (xh * pfw_ref[...]).astype(BF16)

    return _rowwise("mid", body, t, tm, [("row", x, d, 0), ("row", mix, d, 0), ("full", pmw), ("full", pfw)],
                    [(d, F32), (d, BF16)])


def _ffn_gate_up(ug_ref, ugh_ref, uu_ref, uuh_ref, cw_ref, cb_ref, keep):
    tg = _conv_taps(ug_ref[...], ugh_ref[...] * keep, 3)
    tu = _conv_taps(uu_ref[...], uuh_ref[...] * keep, 3)
    gate = _conv(tg, cw_ref[:, 0:D_FF]) + cb_ref[:, 0:D_FF]
    up = _conv(tu, cw_ref[:, D_FF:2 * D_FF]) + cb_ref[:, D_FF:2 * D_FF]
    return tg, tu, gate, up


def _ffn_act(u_pre, cw, cb, seq, tm):
    t = u_pre.shape[0]

    def body(i, ug_ref, ugh_ref, uu_ref, uuh_ref, cw_ref, cb_ref, act_ref):
        keep, _ = _seq_flags(i, seq, tm)
        _, _, gate, up = _ffn_gate_up(ug_ref, ugh_ref, uu_ref, uuh_ref, cw_ref, cb_ref, keep)
        act_ref[...] = (gate * _sigmoid(gate) * up).astype(BF16)

    ins = [("row", u_pre, D_FF, 0), ("prev", u_pre, D_FF, 0), ("row", u_pre, D_FF, 1), ("prev", u_pre, D_FF, 1),
           ("full", cw), ("full", cb)]
    return _rowwise("ffn_act", body, t, tm, ins, [(D_FF, BF16)])[0]


def _final(x1, f, tgt, w, tm):
    t, d = x1.shape

    def body(i, x1_ref, f_ref, t_ref, w_ref, dy_ref, df_ref, loss_ref, dw_ref):
        fh, r = _rms(f_ref[...], d)
        e = x1_ref[...] + fh * w_ref[...] - t_ref[...]
        loss_ref[...] += _colsum(e * e) * (0.5 / d)
        dy = e * (1.0 / d)
        dy_ref[...] = dy
        dw_ref[...] += _colsum(dy * fh)
        df_ref[...] = _rms_bwd(fh, r, dy * w_ref[...], d).astype(BF16)

    return _rowwise("final", body, t, tm, [("row", x1, d, 0), ("row", f, d, 0), ("row", tgt, d, 0), ("full", w)],
                    [(d, F32), (d, BF16)], accs=[(1, d), (1, d)])


def _ffn_bwd(u_pre, dact, cw, cb, seq, tm):
    t = u_pre.shape[0]

    def body(i, ug_ref, ugh_ref, ugn_ref, uu_ref, uuh_ref, uun_ref, da_ref, dan_ref, cw_ref, cb_ref, dpre_ref, dcw_ref, dcb_ref):
        keep, keep_next = _seq_flags(i, seq, tm)
        ext = lambda a_ref, n_ref: jnp.concatenate([a_ref[...], n_ref[...]], axis=0)
        rows = tm + SUBLANES
        tg = _conv_taps(ext(ug_ref, ugn_ref), ugh_ref[...] * keep, 3)
        tu = _conv_taps(ext(uu_ref, uun_ref), uuh_ref[...] * keep, 3)
        gate = _conv(tg, cw_ref[:, 0:D_FF]) + cb_ref[:, 0:D_FF]
        up = _conv(tu, cw_ref[:, D_FF:2 * D_FF]) + cb_ref[:, D_FF:2 * D_FF]
        sg = _sigmoid(gate)
        da = jnp.concatenate([da_ref[...], dan_ref[...] * keep_next], axis=0)
        for off, grad, taps in ((0, da * up * _dsilu(gate, sg), tg), (D_FF, da * gate * sg, tu)):
            own = grad[0:tm]
            acc = own * cw_ref[2:3, off:off + D_FF]
            for j in (1, 2):
                acc = acc + pltpu.roll(grad, rows - j, 0)[0:tm] * cw_ref[2 - j:3 - j, off:off + D_FF]
            dpre_ref[:, off:off + D_FF] = acc.astype(BF16)
            dcb_ref[:, off:off + D_FF] += _colsum(own)
            for k in range(3):
                dcw_ref[k:k + 1, off:off + D_FF] += _colsum(own * taps[k][0:tm])

    ins = []
    for cb_ in range(2):
        ins += [("row", u_pre, D_FF, cb_), ("prev", u_pre, D_FF, cb_), ("next", u_pre, D_FF, cb_)]
    ins += [("row", dact, D_FF, 0), ("next", dact, D_FF, 0), ("full", cw), ("full", cb)]
    return _rowwise("ffn_bwd", body, t, tm, ins, [(2 * D_FF, BF16)], accs=[(SUBLANES, 2 * D_FF), (1, 2 * D_FF)])


def _assemble_dproj(dpre_gdn, dza, dzs, dpre_ssd, dsm, gcw, scw, seq, tm):
    t = dpre_gdn.shape[0]
    d = D_MODEL

    def body(i, dg_ref, dgn_ref, dza_ref, dzs_ref, ds_ref, dsn_ref, dsm_ref, gcw_ref, scw_ref, o_ref):
        _, keep = _seq_flags(i, seq, tm)
        pieces = [(dg_ref, dgn_ref, gcw_ref, 0, c0) for c0 in range(0, 3 * d, d)]
        pieces += [(ds_ref, dsn_ref, scw_ref, 5 * d, c0) for c0 in (0, d)]
        for d_ref, n_ref, cw_ref, base, c0 in pieces:
            w = min(d, d_ref.shape[1] - c0)
            x = d_ref[:, c0:c0 + w]
            halo = n_ref[:, c0:c0 + w] * keep
            acc = x * cw_ref[3:4, c0:c0 + w]
            for j in range(1, 4):
                acc = acc + _shift_up(x, halo, j) * cw_ref[3 - j:4 - j, c0:c0 + w]
            o_ref[:, base + c0:base + c0 + w] = acc.astype(BF16)
        o_ref[:, 3 * d:4 * d] = dza_ref[...]
        o_ref[:, 4 * d:5 * d] = dzs_ref[...]
        o_ref[:, 6 * d + 512:6 * d + 512 + LANES] = dsm_ref[...]
        o_ref[:, 6 * d + 512 + LANES:PROJ_W] = jnp.zeros((tm, PROJ_W - (6 * d + 512 + LANES)), BF16)

    ins = [("row", dpre_gdn, 3 * d, 0), ("next", dpre_gdn, 3 * d, 0), ("row", dza, d, 0), ("row", dzs, d, 0),
           ("row", dpre_ssd, d + 512, 0), ("next", dpre_ssd, d + 512, 0), ("row", dsm, LANES, 0), ("full", gcw), ("full", scw)]
    return _rowwise("assemble_dproj", body, t, tm, ins, [(PROJ_W, BF16)])[0]


def _mid_bwd(x1, mix, dy, dh2, pmw, pfw, tm):
    t, d = x1.shape

    def body(i, x1_ref, mix_ref, dy_ref, dh2_ref, pmw_ref, pfw_ref, dx1_ref, dmix_ref, dpm_ref, dpf_ref):
        xh, r2 = _rms(x1_ref[...], d)
        dh2 = dh2_ref[...]
        dpf_ref[...] += _colsum(dh2 * xh)
        dx1 = dy_ref[...] + _rms_bwd(xh, r2, dh2 * pfw_ref[...], d)
        dx1_ref[...] = dx1
        mh, r = _rms(mix_ref[...], d)
        dpm_ref[...] += _colsum(dx1 * mh)
        dmix_ref[...] = _rms_bwd(mh, r, dx1 * pmw_ref[...], d).astype(BF16)

    ins = [("row", x1, d, 0), ("row", mix, d, 0), ("row", dy, d, 0), ("row", dh2, d, 0), ("full", pmw), ("full", pfw)]
    return _rowwise("mid_bwd", body, t, tm, ins, [(d, F32), (d, BF16)], accs=[(1, d), (1, d)])


def _gate_norm_bwd(o_gdn, y_ssd, xs, proj, dmixin, gnw, snw, dvec, tm):
    t = o_gdn.shape[0]
    d = D_MODEL

    def body(i, o_ref, za_ref, y_ref, xs_ref, zs_ref, dma_ref, dms_ref, gnw_ref, snw_ref, dv_ref,
             do_ref, dza_ref, dy_ref, dxs_ref, dzs_ref, dgnw_ref, dsnw_ref, dd_ref):
        for hh in range(GDN_HEADS):
            sl = slice(hh * GDN_DK, (hh + 1) * GDN_DK)
            oh, r = _rms(o_ref[:, sl], GDN_DK)
            z = za_ref[:, sl]
            sz = _sigmoid(z)
            dm = dma_ref[:, sl]
            don = dm * (z * sz)
            dza_ref[:, sl] = (dm * oh * gnw_ref[...] * _dsilu(z, sz)).astype(BF16)
            dgnw_ref[...] += _colsum(don * oh)
            do_ref[:, sl] = _rms_bwd(oh, r, don * gnw_ref[...], GDN_DK)
        zs = zs_ref[...]
        sz = _sigmoid(zs)
        sil = zs * sz
        x = xs_ref[...]
        y0 = y_ref[...] + dv_ref[...] * x
        yg = y0 * sil
        dms = dms_ref[...]
        for g in range(SSD_GROUPS):
            sl = slice(g * 512, (g + 1) * 512)
            yh, r = _rms(yg[:, sl], 512)
            dsnw_ref[:, sl] += _colsum(dms[:, sl] * yh)
            dyg = _rms_bwd(yh, r, dms[:, sl] * snw_ref[:, sl], 512)
            dy0 = dyg * sil[:, sl]
            dzs_ref[:, sl] = (dyg * y0[:, sl] * _dsilu(zs[:, sl], sz[:, sl])).astype(BF16)
            dy_ref[:, sl] = dy0
            dxs_ref[:, sl] = dy0 * dv_ref[:, sl]
            dd_ref[:, sl] += _colsum(dy0 * x[:, sl])

    ins = [("row", o_gdn, d, 0), ("row", proj, d, 3), ("row", y_ssd, d, 0), ("row", xs, d, 0), ("row", proj, d, 4),
           ("row", dmixin, d, 0), ("row", dmixin, d, 1), ("full", gnw), ("full", snw), ("full", dvec)]
    return _rowwise("gate_norm_bwd", body, t, tm, ins, [(d, F32), (d, BF16), (d, F32), (d, F32), (d, BF16)],
                    accs=[(1, GDN_DK), (1, d), (1, d)])


def _ssd_chunk_bwd(xs, bc, dtx, acsx, acs, ar, dy, sts, bsz, seq, sb):
    nsb, ncb, _, sp = _ssd_specs(seq, sb)

    def body(x_ref, dtx_ref, ax_ref, bc_ref, acs_ref, ar_ref, dy_ref, sts_ref, dx_ref, dbc_ref, ddt_ref, dacs_ref, dst_scr):
        @pl.when(pl.program_id(1) == 0)
        def _():
            dst_scr[...] = jnp.zeros_like(dst_scr)

        c = _chunk_consts()
        lane5 = lax.broadcasted_iota(jnp.int32, (CHUNK, SSD_GW), 1) // SSD_P
        row5 = lax.broadcasted_iota(jnp.int32, (CHUNK, SSD_GW), 0)
        sel_in = lax.broadcasted_iota(jnp.int32, (SSD_GW, LANES), 0) // SSD_P
        sel_out = lax.broadcasted_iota(jnp.int32, (SSD_GW, LANES), 1)

        def chunk(nn, carry):
            n = ncb - 1 - nn
            r = pl.ds(pl.multiple_of(n * CHUNK, CHUNK), CHUNK)
            rs = pl.ds(pl.multiple_of(n * SSD_N, SSD_N), SSD_N)
            acsv = acs_ref[0, r, :]
            ddt = jnp.zeros((CHUNK, LANES), F32)
            dacs = jnp.zeros((CHUNK, LANES), F32)
            for g in range(SSD_GROUPS):
                gl = slice(g * SSD_GW, (g + 1) * SSD_GW)
                x, dt, ax, dyv = x_ref[0, r, gl], dtx_ref[0, r, gl], ax_ref[0, r, gl], dy_ref[0, r, gl]
                bm = bc_ref[0, r, g * SSD_N:(g + 1) * SSD_N]
                cm = bc_ref[0, r, (SSD_GROUPS + g) * SSD_N:(SSD_GROUPS + g + 1) * SSD_N]
                st = sts_ref[0, rs, gl]
                dst = dst_scr[:, gl]
                rsel = (sel_in + (16 + g * SSD_HPG) == sel_out).astype(F32)
                xdt = x * dt
                cbm = _mm(cm, bm, NT)
                al = ax[CHUNK - 1:CHUNK, :]
                ex, el = jnp.exp(ax), jnp.exp(al)
                dec = jnp.exp(al - ax)
                xd = xdt * dec
                dye = dyv * ex
                dxd = _mm(bm, dst)
                dxdt = dec * dxd
                dcm = _mm(dye, st, NT)
                dbm = _mm(xd, dst, NT)
                z = dye * _mm(cm, st) - dxd * xd
                zl = _colsum(dst * st) * el + _colsum(dxd * xd)
                z = z + jnp.where(row5 == CHUNK - 1, zl, 0.0)
                dcb = jnp.zeros((CHUNK, CHUNK), F32)
                for hh in range(SSD_HPG):
                    head = g * SSD_HPG + hh
                    lm, gm = _ssd_head(acsv, ar_ref, n, head, cbm, c)
                    dym = jnp.where(lane5 == hh, dyv, 0.0)
                    dxdt = dxdt + _mm(gm, dym, TN)
                    dg = _mm(dym, xdt, NT)
                    dcb = dcb + dg * lm
                    pm = dg * gm
                    dacs = dacs + jnp.where(c["lane"] == head + 16, _rowsum(pm) - _mmsel(pm, c["ones"], TN), 0.0)
                dbc_ref[0, r, (SSD_GROUPS + g) * SSD_N:(SSD_GROUPS + g + 1) * SSD_N] = dcm + _mm(dcb, bm)
                dbc_ref[0, r, g * SSD_N:(g + 1) * SSD_N] = dbm + _mm(dcb, cm, TN)
                dacs = dacs + _mmsel(z, rsel)
                ddt = ddt + _mmsel(dxdt * x, rsel)
                dx_ref[0, r, gl] = dxdt * dt
                dst_scr[:, gl] = dst * el + _mm(cm, dye, TN)
            ddt_ref[0, r, :] = ddt
            dacs_ref[0, r, :] = dacs
            return carry

        lax.fori_loop(0, ncb, chunk, 0)

    return _pcall(
        body, name="ssd_chunk_bwd", grid=(bsz, nsb),
        in_specs=[sp["wide"](), sp["wide"](), sp["wide"](), sp["bc"](), sp["small"](), sp["ar"], sp["wide"](), sp["st"]],
        out_specs=[sp["wide"](), sp["bc"](), sp["small"](), sp["small"]()],
        out_shape=[jax.ShapeDtypeStruct((bsz, seq, SSD_HEADS * SSD_P), F32),
                   jax.ShapeDtypeStruct((bsz, seq, 2 * SSD_GROUPS * SSD_N), F32),
                   jax.ShapeDtypeStruct((bsz, seq, LANES), F32), jax.ShapeDtypeStruct((bsz, seq, LANES), F32)],
        scratch_shapes=[pltpu.VMEM((SSD_N, SSD_HEADS * SSD_P), F32)],
        compiler_params=_params(("parallel", "arbitrary")),
    )(xs, dtx, acsx, bc, acs, ar, dy, sts)


def _ssd_prep_bwd(proj, dxs_c, dxs_d, dbc, ddt, dacs, dsm_gdn, cw, cb, sp, seq, tm):
    t = proj.shape[0]
    d = D_MODEL

    def body(i, x_ref, xh_ref, bc_ref, bch_ref, sm_ref, dxc_ref, dxd_ref, dbc_ref, ddt_ref, dacs_ref, dsg_ref,
             cw_ref, cb_ref, sp_ref, dpre_ref, dsm_ref, dcw_ref, dcb_ref, dsp_ref):
        keep, _ = _seq_flags(i, seq, tm)
        parts = ((x_ref, xh_ref, 0, d, (dxc_ref[...] + dxd_ref[...],)),
                 (bc_ref, bch_ref, d, 512, (dbc_ref[...],)))
        for xr, hr, off, w, grads in parts:
            taps = _conv_taps(xr[...], hr[...] * keep, 4)
            y = _conv(taps, cw_ref[:, off:off + w]) + cb_ref[:, off:off + w]
            ds_ = _dsilu(y, _sigmoid(y))
            o = 0
            for gr in grads:
                wg = gr.shape[1]
                dpre = gr * ds_[:, o:o + wg]
                dpre_ref[:, off + o:off + o + wg] = dpre
                dcb_ref[:, off + o:off + o + wg] += _colsum(dpre)
                for k in range(4):
                    dcw_ref[k:k + 1, off + o:off + o + wg] += _colsum(dpre * taps[k][:, o:o + wg])
                o += wg
        sm = sm_ref[...]
        lane = lax.broadcasted_iota(jnp.int32, sm.shape, 1)
        valid = (lane >= 16) & (lane < 32)
        xb = sm + sp_ref[1:2, :]
        dt = jnp.where(valid, _softplus(xb), 0.0)
        a_neg = -jnp.exp(sp_ref[0:1, :])
        dadt_s = _mmx(_block_tri(tm, True), dacs_ref[...])
        dxb = jnp.where(valid, (ddt_ref[...] + dadt_s * a_neg) * _sigmoid(xb), 0.0)
        dsm_ref[...] = (dsg_ref[...] + dxb).astype(BF16)
        dsp_ref[1:2, :] += _colsum(dxb)
        dsp_ref[0:1, :] += jnp.where(valid[0:1, :], _colsum(dadt_s * dt) * a_neg, 0.0)

    ins = [("row", proj, d, 5), ("prev", proj, d, 5), ("row", proj, 512, 12), ("prev", proj, 512, 12),
           ("row", proj, LANES, SMALL_CB), ("row", dxs_c, d, 0), ("row", dxs_d, d, 0), ("row", dbc, 512, 0),
           ("row", ddt, LANES, 0), ("row", dacs, LANES, 0), ("row", dsm_gdn, LANES, 0),
           ("full", cw), ("full", cb), ("full", sp)]
    return _rowwise("ssd_prep_bwd", body, t, tm, ins, [(d + 512, F32), (LANES, BF16)],
                    accs=[(SUBLANES, d + 512), (1, d + 512), (SUBLANES, LANES)])


def _gdn_chunk_bwd(qn, kn, vv, gs, gr, do, sts, tis, bsz, seq, sb, riders):
    hb = GDN_HB
    nsb, ncb, sp = _gdn_specs(seq, sb, hb, True)
    grid = (bsz, GDN_HEADS // hb, nsb)
    any_spec, rider_shapes, rider_sems, wrap = _riding_exchange(riders, True, 8, 4, grid)

    def body(q_ref, k_ref, v_ref, gs_ref, gr_ref, do_ref, st_ref, ti_ref, dq_ref, dk_ref, dv_ref, dgb_ref, ds_scr):
        hg = pl.program_id(1)

        @pl.when(pl.program_id(2) == 0)
        def _():
            ds_scr[...] = jnp.zeros_like(ds_scr)

        c = _chunk_consts()

        def chunk(nn, carry):
            n = ncb - 1 - nn
            r = pl.ds(pl.multiple_of(n * CHUNK, CHUNK), CHUNK)
            rs = pl.ds(pl.multiple_of(n * GDN_DK, GDN_DK), GDN_DK)
            gsv = gs_ref[0, r, :]
            heads = list(range(hb))
            sls = [slice(ih * GDN_DK, (ih + 1) * GDN_DK) for ih in heads]
            q = [q_ref[0, r, sl] for sl in sls]
            k = [k_ref[0, r, sl] for sl in sls]
            v = [v_ref[0, r, sl] for sl in sls]
            do_ = [do_ref[0, r, sl] for sl in sls]
            s = [st_ref[0, ih, rs, :] for ih in heads]
            tinv = [ti_ref[0, ih, r, :] for ih in heads]
            dsn = [ds_scr[ih] for ih in heads]
            beta, dc, eg, egl, ekd = zip(*[
                _gdn_gates(gsv, gr_ref[0, n, pl.ds(ih, 1), :], ih, c) for ih in heads])
            mul = lambda a, b: a * b
            kb = _hmap(mul, k, beta)
            rhs_w = _hmap(mul, kb, eg)
            u = _hmap(lambda t_, a, b: _mm3(t_, a * b), tinv, v, beta)
            w = _hmap(_mm3, tinv, rhs_w)
            amat = _hmap(lambda a, b, d_: jnp.where(c["strict"], _mm(a, b, NT) * d_, 0.0), kb, k, dc)
            qk = _hmap(lambda a, b, d_: _mm(a, b, NT) * d_, q, k, dc)
            qd = _hmap(mul, q, eg)
            kd = _hmap(mul, k, ekd)
            v_new = _hmap(lambda a, b, s_: a - _mm(b, s_), u, w, s)
            dv_new = _hmap(lambda qk_, d_, kd_, dn: _mm(qk_, d_, TN) + _mm(kd_, dn), qk, do_, kd, dsn)
            dqk = _hmap(lambda d_, vn: _mm(d_, vn, NT), do_, v_new)
            dqd = _hmap(lambda d_, s_: _mm(d_, s_, NT), do_, s)
            ds_new = _hmap(lambda qd_, d_, dn, e, w_, dvn: _mm(qd_, d_, TN) + dn * e - _mm(w_, dvn, TN),
                           qd, do_, dsn, egl, w, dv_new)
            dkd = _hmap(lambda vn, dn: _mm(vn, dn, NT), v_new, dsn)
            dgl = _hmap(lambda s_, dn, e: _colsum(_rowsum(s_ * dn)) * e, s, dsn, egl)
            dw = _hmap(lambda dvn, s_: -_mm(dvn, s_, NT), dv_new, s)
            dru = _hmap(lambda t_, a: _mm3(t_, a, TN), tinv, dv_new)
            drw = _hmap(lambda t_, a: _mm3(t_, a, TN), tinv, dw)
            da = _hmap(lambda a, u_, b, w_: jnp.where(c["strict"], -(_mm(a, u_, NT) + _mm(b, w_, NT)), 0.0), dru, u, drw, w)
            m = _hmap(mul, da, dc)
            dkb = _hmap(lambda a, e, m_, k_: a * e + _mm(m_, k_), drw, eg, m, k)
            mq = _hmap(mul, dqk, dc)
            dq = _hmap(lambda mq_, k_, a, e: _mm(mq_, k_) + a * e, mq, k, dqd, eg)
            dk = _hmap(lambda m_, kb_, mq_, q_, a, e, b, be: _mm(m_, kb_, TN) + _mm(mq_, q_, TN) + a * e + b * be,
                       m, kb, mq, q, dkd, ekd, dkb, beta)
            dbeta = _hmap(lambda a, v_, b, k_: _rowsum(a * v_) + _rowsum(b * k_), dru, v, dkb, k)
            pq = _hmap(lambda a, am, b, qk_: a * am + b * qk_, da, amat, dqk, qk)
            ekk = _hmap(lambda a, b: _rowsum(a * b), dkd, kd)
            dgc = _hmap(lambda pq_, a, rw, b, qd_, e, gl_: (
                _rowsum(pq_) - _mmsel(pq_, c["ones"], TN) + (_rowsum(a * rw) + _rowsum(b * qd_) - e)
                + jnp.where(c["row1"] == CHUNK - 1, _colsum(e) + gl_, 0.0)), pq, drw, rhs_w, dqd, qd, ekk, dgl)
            for ih in heads:
                ds_scr[ih] = ds_new[ih]
                dv_ref[0, r, sls[ih]] = dru[ih] * beta[ih]
                dq_ref[0, r, sls[ih]] = dq[ih]
                dk_ref[0, r, sls[ih]] = dk[ih]
                dgb_ref[0, r, sls[ih]] = jnp.where(c["lane"] == 0, dbeta[ih], jnp.where(c["lane"] == 1, dgc[ih], 0.0))
            return carry

        lax.fori_loop(0, ncb, chunk, 0)

    res = _pcall(
        wrap(body), name="gdn_chunk_bwd", grid=grid,
        in_specs=[sp["wide"](), sp["wide"](), sp["wide"](), sp["gs"], sp["gr"], sp["wide"](), sp["st"], sp["ti"]] + any_spec,
        out_specs=[sp["wide"](), sp["wide"](), sp["wide"](), sp["wide"]()] + any_spec,
        out_shape=[jax.ShapeDtypeStruct((bsz, seq, D_MODEL), F32)] * 4 + rider_shapes,
        scratch_shapes=[pltpu.VMEM((hb, GDN_DK, GDN_DK), F32)] + rider_sems,
        compiler_params=_params(("arbitrary", "arbitrary", "arbitrary")),
    )(qn, kn, vv, gs, gr, do, sts, tis, *riders)
    return res[:4], res[4:]


def _gdn_prep_bwd(proj, dqn, dkn, dvv, dgb, cw, gp, seq, tm):
    t = proj.shape[0]
    d = D_MODEL

    def body(i, q_ref, qh_ref, k_ref, kh_ref, v_ref, vh_ref, sm_ref, dq_ref, dk_ref, dv_ref, dgb_ref, cw_ref, gp_ref,
             dpre_ref, dsm_ref, dcw_ref, dgp_ref):
        keep, _ = _seq_flags(i, seq, tm)
        for x_ref, h_ref, g_ref, off, scale in ((q_ref, qh_ref, dq_ref, 0, GDN_DK ** -0.5),
                                               (k_ref, kh_ref, dk_ref, d, 1.0), (v_ref, vh_ref, dv_ref, 2 * d, None)):
            taps = _conv_taps(x_ref[...], h_ref[...] * keep, 4)
            y = _conv(taps, cw_ref[:, off:off + d])
            sy = _sigmoid(y)
            ds_ = _dsilu(y, sy)
            if scale is None:
                dpre = g_ref[...] * ds_
                dpre_ref[:, off:off + d] = dpre
                for k in range(4):
                    dcw_ref[k:k + 1, off:off + d] += _colsum(dpre * taps[k])
            else:
                a = y * sy
                for hh in range(GDN_HEADS):
                    sl = slice(hh * GDN_DK, (hh + 1) * GDN_DK)
                    s = a[:, sl]
                    n = lax.rsqrt(_rowsum(s * s) + EPS)
                    ah = s * n
                    gq = g_ref[:, sl]
                    dpre = (scale * n) * (gq - ah * _rowsum(gq * ah)) * ds_[:, sl]
                    dpre_ref[:, off + hh * GDN_DK:off + (hh + 1) * GDN_DK] = dpre
                    for k in range(4):
                        dcw_ref[k:k + 1, off + hh * GDN_DK:off + (hh + 1) * GDN_DK] += _colsum(dpre * taps[k][:, sl])
        sm = sm_ref[...]
        lane = lax.broadcasted_iota(jnp.int32, sm.shape, 1)
        si = lax.broadcasted_iota(jnp.int32, (d, LANES), 0)
        so = lax.broadcasted_iota(jnp.int32, (d, LANES), 1)
        sel = (((si % GDN_DK == 0) & (so == si // GDN_DK)) | ((si % GDN_DK == 1) & (so == si // GDN_DK + 8))).astype(F32)
        dsel = _mmx(dgb_ref[...], sel)
        is_g = (lane >= 8) & (lane < 16)
        dsel = jnp.where(is_g, _mmx(_block_tri(tm, True), dsel), dsel)
        beta = _sigmoid(sm)
        xb = sm + gp_ref[1:2, :]
        a_neg = -jnp.exp(gp_ref[0:1, :])
        sp = _softplus(xb)
        dxb = jnp.where(is_g, dsel * a_neg * _sigmoid(xb), 0.0)
        dsm_ref[...] = jnp.where(lane < 8, dsel * beta * (1.0 - beta), dxb)
        dgp_ref[1:2, :] += _colsum(dxb)
        dgp_ref[0:1, :] += _colsum(jnp.where(is_g, dsel * a_neg * sp, 0.0))

    ins = []
    for cb in range(3):
        ins += [("row", proj, d, cb), ("prev", proj, d, cb)]
    ins += [("row", proj, LANES, SMALL_CB), ("row", dqn, d, 0), ("row", dkn, d, 0), ("row", dvv, d, 0), ("row", dgb, d, 0),
            ("full", cw), ("full", gp)]
    return _rowwise("gdn_prep_bwd", body, t, tm, ins, [(3 * d, F32), (LANES, F32)],
                    accs=[(SUBLANES, 3 * d), (SUBLANES, LANES)])


def _first_bwd(x, dh1, dx1, w, tm):
    t, d = x.shape

    def body(i, x_ref, dh_ref, dx1_ref, w_ref, dx_ref, dw_ref):
        xh, r = _rms(x_ref[...], d)
        dh = dh_ref[...]
        dw_ref[...] += _colsum(dh * xh)
        dx_ref[...] = dx1_ref[...] + _rms_bwd(xh, r, dh * w_ref[...], d)

    return _rowwise("first_bwd", body, t, tm, [("row", x, d, 0), ("row", dh1, d, 0), ("row", dx1, d, 0), ("full", w)],
                    [(d, F32)], accs=[(1, d)])


def _exchange(name, arrays, scatter):
    n = len(arrays)

    def body(*refs):
        ins, outs, sems = refs[:n], refs[n:2 * n], refs[2 * n:]
        _exchange_phase(ins, outs, sems, scatter, start=True)
        _exchange_phase(ins, outs, sems, scatter, start=False)

    return _pcall(
        body, name=name,
        in_specs=[pl.BlockSpec(memory_space=pl.ANY)] * n,
        out_specs=[pl.BlockSpec(memory_space=pl.ANY)] * n,
        out_shape=_exchange_out_shapes(arrays, scatter),
        scratch_shapes=_exchange_sems(n),
    )(*arrays)


def _exchange_out_shapes(arrays, scatter):
    return [jax.ShapeDtypeStruct(a.shape if scatter else (N_DEV,) + a.shape, a.dtype) for a in arrays]


def _exchange_sems(n):
    return [pltpu.SemaphoreType.DMA((n, N_DEV - 1)), pltpu.SemaphoreType.DMA((n, N_DEV - 1)), pltpu.SemaphoreType.DMA((n,))]


def _exchange_phase(ins, outs, sems, scatter, start):
    send_sems, recv_sems, loc_sems = sems
    x, y, c = lax.axis_index("x"), lax.axis_index("y"), lax.axis_index("c")
    me = 4 * x + 2 * y + c
    for t in range(len(ins)):
        loc = pltpu.make_async_copy(ins[t].at[me] if scatter else ins[t], outs[t].at[me], loc_sems.at[t])
        if start:
            loc.start()
        else:
            loc.wait()
        for k in range(N_DEV - 1):
            bx, by, bc = ((k + 1) >> 2) & 1, ((k + 1) >> 1) & 1, (k + 1) & 1
            px = 1 - x if bx else x
            py = 1 - y if by else y
            pc = 1 - c if bc else c
            peer = 4 * px + 2 * py + pc
            src = ins[t].at[peer] if scatter else ins[t]
            copy = lambda dst: pltpu.make_async_remote_copy(
                src_ref=src, dst_ref=dst, send_sem=send_sems.at[t, k], recv_sem=recv_sems.at[t, k],
                device_id=(px, py, pc), device_id_type=pl.DeviceIdType.MESH)
            if start:
                copy(outs[t].at[me]).start()
            else:
                copy(outs[t].at[me]).wait_send()
                copy(outs[t].at[peer]).wait_recv()


def _adam_math(w, g, m, v):
    m = ADAM_B1 * m + (1.0 - ADAM_B1) * g
    v = ADAM_B2 * v + (1.0 - ADAM_B2) * (g * g)
    m_hat = m / (1.0 - ADAM_B1 ** ADAM_STEP)
    v_hat = v / (1.0 - ADAM_B2 ** ADAM_STEP)
    delta = -ADAM_LR * (m_hat / (jnp.sqrt(v_hat) + ADAM_EPS) + ADAM_WD * w)
    return delta, m, v


def _adam_big(name, parts, w, m, v, tm):
    r, c = w.shape
    tm = tm if r % tm == 0 else r

    def body(p_ref, w_ref, m_ref, v_ref, g_ref, d_ref, nm_ref, nv_ref):
        g = p_ref[0].astype(F32)
        for s in range(1, N_DEV):
            g = g + p_ref[s].astype(F32)
        g_ref[...] = g
        d_ref[...], nm_ref[...], nv_ref[...] = _adam_math(w_ref[...], g, m_ref[...], v_ref[...])

    blk = lambda: pl.BlockSpec((tm, c), lambda i: (i, 0))
    return _pcall(
        body, name=name, grid=(r // tm,),
        in_specs=[pl.BlockSpec((N_DEV, tm, c), lambda i: (0, i, 0)), blk(), blk(), blk()],
        out_specs=[blk(), blk(), blk(), blk()],
        out_shape=[jax.ShapeDtypeStruct((r, c), F32)] * 4,
        compiler_params=_params(("parallel",)),
    )(parts, w, m, v)


SMALL_ROWS = 56
ROW_DD, ROW_LOSS = 5, 6


def _small_sum(gathered):
    def body(g_ref, o_ref, x_ref):
        s = g_ref[0]
        for dev in range(1, N_DEV):
            s = s + g_ref[dev]
        o_ref[...] = s
        ri = lax.broadcasted_iota(jnp.int32, (D_MODEL, LANES), 0)
        ro = lax.broadcasted_iota(jnp.int32, (D_MODEL, LANES), 1)
        heads = _mmx(jnp.broadcast_to(s[ROW_DD:ROW_DD + 1, :], (SUBLANES, D_MODEL)), (ri // SSD_P == ro).astype(F32))
        loss = _rowsum(jnp.broadcast_to(s[ROW_LOSS:ROW_LOSS + 1, :], (SUBLANES, D_MODEL)))
        row = lax.broadcasted_iota(jnp.int32, (SUBLANES, LANES), 0)
        x_ref[...] = jnp.where(row == 0, heads, jnp.broadcast_to(loss, (SUBLANES, LANES)))

    return _pcall(
        body, name="small_sum",
        out_shape=[jax.ShapeDtypeStruct((SMALL_ROWS, D_MODEL), F32), jax.ShapeDtypeStruct((SUBLANES, LANES), F32)],
        compiler_params=_params(None),
    )(gathered)


def _adam_small(g, w, m, v):
    def body(g_ref, w_ref, m_ref, v_ref, d_ref, nm_ref, nv_ref):
        d_ref[...], nm_ref[...], nv_ref[...] = _adam_math(w_ref[...], g_ref[...], m_ref[...], v_ref[...])

    return _pcall(body, name="adam_small", out_shape=[jax.ShapeDtypeStruct(g.shape, F32)] * 3,
                  compiler_params=_params(None))(g, w, m, v)


def _pack(pieces, rows):
    flat = jnp.concatenate([p.reshape(-1).astype(F32) for p in pieces])
    return jnp.pad(flat, (0, rows * D_MODEL - flat.shape[0])).reshape(rows, D_MODEL)


def _unpack(packed, shapes):
    flat = packed.reshape(-1)
    out, off = [], 0
    for shp in shapes:
        size = 1
        for s in shp:
            size *= s
        out.append(flat[off:off + size].reshape(shp))
        off += size
    return out


def _permute_in(w):
    pad = jnp.zeros((w.shape[0], PROJ_W - D_IN), w.dtype)
    return jnp.concatenate([w[:, 0:4096], w[:, 4112:6672], w[:, 4096:4112], w[:, 6672:6688], pad], axis=1)


def _unpermute_in(g):
    return jnp.concatenate([g[:, 0:4096], g[:, 6656:6672], g[:, 4096:6656], g[:, 6672:6688]], axis=1)


def _lane_row(vec, start):
    return jnp.zeros((LANES,), F32).at[start:start + vec.shape[0]].set(vec)


def _cols_from_shards(g):
    return jnp.transpose(g, (1, 0, 2)).reshape(g.shape[1], N_DEV * g.shape[2])


def _cols_to_shards(a):
    return jnp.transpose(a.astype(BF16).reshape(a.shape[0], N_DEV, a.shape[1] // N_DEV), (1, 0, 2))


def _rows_to_shards(a):
    return a.astype(BF16).reshape(N_DEV, a.shape[0] // N_DEV, a.shape[1])


def _local_step(x, tgt, wp_in, rest, p, rest_is_sharded):
    bsz, seq, d = x.shape
    t = bsz * seq
    x2 = x.reshape(t, d)
    tgt2 = tgt.reshape(t, d)
    tm = min(256, seq)
    tm_wide = min(128, seq)
    sb = min(512, seq)

    gp = jnp.zeros((SUBLANES, LANES), F32).at[0].set(_lane_row(p["gdn_a_log"], 8)).at[1].set(_lane_row(p["gdn_dt_bias"], 8))
    sp = jnp.zeros((SUBLANES, LANES), F32).at[0].set(_lane_row(p["ssd_a_log"], 16)).at[1].set(_lane_row(p["ssd_dt_bias"], 16))
    dvec = jnp.repeat(p["ssd_d"], SSD_P).reshape(1, d)
    row = lambda v: v.reshape(1, -1)
    pre_mix, post_mix, pre_ffn, post_ffn = (row(p[k]) for k in ("pre_mix_norm", "post_mix_norm", "pre_ffn_norm", "post_ffn_norm"))
    gnw, snw = row(p["gdn_norm_w"]), row(p["ssd_norm_w"])
    gcw, scw, scb, fcw, fcb = p["gdn_conv_w"], p["ssd_conv_w"], row(p["ssd_conv_b"]), p["ffn_conv_w"], row(p["ffn_conv_b"])

    h1 = _norm_cast("norm_in", x2, pre_mix, tm)
    proj = _matmul("mm_proj", h1, wp_in, "nn", F32)
    b3 = lambda a: a.reshape(bsz, seq, a.shape[-1])
    b2 = lambda a: a.reshape(t, a.shape[-1])
    rows_of = lambda a, lo, n: jnp.transpose(a[:, lo:lo + n].reshape(bsz, seq // CHUNK, CHUNK, n), (0, 3, 1, 2))
    qn, kn, vv, gs = (b3(a) for a in _gdn_prep(proj, gcw, gp, seq, tm))
    gr = jnp.transpose(b2(gs)[:, 8:8 + GDN_HEADS].reshape(bsz, seq // CHUNK, CHUNK, GDN_HEADS), (0, 1, 3, 2))
    (o_gdn, gdn_st, gdn_ti), gathered = _gdn_chunk_fwd(qn, kn, vv, gs, gr, bsz, seq, sb, list(rest) if rest_is_sharded else [])
    if rest_is_sharded:
        w_out, w_up, w_down = gathered[0].reshape(-1, d), _cols_from_shards(gathered[1]), gathered[2].reshape(-1, d)
    else:
        w_out, w_up, w_down = rest
    o_gdn = b2(o_gdn)
    xs, bc, dtx, acsx, acs = _ssd_prep(proj, scw, scb, sp, seq, tm)
    ar = rows_of(acs, 16, SSD_HEADS)
    y_ssd, ssd_st = _ssd_chunk_fwd(b3(xs), b3(bc), b3(dtx), b3(acsx), b3(acs), ar, bsz, seq, sb)
    y_ssd = b2(y_ssd)
    mixin = _gate_norm(o_gdn, y_ssd, xs, proj, gnw, snw, dvec, tm)
    mix = _matmul("mm_out", mixin, w_out, "nn", F32)
    x1, h2 = _mid(x2, mix, post_mix, pre_ffn, tm)
    u_pre = _matmul("mm_up", h2, w_up, "nn", F32)
    act = _ffn_act(u_pre, fcw, fcb, seq, tm_wide)
    f = _matmul("mm_down", act, w_down, "nn", F32, tk=1408)
    dy, df, loss_lanes, d_post_ffn = _final(x1, f, tgt2, post_ffn, tm)

    g_down = _matmul("mm_dw_down", act, df, "tn", F32, tm=1408)
    dact = _matmul("mm_dact", df, w_down, "nt", F32, tn=1408)
    du_pre, d_fcw, d_fcb = _ffn_bwd(u_pre, dact, fcw, fcb, seq, tm_wide)
    g_up = _matmul("mm_dw_up", h2, du_pre, "tn", F32)
    dh2 = _matmul("mm_dh2", du_pre, w_up, "nt", F32)
    dx1, dmix, d_post_mix, d_pre_ffn = _mid_bwd(x1, mix, dy, dh2, post_mix, pre_ffn, tm)
    g_out = _matmul("mm_dw_out", mixin, dmix, "tn", F32)
    dmixin = _matmul("mm_dmixin", dmix, w_out, "nt", F32)
    do_gdn, dza, dy_ssd, dxs_d, dzs, d_gnw, d_snw, d_dd = _gate_norm_bwd(o_gdn, y_ssd, xs, proj, dmixin, gnw, snw, dvec, tm)
    dxs_c, dbc, ddt, dacs = (b2(a) for a in _ssd_chunk_bwd(
        b3(xs), b3(bc), b3(dtx), b3(acsx), b3(acs), ar, b3(dy_ssd), ssd_st, bsz, seq, sb))
    riders = [_rows_to_shards(g_out), _cols_to_shards(g_up), _rows_to_shards(g_down)] if rest_is_sharded else []
    dgdn, received = _gdn_chunk_bwd(qn, kn, vv, gs, gr, b3(do_gdn), gdn_st, gdn_ti, bsz, seq, min(256, seq), riders)
    if rest_is_sharded:
        g_out, g_up, g_down = received
    dqn, dkn, dvv, dgb = (b2(a) for a in dgdn)
    dpre_gdn, dsm_gdn, d_gcw, d_gp = _gdn_prep_bwd(proj, dqn, dkn, dvv, dgb, gcw, gp, seq, tm)
    dpre_ssd, dsm, d_scw, d_scb, d_sp = _ssd_prep_bwd(proj, dxs_c, dxs_d, dbc, ddt, dacs, dsm_gdn, scw, scb, sp, seq, tm)
    dproj = _assemble_dproj(dpre_gdn, dza, dzs, dpre_ssd, dsm, gcw, scw, seq, tm)
    g_in = _matmul("mm_dw_in", h1, dproj, "tn", F32)
    if rest_is_sharded:
        dh1, (g_in,) = _matmul("mm_dh1", dproj, wp_in, "nt", F32, scatter_riders=[_cols_to_shards(_unpermute_in(g_in))])
    else:
        dh1 = _matmul("mm_dh1", dproj, wp_in, "nt", F32)
    dx, d_pre_mix = _first_bwd(x2, dh1, dx1, pre_mix, tm)

    small = dict(pre_mix_norm=d_pre_mix, ssd_norm_w=d_snw, post_mix_norm=d_post_mix, pre_ffn_norm=d_pre_ffn,
                 post_ffn_norm=d_post_ffn, dd_lanes=d_dd, loss_lanes=loss_lanes, gdn_gates=d_gp, ssd_gates=d_sp,
                 gdn_norm_w=d_gnw, gdn_conv_w=d_gcw[0:4], ssd_conv_w=d_scw[0:4], ssd_conv_b=d_scb,
                 ffn_conv_w=d_fcw[0:3], ffn_conv_b=d_fcb)
    return dx.reshape(bsz, seq, d), g_in, g_out, g_up, g_down, small


def kernel(x, pre_mix_norm, w_in, gdn_conv_w, gdn_a_log, gdn_dt_bias, gdn_norm_w, ssd_conv_w, ssd_conv_b, ssd_a_log, ssd_dt_bias, ssd_d, ssd_norm_w, w_out, post_mix_norm, pre_ffn_norm, w_up, ffn_conv_w, ffn_conv_b, w_down, post_ffn_norm, loss_target, m_pre_mix_norm, m_w_in, m_gdn_conv_w, m_gdn_a_log, m_gdn_dt_bias, m_gdn_norm_w, m_ssd_conv_w, m_ssd_conv_b, m_ssd_a_log, m_ssd_dt_bias, m_ssd_d, m_ssd_norm_w, m_w_out, m_post_mix_norm, m_pre_ffn_norm, m_w_up, m_ffn_conv_w, m_ffn_conv_b, m_w_down, m_post_ffn_norm, v_pre_mix_norm, v_w_in, v_gdn_conv_w, v_gdn_a_log, v_gdn_dt_bias, v_gdn_norm_w, v_ssd_conv_w, v_ssd_conv_b, v_ssd_a_log, v_ssd_dt_bias, v_ssd_d, v_ssd_norm_w, v_w_out, v_post_mix_norm, v_pre_ffn_norm, v_w_up, v_ffn_conv_w, v_ffn_conv_b, v_w_down, v_post_ffn_norm):
    names = ["pre_mix_norm", "w_in", "gdn_conv_w", "gdn_a_log", "gdn_dt_bias", "gdn_norm_w", "ssd_conv_w", "ssd_conv_b",
             "ssd_a_log", "ssd_dt_bias", "ssd_d", "ssd_norm_w", "w_out", "post_mix_norm", "pre_ffn_norm", "w_up",
             "ffn_conv_w", "ffn_conv_b", "w_down", "post_ffn_norm"]
    w_args = [pre_mix_norm, w_in, gdn_conv_w, gdn_a_log, gdn_dt_bias, gdn_norm_w, ssd_conv_w, ssd_conv_b, ssd_a_log, ssd_dt_bias, ssd_d, ssd_norm_w, w_out, post_mix_norm, pre_ffn_norm, w_up, ffn_conv_w, ffn_conv_b, w_down, post_ffn_norm]
    m_args = [m_pre_mix_norm, m_w_in, m_gdn_conv_w, m_gdn_a_log, m_gdn_dt_bias, m_gdn_norm_w, m_ssd_conv_w, m_ssd_conv_b, m_ssd_a_log, m_ssd_dt_bias, m_ssd_d, m_ssd_norm_w, m_w_out, m_post_mix_norm, m_pre_ffn_norm, m_w_up, m_ffn_conv_w, m_ffn_conv_b, m_w_down, m_post_ffn_norm]
    v_args = [v_pre_mix_norm, v_w_in, v_gdn_conv_w, v_gdn_a_log, v_gdn_dt_bias, v_gdn_norm_w, v_ssd_conv_w, v_ssd_conv_b, v_ssd_a_log, v_ssd_dt_bias, v_ssd_d, v_ssd_norm_w, v_w_out, v_post_mix_norm, v_pre_ffn_norm, v_w_up, v_ffn_conv_w, v_ffn_conv_b, v_w_down, v_post_ffn_norm]
    w = {k: a[0] for k, a in zip(names, w_args)}
    m = {k: a[0] for k, a in zip(names, m_args)}
    v = {k: a[0] for k, a in zip(names, v_args)}
    idx = 4 * lax.axis_index("x") + 2 * lax.axis_index("y") + lax.axis_index("c")
    big = ("w_in", "w_out", "w_up", "w_down")
    conv = ("gdn_conv_w", "ssd_conv_w", "ffn_conv_w")

    conv_local = jnp.concatenate([jnp.pad(w[k], ((0, 4 - w[k].shape[0]), (0, 0))) for k in conv], axis=1)
    g_in, g_conv = _exchange("gather_weights", [w["w_in"].astype(BF16), conv_local], scatter=False)
    wp_in = _permute_in(_cols_from_shards(g_in))
    p = {k: w[k] for k in names if k not in big and k not in conv}
    off = 0
    for k in conv:
        cw = w[k].shape[1]
        p[k] = jnp.transpose(g_conv[:, :w[k].shape[0], off:off + cw], (1, 0, 2)).reshape(w[k].shape[0], N_DEV * cw)
        off += cw

    rest = tuple(w[k].astype(BF16) for k in ("w_out", "w_up", "w_down"))
    dx, p_in, p_out, p_up, p_down, small = _local_step(x, loss_target, wp_in, rest, p, True)

    gate_row = jnp.concatenate([small["gdn_gates"][0], small["gdn_gates"][1], small["ssd_gates"][0], small["ssd_gates"][1],
                                small["gdn_norm_w"][0], jnp.zeros((D_MODEL - 5 * LANES,), F32)]).reshape(1, D_MODEL)
    pack = _pack([small["pre_mix_norm"], small["ssd_norm_w"], small["post_mix_norm"], small["pre_ffn_norm"],
                  small["post_ffn_norm"], small["dd_lanes"], small["loss_lanes"], gate_row,
                  small["gdn_conv_w"], small["ssd_conv_w"], jnp.pad(small["ssd_conv_b"], ((0, 0), (0, 512))),
                  jnp.pad(small["ffn_conv_w"].reshape(-1), (0, 17 * D_MODEL - 3 * 2 * D_FF)),
                  jnp.pad(small["ffn_conv_b"], ((0, 0), (0, 512)))], SMALL_ROWS)
    (pack_all,) = _exchange("gather_small", [pack], scatter=False)
    ssum, extra = _small_sum(pack_all)

    grads, deltas, new_m, new_v = {}, {}, {}, {}
    for k, parts in (("w_in", p_in), ("w_out", p_out), ("w_up", p_up), ("w_down", p_down)):
        grads[k], deltas[k], new_m[k], new_v[k] = _adam_big("adam_" + k, parts, w[k], m[k], v[k], 256)

    flat = ssum.reshape(-1)
    gate = ssum[7]
    sg = dict(pre_mix_norm=ssum[0], ssd_norm_w=ssum[1], post_mix_norm=ssum[2], pre_ffn_norm=ssum[3], post_ffn_norm=ssum[4],
              gdn_a_log=gate[8:16], gdn_dt_bias=gate[LANES + 8:LANES + 16], ssd_a_log=gate[2 * LANES + 16:2 * LANES + 32],
              ssd_dt_bias=gate[3 * LANES + 16:3 * LANES + 32], gdn_norm_w=gate[4 * LANES:5 * LANES], ssd_d=extra[0, 0:SSD_HEADS])
    o = 8 * D_MODEL
    full_gcw = flat[o:o + 4 * 3072].reshape(4, 3072)
    o += 12 * D_MODEL
    full_scw = flat[o:o + 4 * 1536].reshape(4, 1536)
    o += 6 * D_MODEL
    sg["ssd_conv_b"] = flat[o:o + 1536]
    o += 2 * D_MODEL
    full_fcw = flat[o:o + 3 * 2 * D_FF].reshape(3, 2 * D_FF)
    o += 17 * D_MODEL
    sg["ffn_conv_b"] = flat[o:o + 2 * D_FF]
    for k, full in (("gdn_conv_w", full_gcw), ("ssd_conv_w", full_scw), ("ffn_conv_w", full_fcw)):
        cw = w[k].shape[1]
        sg[k] = lax.dynamic_slice_in_dim(full, idx * cw, cw, axis=1)
    small_names = [k for k in names if k not in big]
    rows = 24
    gpk = _pack([sg[k] for k in small_names], rows)
    dpk, mpk, vpk = _adam_small(gpk, _pack([w[k] for k in small_names], rows), _pack([m[k] for k in small_names], rows),
                                _pack([v[k] for k in small_names], rows))
    shapes = [w[k].shape for k in small_names]
    for k, g_, d_, m_, v_ in zip(small_names, _unpack(gpk, shapes), _unpack(dpk, shapes), _unpack(mpk, shapes), _unpack(vpk, shapes)):
        grads[k], deltas[k], new_m[k], new_v[k] = g_, d_, m_, v_

    loss = extra[1, 0]
    lead = lambda a: a[None]
    return (loss, dx, *[lead(grads[k]) for k in names], *[lead(deltas[k]) for k in names],
            *[lead(new_m[k]) for k in names], *[lead(new_v[k]) for k in names])
```

```python
import functools

import jax
import jax.numpy as jnp
from jax import lax
from jax.experimental import pallas as pl
from jax.experimental.pallas import tpu as pltpu

F32 = jnp.float32
BF16 = jnp.bfloat16
MXU_DTYPE = jnp.bfloat16
HIGHEST = lax.Precision.HIGHEST
VMEM_LIMIT_V7X = 48 * 1024 * 1024
SUBLANES = 8
LANES = 128

D_MODEL = 1024
GDN_HEADS = 8
GDN_DK = 128
SSD_HEADS = 16
SSD_P = 64
SSD_GROUPS = 2
SSD_HPG = 8
SSD_N = 128
CHUNK = 64
D_FF = 2816
EPS = 1e-6
N_DEV = 8
PROJ_W = 7168
SMALL_CB = 52
D_IN = 6688

ADAM_LR = 0.001
ADAM_B1 = 0.9
ADAM_B2 = 0.999
ADAM_EPS = 1e-08
ADAM_WD = 0.01
ADAM_STEP = 10

NN = (((1,), (0,)), ((), ()))
NT = (((1,), (1,)), ((), ()))
TN = (((0,), (0,)), ((), ()))


def _pcall(body, **kw):
    return pl.pallas_call(body, **kw)


def _mm(a, b, dims=NN):
    return lax.dot_general(a.astype(MXU_DTYPE), b.astype(MXU_DTYPE), dims, preferred_element_type=F32)


def _mmx(a, b, dims=NN):
    return lax.dot_general(a, b, dims, precision=HIGHEST, preferred_element_type=F32)


def _split(a):
    hi = a.astype(MXU_DTYPE)
    return hi, (a - hi.astype(F32)).astype(MXU_DTYPE)


def _mm3(a, b, dims=NN):
    (ah, al), (bh, bl) = _split(a), _split(b)
    dot = lambda p, q: lax.dot_general(p, q, dims, preferred_element_type=F32)
    return dot(ah, bh) + (dot(ah, bl) + dot(al, bh))


def _mmsel(a, sel, dims=NN, terms=2):
    s = sel.astype(MXU_DTYPE)
    out = None
    for _ in range(terms):
        part = a.astype(MXU_DTYPE)
        a = a - part.astype(F32)
        prod = lax.dot_general(part, s, dims, preferred_element_type=F32)
        out = prod if out is None else out + prod
    return out


def _sigmoid(x):
    return 0.5 * jnp.tanh(0.5 * x) + 0.5


def _softplus(x):
    return jnp.maximum(x, 0.0) + jnp.log(1.0 + jnp.exp(-jnp.abs(x)))


def _dsilu(x, s):
    return s * (1.0 + x * (1.0 - s))


def _rowsum(x):
    return jnp.sum(x, axis=1, keepdims=True)


def _colsum(x):
    return jnp.sum(x, axis=0, keepdims=True)


def _pick(dim, pref):
    if dim <= pref:
        return dim
    best = None
    t = LANES
    while t <= pref:
        if dim % t == 0:
            best = t
        t += LANES
    return dim if best is None else best


def _params(sem):
    return pltpu.CompilerParams(dimension_semantics=sem, vmem_limit_bytes=VMEM_LIMIT_V7X)


def _matmul(name, a, b, mode, out_dtype, tm=1024, tn=1024, tk=1024, scatter_riders=()):
    if mode == "nn":
        (m, k), (_, n) = a.shape, b.shape
    elif mode == "nt":
        (m, k), (n, _) = a.shape, b.shape
    else:
        (k, m), (_, n) = a.shape, b.shape
    tm, tn, tk = _pick(m, tm), _pick(n, tn), _pick(k, tk)
    nk = k // tk
    if mode == "tn":
        a_spec = pl.BlockSpec((tk, tm), lambda i, j, kk: (kk, i))
    else:
        a_spec = pl.BlockSpec((tm, tk), lambda i, j, kk: (i, kk))
    if mode == "nt":
        b_spec = pl.BlockSpec((tn, tk), lambda i, j, kk: (j, kk))
    else:
        b_spec = pl.BlockSpec((tk, tn), lambda i, j, kk: (kk, j))
    dims = {"nn": NN, "nt": NT, "tn": TN}[mode]

    def body(a_ref, b_ref, o_ref, *acc):
        if nk == 1:
            o_ref[...] = _mm(a_ref[...], b_ref[...], dims).astype(out_dtype)
            return
        kk = pl.program_id(2)

        @pl.when(kk == 0)
        def _():
            acc[0][...] = jnp.zeros_like(acc[0])

        acc[0][...] += _mm(a_ref[...], b_ref[...], dims)

        @pl.when(kk == nk - 1)
        def _():
            o_ref[...] = acc[0][...].astype(out_dtype)

    grid = (m // tm, n // tn, nk)
    riders = list(scatter_riders)
    any_spec, rider_shapes, rider_sems, wrap = _riding_exchange(riders, True, 2, 1, grid)
    res = _pcall(
        wrap(body), name=name, grid=grid,
        in_specs=[a_spec, b_spec] + any_spec,
        out_specs=[pl.BlockSpec((tm, tn), lambda i, j, kk: (i, j))] + any_spec,
        out_shape=[jax.ShapeDtypeStruct((m, n), out_dtype)] + rider_shapes,
        scratch_shapes=([pltpu.VMEM((tm, tn), F32)] if nk > 1 else []) + rider_sems,
        compiler_params=_params(("arbitrary", "arbitrary", "arbitrary") if riders else ("parallel", "parallel", "arbitrary")),
    )(a, b, *riders)
    return (res[0], res[1:]) if riders else res[0]


def _rowwise(name, body, n_rows, tm, ins, outs, accs=()):
    arrays, in_specs = [], []
    last8 = n_rows // SUBLANES - 1
    per = tm // SUBLANES
    for spec in ins:
        kind, arr = spec[0], spec[1]
        if kind == "full":
            in_specs.append(pl.BlockSpec(arr.shape, lambda i, nd=arr.ndim: (0,) * nd))
        else:
            w, cb = spec[2], spec[3]
            if kind == "row":
                in_specs.append(pl.BlockSpec((tm, w), lambda i, cb=cb: (i, cb)))
            elif kind == "prev":
                in_specs.append(pl.BlockSpec((SUBLANES, w), lambda i, cb=cb: (jnp.maximum(i * per - 1, 0), cb)))
            else:
                in_specs.append(pl.BlockSpec((SUBLANES, w), lambda i, cb=cb: (jnp.minimum((i + 1) * per, last8), cb)))
        arrays.append(arr)
    out_shape = [jax.ShapeDtypeStruct((n_rows, w), dt) for (w, dt) in outs]
    out_shape += [jax.ShapeDtypeStruct(s, F32) for s in accs]
    out_specs = [pl.BlockSpec((tm, w), lambda i: (i, 0)) for (w, _) in outs]
    out_specs += [pl.BlockSpec(s, lambda i: (0, 0)) for s in accs]
    n_io = len(ins) + len(outs)

    def kern(*refs):
        i = pl.program_id(0)
        if accs:
            @pl.when(i == 0)
            def _():
                for r in refs[n_io:]:
                    r[...] = jnp.zeros_like(r)
        body(i, *refs)

    res = _pcall(
        kern, name=name, grid=(n_rows // tm,), in_specs=in_specs, out_specs=out_specs, out_shape=out_shape,
        compiler_params=_params(("arbitrary",)),
    )(*arrays)
    return res


def _shift_down(x, halo, j):
    r = pltpu.roll(x, j, 0)
    hr = pltpu.roll(halo, j, 0)
    rows = lax.broadcasted_iota(jnp.int32, (SUBLANES, x.shape[1]), 0)
    top = jnp.where(rows < j, hr, r[0:SUBLANES])
    return jnp.concatenate([top, r[SUBLANES:]], axis=0)


def _shift_up(x, halo, j):
    tm = x.shape[0]
    r = pltpu.roll(x, tm - j, 0)
    hr = pltpu.roll(halo, SUBLANES - j, 0)
    rows = lax.broadcasted_iota(jnp.int32, (SUBLANES, x.shape[1]), 0)
    bot = jnp.where(rows >= SUBLANES - j, hr, r[tm - SUBLANES:])
    return jnp.concatenate([r[:tm - SUBLANES], bot], axis=0)


def _conv_taps(x, halo, kw):
    return [x if kw - 1 - k == 0 else _shift_down(x, halo, kw - 1 - k) for k in range(kw)]


def _conv(taps, w):
    y = taps[0] * w[0:1]
    for k in range(1, len(taps)):
        y = y + taps[k] * w[k:k + 1]
    return y


def _rms(x, width):
    r = lax.rsqrt(jnp.sum(x * x, axis=-1, keepdims=True) * (1.0 / width) + EPS)
    return x * r, r


def _rms_bwd(xh, r, dxh, width):
    return r * (dxh - xh * (jnp.sum(dxh * xh, axis=-1, keepdims=True) * (1.0 / width)))


def _seq_flags(i, seq, tm):
    nps = seq // tm
    pos = i % nps
    return jnp.where(pos == 0, 0.0, 1.0), jnp.where(pos == nps - 1, 0.0, 1.0)


def _norm_cast(name, x, w, tm):
    t, d = x.shape

    def body(i, x_ref, w_ref, h_ref):
        xh, _ = _rms(x_ref[...], d)
        h_ref[...] = (xh * w_ref[...]).astype(BF16)

    return _rowwise(name, body, t, tm, [("row", x, d, 0), ("full", w)], [(d, BF16)])[0]


def _gdn_prep(proj, cw, gp, seq, tm):
    t = proj.shape[0]
    d = D_MODEL

    def body(i, q_ref, qh_ref, k_ref, kh_ref, v_ref, vh_ref, sm_ref, cw_ref, gp_ref, qn_ref, kn_ref, vv_ref, gs_ref):
        keep, _ = _seq_flags(i, seq, tm)
        for x_ref, h_ref, o_ref, off, scale in ((q_ref, qh_ref, qn_ref, 0, GDN_DK ** -0.5),
                                               (k_ref, kh_ref, kn_ref, d, 1.0), (v_ref, vh_ref, vv_ref, 2 * d, None)):
            y = _conv(_conv_taps(x_ref[...], h_ref[...] * keep, 4), cw_ref[:, off:off + d])
            a = y * _sigmoid(y)
            if scale is None:
                o_ref[...] = a
            else:
                for hh in range(GDN_HEADS):
                    s = a[:, hh * GDN_DK:(hh + 1) * GDN_DK]
                    n = lax.rsqrt(_rowsum(s * s) + EPS)
                    o_ref[:, hh * GDN_DK:(hh + 1) * GDN_DK] = s * (n * scale)
        sm = sm_ref[...]
        lane = lax.broadcasted_iota(jnp.int32, sm.shape, 1)
        beta = _sigmoid(sm)
        g = jnp.where((lane >= 8) & (lane < 16), -jnp.exp(gp_ref[0:1, :]) * _softplus(sm + gp_ref[1:2, :]), 0.0)
        gs_ref[...] = jnp.where(lane < 8, beta, _mmx(_block_tri(tm, False), g))

    ins = []
    for cb in range(3):
        ins += [("row", proj, d, cb), ("prev", proj, d, cb)]
    ins += [("row", proj, LANES, SMALL_CB), ("full", cw), ("full", gp)]
    return _rowwise("gdn_prep", body, t, tm, ins, [(d, F32), (d, F32), (d, F32), (LANES, F32)])


def _block_tri(tm, upper):
    ri = lax.broadcasted_iota(jnp.int32, (tm, tm), 0)
    ci = lax.broadcasted_iota(jnp.int32, (tm, tm), 1)
    tri = (ri <= ci) if upper else (ri >= ci)
    return (tri & ((ri // CHUNK) == (ci // CHUNK))).astype(F32)


def _chunk_consts():
    row = lax.broadcasted_iota(jnp.int32, (CHUNK, CHUNK), 0)
    col = lax.broadcasted_iota(jnp.int32, (CHUNK, CHUNK), 1)
    return dict(
        tril=row >= col, strict=row > col, eye=(row == col).astype(F32),
        lane=lax.broadcasted_iota(jnp.int32, (CHUNK, LANES), 1),
        row1=lax.broadcasted_iota(jnp.int32, (CHUNK, 1), 0),
        ones=jnp.ones((CHUNK, LANES), F32))


def _hmap(fn, *lists):
    return [fn(*a) for a in zip(*lists)]


def _tri_inv(nmats, eye):
    x = [eye - n for n in nmats]
    p = _hmap(_mm3, nmats, nmats)
    for lvl in range(5):
        x = _hmap(lambda xi, pi: xi + _mm3(xi, pi), x, p)
        if lvl < 4:
            p = _hmap(_mm3, p, p)
    return x


def _gdn_gates(gs, gc_row, h, c):
    beta = _rowsum(jnp.where(c["lane"] == h, gs, 0.0))
    gc = _rowsum(jnp.where(c["lane"] == h + 8, gs, 0.0))
    dc = jnp.exp(jnp.where(c["tril"], gc - gc_row, -1e30))
    gl = gc[CHUNK - 1:CHUNK, :]
    return beta, dc, jnp.exp(gc), jnp.exp(gl), jnp.exp(gl - gc)


GDN_HB = GDN_HEADS


def _gdn_specs(seq, sb, hb, backward):
    assert hb == GDN_HEADS
    nsb = seq // sb
    ncb = sb // CHUNK
    order = (lambda j: nsb - 1 - j) if backward else (lambda j: j)
    specs = dict(
        wide=lambda: pl.BlockSpec((1, sb, hb * GDN_DK), lambda b, h, j: (b, order(j), h)),
        gs=pl.BlockSpec((1, sb, LANES), lambda b, h, j: (b, order(j), 0)),
        gr=pl.BlockSpec((1, ncb, GDN_HEADS, CHUNK), lambda b, h, j: (b, order(j), 0, 0)),
        st=pl.BlockSpec((1, hb, ncb * GDN_DK, GDN_DK), lambda b, h, j: (b, h, order(j), 0)),
        ti=pl.BlockSpec((1, hb, sb, CHUNK), lambda b, h, j: (b, h, order(j), 0)))
    return nsb, ncb, specs


def _riding_exchange(arrays, scatter, n_in, n_out, grid):
    n = len(arrays)
    if n == 0:
        return [], [], [], lambda body: body
    any_spec = [pl.BlockSpec(memory_space=pl.ANY)] * n

    def wrap(body):
        def wrapped(*refs):
            ins = refs[n_in:n_in + n]
            outs = refs[n_in + n + n_out:n_in + 2 * n + n_out]
            sems = refs[len(refs) - 3:]
            pid = [pl.program_id(a) for a in range(len(grid))]
            first = functools.reduce(lambda a, b: a & b, [p == 0 for p in pid])
            last = functools.reduce(lambda a, b: a & b, [p == g - 1 for p, g in zip(pid, grid)])

            @pl.when(first)
            def _():
                _exchange_phase(ins, outs, sems, scatter, start=True)

            body(*refs[:n_in], *refs[n_in + n:n_in + n + n_out], *refs[n_in + 2 * n + n_out:len(refs) - 3])

            @pl.when(last)
            def _():
                _exchange_phase(ins, outs, sems, scatter, start=False)

        return wrapped

    return any_spec, _exchange_out_shapes(arrays, scatter), _exchange_sems(n), wrap


def _gdn_chunk_fwd(qn, kn, vv, gs, gr, bsz, seq, sb, riders):
    hb = GDN_HB
    nsb, ncb, sp = _gdn_specs(seq, sb, hb, False)
    grid = (bsz, GDN_HEADS // hb, nsb)
    any_spec, rider_shapes, rider_sems, wrap = _riding_exchange(riders, False, 5, 3, grid)

    def body(q_ref, k_ref, v_ref, gs_ref, gr_ref, o_ref, st_ref, ti_ref, s_scr):
        hg = pl.program_id(1)

        @pl.when(pl.program_id(2) == 0)
        def _():
            s_scr[...] = jnp.zeros_like(s_scr)

        c = _chunk_consts()

        def chunk(n, carry):
            r = pl.ds(pl.multiple_of(n * CHUNK, CHUNK), CHUNK)
            rs = pl.ds(pl.multiple_of(n * GDN_DK, GDN_DK), GDN_DK)
            gsv = gs_ref[0, r, :]
            heads = list(range(hb))
            sls = [slice(ih * GDN_DK, (ih + 1) * GDN_DK) for ih in heads]
            q = [q_ref[0, r, sl] for sl in sls]
            k = [k_ref[0, r, sl] for sl in sls]
            v = [v_ref[0, r, sl] for sl in sls]
            beta, dc, eg, egl, ekd = zip(*[
                _gdn_gates(gsv, gr_ref[0, n, pl.ds(ih, 1), :], ih, c) for ih in heads])
            kb = _hmap(lambda a, b: a * b, k, beta)
            amat = _hmap(lambda a, b, d_: jnp.where(c["strict"], _mm(a, b, NT) * d_, 0.0), kb, k, dc)
            tinv = _tri_inv(amat, c["eye"])
            u = _hmap(lambda t_, a, b: _mm3(t_, a * b), tinv, v, beta)
            w = _hmap(lambda t_, a, b: _mm3(t_, a * b), tinv, kb, eg)
            qk = _hmap(lambda a, b, d_: _mm(a, b, NT) * d_, q, k, dc)
            s = [s_scr[ih] for ih in heads]
            v_new = _hmap(lambda a, b, s_: a - _mm(b, s_), u, w, s)
            o = _hmap(lambda a, e, s_, qk_, vn: _mm(a * e, s_) + _mm(qk_, vn), q, eg, s, qk, v_new)
            s_new = _hmap(lambda s_, e, a, f, vn: s_ * e + _mm(a * f, vn, TN), s, egl, k, ekd, v_new)
            for ih in heads:
                o_ref[0, r, sls[ih]] = o[ih]
                st_ref[0, ih, rs, :] = s[ih]
                ti_ref[0, ih, r, :] = tinv[ih]
                s_scr[ih] = s_new[ih]
            return carry

        lax.fori_loop(0, ncb, chunk, 0)

    t3 = (bsz, seq, D_MODEL)
    res = _pcall(
        wrap(body), name="gdn_chunk_fwd", grid=grid,
        in_specs=[sp["wide"](), sp["wide"](), sp["wide"](), sp["gs"], sp["gr"]] + any_spec,
        out_specs=[sp["wide"](), sp["st"], sp["ti"]] + any_spec,
        out_shape=[jax.ShapeDtypeStruct(t3, F32),
                   jax.ShapeDtypeStruct((bsz, GDN_HEADS, (seq // CHUNK) * GDN_DK, GDN_DK), F32),
                   jax.ShapeDtypeStruct((bsz, GDN_HEADS, seq, CHUNK), F32)] + rider_shapes,
        scratch_shapes=[pltpu.VMEM((hb, GDN_DK, GDN_DK), F32)] + rider_sems,
        compiler_params=_params(("arbitrary", "arbitrary", "arbitrary")),
    )(qn, kn, vv, gs, gr, *riders)
    return res[:3], res[3:]


def _ssd_prep(proj, cw, cb, sp, seq, tm):
    t = proj.shape[0]
    d = D_MODEL
    ssd_w = SSD_HEADS * SSD_P

    def body(i, x_ref, xh_ref, bc_ref, bch_ref, sm_ref, cw_ref, cb_ref, sp_ref, xs_ref, bco_ref, dtx_ref, acsx_ref, acs_ref):
        keep, _ = _seq_flags(i, seq, tm)
        y = _conv(_conv_taps(x_ref[...], xh_ref[...] * keep, 4), cw_ref[:, 0:d]) + cb_ref[:, 0:d]
        xs_ref[...] = y * _sigmoid(y)
        y = _conv(_conv_taps(bc_ref[...], bch_ref[...] * keep, 4), cw_ref[:, d:d + 512]) + cb_ref[:, d:d + 512]
        bco_ref[...] = y * _sigmoid(y)
        sm = sm_ref[...]
        lane = lax.broadcasted_iota(jnp.int32, sm.shape, 1)
        valid = (lane >= 16) & (lane < 32)
        dt = jnp.where(valid, _softplus(sm + sp_ref[1:2, :]), 0.0)
        adt = dt * (-jnp.exp(sp_ref[0:1, :]))
        acs = _mmx(_block_tri(tm, False), adt)
        l64 = lax.broadcasted_iota(jnp.int32, (LANES, ssd_w), 0)
        d64 = lax.broadcasted_iota(jnp.int32, (LANES, ssd_w), 1)
        e64 = (l64 - 16 == d64 // SSD_P).astype(F32)
        dtx_ref[...] = _mmsel(dt, e64, terms=3)
        acsx_ref[...] = _mmsel(acs, e64, terms=3)
        acs_ref[...] = acs

    ins = [("row", proj, d, 5), ("prev", proj, d, 5), ("row", proj, 512, 12), ("prev", proj, 512, 12),
           ("row", proj, LANES, SMALL_CB), ("full", cw), ("full", cb), ("full", sp)]
    return _rowwise("ssd_prep", body, t, tm, ins, [(d, F32), (512, F32), (ssd_w, F32), (ssd_w, F32), (LANES, F32)])


SSD_GW = SSD_HPG * SSD_P


def _ssd_head(acs, ar_ref, n, head, cbm, c):
    col = _rowsum(jnp.where(c["lane"] == head + 16, acs, 0.0))
    lm = jnp.exp(jnp.where(c["tril"], col - ar_ref[0, head, pl.ds(n, 1), :], -1e30))
    return lm, cbm * lm


def _ssd_specs(seq, sb):
    nsb = seq // sb
    ncb = sb // CHUNK
    def specs(order):
        return dict(
            wide=lambda: pl.BlockSpec((1, sb, SSD_HEADS * SSD_P), lambda b, j: (b, order(j), 0)),
            bc=lambda: pl.BlockSpec((1, sb, 2 * SSD_GROUPS * SSD_N), lambda b, j: (b, order(j), 0)),
            half=lambda: pl.BlockSpec((1, sb, SSD_GROUPS * SSD_N), lambda b, j: (b, order(j), 0)),
            small=lambda: pl.BlockSpec((1, sb, LANES), lambda b, j: (b, order(j), 0)),
            ar=pl.BlockSpec((1, SSD_HEADS, ncb, CHUNK), lambda b, j: (b, 0, order(j), 0)),
            st=pl.BlockSpec((1, ncb * SSD_N, SSD_HEADS * SSD_P), lambda b, j: (b, order(j), 0)))
    return nsb, ncb, specs(lambda j: j), specs(lambda j: nsb - 1 - j)


def _ssd_chunk_fwd(xs, bc, dtx, acsx, acs, ar, bsz, seq, sb):
    nsb, ncb, sp, _ = _ssd_specs(seq, sb)

    def body(x_ref, dtx_ref, ax_ref, bc_ref, acs_ref, ar_ref, y_ref, sts_ref, st_scr):
        @pl.when(pl.program_id(1) == 0)
        def _():
            st_scr[...] = jnp.zeros_like(st_scr)

        c = _chunk_consts()
        lane5 = lax.broadcasted_iota(jnp.int32, (CHUNK, SSD_GW), 1) // SSD_P

        def chunk(n, carry):
            r = pl.ds(pl.multiple_of(n * CHUNK, CHUNK), CHUNK)
            rs = pl.ds(pl.multiple_of(n * SSD_N, SSD_N), SSD_N)
            acsv = acs_ref[0, r, :]
            for g in range(SSD_GROUPS):
                gl = slice(g * SSD_GW, (g + 1) * SSD_GW)
                x, dt, ax = x_ref[0, r, gl], dtx_ref[0, r, gl], ax_ref[0, r, gl]
                bm = bc_ref[0, r, g * SSD_N:(g + 1) * SSD_N]
                cm = bc_ref[0, r, (SSD_GROUPS + g) * SSD_N:(SSD_GROUPS + g + 1) * SSD_N]
                xdt = x * dt
                cbm = _mm(cm, bm, NT)
                al = ax[CHUNK - 1:CHUNK, :]
                st = st_scr[:, gl]
                y = _mm(cm, st) * jnp.exp(ax)
                for hh in range(SSD_HPG):
                    _, gm = _ssd_head(acsv, ar_ref, n, g * SSD_HPG + hh, cbm, c)
                    y = y + _mm(gm, jnp.where(lane5 == hh, xdt, 0.0))
                y_ref[0, r, gl] = y
                sts_ref[0, rs, gl] = st
                st_scr[:, gl] = st * jnp.exp(al) + _mm(bm, xdt * jnp.exp(al - ax), TN)
            return carry

        lax.fori_loop(0, ncb, chunk, 0)

    return _pcall(
        body, name="ssd_chunk_fwd", grid=(bsz, nsb),
        in_specs=[sp["wide"](), sp["wide"](), sp["wide"](), sp["bc"](), sp["small"](), sp["ar"]],
        out_specs=[sp["wide"](), sp["st"]],
        out_shape=[jax.ShapeDtypeStruct((bsz, seq, SSD_HEADS * SSD_P), F32),
                   jax.ShapeDtypeStruct((bsz, (seq // CHUNK) * SSD_N, SSD_HEADS * SSD_P), F32)],
        scratch_shapes=[pltpu.VMEM((SSD_N, SSD_HEADS * SSD_P), F32)],
        compiler_params=_params(("parallel", "arbitrary")),
    )(xs, dtx, acsx, bc, acs, ar)


def _gate_norm(o_gdn, y_ssd, xs, proj, gnw, snw, dvec, tm):
    t = o_gdn.shape[0]
    d = D_MODEL

    def body(i, o_ref, za_ref, y_ref, xs_ref, zs_ref, gnw_ref, snw_ref, dv_ref, out_ref):
        for hh in range(GDN_HEADS):
            sl = slice(hh * GDN_DK, (hh + 1) * GDN_DK)
            oh, _ = _rms(o_ref[:, sl], GDN_DK)
            z = za_ref[:, sl]
            out_ref[:, sl] = (oh * gnw_ref[...] * (z * _sigmoid(z))).astype(BF16)
        zs = zs_ref[...]
        yg = (y_ref[...] + dv_ref[...] * xs_ref[...]) * (zs * _sigmoid(zs))
        for g in range(SSD_GROUPS):
            sl = slice(g * 512, (g + 1) * 512)
            yh, _ = _rms(yg[:, sl], 512)
            out_ref[:, d + g * 512:d + (g + 1) * 512] = (yh * snw_ref[:, sl]).astype(BF16)

    ins = [("row", o_gdn, d, 0), ("row", proj, d, 3), ("row", y_ssd, d, 0), ("row", xs, d, 0), ("row", proj, d, 4),
           ("full", gnw), ("full", snw), ("full", dvec)]
    return _rowwise("gate_norm", body, t, tm, ins, [(2 * d, BF16)])[0]


def _mid(x, mix, pmw, pfw, tm):
    t, d = x.shape

    def body(i, x_ref, mix_ref, pmw_ref, pfw_ref, x1_ref, h2_ref):
        mh, _ = _rms(mix_ref[...], d)
        x1 = x_ref[...] + mh * pmw_ref[...]
        x1_ref[...] = x1
        xh, _ = _rms(x1, d)
        h2_ref[...] = (xh * pfw_ref[...]).astype(BF16)

    return _rowwise("mid", body, t, tm, [("row", x, d, 0), ("row", mix, d, 0), ("full", pmw), ("full", pfw)],
                    [(d, F32), (d, BF16)])


def _ffn_gate_up(ug_ref, ugh_ref, uu_ref, uuh_ref, cw_ref, cb_ref, keep):
    tg = _conv_taps(ug_ref[...], ugh_ref[...] * keep, 3)
    tu = _conv_taps(uu_ref[...], uuh_ref[...] * keep, 3)
    gate = _conv(tg, cw_ref[:, 0:D_FF]) + cb_ref[:, 0:D_FF]
    up = _conv(tu, cw_ref[:, D_FF:2 * D_FF]) + cb_ref[:, D_FF:2 * D_FF]
    return tg, tu, gate, up


def _ffn_act(u_pre, cw, cb, seq, tm):
    t = u_pre.shape[0]

    def body(i, ug_ref, ugh_ref, uu_ref, uuh_ref, cw_ref, cb_ref, act_ref):
        keep, _ = _seq_flags(i, seq, tm)
        _, _, gate, up = _ffn_gate_up(ug_ref, ugh_ref, uu_ref, uuh_ref, cw_ref, cb_ref, keep)
        act_ref[...] = (gate * _sigmoid(gate) * up).astype(BF16)

    ins = [("row", u_pre, D_FF, 0), ("prev", u_pre, D_FF, 0), ("row", u_pre, D_FF, 1), ("prev", u_pre, D_FF, 1),
           ("full", cw), ("full", cb)]
    return _rowwise("ffn_act", body, t, tm, ins, [(D_FF, BF16)])[0]


def _final(x1, f, tgt, w, tm):
    t, d = x1.shape

    def body(i, x1_ref, f_ref, t_ref, w_ref, dy_ref, df_ref, loss_ref, dw_ref):
        fh, r = _rms(f_ref[...], d)
        e = x1_ref[...] + fh * w_ref[...] - t_ref[...]
        loss_ref[...] += _colsum(e * e) * (0.5 / d)
        dy = e * (1.0 / d)
        dy_ref[...] = dy
        dw_ref[...] += _colsum(dy * fh)
        df_ref[...] = _rms_bwd(fh, r, dy * w_ref[...], d).astype(BF16)

    return _rowwise("final", body, t, tm, [("row", x1, d, 0), ("row", f, d, 0), ("row", tgt, d, 0), ("full", w)],
                    [(d, F32), (d, BF16)], accs=[(1, d), (1, d)])


def _ffn_bwd(u_pre, dact, cw, cb, seq, tm):
    t = u_pre.shape[0]

    def body(i, ug_ref, ugh_ref, ugn_ref, uu_ref, uuh_ref, uun_ref, da_ref, dan_ref, cw_ref, cb_ref, dpre_ref, dcw_ref, dcb_ref):
        keep, keep_next = _seq_flags(i, seq, tm)
        ext = lambda a_ref, n_ref: jnp.concatenate([a_ref[...], n_ref[...]], axis=0)
        rows = tm + SUBLANES
        tg = _conv_taps(ext(ug_ref, ugn_ref), ugh_ref[...] * keep, 3)
        tu = _conv_taps(ext(uu_ref, uun_ref), uuh_ref[...] * keep, 3)
        gate = _conv(tg, cw_ref[:, 0:D_FF]) + cb_ref[:, 0:D_FF]
        up = _conv(tu, cw_ref[:, D_FF:2 * D_FF]) + cb_ref[:, D_FF:2 * D_FF]
        sg = _sigmoid(gate)
        da = jnp.concatenate([da_ref[...], dan_ref[...] * keep_next], axis=0)
        for off, grad, taps in ((0, da * up * _dsilu(gate, sg), tg), (D_FF, da * gate * sg, tu)):
            own = grad[0:tm]
            acc = own * cw_ref[2:3, off:off + D_FF]
            for j in (1, 2):
                acc = acc + pltpu.roll(grad, rows - j, 0)[0:tm] * cw_ref[2 - j:3 - j, off:off + D_FF]
            dpre_ref[:, off:off + D_FF] = acc.astype(BF16)
            dcb_ref[:, off:off + D_FF] += _colsum(own)
            for k in range(3):
                dcw_ref[k:k + 1, off:off + D_FF] += _colsum(own * taps[k][0:tm])

    ins = []
    for cb_ in range(2):
        ins += [("row", u_pre, D_FF, cb_), ("prev", u_pre, D_FF, cb_), ("next", u_pre, D_FF, cb_)]
    ins += [("row", dact, D_FF, 0), ("next", dact, D_FF, 0), ("full", cw), ("full", cb)]
    return _rowwise("ffn_bwd", body, t, tm, ins, [(2 * D_FF, BF16)], accs=[(SUBLANES, 2 * D_FF), (1, 2 * D_FF)])


def _assemble_dproj(dpre_gdn, dza, dzs, dpre_ssd, dsm, gcw, scw, seq, tm):
    t = dpre_gdn.shape[0]
    d = D_MODEL

    def body(i, dg_ref, dgn_ref, dza_ref, dzs_ref, ds_ref, dsn_ref, dsm_ref, gcw_ref, scw_ref, o_ref):
        _, keep = _seq_flags(i, seq, tm)
        pieces = [(dg_ref, dgn_ref, gcw_ref, 0, c0) for c0 in range(0, 3 * d, d)]
        pieces += [(ds_ref, dsn_ref, scw_ref, 5 * d, c0) for c0 in (0, d)]
        for d_ref, n_ref, cw_ref, base, c0 in pieces:
            w = min(d, d_ref.shape[1] - c0)
            x = d_ref[:, c0:c0 + w]
            halo = n_ref[:, c0:c0 + w] * keep
            acc = x * cw_ref[3:4, c0:c0 + w]
            for j in range(1, 4):
                acc = acc + _shift_up(x, halo, j) * cw_ref[3 - j:4 - j, c0:c0 + w]
            o_ref[:, base + c0:base + c0 + w] = acc.astype(BF16)
        o_ref[:, 3 * d:4 * d] = dza_ref[...]
        o_ref[:, 4 * d:5 * d] = dzs_ref[...]
        o_ref[:, 6 * d + 512:6 * d + 512 + LANES] = dsm_ref[...]
        o_ref[:, 6 * d + 512 + LANES:PROJ_W] = jnp.zeros((tm, PROJ_W - (6 * d + 512 + LANES)), BF16)

    ins = [("row", dpre_gdn, 3 * d, 0), ("next", dpre_gdn, 3 * d, 0), ("row", dza, d, 0), ("row", dzs, d, 0),
           ("row", dpre_ssd, d + 512, 0), ("next", dpre_ssd, d + 512, 0), ("row", dsm, LANES, 0), ("full", gcw), ("full", scw)]
    return _rowwise("assemble_dproj", body, t, tm, ins, [(PROJ_W, BF16)])[0]


def _mid_bwd(x1, mix, dy, dh2, pmw, pfw, tm):
    t, d = x1.shape

    def body(i, x1_ref, mix_ref, dy_ref, dh2_ref, pmw_ref, pfw_ref, dx1_ref, dmix_ref, dpm_ref, dpf_ref):
        xh, r2 = _rms(x1_ref[...], d)
        dh2 = dh2_ref[...]
        dpf_ref[...] += _colsum(dh2 * xh)
        dx1 = dy_ref[...] + _rms_bwd(xh, r2, dh2 * pfw_ref[...], d)
        dx1_ref[...] = dx1
        mh, r = _rms(mix_ref[...], d)
        dpm_ref[...] += _colsum(dx1 * mh)
        dmix_ref[...] = _rms_bwd(mh, r, dx1 * pmw_ref[...], d).astype(BF16)

    ins = [("row", x1, d, 0), ("row", mix, d, 0), ("row", dy, d, 0), ("row", dh2, d, 0), ("full", pmw), ("full", pfw)]
    return _rowwise("mid_bwd", body, t, tm, ins, [(d, F32), (d, BF16)], accs=[(1, d), (1, d)])


def _gate_norm_bwd(o_gdn, y_ssd, xs, proj, dmixin, gnw, snw, dvec, tm):
    t = o_gdn.shape[0]
    d = D_MODEL

    def body(i, o_ref, za_ref, y_ref, xs_ref, zs_ref, dma_ref, dms_ref, gnw_ref, snw_ref, dv_ref,
             do_ref, dza_ref, dy_ref, dxs_ref, dzs_ref, dgnw_ref, dsnw_ref, dd_ref):
        for hh in range(GDN_HEADS):
            sl = slice(hh * GDN_DK, (hh + 1) * GDN_DK)
            oh, r = _rms(o_ref[:, sl], GDN_DK)
            z = za_ref[:, sl]
            sz = _sigmoid(z)
            dm = dma_ref[:, sl]
            don = dm * (z * sz)
            dza_ref[:, sl] = (dm * oh * gnw_ref[...] * _dsilu(z, sz)).astype(BF16)
            dgnw_ref[...] += _colsum(don * oh)
            do_ref[:, sl] = _rms_bwd(oh, r, don * gnw_ref[...], GDN_DK)
        zs = zs_ref[...]
        sz = _sigmoid(zs)
        sil = zs * sz
        x = xs_ref[...]
        y0 = y_ref[...] + dv_ref[...] * x
        yg = y0 * sil
        dms = dms_ref[...]
        for g in range(SSD_GROUPS):
            sl = slice(g * 512, (g + 1) * 512)
            yh, r = _rms(yg[:, sl], 512)
            dsnw_ref[:, sl] += _colsum(dms[:, sl] * yh)
            dyg = _rms_bwd(yh, r, dms[:, sl] * snw_ref[:, sl], 512)
            dy0 = dyg * sil[:, sl]
            dzs_ref[:, sl] = (dyg * y0[:, sl] * _dsilu(zs[:, sl], sz[:, sl])).astype(BF16)
            dy_ref[:, sl] = dy0
            dxs_ref[:, sl] = dy0 * dv_ref[:, sl]
            dd_ref[:, sl] += _colsum(dy0 * x[:, sl])

    ins = [("row", o_gdn, d, 0), ("row", proj, d, 3), ("row", y_ssd, d, 0), ("row", xs, d, 0), ("row", proj, d, 4),
           ("row", dmixin, d, 0), ("row", dmixin, d, 1), ("full", gnw), ("full", snw), ("full", dvec)]
    return _rowwise("gate_norm_bwd", body, t, tm, ins, [(d, F32), (d, BF16), (d, F32), (d, F32), (d, BF16)],
                    accs=[(1, GDN_DK), (1, d), (1, d)])


def _ssd_chunk_bwd(xs, bc, dtx, acsx, acs, ar, dy, sts, bsz, seq, sb):
    nsb, ncb, _, sp = _ssd_specs(seq, sb)

    def body(x_ref, dtx_ref, ax_ref, bc_ref, acs_ref, ar_ref, dy_ref, sts_ref, dx_ref, dbc_ref, ddt_ref, dacs_ref, dst_scr):
        @pl.when(pl.program_id(1) == 0)
        def _():
            dst_scr[...] = jnp.zeros_like(dst_scr)

        c = _chunk_consts()
        lane5 = lax.broadcasted_iota(jnp.int32, (CHUNK, SSD_GW), 1) // SSD_P
        row5 = lax.broadcasted_iota(jnp.int32, (CHUNK, SSD_GW), 0)
        sel_in = lax.broadcasted_iota(jnp.int32, (SSD_GW, LANES), 0) // SSD_P
        sel_out = lax.broadcasted_iota(jnp.int32, (SSD_GW, LANES), 1)

        def chunk(nn, carry):
            n = ncb - 1 - nn
            r = pl.ds(pl.multiple_of(n * CHUNK, CHUNK), CHUNK)
            rs = pl.ds(pl.multiple_of(n * SSD_N, SSD_N), SSD_N)
            acsv = acs_ref[0, r, :]
            ddt = jnp.zeros((CHUNK, LANES), F32)
            dacs = jnp.zeros((CHUNK, LANES), F32)
            for g in range(SSD_GROUPS):
                gl = slice(g * SSD_GW, (g + 1) * SSD_GW)
                x, dt, ax, dyv = x_ref[0, r, gl], dtx_ref[0, r, gl], ax_ref[0, r, gl], dy_ref[0, r, gl]
                bm = bc_ref[0, r, g * SSD_N:(g + 1) * SSD_N]
                cm = bc_ref[0, r, (SSD_GROUPS + g) * SSD_N:(SSD_GROUPS + g + 1) * SSD_N]
                st = sts_ref[0, rs, gl]
                dst = dst_scr[:, gl]
                rsel = (sel_in + (16 + g * SSD_HPG) == sel_out).astype(F32)
                xdt = x * dt
                cbm = _mm(cm, bm, NT)
                al = ax[CHUNK - 1:CHUNK, :]
                ex, el = jnp.exp(ax), jnp.exp(al)
                dec = jnp.exp(al - ax)
                xd = xdt * dec
                dye = dyv * ex
                dxd = _mm(bm, dst)
                dxdt = dec * dxd
                dcm = _mm(dye, st, NT)
                dbm = _mm(xd, dst, NT)
                z = dye * _mm(cm, st) - dxd * xd
                zl = _colsum(dst * st) * el + _colsum(dxd * xd)
                z = z + jnp.where(row5 == CHUNK - 1, zl, 0.0)
                dcb = jnp.zeros((CHUNK, CHUNK), F32)
                for hh in range(SSD_HPG):
                    head = g * SSD_HPG + hh
                    lm, gm = _ssd_head(acsv, ar_ref, n, head, cbm, c)
                    dym = jnp.where(lane5 == hh, dyv, 0.0)
                    dxdt = dxdt + _mm(gm, dym, TN)
                    dg = _mm(dym, xdt, NT)
                    dcb = dcb + dg * lm
                    pm = dg * gm
                    dacs = dacs + jnp.where(c["lane"] == head + 16, _rowsum(pm) - _mmsel(pm, c["ones"], TN), 0.0)
                dbc_ref[0, r, (SSD_GROUPS + g) * SSD_N:(SSD_GROUPS + g + 1) * SSD_N] = dcm + _mm(dcb, bm)
                dbc_ref[0, r, g * SSD_N:(g + 1) * SSD_N] = dbm + _mm(dcb, cm, TN)
                dacs = dacs + _mmsel(z, rsel)
                ddt = ddt + _mmsel(dxdt * x, rsel)
                dx_ref[0, r, gl] = dxdt * dt
                dst_scr[:, gl] = dst * el + _mm(cm, dye, TN)
            ddt_ref[0, r, :] = ddt
            dacs_ref[0, r, :] = dacs
            return carry

        lax.fori_loop(0, ncb, chunk, 0)

    return _pcall(
        body, name="ssd_chunk_bwd", grid=(bsz, nsb),
        in_specs=[sp["wide"](), sp["wide"](), sp["wide"](), sp["bc"](), sp["small"](), sp["ar"], sp["wide"](), sp["st"]],
        out_specs=[sp["wide"](), sp["bc"](), sp["small"](), sp["small"]()],
        out_shape=[jax.ShapeDtypeStruct((bsz, seq, SSD_HEADS * SSD_P), F32),
                   jax.ShapeDtypeStruct((bsz, seq, 2 * SSD_GROUPS * SSD_N), F32),
                   jax.ShapeDtypeStruct((bsz, seq, LANES), F32), jax.ShapeDtypeStruct((bsz, seq, LANES), F32)],
        scratch_shapes=[pltpu.VMEM((SSD_N, SSD_HEADS * SSD_P), F32)],
        compiler_params=_params(("parallel", "arbitrary")),
    )(xs, dtx, acsx, bc, acs, ar, dy, sts)


def _ssd_prep_bwd(proj, dxs_c, dxs_d, dbc, ddt, dacs, dsm_gdn, cw, cb, sp, seq, tm):
    t = proj.shape[0]
    d = D_MODEL

    def body(i, x_ref, xh_ref, bc_ref, bch_ref, sm_ref, dxc_ref, dxd_ref, dbc_ref, ddt_ref, dacs_ref, dsg_ref,
             cw_ref, cb_ref, sp_ref, dpre_ref, dsm_ref, dcw_ref, dcb_ref, dsp_ref):
        keep, _ = _seq_flags(i, seq, tm)
        parts = ((x_ref, xh_ref, 0, d, (dxc_ref[...] + dxd_ref[...],)),
                 (bc_ref, bch_ref, d, 512, (dbc_ref[...],)))
        for xr, hr, off, w, grads in parts:
            taps = _conv_taps(xr[...], hr[...] * keep, 4)
            y = _conv(taps, cw_ref[:, off:off + w]) + cb_ref[:, off:off + w]
            ds_ = _dsilu(y, _sigmoid(y))
            o = 0
            for gr in grads:
                wg = gr.shape[1]
                dpre = gr * ds_[:, o:o + wg]
                dpre_ref[:, off + o:off + o + wg] = dpre
                dcb_ref[:, off + o:off + o + wg] += _colsum(dpre)
                for k in range(4):
                    dcw_ref[k:k + 1, off + o:off + o + wg] += _colsum(dpre * taps[k][:, o:o + wg])
                o += wg
        sm = sm_ref[...]
        lane = lax.broadcasted_iota(jnp.int32, sm.shape, 1)
        valid = (lane >= 16) & (lane < 32)
        xb = sm + sp_ref[1:2, :]
        dt = jnp.where(valid, _softplus(xb), 0.0)
        a_neg = -jnp.exp(sp_ref[0:1, :])
        dadt_s = _mmx(_block_tri(tm, True), dacs_ref[...])
        dxb = jnp.where(valid, (ddt_ref[...] + dadt_s * a_neg) * _sigmoid(xb), 0.0)
        dsm_ref[...] = (dsg_ref[...] + dxb).astype(BF16)
        dsp_ref[1:2, :] += _colsum(dxb)
        dsp_ref[0:1, :] += jnp.where(valid[0:1, :], _colsum(dadt_s * dt) * a_neg, 0.0)

    ins = [("row", proj, d, 5), ("prev", proj, d, 5), ("row", proj, 512, 12), ("prev", proj, 512, 12),
           ("row", proj, LANES, SMALL_CB), ("row", dxs_c, d, 0), ("row", dxs_d, d, 0), ("row", dbc, 512, 0),
           ("row", ddt, LANES, 0), ("row", dacs, LANES, 0), ("row", dsm_gdn, LANES, 0),
           ("full", cw), ("full", cb), ("full", sp)]
    return _rowwise("ssd_prep_bwd", body, t, tm, ins, [(d + 512, F32), (LANES, BF16)],
                    accs=[(SUBLANES, d + 512), (1, d + 512), (SUBLANES, LANES)])


def _gdn_chunk_bwd(qn, kn, vv, gs, gr, do, sts, tis, bsz, seq, sb, riders):
    hb = GDN_HB
    nsb, ncb, sp = _gdn_specs(seq, sb, hb, True)
    grid = (bsz, GDN_HEADS // hb, nsb)
    any_spec, rider_shapes, rider_sems, wrap = _riding_exchange(riders, True, 8, 4, grid)

    def body(q_ref, k_ref, v_ref, gs_ref, gr_ref, do_ref, st_ref, ti_ref, dq_ref, dk_ref, dv_ref, dgb_ref, ds_scr):
        hg = pl.program_id(1)

        @pl.when(pl.program_id(2) == 0)
        def _():
            ds_scr[...] = jnp.zeros_like(ds_scr)

        c = _chunk_consts()

        def chunk(nn, carry):
            n = ncb - 1 - nn
            r = pl.ds(pl.multiple_of(n * CHUNK, CHUNK), CHUNK)
            rs = pl.ds(pl.multiple_of(n * GDN_DK, GDN_DK), GDN_DK)
            gsv = gs_ref[0, r, :]
            heads = list(range(hb))
            sls = [slice(ih * GDN_DK, (ih + 1) * GDN_DK) for ih in heads]
            q = [q_ref[0, r, sl] for sl in sls]
            k = [k_ref[0, r, sl] for sl in sls]
            v = [v_ref[0, r, sl] for sl in sls]
            do_ = [do_ref[0, r, sl] for sl in sls]
            s = [st_ref[0, ih, rs, :] for ih in heads]
            tinv = [ti_ref[0, ih, r, :] for ih in heads]
            dsn = [ds_scr[ih] for ih in heads]
            beta, dc, eg, egl, ekd = zip(*[
                _gdn_gates(gsv, gr_ref[0, n, pl.ds(ih, 1), :], ih, c) for ih in heads])
            mul = lambda a, b: a * b
            kb = _hmap(mul, k, beta)
            rhs_w = _hmap(mul, kb, eg)
            u = _hmap(lambda t_, a, b: _mm3(t_, a * b), tinv, v, beta)
            w = _hmap(_mm3, tinv, rhs_w)
            amat = _hmap(lambda a, b, d_: jnp.where(c["strict"], _mm(a, b, NT) * d_, 0.0), kb, k, dc)
            qk = _hmap(lambda a, b, d_: _mm(a, b, NT) * d_, q, k, dc)
            qd = _hmap(mul, q, eg)
            kd = _hmap(mul, k, ekd)
            v_new = _hmap(lambda a, b, s_: a - _mm(b, s_), u, w, s)
            dv_new = _hmap(lambda qk_, d_, kd_, dn: _mm(qk_, d_, TN) + _mm(kd_, dn), qk, do_, kd, dsn)
            dqk = _hmap(lambda d_, vn: _mm(d_, vn, NT), do_, v_new)
            dqd = _hmap(lambda d_, s_: _mm(d_, s_, NT), do_, s)
            ds_new = _hmap(lambda qd_, d_, dn, e, w_, dvn: _mm(qd_, d_, TN) + dn * e - _mm(w_, dvn, TN),
                           qd, do_, dsn, egl, w, dv_new)
            dkd = _hmap(lambda vn, dn: _mm(vn, dn, NT), v_new, dsn)
            dgl = _hmap(lambda s_, dn, e: _colsum(_rowsum(s_ * dn)) * e, s, dsn, egl)
            dw = _hmap(lambda dvn, s_: -_mm(dvn, s_, NT), dv_new, s)
            dru = _hmap(lambda t_, a: _mm3(t_, a, TN), tinv, dv_new)
            drw = _hmap(lambda t_, a: _mm3(t_, a, TN), tinv, dw)
            da = _hmap(lambda a, u_, b, w_: jnp.where(c["strict"], -(_mm(a, u_, NT) + _mm(b, w_, NT)), 0.0), dru, u, drw, w)
            m = _hmap(mul, da, dc)
            dkb = _hmap(lambda a, e, m_, k_: a * e + _mm(m_, k_), drw, eg, m, k)
            mq = _hmap(mul, dqk, dc)
            dq = _hmap(lambda mq_, k_, a, e: _mm(mq_, k_) + a * e, mq, k, dqd, eg)
            dk = _hmap(lambda m_, kb_, mq_, q_, a, e, b, be: _mm(m_, kb_, TN) + _mm(mq_, q_, TN) + a * e + b * be,
                       m, kb, mq, q, dkd, ekd, dkb, beta)
            dbeta = _hmap(lambda a, v_, b, k_: _rowsum(a * v_) + _rowsum(b * k_), dru, v, dkb, k)
            pq = _hmap(lambda a, am, b, qk_: a * am + b * qk_, da, amat, dqk, qk)
            ekk = _hmap(lambda a, b: _rowsum(a * b), dkd, kd)
            dgc = _hmap(lambda pq_, a, rw, b, qd_, e, gl_: (
                _rowsum(pq_) - _mmsel(pq_, c["ones"], TN) + (_rowsum(a * rw) + _rowsum(b * qd_) - e)
                + jnp.where(c["row1"] == CHUNK - 1, _colsum(e) + gl_, 0.0)), pq, drw, rhs_w, dqd, qd, ekk, dgl)
            for ih in heads:
                ds_scr[ih] = ds_new[ih]
                dv_ref[0, r, sls[ih]] = dru[ih] * beta[ih]
                dq_ref[0, r, sls[ih]] = dq[ih]
                dk_ref[0, r, sls[ih]] = dk[ih]
                dgb_ref[0, r, sls[ih]] = jnp.where(c["lane"] == 0, dbeta[ih], jnp.where(c["lane"] == 1, dgc[ih], 0.0))
            return carry

        lax.fori_loop(0, ncb, chunk, 0)

    res = _pcall(
        wrap(body), name="gdn_chunk_bwd", grid=grid,
        in_specs=[sp["wide"](), sp["wide"](), sp["wide"](), sp["gs"], sp["gr"], sp["wide"](), sp["st"], sp["ti"]] + any_spec,
        out_specs=[sp["wide"](), sp["wide"](), sp["wide"](), sp["wide"]()] + any_spec,
        out_shape=[jax.ShapeDtypeStruct((bsz, seq, D_MODEL), F32)] * 4 + rider_shapes,
        scratch_shapes=[pltpu.VMEM((hb, GDN_DK, GDN_DK), F32)] + rider_sems,
        compiler_params=_params(("arbitrary", "arbitrary", "arbitrary")),
    )(qn, kn, vv, gs, gr, do, sts, tis, *riders)
    return res[:4], res[4:]


def _gdn_prep_bwd(proj, dqn, dkn, dvv, dgb, cw, gp, seq, tm):
    t = proj.shape[0]
    d = D_MODEL

    def body(i, q_ref, qh_ref, k_ref, kh_ref, v_ref, vh_ref, sm_ref, dq_ref, dk_ref, dv_ref, dgb_ref, cw_ref, gp_ref,
             dpre_ref, dsm_ref, dcw_ref, dgp_ref):
        keep, _ = _seq_flags(i, seq, tm)
        for x_ref, h_ref, g_ref, off, scale in ((q_ref, qh_ref, dq_ref, 0, GDN_DK ** -0.5),
                                               (k_ref, kh_ref, dk_ref, d, 1.0), (v_ref, vh_ref, dv_ref, 2 * d, None)):
            taps = _conv_taps(x_ref[...], h_ref[...] * keep, 4)
            y = _conv(taps, cw_ref[:, off:off + d])
            sy = _sigmoid(y)
            ds_ = _dsilu(y, sy)
            if scale is None:
                dpre = g_ref[...] * ds_
                dpre_ref[:, off:off + d] = dpre
                for k in range(4):
                    dcw_ref[k:k + 1, off:off + d] += _colsum(dpre * taps[k])
            else:
                a = y * sy
                for hh in range(GDN_HEADS):
                    sl = slice(hh * GDN_DK, (hh + 1) * GDN_DK)
                    s = a[:, sl]
                    n = lax.rsqrt(_rowsum(s * s) + EPS)
                    ah = s * n
                    gq = g_ref[:, sl]
                    dpre = (scale * n) * (gq - ah * _rowsum(gq * ah)) * ds_[:, sl]
                    dpre_ref[:, off + hh * GDN_DK:off + (hh + 1) * GDN_DK] = dpre
                    for k in range(4):
                        dcw_ref[k:k + 1, off + hh * GDN_DK:off + (hh + 1) * GDN_DK] += _colsum(dpre * taps[k][:, sl])
        sm = sm_ref[...]
        lane = lax.broadcasted_iota(jnp.int32, sm.shape, 1)
        si = lax.broadcasted_iota(jnp.int32, (d, LANES), 0)
        so = lax.broadcasted_iota(jnp.int32, (d, LANES), 1)
        sel = (((si % GDN_DK == 0) & (so == si // GDN_DK)) | ((si % GDN_DK == 1) & (so == si // GDN_DK + 8))).astype(F32)
        dsel = _mmx(dgb_ref[...], sel)
        is_g = (lane >= 8) & (lane < 16)
        dsel = jnp.where(is_g, _mmx(_block_tri(tm, True), dsel), dsel)
        beta = _sigmoid(sm)
        xb = sm + gp_ref[1:2, :]
        a_neg = -jnp.exp(gp_ref[0:1, :])
        sp = _softplus(xb)
        dxb = jnp.where(is_g, dsel * a_neg * _sigmoid(xb), 0.0)
        dsm_ref[...] = jnp.where(lane < 8, dsel * beta * (1.0 - beta), dxb)
        dgp_ref[1:2, :] += _colsum(dxb)
        dgp_ref[0:1, :] += _colsum(jnp.where(is_g, dsel * a_neg * sp, 0.0))

    ins = []
    for cb in range(3):
        ins += [("row", proj, d, cb), ("prev", proj, d, cb)]
    ins += [("row", proj, LANES, SMALL_CB), ("row", dqn, d, 0), ("row", dkn, d, 0), ("row", dvv, d, 0), ("row", dgb, d, 0),
            ("full", cw), ("full", gp)]
    return _rowwise("gdn_prep_bwd", body, t, tm, ins, [(3 * d, F32), (LANES, F32)],
                    accs=[(SUBLANES, 3 * d), (SUBLANES, LANES)])


def _first_bwd(x, dh1, dx1, w, tm):
    t, d = x.shape

    def body(i, x_ref, dh_ref, dx1_ref, w_ref, dx_ref, dw_ref):
        xh, r = _rms(x_ref[...], d)
        dh = dh_ref[...]
        dw_ref[...] += _colsum(dh * xh)
        dx_ref[...] = dx1_ref[...] + _rms_bwd(xh, r, dh * w_ref[...], d)

    return _rowwise("first_bwd", body, t, tm, [("row", x, d, 0), ("row", dh1, d, 0), ("row", dx1, d, 0), ("full", w)],
                    [(d, F32)], accs=[(1, d)])


def _gather_two_level(name, arrays):
    n = len(arrays)
    n_sem = 7

    def body(*refs):
        ins, outs = refs[:n], refs[n:2 * n]
        send_sems, recv_sems, loc_sems = refs[2 * n:]
        x, y, c = lax.axis_index("x"), lax.axis_index("y"), lax.axis_index("c")
        slot = lambda px, py, pc: 4 * px + 2 * py + pc
        sibling = (x, y, 1 - c)
        chips = [(1 - x, y), (x, 1 - y), (1 - x, 1 - y)]

        def copy(t, k, src, block, to):
            return pltpu.make_async_remote_copy(
                src_ref=src, dst_ref=outs[t].at[block], send_sem=send_sems.at[t, k], recv_sem=recv_sems.at[t, k],
                device_id=to, device_id_type=pl.DeviceIdType.MESH)

        own, first, passed = [], [], []
        for t in range(n):
            own.append(pltpu.make_async_copy(ins[t], outs[t].at[slot(x, y, c)], loc_sems.at[t]))
            first.append(copy(t, 0, ins[t], slot(x, y, c), sibling))
            first += [copy(t, 1 + j, ins[t], slot(x, y, c), (px, py, c)) for j, (px, py) in enumerate(chips)]
        for cp in own + first:
            cp.start()
        for t in range(n):
            for j, (px, py) in enumerate(chips):
                copy(t, 1 + j, ins[t], slot(px, py, c), (px, py, c)).wait_recv()
                fwd = copy(t, 4 + j, outs[t].at[slot(px, py, c)], slot(px, py, c), sibling)
                fwd.start()
                passed.append(fwd)
        for t in range(n):
            copy(t, 0, ins[t], slot(x, y, 1 - c), sibling).wait_recv()
            for j, (px, py) in enumerate(chips):
                copy(t, 4 + j, ins[t], slot(px, py, 1 - c), sibling).wait_recv()
        for cp in first + passed:
            cp.wait_send()
        for cp in own:
            cp.wait()

    return _pcall(
        body, name=name,
        in_specs=[pl.BlockSpec(memory_space=pl.ANY)] * n,
        out_specs=[pl.BlockSpec(memory_space=pl.ANY)] * n,
        out_shape=_exchange_out_shapes(arrays, False),
        scratch_shapes=[pltpu.SemaphoreType.DMA((n, n_sem)), pltpu.SemaphoreType.DMA((n, n_sem)), pltpu.SemaphoreType.DMA((n,))],
    )(*arrays)


def _exchange_out_shapes(arrays, scatter):
    return [jax.ShapeDtypeStruct(a.shape if scatter else (N_DEV,) + a.shape, a.dtype) for a in arrays]


def _exchange_sems(n):
    return [pltpu.SemaphoreType.DMA((n, N_DEV - 1)), pltpu.SemaphoreType.DMA((n, N_DEV - 1)), pltpu.SemaphoreType.DMA((n,))]


def _exchange_phase(ins, outs, sems, scatter, start):
    send_sems, recv_sems, loc_sems = sems
    x, y, c = lax.axis_index("x"), lax.axis_index("y"), lax.axis_index("c")
    me = 4 * x + 2 * y + c
    for t in range(len(ins)):
        loc = pltpu.make_async_copy(ins[t].at[me] if scatter else ins[t], outs[t].at[me], loc_sems.at[t])
        if start:
            loc.start()
        else:
            loc.wait()
        for k in range(N_DEV - 1):
            bx, by, bc = ((k + 1) >> 2) & 1, ((k + 1) >> 1) & 1, (k + 1) & 1
            px = 1 - x if bx else x
            py = 1 - y if by else y
            pc = 1 - c if bc else c
            peer = 4 * px + 2 * py + pc
            src = ins[t].at[peer] if scatter else ins[t]
            copy = lambda dst: pltpu.make_async_remote_copy(
                src_ref=src, dst_ref=dst, send_sem=send_sems.at[t, k], recv_sem=recv_sems.at[t, k],
                device_id=(px, py, pc), device_id_type=pl.DeviceIdType.MESH)
            if start:
                copy(outs[t].at[me]).start()
            else:
                copy(outs[t].at[me]).wait_send()
                copy(outs[t].at[peer]).wait_recv()


def _adam_math(w, g, m, v):
    m = ADAM_B1 * m + (1.0 - ADAM_B1) * g
    v = ADAM_B2 * v + (1.0 - ADAM_B2) * (g * g)
    m_hat = m / (1.0 - ADAM_B1 ** ADAM_STEP)
    v_hat = v / (1.0 - ADAM_B2 ** ADAM_STEP)
    delta = -ADAM_LR * (m_hat / (jnp.sqrt(v_hat) + ADAM_EPS) + ADAM_WD * w)
    return delta, m, v


def _adam_big(name, parts, w, m, v, tm):
    r, c = w.shape
    tm = tm if r % tm == 0 else r

    def body(p_ref, w_ref, m_ref, v_ref, g_ref, d_ref, nm_ref, nv_ref):
        g = p_ref[0].astype(F32)
        for s in range(1, N_DEV):
            g = g + p_ref[s].astype(F32)
        g_ref[...] = g
        d_ref[...], nm_ref[...], nv_ref[...] = _adam_math(w_ref[...], g, m_ref[...], v_ref[...])

    blk = lambda: pl.BlockSpec((tm, c), lambda i: (i, 0))
    return _pcall(
        body, name=name, grid=(r // tm,),
        in_specs=[pl.BlockSpec((N_DEV, tm, c), lambda i: (0, i, 0)), blk(), blk(), blk()],
        out_specs=[blk(), blk(), blk(), blk()],
        out_shape=[jax.ShapeDtypeStruct((r, c), F32)] * 4,
        compiler_params=_params(("parallel",)),
    )(parts, w, m, v)


SMALL_ROWS = 56
ROW_DD, ROW_LOSS = 5, 6


def _small_sum(gathered):
    def body(g_ref, o_ref, x_ref):
        s = g_ref[0]
        for dev in range(1, N_DEV):
            s = s + g_ref[dev]
        o_ref[...] = s
        ri = lax.broadcasted_iota(jnp.int32, (D_MODEL, LANES), 0)
        ro = lax.broadcasted_iota(jnp.int32, (D_MODEL, LANES), 1)
        heads = _mmx(jnp.broadcast_to(s[ROW_DD:ROW_DD + 1, :], (SUBLANES, D_MODEL)), (ri // SSD_P == ro).astype(F32))
        loss = _rowsum(jnp.broadcast_to(s[ROW_LOSS:ROW_LOSS + 1, :], (SUBLANES, D_MODEL)))
        row = lax.broadcasted_iota(jnp.int32, (SUBLANES, LANES), 0)
        x_ref[...] = jnp.where(row == 0, heads, jnp.broadcast_to(loss, (SUBLANES, LANES)))

    return _pcall(
        body, name="small_sum",
        out_shape=[jax.ShapeDtypeStruct((SMALL_ROWS, D_MODEL), F32), jax.ShapeDtypeStruct((SUBLANES, LANES), F32)],
        compiler_params=_params(None),
    )(gathered)


def _adam_small(g, w, m, v):
    def body(g_ref, w_ref, m_ref, v_ref, d_ref, nm_ref, nv_ref):
        d_ref[...], nm_ref[...], nv_ref[...] = _adam_math(w_ref[...], g_ref[...], m_ref[...], v_ref[...])

    return _pcall(body, name="adam_small", out_shape=[jax.ShapeDtypeStruct(g.shape, F32)] * 3,
                  compiler_params=_params(None))(g, w, m, v)


def _pack(pieces, rows):
    flat = jnp.concatenate([p.reshape(-1).astype(F32) for p in pieces])
    return jnp.pad(flat, (0, rows * D_MODEL - flat.shape[0])).reshape(rows, D_MODEL)


def _unpack(packed, shapes):
    flat = packed.reshape(-1)
    out, off = [], 0
    for shp in shapes:
        size = 1
        for s in shp:
            size *= s
        out.append(flat[off:off + size].reshape(shp))
        off += size
    return out


def _permute_in(w):
    pad = jnp.zeros((w.shape[0], PROJ_W - D_IN), w.dtype)
    return jnp.concatenate([w[:, 0:4096], w[:, 4112:6672], w[:, 4096:4112], w[:, 6672:6688], pad], axis=1)


def _unpermute_in(g):
    return jnp.concatenate([g[:, 0:4096], g[:, 6656:6672], g[:, 4096:6656], g[:, 6672:6688]], axis=1)


def _lane_row(vec, start):
    return jnp.zeros((LANES,), F32).at[start:start + vec.shape[0]].set(vec)


def _cols_from_shards(g):
    return jnp.transpose(g, (1, 0, 2)).reshape(g.shape[1], N_DEV * g.shape[2])


def _cols_to_shards(a):
    return jnp.transpose(a.astype(BF16).reshape(a.shape[0], N_DEV, a.shape[1] // N_DEV), (1, 0, 2))


def _rows_to_shards(a):
    return a.astype(BF16).reshape(N_DEV, a.shape[0] // N_DEV, a.shape[1])


def _local_step(x, tgt, wp_in, rest, p, rest_is_sharded):
    bsz, seq, d = x.shape
    t = bsz * seq
    x2 = x.reshape(t, d)
    tgt2 = tgt.reshape(t, d)
    tm = min(256, seq)
    tm_wide = min(128, seq)
    sb = min(512, seq)

    gp = jnp.zeros((SUBLANES, LANES), F32).at[0].set(_lane_row(p["gdn_a_log"], 8)).at[1].set(_lane_row(p["gdn_dt_bias"], 8))
    sp = jnp.zeros((SUBLANES, LANES), F32).at[0].set(_lane_row(p["ssd_a_log"], 16)).at[1].set(_lane_row(p["ssd_dt_bias"], 16))
    dvec = jnp.repeat(p["ssd_d"], SSD_P).reshape(1, d)
    row = lambda v: v.reshape(1, -1)
    pre_mix, post_mix, pre_ffn, post_ffn = (row(p[k]) for k in ("pre_mix_norm", "post_mix_norm", "pre_ffn_norm", "post_ffn_norm"))
    gnw, snw = row(p["gdn_norm_w"]), row(p["ssd_norm_w"])
    gcw, scw, scb, fcw, fcb = p["gdn_conv_w"], p["ssd_conv_w"], row(p["ssd_conv_b"]), p["ffn_conv_w"], row(p["ffn_conv_b"])

    h1 = _norm_cast("norm_in", x2, pre_mix, tm)
    proj = _matmul("mm_proj", h1, wp_in, "nn", F32)
    b3 = lambda a: a.reshape(bsz, seq, a.shape[-1])
    b2 = lambda a: a.reshape(t, a.shape[-1])
    rows_of = lambda a, lo, n: jnp.transpose(a[:, lo:lo + n].reshape(bsz, seq // CHUNK, CHUNK, n), (0, 3, 1, 2))
    qn, kn, vv, gs = (b3(a) for a in _gdn_prep(proj, gcw, gp, seq, tm))
    gr = jnp.transpose(b2(gs)[:, 8:8 + GDN_HEADS].reshape(bsz, seq // CHUNK, CHUNK, GDN_HEADS), (0, 1, 3, 2))
    (o_gdn, gdn_st, gdn_ti), gathered = _gdn_chunk_fwd(qn, kn, vv, gs, gr, bsz, seq, sb, list(rest) if rest_is_sharded else [])
    if rest_is_sharded:
        w_out, w_up, w_down = gathered[0].reshape(-1, d), _cols_from_shards(gathered[1]), gathered[2].reshape(-1, d)
    else:
        w_out, w_up, w_down = rest
    o_gdn = b2(o_gdn)
    xs, bc, dtx, acsx, acs = _ssd_prep(proj, scw, scb, sp, seq, tm)
    ar = rows_of(acs, 16, SSD_HEADS)
    y_ssd, ssd_st = _ssd_chunk_fwd(b3(xs), b3(bc), b3(dtx), b3(acsx), b3(acs), ar, bsz, seq, sb)
    y_ssd = b2(y_ssd)
    mixin = _gate_norm(o_gdn, y_ssd, xs, proj, gnw, snw, dvec, tm)
    mix = _matmul("mm_out", mixin, w_out, "nn", F32)
    x1, h2 = _mid(x2, mix, post_mix, pre_ffn, tm)
    u_pre = _matmul("mm_up", h2, w_up, "nn", F32)
    act = _ffn_act(u_pre, fcw, fcb, seq, tm_wide)
    f = _matmul("mm_down", act, w_down, "nn", F32, tk=1408)
    dy, df, loss_lanes, d_post_ffn = _final(x1, f, tgt2, post_ffn, tm)

    g_down = _matmul("mm_dw_down", act, df, "tn", F32, tm=1408)
    dact = _matmul("mm_dact", df, w_down, "nt", F32, tn=1408)
    du_pre, d_fcw, d_fcb = _ffn_bwd(u_pre, dact, fcw, fcb, seq, tm_wide)
    g_up = _matmul("mm_dw_up", h2, du_pre, "tn", F32)
    dh2 = _matmul("mm_dh2", du_pre, w_up, "nt", F32)
    dx1, dmix, d_post_mix, d_pre_ffn = _mid_bwd(x1, mix, dy, dh2, post_mix, pre_ffn, tm)
    g_out = _matmul("mm_dw_out", mixin, dmix, "tn", F32)
    dmixin = _matmul("mm_dmixin", dmix, w_out, "nt", F32)
    do_gdn, dza, dy_ssd, dxs_d, dzs, d_gnw, d_snw, d_dd = _gate_norm_bwd(o_gdn, y_ssd, xs, proj, dmixin, gnw, snw, dvec, tm)
    dxs_c, dbc, ddt, dacs = (b2(a) for a in _ssd_chunk_bwd(
        b3(xs), b3(bc), b3(dtx), b3(acsx), b3(acs), ar, b3(dy_ssd), ssd_st, bsz, seq, sb))
    riders = [_rows_to_shards(g_out), _cols_to_shards(g_up), _rows_to_shards(g_down)] if rest_is_sharded else []
    dgdn, received = _gdn_chunk_bwd(qn, kn, vv, gs, gr, b3(do_gdn), gdn_st, gdn_ti, bsz, seq, min(256, seq), riders)
    if rest_is_sharded:
        g_out, g_up, g_down = received
    dqn, dkn, dvv, dgb = (b2(a) for a in dgdn)
    dpre_gdn, dsm_gdn, d_gcw, d_gp = _gdn_prep_bwd(proj, dqn, dkn, dvv, dgb, gcw, gp, seq, tm)
    dpre_ssd, dsm, d_scw, d_scb, d_sp = _ssd_prep_bwd(proj, dxs_c, dxs_d, dbc, ddt, dacs, dsm_gdn, scw, scb, sp, seq, tm)
    dproj = _assemble_dproj(dpre_gdn, dza, dzs, dpre_ssd, dsm, gcw, scw, seq, tm)
    g_in = _matmul("mm_dw_in", h1, dproj, "tn", F32)
    if rest_is_sharded:
        dh1, (g_in,) = _matmul("mm_dh1", dproj, wp_in, "nt", F32, scatter_riders=[_cols_to_shards(_unpermute_in(g_in))])
    else:
        dh1 = _matmul("mm_dh1", dproj, wp_in, "nt", F32)
    dx, d_pre_mix = _first_bwd(x2, dh1, dx1, pre_mix, tm)

    small = dict(pre_mix_norm=d_pre_mix, ssd_norm_w=d_snw, post_mix_norm=d_post_mix, pre_ffn_norm=d_pre_ffn,
                 post_ffn_norm=d_post_ffn, dd_lanes=d_dd, loss_lanes=loss_lanes, gdn_gates=d_gp, ssd_gates=d_sp,
                 gdn_norm_w=d_gnw, gdn_conv_w=d_gcw[0:4], ssd_conv_w=d_scw[0:4], ssd_conv_b=d_scb,
                 ffn_conv_w=d_fcw[0:3], ffn_conv_b=d_fcb)
    return dx.reshape(bsz, seq, d), g_in, g_out, g_up, g_down, small


def kernel(x, pre_mix_norm, w_in, gdn_conv_w, gdn_a_log, gdn_dt_bias, gdn_norm_w, ssd_conv_w, ssd_conv_b, ssd_a_log, ssd_dt_bias, ssd_d, ssd_norm_w, w_out, post_mix_norm, pre_ffn_norm, w_up, ffn_conv_w, ffn_conv_b, w_down, post_ffn_norm, loss_target, m_pre_mix_norm, m_w_in, m_gdn_conv_w, m_gdn_a_log, m_gdn_dt_bias, m_gdn_norm_w, m_ssd_conv_w, m_ssd_conv_b, m_ssd_a_log, m_ssd_dt_bias, m_ssd_d, m_ssd_norm_w, m_w_out, m_post_mix_norm, m_pre_ffn_norm, m_w_up, m_ffn_conv_w, m_ffn_conv_b, m_w_down, m_post_ffn_norm, v_pre_mix_norm, v_w_in, v_gdn_conv_w, v_gdn_a_log, v_gdn_dt_bias, v_gdn_norm_w, v_ssd_conv_w, v_ssd_conv_b, v_ssd_a_log, v_ssd_dt_bias, v_ssd_d, v_ssd_norm_w, v_w_out, v_post_mix_norm, v_pre_ffn_norm, v_w_up, v_ffn_conv_w, v_ffn_conv_b, v_w_down, v_post_ffn_norm):
    names = ["pre_mix_norm", "w_in", "gdn_conv_w", "gdn_a_log", "gdn_dt_bias", "gdn_norm_w", "ssd_conv_w", "ssd_conv_b",
             "ssd_a_log", "ssd_dt_bias", "ssd_d", "ssd_norm_w", "w_out", "post_mix_norm", "pre_ffn_norm", "w_up",
             "ffn_conv_w", "ffn_conv_b", "w_down", "post_ffn_norm"]
    w_args = [pre_mix_norm, w_in, gdn_conv_w, gdn_a_log, gdn_dt_bias, gdn_norm_w, ssd_conv_w, ssd_conv_b, ssd_a_log, ssd_dt_bias, ssd_d, ssd_norm_w, w_out, post_mix_norm, pre_ffn_norm, w_up, ffn_conv_w, ffn_conv_b, w_down, post_ffn_norm]
    m_args = [m_pre_mix_norm, m_w_in, m_gdn_conv_w, m_gdn_a_log, m_gdn_dt_bias, m_gdn_norm_w, m_ssd_conv_w, m_ssd_conv_b, m_ssd_a_log, m_ssd_dt_bias, m_ssd_d, m_ssd_norm_w, m_w_out, m_post_mix_norm, m_pre_ffn_norm, m_w_up, m_ffn_conv_w, m_ffn_conv_b, m_w_down, m_post_ffn_norm]
    v_args = [v_pre_mix_norm, v_w_in, v_gdn_conv_w, v_gdn_a_log, v_gdn_dt_bias, v_gdn_norm_w, v_ssd_conv_w, v_ssd_conv_b, v_ssd_a_log, v_ssd_dt_bias, v_ssd_d, v_ssd_norm_w, v_w_out, v_post_mix_norm, v_pre_ffn_norm, v_w_up, v_ffn_conv_w, v_ffn_conv_b, v_w_down, v_post_ffn_norm]
    w = {k: a[0] for k, a in zip(names, w_args)}
    m = {k: a[0] for k, a in zip(names, m_args)}
    v = {k: a[0] for k, a in zip(names, v_args)}
    idx = 4 * lax.axis_index("x") + 2 * lax.axis_index("y") + lax.axis_index("c")
    big = ("w_in", "w_out", "w_up", "w_down")
    conv = ("gdn_conv_w", "ssd_conv_w", "ffn_conv_w")

    conv_local = jnp.concatenate([jnp.pad(w[k], ((0, 4 - w[k].shape[0]), (0, 0))) for k in conv], axis=1)
    g_in, g_conv = _gather_two_level("gather_weights", [w["w_in"].astype(BF16), conv_local])
    wp_in = _permute_in(_cols_from_shards(g_in))
    p = {k: w[k] for k in names if k not in big and k not in conv}
    off = 0
    for k in conv:
        cw = w[k].shape[1]
        p[k] = jnp.transpose(g_conv[:, :w[k].shape[0], off:off + cw], (1, 0, 2)).reshape(w[k].shape[0], N_DEV * cw)
        off += cw

    rest = tuple(w[k].astype(BF16) for k in ("w_out", "w_up", "w_down"))
    dx, p_in, p_out, p_up, p_down, small = _local_step(x, loss_target, wp_in, rest, p, True)

    gate_row = jnp.concatenate([small["gdn_gates"][0], small["gdn_gates"][1], small["ssd_gates"][0], small["ssd_gates"][1],
                                small["gdn_norm_w"][0], jnp.zeros((D_MODEL - 5 * LANES,), F32)]).reshape(1, D_MODEL)
    pack = _pack([small["pre_mix_norm"], small["ssd_norm_w"], small["post_mix_norm"], small["pre_ffn_norm"],
                  small["post_ffn_norm"], small["dd_lanes"], small["loss_lanes"], gate_row,
                  small["gdn_conv_w"], small["ssd_conv_w"], jnp.pad(small["ssd_conv_b"], ((0, 0), (0, 512))),
                  jnp.pad(small["ffn_conv_w"].reshape(-1), (0, 17 * D_MODEL - 3 * 2 * D_FF)),
                  jnp.pad(small["ffn_conv_b"], ((0, 0), (0, 512)))], SMALL_ROWS)
    (pack_all,) = _gather_two_level("gather_small", [pack])
    ssum, extra = _small_sum(pack_all)

    grads, deltas, new_m, new_v = {}, {}, {}, {}
    for k, parts in (("w_in", p_in), ("w_out", p_out), ("w_up", p_up), ("w_down", p_down)):
        grads[k], deltas[k], new_m[k], new_v[k] = _adam_big("adam_" + k, parts, w[k], m[k], v[k], 256)

    flat = ssum.reshape(-1)
    gate = ssum[7]
    sg = dict(pre_mix_norm=ssum[0], ssd_norm_w=ssum[1], post_mix_norm=ssum[2], pre_ffn_norm=ssum[3], post_ffn_norm=ssum[4],
              gdn_a_log=gate[8:16], gdn_dt_bias=gate[LANES + 8:LANES + 16], ssd_a_log=gate[2 * LANES + 16:2 * LANES + 32],
              ssd_dt_bias=gate[3 * LANES + 16:3 * LANES + 32], gdn_norm_w=gate[4 * LANES:5 * LANES], ssd_d=extra[0, 0:SSD_HEADS])
    o = 8 * D_MODEL
    full_gcw = flat[o:o + 4 * 3072].reshape(4, 3072)
    o += 12 * D_MODEL
    full_scw = flat[o:o + 4 * 1536].reshape(4, 1536)
    o += 6 * D_MODEL
    sg["ssd_conv_b"] = flat[o:o + 1536]
    o += 2 * D_MODEL
    full_fcw = flat[o:o + 3 * 2 * D_FF].reshape(3, 2 * D_FF)
    o += 17 * D_MODEL
    sg["ffn_conv_b"] = flat[o:o + 2 * D_FF]
    for k, full in (("gdn_conv_w", full_gcw), ("ssd_conv_w", full_scw), ("ffn_conv_w", full_fcw)):
        cw = w[k].shape[1]
        sg[k] = lax.dynamic_slice_in_dim(full, idx * cw, cw, axis=1)
    small_names = [k for k in names if k not in big]
    rows = 24
    gpk = _pack([sg[k] for k in small_names], rows)
    dpk, mpk, vpk = _adam_small(gpk, _pack([w[k] for k in small_names], rows), _pack([m[k] for k in small_names], rows),
                                _pack([v[k] for k in small_names], rows))
    shapes = [w[k].shape for k in small_names]
    for k, g_, d_, m_, v_ in zip(small_names, _unpack(gpk, shapes), _unpack(dpk, shapes), _unpack(mpk, shapes), _unpack(vpk, shapes)):
        grads[k], deltas[k], new_m[k], new_v[k] = g_, d_, m_, v_

    loss = extra[1, 0]
    lead = lambda a: a[None]
    return (loss, dx, *[lead(grads[k]) for k in names], *[lead(deltas[k]) for k in names],
            *[lead(new_m[k]) for k in names], *[lead(new_v[k]) for k in names])
```

```python
import functools

import jax
import jax.numpy as jnp
from jax import lax
from jax.experimental import pallas as pl
from jax.experimental.pallas import tpu as pltpu

F32 = jnp.float32
BF16 = jnp.bfloat16
MXU_DTYPE = jnp.bfloat16
HIGHEST = lax.Precision.HIGHEST
VMEM_LIMIT_V7X = 48 * 1024 * 1024
SUBLANES = 8
LANES = 128

D_MODEL = 1024
GDN_HEADS = 8
GDN_DK = 128
SSD_HEADS = 16
SSD_P = 64
SSD_GROUPS = 2
SSD_HPG = 8
SSD_N = 128
CHUNK = 64
D_FF = 2816
EPS = 1e-6
N_DEV = 8
PROJ_W = 7168
SMALL_CB = 52
D_IN = 6688

ADAM_LR = 0.001
ADAM_B1 = 0.9
ADAM_B2 = 0.999
ADAM_EPS = 1e-08
ADAM_WD = 0.01
ADAM_STEP = 10

NN = (((1,), (0,)), ((), ()))
NT = (((1,), (1,)), ((), ()))
TN = (((0,), (0,)), ((), ()))


def _pcall(body, **kw):
    return pl.pallas_call(body, **kw)


def _mm(a, b, dims=NN):
    return lax.dot_general(a.astype(MXU_DTYPE), b.astype(MXU_DTYPE), dims, preferred_element_type=F32)


def _mmx(a, b, dims=NN):
    return lax.dot_general(a, b, dims, precision=HIGHEST, preferred_element_type=F32)


def _split(a):
    hi = a.astype(MXU_DTYPE)
    return hi, (a - hi.astype(F32)).astype(MXU_DTYPE)


def _mm3(a, b, dims=NN):
    (ah, al), (bh, bl) = _split(a), _split(b)
    dot = lambda p, q: lax.dot_general(p, q, dims, preferred_element_type=F32)
    return dot(ah, bh) + (dot(ah, bl) + dot(al, bh))


def _mmsel(a, sel, dims=NN, terms=2):
    s = sel.astype(MXU_DTYPE)
    out = None
    for _ in range(terms):
        part = a.astype(MXU_DTYPE)
        a = a - part.astype(F32)
        prod = lax.dot_general(part, s, dims, preferred_element_type=F32)
        out = prod if out is None else out + prod
    return out


def _sigmoid(x):
    return 0.5 * jnp.tanh(0.5 * x) + 0.5


def _softplus(x):
    return jnp.maximum(x, 0.0) + jnp.log(1.0 + jnp.exp(-jnp.abs(x)))


def _dsilu(x, s):
    return s * (1.0 + x * (1.0 - s))


def _rowsum(x):
    return jnp.sum(x, axis=1, keepdims=True)


def _colsum(x):
    return jnp.sum(x, axis=0, keepdims=True)


def _pick(dim, pref):
    if dim <= pref:
        return dim
    best = None
    t = LANES
    while t <= pref:
        if dim % t == 0:
            best = t
        t += LANES
    return dim if best is None else best


def _params(sem):
    return pltpu.CompilerParams(dimension_semantics=sem, vmem_limit_bytes=VMEM_LIMIT_V7X)


def _matmul(name, a, b, mode, out_dtype, tm=1024, tn=1024, tk=1024, scatter_riders=()):
    if mode == "nn":
        (m, k), (_, n) = a.shape, b.shape
    elif mode == "nt":
        (m, k), (n, _) = a.shape, b.shape
    else:
        (k, m), (_, n) = a.shape, b.shape
    tm, tn, tk = _pick(m, tm), _pick(n, tn), _pick(k, tk)
    nk = k // tk
    if mode == "tn":
        a_spec = pl.BlockSpec((tk, tm), lambda i, j, kk: (kk, i))
    else:
        a_spec = pl.BlockSpec((tm, tk), lambda i, j, kk: (i, kk))
    if mode == "nt":
        b_spec = pl.BlockSpec((tn, tk), lambda i, j, kk: (j, kk))
    else:
        b_spec = pl.BlockSpec((tk, tn), lambda i, j, kk: (kk, j))
    dims = {"nn": NN, "nt": NT, "tn": TN}[mode]

    def body(a_ref, b_ref, o_ref, *acc):
        if nk == 1:
            o_ref[...] = _mm(a_ref[...], b_ref[...], dims).astype(out_dtype)
            return
        kk = pl.program_id(2)

        @pl.when(kk == 0)
        def _():
            acc[0][...] = jnp.zeros_like(acc[0])

        acc[0][...] += _mm(a_ref[...], b_ref[...], dims)

        @pl.when(kk == nk - 1)
        def _():
            o_ref[...] = acc[0][...].astype(out_dtype)

    grid = (m // tm, n // tn, nk)
    riders = list(scatter_riders)
    any_spec, rider_shapes, rider_sems, wrap = _riding_exchange(riders, True, 2, 1, grid)
    res = _pcall(
        wrap(body), name=name, grid=grid,
        in_specs=[a_spec, b_spec] + any_spec,
        out_specs=[pl.BlockSpec((tm, tn), lambda i, j, kk: (i, j))] + any_spec,
        out_shape=[jax.ShapeDtypeStruct((m, n), out_dtype)] + rider_shapes,
        scratch_shapes=([pltpu.VMEM((tm, tn), F32)] if nk > 1 else []) + rider_sems,
        compiler_params=_params(("arbitrary", "arbitrary", "arbitrary") if riders else ("parallel", "parallel", "arbitrary")),
    )(a, b, *riders)
    return (res[0], res[1:]) if riders else res[0]


def _rowwise(name, body, n_rows, tm, ins, outs, accs=()):
    arrays, in_specs = [], []
    last8 = n_rows // SUBLANES - 1
    per = tm // SUBLANES
    for spec in ins:
        kind, arr = spec[0], spec[1]
        if kind == "full":
            in_specs.append(pl.BlockSpec(arr.shape, lambda i, nd=arr.ndim: (0,) * nd))
        else:
            w, cb = spec[2], spec[3]
            if kind == "row":
                in_specs.append(pl.BlockSpec((tm, w), lambda i, cb=cb: (i, cb)))
            elif kind == "prev":
                in_specs.append(pl.BlockSpec((SUBLANES, w), lambda i, cb=cb: (jnp.maximum(i * per - 1, 0), cb)))
            else:
                in_specs.append(pl.BlockSpec((SUBLANES, w), lambda i, cb=cb: (jnp.minimum((i + 1) * per, last8), cb)))
        arrays.append(arr)
    out_shape = [jax.ShapeDtypeStruct((n_rows, w), dt) for (w, dt) in outs]
    out_shape += [jax.ShapeDtypeStruct(s, F32) for s in accs]
    out_specs = [pl.BlockSpec((tm, w), lambda i: (i, 0)) for (w, _) in outs]
    out_specs += [pl.BlockSpec(s, lambda i: (0, 0)) for s in accs]
    n_io = len(ins) + len(outs)

    def kern(*refs):
        i = pl.program_id(0)
        if accs:
            @pl.when(i == 0)
            def _():
                for r in refs[n_io:]:
                    r[...] = jnp.zeros_like(r)
        body(i, *refs)

    res = _pcall(
        kern, name=name, grid=(n_rows // tm,), in_specs=in_specs, out_specs=out_specs, out_shape=out_shape,
        compiler_params=_params(("arbitrary",)),
    )(*arrays)
    return res


def _shift_down(x, halo, j):
    r = pltpu.roll(x, j, 0)
    hr = pltpu.roll(halo, j, 0)
    rows = lax.broadcasted_iota(jnp.int32, (SUBLANES, x.shape[1]), 0)
    top = jnp.where(rows < j, hr, r[0:SUBLANES])
    return jnp.concatenate([top, r[SUBLANES:]], axis=0)


def _shift_up(x, halo, j):
    tm = x.shape[0]
    r = pltpu.roll(x, tm - j, 0)
    hr = pltpu.roll(halo, SUBLANES - j, 0)
    rows = lax.broadcasted_iota(jnp.int32, (SUBLANES, x.shape[1]), 0)
    bot = jnp.where(rows >= SUBLANES - j, hr, r[tm - SUBLANES:])
    return jnp.concatenate([r[:tm - SUBLANES], bot], axis=0)


def _conv_taps(x, halo, kw):
    return [x if kw - 1 - k == 0 else _shift_down(x, halo, kw - 1 - k) for k in range(kw)]


def _conv(taps, w):
    y = taps[0] * w[0:1]
    for k in range(1, len(taps)):
        y = y + taps[k] * w[k:k + 1]
    return y


def _rms(x, width):
    r = lax.rsqrt(jnp.sum(x * x, axis=-1, keepdims=True) * (1.0 / width) + EPS)
    return x * r, r


def _rms_bwd(xh, r, dxh, width):
    return r * (dxh - xh * (jnp.sum(dxh * xh, axis=-1, keepdims=True) * (1.0 / width)))


def _seq_flags(i, seq, tm):
    nps = seq // tm
    pos = i % nps
    return jnp.where(pos == 0, 0.0, 1.0), jnp.where(pos == nps - 1, 0.0, 1.0)


def _norm_cast(name, x, w, tm):
    t, d = x.shape

    def body(i, x_ref, w_ref, h_ref):
        xh, _ = _rms(x_ref[...], d)
        h_ref[...] = (xh * w_ref[...]).astype(BF16)

    return _rowwise(name, body, t, tm, [("row", x, d, 0), ("full", w)], [(d, BF16)])[0]


def _gdn_prep(proj, cw, gp, seq, tm):
    t = proj.shape[0]
    d = D_MODEL

    def body(i, q_ref, qh_ref, k_ref, kh_ref, v_ref, vh_ref, sm_ref, cw_ref, gp_ref, qn_ref, kn_ref, vv_ref, gs_ref, ypre_ref):
        keep, _ = _seq_flags(i, seq, tm)
        for x_ref, h_ref, o_ref, off, scale in ((q_ref, qh_ref, qn_ref, 0, GDN_DK ** -0.5),
                                               (k_ref, kh_ref, kn_ref, d, 1.0), (v_ref, vh_ref, vv_ref, 2 * d, None)):
            y = _conv(_conv_taps(x_ref[...], h_ref[...] * keep, 4), cw_ref[:, off:off + d])
            ypre_ref[:, off:off + d] = y
            a = y * _sigmoid(y)
            if scale is None:
                o_ref[...] = a
            else:
                for hh in range(GDN_HEADS):
                    s = a[:, hh * GDN_DK:(hh + 1) * GDN_DK]
                    n = lax.rsqrt(_rowsum(s * s) + EPS)
                    o_ref[:, hh * GDN_DK:(hh + 1) * GDN_DK] = s * (n * scale)
        sm = sm_ref[...]
        lane = lax.broadcasted_iota(jnp.int32, sm.shape, 1)
        beta = _sigmoid(sm)
        g = jnp.where((lane >= 8) & (lane < 16), -jnp.exp(gp_ref[0:1, :]) * _softplus(sm + gp_ref[1:2, :]), 0.0)
        gs_ref[...] = jnp.where(lane < 8, beta, _mmx(_block_tri(tm, False), g))

    ins = []
    for cb in range(3):
        ins += [("row", proj, d, cb), ("prev", proj, d, cb)]
    ins += [("row", proj, LANES, SMALL_CB), ("full", cw), ("full", gp)]
    return _rowwise("gdn_prep", body, t, tm, ins, [(d, F32), (d, F32), (d, F32), (LANES, F32), (3 * d, F32)])


def _block_tri(tm, upper):
    ri = lax.broadcasted_iota(jnp.int32, (tm, tm), 0)
    ci = lax.broadcasted_iota(jnp.int32, (tm, tm), 1)
    tri = (ri <= ci) if upper else (ri >= ci)
    return (tri & ((ri // CHUNK) == (ci // CHUNK))).astype(F32)


def _chunk_consts():
    row = lax.broadcasted_iota(jnp.int32, (CHUNK, CHUNK), 0)
    col = lax.broadcasted_iota(jnp.int32, (CHUNK, CHUNK), 1)
    return dict(
        tril=row >= col, strict=row > col, eye=(row == col).astype(F32),
        lane=lax.broadcasted_iota(jnp.int32, (CHUNK, LANES), 1),
        row1=lax.broadcasted_iota(jnp.int32, (CHUNK, 1), 0),
        ones=jnp.ones((CHUNK, LANES), F32))


def _hmap(fn, *lists):
    return [fn(*a) for a in zip(*lists)]


def _tri_inv(nmats, eye):
    x = [eye - n for n in nmats]
    p = _hmap(_mm3, nmats, nmats)
    for lvl in range(5):
        x = _hmap(lambda xi, pi: xi + _mm3(xi, pi), x, p)
        if lvl < 4:
            p = _hmap(_mm3, p, p)
    return x


def _gdn_gates(gs, gc_row, h, c):
    beta = _rowsum(jnp.where(c["lane"] == h, gs, 0.0))
    gc = _rowsum(jnp.where(c["lane"] == h + 8, gs, 0.0))
    dc = jnp.exp(jnp.where(c["tril"], gc - gc_row, -1e30))
    gl = gc[CHUNK - 1:CHUNK, :]
    return beta, dc, jnp.exp(gc), jnp.exp(gl), jnp.exp(gl - gc)


GDN_HB = GDN_HEADS


def _gdn_specs(seq, sb, hb, backward):
    assert hb == GDN_HEADS
    nsb = seq // sb
    ncb = sb // CHUNK
    order = (lambda j: nsb - 1 - j) if backward else (lambda j: j)
    specs = dict(
        wide=lambda: pl.BlockSpec((1, sb, hb * GDN_DK), lambda b, h, j: (b, order(j), h)),
        gs=pl.BlockSpec((1, sb, LANES), lambda b, h, j: (b, order(j), 0)),
        gr=pl.BlockSpec((1, ncb, GDN_HEADS, CHUNK), lambda b, h, j: (b, order(j), 0, 0)),
        st=pl.BlockSpec((1, hb, ncb * GDN_DK, GDN_DK), lambda b, h, j: (b, h, order(j), 0)),
        ti=pl.BlockSpec((1, hb, sb, CHUNK), lambda b, h, j: (b, h, order(j), 0)))
    return nsb, ncb, specs


def _riding_exchange(arrays, scatter, n_in, n_out, grid):
    n = len(arrays)
    if n == 0:
        return [], [], [], lambda body: body
    any_spec = [pl.BlockSpec(memory_space=pl.ANY)] * n

    def wrap(body):
        def wrapped(*refs):
            ins = refs[n_in:n_in + n]
            outs = refs[n_in + n + n_out:n_in + 2 * n + n_out]
            sems = refs[len(refs) - 3:]
            pid = [pl.program_id(a) for a in range(len(grid))]
            first = functools.reduce(lambda a, b: a & b, [p == 0 for p in pid])
            last = functools.reduce(lambda a, b: a & b, [p == g - 1 for p, g in zip(pid, grid)])

            @pl.when(first)
            def _():
                _exchange_phase(ins, outs, sems, scatter, start=True)

            body(*refs[:n_in], *refs[n_in + n:n_in + n + n_out], *refs[n_in + 2 * n + n_out:len(refs) - 3])

            @pl.when(last)
            def _():
                _exchange_phase(ins, outs, sems, scatter, start=False)

        return wrapped

    return any_spec, _exchange_out_shapes(arrays, scatter), _exchange_sems(n), wrap


def _gdn_chunk_fwd(qn, kn, vv, gs, gr, bsz, seq, sb, riders):
    hb = GDN_HB
    nsb, ncb, sp = _gdn_specs(seq, sb, hb, False)
    grid = (bsz, GDN_HEADS // hb, nsb)
    any_spec, rider_shapes, rider_sems, wrap = _riding_exchange(riders, False, 5, 3, grid)

    def body(q_ref, k_ref, v_ref, gs_ref, gr_ref, o_ref, st_ref, ti_ref, s_scr):
        hg = pl.program_id(1)

        @pl.when(pl.program_id(2) == 0)
        def _():
            s_scr[...] = jnp.zeros_like(s_scr)

        c = _chunk_consts()

        def chunk(n, carry):
            r = pl.ds(pl.multiple_of(n * CHUNK, CHUNK), CHUNK)
            rs = pl.ds(pl.multiple_of(n * GDN_DK, GDN_DK), GDN_DK)
            gsv = gs_ref[0, r, :]
            heads = list(range(hb))
            sls = [slice(ih * GDN_DK, (ih + 1) * GDN_DK) for ih in heads]
            q = [q_ref[0, r, sl] for sl in sls]
            k = [k_ref[0, r, sl] for sl in sls]
            v = [v_ref[0, r, sl] for sl in sls]
            beta, dc, eg, egl, ekd = zip(*[
                _gdn_gates(gsv, gr_ref[0, n, pl.ds(ih, 1), :], ih, c) for ih in heads])
            kb = _hmap(lambda a, b: a * b, k, beta)
            amat = _hmap(lambda a, b, d_: jnp.where(c["strict"], _mm(a, b, NT) * d_, 0.0), kb, k, dc)
            tinv = _tri_inv(amat, c["eye"])
            u = _hmap(lambda t_, a, b: _mm3(t_, a * b), tinv, v, beta)
            w = _hmap(lambda t_, a, b: _mm3(t_, a * b), tinv, kb, eg)
            qk = _hmap(lambda a, b, d_: _mm(a, b, NT) * d_, q, k, dc)
            s = [s_scr[ih] for ih in heads]
            v_new = _hmap(lambda a, b, s_: a - _mm(b, s_), u, w, s)
            o = _hmap(lambda a, e, s_, qk_, vn: _mm(a * e, s_) + _mm(qk_, vn), q, eg, s, qk, v_new)
            s_new = _hmap(lambda s_, e, a, f, vn: s_ * e + _mm(a * f, vn, TN), s, egl, k, ekd, v_new)
            for ih in heads:
                o_ref[0, r, sls[ih]] = o[ih]
                st_ref[0, ih, rs, :] = s[ih]
                ti_ref[0, ih, r, :] = tinv[ih]
                s_scr[ih] = s_new[ih]
            return carry

        lax.fori_loop(0, ncb, chunk, 0)

    t3 = (bsz, seq, D_MODEL)
    res = _pcall(
        wrap(body), name="gdn_chunk_fwd", grid=grid,
        in_specs=[sp["wide"](), sp["wide"](), sp["wide"](), sp["gs"], sp["gr"]] + any_spec,
        out_specs=[sp["wide"](), sp["st"], sp["ti"]] + any_spec,
        out_shape=[jax.ShapeDtypeStruct(t3, F32),
                   jax.ShapeDtypeStruct((bsz, GDN_HEADS, (seq // CHUNK) * GDN_DK, GDN_DK), F32),
                   jax.ShapeDtypeStruct((bsz, GDN_HEADS, seq, CHUNK), F32)] + rider_shapes,
        scratch_shapes=[pltpu.VMEM((hb, GDN_DK, GDN_DK), F32)] + rider_sems,
        compiler_params=_params(("arbitrary", "arbitrary", "arbitrary")),
    )(qn, kn, vv, gs, gr, *riders)
    return res[:3], res[3:]


def _ssd_prep(proj, cw, cb, sp, seq, tm):
    t = proj.shape[0]
    d = D_MODEL
    ssd_w = SSD_HEADS * SSD_P

    def body(i, x_ref, xh_ref, bc_ref, bch_ref, sm_ref, cw_ref, cb_ref, sp_ref, xs_ref, bco_ref, dtx_ref, acsx_ref, acs_ref, ypre_ref):
        keep, _ = _seq_flags(i, seq, tm)
        y = _conv(_conv_taps(x_ref[...], xh_ref[...] * keep, 4), cw_ref[:, 0:d]) + cb_ref[:, 0:d]
        ypre_ref[:, 0:d] = y
        xs_ref[...] = y * _sigmoid(y)
        y = _conv(_conv_taps(bc_ref[...], bch_ref[...] * keep, 4), cw_ref[:, d:d + 512]) + cb_ref[:, d:d + 512]
        ypre_ref[:, d:d + 512] = y
        bco_ref[...] = y * _sigmoid(y)
        sm = sm_ref[...]
        lane = lax.broadcasted_iota(jnp.int32, sm.shape, 1)
        valid = (lane >= 16) & (lane < 32)
        dt = jnp.where(valid, _softplus(sm + sp_ref[1:2, :]), 0.0)
        adt = dt * (-jnp.exp(sp_ref[0:1, :]))
        acs = _mmx(_block_tri(tm, False), adt)
        l64 = lax.broadcasted_iota(jnp.int32, (LANES, ssd_w), 0)
        d64 = lax.broadcasted_iota(jnp.int32, (LANES, ssd_w), 1)
        e64 = (l64 - 16 == d64 // SSD_P).astype(F32)
        dtx_ref[...] = _mmsel(dt, e64, terms=3)
        acsx_ref[...] = _mmsel(acs, e64, terms=3)
        acs_ref[...] = acs

    ins = [("row", proj, d, 5), ("prev", proj, d, 5), ("row", proj, 512, 12), ("prev", proj, 512, 12),
           ("row", proj, LANES, SMALL_CB), ("full", cw), ("full", cb), ("full", sp)]
    return _rowwise("ssd_prep", body, t, tm, ins,
                    [(d, F32), (512, F32), (ssd_w, F32), (ssd_w, F32), (LANES, F32), (d + 512, F32)])


SSD_GW = SSD_HPG * SSD_P


def _ssd_head(acs, ar_ref, n, head, cbm, c):
    col = _rowsum(jnp.where(c["lane"] == head + 16, acs, 0.0))
    lm = jnp.exp(jnp.where(c["tril"], col - ar_ref[0, head, pl.ds(n, 1), :], -1e30))
    return lm, cbm * lm


def _ssd_specs(seq, sb):
    nsb = seq // sb
    ncb = sb // CHUNK
    def specs(order):
        return dict(
            wide=lambda: pl.BlockSpec((1, sb, SSD_HEADS * SSD_P), lambda b, j: (b, order(j), 0)),
            bc=lambda: pl.BlockSpec((1, sb, 2 * SSD_GROUPS * SSD_N), lambda b, j: (b, order(j), 0)),
            half=lambda: pl.BlockSpec((1, sb, SSD_GROUPS * SSD_N), lambda b, j: (b, order(j), 0)),
            small=lambda: pl.BlockSpec((1, sb, LANES), lambda b, j: (b, order(j), 0)),
            ar=pl.BlockSpec((1, SSD_HEADS, ncb, CHUNK), lambda b, j: (b, 0, order(j), 0)),
            st=pl.BlockSpec((1, ncb * SSD_N, SSD_HEADS * SSD_P), lambda b, j: (b, order(j), 0)))
    return nsb, ncb, specs(lambda j: j), specs(lambda j: nsb - 1 - j)


def _ssd_chunk_fwd(xs, bc, dtx, acsx, acs, ar, bsz, seq, sb):
    nsb, ncb, sp, _ = _ssd_specs(seq, sb)

    def body(x_ref, dtx_ref, ax_ref, bc_ref, acs_ref, ar_ref, y_ref, sts_ref, st_scr):
        @pl.when(pl.program_id(1) == 0)
        def _():
            st_scr[...] = jnp.zeros_like(st_scr)

        c = _chunk_consts()
        lane5 = lax.broadcasted_iota(jnp.int32, (CHUNK, SSD_GW), 1) // SSD_P

        def chunk(n, carry):
            r = pl.ds(pl.multiple_of(n * CHUNK, CHUNK), CHUNK)
            rs = pl.ds(pl.multiple_of(n * SSD_N, SSD_N), SSD_N)
            acsv = acs_ref[0, r, :]
            for g in range(SSD_GROUPS):
                gl = slice(g * SSD_GW, (g + 1) * SSD_GW)
                x, dt, ax = x_ref[0, r, gl], dtx_ref[0, r, gl], ax_ref[0, r, gl]
                bm = bc_ref[0, r, g * SSD_N:(g + 1) * SSD_N]
                cm = bc_ref[0, r, (SSD_GROUPS + g) * SSD_N:(SSD_GROUPS + g + 1) * SSD_N]
                xdt = x * dt
                cbm = _mm(cm, bm, NT)
                al = ax[CHUNK - 1:CHUNK, :]
                st = st_scr[:, gl]
                y = _mm(cm, st) * jnp.exp(ax)
                for hh in range(SSD_HPG):
                    _, gm = _ssd_head(acsv, ar_ref, n, g * SSD_HPG + hh, cbm, c)
                    y = y + _mm(gm, jnp.where(lane5 == hh, xdt, 0.0))
                y_ref[0, r, gl] = y
                sts_ref[0, rs, gl] = st
                st_scr[:, gl] = st * jnp.exp(al) + _mm(bm, xdt * jnp.exp(al - ax), TN)
            return carry

        lax.fori_loop(0, ncb, chunk, 0)

    return _pcall(
        body, name="ssd_chunk_fwd", grid=(bsz, nsb),
        in_specs=[sp["wide"](), sp["wide"](), sp["wide"](), sp["bc"](), sp["small"](), sp["ar"]],
        out_specs=[sp["wide"](), sp["st"]],
        out_shape=[jax.ShapeDtypeStruct((bsz, seq, SSD_HEADS * SSD_P), F32),
                   jax.ShapeDtypeStruct((bsz, (seq // CHUNK) * SSD_N, SSD_HEADS * SSD_P), F32)],
        scratch_shapes=[pltpu.VMEM((SSD_N, SSD_HEADS * SSD_P), F32)],
        compiler_params=_params(("parallel", "arbitrary")),
    )(xs, dtx, acsx, bc, acs, ar)


def _gate_norm(o_gdn, y_ssd, xs, proj, gnw, snw, dvec, tm):
    t = o_gdn.shape[0]
    d = D_MODEL

    def body(i, o_ref, za_ref, y_ref, xs_ref, zs_ref, gnw_ref, snw_ref, dv_ref, out_ref):
        for hh in range(GDN_HEADS):
            sl = slice(hh * GDN_DK, (hh + 1) * GDN_DK)
            oh, _ = _rms(o_ref[:, sl], GDN_DK)
            z = za_ref[:, sl]
            out_ref[:, sl] = (oh * gnw_ref[...] * (z * _sigmoid(z))).astype(BF16)
        zs = zs_ref[...]
        yg = (y_ref[...] + dv_ref[...] * xs_ref[...]) * (zs * _sigmoid(zs))
        for g in range(SSD_GROUPS):
            sl = slice(g * 512, (g + 1) * 512)
            yh, _ = _rms(yg[:, sl], 512)
            out_ref[:, d + g * 512:d + (g + 1) * 512] = (yh * snw_ref[:, sl]).astype(BF16)

    ins = [("row", o_gdn, d, 0), ("row", proj, d, 3), ("row", y_ssd, d, 0), ("row", xs, d, 0), ("row", proj, d, 4),
           ("full", gnw), ("full", snw), ("full", dvec)]
    return _rowwise("gate_norm", body, t, tm, ins, [(2 * d, BF16)])[0]


def _mid(x, mix, pmw, pfw, tm):
    t, d = x.shape

    def body(i, x_ref, mix_ref, pmw_ref, pfw_ref, x1_ref, h2_ref):
        mh, _ = _rms(mix_ref[...], d)
        x1 = x_ref[...] + mh * pmw_ref[...]
        x1_ref[...] = x1
        xh, _ = _rms(x1, d)
        h2_ref[...] = (xh * pfw_ref[...]).astype(BF16)

    return _rowwise("mid", body, t, tm, [("row", x, d, 0), ("row", mix, d, 0), ("full", pmw), ("full", pfw)],
                    [(d, F32), (d, BF16)])


def _ffn_act(u_pre, cw, cb, seq, tm):
    t = u_pre.shape[0]

    def body(i, ug_ref, ugh_ref, uu_ref, uuh_ref, cw_ref, cb_ref, act_ref, u_ref):
        keep, _ = _seq_flags(i, seq, tm)
        gate = _conv(_conv_taps(ug_ref[...], ugh_ref[...] * keep, 3), cw_ref[:, 0:D_FF]) + cb_ref[:, 0:D_FF]
        up = _conv(_conv_taps(uu_ref[...], uuh_ref[...] * keep, 3), cw_ref[:, D_FF:2 * D_FF]) + cb_ref[:, D_FF:2 * D_FF]
        u_ref[:, 0:D_FF] = gate
        u_ref[:, D_FF:2 * D_FF] = up
        act_ref[...] = (gate * _sigmoid(gate) * up).astype(BF16)

    ins = [("row", u_pre, D_FF, 0), ("prev", u_pre, D_FF, 0), ("row", u_pre, D_FF, 1), ("prev", u_pre, D_FF, 1),
           ("full", cw), ("full", cb)]
    return _rowwise("ffn_act", body, t, tm, ins, [(D_FF, BF16), (2 * D_FF, F32)])


def _final(x1, f, tgt, w, tm):
    t, d = x1.shape

    def body(i, x1_ref, f_ref, t_ref, w_ref, dy_ref, df_ref, loss_ref, dw_ref):
        fh, r = _rms(f_ref[...], d)
        e = x1_ref[...] + fh * w_ref[...] - t_ref[...]
        loss_ref[...] += _colsum(e * e) * (0.5 / d)
        dy = e * (1.0 / d)
        dy_ref[...] = dy
        dw_ref[...] += _colsum(dy * fh)
        df_ref[...] = _rms_bwd(fh, r, dy * w_ref[...], d).astype(BF16)

    return _rowwise("final", body, t, tm, [("row", x1, d, 0), ("row", f, d, 0), ("row", tgt, d, 0), ("full", w)],
                    [(d, F32), (d, BF16)], accs=[(1, d), (1, d)])


def _ffn_bwd(u, u_pre, dact, cw, seq, tm):
    t = u.shape[0]

    def body(i, g_ref, gn_ref, up_ref, upn_ref, xg_ref, xu_ref, da_ref, dan_ref, cw_ref, dpre_ref, dcw_ref, dcb_ref):
        _, keep_next = _seq_flags(i, seq, tm)
        ext = lambda a_ref, n_ref: jnp.concatenate([a_ref[...], n_ref[...]], axis=0)
        rows = tm + SUBLANES
        gate, up = ext(g_ref, gn_ref), ext(up_ref, upn_ref)
        sg = _sigmoid(gate)
        da = jnp.concatenate([da_ref[...], dan_ref[...] * keep_next], axis=0)
        for off, grad, x_ref in ((0, da * up * _dsilu(gate, sg), xg_ref), (D_FF, da * gate * sg, xu_ref)):
            x = x_ref[...]
            own = grad[0:tm]
            acc = own * cw_ref[2:3, off:off + D_FF]
            dcb_ref[:, off:off + D_FF] += _colsum(own)
            dcw_ref[2:3, off:off + D_FF] += _colsum(own * x)
            for j in (1, 2):
                ahead = pltpu.roll(grad, rows - j, 0)[0:tm]
                acc = acc + ahead * cw_ref[2 - j:3 - j, off:off + D_FF]
                dcw_ref[2 - j:3 - j, off:off + D_FF] += _colsum(ahead * x)
            dpre_ref[:, off:off + D_FF] = acc.astype(BF16)

    ins = []
    for cb_ in range(2):
        ins += [("row", u, D_FF, cb_), ("next", u, D_FF, cb_)]
    ins += [("row", u_pre, D_FF, 0), ("row", u_pre, D_FF, 1), ("row", dact, D_FF, 0), ("next", dact, D_FF, 0), ("full", cw)]
    return _rowwise("ffn_bwd", body, t, tm, ins, [(2 * D_FF, BF16)], accs=[(SUBLANES, 2 * D_FF), (1, 2 * D_FF)])


def _assemble_dproj(dpre_gdn, dza, dzs, dpre_ssd, dsm, proj, gcw, scw, seq, tm):
    t = dpre_gdn.shape[0]
    d = D_MODEL

    def body(i, dg_ref, dgn_ref, dza_ref, dzs_ref, ds_ref, dsn_ref, dsm_ref, xq_ref, xk_ref, xv_ref, xx_ref, xbc_ref,
             gcw_ref, scw_ref, o_ref, dgcw_ref, dscw_ref):
        _, keep = _seq_flags(i, seq, tm)
        pieces = [(dg_ref, dgn_ref, gcw_ref, dgcw_ref, x_ref, 0, c0)
                  for x_ref, c0 in ((xq_ref, 0), (xk_ref, d), (xv_ref, 2 * d))]
        pieces += [(ds_ref, dsn_ref, scw_ref, dscw_ref, x_ref, 5 * d, c0) for x_ref, c0 in ((xx_ref, 0), (xbc_ref, d))]
        for d_ref, n_ref, cw_ref, dcw_ref, x_ref, base, c0 in pieces:
            w = x_ref.shape[1]
            x = x_ref[...]
            g = d_ref[:, c0:c0 + w]
            halo = n_ref[:, c0:c0 + w] * keep
            acc = g * cw_ref[3:4, c0:c0 + w]
            dcw_ref[3:4, c0:c0 + w] += _colsum(g * x)
            for j in range(1, 4):
                ahead = _shift_up(g, halo, j)
                acc = acc + ahead * cw_ref[3 - j:4 - j, c0:c0 + w]
                dcw_ref[3 - j:4 - j, c0:c0 + w] += _colsum(ahead * x)
            o_ref[:, base + c0:base + c0 + w] = acc.astype(BF16)
        o_ref[:, 3 * d:4 * d] = dza_ref[...]
        o_ref[:, 4 * d:5 * d] = dzs_ref[...]
        o_ref[:, 6 * d + 512:6 * d + 512 + LANES] = dsm_ref[...]
        o_ref[:, 6 * d + 512 + LANES:PROJ_W] = jnp.zeros((tm, PROJ_W - (6 * d + 512 + LANES)), BF16)

    ins = [("row", dpre_gdn, 3 * d, 0), ("next", dpre_gdn, 3 * d, 0), ("row", dza, d, 0), ("row", dzs, d, 0),
           ("row", dpre_ssd, d + 512, 0), ("next", dpre_ssd, d + 512, 0), ("row", dsm, LANES, 0),
           ("row", proj, d, 0), ("row", proj, d, 1), ("row", proj, d, 2), ("row", proj, d, 5), ("row", proj, 512, 12),
           ("full", gcw), ("full", scw)]
    return _rowwise("assemble_dproj", body, t, tm, ins, [(PROJ_W, BF16)], accs=[(SUBLANES, 3 * d), (SUBLANES, d + 512)])


def _mid_bwd(x1, mix, dy, dh2, pmw, pfw, tm):
    t, d = x1.shape

    def body(i, x1_ref, mix_ref, dy_ref, dh2_ref, pmw_ref, pfw_ref, dx1_ref, dmix_ref, dpm_ref, dpf_ref):
        xh, r2 = _rms(x1_ref[...], d)
        dh2 = dh2_ref[...]
        dpf_ref[...] += _colsum(dh2 * xh)
        dx1 = dy_ref[...] + _rms_bwd(xh, r2, dh2 * pfw_ref[...], d)
        dx1_ref[...] = dx1
        mh, r = _rms(mix_ref[...], d)
        dpm_ref[...] += _colsum(dx1 * mh)
        dmix_ref[...] = _rms_bwd(mh, r, dx1 * pmw_ref[...], d).astype(BF16)

    ins = [("row", x1, d, 0), ("row", mix, d, 0), ("row", dy, d, 0), ("row", dh2, d, 0), ("full", pmw), ("full", pfw)]
    return _rowwise("mid_bwd", body, t, tm, ins, [(d, F32), (d, BF16)], accs=[(1, d), (1, d)])


def _gate_norm_bwd(o_gdn, y_ssd, xs, proj, dmixin, gnw, snw, dvec, tm):
    t = o_gdn.shape[0]
    d = D_MODEL

    def body(i, o_ref, za_ref, y_ref, xs_ref, zs_ref, dma_ref, dms_ref, gnw_ref, snw_ref, dv_ref,
             do_ref, dza_ref, dy_ref, dxs_ref, dzs_ref, dgnw_ref, dsnw_ref, dd_ref):
        for hh in range(GDN_HEADS):
            sl = slice(hh * GDN_DK, (hh + 1) * GDN_DK)
            oh, r = _rms(o_ref[:, sl], GDN_DK)
            z = za_ref[:, sl]
            sz = _sigmoid(z)
            dm = dma_ref[:, sl]
            don = dm * (z * sz)
            dza_ref[:, sl] = (dm * oh * gnw_ref[...] * _dsilu(z, sz)).astype(BF16)
            dgnw_ref[...] += _colsum(don * oh)
            do_ref[:, sl] = _rms_bwd(oh, r, don * gnw_ref[...], GDN_DK)
        zs = zs_ref[...]
        sz = _sigmoid(zs)
        sil = zs * sz
        x = xs_ref[...]
        y0 = y_ref[...] + dv_ref[...] * x
        yg = y0 * sil
        dms = dms_ref[...]
        for g in range(SSD_GROUPS):
            sl = slice(g * 512, (g + 1) * 512)
            yh, r = _rms(yg[:, sl], 512)
            dsnw_ref[:, sl] += _colsum(dms[:, sl] * yh)
            dyg = _rms_bwd(yh, r, dms[:, sl] * snw_ref[:, sl], 512)
            dy0 = dyg * sil[:, sl]
            dzs_ref[:, sl] = (dyg * y0[:, sl] * _dsilu(zs[:, sl], sz[:, sl])).astype(BF16)
            dy_ref[:, sl] = dy0
            dxs_ref[:, sl] = dy0 * dv_ref[:, sl]
            dd_ref[:, sl] += _colsum(dy0 * x[:, sl])

    ins = [("row", o_gdn, d, 0), ("row", proj, d, 3), ("row", y_ssd, d, 0), ("row", xs, d, 0), ("row", proj, d, 4),
           ("row", dmixin, d, 0), ("row", dmixin, d, 1), ("full", gnw), ("full", snw), ("full", dvec)]
    return _rowwise("gate_norm_bwd", body, t, tm, ins, [(d, F32), (d, BF16), (d, F32), (d, F32), (d, BF16)],
                    accs=[(1, GDN_DK), (1, d), (1, d)])


def _ssd_chunk_bwd(xs, bc, dtx, acsx, acs, ar, dy, sts, bsz, seq, sb):
    nsb, ncb, _, sp = _ssd_specs(seq, sb)

    def body(x_ref, dtx_ref, ax_ref, bc_ref, acs_ref, ar_ref, dy_ref, sts_ref, dx_ref, dbc_ref, ddt_ref, dacs_ref, dst_scr):
        @pl.when(pl.program_id(1) == 0)
        def _():
            dst_scr[...] = jnp.zeros_like(dst_scr)

        c = _chunk_consts()
        lane5 = lax.broadcasted_iota(jnp.int32, (CHUNK, SSD_GW), 1) // SSD_P
        row5 = lax.broadcasted_iota(jnp.int32, (CHUNK, SSD_GW), 0)
        sel_in = lax.broadcasted_iota(jnp.int32, (SSD_GW, LANES), 0) // SSD_P
        sel_out = lax.broadcasted_iota(jnp.int32, (SSD_GW, LANES), 1)

        def chunk(nn, carry):
            n = ncb - 1 - nn
            r = pl.ds(pl.multiple_of(n * CHUNK, CHUNK), CHUNK)
            rs = pl.ds(pl.multiple_of(n * SSD_N, SSD_N), SSD_N)
            acsv = acs_ref[0, r, :]
            ddt = jnp.zeros((CHUNK, LANES), F32)
            dacs = jnp.zeros((CHUNK, LANES), F32)
            for g in range(SSD_GROUPS):
                gl = slice(g * SSD_GW, (g + 1) * SSD_GW)
                x, dt, ax, dyv = x_ref[0, r, gl], dtx_ref[0, r, gl], ax_ref[0, r, gl], dy_ref[0, r, gl]
                bm = bc_ref[0, r, g * SSD_N:(g + 1) * SSD_N]
                cm = bc_ref[0, r, (SSD_GROUPS + g) * SSD_N:(SSD_GROUPS + g + 1) * SSD_N]
                st = sts_ref[0, rs, gl]
                dst = dst_scr[:, gl]
                rsel = (sel_in + (16 + g * SSD_HPG) == sel_out).astype(F32)
                xdt = x * dt
                cbm = _mm(cm, bm, NT)
                al = ax[CHUNK - 1:CHUNK, :]
                ex, el = jnp.exp(ax), jnp.exp(al)
                dec = jnp.exp(al - ax)
                xd = xdt * dec
                dye = dyv * ex
                dxd = _mm(bm, dst)
                dxdt = dec * dxd
                dcm = _mm(dye, st, NT)
                dbm = _mm(xd, dst, NT)
                z = dye * _mm(cm, st) - dxd * xd
                zl = _colsum(dst * st) * el + _colsum(dxd * xd)
                z = z + jnp.where(row5 == CHUNK - 1, zl, 0.0)
                dcb = jnp.zeros((CHUNK, CHUNK), F32)
                for hh in range(SSD_HPG):
                    head = g * SSD_HPG + hh
                    lm, gm = _ssd_head(acsv, ar_ref, n, head, cbm, c)
                    dym = jnp.where(lane5 == hh, dyv, 0.0)
                    dxdt = dxdt + _mm(gm, dym, TN)
                    dg = _mm(dym, xdt, NT)
                    dcb = dcb + dg * lm
                    pm = dg * gm
                    dacs = dacs + jnp.where(c["lane"] == head + 16, _rowsum(pm) - _mmsel(pm, c["ones"], TN), 0.0)
                dbc_ref[0, r, (SSD_GROUPS + g) * SSD_N:(SSD_GROUPS + g + 1) * SSD_N] = dcm + _mm(dcb, bm)
                dbc_ref[0, r, g * SSD_N:(g + 1) * SSD_N] = dbm + _mm(dcb, cm, TN)
                dacs = dacs + _mmsel(z, rsel)
                ddt = ddt + _mmsel(dxdt * x, rsel)
                dx_ref[0, r, gl] = dxdt * dt
                dst_scr[:, gl] = dst * el + _mm(cm, dye, TN)
            ddt_ref[0, r, :] = ddt
            dacs_ref[0, r, :] = dacs
            return carry

        lax.fori_loop(0, ncb, chunk, 0)

    return _pcall(
        body, name="ssd_chunk_bwd", grid=(bsz, nsb),
        in_specs=[sp["wide"](), sp["wide"](), sp["wide"](), sp["bc"](), sp["small"](), sp["ar"], sp["wide"](), sp["st"]],
        out_specs=[sp["wide"](), sp["bc"](), sp["small"](), sp["small"]()],
        out_shape=[jax.ShapeDtypeStruct((bsz, seq, SSD_HEADS * SSD_P), F32),
                   jax.ShapeDtypeStruct((bsz, seq, 2 * SSD_GROUPS * SSD_N), F32),
                   jax.ShapeDtypeStruct((bsz, seq, LANES), F32), jax.ShapeDtypeStruct((bsz, seq, LANES), F32)],
        scratch_shapes=[pltpu.VMEM((SSD_N, SSD_HEADS * SSD_P), F32)],
        compiler_params=_params(("parallel", "arbitrary")),
    )(xs, dtx, acsx, bc, acs, ar, dy, sts)


def _ssd_prep_bwd(ypre, proj, dxs_c, dxs_d, dbc, ddt, dacs, dsm_gdn, sp, tm):
    t = proj.shape[0]
    d = D_MODEL

    def body(i, y_ref, sm_ref, dxc_ref, dxd_ref, dbc_ref, ddt_ref, dacs_ref, dsg_ref, sp_ref,
             dpre_ref, dsm_ref, dcb_ref, dsp_ref):
        for off, w, grad in ((0, d, dxc_ref[...] + dxd_ref[...]), (d, 512, dbc_ref[...])):
            y = y_ref[:, off:off + w]
            dpre = grad * _dsilu(y, _sigmoid(y))
            dpre_ref[:, off:off + w] = dpre
            dcb_ref[:, off:off + w] += _colsum(dpre)
        sm = sm_ref[...]
        lane = lax.broadcasted_iota(jnp.int32, sm.shape, 1)
        valid = (lane >= 16) & (lane < 32)
        xb = sm + sp_ref[1:2, :]
        dt = jnp.where(valid, _softplus(xb), 0.0)
        a_neg = -jnp.exp(sp_ref[0:1, :])
        dadt_s = _mmx(_block_tri(tm, True), dacs_ref[...])
        dxb = jnp.where(valid, (ddt_ref[...] + dadt_s * a_neg) * _sigmoid(xb), 0.0)
        dsm_ref[...] = (dsg_ref[...] + dxb).astype(BF16)
        dsp_ref[1:2, :] += _colsum(dxb)
        dsp_ref[0:1, :] += jnp.where(valid[0:1, :], _colsum(dadt_s * dt) * a_neg, 0.0)

    ins = [("row", ypre, d + 512, 0), ("row", proj, LANES, SMALL_CB), ("row", dxs_c, d, 0), ("row", dxs_d, d, 0),
           ("row", dbc, 512, 0), ("row", ddt, LANES, 0), ("row", dacs, LANES, 0), ("row", dsm_gdn, LANES, 0), ("full", sp)]
    return _rowwise("ssd_prep_bwd", body, t, tm, ins, [(d + 512, F32), (LANES, BF16)],
                    accs=[(1, d + 512), (SUBLANES, LANES)])


def _gdn_chunk_bwd(qn, kn, vv, gs, gr, do, sts, tis, bsz, seq, sb, riders):
    hb = GDN_HB
    nsb, ncb, sp = _gdn_specs(seq, sb, hb, True)
    grid = (bsz, GDN_HEADS // hb, nsb)
    any_spec, rider_shapes, rider_sems, wrap = _riding_exchange(riders, True, 8, 4, grid)

    def body(q_ref, k_ref, v_ref, gs_ref, gr_ref, do_ref, st_ref, ti_ref, dq_ref, dk_ref, dv_ref, dgb_ref, ds_scr):
        hg = pl.program_id(1)

        @pl.when(pl.program_id(2) == 0)
        def _():
            ds_scr[...] = jnp.zeros_like(ds_scr)

        c = _chunk_consts()

        def chunk(nn, carry):
            n = ncb - 1 - nn
            r = pl.ds(pl.multiple_of(n * CHUNK, CHUNK), CHUNK)
            rs = pl.ds(pl.multiple_of(n * GDN_DK, GDN_DK), GDN_DK)
            gsv = gs_ref[0, r, :]
            heads = list(range(hb))
            sls = [slice(ih * GDN_DK, (ih + 1) * GDN_DK) for ih in heads]
            q = [q_ref[0, r, sl] for sl in sls]
            k = [k_ref[0, r, sl] for sl in sls]
            v = [v_ref[0, r, sl] for sl in sls]
            do_ = [do_ref[0, r, sl] for sl in sls]
            s = [st_ref[0, ih, rs, :] for ih in heads]
            tinv = [ti_ref[0, ih, r, :] for ih in heads]
            dsn = [ds_scr[ih] for ih in heads]
            beta, dc, eg, egl, ekd = zip(*[
                _gdn_gates(gsv, gr_ref[0, n, pl.ds(ih, 1), :], ih, c) for ih in heads])
            mul = lambda a, b: a * b
            kb = _hmap(mul, k, beta)
            rhs_w = _hmap(mul, kb, eg)
            u = _hmap(lambda t_, a, b: _mm3(t_, a * b), tinv, v, beta)
            w = _hmap(_mm3, tinv, rhs_w)
            amat = _hmap(lambda a, b, d_: jnp.where(c["strict"], _mm(a, b, NT) * d_, 0.0), kb, k, dc)
            qk = _hmap(lambda a, b, d_: _mm(a, b, NT) * d_, q, k, dc)
            qd = _hmap(mul, q, eg)
            kd = _hmap(mul, k, ekd)
            v_new = _hmap(lambda a, b, s_: a - _mm(b, s_), u, w, s)
            dv_new = _hmap(lambda qk_, d_, kd_, dn: _mm(qk_, d_, TN) + _mm(kd_, dn), qk, do_, kd, dsn)
            dqk = _hmap(lambda d_, vn: _mm(d_, vn, NT), do_, v_new)
            dqd = _hmap(lambda d_, s_: _mm(d_, s_, NT), do_, s)
            ds_new = _hmap(lambda qd_, d_, dn, e, w_, dvn: _mm(qd_, d_, TN) + dn * e - _mm(w_, dvn, TN),
                           qd, do_, dsn, egl, w, dv_new)
            dkd = _hmap(lambda vn, dn: _mm(vn, dn, NT), v_new, dsn)
            dgl = _hmap(lambda s_, dn, e: _colsum(_rowsum(s_ * dn)) * e, s, dsn, egl)
            dw = _hmap(lambda dvn, s_: -_mm(dvn, s_, NT), dv_new, s)
            dru = _hmap(lambda t_, a: _mm3(t_, a, TN), tinv, dv_new)
            drw = _hmap(lambda t_, a: _mm3(t_, a, TN), tinv, dw)
            da = _hmap(lambda a, u_, b, w_: jnp.where(c["strict"], -(_mm(a, u_, NT) + _mm(b, w_, NT)), 0.0), dru, u, drw, w)
            m = _hmap(mul, da, dc)
            dkb = _hmap(lambda a, e, m_, k_: a * e + _mm(m_, k_), drw, eg, m, k)
            mq = _hmap(mul, dqk, dc)
            dq = _hmap(lambda mq_, k_, a, e: _mm(mq_, k_) + a * e, mq, k, dqd, eg)
            dk = _hmap(lambda m_, kb_, mq_, q_, a, e, b, be: _mm(m_, kb_, TN) + _mm(mq_, q_, TN) + a * e + b * be,
                       m, kb, mq, q, dkd, ekd, dkb, beta)
            dbeta = _hmap(lambda a, v_, b, k_: _rowsum(a * v_) + _rowsum(b * k_), dru, v, dkb, k)
            pq = _hmap(lambda a, am, b, qk_: a * am + b * qk_, da, amat, dqk, qk)
            ekk = _hmap(lambda a, b: _rowsum(a * b), dkd, kd)
            dgc = _hmap(lambda pq_, a, rw, b, qd_, e, gl_: (
                _rowsum(pq_) - _mmsel(pq_, c["ones"], TN) + (_rowsum(a * rw) + _rowsum(b * qd_) - e)
                + jnp.where(c["row1"] == CHUNK - 1, _colsum(e) + gl_, 0.0)), pq, drw, rhs_w, dqd, qd, ekk, dgl)
            for ih in heads:
                ds_scr[ih] = ds_new[ih]
                dv_ref[0, r, sls[ih]] = dru[ih] * beta[ih]
                dq_ref[0, r, sls[ih]] = dq[ih]
                dk_ref[0, r, sls[ih]] = dk[ih]
                dgb_ref[0, r, sls[ih]] = jnp.where(c["lane"] == 0, dbeta[ih], jnp.where(c["lane"] == 1, dgc[ih], 0.0))
            return carry

        lax.fori_loop(0, ncb, chunk, 0)

    res = _pcall(
        wrap(body), name="gdn_chunk_bwd", grid=grid,
        in_specs=[sp["wide"](), sp["wide"](), sp["wide"](), sp["gs"], sp["gr"], sp["wide"](), sp["st"], sp["ti"]] + any_spec,
        out_specs=[sp["wide"](), sp["wide"](), sp["wide"](), sp["wide"]()] + any_spec,
        out_shape=[jax.ShapeDtypeStruct((bsz, seq, D_MODEL), F32)] * 4 + rider_shapes,
        scratch_shapes=[pltpu.VMEM((hb, GDN_DK, GDN_DK), F32)] + rider_sems,
        compiler_params=_params(("arbitrary", "arbitrary", "arbitrary")),
    )(qn, kn, vv, gs, gr, do, sts, tis, *riders)
    return res[:4], res[4:]


def _gdn_prep_bwd(ypre, proj, dqn, dkn, dvv, dgb, gp, tm):
    t = proj.shape[0]
    d = D_MODEL

    def body(i, y_ref, sm_ref, dq_ref, dk_ref, dv_ref, dgb_ref, gp_ref, dpre_ref, dsm_ref, dgp_ref):
        for g_ref, off, scale in ((dq_ref, 0, GDN_DK ** -0.5), (dk_ref, d, 1.0), (dv_ref, 2 * d, None)):
            y = y_ref[:, off:off + d]
            sy = _sigmoid(y)
            ds_ = _dsilu(y, sy)
            if scale is None:
                dpre_ref[:, off:off + d] = g_ref[...] * ds_
            else:
                a = y * sy
                for hh in range(GDN_HEADS):
                    sl = slice(hh * GDN_DK, (hh + 1) * GDN_DK)
                    s = a[:, sl]
                    n = lax.rsqrt(_rowsum(s * s) + EPS)
                    ah = s * n
                    gq = g_ref[:, sl]
                    dpre_ref[:, off + hh * GDN_DK:off + (hh + 1) * GDN_DK] = (
                        (scale * n) * (gq - ah * _rowsum(gq * ah)) * ds_[:, sl])
        sm = sm_ref[...]
        lane = lax.broadcasted_iota(jnp.int32, sm.shape, 1)
        si = lax.broadcasted_iota(jnp.int32, (d, LANES), 0)
        so = lax.broadcasted_iota(jnp.int32, (d, LANES), 1)
        sel = (((si % GDN_DK == 0) & (so == si // GDN_DK)) | ((si % GDN_DK == 1) & (so == si // GDN_DK + 8))).astype(F32)
        dsel = _mmx(dgb_ref[...], sel)
        is_g = (lane >= 8) & (lane < 16)
        dsel = jnp.where(is_g, _mmx(_block_tri(tm, True), dsel), dsel)
        beta = _sigmoid(sm)
        xb = sm + gp_ref[1:2, :]
        a_neg = -jnp.exp(gp_ref[0:1, :])
        sp = _softplus(xb)
        dxb = jnp.where(is_g, dsel * a_neg * _sigmoid(xb), 0.0)
        dsm_ref[...] = jnp.where(lane < 8, dsel * beta * (1.0 - beta), dxb)
        dgp_ref[1:2, :] += _colsum(dxb)
        dgp_ref[0:1, :] += _colsum(jnp.where(is_g, dsel * a_neg * sp, 0.0))

    ins = [("row", ypre, 3 * d, 0), ("row", proj, LANES, SMALL_CB), ("row", dqn, d, 0), ("row", dkn, d, 0),
           ("row", dvv, d, 0), ("row", dgb, d, 0), ("full", gp)]
    return _rowwise("gdn_prep_bwd", body, t, tm, ins, [(3 * d, F32), (LANES, F32)], accs=[(SUBLANES, LANES)])


def _first_bwd(x, dh1, dx1, w, tm):
    t, d = x.shape

    def body(i, x_ref, dh_ref, dx1_ref, w_ref, dx_ref, dw_ref):
        xh, r = _rms(x_ref[...], d)
        dh = dh_ref[...]
        dw_ref[...] += _colsum(dh * xh)
        dx_ref[...] = dx1_ref[...] + _rms_bwd(xh, r, dh * w_ref[...], d)

    return _rowwise("first_bwd", body, t, tm, [("row", x, d, 0), ("row", dh1, d, 0), ("row", dx1, d, 0), ("full", w)],
                    [(d, F32)], accs=[(1, d)])


def _gather_two_level(name, arrays):
    n = len(arrays)
    n_sem = 7

    def body(*refs):
        ins, outs = refs[:n], refs[n:2 * n]
        send_sems, recv_sems, loc_sems = refs[2 * n:]
        x, y, c = lax.axis_index("x"), lax.axis_index("y"), lax.axis_index("c")
        slot = lambda px, py, pc: 4 * px + 2 * py + pc
        sibling = (x, y, 1 - c)
        chips = [(1 - x, y), (x, 1 - y), (1 - x, 1 - y)]

        def copy(t, k, src, block, to):
            return pltpu.make_async_remote_copy(
                src_ref=src, dst_ref=outs[t].at[block], send_sem=send_sems.at[t, k], recv_sem=recv_sems.at[t, k],
                device_id=to, device_id_type=pl.DeviceIdType.MESH)

        own, first, passed = [], [], []
        for t in range(n):
            own.append(pltpu.make_async_copy(ins[t], outs[t].at[slot(x, y, c)], loc_sems.at[t]))
            first.append(copy(t, 0, ins[t], slot(x, y, c), sibling))
            first += [copy(t, 1 + j, ins[t], slot(x, y, c), (px, py, c)) for j, (px, py) in enumerate(chips)]
        for cp in own + first:
            cp.start()
        for t in range(n):
            for j, (px, py) in enumerate(chips):
                copy(t, 1 + j, ins[t], slot(px, py, c), (px, py, c)).wait_recv()
                fwd = copy(t, 4 + j, outs[t].at[slot(px, py, c)], slot(px, py, c), sibling)
                fwd.start()
                passed.append(fwd)
        for t in range(n):
            copy(t, 0, ins[t], slot(x, y, 1 - c), sibling).wait_recv()
            for j, (px, py) in enumerate(chips):
                copy(t, 4 + j, ins[t], slot(px, py, 1 - c), sibling).wait_recv()
        for cp in first + passed:
            cp.wait_send()
        for cp in own:
            cp.wait()

    return _pcall(
        body, name=name,
        in_specs=[pl.BlockSpec(memory_space=pl.ANY)] * n,
        out_specs=[pl.BlockSpec(memory_space=pl.ANY)] * n,
        out_shape=_exchange_out_shapes(arrays, False),
        scratch_shapes=[pltpu.SemaphoreType.DMA((n, n_sem)), pltpu.SemaphoreType.DMA((n, n_sem)), pltpu.SemaphoreType.DMA((n,))],
    )(*arrays)


def _exchange_out_shapes(arrays, scatter):
    return [jax.ShapeDtypeStruct(a.shape if scatter else (N_DEV,) + a.shape, a.dtype) for a in arrays]


def _exchange_sems(n):
    return [pltpu.SemaphoreType.DMA((n, N_DEV - 1)), pltpu.SemaphoreType.DMA((n, N_DEV - 1)), pltpu.SemaphoreType.DMA((n,))]


def _exchange_phase(ins, outs, sems, scatter, start):
    send_sems, recv_sems, loc_sems = sems
    x, y, c = lax.axis_index("x"), lax.axis_index("y"), lax.axis_index("c")
    me = 4 * x + 2 * y + c
    for t in range(len(ins)):
        loc = pltpu.make_async_copy(ins[t].at[me] if scatter else ins[t], outs[t].at[me], loc_sems.at[t])
        if start:
            loc.start()
        else:
            loc.wait()
        for k in range(N_DEV - 1):
            bx, by, bc = ((k + 1) >> 2) & 1, ((k + 1) >> 1) & 1, (k + 1) & 1
            px = 1 - x if bx else x
            py = 1 - y if by else y
            pc = 1 - c if bc else c
            peer = 4 * px + 2 * py + pc
            src = ins[t].at[peer] if scatter else ins[t]
            copy = lambda dst: pltpu.make_async_remote_copy(
                src_ref=src, dst_ref=dst, send_sem=send_sems.at[t, k], recv_sem=recv_sems.at[t, k],
                device_id=(px, py, pc), device_id_type=pl.DeviceIdType.MESH)
            if start:
                copy(outs[t].at[me]).start()
            else:
                copy(outs[t].at[me]).wait_send()
                copy(outs[t].at[peer]).wait_recv()


def _adam_math(w, g, m, v):
    m = ADAM_B1 * m + (1.0 - ADAM_B1) * g
    v = ADAM_B2 * v + (1.0 - ADAM_B2) * (g * g)
    m_hat = m / (1.0 - ADAM_B1 ** ADAM_STEP)
    v_hat = v / (1.0 - ADAM_B2 ** ADAM_STEP)
    delta = -ADAM_LR * (m_hat / (jnp.sqrt(v_hat) + ADAM_EPS) + ADAM_WD * w)
    return delta, m, v


def _adam_big(name, parts, w, m, v, tm):
    r, c = w.shape
    tm = tm if r % tm == 0 else r

    def body(p_ref, w_ref, m_ref, v_ref, g_ref, d_ref, nm_ref, nv_ref):
        g = p_ref[0].astype(F32)
        for s in range(1, N_DEV):
            g = g + p_ref[s].astype(F32)
        g_ref[...] = g
        d_ref[...], nm_ref[...], nv_ref[...] = _adam_math(w_ref[...], g, m_ref[...], v_ref[...])

    blk = lambda: pl.BlockSpec((tm, c), lambda i: (i, 0))
    return _pcall(
        body, name=name, grid=(r // tm,),
        in_specs=[pl.BlockSpec((N_DEV, tm, c), lambda i: (0, i, 0)), blk(), blk(), blk()],
        out_specs=[blk(), blk(), blk(), blk()],
        out_shape=[jax.ShapeDtypeStruct((r, c), F32)] * 4,
        compiler_params=_params(("parallel",)),
    )(parts, w, m, v)


SMALL_ROWS = 56
ROW_DD, ROW_LOSS = 5, 6


def _small_sum(gathered):
    def body(g_ref, o_ref, x_ref):
        s = g_ref[0]
        for dev in range(1, N_DEV):
            s = s + g_ref[dev]
        o_ref[...] = s
        ri = lax.broadcasted_iota(jnp.int32, (D_MODEL, LANES), 0)
        ro = lax.broadcasted_iota(jnp.int32, (D_MODEL, LANES), 1)
        heads = _mmx(jnp.broadcast_to(s[ROW_DD:ROW_DD + 1, :], (SUBLANES, D_MODEL)), (ri // SSD_P == ro).astype(F32))
        loss = _rowsum(jnp.broadcast_to(s[ROW_LOSS:ROW_LOSS + 1, :], (SUBLANES, D_MODEL)))
        row = lax.broadcasted_iota(jnp.int32, (SUBLANES, LANES), 0)
        x_ref[...] = jnp.where(row == 0, heads, jnp.broadcast_to(loss, (SUBLANES, LANES)))

    return _pcall(
        body, name="small_sum",
        out_shape=[jax.ShapeDtypeStruct((SMALL_ROWS, D_MODEL), F32), jax.ShapeDtypeStruct((SUBLANES, LANES), F32)],
        compiler_params=_params(None),
    )(gathered)


def _adam_small(g, w, m, v):
    def body(g_ref, w_ref, m_ref, v_ref, d_ref, nm_ref, nv_ref):
        d_ref[...], nm_ref[...], nv_ref[...] = _adam_math(w_ref[...], g_ref[...], m_ref[...], v_ref[...])

    return _pcall(body, name="adam_small", out_shape=[jax.ShapeDtypeStruct(g.shape, F32)] * 3,
                  compiler_params=_params(None))(g, w, m, v)


def _pack(pieces, rows):
    flat = jnp.concatenate([p.reshape(-1).astype(F32) for p in pieces])
    return jnp.pad(flat, (0, rows * D_MODEL - flat.shape[0])).reshape(rows, D_MODEL)


def _unpack(packed, shapes):
    flat = packed.reshape(-1)
    out, off = [], 0
    for shp in shapes:
        size = 1
        for s in shp:
            size *= s
        out.append(flat[off:off + size].reshape(shp))
        off += size
    return out


def _permute_in(w):
    pad = jnp.zeros((w.shape[0], PROJ_W - D_IN), w.dtype)
    return jnp.concatenate([w[:, 0:4096], w[:, 4112:6672], w[:, 4096:4112], w[:, 6672:6688], pad], axis=1)


def _unpermute_in(g):
    return jnp.concatenate([g[:, 0:4096], g[:, 6656:6672], g[:, 4096:6656], g[:, 6672:6688]], axis=1)


def _lane_row(vec, start):
    return jnp.zeros((LANES,), F32).at[start:start + vec.shape[0]].set(vec)


def _cols_from_shards(g):
    return jnp.transpose(g, (1, 0, 2)).reshape(g.shape[1], N_DEV * g.shape[2])


def _cols_to_shards(a):
    return jnp.transpose(a.astype(BF16).reshape(a.shape[0], N_DEV, a.shape[1] // N_DEV), (1, 0, 2))


def _rows_to_shards(a):
    return a.astype(BF16).reshape(N_DEV, a.shape[0] // N_DEV, a.shape[1])


def _local_step(x, tgt, wp_in, rest, p, rest_is_sharded):
    bsz, seq, d = x.shape
    t = bsz * seq
    x2 = x.reshape(t, d)
    tgt2 = tgt.reshape(t, d)
    tm = min(256, seq)
    tm_wide = min(128, seq)
    sb = min(512, seq)

    gp = jnp.zeros((SUBLANES, LANES), F32).at[0].set(_lane_row(p["gdn_a_log"], 8)).at[1].set(_lane_row(p["gdn_dt_bias"], 8))
    sp = jnp.zeros((SUBLANES, LANES), F32).at[0].set(_lane_row(p["ssd_a_log"], 16)).at[1].set(_lane_row(p["ssd_dt_bias"], 16))
    dvec = jnp.repeat(p["ssd_d"], SSD_P).reshape(1, d)
    row = lambda v: v.reshape(1, -1)
    pre_mix, post_mix, pre_ffn, post_ffn = (row(p[k]) for k in ("pre_mix_norm", "post_mix_norm", "pre_ffn_norm", "post_ffn_norm"))
    gnw, snw = row(p["gdn_norm_w"]), row(p["ssd_norm_w"])
    gcw, scw, scb, fcw, fcb = p["gdn_conv_w"], p["ssd_conv_w"], row(p["ssd_conv_b"]), p["ffn_conv_w"], row(p["ffn_conv_b"])

    h1 = _norm_cast("norm_in", x2, pre_mix, tm)
    proj = _matmul("mm_proj", h1, wp_in, "nn", F32)
    b3 = lambda a: a.reshape(bsz, seq, a.shape[-1])
    b2 = lambda a: a.reshape(t, a.shape[-1])
    rows_of = lambda a, lo, n: jnp.transpose(a[:, lo:lo + n].reshape(bsz, seq // CHUNK, CHUNK, n), (0, 3, 1, 2))
    qn, kn, vv, gs, ypre_gdn = _gdn_prep(proj, gcw, gp, seq, tm)
    qn, kn, vv, gs = b3(qn), b3(kn), b3(vv), b3(gs)
    gr = jnp.transpose(b2(gs)[:, 8:8 + GDN_HEADS].reshape(bsz, seq // CHUNK, CHUNK, GDN_HEADS), (0, 1, 3, 2))
    (o_gdn, gdn_st, gdn_ti), gathered = _gdn_chunk_fwd(qn, kn, vv, gs, gr, bsz, seq, sb, list(rest) if rest_is_sharded else [])
    if rest_is_sharded:
        w_out, w_up, w_down = gathered[0].reshape(-1, d), _cols_from_shards(gathered[1]), gathered[2].reshape(-1, d)
    else:
        w_out, w_up, w_down = rest
    o_gdn = b2(o_gdn)
    xs, bc, dtx, acsx, acs, ypre_ssd = _ssd_prep(proj, scw, scb, sp, seq, tm)
    ar = rows_of(acs, 16, SSD_HEADS)
    y_ssd, ssd_st = _ssd_chunk_fwd(b3(xs), b3(bc), b3(dtx), b3(acsx), b3(acs), ar, bsz, seq, sb)
    y_ssd = b2(y_ssd)
    mixin = _gate_norm(o_gdn, y_ssd, xs, proj, gnw, snw, dvec, tm)
    mix = _matmul("mm_out", mixin, w_out, "nn", F32)
    x1, h2 = _mid(x2, mix, post_mix, pre_ffn, tm)
    u_pre = _matmul("mm_up", h2, w_up, "nn", F32)
    act, u = _ffn_act(u_pre, fcw, fcb, seq, tm_wide)
    f = _matmul("mm_down", act, w_down, "nn", F32, tk=1408)
    dy, df, loss_lanes, d_post_ffn = _final(x1, f, tgt2, post_ffn, tm)

    g_down = _matmul("mm_dw_down", act, df, "tn", F32, tm=1408)
    dact = _matmul("mm_dact", df, w_down, "nt", F32, tn=1408)
    du_pre, d_fcw, d_fcb = _ffn_bwd(u, u_pre, dact, fcw, seq, tm_wide)
    g_up = _matmul("mm_dw_up", h2, du_pre, "tn", F32)
    dh2 = _matmul("mm_dh2", du_pre, w_up, "nt", F32)
    dx1, dmix, d_post_mix, d_pre_ffn = _mid_bwd(x1, mix, dy, dh2, post_mix, pre_ffn, tm)
    g_out = _matmul("mm_dw_out", mixin, dmix, "tn", F32)
    dmixin = _matmul("mm_dmixin", dmix, w_out, "nt", F32)
    do_gdn, dza, dy_ssd, dxs_d, dzs, d_gnw, d_snw, d_dd = _gate_norm_bwd(o_gdn, y_ssd, xs, proj, dmixin, gnw, snw, dvec, tm)
    dxs_c, dbc, ddt, dacs = (b2(a) for a in _ssd_chunk_bwd(
        b3(xs), b3(bc), b3(dtx), b3(acsx), b3(acs), ar, b3(dy_ssd), ssd_st, bsz, seq, sb))
    riders = [_rows_to_shards(g_out), _cols_to_shards(g_up), _rows_to_shards(g_down)] if rest_is_sharded else []
    dgdn, received = _gdn_chunk_bwd(qn, kn, vv, gs, gr, b3(do_gdn), gdn_st, gdn_ti, bsz, seq, min(256, seq), riders)
    if rest_is_sharded:
        g_out, g_up, g_down = received
    dqn, dkn, dvv, dgb = (b2(a) for a in dgdn)
    dpre_gdn, dsm_gdn, d_gp = _gdn_prep_bwd(ypre_gdn, proj, dqn, dkn, dvv, dgb, gp, tm)
    dpre_ssd, dsm, d_scb, d_sp = _ssd_prep_bwd(ypre_ssd, proj, dxs_c, dxs_d, dbc, ddt, dacs, dsm_gdn, sp, tm)
    dproj, d_gcw, d_scw = _assemble_dproj(dpre_gdn, dza, dzs, dpre_ssd, dsm, proj, gcw, scw, seq, tm)
    g_in = _matmul("mm_dw_in", h1, dproj, "tn", F32)
    if rest_is_sharded:
        dh1, (g_in,) = _matmul("mm_dh1", dproj, wp_in, "nt", F32, scatter_riders=[_cols_to_shards(_unpermute_in(g_in))])
    else:
        dh1 = _matmul("mm_dh1", dproj, wp_in, "nt", F32)
    dx, d_pre_mix = _first_bwd(x2, dh1, dx1, pre_mix, tm)

    small = dict(pre_mix_norm=d_pre_mix, ssd_norm_w=d_snw, post_mix_norm=d_post_mix, pre_ffn_norm=d_pre_ffn,
                 post_ffn_norm=d_post_ffn, dd_lanes=d_dd, loss_lanes=loss_lanes, gdn_gates=d_gp, ssd_gates=d_sp,
                 gdn_norm_w=d_gnw, gdn_conv_w=d_gcw[0:4], ssd_conv_w=d_scw[0:4], ssd_conv_b=d_scb,
                 ffn_conv_w=d_fcw[0:3], ffn_conv_b=d_fcb)
    return dx.reshape(bsz, seq, d), g_in, g_out, g_up, g_down, small


def kernel(x, pre_mix_norm, w_in, gdn_conv_w, gdn_a_log, gdn_dt_bias, gdn_norm_w, ssd_conv_w, ssd_conv_b, ssd_a_log, ssd_dt_bias, ssd_d, ssd_norm_w, w_out, post_mix_norm, pre_ffn_norm, w_up, ffn_conv_w, ffn_conv_b, w_down, post_ffn_norm, loss_target, m_pre_mix_norm, m_w_in, m_gdn_conv_w, m_gdn_a_log, m_gdn_dt_bias, m_gdn_norm_w, m_ssd_conv_w, m_ssd_conv_b, m_ssd_a_log, m_ssd_dt_bias, m_ssd_d, m_ssd_norm_w, m_w_out, m_post_mix_norm, m_pre_ffn_norm, m_w_up, m_ffn_conv_w, m_ffn_conv_b, m_w_down, m_post_ffn_norm, v_pre_mix_norm, v_w_in, v_gdn_conv_w, v_gdn_a_log, v_gdn_dt_bias, v_gdn_norm_w, v_ssd_conv_w, v_ssd_conv_b, v_ssd_a_log, v_ssd_dt_bias, v_ssd_d, v_ssd_norm_w, v_w_out, v_post_mix_norm, v_pre_ffn_norm, v_w_up, v_ffn_conv_w, v_ffn_conv_b, v_w_down, v_post_ffn_norm):
    names = ["pre_mix_norm", "w_in", "gdn_conv_w", "gdn_a_log", "gdn_dt_bias", "gdn_norm_w", "ssd_conv_w", "ssd_conv_b",
             "ssd_a_log", "ssd_dt_bias", "ssd_d", "ssd_norm_w", "w_out", "post_mix_norm", "pre_ffn_norm", "w_up",
             "ffn_conv_w", "ffn_conv_b", "w_down", "post_ffn_norm"]
    w_args = [pre_mix_norm, w_in, gdn_conv_w, gdn_a_log, gdn_dt_bias, gdn_norm_w, ssd_conv_w, ssd_conv_b, ssd_a_log, ssd_dt_bias, ssd_d, ssd_norm_w, w_out, post_mix_norm, pre_ffn_norm, w_up, ffn_conv_w, ffn_conv_b, w_down, post_ffn_norm]
    m_args = [m_pre_mix_norm, m_w_in, m_gdn_conv_w, m_gdn_a_log, m_gdn_dt_bias, m_gdn_norm_w, m_ssd_conv_w, m_ssd_conv_b, m_ssd_a_log, m_ssd_dt_bias, m_ssd_d, m_ssd_norm_w, m_w_out, m_post_mix_norm, m_pre_ffn_norm, m_w_up, m_ffn_conv_w, m_ffn_conv_b, m_w_down, m_post_ffn_norm]
    v_args = [v_pre_mix_norm, v_w_in, v_gdn_conv_w, v_gdn_a_log, v_gdn_dt_bias, v_gdn_norm_w, v_ssd_conv_w, v_ssd_conv_b, v_ssd_a_log, v_ssd_dt_bias, v_ssd_d, v_ssd_norm_w, v_w_out, v_post_mix_norm, v_pre_ffn_norm, v_w_up, v_ffn_conv_w, v_ffn_conv_b, v_w_down, v_post_ffn_norm]
    w = {k: a[0] for k, a in zip(names, w_args)}
    m = {k: a[0] for k, a in zip(names, m_args)}
    v = {k: a[0] for k, a in zip(names, v_args)}
    idx = 4 * lax.axis_index("x") + 2 * lax.axis_index("y") + lax.axis_index("c")
    big = ("w_in", "w_out", "w_up", "w_down")
    conv = ("gdn_conv_w", "ssd_conv_w", "ffn_conv_w")

    conv_local = jnp.concatenate([jnp.pad(w[k], ((0, 4 - w[k].shape[0]), (0, 0))) for k in conv], axis=1)
    g_in, g_conv = _gather_two_level("gather_weights", [w["w_in"].astype(BF16), conv_local])
    wp_in = _permute_in(_cols_from_shards(g_in))
    p = {k: w[k] for k in names if k not in big and k not in conv}
    off = 0
    for k in conv:
        cw = w[k].shape[1]
        p[k] = jnp.transpose(g_conv[:, :w[k].shape[0], off:off + cw], (1, 0, 2)).reshape(w[k].shape[0], N_DEV * cw)
        off += cw

    rest = tuple(w[k].astype(BF16) for k in ("w_out", "w_up", "w_down"))
    dx, p_in, p_out, p_up, p_down, small = _local_step(x, loss_target, wp_in, rest, p, True)

    gate_row = jnp.concatenate([small["gdn_gates"][0], small["gdn_gates"][1], small["ssd_gates"][0], small["ssd_gates"][1],
                                small["gdn_norm_w"][0], jnp.zeros((D_MODEL - 5 * LANES,), F32)]).reshape(1, D_MODEL)
    pack = _pack([small["pre_mix_norm"], small["ssd_norm_w"], small["post_mix_norm"], small["pre_ffn_norm"],
                  small["post_ffn_norm"], small["dd_lanes"], small["loss_lanes"], gate_row,
                  small["gdn_conv_w"], small["ssd_conv_w"], jnp.pad(small["ssd_conv_b"], ((0, 0), (0, 512))),
                  jnp.pad(small["ffn_conv_w"].reshape(-1), (0, 17 * D_MODEL - 3 * 2 * D_FF)),
                  jnp.pad(small["ffn_conv_b"], ((0, 0), (0, 512)))], SMALL_ROWS)
    (pack_all,) = _gather_two_level("gather_small", [pack])
    ssum, extra = _small_sum(pack_all)

    grads, deltas, new_m, new_v = {}, {}, {}, {}
    for k, parts in (("w_in", p_in), ("w_out", p_out), ("w_up", p_up), ("w_down", p_down)):
        grads[k], deltas[k], new_m[k], new_v[k] = _adam_big("adam_" + k, parts, w[k], m[k], v[k], 256)

    flat = ssum.reshape(-1)
    gate = ssum[7]
    sg = dict(pre_mix_norm=ssum[0], ssd_norm_w=ssum[1], post_mix_norm=ssum[2], pre_ffn_norm=ssum[3], post_ffn_norm=ssum[4],
              gdn_a_log=gate[8:16], gdn_dt_bias=gate[LANES + 8:LANES + 16], ssd_a_log=gate[2 * LANES + 16:2 * LANES + 32],
              ssd_dt_bias=gate[3 * LANES + 16:3 * LANES + 32], gdn_norm_w=gate[4 * LANES:5 * LANES], ssd_d=extra[0, 0:SSD_HEADS])
    o = 8 * D_MODEL
    full_gcw = flat[o:o + 4 * 3072].reshape(4, 3072)
    o += 12 * D_MODEL
    full_scw = flat[o:o + 4 * 1536].reshape(4, 1536)
    o += 6 * D_MODEL
    sg["ssd_conv_b"] = flat[o:o + 1536]
    o += 2 * D_MODEL
    full_fcw = flat[o:o + 3 * 2 * D_FF].reshape(3, 2 * D_FF)
    o += 17 * D_MODEL
    sg["ffn_conv_b"] = flat[o:o + 2 * D_FF]
    for k, full in (("gdn_conv_w", full_gcw), ("ssd_conv_w", full_scw), ("ffn_conv_w", full_fcw)):
        cw = w[k].shape[1]
        sg[k] = lax.dynamic_slice_in_dim(full, idx * cw, cw, axis=1)
    small_names = [k for k in names if k not in big]
    rows = 24
    gpk = _pack([sg[k] for k in small_names], rows)
    dpk, mpk, vpk = _adam_small(gpk, _pack([w[k] for k in small_names], rows), _pack([m[k] for k in small_names], rows),
                                _pack([v[k] for k in small_names], rows))
    shapes = [w[k].shape for k in small_names]
    for k, g_, d_, m_, v_ in zip(small_names, _unpack(gpk, shapes), _unpack(dpk, shapes), _unpack(mpk, shapes), _unpack(vpk, shapes)):
        grads[k], deltas[k], new_m[k], new_v[k] = g_, d_, m_, v_

    loss = extra[1, 0]
    lead = lambda a: a[None]
    return (loss, dx, *[lead(grads[k]) for k in names], *[lead(deltas[k]) for k in names],
            *[lead(new_m[k]) for k in names], *[lead(new_v[k]) for k in names])
```

```python
import functools

import jax
import jax.numpy as jnp
from jax import lax
from jax.experimental import pallas as pl
from jax.experimental.pallas import tpu as pltpu

F32 = jnp.float32
BF16 = jnp.bfloat16
MXU_DTYPE = jnp.bfloat16
HIGHEST = lax.Precision.HIGHEST
VMEM_LIMIT_V7X = 48 * 1024 * 1024
SUBLANES = 8
LANES = 128

D_MODEL = 1024
GDN_HEADS = 8
GDN_DK = 128
SSD_HEADS = 16
SSD_P = 64
SSD_GROUPS = 2
SSD_HPG = 8
SSD_N = 128
CHUNK = 64
D_FF = 2816
EPS = 1e-6
N_DEV = 8
PROJ_W = 7168
SMALL_CB = 52
D_IN = 6688

ADAM_LR = 0.001
ADAM_B1 = 0.9
ADAM_B2 = 0.999
ADAM_EPS = 1e-08
ADAM_WD = 0.01
ADAM_STEP = 10

NN = (((1,), (0,)), ((), ()))
NT = (((1,), (1,)), ((), ()))
TN = (((0,), (0,)), ((), ()))


def _pcall(body, **kw):
    return pl.pallas_call(body, **kw)


def _mm(a, b, dims=NN):
    return lax.dot_general(a.astype(MXU_DTYPE), b.astype(MXU_DTYPE), dims, preferred_element_type=F32)


def _mmx(a, b, dims=NN):
    return lax.dot_general(a, b, dims, precision=HIGHEST, preferred_element_type=F32)


def _split(a):
    hi = a.astype(MXU_DTYPE)
    return hi, (a - hi.astype(F32)).astype(MXU_DTYPE)


def _mm3(a, b, dims=NN):
    (ah, al), (bh, bl) = _split(a), _split(b)
    dot = lambda p, q: lax.dot_general(p, q, dims, preferred_element_type=F32)
    return dot(ah, bh) + (dot(ah, bl) + dot(al, bh))


def _mmsel(a, sel, dims=NN, terms=2):
    s = sel.astype(MXU_DTYPE)
    out = None
    for _ in range(terms):
        part = a.astype(MXU_DTYPE)
        a = a - part.astype(F32)
        prod = lax.dot_general(part, s, dims, preferred_element_type=F32)
        out = prod if out is None else out + prod
    return out


def _sigmoid(x):
    return 0.5 * jnp.tanh(0.5 * x) + 0.5


def _softplus(x):
    return jnp.maximum(x, 0.0) + jnp.log(1.0 + jnp.exp(-jnp.abs(x)))


def _dsilu(x, s):
    return s * (1.0 + x * (1.0 - s))


def _rowsum(x):
    return jnp.sum(x, axis=1, keepdims=True)


def _colsum(x):
    return jnp.sum(x, axis=0, keepdims=True)


def _pick(dim, pref):
    if dim <= pref:
        return dim
    best = None
    t = LANES
    while t <= pref:
        if dim % t == 0:
            best = t
        t += LANES
    return dim if best is None else best


def _params(sem):
    return pltpu.CompilerParams(dimension_semantics=sem, vmem_limit_bytes=VMEM_LIMIT_V7X)


def _matmul(name, a, b, mode, out_dtype, tm=1024, tn=1024, tk=1024):
    if mode == "nn":
        (m, k), (_, n) = a.shape, b.shape
    elif mode == "nt":
        (m, k), (n, _) = a.shape, b.shape
    else:
        (k, m), (_, n) = a.shape, b.shape
    tm, tn, tk = _pick(m, tm), _pick(n, tn), _pick(k, tk)
    nk = k // tk
    if mode == "tn":
        a_spec = pl.BlockSpec((tk, tm), lambda i, j, kk: (kk, i))
    else:
        a_spec = pl.BlockSpec((tm, tk), lambda i, j, kk: (i, kk))
    if mode == "nt":
        b_spec = pl.BlockSpec((tn, tk), lambda i, j, kk: (j, kk))
    else:
        b_spec = pl.BlockSpec((tk, tn), lambda i, j, kk: (kk, j))
    dims = {"nn": NN, "nt": NT, "tn": TN}[mode]

    def body(a_ref, b_ref, o_ref, *acc):
        if nk == 1:
            o_ref[...] = _mm(a_ref[...], b_ref[...], dims).astype(out_dtype)
            return
        kk = pl.program_id(2)

        @pl.when(kk == 0)
        def _():
            acc[0][...] = jnp.zeros_like(acc[0])

        acc[0][...] += _mm(a_ref[...], b_ref[...], dims)

        @pl.when(kk == nk - 1)
        def _():
            o_ref[...] = acc[0][...].astype(out_dtype)

    return _pcall(
        body, name=name, grid=(m // tm, n // tn, nk),
        in_specs=[a_spec, b_spec],
        out_specs=pl.BlockSpec((tm, tn), lambda i, j, kk: (i, j)),
        out_shape=jax.ShapeDtypeStruct((m, n), out_dtype),
        scratch_shapes=[pltpu.VMEM((tm, tn), F32)] if nk > 1 else [],
        compiler_params=_params(("parallel", "parallel", "arbitrary")),
    )(a, b)


def _matmul_rows(name, a, b, mode, epilogue, row_ins, full_ins, outs, accs=(), tm=512, tk=1024, scatter_riders=()):
    if mode == "nn":
        (m, k), (_, n) = a.shape, b.shape
    else:
        (m, k), (n, _) = a.shape, b.shape
    tm, tk = _pick(m, tm), _pick(k, tk)
    nk = k // tk
    a_spec = pl.BlockSpec((tm, tk), lambda i, kk: (i, kk))
    b_spec = pl.BlockSpec((n, tk), lambda i, kk: (0, kk)) if mode == "nt" else pl.BlockSpec((tk, n), lambda i, kk: (kk, 0))
    dims = NT if mode == "nt" else NN
    n_row, n_full, n_out, n_acc = len(row_ins), len(full_ins), len(outs), len(accs)

    def body(a_ref, b_ref, *rest):
        ins = rest[:n_row + n_full]
        out_refs = rest[n_row + n_full:n_row + n_full + n_out]
        acc_refs = rest[n_row + n_full + n_out:n_row + n_full + n_out + n_acc]
        prod_scr = rest[-1]
        i, kk = pl.program_id(0), pl.program_id(1)

        if n_acc:
            @pl.when((i == 0) & (kk == 0))
            def _():
                for r in acc_refs:
                    r[...] = jnp.zeros_like(r)

        @pl.when(kk == 0)
        def _():
            prod_scr[...] = jnp.zeros_like(prod_scr)

        prod_scr[...] += _mm(a_ref[...], b_ref[...], dims)

        @pl.when(kk == nk - 1)
        def _():
            epilogue(prod_scr[...], *ins, *out_refs, *acc_refs)

    grid = (m // tm, nk)
    riders = list(scatter_riders)
    n_in = 2 + n_row + n_full
    any_spec, rider_shapes, rider_sems, wrap = _riding_exchange(riders, True, n_in, n_out + n_acc, grid)
    in_specs = [a_spec, b_spec] + [pl.BlockSpec((tm, r.shape[1]), lambda i, kk: (i, 0)) for r in row_ins]
    in_specs += [pl.BlockSpec(f.shape, lambda i, kk, nd=f.ndim: (0,) * nd) for f in full_ins]
    out_specs = [pl.BlockSpec((tm, w), lambda i, kk: (i, 0)) for w, _ in outs]
    out_specs += [pl.BlockSpec(s, lambda i, kk: (0, 0)) for s in accs]
    out_shape = [jax.ShapeDtypeStruct((m, w), dt) for w, dt in outs] + [jax.ShapeDtypeStruct(s, F32) for s in accs]
    res = _pcall(
        wrap(body), name=name, grid=grid,
        in_specs=in_specs + any_spec, out_specs=out_specs + any_spec, out_shape=out_shape + rider_shapes,
        scratch_shapes=[pltpu.VMEM((tm, n), F32)] + rider_sems,
        compiler_params=_params(("arbitrary", "arbitrary")),
    )(a, b, *row_ins, *full_ins, *riders)
    return (res[:n_out + n_acc], res[n_out + n_acc:]) if riders else res


def _rowwise(name, body, n_rows, tm, ins, outs, accs=()):
    arrays, in_specs = [], []
    last8 = n_rows // SUBLANES - 1
    per = tm // SUBLANES
    for spec in ins:
        kind, arr = spec[0], spec[1]
        if kind == "full":
            in_specs.append(pl.BlockSpec(arr.shape, lambda i, nd=arr.ndim: (0,) * nd))
        else:
            w, cb = spec[2], spec[3]
            if kind == "row":
                in_specs.append(pl.BlockSpec((tm, w), lambda i, cb=cb: (i, cb)))
            elif kind == "prev":
                in_specs.append(pl.BlockSpec((SUBLANES, w), lambda i, cb=cb: (jnp.maximum(i * per - 1, 0), cb)))
            else:
                in_specs.append(pl.BlockSpec((SUBLANES, w), lambda i, cb=cb: (jnp.minimum((i + 1) * per, last8), cb)))
        arrays.append(arr)
    out_shape = [jax.ShapeDtypeStruct((n_rows, w), dt) for (w, dt) in outs]
    out_shape += [jax.ShapeDtypeStruct(s, F32) for s in accs]
    out_specs = [pl.BlockSpec((tm, w), lambda i: (i, 0)) for (w, _) in outs]
    out_specs += [pl.BlockSpec(s, lambda i: (0, 0)) for s in accs]
    n_io = len(ins) + len(outs)

    def kern(*refs):
        i = pl.program_id(0)
        if accs:
            @pl.when(i == 0)
            def _():
                for r in refs[n_io:]:
                    r[...] = jnp.zeros_like(r)
        body(i, *refs)

    res = _pcall(
        kern, name=name, grid=(n_rows // tm,), in_specs=in_specs, out_specs=out_specs, out_shape=out_shape,
        compiler_params=_params(("arbitrary",)),
    )(*arrays)
    return res


def _shift_down(x, halo, j):
    r = pltpu.roll(x, j, 0)
    hr = pltpu.roll(halo, j, 0)
    rows = lax.broadcasted_iota(jnp.int32, (SUBLANES, x.shape[1]), 0)
    top = jnp.where(rows < j, hr, r[0:SUBLANES])
    return jnp.concatenate([top, r[SUBLANES:]], axis=0)


def _shift_up(x, halo, j):
    tm = x.shape[0]
    r = pltpu.roll(x, tm - j, 0)
    hr = pltpu.roll(halo, SUBLANES - j, 0)
    rows = lax.broadcasted_iota(jnp.int32, (SUBLANES, x.shape[1]), 0)
    bot = jnp.where(rows >= SUBLANES - j, hr, r[tm - SUBLANES:])
    return jnp.concatenate([r[:tm - SUBLANES], bot], axis=0)


def _conv_taps(x, halo, kw):
    return [x if kw - 1 - k == 0 else _shift_down(x, halo, kw - 1 - k) for k in range(kw)]


def _conv(taps, w):
    y = taps[0] * w[0:1]
    for k in range(1, len(taps)):
        y = y + taps[k] * w[k:k + 1]
    return y


def _rms(x, width):
    r = lax.rsqrt(jnp.sum(x * x, axis=-1, keepdims=True) * (1.0 / width) + EPS)
    return x * r, r


def _rms_bwd(xh, r, dxh, width):
    return r * (dxh - xh * (jnp.sum(dxh * xh, axis=-1, keepdims=True) * (1.0 / width)))


def _seq_flags(i, seq, tm):
    nps = seq // tm
    pos = i % nps
    return jnp.where(pos == 0, 0.0, 1.0), jnp.where(pos == nps - 1, 0.0, 1.0)


def _norm_cast(name, x, w, tm):
    t, d = x.shape

    def body(i, x_ref, w_ref, h_ref):
        xh, _ = _rms(x_ref[...], d)
        h_ref[...] = (xh * w_ref[...]).astype(BF16)

    return _rowwise(name, body, t, tm, [("row", x, d, 0), ("full", w)], [(d, BF16)])[0]


def _gdn_prep(proj, cw, gp, seq, tm):
    t = proj.shape[0]
    d = D_MODEL

    def body(i, q_ref, qh_ref, k_ref, kh_ref, v_ref, vh_ref, sm_ref, cw_ref, gp_ref, qn_ref, kn_ref, vv_ref, gs_ref, ypre_ref):
        keep, _ = _seq_flags(i, seq, tm)
        for x_ref, h_ref, o_ref, off, scale in ((q_ref, qh_ref, qn_ref, 0, GDN_DK ** -0.5),
                                               (k_ref, kh_ref, kn_ref, d, 1.0), (v_ref, vh_ref, vv_ref, 2 * d, None)):
            y = _conv(_conv_taps(x_ref[...], h_ref[...] * keep, 4), cw_ref[:, off:off + d])
            ypre_ref[:, off:off + d] = y
            a = y * _sigmoid(y)
            if scale is None:
                o_ref[...] = a
            else:
                for hh in range(GDN_HEADS):
                    s = a[:, hh * GDN_DK:(hh + 1) * GDN_DK]
                    n = lax.rsqrt(_rowsum(s * s) + EPS)
                    o_ref[:, hh * GDN_DK:(hh + 1) * GDN_DK] = s * (n * scale)
        sm = sm_ref[...]
        lane = lax.broadcasted_iota(jnp.int32, sm.shape, 1)
        beta = _sigmoid(sm)
        g = jnp.where((lane >= 8) & (lane < 16), -jnp.exp(gp_ref[0:1, :]) * _softplus(sm + gp_ref[1:2, :]), 0.0)
        gs_ref[...] = jnp.where(lane < 8, beta, _mmx(_block_tri(tm, False), g))

    ins = []
    for cb in range(3):
        ins += [("row", proj, d, cb), ("prev", proj, d, cb)]
    ins += [("row", proj, LANES, SMALL_CB), ("full", cw), ("full", gp)]
    return _rowwise("gdn_prep", body, t, tm, ins, [(d, F32), (d, F32), (d, F32), (LANES, F32), (3 * d, F32)])


def _block_tri(tm, upper):
    ri = lax.broadcasted_iota(jnp.int32, (tm, tm), 0)
    ci = lax.broadcasted_iota(jnp.int32, (tm, tm), 1)
    tri = (ri <= ci) if upper else (ri >= ci)
    return (tri & ((ri // CHUNK) == (ci // CHUNK))).astype(F32)


def _chunk_consts():
    row = lax.broadcasted_iota(jnp.int32, (CHUNK, CHUNK), 0)
    col = lax.broadcasted_iota(jnp.int32, (CHUNK, CHUNK), 1)
    return dict(
        tril=row >= col, strict=row > col, eye=(row == col).astype(F32),
        lane=lax.broadcasted_iota(jnp.int32, (CHUNK, LANES), 1),
        row1=lax.broadcasted_iota(jnp.int32, (CHUNK, 1), 0),
        ones=jnp.ones((CHUNK, LANES), F32))


def _hmap(fn, *lists):
    return [fn(*a) for a in zip(*lists)]


def _tri_inv(nmats, eye):
    x = [eye - n for n in nmats]
    p = _hmap(_mm3, nmats, nmats)
    for lvl in range(5):
        x = _hmap(lambda xi, pi: xi + _mm3(xi, pi), x, p)
        if lvl < 4:
            p = _hmap(_mm3, p, p)
    return x


def _gdn_gates(gs, gc_row, h, c):
    beta = _rowsum(jnp.where(c["lane"] == h, gs, 0.0))
    gc = _rowsum(jnp.where(c["lane"] == h + 8, gs, 0.0))
    dc = jnp.exp(jnp.where(c["tril"], gc - gc_row, -1e30))
    gl = gc[CHUNK - 1:CHUNK, :]
    return beta, dc, jnp.exp(gc), jnp.exp(gl), jnp.exp(gl - gc)


GDN_HB = GDN_HEADS


def _gdn_specs(seq, sb, hb, backward):
    assert hb == GDN_HEADS
    nsb = seq // sb
    ncb = sb // CHUNK
    order = (lambda j: nsb - 1 - j) if backward else (lambda j: j)
    specs = dict(
        wide=lambda: pl.BlockSpec((1, sb, hb * GDN_DK), lambda b, h, j: (b, order(j), h)),
        gs=pl.BlockSpec((1, sb, LANES), lambda b, h, j: (b, order(j), 0)),
        gr=pl.BlockSpec((1, ncb, GDN_HEADS, CHUNK), lambda b, h, j: (b, order(j), 0, 0)),
        st=pl.BlockSpec((1, hb, ncb * GDN_DK, GDN_DK), lambda b, h, j: (b, h, order(j), 0)),
        ti=pl.BlockSpec((1, hb, sb, CHUNK), lambda b, h, j: (b, h, order(j), 0)))
    return nsb, ncb, specs


def _riding_exchange(arrays, scatter, n_in, n_out, grid):
    n = len(arrays)
    if n == 0:
        return [], [], [], lambda body: body
    any_spec = [pl.BlockSpec(memory_space=pl.ANY)] * n

    def wrap(body):
        def wrapped(*refs):
            ins = refs[n_in:n_in + n]
            outs = refs[n_in + n + n_out:n_in + 2 * n + n_out]
            sems = refs[len(refs) - 3:]
            pid = [pl.program_id(a) for a in range(len(grid))]
            first = functools.reduce(lambda a, b: a & b, [p == 0 for p in pid])
            last = functools.reduce(lambda a, b: a & b, [p == g - 1 for p, g in zip(pid, grid)])

            @pl.when(first)
            def _():
                _exchange_phase(ins, outs, sems, scatter, start=True)

            body(*refs[:n_in], *refs[n_in + n:n_in + n + n_out], *refs[n_in + 2 * n + n_out:len(refs) - 3])

            @pl.when(last)
            def _():
                _exchange_phase(ins, outs, sems, scatter, start=False)

        return wrapped

    return any_spec, _exchange_out_shapes(arrays, scatter), _exchange_sems(n), wrap


def _gdn_chunk_fwd(qn, kn, vv, gs, gr, bsz, seq, sb, riders):
    hb = GDN_HB
    nsb, ncb, sp = _gdn_specs(seq, sb, hb, False)
    grid = (bsz, GDN_HEADS // hb, nsb)
    any_spec, rider_shapes, rider_sems, wrap = _riding_exchange(riders, False, 5, 3, grid)

    def body(q_ref, k_ref, v_ref, gs_ref, gr_ref, o_ref, st_ref, ti_ref, s_scr):
        hg = pl.program_id(1)

        @pl.when(pl.program_id(2) == 0)
        def _():
            s_scr[...] = jnp.zeros_like(s_scr)

        c = _chunk_consts()

        def chunk(n, carry):
            r = pl.ds(pl.multiple_of(n * CHUNK, CHUNK), CHUNK)
            rs = pl.ds(pl.multiple_of(n * GDN_DK, GDN_DK), GDN_DK)
            gsv = gs_ref[0, r, :]
            heads = list(range(hb))
            sls = [slice(ih * GDN_DK, (ih + 1) * GDN_DK) for ih in heads]
            q = [q_ref[0, r, sl] for sl in sls]
            k = [k_ref[0, r, sl] for sl in sls]
            v = [v_ref[0, r, sl] for sl in sls]
            beta, dc, eg, egl, ekd = zip(*[
                _gdn_gates(gsv, gr_ref[0, n, pl.ds(ih, 1), :], ih, c) for ih in heads])
            kb = _hmap(lambda a, b: a * b, k, beta)
            amat = _hmap(lambda a, b, d_: jnp.where(c["strict"], _mm(a, b, NT) * d_, 0.0), kb, k, dc)
            tinv = _tri_inv(amat, c["eye"])
            u = _hmap(lambda t_, a, b: _mm3(t_, a * b), tinv, v, beta)
            w = _hmap(lambda t_, a, b: _mm3(t_, a * b), tinv, kb, eg)
            qk = _hmap(lambda a, b, d_: _mm(a, b, NT) * d_, q, k, dc)
            s = [s_scr[ih] for ih in heads]
            v_new = _hmap(lambda a, b, s_: a - _mm(b, s_), u, w, s)
            o = _hmap(lambda a, e, s_, qk_, vn: _mm(a * e, s_) + _mm(qk_, vn), q, eg, s, qk, v_new)
            s_new = _hmap(lambda s_, e, a, f, vn: s_ * e + _mm(a * f, vn, TN), s, egl, k, ekd, v_new)
            for ih in heads:
                o_ref[0, r, sls[ih]] = o[ih]
                st_ref[0, ih, rs, :] = s[ih]
                ti_ref[0, ih, r, :] = tinv[ih]
                s_scr[ih] = s_new[ih]
            return carry

        lax.fori_loop(0, ncb, chunk, 0)

    t3 = (bsz, seq, D_MODEL)
    res = _pcall(
        wrap(body), name="gdn_chunk_fwd", grid=grid,
        in_specs=[sp["wide"](), sp["wide"](), sp["wide"](), sp["gs"], sp["gr"]] + any_spec,
        out_specs=[sp["wide"](), sp["st"], sp["ti"]] + any_spec,
        out_shape=[jax.ShapeDtypeStruct(t3, F32),
                   jax.ShapeDtypeStruct((bsz, GDN_HEADS, (seq // CHUNK) * GDN_DK, GDN_DK), F32),
                   jax.ShapeDtypeStruct((bsz, GDN_HEADS, seq, CHUNK), F32)] + rider_shapes,
        scratch_shapes=[pltpu.VMEM((hb, GDN_DK, GDN_DK), F32)] + rider_sems,
        compiler_params=_params(("arbitrary", "arbitrary", "arbitrary")),
    )(qn, kn, vv, gs, gr, *riders)
    return res[:3], res[3:]


def _ssd_prep(proj, cw, cb, sp, seq, tm):
    t = proj.shape[0]
    d = D_MODEL
    ssd_w = SSD_HEADS * SSD_P

    def body(i, x_ref, xh_ref, bc_ref, bch_ref, sm_ref, cw_ref, cb_ref, sp_ref, xs_ref, bco_ref, dtx_ref, acsx_ref, acs_ref, ypre_ref):
        keep, _ = _seq_flags(i, seq, tm)
        y = _conv(_conv_taps(x_ref[...], xh_ref[...] * keep, 4), cw_ref[:, 0:d]) + cb_ref[:, 0:d]
        ypre_ref[:, 0:d] = y
        xs_ref[...] = y * _sigmoid(y)
        y = _conv(_conv_taps(bc_ref[...], bch_ref[...] * keep, 4), cw_ref[:, d:d + 512]) + cb_ref[:, d:d + 512]
        ypre_ref[:, d:d + 512] = y
        bco_ref[...] = y * _sigmoid(y)
        sm = sm_ref[...]
        lane = lax.broadcasted_iota(jnp.int32, sm.shape, 1)
        valid = (lane >= 16) & (lane < 32)
        dt = jnp.where(valid, _softplus(sm + sp_ref[1:2, :]), 0.0)
        adt = dt * (-jnp.exp(sp_ref[0:1, :]))
        acs = _mmx(_block_tri(tm, False), adt)
        l64 = lax.broadcasted_iota(jnp.int32, (LANES, ssd_w), 0)
        d64 = lax.broadcasted_iota(jnp.int32, (LANES, ssd_w), 1)
        e64 = (l64 - 16 == d64 // SSD_P).astype(F32)
        dtx_ref[...] = _mmsel(dt, e64, terms=3)
        acsx_ref[...] = _mmsel(acs, e64, terms=3)
        acs_ref[...] = acs

    ins = [("row", proj, d, 5), ("prev", proj, d, 5), ("row", proj, 512, 12), ("prev", proj, 512, 12),
           ("row", proj, LANES, SMALL_CB), ("full", cw), ("full", cb), ("full", sp)]
    return _rowwise("ssd_prep", body, t, tm, ins,
                    [(d, F32), (512, F32), (ssd_w, F32), (ssd_w, F32), (LANES, F32), (d + 512, F32)])


SSD_GW = SSD_HPG * SSD_P


def _ssd_head(acs, ar_ref, n, head, cbm, c):
    col = _rowsum(jnp.where(c["lane"] == head + 16, acs, 0.0))
    lm = jnp.exp(jnp.where(c["tril"], col - ar_ref[0, head, pl.ds(n, 1), :], -1e30))
    return lm, cbm * lm


def _ssd_specs(seq, sb):
    nsb = seq // sb
    ncb = sb // CHUNK
    def specs(order):
        return dict(
            wide=lambda: pl.BlockSpec((1, sb, SSD_HEADS * SSD_P), lambda b, j: (b, order(j), 0)),
            bc=lambda: pl.BlockSpec((1, sb, 2 * SSD_GROUPS * SSD_N), lambda b, j: (b, order(j), 0)),
            half=lambda: pl.BlockSpec((1, sb, SSD_GROUPS * SSD_N), lambda b, j: (b, order(j), 0)),
            small=lambda: pl.BlockSpec((1, sb, LANES), lambda b, j: (b, order(j), 0)),
            ar=pl.BlockSpec((1, SSD_HEADS, ncb, CHUNK), lambda b, j: (b, 0, order(j), 0)),
            st=pl.BlockSpec((1, ncb * SSD_N, SSD_HEADS * SSD_P), lambda b, j: (b, order(j), 0)))
    return nsb, ncb, specs(lambda j: j), specs(lambda j: nsb - 1 - j)


def _ssd_chunk_fwd(xs, bc, dtx, acsx, acs, ar, bsz, seq, sb):
    nsb, ncb, sp, _ = _ssd_specs(seq, sb)

    def body(x_ref, dtx_ref, ax_ref, bc_ref, acs_ref, ar_ref, y_ref, sts_ref, st_scr):
        @pl.when(pl.program_id(1) == 0)
        def _():
            st_scr[...] = jnp.zeros_like(st_scr)

        c = _chunk_consts()
        lane5 = lax.broadcasted_iota(jnp.int32, (CHUNK, SSD_GW), 1) // SSD_P

        def chunk(n, carry):
            r = pl.ds(pl.multiple_of(n * CHUNK, CHUNK), CHUNK)
            rs = pl.ds(pl.multiple_of(n * SSD_N, SSD_N), SSD_N)
            acsv = acs_ref[0, r, :]
            for g in range(SSD_GROUPS):
                gl = slice(g * SSD_GW, (g + 1) * SSD_GW)
                x, dt, ax = x_ref[0, r, gl], dtx_ref[0, r, gl], ax_ref[0, r, gl]
                bm = bc_ref[0, r, g * SSD_N:(g + 1) * SSD_N]
                cm = bc_ref[0, r, (SSD_GROUPS + g) * SSD_N:(SSD_GROUPS + g + 1) * SSD_N]
                xdt = x * dt
                cbm = _mm(cm, bm, NT)
                al = ax[CHUNK - 1:CHUNK, :]
                st = st_scr[:, gl]
                y = _mm(cm, st) * jnp.exp(ax)
                for hh in range(SSD_HPG):
                    _, gm = _ssd_head(acsv, ar_ref, n, g * SSD_HPG + hh, cbm, c)
                    y = y + _mm(gm, jnp.where(lane5 == hh, xdt, 0.0))
                y_ref[0, r, gl] = y
                sts_ref[0, rs, gl] = st
                st_scr[:, gl] = st * jnp.exp(al) + _mm(bm, xdt * jnp.exp(al - ax), TN)
            return carry

        lax.fori_loop(0, ncb, chunk, 0)

    return _pcall(
        body, name="ssd_chunk_fwd", grid=(bsz, nsb),
        in_specs=[sp["wide"](), sp["wide"](), sp["wide"](), sp["bc"](), sp["small"](), sp["ar"]],
        out_specs=[sp["wide"](), sp["st"]],
        out_shape=[jax.ShapeDtypeStruct((bsz, seq, SSD_HEADS * SSD_P), F32),
                   jax.ShapeDtypeStruct((bsz, (seq // CHUNK) * SSD_N, SSD_HEADS * SSD_P), F32)],
        scratch_shapes=[pltpu.VMEM((SSD_N, SSD_HEADS * SSD_P), F32)],
        compiler_params=_params(("parallel", "arbitrary")),
    )(xs, dtx, acsx, bc, acs, ar)


def _gate_norm(o_gdn, y_ssd, xs, proj, gnw, snw, dvec, tm):
    t = o_gdn.shape[0]
    d = D_MODEL

    def body(i, o_ref, za_ref, y_ref, xs_ref, zs_ref, gnw_ref, snw_ref, dv_ref, out_ref):
        for hh in range(GDN_HEADS):
            sl = slice(hh * GDN_DK, (hh + 1) * GDN_DK)
            oh, _ = _rms(o_ref[:, sl], GDN_DK)
            z = za_ref[:, sl]
            out_ref[:, sl] = (oh * gnw_ref[...] * (z * _sigmoid(z))).astype(BF16)
        zs = zs_ref[...]
        yg = (y_ref[...] + dv_ref[...] * xs_ref[...]) * (zs * _sigmoid(zs))
        for g in range(SSD_GROUPS):
            sl = slice(g * 512, (g + 1) * 512)
            yh, _ = _rms(yg[:, sl], 512)
            out_ref[:, d + g * 512:d + (g + 1) * 512] = (yh * snw_ref[:, sl]).astype(BF16)

    ins = [("row", o_gdn, d, 0), ("row", proj, d, 3), ("row", y_ssd, d, 0), ("row", xs, d, 0), ("row", proj, d, 4),
           ("full", gnw), ("full", snw), ("full", dvec)]
    return _rowwise("gate_norm", body, t, tm, ins, [(2 * d, BF16)])[0]


def _out_mid(mixin, w_out, x, pmw, pfw):
    d = D_MODEL

    def epilogue(mix, x_ref, pmw_ref, pfw_ref, mix_ref, x1_ref, h2_ref):
        mix_ref[...] = mix
        mh, _ = _rms(mix, d)
        x1 = x_ref[...] + mh * pmw_ref[...]
        x1_ref[...] = x1
        xh, _ = _rms(x1, d)
        h2_ref[...] = (xh * pfw_ref[...]).astype(BF16)

    return _matmul_rows("mm_out_mid", mixin, w_out, "nn", epilogue, [x], [pmw, pfw], [(d, F32), (d, F32), (d, BF16)])


def _ffn_act(u_pre, cw, cb, seq, tm):
    t = u_pre.shape[0]

    def body(i, ug_ref, ugh_ref, uu_ref, uuh_ref, cw_ref, cb_ref, act_ref, u_ref):
        keep, _ = _seq_flags(i, seq, tm)
        gate = _conv(_conv_taps(ug_ref[...], ugh_ref[...] * keep, 3), cw_ref[:, 0:D_FF]) + cb_ref[:, 0:D_FF]
        up = _conv(_conv_taps(uu_ref[...], uuh_ref[...] * keep, 3), cw_ref[:, D_FF:2 * D_FF]) + cb_ref[:, D_FF:2 * D_FF]
        u_ref[:, 0:D_FF] = gate
        u_ref[:, D_FF:2 * D_FF] = up
        act_ref[...] = (gate * _sigmoid(gate) * up).astype(BF16)

    ins = [("row", u_pre, D_FF, 0), ("prev", u_pre, D_FF, 0), ("row", u_pre, D_FF, 1), ("prev", u_pre, D_FF, 1),
           ("full", cw), ("full", cb)]
    return _rowwise("ffn_act", body, t, tm, ins, [(D_FF, BF16), (2 * D_FF, F32)])


def _down_final(act, w_down, x1, tgt, w):
    d = D_MODEL

    def epilogue(f, x1_ref, t_ref, w_ref, dy_ref, df_ref, loss_ref, dw_ref):
        fh, r = _rms(f, d)
        e = x1_ref[...] + fh * w_ref[...] - t_ref[...]
        loss_ref[...] += _colsum(e * e) * (0.5 / d)
        dy = e * (1.0 / d)
        dy_ref[...] = dy
        dw_ref[...] += _colsum(dy * fh)
        df_ref[...] = _rms_bwd(fh, r, dy * w_ref[...], d).astype(BF16)

    return _matmul_rows("mm_down_final", act, w_down, "nn", epilogue, [x1, tgt], [w], [(d, F32), (d, BF16)],
                        accs=[(1, d), (1, d)], tk=D_FF // 2)


def _ffn_bwd(u, u_pre, dact, cw, seq, tm):
    t = u.shape[0]

    def body(i, g_ref, gn_ref, up_ref, upn_ref, xg_ref, xu_ref, da_ref, dan_ref, cw_ref, dpre_ref, dcw_ref, dcb_ref):
        _, keep_next = _seq_flags(i, seq, tm)
        ext = lambda a_ref, n_ref: jnp.concatenate([a_ref[...], n_ref[...]], axis=0)
        rows = tm + SUBLANES
        gate, up = ext(g_ref, gn_ref), ext(up_ref, upn_ref)
        sg = _sigmoid(gate)
        da = jnp.concatenate([da_ref[...], dan_ref[...] * keep_next], axis=0)
        for off, grad, x_ref in ((0, da * up * _dsilu(gate, sg), xg_ref), (D_FF, da * gate * sg, xu_ref)):
            x = x_ref[...]
            own = grad[0:tm]
            acc = own * cw_ref[2:3, off:off + D_FF]
            dcb_ref[:, off:off + D_FF] += _colsum(own)
            dcw_ref[2:3, off:off + D_FF] += _colsum(own * x)
            for j in (1, 2):
                ahead = pltpu.roll(grad, rows - j, 0)[0:tm]
                acc = acc + ahead * cw_ref[2 - j:3 - j, off:off + D_FF]
                dcw_ref[2 - j:3 - j, off:off + D_FF] += _colsum(ahead * x)
            dpre_ref[:, off:off + D_FF] = acc.astype(BF16)

    ins = []
    for cb_ in range(2):
        ins += [("row", u, D_FF, cb_), ("next", u, D_FF, cb_)]
    ins += [("row", u_pre, D_FF, 0), ("row", u_pre, D_FF, 1), ("row", dact, D_FF, 0), ("next", dact, D_FF, 0), ("full", cw)]
    return _rowwise("ffn_bwd", body, t, tm, ins, [(2 * D_FF, BF16)], accs=[(SUBLANES, 2 * D_FF), (1, 2 * D_FF)])


def _assemble_dproj(dpre_gdn, dza, dzs, dpre_ssd, dsm, proj, gcw, scw, seq, tm):
    t = dpre_gdn.shape[0]
    d = D_MODEL

    def body(i, dg_ref, dgn_ref, dza_ref, dzs_ref, ds_ref, dsn_ref, dsm_ref, xq_ref, xk_ref, xv_ref, xx_ref, xbc_ref,
             gcw_ref, scw_ref, o_ref, dgcw_ref, dscw_ref):
        _, keep = _seq_flags(i, seq, tm)
        pieces = [(dg_ref, dgn_ref, gcw_ref, dgcw_ref, x_ref, 0, c0)
                  for x_ref, c0 in ((xq_ref, 0), (xk_ref, d), (xv_ref, 2 * d))]
        pieces += [(ds_ref, dsn_ref, scw_ref, dscw_ref, x_ref, 5 * d, c0) for x_ref, c0 in ((xx_ref, 0), (xbc_ref, d))]
        for d_ref, n_ref, cw_ref, dcw_ref, x_ref, base, c0 in pieces:
            w = x_ref.shape[1]
            x = x_ref[...]
            g = d_ref[:, c0:c0 + w]
            halo = n_ref[:, c0:c0 + w] * keep
            acc = g * cw_ref[3:4, c0:c0 + w]
            dcw_ref[3:4, c0:c0 + w] += _colsum(g * x)
            for j in range(1, 4):
                ahead = _shift_up(g, halo, j)
                acc = acc + ahead * cw_ref[3 - j:4 - j, c0:c0 + w]
                dcw_ref[3 - j:4 - j, c0:c0 + w] += _colsum(ahead * x)
            o_ref[:, base + c0:base + c0 + w] = acc.astype(BF16)
        o_ref[:, 3 * d:4 * d] = dza_ref[...]
        o_ref[:, 4 * d:5 * d] = dzs_ref[...]
        o_ref[:, 6 * d + 512:6 * d + 512 + LANES] = dsm_ref[...]
        o_ref[:, 6 * d + 512 + LANES:PROJ_W] = jnp.zeros((tm, PROJ_W - (6 * d + 512 + LANES)), BF16)

    ins = [("row", dpre_gdn, 3 * d, 0), ("next", dpre_gdn, 3 * d, 0), ("row", dza, d, 0), ("row", dzs, d, 0),
           ("row", dpre_ssd, d + 512, 0), ("next", dpre_ssd, d + 512, 0), ("row", dsm, LANES, 0),
           ("row", proj, d, 0), ("row", proj, d, 1), ("row", proj, d, 2), ("row", proj, d, 5), ("row", proj, 512, 12),
           ("full", gcw), ("full", scw)]
    return _rowwise("assemble_dproj", body, t, tm, ins, [(PROJ_W, BF16)], accs=[(SUBLANES, 3 * d), (SUBLANES, d + 512)])


def _dh2_mid_bwd(du_pre, w_up, x1, mix, dy, pmw, pfw):
    d = D_MODEL

    def epilogue(dh2, x1_ref, mix_ref, dy_ref, pmw_ref, pfw_ref, dx1_ref, dmix_ref, dpm_ref, dpf_ref):
        xh, r2 = _rms(x1_ref[...], d)
        dpf_ref[...] += _colsum(dh2 * xh)
        dx1 = dy_ref[...] + _rms_bwd(xh, r2, dh2 * pfw_ref[...], d)
        dx1_ref[...] = dx1
        mh, r = _rms(mix_ref[...], d)
        dpm_ref[...] += _colsum(dx1 * mh)
        dmix_ref[...] = _rms_bwd(mh, r, dx1 * pmw_ref[...], d).astype(BF16)

    return _matmul_rows("mm_dh2_mid_bwd", du_pre, w_up, "nt", epilogue, [x1, mix, dy], [pmw, pfw],
                        [(d, F32), (d, BF16)], accs=[(1, d), (1, d)], tk=512)


def _gate_norm_bwd(o_gdn, y_ssd, xs, proj, dmixin, gnw, snw, dvec, tm):
    t = o_gdn.shape[0]
    d = D_MODEL

    def body(i, o_ref, za_ref, y_ref, xs_ref, zs_ref, dma_ref, dms_ref, gnw_ref, snw_ref, dv_ref,
             do_ref, dza_ref, dy_ref, dxs_ref, dzs_ref, dgnw_ref, dsnw_ref, dd_ref):
        for hh in range(GDN_HEADS):
            sl = slice(hh * GDN_DK, (hh + 1) * GDN_DK)
            oh, r = _rms(o_ref[:, sl], GDN_DK)
            z = za_ref[:, sl]
            sz = _sigmoid(z)
            dm = dma_ref[:, sl]
            don = dm * (z * sz)
            dza_ref[:, sl] = (dm * oh * gnw_ref[...] * _dsilu(z, sz)).astype(BF16)
            dgnw_ref[...] += _colsum(don * oh)
            do_ref[:, sl] = _rms_bwd(oh, r, don * gnw_ref[...], GDN_DK)
        zs = zs_ref[...]
        sz = _sigmoid(zs)
        sil = zs * sz
        x = xs_ref[...]
        y0 = y_ref[...] + dv_ref[...] * x
        yg = y0 * sil
        dms = dms_ref[...]
        for g in range(SSD_GROUPS):
            sl = slice(g * 512, (g + 1) * 512)
            yh, r = _rms(yg[:, sl], 512)
            dsnw_ref[:, sl] += _colsum(dms[:, sl] * yh)
            dyg = _rms_bwd(yh, r, dms[:, sl] * snw_ref[:, sl], 512)
            dy0 = dyg * sil[:, sl]
            dzs_ref[:, sl] = (dyg * y0[:, sl] * _dsilu(zs[:, sl], sz[:, sl])).astype(BF16)
            dy_ref[:, sl] = dy0
            dxs_ref[:, sl] = dy0 * dv_ref[:, sl]
            dd_ref[:, sl] += _colsum(dy0 * x[:, sl])

    ins = [("row", o_gdn, d, 0), ("row", proj, d, 3), ("row", y_ssd, d, 0), ("row", xs, d, 0), ("row", proj, d, 4),
           ("row", dmixin, d, 0), ("row", dmixin, d, 1), ("full", gnw), ("full", snw), ("full", dvec)]
    return _rowwise("gate_norm_bwd", body, t, tm, ins, [(d, F32), (d, BF16), (d, F32), (d, F32), (d, BF16)],
                    accs=[(1, GDN_DK), (1, d), (1, d)])


def _ssd_chunk_bwd(xs, bc, dtx, acsx, acs, ar, dy, sts, bsz, seq, sb):
    nsb, ncb, _, sp = _ssd_specs(seq, sb)

    def body(x_ref, dtx_ref, ax_ref, bc_ref, acs_ref, ar_ref, dy_ref, sts_ref, dx_ref, dbc_ref, ddt_ref, dacs_ref, dst_scr):
        @pl.when(pl.program_id(1) == 0)
        def _():
            dst_scr[...] = jnp.zeros_like(dst_scr)

        c = _chunk_consts()
        lane5 = lax.broadcasted_iota(jnp.int32, (CHUNK, SSD_GW), 1) // SSD_P
        row5 = lax.broadcasted_iota(jnp.int32, (CHUNK, SSD_GW), 0)
        sel_in = lax.broadcasted_iota(jnp.int32, (SSD_GW, LANES), 0) // SSD_P
        sel_out = lax.broadcasted_iota(jnp.int32, (SSD_GW, LANES), 1)

        def chunk(nn, carry):
            n = ncb - 1 - nn
            r = pl.ds(pl.multiple_of(n * CHUNK, CHUNK), CHUNK)
            rs = pl.ds(pl.multiple_of(n * SSD_N, SSD_N), SSD_N)
            acsv = acs_ref[0, r, :]
            ddt = jnp.zeros((CHUNK, LANES), F32)
            dacs = jnp.zeros((CHUNK, LANES), F32)
            for g in range(SSD_GROUPS):
                gl = slice(g * SSD_GW, (g + 1) * SSD_GW)
                x, dt, ax, dyv = x_ref[0, r, gl], dtx_ref[0, r, gl], ax_ref[0, r, gl], dy_ref[0, r, gl]
                bm = bc_ref[0, r, g * SSD_N:(g + 1) * SSD_N]
                cm = bc_ref[0, r, (SSD_GROUPS + g) * SSD_N:(SSD_GROUPS + g + 1) * SSD_N]
                st = sts_ref[0, rs, gl]
                dst = dst_scr[:, gl]
                rsel = (sel_in + (16 + g * SSD_HPG) == sel_out).astype(F32)
                xdt = x * dt
                cbm = _mm(cm, bm, NT)
                al = ax[CHUNK - 1:CHUNK, :]
                ex, el = jnp.exp(ax), jnp.exp(al)
                dec = jnp.exp(al - ax)
                xd = xdt * dec
                dye = dyv * ex
                dxd = _mm(bm, dst)
                dxdt = dec * dxd
                dcm = _mm(dye, st, NT)
                dbm = _mm(xd, dst, NT)
                z = dye * _mm(cm, st) - dxd * xd
                zl = _colsum(dst * st) * el + _colsum(dxd * xd)
                z = z + jnp.where(row5 == CHUNK - 1, zl, 0.0)
                dcb = jnp.zeros((CHUNK, CHUNK), F32)
                for hh in range(SSD_HPG):
                    head = g * SSD_HPG + hh
                    lm, gm = _ssd_head(acsv, ar_ref, n, head, cbm, c)
                    dym = jnp.where(lane5 == hh, dyv, 0.0)
                    dxdt = dxdt + _mm(gm, dym, TN)
                    dg = _mm(dym, xdt, NT)
                    dcb = dcb + dg * lm
                    pm = dg * gm
                    dacs = dacs + jnp.where(c["lane"] == head + 16, _rowsum(pm) - _mmsel(pm, c["ones"], TN), 0.0)
                dbc_ref[0, r, (SSD_GROUPS + g) * SSD_N:(SSD_GROUPS + g + 1) * SSD_N] = dcm + _mm(dcb, bm)
                dbc_ref[0, r, g * SSD_N:(g + 1) * SSD_N] = dbm + _mm(dcb, cm, TN)
                dacs = dacs + _mmsel(z, rsel)
                ddt = ddt + _mmsel(dxdt * x, rsel)
                dx_ref[0, r, gl] = dxdt * dt
                dst_scr[:, gl] = dst * el + _mm(cm, dye, TN)
            ddt_ref[0, r, :] = ddt
            dacs_ref[0, r, :] = dacs
            return carry

        lax.fori_loop(0, ncb, chunk, 0)

    return _pcall(
        body, name="ssd_chunk_bwd", grid=(bsz, nsb),
        in_specs=[sp["wide"](), sp["wide"](), sp["wide"](), sp["bc"](), sp["small"](), sp["ar"], sp["wide"](), sp["st"]],
        out_specs=[sp["wide"](), sp["bc"](), sp["small"](), sp["small"]()],
        out_shape=[jax.ShapeDtypeStruct((bsz, seq, SSD_HEADS * SSD_P), F32),
                   jax.ShapeDtypeStruct((bsz, seq, 2 * SSD_GROUPS * SSD_N), F32),
                   jax.ShapeDtypeStruct((bsz, seq, LANES), F32), jax.ShapeDtypeStruct((bsz, seq, LANES), F32)],
        scratch_shapes=[pltpu.VMEM((SSD_N, SSD_HEADS * SSD_P), F32)],
        compiler_params=_params(("parallel", "arbitrary")),
    )(xs, dtx, acsx, bc, acs, ar, dy, sts)


def _ssd_prep_bwd(ypre, proj, dxs_c, dxs_d, dbc, ddt, dacs, dsm_gdn, sp, tm):
    t = proj.shape[0]
    d = D_MODEL

    def body(i, y_ref, sm_ref, dxc_ref, dxd_ref, dbc_ref, ddt_ref, dacs_ref, dsg_ref, sp_ref,
             dpre_ref, dsm_ref, dcb_ref, dsp_ref):
        for off, w, grad in ((0, d, dxc_ref[...] + dxd_ref[...]), (d, 512, dbc_ref[...])):
            y = y_ref[:, off:off + w]
            dpre = grad * _dsilu(y, _sigmoid(y))
            dpre_ref[:, off:off + w] = dpre
            dcb_ref[:, off:off + w] += _colsum(dpre)
        sm = sm_ref[...]
        lane = lax.broadcasted_iota(jnp.int32, sm.shape, 1)
        valid = (lane >= 16) & (lane < 32)
        xb = sm + sp_ref[1:2, :]
        dt = jnp.where(valid, _softplus(xb), 0.0)
        a_neg = -jnp.exp(sp_ref[0:1, :])
        dadt_s = _mmx(_block_tri(tm, True), dacs_ref[...])
        dxb = jnp.where(valid, (ddt_ref[...] + dadt_s * a_neg) * _sigmoid(xb), 0.0)
        dsm_ref[...] = (dsg_ref[...] + dxb).astype(BF16)
        dsp_ref[1:2, :] += _colsum(dxb)
        dsp_ref[0:1, :] += jnp.where(valid[0:1, :], _colsum(dadt_s * dt) * a_neg, 0.0)

    ins = [("row", ypre, d + 512, 0), ("row", proj, LANES, SMALL_CB), ("row", dxs_c, d, 0), ("row", dxs_d, d, 0),
           ("row", dbc, 512, 0), ("row", ddt, LANES, 0), ("row", dacs, LANES, 0), ("row", dsm_gdn, LANES, 0), ("full", sp)]
    return _rowwise("ssd_prep_bwd", body, t, tm, ins, [(d + 512, F32), (LANES, BF16)],
                    accs=[(1, d + 512), (SUBLANES, LANES)])


def _gdn_chunk_bwd(qn, kn, vv, gs, gr, do, sts, tis, bsz, seq, sb, riders):
    hb = GDN_HB
    nsb, ncb, sp = _gdn_specs(seq, sb, hb, True)
    grid = (bsz, GDN_HEADS // hb, nsb)
    any_spec, rider_shapes, rider_sems, wrap = _riding_exchange(riders, True, 8, 4, grid)

    def body(q_ref, k_ref, v_ref, gs_ref, gr_ref, do_ref, st_ref, ti_ref, dq_ref, dk_ref, dv_ref, dgb_ref, ds_scr):
        hg = pl.program_id(1)

        @pl.when(pl.program_id(2) == 0)
        def _():
            ds_scr[...] = jnp.zeros_like(ds_scr)

        c = _chunk_consts()

        def chunk(nn, carry):
            n = ncb - 1 - nn
            r = pl.ds(pl.multiple_of(n * CHUNK, CHUNK), CHUNK)
            rs = pl.ds(pl.multiple_of(n * GDN_DK, GDN_DK), GDN_DK)
            gsv = gs_ref[0, r, :]
            heads = list(range(hb))
            sls = [slice(ih * GDN_DK, (ih + 1) * GDN_DK) for ih in heads]
            q = [q_ref[0, r, sl] for sl in sls]
            k = [k_ref[0, r, sl] for sl in sls]
            v = [v_ref[0, r, sl] for sl in sls]
            do_ = [do_ref[0, r, sl] for sl in sls]
            s = [st_ref[0, ih, rs, :] for ih in heads]
            tinv = [ti_ref[0, ih, r, :] for ih in heads]
            dsn = [ds_scr[ih] for ih in heads]
            beta, dc, eg, egl, ekd = zip(*[
                _gdn_gates(gsv, gr_ref[0, n, pl.ds(ih, 1), :], ih, c) for ih in heads])
            mul = lambda a, b: a * b
            kb = _hmap(mul, k, beta)
            rhs_w = _hmap(mul, kb, eg)
            u = _hmap(lambda t_, a, b: _mm3(t_, a * b), tinv, v, beta)
            w = _hmap(_mm3, tinv, rhs_w)
            amat = _hmap(lambda a, b, d_: jnp.where(c["strict"], _mm(a, b, NT) * d_, 0.0), kb, k, dc)
            qk = _hmap(lambda a, b, d_: _mm(a, b, NT) * d_, q, k, dc)
            qd = _hmap(mul, q, eg)
            kd = _hmap(mul, k, ekd)
            v_new = _hmap(lambda a, b, s_: a - _mm(b, s_), u, w, s)
            dv_new = _hmap(lambda qk_, d_, kd_, dn: _mm(qk_, d_, TN) + _mm(kd_, dn), qk, do_, kd, dsn)
            dqk = _hmap(lambda d_, vn: _mm(d_, vn, NT), do_, v_new)
            dqd = _hmap(lambda d_, s_: _mm(d_, s_, NT), do_, s)
            ds_new = _hmap(lambda qd_, d_, dn, e, w_, dvn: _mm(qd_, d_, TN) + dn * e - _mm(w_, dvn, TN),
                           qd, do_, dsn, egl, w, dv_new)
            dkd = _hmap(lambda vn, dn: _mm(vn, dn, NT), v_new, dsn)
            dgl = _hmap(lambda s_, dn, e: _colsum(_rowsum(s_ * dn)) * e, s, dsn, egl)
            dw = _hmap(lambda dvn, s_: -_mm(dvn, s_, NT), dv_new, s)
            dru = _hmap(lambda t_, a: _mm3(t_, a, TN), tinv, dv_new)
            drw = _hmap(lambda t_, a: _mm3(t_, a, TN), tinv, dw)
            da = _hmap(lambda a, u_, b, w_: jnp.where(c["strict"], -(_mm(a, u_, NT) + _mm(b, w_, NT)), 0.0), dru, u, drw, w)
            m = _hmap(mul, da, dc)
            dkb = _hmap(lambda a, e, m_, k_: a * e + _mm(m_, k_), drw, eg, m, k)
            mq = _hmap(mul, dqk, dc)
            dq = _hmap(lambda mq_, k_, a, e: _mm(mq_, k_) + a * e, mq, k, dqd, eg)
            dk = _hmap(lambda m_, kb_, mq_, q_, a, e, b, be: _mm(m_, kb_, TN) + _mm(mq_, q_, TN) + a * e + b * be,
                       m, kb, mq, q, dkd, ekd, dkb, beta)
            dbeta = _hmap(lambda a, v_, b, k_: _rowsum(a * v_) + _rowsum(b * k_), dru, v, dkb, k)
            pq = _hmap(lambda a, am, b, qk_: a * am + b * qk_, da, amat, dqk, qk)
            ekk = _hmap(lambda a, b: _rowsum(a * b), dkd, kd)
            dgc = _hmap(lambda pq_, a, rw, b, qd_, e, gl_: (
                _rowsum(pq_) - _mmsel(pq_, c["ones"], TN) + (_rowsum(a * rw) + _rowsum(b * qd_) - e)
                + jnp.where(c["row1"] == CHUNK - 1, _colsum(e) + gl_, 0.0)), pq, drw, rhs_w, dqd, qd, ekk, dgl)
            for ih in heads:
                ds_scr[ih] = ds_new[ih]
                dv_ref[0, r, sls[ih]] = dru[ih] * beta[ih]
                dq_ref[0, r, sls[ih]] = dq[ih]
                dk_ref[0, r, sls[ih]] = dk[ih]
                dgb_ref[0, r, sls[ih]] = jnp.where(c["lane"] == 0, dbeta[ih], jnp.where(c["lane"] == 1, dgc[ih], 0.0))
            return carry

        lax.fori_loop(0, ncb, chunk, 0)

    res = _pcall(
        wrap(body), name="gdn_chunk_bwd", grid=grid,
        in_specs=[sp["wide"](), sp["wide"](), sp["wide"](), sp["gs"], sp["gr"], sp["wide"](), sp["st"], sp["ti"]] + any_spec,
        out_specs=[sp["wide"](), sp["wide"](), sp["wide"](), sp["wide"]()] + any_spec,
        out_shape=[jax.ShapeDtypeStruct((bsz, seq, D_MODEL), F32)] * 4 + rider_shapes,
        scratch_shapes=[pltpu.VMEM((hb, GDN_DK, GDN_DK), F32)] + rider_sems,
        compiler_params=_params(("arbitrary", "arbitrary", "arbitrary")),
    )(qn, kn, vv, gs, gr, do, sts, tis, *riders)
    return res[:4], res[4:]


def _gdn_prep_bwd(ypre, proj, dqn, dkn, dvv, dgb, gp, tm):
    t = proj.shape[0]
    d = D_MODEL

    def body(i, y_ref, sm_ref, dq_ref, dk_ref, dv_ref, dgb_ref, gp_ref, dpre_ref, dsm_ref, dgp_ref):
        for g_ref, off, scale in ((dq_ref, 0, GDN_DK ** -0.5), (dk_ref, d, 1.0), (dv_ref, 2 * d, None)):
            y = y_ref[:, off:off + d]
            sy = _sigmoid(y)
            ds_ = _dsilu(y, sy)
            if scale is None:
                dpre_ref[:, off:off + d] = g_ref[...] * ds_
            else:
                a = y * sy
                for hh in range(GDN_HEADS):
                    sl = slice(hh * GDN_DK, (hh + 1) * GDN_DK)
                    s = a[:, sl]
                    n = lax.rsqrt(_rowsum(s * s) + EPS)
                    ah = s * n
                    gq = g_ref[:, sl]
                    dpre_ref[:, off + hh * GDN_DK:off + (hh + 1) * GDN_DK] = (
                        (scale * n) * (gq - ah * _rowsum(gq * ah)) * ds_[:, sl])
        sm = sm_ref[...]
        lane = lax.broadcasted_iota(jnp.int32, sm.shape, 1)
        si = lax.broadcasted_iota(jnp.int32, (d, LANES), 0)
        so = lax.broadcasted_iota(jnp.int32, (d, LANES), 1)
        sel = (((si % GDN_DK == 0) & (so == si // GDN_DK)) | ((si % GDN_DK == 1) & (so == si // GDN_DK + 8))).astype(F32)
        dsel = _mmx(dgb_ref[...], sel)
        is_g = (lane >= 8) & (lane < 16)
        dsel = jnp.where(is_g, _mmx(_block_tri(tm, True), dsel), dsel)
        beta = _sigmoid(sm)
        xb = sm + gp_ref[1:2, :]
        a_neg = -jnp.exp(gp_ref[0:1, :])
        sp = _softplus(xb)
        dxb = jnp.where(is_g, dsel * a_neg * _sigmoid(xb), 0.0)
        dsm_ref[...] = jnp.where(lane < 8, dsel * beta * (1.0 - beta), dxb)
        dgp_ref[1:2, :] += _colsum(dxb)
        dgp_ref[0:1, :] += _colsum(jnp.where(is_g, dsel * a_neg * sp, 0.0))

    ins = [("row", ypre, 3 * d, 0), ("row", proj, LANES, SMALL_CB), ("row", dqn, d, 0), ("row", dkn, d, 0),
           ("row", dvv, d, 0), ("row", dgb, d, 0), ("full", gp)]
    return _rowwise("gdn_prep_bwd", body, t, tm, ins, [(3 * d, F32), (LANES, F32)], accs=[(SUBLANES, LANES)])


def _dh1_first_bwd(dproj, wp_in, x, dx1, w, scatter_riders):
    d = D_MODEL

    def epilogue(dh, x_ref, dx1_ref, w_ref, dx_ref, dw_ref):
        xh, r = _rms(x_ref[...], d)
        dw_ref[...] += _colsum(dh * xh)
        dx_ref[...] = dx1_ref[...] + _rms_bwd(xh, r, dh * w_ref[...], d)

    return _matmul_rows("mm_dh1_first_bwd", dproj, wp_in, "nt", epilogue, [x, dx1], [w], [(d, F32)], accs=[(1, d)],
                        scatter_riders=scatter_riders)


def _gather_two_level(name, arrays):
    n = len(arrays)
    n_sem = 7

    def body(*refs):
        ins, outs = refs[:n], refs[n:2 * n]
        send_sems, recv_sems, loc_sems = refs[2 * n:]
        x, y, c = lax.axis_index("x"), lax.axis_index("y"), lax.axis_index("c")
        slot = lambda px, py, pc: 4 * px + 2 * py + pc
        sibling = (x, y, 1 - c)
        chips = [(1 - x, y), (x, 1 - y), (1 - x, 1 - y)]

        def copy(t, k, src, block, to):
            return pltpu.make_async_remote_copy(
                src_ref=src, dst_ref=outs[t].at[block], send_sem=send_sems.at[t, k], recv_sem=recv_sems.at[t, k],
                device_id=to, device_id_type=pl.DeviceIdType.MESH)

        own, first, passed = [], [], []
        for t in range(n):
            own.append(pltpu.make_async_copy(ins[t], outs[t].at[slot(x, y, c)], loc_sems.at[t]))
            first.append(copy(t, 0, ins[t], slot(x, y, c), sibling))
            first += [copy(t, 1 + j, ins[t], slot(x, y, c), (px, py, c)) for j, (px, py) in enumerate(chips)]
        for cp in own + first:
            cp.start()
        for t in range(n):
            for j, (px, py) in enumerate(chips):
                copy(t, 1 + j, ins[t], slot(px, py, c), (px, py, c)).wait_recv()
                fwd = copy(t, 4 + j, outs[t].at[slot(px, py, c)], slot(px, py, c), sibling)
                fwd.start()
                passed.append(fwd)
        for t in range(n):
            copy(t, 0, ins[t], slot(x, y, 1 - c), sibling).wait_recv()
            for j, (px, py) in enumerate(chips):
                copy(t, 4 + j, ins[t], slot(px, py, 1 - c), sibling).wait_recv()
        for cp in first + passed:
            cp.wait_send()
        for cp in own:
            cp.wait()

    return _pcall(
        body, name=name,
        in_specs=[pl.BlockSpec(memory_space=pl.ANY)] * n,
        out_specs=[pl.BlockSpec(memory_space=pl.ANY)] * n,
        out_shape=_exchange_out_shapes(arrays, False),
        scratch_shapes=[pltpu.SemaphoreType.DMA((n, n_sem)), pltpu.SemaphoreType.DMA((n, n_sem)), pltpu.SemaphoreType.DMA((n,))],
    )(*arrays)


def _exchange_out_shapes(arrays, scatter):
    return [jax.ShapeDtypeStruct(a.shape if scatter else (N_DEV,) + a.shape, a.dtype) for a in arrays]


def _exchange_sems(n):
    return [pltpu.SemaphoreType.DMA((n, N_DEV - 1)), pltpu.SemaphoreType.DMA((n, N_DEV - 1)), pltpu.SemaphoreType.DMA((n,))]


def _exchange_phase(ins, outs, sems, scatter, start):
    send_sems, recv_sems, loc_sems = sems
    x, y, c = lax.axis_index("x"), lax.axis_index("y"), lax.axis_index("c")
    me = 4 * x + 2 * y + c
    for t in range(len(ins)):
        loc = pltpu.make_async_copy(ins[t].at[me] if scatter else ins[t], outs[t].at[me], loc_sems.at[t])
        if start:
            loc.start()
        else:
            loc.wait()
        for k in range(N_DEV - 1):
            bx, by, bc = ((k + 1) >> 2) & 1, ((k + 1) >> 1) & 1, (k + 1) & 1
            px = 1 - x if bx else x
            py = 1 - y if by else y
            pc = 1 - c if bc else c
            peer = 4 * px + 2 * py + pc
            src = ins[t].at[peer] if scatter else ins[t]
            copy = lambda dst: pltpu.make_async_remote_copy(
                src_ref=src, dst_ref=dst, send_sem=send_sems.at[t, k], recv_sem=recv_sems.at[t, k],
                device_id=(px, py, pc), device_id_type=pl.DeviceIdType.MESH)
            if start:
                copy(outs[t].at[me]).start()
            else:
                copy(outs[t].at[me]).wait_send()
                copy(outs[t].at[peer]).wait_recv()


def _adam_math(w, g, m, v):
    m = ADAM_B1 * m + (1.0 - ADAM_B1) * g
    v = ADAM_B2 * v + (1.0 - ADAM_B2) * (g * g)
    m_hat = m / (1.0 - ADAM_B1 ** ADAM_STEP)
    v_hat = v / (1.0 - ADAM_B2 ** ADAM_STEP)
    delta = -ADAM_LR * (m_hat / (jnp.sqrt(v_hat) + ADAM_EPS) + ADAM_WD * w)
    return delta, m, v


def _adam_big(name, parts, w, m, v, tm):
    r, c = w.shape
    tm = tm if r % tm == 0 else r

    def body(p_ref, w_ref, m_ref, v_ref, g_ref, d_ref, nm_ref, nv_ref):
        g = p_ref[0].astype(F32)
        for s in range(1, N_DEV):
            g = g + p_ref[s].astype(F32)
        g_ref[...] = g
        d_ref[...], nm_ref[...], nv_ref[...] = _adam_math(w_ref[...], g, m_ref[...], v_ref[...])

    blk = lambda: pl.BlockSpec((tm, c), lambda i: (i, 0))
    return _pcall(
        body, name=name, grid=(r // tm,),
        in_specs=[pl.BlockSpec((N_DEV, tm, c), lambda i: (0, i, 0)), blk(), blk(), blk()],
        out_specs=[blk(), blk(), blk(), blk()],
        out_shape=[jax.ShapeDtypeStruct((r, c), F32)] * 4,
        compiler_params=_params(("parallel",)),
    )(parts, w, m, v)


SMALL_ROWS = 56
ROW_DD, ROW_LOSS = 5, 6


def _small_sum(gathered):
    def body(g_ref, o_ref, x_ref):
        s = g_ref[0]
        for dev in range(1, N_DEV):
            s = s + g_ref[dev]
        o_ref[...] = s
        ri = lax.broadcasted_iota(jnp.int32, (D_MODEL, LANES), 0)
        ro = lax.broadcasted_iota(jnp.int32, (D_MODEL, LANES), 1)
        heads = _mmx(jnp.broadcast_to(s[ROW_DD:ROW_DD + 1, :], (SUBLANES, D_MODEL)), (ri // SSD_P == ro).astype(F32))
        loss = _rowsum(jnp.broadcast_to(s[ROW_LOSS:ROW_LOSS + 1, :], (SUBLANES, D_MODEL)))
        row = lax.broadcasted_iota(jnp.int32, (SUBLANES, LANES), 0)
        x_ref[...] = jnp.where(row == 0, heads, jnp.broadcast_to(loss, (SUBLANES, LANES)))

    return _pcall(
        body, name="small_sum",
        out_shape=[jax.ShapeDtypeStruct((SMALL_ROWS, D_MODEL), F32), jax.ShapeDtypeStruct((SUBLANES, LANES), F32)],
        compiler_params=_params(None),
    )(gathered)


def _adam_small(g, w, m, v):
    def body(g_ref, w_ref, m_ref, v_ref, d_ref, nm_ref, nv_ref):
        d_ref[...], nm_ref[...], nv_ref[...] = _adam_math(w_ref[...], g_ref[...], m_ref[...], v_ref[...])

    return _pcall(body, name="adam_small", out_shape=[jax.ShapeDtypeStruct(g.shape, F32)] * 3,
                  compiler_params=_params(None))(g, w, m, v)


def _pack(pieces, rows):
    flat = jnp.concatenate([p.reshape(-1).astype(F32) for p in pieces])
    return jnp.pad(flat, (0, rows * D_MODEL - flat.shape[0])).reshape(rows, D_MODEL)


def _unpack(packed, shapes):
    flat = packed.reshape(-1)
    out, off = [], 0
    for shp in shapes:
        size = 1
        for s in shp:
            size *= s
        out.append(flat[off:off + size].reshape(shp))
        off += size
    return out


def _permute_in(w):
    pad = jnp.zeros((w.shape[0], PROJ_W - D_IN), w.dtype)
    return jnp.concatenate([w[:, 0:4096], w[:, 4112:6672], w[:, 4096:4112], w[:, 6672:6688], pad], axis=1)


def _unpermute_in(g):
    return jnp.concatenate([g[:, 0:4096], g[:, 6656:6672], g[:, 4096:6656], g[:, 6672:6688]], axis=1)


def _lane_row(vec, start):
    return jnp.zeros((LANES,), F32).at[start:start + vec.shape[0]].set(vec)


def _cols_from_shards(g):
    return jnp.transpose(g, (1, 0, 2)).reshape(g.shape[1], N_DEV * g.shape[2])


def _cols_to_shards(a):
    return jnp.transpose(a.astype(BF16).reshape(a.shape[0], N_DEV, a.shape[1] // N_DEV), (1, 0, 2))


def _rows_to_shards(a):
    return a.astype(BF16).reshape(N_DEV, a.shape[0] // N_DEV, a.shape[1])


def _local_step(x, tgt, wp_in, rest, p, rest_is_sharded):
    bsz, seq, d = x.shape
    t = bsz * seq
    x2 = x.reshape(t, d)
    tgt2 = tgt.reshape(t, d)
    tm = min(256, seq)
    tm_wide = min(128, seq)
    sb = min(512, seq)

    gp = jnp.zeros((SUBLANES, LANES), F32).at[0].set(_lane_row(p["gdn_a_log"], 8)).at[1].set(_lane_row(p["gdn_dt_bias"], 8))
    sp = jnp.zeros((SUBLANES, LANES), F32).at[0].set(_lane_row(p["ssd_a_log"], 16)).at[1].set(_lane_row(p["ssd_dt_bias"], 16))
    dvec = jnp.repeat(p["ssd_d"], SSD_P).reshape(1, d)
    row = lambda v: v.reshape(1, -1)
    pre_mix, post_mix, pre_ffn, post_ffn = (row(p[k]) for k in ("pre_mix_norm", "post_mix_norm", "pre_ffn_norm", "post_ffn_norm"))
    gnw, snw = row(p["gdn_norm_w"]), row(p["ssd_norm_w"])
    gcw, scw, scb, fcw, fcb = p["gdn_conv_w"], p["ssd_conv_w"], row(p["ssd_conv_b"]), p["ffn_conv_w"], row(p["ffn_conv_b"])

    h1 = _norm_cast("norm_in", x2, pre_mix, tm)
    proj = _matmul("mm_proj", h1, wp_in, "nn", F32)
    b3 = lambda a: a.reshape(bsz, seq, a.shape[-1])
    b2 = lambda a: a.reshape(t, a.shape[-1])
    rows_of = lambda a, lo, n: jnp.transpose(a[:, lo:lo + n].reshape(bsz, seq // CHUNK, CHUNK, n), (0, 3, 1, 2))
    qn, kn, vv, gs, ypre_gdn = _gdn_prep(proj, gcw, gp, seq, tm)
    qn, kn, vv, gs = b3(qn), b3(kn), b3(vv), b3(gs)
    gr = jnp.transpose(b2(gs)[:, 8:8 + GDN_HEADS].reshape(bsz, seq // CHUNK, CHUNK, GDN_HEADS), (0, 1, 3, 2))
    (o_gdn, gdn_st, gdn_ti), gathered = _gdn_chunk_fwd(qn, kn, vv, gs, gr, bsz, seq, sb, list(rest) if rest_is_sharded else [])
    if rest_is_sharded:
        w_out, w_up, w_down = gathered[0].reshape(-1, d), _cols_from_shards(gathered[1]), gathered[2].reshape(-1, d)
    else:
        w_out, w_up, w_down = rest
    o_gdn = b2(o_gdn)
    xs, bc, dtx, acsx, acs, ypre_ssd = _ssd_prep(proj, scw, scb, sp, seq, tm)
    ar = rows_of(acs, 16, SSD_HEADS)
    y_ssd, ssd_st = _ssd_chunk_fwd(b3(xs), b3(bc), b3(dtx), b3(acsx), b3(acs), ar, bsz, seq, sb)
    y_ssd = b2(y_ssd)
    mixin = _gate_norm(o_gdn, y_ssd, xs, proj, gnw, snw, dvec, tm)
    mix, x1, h2 = _out_mid(mixin, w_out, x2, post_mix, pre_ffn)
    u_pre = _matmul("mm_up", h2, w_up, "nn", F32)
    act, u = _ffn_act(u_pre, fcw, fcb, seq, tm_wide)
    dy, df, loss_lanes, d_post_ffn = _down_final(act, w_down, x1, tgt2, post_ffn)

    g_down = _matmul("mm_dw_down", act, df, "tn", F32, tm=1408)
    dact = _matmul("mm_dact", df, w_down, "nt", F32, tn=1408)
    du_pre, d_fcw, d_fcb = _ffn_bwd(u, u_pre, dact, fcw, seq, tm_wide)
    g_up = _matmul("mm_dw_up", h2, du_pre, "tn", F32)
    dx1, dmix, d_post_mix, d_pre_ffn = _dh2_mid_bwd(du_pre, w_up, x1, mix, dy, post_mix, pre_ffn)
    g_out = _matmul("mm_dw_out", mixin, dmix, "tn", F32)
    dmixin = _matmul("mm_dmixin", dmix, w_out, "nt", F32)
    do_gdn, dza, dy_ssd, dxs_d, dzs, d_gnw, d_snw, d_dd = _gate_norm_bwd(o_gdn, y_ssd, xs, proj, dmixin, gnw, snw, dvec, tm)
    dxs_c, dbc, ddt, dacs = (b2(a) for a in _ssd_chunk_bwd(
        b3(xs), b3(bc), b3(dtx), b3(acsx), b3(acs), ar, b3(dy_ssd), ssd_st, bsz, seq, sb))
    riders = [_rows_to_shards(g_out), _cols_to_shards(g_up), _rows_to_shards(g_down)] if rest_is_sharded else []
    dgdn, received = _gdn_chunk_bwd(qn, kn, vv, gs, gr, b3(do_gdn), gdn_st, gdn_ti, bsz, seq, min(256, seq), riders)
    if rest_is_sharded:
        g_out, g_up, g_down = received
    dqn, dkn, dvv, dgb = (b2(a) for a in dgdn)
    dpre_gdn, dsm_gdn, d_gp = _gdn_prep_bwd(ypre_gdn, proj, dqn, dkn, dvv, dgb, gp, tm)
    dpre_ssd, dsm, d_scb, d_sp = _ssd_prep_bwd(ypre_ssd, proj, dxs_c, dxs_d, dbc, ddt, dacs, dsm_gdn, sp, tm)
    dproj, d_gcw, d_scw = _assemble_dproj(dpre_gdn, dza, dzs, dpre_ssd, dsm, proj, gcw, scw, seq, tm)
    g_in = _matmul("mm_dw_in", h1, dproj, "tn", F32)
    if rest_is_sharded:
        (dx, d_pre_mix), (g_in,) = _dh1_first_bwd(dproj, wp_in, x2, dx1, pre_mix, [_cols_to_shards(_unpermute_in(g_in))])
    else:
        dx, d_pre_mix = _dh1_first_bwd(dproj, wp_in, x2, dx1, pre_mix, [])

    small = dict(pre_mix_norm=d_pre_mix, ssd_norm_w=d_snw, post_mix_norm=d_post_mix, pre_ffn_norm=d_pre_ffn,
                 post_ffn_norm=d_post_ffn, dd_lanes=d_dd, loss_lanes=loss_lanes, gdn_gates=d_gp, ssd_gates=d_sp,
                 gdn_norm_w=d_gnw, gdn_conv_w=d_gcw[0:4], ssd_conv_w=d_scw[0:4], ssd_conv_b=d_scb,
                 ffn_conv_w=d_fcw[0:3], ffn_conv_b=d_fcb)
    return dx.reshape(bsz, seq, d), g_in, g_out, g_up, g_down, small


def kernel(x, pre_mix_norm, w_in, gdn_conv_w, gdn_a_log, gdn_dt_bias, gdn_norm_w, ssd_conv_w, ssd_conv_b, ssd_a_log, ssd_dt_bias, ssd_d, ssd_norm_w, w_out, post_mix_norm, pre_ffn_norm, w_up, ffn_conv_w, ffn_conv_b, w_down, post_ffn_norm, loss_target, m_pre_mix_norm, m_w_in, m_gdn_conv_w, m_gdn_a_log, m_gdn_dt_bias, m_gdn_norm_w, m_ssd_conv_w, m_ssd_conv_b, m_ssd_a_log, m_ssd_dt_bias, m_ssd_d, m_ssd_norm_w, m_w_out, m_post_mix_norm, m_pre_ffn_norm, m_w_up, m_ffn_conv_w, m_ffn_conv_b, m_w_down, m_post_ffn_norm, v_pre_mix_norm, v_w_in, v_gdn_conv_w, v_gdn_a_log, v_gdn_dt_bias, v_gdn_norm_w, v_ssd_conv_w, v_ssd_conv_b, v_ssd_a_log, v_ssd_dt_bias, v_ssd_d, v_ssd_norm_w, v_w_out, v_post_mix_norm, v_pre_ffn_norm, v_w_up, v_ffn_conv_w, v_ffn_conv_b, v_w_down, v_post_ffn_norm):
    names = ["pre_mix_norm", "w_in", "gdn_conv_w", "gdn_a_log", "gdn_dt_bias", "gdn_norm_w", "ssd_conv_w", "ssd_conv_b",
             "ssd_a_log", "ssd_dt_bias", "ssd_d", "ssd_norm_w", "w_out", "post_mix_norm", "pre_ffn_norm", "w_up",
             "ffn_conv_w", "ffn_conv_b", "w_down", "post_ffn_norm"]
    w_args = [pre_mix_norm, w_in, gdn_conv_w, gdn_a_log, gdn_dt_bias, gdn_norm_w, ssd_conv_w, ssd_conv_b, ssd_a_log, ssd_dt_bias, ssd_d, ssd_norm_w, w_out, post_mix_norm, pre_ffn_norm, w_up, ffn_conv_w, ffn_conv_b, w_down, post_ffn_norm]
    m_args = [m_pre_mix_norm, m_w_in, m_gdn_conv_w, m_gdn_a_log, m_gdn_dt_bias, m_gdn_norm_w, m_ssd_conv_w, m_ssd_conv_b, m_ssd_a_log, m_ssd_dt_bias, m_ssd_d, m_ssd_norm_w, m_w_out, m_post_mix_norm, m_pre_ffn_norm, m_w_up, m_ffn_conv_w, m_ffn_conv_b, m_w_down, m_post_ffn_norm]
    v_args = [v_pre_mix_norm, v_w_in, v_gdn_conv_w, v_gdn_a_log, v_gdn_dt_bias, v_gdn_norm_w, v_ssd_conv_w, v_ssd_conv_b, v_ssd_a_log, v_ssd_dt_bias, v_ssd_d, v_ssd_norm_w, v_w_out, v_post_mix_norm, v_pre_ffn_norm, v_w_up, v_ffn_conv_w, v_ffn_conv_b, v_w_down, v_post_ffn_norm]
    w = {k: a[0] for k, a in zip(names, w_args)}
    m = {k: a[0] for k, a in zip(names, m_args)}
    v = {k: a[0] for k, a in zip(names, v_args)}
    idx = 4 * lax.axis_index("x") + 2 * lax.axis_index("y") + lax.axis_index("c")
    big = ("w_in", "w_out", "w_up", "w_down")
    conv = ("gdn_conv_w", "ssd_conv_w", "ffn_conv_w")

    conv_local = jnp.concatenate([jnp.pad(w[k], ((0, 4 - w[k].shape[0]), (0, 0))) for k in conv], axis=1)
    g_in, g_conv = _gather_two_level("gather_weights", [w["w_in"].astype(BF16), conv_local])
    wp_in = _permute_in(_cols_from_shards(g_in))
    p = {k: w[k] for k in names if k not in big and k not in conv}
    off = 0
    for k in conv:
        cw = w[k].shape[1]
        p[k] = jnp.transpose(g_conv[:, :w[k].shape[0], off:off + cw], (1, 0, 2)).reshape(w[k].shape[0], N_DEV * cw)
        off += cw

    rest = tuple(w[k].astype(BF16) for k in ("w_out", "w_up", "w_down"))
    dx, p_in, p_out, p_up, p_down, small = _local_step(x, loss_target, wp_in, rest, p, True)

    gate_row = jnp.concatenate([small["gdn_gates"][0], small["gdn_gates"][1], small["ssd_gates"][0], small["ssd_gates"][1],
                                small["gdn_norm_w"][0], jnp.zeros((D_MODEL - 5 * LANES,), F32)]).reshape(1, D_MODEL)
    pack = _pack([small["pre_mix_norm"], small["ssd_norm_w"], small["post_mix_norm"], small["pre_ffn_norm"],
                  small["post_ffn_norm"], small["dd_lanes"], small["loss_lanes"], gate_row,
                  small["gdn_conv_w"], small["ssd_conv_w"], jnp.pad(small["ssd_conv_b"], ((0, 0), (0, 512))),
                  jnp.pad(small["ffn_conv_w"].reshape(-1), (0, 17 * D_MODEL - 3 * 2 * D_FF)),
                  jnp.pad(small["ffn_conv_b"], ((0, 0), (0, 512)))], SMALL_ROWS)
    (pack_all,) = _gather_two_level("gather_small", [pack])
    ssum, extra = _small_sum(pack_all)

    grads, deltas, new_m, new_v = {}, {}, {}, {}
    for k, parts in (("w_in", p_in), ("w_out", p_out), ("w_up", p_up), ("w_down", p_down)):
        grads[k], deltas[k], new_m[k], new_v[k] = _adam_big("adam_" + k, parts, w[k], m[k], v[k], 256)

    flat = ssum.reshape(-1)
    gate = ssum[7]
    sg = dict(pre_mix_norm=ssum[0], ssd_norm_w=ssum[1], post_mix_norm=ssum[2], pre_ffn_norm=ssum[3], post_ffn_norm=ssum[4],
              gdn_a_log=gate[8:16], gdn_dt_bias=gate[LANES + 8:LANES + 16], ssd_a_log=gate[2 * LANES + 16:2 * LANES + 32],
              ssd_dt_bias=gate[3 * LANES + 16:3 * LANES + 32], gdn_norm_w=gate[4 * LANES:5 * LANES], ssd_d=extra[0, 0:SSD_HEADS])
    o = 8 * D_MODEL
    full_gcw = flat[o:o + 4 * 3072].reshape(4, 3072)
    o += 12 * D_MODEL
    full_scw = flat[o:o + 4 * 1536].reshape(4, 1536)
    o += 6 * D_MODEL
    sg["ssd_conv_b"] = flat[o:o + 1536]
    o += 2 * D_MODEL
    full_fcw = flat[o:o + 3 * 2 * D_FF].reshape(3, 2 * D_FF)
    o += 17 * D_MODEL
    sg["ffn_conv_b"] = flat[o:o + 2 * D_FF]
    for k, full in (("gdn_conv_w", full_gcw), ("ssd_conv_w", full_scw), ("ffn_conv_w", full_fcw)):
        cw = w[k].shape[1]
        sg[k] = lax.dynamic_slice_in_dim(full, idx * cw, cw, axis=1)
    small_names = [k for k in names if k not in big]
    rows = 24
    gpk = _pack([sg[k] for k in small_names], rows)
    dpk, mpk, vpk = _adam_small(gpk, _pack([w[k] for k in small_names], rows), _pack([m[k] for k in small_names], rows),
                                _pack([v[k] for k in small_names], rows))
    shapes = [w[k].shape for k in small_names]
    for k, g_, d_, m_, v_ in zip(small_names, _unpack(gpk, shapes), _unpack(dpk, shapes), _unpack(mpk, shapes), _unpack(vpk, shapes)):
        grads[k], deltas[k], new_m[k], new_v[k] = g_, d_, m_, v_

    loss = extra[1, 0]
    lead = lambda a: a[None]
    return (loss, dx, *[lead(grads[k]) for k in names], *[lead(deltas[k]) for k in names],
            *[lead(new_m[k]) for k in names], *[lead(new_v[k]) for k in names])
```

```python
import functools

import jax
import jax.numpy as jnp
from jax import lax
from jax.experimental import pallas as pl
from jax.experimental.pallas import tpu as pltpu

F32 = jnp.float32
BF16 = jnp.bfloat16
MXU_DTYPE = jnp.bfloat16
HIGHEST = lax.Precision.HIGHEST
VMEM_LIMIT_V7X = 48 * 1024 * 1024
SUBLANES = 8
LANES = 128

D_MODEL = 1024
GDN_HEADS = 8
GDN_DK = 128
SSD_HEADS = 16
SSD_P = 64
SSD_GROUPS = 2
SSD_HPG = 8
SSD_N = 128
CHUNK = 64
D_FF = 2816
EPS = 1e-6
N_DEV = 8
PROJ_W = 7168
SMALL_CB = 52
D_IN = 6688

ADAM_LR = 0.001
ADAM_B1 = 0.9
ADAM_B2 = 0.999
ADAM_EPS = 1e-08
ADAM_WD = 0.01
ADAM_STEP = 10

NN = (((1,), (0,)), ((), ()))
NT = (((1,), (1,)), ((), ()))
TN = (((0,), (0,)), ((), ()))


def _pcall(body, **kw):
    return pl.pallas_call(body, **kw)


def _mm(a, b, dims=NN):
    return lax.dot_general(a.astype(MXU_DTYPE), b.astype(MXU_DTYPE), dims, preferred_element_type=F32)


def _mmx(a, b, dims=NN):
    return lax.dot_general(a, b, dims, precision=HIGHEST, preferred_element_type=F32)


def _split(a):
    hi = a.astype(MXU_DTYPE)
    return hi, (a - hi.astype(F32)).astype(MXU_DTYPE)


def _mm3(a, b, dims=NN):
    (ah, al), (bh, bl) = _split(a), _split(b)
    dot = lambda p, q: lax.dot_general(p, q, dims, preferred_element_type=F32)
    return dot(ah, bh) + (dot(ah, bl) + dot(al, bh))


def _mmsel(a, sel, dims=NN, terms=2):
    s = sel.astype(MXU_DTYPE)
    out = None
    for _ in range(terms):
        part = a.astype(MXU_DTYPE)
        a = a - part.astype(F32)
        prod = lax.dot_general(part, s, dims, preferred_element_type=F32)
        out = prod if out is None else out + prod
    return out


def _sigmoid(x):
    return 0.5 * jnp.tanh(0.5 * x) + 0.5


def _softplus(x):
    return jnp.maximum(x, 0.0) + jnp.log(1.0 + jnp.exp(-jnp.abs(x)))


def _dsilu(x, s):
    return s * (1.0 + x * (1.0 - s))


def _rowsum(x):
    return jnp.sum(x, axis=1, keepdims=True)


def _colsum(x):
    return jnp.sum(x, axis=0, keepdims=True)


def _pick(dim, pref):
    if dim <= pref:
        return dim
    best = None
    t = LANES
    while t <= pref:
        if dim % t == 0:
            best = t
        t += LANES
    return dim if best is None else best


def _params(sem):
    return pltpu.CompilerParams(dimension_semantics=sem, vmem_limit_bytes=VMEM_LIMIT_V7X)


def _matmul(name, a, b, mode, out_dtype, tm=1024, tn=1024, tk=1024):
    if mode == "nn":
        (m, k), (_, n) = a.shape, b.shape
    elif mode == "nt":
        (m, k), (n, _) = a.shape, b.shape
    else:
        (k, m), (_, n) = a.shape, b.shape
    tm, tn, tk = _pick(m, tm), _pick(n, tn), _pick(k, tk)
    nk = k // tk
    if mode == "tn":
        a_spec = pl.BlockSpec((tk, tm), lambda i, j, kk: (kk, i))
    else:
        a_spec = pl.BlockSpec((tm, tk), lambda i, j, kk: (i, kk))
    if mode == "nt":
        b_spec = pl.BlockSpec((tn, tk), lambda i, j, kk: (j, kk))
    else:
        b_spec = pl.BlockSpec((tk, tn), lambda i, j, kk: (kk, j))
    dims = {"nn": NN, "nt": NT, "tn": TN}[mode]

    def body(a_ref, b_ref, o_ref, *acc):
        if nk == 1:
            o_ref[...] = _mm(a_ref[...], b_ref[...], dims).astype(out_dtype)
            return
        kk = pl.program_id(2)

        @pl.when(kk == 0)
        def _():
            acc[0][...] = jnp.zeros_like(acc[0])

        acc[0][...] += _mm(a_ref[...], b_ref[...], dims)

        @pl.when(kk == nk - 1)
        def _():
            o_ref[...] = acc[0][...].astype(out_dtype)

    return _pcall(
        body, name=name, grid=(m // tm, n // tn, nk),
        in_specs=[a_spec, b_spec],
        out_specs=pl.BlockSpec((tm, tn), lambda i, j, kk: (i, j)),
        out_shape=jax.ShapeDtypeStruct((m, n), out_dtype),
        scratch_shapes=[pltpu.VMEM((tm, tn), F32)] if nk > 1 else [],
        compiler_params=_params(("parallel", "parallel", "arbitrary")),
    )(a, b)


def _matmul_rows(name, a, b, mode, epilogue, row_ins, full_ins, outs, accs=(), tm=512, tk=1024, scatter_riders=()):
    if mode == "nn":
        (m, k), (_, n) = a.shape, b.shape
    else:
        (m, k), (n, _) = a.shape, b.shape
    tm, tk = _pick(m, tm), _pick(k, tk)
    nk = k // tk
    a_spec = pl.BlockSpec((tm, tk), lambda i, kk: (i, kk))
    b_spec = pl.BlockSpec((n, tk), lambda i, kk: (0, kk)) if mode == "nt" else pl.BlockSpec((tk, n), lambda i, kk: (kk, 0))
    dims = NT if mode == "nt" else NN
    n_row, n_full, n_out, n_acc = len(row_ins), len(full_ins), len(outs), len(accs)

    def body(a_ref, b_ref, *rest):
        ins = rest[:n_row + n_full]
        out_refs = rest[n_row + n_full:n_row + n_full + n_out]
        acc_refs = rest[n_row + n_full + n_out:n_row + n_full + n_out + n_acc]
        prod_scr = rest[-1]
        i, kk = pl.program_id(0), pl.program_id(1)

        if n_acc:
            @pl.when((i == 0) & (kk == 0))
            def _():
                for r in acc_refs:
                    r[...] = jnp.zeros_like(r)

        if nk == 1:
            epilogue(_mm(a_ref[...], b_ref[...], dims), *ins, *out_refs, *acc_refs)
            return

        @pl.when(kk == 0)
        def _():
            prod_scr[...] = jnp.zeros_like(prod_scr)

        prod_scr[...] += _mm(a_ref[...], b_ref[...], dims)

        @pl.when(kk == nk - 1)
        def _():
            epilogue(prod_scr[...], *ins, *out_refs, *acc_refs)

    grid = (m // tm, nk)
    riders = list(scatter_riders)
    n_in = 2 + n_row + n_full
    any_spec, rider_shapes, rider_sems, wrap = _riding_exchange(riders, True, n_in, n_out + n_acc, grid)
    in_specs = [a_spec, b_spec] + [pl.BlockSpec((tm, r.shape[1]), lambda i, kk: (i, 0)) for r in row_ins]
    in_specs += [pl.BlockSpec(f.shape, lambda i, kk, nd=f.ndim: (0,) * nd) for f in full_ins]
    out_specs = [pl.BlockSpec((tm, w), lambda i, kk: (i, 0)) for w, _ in outs]
    out_specs += [pl.BlockSpec(s, lambda i, kk: (0, 0)) for s in accs]
    out_shape = [jax.ShapeDtypeStruct((m, w), dt) for w, dt in outs] + [jax.ShapeDtypeStruct(s, F32) for s in accs]
    res = _pcall(
        wrap(body), name=name, grid=grid,
        in_specs=in_specs + any_spec, out_specs=out_specs + any_spec, out_shape=out_shape + rider_shapes,
        scratch_shapes=[pltpu.VMEM((tm, n), F32)] + rider_sems,
        compiler_params=_params(("arbitrary", "arbitrary")),
    )(a, b, *row_ins, *full_ins, *riders)
    return (res[:n_out + n_acc], res[n_out + n_acc:]) if riders else res


def _rowwise(name, body, n_rows, tm, ins, outs, accs=()):
    arrays, in_specs = [], []
    last8 = n_rows // SUBLANES - 1
    per = tm // SUBLANES
    for spec in ins:
        kind, arr = spec[0], spec[1]
        if kind == "full":
            in_specs.append(pl.BlockSpec(arr.shape, lambda i, nd=arr.ndim: (0,) * nd))
        else:
            w, cb = spec[2], spec[3]
            if kind == "row":
                in_specs.append(pl.BlockSpec((tm, w), lambda i, cb=cb: (i, cb)))
            elif kind == "prev":
                in_specs.append(pl.BlockSpec((SUBLANES, w), lambda i, cb=cb: (jnp.maximum(i * per - 1, 0), cb)))
            else:
                in_specs.append(pl.BlockSpec((SUBLANES, w), lambda i, cb=cb: (jnp.minimum((i + 1) * per, last8), cb)))
        arrays.append(arr)
    out_shape = [jax.ShapeDtypeStruct((n_rows, w), dt) for (w, dt) in outs]
    out_shape += [jax.ShapeDtypeStruct(s, F32) for s in accs]
    out_specs = [pl.BlockSpec((tm, w), lambda i: (i, 0)) for (w, _) in outs]
    out_specs += [pl.BlockSpec(s, lambda i: (0, 0)) for s in accs]
    n_io = len(ins) + len(outs)

    def kern(*refs):
        i = pl.program_id(0)
        if accs:
            @pl.when(i == 0)
            def _():
                for r in refs[n_io:]:
                    r[...] = jnp.zeros_like(r)
        body(i, *refs)

    res = _pcall(
        kern, name=name, grid=(n_rows // tm,), in_specs=in_specs, out_specs=out_specs, out_shape=out_shape,
        compiler_params=_params(("arbitrary",)),
    )(*arrays)
    return res


def _shift_down(x, halo, j):
    r = pltpu.roll(x, j, 0)
    hr = pltpu.roll(halo, j, 0)
    rows = lax.broadcasted_iota(jnp.int32, (SUBLANES, x.shape[1]), 0)
    top = jnp.where(rows < j, hr, r[0:SUBLANES])
    return jnp.concatenate([top, r[SUBLANES:]], axis=0)


def _shift_up(x, halo, j):
    tm = x.shape[0]
    r = pltpu.roll(x, tm - j, 0)
    hr = pltpu.roll(halo, SUBLANES - j, 0)
    rows = lax.broadcasted_iota(jnp.int32, (SUBLANES, x.shape[1]), 0)
    bot = jnp.where(rows >= SUBLANES - j, hr, r[tm - SUBLANES:])
    return jnp.concatenate([r[:tm - SUBLANES], bot], axis=0)


def _conv_taps(x, halo, kw):
    return [x if kw - 1 - k == 0 else _shift_down(x, halo, kw - 1 - k) for k in range(kw)]


def _conv(taps, w):
    y = taps[0] * w[0:1]
    for k in range(1, len(taps)):
        y = y + taps[k] * w[k:k + 1]
    return y


def _rms(x, width):
    r = lax.rsqrt(jnp.sum(x * x, axis=-1, keepdims=True) * (1.0 / width) + EPS)
    return x * r, r


def _rms_bwd(xh, r, dxh, width):
    return r * (dxh - xh * (jnp.sum(dxh * xh, axis=-1, keepdims=True) * (1.0 / width)))


def _seq_flags(i, seq, tm):
    nps = seq // tm
    pos = i % nps
    return jnp.where(pos == 0, 0.0, 1.0), jnp.where(pos == nps - 1, 0.0, 1.0)


def _norm_cast(name, x, w, tm):
    t, d = x.shape

    def body(i, x_ref, w_ref, h_ref):
        xh, _ = _rms(x_ref[...], d)
        h_ref[...] = (xh * w_ref[...]).astype(BF16)

    return _rowwise(name, body, t, tm, [("row", x, d, 0), ("full", w)], [(d, BF16)])[0]


def _gdn_prep(proj, cw, gp, seq, tm):
    t = proj.shape[0]
    d = D_MODEL

    def body(i, q_ref, qh_ref, k_ref, kh_ref, v_ref, vh_ref, sm_ref, cw_ref, gp_ref, qn_ref, kn_ref, vv_ref, gs_ref, ypre_ref):
        keep, _ = _seq_flags(i, seq, tm)
        for x_ref, h_ref, o_ref, off, scale in ((q_ref, qh_ref, qn_ref, 0, GDN_DK ** -0.5),
                                               (k_ref, kh_ref, kn_ref, d, 1.0), (v_ref, vh_ref, vv_ref, 2 * d, None)):
            y = _conv(_conv_taps(x_ref[...], h_ref[...] * keep, 4), cw_ref[:, off:off + d])
            ypre_ref[:, off:off + d] = y
            a = y * _sigmoid(y)
            if scale is None:
                o_ref[...] = a
            else:
                for hh in range(GDN_HEADS):
                    s = a[:, hh * GDN_DK:(hh + 1) * GDN_DK]
                    n = lax.rsqrt(_rowsum(s * s) + EPS)
                    o_ref[:, hh * GDN_DK:(hh + 1) * GDN_DK] = s * (n * scale)
        sm = sm_ref[...]
        lane = lax.broadcasted_iota(jnp.int32, sm.shape, 1)
        beta = _sigmoid(sm)
        g = jnp.where((lane >= 8) & (lane < 16), -jnp.exp(gp_ref[0:1, :]) * _softplus(sm + gp_ref[1:2, :]), 0.0)
        gs_ref[...] = jnp.where(lane < 8, beta, _mmx(_block_tri(tm, False), g))

    ins = []
    for cb in range(3):
        ins += [("row", proj, d, cb), ("prev", proj, d, cb)]
    ins += [("row", proj, LANES, SMALL_CB), ("full", cw), ("full", gp)]
    return _rowwise("gdn_prep", body, t, tm, ins, [(d, F32), (d, F32), (d, F32), (LANES, F32), (3 * d, F32)])


def _block_tri(tm, upper):
    ri = lax.broadcasted_iota(jnp.int32, (tm, tm), 0)
    ci = lax.broadcasted_iota(jnp.int32, (tm, tm), 1)
    tri = (ri <= ci) if upper else (ri >= ci)
    return (tri & ((ri // CHUNK) == (ci // CHUNK))).astype(F32)


def _chunk_consts():
    row = lax.broadcasted_iota(jnp.int32, (CHUNK, CHUNK), 0)
    col = lax.broadcasted_iota(jnp.int32, (CHUNK, CHUNK), 1)
    return dict(
        tril=row >= col, strict=row > col, eye=(row == col).astype(F32),
        lane=lax.broadcasted_iota(jnp.int32, (CHUNK, LANES), 1),
        row1=lax.broadcasted_iota(jnp.int32, (CHUNK, 1), 0),
        ones=jnp.ones((CHUNK, LANES), F32))


def _hmap(fn, *lists):
    return [fn(*a) for a in zip(*lists)]


def _tri_inv(nmats, eye):
    x = [eye - n for n in nmats]
    p = _hmap(_mm3, nmats, nmats)
    for lvl in range(5):
        x = _hmap(lambda xi, pi: xi + _mm3(xi, pi), x, p)
        if lvl < 4:
            p = _hmap(_mm3, p, p)
    return x


def _gdn_gates(gs, gc_row, h, c):
    beta = _rowsum(jnp.where(c["lane"] == h, gs, 0.0))
    gc = _rowsum(jnp.where(c["lane"] == h + 8, gs, 0.0))
    dc = jnp.exp(jnp.where(c["tril"], gc - gc_row, -1e30))
    gl = gc[CHUNK - 1:CHUNK, :]
    return beta, dc, jnp.exp(gc), jnp.exp(gl), jnp.exp(gl - gc)


GDN_HB = GDN_HEADS


def _gdn_specs(seq, sb, hb, backward):
    assert hb == GDN_HEADS
    nsb = seq // sb
    ncb = sb // CHUNK
    order = (lambda j: nsb - 1 - j) if backward else (lambda j: j)
    specs = dict(
        wide=lambda: pl.BlockSpec((1, sb, hb * GDN_DK), lambda b, h, j: (b, order(j), h)),
        gs=pl.BlockSpec((1, sb, LANES), lambda b, h, j: (b, order(j), 0)),
        gr=pl.BlockSpec((1, ncb, GDN_HEADS, CHUNK), lambda b, h, j: (b, order(j), 0, 0)),
        st=pl.BlockSpec((1, hb, ncb * GDN_DK, GDN_DK), lambda b, h, j: (b, h, order(j), 0)),
        ti=pl.BlockSpec((1, hb, sb, CHUNK), lambda b, h, j: (b, h, order(j), 0)))
    return nsb, ncb, specs


def _riding_exchange(arrays, scatter, n_in, n_out, grid):
    n = len(arrays)
    if n == 0:
        return [], [], [], lambda body: body
    any_spec = [pl.BlockSpec(memory_space=pl.ANY)] * n

    def wrap(body):
        def wrapped(*refs):
            ins = refs[n_in:n_in + n]
            outs = refs[n_in + n + n_out:n_in + 2 * n + n_out]
            sems = refs[len(refs) - 3:]
            pid = [pl.program_id(a) for a in range(len(grid))]
            first = functools.reduce(lambda a, b: a & b, [p == 0 for p in pid])
            last = functools.reduce(lambda a, b: a & b, [p == g - 1 for p, g in zip(pid, grid)])

            @pl.when(first)
            def _():
                _exchange_phase(ins, outs, sems, scatter, start=True)

            body(*refs[:n_in], *refs[n_in + n:n_in + n + n_out], *refs[n_in + 2 * n + n_out:len(refs) - 3])

            @pl.when(last)
            def _():
                _exchange_phase(ins, outs, sems, scatter, start=False)

        return wrapped

    return any_spec, _exchange_out_shapes(arrays, scatter), _exchange_sems(n), wrap


def _gdn_chunk_fwd(qn, kn, vv, gs, gr, bsz, seq, sb, riders):
    hb = GDN_HB
    nsb, ncb, sp = _gdn_specs(seq, sb, hb, False)
    grid = (bsz, GDN_HEADS // hb, nsb)
    any_spec, rider_shapes, rider_sems, wrap = _riding_exchange(riders, False, 5, 3, grid)

    def body(q_ref, k_ref, v_ref, gs_ref, gr_ref, o_ref, st_ref, ti_ref, s_scr):
        hg = pl.program_id(1)

        @pl.when(pl.program_id(2) == 0)
        def _():
            s_scr[...] = jnp.zeros_like(s_scr)

        c = _chunk_consts()

        def chunk(n, carry):
            r = pl.ds(pl.multiple_of(n * CHUNK, CHUNK), CHUNK)
            rs = pl.ds(pl.multiple_of(n * GDN_DK, GDN_DK), GDN_DK)
            gsv = gs_ref[0, r, :]
            heads = list(range(hb))
            sls = [slice(ih * GDN_DK, (ih + 1) * GDN_DK) for ih in heads]
            q = [q_ref[0, r, sl] for sl in sls]
            k = [k_ref[0, r, sl] for sl in sls]
            v = [v_ref[0, r, sl] for sl in sls]
            beta, dc, eg, egl, ekd = zip(*[
                _gdn_gates(gsv, gr_ref[0, n, pl.ds(ih, 1), :], ih, c) for ih in heads])
            kb = _hmap(lambda a, b: a * b, k, beta)
            amat = _hmap(lambda a, b, d_: jnp.where(c["strict"], _mm(a, b, NT) * d_, 0.0), kb, k, dc)
            tinv = _tri_inv(amat, c["eye"])
            u = _hmap(lambda t_, a, b: _mm3(t_, a * b), tinv, v, beta)
            w = _hmap(lambda t_, a, b: _mm3(t_, a * b), tinv, kb, eg)
            qk = _hmap(lambda a, b, d_: _mm(a, b, NT) * d_, q, k, dc)
            s = [s_scr[ih] for ih in heads]
            v_new = _hmap(lambda a, b, s_: a - _mm(b, s_), u, w, s)
            o = _hmap(lambda a, e, s_, qk_, vn: _mm(a * e, s_) + _mm(qk_, vn), q, eg, s, qk, v_new)
            s_new = _hmap(lambda s_, e, a, f, vn: s_ * e + _mm(a * f, vn, TN), s, egl, k, ekd, v_new)
            for ih in heads:
                o_ref[0, r, sls[ih]] = o[ih]
                st_ref[0, ih, rs, :] = s[ih]
                ti_ref[0, ih, r, :] = tinv[ih]
                s_scr[ih] = s_new[ih]
            return carry

        lax.fori_loop(0, ncb, chunk, 0)

    t3 = (bsz, seq, D_MODEL)
    res = _pcall(
        wrap(body), name="gdn_chunk_fwd", grid=grid,
        in_specs=[sp["wide"](), sp["wide"](), sp["wide"](), sp["gs"], sp["gr"]] + any_spec,
        out_specs=[sp["wide"](), sp["st"], sp["ti"]] + any_spec,
        out_shape=[jax.ShapeDtypeStruct(t3, F32),
                   jax.ShapeDtypeStruct((bsz, GDN_HEADS, (seq // CHUNK) * GDN_DK, GDN_DK), F32),
                   jax.ShapeDtypeStruct((bsz, GDN_HEADS, seq, CHUNK), F32)] + rider_shapes,
        scratch_shapes=[pltpu.VMEM((hb, GDN_DK, GDN_DK), F32)] + rider_sems,
        compiler_params=_params(("arbitrary", "arbitrary", "arbitrary")),
    )(qn, kn, vv, gs, gr, *riders)
    return res[:3], res[3:]


def _ssd_prep(proj, cw, cb, sp, seq, tm):
    t = proj.shape[0]
    d = D_MODEL
    ssd_w = SSD_HEADS * SSD_P

    def body(i, x_ref, xh_ref, bc_ref, bch_ref, sm_ref, cw_ref, cb_ref, sp_ref, xs_ref, bco_ref, dtx_ref, acsx_ref, acs_ref, ypre_ref):
        keep, _ = _seq_flags(i, seq, tm)
        y = _conv(_conv_taps(x_ref[...], xh_ref[...] * keep, 4), cw_ref[:, 0:d]) + cb_ref[:, 0:d]
        ypre_ref[:, 0:d] = y
        xs_ref[...] = y * _sigmoid(y)
        y = _conv(_conv_taps(bc_ref[...], bch_ref[...] * keep, 4), cw_ref[:, d:d + 512]) + cb_ref[:, d:d + 512]
        ypre_ref[:, d:d + 512] = y
        bco_ref[...] = y * _sigmoid(y)
        sm = sm_ref[...]
        lane = lax.broadcasted_iota(jnp.int32, sm.shape, 1)
        valid = (lane >= 16) & (lane < 32)
        dt = jnp.where(valid, _softplus(sm + sp_ref[1:2, :]), 0.0)
        adt = dt * (-jnp.exp(sp_ref[0:1, :]))
        acs = _mmx(_block_tri(tm, False), adt)
        l64 = lax.broadcasted_iota(jnp.int32, (LANES, ssd_w), 0)
        d64 = lax.broadcasted_iota(jnp.int32, (LANES, ssd_w), 1)
        e64 = (l64 - 16 == d64 // SSD_P).astype(F32)
        dtx_ref[...] = _mmsel(dt, e64, terms=3)
        acsx_ref[...] = _mmsel(acs, e64, terms=3)
        acs_ref[...] = acs

    ins = [("row", proj, d, 5), ("prev", proj, d, 5), ("row", proj, 512, 12), ("prev", proj, 512, 12),
           ("row", proj, LANES, SMALL_CB), ("full", cw), ("full", cb), ("full", sp)]
    return _rowwise("ssd_prep", body, t, tm, ins,
                    [(d, F32), (512, F32), (ssd_w, F32), (ssd_w, F32), (LANES, F32), (d + 512, F32)])


SSD_GW = SSD_HPG * SSD_P


def _ssd_head(acs, ar_ref, n, head, cbm, c):
    col = _rowsum(jnp.where(c["lane"] == head + 16, acs, 0.0))
    lm = jnp.exp(jnp.where(c["tril"], col - ar_ref[0, head, pl.ds(n, 1), :], -1e30))
    return lm, cbm * lm


def _ssd_specs(seq, sb):
    nsb = seq // sb
    ncb = sb // CHUNK
    def specs(order):
        return dict(
            wide=lambda: pl.BlockSpec((1, sb, SSD_HEADS * SSD_P), lambda b, j: (b, order(j), 0)),
            bc=lambda: pl.BlockSpec((1, sb, 2 * SSD_GROUPS * SSD_N), lambda b, j: (b, order(j), 0)),
            half=lambda: pl.BlockSpec((1, sb, SSD_GROUPS * SSD_N), lambda b, j: (b, order(j), 0)),
            small=lambda: pl.BlockSpec((1, sb, LANES), lambda b, j: (b, order(j), 0)),
            ar=pl.BlockSpec((1, SSD_HEADS, ncb, CHUNK), lambda b, j: (b, 0, order(j), 0)),
            st=pl.BlockSpec((1, ncb * SSD_N, SSD_HEADS * SSD_P), lambda b, j: (b, order(j), 0)))
    return nsb, ncb, specs(lambda j: j), specs(lambda j: nsb - 1 - j)


def _ssd_chunk_fwd(xs, bc, dtx, acsx, acs, ar, bsz, seq, sb):
    nsb, ncb, sp, _ = _ssd_specs(seq, sb)

    def body(x_ref, dtx_ref, ax_ref, bc_ref, acs_ref, ar_ref, y_ref, sts_ref, st_scr):
        @pl.when(pl.program_id(1) == 0)
        def _():
            st_scr[...] = jnp.zeros_like(st_scr)

        c = _chunk_consts()
        lane5 = lax.broadcasted_iota(jnp.int32, (CHUNK, SSD_GW), 1) // SSD_P

        def chunk(n, carry):
            r = pl.ds(pl.multiple_of(n * CHUNK, CHUNK), CHUNK)
            rs = pl.ds(pl.multiple_of(n * SSD_N, SSD_N), SSD_N)
            acsv = acs_ref[0, r, :]
            for g in range(SSD_GROUPS):
                gl = slice(g * SSD_GW, (g + 1) * SSD_GW)
                x, dt, ax = x_ref[0, r, gl], dtx_ref[0, r, gl], ax_ref[0, r, gl]
                bm = bc_ref[0, r, g * SSD_N:(g + 1) * SSD_N]
                cm = bc_ref[0, r, (SSD_GROUPS + g) * SSD_N:(SSD_GROUPS + g + 1) * SSD_N]
                xdt = x * dt
                cbm = _mm(cm, bm, NT)
                al = ax[CHUNK - 1:CHUNK, :]
                st = st_scr[:, gl]
                y = _mm(cm, st) * jnp.exp(ax)
                for hh in range(SSD_HPG):
                    _, gm = _ssd_head(acsv, ar_ref, n, g * SSD_HPG + hh, cbm, c)
                    y = y + _mm(gm, jnp.where(lane5 == hh, xdt, 0.0))
                y_ref[0, r, gl] = y
                sts_ref[0, rs, gl] = st
                st_scr[:, gl] = st * jnp.exp(al) + _mm(bm, xdt * jnp.exp(al - ax), TN)
            return carry

        lax.fori_loop(0, ncb, chunk, 0)

    return _pcall(
        body, name="ssd_chunk_fwd", grid=(bsz, nsb),
        in_specs=[sp["wide"](), sp["wide"](), sp["wide"](), sp["bc"](), sp["small"](), sp["ar"]],
        out_specs=[sp["wide"](), sp["st"]],
        out_shape=[jax.ShapeDtypeStruct((bsz, seq, SSD_HEADS * SSD_P), F32),
                   jax.ShapeDtypeStruct((bsz, (seq // CHUNK) * SSD_N, SSD_HEADS * SSD_P), F32)],
        scratch_shapes=[pltpu.VMEM((SSD_N, SSD_HEADS * SSD_P), F32)],
        compiler_params=_params(("parallel", "arbitrary")),
    )(xs, dtx, acsx, bc, acs, ar)


def _gate_norm(o_gdn, y_ssd, xs, proj, gnw, snw, dvec, tm):
    t = o_gdn.shape[0]
    d = D_MODEL

    def body(i, o_ref, za_ref, y_ref, xs_ref, zs_ref, gnw_ref, snw_ref, dv_ref, out_ref):
        for hh in range(GDN_HEADS):
            sl = slice(hh * GDN_DK, (hh + 1) * GDN_DK)
            oh, _ = _rms(o_ref[:, sl], GDN_DK)
            z = za_ref[:, sl]
            out_ref[:, sl] = (oh * gnw_ref[...] * (z * _sigmoid(z))).astype(BF16)
        zs = zs_ref[...]
        yg = (y_ref[...] + dv_ref[...] * xs_ref[...]) * (zs * _sigmoid(zs))
        for g in range(SSD_GROUPS):
            sl = slice(g * 512, (g + 1) * 512)
            yh, _ = _rms(yg[:, sl], 512)
            out_ref[:, d + g * 512:d + (g + 1) * 512] = (yh * snw_ref[:, sl]).astype(BF16)

    ins = [("row", o_gdn, d, 0), ("row", proj, d, 3), ("row", y_ssd, d, 0), ("row", xs, d, 0), ("row", proj, d, 4),
           ("full", gnw), ("full", snw), ("full", dvec)]
    return _rowwise("gate_norm", body, t, tm, ins, [(2 * d, BF16)])[0]


def _out_mid(mixin, w_out, x, pmw, pfw):
    d = D_MODEL

    def epilogue(mix, x_ref, pmw_ref, pfw_ref, mix_ref, x1_ref, h2_ref):
        mix_ref[...] = mix
        mh, _ = _rms(mix, d)
        x1 = x_ref[...] + mh * pmw_ref[...]
        x1_ref[...] = x1
        xh, _ = _rms(x1, d)
        h2_ref[...] = (xh * pfw_ref[...]).astype(BF16)

    return _matmul_rows("mm_out_mid", mixin, w_out, "nn", epilogue, [x], [pmw, pfw], [(d, F32), (d, F32), (d, BF16)],
                        tk=2 * d)


def _ffn_act(u_pre, cw, cb, seq, tm):
    t = u_pre.shape[0]

    def body(i, ug_ref, ugh_ref, uu_ref, uuh_ref, cw_ref, cb_ref, act_ref, u_ref):
        keep, _ = _seq_flags(i, seq, tm)
        gate = _conv(_conv_taps(ug_ref[...], ugh_ref[...] * keep, 3), cw_ref[:, 0:D_FF]) + cb_ref[:, 0:D_FF]
        up = _conv(_conv_taps(uu_ref[...], uuh_ref[...] * keep, 3), cw_ref[:, D_FF:2 * D_FF]) + cb_ref[:, D_FF:2 * D_FF]
        u_ref[:, 0:D_FF] = gate
        u_ref[:, D_FF:2 * D_FF] = up
        act_ref[...] = (gate * _sigmoid(gate) * up).astype(BF16)

    ins = [("row", u_pre, D_FF, 0), ("prev", u_pre, D_FF, 0), ("row", u_pre, D_FF, 1), ("prev", u_pre, D_FF, 1),
           ("full", cw), ("full", cb)]
    return _rowwise("ffn_act", body, t, tm, ins, [(D_FF, BF16), (2 * D_FF, F32)])


def _down_final(act, w_down, x1, tgt, w):
    d = D_MODEL

    def epilogue(f, x1_ref, t_ref, w_ref, dy_ref, df_ref, loss_ref, dw_ref):
        fh, r = _rms(f, d)
        e = x1_ref[...] + fh * w_ref[...] - t_ref[...]
        loss_ref[...] += _colsum(e * e) * (0.5 / d)
        dy = e * (1.0 / d)
        dy_ref[...] = dy
        dw_ref[...] += _colsum(dy * fh)
        df_ref[...] = _rms_bwd(fh, r, dy * w_ref[...], d).astype(BF16)

    return _matmul_rows("mm_down_final", act, w_down, "nn", epilogue, [x1, tgt], [w], [(d, F32), (d, BF16)],
                        accs=[(1, d), (1, d)], tk=D_FF)


def _ffn_bwd(u, u_pre, dact, cw, seq, tm):
    t = u.shape[0]

    def body(i, g_ref, gn_ref, up_ref, upn_ref, xg_ref, xu_ref, da_ref, dan_ref, cw_ref, dpre_ref, dcw_ref, dcb_ref):
        _, keep_next = _seq_flags(i, seq, tm)
        ext = lambda a_ref, n_ref: jnp.concatenate([a_ref[...], n_ref[...]], axis=0)
        rows = tm + SUBLANES
        gate, up = ext(g_ref, gn_ref), ext(up_ref, upn_ref)
        sg = _sigmoid(gate)
        da = jnp.concatenate([da_ref[...], dan_ref[...] * keep_next], axis=0)
        for off, grad, x_ref in ((0, da * up * _dsilu(gate, sg), xg_ref), (D_FF, da * gate * sg, xu_ref)):
            x = x_ref[...]
            own = grad[0:tm]
            acc = own * cw_ref[2:3, off:off + D_FF]
            dcb_ref[:, off:off + D_FF] += _colsum(own)
            dcw_ref[2:3, off:off + D_FF] += _colsum(own * x)
            for j in (1, 2):
                ahead = pltpu.roll(grad, rows - j, 0)[0:tm]
                acc = acc + ahead * cw_ref[2 - j:3 - j, off:off + D_FF]
                dcw_ref[2 - j:3 - j, off:off + D_FF] += _colsum(ahead * x)
            dpre_ref[:, off:off + D_FF] = acc.astype(BF16)

    ins = []
    for cb_ in range(2):
        ins += [("row", u, D_FF, cb_), ("next", u, D_FF, cb_)]
    ins += [("row", u_pre, D_FF, 0), ("row", u_pre, D_FF, 1), ("row", dact, D_FF, 0), ("next", dact, D_FF, 0), ("full", cw)]
    return _rowwise("ffn_bwd", body, t, tm, ins, [(2 * D_FF, BF16)], accs=[(SUBLANES, 2 * D_FF), (1, 2 * D_FF)])


def _assemble_dproj(dpre_gdn, dza, dzs, dpre_ssd, dsm, proj, gcw, scw, seq, tm):
    t = dpre_gdn.shape[0]
    d = D_MODEL

    def body(i, dg_ref, dgn_ref, dza_ref, dzs_ref, ds_ref, dsn_ref, dsm_ref, xq_ref, xk_ref, xv_ref, xx_ref, xbc_ref,
             gcw_ref, scw_ref, o_ref, dgcw_ref, dscw_ref):
        _, keep = _seq_flags(i, seq, tm)
        pieces = [(dg_ref, dgn_ref, gcw_ref, dgcw_ref, x_ref, 0, c0)
                  for x_ref, c0 in ((xq_ref, 0), (xk_ref, d), (xv_ref, 2 * d))]
        pieces += [(ds_ref, dsn_ref, scw_ref, dscw_ref, x_ref, 5 * d, c0) for x_ref, c0 in ((xx_ref, 0), (xbc_ref, d))]
        for d_ref, n_ref, cw_ref, dcw_ref, x_ref, base, c0 in pieces:
            w = x_ref.shape[1]
            x = x_ref[...]
            g = d_ref[:, c0:c0 + w]
            halo = n_ref[:, c0:c0 + w] * keep
            acc = g * cw_ref[3:4, c0:c0 + w]
            dcw_ref[3:4, c0:c0 + w] += _colsum(g * x)
            for j in range(1, 4):
                ahead = _shift_up(g, halo, j)
                acc = acc + ahead * cw_ref[3 - j:4 - j, c0:c0 + w]
                dcw_ref[3 - j:4 - j, c0:c0 + w] += _colsum(ahead * x)
            o_ref[:, base + c0:base + c0 + w] = acc.astype(BF16)
        o_ref[:, 3 * d:4 * d] = dza_ref[...]
        o_ref[:, 4 * d:5 * d] = dzs_ref[...]
        o_ref[:, 6 * d + 512:6 * d + 512 + LANES] = dsm_ref[...]
        o_ref[:, 6 * d + 512 + LANES:PROJ_W] = jnp.zeros((tm, PROJ_W - (6 * d + 512 + LANES)), BF16)

    ins = [("row", dpre_gdn, 3 * d, 0), ("next", dpre_gdn, 3 * d, 0), ("row", dza, d, 0), ("row", dzs, d, 0),
           ("row", dpre_ssd, d + 512, 0), ("next", dpre_ssd, d + 512, 0), ("row", dsm, LANES, 0),
           ("row", proj, d, 0), ("row", proj, d, 1), ("row", proj, d, 2), ("row", proj, d, 5), ("row", proj, 512, 12),
           ("full", gcw), ("full", scw)]
    return _rowwise("assemble_dproj", body, t, tm, ins, [(PROJ_W, BF16)], accs=[(SUBLANES, 3 * d), (SUBLANES, d + 512)])


def _dh2_mid_bwd(du_pre, w_up, x1, mix, dy, pmw, pfw):
    d = D_MODEL

    def epilogue(dh2, x1_ref, mix_ref, dy_ref, pmw_ref, pfw_ref, dx1_ref, dmix_ref, dpm_ref, dpf_ref):
        xh, r2 = _rms(x1_ref[...], d)
        dpf_ref[...] += _colsum(dh2 * xh)
        dx1 = dy_ref[...] + _rms_bwd(xh, r2, dh2 * pfw_ref[...], d)
        dx1_ref[...] = dx1
        mh, r = _rms(mix_ref[...], d)
        dpm_ref[...] += _colsum(dx1 * mh)
        dmix_ref[...] = _rms_bwd(mh, r, dx1 * pmw_ref[...], d).astype(BF16)

    return _matmul_rows("mm_dh2_mid_bwd", du_pre, w_up, "nt", epilogue, [x1, mix, dy], [pmw, pfw],
                        [(d, F32), (d, BF16)], accs=[(1, d), (1, d)], tk=D_FF)


def _gate_norm_bwd(o_gdn, y_ssd, xs, proj, dmixin, gnw, snw, dvec, tm):
    t = o_gdn.shape[0]
    d = D_MODEL

    def body(i, o_ref, za_ref, y_ref, xs_ref, zs_ref, dma_ref, dms_ref, gnw_ref, snw_ref, dv_ref,
             do_ref, dza_ref, dy_ref, dxs_ref, dzs_ref, dgnw_ref, dsnw_ref, dd_ref):
        for hh in range(GDN_HEADS):
            sl = slice(hh * GDN_DK, (hh + 1) * GDN_DK)
            oh, r = _rms(o_ref[:, sl], GDN_DK)
            z = za_ref[:, sl]
            sz = _sigmoid(z)
            dm = dma_ref[:, sl]
            don = dm * (z * sz)
            dza_ref[:, sl] = (dm * oh * gnw_ref[...] * _dsilu(z, sz)).astype(BF16)
            dgnw_ref[...] += _colsum(don * oh)
            do_ref[:, sl] = _rms_bwd(oh, r, don * gnw_ref[...], GDN_DK)
        zs = zs_ref[...]
        sz = _sigmoid(zs)
        sil = zs * sz
        x = xs_ref[...]
        y0 = y_ref[...] + dv_ref[...] * x
        yg = y0 * sil
        dms = dms_ref[...]
        for g in range(SSD_GROUPS):
            sl = slice(g * 512, (g + 1) * 512)
            yh, r = _rms(yg[:, sl], 512)
            dsnw_ref[:, sl] += _colsum(dms[:, sl] * yh)
            dyg = _rms_bwd(yh, r, dms[:, sl] * snw_ref[:, sl], 512)
            dy0 = dyg * sil[:, sl]
            dzs_ref[:, sl] = (dyg * y0[:, sl] * _dsilu(zs[:, sl], sz[:, sl])).astype(BF16)
            dy_ref[:, sl] = dy0
            dxs_ref[:, sl] = dy0 * dv_ref[:, sl]
            dd_ref[:, sl] += _colsum(dy0 * x[:, sl])

    ins = [("row", o_gdn, d, 0), ("row", proj, d, 3), ("row", y_ssd, d, 0), ("row", xs, d, 0), ("row", proj, d, 4),
           ("row", dmixin, d, 0), ("row", dmixin, d, 1), ("full", gnw), ("full", snw), ("full", dvec)]
    return _rowwise("gate_norm_bwd", body, t, tm, ins, [(d, F32), (d, BF16), (d, F32), (d, F32), (d, BF16)],
                    accs=[(1, GDN_DK), (1, d), (1, d)])


def _ssd_chunk_bwd(xs, bc, dtx, acsx, acs, ar, dy, sts, bsz, seq, sb):
    nsb, ncb, _, sp = _ssd_specs(seq, sb)

    def body(x_ref, dtx_ref, ax_ref, bc_ref, acs_ref, ar_ref, dy_ref, sts_ref, dx_ref, dbc_ref, ddt_ref, dacs_ref, dst_scr):
        @pl.when(pl.program_id(1) == 0)
        def _():
            dst_scr[...] = jnp.zeros_like(dst_scr)

        c = _chunk_consts()
        lane5 = lax.broadcasted_iota(jnp.int32, (CHUNK, SSD_GW), 1) // SSD_P
        row5 = lax.broadcasted_iota(jnp.int32, (CHUNK, SSD_GW), 0)
        sel_in = lax.broadcasted_iota(jnp.int32, (SSD_GW, LANES), 0) // SSD_P
        sel_out = lax.broadcasted_iota(jnp.int32, (SSD_GW, LANES), 1)

        def chunk(nn, carry):
            n = ncb - 1 - nn
            r = pl.ds(pl.multiple_of(n * CHUNK, CHUNK), CHUNK)
            rs = pl.ds(pl.multiple_of(n * SSD_N, SSD_N), SSD_N)
            acsv = acs_ref[0, r, :]
            ddt = jnp.zeros((CHUNK, LANES), F32)
            dacs = jnp.zeros((CHUNK, LANES), F32)
            for g in range(SSD_GROUPS):
                gl = slice(g * SSD_GW, (g + 1) * SSD_GW)
                x, dt, ax, dyv = x_ref[0, r, gl], dtx_ref[0, r, gl], ax_ref[0, r, gl], dy_ref[0, r, gl]
                bm = bc_ref[0, r, g * SSD_N:(g + 1) * SSD_N]
                cm = bc_ref[0, r, (SSD_GROUPS + g) * SSD_N:(SSD_GROUPS + g + 1) * SSD_N]
                st = sts_ref[0, rs, gl]
                dst = dst_scr[:, gl]
                rsel = (sel_in + (16 + g * SSD_HPG) == sel_out).astype(F32)
                xdt = x * dt
                cbm = _mm(cm, bm, NT)
                al = ax[CHUNK - 1:CHUNK, :]
                ex, el = jnp.exp(ax), jnp.exp(al)
                dec = jnp.exp(al - ax)
                xd = xdt * dec
                dye = dyv * ex
                dxd = _mm(bm, dst)
                dxdt = dec * dxd
                dcm = _mm(dye, st, NT)
                dbm = _mm(xd, dst, NT)
                z = dye * _mm(cm, st) - dxd * xd
                zl = _colsum(dst * st) * el + _colsum(dxd * xd)
                z = z + jnp.where(row5 == CHUNK - 1, zl, 0.0)
                dcb = jnp.zeros((CHUNK, CHUNK), F32)
                for hh in range(SSD_HPG):
                    head = g * SSD_HPG + hh
                    lm, gm = _ssd_head(acsv, ar_ref, n, head, cbm, c)
                    dym = jnp.where(lane5 == hh, dyv, 0.0)
                    dxdt = dxdt + _mm(gm, dym, TN)
                    dg = _mm(dym, xdt, NT)
                    dcb = dcb + dg * lm
                    pm = dg * gm
                    dacs = dacs + jnp.where(c["lane"] == head + 16, _rowsum(pm) - _mmsel(pm, c["ones"], TN), 0.0)
                dbc_ref[0, r, (SSD_GROUPS + g) * SSD_N:(SSD_GROUPS + g + 1) * SSD_N] = dcm + _mm(dcb, bm)
                dbc_ref[0, r, g * SSD_N:(g + 1) * SSD_N] = dbm + _mm(dcb, cm, TN)
                dacs = dacs + _mmsel(z, rsel)
                ddt = ddt + _mmsel(dxdt * x, rsel)
                dx_ref[0, r, gl] = dxdt * dt
                dst_scr[:, gl] = dst * el + _mm(cm, dye, TN)
            ddt_ref[0, r, :] = ddt
            dacs_ref[0, r, :] = dacs
            return carry

        lax.fori_loop(0, ncb, chunk, 0)

    return _pcall(
        body, name="ssd_chunk_bwd", grid=(bsz, nsb),
        in_specs=[sp["wide"](), sp["wide"](), sp["wide"](), sp["bc"](), sp["small"](), sp["ar"], sp["wide"](), sp["st"]],
        out_specs=[sp["wide"](), sp["bc"](), sp["small"](), sp["small"]()],
        out_shape=[jax.ShapeDtypeStruct((bsz, seq, SSD_HEADS * SSD_P), F32),
                   jax.ShapeDtypeStruct((bsz, seq, 2 * SSD_GROUPS * SSD_N), F32),
                   jax.ShapeDtypeStruct((bsz, seq, LANES), F32), jax.ShapeDtypeStruct((bsz, seq, LANES), F32)],
        scratch_shapes=[pltpu.VMEM((SSD_N, SSD_HEADS * SSD_P), F32)],
        compiler_params=_params(("parallel", "arbitrary")),
    )(xs, dtx, acsx, bc, acs, ar, dy, sts)


def _ssd_prep_bwd(ypre, proj, dxs_c, dxs_d, dbc, ddt, dacs, dsm_gdn, sp, tm):
    t = proj.shape[0]
    d = D_MODEL

    def body(i, y_ref, sm_ref, dxc_ref, dxd_ref, dbc_ref, ddt_ref, dacs_ref, dsg_ref, sp_ref,
             dpre_ref, dsm_ref, dcb_ref, dsp_ref):
        for off, w, grad in ((0, d, dxc_ref[...] + dxd_ref[...]), (d, 512, dbc_ref[...])):
            y = y_ref[:, off:off + w]
            dpre = grad * _dsilu(y, _sigmoid(y))
            dpre_ref[:, off:off + w] = dpre
            dcb_ref[:, off:off + w] += _colsum(dpre)
        sm = sm_ref[...]
        lane = lax.broadcasted_iota(jnp.int32, sm.shape, 1)
        valid = (lane >= 16) & (lane < 32)
        xb = sm + sp_ref[1:2, :]
        dt = jnp.where(valid, _softplus(xb), 0.0)
        a_neg = -jnp.exp(sp_ref[0:1, :])
        dadt_s = _mmx(_block_tri(tm, True), dacs_ref[...])
        dxb = jnp.where(valid, (ddt_ref[...] + dadt_s * a_neg) * _sigmoid(xb), 0.0)
        dsm_ref[...] = (dsg_ref[...] + dxb).astype(BF16)
        dsp_ref[1:2, :] += _colsum(dxb)
        dsp_ref[0:1, :] += jnp.where(valid[0:1, :], _colsum(dadt_s * dt) * a_neg, 0.0)

    ins = [("row", ypre, d + 512, 0), ("row", proj, LANES, SMALL_CB), ("row", dxs_c, d, 0), ("row", dxs_d, d, 0),
           ("row", dbc, 512, 0), ("row", ddt, LANES, 0), ("row", dacs, LANES, 0), ("row", dsm_gdn, LANES, 0), ("full", sp)]
    return _rowwise("ssd_prep_bwd", body, t, tm, ins, [(d + 512, F32), (LANES, BF16)],
                    accs=[(1, d + 512), (SUBLANES, LANES)])


def _gdn_chunk_bwd(qn, kn, vv, gs, gr, do, sts, tis, bsz, seq, sb, riders):
    hb = GDN_HB
    nsb, ncb, sp = _gdn_specs(seq, sb, hb, True)
    grid = (bsz, GDN_HEADS // hb, nsb)
    any_spec, rider_shapes, rider_sems, wrap = _riding_exchange(riders, True, 8, 4, grid)

    def body(q_ref, k_ref, v_ref, gs_ref, gr_ref, do_ref, st_ref, ti_ref, dq_ref, dk_ref, dv_ref, dgb_ref, ds_scr):
        hg = pl.program_id(1)

        @pl.when(pl.program_id(2) == 0)
        def _():
            ds_scr[...] = jnp.zeros_like(ds_scr)

        c = _chunk_consts()

        def chunk(nn, carry):
            n = ncb - 1 - nn
            r = pl.ds(pl.multiple_of(n * CHUNK, CHUNK), CHUNK)
            rs = pl.ds(pl.multiple_of(n * GDN_DK, GDN_DK), GDN_DK)
            gsv = gs_ref[0, r, :]
            heads = list(range(hb))
            sls = [slice(ih * GDN_DK, (ih + 1) * GDN_DK) for ih in heads]
            q = [q_ref[0, r, sl] for sl in sls]
            k = [k_ref[0, r, sl] for sl in sls]
            v = [v_ref[0, r, sl] for sl in sls]
            do_ = [do_ref[0, r, sl] for sl in sls]
            s = [st_ref[0, ih, rs, :] for ih in heads]
            tinv = [ti_ref[0, ih, r, :] for ih in heads]
            dsn = [ds_scr[ih] for ih in heads]
            beta, dc, eg, egl, ekd = zip(*[
                _gdn_gates(gsv, gr_ref[0, n, pl.ds(ih, 1), :], ih, c) for ih in heads])
            mul = lambda a, b: a * b
            kb = _hmap(mul, k, beta)
            rhs_w = _hmap(mul, kb, eg)
            u = _hmap(lambda t_, a, b: _mm3(t_, a * b), tinv, v, beta)
            w = _hmap(_mm3, tinv, rhs_w)
            amat = _hmap(lambda a, b, d_: jnp.where(c["strict"], _mm(a, b, NT) * d_, 0.0), kb, k, dc)
            qk = _hmap(lambda a, b, d_: _mm(a, b, NT) * d_, q, k, dc)
            qd = _hmap(mul, q, eg)
            kd = _hmap(mul, k, ekd)
            v_new = _hmap(lambda a, b, s_: a - _mm(b, s_), u, w, s)
            dv_new = _hmap(lambda qk_, d_, kd_, dn: _mm(qk_, d_, TN) + _mm(kd_, dn), qk, do_, kd, dsn)
            dqk = _hmap(lambda d_, vn: _mm(d_, vn, NT), do_, v_new)
            dqd = _hmap(lambda d_, s_: _mm(d_, s_, NT), do_, s)
            ds_new = _hmap(lambda qd_, d_, dn, e, w_, dvn: _mm(qd_, d_, TN) + dn * e - _mm(w_, dvn, TN),
                           qd, do_, dsn, egl, w, dv_new)
            dkd = _hmap(lambda vn, dn: _mm(vn, dn, NT), v_new, dsn)
            dgl = _hmap(lambda s_, dn, e: _colsum(_rowsum(s_ * dn)) * e, s, dsn, egl)
            dw = _hmap(lambda dvn, s_: -_mm(dvn, s_, NT), dv_new, s)
            dru = _hmap(lambda t_, a: _mm3(t_, a, TN), tinv, dv_new)
            drw = _hmap(lambda t_, a: _mm3(t_, a, TN), tinv, dw)
            da = _hmap(lambda a, u_, b, w_: jnp.where(c["strict"], -(_mm(a, u_, NT) + _mm(b, w_, NT)), 0.0), dru, u, drw, w)
            m = _hmap(mul, da, dc)
            dkb = _hmap(lambda a, e, m_, k_: a * e + _mm(m_, k_), drw, eg, m, k)
            mq = _hmap(mul, dqk, dc)
            dq = _hmap(lambda mq_, k_, a, e: _mm(mq_, k_) + a * e, mq, k, dqd, eg)
            dk = _hmap(lambda m_, kb_, mq_, q_, a, e, b, be: _mm(m_, kb_, TN) + _mm(mq_, q_, TN) + a * e + b * be,
                       m, kb, mq, q, dkd, ekd, dkb, beta)
            dbeta = _hmap(lambda a, v_, b, k_: _rowsum(a * v_) + _rowsum(b * k_), dru, v, dkb, k)
            pq = _hmap(lambda a, am, b, qk_: a * am + b * qk_, da, amat, dqk, qk)
            ekk = _hmap(lambda a, b: _rowsum(a * b), dkd, kd)
            dgc = _hmap(lambda pq_, a, rw, b, qd_, e, gl_: (
                _rowsum(pq_) - _mmsel(pq_, c["ones"], TN) + (_rowsum(a * rw) + _rowsum(b * qd_) - e)
                + jnp.where(c["row1"] == CHUNK - 1, _colsum(e) + gl_, 0.0)), pq, drw, rhs_w, dqd, qd, ekk, dgl)
            for ih in heads:
                ds_scr[ih] = ds_new[ih]
                dv_ref[0, r, sls[ih]] = dru[ih] * beta[ih]
                dq_ref[0, r, sls[ih]] = dq[ih]
                dk_ref[0, r, sls[ih]] = dk[ih]
                dgb_ref[0, r, sls[ih]] = jnp.where(c["lane"] == 0, dbeta[ih], jnp.where(c["lane"] == 1, dgc[ih], 0.0))
            return carry

        lax.fori_loop(0, ncb, chunk, 0)

    res = _pcall(
        wrap(body), name="gdn_chunk_bwd", grid=grid,
        in_specs=[sp["wide"](), sp["wide"](), sp["wide"](), sp["gs"], sp["gr"], sp["wide"](), sp["st"], sp["ti"]] + any_spec,
        out_specs=[sp["wide"](), sp["wide"](), sp["wide"](), sp["wide"]()] + any_spec,
        out_shape=[jax.ShapeDtypeStruct((bsz, seq, D_MODEL), F32)] * 4 + rider_shapes,
        scratch_shapes=[pltpu.VMEM((hb, GDN_DK, GDN_DK), F32)] + rider_sems,
        compiler_params=_params(("arbitrary", "arbitrary", "arbitrary")),
    )(qn, kn, vv, gs, gr, do, sts, tis, *riders)
    return res[:4], res[4:]


def _gdn_prep_bwd(ypre, proj, dqn, dkn, dvv, dgb, gp, tm):
    t = proj.shape[0]
    d = D_MODEL

    def body(i, y_ref, sm_ref, dq_ref, dk_ref, dv_ref, dgb_ref, gp_ref, dpre_ref, dsm_ref, dgp_ref):
        for g_ref, off, scale in ((dq_ref, 0, GDN_DK ** -0.5), (dk_ref, d, 1.0), (dv_ref, 2 * d, None)):
            y = y_ref[:, off:off + d]
            sy = _sigmoid(y)
            ds_ = _dsilu(y, sy)
            if scale is None:
                dpre_ref[:, off:off + d] = g_ref[...] * ds_
            else:
                a = y * sy
                for hh in range(GDN_HEADS):
                    sl = slice(hh * GDN_DK, (hh + 1) * GDN_DK)
                    s = a[:, sl]
                    n = lax.rsqrt(_rowsum(s * s) + EPS)
                    ah = s * n
                    gq = g_ref[:, sl]
                    dpre_ref[:, off + hh * GDN_DK:off + (hh + 1) * GDN_DK] = (
                        (scale * n) * (gq - ah * _rowsum(gq * ah)) * ds_[:, sl])
        sm = sm_ref[...]
        lane = lax.broadcasted_iota(jnp.int32, sm.shape, 1)
        si = lax.broadcasted_iota(jnp.int32, (d, LANES), 0)
        so = lax.broadcasted_iota(jnp.int32, (d, LANES), 1)
        sel = (((si % GDN_DK == 0) & (so == si // GDN_DK)) | ((si % GDN_DK == 1) & (so == si // GDN_DK + 8))).astype(F32)
        dsel = _mmx(dgb_ref[...], sel)
        is_g = (lane >= 8) & (lane < 16)
        dsel = jnp.where(is_g, _mmx(_block_tri(tm, True), dsel), dsel)
        beta = _sigmoid(sm)
        xb = sm + gp_ref[1:2, :]
        a_neg = -jnp.exp(gp_ref[0:1, :])
        sp = _softplus(xb)
        dxb = jnp.where(is_g, dsel * a_neg * _sigmoid(xb), 0.0)
        dsm_ref[...] = jnp.where(lane < 8, dsel * beta * (1.0 - beta), dxb)
        dgp_ref[1:2, :] += _colsum(dxb)
        dgp_ref[0:1, :] += _colsum(jnp.where(is_g, dsel * a_neg * sp, 0.0))

    ins = [("row", ypre, 3 * d, 0), ("row", proj, LANES, SMALL_CB), ("row", dqn, d, 0), ("row", dkn, d, 0),
           ("row", dvv, d, 0), ("row", dgb, d, 0), ("full", gp)]
    return _rowwise("gdn_prep_bwd", body, t, tm, ins, [(3 * d, F32), (LANES, F32)], accs=[(SUBLANES, LANES)])


def _dh1_first_bwd(dproj, wp_in, x, dx1, w, scatter_riders):
    d = D_MODEL

    def epilogue(dh, x_ref, dx1_ref, w_ref, dx_ref, dw_ref):
        xh, r = _rms(x_ref[...], d)
        dw_ref[...] += _colsum(dh * xh)
        dx_ref[...] = dx1_ref[...] + _rms_bwd(xh, r, dh * w_ref[...], d)

    return _matmul_rows("mm_dh1_first_bwd", dproj, wp_in, "nt", epilogue, [x, dx1], [w], [(d, F32)], accs=[(1, d)],
                        tk=PROJ_W // 2, scatter_riders=scatter_riders)


def _gather_two_level(name, arrays):
    n = len(arrays)
    n_sem = 7

    def body(*refs):
        ins, outs = refs[:n], refs[n:2 * n]
        send_sems, recv_sems, loc_sems = refs[2 * n:]
        x, y, c = lax.axis_index("x"), lax.axis_index("y"), lax.axis_index("c")
        slot = lambda px, py, pc: 4 * px + 2 * py + pc
        sibling = (x, y, 1 - c)
        chips = [(1 - x, y), (x, 1 - y), (1 - x, 1 - y)]

        def copy(t, k, src, block, to):
            return pltpu.make_async_remote_copy(
                src_ref=src, dst_ref=outs[t].at[block], send_sem=send_sems.at[t, k], recv_sem=recv_sems.at[t, k],
                device_id=to, device_id_type=pl.DeviceIdType.MESH)

        own, first, passed = [], [], []
        for t in range(n):
            own.append(pltpu.make_async_copy(ins[t], outs[t].at[slot(x, y, c)], loc_sems.at[t]))
            first.append(copy(t, 0, ins[t], slot(x, y, c), sibling))
            first += [copy(t, 1 + j, ins[t], slot(x, y, c), (px, py, c)) for j, (px, py) in enumerate(chips)]
        for cp in own + first:
            cp.start()
        for t in range(n):
            for j, (px, py) in enumerate(chips):
                copy(t, 1 + j, ins[t], slot(px, py, c), (px, py, c)).wait_recv()
                fwd = copy(t, 4 + j, outs[t].at[slot(px, py, c)], slot(px, py, c), sibling)
                fwd.start()
                passed.append(fwd)
        for t in range(n):
            copy(t, 0, ins[t], slot(x, y, 1 - c), sibling).wait_recv()
            for j, (px, py) in enumerate(chips):
                copy(t, 4 + j, ins[t], slot(px, py, 1 - c), sibling).wait_recv()
        for cp in first + passed:
            cp.wait_send()
        for cp in own:
            cp.wait()

    return _pcall(
        body, name=name,
        in_specs=[pl.BlockSpec(memory_space=pl.ANY)] * n,
        out_specs=[pl.BlockSpec(memory_space=pl.ANY)] * n,
        out_shape=_exchange_out_shapes(arrays, False),
        scratch_shapes=[pltpu.SemaphoreType.DMA((n, n_sem)), pltpu.SemaphoreType.DMA((n, n_sem)), pltpu.SemaphoreType.DMA((n,))],
    )(*arrays)


def _exchange_out_shapes(arrays, scatter):
    return [jax.ShapeDtypeStruct(a.shape if scatter else (N_DEV,) + a.shape, a.dtype) for a in arrays]


def _exchange_sems(n):
    return [pltpu.SemaphoreType.DMA((n, N_DEV - 1)), pltpu.SemaphoreType.DMA((n, N_DEV - 1)), pltpu.SemaphoreType.DMA((n,))]


def _exchange_phase(ins, outs, sems, scatter, start):
    send_sems, recv_sems, loc_sems = sems
    x, y, c = lax.axis_index("x"), lax.axis_index("y"), lax.axis_index("c")
    me = 4 * x + 2 * y + c
    for t in range(len(ins)):
        loc = pltpu.make_async_copy(ins[t].at[me] if scatter else ins[t], outs[t].at[me], loc_sems.at[t])
        if start:
            loc.start()
        else:
            loc.wait()
        for k in range(N_DEV - 1):
            bx, by, bc = ((k + 1) >> 2) & 1, ((k + 1) >> 1) & 1, (k + 1) & 1
            px = 1 - x if bx else x
            py = 1 - y if by else y
            pc = 1 - c if bc else c
            peer = 4 * px + 2 * py + pc
            src = ins[t].at[peer] if scatter else ins[t]
            copy = lambda dst: pltpu.make_async_remote_copy(
                src_ref=src, dst_ref=dst, send_sem=send_sems.at[t, k], recv_sem=recv_sems.at[t, k],
                device_id=(px, py, pc), device_id_type=pl.DeviceIdType.MESH)
            if start:
                copy(outs[t].at[me]).start()
            else:
                copy(outs[t].at[me]).wait_send()
                copy(outs[t].at[peer]).wait_recv()


def _adam_math(w, g, m, v):
    m = ADAM_B1 * m + (1.0 - ADAM_B1) * g
    v = ADAM_B2 * v + (1.0 - ADAM_B2) * (g * g)
    m_hat = m / (1.0 - ADAM_B1 ** ADAM_STEP)
    v_hat = v / (1.0 - ADAM_B2 ** ADAM_STEP)
    delta = -ADAM_LR * (m_hat / (jnp.sqrt(v_hat) + ADAM_EPS) + ADAM_WD * w)
    return delta, m, v


def _adam_big(name, parts, w, m, v, tm):
    r, c = w.shape
    tm = tm if r % tm == 0 else r

    def body(p_ref, w_ref, m_ref, v_ref, g_ref, d_ref, nm_ref, nv_ref):
        g = p_ref[0].astype(F32)
        for s in range(1, N_DEV):
            g = g + p_ref[s].astype(F32)
        g_ref[...] = g
        d_ref[...], nm_ref[...], nv_ref[...] = _adam_math(w_ref[...], g, m_ref[...], v_ref[...])

    blk = lambda: pl.BlockSpec((tm, c), lambda i: (i, 0))
    return _pcall(
        body, name=name, grid=(r // tm,),
        in_specs=[pl.BlockSpec((N_DEV, tm, c), lambda i: (0, i, 0)), blk(), blk(), blk()],
        out_specs=[blk(), blk(), blk(), blk()],
        out_shape=[jax.ShapeDtypeStruct((r, c), F32)] * 4,
        compiler_params=_params(("parallel",)),
    )(parts, w, m, v)


SMALL_ROWS = 56
ROW_DD, ROW_LOSS = 5, 6


def _small_sum(gathered):
    def body(g_ref, o_ref, x_ref):
        s = g_ref[0]
        for dev in range(1, N_DEV):
            s = s + g_ref[dev]
        o_ref[...] = s
        ri = lax.broadcasted_iota(jnp.int32, (D_MODEL, LANES), 0)
        ro = lax.broadcasted_iota(jnp.int32, (D_MODEL, LANES), 1)
        heads = _mmx(jnp.broadcast_to(s[ROW_DD:ROW_DD + 1, :], (SUBLANES, D_MODEL)), (ri // SSD_P == ro).astype(F32))
        loss = _rowsum(jnp.broadcast_to(s[ROW_LOSS:ROW_LOSS + 1, :], (SUBLANES, D_MODEL)))
        row = lax.broadcasted_iota(jnp.int32, (SUBLANES, LANES), 0)
        x_ref[...] = jnp.where(row == 0, heads, jnp.broadcast_to(loss, (SUBLANES, LANES)))

    return _pcall(
        body, name="small_sum",
        out_shape=[jax.ShapeDtypeStruct((SMALL_ROWS, D_MODEL), F32), jax.ShapeDtypeStruct((SUBLANES, LANES), F32)],
        compiler_params=_params(None),
    )(gathered)


def _adam_small(g, w, m, v):
    def body(g_ref, w_ref, m_ref, v_ref, d_ref, nm_ref, nv_ref):
        d_ref[...], nm_ref[...], nv_ref[...] = _adam_math(w_ref[...], g_ref[...], m_ref[...], v_ref[...])

    return _pcall(body, name="adam_small", out_shape=[jax.ShapeDtypeStruct(g.shape, F32)] * 3,
                  compiler_params=_params(None))(g, w, m, v)


def _pack(pieces, rows):
    flat = jnp.concatenate([p.reshape(-1).astype(F32) for p in pieces])
    return jnp.pad(flat, (0, rows * D_MODEL - flat.shape[0])).reshape(rows, D_MODEL)


def _unpack(packed, shapes):
    flat = packed.reshape(-1)
    out, off = [], 0
    for shp in shapes:
        size = 1
        for s in shp:
            size *= s
        out.append(flat[off:off + size].reshape(shp))
        off += size
    return out


def _permute_in(w):
    pad = jnp.zeros((w.shape[0], PROJ_W - D_IN), w.dtype)
    return jnp.concatenate([w[:, 0:4096], w[:, 4112:6672], w[:, 4096:4112], w[:, 6672:6688], pad], axis=1)


def _unpermute_in(g):
    return jnp.concatenate([g[:, 0:4096], g[:, 6656:6672], g[:, 4096:6656], g[:, 6672:6688]], axis=1)


def _lane_row(vec, start):
    return jnp.zeros((LANES,), F32).at[start:start + vec.shape[0]].set(vec)


def _cols_from_shards(g):
    return jnp.transpose(g, (1, 0, 2)).reshape(g.shape[1], N_DEV * g.shape[2])


def _cols_to_shards(a):
    return jnp.transpose(a.astype(BF16).reshape(a.shape[0], N_DEV, a.shape[1] // N_DEV), (1, 0, 2))


def _rows_to_shards(a):
    return a.astype(BF16).reshape(N_DEV, a.shape[0] // N_DEV, a.shape[1])


def _local_step(x, tgt, wp_in, rest, p, rest_is_sharded):
    bsz, seq, d = x.shape
    t = bsz * seq
    x2 = x.reshape(t, d)
    tgt2 = tgt.reshape(t, d)
    tm = min(256, seq)
    tm_wide = min(128, seq)
    sb = min(512, seq)

    gp = jnp.zeros((SUBLANES, LANES), F32).at[0].set(_lane_row(p["gdn_a_log"], 8)).at[1].set(_lane_row(p["gdn_dt_bias"], 8))
    sp = jnp.zeros((SUBLANES, LANES), F32).at[0].set(_lane_row(p["ssd_a_log"], 16)).at[1].set(_lane_row(p["ssd_dt_bias"], 16))
    dvec = jnp.repeat(p["ssd_d"], SSD_P).reshape(1, d)
    row = lambda v: v.reshape(1, -1)
    pre_mix, post_mix, pre_ffn, post_ffn = (row(p[k]) for k in ("pre_mix_norm", "post_mix_norm", "pre_ffn_norm", "post_ffn_norm"))
    gnw, snw = row(p["gdn_norm_w"]), row(p["ssd_norm_w"])
    gcw, scw, scb, fcw, fcb = p["gdn_conv_w"], p["ssd_conv_w"], row(p["ssd_conv_b"]), p["ffn_conv_w"], row(p["ffn_conv_b"])

    h1 = _norm_cast("norm_in", x2, pre_mix, tm)
    proj = _matmul("mm_proj", h1, wp_in, "nn", F32)
    b3 = lambda a: a.reshape(bsz, seq, a.shape[-1])
    b2 = lambda a: a.reshape(t, a.shape[-1])
    rows_of = lambda a, lo, n: jnp.transpose(a[:, lo:lo + n].reshape(bsz, seq // CHUNK, CHUNK, n), (0, 3, 1, 2))
    qn, kn, vv, gs, ypre_gdn = _gdn_prep(proj, gcw, gp, seq, tm)
    qn, kn, vv, gs = b3(qn), b3(kn), b3(vv), b3(gs)
    gr = jnp.transpose(b2(gs)[:, 8:8 + GDN_HEADS].reshape(bsz, seq // CHUNK, CHUNK, GDN_HEADS), (0, 1, 3, 2))
    (o_gdn, gdn_st, gdn_ti), gathered = _gdn_chunk_fwd(qn, kn, vv, gs, gr, bsz, seq, sb, list(rest) if rest_is_sharded else [])
    if rest_is_sharded:
        w_out, w_up, w_down = gathered[0].reshape(-1, d), _cols_from_shards(gathered[1]), gathered[2].reshape(-1, d)
    else:
        w_out, w_up, w_down = rest
    o_gdn = b2(o_gdn)
    xs, bc, dtx, acsx, acs, ypre_ssd = _ssd_prep(proj, scw, scb, sp, seq, tm)
    ar = rows_of(acs, 16, SSD_HEADS)
    y_ssd, ssd_st = _ssd_chunk_fwd(b3(xs), b3(bc), b3(dtx), b3(acsx), b3(acs), ar, bsz, seq, sb)
    y_ssd = b2(y_ssd)
    mixin = _gate_norm(o_gdn, y_ssd, xs, proj, gnw, snw, dvec, tm)
    mix, x1, h2 = _out_mid(mixin, w_out, x2, post_mix, pre_ffn)
    u_pre = _matmul("mm_up", h2, w_up, "nn", F32)
    act, u = _ffn_act(u_pre, fcw, fcb, seq, tm_wide)
    dy, df, loss_lanes, d_post_ffn = _down_final(act, w_down, x1, tgt2, post_ffn)

    g_down = _matmul("mm_dw_down", act, df, "tn", F32, tm=1408, tk=2048)
    dact = _matmul("mm_dact", df, w_down, "nt", F32, tn=1408)
    du_pre, d_fcw, d_fcb = _ffn_bwd(u, u_pre, dact, fcw, seq, tm_wide)
    g_up = _matmul("mm_dw_up", h2, du_pre, "tn", F32, tk=2048)
    dx1, dmix, d_post_mix, d_pre_ffn = _dh2_mid_bwd(du_pre, w_up, x1, mix, dy, post_mix, pre_ffn)
    g_out = _matmul("mm_dw_out", mixin, dmix, "tn", F32, tk=2048)
    dmixin = _matmul("mm_dmixin", dmix, w_out, "nt", F32)
    do_gdn, dza, dy_ssd, dxs_d, dzs, d_gnw, d_snw, d_dd = _gate_norm_bwd(o_gdn, y_ssd, xs, proj, dmixin, gnw, snw, dvec, tm)
    dxs_c, dbc, ddt, dacs = (b2(a) for a in _ssd_chunk_bwd(
        b3(xs), b3(bc), b3(dtx), b3(acsx), b3(acs), ar, b3(dy_ssd), ssd_st, bsz, seq, sb))
    riders = [_rows_to_shards(g_out), _cols_to_shards(g_up), _rows_to_shards(g_down)] if rest_is_sharded else []
    dgdn, received = _gdn_chunk_bwd(qn, kn, vv, gs, gr, b3(do_gdn), gdn_st, gdn_ti, bsz, seq, min(256, seq), riders)
    if rest_is_sharded:
        g_out, g_up, g_down = received
    dqn, dkn, dvv, dgb = (b2(a) for a in dgdn)
    dpre_gdn, dsm_gdn, d_gp = _gdn_prep_bwd(ypre_gdn, proj, dqn, dkn, dvv, dgb, gp, tm)
    dpre_ssd, dsm, d_scb, d_sp = _ssd_prep_bwd(ypre_ssd, proj, dxs_c, dxs_d, dbc, ddt, dacs, dsm_gdn, sp, tm)
    dproj, d_gcw, d_scw = _assemble_dproj(dpre_gdn, dza, dzs, dpre_ssd, dsm, proj, gcw, scw, seq, tm)
    g_in = _matmul("mm_dw_in", h1, dproj, "tn", F32, tk=2048)
    if rest_is_sharded:
        (dx, d_pre_mix), (g_in,) = _dh1_first_bwd(dproj, wp_in, x2, dx1, pre_mix, [_cols_to_shards(_unpermute_in(g_in))])
    else:
        dx, d_pre_mix = _dh1_first_bwd(dproj, wp_in, x2, dx1, pre_mix, [])

    small = dict(pre_mix_norm=d_pre_mix, ssd_norm_w=d_snw, post_mix_norm=d_post_mix, pre_ffn_norm=d_pre_ffn,
                 post_ffn_norm=d_post_ffn, dd_lanes=d_dd, loss_lanes=loss_lanes, gdn_gates=d_gp, ssd_gates=d_sp,
                 gdn_norm_w=d_gnw, gdn_conv_w=d_gcw[0:4], ssd_conv_w=d_scw[0:4], ssd_conv_b=d_scb,
                 ffn_conv_w=d_fcw[0:3], ffn_conv_b=d_fcb)
    return dx.reshape(bsz, seq, d), g_in, g_out, g_up, g_down, small


def kernel(x, pre_mix_norm, w_in, gdn_conv_w, gdn_a_log, gdn_dt_bias, gdn_norm_w, ssd_conv_w, ssd_conv_b, ssd_a_log, ssd_dt_bias, ssd_d, ssd_norm_w, w_out, post_mix_norm, pre_ffn_norm, w_up, ffn_conv_w, ffn_conv_b, w_down, post_ffn_norm, loss_target, m_pre_mix_norm, m_w_in, m_gdn_conv_w, m_gdn_a_log, m_gdn_dt_bias, m_gdn_norm_w, m_ssd_conv_w, m_ssd_conv_b, m_ssd_a_log, m_ssd_dt_bias, m_ssd_d, m_ssd_norm_w, m_w_out, m_post_mix_norm, m_pre_ffn_norm, m_w_up, m_ffn_conv_w, m_ffn_conv_b, m_w_down, m_post_ffn_norm, v_pre_mix_norm, v_w_in, v_gdn_conv_w, v_gdn_a_log, v_gdn_dt_bias, v_gdn_norm_w, v_ssd_conv_w, v_ssd_conv_b, v_ssd_a_log, v_ssd_dt_bias, v_ssd_d, v_ssd_norm_w, v_w_out, v_post_mix_norm, v_pre_ffn_norm, v_w_up, v_ffn_conv_w, v_ffn_conv_b, v_w_down, v_post_ffn_norm):
    names = ["pre_mix_norm", "w_in", "gdn_conv_w", "gdn_a_log", "gdn_dt_bias", "gdn_norm_w", "ssd_conv_w", "ssd_conv_b",
             "ssd_a_log", "ssd_dt_bias", "ssd_d", "ssd_norm_w", "w_out", "post_mix_norm", "pre_ffn_norm", "w_up",
             "ffn_conv_w", "ffn_conv_b", "w_down", "post_ffn_norm"]
    w_args = [pre_mix_norm, w_in, gdn_conv_w, gdn_a_log, gdn_dt_bias, gdn_norm_w, ssd_conv_w, ssd_conv_b, ssd_a_log, ssd_dt_bias, ssd_d, ssd_norm_w, w_out, post_mix_norm, pre_ffn_norm, w_up, ffn_conv_w, ffn_conv_b, w_down, post_ffn_norm]
    m_args = [m_pre_mix_norm, m_w_in, m_gdn_conv_w, m_gdn_a_log, m_gdn_dt_bias, m_gdn_norm_w, m_ssd_conv_w, m_ssd_conv_b, m_ssd_a_log, m_ssd_dt_bias, m_ssd_d, m_ssd_norm_w, m_w_out, m_post_mix_norm, m_pre_ffn_norm, m_w_up, m_ffn_conv_w, m_ffn_conv_b, m_w_down, m_post_ffn_norm]
    v_args = [v_pre_mix_norm, v_w_in, v_gdn_conv_w, v_gdn_a_log, v_gdn_dt_bias, v_gdn_norm_w, v_ssd_conv_w, v_ssd_conv_b, v_ssd_a_log, v_ssd_dt_bias, v_ssd_d, v_ssd_norm_w, v_w_out, v_post_mix_norm, v_pre_ffn_norm, v_w_up, v_ffn_conv_w, v_ffn_conv_b, v_w_down, v_post_ffn_norm]
    w = {k: a[0] for k, a in zip(names, w_args)}
    m = {k: a[0] for k, a in zip(names, m_args)}
    v = {k: a[0] for k, a in zip(names, v_args)}
    idx = 4 * lax.axis_index("x") + 2 * lax.axis_index("y") + lax.axis_index("c")
    big = ("w_in", "w_out", "w_up", "w_down")
    conv = ("gdn_conv_w", "ssd_conv_w", "ffn_conv_w")

    conv_local = jnp.concatenate([jnp.pad(w[k], ((0, 4 - w[k].shape[0]), (0, 0))) for k in conv], axis=1)
    g_in, g_conv = _gather_two_level("gather_weights", [w["w_in"].astype(BF16), conv_local])
    wp_in = _permute_in(_cols_from_shards(g_in))
    p = {k: w[k] for k in names if k not in big and k not in conv}
    off = 0
    for k in conv:
        cw = w[k].shape[1]
        p[k] = jnp.transpose(g_conv[:, :w[k].shape[0], off:off + cw], (1, 0, 2)).reshape(w[k].shape[0], N_DEV * cw)
        off += cw

    rest = tuple(w[k].astype(BF16) for k in ("w_out", "w_up", "w_down"))
    dx, p_in, p_out, p_up, p_down, small = _local_step(x, loss_target, wp_in, rest, p, True)

    gate_row = jnp.concatenate([small["gdn_gates"][0], small["gdn_gates"][1], small["ssd_gates"][0], small["ssd_gates"][1],
                                small["gdn_norm_w"][0], jnp.zeros((D_MODEL - 5 * LANES,), F32)]).reshape(1, D_MODEL)
    pack = _pack([small["pre_mix_norm"], small["ssd_norm_w"], small["post_mix_norm"], small["pre_ffn_norm"],
                  small["post_ffn_norm"], small["dd_lanes"], small["loss_lanes"], gate_row,
                  small["gdn_conv_w"], small["ssd_conv_w"], jnp.pad(small["ssd_conv_b"], ((0, 0), (0, 512))),
                  jnp.pad(small["ffn_conv_w"].reshape(-1), (0, 17 * D_MODEL - 3 * 2 * D_FF)),
                  jnp.pad(small["ffn_conv_b"], ((0, 0), (0, 512)))], SMALL_ROWS)
    (pack_all,) = _gather_two_level("gather_small", [pack])
    ssum, extra = _small_sum(pack_all)

    grads, deltas, new_m, new_v = {}, {}, {}, {}
    for k, parts in (("w_in", p_in), ("w_out", p_out), ("w_up", p_up), ("w_down", p_down)):
        grads[k], deltas[k], new_m[k], new_v[k] = _adam_big("adam_" + k, parts, w[k], m[k], v[k], 256)

    flat = ssum.reshape(-1)
    gate = ssum[7]
    sg = dict(pre_mix_norm=ssum[0], ssd_norm_w=ssum[1], post_mix_norm=ssum[2], pre_ffn_norm=ssum[3], post_ffn_norm=ssum[4],
              gdn_a_log=gate[8:16], gdn_dt_bias=gate[LANES + 8:LANES + 16], ssd_a_log=gate[2 * LANES + 16:2 * LANES + 32],
              ssd_dt_bias=gate[3 * LANES + 16:3 * LANES + 32], gdn_norm_w=gate[4 * LANES:5 * LANES], ssd_d=extra[0, 0:SSD_HEADS])
    o = 8 * D_MODEL
    full_gcw = flat[o:o + 4 * 3072].reshape(4, 3072)
    o += 12 * D_MODEL
    full_scw = flat[o:o + 4 * 1536].reshape(4, 1536)
    o += 6 * D_MODEL
    sg["ssd_conv_b"] = flat[o:o + 1536]
    o += 2 * D_MODEL
    full_fcw = flat[o:o + 3 * 2 * D_FF].reshape(3, 2 * D_FF)
    o += 17 * D_MODEL
    sg["ffn_conv_b"] = flat[o:o + 2 * D_FF]
    for k, full in (("gdn_conv_w", full_gcw), ("ssd_conv_w", full_scw), ("ffn_conv_w", full_fcw)):
        cw = w[k].shape[1]
        sg[k] = lax.dynamic_slice_in_dim(full, idx * cw, cw, axis=1)
    small_names = [k for k in names if k not in big]
    rows = 24
    gpk = _pack([sg[k] for k in small_names], rows)
    dpk, mpk, vpk = _adam_small(gpk, _pack([w[k] for k in small_names], rows), _pack([m[k] for k in small_names], rows),
                                _pack([v[k] for k in small_names], rows))
    shapes = [w[k].shape for k in small_names]
    for k, g_, d_, m_, v_ in zip(small_names, _unpack(gpk, shapes), _unpack(dpk, shapes), _unpack(mpk, shapes), _unpack(vpk, shapes)):
        grads[k], deltas[k], new_m[k], new_v[k] = g_, d_, m_, v_

    loss = extra[1, 0]
    lead = lambda a: a[None]
    return (loss, dx, *[lead(grads[k]) for k in names], *[lead(deltas[k]) for k in names],
            *[lead(new_m[k]) for k in names], *[lead(new_v[k]) for k in names])
```

```python
import functools

import jax
import jax.numpy as jnp
from jax import lax
from jax.experimental import pallas as pl
from jax.experimental.pallas import tpu as pltpu

F32 = jnp.float32
BF16 = jnp.bfloat16
MXU_DTYPE = jnp.bfloat16
HIGHEST = lax.Precision.HIGHEST
VMEM_LIMIT_V7X = 48 * 1024 * 1024
SUBLANES = 8
LANES = 128

D_MODEL = 1024
GDN_HEADS = 8
GDN_DK = 128
SSD_HEADS = 16
SSD_P = 64
SSD_GROUPS = 2
SSD_HPG = 8
SSD_N = 128
CHUNK = 64
D_FF = 2816
EPS = 1e-6
N_DEV = 8
PROJ_W = 7168
SMALL_CB = 52
D_IN = 6688

ADAM_LR = 0.001
ADAM_B1 = 0.9
ADAM_B2 = 0.999
ADAM_EPS = 1e-08
ADAM_WD = 0.01
ADAM_STEP = 10

NN = (((1,), (0,)), ((), ()))
NT = (((1,), (1,)), ((), ()))
TN = (((0,), (0,)), ((), ()))


def _pcall(body, **kw):
    return pl.pallas_call(body, **kw)


def _mm(a, b, dims=NN):
    return lax.dot_general(a.astype(MXU_DTYPE), b.astype(MXU_DTYPE), dims, preferred_element_type=F32)


def _mmx(a, b, dims=NN):
    return lax.dot_general(a, b, dims, precision=HIGHEST, preferred_element_type=F32)


def _split(a):
    hi = a.astype(MXU_DTYPE)
    return hi, (a - hi.astype(F32)).astype(MXU_DTYPE)


def _mm3(a, b, dims=NN):
    (ah, al), (bh, bl) = _split(a), _split(b)
    dot = lambda p, q: lax.dot_general(p, q, dims, preferred_element_type=F32)
    return dot(ah, bh) + (dot(ah, bl) + dot(al, bh))


def _mmsel(a, sel, dims=NN, terms=2):
    s = sel.astype(MXU_DTYPE)
    out = None
    for _ in range(terms):
        part = a.astype(MXU_DTYPE)
        a = a - part.astype(F32)
        prod = lax.dot_general(part, s, dims, preferred_element_type=F32)
        out = prod if out is None else out + prod
    return out


def _sigmoid(x):
    return 0.5 * jnp.tanh(0.5 * x) + 0.5


def _softplus(x):
    return jnp.maximum(x, 0.0) + jnp.log(1.0 + jnp.exp(-jnp.abs(x)))


def _dsilu(x, s):
    return s * (1.0 + x * (1.0 - s))


def _rowsum(x):
    return jnp.sum(x, axis=1, keepdims=True)


def _colsum(x):
    return jnp.sum(x, axis=0, keepdims=True)


def _pick(dim, pref):
    if dim <= pref:
        return dim
    best = None
    t = LANES
    while t <= pref:
        if dim % t == 0:
            best = t
        t += LANES
    return dim if best is None else best


def _params(sem):
    return pltpu.CompilerParams(dimension_semantics=sem, vmem_limit_bytes=VMEM_LIMIT_V7X)


def _matmul(name, a, b, mode, out_dtype, tm=1024, tn=1024, tk=1024):
    if mode == "nn":
        (m, k), (_, n) = a.shape, b.shape
    elif mode == "nt":
        (m, k), (n, _) = a.shape, b.shape
    else:
        (k, m), (_, n) = a.shape, b.shape
    tm, tn, tk = _pick(m, tm), _pick(n, tn), _pick(k, tk)
    nk = k // tk
    if mode == "tn":
        a_spec = pl.BlockSpec((tk, tm), lambda i, j, kk: (kk, i))
    else:
        a_spec = pl.BlockSpec((tm, tk), lambda i, j, kk: (i, kk))
    if mode == "nt":
        b_spec = pl.BlockSpec((tn, tk), lambda i, j, kk: (j, kk))
    else:
        b_spec = pl.BlockSpec((tk, tn), lambda i, j, kk: (kk, j))
    dims = {"nn": NN, "nt": NT, "tn": TN}[mode]

    def body(a_ref, b_ref, o_ref, *acc):
        if nk == 1:
            o_ref[...] = _mm(a_ref[...], b_ref[...], dims).astype(out_dtype)
            return
        kk = pl.program_id(2)

        @pl.when(kk == 0)
        def _():
            acc[0][...] = jnp.zeros_like(acc[0])

        acc[0][...] += _mm(a_ref[...], b_ref[...], dims)

        @pl.when(kk == nk - 1)
        def _():
            o_ref[...] = acc[0][...].astype(out_dtype)

    return _pcall(
        body, name=name, grid=(m // tm, n // tn, nk),
        in_specs=[a_spec, b_spec],
        out_specs=pl.BlockSpec((tm, tn), lambda i, j, kk: (i, j)),
        out_shape=jax.ShapeDtypeStruct((m, n), out_dtype),
        scratch_shapes=[pltpu.VMEM((tm, tn), F32)] if nk > 1 else [],
        compiler_params=_params(("parallel", "parallel", "arbitrary")),
    )(a, b)


def _matmul_rows(name, a, b, mode, epilogue, row_ins, full_ins, outs, accs=(), tm=512, tk=1024, scatter_riders=()):
    if mode == "nn":
        (m, k), (_, n) = a.shape, b.shape
    else:
        (m, k), (n, _) = a.shape, b.shape
    tm, tk = _pick(m, tm), _pick(k, tk)
    nk = k // tk
    a_spec = pl.BlockSpec((tm, tk), lambda i, kk: (i, kk))
    b_spec = pl.BlockSpec((n, tk), lambda i, kk: (0, kk)) if mode == "nt" else pl.BlockSpec((tk, n), lambda i, kk: (kk, 0))
    dims = NT if mode == "nt" else NN
    n_row, n_full, n_out, n_acc = len(row_ins), len(full_ins), len(outs), len(accs)

    def body(a_ref, b_ref, *rest):
        ins = rest[:n_row + n_full]
        out_refs = rest[n_row + n_full:n_row + n_full + n_out]
        acc_refs = rest[n_row + n_full + n_out:n_row + n_full + n_out + n_acc]
        prod_scr = rest[-1]
        i, kk = pl.program_id(0), pl.program_id(1)

        if n_acc:
            @pl.when((i == 0) & (kk == 0))
            def _():
                for r in acc_refs:
                    r[...] = jnp.zeros_like(r)

        if nk == 1:
            epilogue(_mm(a_ref[...], b_ref[...], dims), *ins, *out_refs, *acc_refs)
            return

        @pl.when(kk == 0)
        def _():
            prod_scr[...] = jnp.zeros_like(prod_scr)

        prod_scr[...] += _mm(a_ref[...], b_ref[...], dims)

        @pl.when(kk == nk - 1)
        def _():
            epilogue(prod_scr[...], *ins, *out_refs, *acc_refs)

    grid = (m // tm, nk)
    riders = list(scatter_riders)
    n_in = 2 + n_row + n_full
    any_spec, rider_shapes, rider_sems, wrap = _riding_exchange(riders, True, n_in, n_out + n_acc, grid)
    row_ins = [r if isinstance(r, tuple) else (r, r.shape[1], 0) for r in row_ins]
    in_specs = [a_spec, b_spec] + [pl.BlockSpec((tm, w), lambda i, kk, cb=cb: (i, cb)) for _, w, cb in row_ins]
    in_specs += [pl.BlockSpec(f.shape, lambda i, kk, nd=f.ndim: (0,) * nd) for f in full_ins]
    row_ins = [r for r, _, _ in row_ins]
    out_specs = [pl.BlockSpec((tm, w), lambda i, kk: (i, 0)) for w, _ in outs]
    out_specs += [pl.BlockSpec(s, lambda i, kk: (0, 0)) for s in accs]
    out_shape = [jax.ShapeDtypeStruct((m, w), dt) for w, dt in outs] + [jax.ShapeDtypeStruct(s, F32) for s in accs]
    res = _pcall(
        wrap(body), name=name, grid=grid,
        in_specs=in_specs + any_spec, out_specs=out_specs + any_spec, out_shape=out_shape + rider_shapes,
        scratch_shapes=[pltpu.VMEM((tm, n), F32)] + rider_sems,
        compiler_params=_params(("arbitrary", "arbitrary")),
    )(a, b, *row_ins, *full_ins, *riders)
    return (res[:n_out + n_acc], res[n_out + n_acc:]) if riders else res


def _rowwise(name, body, n_rows, tm, ins, outs, accs=()):
    arrays, in_specs = [], []
    last8 = n_rows // SUBLANES - 1
    per = tm // SUBLANES
    for spec in ins:
        kind, arr = spec[0], spec[1]
        if kind == "full":
            in_specs.append(pl.BlockSpec(arr.shape, lambda i, nd=arr.ndim: (0,) * nd))
        else:
            w, cb = spec[2], spec[3]
            if kind == "row":
                in_specs.append(pl.BlockSpec((tm, w), lambda i, cb=cb: (i, cb)))
            elif kind == "prev":
                in_specs.append(pl.BlockSpec((SUBLANES, w), lambda i, cb=cb: (jnp.maximum(i * per - 1, 0), cb)))
            else:
                in_specs.append(pl.BlockSpec((SUBLANES, w), lambda i, cb=cb: (jnp.minimum((i + 1) * per, last8), cb)))
        arrays.append(arr)
    out_shape = [jax.ShapeDtypeStruct((n_rows, w), dt) for (w, dt) in outs]
    out_shape += [jax.ShapeDtypeStruct(s, F32) for s in accs]
    out_specs = [pl.BlockSpec((tm, w), lambda i: (i, 0)) for (w, _) in outs]
    out_specs += [pl.BlockSpec(s, lambda i: (0, 0)) for s in accs]
    n_io = len(ins) + len(outs)

    def kern(*refs):
        i = pl.program_id(0)
        if accs:
            @pl.when(i == 0)
            def _():
                for r in refs[n_io:]:
                    r[...] = jnp.zeros_like(r)
        body(i, *refs)

    res = _pcall(
        kern, name=name, grid=(n_rows // tm,), in_specs=in_specs, out_specs=out_specs, out_shape=out_shape,
        compiler_params=_params(("arbitrary",)),
    )(*arrays)
    return res


def _shift_down(x, halo, j):
    r = pltpu.roll(x, j, 0)
    hr = pltpu.roll(halo, j, 0)
    rows = lax.broadcasted_iota(jnp.int32, (SUBLANES, x.shape[1]), 0)
    top = jnp.where(rows < j, hr, r[0:SUBLANES])
    return jnp.concatenate([top, r[SUBLANES:]], axis=0)


def _shift_up(x, halo, j):
    tm = x.shape[0]
    r = pltpu.roll(x, tm - j, 0)
    hr = pltpu.roll(halo, SUBLANES - j, 0)
    rows = lax.broadcasted_iota(jnp.int32, (SUBLANES, x.shape[1]), 0)
    bot = jnp.where(rows >= SUBLANES - j, hr, r[tm - SUBLANES:])
    return jnp.concatenate([r[:tm - SUBLANES], bot], axis=0)


def _conv_taps(x, halo, kw):
    return [x if kw - 1 - k == 0 else _shift_down(x, halo, kw - 1 - k) for k in range(kw)]


def _conv(taps, w):
    y = taps[0] * w[0:1]
    for k in range(1, len(taps)):
        y = y + taps[k] * w[k:k + 1]
    return y


def _rms(x, width):
    r = lax.rsqrt(jnp.sum(x * x, axis=-1, keepdims=True) * (1.0 / width) + EPS)
    return x * r, r


def _rms_bwd(xh, r, dxh, width):
    return r * (dxh - xh * (jnp.sum(dxh * xh, axis=-1, keepdims=True) * (1.0 / width)))


def _seq_flags(i, seq, tm):
    nps = seq // tm
    pos = i % nps
    return jnp.where(pos == 0, 0.0, 1.0), jnp.where(pos == nps - 1, 0.0, 1.0)


def _norm_cast(name, x, w, tm):
    t, d = x.shape

    def body(i, x_ref, w_ref, h_ref):
        xh, _ = _rms(x_ref[...], d)
        h_ref[...] = (xh * w_ref[...]).astype(BF16)

    return _rowwise(name, body, t, tm, [("row", x, d, 0), ("full", w)], [(d, BF16)])[0]


def _gdn_prep(proj, cw, gp, seq, tm):
    t = proj.shape[0]
    d = D_MODEL

    def body(i, q_ref, qh_ref, k_ref, kh_ref, v_ref, vh_ref, sm_ref, cw_ref, gp_ref, qn_ref, kn_ref, vv_ref, gs_ref, ypre_ref):
        keep, _ = _seq_flags(i, seq, tm)
        for x_ref, h_ref, o_ref, off, scale in ((q_ref, qh_ref, qn_ref, 0, GDN_DK ** -0.5),
                                               (k_ref, kh_ref, kn_ref, d, 1.0), (v_ref, vh_ref, vv_ref, 2 * d, None)):
            y = _conv(_conv_taps(x_ref[...], h_ref[...] * keep, 4), cw_ref[:, off:off + d])
            ypre_ref[:, off:off + d] = y
            a = y * _sigmoid(y)
            if scale is None:
                o_ref[...] = a
            else:
                for hh in range(GDN_HEADS):
                    s = a[:, hh * GDN_DK:(hh + 1) * GDN_DK]
                    n = lax.rsqrt(_rowsum(s * s) + EPS)
                    o_ref[:, hh * GDN_DK:(hh + 1) * GDN_DK] = s * (n * scale)
        sm = sm_ref[...]
        lane = lax.broadcasted_iota(jnp.int32, sm.shape, 1)
        beta = _sigmoid(sm)
        g = jnp.where((lane >= 8) & (lane < 16), -jnp.exp(gp_ref[0:1, :]) * _softplus(sm + gp_ref[1:2, :]), 0.0)
        gs_ref[...] = jnp.where(lane < 8, beta, _mmx(_block_tri(tm, False), g))

    ins = []
    for cb in range(3):
        ins += [("row", proj, d, cb), ("prev", proj, d, cb)]
    ins += [("row", proj, LANES, SMALL_CB), ("full", cw), ("full", gp)]
    return _rowwise("gdn_prep", body, t, tm, ins, [(d, F32), (d, F32), (d, F32), (LANES, F32), (3 * d, F32)])


def _block_tri(tm, upper):
    ri = lax.broadcasted_iota(jnp.int32, (tm, tm), 0)
    ci = lax.broadcasted_iota(jnp.int32, (tm, tm), 1)
    tri = (ri <= ci) if upper else (ri >= ci)
    return (tri & ((ri // CHUNK) == (ci // CHUNK))).astype(F32)


def _chunk_consts():
    row = lax.broadcasted_iota(jnp.int32, (CHUNK, CHUNK), 0)
    col = lax.broadcasted_iota(jnp.int32, (CHUNK, CHUNK), 1)
    return dict(
        tril=row >= col, strict=row > col, eye=(row == col).astype(F32),
        lane=lax.broadcasted_iota(jnp.int32, (CHUNK, LANES), 1),
        row1=lax.broadcasted_iota(jnp.int32, (CHUNK, 1), 0),
        ones=jnp.ones((CHUNK, LANES), F32))


def _hmap(fn, *lists):
    return [fn(*a) for a in zip(*lists)]


def _tri_inv(nmats, eye):
    x = [eye - n for n in nmats]
    p = _hmap(_mm3, nmats, nmats)
    for lvl in range(5):
        x = _hmap(lambda xi, pi: xi + _mm3(xi, pi), x, p)
        if lvl < 4:
            p = _hmap(_mm3, p, p)
    return x


def _gdn_gates(gs, gc_row, h, c):
    beta = _rowsum(jnp.where(c["lane"] == h, gs, 0.0))
    gc = _rowsum(jnp.where(c["lane"] == h + 8, gs, 0.0))
    dc = jnp.exp(jnp.where(c["tril"], gc - gc_row, -1e30))
    gl = gc[CHUNK - 1:CHUNK, :]
    return beta, dc, jnp.exp(gc), jnp.exp(gl), jnp.exp(gl - gc)


GDN_HB = GDN_HEADS


def _gdn_specs(seq, sb, hb, backward):
    assert hb == GDN_HEADS
    nsb = seq // sb
    ncb = sb // CHUNK
    order = (lambda j: nsb - 1 - j) if backward else (lambda j: j)
    specs = dict(
        wide=lambda: pl.BlockSpec((1, sb, hb * GDN_DK), lambda b, h, j: (b, order(j), h)),
        gs=pl.BlockSpec((1, sb, LANES), lambda b, h, j: (b, order(j), 0)),
        gr=pl.BlockSpec((1, ncb, GDN_HEADS, CHUNK), lambda b, h, j: (b, order(j), 0, 0)),
        st=pl.BlockSpec((1, hb, ncb * GDN_DK, GDN_DK), lambda b, h, j: (b, h, order(j), 0)),
        ti=pl.BlockSpec((1, hb, sb, CHUNK), lambda b, h, j: (b, h, order(j), 0)))
    return nsb, ncb, specs


def _riding_exchange(arrays, scatter, n_in, n_out, grid):
    n = len(arrays)
    if n == 0:
        return [], [], [], lambda body: body
    any_spec = [pl.BlockSpec(memory_space=pl.ANY)] * n

    def wrap(body):
        def wrapped(*refs):
            ins = refs[n_in:n_in + n]
            outs = refs[n_in + n + n_out:n_in + 2 * n + n_out]
            sems = refs[len(refs) - 3:]
            pid = [pl.program_id(a) for a in range(len(grid))]
            first = functools.reduce(lambda a, b: a & b, [p == 0 for p in pid])
            last = functools.reduce(lambda a, b: a & b, [p == g - 1 for p, g in zip(pid, grid)])

            @pl.when(first)
            def _():
                _exchange_phase(ins, outs, sems, scatter, start=True)

            body(*refs[:n_in], *refs[n_in + n:n_in + n + n_out], *refs[n_in + 2 * n + n_out:len(refs) - 3])

            @pl.when(last)
            def _():
                _exchange_phase(ins, outs, sems, scatter, start=False)

        return wrapped

    return any_spec, _exchange_out_shapes(arrays, scatter), _exchange_sems(n), wrap


def _gdn_chunk_fwd(qn, kn, vv, gs, gr, bsz, seq, sb, riders):
    hb = GDN_HB
    nsb, ncb, sp = _gdn_specs(seq, sb, hb, False)
    grid = (bsz, GDN_HEADS // hb, nsb)
    any_spec, rider_shapes, rider_sems, wrap = _riding_exchange(riders, False, 5, 3, grid)

    def body(q_ref, k_ref, v_ref, gs_ref, gr_ref, o_ref, st_ref, ti_ref, s_scr):
        hg = pl.program_id(1)

        @pl.when(pl.program_id(2) == 0)
        def _():
            s_scr[...] = jnp.zeros_like(s_scr)

        c = _chunk_consts()

        def chunk(n, carry):
            r = pl.ds(pl.multiple_of(n * CHUNK, CHUNK), CHUNK)
            rs = pl.ds(pl.multiple_of(n * GDN_DK, GDN_DK), GDN_DK)
            gsv = gs_ref[0, r, :]
            heads = list(range(hb))
            sls = [slice(ih * GDN_DK, (ih + 1) * GDN_DK) for ih in heads]
            q = [q_ref[0, r, sl] for sl in sls]
            k = [k_ref[0, r, sl] for sl in sls]
            v = [v_ref[0, r, sl] for sl in sls]
            beta, dc, eg, egl, ekd = zip(*[
                _gdn_gates(gsv, gr_ref[0, n, pl.ds(ih, 1), :], ih, c) for ih in heads])
            kb = _hmap(lambda a, b: a * b, k, beta)
            amat = _hmap(lambda a, b, d_: jnp.where(c["strict"], _mm(a, b, NT) * d_, 0.0), kb, k, dc)
            tinv = _tri_inv(amat, c["eye"])
            u = _hmap(lambda t_, a, b: _mm3(t_, a * b), tinv, v, beta)
            w = _hmap(lambda t_, a, b: _mm3(t_, a * b), tinv, kb, eg)
            qk = _hmap(lambda a, b, d_: _mm(a, b, NT) * d_, q, k, dc)
            s = [s_scr[ih] for ih in heads]
            v_new = _hmap(lambda a, b, s_: a - _mm(b, s_), u, w, s)
            o = _hmap(lambda a, e, s_, qk_, vn: _mm(a * e, s_) + _mm(qk_, vn), q, eg, s, qk, v_new)
            s_new = _hmap(lambda s_, e, a, f, vn: s_ * e + _mm(a * f, vn, TN), s, egl, k, ekd, v_new)
            for ih in heads:
                o_ref[0, r, sls[ih]] = o[ih]
                st_ref[0, ih, rs, :] = s[ih]
                ti_ref[0, ih, r, :] = tinv[ih]
                s_scr[ih] = s_new[ih]
            return carry

        lax.fori_loop(0, ncb, chunk, 0)

    t3 = (bsz, seq, D_MODEL)
    res = _pcall(
        wrap(body), name="gdn_chunk_fwd", grid=grid,
        in_specs=[sp["wide"](), sp["wide"](), sp["wide"](), sp["gs"], sp["gr"]] + any_spec,
        out_specs=[sp["wide"](), sp["st"], sp["ti"]] + any_spec,
        out_shape=[jax.ShapeDtypeStruct(t3, F32),
                   jax.ShapeDtypeStruct((bsz, GDN_HEADS, (seq // CHUNK) * GDN_DK, GDN_DK), F32),
                   jax.ShapeDtypeStruct((bsz, GDN_HEADS, seq, CHUNK), F32)] + rider_shapes,
        scratch_shapes=[pltpu.VMEM((hb, GDN_DK, GDN_DK), F32)] + rider_sems,
        compiler_params=_params(("arbitrary", "arbitrary", "arbitrary")),
    )(qn, kn, vv, gs, gr, *riders)
    return res[:3], res[3:]


def _ssd_prep(proj, cw, cb, sp, seq, tm):
    t = proj.shape[0]
    d = D_MODEL
    ssd_w = SSD_HEADS * SSD_P

    def body(i, x_ref, xh_ref, bc_ref, bch_ref, sm_ref, cw_ref, cb_ref, sp_ref, xs_ref, bco_ref, dtx_ref, acsx_ref, acs_ref, ypre_ref):
        keep, _ = _seq_flags(i, seq, tm)
        y = _conv(_conv_taps(x_ref[...], xh_ref[...] * keep, 4), cw_ref[:, 0:d]) + cb_ref[:, 0:d]
        ypre_ref[:, 0:d] = y
        xs_ref[...] = y * _sigmoid(y)
        y = _conv(_conv_taps(bc_ref[...], bch_ref[...] * keep, 4), cw_ref[:, d:d + 512]) + cb_ref[:, d:d + 512]
        ypre_ref[:, d:d + 512] = y
        bco_ref[...] = y * _sigmoid(y)
        sm = sm_ref[...]
        lane = lax.broadcasted_iota(jnp.int32, sm.shape, 1)
        valid = (lane >= 16) & (lane < 32)
        dt = jnp.where(valid, _softplus(sm + sp_ref[1:2, :]), 0.0)
        adt = dt * (-jnp.exp(sp_ref[0:1, :]))
        acs = _mmx(_block_tri(tm, False), adt)
        l64 = lax.broadcasted_iota(jnp.int32, (LANES, ssd_w), 0)
        d64 = lax.broadcasted_iota(jnp.int32, (LANES, ssd_w), 1)
        e64 = (l64 - 16 == d64 // SSD_P).astype(F32)
        dtx_ref[...] = _mmsel(dt, e64, terms=3)
        acsx_ref[...] = _mmsel(acs, e64, terms=3)
        acs_ref[...] = acs

    ins = [("row", proj, d, 5), ("prev", proj, d, 5), ("row", proj, 512, 12), ("prev", proj, 512, 12),
           ("row", proj, LANES, SMALL_CB), ("full", cw), ("full", cb), ("full", sp)]
    return _rowwise("ssd_prep", body, t, tm, ins,
                    [(d, F32), (512, F32), (ssd_w, F32), (ssd_w, F32), (LANES, F32), (d + 512, F32)])


SSD_GW = SSD_HPG * SSD_P


def _ssd_head(acs, ar_ref, n, head, cbm, c):
    col = _rowsum(jnp.where(c["lane"] == head + 16, acs, 0.0))
    lm = jnp.exp(jnp.where(c["tril"], col - ar_ref[0, head, pl.ds(n, 1), :], -1e30))
    return lm, cbm * lm


def _ssd_specs(seq, sb):
    nsb = seq // sb
    ncb = sb // CHUNK
    def specs(order):
        return dict(
            wide=lambda: pl.BlockSpec((1, sb, SSD_HEADS * SSD_P), lambda b, j: (b, order(j), 0)),
            bc=lambda: pl.BlockSpec((1, sb, 2 * SSD_GROUPS * SSD_N), lambda b, j: (b, order(j), 0)),
            half=lambda: pl.BlockSpec((1, sb, SSD_GROUPS * SSD_N), lambda b, j: (b, order(j), 0)),
            small=lambda: pl.BlockSpec((1, sb, LANES), lambda b, j: (b, order(j), 0)),
            ar=pl.BlockSpec((1, SSD_HEADS, ncb, CHUNK), lambda b, j: (b, 0, order(j), 0)),
            st=pl.BlockSpec((1, ncb * SSD_N, SSD_HEADS * SSD_P), lambda b, j: (b, order(j), 0)))
    return nsb, ncb, specs(lambda j: j), specs(lambda j: nsb - 1 - j)


def _ssd_chunk_fwd(xs, bc, dtx, acsx, acs, ar, bsz, seq, sb):
    nsb, ncb, sp, _ = _ssd_specs(seq, sb)

    def body(x_ref, dtx_ref, ax_ref, bc_ref, acs_ref, ar_ref, y_ref, sts_ref, st_scr):
        @pl.when(pl.program_id(1) == 0)
        def _():
            st_scr[...] = jnp.zeros_like(st_scr)

        c = _chunk_consts()
        lane5 = lax.broadcasted_iota(jnp.int32, (CHUNK, SSD_GW), 1) // SSD_P

        def chunk(n, carry):
            r = pl.ds(pl.multiple_of(n * CHUNK, CHUNK), CHUNK)
            rs = pl.ds(pl.multiple_of(n * SSD_N, SSD_N), SSD_N)
            acsv = acs_ref[0, r, :]
            for g in range(SSD_GROUPS):
                gl = slice(g * SSD_GW, (g + 1) * SSD_GW)
                x, dt, ax = x_ref[0, r, gl], dtx_ref[0, r, gl], ax_ref[0, r, gl]
                bm = bc_ref[0, r, g * SSD_N:(g + 1) * SSD_N]
                cm = bc_ref[0, r, (SSD_GROUPS + g) * SSD_N:(SSD_GROUPS + g + 1) * SSD_N]
                xdt = x * dt
                cbm = _mm(cm, bm, NT)
                al = ax[CHUNK - 1:CHUNK, :]
                st = st_scr[:, gl]
                y = _mm(cm, st) * jnp.exp(ax)
                for hh in range(SSD_HPG):
                    _, gm = _ssd_head(acsv, ar_ref, n, g * SSD_HPG + hh, cbm, c)
                    y = y + _mm(gm, jnp.where(lane5 == hh, xdt, 0.0))
                y_ref[0, r, gl] = y
                sts_ref[0, rs, gl] = st
                st_scr[:, gl] = st * jnp.exp(al) + _mm(bm, xdt * jnp.exp(al - ax), TN)
            return carry

        lax.fori_loop(0, ncb, chunk, 0)

    return _pcall(
        body, name="ssd_chunk_fwd", grid=(bsz, nsb),
        in_specs=[sp["wide"](), sp["wide"](), sp["wide"](), sp["bc"](), sp["small"](), sp["ar"]],
        out_specs=[sp["wide"](), sp["st"]],
        out_shape=[jax.ShapeDtypeStruct((bsz, seq, SSD_HEADS * SSD_P), F32),
                   jax.ShapeDtypeStruct((bsz, (seq // CHUNK) * SSD_N, SSD_HEADS * SSD_P), F32)],
        scratch_shapes=[pltpu.VMEM((SSD_N, SSD_HEADS * SSD_P), F32)],
        compiler_params=_params(("parallel", "arbitrary")),
    )(xs, dtx, acsx, bc, acs, ar)


def _gate_norm(o_gdn, y_ssd, xs, proj, gnw, snw, dvec, tm):
    t = o_gdn.shape[0]
    d = D_MODEL

    def body(i, o_ref, za_ref, y_ref, xs_ref, zs_ref, gnw_ref, snw_ref, dv_ref, out_ref):
        for hh in range(GDN_HEADS):
            sl = slice(hh * GDN_DK, (hh + 1) * GDN_DK)
            oh, _ = _rms(o_ref[:, sl], GDN_DK)
            z = za_ref[:, sl]
            out_ref[:, sl] = (oh * gnw_ref[...] * (z * _sigmoid(z))).astype(BF16)
        zs = zs_ref[...]
        yg = (y_ref[...] + dv_ref[...] * xs_ref[...]) * (zs * _sigmoid(zs))
        for g in range(SSD_GROUPS):
            sl = slice(g * 512, (g + 1) * 512)
            yh, _ = _rms(yg[:, sl], 512)
            out_ref[:, d + g * 512:d + (g + 1) * 512] = (yh * snw_ref[:, sl]).astype(BF16)

    ins = [("row", o_gdn, d, 0), ("row", proj, d, 3), ("row", y_ssd, d, 0), ("row", xs, d, 0), ("row", proj, d, 4),
           ("full", gnw), ("full", snw), ("full", dvec)]
    return _rowwise("gate_norm", body, t, tm, ins, [(2 * d, BF16)])[0]


def _out_mid(mixin, w_out, x, pmw, pfw):
    d = D_MODEL

    def epilogue(mix, x_ref, pmw_ref, pfw_ref, mix_ref, x1_ref, h2_ref):
        mix_ref[...] = mix
        mh, _ = _rms(mix, d)
        x1 = x_ref[...] + mh * pmw_ref[...]
        x1_ref[...] = x1
        xh, _ = _rms(x1, d)
        h2_ref[...] = (xh * pfw_ref[...]).astype(BF16)

    return _matmul_rows("mm_out_mid", mixin, w_out, "nn", epilogue, [x], [pmw, pfw], [(d, F32), (d, F32), (d, BF16)],
                        tk=2 * d)


def _ffn_act(u_pre, cw, cb, seq, tm):
    t = u_pre.shape[0]

    def body(i, ug_ref, ugh_ref, uu_ref, uuh_ref, cw_ref, cb_ref, act_ref, u_ref):
        keep, _ = _seq_flags(i, seq, tm)
        gate = _conv(_conv_taps(ug_ref[...], ugh_ref[...] * keep, 3), cw_ref[:, 0:D_FF]) + cb_ref[:, 0:D_FF]
        up = _conv(_conv_taps(uu_ref[...], uuh_ref[...] * keep, 3), cw_ref[:, D_FF:2 * D_FF]) + cb_ref[:, D_FF:2 * D_FF]
        u_ref[:, 0:D_FF] = gate
        u_ref[:, D_FF:2 * D_FF] = up
        act_ref[...] = (gate * _sigmoid(gate) * up).astype(BF16)

    ins = [("row", u_pre, D_FF, 0), ("prev", u_pre, D_FF, 0), ("row", u_pre, D_FF, 1), ("prev", u_pre, D_FF, 1),
           ("full", cw), ("full", cb)]
    return _rowwise("ffn_act", body, t, tm, ins, [(D_FF, BF16), (2 * D_FF, F32)])


def _down_final(act, w_down, x1, tgt, w):
    d = D_MODEL

    def epilogue(f, x1_ref, t_ref, w_ref, dy_ref, df_ref, loss_ref, dw_ref):
        fh, r = _rms(f, d)
        e = x1_ref[...] + fh * w_ref[...] - t_ref[...]
        loss_ref[...] += _colsum(e * e) * (0.5 / d)
        dy = e * (1.0 / d)
        dy_ref[...] = dy
        dw_ref[...] += _colsum(dy * fh)
        df_ref[...] = _rms_bwd(fh, r, dy * w_ref[...], d).astype(BF16)

    return _matmul_rows("mm_down_final", act, w_down, "nn", epilogue, [x1, tgt], [w], [(d, F32), (d, BF16)],
                        accs=[(1, d), (1, d)], tk=D_FF)


def _ffn_bwd(u, u_pre, dact, cw, seq, tm):
    t = u.shape[0]

    def body(i, g_ref, gn_ref, up_ref, upn_ref, xg_ref, xu_ref, da_ref, dan_ref, cw_ref, dpre_ref, dcw_ref, dcb_ref):
        _, keep_next = _seq_flags(i, seq, tm)
        ext = lambda a_ref, n_ref: jnp.concatenate([a_ref[...], n_ref[...]], axis=0)
        rows = tm + SUBLANES
        gate, up = ext(g_ref, gn_ref), ext(up_ref, upn_ref)
        sg = _sigmoid(gate)
        da = jnp.concatenate([da_ref[...], dan_ref[...] * keep_next], axis=0)
        for off, grad, x_ref in ((0, da * up * _dsilu(gate, sg), xg_ref), (D_FF, da * gate * sg, xu_ref)):
            x = x_ref[...]
            own = grad[0:tm]
            acc = own * cw_ref[2:3, off:off + D_FF]
            dcb_ref[:, off:off + D_FF] += _colsum(own)
            dcw_ref[2:3, off:off + D_FF] += _colsum(own * x)
            for j in (1, 2):
                ahead = pltpu.roll(grad, rows - j, 0)[0:tm]
                acc = acc + ahead * cw_ref[2 - j:3 - j, off:off + D_FF]
                dcw_ref[2 - j:3 - j, off:off + D_FF] += _colsum(ahead * x)
            dpre_ref[:, off:off + D_FF] = acc.astype(BF16)

    ins = []
    for cb_ in range(2):
        ins += [("row", u, D_FF, cb_), ("next", u, D_FF, cb_)]
    ins += [("row", u_pre, D_FF, 0), ("row", u_pre, D_FF, 1), ("row", dact, D_FF, 0), ("next", dact, D_FF, 0), ("full", cw)]
    return _rowwise("ffn_bwd", body, t, tm, ins, [(2 * D_FF, BF16)], accs=[(SUBLANES, 2 * D_FF), (1, 2 * D_FF)])


def _assemble_dproj(dpre_gdn, dza, dzs, dpre_ssd, dsm, proj, gcw, scw, seq, tm):
    t = dpre_gdn.shape[0]
    d = D_MODEL

    def body(i, dg_ref, dgn_ref, dza_ref, dzs_ref, ds_ref, dsn_ref, dsm_ref, xq_ref, xk_ref, xv_ref, xx_ref, xbc_ref,
             gcw_ref, scw_ref, o_ref, dgcw_ref, dscw_ref):
        _, keep = _seq_flags(i, seq, tm)
        pieces = [(dg_ref, dgn_ref, gcw_ref, dgcw_ref, x_ref, 0, c0)
                  for x_ref, c0 in ((xq_ref, 0), (xk_ref, d), (xv_ref, 2 * d))]
        pieces += [(ds_ref, dsn_ref, scw_ref, dscw_ref, x_ref, 5 * d, c0) for x_ref, c0 in ((xx_ref, 0), (xbc_ref, d))]
        for d_ref, n_ref, cw_ref, dcw_ref, x_ref, base, c0 in pieces:
            w = x_ref.shape[1]
            x = x_ref[...]
            g = d_ref[:, c0:c0 + w]
            halo = n_ref[:, c0:c0 + w] * keep
            acc = g * cw_ref[3:4, c0:c0 + w]
            dcw_ref[3:4, c0:c0 + w] += _colsum(g * x)
            for j in range(1, 4):
                ahead = _shift_up(g, halo, j)
                acc = acc + ahead * cw_ref[3 - j:4 - j, c0:c0 + w]
                dcw_ref[3 - j:4 - j, c0:c0 + w] += _colsum(ahead * x)
            o_ref[:, base + c0:base + c0 + w] = acc.astype(BF16)
        o_ref[:, 3 * d:4 * d] = dza_ref[...]
        o_ref[:, 4 * d:5 * d] = dzs_ref[...]
        o_ref[:, 6 * d + 512:6 * d + 512 + LANES] = dsm_ref[...]
        o_ref[:, 6 * d + 512 + LANES:PROJ_W] = jnp.zeros((tm, PROJ_W - (6 * d + 512 + LANES)), BF16)

    ins = [("row", dpre_gdn, 3 * d, 0), ("next", dpre_gdn, 3 * d, 0), ("row", dza, d, 0), ("row", dzs, d, 0),
           ("row", dpre_ssd, d + 512, 0), ("next", dpre_ssd, d + 512, 0), ("row", dsm, LANES, 0),
           ("row", proj, d, 0), ("row", proj, d, 1), ("row", proj, d, 2), ("row", proj, d, 5), ("row", proj, 512, 12),
           ("full", gcw), ("full", scw)]
    return _rowwise("assemble_dproj", body, t, tm, ins, [(PROJ_W, BF16)], accs=[(SUBLANES, 3 * d), (SUBLANES, d + 512)])


def _dh2_mid_bwd(du_pre, w_up, x1, mix, dy, pmw, pfw):
    d = D_MODEL

    def epilogue(dh2, x1_ref, mix_ref, dy_ref, pmw_ref, pfw_ref, dx1_ref, dmix_ref, dpm_ref, dpf_ref):
        xh, r2 = _rms(x1_ref[...], d)
        dpf_ref[...] += _colsum(dh2 * xh)
        dx1 = dy_ref[...] + _rms_bwd(xh, r2, dh2 * pfw_ref[...], d)
        dx1_ref[...] = dx1
        mh, r = _rms(mix_ref[...], d)
        dpm_ref[...] += _colsum(dx1 * mh)
        dmix_ref[...] = _rms_bwd(mh, r, dx1 * pmw_ref[...], d).astype(BF16)

    return _matmul_rows("mm_dh2_mid_bwd", du_pre, w_up, "nt", epilogue, [x1, mix, dy], [pmw, pfw],
                        [(d, F32), (d, BF16)], accs=[(1, d), (1, d)], tk=D_FF)


def _dmixin_gate_norm_bwd(dmix, w_out, o_gdn, y_ssd, xs, proj, gnw, snw, dvec):
    d = D_MODEL

    def epilogue(dmixin, o_ref, za_ref, y_ref, xs_ref, zs_ref, gnw_ref, snw_ref, dv_ref,
                 do_ref, dza_ref, dy_ref, dxs_ref, dzs_ref, dgnw_ref, dsnw_ref, dd_ref):
        for hh in range(GDN_HEADS):
            sl = slice(hh * GDN_DK, (hh + 1) * GDN_DK)
            oh, r = _rms(o_ref[:, sl], GDN_DK)
            z = za_ref[:, sl]
            sz = _sigmoid(z)
            dm = dmixin[:, sl]
            don = dm * (z * sz)
            dza_ref[:, sl] = (dm * oh * gnw_ref[...] * _dsilu(z, sz)).astype(BF16)
            dgnw_ref[...] += _colsum(don * oh)
            do_ref[:, sl] = _rms_bwd(oh, r, don * gnw_ref[...], GDN_DK)
        zs = zs_ref[...]
        sz = _sigmoid(zs)
        sil = zs * sz
        x = xs_ref[...]
        y0 = y_ref[...] + dv_ref[...] * x
        yg = y0 * sil
        dms = dmixin[:, d:2 * d]
        for g in range(SSD_GROUPS):
            sl = slice(g * 512, (g + 1) * 512)
            yh, r = _rms(yg[:, sl], 512)
            dsnw_ref[:, sl] += _colsum(dms[:, sl] * yh)
            dyg = _rms_bwd(yh, r, dms[:, sl] * snw_ref[:, sl], 512)
            dy0 = dyg * sil[:, sl]
            dzs_ref[:, sl] = (dyg * y0[:, sl] * _dsilu(zs[:, sl], sz[:, sl])).astype(BF16)
            dy_ref[:, sl] = dy0
            dxs_ref[:, sl] = dy0 * dv_ref[:, sl]
            dd_ref[:, sl] += _colsum(dy0 * x[:, sl])

    row_ins = [o_gdn, (proj, d, 3), y_ssd, xs, (proj, d, 4)]
    return _matmul_rows("mm_dmixin_gate_norm_bwd", dmix, w_out, "nt", epilogue, row_ins, [gnw, snw, dvec],
                        [(d, F32), (d, BF16), (d, F32), (d, F32), (d, BF16)], accs=[(1, GDN_DK), (1, d), (1, d)], tm=256)


def _ssd_chunk_bwd(xs, bc, dtx, acsx, acs, ar, dy, sts, bsz, seq, sb):
    nsb, ncb, _, sp = _ssd_specs(seq, sb)

    def body(x_ref, dtx_ref, ax_ref, bc_ref, acs_ref, ar_ref, dy_ref, sts_ref, dx_ref, dbc_ref, ddt_ref, dacs_ref, dst_scr):
        @pl.when(pl.program_id(1) == 0)
        def _():
            dst_scr[...] = jnp.zeros_like(dst_scr)

        c = _chunk_consts()
        lane5 = lax.broadcasted_iota(jnp.int32, (CHUNK, SSD_GW), 1) // SSD_P
        row5 = lax.broadcasted_iota(jnp.int32, (CHUNK, SSD_GW), 0)
        sel_in = lax.broadcasted_iota(jnp.int32, (SSD_GW, LANES), 0) // SSD_P
        sel_out = lax.broadcasted_iota(jnp.int32, (SSD_GW, LANES), 1)

        def chunk(nn, carry):
            n = ncb - 1 - nn
            r = pl.ds(pl.multiple_of(n * CHUNK, CHUNK), CHUNK)
            rs = pl.ds(pl.multiple_of(n * SSD_N, SSD_N), SSD_N)
            acsv = acs_ref[0, r, :]
            ddt = jnp.zeros((CHUNK, LANES), F32)
            dacs = jnp.zeros((CHUNK, LANES), F32)
            for g in range(SSD_GROUPS):
                gl = slice(g * SSD_GW, (g + 1) * SSD_GW)
                x, dt, ax, dyv = x_ref[0, r, gl], dtx_ref[0, r, gl], ax_ref[0, r, gl], dy_ref[0, r, gl]
                bm = bc_ref[0, r, g * SSD_N:(g + 1) * SSD_N]
                cm = bc_ref[0, r, (SSD_GROUPS + g) * SSD_N:(SSD_GROUPS + g + 1) * SSD_N]
                st = sts_ref[0, rs, gl]
                dst = dst_scr[:, gl]
                rsel = (sel_in + (16 + g * SSD_HPG) == sel_out).astype(F32)
                xdt = x * dt
                cbm = _mm(cm, bm, NT)
                al = ax[CHUNK - 1:CHUNK, :]
                ex, el = jnp.exp(ax), jnp.exp(al)
                dec = jnp.exp(al - ax)
                xd = xdt * dec
                dye = dyv * ex
                dxd = _mm(bm, dst)
                dxdt = dec * dxd
                dcm = _mm(dye, st, NT)
                dbm = _mm(xd, dst, NT)
                z = dye * _mm(cm, st) - dxd * xd
                zl = _colsum(dst * st) * el + _colsum(dxd * xd)
                z = z + jnp.where(row5 == CHUNK - 1, zl, 0.0)
                dcb = jnp.zeros((CHUNK, CHUNK), F32)
                for hh in range(SSD_HPG):
                    head = g * SSD_HPG + hh
                    lm, gm = _ssd_head(acsv, ar_ref, n, head, cbm, c)
                    dym = jnp.where(lane5 == hh, dyv, 0.0)
                    dxdt = dxdt + _mm(gm, dym, TN)
                    dg = _mm(dym, xdt, NT)
                    dcb = dcb + dg * lm
                    pm = dg * gm
                    dacs = dacs + jnp.where(c["lane"] == head + 16, _rowsum(pm) - _mmsel(pm, c["ones"], TN), 0.0)
                dbc_ref[0, r, (SSD_GROUPS + g) * SSD_N:(SSD_GROUPS + g + 1) * SSD_N] = dcm + _mm(dcb, bm)
                dbc_ref[0, r, g * SSD_N:(g + 1) * SSD_N] = dbm + _mm(dcb, cm, TN)
                dacs = dacs + _mmsel(z, rsel)
                ddt = ddt + _mmsel(dxdt * x, rsel)
                dx_ref[0, r, gl] = dxdt * dt
                dst_scr[:, gl] = dst * el + _mm(cm, dye, TN)
            ddt_ref[0, r, :] = ddt
            dacs_ref[0, r, :] = dacs
            return carry

        lax.fori_loop(0, ncb, chunk, 0)

    return _pcall(
        body, name="ssd_chunk_bwd", grid=(bsz, nsb),
        in_specs=[sp["wide"](), sp["wide"](), sp["wide"](), sp["bc"](), sp["small"](), sp["ar"], sp["wide"](), sp["st"]],
        out_specs=[sp["wide"](), sp["bc"](), sp["small"](), sp["small"]()],
        out_shape=[jax.ShapeDtypeStruct((bsz, seq, SSD_HEADS * SSD_P), F32),
                   jax.ShapeDtypeStruct((bsz, seq, 2 * SSD_GROUPS * SSD_N), F32),
                   jax.ShapeDtypeStruct((bsz, seq, LANES), F32), jax.ShapeDtypeStruct((bsz, seq, LANES), F32)],
        scratch_shapes=[pltpu.VMEM((SSD_N, SSD_HEADS * SSD_P), F32)],
        compiler_params=_params(("parallel", "arbitrary")),
    )(xs, dtx, acsx, bc, acs, ar, dy, sts)


def _ssd_prep_bwd(ypre, proj, dxs_c, dxs_d, dbc, ddt, dacs, dsm_gdn, sp, tm):
    t = proj.shape[0]
    d = D_MODEL

    def body(i, y_ref, sm_ref, dxc_ref, dxd_ref, dbc_ref, ddt_ref, dacs_ref, dsg_ref, sp_ref,
             dpre_ref, dsm_ref, dcb_ref, dsp_ref):
        for off, w, grad in ((0, d, dxc_ref[...] + dxd_ref[...]), (d, 512, dbc_ref[...])):
            y = y_ref[:, off:off + w]
            dpre = grad * _dsilu(y, _sigmoid(y))
            dpre_ref[:, off:off + w] = dpre
            dcb_ref[:, off:off + w] += _colsum(dpre)
        sm = sm_ref[...]
        lane = lax.broadcasted_iota(jnp.int32, sm.shape, 1)
        valid = (lane >= 16) & (lane < 32)
        xb = sm + sp_ref[1:2, :]
        dt = jnp.where(valid, _softplus(xb), 0.0)
        a_neg = -jnp.exp(sp_ref[0:1, :])
        dadt_s = _mmx(_block_tri(tm, True), dacs_ref[...])
        dxb = jnp.where(valid, (ddt_ref[...] + dadt_s * a_neg) * _sigmoid(xb), 0.0)
        dsm_ref[...] = (dsg_ref[...] + dxb).astype(BF16)
        dsp_ref[1:2, :] += _colsum(dxb)
        dsp_ref[0:1, :] += jnp.where(valid[0:1, :], _colsum(dadt_s * dt) * a_neg, 0.0)

    ins = [("row", ypre, d + 512, 0), ("row", proj, LANES, SMALL_CB), ("row", dxs_c, d, 0), ("row", dxs_d, d, 0),
           ("row", dbc, 512, 0), ("row", ddt, LANES, 0), ("row", dacs, LANES, 0), ("row", dsm_gdn, LANES, 0), ("full", sp)]
    return _rowwise("ssd_prep_bwd", body, t, tm, ins, [(d + 512, F32), (LANES, BF16)],
                    accs=[(1, d + 512), (SUBLANES, LANES)])


def _gdn_chunk_bwd(qn, kn, vv, gs, gr, do, sts, tis, bsz, seq, sb, riders):
    hb = GDN_HB
    nsb, ncb, sp = _gdn_specs(seq, sb, hb, True)
    grid = (bsz, GDN_HEADS // hb, nsb)
    any_spec, rider_shapes, rider_sems, wrap = _riding_exchange(riders, True, 8, 4, grid)

    def body(q_ref, k_ref, v_ref, gs_ref, gr_ref, do_ref, st_ref, ti_ref, dq_ref, dk_ref, dv_ref, dgb_ref, ds_scr):
        hg = pl.program_id(1)

        @pl.when(pl.program_id(2) == 0)
        def _():
            ds_scr[...] = jnp.zeros_like(ds_scr)

        c = _chunk_consts()

        def chunk(nn, carry):
            n = ncb - 1 - nn
            r = pl.ds(pl.multiple_of(n * CHUNK, CHUNK), CHUNK)
            rs = pl.ds(pl.multiple_of(n * GDN_DK, GDN_DK), GDN_DK)
            gsv = gs_ref[0, r, :]
            heads = list(range(hb))
            sls = [slice(ih * GDN_DK, (ih + 1) * GDN_DK) for ih in heads]
            q = [q_ref[0, r, sl] for sl in sls]
            k = [k_ref[0, r, sl] for sl in sls]
            v = [v_ref[0, r, sl] for sl in sls]
            do_ = [do_ref[0, r, sl] for sl in sls]
            s = [st_ref[0, ih, rs, :] for ih in heads]
            tinv = [ti_ref[0, ih, r, :] for ih in heads]
            dsn = [ds_scr[ih] for ih in heads]
            beta, dc, eg, egl, ekd = zip(*[
                _gdn_gates(gsv, gr_ref[0, n, pl.ds(ih, 1), :], ih, c) for ih in heads])
            mul = lambda a, b: a * b
            kb = _hmap(mul, k, beta)
            rhs_w = _hmap(mul, kb, eg)
            u = _hmap(lambda t_, a, b: _mm3(t_, a * b), tinv, v, beta)
            w = _hmap(_mm3, tinv, rhs_w)
            amat = _hmap(lambda a, b, d_: jnp.where(c["strict"], _mm(a, b, NT) * d_, 0.0), kb, k, dc)
            qk = _hmap(lambda a, b, d_: _mm(a, b, NT) * d_, q, k, dc)
            qd = _hmap(mul, q, eg)
            kd = _hmap(mul, k, ekd)
            v_new = _hmap(lambda a, b, s_: a - _mm(b, s_), u, w, s)
            dv_new = _hmap(lambda qk_, d_, kd_, dn: _mm(qk_, d_, TN) + _mm(kd_, dn), qk, do_, kd, dsn)
            dqk = _hmap(lambda d_, vn: _mm(d_, vn, NT), do_, v_new)
            dqd = _hmap(lambda d_, s_: _mm(d_, s_, NT), do_, s)
            ds_new = _hmap(lambda qd_, d_, dn, e, w_, dvn: _mm(qd_, d_, TN) + dn * e - _mm(w_, dvn, TN),
                           qd, do_, dsn, egl, w, dv_new)
            dkd = _hmap(lambda vn, dn: _mm(vn, dn, NT), v_new, dsn)
            dgl = _hmap(lambda s_, dn, e: _colsum(_rowsum(s_ * dn)) * e, s, dsn, egl)
            dw = _hmap(lambda dvn, s_: -_mm(dvn, s_, NT), dv_new, s)
            dru = _hmap(lambda t_, a: _mm3(t_, a, TN), tinv, dv_new)
            drw = _hmap(lambda t_, a: _mm3(t_, a, TN), tinv, dw)
            da = _hmap(lambda a, u_, b, w_: jnp.where(c["strict"], -(_mm(a, u_, NT) + _mm(b, w_, NT)), 0.0), dru, u, drw, w)
            m = _hmap(mul, da, dc)
            dkb = _hmap(lambda a, e, m_, k_: a * e + _mm(m_, k_), drw, eg, m, k)
            mq = _hmap(mul, dqk, dc)
            dq = _hmap(lambda mq_, k_, a, e: _mm(mq_, k_) + a * e, mq, k, dqd, eg)
            dk = _hmap(lambda m_, kb_, mq_, q_, a, e, b, be: _mm(m_, kb_, TN) + _mm(mq_, q_, TN) + a * e + b * be,
                       m, kb, mq, q, dkd, ekd, dkb, beta)
            dbeta = _hmap(lambda a, v_, b, k_: _rowsum(a * v_) + _rowsum(b * k_), dru, v, dkb, k)
            pq = _hmap(lambda a, am, b, qk_: a * am + b * qk_, da, amat, dqk, qk)
            ekk = _hmap(lambda a, b: _rowsum(a * b), dkd, kd)
            dgc = _hmap(lambda pq_, a, rw, b, qd_, e, gl_: (
                _rowsum(pq_) - _mmsel(pq_, c["ones"], TN) + (_rowsum(a * rw) + _rowsum(b * qd_) - e)
                + jnp.where(c["row1"] == CHUNK - 1, _colsum(e) + gl_, 0.0)), pq, drw, rhs_w, dqd, qd, ekk, dgl)
            for ih in heads:
                ds_scr[ih] = ds_new[ih]
                dv_ref[0, r, sls[ih]] = dru[ih] * beta[ih]
                dq_ref[0, r, sls[ih]] = dq[ih]
                dk_ref[0, r, sls[ih]] = dk[ih]
                dgb_ref[0, r, sls[ih]] = jnp.where(c["lane"] == 0, dbeta[ih], jnp.where(c["lane"] == 1, dgc[ih], 0.0))
            return carry

        lax.fori_loop(0, ncb, chunk, 0)

    res = _pcall(
        wrap(body), name="gdn_chunk_bwd", grid=grid,
        in_specs=[sp["wide"](), sp["wide"](), sp["wide"](), sp["gs"], sp["gr"], sp["wide"](), sp["st"], sp["ti"]] + any_spec,
        out_specs=[sp["wide"](), sp["wide"](), sp["wide"](), sp["wide"]()] + any_spec,
        out_shape=[jax.ShapeDtypeStruct((bsz, seq, D_MODEL), F32)] * 4 + rider_shapes,
        scratch_shapes=[pltpu.VMEM((hb, GDN_DK, GDN_DK), F32)] + rider_sems,
        compiler_params=_params(("arbitrary", "arbitrary", "arbitrary")),
    )(qn, kn, vv, gs, gr, do, sts, tis, *riders)
    return res[:4], res[4:]


def _gdn_prep_bwd(ypre, proj, dqn, dkn, dvv, dgb, gp, tm):
    t = proj.shape[0]
    d = D_MODEL

    def body(i, y_ref, sm_ref, dq_ref, dk_ref, dv_ref, dgb_ref, gp_ref, dpre_ref, dsm_ref, dgp_ref):
        for g_ref, off, scale in ((dq_ref, 0, GDN_DK ** -0.5), (dk_ref, d, 1.0), (dv_ref, 2 * d, None)):
            y = y_ref[:, off:off + d]
            sy = _sigmoid(y)
            ds_ = _dsilu(y, sy)
            if scale is None:
                dpre_ref[:, off:off + d] = g_ref[...] * ds_
            else:
                a = y * sy
                for hh in range(GDN_HEADS):
                    sl = slice(hh * GDN_DK, (hh + 1) * GDN_DK)
                    s = a[:, sl]
                    n = lax.rsqrt(_rowsum(s * s) + EPS)
                    ah = s * n
                    gq = g_ref[:, sl]
                    dpre_ref[:, off + hh * GDN_DK:off + (hh + 1) * GDN_DK] = (
                        (scale * n) * (gq - ah * _rowsum(gq * ah)) * ds_[:, sl])
        sm = sm_ref[...]
        lane = lax.broadcasted_iota(jnp.int32, sm.shape, 1)
        si = lax.broadcasted_iota(jnp.int32, (d, LANES), 0)
        so = lax.broadcasted_iota(jnp.int32, (d, LANES), 1)
        sel = (((si % GDN_DK == 0) & (so == si // GDN_DK)) | ((si % GDN_DK == 1) & (so == si // GDN_DK + 8))).astype(F32)
        dsel = _mmx(dgb_ref[...], sel)
        is_g = (lane >= 8) & (lane < 16)
        dsel = jnp.where(is_g, _mmx(_block_tri(tm, True), dsel), dsel)
        beta = _sigmoid(sm)
        xb = sm + gp_ref[1:2, :]
        a_neg = -jnp.exp(gp_ref[0:1, :])
        sp = _softplus(xb)
        dxb = jnp.where(is_g, dsel * a_neg * _sigmoid(xb), 0.0)
        dsm_ref[...] = jnp.where(lane < 8, dsel * beta * (1.0 - beta), dxb)
        dgp_ref[1:2, :] += _colsum(dxb)
        dgp_ref[0:1, :] += _colsum(jnp.where(is_g, dsel * a_neg * sp, 0.0))

    ins = [("row", ypre, 3 * d, 0), ("row", proj, LANES, SMALL_CB), ("row", dqn, d, 0), ("row", dkn, d, 0),
           ("row", dvv, d, 0), ("row", dgb, d, 0), ("full", gp)]
    return _rowwise("gdn_prep_bwd", body, t, tm, ins, [(3 * d, F32), (LANES, F32)], accs=[(SUBLANES, LANES)])


def _dh1_first_bwd(dproj, wp_in, x, dx1, w, scatter_riders):
    d = D_MODEL

    def epilogue(dh, x_ref, dx1_ref, w_ref, dx_ref, dw_ref):
        xh, r = _rms(x_ref[...], d)
        dw_ref[...] += _colsum(dh * xh)
        dx_ref[...] = dx1_ref[...] + _rms_bwd(xh, r, dh * w_ref[...], d)

    return _matmul_rows("mm_dh1_first_bwd", dproj, wp_in, "nt", epilogue, [x, dx1], [w], [(d, F32)], accs=[(1, d)],
                        tk=PROJ_W // 2, scatter_riders=scatter_riders)


def _gather_two_level(name, arrays):
    n = len(arrays)
    n_sem = 7

    def body(*refs):
        ins, outs = refs[:n], refs[n:2 * n]
        send_sems, recv_sems, loc_sems = refs[2 * n:]
        x, y, c = lax.axis_index("x"), lax.axis_index("y"), lax.axis_index("c")
        slot = lambda px, py, pc: 4 * px + 2 * py + pc
        sibling = (x, y, 1 - c)
        chips = [(1 - x, y), (x, 1 - y), (1 - x, 1 - y)]

        def copy(t, k, src, block, to):
            return pltpu.make_async_remote_copy(
                src_ref=src, dst_ref=outs[t].at[block], send_sem=send_sems.at[t, k], recv_sem=recv_sems.at[t, k],
                device_id=to, device_id_type=pl.DeviceIdType.MESH)

        own, first, passed = [], [], []
        for t in range(n):
            own.append(pltpu.make_async_copy(ins[t], outs[t].at[slot(x, y, c)], loc_sems.at[t]))
            first.append(copy(t, 0, ins[t], slot(x, y, c), sibling))
            first += [copy(t, 1 + j, ins[t], slot(x, y, c), (px, py, c)) for j, (px, py) in enumerate(chips)]
        for cp in own + first:
            cp.start()
        for t in range(n):
            for j, (px, py) in enumerate(chips):
                copy(t, 1 + j, ins[t], slot(px, py, c), (px, py, c)).wait_recv()
                fwd = copy(t, 4 + j, outs[t].at[slot(px, py, c)], slot(px, py, c), sibling)
                fwd.start()
                passed.append(fwd)
        for t in range(n):
            copy(t, 0, ins[t], slot(x, y, 1 - c), sibling).wait_recv()
            for j, (px, py) in enumerate(chips):
                copy(t, 4 + j, ins[t], slot(px, py, 1 - c), sibling).wait_recv()
        for cp in first + passed:
            cp.wait_send()
        for cp in own:
            cp.wait()

    return _pcall(
        body, name=name,
        in_specs=[pl.BlockSpec(memory_space=pl.ANY)] * n,
        out_specs=[pl.BlockSpec(memory_space=pl.ANY)] * n,
        out_shape=_exchange_out_shapes(arrays, False),
        scratch_shapes=[pltpu.SemaphoreType.DMA((n, n_sem)), pltpu.SemaphoreType.DMA((n, n_sem)), pltpu.SemaphoreType.DMA((n,))],
    )(*arrays)


def _exchange_out_shapes(arrays, scatter):
    return [jax.ShapeDtypeStruct(a.shape if scatter else (N_DEV,) + a.shape, a.dtype) for a in arrays]


def _exchange_sems(n):
    return [pltpu.SemaphoreType.DMA((n, N_DEV - 1)), pltpu.SemaphoreType.DMA((n, N_DEV - 1)), pltpu.SemaphoreType.DMA((n,))]


def _exchange_phase(ins, outs, sems, scatter, start):
    send_sems, recv_sems, loc_sems = sems
    x, y, c = lax.axis_index("x"), lax.axis_index("y"), lax.axis_index("c")
    me = 4 * x + 2 * y + c
    for t in range(len(ins)):
        loc = pltpu.make_async_copy(ins[t].at[me] if scatter else ins[t], outs[t].at[me], loc_sems.at[t])
        if start:
            loc.start()
        else:
            loc.wait()
        for k in range(N_DEV - 1):
            bx, by, bc = ((k + 1) >> 2) & 1, ((k + 1) >> 1) & 1, (k + 1) & 1
            px = 1 - x if bx else x
            py = 1 - y if by else y
            pc = 1 - c if bc else c
            peer = 4 * px + 2 * py + pc
            src = ins[t].at[peer] if scatter else ins[t]
            copy = lambda dst: pltpu.make_async_remote_copy(
                src_ref=src, dst_ref=dst, send_sem=send_sems.at[t, k], recv_sem=recv_sems.at[t, k],
                device_id=(px, py, pc), device_id_type=pl.DeviceIdType.MESH)
            if start:
                copy(outs[t].at[me]).start()
            else:
                copy(outs[t].at[me]).wait_send()
                copy(outs[t].at[peer]).wait_recv()


def _adam_math(w, g, m, v):
    m = ADAM_B1 * m + (1.0 - ADAM_B1) * g
    v = ADAM_B2 * v + (1.0 - ADAM_B2) * (g * g)
    m_hat = m / (1.0 - ADAM_B1 ** ADAM_STEP)
    v_hat = v / (1.0 - ADAM_B2 ** ADAM_STEP)
    delta = -ADAM_LR * (m_hat / (jnp.sqrt(v_hat) + ADAM_EPS) + ADAM_WD * w)
    return delta, m, v


def _adam_big(name, parts, w, m, v, tm):
    r, c = w.shape
    tm = tm if r % tm == 0 else r

    def body(p_ref, w_ref, m_ref, v_ref, g_ref, d_ref, nm_ref, nv_ref):
        g = p_ref[0].astype(F32)
        for s in range(1, N_DEV):
            g = g + p_ref[s].astype(F32)
        g_ref[...] = g
        d_ref[...], nm_ref[...], nv_ref[...] = _adam_math(w_ref[...], g, m_ref[...], v_ref[...])

    blk = lambda: pl.BlockSpec((tm, c), lambda i: (i, 0))
    return _pcall(
        body, name=name, grid=(r // tm,),
        in_specs=[pl.BlockSpec((N_DEV, tm, c), lambda i: (0, i, 0)), blk(), blk(), blk()],
        out_specs=[blk(), blk(), blk(), blk()],
        out_shape=[jax.ShapeDtypeStruct((r, c), F32)] * 4,
        compiler_params=_params(("parallel",)),
    )(parts, w, m, v)


SMALL_ROWS = 56
ROW_DD, ROW_LOSS = 5, 6


def _small_sum(gathered):
    def body(g_ref, o_ref, x_ref):
        s = g_ref[0]
        for dev in range(1, N_DEV):
            s = s + g_ref[dev]
        o_ref[...] = s
        ri = lax.broadcasted_iota(jnp.int32, (D_MODEL, LANES), 0)
        ro = lax.broadcasted_iota(jnp.int32, (D_MODEL, LANES), 1)
        heads = _mmx(jnp.broadcast_to(s[ROW_DD:ROW_DD + 1, :], (SUBLANES, D_MODEL)), (ri // SSD_P == ro).astype(F32))
        loss = _rowsum(jnp.broadcast_to(s[ROW_LOSS:ROW_LOSS + 1, :], (SUBLANES, D_MODEL)))
        row = lax.broadcasted_iota(jnp.int32, (SUBLANES, LANES), 0)
        x_ref[...] = jnp.where(row == 0, heads, jnp.broadcast_to(loss, (SUBLANES, LANES)))

    return _pcall(
        body, name="small_sum",
        out_shape=[jax.ShapeDtypeStruct((SMALL_ROWS, D_MODEL), F32), jax.ShapeDtypeStruct((SUBLANES, LANES), F32)],
        compiler_params=_params(None),
    )(gathered)


def _adam_small(g, w, m, v):
    def body(g_ref, w_ref, m_ref, v_ref, d_ref, nm_ref, nv_ref):
        d_ref[...], nm_ref[...], nv_ref[...] = _adam_math(w_ref[...], g_ref[...], m_ref[...], v_ref[...])

    return _pcall(body, name="adam_small", out_shape=[jax.ShapeDtypeStruct(g.shape, F32)] * 3,
                  compiler_params=_params(None))(g, w, m, v)


def _pack(pieces, rows):
    flat = jnp.concatenate([p.reshape(-1).astype(F32) for p in pieces])
    return jnp.pad(flat, (0, rows * D_MODEL - flat.shape[0])).reshape(rows, D_MODEL)


def _unpack(packed, shapes):
    flat = packed.reshape(-1)
    out, off = [], 0
    for shp in shapes:
        size = 1
        for s in shp:
            size *= s
        out.append(flat[off:off + size].reshape(shp))
        off += size
    return out


def _permute_in(w):
    pad = jnp.zeros((w.shape[0], PROJ_W - D_IN), w.dtype)
    return jnp.concatenate([w[:, 0:4096], w[:, 4112:6672], w[:, 4096:4112], w[:, 6672:6688], pad], axis=1)


def _unpermute_in(g):
    return jnp.concatenate([g[:, 0:4096], g[:, 6656:6672], g[:, 4096:6656], g[:, 6672:6688]], axis=1)


def _lane_row(vec, start):
    return jnp.zeros((LANES,), F32).at[start:start + vec.shape[0]].set(vec)


def _cols_from_shards(g):
    return jnp.transpose(g, (1, 0, 2)).reshape(g.shape[1], N_DEV * g.shape[2])


def _cols_to_shards(a):
    return jnp.transpose(a.astype(BF16).reshape(a.shape[0], N_DEV, a.shape[1] // N_DEV), (1, 0, 2))


def _rows_to_shards(a):
    return a.astype(BF16).reshape(N_DEV, a.shape[0] // N_DEV, a.shape[1])


def _local_step(x, tgt, wp_in, rest, p, rest_is_sharded):
    bsz, seq, d = x.shape
    t = bsz * seq
    x2 = x.reshape(t, d)
    tgt2 = tgt.reshape(t, d)
    tm = min(256, seq)
    tm_wide = min(128, seq)
    sb = min(512, seq)

    gp = jnp.zeros((SUBLANES, LANES), F32).at[0].set(_lane_row(p["gdn_a_log"], 8)).at[1].set(_lane_row(p["gdn_dt_bias"], 8))
    sp = jnp.zeros((SUBLANES, LANES), F32).at[0].set(_lane_row(p["ssd_a_log"], 16)).at[1].set(_lane_row(p["ssd_dt_bias"], 16))
    dvec = jnp.repeat(p["ssd_d"], SSD_P).reshape(1, d)
    row = lambda v: v.reshape(1, -1)
    pre_mix, post_mix, pre_ffn, post_ffn = (row(p[k]) for k in ("pre_mix_norm", "post_mix_norm", "pre_ffn_norm", "post_ffn_norm"))
    gnw, snw = row(p["gdn_norm_w"]), row(p["ssd_norm_w"])
    gcw, scw, scb, fcw, fcb = p["gdn_conv_w"], p["ssd_conv_w"], row(p["ssd_conv_b"]), p["ffn_conv_w"], row(p["ffn_conv_b"])

    h1 = _norm_cast("norm_in", x2, pre_mix, tm)
    proj = _matmul("mm_proj", h1, wp_in, "nn", F32)
    b3 = lambda a: a.reshape(bsz, seq, a.shape[-1])
    b2 = lambda a: a.reshape(t, a.shape[-1])
    rows_of = lambda a, lo, n: jnp.transpose(a[:, lo:lo + n].reshape(bsz, seq // CHUNK, CHUNK, n), (0, 3, 1, 2))
    qn, kn, vv, gs, ypre_gdn = _gdn_prep(proj, gcw, gp, seq, tm)
    qn, kn, vv, gs = b3(qn), b3(kn), b3(vv), b3(gs)
    gr = jnp.transpose(b2(gs)[:, 8:8 + GDN_HEADS].reshape(bsz, seq // CHUNK, CHUNK, GDN_HEADS), (0, 1, 3, 2))
    (o_gdn, gdn_st, gdn_ti), gathered = _gdn_chunk_fwd(qn, kn, vv, gs, gr, bsz, seq, sb, list(rest) if rest_is_sharded else [])
    if rest_is_sharded:
        w_out, w_up, w_down = gathered[0].reshape(-1, d), _cols_from_shards(gathered[1]), gathered[2].reshape(-1, d)
    else:
        w_out, w_up, w_down = rest
    o_gdn = b2(o_gdn)
    xs, bc, dtx, acsx, acs, ypre_ssd = _ssd_prep(proj, scw, scb, sp, seq, tm)
    ar = rows_of(acs, 16, SSD_HEADS)
    y_ssd, ssd_st = _ssd_chunk_fwd(b3(xs), b3(bc), b3(dtx), b3(acsx), b3(acs), ar, bsz, seq, sb)
    y_ssd = b2(y_ssd)
    mixin = _gate_norm(o_gdn, y_ssd, xs, proj, gnw, snw, dvec, tm)
    mix, x1, h2 = _out_mid(mixin, w_out, x2, post_mix, pre_ffn)
    u_pre = _matmul("mm_up", h2, w_up, "nn", F32)
    act, u = _ffn_act(u_pre, fcw, fcb, seq, tm_wide)
    dy, df, loss_lanes, d_post_ffn = _down_final(act, w_down, x1, tgt2, post_ffn)

    g_down = _matmul("mm_dw_down", act, df, "tn", BF16, tm=1408, tk=2048)
    dact = _matmul("mm_dact", df, w_down, "nt", F32, tn=1408)
    du_pre, d_fcw, d_fcb = _ffn_bwd(u, u_pre, dact, fcw, seq, tm_wide)
    g_up = _matmul("mm_dw_up", h2, du_pre, "tn", BF16, tk=2048)
    dx1, dmix, d_post_mix, d_pre_ffn = _dh2_mid_bwd(du_pre, w_up, x1, mix, dy, post_mix, pre_ffn)
    g_out = _matmul("mm_dw_out", mixin, dmix, "tn", BF16, tk=2048)
    do_gdn, dza, dy_ssd, dxs_d, dzs, d_gnw, d_snw, d_dd = _dmixin_gate_norm_bwd(dmix, w_out, o_gdn, y_ssd, xs, proj, gnw, snw, dvec)
    dxs_c, dbc, ddt, dacs = (b2(a) for a in _ssd_chunk_bwd(
        b3(xs), b3(bc), b3(dtx), b3(acsx), b3(acs), ar, b3(dy_ssd), ssd_st, bsz, seq, sb))
    riders = [_rows_to_shards(g_out), _cols_to_shards(g_up), _rows_to_shards(g_down)] if rest_is_sharded else []
    dgdn, received = _gdn_chunk_bwd(qn, kn, vv, gs, gr, b3(do_gdn), gdn_st, gdn_ti, bsz, seq, min(256, seq), riders)
    if rest_is_sharded:
        g_out, g_up, g_down = received
    dqn, dkn, dvv, dgb = (b2(a) for a in dgdn)
    dpre_gdn, dsm_gdn, d_gp = _gdn_prep_bwd(ypre_gdn, proj, dqn, dkn, dvv, dgb, gp, tm)
    dpre_ssd, dsm, d_scb, d_sp = _ssd_prep_bwd(ypre_ssd, proj, dxs_c, dxs_d, dbc, ddt, dacs, dsm_gdn, sp, tm)
    dproj, d_gcw, d_scw = _assemble_dproj(dpre_gdn, dza, dzs, dpre_ssd, dsm, proj, gcw, scw, seq, tm)
    g_in = _matmul("mm_dw_in", h1, dproj, "tn", BF16, tk=2048)
    if rest_is_sharded:
        (dx, d_pre_mix), (g_in,) = _dh1_first_bwd(dproj, wp_in, x2, dx1, pre_mix, [_cols_to_shards(_unpermute_in(g_in))])
    else:
        dx, d_pre_mix = _dh1_first_bwd(dproj, wp_in, x2, dx1, pre_mix, [])

    small = dict(pre_mix_norm=d_pre_mix, ssd_norm_w=d_snw, post_mix_norm=d_post_mix, pre_ffn_norm=d_pre_ffn,
                 post_ffn_norm=d_post_ffn, dd_lanes=d_dd, loss_lanes=loss_lanes, gdn_gates=d_gp, ssd_gates=d_sp,
                 gdn_norm_w=d_gnw, gdn_conv_w=d_gcw[0:4], ssd_conv_w=d_scw[0:4], ssd_conv_b=d_scb,
                 ffn_conv_w=d_fcw[0:3], ffn_conv_b=d_fcb)
    return dx.reshape(bsz, seq, d), g_in, g_out, g_up, g_down, small


def kernel(x, pre_mix_norm, w_in, gdn_conv_w, gdn_a_log, gdn_dt_bias, gdn_norm_w, ssd_conv_w, ssd_conv_b, ssd_a_log, ssd_dt_bias, ssd_d, ssd_norm_w, w_out, post_mix_norm, pre_ffn_norm, w_up, ffn_conv_w, ffn_conv_b, w_down, post_ffn_norm, loss_target, m_pre_mix_norm, m_w_in, m_gdn_conv_w, m_gdn_a_log, m_gdn_dt_bias, m_gdn_norm_w, m_ssd_conv_w, m_ssd_conv_b, m_ssd_a_log, m_ssd_dt_bias, m_ssd_d, m_ssd_norm_w, m_w_out, m_post_mix_norm, m_pre_ffn_norm, m_w_up, m_ffn_conv_w, m_ffn_conv_b, m_w_down, m_post_ffn_norm, v_pre_mix_norm, v_w_in, v_gdn_conv_w, v_gdn_a_log, v_gdn_dt_bias, v_gdn_norm_w, v_ssd_conv_w, v_ssd_conv_b, v_ssd_a_log, v_ssd_dt_bias, v_ssd_d, v_ssd_norm_w, v_w_out, v_post_mix_norm, v_pre_ffn_norm, v_w_up, v_ffn_conv_w, v_ffn_conv_b, v_w_down, v_post_ffn_norm):
    names = ["pre_mix_norm", "w_in", "gdn_conv_w", "gdn_a_log", "gdn_dt_bias", "gdn_norm_w", "ssd_conv_w", "ssd_conv_b",
             "ssd_a_log", "ssd_dt_bias", "ssd_d", "ssd_norm_w", "w_out", "post_mix_norm", "pre_ffn_norm", "w_up",
             "ffn_conv_w", "ffn_conv_b", "w_down", "post_ffn_norm"]
    w_args = [pre_mix_norm, w_in, gdn_conv_w, gdn_a_log, gdn_dt_bias, gdn_norm_w, ssd_conv_w, ssd_conv_b, ssd_a_log, ssd_dt_bias, ssd_d, ssd_norm_w, w_out, post_mix_norm, pre_ffn_norm, w_up, ffn_conv_w, ffn_conv_b, w_down, post_ffn_norm]
    m_args = [m_pre_mix_norm, m_w_in, m_gdn_conv_w, m_gdn_a_log, m_gdn_dt_bias, m_gdn_norm_w, m_ssd_conv_w, m_ssd_conv_b, m_ssd_a_log, m_ssd_dt_bias, m_ssd_d, m_ssd_norm_w, m_w_out, m_post_mix_norm, m_pre_ffn_norm, m_w_up, m_ffn_conv_w, m_ffn_conv_b, m_w_down, m_post_ffn_norm]
    v_args = [v_pre_mix_norm, v_w_in, v_gdn_conv_w, v_gdn_a_log, v_gdn_dt_bias, v_gdn_norm_w, v_ssd_conv_w, v_ssd_conv_b, v_ssd_a_log, v_ssd_dt_bias, v_ssd_d, v_ssd_norm_w, v_w_out, v_post_mix_norm, v_pre_ffn_norm, v_w_up, v_ffn_conv_w, v_ffn_conv_b, v_w_down, v_post_ffn_norm]
    w = {k: a[0] for k, a in zip(names, w_args)}
    m = {k: a[0] for k, a in zip(names, m_args)}
    v = {k: a[0] for k, a in zip(names, v_args)}
    idx = 4 * lax.axis_index("x") + 2 * lax.axis_index("y") + lax.axis_index("c")
    big = ("w_in", "w_out", "w_up", "w_down")
    conv = ("gdn_conv_w", "ssd_conv_w", "ffn_conv_w")

    conv_local = jnp.concatenate([jnp.pad(w[k], ((0, 4 - w[k].shape[0]), (0, 0))) for k in conv], axis=1)
    g_in, g_conv = _gather_two_level("gather_weights", [w["w_in"].astype(BF16), conv_local])
    wp_in = _permute_in(_cols_from_shards(g_in))
    p = {k: w[k] for k in names if k not in big and k not in conv}
    off = 0
    for k in conv:
        cw = w[k].shape[1]
        p[k] = jnp.transpose(g_conv[:, :w[k].shape[0], off:off + cw], (1, 0, 2)).reshape(w[k].shape[0], N_DEV * cw)
        off += cw

    rest = tuple(w[k].astype(BF16) for k in ("w_out", "w_up", "w_down"))
    dx, p_in, p_out, p_up, p_down, small = _local_step(x, loss_target, wp_in, rest, p, True)

    gate_row = jnp.concatenate([small["gdn_gates"][0], small["gdn_gates"][1], small["ssd_gates"][0], small["ssd_gates"][1],
                                small["gdn_norm_w"][0], jnp.zeros((D_MODEL - 5 * LANES,), F32)]).reshape(1, D_MODEL)
    pack = _pack([small["pre_mix_norm"], small["ssd_norm_w"], small["post_mix_norm"], small["pre_ffn_norm"],
                  small["post_ffn_norm"], small["dd_lanes"], small["loss_lanes"], gate_row,
                  small["gdn_conv_w"], small["ssd_conv_w"], jnp.pad(small["ssd_conv_b"], ((0, 0), (0, 512))),
                  jnp.pad(small["ffn_conv_w"].reshape(-1), (0, 17 * D_MODEL - 3 * 2 * D_FF)),
                  jnp.pad(small["ffn_conv_b"], ((0, 0), (0, 512)))], SMALL_ROWS)
    (pack_all,) = _gather_two_level("gather_small", [pack])
    ssum, extra = _small_sum(pack_all)

    grads, deltas, new_m, new_v = {}, {}, {}, {}
    for k, parts in (("w_in", p_in), ("w_out", p_out), ("w_up", p_up), ("w_down", p_down)):
        grads[k], deltas[k], new_m[k], new_v[k] = _adam_big("adam_" + k, parts, w[k], m[k], v[k], 256)

    flat = ssum.reshape(-1)
    gate = ssum[7]
    sg = dict(pre_mix_norm=ssum[0], ssd_norm_w=ssum[1], post_mix_norm=ssum[2], pre_ffn_norm=ssum[3], post_ffn_norm=ssum[4],
              gdn_a_log=gate[8:16], gdn_dt_bias=gate[LANES + 8:LANES + 16], ssd_a_log=gate[2 * LANES + 16:2 * LANES + 32],
              ssd_dt_bias=gate[3 * LANES + 16:3 * LANES + 32], gdn_norm_w=gate[4 * LANES:5 * LANES], ssd_d=extra[0, 0:SSD_HEADS])
    o = 8 * D_MODEL
    full_gcw = flat[o:o + 4 * 3072].reshape(4, 3072)
    o += 12 * D_MODEL
    full_scw = flat[o:o + 4 * 1536].reshape(4, 1536)
    o += 6 * D_MODEL
    sg["ssd_conv_b"] = flat[o:o + 1536]
    o += 2 * D_MODEL
    full_fcw = flat[o:o + 3 * 2 * D_FF].reshape(3, 2 * D_FF)
    o += 17 * D_MODEL
    sg["ffn_conv_b"] = flat[o:o + 2 * D_FF]
    for k, full in (("gdn_conv_w", full_gcw), ("ssd_conv_w", full_scw), ("ffn_conv_w", full_fcw)):
        cw = w[k].shape[1]
        sg[k] = lax.dynamic_slice_in_dim(full, idx * cw, cw, axis=1)
    small_names = [k for k in names if k not in big]
    rows = 24
    gpk = _pack([sg[k] for k in small_names], rows)
    dpk, mpk, vpk = _adam_small(gpk, _pack([w[k] for k in small_names], rows), _pack([m[k] for k in small_names], rows),
                                _pack([v[k] for k in small_names], rows))
    shapes = [w[k].shape for k in small_names]
    for k, g_, d_, m_, v_ in zip(small_names, _unpack(gpk, shapes), _unpack(dpk, shapes), _unpack(mpk, shapes), _unpack(vpk, shapes)):
        grads[k], deltas[k], new_m[k], new_v[k] = g_, d_, m_, v_

    loss = extra[1, 0]
    lead = lambda a: a[None]
    return (loss, dx, *[lead(grads[k]) for k in names], *[lead(deltas[k]) for k in names],
            *[lead(new_m[k]) for k in names], *[lead(new_v[k]) for k in names])
```

```python
import functools

import jax
import jax.numpy as jnp
from jax import lax
from jax.experimental import pallas as pl
from jax.experimental.pallas import tpu as pltpu

F32 = jnp.float32
BF16 = jnp.bfloat16
MXU_DTYPE = jnp.bfloat16
HIGHEST = lax.Precision.HIGHEST
VMEM_LIMIT_V7X = 48 * 1024 * 1024
SUBLANES = 8
LANES = 128

D_MODEL = 1024
GDN_HEADS = 8
GDN_DK = 128
SSD_HEADS = 16
SSD_P = 64
SSD_GROUPS = 2
SSD_HPG = 8
SSD_N = 128
CHUNK = 64
D_FF = 2816
EPS = 1e-6
N_DEV = 8
PROJ_W = 7168
SMALL_CB = 52
D_IN = 6688

ADAM_LR = 0.001
ADAM_B1 = 0.9
ADAM_B2 = 0.999
ADAM_EPS = 1e-08
ADAM_WD = 0.01
ADAM_STEP = 10

NN = (((1,), (0,)), ((), ()))
NT = (((1,), (1,)), ((), ()))
TN = (((0,), (0,)), ((), ()))


def _pcall(body, **kw):
    return pl.pallas_call(body, **kw)


def _mm(a, b, dims=NN):
    return lax.dot_general(a.astype(MXU_DTYPE), b.astype(MXU_DTYPE), dims, preferred_element_type=F32)


def _mmx(a, b, dims=NN):
    return lax.dot_general(a, b, dims, precision=HIGHEST, preferred_element_type=F32)


def _split(a):
    hi = a.astype(MXU_DTYPE)
    return hi, (a - hi.astype(F32)).astype(MXU_DTYPE)


def _mm3(a, b, dims=NN):
    (ah, al), (bh, bl) = _split(a), _split(b)
    dot = lambda p, q: lax.dot_general(p, q, dims, preferred_element_type=F32)
    return dot(ah, bh) + (dot(ah, bl) + dot(al, bh))


def _mmsel(a, sel, dims=NN, terms=2):
    s = sel.astype(MXU_DTYPE)
    out = None
    for _ in range(terms):
        part = a.astype(MXU_DTYPE)
        a = a - part.astype(F32)
        prod = lax.dot_general(part, s, dims, preferred_element_type=F32)
        out = prod if out is None else out + prod
    return out


def _sigmoid(x):
    return 0.5 * jnp.tanh(0.5 * x) + 0.5


def _softplus(x):
    return jnp.maximum(x, 0.0) + jnp.log(1.0 + jnp.exp(-jnp.abs(x)))


def _dsilu(x, s):
    return s * (1.0 + x * (1.0 - s))


def _rowsum(x):
    return jnp.sum(x, axis=1, keepdims=True)


def _colsum(x):
    return jnp.sum(x, axis=0, keepdims=True)


def _pick(dim, pref):
    if dim <= pref:
        return dim
    best = None
    t = LANES
    while t <= pref:
        if dim % t == 0:
            best = t
        t += LANES
    return dim if best is None else best


def _params(sem):
    return pltpu.CompilerParams(dimension_semantics=sem, vmem_limit_bytes=VMEM_LIMIT_V7X)


def _matmul(name, a, b, mode, out_dtype, tm=1024, tn=1024, tk=1024):
    if mode == "nn":
        (m, k), (_, n) = a.shape, b.shape
    elif mode == "nt":
        (m, k), (n, _) = a.shape, b.shape
    else:
        (k, m), (_, n) = a.shape, b.shape
    tm, tn, tk = _pick(m, tm), _pick(n, tn), _pick(k, tk)
    nk = k // tk
    if mode == "tn":
        a_spec = pl.BlockSpec((tk, tm), lambda i, j, kk: (kk, i))
    else:
        a_spec = pl.BlockSpec((tm, tk), lambda i, j, kk: (i, kk))
    if mode == "nt":
        b_spec = pl.BlockSpec((tn, tk), lambda i, j, kk: (j, kk))
    else:
        b_spec = pl.BlockSpec((tk, tn), lambda i, j, kk: (kk, j))
    dims = {"nn": NN, "nt": NT, "tn": TN}[mode]

    def body(a_ref, b_ref, o_ref, *acc):
        if nk == 1:
            o_ref[...] = _mm(a_ref[...], b_ref[...], dims).astype(out_dtype)
            return
        kk = pl.program_id(2)

        @pl.when(kk == 0)
        def _():
            acc[0][...] = jnp.zeros_like(acc[0])

        acc[0][...] += _mm(a_ref[...], b_ref[...], dims)

        @pl.when(kk == nk - 1)
        def _():
            o_ref[...] = acc[0][...].astype(out_dtype)

    return _pcall(
        body, name=name, grid=(m // tm, n // tn, nk),
        in_specs=[a_spec, b_spec],
        out_specs=pl.BlockSpec((tm, tn), lambda i, j, kk: (i, j)),
        out_shape=jax.ShapeDtypeStruct((m, n), out_dtype),
        scratch_shapes=[pltpu.VMEM((tm, tn), F32)] if nk > 1 else [],
        compiler_params=_params(("parallel", "parallel", "arbitrary")),
    )(a, b)


def _matmul_rows(name, a, b, mode, epilogue, row_ins, full_ins, outs, accs=(), tm=512, tk=1024, scatter_riders=()):
    if mode == "nn":
        (m, k), (_, n) = a.shape, b.shape
    else:
        (m, k), (n, _) = a.shape, b.shape
    tm, tk = _pick(m, tm), _pick(k, tk)
    nk = k // tk
    a_spec = pl.BlockSpec((tm, tk), lambda i, kk: (i, kk))
    b_spec = pl.BlockSpec((n, tk), lambda i, kk: (0, kk)) if mode == "nt" else pl.BlockSpec((tk, n), lambda i, kk: (kk, 0))
    dims = NT if mode == "nt" else NN
    n_row, n_full, n_out, n_acc = len(row_ins), len(full_ins), len(outs), len(accs)

    def body(a_ref, b_ref, *rest):
        ins = rest[:n_row + n_full]
        out_refs = rest[n_row + n_full:n_row + n_full + n_out]
        acc_refs = rest[n_row + n_full + n_out:n_row + n_full + n_out + n_acc]
        prod_scr = rest[-1]
        i, kk = pl.program_id(0), pl.program_id(1)

        if n_acc:
            @pl.when((i == 0) & (kk == 0))
            def _():
                for r in acc_refs:
                    r[...] = jnp.zeros_like(r)

        if nk == 1:
            epilogue(_mm(a_ref[...], b_ref[...], dims), *ins, *out_refs, *acc_refs)
            return

        @pl.when(kk == 0)
        def _():
            prod_scr[...] = jnp.zeros_like(prod_scr)

        prod_scr[...] += _mm(a_ref[...], b_ref[...], dims)

        @pl.when(kk == nk - 1)
        def _():
            epilogue(prod_scr[...], *ins, *out_refs, *acc_refs)

    grid = (m // tm, nk)
    riders = list(scatter_riders)
    n_in = 2 + n_row + n_full
    any_spec, rider_shapes, rider_sems, wrap = _riding_exchange(riders, True, n_in, n_out + n_acc, grid)
    row_ins = [r if isinstance(r, tuple) else (r, r.shape[1], 0) for r in row_ins]
    in_specs = [a_spec, b_spec] + [pl.BlockSpec((tm, w), lambda i, kk, cb=cb: (i, cb)) for _, w, cb in row_ins]
    in_specs += [pl.BlockSpec(f.shape, lambda i, kk, nd=f.ndim: (0,) * nd) for f in full_ins]
    row_ins = [r for r, _, _ in row_ins]
    out_specs = [pl.BlockSpec((tm, w), lambda i, kk: (i, 0)) for w, _ in outs]
    out_specs += [pl.BlockSpec(s, lambda i, kk: (0, 0)) for s in accs]
    out_shape = [jax.ShapeDtypeStruct((m, w), dt) for w, dt in outs] + [jax.ShapeDtypeStruct(s, F32) for s in accs]
    res = _pcall(
        wrap(body), name=name, grid=grid,
        in_specs=in_specs + any_spec, out_specs=out_specs + any_spec, out_shape=out_shape + rider_shapes,
        scratch_shapes=[pltpu.VMEM((tm, n), F32)] + rider_sems,
        compiler_params=_params(("arbitrary", "arbitrary")),
    )(a, b, *row_ins, *full_ins, *riders)
    return (res[:n_out + n_acc], res[n_out + n_acc:]) if riders else res


def _rowwise(name, body, n_rows, tm, ins, outs, accs=()):
    arrays, in_specs = [], []
    last8 = n_rows // SUBLANES - 1
    per = tm // SUBLANES
    for spec in ins:
        kind, arr = spec[0], spec[1]
        if kind == "full":
            in_specs.append(pl.BlockSpec(arr.shape, lambda i, nd=arr.ndim: (0,) * nd))
        else:
            w, cb = spec[2], spec[3]
            if kind == "row":
                in_specs.append(pl.BlockSpec((tm, w), lambda i, cb=cb: (i, cb)))
            elif kind == "prev":
                in_specs.append(pl.BlockSpec((SUBLANES, w), lambda i, cb=cb: (jnp.maximum(i * per - 1, 0), cb)))
            else:
                in_specs.append(pl.BlockSpec((SUBLANES, w), lambda i, cb=cb: (jnp.minimum((i + 1) * per, last8), cb)))
        arrays.append(arr)
    out_shape = [jax.ShapeDtypeStruct((n_rows, w), dt) for (w, dt) in outs]
    out_shape += [jax.ShapeDtypeStruct(s, F32) for s in accs]
    out_specs = [pl.BlockSpec((tm, w), lambda i: (i, 0)) for (w, _) in outs]
    out_specs += [pl.BlockSpec(s, lambda i: (0, 0)) for s in accs]
    n_io = len(ins) + len(outs)

    def kern(*refs):
        i = pl.program_id(0)
        if accs:
            @pl.when(i == 0)
            def _():
                for r in refs[n_io:]:
                    r[...] = jnp.zeros_like(r)
        body(i, *refs)

    res = _pcall(
        kern, name=name, grid=(n_rows // tm,), in_specs=in_specs, out_specs=out_specs, out_shape=out_shape,
        compiler_params=_params(("arbitrary",)),
    )(*arrays)
    return res


def _shift_down(x, halo, j):
    r = pltpu.roll(x, j, 0)
    hr = pltpu.roll(halo, j, 0)
    rows = lax.broadcasted_iota(jnp.int32, (SUBLANES, x.shape[1]), 0)
    top = jnp.where(rows < j, hr, r[0:SUBLANES])
    return jnp.concatenate([top, r[SUBLANES:]], axis=0)


def _shift_up(x, halo, j):
    tm = x.shape[0]
    r = pltpu.roll(x, tm - j, 0)
    hr = pltpu.roll(halo, SUBLANES - j, 0)
    rows = lax.broadcasted_iota(jnp.int32, (SUBLANES, x.shape[1]), 0)
    bot = jnp.where(rows >= SUBLANES - j, hr, r[tm - SUBLANES:])
    return jnp.concatenate([r[:tm - SUBLANES], bot], axis=0)


def _conv_taps(x, halo, kw):
    return [x if kw - 1 - k == 0 else _shift_down(x, halo, kw - 1 - k) for k in range(kw)]


def _conv(taps, w):
    y = taps[0] * w[0:1]
    for k in range(1, len(taps)):
        y = y + taps[k] * w[k:k + 1]
    return y


def _rms(x, width):
    r = lax.rsqrt(jnp.sum(x * x, axis=-1, keepdims=True) * (1.0 / width) + EPS)
    return x * r, r


def _rms_bwd(xh, r, dxh, width):
    return r * (dxh - xh * (jnp.sum(dxh * xh, axis=-1, keepdims=True) * (1.0 / width)))


def _seq_flags(i, seq, tm):
    nps = seq // tm
    pos = i % nps
    return jnp.where(pos == 0, 0.0, 1.0), jnp.where(pos == nps - 1, 0.0, 1.0)


def _norm_cast(name, x, w, tm):
    t, d = x.shape

    def body(i, x_ref, w_ref, h_ref):
        xh, _ = _rms(x_ref[...], d)
        h_ref[...] = (xh * w_ref[...]).astype(BF16)

    return _rowwise(name, body, t, tm, [("row", x, d, 0), ("full", w)], [(d, BF16)])[0]


def _gdn_prep(proj, cw, gp, seq, tm):
    t = proj.shape[0]
    d = D_MODEL

    def body(i, q_ref, qh_ref, k_ref, kh_ref, v_ref, vh_ref, sm_ref, cw_ref, gp_ref, qn_ref, kn_ref, vv_ref, gs_ref, ypre_ref):
        keep, _ = _seq_flags(i, seq, tm)
        for x_ref, h_ref, o_ref, off, scale in ((q_ref, qh_ref, qn_ref, 0, GDN_DK ** -0.5),
                                               (k_ref, kh_ref, kn_ref, d, 1.0), (v_ref, vh_ref, vv_ref, 2 * d, None)):
            y = _conv(_conv_taps(x_ref[...], h_ref[...] * keep, 4), cw_ref[:, off:off + d])
            ypre_ref[:, off:off + d] = y
            a = y * _sigmoid(y)
            if scale is None:
                o_ref[...] = a
            else:
                for hh in range(GDN_HEADS):
                    s = a[:, hh * GDN_DK:(hh + 1) * GDN_DK]
                    n = lax.rsqrt(_rowsum(s * s) + EPS)
                    o_ref[:, hh * GDN_DK:(hh + 1) * GDN_DK] = s * (n * scale)
        sm = sm_ref[...]
        lane = lax.broadcasted_iota(jnp.int32, sm.shape, 1)
        beta = _sigmoid(sm)
        g = jnp.where((lane >= 8) & (lane < 16), -jnp.exp(gp_ref[0:1, :]) * _softplus(sm + gp_ref[1:2, :]), 0.0)
        gs_ref[...] = jnp.where(lane < 8, beta, _mmx(_block_tri(tm, False), g))

    ins = []
    for cb in range(3):
        ins += [("row", proj, d, cb), ("prev", proj, d, cb)]
    ins += [("row", proj, LANES, SMALL_CB), ("full", cw), ("full", gp)]
    return _rowwise("gdn_prep", body, t, tm, ins, [(d, F32), (d, F32), (d, F32), (LANES, F32), (3 * d, F32)])


def _block_tri(tm, upper):
    ri = lax.broadcasted_iota(jnp.int32, (tm, tm), 0)
    ci = lax.broadcasted_iota(jnp.int32, (tm, tm), 1)
    tri = (ri <= ci) if upper else (ri >= ci)
    return (tri & ((ri // CHUNK) == (ci // CHUNK))).astype(F32)


def _chunk_consts():
    row = lax.broadcasted_iota(jnp.int32, (CHUNK, CHUNK), 0)
    col = lax.broadcasted_iota(jnp.int32, (CHUNK, CHUNK), 1)
    return dict(
        tril=row >= col, strict=row > col, eye=(row == col).astype(F32),
        lane=lax.broadcasted_iota(jnp.int32, (CHUNK, LANES), 1),
        row1=lax.broadcasted_iota(jnp.int32, (CHUNK, 1), 0),
        ones=jnp.ones((CHUNK, LANES), F32))


def _hmap(fn, *lists):
    return [fn(*a) for a in zip(*lists)]


def _tri_inv(nmats, eye):
    x = [eye - n for n in nmats]
    p = _hmap(_mm3, nmats, nmats)
    for lvl in range(5):
        x = _hmap(lambda xi, pi: xi + _mm3(xi, pi), x, p)
        if lvl < 4:
            p = _hmap(_mm3, p, p)
    return x


def _gdn_gates(gs, gc_row, h, c):
    beta = _rowsum(jnp.where(c["lane"] == h, gs, 0.0))
    gc = _rowsum(jnp.where(c["lane"] == h + 8, gs, 0.0))
    dc = jnp.exp(jnp.where(c["tril"], gc - gc_row, -1e30))
    gl = gc[CHUNK - 1:CHUNK, :]
    return beta, dc, jnp.exp(gc), jnp.exp(gl), jnp.exp(gl - gc)


GDN_HB = GDN_HEADS


def _gdn_specs(seq, sb, hb, backward):
    assert hb == GDN_HEADS
    nsb = seq // sb
    ncb = sb // CHUNK
    order = (lambda j: nsb - 1 - j) if backward else (lambda j: j)
    specs = dict(
        wide=lambda: pl.BlockSpec((1, sb, hb * GDN_DK), lambda b, h, j: (b, order(j), h)),
        gs=pl.BlockSpec((1, sb, LANES), lambda b, h, j: (b, order(j), 0)),
        gr=pl.BlockSpec((1, ncb, GDN_HEADS, CHUNK), lambda b, h, j: (b, order(j), 0, 0)),
        st=pl.BlockSpec((1, hb, ncb * GDN_DK, GDN_DK), lambda b, h, j: (b, h, order(j), 0)),
        ti=pl.BlockSpec((1, hb, sb, CHUNK), lambda b, h, j: (b, h, order(j), 0)))
    return nsb, ncb, specs


def _riding_exchange(arrays, scatter, n_in, n_out, grid):
    n = len(arrays)
    if n == 0:
        return [], [], [], lambda body: body
    any_spec = [pl.BlockSpec(memory_space=pl.ANY)] * n

    def wrap(body):
        def wrapped(*refs):
            ins = refs[n_in:n_in + n]
            outs = refs[n_in + n + n_out:n_in + 2 * n + n_out]
            sems = refs[len(refs) - 3:]
            pid = [pl.program_id(a) for a in range(len(grid))]
            first = functools.reduce(lambda a, b: a & b, [p == 0 for p in pid])
            last = functools.reduce(lambda a, b: a & b, [p == g - 1 for p, g in zip(pid, grid)])

            @pl.when(first)
            def _():
                _exchange_phase(ins, outs, sems, scatter, start=True)

            body(*refs[:n_in], *refs[n_in + n:n_in + n + n_out], *refs[n_in + 2 * n + n_out:len(refs) - 3])

            @pl.when(last)
            def _():
                _exchange_phase(ins, outs, sems, scatter, start=False)

        return wrapped

    return any_spec, _exchange_out_shapes(arrays, scatter), _exchange_sems(n), wrap


def _gdn_chunk_fwd(qn, kn, vv, gs, gr, bsz, seq, sb, riders):
    hb = GDN_HB
    nsb, ncb, sp = _gdn_specs(seq, sb, hb, False)
    grid = (bsz, GDN_HEADS // hb, nsb)
    any_spec, rider_shapes, rider_sems, wrap = _riding_exchange(riders, False, 5, 3, grid)

    def body(q_ref, k_ref, v_ref, gs_ref, gr_ref, o_ref, st_ref, ti_ref, s_scr):
        hg = pl.program_id(1)

        @pl.when(pl.program_id(2) == 0)
        def _():
            s_scr[...] = jnp.zeros_like(s_scr)

        c = _chunk_consts()

        def chunk(n, carry):
            r = pl.ds(pl.multiple_of(n * CHUNK, CHUNK), CHUNK)
            rs = pl.ds(pl.multiple_of(n * GDN_DK, GDN_DK), GDN_DK)
            gsv = gs_ref[0, r, :]
            heads = list(range(hb))
            sls = [slice(ih * GDN_DK, (ih + 1) * GDN_DK) for ih in heads]
            q = [q_ref[0, r, sl] for sl in sls]
            k = [k_ref[0, r, sl] for sl in sls]
            v = [v_ref[0, r, sl] for sl in sls]
            beta, dc, eg, egl, ekd = zip(*[
                _gdn_gates(gsv, gr_ref[0, n, pl.ds(ih, 1), :], ih, c) for ih in heads])
            kb = _hmap(lambda a, b: a * b, k, beta)
            amat = _hmap(lambda a, b, d_: jnp.where(c["strict"], _mm(a, b, NT) * d_, 0.0), kb, k, dc)
            tinv = _tri_inv(amat, c["eye"])
            u = _hmap(lambda t_, a, b: _mm3(t_, a * b), tinv, v, beta)
            w = _hmap(lambda t_, a, b: _mm3(t_, a * b), tinv, kb, eg)
            qk = _hmap(lambda a, b, d_: _mm(a, b, NT) * d_, q, k, dc)
            s = [s_scr[ih] for ih in heads]
            v_new = _hmap(lambda a, b, s_: a - _mm(b, s_), u, w, s)
            o = _hmap(lambda a, e, s_, qk_, vn: _mm(a * e, s_) + _mm(qk_, vn), q, eg, s, qk, v_new)
            s_new = _hmap(lambda s_, e, a, f, vn: s_ * e + _mm(a * f, vn, TN), s, egl, k, ekd, v_new)
            for ih in heads:
                o_ref[0, r, sls[ih]] = o[ih]
                st_ref[0, ih, rs, :] = s[ih]
                ti_ref[0, ih, r, :] = tinv[ih]
                s_scr[ih] = s_new[ih]
            return carry

        lax.fori_loop(0, ncb, chunk, 0)

    t3 = (bsz, seq, D_MODEL)
    res = _pcall(
        wrap(body), name="gdn_chunk_fwd", grid=grid,
        in_specs=[sp["wide"](), sp["wide"](), sp["wide"](), sp["gs"], sp["gr"]] + any_spec,
        out_specs=[sp["wide"](), sp["st"], sp["ti"]] + any_spec,
        out_shape=[jax.ShapeDtypeStruct(t3, F32),
                   jax.ShapeDtypeStruct((bsz, GDN_HEADS, (seq // CHUNK) * GDN_DK, GDN_DK), F32),
                   jax.ShapeDtypeStruct((bsz, GDN_HEADS, seq, CHUNK), F32)] + rider_shapes,
        scratch_shapes=[pltpu.VMEM((hb, GDN_DK, GDN_DK), F32)] + rider_sems,
        compiler_params=_params(("arbitrary", "arbitrary", "arbitrary")),
    )(qn, kn, vv, gs, gr, *riders)
    return res[:3], res[3:]


def _ssd_prep(proj, cw, cb, sp, seq, tm):
    t = proj.shape[0]
    d = D_MODEL
    ssd_w = SSD_HEADS * SSD_P

    def body(i, x_ref, xh_ref, bc_ref, bch_ref, sm_ref, cw_ref, cb_ref, sp_ref, xs_ref, bco_ref, dtx_ref, acsx_ref, acs_ref, ypre_ref):
        keep, _ = _seq_flags(i, seq, tm)
        y = _conv(_conv_taps(x_ref[...], xh_ref[...] * keep, 4), cw_ref[:, 0:d]) + cb_ref[:, 0:d]
        ypre_ref[:, 0:d] = y
        xs_ref[...] = y * _sigmoid(y)
        y = _conv(_conv_taps(bc_ref[...], bch_ref[...] * keep, 4), cw_ref[:, d:d + 512]) + cb_ref[:, d:d + 512]
        ypre_ref[:, d:d + 512] = y
        bco_ref[...] = y * _sigmoid(y)
        sm = sm_ref[...]
        lane = lax.broadcasted_iota(jnp.int32, sm.shape, 1)
        valid = (lane >= 16) & (lane < 32)
        dt = jnp.where(valid, _softplus(sm + sp_ref[1:2, :]), 0.0)
        adt = dt * (-jnp.exp(sp_ref[0:1, :]))
        acs = _mmx(_block_tri(tm, False), adt)
        l64 = lax.broadcasted_iota(jnp.int32, (LANES, ssd_w), 0)
        d64 = lax.broadcasted_iota(jnp.int32, (LANES, ssd_w), 1)
        e64 = (l64 - 16 == d64 // SSD_P).astype(F32)
        dtx_ref[...] = _mmsel(dt, e64, terms=3)
        acsx_ref[...] = _mmsel(acs, e64, terms=3)
        acs_ref[...] = acs

    ins = [("row", proj, d, 5), ("prev", proj, d, 5), ("row", proj, 512, 12), ("prev", proj, 512, 12),
           ("row", proj, LANES, SMALL_CB), ("full", cw), ("full", cb), ("full", sp)]
    return _rowwise("ssd_prep", body, t, tm, ins,
                    [(d, F32), (512, F32), (ssd_w, F32), (ssd_w, F32), (LANES, F32), (d + 512, F32)])


SSD_GW = SSD_HPG * SSD_P


def _ssd_head(acs, ar_ref, n, head, cbm, c):
    col = _rowsum(jnp.where(c["lane"] == head + 16, acs, 0.0))
    lm = jnp.exp(jnp.where(c["tril"], col - ar_ref[0, head, pl.ds(n, 1), :], -1e30))
    return lm, cbm * lm


def _ssd_specs(seq, sb):
    nsb = seq // sb
    ncb = sb // CHUNK
    def specs(order):
        return dict(
            wide=lambda: pl.BlockSpec((1, sb, SSD_HEADS * SSD_P), lambda b, j: (b, order(j), 0)),
            bc=lambda: pl.BlockSpec((1, sb, 2 * SSD_GROUPS * SSD_N), lambda b, j: (b, order(j), 0)),
            half=lambda: pl.BlockSpec((1, sb, SSD_GROUPS * SSD_N), lambda b, j: (b, order(j), 0)),
            small=lambda: pl.BlockSpec((1, sb, LANES), lambda b, j: (b, order(j), 0)),
            ar=pl.BlockSpec((1, SSD_HEADS, ncb, CHUNK), lambda b, j: (b, 0, order(j), 0)),
            st=pl.BlockSpec((1, ncb * SSD_N, SSD_HEADS * SSD_P), lambda b, j: (b, order(j), 0)))
    return nsb, ncb, specs(lambda j: j), specs(lambda j: nsb - 1 - j)


def _ssd_chunk_fwd(xs, bc, dtx, acsx, acs, ar, bsz, seq, sb):
    nsb, ncb, sp, _ = _ssd_specs(seq, sb)

    def body(x_ref, dtx_ref, ax_ref, bc_ref, acs_ref, ar_ref, y_ref, sts_ref, st_scr):
        @pl.when(pl.program_id(1) == 0)
        def _():
            st_scr[...] = jnp.zeros_like(st_scr)

        c = _chunk_consts()
        lane5 = lax.broadcasted_iota(jnp.int32, (CHUNK, SSD_GW), 1) // SSD_P

        def chunk(n, carry):
            r = pl.ds(pl.multiple_of(n * CHUNK, CHUNK), CHUNK)
            rs = pl.ds(pl.multiple_of(n * SSD_N, SSD_N), SSD_N)
            acsv = acs_ref[0, r, :]
            for g in range(SSD_GROUPS):
                gl = slice(g * SSD_GW, (g + 1) * SSD_GW)
                x, dt, ax = x_ref[0, r, gl], dtx_ref[0, r, gl], ax_ref[0, r, gl]
                bm = bc_ref[0, r, g * SSD_N:(g + 1) * SSD_N]
                cm = bc_ref[0, r, (SSD_GROUPS + g) * SSD_N:(SSD_GROUPS + g + 1) * SSD_N]
                xdt = x * dt
                cbm = _mm(cm, bm, NT)
                al = ax[CHUNK - 1:CHUNK, :]
                st = st_scr[:, gl]
                y = _mm(cm, st) * jnp.exp(ax)
                for hh in range(SSD_HPG):
                    _, gm = _ssd_head(acsv, ar_ref, n, g * SSD_HPG + hh, cbm, c)
                    y = y + _mm(gm, jnp.where(lane5 == hh, xdt, 0.0))
                y_ref[0, r, gl] = y
                sts_ref[0, rs, gl] = st
                st_scr[:, gl] = st * jnp.exp(al) + _mm(bm, xdt * jnp.exp(al - ax), TN)
            return carry

        lax.fori_loop(0, ncb, chunk, 0)

    return _pcall(
        body, name="ssd_chunk_fwd", grid=(bsz, nsb),
        in_specs=[sp["wide"](), sp["wide"](), sp["wide"](), sp["bc"](), sp["small"](), sp["ar"]],
        out_specs=[sp["wide"](), sp["st"]],
        out_shape=[jax.ShapeDtypeStruct((bsz, seq, SSD_HEADS * SSD_P), F32),
                   jax.ShapeDtypeStruct((bsz, (seq // CHUNK) * SSD_N, SSD_HEADS * SSD_P), F32)],
        scratch_shapes=[pltpu.VMEM((SSD_N, SSD_HEADS * SSD_P), F32)],
        compiler_params=_params(("parallel", "arbitrary")),
    )(xs, dtx, acsx, bc, acs, ar)


def _gate_norm(o_gdn, y_ssd, xs, proj, gnw, snw, dvec, tm):
    t = o_gdn.shape[0]
    d = D_MODEL

    def body(i, o_ref, za_ref, y_ref, xs_ref, zs_ref, gnw_ref, snw_ref, dv_ref, out_ref):
        for hh in range(GDN_HEADS):
            sl = slice(hh * GDN_DK, (hh + 1) * GDN_DK)
            oh, _ = _rms(o_ref[:, sl], GDN_DK)
            z = za_ref[:, sl]
            out_ref[:, sl] = (oh * gnw_ref[...] * (z * _sigmoid(z))).astype(BF16)
        zs = zs_ref[...]
        yg = (y_ref[...] + dv_ref[...] * xs_ref[...]) * (zs * _sigmoid(zs))
        for g in range(SSD_GROUPS):
            sl = slice(g * 512, (g + 1) * 512)
            yh, _ = _rms(yg[:, sl], 512)
            out_ref[:, d + g * 512:d + (g + 1) * 512] = (yh * snw_ref[:, sl]).astype(BF16)

    ins = [("row", o_gdn, d, 0), ("row", proj, d, 3), ("row", y_ssd, d, 0), ("row", xs, d, 0), ("row", proj, d, 4),
           ("full", gnw), ("full", snw), ("full", dvec)]
    return _rowwise("gate_norm", body, t, tm, ins, [(2 * d, BF16)])[0]


def _out_mid(mixin, w_out, x, pmw, pfw):
    d = D_MODEL

    def epilogue(mix, x_ref, pmw_ref, pfw_ref, mix_ref, x1_ref, h2_ref):
        mix_ref[...] = mix
        mh, _ = _rms(mix, d)
        x1 = x_ref[...] + mh * pmw_ref[...]
        x1_ref[...] = x1
        xh, _ = _rms(x1, d)
        h2_ref[...] = (xh * pfw_ref[...]).astype(BF16)

    return _matmul_rows("mm_out_mid", mixin, w_out, "nn", epilogue, [x], [pmw, pfw], [(d, F32), (d, F32), (d, BF16)],
                        tk=2 * d)


def _ffn_act(u_pre, cw, cb, seq, tm):
    t = u_pre.shape[0]

    def body(i, ug_ref, ugh_ref, uu_ref, uuh_ref, cw_ref, cb_ref, act_ref, u_ref):
        keep, _ = _seq_flags(i, seq, tm)
        gate = _conv(_conv_taps(ug_ref[...], ugh_ref[...] * keep, 3), cw_ref[:, 0:D_FF]) + cb_ref[:, 0:D_FF]
        up = _conv(_conv_taps(uu_ref[...], uuh_ref[...] * keep, 3), cw_ref[:, D_FF:2 * D_FF]) + cb_ref[:, D_FF:2 * D_FF]
        u_ref[:, 0:D_FF] = gate
        u_ref[:, D_FF:2 * D_FF] = up
        act_ref[...] = (gate * _sigmoid(gate) * up).astype(BF16)

    ins = [("row", u_pre, D_FF, 0), ("prev", u_pre, D_FF, 0), ("row", u_pre, D_FF, 1), ("prev", u_pre, D_FF, 1),
           ("full", cw), ("full", cb)]
    return _rowwise("ffn_act", body, t, tm, ins, [(D_FF, BF16), (2 * D_FF, F32)])


def _down_final(act, w_down, x1, tgt, w):
    d = D_MODEL

    def epilogue(f, x1_ref, t_ref, w_ref, dy_ref, df_ref, loss_ref, dw_ref):
        fh, r = _rms(f, d)
        e = x1_ref[...] + fh * w_ref[...] - t_ref[...]
        loss_ref[...] += _colsum(e * e) * (0.5 / d)
        dy = e * (1.0 / d)
        dy_ref[...] = dy
        dw_ref[...] += _colsum(dy * fh)
        df_ref[...] = _rms_bwd(fh, r, dy * w_ref[...], d).astype(BF16)

    return _matmul_rows("mm_down_final", act, w_down, "nn", epilogue, [x1, tgt], [w], [(d, F32), (d, BF16)],
                        accs=[(1, d), (1, d)], tk=D_FF)


def _ffn_bwd(u, u_pre, dact, cw, seq, tm):
    t = u.shape[0]

    def body(i, g_ref, gn_ref, up_ref, upn_ref, xg_ref, xu_ref, da_ref, dan_ref, cw_ref, dpre_ref, dcw_ref, dcb_ref):
        _, keep_next = _seq_flags(i, seq, tm)
        ext = lambda a_ref, n_ref: jnp.concatenate([a_ref[...], n_ref[...]], axis=0)
        rows = tm + SUBLANES
        gate, up = ext(g_ref, gn_ref), ext(up_ref, upn_ref)
        sg = _sigmoid(gate)
        da = jnp.concatenate([da_ref[...], dan_ref[...] * keep_next], axis=0)
        for off, grad, x_ref in ((0, da * up * _dsilu(gate, sg), xg_ref), (D_FF, da * gate * sg, xu_ref)):
            x = x_ref[...]
            own = grad[0:tm]
            acc = own * cw_ref[2:3, off:off + D_FF]
            dcb_ref[:, off:off + D_FF] += _colsum(own)
            dcw_ref[2:3, off:off + D_FF] += _colsum(own * x)
            for j in (1, 2):
                ahead = pltpu.roll(grad, rows - j, 0)[0:tm]
                acc = acc + ahead * cw_ref[2 - j:3 - j, off:off + D_FF]
                dcw_ref[2 - j:3 - j, off:off + D_FF] += _colsum(ahead * x)
            dpre_ref[:, off:off + D_FF] = acc.astype(BF16)

    ins = []
    for cb_ in range(2):
        ins += [("row", u, D_FF, cb_), ("next", u, D_FF, cb_)]
    ins += [("row", u_pre, D_FF, 0), ("row", u_pre, D_FF, 1), ("row", dact, D_FF, 0), ("next", dact, D_FF, 0), ("full", cw)]
    return _rowwise("ffn_bwd", body, t, tm, ins, [(2 * D_FF, BF16)], accs=[(SUBLANES, 2 * D_FF), (1, 2 * D_FF)])


def _assemble_dproj(dpre_qkv, dza, dzs, dpre_ssd, dsm, proj, gcw, scw, seq, tm):
    t = dza.shape[0]
    d = D_MODEL

    def body(i, dq_ref, dqn_ref, dk_ref, dkn_ref, dv_ref, dvn_ref, dza_ref, dzs_ref, ds_ref, dsn_ref, dsm_ref,
             xq_ref, xk_ref, xv_ref, xx_ref, xbc_ref, gcw_ref, scw_ref, o_ref, dgcw_ref, dscw_ref):
        _, keep = _seq_flags(i, seq, tm)
        pieces = [(g_ref, n_ref, 0, gcw_ref, dgcw_ref, x_ref, 0, c0) for g_ref, n_ref, x_ref, c0 in (
            (dq_ref, dqn_ref, xq_ref, 0), (dk_ref, dkn_ref, xk_ref, d), (dv_ref, dvn_ref, xv_ref, 2 * d))]
        pieces += [(ds_ref, dsn_ref, c0, scw_ref, dscw_ref, x_ref, 5 * d, c0) for x_ref, c0 in ((xx_ref, 0), (xbc_ref, d))]
        for d_ref, n_ref, dcol, cw_ref, dcw_ref, x_ref, base, c0 in pieces:
            w = x_ref.shape[1]
            x = x_ref[...]
            g = d_ref[:, dcol:dcol + w]
            halo = n_ref[:, dcol:dcol + w] * keep
            acc = g * cw_ref[3:4, c0:c0 + w]
            dcw_ref[3:4, c0:c0 + w] += _colsum(g * x)
            for j in range(1, 4):
                ahead = _shift_up(g, halo, j)
                acc = acc + ahead * cw_ref[3 - j:4 - j, c0:c0 + w]
                dcw_ref[3 - j:4 - j, c0:c0 + w] += _colsum(ahead * x)
            o_ref[:, base + c0:base + c0 + w] = acc.astype(BF16)
        o_ref[:, 3 * d:4 * d] = dza_ref[...]
        o_ref[:, 4 * d:5 * d] = dzs_ref[...]
        o_ref[:, 6 * d + 512:6 * d + 512 + LANES] = dsm_ref[...]
        o_ref[:, 6 * d + 512 + LANES:PROJ_W] = jnp.zeros((tm, PROJ_W - (6 * d + 512 + LANES)), BF16)

    ins = []
    for g in dpre_qkv:
        ins += [("row", g, d, 0), ("next", g, d, 0)]
    ins += [("row", dza, d, 0), ("row", dzs, d, 0),
           ("row", dpre_ssd, d + 512, 0), ("next", dpre_ssd, d + 512, 0), ("row", dsm, LANES, 0),
           ("row", proj, d, 0), ("row", proj, d, 1), ("row", proj, d, 2), ("row", proj, d, 5), ("row", proj, 512, 12),
           ("full", gcw), ("full", scw)]
    return _rowwise("assemble_dproj", body, t, tm, ins, [(PROJ_W, BF16)], accs=[(SUBLANES, 3 * d), (SUBLANES, d + 512)])


def _dh2_mid_bwd(du_pre, w_up, x1, mix, dy, pmw, pfw):
    d = D_MODEL

    def epilogue(dh2, x1_ref, mix_ref, dy_ref, pmw_ref, pfw_ref, dx1_ref, dmix_ref, dpm_ref, dpf_ref):
        xh, r2 = _rms(x1_ref[...], d)
        dpf_ref[...] += _colsum(dh2 * xh)
        dx1 = dy_ref[...] + _rms_bwd(xh, r2, dh2 * pfw_ref[...], d)
        dx1_ref[...] = dx1
        mh, r = _rms(mix_ref[...], d)
        dpm_ref[...] += _colsum(dx1 * mh)
        dmix_ref[...] = _rms_bwd(mh, r, dx1 * pmw_ref[...], d).astype(BF16)

    return _matmul_rows("mm_dh2_mid_bwd", du_pre, w_up, "nt", epilogue, [x1, mix, dy], [pmw, pfw],
                        [(d, F32), (d, BF16)], accs=[(1, d), (1, d)], tk=D_FF)


def _dmixin_gate_norm_bwd(dmix, w_out, o_gdn, y_ssd, xs, proj, gnw, snw, dvec):
    d = D_MODEL

    def epilogue(dmixin, o_ref, za_ref, y_ref, xs_ref, zs_ref, gnw_ref, snw_ref, dv_ref,
                 do_ref, dza_ref, dy_ref, dxs_ref, dzs_ref, dgnw_ref, dsnw_ref, dd_ref):
        for hh in range(GDN_HEADS):
            sl = slice(hh * GDN_DK, (hh + 1) * GDN_DK)
            oh, r = _rms(o_ref[:, sl], GDN_DK)
            z = za_ref[:, sl]
            sz = _sigmoid(z)
            dm = dmixin[:, sl]
            don = dm * (z * sz)
            dza_ref[:, sl] = (dm * oh * gnw_ref[...] * _dsilu(z, sz)).astype(BF16)
            dgnw_ref[...] += _colsum(don * oh)
            do_ref[:, sl] = _rms_bwd(oh, r, don * gnw_ref[...], GDN_DK)
        zs = zs_ref[...]
        sz = _sigmoid(zs)
        sil = zs * sz
        x = xs_ref[...]
        y0 = y_ref[...] + dv_ref[...] * x
        yg = y0 * sil
        dms = dmixin[:, d:2 * d]
        for g in range(SSD_GROUPS):
            sl = slice(g * 512, (g + 1) * 512)
            yh, r = _rms(yg[:, sl], 512)
            dsnw_ref[:, sl] += _colsum(dms[:, sl] * yh)
            dyg = _rms_bwd(yh, r, dms[:, sl] * snw_ref[:, sl], 512)
            dy0 = dyg * sil[:, sl]
            dzs_ref[:, sl] = (dyg * y0[:, sl] * _dsilu(zs[:, sl], sz[:, sl])).astype(BF16)
            dy_ref[:, sl] = dy0
            dxs_ref[:, sl] = dy0 * dv_ref[:, sl]
            dd_ref[:, sl] += _colsum(dy0 * x[:, sl])

    row_ins = [o_gdn, (proj, d, 3), y_ssd, xs, (proj, d, 4)]
    return _matmul_rows("mm_dmixin_gate_norm_bwd", dmix, w_out, "nt", epilogue, row_ins, [gnw, snw, dvec],
                        [(d, F32), (d, BF16), (d, F32), (d, F32), (d, BF16)], accs=[(1, GDN_DK), (1, d), (1, d)], tm=256)


def _ssd_chunk_bwd(xs, bc, dtx, acsx, acs, ar, dy, sts, bsz, seq, sb):
    nsb, ncb, _, sp = _ssd_specs(seq, sb)

    def body(x_ref, dtx_ref, ax_ref, bc_ref, acs_ref, ar_ref, dy_ref, sts_ref, dx_ref, dbc_ref, ddt_ref, dacs_ref, dst_scr):
        @pl.when(pl.program_id(1) == 0)
        def _():
            dst_scr[...] = jnp.zeros_like(dst_scr)

        c = _chunk_consts()
        lane5 = lax.broadcasted_iota(jnp.int32, (CHUNK, SSD_GW), 1) // SSD_P
        row5 = lax.broadcasted_iota(jnp.int32, (CHUNK, SSD_GW), 0)
        sel_in = lax.broadcasted_iota(jnp.int32, (SSD_GW, LANES), 0) // SSD_P
        sel_out = lax.broadcasted_iota(jnp.int32, (SSD_GW, LANES), 1)

        def chunk(nn, carry):
            n = ncb - 1 - nn
            r = pl.ds(pl.multiple_of(n * CHUNK, CHUNK), CHUNK)
            rs = pl.ds(pl.multiple_of(n * SSD_N, SSD_N), SSD_N)
            acsv = acs_ref[0, r, :]
            ddt = jnp.zeros((CHUNK, LANES), F32)
            dacs = jnp.zeros((CHUNK, LANES), F32)
            for g in range(SSD_GROUPS):
                gl = slice(g * SSD_GW, (g + 1) * SSD_GW)
                x, dt, ax, dyv = x_ref[0, r, gl], dtx_ref[0, r, gl], ax_ref[0, r, gl], dy_ref[0, r, gl]
                bm = bc_ref[0, r, g * SSD_N:(g + 1) * SSD_N]
                cm = bc_ref[0, r, (SSD_GROUPS + g) * SSD_N:(SSD_GROUPS + g + 1) * SSD_N]
                st = sts_ref[0, rs, gl]
                dst = dst_scr[:, gl]
                rsel = (sel_in + (16 + g * SSD_HPG) == sel_out).astype(F32)
                xdt = x * dt
                cbm = _mm(cm, bm, NT)
                al = ax[CHUNK - 1:CHUNK, :]
                ex, el = jnp.exp(ax), jnp.exp(al)
                dec = jnp.exp(al - ax)
                xd = xdt * dec
                dye = dyv * ex
                dxd = _mm(bm, dst)
                dxdt = dec * dxd
                dcm = _mm(dye, st, NT)
                dbm = _mm(xd, dst, NT)
                z = dye * _mm(cm, st) - dxd * xd
                zl = _colsum(dst * st) * el + _colsum(dxd * xd)
                z = z + jnp.where(row5 == CHUNK - 1, zl, 0.0)
                dcb = jnp.zeros((CHUNK, CHUNK), F32)
                for hh in range(SSD_HPG):
                    head = g * SSD_HPG + hh
                    lm, gm = _ssd_head(acsv, ar_ref, n, head, cbm, c)
                    dym = jnp.where(lane5 == hh, dyv, 0.0)
                    dxdt = dxdt + _mm(gm, dym, TN)
                    dg = _mm(dym, xdt, NT)
                    dcb = dcb + dg * lm
                    pm = dg * gm
                    dacs = dacs + jnp.where(c["lane"] == head + 16, _rowsum(pm) - _mmsel(pm, c["ones"], TN), 0.0)
                dbc_ref[0, r, (SSD_GROUPS + g) * SSD_N:(SSD_GROUPS + g + 1) * SSD_N] = dcm + _mm(dcb, bm)
                dbc_ref[0, r, g * SSD_N:(g + 1) * SSD_N] = dbm + _mm(dcb, cm, TN)
                dacs = dacs + _mmsel(z, rsel)
                ddt = ddt + _mmsel(dxdt * x, rsel)
                dx_ref[0, r, gl] = dxdt * dt
                dst_scr[:, gl] = dst * el + _mm(cm, dye, TN)
            ddt_ref[0, r, :] = ddt
            dacs_ref[0, r, :] = dacs
            return carry

        lax.fori_loop(0, ncb, chunk, 0)

    return _pcall(
        body, name="ssd_chunk_bwd", grid=(bsz, nsb),
        in_specs=[sp["wide"](), sp["wide"](), sp["wide"](), sp["bc"](), sp["small"](), sp["ar"], sp["wide"](), sp["st"]],
        out_specs=[sp["wide"](), sp["bc"](), sp["small"](), sp["small"]()],
        out_shape=[jax.ShapeDtypeStruct((bsz, seq, SSD_HEADS * SSD_P), F32),
                   jax.ShapeDtypeStruct((bsz, seq, 2 * SSD_GROUPS * SSD_N), F32),
                   jax.ShapeDtypeStruct((bsz, seq, LANES), F32), jax.ShapeDtypeStruct((bsz, seq, LANES), F32)],
        scratch_shapes=[pltpu.VMEM((SSD_N, SSD_HEADS * SSD_P), F32)],
        compiler_params=_params(("parallel", "arbitrary")),
    )(xs, dtx, acsx, bc, acs, ar, dy, sts)


def _ssd_prep_bwd(ypre, proj, dxs_c, dxs_d, dbc, ddt, dacs, dsm_gdn, sp, tm):
    t = proj.shape[0]
    d = D_MODEL

    def body(i, y_ref, sm_ref, dxc_ref, dxd_ref, dbc_ref, ddt_ref, dacs_ref, dsg_ref, sp_ref,
             dpre_ref, dsm_ref, dcb_ref, dsp_ref):
        for off, w, grad in ((0, d, dxc_ref[...] + dxd_ref[...]), (d, 512, dbc_ref[...])):
            y = y_ref[:, off:off + w]
            dpre = grad * _dsilu(y, _sigmoid(y))
            dpre_ref[:, off:off + w] = dpre
            dcb_ref[:, off:off + w] += _colsum(dpre)
        sm = sm_ref[...]
        lane = lax.broadcasted_iota(jnp.int32, sm.shape, 1)
        valid = (lane >= 16) & (lane < 32)
        xb = sm + sp_ref[1:2, :]
        dt = jnp.where(valid, _softplus(xb), 0.0)
        a_neg = -jnp.exp(sp_ref[0:1, :])
        dadt_s = _mmx(_block_tri(tm, True), dacs_ref[...])
        dxb = jnp.where(valid, (ddt_ref[...] + dadt_s * a_neg) * _sigmoid(xb), 0.0)
        dsm_ref[...] = (dsg_ref[...] + dxb).astype(BF16)
        dsp_ref[1:2, :] += _colsum(dxb)
        dsp_ref[0:1, :] += jnp.where(valid[0:1, :], _colsum(dadt_s * dt) * a_neg, 0.0)

    ins = [("row", ypre, d + 512, 0), ("row", proj, LANES, SMALL_CB), ("row", dxs_c, d, 0), ("row", dxs_d, d, 0),
           ("row", dbc, 512, 0), ("row", ddt, LANES, 0), ("row", dacs, LANES, 0), ("row", dsm_gdn, LANES, 0), ("full", sp)]
    return _rowwise("ssd_prep_bwd", body, t, tm, ins, [(d + 512, F32), (LANES, BF16)],
                    accs=[(1, d + 512), (SUBLANES, LANES)])


def _through_norm_silu(g, y, scale):
    sy = _sigmoid(y)
    ds_ = _dsilu(y, sy)
    if scale is None:
        return g * ds_
    a = y * sy
    n = lax.rsqrt(_rowsum(a * a) + EPS)
    ah = a * n
    return (scale * n) * (g - ah * _rowsum(g * ah)) * ds_


def _gdn_chunk_bwd(qn, kn, vv, gs, gr, do, sts, tis, ypre, bsz, seq, sb, riders):
    hb = GDN_HB
    nsb, ncb, sp = _gdn_specs(seq, sb, hb, True)
    grid = (bsz, GDN_HEADS // hb, nsb)
    any_spec, rider_shapes, rider_sems, wrap = _riding_exchange(riders, True, 11, 4, grid)
    ypre_spec = lambda cb: pl.BlockSpec((1, sb, hb * GDN_DK), lambda b, h, j: (b, nsb - 1 - j, cb))

    def body(q_ref, k_ref, v_ref, gs_ref, gr_ref, do_ref, st_ref, ti_ref, yq_ref, yk_ref, yv_ref,
             dq_ref, dk_ref, dv_ref, dgb_ref, ds_scr):
        @pl.when(pl.program_id(2) == 0)
        def _():
            ds_scr[...] = jnp.zeros_like(ds_scr)

        c = _chunk_consts()

        def chunk(nn, carry):
            n = ncb - 1 - nn
            r = pl.ds(pl.multiple_of(n * CHUNK, CHUNK), CHUNK)
            rs = pl.ds(pl.multiple_of(n * GDN_DK, GDN_DK), GDN_DK)
            gsv = gs_ref[0, r, :]
            heads = list(range(hb))
            sls = [slice(ih * GDN_DK, (ih + 1) * GDN_DK) for ih in heads]
            q = [q_ref[0, r, sl] for sl in sls]
            k = [k_ref[0, r, sl] for sl in sls]
            v = [v_ref[0, r, sl] for sl in sls]
            do_ = [do_ref[0, r, sl] for sl in sls]
            s = [st_ref[0, ih, rs, :] for ih in heads]
            tinv = [ti_ref[0, ih, r, :] for ih in heads]
            dsn = [ds_scr[ih] for ih in heads]
            beta, dc, eg, egl, ekd = zip(*[
                _gdn_gates(gsv, gr_ref[0, n, pl.ds(ih, 1), :], ih, c) for ih in heads])
            mul = lambda a, b: a * b
            kb = _hmap(mul, k, beta)
            rhs_w = _hmap(mul, kb, eg)
            u = _hmap(lambda t_, a, b: _mm3(t_, a * b), tinv, v, beta)
            w = _hmap(_mm3, tinv, rhs_w)
            amat = _hmap(lambda a, b, d_: jnp.where(c["strict"], _mm(a, b, NT) * d_, 0.0), kb, k, dc)
            qk = _hmap(lambda a, b, d_: _mm(a, b, NT) * d_, q, k, dc)
            qd = _hmap(mul, q, eg)
            kd = _hmap(mul, k, ekd)
            v_new = _hmap(lambda a, b, s_: a - _mm(b, s_), u, w, s)
            dv_new = _hmap(lambda qk_, d_, kd_, dn: _mm(qk_, d_, TN) + _mm(kd_, dn), qk, do_, kd, dsn)
            dqk = _hmap(lambda d_, vn: _mm(d_, vn, NT), do_, v_new)
            dqd = _hmap(lambda d_, s_: _mm(d_, s_, NT), do_, s)
            ds_new = _hmap(lambda qd_, d_, dn, e, w_, dvn: _mm(qd_, d_, TN) + dn * e - _mm(w_, dvn, TN),
                           qd, do_, dsn, egl, w, dv_new)
            dkd = _hmap(lambda vn, dn: _mm(vn, dn, NT), v_new, dsn)
            dgl = _hmap(lambda s_, dn, e: _colsum(_rowsum(s_ * dn)) * e, s, dsn, egl)
            dw = _hmap(lambda dvn, s_: -_mm(dvn, s_, NT), dv_new, s)
            dru = _hmap(lambda t_, a: _mm3(t_, a, TN), tinv, dv_new)
            drw = _hmap(lambda t_, a: _mm3(t_, a, TN), tinv, dw)
            da = _hmap(lambda a, u_, b, w_: jnp.where(c["strict"], -(_mm(a, u_, NT) + _mm(b, w_, NT)), 0.0), dru, u, drw, w)
            m = _hmap(mul, da, dc)
            dkb = _hmap(lambda a, e, m_, k_: a * e + _mm(m_, k_), drw, eg, m, k)
            mq = _hmap(mul, dqk, dc)
            dq = _hmap(lambda mq_, k_, a, e: _mm(mq_, k_) + a * e, mq, k, dqd, eg)
            dk = _hmap(lambda m_, kb_, mq_, q_, a, e, b, be: _mm(m_, kb_, TN) + _mm(mq_, q_, TN) + a * e + b * be,
                       m, kb, mq, q, dkd, ekd, dkb, beta)
            dbeta = _hmap(lambda a, v_, b, k_: _rowsum(a * v_) + _rowsum(b * k_), dru, v, dkb, k)
            pq = _hmap(lambda a, am, b, qk_: a * am + b * qk_, da, amat, dqk, qk)
            ekk = _hmap(lambda a, b: _rowsum(a * b), dkd, kd)
            dgc = _hmap(lambda pq_, a, rw, b, qd_, e, gl_: (
                _rowsum(pq_) - _mmsel(pq_, c["ones"], TN) + (_rowsum(a * rw) + _rowsum(b * qd_) - e)
                + jnp.where(c["row1"] == CHUNK - 1, _colsum(e) + gl_, 0.0)), pq, drw, rhs_w, dqd, qd, ekk, dgl)
            dv = _hmap(mul, dru, beta)
            dyq = _hmap(lambda g_, sl: _through_norm_silu(g_, yq_ref[0, r, sl], GDN_DK ** -0.5), dq, sls)
            dyk = _hmap(lambda g_, sl: _through_norm_silu(g_, yk_ref[0, r, sl], 1.0), dk, sls)
            dyv = _hmap(lambda g_, sl: _through_norm_silu(g_, yv_ref[0, r, sl], None), dv, sls)
            for ih in heads:
                ds_scr[ih] = ds_new[ih]
                dq_ref[0, r, sls[ih]] = dyq[ih]
                dk_ref[0, r, sls[ih]] = dyk[ih]
                dv_ref[0, r, sls[ih]] = dyv[ih]
                dgb_ref[0, r, sls[ih]] = jnp.where(c["lane"] == 0, dbeta[ih], jnp.where(c["lane"] == 1, dgc[ih], 0.0))
            return carry

        lax.fori_loop(0, ncb, chunk, 0)

    res = _pcall(
        wrap(body), name="gdn_chunk_bwd", grid=grid,
        in_specs=[sp["wide"](), sp["wide"](), sp["wide"](), sp["gs"], sp["gr"], sp["wide"](), sp["st"], sp["ti"],
                  ypre_spec(0), ypre_spec(1), ypre_spec(2)] + any_spec,
        out_specs=[sp["wide"](), sp["wide"](), sp["wide"](), sp["wide"]()] + any_spec,
        out_shape=[jax.ShapeDtypeStruct((bsz, seq, D_MODEL), F32)] * 4 + rider_shapes,
        scratch_shapes=[pltpu.VMEM((hb, GDN_DK, GDN_DK), F32)] + rider_sems,
        compiler_params=_params(("arbitrary", "arbitrary", "arbitrary")),
    )(qn, kn, vv, gs, gr, do, sts, tis, ypre, ypre, ypre, *riders)
    return res[:4], res[4:]


def _gdn_gates_bwd(proj, dgb, gp, tm):
    t = proj.shape[0]
    d = D_MODEL

    def body(i, sm_ref, dgb_ref, gp_ref, dsm_ref, dgp_ref):
        sm = sm_ref[...]
        lane = lax.broadcasted_iota(jnp.int32, sm.shape, 1)
        si = lax.broadcasted_iota(jnp.int32, (d, LANES), 0)
        so = lax.broadcasted_iota(jnp.int32, (d, LANES), 1)
        sel = (((si % GDN_DK == 0) & (so == si // GDN_DK)) | ((si % GDN_DK == 1) & (so == si // GDN_DK + 8))).astype(F32)
        dsel = _mmx(dgb_ref[...], sel)
        is_g = (lane >= 8) & (lane < 16)
        dsel = jnp.where(is_g, _mmx(_block_tri(tm, True), dsel), dsel)
        beta = _sigmoid(sm)
        xb = sm + gp_ref[1:2, :]
        a_neg = -jnp.exp(gp_ref[0:1, :])
        sp = _softplus(xb)
        dxb = jnp.where(is_g, dsel * a_neg * _sigmoid(xb), 0.0)
        dsm_ref[...] = jnp.where(lane < 8, dsel * beta * (1.0 - beta), dxb)
        dgp_ref[1:2, :] += _colsum(dxb)
        dgp_ref[0:1, :] += _colsum(jnp.where(is_g, dsel * a_neg * sp, 0.0))

    ins = [("row", proj, LANES, SMALL_CB), ("row", dgb, d, 0), ("full", gp)]
    return _rowwise("gdn_gates_bwd", body, t, tm, ins, [(LANES, F32)], accs=[(SUBLANES, LANES)])


def _dh1_first_bwd(dproj, wp_in, x, dx1, w, scatter_riders):
    d = D_MODEL

    def epilogue(dh, x_ref, dx1_ref, w_ref, dx_ref, dw_ref):
        xh, r = _rms(x_ref[...], d)
        dw_ref[...] += _colsum(dh * xh)
        dx_ref[...] = dx1_ref[...] + _rms_bwd(xh, r, dh * w_ref[...], d)

    return _matmul_rows("mm_dh1_first_bwd", dproj, wp_in, "nt", epilogue, [x, dx1], [w], [(d, F32)], accs=[(1, d)],
                        tk=PROJ_W // 2, scatter_riders=scatter_riders)


def _gather_two_level(name, arrays):
    n = len(arrays)
    n_sem = 7

    def body(*refs):
        ins, outs = refs[:n], refs[n:2 * n]
        send_sems, recv_sems, loc_sems = refs[2 * n:]
        x, y, c = lax.axis_index("x"), lax.axis_index("y"), lax.axis_index("c")
        slot = lambda px, py, pc: 4 * px + 2 * py + pc
        sibling = (x, y, 1 - c)
        chips = [(1 - x, y), (x, 1 - y), (1 - x, 1 - y)]

        def copy(t, k, src, block, to):
            return pltpu.make_async_remote_copy(
                src_ref=src, dst_ref=outs[t].at[block], send_sem=send_sems.at[t, k], recv_sem=recv_sems.at[t, k],
                device_id=to, device_id_type=pl.DeviceIdType.MESH)

        own, first, passed = [], [], []
        for t in range(n):
            own.append(pltpu.make_async_copy(ins[t], outs[t].at[slot(x, y, c)], loc_sems.at[t]))
            first.append(copy(t, 0, ins[t], slot(x, y, c), sibling))
            first += [copy(t, 1 + j, ins[t], slot(x, y, c), (px, py, c)) for j, (px, py) in enumerate(chips)]
        for cp in own + first:
            cp.start()
        for t in range(n):
            for j, (px, py) in enumerate(chips):
                copy(t, 1 + j, ins[t], slot(px, py, c), (px, py, c)).wait_recv()
                fwd = copy(t, 4 + j, outs[t].at[slot(px, py, c)], slot(px, py, c), sibling)
                fwd.start()
                passed.append(fwd)
        for t in range(n):
            copy(t, 0, ins[t], slot(x, y, 1 - c), sibling).wait_recv()
            for j, (px, py) in enumerate(chips):
                copy(t, 4 + j, ins[t], slot(px, py, 1 - c), sibling).wait_recv()
        for cp in first + passed:
            cp.wait_send()
        for cp in own:
            cp.wait()

    return _pcall(
        body, name=name,
        in_specs=[pl.BlockSpec(memory_space=pl.ANY)] * n,
        out_specs=[pl.BlockSpec(memory_space=pl.ANY)] * n,
        out_shape=_exchange_out_shapes(arrays, False),
        scratch_shapes=[pltpu.SemaphoreType.DMA((n, n_sem)), pltpu.SemaphoreType.DMA((n, n_sem)), pltpu.SemaphoreType.DMA((n,))],
    )(*arrays)


def _exchange_out_shapes(arrays, scatter):
    return [jax.ShapeDtypeStruct(a.shape if scatter else (N_DEV,) + a.shape, a.dtype) for a in arrays]


def _exchange_sems(n):
    return [pltpu.SemaphoreType.DMA((n, N_DEV - 1)), pltpu.SemaphoreType.DMA((n, N_DEV - 1)), pltpu.SemaphoreType.DMA((n,))]


def _exchange_phase(ins, outs, sems, scatter, start):
    send_sems, recv_sems, loc_sems = sems
    x, y, c = lax.axis_index("x"), lax.axis_index("y"), lax.axis_index("c")
    me = 4 * x + 2 * y + c
    for t in range(len(ins)):
        loc = pltpu.make_async_copy(ins[t].at[me] if scatter else ins[t], outs[t].at[me], loc_sems.at[t])
        if start:
            loc.start()
        else:
            loc.wait()
        for k in range(N_DEV - 1):
            bx, by, bc = ((k + 1) >> 2) & 1, ((k + 1) >> 1) & 1, (k + 1) & 1
            px = 1 - x if bx else x
            py = 1 - y if by else y
            pc = 1 - c if bc else c
            peer = 4 * px + 2 * py + pc
            src = ins[t].at[peer] if scatter else ins[t]
            copy = lambda dst: pltpu.make_async_remote_copy(
                src_ref=src, dst_ref=dst, send_sem=send_sems.at[t, k], recv_sem=recv_sems.at[t, k],
                device_id=(px, py, pc), device_id_type=pl.DeviceIdType.MESH)
            if start:
                copy(outs[t].at[me]).start()
            else:
                copy(outs[t].at[me]).wait_send()
                copy(outs[t].at[peer]).wait_recv()


def _adam_math(w, g, m, v):
    m = ADAM_B1 * m + (1.0 - ADAM_B1) * g
    v = ADAM_B2 * v + (1.0 - ADAM_B2) * (g * g)
    m_hat = m / (1.0 - ADAM_B1 ** ADAM_STEP)
    v_hat = v / (1.0 - ADAM_B2 ** ADAM_STEP)
    delta = -ADAM_LR * (m_hat / (jnp.sqrt(v_hat) + ADAM_EPS) + ADAM_WD * w)
    return delta, m, v


def _adam_big(name, parts, w, m, v, tm):
    r, c = w.shape
    tm = tm if r % tm == 0 else r

    def body(p_ref, w_ref, m_ref, v_ref, g_ref, d_ref, nm_ref, nv_ref):
        g = p_ref[0].astype(F32)
        for s in range(1, N_DEV):
            g = g + p_ref[s].astype(F32)
        g_ref[...] = g
        d_ref[...], nm_ref[...], nv_ref[...] = _adam_math(w_ref[...], g, m_ref[...], v_ref[...])

    blk = lambda: pl.BlockSpec((tm, c), lambda i: (i, 0))
    return _pcall(
        body, name=name, grid=(r // tm,),
        in_specs=[pl.BlockSpec((N_DEV, tm, c), lambda i: (0, i, 0)), blk(), blk(), blk()],
        out_specs=[blk(), blk(), blk(), blk()],
        out_shape=[jax.ShapeDtypeStruct((r, c), F32)] * 4,
        compiler_params=_params(("parallel",)),
    )(parts, w, m, v)


SMALL_ROWS = 56
ROW_DD, ROW_LOSS = 5, 6


def _small_sum(gathered):
    def body(g_ref, o_ref, x_ref):
        s = g_ref[0]
        for dev in range(1, N_DEV):
            s = s + g_ref[dev]
        o_ref[...] = s
        ri = lax.broadcasted_iota(jnp.int32, (D_MODEL, LANES), 0)
        ro = lax.broadcasted_iota(jnp.int32, (D_MODEL, LANES), 1)
        heads = _mmx(jnp.broadcast_to(s[ROW_DD:ROW_DD + 1, :], (SUBLANES, D_MODEL)), (ri // SSD_P == ro).astype(F32))
        loss = _rowsum(jnp.broadcast_to(s[ROW_LOSS:ROW_LOSS + 1, :], (SUBLANES, D_MODEL)))
        row = lax.broadcasted_iota(jnp.int32, (SUBLANES, LANES), 0)
        x_ref[...] = jnp.where(row == 0, heads, jnp.broadcast_to(loss, (SUBLANES, LANES)))

    return _pcall(
        body, name="small_sum",
        out_shape=[jax.ShapeDtypeStruct((SMALL_ROWS, D_MODEL), F32), jax.ShapeDtypeStruct((SUBLANES, LANES), F32)],
        compiler_params=_params(None),
    )(gathered)


def _adam_small(g, w, m, v):
    def body(g_ref, w_ref, m_ref, v_ref, d_ref, nm_ref, nv_ref):
        d_ref[...], nm_ref[...], nv_ref[...] = _adam_math(w_ref[...], g_ref[...], m_ref[...], v_ref[...])

    return _pcall(body, name="adam_small", out_shape=[jax.ShapeDtypeStruct(g.shape, F32)] * 3,
                  compiler_params=_params(None))(g, w, m, v)


def _pack(pieces, rows):
    flat = jnp.concatenate([p.reshape(-1).astype(F32) for p in pieces])
    return jnp.pad(flat, (0, rows * D_MODEL - flat.shape[0])).reshape(rows, D_MODEL)


def _unpack(packed, shapes):
    flat = packed.reshape(-1)
    out, off = [], 0
    for shp in shapes:
        size = 1
        for s in shp:
            size *= s
        out.append(flat[off:off + size].reshape(shp))
        off += size
    return out


def _permute_in(w):
    pad = jnp.zeros((w.shape[0], PROJ_W - D_IN), w.dtype)
    return jnp.concatenate([w[:, 0:4096], w[:, 4112:6672], w[:, 4096:4112], w[:, 6672:6688], pad], axis=1)


def _unpermute_in(g):
    return jnp.concatenate([g[:, 0:4096], g[:, 6656:6672], g[:, 4096:6656], g[:, 6672:6688]], axis=1)


def _lane_row(vec, start):
    return jnp.zeros((LANES,), F32).at[start:start + vec.shape[0]].set(vec)


def _cols_from_shards(g):
    return jnp.transpose(g, (1, 0, 2)).reshape(g.shape[1], N_DEV * g.shape[2])


def _cols_to_shards(a):
    return jnp.transpose(a.astype(BF16).reshape(a.shape[0], N_DEV, a.shape[1] // N_DEV), (1, 0, 2))


def _rows_to_shards(a):
    return a.astype(BF16).reshape(N_DEV, a.shape[0] // N_DEV, a.shape[1])


def _local_step(x, tgt, wp_in, rest, p, rest_is_sharded):
    bsz, seq, d = x.shape
    t = bsz * seq
    x2 = x.reshape(t, d)
    tgt2 = tgt.reshape(t, d)
    tm = min(256, seq)
    tm_wide = min(128, seq)
    sb = min(512, seq)

    gp = jnp.zeros((SUBLANES, LANES), F32).at[0].set(_lane_row(p["gdn_a_log"], 8)).at[1].set(_lane_row(p["gdn_dt_bias"], 8))
    sp = jnp.zeros((SUBLANES, LANES), F32).at[0].set(_lane_row(p["ssd_a_log"], 16)).at[1].set(_lane_row(p["ssd_dt_bias"], 16))
    dvec = jnp.repeat(p["ssd_d"], SSD_P).reshape(1, d)
    row = lambda v: v.reshape(1, -1)
    pre_mix, post_mix, pre_ffn, post_ffn = (row(p[k]) for k in ("pre_mix_norm", "post_mix_norm", "pre_ffn_norm", "post_ffn_norm"))
    gnw, snw = row(p["gdn_norm_w"]), row(p["ssd_norm_w"])
    gcw, scw, scb, fcw, fcb = p["gdn_conv_w"], p["ssd_conv_w"], row(p["ssd_conv_b"]), p["ffn_conv_w"], row(p["ffn_conv_b"])

    h1 = _norm_cast("norm_in", x2, pre_mix, tm)
    proj = _matmul("mm_proj", h1, wp_in, "nn", F32)
    b3 = lambda a: a.reshape(bsz, seq, a.shape[-1])
    b2 = lambda a: a.reshape(t, a.shape[-1])
    rows_of = lambda a, lo, n: jnp.transpose(a[:, lo:lo + n].reshape(bsz, seq // CHUNK, CHUNK, n), (0, 3, 1, 2))
    qn, kn, vv, gs, ypre_gdn = _gdn_prep(proj, gcw, gp, seq, tm)
    qn, kn, vv, gs = b3(qn), b3(kn), b3(vv), b3(gs)
    gr = jnp.transpose(b2(gs)[:, 8:8 + GDN_HEADS].reshape(bsz, seq // CHUNK, CHUNK, GDN_HEADS), (0, 1, 3, 2))
    (o_gdn, gdn_st, gdn_ti), gathered = _gdn_chunk_fwd(qn, kn, vv, gs, gr, bsz, seq, sb, list(rest) if rest_is_sharded else [])
    if rest_is_sharded:
        w_out, w_up, w_down = gathered[0].reshape(-1, d), _cols_from_shards(gathered[1]), gathered[2].reshape(-1, d)
    else:
        w_out, w_up, w_down = rest
    o_gdn = b2(o_gdn)
    xs, bc, dtx, acsx, acs, ypre_ssd = _ssd_prep(proj, scw, scb, sp, seq, tm)
    ar = rows_of(acs, 16, SSD_HEADS)
    y_ssd, ssd_st = _ssd_chunk_fwd(b3(xs), b3(bc), b3(dtx), b3(acsx), b3(acs), ar, bsz, seq, sb)
    y_ssd = b2(y_ssd)
    mixin = _gate_norm(o_gdn, y_ssd, xs, proj, gnw, snw, dvec, tm)
    mix, x1, h2 = _out_mid(mixin, w_out, x2, post_mix, pre_ffn)
    u_pre = _matmul("mm_up", h2, w_up, "nn", F32)
    act, u = _ffn_act(u_pre, fcw, fcb, seq, tm_wide)
    dy, df, loss_lanes, d_post_ffn = _down_final(act, w_down, x1, tgt2, post_ffn)

    g_down = _matmul("mm_dw_down", act, df, "tn", BF16, tm=1408, tk=2048)
    dact = _matmul("mm_dact", df, w_down, "nt", F32, tn=1408)
    du_pre, d_fcw, d_fcb = _ffn_bwd(u, u_pre, dact, fcw, seq, tm_wide)
    g_up = _matmul("mm_dw_up", h2, du_pre, "tn", BF16, tk=2048)
    dx1, dmix, d_post_mix, d_pre_ffn = _dh2_mid_bwd(du_pre, w_up, x1, mix, dy, post_mix, pre_ffn)
    g_out = _matmul("mm_dw_out", mixin, dmix, "tn", BF16, tk=2048)
    do_gdn, dza, dy_ssd, dxs_d, dzs, d_gnw, d_snw, d_dd = _dmixin_gate_norm_bwd(dmix, w_out, o_gdn, y_ssd, xs, proj, gnw, snw, dvec)
    dxs_c, dbc, ddt, dacs = (b2(a) for a in _ssd_chunk_bwd(
        b3(xs), b3(bc), b3(dtx), b3(acsx), b3(acs), ar, b3(dy_ssd), ssd_st, bsz, seq, sb))
    riders = [_rows_to_shards(g_out), _cols_to_shards(g_up), _rows_to_shards(g_down)] if rest_is_sharded else []
    dgdn, received = _gdn_chunk_bwd(qn, kn, vv, gs, gr, b3(do_gdn), gdn_st, gdn_ti, b3(ypre_gdn), bsz, seq, min(256, seq), riders)
    if rest_is_sharded:
        g_out, g_up, g_down = received
    dyq, dyk, dyv, dgb = (b2(a) for a in dgdn)
    dsm_gdn, d_gp = _gdn_gates_bwd(proj, dgb, gp, tm)
    dpre_ssd, dsm, d_scb, d_sp = _ssd_prep_bwd(ypre_ssd, proj, dxs_c, dxs_d, dbc, ddt, dacs, dsm_gdn, sp, tm)
    dproj, d_gcw, d_scw = _assemble_dproj((dyq, dyk, dyv), dza, dzs, dpre_ssd, dsm, proj, gcw, scw, seq, tm)
    g_in = _matmul("mm_dw_in", h1, dproj, "tn", BF16, tk=2048)
    if rest_is_sharded:
        (dx, d_pre_mix), (g_in,) = _dh1_first_bwd(dproj, wp_in, x2, dx1, pre_mix, [_cols_to_shards(_unpermute_in(g_in))])
    else:
        dx, d_pre_mix = _dh1_first_bwd(dproj, wp_in, x2, dx1, pre_mix, [])

    small = dict(pre_mix_norm=d_pre_mix, ssd_norm_w=d_snw, post_mix_norm=d_post_mix, pre_ffn_norm=d_pre_ffn,
                 post_ffn_norm=d_post_ffn, dd_lanes=d_dd, loss_lanes=loss_lanes, gdn_gates=d_gp, ssd_gates=d_sp,
                 gdn_norm_w=d_gnw, gdn_conv_w=d_gcw[0:4], ssd_conv_w=d_scw[0:4], ssd_conv_b=d_scb,
                 ffn_conv_w=d_fcw[0:3], ffn_conv_b=d_fcb)
    return dx.reshape(bsz, seq, d), g_in, g_out, g_up, g_down, small


def kernel(x, pre_mix_norm, w_in, gdn_conv_w, gdn_a_log, gdn_dt_bias, gdn_norm_w, ssd_conv_w, ssd_conv_b, ssd_a_log, ssd_dt_bias, ssd_d, ssd_norm_w, w_out, post_mix_norm, pre_ffn_norm, w_up, ffn_conv_w, ffn_conv_b, w_down, post_ffn_norm, loss_target, m_pre_mix_norm, m_w_in, m_gdn_conv_w, m_gdn_a_log, m_gdn_dt_bias, m_gdn_norm_w, m_ssd_conv_w, m_ssd_conv_b, m_ssd_a_log, m_ssd_dt_bias, m_ssd_d, m_ssd_norm_w, m_w_out, m_post_mix_norm, m_pre_ffn_norm, m_w_up, m_ffn_conv_w, m_ffn_conv_b, m_w_down, m_post_ffn_norm, v_pre_mix_norm, v_w_in, v_gdn_conv_w, v_gdn_a_log, v_gdn_dt_bias, v_gdn_norm_w, v_ssd_conv_w, v_ssd_conv_b, v_ssd_a_log, v_ssd_dt_bias, v_ssd_d, v_ssd_norm_w, v_w_out, v_post_mix_norm, v_pre_ffn_norm, v_w_up, v_ffn_conv_w, v_ffn_conv_b, v_w_down, v_post_ffn_norm):
    names = ["pre_mix_norm", "w_in", "gdn_conv_w", "gdn_a_log", "gdn_dt_bias", "gdn_norm_w", "ssd_conv_w", "ssd_conv_b",
             "ssd_a_log", "ssd_dt_bias", "ssd_d", "ssd_norm_w", "w_out", "post_mix_norm", "pre_ffn_norm", "w_up",
             "ffn_conv_w", "ffn_conv_b", "w_down", "post_ffn_norm"]
    w_args = [pre_mix_norm, w_in, gdn_conv_w, gdn_a_log, gdn_dt_bias, gdn_norm_w, ssd_conv_w, ssd_conv_b, ssd_a_log, ssd_dt_bias, ssd_d, ssd_norm_w, w_out, post_mix_norm, pre_ffn_norm, w_up, ffn_conv_w, ffn_conv_b, w_down, post_ffn_norm]
    m_args = [m_pre_mix_norm, m_w_in, m_gdn_conv_w, m_gdn_a_log, m_gdn_dt_bias, m_gdn_norm_w, m_ssd_conv_w, m_ssd_conv_b, m_ssd_a_log, m_ssd_dt_bias, m_ssd_d, m_ssd_norm_w, m_w_out, m_post_mix_norm, m_pre_ffn_norm, m_w_up, m_ffn_conv_w, m_ffn_conv_b, m_w_down, m_post_ffn_norm]
    v_args = [v_pre_mix_norm, v_w_in, v_gdn_conv_w, v_gdn_a_log, v_gdn_dt_bias, v_gdn_norm_w, v_ssd_conv_w, v_ssd_conv_b, v_ssd_a_log, v_ssd_dt_bias, v_ssd_d, v_ssd_norm_w, v_w_out, v_post_mix_norm, v_pre_ffn_norm, v_w_up, v_ffn_conv_w, v_ffn_conv_b, v_w_down, v_post_ffn_norm]
    w = {k: a[0] for k, a in zip(names, w_args)}
    m = {k: a[0] for k, a in zip(names, m_args)}
    v = {k: a[0] for k, a in zip(names, v_args)}
    idx = 4 * lax.axis_index("x") + 2 * lax.axis_index("y") + lax.axis_index("c")
    big = ("w_in", "w_out", "w_up", "w_down")
    conv = ("gdn_conv_w", "ssd_conv_w", "ffn_conv_w")

    conv_local = jnp.concatenate([jnp.pad(w[k], ((0, 4 - w[k].shape[0]), (0, 0))) for k in conv], axis=1)
    g_in, g_conv = _gather_two_level("gather_weights", [w["w_in"].astype(BF16), conv_local])
    wp_in = _permute_in(_cols_from_shards(g_in))
    p = {k: w[k] for k in names if k not in big and k not in conv}
    off = 0
    for k in conv:
        cw = w[k].shape[1]
        p[k] = jnp.transpose(g_conv[:, :w[k].shape[0], off:off + cw], (1, 0, 2)).reshape(w[k].shape[0], N_DEV * cw)
        off += cw

    rest = tuple(w[k].astype(BF16) for k in ("w_out", "w_up", "w_down"))
    dx, p_in, p_out, p_up, p_down, small = _local_step(x, loss_target, wp_in, rest, p, True)

    gate_row = jnp.concatenate([small["gdn_gates"][0], small["gdn_gates"][1], small["ssd_gates"][0], small["ssd_gates"][1],
                                small["gdn_norm_w"][0], jnp.zeros((D_MODEL - 5 * LANES,), F32)]).reshape(1, D_MODEL)
    pack = _pack([small["pre_mix_norm"], small["ssd_norm_w"], small["post_mix_norm"], small["pre_ffn_norm"],
                  small["post_ffn_norm"], small["dd_lanes"], small["loss_lanes"], gate_row,
                  small["gdn_conv_w"], small["ssd_conv_w"], jnp.pad(small["ssd_conv_b"], ((0, 0), (0, 512))),
                  jnp.pad(small["ffn_conv_w"].reshape(-1), (0, 17 * D_MODEL - 3 * 2 * D_FF)),
                  jnp.pad(small["ffn_conv_b"], ((0, 0), (0, 512)))], SMALL_ROWS)
    (pack_all,) = _gather_two_level("gather_small", [pack])
    ssum, extra = _small_sum(pack_all)

    grads, deltas, new_m, new_v = {}, {}, {}, {}
    for k, parts in (("w_in", p_in), ("w_out", p_out), ("w_up", p_up), ("w_down", p_down)):
        grads[k], deltas[k], new_m[k], new_v[k] = _adam_big("adam_" + k, parts, w[k], m[k], v[k], 256)

    flat = ssum.reshape(-1)
    gate = ssum[7]
    sg = dict(pre_mix_norm=ssum[0], ssd_norm_w=ssum[1], post_mix_norm=ssum[2], pre_ffn_norm=ssum[3], post_ffn_norm=ssum[4],
              gdn_a_log=gate[8:16], gdn_dt_bias=gate[LANES + 8:LANES + 16], ssd_a_log=gate[2 * LANES + 16:2 * LANES + 32],
              ssd_dt_bias=gate[3 * LANES + 16:3 * LANES + 32], gdn_norm_w=gate[4 * LANES:5 * LANES], ssd_d=extra[0, 0:SSD_HEADS])
    o = 8 * D_MODEL
    full_gcw = flat[o:o + 4 * 3072].reshape(4, 3072)
    o += 12 * D_MODEL
    full_scw = flat[o:o + 4 * 1536].reshape(4, 1536)
    o += 6 * D_MODEL
    sg["ssd_conv_b"] = flat[o:o + 1536]
    o += 2 * D_MODEL
    full_fcw = flat[o:o + 3 * 2 * D_FF].reshape(3, 2 * D_FF)
    o += 17 * D_MODEL
    sg["ffn_conv_b"] = flat[o:o + 2 * D_FF]
    for k, full in (("gdn_conv_w", full_gcw), ("ssd_conv_w", full_scw), ("ffn_conv_w", full_fcw)):
        cw = w[k].shape[1]
        sg[k] = lax.dynamic_slice_in_dim(full, idx * cw, cw, axis=1)
    small_names = [k for k in names if k not in big]
    rows = 24
    gpk = _pack([sg[k] for k in small_names], rows)
    dpk, mpk, vpk = _adam_small(gpk, _pack([w[k] for k in small_names], rows), _pack([m[k] for k in small_names], rows),
                                _pack([v[k] for k in small_names], rows))
    shapes = [w[k].shape for k in small_names]
    for k, g_, d_, m_, v_ in zip(small_names, _unpack(gpk, shapes), _unpack(dpk, shapes), _unpack(mpk, shapes), _unpack(vpk, shapes)):
        grads[k], deltas[k], new_m[k], new_v[k] = g_, d_, m_, v_

    loss = extra[1, 0]
    lead = lambda a: a[None]
    return (loss, dx, *[lead(grads[k]) for k in names], *[lead(deltas[k]) for k in names],
            *[lead(new_m[k]) for k in names], *[lead(new_v[k]) for k in names])
```

```python
import functools

import jax
import jax.numpy as jnp
from jax import lax
from jax.experimental import pallas as pl
from jax.experimental.pallas import tpu as pltpu

F32 = jnp.float32
BF16 = jnp.bfloat16
MXU_DTYPE = jnp.bfloat16
HIGHEST = lax.Precision.HIGHEST
VMEM_LIMIT_V7X = 48 * 1024 * 1024
SUBLANES = 8
LANES = 128

D_MODEL = 1024
GDN_HEADS = 8
GDN_DK = 128
SSD_HEADS = 16
SSD_P = 64
SSD_GROUPS = 2
SSD_HPG = 8
SSD_N = 128
CHUNK = 64
D_FF = 2816
EPS = 1e-6
N_DEV = 8
PROJ_W = 7168
SMALL_CB = 52
D_IN = 6688

ADAM_LR = 0.001
ADAM_B1 = 0.9
ADAM_B2 = 0.999
ADAM_EPS = 1e-08
ADAM_WD = 0.01
ADAM_STEP = 10

NN = (((1,), (0,)), ((), ()))
NT = (((1,), (1,)), ((), ()))
TN = (((0,), (0,)), ((), ()))


def _pcall(body, **kw):
    return pl.pallas_call(body, **kw)


def _mm(a, b, dims=NN):
    return lax.dot_general(a.astype(MXU_DTYPE), b.astype(MXU_DTYPE), dims, preferred_element_type=F32)


def _mmx(a, b, dims=NN):
    return lax.dot_general(a, b, dims, precision=HIGHEST, preferred_element_type=F32)


def _split(a):
    hi = a.astype(MXU_DTYPE)
    return hi, (a - hi.astype(F32)).astype(MXU_DTYPE)


def _mm3(a, b, dims=NN):
    (ah, al), (bh, bl) = _split(a), _split(b)
    dot = lambda p, q: lax.dot_general(p, q, dims, preferred_element_type=F32)
    return dot(ah, bh) + (dot(ah, bl) + dot(al, bh))


def _mmsel(a, sel, dims=NN, terms=2):
    s = sel.astype(MXU_DTYPE)
    out = None
    for _ in range(terms):
        part = a.astype(MXU_DTYPE)
        a = a - part.astype(F32)
        prod = lax.dot_general(part, s, dims, preferred_element_type=F32)
        out = prod if out is None else out + prod
    return out


def _sigmoid(x):
    return 0.5 * jnp.tanh(0.5 * x) + 0.5


def _softplus(x):
    return jnp.maximum(x, 0.0) + jnp.log(1.0 + jnp.exp(-jnp.abs(x)))


def _dsilu(x, s):
    return s * (1.0 + x * (1.0 - s))


def _rowsum(x):
    return jnp.sum(x, axis=1, keepdims=True)


def _colsum(x):
    return jnp.sum(x, axis=0, keepdims=True)


def _pick(dim, pref):
    if dim <= pref:
        return dim
    best = None
    t = LANES
    while t <= pref:
        if dim % t == 0:
            best = t
        t += LANES
    return dim if best is None else best


def _params(sem):
    return pltpu.CompilerParams(dimension_semantics=sem, vmem_limit_bytes=VMEM_LIMIT_V7X)


def _matmul(name, a, b, mode, out_dtype, tm=1024, tn=1024, tk=1024):
    if mode == "nn":
        (m, k), (_, n) = a.shape, b.shape
    elif mode == "nt":
        (m, k), (n, _) = a.shape, b.shape
    else:
        (k, m), (_, n) = a.shape, b.shape
    tm, tn, tk = _pick(m, tm), _pick(n, tn), _pick(k, tk)
    nk = k // tk
    if mode == "tn":
        a_spec = pl.BlockSpec((tk, tm), lambda i, j, kk: (kk, i))
    else:
        a_spec = pl.BlockSpec((tm, tk), lambda i, j, kk: (i, kk))
    if mode == "nt":
        b_spec = pl.BlockSpec((tn, tk), lambda i, j, kk: (j, kk))
    else:
        b_spec = pl.BlockSpec((tk, tn), lambda i, j, kk: (kk, j))
    dims = {"nn": NN, "nt": NT, "tn": TN}[mode]

    def body(a_ref, b_ref, o_ref, *acc):
        if nk == 1:
            o_ref[...] = _mm(a_ref[...], b_ref[...], dims).astype(out_dtype)
            return
        kk = pl.program_id(2)

        @pl.when(kk == 0)
        def _():
            acc[0][...] = jnp.zeros_like(acc[0])

        acc[0][...] += _mm(a_ref[...], b_ref[...], dims)

        @pl.when(kk == nk - 1)
        def _():
            o_ref[...] = acc[0][...].astype(out_dtype)

    return _pcall(
        body, name=name, grid=(m // tm, n // tn, nk),
        in_specs=[a_spec, b_spec],
        out_specs=pl.BlockSpec((tm, tn), lambda i, j, kk: (i, j)),
        out_shape=jax.ShapeDtypeStruct((m, n), out_dtype),
        scratch_shapes=[pltpu.VMEM((tm, tn), F32)] if nk > 1 else [],
        compiler_params=_params(("parallel", "parallel", "arbitrary")),
    )(a, b)


def _matmul_rows(name, a, b, mode, epilogue, row_ins, full_ins, outs, accs=(), tm=512, tk=1024, scatter_riders=()):
    if mode == "nn":
        (m, k), (_, n) = a.shape, b.shape
    else:
        (m, k), (n, _) = a.shape, b.shape
    tm, tk = _pick(m, tm), _pick(k, tk)
    nk = k // tk
    a_spec = pl.BlockSpec((tm, tk), lambda i, kk: (i, kk))
    b_spec = pl.BlockSpec((n, tk), lambda i, kk: (0, kk)) if mode == "nt" else pl.BlockSpec((tk, n), lambda i, kk: (kk, 0))
    dims = NT if mode == "nt" else NN
    n_row, n_full, n_out, n_acc = len(row_ins), len(full_ins), len(outs), len(accs)

    def body(a_ref, b_ref, *rest):
        ins = rest[:n_row + n_full]
        out_refs = rest[n_row + n_full:n_row + n_full + n_out]
        acc_refs = rest[n_row + n_full + n_out:n_row + n_full + n_out + n_acc]
        prod_scr = rest[-1]
        i, kk = pl.program_id(0), pl.program_id(1)

        if n_acc:
            @pl.when((i == 0) & (kk == 0))
            def _():
                for r in acc_refs:
                    r[...] = jnp.zeros_like(r)

        if nk == 1:
            epilogue(_mm(a_ref[...], b_ref[...], dims), *ins, *out_refs, *acc_refs)
            return

        @pl.when(kk == 0)
        def _():
            prod_scr[...] = jnp.zeros_like(prod_scr)

        prod_scr[...] += _mm(a_ref[...], b_ref[...], dims)

        @pl.when(kk == nk - 1)
        def _():
            epilogue(prod_scr[...], *ins, *out_refs, *acc_refs)

    grid = (m // tm, nk)
    riders = list(scatter_riders)
    n_in = 2 + n_row + n_full
    any_spec, rider_shapes, rider_sems, wrap = _riding_exchange(riders, True, n_in, n_out + n_acc, grid)
    row_ins = [r if isinstance(r, tuple) else (r, r.shape[1], 0) for r in row_ins]
    in_specs = [a_spec, b_spec] + [pl.BlockSpec((tm, w), lambda i, kk, cb=cb: (i, cb)) for _, w, cb in row_ins]
    in_specs += [pl.BlockSpec(f.shape, lambda i, kk, nd=f.ndim: (0,) * nd) for f in full_ins]
    row_ins = [r for r, _, _ in row_ins]
    out_specs = [pl.BlockSpec((tm, w), lambda i, kk: (i, 0)) for w, _ in outs]
    out_specs += [pl.BlockSpec(s, lambda i, kk: (0, 0)) for s in accs]
    out_shape = [jax.ShapeDtypeStruct((m, w), dt) for w, dt in outs] + [jax.ShapeDtypeStruct(s, F32) for s in accs]
    res = _pcall(
        wrap(body), name=name, grid=grid,
        in_specs=in_specs + any_spec, out_specs=out_specs + any_spec, out_shape=out_shape + rider_shapes,
        scratch_shapes=[pltpu.VMEM((tm, n), F32)] + rider_sems,
        compiler_params=_params(("arbitrary", "arbitrary")),
    )(a, b, *row_ins, *full_ins, *riders)
    return (res[:n_out + n_acc], res[n_out + n_acc:]) if riders else res


def _rowwise(name, body, n_rows, tm, ins, outs, accs=()):
    arrays, in_specs = [], []
    last8 = n_rows // SUBLANES - 1
    per = tm // SUBLANES
    for spec in ins:
        kind, arr = spec[0], spec[1]
        if kind == "full":
            in_specs.append(pl.BlockSpec(arr.shape, lambda i, nd=arr.ndim: (0,) * nd))
        else:
            w, cb = spec[2], spec[3]
            if kind == "row":
                in_specs.append(pl.BlockSpec((tm, w), lambda i, cb=cb: (i, cb)))
            elif kind == "prev":
                in_specs.append(pl.BlockSpec((SUBLANES, w), lambda i, cb=cb: (jnp.maximum(i * per - 1, 0), cb)))
            else:
                in_specs.append(pl.BlockSpec((SUBLANES, w), lambda i, cb=cb: (jnp.minimum((i + 1) * per, last8), cb)))
        arrays.append(arr)
    out_shape = [jax.ShapeDtypeStruct((n_rows, w), dt) for (w, dt) in outs]
    out_shape += [jax.ShapeDtypeStruct(s, F32) for s in accs]
    out_specs = [pl.BlockSpec((tm, w), lambda i: (i, 0)) for (w, _) in outs]
    out_specs += [pl.BlockSpec(s, lambda i: (0, 0)) for s in accs]
    n_io = len(ins) + len(outs)

    def kern(*refs):
        i = pl.program_id(0)
        if accs:
            @pl.when(i == 0)
            def _():
                for r in refs[n_io:]:
                    r[...] = jnp.zeros_like(r)
        body(i, *refs)

    res = _pcall(
        kern, name=name, grid=(n_rows // tm,), in_specs=in_specs, out_specs=out_specs, out_shape=out_shape,
        compiler_params=_params(("arbitrary",)),
    )(*arrays)
    return res


def _shift_down(x, halo, j):
    r = pltpu.roll(x, j, 0)
    hr = pltpu.roll(halo, j, 0)
    rows = lax.broadcasted_iota(jnp.int32, (SUBLANES, x.shape[1]), 0)
    top = jnp.where(rows < j, hr, r[0:SUBLANES])
    return jnp.concatenate([top, r[SUBLANES:]], axis=0)


def _shift_up(x, halo, j):
    tm = x.shape[0]
    r = pltpu.roll(x, tm - j, 0)
    hr = pltpu.roll(halo, SUBLANES - j, 0)
    rows = lax.broadcasted_iota(jnp.int32, (SUBLANES, x.shape[1]), 0)
    bot = jnp.where(rows >= SUBLANES - j, hr, r[tm - SUBLANES:])
    return jnp.concatenate([r[:tm - SUBLANES], bot], axis=0)


def _conv_taps(x, halo, kw):
    return [x if kw - 1 - k == 0 else _shift_down(x, halo, kw - 1 - k) for k in range(kw)]


def _conv(taps, w):
    y = taps[0] * w[0:1]
    for k in range(1, len(taps)):
        y = y + taps[k] * w[k:k + 1]
    return y


def _rms(x, width):
    r = lax.rsqrt(jnp.sum(x * x, axis=-1, keepdims=True) * (1.0 / width) + EPS)
    return x * r, r


def _rms_bwd(xh, r, dxh, width):
    return r * (dxh - xh * (jnp.sum(dxh * xh, axis=-1, keepdims=True) * (1.0 / width)))


def _seq_flags(i, seq, tm):
    nps = seq // tm
    pos = i % nps
    return jnp.where(pos == 0, 0.0, 1.0), jnp.where(pos == nps - 1, 0.0, 1.0)


def _norm_cast(name, x, w, tm):
    t, d = x.shape

    def body(i, x_ref, w_ref, h_ref):
        xh, _ = _rms(x_ref[...], d)
        h_ref[...] = (xh * w_ref[...]).astype(BF16)

    return _rowwise(name, body, t, tm, [("row", x, d, 0), ("full", w)], [(d, BF16)])[0]


def _gdn_prep(proj, cw, gp, seq, tm):
    t = proj.shape[0]
    d = D_MODEL

    def body(i, q_ref, qh_ref, k_ref, kh_ref, v_ref, vh_ref, sm_ref, cw_ref, gp_ref, qn_ref, kn_ref, vv_ref, gs_ref, ypre_ref):
        keep, _ = _seq_flags(i, seq, tm)
        for x_ref, h_ref, o_ref, off, scale in ((q_ref, qh_ref, qn_ref, 0, GDN_DK ** -0.5),
                                               (k_ref, kh_ref, kn_ref, d, 1.0), (v_ref, vh_ref, vv_ref, 2 * d, None)):
            y = _conv(_conv_taps(x_ref[...], h_ref[...] * keep, 4), cw_ref[:, off:off + d])
            ypre_ref[:, off:off + d] = y
            a = y * _sigmoid(y)
            if scale is None:
                o_ref[...] = a
            else:
                for hh in range(GDN_HEADS):
                    s = a[:, hh * GDN_DK:(hh + 1) * GDN_DK]
                    n = lax.rsqrt(_rowsum(s * s) + EPS)
                    o_ref[:, hh * GDN_DK:(hh + 1) * GDN_DK] = s * (n * scale)
        sm = sm_ref[...]
        lane = lax.broadcasted_iota(jnp.int32, sm.shape, 1)
        beta = _sigmoid(sm)
        g = jnp.where((lane >= 8) & (lane < 16), -jnp.exp(gp_ref[0:1, :]) * _softplus(sm + gp_ref[1:2, :]), 0.0)
        gs_ref[...] = jnp.where(lane < 8, beta, _mmx(_block_tri(tm, False), g))

    ins = []
    for cb in range(3):
        ins += [("row", proj, d, cb), ("prev", proj, d, cb)]
    ins += [("row", proj, LANES, SMALL_CB), ("full", cw), ("full", gp)]
    return _rowwise("gdn_prep", body, t, tm, ins, [(d, F32), (d, F32), (d, F32), (LANES, F32), (3 * d, F32)])


def _block_tri(tm, upper):
    ri = lax.broadcasted_iota(jnp.int32, (tm, tm), 0)
    ci = lax.broadcasted_iota(jnp.int32, (tm, tm), 1)
    tri = (ri <= ci) if upper else (ri >= ci)
    return (tri & ((ri // CHUNK) == (ci // CHUNK))).astype(F32)


def _chunk_consts():
    row = lax.broadcasted_iota(jnp.int32, (CHUNK, CHUNK), 0)
    col = lax.broadcasted_iota(jnp.int32, (CHUNK, CHUNK), 1)
    return dict(
        tril=row >= col, strict=row > col, eye=(row == col).astype(F32),
        lane=lax.broadcasted_iota(jnp.int32, (CHUNK, LANES), 1),
        row1=lax.broadcasted_iota(jnp.int32, (CHUNK, 1), 0),
        ones=jnp.ones((CHUNK, LANES), F32))


def _hmap(fn, *lists):
    return [fn(*a) for a in zip(*lists)]


def _tri_inv(nmats, eye):
    x = [eye - n for n in nmats]
    p = _hmap(_mm3, nmats, nmats)
    for lvl in range(5):
        x = _hmap(lambda xi, pi: xi + _mm3(xi, pi), x, p)
        if lvl < 4:
            p = _hmap(_mm3, p, p)
    return x


def _gdn_gates(gs, gc_row, h, c):
    beta = _rowsum(jnp.where(c["lane"] == h, gs, 0.0))
    gc = _rowsum(jnp.where(c["lane"] == h + 8, gs, 0.0))
    dc = jnp.exp(jnp.where(c["tril"], gc - gc_row, -1e30))
    gl = gc[CHUNK - 1:CHUNK, :]
    return beta, dc, jnp.exp(gc), jnp.exp(gl), jnp.exp(gl - gc)


GDN_HB = GDN_HEADS


def _gdn_specs(seq, sb, hb, backward):
    assert hb == GDN_HEADS
    nsb = seq // sb
    ncb = sb // CHUNK
    order = (lambda j: nsb - 1 - j) if backward else (lambda j: j)
    specs = dict(
        wide=lambda: pl.BlockSpec((1, sb, hb * GDN_DK), lambda b, h, j: (b, order(j), h)),
        gs=lambda: pl.BlockSpec((1, sb, LANES), lambda b, h, j: (b, order(j), 0)),
        gr=pl.BlockSpec((1, ncb, GDN_HEADS, CHUNK), lambda b, h, j: (b, order(j), 0, 0)),
        st=pl.BlockSpec((1, hb, ncb * GDN_DK, GDN_DK), lambda b, h, j: (b, h, order(j), 0)),
        ti=pl.BlockSpec((1, hb, sb, CHUNK), lambda b, h, j: (b, h, order(j), 0)))
    return nsb, ncb, specs


def _riding_exchange(arrays, scatter, n_in, n_out, grid):
    n = len(arrays)
    if n == 0:
        return [], [], [], lambda body: body
    any_spec = [pl.BlockSpec(memory_space=pl.ANY)] * n

    def wrap(body):
        def wrapped(*refs):
            ins = refs[n_in:n_in + n]
            outs = refs[n_in + n + n_out:n_in + 2 * n + n_out]
            sems = refs[len(refs) - 3:]
            pid = [pl.program_id(a) for a in range(len(grid))]
            first = functools.reduce(lambda a, b: a & b, [p == 0 for p in pid])
            last = functools.reduce(lambda a, b: a & b, [p == g - 1 for p, g in zip(pid, grid)])

            @pl.when(first)
            def _():
                _exchange_phase(ins, outs, sems, scatter, start=True)

            body(*refs[:n_in], *refs[n_in + n:n_in + n + n_out], *refs[n_in + 2 * n + n_out:len(refs) - 3])

            @pl.when(last)
            def _():
                _exchange_phase(ins, outs, sems, scatter, start=False)

        return wrapped

    return any_spec, _exchange_out_shapes(arrays, scatter), _exchange_sems(n), wrap


def _gdn_chunk_fwd(qn, kn, vv, gs, gr, bsz, seq, sb, riders):
    hb = GDN_HB
    nsb, ncb, sp = _gdn_specs(seq, sb, hb, False)
    grid = (bsz, GDN_HEADS // hb, nsb)
    any_spec, rider_shapes, rider_sems, wrap = _riding_exchange(riders, False, 5, 3, grid)

    def body(q_ref, k_ref, v_ref, gs_ref, gr_ref, o_ref, st_ref, ti_ref, s_scr):
        hg = pl.program_id(1)

        @pl.when(pl.program_id(2) == 0)
        def _():
            s_scr[...] = jnp.zeros_like(s_scr)

        c = _chunk_consts()

        def chunk(n, carry):
            r = pl.ds(pl.multiple_of(n * CHUNK, CHUNK), CHUNK)
            rs = pl.ds(pl.multiple_of(n * GDN_DK, GDN_DK), GDN_DK)
            gsv = gs_ref[0, r, :]
            heads = list(range(hb))
            sls = [slice(ih * GDN_DK, (ih + 1) * GDN_DK) for ih in heads]
            q = [q_ref[0, r, sl] for sl in sls]
            k = [k_ref[0, r, sl] for sl in sls]
            v = [v_ref[0, r, sl] for sl in sls]
            beta, dc, eg, egl, ekd = zip(*[
                _gdn_gates(gsv, gr_ref[0, n, pl.ds(ih, 1), :], ih, c) for ih in heads])
            kb = _hmap(lambda a, b: a * b, k, beta)
            amat = _hmap(lambda a, b, d_: jnp.where(c["strict"], _mm(a, b, NT) * d_, 0.0), kb, k, dc)
            tinv = _tri_inv(amat, c["eye"])
            u = _hmap(lambda t_, a, b: _mm3(t_, a * b), tinv, v, beta)
            w = _hmap(lambda t_, a, b: _mm3(t_, a * b), tinv, kb, eg)
            qk = _hmap(lambda a, b, d_: _mm(a, b, NT) * d_, q, k, dc)
            s = [s_scr[ih] for ih in heads]
            v_new = _hmap(lambda a, b, s_: a - _mm(b, s_), u, w, s)
            o = _hmap(lambda a, e, s_, qk_, vn: _mm(a * e, s_) + _mm(qk_, vn), q, eg, s, qk, v_new)
            s_new = _hmap(lambda s_, e, a, f, vn: s_ * e + _mm(a * f, vn, TN), s, egl, k, ekd, v_new)
            for ih in heads:
                o_ref[0, r, sls[ih]] = o[ih]
                st_ref[0, ih, rs, :] = s[ih]
                ti_ref[0, ih, r, :] = tinv[ih]
                s_scr[ih] = s_new[ih]
            return carry

        lax.fori_loop(0, ncb, chunk, 0)

    t3 = (bsz, seq, D_MODEL)
    res = _pcall(
        wrap(body), name="gdn_chunk_fwd", grid=grid,
        in_specs=[sp["wide"](), sp["wide"](), sp["wide"](), sp["gs"](), sp["gr"]] + any_spec,
        out_specs=[sp["wide"](), sp["st"], sp["ti"]] + any_spec,
        out_shape=[jax.ShapeDtypeStruct(t3, F32),
                   jax.ShapeDtypeStruct((bsz, GDN_HEADS, (seq // CHUNK) * GDN_DK, GDN_DK), F32),
                   jax.ShapeDtypeStruct((bsz, GDN_HEADS, seq, CHUNK), F32)] + rider_shapes,
        scratch_shapes=[pltpu.VMEM((hb, GDN_DK, GDN_DK), F32)] + rider_sems,
        compiler_params=_params(("arbitrary", "arbitrary", "arbitrary")),
    )(qn, kn, vv, gs, gr, *riders)
    return res[:3], res[3:]


def _ssd_prep(proj, cw, cb, sp, seq, tm):
    t = proj.shape[0]
    d = D_MODEL
    ssd_w = SSD_HEADS * SSD_P

    def body(i, x_ref, xh_ref, bc_ref, bch_ref, sm_ref, cw_ref, cb_ref, sp_ref, xs_ref, bco_ref, dtx_ref, acsx_ref, acs_ref, ypre_ref):
        keep, _ = _seq_flags(i, seq, tm)
        y = _conv(_conv_taps(x_ref[...], xh_ref[...] * keep, 4), cw_ref[:, 0:d]) + cb_ref[:, 0:d]
        ypre_ref[:, 0:d] = y
        xs_ref[...] = y * _sigmoid(y)
        y = _conv(_conv_taps(bc_ref[...], bch_ref[...] * keep, 4), cw_ref[:, d:d + 512]) + cb_ref[:, d:d + 512]
        ypre_ref[:, d:d + 512] = y
        bco_ref[...] = y * _sigmoid(y)
        sm = sm_ref[...]
        lane = lax.broadcasted_iota(jnp.int32, sm.shape, 1)
        valid = (lane >= 16) & (lane < 32)
        dt = jnp.where(valid, _softplus(sm + sp_ref[1:2, :]), 0.0)
        adt = dt * (-jnp.exp(sp_ref[0:1, :]))
        acs = _mmx(_block_tri(tm, False), adt)
        l64 = lax.broadcasted_iota(jnp.int32, (LANES, ssd_w), 0)
        d64 = lax.broadcasted_iota(jnp.int32, (LANES, ssd_w), 1)
        e64 = (l64 - 16 == d64 // SSD_P).astype(F32)
        dtx_ref[...] = _mmsel(dt, e64, terms=3)
        acsx_ref[...] = _mmsel(acs, e64, terms=3)
        acs_ref[...] = acs

    ins = [("row", proj, d, 5), ("prev", proj, d, 5), ("row", proj, 512, 12), ("prev", proj, 512, 12),
           ("row", proj, LANES, SMALL_CB), ("full", cw), ("full", cb), ("full", sp)]
    return _rowwise("ssd_prep", body, t, tm, ins,
                    [(d, F32), (512, F32), (ssd_w, F32), (ssd_w, F32), (LANES, F32), (d + 512, F32)])


SSD_GW = SSD_HPG * SSD_P


def _ssd_head(acs, ar_ref, n, head, cbm, c):
    col = _rowsum(jnp.where(c["lane"] == head + 16, acs, 0.0))
    lm = jnp.exp(jnp.where(c["tril"], col - ar_ref[0, head, pl.ds(n, 1), :], -1e30))
    return lm, cbm * lm


def _ssd_specs(seq, sb):
    nsb = seq // sb
    ncb = sb // CHUNK
    def specs(order):
        return dict(
            wide=lambda: pl.BlockSpec((1, sb, SSD_HEADS * SSD_P), lambda b, j: (b, order(j), 0)),
            bc=lambda: pl.BlockSpec((1, sb, 2 * SSD_GROUPS * SSD_N), lambda b, j: (b, order(j), 0)),
            half=lambda: pl.BlockSpec((1, sb, SSD_GROUPS * SSD_N), lambda b, j: (b, order(j), 0)),
            small=lambda: pl.BlockSpec((1, sb, LANES), lambda b, j: (b, order(j), 0)),
            ar=pl.BlockSpec((1, SSD_HEADS, ncb, CHUNK), lambda b, j: (b, 0, order(j), 0)),
            st=pl.BlockSpec((1, ncb * SSD_N, SSD_HEADS * SSD_P), lambda b, j: (b, order(j), 0)))
    return nsb, ncb, specs(lambda j: j), specs(lambda j: nsb - 1 - j)


def _ssd_chunk_fwd(xs, bc, dtx, acsx, acs, ar, bsz, seq, sb):
    nsb, ncb, sp, _ = _ssd_specs(seq, sb)

    def body(x_ref, dtx_ref, ax_ref, bc_ref, acs_ref, ar_ref, y_ref, sts_ref, st_scr):
        @pl.when(pl.program_id(1) == 0)
        def _():
            st_scr[...] = jnp.zeros_like(st_scr)

        c = _chunk_consts()
        lane5 = lax.broadcasted_iota(jnp.int32, (CHUNK, SSD_GW), 1) // SSD_P

        def chunk(n, carry):
            r = pl.ds(pl.multiple_of(n * CHUNK, CHUNK), CHUNK)
            rs = pl.ds(pl.multiple_of(n * SSD_N, SSD_N), SSD_N)
            acsv = acs_ref[0, r, :]
            for g in range(SSD_GROUPS):
                gl = slice(g * SSD_GW, (g + 1) * SSD_GW)
                x, dt, ax = x_ref[0, r, gl], dtx_ref[0, r, gl], ax_ref[0, r, gl]
                bm = bc_ref[0, r, g * SSD_N:(g + 1) * SSD_N]
                cm = bc_ref[0, r, (SSD_GROUPS + g) * SSD_N:(SSD_GROUPS + g + 1) * SSD_N]
                xdt = x * dt
                cbm = _mm(cm, bm, NT)
                al = ax[CHUNK - 1:CHUNK, :]
                st = st_scr[:, gl]
                y = _mm(cm, st) * jnp.exp(ax)
                for hh in range(SSD_HPG):
                    _, gm = _ssd_head(acsv, ar_ref, n, g * SSD_HPG + hh, cbm, c)
                    y = y + _mm(gm, jnp.where(lane5 == hh, xdt, 0.0))
                y_ref[0, r, gl] = y
                sts_ref[0, rs, gl] = st
                st_scr[:, gl] = st * jnp.exp(al) + _mm(bm, xdt * jnp.exp(al - ax), TN)
            return carry

        lax.fori_loop(0, ncb, chunk, 0)

    return _pcall(
        body, name="ssd_chunk_fwd", grid=(bsz, nsb),
        in_specs=[sp["wide"](), sp["wide"](), sp["wide"](), sp["bc"](), sp["small"](), sp["ar"]],
        out_specs=[sp["wide"](), sp["st"]],
        out_shape=[jax.ShapeDtypeStruct((bsz, seq, SSD_HEADS * SSD_P), F32),
                   jax.ShapeDtypeStruct((bsz, (seq // CHUNK) * SSD_N, SSD_HEADS * SSD_P), F32)],
        scratch_shapes=[pltpu.VMEM((SSD_N, SSD_HEADS * SSD_P), F32)],
        compiler_params=_params(("parallel", "arbitrary")),
    )(xs, dtx, acsx, bc, acs, ar)


def _gate_norm(o_gdn, y_ssd, xs, proj, gnw, snw, dvec, tm):
    t = o_gdn.shape[0]
    d = D_MODEL

    def body(i, o_ref, za_ref, y_ref, xs_ref, zs_ref, gnw_ref, snw_ref, dv_ref, out_ref):
        for hh in range(GDN_HEADS):
            sl = slice(hh * GDN_DK, (hh + 1) * GDN_DK)
            oh, _ = _rms(o_ref[:, sl], GDN_DK)
            z = za_ref[:, sl]
            out_ref[:, sl] = (oh * gnw_ref[...] * (z * _sigmoid(z))).astype(BF16)
        zs = zs_ref[...]
        yg = (y_ref[...] + dv_ref[...] * xs_ref[...]) * (zs * _sigmoid(zs))
        for g in range(SSD_GROUPS):
            sl = slice(g * 512, (g + 1) * 512)
            yh, _ = _rms(yg[:, sl], 512)
            out_ref[:, d + g * 512:d + (g + 1) * 512] = (yh * snw_ref[:, sl]).astype(BF16)

    ins = [("row", o_gdn, d, 0), ("row", proj, d, 3), ("row", y_ssd, d, 0), ("row", xs, d, 0), ("row", proj, d, 4),
           ("full", gnw), ("full", snw), ("full", dvec)]
    return _rowwise("gate_norm", body, t, tm, ins, [(2 * d, BF16)])[0]


def _out_mid(mixin, w_out, x, pmw, pfw):
    d = D_MODEL

    def epilogue(mix, x_ref, pmw_ref, pfw_ref, mix_ref, x1_ref, h2_ref):
        mix_ref[...] = mix
        mh, _ = _rms(mix, d)
        x1 = x_ref[...] + mh * pmw_ref[...]
        x1_ref[...] = x1
        xh, _ = _rms(x1, d)
        h2_ref[...] = (xh * pfw_ref[...]).astype(BF16)

    return _matmul_rows("mm_out_mid", mixin, w_out, "nn", epilogue, [x], [pmw, pfw], [(d, F32), (d, F32), (d, BF16)],
                        tk=2 * d)


def _ffn_act(u_pre, cw, cb, seq, tm):
    t = u_pre.shape[0]

    def body(i, ug_ref, ugh_ref, uu_ref, uuh_ref, cw_ref, cb_ref, act_ref, u_ref):
        keep, _ = _seq_flags(i, seq, tm)
        gate = _conv(_conv_taps(ug_ref[...], ugh_ref[...] * keep, 3), cw_ref[:, 0:D_FF]) + cb_ref[:, 0:D_FF]
        up = _conv(_conv_taps(uu_ref[...], uuh_ref[...] * keep, 3), cw_ref[:, D_FF:2 * D_FF]) + cb_ref[:, D_FF:2 * D_FF]
        u_ref[:, 0:D_FF] = gate
        u_ref[:, D_FF:2 * D_FF] = up
        act_ref[...] = (gate * _sigmoid(gate) * up).astype(BF16)

    ins = [("row", u_pre, D_FF, 0), ("prev", u_pre, D_FF, 0), ("row", u_pre, D_FF, 1), ("prev", u_pre, D_FF, 1),
           ("full", cw), ("full", cb)]
    return _rowwise("ffn_act", body, t, tm, ins, [(D_FF, BF16), (2 * D_FF, F32)])


def _down_final(act, w_down, x1, tgt, w):
    d = D_MODEL

    def epilogue(f, x1_ref, t_ref, w_ref, dy_ref, df_ref, loss_ref, dw_ref):
        fh, r = _rms(f, d)
        e = x1_ref[...] + fh * w_ref[...] - t_ref[...]
        loss_ref[...] += _colsum(e * e) * (0.5 / d)
        dy = e * (1.0 / d)
        dy_ref[...] = dy
        dw_ref[...] += _colsum(dy * fh)
        df_ref[...] = _rms_bwd(fh, r, dy * w_ref[...], d).astype(BF16)

    return _matmul_rows("mm_down_final", act, w_down, "nn", epilogue, [x1, tgt], [w], [(d, F32), (d, BF16)],
                        accs=[(1, d), (1, d)], tk=D_FF)


def _ffn_bwd(u, u_pre, dact, cw, seq, tm):
    t = u.shape[0]

    def body(i, g_ref, gn_ref, up_ref, upn_ref, xg_ref, xu_ref, da_ref, dan_ref, cw_ref, dpre_ref, dcw_ref, dcb_ref):
        _, keep_next = _seq_flags(i, seq, tm)
        ext = lambda a_ref, n_ref: jnp.concatenate([a_ref[...], n_ref[...]], axis=0)
        rows = tm + SUBLANES
        gate, up = ext(g_ref, gn_ref), ext(up_ref, upn_ref)
        sg = _sigmoid(gate)
        da = jnp.concatenate([da_ref[...], dan_ref[...] * keep_next], axis=0)
        for off, grad, x_ref in ((0, da * up * _dsilu(gate, sg), xg_ref), (D_FF, da * gate * sg, xu_ref)):
            x = x_ref[...]
            own = grad[0:tm]
            acc = own * cw_ref[2:3, off:off + D_FF]
            dcb_ref[:, off:off + D_FF] += _colsum(own)
            dcw_ref[2:3, off:off + D_FF] += _colsum(own * x)
            for j in (1, 2):
                ahead = pltpu.roll(grad, rows - j, 0)[0:tm]
                acc = acc + ahead * cw_ref[2 - j:3 - j, off:off + D_FF]
                dcw_ref[2 - j:3 - j, off:off + D_FF] += _colsum(ahead * x)
            dpre_ref[:, off:off + D_FF] = acc.astype(BF16)

    ins = []
    for cb_ in range(2):
        ins += [("row", u, D_FF, cb_), ("next", u, D_FF, cb_)]
    ins += [("row", u_pre, D_FF, 0), ("row", u_pre, D_FF, 1), ("row", dact, D_FF, 0), ("next", dact, D_FF, 0), ("full", cw)]
    return _rowwise("ffn_bwd", body, t, tm, ins, [(2 * D_FF, BF16)], accs=[(SUBLANES, 2 * D_FF), (1, 2 * D_FF)])


def _assemble_dproj(dpre_qkv, dza, dzs, dpre_ssd, dsm, proj, gcw, scw, seq, tm):
    t = dza.shape[0]
    d = D_MODEL

    def body(i, dq_ref, dqn_ref, dk_ref, dkn_ref, dv_ref, dvn_ref, dza_ref, dzs_ref, ds_ref, dsn_ref, dsm_ref,
             xq_ref, xk_ref, xv_ref, xx_ref, xbc_ref, gcw_ref, scw_ref, o_ref, dgcw_ref, dscw_ref):
        _, keep = _seq_flags(i, seq, tm)
        pieces = [(g_ref, n_ref, 0, gcw_ref, dgcw_ref, x_ref, 0, c0) for g_ref, n_ref, x_ref, c0 in (
            (dq_ref, dqn_ref, xq_ref, 0), (dk_ref, dkn_ref, xk_ref, d), (dv_ref, dvn_ref, xv_ref, 2 * d))]
        pieces += [(ds_ref, dsn_ref, c0, scw_ref, dscw_ref, x_ref, 5 * d, c0) for x_ref, c0 in ((xx_ref, 0), (xbc_ref, d))]
        for d_ref, n_ref, dcol, cw_ref, dcw_ref, x_ref, base, c0 in pieces:
            w = x_ref.shape[1]
            x = x_ref[...]
            g = d_ref[:, dcol:dcol + w]
            halo = n_ref[:, dcol:dcol + w] * keep
            acc = g * cw_ref[3:4, c0:c0 + w]
            dcw_ref[3:4, c0:c0 + w] += _colsum(g * x)
            for j in range(1, 4):
                ahead = _shift_up(g, halo, j)
                acc = acc + ahead * cw_ref[3 - j:4 - j, c0:c0 + w]
                dcw_ref[3 - j:4 - j, c0:c0 + w] += _colsum(ahead * x)
            o_ref[:, base + c0:base + c0 + w] = acc.astype(BF16)
        o_ref[:, 3 * d:4 * d] = dza_ref[...]
        o_ref[:, 4 * d:5 * d] = dzs_ref[...]
        o_ref[:, 6 * d + 512:6 * d + 512 + LANES] = dsm_ref[...]
        o_ref[:, 6 * d + 512 + LANES:PROJ_W] = jnp.zeros((tm, PROJ_W - (6 * d + 512 + LANES)), BF16)

    ins = []
    for g in dpre_qkv:
        ins += [("row", g, d, 0), ("next", g, d, 0)]
    ins += [("row", dza, d, 0), ("row", dzs, d, 0),
           ("row", dpre_ssd, d + 512, 0), ("next", dpre_ssd, d + 512, 0), ("row", dsm, LANES, 0),
           ("row", proj, d, 0), ("row", proj, d, 1), ("row", proj, d, 2), ("row", proj, d, 5), ("row", proj, 512, 12),
           ("full", gcw), ("full", scw)]
    return _rowwise("assemble_dproj", body, t, tm, ins, [(PROJ_W, BF16)], accs=[(SUBLANES, 3 * d), (SUBLANES, d + 512)])


def _dh2_mid_bwd(du_pre, w_up, x1, mix, dy, pmw, pfw):
    d = D_MODEL

    def epilogue(dh2, x1_ref, mix_ref, dy_ref, pmw_ref, pfw_ref, dx1_ref, dmix_ref, dpm_ref, dpf_ref):
        xh, r2 = _rms(x1_ref[...], d)
        dpf_ref[...] += _colsum(dh2 * xh)
        dx1 = dy_ref[...] + _rms_bwd(xh, r2, dh2 * pfw_ref[...], d)
        dx1_ref[...] = dx1
        mh, r = _rms(mix_ref[...], d)
        dpm_ref[...] += _colsum(dx1 * mh)
        dmix_ref[...] = _rms_bwd(mh, r, dx1 * pmw_ref[...], d).astype(BF16)

    return _matmul_rows("mm_dh2_mid_bwd", du_pre, w_up, "nt", epilogue, [x1, mix, dy], [pmw, pfw],
                        [(d, F32), (d, BF16)], accs=[(1, d), (1, d)], tk=D_FF)


def _dmixin_gate_norm_bwd(dmix, w_out, o_gdn, y_ssd, xs, proj, gnw, snw, dvec):
    d = D_MODEL

    def epilogue(dmixin, o_ref, za_ref, y_ref, xs_ref, zs_ref, gnw_ref, snw_ref, dv_ref,
                 do_ref, dza_ref, dy_ref, dxs_ref, dzs_ref, dgnw_ref, dsnw_ref, dd_ref):
        for hh in range(GDN_HEADS):
            sl = slice(hh * GDN_DK, (hh + 1) * GDN_DK)
            oh, r = _rms(o_ref[:, sl], GDN_DK)
            z = za_ref[:, sl]
            sz = _sigmoid(z)
            dm = dmixin[:, sl]
            don = dm * (z * sz)
            dza_ref[:, sl] = (dm * oh * gnw_ref[...] * _dsilu(z, sz)).astype(BF16)
            dgnw_ref[...] += _colsum(don * oh)
            do_ref[:, sl] = _rms_bwd(oh, r, don * gnw_ref[...], GDN_DK)
        zs = zs_ref[...]
        sz = _sigmoid(zs)
        sil = zs * sz
        x = xs_ref[...]
        y0 = y_ref[...] + dv_ref[...] * x
        yg = y0 * sil
        dms = dmixin[:, d:2 * d]
        for g in range(SSD_GROUPS):
            sl = slice(g * 512, (g + 1) * 512)
            yh, r = _rms(yg[:, sl], 512)
            dsnw_ref[:, sl] += _colsum(dms[:, sl] * yh)
            dyg = _rms_bwd(yh, r, dms[:, sl] * snw_ref[:, sl], 512)
            dy0 = dyg * sil[:, sl]
            dzs_ref[:, sl] = (dyg * y0[:, sl] * _dsilu(zs[:, sl], sz[:, sl])).astype(BF16)
            dy_ref[:, sl] = dy0
            dxs_ref[:, sl] = dy0 * dv_ref[:, sl]
            dd_ref[:, sl] += _colsum(dy0 * x[:, sl])

    row_ins = [o_gdn, (proj, d, 3), y_ssd, xs, (proj, d, 4)]
    return _matmul_rows("mm_dmixin_gate_norm_bwd", dmix, w_out, "nt", epilogue, row_ins, [gnw, snw, dvec],
                        [(d, F32), (d, BF16), (d, F32), (d, F32), (d, BF16)], accs=[(1, GDN_DK), (1, d), (1, d)], tm=256)


def _ssd_chunk_bwd(xs, bc, dtx, acsx, acs, ar, dy, sts, bsz, seq, sb):
    nsb, ncb, _, sp = _ssd_specs(seq, sb)

    def body(x_ref, dtx_ref, ax_ref, bc_ref, acs_ref, ar_ref, dy_ref, sts_ref, dx_ref, dbc_ref, ddt_ref, dacs_ref, dst_scr):
        @pl.when(pl.program_id(1) == 0)
        def _():
            dst_scr[...] = jnp.zeros_like(dst_scr)

        c = _chunk_consts()
        lane5 = lax.broadcasted_iota(jnp.int32, (CHUNK, SSD_GW), 1) // SSD_P
        row5 = lax.broadcasted_iota(jnp.int32, (CHUNK, SSD_GW), 0)
        sel_in = lax.broadcasted_iota(jnp.int32, (SSD_GW, LANES), 0) // SSD_P
        sel_out = lax.broadcasted_iota(jnp.int32, (SSD_GW, LANES), 1)

        def chunk(nn, carry):
            n = ncb - 1 - nn
            r = pl.ds(pl.multiple_of(n * CHUNK, CHUNK), CHUNK)
            rs = pl.ds(pl.multiple_of(n * SSD_N, SSD_N), SSD_N)
            acsv = acs_ref[0, r, :]
            ddt = jnp.zeros((CHUNK, LANES), F32)
            dacs = jnp.zeros((CHUNK, LANES), F32)
            for g in range(SSD_GROUPS):
                gl = slice(g * SSD_GW, (g + 1) * SSD_GW)
                x, dt, ax, dyv = x_ref[0, r, gl], dtx_ref[0, r, gl], ax_ref[0, r, gl], dy_ref[0, r, gl]
                bm = bc_ref[0, r, g * SSD_N:(g + 1) * SSD_N]
                cm = bc_ref[0, r, (SSD_GROUPS + g) * SSD_N:(SSD_GROUPS + g + 1) * SSD_N]
                st = sts_ref[0, rs, gl]
                dst = dst_scr[:, gl]
                rsel = (sel_in + (16 + g * SSD_HPG) == sel_out).astype(F32)
                xdt = x * dt
                cbm = _mm(cm, bm, NT)
                al = ax[CHUNK - 1:CHUNK, :]
                ex, el = jnp.exp(ax), jnp.exp(al)
                dec = jnp.exp(al - ax)
                xd = xdt * dec
                dye = dyv * ex
                dxd = _mm(bm, dst)
                dxdt = dec * dxd
                dcm = _mm(dye, st, NT)
                dbm = _mm(xd, dst, NT)
                z = dye * _mm(cm, st) - dxd * xd
                zl = _colsum(dst * st) * el + _colsum(dxd * xd)
                z = z + jnp.where(row5 == CHUNK - 1, zl, 0.0)
                dcb = jnp.zeros((CHUNK, CHUNK), F32)
                for hh in range(SSD_HPG):
                    head = g * SSD_HPG + hh
                    lm, gm = _ssd_head(acsv, ar_ref, n, head, cbm, c)
                    dym = jnp.where(lane5 == hh, dyv, 0.0)
                    dxdt = dxdt + _mm(gm, dym, TN)
                    dg = _mm(dym, xdt, NT)
                    dcb = dcb + dg * lm
                    pm = dg * gm
                    dacs = dacs + jnp.where(c["lane"] == head + 16, _rowsum(pm) - _mmsel(pm, c["ones"], TN), 0.0)
                dbc_ref[0, r, (SSD_GROUPS + g) * SSD_N:(SSD_GROUPS + g + 1) * SSD_N] = dcm + _mm(dcb, bm)
                dbc_ref[0, r, g * SSD_N:(g + 1) * SSD_N] = dbm + _mm(dcb, cm, TN)
                dacs = dacs + _mmsel(z, rsel)
                ddt = ddt + _mmsel(dxdt * x, rsel)
                dx_ref[0, r, gl] = dxdt * dt
                dst_scr[:, gl] = dst * el + _mm(cm, dye, TN)
            ddt_ref[0, r, :] = ddt
            dacs_ref[0, r, :] = dacs
            return carry

        lax.fori_loop(0, ncb, chunk, 0)

    return _pcall(
        body, name="ssd_chunk_bwd", grid=(bsz, nsb),
        in_specs=[sp["wide"](), sp["wide"](), sp["wide"](), sp["bc"](), sp["small"](), sp["ar"], sp["wide"](), sp["st"]],
        out_specs=[sp["wide"](), sp["bc"](), sp["small"](), sp["small"]()],
        out_shape=[jax.ShapeDtypeStruct((bsz, seq, SSD_HEADS * SSD_P), F32),
                   jax.ShapeDtypeStruct((bsz, seq, 2 * SSD_GROUPS * SSD_N), F32),
                   jax.ShapeDtypeStruct((bsz, seq, LANES), F32), jax.ShapeDtypeStruct((bsz, seq, LANES), F32)],
        scratch_shapes=[pltpu.VMEM((SSD_N, SSD_HEADS * SSD_P), F32)],
        compiler_params=_params(("parallel", "arbitrary")),
    )(xs, dtx, acsx, bc, acs, ar, dy, sts)


def _ssd_prep_bwd(ypre, proj, dxs_c, dxs_d, dbc, ddt, dacs, dsm_gdn, sp, tm):
    t = proj.shape[0]
    d = D_MODEL

    def body(i, y_ref, sm_ref, dxc_ref, dxd_ref, dbc_ref, ddt_ref, dacs_ref, dsg_ref, sp_ref,
             dpre_ref, dsm_ref, dcb_ref, dsp_ref):
        for off, w, grad in ((0, d, dxc_ref[...] + dxd_ref[...]), (d, 512, dbc_ref[...])):
            y = y_ref[:, off:off + w]
            dpre = grad * _dsilu(y, _sigmoid(y))
            dpre_ref[:, off:off + w] = dpre
            dcb_ref[:, off:off + w] += _colsum(dpre)
        sm = sm_ref[...]
        lane = lax.broadcasted_iota(jnp.int32, sm.shape, 1)
        valid = (lane >= 16) & (lane < 32)
        xb = sm + sp_ref[1:2, :]
        dt = jnp.where(valid, _softplus(xb), 0.0)
        a_neg = -jnp.exp(sp_ref[0:1, :])
        dadt_s = _mmx(_block_tri(tm, True), dacs_ref[...])
        dxb = jnp.where(valid, (ddt_ref[...] + dadt_s * a_neg) * _sigmoid(xb), 0.0)
        dsm_ref[...] = (dsg_ref[...] + dxb).astype(BF16)
        dsp_ref[1:2, :] += _colsum(dxb)
        dsp_ref[0:1, :] += jnp.where(valid[0:1, :], _colsum(dadt_s * dt) * a_neg, 0.0)

    ins = [("row", ypre, d + 512, 0), ("row", proj, LANES, SMALL_CB), ("row", dxs_c, d, 0), ("row", dxs_d, d, 0),
           ("row", dbc, 512, 0), ("row", ddt, LANES, 0), ("row", dacs, LANES, 0), ("row", dsm_gdn, LANES, 0), ("full", sp)]
    return _rowwise("ssd_prep_bwd", body, t, tm, ins, [(d + 512, F32), (LANES, BF16)],
                    accs=[(1, d + 512), (SUBLANES, LANES)])


def _through_norm_silu(g, y, scale):
    sy = _sigmoid(y)
    ds_ = _dsilu(y, sy)
    if scale is None:
        return g * ds_
    a = y * sy
    n = lax.rsqrt(_rowsum(a * a) + EPS)
    ah = a * n
    return (scale * n) * (g - ah * _rowsum(g * ah)) * ds_


def _gdn_chunk_bwd(qn, kn, vv, gs, gr, do, sts, tis, ypre, bsz, seq, sb, riders):
    hb = GDN_HB
    nsb, ncb, sp = _gdn_specs(seq, sb, hb, True)
    grid = (bsz, GDN_HEADS // hb, nsb)
    any_spec, rider_shapes, rider_sems, wrap = _riding_exchange(riders, True, 11, 4, grid)
    ypre_spec = lambda cb: pl.BlockSpec((1, sb, hb * GDN_DK), lambda b, h, j: (b, nsb - 1 - j, cb))

    def body(q_ref, k_ref, v_ref, gs_ref, gr_ref, do_ref, st_ref, ti_ref, yq_ref, yk_ref, yv_ref,
             dq_ref, dk_ref, dv_ref, dgs_ref, ds_scr):
        @pl.when(pl.program_id(2) == 0)
        def _():
            ds_scr[...] = jnp.zeros_like(ds_scr)

        c = _chunk_consts()

        def chunk(nn, carry):
            n = ncb - 1 - nn
            r = pl.ds(pl.multiple_of(n * CHUNK, CHUNK), CHUNK)
            rs = pl.ds(pl.multiple_of(n * GDN_DK, GDN_DK), GDN_DK)
            gsv = gs_ref[0, r, :]
            heads = list(range(hb))
            sls = [slice(ih * GDN_DK, (ih + 1) * GDN_DK) for ih in heads]
            q = [q_ref[0, r, sl] for sl in sls]
            k = [k_ref[0, r, sl] for sl in sls]
            v = [v_ref[0, r, sl] for sl in sls]
            do_ = [do_ref[0, r, sl] for sl in sls]
            s = [st_ref[0, ih, rs, :] for ih in heads]
            tinv = [ti_ref[0, ih, r, :] for ih in heads]
            dsn = [ds_scr[ih] for ih in heads]
            beta, dc, eg, egl, ekd = zip(*[
                _gdn_gates(gsv, gr_ref[0, n, pl.ds(ih, 1), :], ih, c) for ih in heads])
            mul = lambda a, b: a * b
            kb = _hmap(mul, k, beta)
            rhs_w = _hmap(mul, kb, eg)
            u = _hmap(lambda t_, a, b: _mm3(t_, a * b), tinv, v, beta)
            w = _hmap(_mm3, tinv, rhs_w)
            amat = _hmap(lambda a, b, d_: jnp.where(c["strict"], _mm(a, b, NT) * d_, 0.0), kb, k, dc)
            qk = _hmap(lambda a, b, d_: _mm(a, b, NT) * d_, q, k, dc)
            qd = _hmap(mul, q, eg)
            kd = _hmap(mul, k, ekd)
            v_new = _hmap(lambda a, b, s_: a - _mm(b, s_), u, w, s)
            dv_new = _hmap(lambda qk_, d_, kd_, dn: _mm(qk_, d_, TN) + _mm(kd_, dn), qk, do_, kd, dsn)
            dqk = _hmap(lambda d_, vn: _mm(d_, vn, NT), do_, v_new)
            dqd = _hmap(lambda d_, s_: _mm(d_, s_, NT), do_, s)
            ds_new = _hmap(lambda qd_, d_, dn, e, w_, dvn: _mm(qd_, d_, TN) + dn * e - _mm(w_, dvn, TN),
                           qd, do_, dsn, egl, w, dv_new)
            dkd = _hmap(lambda vn, dn: _mm(vn, dn, NT), v_new, dsn)
            dgl = _hmap(lambda s_, dn, e: _colsum(_rowsum(s_ * dn)) * e, s, dsn, egl)
            dw = _hmap(lambda dvn, s_: -_mm(dvn, s_, NT), dv_new, s)
            dru = _hmap(lambda t_, a: _mm3(t_, a, TN), tinv, dv_new)
            drw = _hmap(lambda t_, a: _mm3(t_, a, TN), tinv, dw)
            da = _hmap(lambda a, u_, b, w_: jnp.where(c["strict"], -(_mm(a, u_, NT) + _mm(b, w_, NT)), 0.0), dru, u, drw, w)
            m = _hmap(mul, da, dc)
            dkb = _hmap(lambda a, e, m_, k_: a * e + _mm(m_, k_), drw, eg, m, k)
            mq = _hmap(mul, dqk, dc)
            dq = _hmap(lambda mq_, k_, a, e: _mm(mq_, k_) + a * e, mq, k, dqd, eg)
            dk = _hmap(lambda m_, kb_, mq_, q_, a, e, b, be: _mm(m_, kb_, TN) + _mm(mq_, q_, TN) + a * e + b * be,
                       m, kb, mq, q, dkd, ekd, dkb, beta)
            dbeta = _hmap(lambda a, v_, b, k_: _rowsum(a * v_) + _rowsum(b * k_), dru, v, dkb, k)
            pq = _hmap(lambda a, am, b, qk_: a * am + b * qk_, da, amat, dqk, qk)
            ekk = _hmap(lambda a, b: _rowsum(a * b), dkd, kd)
            dgc = _hmap(lambda pq_, a, rw, b, qd_, e, gl_: (
                _rowsum(pq_) - _mmsel(pq_, c["ones"], TN) + (_rowsum(a * rw) + _rowsum(b * qd_) - e)
                + jnp.where(c["row1"] == CHUNK - 1, _colsum(e) + gl_, 0.0)), pq, drw, rhs_w, dqd, qd, ekk, dgl)
            dv = _hmap(mul, dru, beta)
            dyq = _hmap(lambda g_, sl: _through_norm_silu(g_, yq_ref[0, r, sl], GDN_DK ** -0.5), dq, sls)
            dyk = _hmap(lambda g_, sl: _through_norm_silu(g_, yk_ref[0, r, sl], 1.0), dk, sls)
            dyv = _hmap(lambda g_, sl: _through_norm_silu(g_, yv_ref[0, r, sl], None), dv, sls)
            dgs = jnp.zeros((CHUNK, LANES), F32)
            for ih in heads:
                ds_scr[ih] = ds_new[ih]
                dq_ref[0, r, sls[ih]] = dyq[ih]
                dk_ref[0, r, sls[ih]] = dyk[ih]
                dv_ref[0, r, sls[ih]] = dyv[ih]
                dgs = dgs + jnp.where(c["lane"] == ih, dbeta[ih], jnp.where(c["lane"] == ih + 8, dgc[ih], 0.0))
            dgs_ref[0, r, :] = dgs
            return carry

        lax.fori_loop(0, ncb, chunk, 0)

    res = _pcall(
        wrap(body), name="gdn_chunk_bwd", grid=grid,
        in_specs=[sp["wide"](), sp["wide"](), sp["wide"](), sp["gs"](), sp["gr"], sp["wide"](), sp["st"], sp["ti"],
                  ypre_spec(0), ypre_spec(1), ypre_spec(2)] + any_spec,
        out_specs=[sp["wide"](), sp["wide"](), sp["wide"](), sp["gs"]()] + any_spec,
        out_shape=[jax.ShapeDtypeStruct((bsz, seq, D_MODEL), F32)] * 3 + [jax.ShapeDtypeStruct((bsz, seq, LANES), F32)]
        + rider_shapes,
        scratch_shapes=[pltpu.VMEM((hb, GDN_DK, GDN_DK), F32)] + rider_sems,
        compiler_params=_params(("arbitrary", "arbitrary", "arbitrary")),
    )(qn, kn, vv, gs, gr, do, sts, tis, ypre, ypre, ypre, *riders)
    return res[:4], res[4:]


def _gdn_gates_bwd(proj, dgs, gp, tm):
    t = proj.shape[0]

    def body(i, sm_ref, dgs_ref, gp_ref, dsm_ref, dgp_ref):
        sm = sm_ref[...]
        lane = lax.broadcasted_iota(jnp.int32, sm.shape, 1)
        dsel = dgs_ref[...]
        is_g = (lane >= 8) & (lane < 16)
        dsel = jnp.where(is_g, _mmx(_block_tri(tm, True), dsel), dsel)
        beta = _sigmoid(sm)
        xb = sm + gp_ref[1:2, :]
        a_neg = -jnp.exp(gp_ref[0:1, :])
        sp = _softplus(xb)
        dxb = jnp.where(is_g, dsel * a_neg * _sigmoid(xb), 0.0)
        dsm_ref[...] = jnp.where(lane < 8, dsel * beta * (1.0 - beta), dxb)
        dgp_ref[1:2, :] += _colsum(dxb)
        dgp_ref[0:1, :] += _colsum(jnp.where(is_g, dsel * a_neg * sp, 0.0))

    ins = [("row", proj, LANES, SMALL_CB), ("row", dgs, LANES, 0), ("full", gp)]
    return _rowwise("gdn_gates_bwd", body, t, tm, ins, [(LANES, F32)], accs=[(SUBLANES, LANES)])


def _dh1_first_bwd(dproj, wp_in, x, dx1, w, scatter_riders):
    d = D_MODEL

    def epilogue(dh, x_ref, dx1_ref, w_ref, dx_ref, dw_ref):
        xh, r = _rms(x_ref[...], d)
        dw_ref[...] += _colsum(dh * xh)
        dx_ref[...] = dx1_ref[...] + _rms_bwd(xh, r, dh * w_ref[...], d)

    return _matmul_rows("mm_dh1_first_bwd", dproj, wp_in, "nt", epilogue, [x, dx1], [w], [(d, F32)], accs=[(1, d)],
                        tk=PROJ_W // 2, scatter_riders=scatter_riders)


def _gather_two_level(name, arrays):
    n = len(arrays)
    n_sem = 7

    def body(*refs):
        ins, outs = refs[:n], refs[n:2 * n]
        send_sems, recv_sems, loc_sems = refs[2 * n:]
        x, y, c = lax.axis_index("x"), lax.axis_index("y"), lax.axis_index("c")
        slot = lambda px, py, pc: 4 * px + 2 * py + pc
        sibling = (x, y, 1 - c)
        chips = [(1 - x, y), (x, 1 - y), (1 - x, 1 - y)]

        def copy(t, k, src, block, to):
            return pltpu.make_async_remote_copy(
                src_ref=src, dst_ref=outs[t].at[block], send_sem=send_sems.at[t, k], recv_sem=recv_sems.at[t, k],
                device_id=to, device_id_type=pl.DeviceIdType.MESH)

        own, first, passed = [], [], []
        for t in range(n):
            own.append(pltpu.make_async_copy(ins[t], outs[t].at[slot(x, y, c)], loc_sems.at[t]))
            first.append(copy(t, 0, ins[t], slot(x, y, c), sibling))
            first += [copy(t, 1 + j, ins[t], slot(x, y, c), (px, py, c)) for j, (px, py) in enumerate(chips)]
        for cp in own + first:
            cp.start()
        for t in range(n):
            for j, (px, py) in enumerate(chips):
                copy(t, 1 + j, ins[t], slot(px, py, c), (px, py, c)).wait_recv()
                fwd = copy(t, 4 + j, outs[t].at[slot(px, py, c)], slot(px, py, c), sibling)
                fwd.start()
                passed.append(fwd)
        for t in range(n):
            copy(t, 0, ins[t], slot(x, y, 1 - c), sibling).wait_recv()
            for j, (px, py) in enumerate(chips):
                copy(t, 4 + j, ins[t], slot(px, py, 1 - c), sibling).wait_recv()
        for cp in first + passed:
            cp.wait_send()
        for cp in own:
            cp.wait()

    return _pcall(
        body, name=name,
        in_specs=[pl.BlockSpec(memory_space=pl.ANY)] * n,
        out_specs=[pl.BlockSpec(memory_space=pl.ANY)] * n,
        out_shape=_exchange_out_shapes(arrays, False),
        scratch_shapes=[pltpu.SemaphoreType.DMA((n, n_sem)), pltpu.SemaphoreType.DMA((n, n_sem)), pltpu.SemaphoreType.DMA((n,))],
    )(*arrays)


def _exchange_out_shapes(arrays, scatter):
    return [jax.ShapeDtypeStruct(a.shape if scatter else (N_DEV,) + a.shape, a.dtype) for a in arrays]


def _exchange_sems(n):
    return [pltpu.SemaphoreType.DMA((n, N_DEV - 1)), pltpu.SemaphoreType.DMA((n, N_DEV - 1)), pltpu.SemaphoreType.DMA((n,))]


def _exchange_phase(ins, outs, sems, scatter, start):
    send_sems, recv_sems, loc_sems = sems
    x, y, c = lax.axis_index("x"), lax.axis_index("y"), lax.axis_index("c")
    me = 4 * x + 2 * y + c
    for t in range(len(ins)):
        loc = pltpu.make_async_copy(ins[t].at[me] if scatter else ins[t], outs[t].at[me], loc_sems.at[t])
        if start:
            loc.start()
        else:
            loc.wait()
        for k in range(N_DEV - 1):
            bx, by, bc = ((k + 1) >> 2) & 1, ((k + 1) >> 1) & 1, (k + 1) & 1
            px = 1 - x if bx else x
            py = 1 - y if by else y
            pc = 1 - c if bc else c
            peer = 4 * px + 2 * py + pc
            src = ins[t].at[peer] if scatter else ins[t]
            copy = lambda dst: pltpu.make_async_remote_copy(
                src_ref=src, dst_ref=dst, send_sem=send_sems.at[t, k], recv_sem=recv_sems.at[t, k],
                device_id=(px, py, pc), device_id_type=pl.DeviceIdType.MESH)
            if start:
                copy(outs[t].at[me]).start()
            else:
                copy(outs[t].at[me]).wait_send()
                copy(outs[t].at[peer]).wait_recv()


def _adam_math(w, g, m, v):
    m = ADAM_B1 * m + (1.0 - ADAM_B1) * g
    v = ADAM_B2 * v + (1.0 - ADAM_B2) * (g * g)
    m_hat = m / (1.0 - ADAM_B1 ** ADAM_STEP)
    v_hat = v / (1.0 - ADAM_B2 ** ADAM_STEP)
    delta = -ADAM_LR * (m_hat / (jnp.sqrt(v_hat) + ADAM_EPS) + ADAM_WD * w)
    return delta, m, v


def _adam_big(name, parts, w, m, v, tm):
    r, c = w.shape
    tm = tm if r % tm == 0 else r

    def body(p_ref, w_ref, m_ref, v_ref, g_ref, d_ref, nm_ref, nv_ref):
        g = p_ref[0].astype(F32)
        for s in range(1, N_DEV):
            g = g + p_ref[s].astype(F32)
        g_ref[...] = g
        d_ref[...], nm_ref[...], nv_ref[...] = _adam_math(w_ref[...], g, m_ref[...], v_ref[...])

    blk = lambda: pl.BlockSpec((tm, c), lambda i: (i, 0))
    return _pcall(
        body, name=name, grid=(r // tm,),
        in_specs=[pl.BlockSpec((N_DEV, tm, c), lambda i: (0, i, 0)), blk(), blk(), blk()],
        out_specs=[blk(), blk(), blk(), blk()],
        out_shape=[jax.ShapeDtypeStruct((r, c), F32)] * 4,
        compiler_params=_params(("parallel",)),
    )(parts, w, m, v)


SMALL_ROWS = 56
ROW_DD, ROW_LOSS = 5, 6


def _small_sum(gathered):
    def body(g_ref, o_ref, x_ref):
        s = g_ref[0]
        for dev in range(1, N_DEV):
            s = s + g_ref[dev]
        o_ref[...] = s
        ri = lax.broadcasted_iota(jnp.int32, (D_MODEL, LANES), 0)
        ro = lax.broadcasted_iota(jnp.int32, (D_MODEL, LANES), 1)
        heads = _mmx(jnp.broadcast_to(s[ROW_DD:ROW_DD + 1, :], (SUBLANES, D_MODEL)), (ri // SSD_P == ro).astype(F32))
        loss = _rowsum(jnp.broadcast_to(s[ROW_LOSS:ROW_LOSS + 1, :], (SUBLANES, D_MODEL)))
        row = lax.broadcasted_iota(jnp.int32, (SUBLANES, LANES), 0)
        x_ref[...] = jnp.where(row == 0, heads, jnp.broadcast_to(loss, (SUBLANES, LANES)))

    return _pcall(
        body, name="small_sum",
        out_shape=[jax.ShapeDtypeStruct((SMALL_ROWS, D_MODEL), F32), jax.ShapeDtypeStruct((SUBLANES, LANES), F32)],
        compiler_params=_params(None),
    )(gathered)


def _adam_small(g, w, m, v):
    def body(g_ref, w_ref, m_ref, v_ref, d_ref, nm_ref, nv_ref):
        d_ref[...], nm_ref[...], nv_ref[...] = _adam_math(w_ref[...], g_ref[...], m_ref[...], v_ref[...])

    return _pcall(body, name="adam_small", out_shape=[jax.ShapeDtypeStruct(g.shape, F32)] * 3,
                  compiler_params=_params(None))(g, w, m, v)


def _pack(pieces, rows):
    flat = jnp.concatenate([p.reshape(-1).astype(F32) for p in pieces])
    return jnp.pad(flat, (0, rows * D_MODEL - flat.shape[0])).reshape(rows, D_MODEL)


def _unpack(packed, shapes):
    flat = packed.reshape(-1)
    out, off = [], 0
    for shp in shapes:
        size = 1
        for s in shp:
            size *= s
        out.append(flat[off:off + size].reshape(shp))
        off += size
    return out


def _permute_in(w):
    pad = jnp.zeros((w.shape[0], PROJ_W - D_IN), w.dtype)
    return jnp.concatenate([w[:, 0:4096], w[:, 4112:6672], w[:, 4096:4112], w[:, 6672:6688], pad], axis=1)


def _unpermute_in(g):
    return jnp.concatenate([g[:, 0:4096], g[:, 6656:6672], g[:, 4096:6656], g[:, 6672:6688]], axis=1)


def _lane_row(vec, start):
    return jnp.zeros((LANES,), F32).at[start:start + vec.shape[0]].set(vec)


def _cols_from_shards(g):
    return jnp.transpose(g, (1, 0, 2)).reshape(g.shape[1], N_DEV * g.shape[2])


def _cols_to_shards(a):
    return jnp.transpose(a.astype(BF16).reshape(a.shape[0], N_DEV, a.shape[1] // N_DEV), (1, 0, 2))


def _rows_to_shards(a):
    return a.astype(BF16).reshape(N_DEV, a.shape[0] // N_DEV, a.shape[1])


def _local_step(x, tgt, wp_in, rest, p, rest_is_sharded):
    bsz, seq, d = x.shape
    t = bsz * seq
    x2 = x.reshape(t, d)
    tgt2 = tgt.reshape(t, d)
    tm = min(256, seq)
    tm_wide = min(128, seq)
    sb = min(512, seq)

    gp = jnp.zeros((SUBLANES, LANES), F32).at[0].set(_lane_row(p["gdn_a_log"], 8)).at[1].set(_lane_row(p["gdn_dt_bias"], 8))
    sp = jnp.zeros((SUBLANES, LANES), F32).at[0].set(_lane_row(p["ssd_a_log"], 16)).at[1].set(_lane_row(p["ssd_dt_bias"], 16))
    dvec = jnp.repeat(p["ssd_d"], SSD_P).reshape(1, d)
    row = lambda v: v.reshape(1, -1)
    pre_mix, post_mix, pre_ffn, post_ffn = (row(p[k]) for k in ("pre_mix_norm", "post_mix_norm", "pre_ffn_norm", "post_ffn_norm"))
    gnw, snw = row(p["gdn_norm_w"]), row(p["ssd_norm_w"])
    gcw, scw, scb, fcw, fcb = p["gdn_conv_w"], p["ssd_conv_w"], row(p["ssd_conv_b"]), p["ffn_conv_w"], row(p["ffn_conv_b"])

    h1 = _norm_cast("norm_in", x2, pre_mix, tm)
    proj = _matmul("mm_proj", h1, wp_in, "nn", F32)
    b3 = lambda a: a.reshape(bsz, seq, a.shape[-1])
    b2 = lambda a: a.reshape(t, a.shape[-1])
    rows_of = lambda a, lo, n: jnp.transpose(a[:, lo:lo + n].reshape(bsz, seq // CHUNK, CHUNK, n), (0, 3, 1, 2))
    qn, kn, vv, gs, ypre_gdn = _gdn_prep(proj, gcw, gp, seq, tm)
    qn, kn, vv, gs = b3(qn), b3(kn), b3(vv), b3(gs)
    gr = jnp.transpose(b2(gs)[:, 8:8 + GDN_HEADS].reshape(bsz, seq // CHUNK, CHUNK, GDN_HEADS), (0, 1, 3, 2))
    (o_gdn, gdn_st, gdn_ti), gathered = _gdn_chunk_fwd(qn, kn, vv, gs, gr, bsz, seq, sb, list(rest) if rest_is_sharded else [])
    if rest_is_sharded:
        w_out, w_up, w_down = gathered[0].reshape(-1, d), _cols_from_shards(gathered[1]), gathered[2].reshape(-1, d)
    else:
        w_out, w_up, w_down = rest
    o_gdn = b2(o_gdn)
    xs, bc, dtx, acsx, acs, ypre_ssd = _ssd_prep(proj, scw, scb, sp, seq, tm)
    ar = rows_of(acs, 16, SSD_HEADS)
    y_ssd, ssd_st = _ssd_chunk_fwd(b3(xs), b3(bc), b3(dtx), b3(acsx), b3(acs), ar, bsz, seq, sb)
    y_ssd = b2(y_ssd)
    mixin = _gate_norm(o_gdn, y_ssd, xs, proj, gnw, snw, dvec, tm)
    mix, x1, h2 = _out_mid(mixin, w_out, x2, post_mix, pre_ffn)
    u_pre = _matmul("mm_up", h2, w_up, "nn", F32)
    act, u = _ffn_act(u_pre, fcw, fcb, seq, tm_wide)
    dy, df, loss_lanes, d_post_ffn = _down_final(act, w_down, x1, tgt2, post_ffn)

    g_down = _matmul("mm_dw_down", act, df, "tn", BF16, tm=1408, tk=2048)
    dact = _matmul("mm_dact", df, w_down, "nt", F32, tn=1408)
    du_pre, d_fcw, d_fcb = _ffn_bwd(u, u_pre, dact, fcw, seq, tm_wide)
    g_up = _matmul("mm_dw_up", h2, du_pre, "tn", BF16, tk=2048)
    dx1, dmix, d_post_mix, d_pre_ffn = _dh2_mid_bwd(du_pre, w_up, x1, mix, dy, post_mix, pre_ffn)
    g_out = _matmul("mm_dw_out", mixin, dmix, "tn", BF16, tk=2048)
    do_gdn, dza, dy_ssd, dxs_d, dzs, d_gnw, d_snw, d_dd = _dmixin_gate_norm_bwd(dmix, w_out, o_gdn, y_ssd, xs, proj, gnw, snw, dvec)
    dxs_c, dbc, ddt, dacs = (b2(a) for a in _ssd_chunk_bwd(
        b3(xs), b3(bc), b3(dtx), b3(acsx), b3(acs), ar, b3(dy_ssd), ssd_st, bsz, seq, sb))
    riders = [_rows_to_shards(g_out), _cols_to_shards(g_up), _rows_to_shards(g_down)] if rest_is_sharded else []
    dgdn, received = _gdn_chunk_bwd(qn, kn, vv, gs, gr, b3(do_gdn), gdn_st, gdn_ti, b3(ypre_gdn), bsz, seq, min(256, seq), riders)
    if rest_is_sharded:
        g_out, g_up, g_down = received
    dyq, dyk, dyv, dgs = (b2(a) for a in dgdn)
    dsm_gdn, d_gp = _gdn_gates_bwd(proj, dgs, gp, tm)
    dpre_ssd, dsm, d_scb, d_sp = _ssd_prep_bwd(ypre_ssd, proj, dxs_c, dxs_d, dbc, ddt, dacs, dsm_gdn, sp, tm)
    dproj, d_gcw, d_scw = _assemble_dproj((dyq, dyk, dyv), dza, dzs, dpre_ssd, dsm, proj, gcw, scw, seq, tm)
    g_in = _matmul("mm_dw_in", h1, dproj, "tn", BF16, tk=2048)
    if rest_is_sharded:
        (dx, d_pre_mix), (g_in,) = _dh1_first_bwd(dproj, wp_in, x2, dx1, pre_mix, [_cols_to_shards(_unpermute_in(g_in))])
    else:
        dx, d_pre_mix = _dh1_first_bwd(dproj, wp_in, x2, dx1, pre_mix, [])

    small = dict(pre_mix_norm=d_pre_mix, ssd_norm_w=d_snw, post_mix_norm=d_post_mix, pre_ffn_norm=d_pre_ffn,
                 post_ffn_norm=d_post_ffn, dd_lanes=d_dd, loss_lanes=loss_lanes, gdn_gates=d_gp, ssd_gates=d_sp,
                 gdn_norm_w=d_gnw, gdn_conv_w=d_gcw[0:4], ssd_conv_w=d_scw[0:4], ssd_conv_b=d_scb,
                 ffn_conv_w=d_fcw[0:3], ffn_conv_b=d_fcb)
    return dx.reshape(bsz, seq, d), g_in, g_out, g_up, g_down, small


def kernel(x, pre_mix_norm, w_in, gdn_conv_w, gdn_a_log, gdn_dt_bias, gdn_norm_w, ssd_conv_w, ssd_conv_b, ssd_a_log, ssd_dt_bias, ssd_d, ssd_norm_w, w_out, post_mix_norm, pre_ffn_norm, w_up, ffn_conv_w, ffn_conv_b, w_down, post_ffn_norm, loss_target, m_pre_mix_norm, m_w_in, m_gdn_conv_w, m_gdn_a_log, m_gdn_dt_bias, m_gdn_norm_w, m_ssd_conv_w, m_ssd_conv_b, m_ssd_a_log, m_ssd_dt_bias, m_ssd_d, m_ssd_norm_w, m_w_out, m_post_mix_norm, m_pre_ffn_norm, m_w_up, m_ffn_conv_w, m_ffn_conv_b, m_w_down, m_post_ffn_norm, v_pre_mix_norm, v_w_in, v_gdn_conv_w, v_gdn_a_log, v_gdn_dt_bias, v_gdn_norm_w, v_ssd_conv_w, v_ssd_conv_b, v_ssd_a_log, v_ssd_dt_bias, v_ssd_d, v_ssd_norm_w, v_w_out, v_post_mix_norm, v_pre_ffn_norm, v_w_up, v_ffn_conv_w, v_ffn_conv_b, v_w_down, v_post_ffn_norm):
    names = ["pre_mix_norm", "w_in", "gdn_conv_w", "gdn_a_log", "gdn_dt_bias", "gdn_norm_w", "ssd_conv_w", "ssd_conv_b",
             "ssd_a_log", "ssd_dt_bias", "ssd_d", "ssd_norm_w", "w_out", "post_mix_norm", "pre_ffn_norm", "w_up",
             "ffn_conv_w", "ffn_conv_b", "w_down", "post_ffn_norm"]
    w_args = [pre_mix_norm, w_in, gdn_conv_w, gdn_a_log, gdn_dt_bias, gdn_norm_w, ssd_conv_w, ssd_conv_b, ssd_a_log, ssd_dt_bias, ssd_d, ssd_norm_w, w_out, post_mix_norm, pre_ffn_norm, w_up, ffn_conv_w, ffn_conv_b, w_down, post_ffn_norm]
    m_args = [m_pre_mix_norm, m_w_in, m_gdn_conv_w, m_gdn_a_log, m_gdn_dt_bias, m_gdn_norm_w, m_ssd_conv_w, m_ssd_conv_b, m_ssd_a_log, m_ssd_dt_bias, m_ssd_d, m_ssd_norm_w, m_w_out, m_post_mix_norm, m_pre_ffn_norm, m_w_up, m_ffn_conv_w, m_ffn_conv_b, m_w_down, m_post_ffn_norm]
    v_args = [v_pre_mix_norm, v_w_in, v_gdn_conv_w, v_gdn_a_log, v_gdn_dt_bias, v_gdn_norm_w, v_ssd_conv_w, v_ssd_conv_b, v_ssd_a_log, v_ssd_dt_bias, v_ssd_d, v_ssd_norm_w, v_w_out, v_post_mix_norm, v_pre_ffn_norm, v_w_up, v_ffn_conv_w, v_ffn_conv_b, v_w_down, v_post_ffn_norm]
    w = {k: a[0] for k, a in zip(names, w_args)}
    m = {k: a[0] for k, a in zip(names, m_args)}
    v = {k: a[0] for k, a in zip(names, v_args)}
    idx = 4 * lax.axis_index("x") + 2 * lax.axis_index("y") + lax.axis_index("c")
    big = ("w_in", "w_out", "w_up", "w_down")
    conv = ("gdn_conv_w", "ssd_conv_w", "ffn_conv_w")

    conv_local = jnp.concatenate([jnp.pad(w[k], ((0, 4 - w[k].shape[0]), (0, 0))) for k in conv], axis=1)
    g_in, g_conv = _gather_two_level("gather_weights", [w["w_in"].astype(BF16), conv_local])
    wp_in = _permute_in(_cols_from_shards(g_in))
    p = {k: w[k] for k in names if k not in big and k not in conv}
    off = 0
    for k in conv:
        cw = w[k].shape[1]
        p[k] = jnp.transpose(g_conv[:, :w[k].shape[0], off:off + cw], (1, 0, 2)).reshape(w[k].shape[0], N_DEV * cw)
        off += cw

    rest = tuple(w[k].astype(BF16) for k in ("w_out", "w_up", "w_down"))
    dx, p_in, p_out, p_up, p_down, small = _local_step(x, loss_target, wp_in, rest, p, True)

    gate_row = jnp.concatenate([small["gdn_gates"][0], small["gdn_gates"][1], small["ssd_gates"][0], small["ssd_gates"][1],
                                small["gdn_norm_w"][0], jnp.zeros((D_MODEL - 5 * LANES,), F32)]).reshape(1, D_MODEL)
    pack = _pack([small["pre_mix_norm"], small["ssd_norm_w"], small["post_mix_norm"], small["pre_ffn_norm"],
                  small["post_ffn_norm"], small["dd_lanes"], small["loss_lanes"], gate_row,
                  small["gdn_conv_w"], small["ssd_conv_w"], jnp.pad(small["ssd_conv_b"], ((0, 0), (0, 512))),
                  jnp.pad(small["ffn_conv_w"].reshape(-1), (0, 17 * D_MODEL - 3 * 2 * D_FF)),
                  jnp.pad(small["ffn_conv_b"], ((0, 0), (0, 512)))], SMALL_ROWS)
    (pack_all,) = _gather_two_level("gather_small", [pack])
    ssum, extra = _small_sum(pack_all)

    grads, deltas, new_m, new_v = {}, {}, {}, {}
    for k, parts in (("w_in", p_in), ("w_out", p_out), ("w_up", p_up), ("w_down", p_down)):
        grads[k], deltas[k], new_m[k], new_v[k] = _adam_big("adam_" + k, parts, w[k], m[k], v[k], 256)

    flat = ssum.reshape(-1)
    gate = ssum[7]
    sg = dict(pre_mix_norm=ssum[0], ssd_norm_w=ssum[1], post_mix_norm=ssum[2], pre_ffn_norm=ssum[3], post_ffn_norm=ssum[4],
              gdn_a_log=gate[8:16], gdn_dt_bias=gate[LANES + 8:LANES + 16], ssd_a_log=gate[2 * LANES + 16:2 * LANES + 32],
              ssd_dt_bias=gate[3 * LANES + 16:3 * LANES + 32], gdn_norm_w=gate[4 * LANES:5 * LANES], ssd_d=extra[0, 0:SSD_HEADS])
    o = 8 * D_MODEL
    full_gcw = flat[o:o + 4 * 3072].reshape(4, 3072)
    o += 12 * D_MODEL
    full_scw = flat[o:o + 4 * 1536].reshape(4, 1536)
    o += 6 * D_MODEL
    sg["ssd_conv_b"] = flat[o:o + 1536]
    o += 2 * D_MODEL
    full_fcw = flat[o:o + 3 * 2 * D_FF].reshape(3, 2 * D_FF)
    o += 17 * D_MODEL
    sg["ffn_conv_b"] = flat[o:o + 2 * D_FF]
    for k, full in (("gdn_conv_w", full_gcw), ("ssd_conv_w", full_scw), ("ffn_conv_w", full_fcw)):
        cw = w[k].shape[1]
        sg[k] = lax.dynamic_slice_in_dim(full, idx * cw, cw, axis=1)
    small_names = [k for k in names if k not in big]
    rows = 24
    gpk = _pack([sg[k] for k in small_names], rows)
    dpk, mpk, vpk = _adam_small(gpk, _pack([w[k] for k in small_names], rows), _pack([m[k] for k in small_names], rows),
                                _pack([v[k] for k in small_names], rows))
    shapes = [w[k].shape for k in small_names]
    for k, g_, d_, m_, v_ in zip(small_names, _unpack(gpk, shapes), _unpack(dpk, shapes), _unpack(mpk, shapes), _unpack(vpk, shapes)):
        grads[k], deltas[k], new_m[k], new_v[k] = g_, d_, m_, v_

    loss = extra[1, 0]
    lead = lambda a: a[None]
    return (loss, dx, *[lead(grads[k]) for k in names], *[lead(deltas[k]) for k in names],
            *[lead(new_m[k]) for k in names], *[lead(new_v[k]) for k in names])
```

```python
import functools

import jax
import jax.numpy as jnp
from jax import lax
from jax.experimental import pallas as pl
from jax.experimental.pallas import tpu as pltpu

F32 = jnp.float32
BF16 = jnp.bfloat16
MXU_DTYPE = jnp.bfloat16
HIGHEST = lax.Precision.HIGHEST
VMEM_LIMIT_V7X = 48 * 1024 * 1024
SUBLANES = 8
LANES = 128

D_MODEL = 1024
GDN_HEADS = 8
GDN_DK = 128
SSD_HEADS = 16
SSD_P = 64
SSD_GROUPS = 2
SSD_HPG = 8
SSD_N = 128
CHUNK = 64
D_FF = 2816
EPS = 1e-6
N_DEV = 8
PROJ_W = 7168
SMALL_CB = 52
D_IN = 6688

ADAM_LR = 0.001
ADAM_B1 = 0.9
ADAM_B2 = 0.999
ADAM_EPS = 1e-08
ADAM_WD = 0.01
ADAM_STEP = 10

NN = (((1,), (0,)), ((), ()))
NT = (((1,), (1,)), ((), ()))
TN = (((0,), (0,)), ((), ()))


def _pcall(body, **kw):
    return pl.pallas_call(body, **kw)


def _mm(a, b, dims=NN):
    return lax.dot_general(a.astype(MXU_DTYPE), b.astype(MXU_DTYPE), dims, preferred_element_type=F32)


def _mmx(a, b, dims=NN):
    return lax.dot_general(a, b, dims, precision=HIGHEST, preferred_element_type=F32)


def _split(a):
    hi = a.astype(MXU_DTYPE)
    return hi, (a - hi.astype(F32)).astype(MXU_DTYPE)


def _mm3(a, b, dims=NN):
    (ah, al), (bh, bl) = _split(a), _split(b)
    dot = lambda p, q: lax.dot_general(p, q, dims, preferred_element_type=F32)
    return dot(ah, bh) + (dot(ah, bl) + dot(al, bh))


def _mmsel(a, sel, dims=NN, terms=2):
    s = sel.astype(MXU_DTYPE)
    out = None
    for _ in range(terms):
        part = a.astype(MXU_DTYPE)
        a = a - part.astype(F32)
        prod = lax.dot_general(part, s, dims, preferred_element_type=F32)
        out = prod if out is None else out + prod
    return out


def _sigmoid(x):
    return 0.5 * jnp.tanh(0.5 * x) + 0.5


def _softplus(x):
    return jnp.maximum(x, 0.0) + jnp.log(1.0 + jnp.exp(-jnp.abs(x)))


def _dsilu(x, s):
    return s * (1.0 + x * (1.0 - s))


def _rowsum(x):
    return jnp.sum(x, axis=1, keepdims=True)


def _colsum(x):
    return jnp.sum(x, axis=0, keepdims=True)


def _pick(dim, pref):
    if dim <= pref:
        return dim
    best = None
    t = LANES
    while t <= pref:
        if dim % t == 0:
            best = t
        t += LANES
    return dim if best is None else best


def _params(sem):
    return pltpu.CompilerParams(dimension_semantics=sem, vmem_limit_bytes=VMEM_LIMIT_V7X)


def _matmul(name, a, b, mode, out_dtype, tm=1024, tn=1024, tk=1024):
    if mode == "nn":
        (m, k), (_, n) = a.shape, b.shape
    elif mode == "nt":
        (m, k), (n, _) = a.shape, b.shape
    else:
        (k, m), (_, n) = a.shape, b.shape
    tm, tn, tk = _pick(m, tm), _pick(n, tn), _pick(k, tk)
    nk = k // tk
    if mode == "tn":
        a_spec = pl.BlockSpec((tk, tm), lambda i, j, kk: (kk, i))
    else:
        a_spec = pl.BlockSpec((tm, tk), lambda i, j, kk: (i, kk))
    if mode == "nt":
        b_spec = pl.BlockSpec((tn, tk), lambda i, j, kk: (j, kk))
    else:
        b_spec = pl.BlockSpec((tk, tn), lambda i, j, kk: (kk, j))
    dims = {"nn": NN, "nt": NT, "tn": TN}[mode]

    def body(a_ref, b_ref, o_ref, *acc):
        if nk == 1:
            o_ref[...] = _mm(a_ref[...], b_ref[...], dims).astype(out_dtype)
            return
        kk = pl.program_id(2)

        @pl.when(kk == 0)
        def _():
            acc[0][...] = jnp.zeros_like(acc[0])

        acc[0][...] += _mm(a_ref[...], b_ref[...], dims)

        @pl.when(kk == nk - 1)
        def _():
            o_ref[...] = acc[0][...].astype(out_dtype)

    return _pcall(
        body, name=name, grid=(m // tm, n // tn, nk),
        in_specs=[a_spec, b_spec],
        out_specs=pl.BlockSpec((tm, tn), lambda i, j, kk: (i, j)),
        out_shape=jax.ShapeDtypeStruct((m, n), out_dtype),
        scratch_shapes=[pltpu.VMEM((tm, tn), F32)] if nk > 1 else [],
        compiler_params=_params(("parallel", "parallel", "arbitrary")),
    )(a, b)


def _matmul_rows(name, a, b, mode, epilogue, row_ins, full_ins, outs, accs=(), tm=512, tk=1024, scatter_riders=()):
    if mode == "nn":
        (m, k), (_, n) = a.shape, b.shape
    else:
        (m, k), (n, _) = a.shape, b.shape
    tm, tk = _pick(m, tm), _pick(k, tk)
    nk = k // tk
    a_spec = pl.BlockSpec((tm, tk), lambda i, kk: (i, kk))
    b_spec = pl.BlockSpec((n, tk), lambda i, kk: (0, kk)) if mode == "nt" else pl.BlockSpec((tk, n), lambda i, kk: (kk, 0))
    dims = NT if mode == "nt" else NN
    n_row, n_full, n_out, n_acc = len(row_ins), len(full_ins), len(outs), len(accs)

    def body(a_ref, b_ref, *rest):
        ins = rest[:n_row + n_full]
        out_refs = rest[n_row + n_full:n_row + n_full + n_out]
        acc_refs = rest[n_row + n_full + n_out:n_row + n_full + n_out + n_acc]
        prod_scr = rest[-1]
        i, kk = pl.program_id(0), pl.program_id(1)

        if n_acc:
            @pl.when((i == 0) & (kk == 0))
            def _():
                for r in acc_refs:
                    r[...] = jnp.zeros_like(r)

        if nk == 1:
            epilogue(_mm(a_ref[...], b_ref[...], dims), *ins, *out_refs, *acc_refs)
            return

        @pl.when(kk == 0)
        def _():
            prod_scr[...] = jnp.zeros_like(prod_scr)

        prod_scr[...] += _mm(a_ref[...], b_ref[...], dims)

        @pl.when(kk == nk - 1)
        def _():
            epilogue(prod_scr[...], *ins, *out_refs, *acc_refs)

    grid = (m // tm, nk)
    riders = list(scatter_riders)
    n_in = 2 + n_row + n_full
    any_spec, rider_shapes, rider_sems, wrap = _riding_exchange(riders, True, n_in, n_out + n_acc, grid)
    row_ins = [r if isinstance(r, tuple) else (r, r.shape[1], 0) for r in row_ins]
    in_specs = [a_spec, b_spec] + [pl.BlockSpec((tm, w), lambda i, kk, cb=cb: (i, cb)) for _, w, cb in row_ins]
    in_specs += [pl.BlockSpec(f.shape, lambda i, kk, nd=f.ndim: (0,) * nd) for f in full_ins]
    row_ins = [r for r, _, _ in row_ins]
    out_specs = [pl.BlockSpec((tm, w), lambda i, kk: (i, 0)) for w, _ in outs]
    out_specs += [pl.BlockSpec(s, lambda i, kk: (0, 0)) for s in accs]
    out_shape = [jax.ShapeDtypeStruct((m, w), dt) for w, dt in outs] + [jax.ShapeDtypeStruct(s, F32) for s in accs]
    res = _pcall(
        wrap(body), name=name, grid=grid,
        in_specs=in_specs + any_spec, out_specs=out_specs + any_spec, out_shape=out_shape + rider_shapes,
        scratch_shapes=[pltpu.VMEM((tm, n), F32)] + rider_sems,
        compiler_params=_params(("arbitrary", "arbitrary")),
    )(a, b, *row_ins, *full_ins, *riders)
    return (res[:n_out + n_acc], res[n_out + n_acc:]) if riders else res


def _rowwise(name, body, n_rows, tm, ins, outs, accs=()):
    arrays, in_specs = [], []
    last8 = n_rows // SUBLANES - 1
    per = tm // SUBLANES
    for spec in ins:
        kind, arr = spec[0], spec[1]
        if kind == "full":
            in_specs.append(pl.BlockSpec(arr.shape, lambda i, nd=arr.ndim: (0,) * nd))
        else:
            w, cb = spec[2], spec[3]
            if kind == "row":
                in_specs.append(pl.BlockSpec((tm, w), lambda i, cb=cb: (i, cb)))
            elif kind == "prev":
                in_specs.append(pl.BlockSpec((SUBLANES, w), lambda i, cb=cb: (jnp.maximum(i * per - 1, 0), cb)))
            else:
                in_specs.append(pl.BlockSpec((SUBLANES, w), lambda i, cb=cb: (jnp.minimum((i + 1) * per, last8), cb)))
        arrays.append(arr)
    out_shape = [jax.ShapeDtypeStruct((n_rows, w), dt) for (w, dt) in outs]
    out_shape += [jax.ShapeDtypeStruct(s, F32) for s in accs]
    out_specs = [pl.BlockSpec((tm, w), lambda i: (i, 0)) for (w, _) in outs]
    out_specs += [pl.BlockSpec(s, lambda i: (0, 0)) for s in accs]
    n_io = len(ins) + len(outs)

    def kern(*refs):
        i = pl.program_id(0)
        if accs:
            @pl.when(i == 0)
            def _():
                for r in refs[n_io:]:
                    r[...] = jnp.zeros_like(r)
        body(i, *refs)

    res = _pcall(
        kern, name=name, grid=(n_rows // tm,), in_specs=in_specs, out_specs=out_specs, out_shape=out_shape,
        compiler_params=_params(("arbitrary",)),
    )(*arrays)
    return res


def _shift_down(x, halo, j):
    r = pltpu.roll(x, j, 0)
    hr = pltpu.roll(halo, j, 0)
    rows = lax.broadcasted_iota(jnp.int32, (SUBLANES, x.shape[1]), 0)
    top = jnp.where(rows < j, hr, r[0:SUBLANES])
    return jnp.concatenate([top, r[SUBLANES:]], axis=0)


def _shift_up(x, halo, j):
    tm = x.shape[0]
    r = pltpu.roll(x, tm - j, 0)
    hr = pltpu.roll(halo, SUBLANES - j, 0)
    rows = lax.broadcasted_iota(jnp.int32, (SUBLANES, x.shape[1]), 0)
    bot = jnp.where(rows >= SUBLANES - j, hr, r[tm - SUBLANES:])
    return jnp.concatenate([r[:tm - SUBLANES], bot], axis=0)


def _conv_taps(x, halo, kw):
    return [x if kw - 1 - k == 0 else _shift_down(x, halo, kw - 1 - k) for k in range(kw)]


def _conv(taps, w):
    y = taps[0] * w[0:1]
    for k in range(1, len(taps)):
        y = y + taps[k] * w[k:k + 1]
    return y


def _rms(x, width):
    r = lax.rsqrt(jnp.sum(x * x, axis=-1, keepdims=True) * (1.0 / width) + EPS)
    return x * r, r


def _rms_bwd(xh, r, dxh, width):
    return r * (dxh - xh * (jnp.sum(dxh * xh, axis=-1, keepdims=True) * (1.0 / width)))


def _seq_flags(i, seq, tm):
    nps = seq // tm
    pos = i % nps
    return jnp.where(pos == 0, 0.0, 1.0), jnp.where(pos == nps - 1, 0.0, 1.0)


def _norm_cast(name, x, w, tm):
    t, d = x.shape

    def body(i, x_ref, w_ref, h_ref):
        xh, _ = _rms(x_ref[...], d)
        h_ref[...] = (xh * w_ref[...]).astype(BF16)

    return _rowwise(name, body, t, tm, [("row", x, d, 0), ("full", w)], [(d, BF16)])[0]


def _gdn_prep(proj, cw, gp, seq, tm):
    t = proj.shape[0]
    d = D_MODEL

    def body(i, q_ref, qh_ref, k_ref, kh_ref, v_ref, vh_ref, sm_ref, cw_ref, gp_ref, qn_ref, kn_ref, vv_ref, gs_ref, ypre_ref):
        keep, _ = _seq_flags(i, seq, tm)
        for x_ref, h_ref, o_ref, off, scale in ((q_ref, qh_ref, qn_ref, 0, GDN_DK ** -0.5),
                                               (k_ref, kh_ref, kn_ref, d, 1.0), (v_ref, vh_ref, vv_ref, 2 * d, None)):
            y = _conv(_conv_taps(x_ref[...], h_ref[...] * keep, 4), cw_ref[:, off:off + d])
            ypre_ref[:, off:off + d] = y
            a = y * _sigmoid(y)
            if scale is None:
                o_ref[...] = a
            else:
                for hh in range(GDN_HEADS):
                    s = a[:, hh * GDN_DK:(hh + 1) * GDN_DK]
                    n = lax.rsqrt(_rowsum(s * s) + EPS)
                    o_ref[:, hh * GDN_DK:(hh + 1) * GDN_DK] = s * (n * scale)
        sm = sm_ref[...]
        lane = lax.broadcasted_iota(jnp.int32, sm.shape, 1)
        beta = _sigmoid(sm)
        g = jnp.where((lane >= 8) & (lane < 16), -jnp.exp(gp_ref[0:1, :]) * _softplus(sm + gp_ref[1:2, :]), 0.0)
        gs_ref[...] = jnp.where(lane < 8, beta, _mmx(_block_tri(tm, False), g))

    ins = []
    for cb in range(3):
        ins += [("row", proj, d, cb), ("prev", proj, d, cb)]
    ins += [("row", proj, LANES, SMALL_CB), ("full", cw), ("full", gp)]
    return _rowwise("gdn_prep", body, t, tm, ins, [(d, F32), (d, F32), (d, F32), (LANES, F32), (3 * d, F32)])


def _block_tri(tm, upper):
    ri = lax.broadcasted_iota(jnp.int32, (tm, tm), 0)
    ci = lax.broadcasted_iota(jnp.int32, (tm, tm), 1)
    tri = (ri <= ci) if upper else (ri >= ci)
    return (tri & ((ri // CHUNK) == (ci // CHUNK))).astype(F32)


def _chunk_consts():
    row = lax.broadcasted_iota(jnp.int32, (CHUNK, CHUNK), 0)
    col = lax.broadcasted_iota(jnp.int32, (CHUNK, CHUNK), 1)
    return dict(
        tril=row >= col, strict=row > col, eye=(row == col).astype(F32),
        lane=lax.broadcasted_iota(jnp.int32, (CHUNK, LANES), 1),
        row1=lax.broadcasted_iota(jnp.int32, (CHUNK, 1), 0),
        ones=jnp.ones((CHUNK, LANES), F32))


def _hmap(fn, *lists):
    return [fn(*a) for a in zip(*lists)]


def _tri_inv(nmats, eye):
    x = [eye - n for n in nmats]
    p = _hmap(_mm3, nmats, nmats)
    for lvl in range(5):
        x = _hmap(lambda xi, pi: xi + _mm3(xi, pi), x, p)
        if lvl < 4:
            p = _hmap(_mm3, p, p)
    return x


def _gdn_gates(gs, gc_row, h, c):
    beta = _rowsum(jnp.where(c["lane"] == h, gs, 0.0))
    gc = _rowsum(jnp.where(c["lane"] == h + 8, gs, 0.0))
    dc = jnp.exp(jnp.where(c["tril"], gc - gc_row, -1e30))
    gl = gc[CHUNK - 1:CHUNK, :]
    return beta, dc, jnp.exp(gc), jnp.exp(gl), jnp.exp(gl - gc)


GDN_HB = GDN_HEADS


def _gdn_specs(seq, sb, hb, backward):
    assert hb == GDN_HEADS
    nsb = seq // sb
    ncb = sb // CHUNK
    order = (lambda j: nsb - 1 - j) if backward else (lambda j: j)
    specs = dict(
        wide=lambda: pl.BlockSpec((1, sb, hb * GDN_DK), lambda b, h, j: (b, order(j), h)),
        gs=lambda: pl.BlockSpec((1, sb, LANES), lambda b, h, j: (b, order(j), 0)),
        gr=pl.BlockSpec((1, ncb, GDN_HEADS, CHUNK), lambda b, h, j: (b, order(j), 0, 0)),
        st=pl.BlockSpec((1, hb, ncb * GDN_DK, GDN_DK), lambda b, h, j: (b, h, order(j), 0)),
        ti=pl.BlockSpec((1, hb, sb, CHUNK), lambda b, h, j: (b, h, order(j), 0)))
    return nsb, ncb, specs


def _riding_exchange(arrays, scatter, n_in, n_out, grid):
    n = len(arrays)
    if n == 0:
        return [], [], [], lambda body: body
    any_spec = [pl.BlockSpec(memory_space=pl.ANY)] * n

    def wrap(body):
        def wrapped(*refs):
            ins = refs[n_in:n_in + n]
            outs = refs[n_in + n + n_out:n_in + 2 * n + n_out]
            sems = refs[len(refs) - 3:]
            pid = [pl.program_id(a) for a in range(len(grid))]
            first = functools.reduce(lambda a, b: a & b, [p == 0 for p in pid])
            last = functools.reduce(lambda a, b: a & b, [p == g - 1 for p, g in zip(pid, grid)])

            @pl.when(first)
            def _():
                _exchange_phase(ins, outs, sems, scatter, start=True)

            body(*refs[:n_in], *refs[n_in + n:n_in + n + n_out], *refs[n_in + 2 * n + n_out:len(refs) - 3])

            @pl.when(last)
            def _():
                _exchange_phase(ins, outs, sems, scatter, start=False)

        return wrapped

    return any_spec, _exchange_out_shapes(arrays, scatter), _exchange_sems(n), wrap


def _gdn_chunk_fwd(qn, kn, vv, gs, gr, bsz, seq, sb, riders):
    hb = GDN_HB
    nsb, ncb, sp = _gdn_specs(seq, sb, hb, False)
    grid = (bsz, GDN_HEADS // hb, nsb)
    any_spec, rider_shapes, rider_sems, wrap = _riding_exchange(riders, False, 5, 3, grid)

    def body(q_ref, k_ref, v_ref, gs_ref, gr_ref, o_ref, st_ref, ti_ref, s_scr):
        hg = pl.program_id(1)

        @pl.when(pl.program_id(2) == 0)
        def _():
            s_scr[...] = jnp.zeros_like(s_scr)

        c = _chunk_consts()

        def chunk(n, carry):
            r = pl.ds(pl.multiple_of(n * CHUNK, CHUNK), CHUNK)
            rs = pl.ds(pl.multiple_of(n * GDN_DK, GDN_DK), GDN_DK)
            gsv = gs_ref[0, r, :]
            heads = list(range(hb))
            sls = [slice(ih * GDN_DK, (ih + 1) * GDN_DK) for ih in heads]
            q = [q_ref[0, r, sl] for sl in sls]
            k = [k_ref[0, r, sl] for sl in sls]
            v = [v_ref[0, r, sl] for sl in sls]
            beta, dc, eg, egl, ekd = zip(*[
                _gdn_gates(gsv, gr_ref[0, n, pl.ds(ih, 1), :], ih, c) for ih in heads])
            kb = _hmap(lambda a, b: a * b, k, beta)
            amat = _hmap(lambda a, b, d_: jnp.where(c["strict"], _mm(a, b, NT) * d_, 0.0), kb, k, dc)
            tinv = _tri_inv(amat, c["eye"])
            u = _hmap(lambda t_, a, b: _mm3(t_, a * b), tinv, v, beta)
            w = _hmap(lambda t_, a, b: _mm3(t_, a * b), tinv, kb, eg)
            qk = _hmap(lambda a, b, d_: _mm(a, b, NT) * d_, q, k, dc)
            s = [s_scr[ih] for ih in heads]
            v_new = _hmap(lambda a, b, s_: a - _mm(b, s_), u, w, s)
            o = _hmap(lambda a, e, s_, qk_, vn: _mm(a * e, s_) + _mm(qk_, vn), q, eg, s, qk, v_new)
            s_new = _hmap(lambda s_, e, a, f, vn: s_ * e + _mm(a * f, vn, TN), s, egl, k, ekd, v_new)
            for ih in heads:
                o_ref[0, r, sls[ih]] = o[ih]
                st_ref[0, ih, rs, :] = s[ih]
                ti_ref[0, ih, r, :] = tinv[ih]
                s_scr[ih] = s_new[ih]
            return carry

        lax.fori_loop(0, ncb, chunk, 0)

    t3 = (bsz, seq, D_MODEL)
    res = _pcall(
        wrap(body), name="gdn_chunk_fwd", grid=grid,
        in_specs=[sp["wide"](), sp["wide"](), sp["wide"](), sp["gs"](), sp["gr"]] + any_spec,
        out_specs=[sp["wide"](), sp["st"], sp["ti"]] + any_spec,
        out_shape=[jax.ShapeDtypeStruct(t3, F32),
                   jax.ShapeDtypeStruct((bsz, GDN_HEADS, (seq // CHUNK) * GDN_DK, GDN_DK), F32),
                   jax.ShapeDtypeStruct((bsz, GDN_HEADS, seq, CHUNK), F32)] + rider_shapes,
        scratch_shapes=[pltpu.VMEM((hb, GDN_DK, GDN_DK), F32)] + rider_sems,
        compiler_params=_params(("arbitrary", "arbitrary", "arbitrary")),
    )(qn, kn, vv, gs, gr, *riders)
    return res[:3], res[3:]


def _ssd_prep(proj, cw, cb, sp, seq, tm):
    t = proj.shape[0]
    d = D_MODEL
    ssd_w = SSD_HEADS * SSD_P

    def body(i, x_ref, xh_ref, bc_ref, bch_ref, sm_ref, cw_ref, cb_ref, sp_ref, xs_ref, bco_ref, dtx_ref, acsx_ref, acs_ref, ypre_ref):
        keep, _ = _seq_flags(i, seq, tm)
        y = _conv(_conv_taps(x_ref[...], xh_ref[...] * keep, 4), cw_ref[:, 0:d]) + cb_ref[:, 0:d]
        ypre_ref[:, 0:d] = y
        xs_ref[...] = y * _sigmoid(y)
        y = _conv(_conv_taps(bc_ref[...], bch_ref[...] * keep, 4), cw_ref[:, d:d + 512]) + cb_ref[:, d:d + 512]
        ypre_ref[:, d:d + 512] = y
        bco_ref[...] = y * _sigmoid(y)
        sm = sm_ref[...]
        lane = lax.broadcasted_iota(jnp.int32, sm.shape, 1)
        valid = (lane >= 16) & (lane < 32)
        dt = jnp.where(valid, _softplus(sm + sp_ref[1:2, :]), 0.0)
        adt = dt * (-jnp.exp(sp_ref[0:1, :]))
        acs = _mmx(_block_tri(tm, False), adt)
        l64 = lax.broadcasted_iota(jnp.int32, (LANES, ssd_w), 0)
        d64 = lax.broadcasted_iota(jnp.int32, (LANES, ssd_w), 1)
        e64 = (l64 - 16 == d64 // SSD_P).astype(F32)
        dtx_ref[...] = _mmsel(dt, e64, terms=3)
        acsx_ref[...] = _mmsel(acs, e64, terms=3)
        acs_ref[...] = acs

    ins = [("row", proj, d, 5), ("prev", proj, d, 5), ("row", proj, 512, 12), ("prev", proj, 512, 12),
           ("row", proj, LANES, SMALL_CB), ("full", cw), ("full", cb), ("full", sp)]
    return _rowwise("ssd_prep", body, t, tm, ins,
                    [(d, F32), (512, F32), (ssd_w, F32), (ssd_w, F32), (LANES, F32), (d + 512, F32)])


SSD_GW = SSD_HPG * SSD_P


def _ssd_head(acs, ar_ref, n, head, cbm, c):
    col = _rowsum(jnp.where(c["lane"] == head + 16, acs, 0.0))
    lm = jnp.exp(jnp.where(c["tril"], col - ar_ref[0, head, pl.ds(n, 1), :], -1e30))
    return lm, cbm * lm


def _ssd_specs(seq, sb):
    nsb = seq // sb
    ncb = sb // CHUNK
    def specs(order):
        return dict(
            wide=lambda: pl.BlockSpec((1, sb, SSD_HEADS * SSD_P), lambda b, j: (b, order(j), 0)),
            bc=lambda: pl.BlockSpec((1, sb, 2 * SSD_GROUPS * SSD_N), lambda b, j: (b, order(j), 0)),
            half=lambda: pl.BlockSpec((1, sb, SSD_GROUPS * SSD_N), lambda b, j: (b, order(j), 0)),
            small=lambda: pl.BlockSpec((1, sb, LANES), lambda b, j: (b, order(j), 0)),
            ar=pl.BlockSpec((1, SSD_HEADS, ncb, CHUNK), lambda b, j: (b, 0, order(j), 0)),
            st=pl.BlockSpec((1, ncb * SSD_N, SSD_HEADS * SSD_P), lambda b, j: (b, order(j), 0)))
    return nsb, ncb, specs(lambda j: j), specs(lambda j: nsb - 1 - j)


def _ssd_chunk_fwd(xs, bc, dtx, acsx, acs, ar, bsz, seq, sb):
    nsb, ncb, sp, _ = _ssd_specs(seq, sb)

    def body(x_ref, dtx_ref, ax_ref, bc_ref, acs_ref, ar_ref, y_ref, sts_ref, st_scr):
        @pl.when(pl.program_id(1) == 0)
        def _():
            st_scr[...] = jnp.zeros_like(st_scr)

        c = _chunk_consts()
        lane5 = lax.broadcasted_iota(jnp.int32, (CHUNK, SSD_GW), 1) // SSD_P

        def chunk(n, carry):
            r = pl.ds(pl.multiple_of(n * CHUNK, CHUNK), CHUNK)
            rs = pl.ds(pl.multiple_of(n * SSD_N, SSD_N), SSD_N)
            acsv = acs_ref[0, r, :]
            for g in range(SSD_GROUPS):
                gl = slice(g * SSD_GW, (g + 1) * SSD_GW)
                x, dt, ax = x_ref[0, r, gl], dtx_ref[0, r, gl], ax_ref[0, r, gl]
                bm = bc_ref[0, r, g * SSD_N:(g + 1) * SSD_N]
                cm = bc_ref[0, r, (SSD_GROUPS + g) * SSD_N:(SSD_GROUPS + g + 1) * SSD_N]
                xdt = x * dt
                cbm = _mm(cm, bm, NT)
                al = ax[CHUNK - 1:CHUNK, :]
                st = st_scr[:, gl]
                y = _mm(cm, st) * jnp.exp(ax)
                for hh in range(SSD_HPG):
                    _, gm = _ssd_head(acsv, ar_ref, n, g * SSD_HPG + hh, cbm, c)
                    y = y + _mm(gm, jnp.where(lane5 == hh, xdt, 0.0))
                y_ref[0, r, gl] = y
                sts_ref[0, rs, gl] = st
                st_scr[:, gl] = st * jnp.exp(al) + _mm(bm, xdt * jnp.exp(al - ax), TN)
            return carry

        lax.fori_loop(0, ncb, chunk, 0)

    return _pcall(
        body, name="ssd_chunk_fwd", grid=(bsz, nsb),
        in_specs=[sp["wide"](), sp["wide"](), sp["wide"](), sp["bc"](), sp["small"](), sp["ar"]],
        out_specs=[sp["wide"](), sp["st"]],
        out_shape=[jax.ShapeDtypeStruct((bsz, seq, SSD_HEADS * SSD_P), F32),
                   jax.ShapeDtypeStruct((bsz, (seq // CHUNK) * SSD_N, SSD_HEADS * SSD_P), F32)],
        scratch_shapes=[pltpu.VMEM((SSD_N, SSD_HEADS * SSD_P), F32)],
        compiler_params=_params(("parallel", "arbitrary")),
    )(xs, dtx, acsx, bc, acs, ar)


def _gate_norm(o_gdn, y_ssd, xs, proj, gnw, snw, dvec, tm):
    t = o_gdn.shape[0]
    d = D_MODEL

    def body(i, o_ref, za_ref, y_ref, xs_ref, zs_ref, gnw_ref, snw_ref, dv_ref, out_ref):
        for hh in range(GDN_HEADS):
            sl = slice(hh * GDN_DK, (hh + 1) * GDN_DK)
            oh, _ = _rms(o_ref[:, sl], GDN_DK)
            z = za_ref[:, sl]
            out_ref[:, sl] = (oh * gnw_ref[...] * (z * _sigmoid(z))).astype(BF16)
        zs = zs_ref[...]
        yg = (y_ref[...] + dv_ref[...] * xs_ref[...]) * (zs * _sigmoid(zs))
        for g in range(SSD_GROUPS):
            sl = slice(g * 512, (g + 1) * 512)
            yh, _ = _rms(yg[:, sl], 512)
            out_ref[:, d + g * 512:d + (g + 1) * 512] = (yh * snw_ref[:, sl]).astype(BF16)

    ins = [("row", o_gdn, d, 0), ("row", proj, d, 3), ("row", y_ssd, d, 0), ("row", xs, d, 0), ("row", proj, d, 4),
           ("full", gnw), ("full", snw), ("full", dvec)]
    return _rowwise("gate_norm", body, t, tm, ins, [(2 * d, BF16)])[0]


def _out_mid(mixin, w_out, x, pmw, pfw):
    d = D_MODEL

    def epilogue(mix, x_ref, pmw_ref, pfw_ref, mix_ref, x1_ref, h2_ref):
        mix_ref[...] = mix
        mh, _ = _rms(mix, d)
        x1 = x_ref[...] + mh * pmw_ref[...]
        x1_ref[...] = x1
        xh, _ = _rms(x1, d)
        h2_ref[...] = (xh * pfw_ref[...]).astype(BF16)

    return _matmul_rows("mm_out_mid", mixin, w_out, "nn", epilogue, [x], [pmw, pfw], [(d, F32), (d, F32), (d, BF16)],
                        tk=2 * d)


def _ffn_act(u_pre, cw, cb, seq, tm):
    t = u_pre.shape[0]

    def body(i, ug_ref, ugh_ref, uu_ref, uuh_ref, cw_ref, cb_ref, act_ref, u_ref):
        keep, _ = _seq_flags(i, seq, tm)
        gate = _conv(_conv_taps(ug_ref[...], ugh_ref[...] * keep, 3), cw_ref[:, 0:D_FF]) + cb_ref[:, 0:D_FF]
        up = _conv(_conv_taps(uu_ref[...], uuh_ref[...] * keep, 3), cw_ref[:, D_FF:2 * D_FF]) + cb_ref[:, D_FF:2 * D_FF]
        u_ref[:, 0:D_FF] = gate
        u_ref[:, D_FF:2 * D_FF] = up
        act_ref[...] = (gate * _sigmoid(gate) * up).astype(BF16)

    ins = [("row", u_pre, D_FF, 0), ("prev", u_pre, D_FF, 0), ("row", u_pre, D_FF, 1), ("prev", u_pre, D_FF, 1),
           ("full", cw), ("full", cb)]
    return _rowwise("ffn_act", body, t, tm, ins, [(D_FF, BF16), (2 * D_FF, F32)])


def _down_final(act, w_down, x1, tgt, w):
    d = D_MODEL

    def epilogue(f, x1_ref, t_ref, w_ref, dy_ref, df_ref, loss_ref, dw_ref):
        fh, r = _rms(f, d)
        e = x1_ref[...] + fh * w_ref[...] - t_ref[...]
        loss_ref[...] += _colsum(e * e) * (0.5 / d)
        dy = e * (1.0 / d)
        dy_ref[...] = dy
        dw_ref[...] += _colsum(dy * fh)
        df_ref[...] = _rms_bwd(fh, r, dy * w_ref[...], d).astype(BF16)

    return _matmul_rows("mm_down_final", act, w_down, "nn", epilogue, [x1, tgt], [w], [(d, F32), (d, BF16)],
                        accs=[(1, d), (1, d)], tk=D_FF)


def _ffn_bwd(u, u_pre, dact, cw, seq, tm):
    t = u.shape[0]

    def body(i, g_ref, gn_ref, up_ref, upn_ref, xg_ref, xu_ref, da_ref, dan_ref, cw_ref, dpre_ref, dcw_ref, dcb_ref):
        _, keep_next = _seq_flags(i, seq, tm)
        ext = lambda a_ref, n_ref: jnp.concatenate([a_ref[...], n_ref[...]], axis=0)
        rows = tm + SUBLANES
        gate, up = ext(g_ref, gn_ref), ext(up_ref, upn_ref)
        sg = _sigmoid(gate)
        da = jnp.concatenate([da_ref[...], dan_ref[...] * keep_next], axis=0)
        for off, grad, x_ref in ((0, da * up * _dsilu(gate, sg), xg_ref), (D_FF, da * gate * sg, xu_ref)):
            x = x_ref[...]
            own = grad[0:tm]
            acc = own * cw_ref[2:3, off:off + D_FF]
            dcb_ref[:, off:off + D_FF] += _colsum(own)
            dcw_ref[2:3, off:off + D_FF] += _colsum(own * x)
            for j in (1, 2):
                ahead = pltpu.roll(grad, rows - j, 0)[0:tm]
                acc = acc + ahead * cw_ref[2 - j:3 - j, off:off + D_FF]
                dcw_ref[2 - j:3 - j, off:off + D_FF] += _colsum(ahead * x)
            dpre_ref[:, off:off + D_FF] = acc.astype(BF16)

    ins = []
    for cb_ in range(2):
        ins += [("row", u, D_FF, cb_), ("next", u, D_FF, cb_)]
    ins += [("row", u_pre, D_FF, 0), ("row", u_pre, D_FF, 1), ("row", dact, D_FF, 0), ("next", dact, D_FF, 0), ("full", cw)]
    return _rowwise("ffn_bwd", body, t, tm, ins, [(2 * D_FF, BF16)], accs=[(SUBLANES, 2 * D_FF), (1, 2 * D_FF)])


def _assemble_dproj(dpre_qkv, dza, dzs, dpre_xbc, dsm, proj, gcw, scw, seq, tm):
    t = dza.shape[0]
    d = D_MODEL

    def body(i, dq_ref, dqn_ref, dk_ref, dkn_ref, dv_ref, dvn_ref, dx_ref, dxn_ref, dbc_ref, dbcn_ref, dza_ref, dzs_ref,
             dsm_ref, xq_ref, xk_ref, xv_ref, xx_ref, xbc_ref, gcw_ref, scw_ref, o_ref, dgcw_ref, dscw_ref):
        _, keep = _seq_flags(i, seq, tm)
        pieces = [(g_ref, n_ref, gcw_ref, dgcw_ref, x_ref, 0, c0) for g_ref, n_ref, x_ref, c0 in (
            (dq_ref, dqn_ref, xq_ref, 0), (dk_ref, dkn_ref, xk_ref, d), (dv_ref, dvn_ref, xv_ref, 2 * d))]
        pieces += [(g_ref, n_ref, scw_ref, dscw_ref, x_ref, 5 * d, c0) for g_ref, n_ref, x_ref, c0 in (
            (dx_ref, dxn_ref, xx_ref, 0), (dbc_ref, dbcn_ref, xbc_ref, d))]
        for d_ref, n_ref, cw_ref, dcw_ref, x_ref, base, c0 in pieces:
            w = x_ref.shape[1]
            x = x_ref[...]
            g = d_ref[...]
            halo = n_ref[...] * keep
            acc = g * cw_ref[3:4, c0:c0 + w]
            dcw_ref[3:4, c0:c0 + w] += _colsum(g * x)
            for j in range(1, 4):
                ahead = _shift_up(g, halo, j)
                acc = acc + ahead * cw_ref[3 - j:4 - j, c0:c0 + w]
                dcw_ref[3 - j:4 - j, c0:c0 + w] += _colsum(ahead * x)
            o_ref[:, base + c0:base + c0 + w] = acc.astype(BF16)
        o_ref[:, 3 * d:4 * d] = dza_ref[...]
        o_ref[:, 4 * d:5 * d] = dzs_ref[...]
        o_ref[:, 6 * d + 512:6 * d + 512 + LANES] = dsm_ref[...]
        o_ref[:, 6 * d + 512 + LANES:PROJ_W] = jnp.zeros((tm, PROJ_W - (6 * d + 512 + LANES)), BF16)

    ins = []
    for g in tuple(dpre_qkv) + tuple(dpre_xbc):
        ins += [("row", g, g.shape[1], 0), ("next", g, g.shape[1], 0)]
    ins += [("row", dza, d, 0), ("row", dzs, d, 0), ("row", dsm, LANES, 0),
           ("row", proj, d, 0), ("row", proj, d, 1), ("row", proj, d, 2), ("row", proj, d, 5), ("row", proj, 512, 12),
           ("full", gcw), ("full", scw)]
    return _rowwise("assemble_dproj", body, t, tm, ins, [(PROJ_W, BF16)], accs=[(SUBLANES, 3 * d), (SUBLANES, d + 512)])


def _dh2_mid_bwd(du_pre, w_up, x1, mix, dy, pmw, pfw):
    d = D_MODEL

    def epilogue(dh2, x1_ref, mix_ref, dy_ref, pmw_ref, pfw_ref, dx1_ref, dmix_ref, dpm_ref, dpf_ref):
        xh, r2 = _rms(x1_ref[...], d)
        dpf_ref[...] += _colsum(dh2 * xh)
        dx1 = dy_ref[...] + _rms_bwd(xh, r2, dh2 * pfw_ref[...], d)
        dx1_ref[...] = dx1
        mh, r = _rms(mix_ref[...], d)
        dpm_ref[...] += _colsum(dx1 * mh)
        dmix_ref[...] = _rms_bwd(mh, r, dx1 * pmw_ref[...], d).astype(BF16)

    return _matmul_rows("mm_dh2_mid_bwd", du_pre, w_up, "nt", epilogue, [x1, mix, dy], [pmw, pfw],
                        [(d, F32), (d, BF16)], accs=[(1, d), (1, d)], tk=D_FF)


def _dmixin_gate_norm_bwd(dmix, w_out, o_gdn, y_ssd, xs, proj, gnw, snw, dvec):
    d = D_MODEL

    def epilogue(dmixin, o_ref, za_ref, y_ref, xs_ref, zs_ref, gnw_ref, snw_ref, dv_ref,
                 do_ref, dza_ref, dy_ref, dxs_ref, dzs_ref, dgnw_ref, dsnw_ref, dd_ref):
        for hh in range(GDN_HEADS):
            sl = slice(hh * GDN_DK, (hh + 1) * GDN_DK)
            oh, r = _rms(o_ref[:, sl], GDN_DK)
            z = za_ref[:, sl]
            sz = _sigmoid(z)
            dm = dmixin[:, sl]
            don = dm * (z * sz)
            dza_ref[:, sl] = (dm * oh * gnw_ref[...] * _dsilu(z, sz)).astype(BF16)
            dgnw_ref[...] += _colsum(don * oh)
            do_ref[:, sl] = _rms_bwd(oh, r, don * gnw_ref[...], GDN_DK)
        zs = zs_ref[...]
        sz = _sigmoid(zs)
        sil = zs * sz
        x = xs_ref[...]
        y0 = y_ref[...] + dv_ref[...] * x
        yg = y0 * sil
        dms = dmixin[:, d:2 * d]
        for g in range(SSD_GROUPS):
            sl = slice(g * 512, (g + 1) * 512)
            yh, r = _rms(yg[:, sl], 512)
            dsnw_ref[:, sl] += _colsum(dms[:, sl] * yh)
            dyg = _rms_bwd(yh, r, dms[:, sl] * snw_ref[:, sl], 512)
            dy0 = dyg * sil[:, sl]
            dzs_ref[:, sl] = (dyg * y0[:, sl] * _dsilu(zs[:, sl], sz[:, sl])).astype(BF16)
            dy_ref[:, sl] = dy0
            dxs_ref[:, sl] = dy0 * dv_ref[:, sl]
            dd_ref[:, sl] += _colsum(dy0 * x[:, sl])

    row_ins = [o_gdn, (proj, d, 3), y_ssd, xs, (proj, d, 4)]
    return _matmul_rows("mm_dmixin_gate_norm_bwd", dmix, w_out, "nt", epilogue, row_ins, [gnw, snw, dvec],
                        [(d, F32), (d, BF16), (d, F32), (d, F32), (d, BF16)], accs=[(1, GDN_DK), (1, d), (1, d)], tm=256)


def _ssd_chunk_bwd(xs, bc, dtx, acsx, acs, ar, dy, sts, ypre, dxs_d, bsz, seq, sb):
    nsb, ncb, _, sp = _ssd_specs(seq, sb)
    bc_w = 2 * SSD_GROUPS * SSD_N
    x_w = SSD_HEADS * SSD_P

    def body(x_ref, dtx_ref, ax_ref, bc_ref, acs_ref, ar_ref, dy_ref, sts_ref, yx_ref, ybc_ref, dxd_ref,
             dx_ref, dbc_ref, ddt_ref, dacs_ref, dbx_ref, dbbc_ref, dst_scr):
        @pl.when(pl.program_id(1) == 0)
        def _():
            dst_scr[...] = jnp.zeros_like(dst_scr)

        @pl.when((pl.program_id(0) == 0) & (pl.program_id(1) == 0))
        def _():
            dbx_ref[...] = jnp.zeros_like(dbx_ref)
            dbbc_ref[...] = jnp.zeros_like(dbbc_ref)

        def to_conv_out(grad, y):
            return grad * _dsilu(y, _sigmoid(y))

        c = _chunk_consts()
        lane5 = lax.broadcasted_iota(jnp.int32, (CHUNK, SSD_GW), 1) // SSD_P
        row5 = lax.broadcasted_iota(jnp.int32, (CHUNK, SSD_GW), 0)
        sel_in = lax.broadcasted_iota(jnp.int32, (SSD_GW, LANES), 0) // SSD_P
        sel_out = lax.broadcasted_iota(jnp.int32, (SSD_GW, LANES), 1)

        def chunk(nn, carry):
            n = ncb - 1 - nn
            r = pl.ds(pl.multiple_of(n * CHUNK, CHUNK), CHUNK)
            rs = pl.ds(pl.multiple_of(n * SSD_N, SSD_N), SSD_N)
            acsv = acs_ref[0, r, :]
            ddt = jnp.zeros((CHUNK, LANES), F32)
            dacs = jnp.zeros((CHUNK, LANES), F32)
            for g in range(SSD_GROUPS):
                gl = slice(g * SSD_GW, (g + 1) * SSD_GW)
                x, dt, ax, dyv = x_ref[0, r, gl], dtx_ref[0, r, gl], ax_ref[0, r, gl], dy_ref[0, r, gl]
                bm = bc_ref[0, r, g * SSD_N:(g + 1) * SSD_N]
                cm = bc_ref[0, r, (SSD_GROUPS + g) * SSD_N:(SSD_GROUPS + g + 1) * SSD_N]
                st = sts_ref[0, rs, gl]
                dst = dst_scr[:, gl]
                rsel = (sel_in + (16 + g * SSD_HPG) == sel_out).astype(F32)
                xdt = x * dt
                cbm = _mm(cm, bm, NT)
                al = ax[CHUNK - 1:CHUNK, :]
                ex, el = jnp.exp(ax), jnp.exp(al)
                dec = jnp.exp(al - ax)
                xd = xdt * dec
                dye = dyv * ex
                dxd = _mm(bm, dst)
                dxdt = dec * dxd
                dcm = _mm(dye, st, NT)
                dbm = _mm(xd, dst, NT)
                z = dye * _mm(cm, st) - dxd * xd
                zl = _colsum(dst * st) * el + _colsum(dxd * xd)
                z = z + jnp.where(row5 == CHUNK - 1, zl, 0.0)
                dcb = jnp.zeros((CHUNK, CHUNK), F32)
                for hh in range(SSD_HPG):
                    head = g * SSD_HPG + hh
                    lm, gm = _ssd_head(acsv, ar_ref, n, head, cbm, c)
                    dym = jnp.where(lane5 == hh, dyv, 0.0)
                    dxdt = dxdt + _mm(gm, dym, TN)
                    dg = _mm(dym, xdt, NT)
                    dcb = dcb + dg * lm
                    pm = dg * gm
                    dacs = dacs + jnp.where(c["lane"] == head + 16, _rowsum(pm) - _mmsel(pm, c["ones"], TN), 0.0)
                for sl, grad in ((slice((SSD_GROUPS + g) * SSD_N, (SSD_GROUPS + g + 1) * SSD_N), dcm + _mm(dcb, bm)),
                                 (slice(g * SSD_N, (g + 1) * SSD_N), dbm + _mm(dcb, cm, TN))):
                    dpre = to_conv_out(grad, ybc_ref[0, r, sl])
                    dbc_ref[0, r, sl] = dpre
                    dbbc_ref[:, sl] += _colsum(dpre)
                dacs = dacs + _mmsel(z, rsel)
                ddt = ddt + _mmsel(dxdt * x, rsel)
                dpre = to_conv_out(dxdt * dt + dxd_ref[0, r, gl], yx_ref[0, r, gl])
                dx_ref[0, r, gl] = dpre
                dbx_ref[:, gl] += _colsum(dpre)
                dst_scr[:, gl] = dst * el + _mm(cm, dye, TN)
            ddt_ref[0, r, :] = ddt
            dacs_ref[0, r, :] = dacs
            return carry

        lax.fori_loop(0, ncb, chunk, 0)

    nsb_rev = lambda j: nsb - 1 - j
    return _pcall(
        body, name="ssd_chunk_bwd", grid=(bsz, nsb),
        in_specs=[sp["wide"](), sp["wide"](), sp["wide"](), sp["bc"](), sp["small"](), sp["ar"], sp["wide"](), sp["st"],
                  sp["wide"](), pl.BlockSpec((1, sb, bc_w), lambda b, j: (b, nsb_rev(j), x_w // bc_w)), sp["wide"]()],
        out_specs=[sp["wide"](), sp["bc"](), sp["small"](), sp["small"](),
                   pl.BlockSpec((1, x_w), lambda b, j: (0, 0)), pl.BlockSpec((1, bc_w), lambda b, j: (0, 0))],
        out_shape=[jax.ShapeDtypeStruct((bsz, seq, x_w), F32), jax.ShapeDtypeStruct((bsz, seq, bc_w), F32),
                   jax.ShapeDtypeStruct((bsz, seq, LANES), F32), jax.ShapeDtypeStruct((bsz, seq, LANES), F32),
                   jax.ShapeDtypeStruct((1, x_w), F32), jax.ShapeDtypeStruct((1, bc_w), F32)],
        scratch_shapes=[pltpu.VMEM((SSD_N, x_w), F32)],
        compiler_params=_params(("arbitrary", "arbitrary")),
    )(xs, dtx, acsx, bc, acs, ar, dy, sts, ypre, ypre, dxs_d)


def _ssd_gates_bwd(proj, ddt, dacs, dsm_gdn, sp, tm):
    t = proj.shape[0]

    def body(i, sm_ref, ddt_ref, dacs_ref, dsg_ref, sp_ref, dsm_ref, dsp_ref):
        sm = sm_ref[...]
        lane = lax.broadcasted_iota(jnp.int32, sm.shape, 1)
        valid = (lane >= 16) & (lane < 32)
        xb = sm + sp_ref[1:2, :]
        dt = jnp.where(valid, _softplus(xb), 0.0)
        a_neg = -jnp.exp(sp_ref[0:1, :])
        dadt_s = _mmx(_block_tri(tm, True), dacs_ref[...])
        dxb = jnp.where(valid, (ddt_ref[...] + dadt_s * a_neg) * _sigmoid(xb), 0.0)
        dsm_ref[...] = (dsg_ref[...] + dxb).astype(BF16)
        dsp_ref[1:2, :] += _colsum(dxb)
        dsp_ref[0:1, :] += jnp.where(valid[0:1, :], _colsum(dadt_s * dt) * a_neg, 0.0)

    ins = [("row", proj, LANES, SMALL_CB), ("row", ddt, LANES, 0), ("row", dacs, LANES, 0), ("row", dsm_gdn, LANES, 0),
           ("full", sp)]
    return _rowwise("ssd_gates_bwd", body, t, tm, ins, [(LANES, BF16)], accs=[(SUBLANES, LANES)])


def _through_norm_silu(g, y, scale):
    sy = _sigmoid(y)
    ds_ = _dsilu(y, sy)
    if scale is None:
        return g * ds_
    a = y * sy
    n = lax.rsqrt(_rowsum(a * a) + EPS)
    ah = a * n
    return (scale * n) * (g - ah * _rowsum(g * ah)) * ds_


def _gdn_chunk_bwd(qn, kn, vv, gs, gr, do, sts, tis, ypre, bsz, seq, sb, riders):
    hb = GDN_HB
    nsb, ncb, sp = _gdn_specs(seq, sb, hb, True)
    grid = (bsz, GDN_HEADS // hb, nsb)
    any_spec, rider_shapes, rider_sems, wrap = _riding_exchange(riders, True, 11, 4, grid)
    ypre_spec = lambda cb: pl.BlockSpec((1, sb, hb * GDN_DK), lambda b, h, j: (b, nsb - 1 - j, cb))

    def body(q_ref, k_ref, v_ref, gs_ref, gr_ref, do_ref, st_ref, ti_ref, yq_ref, yk_ref, yv_ref,
             dq_ref, dk_ref, dv_ref, dgs_ref, ds_scr):
        @pl.when(pl.program_id(2) == 0)
        def _():
            ds_scr[...] = jnp.zeros_like(ds_scr)

        c = _chunk_consts()

        def chunk(nn, carry):
            n = ncb - 1 - nn
            r = pl.ds(pl.multiple_of(n * CHUNK, CHUNK), CHUNK)
            rs = pl.ds(pl.multiple_of(n * GDN_DK, GDN_DK), GDN_DK)
            gsv = gs_ref[0, r, :]
            heads = list(range(hb))
            sls = [slice(ih * GDN_DK, (ih + 1) * GDN_DK) for ih in heads]
            q = [q_ref[0, r, sl] for sl in sls]
            k = [k_ref[0, r, sl] for sl in sls]
            v = [v_ref[0, r, sl] for sl in sls]
            do_ = [do_ref[0, r, sl] for sl in sls]
            s = [st_ref[0, ih, rs, :] for ih in heads]
            tinv = [ti_ref[0, ih, r, :] for ih in heads]
            dsn = [ds_scr[ih] for ih in heads]
            beta, dc, eg, egl, ekd = zip(*[
                _gdn_gates(gsv, gr_ref[0, n, pl.ds(ih, 1), :], ih, c) for ih in heads])
            mul = lambda a, b: a * b
            kb = _hmap(mul, k, beta)
            rhs_w = _hmap(mul, kb, eg)
            u = _hmap(lambda t_, a, b: _mm3(t_, a * b), tinv, v, beta)
            w = _hmap(_mm3, tinv, rhs_w)
            amat = _hmap(lambda a, b, d_: jnp.where(c["strict"], _mm(a, b, NT) * d_, 0.0), kb, k, dc)
            qk = _hmap(lambda a, b, d_: _mm(a, b, NT) * d_, q, k, dc)
            qd = _hmap(mul, q, eg)
            kd = _hmap(mul, k, ekd)
            v_new = _hmap(lambda a, b, s_: a - _mm(b, s_), u, w, s)
            dv_new = _hmap(lambda qk_, d_, kd_, dn: _mm(qk_, d_, TN) + _mm(kd_, dn), qk, do_, kd, dsn)
            dqk = _hmap(lambda d_, vn: _mm(d_, vn, NT), do_, v_new)
            dqd = _hmap(lambda d_, s_: _mm(d_, s_, NT), do_, s)
            ds_new = _hmap(lambda qd_, d_, dn, e, w_, dvn: _mm(qd_, d_, TN) + dn * e - _mm(w_, dvn, TN),
                           qd, do_, dsn, egl, w, dv_new)
            dkd = _hmap(lambda vn, dn: _mm(vn, dn, NT), v_new, dsn)
            dgl = _hmap(lambda s_, dn, e: _colsum(_rowsum(s_ * dn)) * e, s, dsn, egl)
            dw = _hmap(lambda dvn, s_: -_mm(dvn, s_, NT), dv_new, s)
            dru = _hmap(lambda t_, a: _mm3(t_, a, TN), tinv, dv_new)
            drw = _hmap(lambda t_, a: _mm3(t_, a, TN), tinv, dw)
            da = _hmap(lambda a, u_, b, w_: jnp.where(c["strict"], -(_mm(a, u_, NT) + _mm(b, w_, NT)), 0.0), dru, u, drw, w)
            m = _hmap(mul, da, dc)
            dkb = _hmap(lambda a, e, m_, k_: a * e + _mm(m_, k_), drw, eg, m, k)
            mq = _hmap(mul, dqk, dc)
            dq = _hmap(lambda mq_, k_, a, e: _mm(mq_, k_) + a * e, mq, k, dqd, eg)
            dk = _hmap(lambda m_, kb_, mq_, q_, a, e, b, be: _mm(m_, kb_, TN) + _mm(mq_, q_, TN) + a * e + b * be,
                       m, kb, mq, q, dkd, ekd, dkb, beta)
            dbeta = _hmap(lambda a, v_, b, k_: _rowsum(a * v_) + _rowsum(b * k_), dru, v, dkb, k)
            pq = _hmap(lambda a, am, b, qk_: a * am + b * qk_, da, amat, dqk, qk)
            ekk = _hmap(lambda a, b: _rowsum(a * b), dkd, kd)
            dgc = _hmap(lambda pq_, a, rw, b, qd_, e, gl_: (
                _rowsum(pq_) - _mmsel(pq_, c["ones"], TN) + (_rowsum(a * rw) + _rowsum(b * qd_) - e)
                + jnp.where(c["row1"] == CHUNK - 1, _colsum(e) + gl_, 0.0)), pq, drw, rhs_w, dqd, qd, ekk, dgl)
            dv = _hmap(mul, dru, beta)
            dyq = _hmap(lambda g_, sl: _through_norm_silu(g_, yq_ref[0, r, sl], GDN_DK ** -0.5), dq, sls)
            dyk = _hmap(lambda g_, sl: _through_norm_silu(g_, yk_ref[0, r, sl], 1.0), dk, sls)
            dyv = _hmap(lambda g_, sl: _through_norm_silu(g_, yv_ref[0, r, sl], None), dv, sls)
            dgs = jnp.zeros((CHUNK, LANES), F32)
            for ih in heads:
                ds_scr[ih] = ds_new[ih]
                dq_ref[0, r, sls[ih]] = dyq[ih]
                dk_ref[0, r, sls[ih]] = dyk[ih]
                dv_ref[0, r, sls[ih]] = dyv[ih]
                dgs = dgs + jnp.where(c["lane"] == ih, dbeta[ih], jnp.where(c["lane"] == ih + 8, dgc[ih], 0.0))
            dgs_ref[0, r, :] = dgs
            return carry

        lax.fori_loop(0, ncb, chunk, 0)

    res = _pcall(
        wrap(body), name="gdn_chunk_bwd", grid=grid,
        in_specs=[sp["wide"](), sp["wide"](), sp["wide"](), sp["gs"](), sp["gr"], sp["wide"](), sp["st"], sp["ti"],
                  ypre_spec(0), ypre_spec(1), ypre_spec(2)] + any_spec,
        out_specs=[sp["wide"](), sp["wide"](), sp["wide"](), sp["gs"]()] + any_spec,
        out_shape=[jax.ShapeDtypeStruct((bsz, seq, D_MODEL), F32)] * 3 + [jax.ShapeDtypeStruct((bsz, seq, LANES), F32)]
        + rider_shapes,
        scratch_shapes=[pltpu.VMEM((hb, GDN_DK, GDN_DK), F32)] + rider_sems,
        compiler_params=_params(("arbitrary", "arbitrary", "arbitrary")),
    )(qn, kn, vv, gs, gr, do, sts, tis, ypre, ypre, ypre, *riders)
    return res[:4], res[4:]


def _gdn_gates_bwd(proj, dgs, gp, tm):
    t = proj.shape[0]

    def body(i, sm_ref, dgs_ref, gp_ref, dsm_ref, dgp_ref):
        sm = sm_ref[...]
        lane = lax.broadcasted_iota(jnp.int32, sm.shape, 1)
        dsel = dgs_ref[...]
        is_g = (lane >= 8) & (lane < 16)
        dsel = jnp.where(is_g, _mmx(_block_tri(tm, True), dsel), dsel)
        beta = _sigmoid(sm)
        xb = sm + gp_ref[1:2, :]
        a_neg = -jnp.exp(gp_ref[0:1, :])
        sp = _softplus(xb)
        dxb = jnp.where(is_g, dsel * a_neg * _sigmoid(xb), 0.0)
        dsm_ref[...] = jnp.where(lane < 8, dsel * beta * (1.0 - beta), dxb)
        dgp_ref[1:2, :] += _colsum(dxb)
        dgp_ref[0:1, :] += _colsum(jnp.where(is_g, dsel * a_neg * sp, 0.0))

    ins = [("row", proj, LANES, SMALL_CB), ("row", dgs, LANES, 0), ("full", gp)]
    return _rowwise("gdn_gates_bwd", body, t, tm, ins, [(LANES, F32)], accs=[(SUBLANES, LANES)])


def _dh1_first_bwd(dproj, wp_in, x, dx1, w, scatter_riders):
    d = D_MODEL

    def epilogue(dh, x_ref, dx1_ref, w_ref, dx_ref, dw_ref):
        xh, r = _rms(x_ref[...], d)
        dw_ref[...] += _colsum(dh * xh)
        dx_ref[...] = dx1_ref[...] + _rms_bwd(xh, r, dh * w_ref[...], d)

    return _matmul_rows("mm_dh1_first_bwd", dproj, wp_in, "nt", epilogue, [x, dx1], [w], [(d, F32)], accs=[(1, d)],
                        tk=PROJ_W // 2, scatter_riders=scatter_riders)


def _gather_two_level(name, arrays):
    n = len(arrays)
    n_sem = 7

    def body(*refs):
        ins, outs = refs[:n], refs[n:2 * n]
        send_sems, recv_sems, loc_sems = refs[2 * n:]
        x, y, c = lax.axis_index("x"), lax.axis_index("y"), lax.axis_index("c")
        slot = lambda px, py, pc: 4 * px + 2 * py + pc
        sibling = (x, y, 1 - c)
        chips = [(1 - x, y), (x, 1 - y), (1 - x, 1 - y)]

        def copy(t, k, src, block, to):
            return pltpu.make_async_remote_copy(
                src_ref=src, dst_ref=outs[t].at[block], send_sem=send_sems.at[t, k], recv_sem=recv_sems.at[t, k],
                device_id=to, device_id_type=pl.DeviceIdType.MESH)

        own, first, passed = [], [], []
        for t in range(n):
            own.append(pltpu.make_async_copy(ins[t], outs[t].at[slot(x, y, c)], loc_sems.at[t]))
            first.append(copy(t, 0, ins[t], slot(x, y, c), sibling))
            first += [copy(t, 1 + j, ins[t], slot(x, y, c), (px, py, c)) for j, (px, py) in enumerate(chips)]
        for cp in own + first:
            cp.start()
        for t in range(n):
            for j, (px, py) in enumerate(chips):
                copy(t, 1 + j, ins[t], slot(px, py, c), (px, py, c)).wait_recv()
                fwd = copy(t, 4 + j, outs[t].at[slot(px, py, c)], slot(px, py, c), sibling)
                fwd.start()
                passed.append(fwd)
        for t in range(n):
            copy(t, 0, ins[t], slot(x, y, 1 - c), sibling).wait_recv()
            for j, (px, py) in enumerate(chips):
                copy(t, 4 + j, ins[t], slot(px, py, 1 - c), sibling).wait_recv()
        for cp in first + passed:
            cp.wait_send()
        for cp in own:
            cp.wait()

    return _pcall(
        body, name=name,
        in_specs=[pl.BlockSpec(memory_space=pl.ANY)] * n,
        out_specs=[pl.BlockSpec(memory_space=pl.ANY)] * n,
        out_shape=_exchange_out_shapes(arrays, False),
        scratch_shapes=[pltpu.SemaphoreType.DMA((n, n_sem)), pltpu.SemaphoreType.DMA((n, n_sem)), pltpu.SemaphoreType.DMA((n,))],
    )(*arrays)


def _exchange_out_shapes(arrays, scatter):
    return [jax.ShapeDtypeStruct(a.shape if scatter else (N_DEV,) + a.shape, a.dtype) for a in arrays]


def _exchange_sems(n):
    return [pltpu.SemaphoreType.DMA((n, N_DEV - 1)), pltpu.SemaphoreType.DMA((n, N_DEV - 1)), pltpu.SemaphoreType.DMA((n,))]


def _exchange_phase(ins, outs, sems, scatter, start):
    send_sems, recv_sems, loc_sems = sems
    x, y, c = lax.axis_index("x"), lax.axis_index("y"), lax.axis_index("c")
    me = 4 * x + 2 * y + c
    for t in range(len(ins)):
        loc = pltpu.make_async_copy(ins[t].at[me] if scatter else ins[t], outs[t].at[me], loc_sems.at[t])
        if start:
            loc.start()
        else:
            loc.wait()
        for k in range(N_DEV - 1):
            bx, by, bc = ((k + 1) >> 2) & 1, ((k + 1) >> 1) & 1, (k + 1) & 1
            px = 1 - x if bx else x
            py = 1 - y if by else y
            pc = 1 - c if bc else c
            peer = 4 * px + 2 * py + pc
            src = ins[t].at[peer] if scatter else ins[t]
            copy = lambda dst: pltpu.make_async_remote_copy(
                src_ref=src, dst_ref=dst, send_sem=send_sems.at[t, k], recv_sem=recv_sems.at[t, k],
                device_id=(px, py, pc), device_id_type=pl.DeviceIdType.MESH)
            if start:
                copy(outs[t].at[me]).start()
            else:
                copy(outs[t].at[me]).wait_send()
                copy(outs[t].at[peer]).wait_recv()


def _adam_math(w, g, m, v):
    m = ADAM_B1 * m + (1.0 - ADAM_B1) * g
    v = ADAM_B2 * v + (1.0 - ADAM_B2) * (g * g)
    m_hat = m / (1.0 - ADAM_B1 ** ADAM_STEP)
    v_hat = v / (1.0 - ADAM_B2 ** ADAM_STEP)
    delta = -ADAM_LR * (m_hat / (jnp.sqrt(v_hat) + ADAM_EPS) + ADAM_WD * w)
    return delta, m, v


def _adam_big(name, parts, w, m, v, tm):
    r, c = w.shape
    tm = tm if r % tm == 0 else r

    def body(p_ref, w_ref, m_ref, v_ref, g_ref, d_ref, nm_ref, nv_ref):
        g = p_ref[0].astype(F32)
        for s in range(1, N_DEV):
            g = g + p_ref[s].astype(F32)
        g_ref[...] = g
        d_ref[...], nm_ref[...], nv_ref[...] = _adam_math(w_ref[...], g, m_ref[...], v_ref[...])

    blk = lambda: pl.BlockSpec((tm, c), lambda i: (i, 0))
    return _pcall(
        body, name=name, grid=(r // tm,),
        in_specs=[pl.BlockSpec((N_DEV, tm, c), lambda i: (0, i, 0)), blk(), blk(), blk()],
        out_specs=[blk(), blk(), blk(), blk()],
        out_shape=[jax.ShapeDtypeStruct((r, c), F32)] * 4,
        compiler_params=_params(("parallel",)),
    )(parts, w, m, v)


SMALL_ROWS = 56
ROW_DD, ROW_LOSS = 5, 6


def _small_sum(gathered):
    def body(g_ref, o_ref, x_ref):
        s = g_ref[0]
        for dev in range(1, N_DEV):
            s = s + g_ref[dev]
        o_ref[...] = s
        ri = lax.broadcasted_iota(jnp.int32, (D_MODEL, LANES), 0)
        ro = lax.broadcasted_iota(jnp.int32, (D_MODEL, LANES), 1)
        heads = _mmx(jnp.broadcast_to(s[ROW_DD:ROW_DD + 1, :], (SUBLANES, D_MODEL)), (ri // SSD_P == ro).astype(F32))
        loss = _rowsum(jnp.broadcast_to(s[ROW_LOSS:ROW_LOSS + 1, :], (SUBLANES, D_MODEL)))
        row = lax.broadcasted_iota(jnp.int32, (SUBLANES, LANES), 0)
        x_ref[...] = jnp.where(row == 0, heads, jnp.broadcast_to(loss, (SUBLANES, LANES)))

    return _pcall(
        body, name="small_sum",
        out_shape=[jax.ShapeDtypeStruct((SMALL_ROWS, D_MODEL), F32), jax.ShapeDtypeStruct((SUBLANES, LANES), F32)],
        compiler_params=_params(None),
    )(gathered)


def _adam_small(g, w, m, v):
    def body(g_ref, w_ref, m_ref, v_ref, d_ref, nm_ref, nv_ref):
        d_ref[...], nm_ref[...], nv_ref[...] = _adam_math(w_ref[...], g_ref[...], m_ref[...], v_ref[...])

    return _pcall(body, name="adam_small", out_shape=[jax.ShapeDtypeStruct(g.shape, F32)] * 3,
                  compiler_params=_params(None))(g, w, m, v)


def _pack(pieces, rows):
    flat = jnp.concatenate([p.reshape(-1).astype(F32) for p in pieces])
    return jnp.pad(flat, (0, rows * D_MODEL - flat.shape[0])).reshape(rows, D_MODEL)


def _unpack(packed, shapes):
    flat = packed.reshape(-1)
    out, off = [], 0
    for shp in shapes:
        size = 1
        for s in shp:
            size *= s
        out.append(flat[off:off + size].reshape(shp))
        off += size
    return out


def _permute_in(w):
    pad = jnp.zeros((w.shape[0], PROJ_W - D_IN), w.dtype)
    return jnp.concatenate([w[:, 0:4096], w[:, 4112:6672], w[:, 4096:4112], w[:, 6672:6688], pad], axis=1)


def _unpermute_in(g):
    return jnp.concatenate([g[:, 0:4096], g[:, 6656:6672], g[:, 4096:6656], g[:, 6672:6688]], axis=1)


def _lane_row(vec, start):
    return jnp.zeros((LANES,), F32).at[start:start + vec.shape[0]].set(vec)


def _cols_from_shards(g):
    return jnp.transpose(g, (1, 0, 2)).reshape(g.shape[1], N_DEV * g.shape[2])


def _cols_to_shards(a):
    return jnp.transpose(a.astype(BF16).reshape(a.shape[0], N_DEV, a.shape[1] // N_DEV), (1, 0, 2))


def _rows_to_shards(a):
    return a.astype(BF16).reshape(N_DEV, a.shape[0] // N_DEV, a.shape[1])


def _local_step(x, tgt, wp_in, rest, p, rest_is_sharded):
    bsz, seq, d = x.shape
    t = bsz * seq
    x2 = x.reshape(t, d)
    tgt2 = tgt.reshape(t, d)
    tm = min(256, seq)
    tm_wide = min(128, seq)
    sb = min(512, seq)

    gp = jnp.zeros((SUBLANES, LANES), F32).at[0].set(_lane_row(p["gdn_a_log"], 8)).at[1].set(_lane_row(p["gdn_dt_bias"], 8))
    sp = jnp.zeros((SUBLANES, LANES), F32).at[0].set(_lane_row(p["ssd_a_log"], 16)).at[1].set(_lane_row(p["ssd_dt_bias"], 16))
    dvec = jnp.repeat(p["ssd_d"], SSD_P).reshape(1, d)
    row = lambda v: v.reshape(1, -1)
    pre_mix, post_mix, pre_ffn, post_ffn = (row(p[k]) for k in ("pre_mix_norm", "post_mix_norm", "pre_ffn_norm", "post_ffn_norm"))
    gnw, snw = row(p["gdn_norm_w"]), row(p["ssd_norm_w"])
    gcw, scw, scb, fcw, fcb = p["gdn_conv_w"], p["ssd_conv_w"], row(p["ssd_conv_b"]), p["ffn_conv_w"], row(p["ffn_conv_b"])

    h1 = _norm_cast("norm_in", x2, pre_mix, tm)
    proj = _matmul("mm_proj", h1, wp_in, "nn", F32)
    b3 = lambda a: a.reshape(bsz, seq, a.shape[-1])
    b2 = lambda a: a.reshape(t, a.shape[-1])
    rows_of = lambda a, lo, n: jnp.transpose(a[:, lo:lo + n].reshape(bsz, seq // CHUNK, CHUNK, n), (0, 3, 1, 2))
    qn, kn, vv, gs, ypre_gdn = _gdn_prep(proj, gcw, gp, seq, tm)
    qn, kn, vv, gs = b3(qn), b3(kn), b3(vv), b3(gs)
    gr = jnp.transpose(b2(gs)[:, 8:8 + GDN_HEADS].reshape(bsz, seq // CHUNK, CHUNK, GDN_HEADS), (0, 1, 3, 2))
    (o_gdn, gdn_st, gdn_ti), gathered = _gdn_chunk_fwd(qn, kn, vv, gs, gr, bsz, seq, sb, list(rest) if rest_is_sharded else [])
    if rest_is_sharded:
        w_out, w_up, w_down = gathered[0].reshape(-1, d), _cols_from_shards(gathered[1]), gathered[2].reshape(-1, d)
    else:
        w_out, w_up, w_down = rest
    o_gdn = b2(o_gdn)
    xs, bc, dtx, acsx, acs, ypre_ssd = _ssd_prep(proj, scw, scb, sp, seq, tm)
    ar = rows_of(acs, 16, SSD_HEADS)
    y_ssd, ssd_st = _ssd_chunk_fwd(b3(xs), b3(bc), b3(dtx), b3(acsx), b3(acs), ar, bsz, seq, sb)
    y_ssd = b2(y_ssd)
    mixin = _gate_norm(o_gdn, y_ssd, xs, proj, gnw, snw, dvec, tm)
    mix, x1, h2 = _out_mid(mixin, w_out, x2, post_mix, pre_ffn)
    u_pre = _matmul("mm_up", h2, w_up, "nn", F32)
    act, u = _ffn_act(u_pre, fcw, fcb, seq, tm_wide)
    dy, df, loss_lanes, d_post_ffn = _down_final(act, w_down, x1, tgt2, post_ffn)

    g_down = _matmul("mm_dw_down", act, df, "tn", BF16, tm=1408, tk=2048)
    dact = _matmul("mm_dact", df, w_down, "nt", F32, tn=1408)
    du_pre, d_fcw, d_fcb = _ffn_bwd(u, u_pre, dact, fcw, seq, tm_wide)
    g_up = _matmul("mm_dw_up", h2, du_pre, "tn", BF16, tk=2048)
    dx1, dmix, d_post_mix, d_pre_ffn = _dh2_mid_bwd(du_pre, w_up, x1, mix, dy, post_mix, pre_ffn)
    g_out = _matmul("mm_dw_out", mixin, dmix, "tn", BF16, tk=2048)
    do_gdn, dza, dy_ssd, dxs_d, dzs, d_gnw, d_snw, d_dd = _dmixin_gate_norm_bwd(dmix, w_out, o_gdn, y_ssd, xs, proj, gnw, snw, dvec)
    dyx, dybc, ddt, dacs, d_scb_x, d_scb_bc = _ssd_chunk_bwd(
        b3(xs), b3(bc), b3(dtx), b3(acsx), b3(acs), ar, b3(dy_ssd), ssd_st, b3(ypre_ssd), b3(dxs_d), bsz, seq, sb)
    dyx, dybc, ddt, dacs = b2(dyx), b2(dybc), b2(ddt), b2(dacs)
    d_scb = jnp.concatenate([d_scb_x, d_scb_bc], axis=1)
    riders = [_rows_to_shards(g_out), _cols_to_shards(g_up), _rows_to_shards(g_down)] if rest_is_sharded else []
    dgdn, received = _gdn_chunk_bwd(qn, kn, vv, gs, gr, b3(do_gdn), gdn_st, gdn_ti, b3(ypre_gdn), bsz, seq, min(256, seq), riders)
    if rest_is_sharded:
        g_out, g_up, g_down = received
    dyq, dyk, dyv, dgs = (b2(a) for a in dgdn)
    dsm_gdn, d_gp = _gdn_gates_bwd(proj, dgs, gp, tm)
    dsm, d_sp = _ssd_gates_bwd(proj, ddt, dacs, dsm_gdn, sp, tm)
    dproj, d_gcw, d_scw = _assemble_dproj((dyq, dyk, dyv), dza, dzs, (dyx, dybc), dsm, proj, gcw, scw, seq, tm)
    g_in = _matmul("mm_dw_in", h1, dproj, "tn", BF16, tk=2048)
    if rest_is_sharded:
        (dx, d_pre_mix), (g_in,) = _dh1_first_bwd(dproj, wp_in, x2, dx1, pre_mix, [_cols_to_shards(_unpermute_in(g_in))])
    else:
        dx, d_pre_mix = _dh1_first_bwd(dproj, wp_in, x2, dx1, pre_mix, [])

    small = dict(pre_mix_norm=d_pre_mix, ssd_norm_w=d_snw, post_mix_norm=d_post_mix, pre_ffn_norm=d_pre_ffn,
                 post_ffn_norm=d_post_ffn, dd_lanes=d_dd, loss_lanes=loss_lanes, gdn_gates=d_gp, ssd_gates=d_sp,
                 gdn_norm_w=d_gnw, gdn_conv_w=d_gcw[0:4], ssd_conv_w=d_scw[0:4], ssd_conv_b=d_scb,
                 ffn_conv_w=d_fcw[0:3], ffn_conv_b=d_fcb)
    return dx.reshape(bsz, seq, d), g_in, g_out, g_up, g_down, small


def kernel(x, pre_mix_norm, w_in, gdn_conv_w, gdn_a_log, gdn_dt_bias, gdn_norm_w, ssd_conv_w, ssd_conv_b, ssd_a_log, ssd_dt_bias, ssd_d, ssd_norm_w, w_out, post_mix_norm, pre_ffn_norm, w_up, ffn_conv_w, ffn_conv_b, w_down, post_ffn_norm, loss_target, m_pre_mix_norm, m_w_in, m_gdn_conv_w, m_gdn_a_log, m_gdn_dt_bias, m_gdn_norm_w, m_ssd_conv_w, m_ssd_conv_b, m_ssd_a_log, m_ssd_dt_bias, m_ssd_d, m_ssd_norm_w, m_w_out, m_post_mix_norm, m_pre_ffn_norm, m_w_up, m_ffn_conv_w, m_ffn_conv_b, m_w_down, m_post_ffn_norm, v_pre_mix_norm, v_w_in, v_gdn_conv_w, v_gdn_a_log, v_gdn_dt_bias, v_gdn_norm_w, v_ssd_conv_w, v_ssd_conv_b, v_ssd_a_log, v_ssd_dt_bias, v_ssd_d, v_ssd_norm_w, v_w_out, v_post_mix_norm, v_pre_ffn_norm, v_w_up, v_ffn_conv_w, v_ffn_conv_b, v_w_down, v_post_ffn_norm):
    names = ["pre_mix_norm", "w_in", "gdn_conv_w", "gdn_a_log", "gdn_dt_bias", "gdn_norm_w", "ssd_conv_w", "ssd_conv_b",
             "ssd_a_log", "ssd_dt_bias", "ssd_d", "ssd_norm_w", "w_out", "post_mix_norm", "pre_ffn_norm", "w_up",
             "ffn_conv_w", "ffn_conv_b", "w_down", "post_ffn_norm"]
    w_args = [pre_mix_norm, w_in, gdn_conv_w, gdn_a_log, gdn_dt_bias, gdn_norm_w, ssd_conv_w, ssd_conv_b, ssd_a_log, ssd_dt_bias, ssd_d, ssd_norm_w, w_out, post_mix_norm, pre_ffn_norm, w_up, ffn_conv_w, ffn_conv_b, w_down, post_ffn_norm]
    m_args = [m_pre_mix_norm, m_w_in, m_gdn_conv_w, m_gdn_a_log, m_gdn_dt_bias, m_gdn_norm_w, m_ssd_conv_w, m_ssd_conv_b, m_ssd_a_log, m_ssd_dt_bias, m_ssd_d, m_ssd_norm_w, m_w_out, m_post_mix_norm, m_pre_ffn_norm, m_w_up, m_ffn_conv_w, m_ffn_conv_b, m_w_down, m_post_ffn_norm]
    v_args = [v_pre_mix_norm, v_w_in, v_gdn_conv_w, v_gdn_a_log, v_gdn_dt_bias, v_gdn_norm_w, v_ssd_conv_w, v_ssd_conv_b, v_ssd_a_log, v_ssd_dt_bias, v_ssd_d, v_ssd_norm_w, v_w_out, v_post_mix_norm, v_pre_ffn_norm, v_w_up, v_ffn_conv_w, v_ffn_conv_b, v_w_down, v_post_ffn_norm]
    w = {k: a[0] for k, a in zip(names, w_args)}
    m = {k: a[0] for k, a in zip(names, m_args)}
    v = {k: a[0] for k, a in zip(names, v_args)}
    idx = 4 * lax.axis_index("x") + 2 * lax.axis_index("y") + lax.axis_index("c")
    big = ("w_in", "w_out", "w_up", "w_down")
    conv = ("gdn_conv_w", "ssd_conv_w", "ffn_conv_w")

    conv_local = jnp.concatenate([jnp.pad(w[k], ((0, 4 - w[k].shape[0]), (0, 0))) for k in conv], axis=1)
    g_in, g_conv = _gather_two_level("gather_weights", [w["w_in"].astype(BF16), conv_local])
    wp_in = _permute_in(_cols_from_shards(g_in))
    p = {k: w[k] for k in names if k not in big and k not in conv}
    off = 0
    for k in conv:
        cw = w[k].shape[1]
        p[k] = jnp.transpose(g_conv[:, :w[k].shape[0], off:off + cw], (1, 0, 2)).reshape(w[k].shape[0], N_DEV * cw)
        off += cw

    rest = tuple(w[k].astype(BF16) for k in ("w_out", "w_up", "w_down"))
    dx, p_in, p_out, p_up, p_down, small = _local_step(x, loss_target, wp_in, rest, p, True)

    gate_row = jnp.concatenate([small["gdn_gates"][0], small["gdn_gates"][1], small["ssd_gates"][0], small["ssd_gates"][1],
                                small["gdn_norm_w"][0], jnp.zeros((D_MODEL - 5 * LANES,), F32)]).reshape(1, D_MODEL)
    pack = _pack([small["pre_mix_norm"], small["ssd_norm_w"], small["post_mix_norm"], small["pre_ffn_norm"],
                  small["post_ffn_norm"], small["dd_lanes"], small["loss_lanes"], gate_row,
                  small["gdn_conv_w"], small["ssd_conv_w"], jnp.pad(small["ssd_conv_b"], ((0, 0), (0, 512))),
                  jnp.pad(small["ffn_conv_w"].reshape(-1), (0, 17 * D_MODEL - 3 * 2 * D_FF)),
                  jnp.pad(small["ffn_conv_b"], ((0, 0), (0, 512)))], SMALL_ROWS)
    (pack_all,) = _gather_two_level("gather_small", [pack])
    ssum, extra = _small_sum(pack_all)

    grads, deltas, new_m, new_v = {}, {}, {}, {}
    for k, parts in (("w_in", p_in), ("w_out", p_out), ("w_up", p_up), ("w_down", p_down)):
        grads[k], deltas[k], new_m[k], new_v[k] = _adam_big("adam_" + k, parts, w[k], m[k], v[k], 256)

    flat = ssum.reshape(-1)
    gate = ssum[7]
    sg = dict(pre_mix_norm=ssum[0], ssd_norm_w=ssum[1], post_mix_norm=ssum[2], pre_ffn_norm=ssum[3], post_ffn_norm=ssum[4],
              gdn_a_log=gate[8:16], gdn_dt_bias=gate[LANES + 8:LANES + 16], ssd_a_log=gate[2 * LANES + 16:2 * LANES + 32],
              ssd_dt_bias=gate[3 * LANES + 16:3 * LANES + 32], gdn_norm_w=gate[4 * LANES:5 * LANES], ssd_d=extra[0, 0:SSD_HEADS])
    o = 8 * D_MODEL
    full_gcw = flat[o:o + 4 * 3072].reshape(4, 3072)
    o += 12 * D_MODEL
    full_scw = flat[o:o + 4 * 1536].reshape(4, 1536)
    o += 6 * D_MODEL
    sg["ssd_conv_b"] = flat[o:o + 1536]
    o += 2 * D_MODEL
    full_fcw = flat[o:o + 3 * 2 * D_FF].reshape(3, 2 * D_FF)
    o += 17 * D_MODEL
    sg["ffn_conv_b"] = flat[o:o + 2 * D_FF]
    for k, full in (("gdn_conv_w", full_gcw), ("ssd_conv_w", full_scw), ("ffn_conv_w", full_fcw)):
        cw = w[k].shape[1]
        sg[k] = lax.dynamic_slice_in_dim(full, idx * cw, cw, axis=1)
    small_names = [k for k in names if k not in big]
    rows = 24
    gpk = _pack([sg[k] for k in small_names], rows)
    dpk, mpk, vpk = _adam_small(gpk, _pack([w[k] for k in small_names], rows), _pack([m[k] for k in small_names], rows),
                                _pack([v[k] for k in small_names], rows))
    shapes = [w[k].shape for k in small_names]
    for k, g_, d_, m_, v_ in zip(small_names, _unpack(gpk, shapes), _unpack(dpk, shapes), _unpack(mpk, shapes), _unpack(vpk, shapes)):
        grads[k], deltas[k], new_m[k], new_v[k] = g_, d_, m_, v_

    loss = extra[1, 0]
    lead = lambda a: a[None]
    return (loss, dx, *[lead(grads[k]) for k in names], *[lead(deltas[k]) for k in names],
            *[lead(new_m[k]) for k in names], *[lead(new_v[k]) for k in names])
```

```python
import functools

import jax
import jax.numpy as jnp
from jax import lax
from jax.experimental import pallas as pl
from jax.experimental.pallas import tpu as pltpu

F32 = jnp.float32
BF16 = jnp.bfloat16
MXU_DTYPE = jnp.bfloat16
HIGHEST = lax.Precision.HIGHEST
VMEM_LIMIT_V7X = 48 * 1024 * 1024
SUBLANES = 8
LANES = 128

D_MODEL = 1024
GDN_HEADS = 8
GDN_DK = 128
SSD_HEADS = 16
SSD_P = 64
SSD_GROUPS = 2
SSD_HPG = 8
SSD_N = 128
CHUNK = 128
D_FF = 2816
EPS = 1e-6
N_DEV = 8
PROJ_W = 7168
SMALL_CB = 52
D_IN = 6688

ADAM_LR = 0.001
ADAM_B1 = 0.9
ADAM_B2 = 0.999
ADAM_EPS = 1e-08
ADAM_WD = 0.01
ADAM_STEP = 10

NN = (((1,), (0,)), ((), ()))
NT = (((1,), (1,)), ((), ()))
TN = (((0,), (0,)), ((), ()))


def _pcall(body, **kw):
    return pl.pallas_call(body, **kw)


def _mm(a, b, dims=NN):
    return lax.dot_general(a.astype(MXU_DTYPE), b.astype(MXU_DTYPE), dims, preferred_element_type=F32)


def _mmx(a, b, dims=NN):
    return lax.dot_general(a, b, dims, precision=HIGHEST, preferred_element_type=F32)


def _split(a):
    hi = a.astype(MXU_DTYPE)
    return hi, (a - hi.astype(F32)).astype(MXU_DTYPE)


def _mm3(a, b, dims=NN):
    (ah, al), (bh, bl) = _split(a), _split(b)
    dot = lambda p, q: lax.dot_general(p, q, dims, preferred_element_type=F32)
    return dot(ah, bh) + (dot(ah, bl) + dot(al, bh))


def _mmsel(a, sel, dims=NN, terms=2):
    s = sel.astype(MXU_DTYPE)
    out = None
    for _ in range(terms):
        part = a.astype(MXU_DTYPE)
        a = a - part.astype(F32)
        prod = lax.dot_general(part, s, dims, preferred_element_type=F32)
        out = prod if out is None else out + prod
    return out


def _sigmoid(x):
    return 0.5 * jnp.tanh(0.5 * x) + 0.5


def _softplus(x):
    return jnp.maximum(x, 0.0) + jnp.log(1.0 + jnp.exp(-jnp.abs(x)))


def _dsilu(x, s):
    return s * (1.0 + x * (1.0 - s))


def _rowsum(x):
    return jnp.sum(x, axis=1, keepdims=True)


def _colsum(x):
    return jnp.sum(x, axis=0, keepdims=True)


def _pick(dim, pref):
    if dim <= pref:
        return dim
    best = None
    t = LANES
    while t <= pref:
        if dim % t == 0:
            best = t
        t += LANES
    return dim if best is None else best


def _params(sem):
    return pltpu.CompilerParams(dimension_semantics=sem, vmem_limit_bytes=VMEM_LIMIT_V7X)


def _matmul(name, a, b, mode, out_dtype, tm=1024, tn=1024, tk=1024):
    if mode == "nn":
        (m, k), (_, n) = a.shape, b.shape
    elif mode == "nt":
        (m, k), (n, _) = a.shape, b.shape
    else:
        (k, m), (_, n) = a.shape, b.shape
    tm, tn, tk = _pick(m, tm), _pick(n, tn), _pick(k, tk)
    nk = k // tk
    if mode == "tn":
        a_spec = pl.BlockSpec((tk, tm), lambda i, j, kk: (kk, i))
    else:
        a_spec = pl.BlockSpec((tm, tk), lambda i, j, kk: (i, kk))
    if mode == "nt":
        b_spec = pl.BlockSpec((tn, tk), lambda i, j, kk: (j, kk))
    else:
        b_spec = pl.BlockSpec((tk, tn), lambda i, j, kk: (kk, j))
    dims = {"nn": NN, "nt": NT, "tn": TN}[mode]

    def body(a_ref, b_ref, o_ref, *acc):
        if nk == 1:
            o_ref[...] = _mm(a_ref[...], b_ref[...], dims).astype(out_dtype)
            return
        kk = pl.program_id(2)

        @pl.when(kk == 0)
        def _():
            acc[0][...] = jnp.zeros_like(acc[0])

        acc[0][...] += _mm(a_ref[...], b_ref[...], dims)

        @pl.when(kk == nk - 1)
        def _():
            o_ref[...] = acc[0][...].astype(out_dtype)

    return _pcall(
        body, name=name, grid=(m // tm, n // tn, nk),
        in_specs=[a_spec, b_spec],
        out_specs=pl.BlockSpec((tm, tn), lambda i, j, kk: (i, j)),
        out_shape=jax.ShapeDtypeStruct((m, n), out_dtype),
        scratch_shapes=[pltpu.VMEM((tm, tn), F32)] if nk > 1 else [],
        compiler_params=_params(("parallel", "parallel", "arbitrary")),
    )(a, b)


def _matmul_rows(name, a, b, mode, epilogue, row_ins, full_ins, outs, accs=(), tm=512, tk=1024, scatter_riders=()):
    if mode == "nn":
        (m, k), (_, n) = a.shape, b.shape
    else:
        (m, k), (n, _) = a.shape, b.shape
    tm, tk = _pick(m, tm), _pick(k, tk)
    nk = k // tk
    a_spec = pl.BlockSpec((tm, tk), lambda i, kk: (i, kk))
    b_spec = pl.BlockSpec((n, tk), lambda i, kk: (0, kk)) if mode == "nt" else pl.BlockSpec((tk, n), lambda i, kk: (kk, 0))
    dims = NT if mode == "nt" else NN
    n_row, n_full, n_out, n_acc = len(row_ins), len(full_ins), len(outs), len(accs)

    def body(a_ref, b_ref, *rest):
        ins = rest[:n_row + n_full]
        out_refs = rest[n_row + n_full:n_row + n_full + n_out]
        acc_refs = rest[n_row + n_full + n_out:n_row + n_full + n_out + n_acc]
        prod_scr = rest[-1]
        i, kk = pl.program_id(0), pl.program_id(1)

        if n_acc:
            @pl.when((i == 0) & (kk == 0))
            def _():
                for r in acc_refs:
                    r[...] = jnp.zeros_like(r)

        if nk == 1:
            epilogue(_mm(a_ref[...], b_ref[...], dims), *ins, *out_refs, *acc_refs)
            return

        @pl.when(kk == 0)
        def _():
            prod_scr[...] = jnp.zeros_like(prod_scr)

        prod_scr[...] += _mm(a_ref[...], b_ref[...], dims)

        @pl.when(kk == nk - 1)
        def _():
            epilogue(prod_scr[...], *ins, *out_refs, *acc_refs)

    grid = (m // tm, nk)
    riders = list(scatter_riders)
    n_in = 2 + n_row + n_full
    any_spec, rider_shapes, rider_sems, wrap = _riding_exchange(riders, True, n_in, n_out + n_acc, grid)
    row_ins = [r if isinstance(r, tuple) else (r, r.shape[1], 0) for r in row_ins]
    in_specs = [a_spec, b_spec] + [pl.BlockSpec((tm, w), lambda i, kk, cb=cb: (i, cb)) for _, w, cb in row_ins]
    in_specs += [pl.BlockSpec(f.shape, lambda i, kk, nd=f.ndim: (0,) * nd) for f in full_ins]
    row_ins = [r for r, _, _ in row_ins]
    out_specs = [pl.BlockSpec((tm, w), lambda i, kk: (i, 0)) for w, _ in outs]
    out_specs += [pl.BlockSpec(s, lambda i, kk: (0, 0)) for s in accs]
    out_shape = [jax.ShapeDtypeStruct((m, w), dt) for w, dt in outs] + [jax.ShapeDtypeStruct(s, F32) for s in accs]
    res = _pcall(
        wrap(body), name=name, grid=grid,
        in_specs=in_specs + any_spec, out_specs=out_specs + any_spec, out_shape=out_shape + rider_shapes,
        scratch_shapes=[pltpu.VMEM((tm, n), F32)] + rider_sems,
        compiler_params=_params(("arbitrary", "arbitrary")),
    )(a, b, *row_ins, *full_ins, *riders)
    return (res[:n_out + n_acc], res[n_out + n_acc:]) if riders else res


def _rowwise(name, body, n_rows, tm, ins, outs, accs=()):
    arrays, in_specs = [], []
    last8 = n_rows // SUBLANES - 1
    per = tm // SUBLANES
    for spec in ins:
        kind, arr = spec[0], spec[1]
        if kind == "full":
            in_specs.append(pl.BlockSpec(arr.shape, lambda i, nd=arr.ndim: (0,) * nd))
        else:
            w, cb = spec[2], spec[3]
            if kind == "row":
                in_specs.append(pl.BlockSpec((tm, w), lambda i, cb=cb: (i, cb)))
            elif kind == "prev":
                in_specs.append(pl.BlockSpec((SUBLANES, w), lambda i, cb=cb: (jnp.maximum(i * per - 1, 0), cb)))
            else:
                in_specs.append(pl.BlockSpec((SUBLANES, w), lambda i, cb=cb: (jnp.minimum((i + 1) * per, last8), cb)))
        arrays.append(arr)
    out_shape = [jax.ShapeDtypeStruct((n_rows, w), dt) for (w, dt) in outs]
    out_shape += [jax.ShapeDtypeStruct(s, F32) for s in accs]
    out_specs = [pl.BlockSpec((tm, w), lambda i: (i, 0)) for (w, _) in outs]
    out_specs += [pl.BlockSpec(s, lambda i: (0, 0)) for s in accs]
    n_io = len(ins) + len(outs)

    def kern(*refs):
        i = pl.program_id(0)
        if accs:
            @pl.when(i == 0)
            def _():
                for r in refs[n_io:]:
                    r[...] = jnp.zeros_like(r)
        body(i, *refs)

    res = _pcall(
        kern, name=name, grid=(n_rows // tm,), in_specs=in_specs, out_specs=out_specs, out_shape=out_shape,
        compiler_params=_params(("arbitrary",)),
    )(*arrays)
    return res


def _shift_down(x, halo, j):
    r = pltpu.roll(x, j, 0)
    hr = pltpu.roll(halo, j, 0)
    rows = lax.broadcasted_iota(jnp.int32, (SUBLANES, x.shape[1]), 0)
    top = jnp.where(rows < j, hr, r[0:SUBLANES])
    return jnp.concatenate([top, r[SUBLANES:]], axis=0)


def _shift_up(x, halo, j):
    tm = x.shape[0]
    r = pltpu.roll(x, tm - j, 0)
    hr = pltpu.roll(halo, SUBLANES - j, 0)
    rows = lax.broadcasted_iota(jnp.int32, (SUBLANES, x.shape[1]), 0)
    bot = jnp.where(rows >= SUBLANES - j, hr, r[tm - SUBLANES:])
    return jnp.concatenate([r[:tm - SUBLANES], bot], axis=0)


def _conv_taps(x, halo, kw):
    return [x if kw - 1 - k == 0 else _shift_down(x, halo, kw - 1 - k) for k in range(kw)]


def _conv(taps, w):
    y = taps[0] * w[0:1]
    for k in range(1, len(taps)):
        y = y + taps[k] * w[k:k + 1]
    return y


def _rms(x, width):
    r = lax.rsqrt(jnp.sum(x * x, axis=-1, keepdims=True) * (1.0 / width) + EPS)
    return x * r, r


def _rms_bwd(xh, r, dxh, width):
    return r * (dxh - xh * (jnp.sum(dxh * xh, axis=-1, keepdims=True) * (1.0 / width)))


def _seq_flags(i, seq, tm):
    nps = seq // tm
    pos = i % nps
    return jnp.where(pos == 0, 0.0, 1.0), jnp.where(pos == nps - 1, 0.0, 1.0)


def _norm_cast(name, x, w, tm):
    t, d = x.shape

    def body(i, x_ref, w_ref, h_ref):
        xh, _ = _rms(x_ref[...], d)
        h_ref[...] = (xh * w_ref[...]).astype(BF16)

    return _rowwise(name, body, t, tm, [("row", x, d, 0), ("full", w)], [(d, BF16)])[0]


def _gdn_prep(proj, cw, gp, seq, tm):
    t = proj.shape[0]
    d = D_MODEL

    def body(i, q_ref, qh_ref, k_ref, kh_ref, v_ref, vh_ref, sm_ref, cw_ref, gp_ref, qn_ref, kn_ref, vv_ref, gs_ref, ypre_ref):
        keep, _ = _seq_flags(i, seq, tm)
        for x_ref, h_ref, o_ref, off, scale in ((q_ref, qh_ref, qn_ref, 0, GDN_DK ** -0.5),
                                               (k_ref, kh_ref, kn_ref, d, 1.0), (v_ref, vh_ref, vv_ref, 2 * d, None)):
            y = _conv(_conv_taps(x_ref[...], h_ref[...] * keep, 4), cw_ref[:, off:off + d])
            ypre_ref[:, off:off + d] = y
            a = y * _sigmoid(y)
            if scale is None:
                o_ref[...] = a
            else:
                for hh in range(GDN_HEADS):
                    s = a[:, hh * GDN_DK:(hh + 1) * GDN_DK]
                    n = lax.rsqrt(_rowsum(s * s) + EPS)
                    o_ref[:, hh * GDN_DK:(hh + 1) * GDN_DK] = s * (n * scale)
        sm = sm_ref[...]
        lane = lax.broadcasted_iota(jnp.int32, sm.shape, 1)
        beta = _sigmoid(sm)
        g = jnp.where((lane >= 8) & (lane < 16), -jnp.exp(gp_ref[0:1, :]) * _softplus(sm + gp_ref[1:2, :]), 0.0)
        gs_ref[...] = jnp.where(lane < 8, beta, _mmx(_block_tri(tm, False), g))

    ins = []
    for cb in range(3):
        ins += [("row", proj, d, cb), ("prev", proj, d, cb)]
    ins += [("row", proj, LANES, SMALL_CB), ("full", cw), ("full", gp)]
    return _rowwise("gdn_prep", body, t, tm, ins, [(d, F32), (d, F32), (d, F32), (LANES, F32), (3 * d, F32)])


def _block_tri(tm, upper):
    ri = lax.broadcasted_iota(jnp.int32, (tm, tm), 0)
    ci = lax.broadcasted_iota(jnp.int32, (tm, tm), 1)
    tri = (ri <= ci) if upper else (ri >= ci)
    return (tri & ((ri // CHUNK) == (ci // CHUNK))).astype(F32)


def _chunk_consts():
    row = lax.broadcasted_iota(jnp.int32, (CHUNK, CHUNK), 0)
    col = lax.broadcasted_iota(jnp.int32, (CHUNK, CHUNK), 1)
    return dict(
        tril=row >= col, strict=row > col, eye=(row == col).astype(F32),
        lane=lax.broadcasted_iota(jnp.int32, (CHUNK, LANES), 1),
        row1=lax.broadcasted_iota(jnp.int32, (CHUNK, 1), 0),
        ones=jnp.ones((CHUNK, LANES), F32))


def _hmap(fn, *lists):
    return [fn(*a) for a in zip(*lists)]


def _tri_inv(nmats, eye):
    levels = CHUNK.bit_length() - 2
    x = [eye - n for n in nmats]
    p = _hmap(_mm3, nmats, nmats)
    for lvl in range(levels):
        x = _hmap(lambda xi, pi: xi + _mm3(xi, pi), x, p)
        if lvl < levels - 1:
            p = _hmap(_mm3, p, p)
    return x


def _gdn_gates(gs, gc_row, h, c):
    beta = _rowsum(jnp.where(c["lane"] == h, gs, 0.0))
    gc = _rowsum(jnp.where(c["lane"] == h + 8, gs, 0.0))
    dc = jnp.exp(jnp.where(c["tril"], gc - gc_row, -1e30))
    gl = gc[CHUNK - 1:CHUNK, :]
    return beta, dc, jnp.exp(gc), jnp.exp(gl), jnp.exp(gl - gc)


GDN_HB = GDN_HEADS


def _gdn_specs(seq, sb, hb, backward):
    assert hb == GDN_HEADS
    nsb = seq // sb
    ncb = sb // CHUNK
    order = (lambda j: nsb - 1 - j) if backward else (lambda j: j)
    specs = dict(
        wide=lambda: pl.BlockSpec((1, sb, hb * GDN_DK), lambda b, h, j: (b, order(j), h)),
        gs=lambda: pl.BlockSpec((1, sb, LANES), lambda b, h, j: (b, order(j), 0)),
        gr=pl.BlockSpec((1, ncb, GDN_HEADS, CHUNK), lambda b, h, j: (b, order(j), 0, 0)),
        st=pl.BlockSpec((1, hb, ncb * GDN_DK, GDN_DK), lambda b, h, j: (b, h, order(j), 0)),
        ti=pl.BlockSpec((1, hb, sb, CHUNK), lambda b, h, j: (b, h, order(j), 0)))
    return nsb, ncb, specs


def _riding_exchange(arrays, scatter, n_in, n_out, grid):
    n = len(arrays)
    if n == 0:
        return [], [], [], lambda body: body
    any_spec = [pl.BlockSpec(memory_space=pl.ANY)] * n

    def wrap(body):
        def wrapped(*refs):
            ins = refs[n_in:n_in + n]
            outs = refs[n_in + n + n_out:n_in + 2 * n + n_out]
            sems = refs[len(refs) - 3:]
            pid = [pl.program_id(a) for a in range(len(grid))]
            first = functools.reduce(lambda a, b: a & b, [p == 0 for p in pid])
            last = functools.reduce(lambda a, b: a & b, [p == g - 1 for p, g in zip(pid, grid)])

            @pl.when(first)
            def _():
                _exchange_phase(ins, outs, sems, scatter, start=True)

            body(*refs[:n_in], *refs[n_in + n:n_in + n + n_out], *refs[n_in + 2 * n + n_out:len(refs) - 3])

            @pl.when(last)
            def _():
                _exchange_phase(ins, outs, sems, scatter, start=False)

        return wrapped

    return any_spec, _exchange_out_shapes(arrays, scatter), _exchange_sems(n), wrap


def _gdn_chunk_fwd(qn, kn, vv, gs, gr, bsz, seq, sb, riders):
    hb = GDN_HB
    nsb, ncb, sp = _gdn_specs(seq, sb, hb, False)
    grid = (bsz, GDN_HEADS // hb, nsb)
    any_spec, rider_shapes, rider_sems, wrap = _riding_exchange(riders, False, 5, 3, grid)

    def body(q_ref, k_ref, v_ref, gs_ref, gr_ref, o_ref, st_ref, ti_ref, s_scr):
        hg = pl.program_id(1)

        @pl.when(pl.program_id(2) == 0)
        def _():
            s_scr[...] = jnp.zeros_like(s_scr)

        c = _chunk_consts()

        def chunk(n, carry):
            r = pl.ds(pl.multiple_of(n * CHUNK, CHUNK), CHUNK)
            rs = pl.ds(pl.multiple_of(n * GDN_DK, GDN_DK), GDN_DK)
            gsv = gs_ref[0, r, :]
            heads = list(range(hb))
            sls = [slice(ih * GDN_DK, (ih + 1) * GDN_DK) for ih in heads]
            q = [q_ref[0, r, sl] for sl in sls]
            k = [k_ref[0, r, sl] for sl in sls]
            v = [v_ref[0, r, sl] for sl in sls]
            beta, dc, eg, egl, ekd = zip(*[
                _gdn_gates(gsv, gr_ref[0, n, pl.ds(ih, 1), :], ih, c) for ih in heads])
            kb = _hmap(lambda a, b: a * b, k, beta)
            amat = _hmap(lambda a, b, d_: jnp.where(c["strict"], _mm(a, b, NT) * d_, 0.0), kb, k, dc)
            tinv = _tri_inv(amat, c["eye"])
            u = _hmap(lambda t_, a, b: _mm3(t_, a * b), tinv, v, beta)
            w = _hmap(lambda t_, a, b: _mm3(t_, a * b), tinv, kb, eg)
            qk = _hmap(lambda a, b, d_: _mm(a, b, NT) * d_, q, k, dc)
            s = [s_scr[ih] for ih in heads]
            v_new = _hmap(lambda a, b, s_: a - _mm(b, s_), u, w, s)
            o = _hmap(lambda a, e, s_, qk_, vn: _mm(a * e, s_) + _mm(qk_, vn), q, eg, s, qk, v_new)
            s_new = _hmap(lambda s_, e, a, f, vn: s_ * e + _mm(a * f, vn, TN), s, egl, k, ekd, v_new)
            for ih in heads:
                o_ref[0, r, sls[ih]] = o[ih]
                st_ref[0, ih, rs, :] = s[ih]
                ti_ref[0, ih, r, :] = tinv[ih]
                s_scr[ih] = s_new[ih]
            return carry

        lax.fori_loop(0, ncb, chunk, 0)

    t3 = (bsz, seq, D_MODEL)
    res = _pcall(
        wrap(body), name="gdn_chunk_fwd", grid=grid,
        in_specs=[sp["wide"](), sp["wide"](), sp["wide"](), sp["gs"](), sp["gr"]] + any_spec,
        out_specs=[sp["wide"](), sp["st"], sp["ti"]] + any_spec,
        out_shape=[jax.ShapeDtypeStruct(t3, F32),
                   jax.ShapeDtypeStruct((bsz, GDN_HEADS, (seq // CHUNK) * GDN_DK, GDN_DK), F32),
                   jax.ShapeDtypeStruct((bsz, GDN_HEADS, seq, CHUNK), F32)] + rider_shapes,
        scratch_shapes=[pltpu.VMEM((hb, GDN_DK, GDN_DK), F32)] + rider_sems,
        compiler_params=_params(("arbitrary", "arbitrary", "arbitrary")),
    )(qn, kn, vv, gs, gr, *riders)
    return res[:3], res[3:]


def _ssd_prep(proj, cw, cb, sp, seq, tm):
    t = proj.shape[0]
    d = D_MODEL
    ssd_w = SSD_HEADS * SSD_P

    def body(i, x_ref, xh_ref, bc_ref, bch_ref, sm_ref, cw_ref, cb_ref, sp_ref, xs_ref, bco_ref, dtx_ref, acsx_ref, acs_ref, ypre_ref):
        keep, _ = _seq_flags(i, seq, tm)
        y = _conv(_conv_taps(x_ref[...], xh_ref[...] * keep, 4), cw_ref[:, 0:d]) + cb_ref[:, 0:d]
        ypre_ref[:, 0:d] = y
        xs_ref[...] = y * _sigmoid(y)
        y = _conv(_conv_taps(bc_ref[...], bch_ref[...] * keep, 4), cw_ref[:, d:d + 512]) + cb_ref[:, d:d + 512]
        ypre_ref[:, d:d + 512] = y
        bco_ref[...] = y * _sigmoid(y)
        sm = sm_ref[...]
        lane = lax.broadcasted_iota(jnp.int32, sm.shape, 1)
        valid = (lane >= 16) & (lane < 32)
        dt = jnp.where(valid, _softplus(sm + sp_ref[1:2, :]), 0.0)
        adt = dt * (-jnp.exp(sp_ref[0:1, :]))
        acs = _mmx(_block_tri(tm, False), adt)
        l64 = lax.broadcasted_iota(jnp.int32, (LANES, ssd_w), 0)
        d64 = lax.broadcasted_iota(jnp.int32, (LANES, ssd_w), 1)
        e64 = (l64 - 16 == d64 // SSD_P).astype(F32)
        dtx_ref[...] = _mmsel(dt, e64, terms=3)
        acsx_ref[...] = _mmsel(acs, e64, terms=3)
        acs_ref[...] = acs

    ins = [("row", proj, d, 5), ("prev", proj, d, 5), ("row", proj, 512, 12), ("prev", proj, 512, 12),
           ("row", proj, LANES, SMALL_CB), ("full", cw), ("full", cb), ("full", sp)]
    return _rowwise("ssd_prep", body, t, tm, ins,
                    [(d, F32), (512, F32), (ssd_w, F32), (ssd_w, F32), (LANES, F32), (d + 512, F32)])


SSD_GW = SSD_HPG * SSD_P


def _ssd_head(acs, ar_ref, n, head, cbm, c):
    col = _rowsum(jnp.where(c["lane"] == head + 16, acs, 0.0))
    lm = jnp.exp(jnp.where(c["tril"], col - ar_ref[0, n, pl.ds(head, 1), :], -1e30))
    return lm, cbm * lm


def _ssd_specs(seq, sb):
    nsb = seq // sb
    ncb = sb // CHUNK
    def specs(order):
        return dict(
            wide=lambda: pl.BlockSpec((1, sb, SSD_HEADS * SSD_P), lambda b, j: (b, order(j), 0)),
            bc=lambda: pl.BlockSpec((1, sb, 2 * SSD_GROUPS * SSD_N), lambda b, j: (b, order(j), 0)),
            half=lambda: pl.BlockSpec((1, sb, SSD_GROUPS * SSD_N), lambda b, j: (b, order(j), 0)),
            small=lambda: pl.BlockSpec((1, sb, LANES), lambda b, j: (b, order(j), 0)),
            ar=pl.BlockSpec((1, ncb, SSD_HEADS, CHUNK), lambda b, j: (b, order(j), 0, 0)),
            st=pl.BlockSpec((1, ncb * SSD_N, SSD_HEADS * SSD_P), lambda b, j: (b, order(j), 0)))
    return nsb, ncb, specs(lambda j: j), specs(lambda j: nsb - 1 - j)


def _ssd_chunk_fwd(xs, bc, dtx, acsx, acs, ar, bsz, seq, sb):
    nsb, ncb, sp, _ = _ssd_specs(seq, sb)

    def body(x_ref, dtx_ref, ax_ref, bc_ref, acs_ref, ar_ref, y_ref, sts_ref, st_scr):
        @pl.when(pl.program_id(1) == 0)
        def _():
            st_scr[...] = jnp.zeros_like(st_scr)

        c = _chunk_consts()
        lane5 = lax.broadcasted_iota(jnp.int32, (CHUNK, SSD_GW), 1) // SSD_P

        def chunk(n, carry):
            r = pl.ds(pl.multiple_of(n * CHUNK, CHUNK), CHUNK)
            rs = pl.ds(pl.multiple_of(n * SSD_N, SSD_N), SSD_N)
            acsv = acs_ref[0, r, :]
            for g in range(SSD_GROUPS):
                gl = slice(g * SSD_GW, (g + 1) * SSD_GW)
                x, dt, ax = x_ref[0, r, gl], dtx_ref[0, r, gl], ax_ref[0, r, gl]
                bm = bc_ref[0, r, g * SSD_N:(g + 1) * SSD_N]
                cm = bc_ref[0, r, (SSD_GROUPS + g) * SSD_N:(SSD_GROUPS + g + 1) * SSD_N]
                xdt = x * dt
                cbm = _mm(cm, bm, NT)
                al = ax[CHUNK - 1:CHUNK, :]
                st = st_scr[:, gl]
                y = _mm(cm, st) * jnp.exp(ax)
                for hh in range(SSD_HPG):
                    _, gm = _ssd_head(acsv, ar_ref, n, g * SSD_HPG + hh, cbm, c)
                    y = y + _mm(gm, jnp.where(lane5 == hh, xdt, 0.0))
                y_ref[0, r, gl] = y
                sts_ref[0, rs, gl] = st
                st_scr[:, gl] = st * jnp.exp(al) + _mm(bm, xdt * jnp.exp(al - ax), TN)
            return carry

        lax.fori_loop(0, ncb, chunk, 0)

    return _pcall(
        body, name="ssd_chunk_fwd", grid=(bsz, nsb),
        in_specs=[sp["wide"](), sp["wide"](), sp["wide"](), sp["bc"](), sp["small"](), sp["ar"]],
        out_specs=[sp["wide"](), sp["st"]],
        out_shape=[jax.ShapeDtypeStruct((bsz, seq, SSD_HEADS * SSD_P), F32),
                   jax.ShapeDtypeStruct((bsz, (seq // CHUNK) * SSD_N, SSD_HEADS * SSD_P), F32)],
        scratch_shapes=[pltpu.VMEM((SSD_N, SSD_HEADS * SSD_P), F32)],
        compiler_params=_params(("parallel", "arbitrary")),
    )(xs, dtx, acsx, bc, acs, ar)


def _gate_norm(o_gdn, y_ssd, xs, proj, gnw, snw, dvec, tm):
    t = o_gdn.shape[0]
    d = D_MODEL

    def body(i, o_ref, za_ref, y_ref, xs_ref, zs_ref, gnw_ref, snw_ref, dv_ref, out_ref):
        for hh in range(GDN_HEADS):
            sl = slice(hh * GDN_DK, (hh + 1) * GDN_DK)
            oh, _ = _rms(o_ref[:, sl], GDN_DK)
            z = za_ref[:, sl]
            out_ref[:, sl] = (oh * gnw_ref[...] * (z * _sigmoid(z))).astype(BF16)
        zs = zs_ref[...]
        yg = (y_ref[...] + dv_ref[...] * xs_ref[...]) * (zs * _sigmoid(zs))
        for g in range(SSD_GROUPS):
            sl = slice(g * 512, (g + 1) * 512)
            yh, _ = _rms(yg[:, sl], 512)
            out_ref[:, d + g * 512:d + (g + 1) * 512] = (yh * snw_ref[:, sl]).astype(BF16)

    ins = [("row", o_gdn, d, 0), ("row", proj, d, 3), ("row", y_ssd, d, 0), ("row", xs, d, 0), ("row", proj, d, 4),
           ("full", gnw), ("full", snw), ("full", dvec)]
    return _rowwise("gate_norm", body, t, tm, ins, [(2 * d, BF16)])[0]


def _out_mid(mixin, w_out, x, pmw, pfw):
    d = D_MODEL

    def epilogue(mix, x_ref, pmw_ref, pfw_ref, mix_ref, x1_ref, h2_ref):
        mix_ref[...] = mix
        mh, _ = _rms(mix, d)
        x1 = x_ref[...] + mh * pmw_ref[...]
        x1_ref[...] = x1
        xh, _ = _rms(x1, d)
        h2_ref[...] = (xh * pfw_ref[...]).astype(BF16)

    return _matmul_rows("mm_out_mid", mixin, w_out, "nn", epilogue, [x], [pmw, pfw], [(d, F32), (d, F32), (d, BF16)],
                        tk=2 * d)


def _ffn_act(u_pre, cw, cb, seq, tm):
    t = u_pre.shape[0]

    def body(i, ug_ref, ugh_ref, uu_ref, uuh_ref, cw_ref, cb_ref, act_ref, u_ref):
        keep, _ = _seq_flags(i, seq, tm)
        gate = _conv(_conv_taps(ug_ref[...], ugh_ref[...] * keep, 3), cw_ref[:, 0:D_FF]) + cb_ref[:, 0:D_FF]
        up = _conv(_conv_taps(uu_ref[...], uuh_ref[...] * keep, 3), cw_ref[:, D_FF:2 * D_FF]) + cb_ref[:, D_FF:2 * D_FF]
        u_ref[:, 0:D_FF] = gate
        u_ref[:, D_FF:2 * D_FF] = up
        act_ref[...] = (gate * _sigmoid(gate) * up).astype(BF16)

    ins = [("row", u_pre, D_FF, 0), ("prev", u_pre, D_FF, 0), ("row", u_pre, D_FF, 1), ("prev", u_pre, D_FF, 1),
           ("full", cw), ("full", cb)]
    return _rowwise("ffn_act", body, t, tm, ins, [(D_FF, BF16), (2 * D_FF, F32)])


def _down_final(act, w_down, x1, tgt, w):
    d = D_MODEL

    def epilogue(f, x1_ref, t_ref, w_ref, dy_ref, df_ref, loss_ref, dw_ref):
        fh, r = _rms(f, d)
        e = x1_ref[...] + fh * w_ref[...] - t_ref[...]
        loss_ref[...] += _colsum(e * e) * (0.5 / d)
        dy = e * (1.0 / d)
        dy_ref[...] = dy
        dw_ref[...] += _colsum(dy * fh)
        df_ref[...] = _rms_bwd(fh, r, dy * w_ref[...], d).astype(BF16)

    return _matmul_rows("mm_down_final", act, w_down, "nn", epilogue, [x1, tgt], [w], [(d, F32), (d, BF16)],
                        accs=[(1, d), (1, d)], tk=D_FF)


def _ffn_bwd(u, u_pre, dact, cw, seq, tm):
    t = u.shape[0]

    def body(i, g_ref, gn_ref, up_ref, upn_ref, xg_ref, xu_ref, da_ref, dan_ref, cw_ref, dpre_ref, dcw_ref, dcb_ref):
        _, keep_next = _seq_flags(i, seq, tm)
        ext = lambda a_ref, n_ref: jnp.concatenate([a_ref[...], n_ref[...]], axis=0)
        rows = tm + SUBLANES
        gate, up = ext(g_ref, gn_ref), ext(up_ref, upn_ref)
        sg = _sigmoid(gate)
        da = jnp.concatenate([da_ref[...], dan_ref[...] * keep_next], axis=0)
        for off, grad, x_ref in ((0, da * up * _dsilu(gate, sg), xg_ref), (D_FF, da * gate * sg, xu_ref)):
            x = x_ref[...]
            own = grad[0:tm]
            acc = own * cw_ref[2:3, off:off + D_FF]
            dcb_ref[:, off:off + D_FF] += _colsum(own)
            dcw_ref[2:3, off:off + D_FF] += _colsum(own * x)
            for j in (1, 2):
                ahead = pltpu.roll(grad, rows - j, 0)[0:tm]
                acc = acc + ahead * cw_ref[2 - j:3 - j, off:off + D_FF]
                dcw_ref[2 - j:3 - j, off:off + D_FF] += _colsum(ahead * x)
            dpre_ref[:, off:off + D_FF] = acc.astype(BF16)

    ins = []
    for cb_ in range(2):
        ins += [("row", u, D_FF, cb_), ("next", u, D_FF, cb_)]
    ins += [("row", u_pre, D_FF, 0), ("row", u_pre, D_FF, 1), ("row", dact, D_FF, 0), ("next", dact, D_FF, 0), ("full", cw)]
    return _rowwise("ffn_bwd", body, t, tm, ins, [(2 * D_FF, BF16)], accs=[(SUBLANES, 2 * D_FF), (1, 2 * D_FF)])


def _assemble_dproj(dpre_qkv, dza, dzs, dpre_xbc, dsm, proj, gcw, scw, seq, tm):
    t = dza.shape[0]
    d = D_MODEL

    def body(i, dq_ref, dqn_ref, dk_ref, dkn_ref, dv_ref, dvn_ref, dx_ref, dxn_ref, dbc_ref, dbcn_ref, dza_ref, dzs_ref,
             dsm_ref, xq_ref, xk_ref, xv_ref, xx_ref, xbc_ref, gcw_ref, scw_ref, o_ref, dgcw_ref, dscw_ref):
        _, keep = _seq_flags(i, seq, tm)
        pieces = [(g_ref, n_ref, gcw_ref, dgcw_ref, x_ref, 0, c0) for g_ref, n_ref, x_ref, c0 in (
            (dq_ref, dqn_ref, xq_ref, 0), (dk_ref, dkn_ref, xk_ref, d), (dv_ref, dvn_ref, xv_ref, 2 * d))]
        pieces += [(g_ref, n_ref, scw_ref, dscw_ref, x_ref, 5 * d, c0) for g_ref, n_ref, x_ref, c0 in (
            (dx_ref, dxn_ref, xx_ref, 0), (dbc_ref, dbcn_ref, xbc_ref, d))]
        for d_ref, n_ref, cw_ref, dcw_ref, x_ref, base, c0 in pieces:
            w = x_ref.shape[1]
            x = x_ref[...]
            g = d_ref[...]
            halo = n_ref[...] * keep
            acc = g * cw_ref[3:4, c0:c0 + w]
            dcw_ref[3:4, c0:c0 + w] += _colsum(g * x)
            for j in range(1, 4):
                ahead = _shift_up(g, halo, j)
                acc = acc + ahead * cw_ref[3 - j:4 - j, c0:c0 + w]
                dcw_ref[3 - j:4 - j, c0:c0 + w] += _colsum(ahead * x)
            o_ref[:, base + c0:base + c0 + w] = acc.astype(BF16)
        o_ref[:, 3 * d:4 * d] = dza_ref[...]
        o_ref[:, 4 * d:5 * d] = dzs_ref[...]
        o_ref[:, 6 * d + 512:6 * d + 512 + LANES] = dsm_ref[...]
        o_ref[:, 6 * d + 512 + LANES:PROJ_W] = jnp.zeros((tm, PROJ_W - (6 * d + 512 + LANES)), BF16)

    ins = []
    for g in tuple(dpre_qkv) + tuple(dpre_xbc):
        ins += [("row", g, g.shape[1], 0), ("next", g, g.shape[1], 0)]
    ins += [("row", dza, d, 0), ("row", dzs, d, 0), ("row", dsm, LANES, 0),
           ("row", proj, d, 0), ("row", proj, d, 1), ("row", proj, d, 2), ("row", proj, d, 5), ("row", proj, 512, 12),
           ("full", gcw), ("full", scw)]
    return _rowwise("assemble_dproj", body, t, tm, ins, [(PROJ_W, BF16)], accs=[(SUBLANES, 3 * d), (SUBLANES, d + 512)])


def _dh2_mid_bwd(du_pre, w_up, x1, mix, dy, pmw, pfw):
    d = D_MODEL

    def epilogue(dh2, x1_ref, mix_ref, dy_ref, pmw_ref, pfw_ref, dx1_ref, dmix_ref, dpm_ref, dpf_ref):
        xh, r2 = _rms(x1_ref[...], d)
        dpf_ref[...] += _colsum(dh2 * xh)
        dx1 = dy_ref[...] + _rms_bwd(xh, r2, dh2 * pfw_ref[...], d)
        dx1_ref[...] = dx1
        mh, r = _rms(mix_ref[...], d)
        dpm_ref[...] += _colsum(dx1 * mh)
        dmix_ref[...] = _rms_bwd(mh, r, dx1 * pmw_ref[...], d).astype(BF16)

    return _matmul_rows("mm_dh2_mid_bwd", du_pre, w_up, "nt", epilogue, [x1, mix, dy], [pmw, pfw],
                        [(d, F32), (d, BF16)], accs=[(1, d), (1, d)], tk=D_FF)


def _dmixin_gate_norm_bwd(dmix, w_out, o_gdn, y_ssd, xs, proj, gnw, snw, dvec):
    d = D_MODEL

    def epilogue(dmixin, o_ref, za_ref, y_ref, xs_ref, zs_ref, gnw_ref, snw_ref, dv_ref,
                 do_ref, dza_ref, dy_ref, dxs_ref, dzs_ref, dgnw_ref, dsnw_ref, dd_ref):
        for hh in range(GDN_HEADS):
            sl = slice(hh * GDN_DK, (hh + 1) * GDN_DK)
            oh, r = _rms(o_ref[:, sl], GDN_DK)
            z = za_ref[:, sl]
            sz = _sigmoid(z)
            dm = dmixin[:, sl]
            don = dm * (z * sz)
            dza_ref[:, sl] = (dm * oh * gnw_ref[...] * _dsilu(z, sz)).astype(BF16)
            dgnw_ref[...] += _colsum(don * oh)
            do_ref[:, sl] = _rms_bwd(oh, r, don * gnw_ref[...], GDN_DK)
        zs = zs_ref[...]
        sz = _sigmoid(zs)
        sil = zs * sz
        x = xs_ref[...]
        y0 = y_ref[...] + dv_ref[...] * x
        yg = y0 * sil
        dms = dmixin[:, d:2 * d]
        for g in range(SSD_GROUPS):
            sl = slice(g * 512, (g + 1) * 512)
            yh, r = _rms(yg[:, sl], 512)
            dsnw_ref[:, sl] += _colsum(dms[:, sl] * yh)
            dyg = _rms_bwd(yh, r, dms[:, sl] * snw_ref[:, sl], 512)
            dy0 = dyg * sil[:, sl]
            dzs_ref[:, sl] = (dyg * y0[:, sl] * _dsilu(zs[:, sl], sz[:, sl])).astype(BF16)
            dy_ref[:, sl] = dy0
            dxs_ref[:, sl] = dy0 * dv_ref[:, sl]
            dd_ref[:, sl] += _colsum(dy0 * x[:, sl])

    row_ins = [o_gdn, (proj, d, 3), y_ssd, xs, (proj, d, 4)]
    return _matmul_rows("mm_dmixin_gate_norm_bwd", dmix, w_out, "nt", epilogue, row_ins, [gnw, snw, dvec],
                        [(d, F32), (d, BF16), (d, F32), (d, F32), (d, BF16)], accs=[(1, GDN_DK), (1, d), (1, d)], tm=256)


def _ssd_chunk_bwd(xs, bc, dtx, acsx, acs, ar, dy, sts, ypre, dxs_d, bsz, seq, sb):
    nsb, ncb, _, sp = _ssd_specs(seq, sb)
    bc_w = 2 * SSD_GROUPS * SSD_N
    x_w = SSD_HEADS * SSD_P

    def body(x_ref, dtx_ref, ax_ref, bc_ref, acs_ref, ar_ref, dy_ref, sts_ref, yx_ref, ybc_ref, dxd_ref,
             dx_ref, dbc_ref, ddt_ref, dacs_ref, dbx_ref, dbbc_ref, dst_scr):
        @pl.when(pl.program_id(1) == 0)
        def _():
            dst_scr[...] = jnp.zeros_like(dst_scr)

        @pl.when((pl.program_id(0) == 0) & (pl.program_id(1) == 0))
        def _():
            dbx_ref[...] = jnp.zeros_like(dbx_ref)
            dbbc_ref[...] = jnp.zeros_like(dbbc_ref)

        def to_conv_out(grad, y):
            return grad * _dsilu(y, _sigmoid(y))

        c = _chunk_consts()
        lane5 = lax.broadcasted_iota(jnp.int32, (CHUNK, SSD_GW), 1) // SSD_P
        row5 = lax.broadcasted_iota(jnp.int32, (CHUNK, SSD_GW), 0)
        sel_in = lax.broadcasted_iota(jnp.int32, (SSD_GW, LANES), 0) // SSD_P
        sel_out = lax.broadcasted_iota(jnp.int32, (SSD_GW, LANES), 1)

        def chunk(nn, carry):
            n = ncb - 1 - nn
            r = pl.ds(pl.multiple_of(n * CHUNK, CHUNK), CHUNK)
            rs = pl.ds(pl.multiple_of(n * SSD_N, SSD_N), SSD_N)
            acsv = acs_ref[0, r, :]
            ddt = jnp.zeros((CHUNK, LANES), F32)
            dacs = jnp.zeros((CHUNK, LANES), F32)
            for g in range(SSD_GROUPS):
                gl = slice(g * SSD_GW, (g + 1) * SSD_GW)
                x, dt, ax, dyv = x_ref[0, r, gl], dtx_ref[0, r, gl], ax_ref[0, r, gl], dy_ref[0, r, gl]
                bm = bc_ref[0, r, g * SSD_N:(g + 1) * SSD_N]
                cm = bc_ref[0, r, (SSD_GROUPS + g) * SSD_N:(SSD_GROUPS + g + 1) * SSD_N]
                st = sts_ref[0, rs, gl]
                dst = dst_scr[:, gl]
                rsel = (sel_in + (16 + g * SSD_HPG) == sel_out).astype(F32)
                xdt = x * dt
                cbm = _mm(cm, bm, NT)
                al = ax[CHUNK - 1:CHUNK, :]
                ex, el = jnp.exp(ax), jnp.exp(al)
                dec = jnp.exp(al - ax)
                xd = xdt * dec
                dye = dyv * ex
                dxd = _mm(bm, dst)
                dxdt = dec * dxd
                dcm = _mm(dye, st, NT)
                dbm = _mm(xd, dst, NT)
                z = dye * _mm(cm, st) - dxd * xd
                zl = _colsum(dst * st) * el + _colsum(dxd * xd)
                z = z + jnp.where(row5 == CHUNK - 1, zl, 0.0)
                dcb = jnp.zeros((CHUNK, CHUNK), F32)
                for hh in range(SSD_HPG):
                    head = g * SSD_HPG + hh
                    lm, gm = _ssd_head(acsv, ar_ref, n, head, cbm, c)
                    dym = jnp.where(lane5 == hh, dyv, 0.0)
                    dxdt = dxdt + _mm(gm, dym, TN)
                    dg = _mm(dym, xdt, NT)
                    dcb = dcb + dg * lm
                    pm = dg * gm
                    dacs = dacs + jnp.where(c["lane"] == head + 16, _rowsum(pm) - _mmsel(pm, c["ones"], TN), 0.0)
                for sl, grad in ((slice((SSD_GROUPS + g) * SSD_N, (SSD_GROUPS + g + 1) * SSD_N), dcm + _mm(dcb, bm)),
                                 (slice(g * SSD_N, (g + 1) * SSD_N), dbm + _mm(dcb, cm, TN))):
                    dpre = to_conv_out(grad, ybc_ref[0, r, sl])
                    dbc_ref[0, r, sl] = dpre
                    dbbc_ref[:, sl] += _colsum(dpre)
                dacs = dacs + _mmsel(z, rsel)
                ddt = ddt + _mmsel(dxdt * x, rsel)
                dpre = to_conv_out(dxdt * dt + dxd_ref[0, r, gl], yx_ref[0, r, gl])
                dx_ref[0, r, gl] = dpre
                dbx_ref[:, gl] += _colsum(dpre)
                dst_scr[:, gl] = dst * el + _mm(cm, dye, TN)
            ddt_ref[0, r, :] = ddt
            dacs_ref[0, r, :] = dacs
            return carry

        lax.fori_loop(0, ncb, chunk, 0)

    nsb_rev = lambda j: nsb - 1 - j
    return _pcall(
        body, name="ssd_chunk_bwd", grid=(bsz, nsb),
        in_specs=[sp["wide"](), sp["wide"](), sp["wide"](), sp["bc"](), sp["small"](), sp["ar"], sp["wide"](), sp["st"],
                  sp["wide"](), pl.BlockSpec((1, sb, bc_w), lambda b, j: (b, nsb_rev(j), x_w // bc_w)), sp["wide"]()],
        out_specs=[sp["wide"](), sp["bc"](), sp["small"](), sp["small"](),
                   pl.BlockSpec((1, x_w), lambda b, j: (0, 0)), pl.BlockSpec((1, bc_w), lambda b, j: (0, 0))],
        out_shape=[jax.ShapeDtypeStruct((bsz, seq, x_w), F32), jax.ShapeDtypeStruct((bsz, seq, bc_w), F32),
                   jax.ShapeDtypeStruct((bsz, seq, LANES), F32), jax.ShapeDtypeStruct((bsz, seq, LANES), F32),
                   jax.ShapeDtypeStruct((1, x_w), F32), jax.ShapeDtypeStruct((1, bc_w), F32)],
        scratch_shapes=[pltpu.VMEM((SSD_N, x_w), F32)],
        compiler_params=_params(("arbitrary", "arbitrary")),
    )(xs, dtx, acsx, bc, acs, ar, dy, sts, ypre, ypre, dxs_d)


def _ssd_gates_bwd(proj, ddt, dacs, dsm_gdn, sp, tm):
    t = proj.shape[0]

    def body(i, sm_ref, ddt_ref, dacs_ref, dsg_ref, sp_ref, dsm_ref, dsp_ref):
        sm = sm_ref[...]
        lane = lax.broadcasted_iota(jnp.int32, sm.shape, 1)
        valid = (lane >= 16) & (lane < 32)
        xb = sm + sp_ref[1:2, :]
        dt = jnp.where(valid, _softplus(xb), 0.0)
        a_neg = -jnp.exp(sp_ref[0:1, :])
        dadt_s = _mmx(_block_tri(tm, True), dacs_ref[...])
        dxb = jnp.where(valid, (ddt_ref[...] + dadt_s * a_neg) * _sigmoid(xb), 0.0)
        dsm_ref[...] = (dsg_ref[...] + dxb).astype(BF16)
        dsp_ref[1:2, :] += _colsum(dxb)
        dsp_ref[0:1, :] += jnp.where(valid[0:1, :], _colsum(dadt_s * dt) * a_neg, 0.0)

    ins = [("row", proj, LANES, SMALL_CB), ("row", ddt, LANES, 0), ("row", dacs, LANES, 0), ("row", dsm_gdn, LANES, 0),
           ("full", sp)]
    return _rowwise("ssd_gates_bwd", body, t, tm, ins, [(LANES, BF16)], accs=[(SUBLANES, LANES)])


def _through_norm_silu(g, y, scale):
    sy = _sigmoid(y)
    ds_ = _dsilu(y, sy)
    if scale is None:
        return g * ds_
    a = y * sy
    n = lax.rsqrt(_rowsum(a * a) + EPS)
    ah = a * n
    return (scale * n) * (g - ah * _rowsum(g * ah)) * ds_


def _gdn_chunk_bwd(qn, kn, vv, gs, gr, do, sts, tis, ypre, bsz, seq, sb, riders):
    hb = GDN_HB
    nsb, ncb, sp = _gdn_specs(seq, sb, hb, True)
    grid = (bsz, GDN_HEADS // hb, nsb)
    any_spec, rider_shapes, rider_sems, wrap = _riding_exchange(riders, True, 11, 4, grid)
    ypre_spec = lambda cb: pl.BlockSpec((1, sb, hb * GDN_DK), lambda b, h, j: (b, nsb - 1 - j, cb))

    def body(q_ref, k_ref, v_ref, gs_ref, gr_ref, do_ref, st_ref, ti_ref, yq_ref, yk_ref, yv_ref,
             dq_ref, dk_ref, dv_ref, dgs_ref, ds_scr):
        @pl.when(pl.program_id(2) == 0)
        def _():
            ds_scr[...] = jnp.zeros_like(ds_scr)

        c = _chunk_consts()

        def chunk(nn, carry):
            n = ncb - 1 - nn
            r = pl.ds(pl.multiple_of(n * CHUNK, CHUNK), CHUNK)
            rs = pl.ds(pl.multiple_of(n * GDN_DK, GDN_DK), GDN_DK)
            gsv = gs_ref[0, r, :]
            heads = list(range(hb))
            sls = [slice(ih * GDN_DK, (ih + 1) * GDN_DK) for ih in heads]
            q = [q_ref[0, r, sl] for sl in sls]
            k = [k_ref[0, r, sl] for sl in sls]
            v = [v_ref[0, r, sl] for sl in sls]
            do_ = [do_ref[0, r, sl] for sl in sls]
            s = [st_ref[0, ih, rs, :] for ih in heads]
            tinv = [ti_ref[0, ih, r, :] for ih in heads]
            dsn = [ds_scr[ih] for ih in heads]
            beta, dc, eg, egl, ekd = zip(*[
                _gdn_gates(gsv, gr_ref[0, n, pl.ds(ih, 1), :], ih, c) for ih in heads])
            mul = lambda a, b: a * b
            kb = _hmap(mul, k, beta)
            rhs_w = _hmap(mul, kb, eg)
            u = _hmap(lambda t_, a, b: _mm3(t_, a * b), tinv, v, beta)
            w = _hmap(_mm3, tinv, rhs_w)
            amat = _hmap(lambda a, b, d_: jnp.where(c["strict"], _mm(a, b, NT) * d_, 0.0), kb, k, dc)
            qk = _hmap(lambda a, b, d_: _mm(a, b, NT) * d_, q, k, dc)
            qd = _hmap(mul, q, eg)
            kd = _hmap(mul, k, ekd)
            v_new = _hmap(lambda a, b, s_: a - _mm(b, s_), u, w, s)
            dv_new = _hmap(lambda qk_, d_, kd_, dn: _mm(qk_, d_, TN) + _mm(kd_, dn), qk, do_, kd, dsn)
            dqk = _hmap(lambda d_, vn: _mm(d_, vn, NT), do_, v_new)
            dqd = _hmap(lambda d_, s_: _mm(d_, s_, NT), do_, s)
            ds_new = _hmap(lambda qd_, d_, dn, e, w_, dvn: _mm(qd_, d_, TN) + dn * e - _mm(w_, dvn, TN),
                           qd, do_, dsn, egl, w, dv_new)
            dkd = _hmap(lambda vn, dn: _mm(vn, dn, NT), v_new, dsn)
            dgl = _hmap(lambda s_, dn, e: _colsum(_rowsum(s_ * dn)) * e, s, dsn, egl)
            dw = _hmap(lambda dvn, s_: -_mm(dvn, s_, NT), dv_new, s)
            dru = _hmap(lambda t_, a: _mm3(t_, a, TN), tinv, dv_new)
            drw = _hmap(lambda t_, a: _mm3(t_, a, TN), tinv, dw)
            da = _hmap(lambda a, u_, b, w_: jnp.where(c["strict"], -(_mm(a, u_, NT) + _mm(b, w_, NT)), 0.0), dru, u, drw, w)
            m = _hmap(mul, da, dc)
            dkb = _hmap(lambda a, e, m_, k_: a * e + _mm(m_, k_), drw, eg, m, k)
            mq = _hmap(mul, dqk, dc)
            dq = _hmap(lambda mq_, k_, a, e: _mm(mq_, k_) + a * e, mq, k, dqd, eg)
            dk = _hmap(lambda m_, kb_, mq_, q_, a, e, b, be: _mm(m_, kb_, TN) + _mm(mq_, q_, TN) + a * e + b * be,
                       m, kb, mq, q, dkd, ekd, dkb, beta)
            dbeta = _hmap(lambda a, v_, b, k_: _rowsum(a * v_) + _rowsum(b * k_), dru, v, dkb, k)
            pq = _hmap(lambda a, am, b, qk_: a * am + b * qk_, da, amat, dqk, qk)
            ekk = _hmap(lambda a, b: _rowsum(a * b), dkd, kd)
            dgc = _hmap(lambda pq_, a, rw, b, qd_, e, gl_: (
                _rowsum(pq_) - _mmsel(pq_, c["ones"], TN) + (_rowsum(a * rw) + _rowsum(b * qd_) - e)
                + jnp.where(c["row1"] == CHUNK - 1, _colsum(e) + gl_, 0.0)), pq, drw, rhs_w, dqd, qd, ekk, dgl)
            dv = _hmap(mul, dru, beta)
            dyq = _hmap(lambda g_, sl: _through_norm_silu(g_, yq_ref[0, r, sl], GDN_DK ** -0.5), dq, sls)
            dyk = _hmap(lambda g_, sl: _through_norm_silu(g_, yk_ref[0, r, sl], 1.0), dk, sls)
            dyv = _hmap(lambda g_, sl: _through_norm_silu(g_, yv_ref[0, r, sl], None), dv, sls)
            dgs = jnp.zeros((CHUNK, LANES), F32)
            for ih in heads:
                ds_scr[ih] = ds_new[ih]
                dq_ref[0, r, sls[ih]] = dyq[ih]
                dk_ref[0, r, sls[ih]] = dyk[ih]
                dv_ref[0, r, sls[ih]] = dyv[ih]
                dgs = dgs + jnp.where(c["lane"] == ih, dbeta[ih], jnp.where(c["lane"] == ih + 8, dgc[ih], 0.0))
            dgs_ref[0, r, :] = dgs
            return carry

        lax.fori_loop(0, ncb, chunk, 0)

    res = _pcall(
        wrap(body), name="gdn_chunk_bwd", grid=grid,
        in_specs=[sp["wide"](), sp["wide"](), sp["wide"](), sp["gs"](), sp["gr"], sp["wide"](), sp["st"], sp["ti"],
                  ypre_spec(0), ypre_spec(1), ypre_spec(2)] + any_spec,
        out_specs=[sp["wide"](), sp["wide"](), sp["wide"](), sp["gs"]()] + any_spec,
        out_shape=[jax.ShapeDtypeStruct((bsz, seq, D_MODEL), F32)] * 3 + [jax.ShapeDtypeStruct((bsz, seq, LANES), F32)]
        + rider_shapes,
        scratch_shapes=[pltpu.VMEM((hb, GDN_DK, GDN_DK), F32)] + rider_sems,
        compiler_params=_params(("arbitrary", "arbitrary", "arbitrary")),
    )(qn, kn, vv, gs, gr, do, sts, tis, ypre, ypre, ypre, *riders)
    return res[:4], res[4:]


def _gdn_gates_bwd(proj, dgs, gp, tm):
    t = proj.shape[0]

    def body(i, sm_ref, dgs_ref, gp_ref, dsm_ref, dgp_ref):
        sm = sm_ref[...]
        lane = lax.broadcasted_iota(jnp.int32, sm.shape, 1)
        dsel = dgs_ref[...]
        is_g = (lane >= 8) & (lane < 16)
        dsel = jnp.where(is_g, _mmx(_block_tri(tm, True), dsel), dsel)
        beta = _sigmoid(sm)
        xb = sm + gp_ref[1:2, :]
        a_neg = -jnp.exp(gp_ref[0:1, :])
        sp = _softplus(xb)
        dxb = jnp.where(is_g, dsel * a_neg * _sigmoid(xb), 0.0)
        dsm_ref[...] = jnp.where(lane < 8, dsel * beta * (1.0 - beta), dxb)
        dgp_ref[1:2, :] += _colsum(dxb)
        dgp_ref[0:1, :] += _colsum(jnp.where(is_g, dsel * a_neg * sp, 0.0))

    ins = [("row", proj, LANES, SMALL_CB), ("row", dgs, LANES, 0), ("full", gp)]
    return _rowwise("gdn_gates_bwd", body, t, tm, ins, [(LANES, F32)], accs=[(SUBLANES, LANES)])


def _dh1_first_bwd(dproj, wp_in, x, dx1, w, scatter_riders):
    d = D_MODEL

    def epilogue(dh, x_ref, dx1_ref, w_ref, dx_ref, dw_ref):
        xh, r = _rms(x_ref[...], d)
        dw_ref[...] += _colsum(dh * xh)
        dx_ref[...] = dx1_ref[...] + _rms_bwd(xh, r, dh * w_ref[...], d)

    return _matmul_rows("mm_dh1_first_bwd", dproj, wp_in, "nt", epilogue, [x, dx1], [w], [(d, F32)], accs=[(1, d)],
                        tk=PROJ_W // 2, scatter_riders=scatter_riders)


def _gather_two_level(name, arrays):
    n = len(arrays)
    n_sem = 7

    def body(*refs):
        ins, outs = refs[:n], refs[n:2 * n]
        send_sems, recv_sems, loc_sems = refs[2 * n:]
        x, y, c = lax.axis_index("x"), lax.axis_index("y"), lax.axis_index("c")
        slot = lambda px, py, pc: 4 * px + 2 * py + pc
        sibling = (x, y, 1 - c)
        chips = [(1 - x, y), (x, 1 - y), (1 - x, 1 - y)]

        def copy(t, k, src, block, to):
            return pltpu.make_async_remote_copy(
                src_ref=src, dst_ref=outs[t].at[block], send_sem=send_sems.at[t, k], recv_sem=recv_sems.at[t, k],
                device_id=to, device_id_type=pl.DeviceIdType.MESH)

        own, first, passed = [], [], []
        for t in range(n):
            own.append(pltpu.make_async_copy(ins[t], outs[t].at[slot(x, y, c)], loc_sems.at[t]))
            first.append(copy(t, 0, ins[t], slot(x, y, c), sibling))
            first += [copy(t, 1 + j, ins[t], slot(x, y, c), (px, py, c)) for j, (px, py) in enumerate(chips)]
        for cp in own + first:
            cp.start()
        for t in range(n):
            for j, (px, py) in enumerate(chips):
                copy(t, 1 + j, ins[t], slot(px, py, c), (px, py, c)).wait_recv()
                fwd = copy(t, 4 + j, outs[t].at[slot(px, py, c)], slot(px, py, c), sibling)
                fwd.start()
                passed.append(fwd)
        for t in range(n):
            copy(t, 0, ins[t], slot(x, y, 1 - c), sibling).wait_recv()
            for j, (px, py) in enumerate(chips):
                copy(t, 4 + j, ins[t], slot(px, py, 1 - c), sibling).wait_recv()
        for cp in first + passed:
            cp.wait_send()
        for cp in own:
            cp.wait()

    return _pcall(
        body, name=name,
        in_specs=[pl.BlockSpec(memory_space=pl.ANY)] * n,
        out_specs=[pl.BlockSpec(memory_space=pl.ANY)] * n,
        out_shape=_exchange_out_shapes(arrays, False),
        scratch_shapes=[pltpu.SemaphoreType.DMA((n, n_sem)), pltpu.SemaphoreType.DMA((n, n_sem)), pltpu.SemaphoreType.DMA((n,))],
    )(*arrays)


def _exchange_out_shapes(arrays, scatter):
    return [jax.ShapeDtypeStruct(a.shape if scatter else (N_DEV,) + a.shape, a.dtype) for a in arrays]


def _exchange_sems(n):
    return [pltpu.SemaphoreType.DMA((n, N_DEV - 1)), pltpu.SemaphoreType.DMA((n, N_DEV - 1)), pltpu.SemaphoreType.DMA((n,))]


def _exchange_phase(ins, outs, sems, scatter, start):
    send_sems, recv_sems, loc_sems = sems
    x, y, c = lax.axis_index("x"), lax.axis_index("y"), lax.axis_index("c")
    me = 4 * x + 2 * y + c
    for t in range(len(ins)):
        loc = pltpu.make_async_copy(ins[t].at[me] if scatter else ins[t], outs[t].at[me], loc_sems.at[t])
        if start:
            loc.start()
        else:
            loc.wait()
        for k in range(N_DEV - 1):
            bx, by, bc = ((k + 1) >> 2) & 1, ((k + 1) >> 1) & 1, (k + 1) & 1
            px = 1 - x if bx else x
            py = 1 - y if by else y
            pc = 1 - c if bc else c
            peer = 4 * px + 2 * py + pc
            src = ins[t].at[peer] if scatter else ins[t]
            copy = lambda dst: pltpu.make_async_remote_copy(
                src_ref=src, dst_ref=dst, send_sem=send_sems.at[t, k], recv_sem=recv_sems.at[t, k],
                device_id=(px, py, pc), device_id_type=pl.DeviceIdType.MESH)
            if start:
                copy(outs[t].at[me]).start()
            else:
                copy(outs[t].at[me]).wait_send()
                copy(outs[t].at[peer]).wait_recv()


def _adam_math(w, g, m, v):
    m = ADAM_B1 * m + (1.0 - ADAM_B1) * g
    v = ADAM_B2 * v + (1.0 - ADAM_B2) * (g * g)
    m_hat = m / (1.0 - ADAM_B1 ** ADAM_STEP)
    v_hat = v / (1.0 - ADAM_B2 ** ADAM_STEP)
    delta = -ADAM_LR * (m_hat / (jnp.sqrt(v_hat) + ADAM_EPS) + ADAM_WD * w)
    return delta, m, v


def _adam_big(name, parts, w, m, v, tm):
    r, c = w.shape
    tm = tm if r % tm == 0 else r

    def body(p_ref, w_ref, m_ref, v_ref, g_ref, d_ref, nm_ref, nv_ref):
        g = p_ref[0].astype(F32)
        for s in range(1, N_DEV):
            g = g + p_ref[s].astype(F32)
        g_ref[...] = g
        d_ref[...], nm_ref[...], nv_ref[...] = _adam_math(w_ref[...], g, m_ref[...], v_ref[...])

    blk = lambda: pl.BlockSpec((tm, c), lambda i: (i, 0))
    return _pcall(
        body, name=name, grid=(r // tm,),
        in_specs=[pl.BlockSpec((N_DEV, tm, c), lambda i: (0, i, 0)), blk(), blk(), blk()],
        out_specs=[blk(), blk(), blk(), blk()],
        out_shape=[jax.ShapeDtypeStruct((r, c), F32)] * 4,
        compiler_params=_params(("parallel",)),
    )(parts, w, m, v)


SMALL_ROWS = 56
ROW_DD, ROW_LOSS = 5, 6


def _small_sum(gathered):
    def body(g_ref, o_ref, x_ref):
        s = g_ref[0]
        for dev in range(1, N_DEV):
            s = s + g_ref[dev]
        o_ref[...] = s
        ri = lax.broadcasted_iota(jnp.int32, (D_MODEL, LANES), 0)
        ro = lax.broadcasted_iota(jnp.int32, (D_MODEL, LANES), 1)
        heads = _mmx(jnp.broadcast_to(s[ROW_DD:ROW_DD + 1, :], (SUBLANES, D_MODEL)), (ri // SSD_P == ro).astype(F32))
        loss = _rowsum(jnp.broadcast_to(s[ROW_LOSS:ROW_LOSS + 1, :], (SUBLANES, D_MODEL)))
        row = lax.broadcasted_iota(jnp.int32, (SUBLANES, LANES), 0)
        x_ref[...] = jnp.where(row == 0, heads, jnp.broadcast_to(loss, (SUBLANES, LANES)))

    return _pcall(
        body, name="small_sum",
        out_shape=[jax.ShapeDtypeStruct((SMALL_ROWS, D_MODEL), F32), jax.ShapeDtypeStruct((SUBLANES, LANES), F32)],
        compiler_params=_params(None),
    )(gathered)


def _adam_small(g, w, m, v):
    def body(g_ref, w_ref, m_ref, v_ref, d_ref, nm_ref, nv_ref):
        d_ref[...], nm_ref[...], nv_ref[...] = _adam_math(w_ref[...], g_ref[...], m_ref[...], v_ref[...])

    return _pcall(body, name="adam_small", out_shape=[jax.ShapeDtypeStruct(g.shape, F32)] * 3,
                  compiler_params=_params(None))(g, w, m, v)


def _pack(pieces, rows):
    flat = jnp.concatenate([p.reshape(-1).astype(F32) for p in pieces])
    return jnp.pad(flat, (0, rows * D_MODEL - flat.shape[0])).reshape(rows, D_MODEL)


def _unpack(packed, shapes):
    flat = packed.reshape(-1)
    out, off = [], 0
    for shp in shapes:
        size = 1
        for s in shp:
            size *= s
        out.append(flat[off:off + size].reshape(shp))
        off += size
    return out


def _permute_in(w):
    pad = jnp.zeros((w.shape[0], PROJ_W - D_IN), w.dtype)
    return jnp.concatenate([w[:, 0:4096], w[:, 4112:6672], w[:, 4096:4112], w[:, 6672:6688], pad], axis=1)


def _unpermute_in(g):
    return jnp.concatenate([g[:, 0:4096], g[:, 6656:6672], g[:, 4096:6656], g[:, 6672:6688]], axis=1)


def _lane_row(vec, start):
    return jnp.zeros((LANES,), F32).at[start:start + vec.shape[0]].set(vec)


def _cols_from_shards(g):
    return jnp.transpose(g, (1, 0, 2)).reshape(g.shape[1], N_DEV * g.shape[2])


def _cols_to_shards(a):
    return jnp.transpose(a.astype(BF16).reshape(a.shape[0], N_DEV, a.shape[1] // N_DEV), (1, 0, 2))


def _rows_to_shards(a):
    return a.astype(BF16).reshape(N_DEV, a.shape[0] // N_DEV, a.shape[1])


def _local_step(x, tgt, wp_in, rest, p, rest_is_sharded):
    bsz, seq, d = x.shape
    t = bsz * seq
    x2 = x.reshape(t, d)
    tgt2 = tgt.reshape(t, d)
    tm = min(256, seq)
    tm_wide = min(128, seq)
    sb = min(512, seq)

    gp = jnp.zeros((SUBLANES, LANES), F32).at[0].set(_lane_row(p["gdn_a_log"], 8)).at[1].set(_lane_row(p["gdn_dt_bias"], 8))
    sp = jnp.zeros((SUBLANES, LANES), F32).at[0].set(_lane_row(p["ssd_a_log"], 16)).at[1].set(_lane_row(p["ssd_dt_bias"], 16))
    dvec = jnp.repeat(p["ssd_d"], SSD_P).reshape(1, d)
    row = lambda v: v.reshape(1, -1)
    pre_mix, post_mix, pre_ffn, post_ffn = (row(p[k]) for k in ("pre_mix_norm", "post_mix_norm", "pre_ffn_norm", "post_ffn_norm"))
    gnw, snw = row(p["gdn_norm_w"]), row(p["ssd_norm_w"])
    gcw, scw, scb, fcw, fcb = p["gdn_conv_w"], p["ssd_conv_w"], row(p["ssd_conv_b"]), p["ffn_conv_w"], row(p["ffn_conv_b"])

    h1 = _norm_cast("norm_in", x2, pre_mix, tm)
    proj = _matmul("mm_proj", h1, wp_in, "nn", F32)
    b3 = lambda a: a.reshape(bsz, seq, a.shape[-1])
    b2 = lambda a: a.reshape(t, a.shape[-1])
    rows_of = lambda a, lo, n: jnp.transpose(a[:, lo:lo + n].reshape(bsz, seq // CHUNK, CHUNK, n), (0, 1, 3, 2))
    qn, kn, vv, gs, ypre_gdn = _gdn_prep(proj, gcw, gp, seq, tm)
    gr = rows_of(gs, 8, GDN_HEADS)
    qn, kn, vv, gs = b3(qn), b3(kn), b3(vv), b3(gs)
    (o_gdn, gdn_st, gdn_ti), gathered = _gdn_chunk_fwd(qn, kn, vv, gs, gr, bsz, seq, sb, list(rest) if rest_is_sharded else [])
    if rest_is_sharded:
        w_out, w_up, w_down = gathered[0].reshape(-1, d), _cols_from_shards(gathered[1]), gathered[2].reshape(-1, d)
    else:
        w_out, w_up, w_down = rest
    o_gdn = b2(o_gdn)
    xs, bc, dtx, acsx, acs, ypre_ssd = _ssd_prep(proj, scw, scb, sp, seq, tm)
    ar = rows_of(acs, 16, SSD_HEADS)
    y_ssd, ssd_st = _ssd_chunk_fwd(b3(xs), b3(bc), b3(dtx), b3(acsx), b3(acs), ar, bsz, seq, sb)
    y_ssd = b2(y_ssd)
    mixin = _gate_norm(o_gdn, y_ssd, xs, proj, gnw, snw, dvec, tm)
    mix, x1, h2 = _out_mid(mixin, w_out, x2, post_mix, pre_ffn)
    u_pre = _matmul("mm_up", h2, w_up, "nn", F32)
    act, u = _ffn_act(u_pre, fcw, fcb, seq, tm_wide)
    dy, df, loss_lanes, d_post_ffn = _down_final(act, w_down, x1, tgt2, post_ffn)

    g_down = _matmul("mm_dw_down", act, df, "tn", BF16, tm=1408, tk=2048)
    dact = _matmul("mm_dact", df, w_down, "nt", F32, tn=1408)
    du_pre, d_fcw, d_fcb = _ffn_bwd(u, u_pre, dact, fcw, seq, tm_wide)
    g_up = _matmul("mm_dw_up", h2, du_pre, "tn", BF16, tk=2048)
    dx1, dmix, d_post_mix, d_pre_ffn = _dh2_mid_bwd(du_pre, w_up, x1, mix, dy, post_mix, pre_ffn)
    g_out = _matmul("mm_dw_out", mixin, dmix, "tn", BF16, tk=2048)
    do_gdn, dza, dy_ssd, dxs_d, dzs, d_gnw, d_snw, d_dd = _dmixin_gate_norm_bwd(dmix, w_out, o_gdn, y_ssd, xs, proj, gnw, snw, dvec)
    dyx, dybc, ddt, dacs, d_scb_x, d_scb_bc = _ssd_chunk_bwd(
        b3(xs), b3(bc), b3(dtx), b3(acsx), b3(acs), ar, b3(dy_ssd), ssd_st, b3(ypre_ssd), b3(dxs_d), bsz, seq, sb)
    dyx, dybc, ddt, dacs = b2(dyx), b2(dybc), b2(ddt), b2(dacs)
    d_scb = jnp.concatenate([d_scb_x, d_scb_bc], axis=1)
    riders = [_rows_to_shards(g_out), _cols_to_shards(g_up), _rows_to_shards(g_down)] if rest_is_sharded else []
    dgdn, received = _gdn_chunk_bwd(qn, kn, vv, gs, gr, b3(do_gdn), gdn_st, gdn_ti, b3(ypre_gdn), bsz, seq, min(256, seq), riders)
    if rest_is_sharded:
        g_out, g_up, g_down = received
    dyq, dyk, dyv, dgs = (b2(a) for a in dgdn)
    dsm_gdn, d_gp = _gdn_gates_bwd(proj, dgs, gp, tm)
    dsm, d_sp = _ssd_gates_bwd(proj, ddt, dacs, dsm_gdn, sp, tm)
    dproj, d_gcw, d_scw = _assemble_dproj((dyq, dyk, dyv), dza, dzs, (dyx, dybc), dsm, proj, gcw, scw, seq, tm)
    g_in = _matmul("mm_dw_in", h1, dproj, "tn", BF16, tk=2048)
    if rest_is_sharded:
        (dx, d_pre_mix), (g_in,) = _dh1_first_bwd(dproj, wp_in, x2, dx1, pre_mix, [_cols_to_shards(_unpermute_in(g_in))])
    else:
        dx, d_pre_mix = _dh1_first_bwd(dproj, wp_in, x2, dx1, pre_mix, [])

    small = dict(pre_mix_norm=d_pre_mix, ssd_norm_w=d_snw, post_mix_norm=d_post_mix, pre_ffn_norm=d_pre_ffn,
                 post_ffn_norm=d_post_ffn, dd_lanes=d_dd, loss_lanes=loss_lanes, gdn_gates=d_gp, ssd_gates=d_sp,
                 gdn_norm_w=d_gnw, gdn_conv_w=d_gcw[0:4], ssd_conv_w=d_scw[0:4], ssd_conv_b=d_scb,
                 ffn_conv_w=d_fcw[0:3], ffn_conv_b=d_fcb)
    return dx.reshape(bsz, seq, d), g_in, g_out, g_up, g_down, small


def kernel(x, pre_mix_norm, w_in, gdn_conv_w, gdn_a_log, gdn_dt_bias, gdn_norm_w, ssd_conv_w, ssd_conv_b, ssd_a_log, ssd_dt_bias, ssd_d, ssd_norm_w, w_out, post_mix_norm, pre_ffn_norm, w_up, ffn_conv_w, ffn_conv_b, w_down, post_ffn_norm, loss_target, m_pre_mix_norm, m_w_in, m_gdn_conv_w, m_gdn_a_log, m_gdn_dt_bias, m_gdn_norm_w, m_ssd_conv_w, m_ssd_conv_b, m_ssd_a_log, m_ssd_dt_bias, m_ssd_d, m_ssd_norm_w, m_w_out, m_post_mix_norm, m_pre_ffn_norm, m_w_up, m_ffn_conv_w, m_ffn_conv_b, m_w_down, m_post_ffn_norm, v_pre_mix_norm, v_w_in, v_gdn_conv_w, v_gdn_a_log, v_gdn_dt_bias, v_gdn_norm_w, v_ssd_conv_w, v_ssd_conv_b, v_ssd_a_log, v_ssd_dt_bias, v_ssd_d, v_ssd_norm_w, v_w_out, v_post_mix_norm, v_pre_ffn_norm, v_w_up, v_ffn_conv_w, v_ffn_conv_b, v_w_down, v_post_ffn_norm):
    names = ["pre_mix_norm", "w_in", "gdn_conv_w", "gdn_a_log", "gdn_dt_bias", "gdn_norm_w", "ssd_conv_w", "ssd_conv_b",
             "ssd_a_log", "ssd_dt_bias", "ssd_d", "ssd_norm_w", "w_out", "post_mix_norm", "pre_ffn_norm", "w_up",
             "ffn_conv_w", "ffn_conv_b", "w_down", "post_ffn_norm"]
    w_args = [pre_mix_norm, w_in, gdn_conv_w, gdn_a_log, gdn_dt_bias, gdn_norm_w, ssd_conv_w, ssd_conv_b, ssd_a_log, ssd_dt_bias, ssd_d, ssd_norm_w, w_out, post_mix_norm, pre_ffn_norm, w_up, ffn_conv_w, ffn_conv_b, w_down, post_ffn_norm]
    m_args = [m_pre_mix_norm, m_w_in, m_gdn_conv_w, m_gdn_a_log, m_gdn_dt_bias, m_gdn_norm_w, m_ssd_conv_w, m_ssd_conv_b, m_ssd_a_log, m_ssd_dt_bias, m_ssd_d, m_ssd_norm_w, m_w_out, m_post_mix_norm, m_pre_ffn_norm, m_w_up, m_ffn_conv_w, m_ffn_conv_b, m_w_down, m_post_ffn_norm]
    v_args = [v_pre_mix_norm, v_w_in, v_gdn_conv_w, v_gdn_a_log, v_gdn_dt_bias, v_gdn_norm_w, v_ssd_conv_w, v_ssd_conv_b, v_ssd_a_log, v_ssd_dt_bias, v_ssd_d, v_ssd_norm_w, v_w_out, v_post_mix_norm, v_pre_ffn_norm, v_w_up, v_ffn_conv_w, v_ffn_conv_b, v_w_down, v_post_ffn_norm]
    w = {k: a[0] for k, a in zip(names, w_args)}
    m = {k: a[0] for k, a in zip(names, m_args)}
    v = {k: a[0] for k, a in zip(names, v_args)}
    idx = 4 * lax.axis_index("x") + 2 * lax.axis_index("y") + lax.axis_index("c")
    big = ("w_in", "w_out", "w_up", "w_down")
    conv = ("gdn_conv_w", "ssd_conv_w", "ffn_conv_w")

    conv_local = jnp.concatenate([jnp.pad(w[k], ((0, 4 - w[k].shape[0]), (0, 0))) for k in conv], axis=1)
    g_in, g_conv = _gather_two_level("gather_weights", [w["w_in"].astype(BF16), conv_local])
    wp_in = _permute_in(_cols_from_shards(g_in))
    p = {k: w[k] for k in names if k not in big and k not in conv}
    off = 0
    for k in conv:
        cw = w[k].shape[1]
        p[k] = jnp.transpose(g_conv[:, :w[k].shape[0], off:off + cw], (1, 0, 2)).reshape(w[k].shape[0], N_DEV * cw)
        off += cw

    rest = tuple(w[k].astype(BF16) for k in ("w_out", "w_up", "w_down"))
    dx, p_in, p_out, p_up, p_down, small = _local_step(x, loss_target, wp_in, rest, p, True)

    gate_row = jnp.concatenate([small["gdn_gates"][0], small["gdn_gates"][1], small["ssd_gates"][0], small["ssd_gates"][1],
                                small["gdn_norm_w"][0], jnp.zeros((D_MODEL - 5 * LANES,), F32)]).reshape(1, D_MODEL)
    pack = _pack([small["pre_mix_norm"], small["ssd_norm_w"], small["post_mix_norm"], small["pre_ffn_norm"],
                  small["post_ffn_norm"], small["dd_lanes"], small["loss_lanes"], gate_row,
                  small["gdn_conv_w"], small["ssd_conv_w"], jnp.pad(small["ssd_conv_b"], ((0, 0), (0, 512))),
                  jnp.pad(small["ffn_conv_w"].reshape(-1), (0, 17 * D_MODEL - 3 * 2 * D_FF)),
                  jnp.pad(small["ffn_conv_b"], ((0, 0), (0, 512)))], SMALL_ROWS)
    (pack_all,) = _gather_two_level("gather_small", [pack])
    ssum, extra = _small_sum(pack_all)

    grads, deltas, new_m, new_v = {}, {}, {}, {}
    for k, parts in (("w_in", p_in), ("w_out", p_out), ("w_up", p_up), ("w_down", p_down)):
        grads[k], deltas[k], new_m[k], new_v[k] = _adam_big("adam_" + k, parts, w[k], m[k], v[k], 256)

    flat = ssum.reshape(-1)
    gate = ssum[7]
    sg = dict(pre_mix_norm=ssum[0], ssd_norm_w=ssum[1], post_mix_norm=ssum[2], pre_ffn_norm=ssum[3], post_ffn_norm=ssum[4],
              gdn_a_log=gate[8:16], gdn_dt_bias=gate[LANES + 8:LANES + 16], ssd_a_log=gate[2 * LANES + 16:2 * LANES + 32],
              ssd_dt_bias=gate[3 * LANES + 16:3 * LANES + 32], gdn_norm_w=gate[4 * LANES:5 * LANES], ssd_d=extra[0, 0:SSD_HEADS])
    o = 8 * D_MODEL
    full_gcw = flat[o:o + 4 * 3072].reshape(4, 3072)
    o += 12 * D_MODEL
    full_scw = flat[o:o + 4 * 1536].reshape(4, 1536)
    o += 6 * D_MODEL
    sg["ssd_conv_b"] = flat[o:o + 1536]
    o += 2 * D_MODEL
    full_fcw = flat[o:o + 3 * 2 * D_FF].reshape(3, 2 * D_FF)
    o += 17 * D_MODEL
    sg["ffn_conv_b"] = flat[o:o + 2 * D_FF]
    for k, full in (("gdn_conv_w", full_gcw), ("ssd_conv_w", full_scw), ("ffn_conv_w", full_fcw)):
        cw = w[k].shape[1]
        sg[k] = lax.dynamic_slice_in_dim(full, idx * cw, cw, axis=1)
    small_names = [k for k in names if k not in big]
    rows = 24
    gpk = _pack([sg[k] for k in small_names], rows)
    dpk, mpk, vpk = _adam_small(gpk, _pack([w[k] for k in small_names], rows), _pack([m[k] for k in small_names], rows),
                                _pack([v[k] for k in small_names], rows))
    shapes = [w[k].shape for k in small_names]
    for k, g_, d_, m_, v_ in zip(small_names, _unpack(gpk, shapes), _unpack(dpk, shapes), _unpack(mpk, shapes), _unpack(vpk, shapes)):
        grads[k], deltas[k], new_m[k], new_v[k] = g_, d_, m_, v_

    loss = extra[1, 0]
    lead = lambda a: a[None]
    return (loss, dx, *[lead(grads[k]) for k in names], *[lead(deltas[k]) for k in names],
            *[lead(new_m[k]) for k in names], *[lead(new_v[k]) for k in names])
```

```python
import functools

import jax
import jax.numpy as jnp
from jax import lax
from jax.experimental import pallas as pl
from jax.experimental.pallas import tpu as pltpu

F32 = jnp.float32
BF16 = jnp.bfloat16
MXU_DTYPE = jnp.bfloat16
HIGHEST = lax.Precision.HIGHEST
VMEM_LIMIT_V7X = 48 * 1024 * 1024
SUBLANES = 8
LANES = 128

D_MODEL = 1024
GDN_HEADS = 8
GDN_DK = 128
SSD_HEADS = 16
SSD_P = 64
SSD_GROUPS = 2
SSD_HPG = 8
SSD_N = 128
CHUNK = 128
D_FF = 2816
EPS = 1e-6
N_DEV = 8
PROJ_W = 7168
SMALL_CB = 52
D_IN = 6688

ADAM_LR = 0.001
ADAM_B1 = 0.9
ADAM_B2 = 0.999
ADAM_EPS = 1e-08
ADAM_WD = 0.01
ADAM_STEP = 10

NN = (((1,), (0,)), ((), ()))
NT = (((1,), (1,)), ((), ()))
TN = (((0,), (0,)), ((), ()))


def _pcall(body, **kw):
    return pl.pallas_call(body, **kw)


def _mm(a, b, dims=NN):
    return lax.dot_general(a.astype(MXU_DTYPE), b.astype(MXU_DTYPE), dims, preferred_element_type=F32)


def _mmx(a, b, dims=NN):
    return lax.dot_general(a, b, dims, precision=HIGHEST, preferred_element_type=F32)


def _split(a):
    hi = a.astype(MXU_DTYPE)
    return hi, (a - hi.astype(F32)).astype(MXU_DTYPE)


def _mm3(a, b, dims=NN):
    (ah, al), (bh, bl) = _split(a), _split(b)
    dot = lambda p, q: lax.dot_general(p, q, dims, preferred_element_type=F32)
    return dot(ah, bh) + (dot(ah, bl) + dot(al, bh))


def _mmsel(a, sel, dims=NN, terms=2):
    s = sel.astype(MXU_DTYPE)
    out = None
    for _ in range(terms):
        part = a.astype(MXU_DTYPE)
        a = a - part.astype(F32)
        prod = lax.dot_general(part, s, dims, preferred_element_type=F32)
        out = prod if out is None else out + prod
    return out


def _sigmoid(x):
    return 0.5 * jnp.tanh(0.5 * x) + 0.5


def _softplus(x):
    return jnp.maximum(x, 0.0) + jnp.log(1.0 + jnp.exp(-jnp.abs(x)))


def _dsilu(x, s):
    return s * (1.0 + x * (1.0 - s))


def _rowsum(x):
    return jnp.sum(x, axis=1, keepdims=True)


def _colsum(x):
    return jnp.sum(x, axis=0, keepdims=True)


def _pick(dim, pref):
    if dim <= pref:
        return dim
    best = None
    t = LANES
    while t <= pref:
        if dim % t == 0:
            best = t
        t += LANES
    return dim if best is None else best


def _params(sem):
    return pltpu.CompilerParams(dimension_semantics=sem, vmem_limit_bytes=VMEM_LIMIT_V7X)


def _matmul(name, a, b, mode, out_dtype, tm=1024, tn=1024, tk=1024):
    if mode == "nn":
        (m, k), (_, n) = a.shape, b.shape
    elif mode == "nt":
        (m, k), (n, _) = a.shape, b.shape
    else:
        (k, m), (_, n) = a.shape, b.shape
    tm, tn, tk = _pick(m, tm), _pick(n, tn), _pick(k, tk)
    nk = k // tk
    if mode == "tn":
        a_spec = pl.BlockSpec((tk, tm), lambda i, j, kk: (kk, i))
    else:
        a_spec = pl.BlockSpec((tm, tk), lambda i, j, kk: (i, kk))
    if mode == "nt":
        b_spec = pl.BlockSpec((tn, tk), lambda i, j, kk: (j, kk))
    else:
        b_spec = pl.BlockSpec((tk, tn), lambda i, j, kk: (kk, j))
    dims = {"nn": NN, "nt": NT, "tn": TN}[mode]

    def body(a_ref, b_ref, o_ref, *acc):
        if nk == 1:
            o_ref[...] = _mm(a_ref[...], b_ref[...], dims).astype(out_dtype)
            return
        kk = pl.program_id(2)

        @pl.when(kk == 0)
        def _():
            acc[0][...] = jnp.zeros_like(acc[0])

        acc[0][...] += _mm(a_ref[...], b_ref[...], dims)

        @pl.when(kk == nk - 1)
        def _():
            o_ref[...] = acc[0][...].astype(out_dtype)

    return _pcall(
        body, name=name, grid=(m // tm, n // tn, nk),
        in_specs=[a_spec, b_spec],
        out_specs=pl.BlockSpec((tm, tn), lambda i, j, kk: (i, j)),
        out_shape=jax.ShapeDtypeStruct((m, n), out_dtype),
        scratch_shapes=[pltpu.VMEM((tm, tn), F32)] if nk > 1 else [],
        compiler_params=_params(("parallel", "parallel", "arbitrary")),
    )(a, b)


def _matmul_rows(name, a, b, mode, epilogue, row_ins, full_ins, outs, accs=(), tm=512, tk=1024, scatter_riders=()):
    if mode == "nn":
        (m, k), (_, n) = a.shape, b.shape
    else:
        (m, k), (n, _) = a.shape, b.shape
    tm, tk = _pick(m, tm), _pick(k, tk)
    nk = k // tk
    a_spec = pl.BlockSpec((tm, tk), lambda i, kk: (i, kk))
    b_spec = pl.BlockSpec((n, tk), lambda i, kk: (0, kk)) if mode == "nt" else pl.BlockSpec((tk, n), lambda i, kk: (kk, 0))
    dims = NT if mode == "nt" else NN
    n_row, n_full, n_out, n_acc = len(row_ins), len(full_ins), len(outs), len(accs)

    def body(a_ref, b_ref, *rest):
        ins = rest[:n_row + n_full]
        out_refs = rest[n_row + n_full:n_row + n_full + n_out]
        acc_refs = rest[n_row + n_full + n_out:n_row + n_full + n_out + n_acc]
        prod_scr = rest[-1]
        i, kk = pl.program_id(0), pl.program_id(1)

        if n_acc:
            @pl.when((i == 0) & (kk == 0))
            def _():
                for r in acc_refs:
                    r[...] = jnp.zeros_like(r)

        if nk == 1:
            epilogue(_mm(a_ref[...], b_ref[...], dims), *ins, *out_refs, *acc_refs)
            return

        @pl.when(kk == 0)
        def _():
            prod_scr[...] = jnp.zeros_like(prod_scr)

        prod_scr[...] += _mm(a_ref[...], b_ref[...], dims)

        @pl.when(kk == nk - 1)
        def _():
            epilogue(prod_scr[...], *ins, *out_refs, *acc_refs)

    grid = (m // tm, nk)
    riders = list(scatter_riders)
    n_in = 2 + n_row + n_full
    any_spec, rider_shapes, rider_sems, wrap = _riding_exchange(riders, True, n_in, n_out + n_acc, grid)
    row_ins = [r if isinstance(r, tuple) else (r, r.shape[1], 0) for r in row_ins]
    in_specs = [a_spec, b_spec] + [pl.BlockSpec((tm, w), lambda i, kk, cb=cb: (i, cb)) for _, w, cb in row_ins]
    in_specs += [pl.BlockSpec(f.shape, lambda i, kk, nd=f.ndim: (0,) * nd) for f in full_ins]
    row_ins = [r for r, _, _ in row_ins]
    out_specs = [pl.BlockSpec((tm, w), lambda i, kk: (i, 0)) for w, _ in outs]
    out_specs += [pl.BlockSpec(s, lambda i, kk: (0, 0)) for s in accs]
    out_shape = [jax.ShapeDtypeStruct((m, w), dt) for w, dt in outs] + [jax.ShapeDtypeStruct(s, F32) for s in accs]
    res = _pcall(
        wrap(body), name=name, grid=grid,
        in_specs=in_specs + any_spec, out_specs=out_specs + any_spec, out_shape=out_shape + rider_shapes,
        scratch_shapes=[pltpu.VMEM((tm, n), F32)] + rider_sems,
        compiler_params=_params(("arbitrary", "arbitrary")),
    )(a, b, *row_ins, *full_ins, *riders)
    return (res[:n_out + n_acc], res[n_out + n_acc:]) if riders else res


def _rowwise(name, body, n_rows, tm, ins, outs, accs=()):
    arrays, in_specs = [], []
    last8 = n_rows // SUBLANES - 1
    per = tm // SUBLANES
    for spec in ins:
        kind, arr = spec[0], spec[1]
        if kind == "full":
            in_specs.append(pl.BlockSpec(arr.shape, lambda i, nd=arr.ndim: (0,) * nd))
        else:
            w, cb = spec[2], spec[3]
            if kind == "row":
                in_specs.append(pl.BlockSpec((tm, w), lambda i, cb=cb: (i, cb)))
            elif kind == "prev":
                in_specs.append(pl.BlockSpec((SUBLANES, w), lambda i, cb=cb: (jnp.maximum(i * per - 1, 0), cb)))
            else:
                in_specs.append(pl.BlockSpec((SUBLANES, w), lambda i, cb=cb: (jnp.minimum((i + 1) * per, last8), cb)))
        arrays.append(arr)
    out_shape = [jax.ShapeDtypeStruct((n_rows, w), dt) for (w, dt) in outs]
    out_shape += [jax.ShapeDtypeStruct(s, F32) for s in accs]
    out_specs = [pl.BlockSpec((tm, w), lambda i: (i, 0)) for (w, _) in outs]
    out_specs += [pl.BlockSpec(s, lambda i: (0, 0)) for s in accs]
    n_io = len(ins) + len(outs)

    def kern(*refs):
        i = pl.program_id(0)
        if accs:
            @pl.when(i == 0)
            def _():
                for r in refs[n_io:]:
                    r[...] = jnp.zeros_like(r)
        body(i, *refs)

    res = _pcall(
        kern, name=name, grid=(n_rows // tm,), in_specs=in_specs, out_specs=out_specs, out_shape=out_shape,
        compiler_params=_params(("arbitrary",)),
    )(*arrays)
    return res


def _shift_down(x, halo, j):
    r = pltpu.roll(x, j, 0)
    hr = pltpu.roll(halo, j, 0)
    rows = lax.broadcasted_iota(jnp.int32, (SUBLANES, x.shape[1]), 0)
    top = jnp.where(rows < j, hr, r[0:SUBLANES])
    return jnp.concatenate([top, r[SUBLANES:]], axis=0)


def _shift_up(x, halo, j):
    tm = x.shape[0]
    r = pltpu.roll(x, tm - j, 0)
    hr = pltpu.roll(halo, SUBLANES - j, 0)
    rows = lax.broadcasted_iota(jnp.int32, (SUBLANES, x.shape[1]), 0)
    bot = jnp.where(rows >= SUBLANES - j, hr, r[tm - SUBLANES:])
    return jnp.concatenate([r[:tm - SUBLANES], bot], axis=0)


def _conv_taps(x, halo, kw):
    return [x if kw - 1 - k == 0 else _shift_down(x, halo, kw - 1 - k) for k in range(kw)]


def _conv(taps, w):
    y = taps[0] * w[0:1]
    for k in range(1, len(taps)):
        y = y + taps[k] * w[k:k + 1]
    return y


def _rms(x, width):
    r = lax.rsqrt(jnp.sum(x * x, axis=-1, keepdims=True) * (1.0 / width) + EPS)
    return x * r, r


def _rms_bwd(xh, r, dxh, width):
    return r * (dxh - xh * (jnp.sum(dxh * xh, axis=-1, keepdims=True) * (1.0 / width)))


def _seq_flags(i, seq, tm):
    nps = seq // tm
    pos = i % nps
    return jnp.where(pos == 0, 0.0, 1.0), jnp.where(pos == nps - 1, 0.0, 1.0)


def _norm_cast(name, x, w, tm):
    t, d = x.shape

    def body(i, x_ref, w_ref, h_ref):
        xh, _ = _rms(x_ref[...], d)
        h_ref[...] = (xh * w_ref[...]).astype(BF16)

    return _rowwise(name, body, t, tm, [("row", x, d, 0), ("full", w)], [(d, BF16)])[0]


def _gdn_prep(proj, cw, gp, seq, tm):
    t = proj.shape[0]
    d = D_MODEL

    def body(i, q_ref, qh_ref, k_ref, kh_ref, v_ref, vh_ref, sm_ref, cw_ref, gp_ref, qn_ref, kn_ref, vv_ref, gs_ref, ypre_ref):
        keep, _ = _seq_flags(i, seq, tm)
        for x_ref, h_ref, o_ref, off, scale in ((q_ref, qh_ref, qn_ref, 0, GDN_DK ** -0.5),
                                               (k_ref, kh_ref, kn_ref, d, 1.0), (v_ref, vh_ref, vv_ref, 2 * d, None)):
            y = _conv(_conv_taps(x_ref[...], h_ref[...] * keep, 4), cw_ref[:, off:off + d])
            ypre_ref[:, off:off + d] = y
            a = y * _sigmoid(y)
            if scale is None:
                o_ref[...] = a
            else:
                for hh in range(GDN_HEADS):
                    s = a[:, hh * GDN_DK:(hh + 1) * GDN_DK]
                    n = lax.rsqrt(_rowsum(s * s) + EPS)
                    o_ref[:, hh * GDN_DK:(hh + 1) * GDN_DK] = s * (n * scale)
        sm = sm_ref[...]
        lane = lax.broadcasted_iota(jnp.int32, sm.shape, 1)
        beta = _sigmoid(sm)
        g = jnp.where((lane >= 8) & (lane < 16), -jnp.exp(gp_ref[0:1, :]) * _softplus(sm + gp_ref[1:2, :]), 0.0)
        gs_ref[...] = jnp.where(lane < 8, beta, _mmx(_block_tri(tm, False), g))

    ins = []
    for cb in range(3):
        ins += [("row", proj, d, cb), ("prev", proj, d, cb)]
    ins += [("row", proj, LANES, SMALL_CB), ("full", cw), ("full", gp)]
    return _rowwise("gdn_prep", body, t, tm, ins, [(d, F32), (d, F32), (d, F32), (LANES, F32), (3 * d, F32)])


def _block_tri(tm, upper):
    ri = lax.broadcasted_iota(jnp.int32, (tm, tm), 0)
    ci = lax.broadcasted_iota(jnp.int32, (tm, tm), 1)
    tri = (ri <= ci) if upper else (ri >= ci)
    return (tri & ((ri // CHUNK) == (ci // CHUNK))).astype(F32)


def _chunk_consts():
    row = lax.broadcasted_iota(jnp.int32, (CHUNK, CHUNK), 0)
    col = lax.broadcasted_iota(jnp.int32, (CHUNK, CHUNK), 1)
    return dict(
        tril=row >= col, strict=row > col, eye=(row == col).astype(F32),
        lane=lax.broadcasted_iota(jnp.int32, (CHUNK, LANES), 1),
        row1=lax.broadcasted_iota(jnp.int32, (CHUNK, 1), 0),
        ones=jnp.ones((CHUNK, LANES), F32))


def _hmap(fn, *lists):
    return [fn(*a) for a in zip(*lists)]


def _tri_inv(nmats, eye):
    levels = CHUNK.bit_length() - 2
    x = [eye - n for n in nmats]
    p = _hmap(_mm3, nmats, nmats)
    for lvl in range(levels):
        x = _hmap(lambda xi, pi: xi + _mm3(xi, pi), x, p)
        if lvl < levels - 1:
            p = _hmap(_mm3, p, p)
    return x


def _gdn_gates(gs, gc_row, h, c):
    beta = _rowsum(jnp.where(c["lane"] == h, gs, 0.0))
    gc = _rowsum(jnp.where(c["lane"] == h + 8, gs, 0.0))
    dc = jnp.exp(jnp.where(c["tril"], gc - gc_row, -1e30))
    gl = gc[CHUNK - 1:CHUNK, :]
    return beta, dc, jnp.exp(gc), jnp.exp(gl), jnp.exp(gl - gc)


GDN_HB = GDN_HEADS


def _gdn_specs(seq, sb, hb, backward):
    assert hb == GDN_HEADS
    nsb = seq // sb
    ncb = sb // CHUNK
    order = (lambda j: nsb - 1 - j) if backward else (lambda j: j)
    specs = dict(
        wide=lambda: pl.BlockSpec((1, sb, hb * GDN_DK), lambda b, h, j: (b, order(j), h)),
        gs=lambda: pl.BlockSpec((1, sb, LANES), lambda b, h, j: (b, order(j), 0)),
        gr=pl.BlockSpec((1, ncb, GDN_HEADS, CHUNK), lambda b, h, j: (b, order(j), 0, 0)),
        st=pl.BlockSpec((1, hb, ncb * GDN_DK, GDN_DK), lambda b, h, j: (b, h, order(j), 0)),
        ti=pl.BlockSpec((1, hb, sb, CHUNK), lambda b, h, j: (b, h, order(j), 0)))
    return nsb, ncb, specs


def _riding_exchange(arrays, scatter, n_in, n_out, grid):
    n = len(arrays)
    if n == 0:
        return [], [], [], lambda body: body
    any_spec = [pl.BlockSpec(memory_space=pl.ANY)] * n

    def wrap(body):
        def wrapped(*refs):
            ins = refs[n_in:n_in + n]
            outs = refs[n_in + n + n_out:n_in + 2 * n + n_out]
            sems = refs[len(refs) - 3:]
            pid = [pl.program_id(a) for a in range(len(grid))]
            first = functools.reduce(lambda a, b: a & b, [p == 0 for p in pid])
            last = functools.reduce(lambda a, b: a & b, [p == g - 1 for p, g in zip(pid, grid)])

            @pl.when(first)
            def _():
                _exchange_phase(ins, outs, sems, scatter, start=True)

            body(*refs[:n_in], *refs[n_in + n:n_in + n + n_out], *refs[n_in + 2 * n + n_out:len(refs) - 3])

            @pl.when(last)
            def _():
                _exchange_phase(ins, outs, sems, scatter, start=False)

        return wrapped

    return any_spec, _exchange_out_shapes(arrays, scatter), _exchange_sems(n), wrap


def _gdn_chunk_fwd(qn, kn, vv, gs, gr, bsz, seq, sb, riders):
    hb = GDN_HB
    nsb, ncb, sp = _gdn_specs(seq, sb, hb, False)
    grid = (bsz, GDN_HEADS // hb, nsb)
    any_spec, rider_shapes, rider_sems, wrap = _riding_exchange(riders, False, 5, 3, grid)

    def body(q_ref, k_ref, v_ref, gs_ref, gr_ref, o_ref, st_ref, ti_ref, s_scr):
        hg = pl.program_id(1)

        @pl.when(pl.program_id(2) == 0)
        def _():
            s_scr[...] = jnp.zeros_like(s_scr)

        c = _chunk_consts()

        def chunk(n, carry):
            r = pl.ds(pl.multiple_of(n * CHUNK, CHUNK), CHUNK)
            rs = pl.ds(pl.multiple_of(n * GDN_DK, GDN_DK), GDN_DK)
            gsv = gs_ref[0, r, :]
            heads = list(range(hb))
            sls = [slice(ih * GDN_DK, (ih + 1) * GDN_DK) for ih in heads]
            q = [q_ref[0, r, sl] for sl in sls]
            k = [k_ref[0, r, sl] for sl in sls]
            v = [v_ref[0, r, sl] for sl in sls]
            beta, dc, eg, egl, ekd = zip(*[
                _gdn_gates(gsv, gr_ref[0, n, pl.ds(ih, 1), :], ih, c) for ih in heads])
            kb = _hmap(lambda a, b: a * b, k, beta)
            amat = _hmap(lambda a, b, d_: jnp.where(c["strict"], _mm(a, b, NT) * d_, 0.0), kb, k, dc)
            tinv = _tri_inv(amat, c["eye"])
            u = _hmap(lambda t_, a, b: _mm3(t_, a * b), tinv, v, beta)
            w = _hmap(lambda t_, a, b: _mm3(t_, a * b), tinv, kb, eg)
            qk = _hmap(lambda a, b, d_: _mm(a, b, NT) * d_, q, k, dc)
            s = [s_scr[ih] for ih in heads]
            v_new = _hmap(lambda a, b, s_: a - _mm(b, s_), u, w, s)
            o = _hmap(lambda a, e, s_, qk_, vn: _mm(a * e, s_) + _mm(qk_, vn), q, eg, s, qk, v_new)
            s_new = _hmap(lambda s_, e, a, f, vn: s_ * e + _mm(a * f, vn, TN), s, egl, k, ekd, v_new)
            for ih in heads:
                o_ref[0, r, sls[ih]] = o[ih]
                st_ref[0, ih, rs, :] = s[ih]
                ti_ref[0, ih, r, :] = tinv[ih]
                s_scr[ih] = s_new[ih]
            return carry

        lax.fori_loop(0, ncb, chunk, 0)

    t3 = (bsz, seq, D_MODEL)
    res = _pcall(
        wrap(body), name="gdn_chunk_fwd", grid=grid,
        in_specs=[sp["wide"](), sp["wide"](), sp["wide"](), sp["gs"](), sp["gr"]] + any_spec,
        out_specs=[sp["wide"](), sp["st"], sp["ti"]] + any_spec,
        out_shape=[jax.ShapeDtypeStruct(t3, F32),
                   jax.ShapeDtypeStruct((bsz, GDN_HEADS, (seq // CHUNK) * GDN_DK, GDN_DK), F32),
                   jax.ShapeDtypeStruct((bsz, GDN_HEADS, seq, CHUNK), F32)] + rider_shapes,
        scratch_shapes=[pltpu.VMEM((hb, GDN_DK, GDN_DK), F32)] + rider_sems,
        compiler_params=_params(("arbitrary", "arbitrary", "arbitrary")),
    )(qn, kn, vv, gs, gr, *riders)
    return res[:3], res[3:]


def _ssd_prep(proj, cw, cb, sp, seq, tm):
    t = proj.shape[0]
    d = D_MODEL
    ssd_w = SSD_HEADS * SSD_P

    def body(i, x_ref, xh_ref, bc_ref, bch_ref, sm_ref, cw_ref, cb_ref, sp_ref, xs_ref, bco_ref, dtx_ref, acsx_ref, acs_ref, ypre_ref):
        keep, _ = _seq_flags(i, seq, tm)
        y = _conv(_conv_taps(x_ref[...], xh_ref[...] * keep, 4), cw_ref[:, 0:d]) + cb_ref[:, 0:d]
        ypre_ref[:, 0:d] = y
        xs_ref[...] = y * _sigmoid(y)
        y = _conv(_conv_taps(bc_ref[...], bch_ref[...] * keep, 4), cw_ref[:, d:d + 512]) + cb_ref[:, d:d + 512]
        ypre_ref[:, d:d + 512] = y
        bco_ref[...] = y * _sigmoid(y)
        sm = sm_ref[...]
        lane = lax.broadcasted_iota(jnp.int32, sm.shape, 1)
        valid = (lane >= 16) & (lane < 32)
        dt = jnp.where(valid, _softplus(sm + sp_ref[1:2, :]), 0.0)
        adt = dt * (-jnp.exp(sp_ref[0:1, :]))
        acs = _mmx(_block_tri(tm, False), adt)
        l64 = lax.broadcasted_iota(jnp.int32, (LANES, ssd_w), 0)
        d64 = lax.broadcasted_iota(jnp.int32, (LANES, ssd_w), 1)
        e64 = (l64 - 16 == d64 // SSD_P).astype(F32)
        dtx_ref[...] = _mmsel(dt, e64, terms=3)
        acsx_ref[...] = _mmsel(acs, e64, terms=3)
        acs_ref[...] = acs

    ins = [("row", proj, d, 5), ("prev", proj, d, 5), ("row", proj, 512, 12), ("prev", proj, 512, 12),
           ("row", proj, LANES, SMALL_CB), ("full", cw), ("full", cb), ("full", sp)]
    return _rowwise("ssd_prep", body, t, tm, ins,
                    [(d, F32), (512, F32), (ssd_w, F32), (ssd_w, F32), (LANES, F32), (d + 512, F32)])


SSD_GW = SSD_HPG * SSD_P


def _ssd_head(acs, ar_ref, n, head, cbm, c):
    col = _rowsum(jnp.where(c["lane"] == head + 16, acs, 0.0))
    lm = jnp.exp(jnp.where(c["tril"], col - ar_ref[0, n, pl.ds(head, 1), :], -1e30))
    return lm, cbm * lm


def _ssd_specs(seq, sb):
    nsb = seq // sb
    ncb = sb // CHUNK
    def specs(order):
        return dict(
            wide=lambda: pl.BlockSpec((1, sb, SSD_HEADS * SSD_P), lambda b, j: (b, order(j), 0)),
            bc=lambda: pl.BlockSpec((1, sb, 2 * SSD_GROUPS * SSD_N), lambda b, j: (b, order(j), 0)),
            half=lambda: pl.BlockSpec((1, sb, SSD_GROUPS * SSD_N), lambda b, j: (b, order(j), 0)),
            small=lambda: pl.BlockSpec((1, sb, LANES), lambda b, j: (b, order(j), 0)),
            ar=pl.BlockSpec((1, ncb, SSD_HEADS, CHUNK), lambda b, j: (b, order(j), 0, 0)),
            st=pl.BlockSpec((1, ncb * SSD_N, SSD_HEADS * SSD_P), lambda b, j: (b, order(j), 0)))
    return nsb, ncb, specs(lambda j: j), specs(lambda j: nsb - 1 - j)


def _ssd_chunk_fwd(xs, bc, dtx, acsx, acs, ar, bsz, seq, sb):
    nsb, ncb, sp, _ = _ssd_specs(seq, sb)

    def body(x_ref, dtx_ref, ax_ref, bc_ref, acs_ref, ar_ref, y_ref, sts_ref, st_scr):
        @pl.when(pl.program_id(1) == 0)
        def _():
            st_scr[...] = jnp.zeros_like(st_scr)

        c = _chunk_consts()
        lane5 = lax.broadcasted_iota(jnp.int32, (CHUNK, SSD_GW), 1) // SSD_P

        def chunk(n, carry):
            r = pl.ds(pl.multiple_of(n * CHUNK, CHUNK), CHUNK)
            rs = pl.ds(pl.multiple_of(n * SSD_N, SSD_N), SSD_N)
            acsv = acs_ref[0, r, :]
            for g in range(SSD_GROUPS):
                gl = slice(g * SSD_GW, (g + 1) * SSD_GW)
                x, dt, ax = x_ref[0, r, gl], dtx_ref[0, r, gl], ax_ref[0, r, gl]
                bm = bc_ref[0, r, g * SSD_N:(g + 1) * SSD_N]
                cm = bc_ref[0, r, (SSD_GROUPS + g) * SSD_N:(SSD_GROUPS + g + 1) * SSD_N]
                xdt = x * dt
                cbm = _mm(cm, bm, NT)
                al = ax[CHUNK - 1:CHUNK, :]
                st = st_scr[:, gl]
                y = _mm(cm, st) * jnp.exp(ax)
                for hh in range(SSD_HPG):
                    _, gm = _ssd_head(acsv, ar_ref, n, g * SSD_HPG + hh, cbm, c)
                    y = y + _mm(gm, jnp.where(lane5 == hh, xdt, 0.0))
                y_ref[0, r, gl] = y
                sts_ref[0, rs, gl] = st
                st_scr[:, gl] = st * jnp.exp(al) + _mm(bm, xdt * jnp.exp(al - ax), TN)
            return carry

        lax.fori_loop(0, ncb, chunk, 0)

    return _pcall(
        body, name="ssd_chunk_fwd", grid=(bsz, nsb),
        in_specs=[sp["wide"](), sp["wide"](), sp["wide"](), sp["bc"](), sp["small"](), sp["ar"]],
        out_specs=[sp["wide"](), sp["st"]],
        out_shape=[jax.ShapeDtypeStruct((bsz, seq, SSD_HEADS * SSD_P), F32),
                   jax.ShapeDtypeStruct((bsz, (seq // CHUNK) * SSD_N, SSD_HEADS * SSD_P), F32)],
        scratch_shapes=[pltpu.VMEM((SSD_N, SSD_HEADS * SSD_P), F32)],
        compiler_params=_params(("parallel", "arbitrary")),
    )(xs, dtx, acsx, bc, acs, ar)


def _gate_norm(o_gdn, y_ssd, xs, proj, gnw, snw, dvec, tm):
    t = o_gdn.shape[0]
    d = D_MODEL

    def body(i, o_ref, za_ref, y_ref, xs_ref, zs_ref, gnw_ref, snw_ref, dv_ref, out_ref):
        for hh in range(GDN_HEADS):
            sl = slice(hh * GDN_DK, (hh + 1) * GDN_DK)
            oh, _ = _rms(o_ref[:, sl], GDN_DK)
            z = za_ref[:, sl]
            out_ref[:, sl] = (oh * gnw_ref[...] * (z * _sigmoid(z))).astype(BF16)
        zs = zs_ref[...]
        yg = (y_ref[...] + dv_ref[...] * xs_ref[...]) * (zs * _sigmoid(zs))
        for g in range(SSD_GROUPS):
            sl = slice(g * 512, (g + 1) * 512)
            yh, _ = _rms(yg[:, sl], 512)
            out_ref[:, d + g * 512:d + (g + 1) * 512] = (yh * snw_ref[:, sl]).astype(BF16)

    ins = [("row", o_gdn, d, 0), ("row", proj, d, 3), ("row", y_ssd, d, 0), ("row", xs, d, 0), ("row", proj, d, 4),
           ("full", gnw), ("full", snw), ("full", dvec)]
    return _rowwise("gate_norm", body, t, tm, ins, [(2 * d, BF16)])[0]


def _out_mid(mixin, w_out, x, pmw, pfw):
    d = D_MODEL

    def epilogue(mix, x_ref, pmw_ref, pfw_ref, mix_ref, x1_ref, h2_ref):
        mix_ref[...] = mix
        mh, _ = _rms(mix, d)
        x1 = x_ref[...] + mh * pmw_ref[...]
        x1_ref[...] = x1
        xh, _ = _rms(x1, d)
        h2_ref[...] = (xh * pfw_ref[...]).astype(BF16)

    return _matmul_rows("mm_out_mid", mixin, w_out, "nn", epilogue, [x], [pmw, pfw], [(d, F32), (d, F32), (d, BF16)],
                        tk=2 * d)


def _ffn_act(u_pre, cw, cb, seq, tm):
    t = u_pre.shape[0]

    def body(i, ug_ref, ugh_ref, uu_ref, uuh_ref, cw_ref, cb_ref, act_ref, u_ref):
        keep, _ = _seq_flags(i, seq, tm)
        gate = _conv(_conv_taps(ug_ref[...], ugh_ref[...] * keep, 3), cw_ref[:, 0:D_FF]) + cb_ref[:, 0:D_FF]
        up = _conv(_conv_taps(uu_ref[...], uuh_ref[...] * keep, 3), cw_ref[:, D_FF:2 * D_FF]) + cb_ref[:, D_FF:2 * D_FF]
        u_ref[:, 0:D_FF] = gate
        u_ref[:, D_FF:2 * D_FF] = up
        act_ref[...] = (gate * _sigmoid(gate) * up).astype(BF16)

    ins = [("row", u_pre, D_FF, 0), ("prev", u_pre, D_FF, 0), ("row", u_pre, D_FF, 1), ("prev", u_pre, D_FF, 1),
           ("full", cw), ("full", cb)]
    return _rowwise("ffn_act", body, t, tm, ins, [(D_FF, BF16), (2 * D_FF, F32)])


def _down_final(act, w_down, x1, tgt, w):
    d = D_MODEL

    def epilogue(f, x1_ref, t_ref, w_ref, dy_ref, df_ref, loss_ref, dw_ref):
        fh, r = _rms(f, d)
        e = x1_ref[...] + fh * w_ref[...] - t_ref[...]
        loss_ref[...] += _colsum(e * e) * (0.5 / d)
        dy = e * (1.0 / d)
        dy_ref[...] = dy
        dw_ref[...] += _colsum(dy * fh)
        df_ref[...] = _rms_bwd(fh, r, dy * w_ref[...], d).astype(BF16)

    return _matmul_rows("mm_down_final", act, w_down, "nn", epilogue, [x1, tgt], [w], [(d, F32), (d, BF16)],
                        accs=[(1, d), (1, d)], tk=D_FF)


def _ffn_bwd(u, u_pre, dact, cw, seq, tm):
    t = u.shape[0]

    def body(i, g_ref, gn_ref, up_ref, upn_ref, xg_ref, xu_ref, da_ref, dan_ref, cw_ref, dpre_ref, dcw_ref, dcb_ref):
        _, keep_next = _seq_flags(i, seq, tm)
        ext = lambda a_ref, n_ref: jnp.concatenate([a_ref[...], n_ref[...]], axis=0)
        rows = tm + SUBLANES
        gate, up = ext(g_ref, gn_ref), ext(up_ref, upn_ref)
        sg = _sigmoid(gate)
        da = jnp.concatenate([da_ref[...], dan_ref[...] * keep_next], axis=0)
        for off, grad, x_ref in ((0, da * up * _dsilu(gate, sg), xg_ref), (D_FF, da * gate * sg, xu_ref)):
            x = x_ref[...]
            own = grad[0:tm]
            acc = own * cw_ref[2:3, off:off + D_FF]
            dcb_ref[:, off:off + D_FF] += _colsum(own)
            dcw_ref[2:3, off:off + D_FF] += _colsum(own * x)
            for j in (1, 2):
                ahead = pltpu.roll(grad, rows - j, 0)[0:tm]
                acc = acc + ahead * cw_ref[2 - j:3 - j, off:off + D_FF]
                dcw_ref[2 - j:3 - j, off:off + D_FF] += _colsum(ahead * x)
            dpre_ref[:, off:off + D_FF] = acc.astype(BF16)

    ins = []
    for cb_ in range(2):
        ins += [("row", u, D_FF, cb_), ("next", u, D_FF, cb_)]
    ins += [("row", u_pre, D_FF, 0), ("row", u_pre, D_FF, 1), ("row", dact, D_FF, 0), ("next", dact, D_FF, 0), ("full", cw)]
    return _rowwise("ffn_bwd", body, t, tm, ins, [(2 * D_FF, BF16)], accs=[(SUBLANES, 2 * D_FF), (1, 2 * D_FF)])


def _assemble_dproj(dpre_qkv, dza, dzs, dpre_xbc, dsm, proj, gcw, scw, seq, tm):
    t = dza.shape[0]
    d = D_MODEL

    def body(i, dq_ref, dqn_ref, dk_ref, dkn_ref, dv_ref, dvn_ref, dx_ref, dxn_ref, dbc_ref, dbcn_ref, dza_ref, dzs_ref,
             dsm_ref, xq_ref, xk_ref, xv_ref, xx_ref, xbc_ref, gcw_ref, scw_ref, o_ref, dgcw_ref, dscw_ref):
        _, keep = _seq_flags(i, seq, tm)
        pieces = [(g_ref, n_ref, gcw_ref, dgcw_ref, x_ref, 0, c0) for g_ref, n_ref, x_ref, c0 in (
            (dq_ref, dqn_ref, xq_ref, 0), (dk_ref, dkn_ref, xk_ref, d), (dv_ref, dvn_ref, xv_ref, 2 * d))]
        pieces += [(g_ref, n_ref, scw_ref, dscw_ref, x_ref, 5 * d, c0) for g_ref, n_ref, x_ref, c0 in (
            (dx_ref, dxn_ref, xx_ref, 0), (dbc_ref, dbcn_ref, xbc_ref, d))]
        for d_ref, n_ref, cw_ref, dcw_ref, x_ref, base, c0 in pieces:
            w = x_ref.shape[1]
            x = x_ref[...]
            g = d_ref[...]
            halo = n_ref[...] * keep
            acc = g * cw_ref[3:4, c0:c0 + w]
            dcw_ref[3:4, c0:c0 + w] += _colsum(g * x)
            for j in range(1, 4):
                ahead = _shift_up(g, halo, j)
                acc = acc + ahead * cw_ref[3 - j:4 - j, c0:c0 + w]
                dcw_ref[3 - j:4 - j, c0:c0 + w] += _colsum(ahead * x)
            o_ref[:, base + c0:base + c0 + w] = acc.astype(BF16)
        o_ref[:, 3 * d:4 * d] = dza_ref[...]
        o_ref[:, 4 * d:5 * d] = dzs_ref[...]
        o_ref[:, 6 * d + 512:6 * d + 512 + LANES] = dsm_ref[...]
        o_ref[:, 6 * d + 512 + LANES:PROJ_W] = jnp.zeros((tm, PROJ_W - (6 * d + 512 + LANES)), BF16)

    ins = []
    for g in tuple(dpre_qkv) + tuple(dpre_xbc):
        ins += [("row", g, g.shape[1], 0), ("next", g, g.shape[1], 0)]
    ins += [("row", dza, d, 0), ("row", dzs, d, 0), ("row", dsm, LANES, 0),
           ("row", proj, d, 0), ("row", proj, d, 1), ("row", proj, d, 2), ("row", proj, d, 5), ("row", proj, 512, 12),
           ("full", gcw), ("full", scw)]
    return _rowwise("assemble_dproj", body, t, tm, ins, [(PROJ_W, BF16)], accs=[(SUBLANES, 3 * d), (SUBLANES, d + 512)])


def _dh2_mid_bwd(du_pre, w_up, x1, mix, dy, pmw, pfw):
    d = D_MODEL

    def epilogue(dh2, x1_ref, mix_ref, dy_ref, pmw_ref, pfw_ref, dx1_ref, dmix_ref, dpm_ref, dpf_ref):
        xh, r2 = _rms(x1_ref[...], d)
        dpf_ref[...] += _colsum(dh2 * xh)
        dx1 = dy_ref[...] + _rms_bwd(xh, r2, dh2 * pfw_ref[...], d)
        dx1_ref[...] = dx1
        mh, r = _rms(mix_ref[...], d)
        dpm_ref[...] += _colsum(dx1 * mh)
        dmix_ref[...] = _rms_bwd(mh, r, dx1 * pmw_ref[...], d).astype(BF16)

    return _matmul_rows("mm_dh2_mid_bwd", du_pre, w_up, "nt", epilogue, [x1, mix, dy], [pmw, pfw],
                        [(d, F32), (d, BF16)], accs=[(1, d), (1, d)], tk=D_FF)


def _dmixin_gate_norm_bwd(dmix, w_out, o_gdn, y_ssd, xs, proj, gnw, snw, dvec):
    d = D_MODEL

    def epilogue(dmixin, o_ref, za_ref, y_ref, xs_ref, zs_ref, gnw_ref, snw_ref, dv_ref,
                 do_ref, dza_ref, dy_ref, dxs_ref, dzs_ref, dgnw_ref, dsnw_ref, dd_ref):
        for hh in range(GDN_HEADS):
            sl = slice(hh * GDN_DK, (hh + 1) * GDN_DK)
            oh, r = _rms(o_ref[:, sl], GDN_DK)
            z = za_ref[:, sl]
            sz = _sigmoid(z)
            dm = dmixin[:, sl]
            don = dm * (z * sz)
            dza_ref[:, sl] = (dm * oh * gnw_ref[...] * _dsilu(z, sz)).astype(BF16)
            dgnw_ref[...] += _colsum(don * oh)
            do_ref[:, sl] = _rms_bwd(oh, r, don * gnw_ref[...], GDN_DK)
        zs = zs_ref[...]
        sz = _sigmoid(zs)
        sil = zs * sz
        x = xs_ref[...]
        y0 = y_ref[...] + dv_ref[...] * x
        yg = y0 * sil
        dms = dmixin[:, d:2 * d]
        for g in range(SSD_GROUPS):
            sl = slice(g * 512, (g + 1) * 512)
            yh, r = _rms(yg[:, sl], 512)
            dsnw_ref[:, sl] += _colsum(dms[:, sl] * yh)
            dyg = _rms_bwd(yh, r, dms[:, sl] * snw_ref[:, sl], 512)
            dy0 = dyg * sil[:, sl]
            dzs_ref[:, sl] = (dyg * y0[:, sl] * _dsilu(zs[:, sl], sz[:, sl])).astype(BF16)
            dy_ref[:, sl] = dy0
            dxs_ref[:, sl] = dy0 * dv_ref[:, sl]
            dd_ref[:, sl] += _colsum(dy0 * x[:, sl])

    row_ins = [o_gdn, (proj, d, 3), y_ssd, xs, (proj, d, 4)]
    return _matmul_rows("mm_dmixin_gate_norm_bwd", dmix, w_out, "nt", epilogue, row_ins, [gnw, snw, dvec],
                        [(d, F32), (d, BF16), (d, F32), (d, F32), (d, BF16)], accs=[(1, GDN_DK), (1, d), (1, d)], tm=256)


def _ssd_chunk_bwd(xs, bc, dtx, acsx, acs, ar, dy, sts, ypre, dxs_d, bsz, seq, sb):
    nsb, ncb, _, sp = _ssd_specs(seq, sb)
    bc_w = 2 * SSD_GROUPS * SSD_N
    x_w = SSD_HEADS * SSD_P

    def body(x_ref, dtx_ref, ax_ref, bc_ref, acs_ref, ar_ref, dy_ref, sts_ref, yx_ref, ybc_ref, dxd_ref,
             dx_ref, dbc_ref, ddt_ref, dacs_ref, dbx_ref, dbbc_ref, dst_scr):
        @pl.when(pl.program_id(1) == 0)
        def _():
            dst_scr[...] = jnp.zeros_like(dst_scr)

        @pl.when((pl.program_id(0) == 0) & (pl.program_id(1) == 0))
        def _():
            dbx_ref[...] = jnp.zeros_like(dbx_ref)
            dbbc_ref[...] = jnp.zeros_like(dbbc_ref)

        def to_conv_out(grad, y):
            return grad * _dsilu(y, _sigmoid(y))

        c = _chunk_consts()
        lane5 = lax.broadcasted_iota(jnp.int32, (CHUNK, SSD_GW), 1) // SSD_P
        row5 = lax.broadcasted_iota(jnp.int32, (CHUNK, SSD_GW), 0)
        sel_in = lax.broadcasted_iota(jnp.int32, (SSD_GW, LANES), 0) // SSD_P
        sel_out = lax.broadcasted_iota(jnp.int32, (SSD_GW, LANES), 1)

        def chunk(nn, carry):
            n = ncb - 1 - nn
            r = pl.ds(pl.multiple_of(n * CHUNK, CHUNK), CHUNK)
            rs = pl.ds(pl.multiple_of(n * SSD_N, SSD_N), SSD_N)
            acsv = acs_ref[0, r, :]
            ddt = jnp.zeros((CHUNK, LANES), F32)
            dacs = jnp.zeros((CHUNK, LANES), F32)
            for g in range(SSD_GROUPS):
                gl = slice(g * SSD_GW, (g + 1) * SSD_GW)
                x, dt, ax, dyv = x_ref[0, r, gl], dtx_ref[0, r, gl], ax_ref[0, r, gl], dy_ref[0, r, gl]
                bm = bc_ref[0, r, g * SSD_N:(g + 1) * SSD_N]
                cm = bc_ref[0, r, (SSD_GROUPS + g) * SSD_N:(SSD_GROUPS + g + 1) * SSD_N]
                st = sts_ref[0, rs, gl]
                dst = dst_scr[:, gl]
                rsel = (sel_in + (16 + g * SSD_HPG) == sel_out).astype(F32)
                xdt = x * dt
                cbm = _mm(cm, bm, NT)
                al = ax[CHUNK - 1:CHUNK, :]
                ex, el = jnp.exp(ax), jnp.exp(al)
                dec = jnp.exp(al - ax)
                xd = xdt * dec
                dye = dyv * ex
                dxd = _mm(bm, dst)
                dxdt = dec * dxd
                dcm = _mm(dye, st, NT)
                dbm = _mm(xd, dst, NT)
                z = dye * _mm(cm, st) - dxd * xd
                zl = _colsum(dst * st) * el + _colsum(dxd * xd)
                z = z + jnp.where(row5 == CHUNK - 1, zl, 0.0)
                dcb = jnp.zeros((CHUNK, CHUNK), F32)
                for hh in range(SSD_HPG):
                    head = g * SSD_HPG + hh
                    lm, gm = _ssd_head(acsv, ar_ref, n, head, cbm, c)
                    dym = jnp.where(lane5 == hh, dyv, 0.0)
                    dxdt = dxdt + _mm(gm, dym, TN)
                    dg = _mm(dym, xdt, NT)
                    dcb = dcb + dg * lm
                    pm = dg * gm
                    dacs = dacs + jnp.where(c["lane"] == head + 16, _rowsum(pm) - _mmsel(pm, c["ones"], TN), 0.0)
                for sl, grad in ((slice((SSD_GROUPS + g) * SSD_N, (SSD_GROUPS + g + 1) * SSD_N), dcm + _mm(dcb, bm)),
                                 (slice(g * SSD_N, (g + 1) * SSD_N), dbm + _mm(dcb, cm, TN))):
                    dpre = to_conv_out(grad, ybc_ref[0, r, sl])
                    dbc_ref[0, r, sl] = dpre
                    dbbc_ref[:, sl] += _colsum(dpre)
                dacs = dacs + _mmsel(z, rsel)
                ddt = ddt + _mmsel(dxdt * x, rsel)
                dpre = to_conv_out(dxdt * dt + dxd_ref[0, r, gl], yx_ref[0, r, gl])
                dx_ref[0, r, gl] = dpre
                dbx_ref[:, gl] += _colsum(dpre)
                dst_scr[:, gl] = dst * el + _mm(cm, dye, TN)
            ddt_ref[0, r, :] = ddt
            dacs_ref[0, r, :] = dacs
            return carry

        lax.fori_loop(0, ncb, chunk, 0)

    nsb_rev = lambda j: nsb - 1 - j
    return _pcall(
        body, name="ssd_chunk_bwd", grid=(bsz, nsb),
        in_specs=[sp["wide"](), sp["wide"](), sp["wide"](), sp["bc"](), sp["small"](), sp["ar"], sp["wide"](), sp["st"],
                  sp["wide"](), pl.BlockSpec((1, sb, bc_w), lambda b, j: (b, nsb_rev(j), x_w // bc_w)), sp["wide"]()],
        out_specs=[sp["wide"](), sp["bc"](), sp["small"](), sp["small"](),
                   pl.BlockSpec((1, x_w), lambda b, j: (0, 0)), pl.BlockSpec((1, bc_w), lambda b, j: (0, 0))],
        out_shape=[jax.ShapeDtypeStruct((bsz, seq, x_w), F32), jax.ShapeDtypeStruct((bsz, seq, bc_w), F32),
                   jax.ShapeDtypeStruct((bsz, seq, LANES), F32), jax.ShapeDtypeStruct((bsz, seq, LANES), F32),
                   jax.ShapeDtypeStruct((1, x_w), F32), jax.ShapeDtypeStruct((1, bc_w), F32)],
        scratch_shapes=[pltpu.VMEM((SSD_N, x_w), F32)],
        compiler_params=_params(("arbitrary", "arbitrary")),
    )(xs, dtx, acsx, bc, acs, ar, dy, sts, ypre, ypre, dxs_d)


def _through_norm_silu(g, y, scale):
    sy = _sigmoid(y)
    ds_ = _dsilu(y, sy)
    if scale is None:
        return g * ds_
    a = y * sy
    n = lax.rsqrt(_rowsum(a * a) + EPS)
    ah = a * n
    return (scale * n) * (g - ah * _rowsum(g * ah)) * ds_


def _gdn_chunk_bwd(qn, kn, vv, gs, gr, do, sts, tis, ypre, bsz, seq, sb, riders):
    hb = GDN_HB
    nsb, ncb, sp = _gdn_specs(seq, sb, hb, True)
    grid = (bsz, GDN_HEADS // hb, nsb)
    any_spec, rider_shapes, rider_sems, wrap = _riding_exchange(riders, True, 11, 4, grid)
    ypre_spec = lambda cb: pl.BlockSpec((1, sb, hb * GDN_DK), lambda b, h, j: (b, nsb - 1 - j, cb))

    def body(q_ref, k_ref, v_ref, gs_ref, gr_ref, do_ref, st_ref, ti_ref, yq_ref, yk_ref, yv_ref,
             dq_ref, dk_ref, dv_ref, dgs_ref, ds_scr):
        @pl.when(pl.program_id(2) == 0)
        def _():
            ds_scr[...] = jnp.zeros_like(ds_scr)

        c = _chunk_consts()

        def chunk(nn, carry):
            n = ncb - 1 - nn
            r = pl.ds(pl.multiple_of(n * CHUNK, CHUNK), CHUNK)
            rs = pl.ds(pl.multiple_of(n * GDN_DK, GDN_DK), GDN_DK)
            gsv = gs_ref[0, r, :]
            heads = list(range(hb))
            sls = [slice(ih * GDN_DK, (ih + 1) * GDN_DK) for ih in heads]
            q = [q_ref[0, r, sl] for sl in sls]
            k = [k_ref[0, r, sl] for sl in sls]
            v = [v_ref[0, r, sl] for sl in sls]
            do_ = [do_ref[0, r, sl] for sl in sls]
            s = [st_ref[0, ih, rs, :] for ih in heads]
            tinv = [ti_ref[0, ih, r, :] for ih in heads]
            dsn = [ds_scr[ih] for ih in heads]
            beta, dc, eg, egl, ekd = zip(*[
                _gdn_gates(gsv, gr_ref[0, n, pl.ds(ih, 1), :], ih, c) for ih in heads])
            mul = lambda a, b: a * b
            kb = _hmap(mul, k, beta)
            rhs_w = _hmap(mul, kb, eg)
            u = _hmap(lambda t_, a, b: _mm3(t_, a * b), tinv, v, beta)
            w = _hmap(_mm3, tinv, rhs_w)
            amat = _hmap(lambda a, b, d_: jnp.where(c["strict"], _mm(a, b, NT) * d_, 0.0), kb, k, dc)
            qk = _hmap(lambda a, b, d_: _mm(a, b, NT) * d_, q, k, dc)
            qd = _hmap(mul, q, eg)
            kd = _hmap(mul, k, ekd)
            v_new = _hmap(lambda a, b, s_: a - _mm(b, s_), u, w, s)
            dv_new = _hmap(lambda qk_, d_, kd_, dn: _mm(qk_, d_, TN) + _mm(kd_, dn), qk, do_, kd, dsn)
            dqk = _hmap(lambda d_, vn: _mm(d_, vn, NT), do_, v_new)
            dqd = _hmap(lambda d_, s_: _mm(d_, s_, NT), do_, s)
            ds_new = _hmap(lambda qd_, d_, dn, e, w_, dvn: _mm(qd_, d_, TN) + dn * e - _mm(w_, dvn, TN),
                           qd, do_, dsn, egl, w, dv_new)
            dkd = _hmap(lambda vn, dn: _mm(vn, dn, NT), v_new, dsn)
            dgl = _hmap(lambda s_, dn, e: _colsum(_rowsum(s_ * dn)) * e, s, dsn, egl)
            dw = _hmap(lambda dvn, s_: -_mm(dvn, s_, NT), dv_new, s)
            dru = _hmap(lambda t_, a: _mm3(t_, a, TN), tinv, dv_new)
            drw = _hmap(lambda t_, a: _mm3(t_, a, TN), tinv, dw)
            da = _hmap(lambda a, u_, b, w_: jnp.where(c["strict"], -(_mm(a, u_, NT) + _mm(b, w_, NT)), 0.0), dru, u, drw, w)
            m = _hmap(mul, da, dc)
            dkb = _hmap(lambda a, e, m_, k_: a * e + _mm(m_, k_), drw, eg, m, k)
            mq = _hmap(mul, dqk, dc)
            dq = _hmap(lambda mq_, k_, a, e: _mm(mq_, k_) + a * e, mq, k, dqd, eg)
            dk = _hmap(lambda m_, kb_, mq_, q_, a, e, b, be: _mm(m_, kb_, TN) + _mm(mq_, q_, TN) + a * e + b * be,
                       m, kb, mq, q, dkd, ekd, dkb, beta)
            dbeta = _hmap(lambda a, v_, b, k_: _rowsum(a * v_) + _rowsum(b * k_), dru, v, dkb, k)
            pq = _hmap(lambda a, am, b, qk_: a * am + b * qk_, da, amat, dqk, qk)
            ekk = _hmap(lambda a, b: _rowsum(a * b), dkd, kd)
            dgc = _hmap(lambda pq_, a, rw, b, qd_, e, gl_: (
                _rowsum(pq_) - _mmsel(pq_, c["ones"], TN) + (_rowsum(a * rw) + _rowsum(b * qd_) - e)
                + jnp.where(c["row1"] == CHUNK - 1, _colsum(e) + gl_, 0.0)), pq, drw, rhs_w, dqd, qd, ekk, dgl)
            dv = _hmap(mul, dru, beta)
            dyq = _hmap(lambda g_, sl: _through_norm_silu(g_, yq_ref[0, r, sl], GDN_DK ** -0.5), dq, sls)
            dyk = _hmap(lambda g_, sl: _through_norm_silu(g_, yk_ref[0, r, sl], 1.0), dk, sls)
            dyv = _hmap(lambda g_, sl: _through_norm_silu(g_, yv_ref[0, r, sl], None), dv, sls)
            dgs = jnp.zeros((CHUNK, LANES), F32)
            for ih in heads:
                ds_scr[ih] = ds_new[ih]
                dq_ref[0, r, sls[ih]] = dyq[ih]
                dk_ref[0, r, sls[ih]] = dyk[ih]
                dv_ref[0, r, sls[ih]] = dyv[ih]
                dgs = dgs + jnp.where(c["lane"] == ih, dbeta[ih], jnp.where(c["lane"] == ih + 8, dgc[ih], 0.0))
            dgs_ref[0, r, :] = dgs
            return carry

        lax.fori_loop(0, ncb, chunk, 0)

    res = _pcall(
        wrap(body), name="gdn_chunk_bwd", grid=grid,
        in_specs=[sp["wide"](), sp["wide"](), sp["wide"](), sp["gs"](), sp["gr"], sp["wide"](), sp["st"], sp["ti"],
                  ypre_spec(0), ypre_spec(1), ypre_spec(2)] + any_spec,
        out_specs=[sp["wide"](), sp["wide"](), sp["wide"](), sp["gs"]()] + any_spec,
        out_shape=[jax.ShapeDtypeStruct((bsz, seq, D_MODEL), F32)] * 3 + [jax.ShapeDtypeStruct((bsz, seq, LANES), F32)]
        + rider_shapes,
        scratch_shapes=[pltpu.VMEM((hb, GDN_DK, GDN_DK), F32)] + rider_sems,
        compiler_params=_params(("arbitrary", "arbitrary", "arbitrary")),
    )(qn, kn, vv, gs, gr, do, sts, tis, ypre, ypre, ypre, *riders)
    return res[:4], res[4:]


def _gates_bwd(proj, dgs, ddt, dacs, gp, sp, tm):
    t = proj.shape[0]

    def body(i, sm_ref, dgs_ref, ddt_ref, dacs_ref, gp_ref, sp_ref, dsm_ref, dgp_ref, dsp_ref):
        sm = sm_ref[...]
        lane = lax.broadcasted_iota(jnp.int32, sm.shape, 1)
        is_g = (lane >= 8) & (lane < 16)
        is_dt = (lane >= 16) & (lane < 32)
        dgs = dgs_ref[...]
        back = _mmx(_block_tri(tm, True), jnp.where(is_g, dgs, 0.0) + dacs_ref[...])
        beta = _sigmoid(sm)
        bias = gp_ref[1:2, :] + sp_ref[1:2, :]
        xb = sm + bias
        soft, dsoft = _softplus(xb), _sigmoid(xb)
        g_neg = -jnp.exp(gp_ref[0:1, :])
        a_neg = -jnp.exp(sp_ref[0:1, :])
        dg = jnp.where(is_g, back * g_neg, 0.0)
        dxb_g = dg * dsoft
        dxb_dt = jnp.where(is_dt, (ddt_ref[...] + back * a_neg) * dsoft, 0.0)
        dsm_ref[...] = (jnp.where(lane < 8, dgs * beta * (1.0 - beta), dxb_g) + dxb_dt).astype(BF16)
        dgp_ref[1:2, :] += _colsum(dxb_g)
        dgp_ref[0:1, :] += _colsum(dg * soft)
        dsp_ref[1:2, :] += _colsum(dxb_dt)
        dsp_ref[0:1, :] += jnp.where(is_dt[0:1, :], _colsum(back * soft) * a_neg, 0.0)

    ins = [("row", proj, LANES, SMALL_CB), ("row", dgs, LANES, 0), ("row", ddt, LANES, 0), ("row", dacs, LANES, 0),
           ("full", gp), ("full", sp)]
    return _rowwise("gates_bwd", body, t, tm, ins, [(LANES, BF16)], accs=[(SUBLANES, LANES), (SUBLANES, LANES)])


def _dh1_first_bwd(dproj, wp_in, x, dx1, w, scatter_riders):
    d = D_MODEL

    def epilogue(dh, x_ref, dx1_ref, w_ref, dx_ref, dw_ref):
        xh, r = _rms(x_ref[...], d)
        dw_ref[...] += _colsum(dh * xh)
        dx_ref[...] = dx1_ref[...] + _rms_bwd(xh, r, dh * w_ref[...], d)

    return _matmul_rows("mm_dh1_first_bwd", dproj, wp_in, "nt", epilogue, [x, dx1], [w], [(d, F32)], accs=[(1, d)],
                        tk=PROJ_W // 2, scatter_riders=scatter_riders)


def _gather_two_level(name, arrays):
    n = len(arrays)
    n_sem = 7

    def body(*refs):
        ins, outs = refs[:n], refs[n:2 * n]
        send_sems, recv_sems, loc_sems = refs[2 * n:]
        x, y, c = lax.axis_index("x"), lax.axis_index("y"), lax.axis_index("c")
        slot = lambda px, py, pc: 4 * px + 2 * py + pc
        sibling = (x, y, 1 - c)
        chips = [(1 - x, y), (x, 1 - y), (1 - x, 1 - y)]

        def copy(t, k, src, block, to):
            return pltpu.make_async_remote_copy(
                src_ref=src, dst_ref=outs[t].at[block], send_sem=send_sems.at[t, k], recv_sem=recv_sems.at[t, k],
                device_id=to, device_id_type=pl.DeviceIdType.MESH)

        own, first, passed = [], [], []
        for t in range(n):
            own.append(pltpu.make_async_copy(ins[t], outs[t].at[slot(x, y, c)], loc_sems.at[t]))
            first.append(copy(t, 0, ins[t], slot(x, y, c), sibling))
            first += [copy(t, 1 + j, ins[t], slot(x, y, c), (px, py, c)) for j, (px, py) in enumerate(chips)]
        for cp in own + first:
            cp.start()
        for t in range(n):
            for j, (px, py) in enumerate(chips):
                copy(t, 1 + j, ins[t], slot(px, py, c), (px, py, c)).wait_recv()
                fwd = copy(t, 4 + j, outs[t].at[slot(px, py, c)], slot(px, py, c), sibling)
                fwd.start()
                passed.append(fwd)
        for t in range(n):
            copy(t, 0, ins[t], slot(x, y, 1 - c), sibling).wait_recv()
            for j, (px, py) in enumerate(chips):
                copy(t, 4 + j, ins[t], slot(px, py, 1 - c), sibling).wait_recv()
        for cp in first + passed:
            cp.wait_send()
        for cp in own:
            cp.wait()

    return _pcall(
        body, name=name,
        in_specs=[pl.BlockSpec(memory_space=pl.ANY)] * n,
        out_specs=[pl.BlockSpec(memory_space=pl.ANY)] * n,
        out_shape=_exchange_out_shapes(arrays, False),
        scratch_shapes=[pltpu.SemaphoreType.DMA((n, n_sem)), pltpu.SemaphoreType.DMA((n, n_sem)), pltpu.SemaphoreType.DMA((n,))],
    )(*arrays)


def _exchange_out_shapes(arrays, scatter):
    return [jax.ShapeDtypeStruct(a.shape if scatter else (N_DEV,) + a.shape, a.dtype) for a in arrays]


def _exchange_sems(n):
    return [pltpu.SemaphoreType.DMA((n, N_DEV - 1)), pltpu.SemaphoreType.DMA((n, N_DEV - 1)), pltpu.SemaphoreType.DMA((n,))]


def _exchange_phase(ins, outs, sems, scatter, start):
    send_sems, recv_sems, loc_sems = sems
    x, y, c = lax.axis_index("x"), lax.axis_index("y"), lax.axis_index("c")
    me = 4 * x + 2 * y + c
    for t in range(len(ins)):
        loc = pltpu.make_async_copy(ins[t].at[me] if scatter else ins[t], outs[t].at[me], loc_sems.at[t])
        if start:
            loc.start()
        else:
            loc.wait()
        for k in range(N_DEV - 1):
            bx, by, bc = ((k + 1) >> 2) & 1, ((k + 1) >> 1) & 1, (k + 1) & 1
            px = 1 - x if bx else x
            py = 1 - y if by else y
            pc = 1 - c if bc else c
            peer = 4 * px + 2 * py + pc
            src = ins[t].at[peer] if scatter else ins[t]
            copy = lambda dst: pltpu.make_async_remote_copy(
                src_ref=src, dst_ref=dst, send_sem=send_sems.at[t, k], recv_sem=recv_sems.at[t, k],
                device_id=(px, py, pc), device_id_type=pl.DeviceIdType.MESH)
            if start:
                copy(outs[t].at[me]).start()
            else:
                copy(outs[t].at[me]).wait_send()
                copy(outs[t].at[peer]).wait_recv()


def _adam_math(w, g, m, v):
    m = ADAM_B1 * m + (1.0 - ADAM_B1) * g
    v = ADAM_B2 * v + (1.0 - ADAM_B2) * (g * g)
    m_hat = m / (1.0 - ADAM_B1 ** ADAM_STEP)
    v_hat = v / (1.0 - ADAM_B2 ** ADAM_STEP)
    delta = -ADAM_LR * (m_hat / (jnp.sqrt(v_hat) + ADAM_EPS) + ADAM_WD * w)
    return delta, m, v


def _adam_big(name, parts, w, m, v, tm):
    r, c = w.shape
    tm = tm if r % tm == 0 else r

    def body(p_ref, w_ref, m_ref, v_ref, g_ref, d_ref, nm_ref, nv_ref):
        g = p_ref[0].astype(F32)
        for s in range(1, N_DEV):
            g = g + p_ref[s].astype(F32)
        g_ref[...] = g
        d_ref[...], nm_ref[...], nv_ref[...] = _adam_math(w_ref[...], g, m_ref[...], v_ref[...])

    blk = lambda: pl.BlockSpec((tm, c), lambda i: (i, 0))
    return _pcall(
        body, name=name, grid=(r // tm,),
        in_specs=[pl.BlockSpec((N_DEV, tm, c), lambda i: (0, i, 0)), blk(), blk(), blk()],
        out_specs=[blk(), blk(), blk(), blk()],
        out_shape=[jax.ShapeDtypeStruct((r, c), F32)] * 4,
        compiler_params=_params(("parallel",)),
    )(parts, w, m, v)


SMALL_ROWS = 56
ROW_DD, ROW_LOSS = 5, 6


def _small_sum(gathered):
    def body(g_ref, o_ref, x_ref):
        s = g_ref[0]
        for dev in range(1, N_DEV):
            s = s + g_ref[dev]
        o_ref[...] = s
        ri = lax.broadcasted_iota(jnp.int32, (D_MODEL, LANES), 0)
        ro = lax.broadcasted_iota(jnp.int32, (D_MODEL, LANES), 1)
        heads = _mmx(jnp.broadcast_to(s[ROW_DD:ROW_DD + 1, :], (SUBLANES, D_MODEL)), (ri // SSD_P == ro).astype(F32))
        loss = _rowsum(jnp.broadcast_to(s[ROW_LOSS:ROW_LOSS + 1, :], (SUBLANES, D_MODEL)))
        row = lax.broadcasted_iota(jnp.int32, (SUBLANES, LANES), 0)
        x_ref[...] = jnp.where(row == 0, heads, jnp.broadcast_to(loss, (SUBLANES, LANES)))

    return _pcall(
        body, name="small_sum",
        out_shape=[jax.ShapeDtypeStruct((SMALL_ROWS, D_MODEL), F32), jax.ShapeDtypeStruct((SUBLANES, LANES), F32)],
        compiler_params=_params(None),
    )(gathered)


def _adam_small(g, w, m, v):
    def body(g_ref, w_ref, m_ref, v_ref, d_ref, nm_ref, nv_ref):
        d_ref[...], nm_ref[...], nv_ref[...] = _adam_math(w_ref[...], g_ref[...], m_ref[...], v_ref[...])

    return _pcall(body, name="adam_small", out_shape=[jax.ShapeDtypeStruct(g.shape, F32)] * 3,
                  compiler_params=_params(None))(g, w, m, v)


def _pack(pieces, rows):
    flat = jnp.concatenate([p.reshape(-1).astype(F32) for p in pieces])
    return jnp.pad(flat, (0, rows * D_MODEL - flat.shape[0])).reshape(rows, D_MODEL)


def _unpack(packed, shapes):
    flat = packed.reshape(-1)
    out, off = [], 0
    for shp in shapes:
        size = 1
        for s in shp:
            size *= s
        out.append(flat[off:off + size].reshape(shp))
        off += size
    return out


def _permute_in(w):
    pad = jnp.zeros((w.shape[0], PROJ_W - D_IN), w.dtype)
    return jnp.concatenate([w[:, 0:4096], w[:, 4112:6672], w[:, 4096:4112], w[:, 6672:6688], pad], axis=1)


def _unpermute_in(g):
    return jnp.concatenate([g[:, 0:4096], g[:, 6656:6672], g[:, 4096:6656], g[:, 6672:6688]], axis=1)


def _lane_row(vec, start):
    return jnp.zeros((LANES,), F32).at[start:start + vec.shape[0]].set(vec)


def _cols_from_shards(g):
    return jnp.transpose(g, (1, 0, 2)).reshape(g.shape[1], N_DEV * g.shape[2])


def _cols_to_shards(a):
    return jnp.transpose(a.astype(BF16).reshape(a.shape[0], N_DEV, a.shape[1] // N_DEV), (1, 0, 2))


def _rows_to_shards(a):
    return a.astype(BF16).reshape(N_DEV, a.shape[0] // N_DEV, a.shape[1])


def _local_step(x, tgt, wp_in, rest, p, rest_is_sharded):
    bsz, seq, d = x.shape
    t = bsz * seq
    x2 = x.reshape(t, d)
    tgt2 = tgt.reshape(t, d)
    tm = min(256, seq)
    tm_wide = min(256, seq)
    sb = min(512, seq)

    gp = jnp.zeros((SUBLANES, LANES), F32).at[0].set(_lane_row(p["gdn_a_log"], 8)).at[1].set(_lane_row(p["gdn_dt_bias"], 8))
    sp = jnp.zeros((SUBLANES, LANES), F32).at[0].set(_lane_row(p["ssd_a_log"], 16)).at[1].set(_lane_row(p["ssd_dt_bias"], 16))
    dvec = jnp.repeat(p["ssd_d"], SSD_P).reshape(1, d)
    row = lambda v: v.reshape(1, -1)
    pre_mix, post_mix, pre_ffn, post_ffn = (row(p[k]) for k in ("pre_mix_norm", "post_mix_norm", "pre_ffn_norm", "post_ffn_norm"))
    gnw, snw = row(p["gdn_norm_w"]), row(p["ssd_norm_w"])
    gcw, scw, scb, fcw, fcb = p["gdn_conv_w"], p["ssd_conv_w"], row(p["ssd_conv_b"]), p["ffn_conv_w"], row(p["ffn_conv_b"])

    h1 = _norm_cast("norm_in", x2, pre_mix, tm)
    proj = _matmul("mm_proj", h1, wp_in, "nn", F32)
    b3 = lambda a: a.reshape(bsz, seq, a.shape[-1])
    b2 = lambda a: a.reshape(t, a.shape[-1])
    rows_of = lambda a, lo, n: jnp.transpose(a[:, lo:lo + n].reshape(bsz, seq // CHUNK, CHUNK, n), (0, 1, 3, 2))
    qn, kn, vv, gs, ypre_gdn = _gdn_prep(proj, gcw, gp, seq, tm)
    gr = rows_of(gs, 8, GDN_HEADS)
    qn, kn, vv, gs = b3(qn), b3(kn), b3(vv), b3(gs)
    (o_gdn, gdn_st, gdn_ti), gathered = _gdn_chunk_fwd(qn, kn, vv, gs, gr, bsz, seq, sb, list(rest) if rest_is_sharded else [])
    if rest_is_sharded:
        w_out, w_up, w_down = gathered[0].reshape(-1, d), _cols_from_shards(gathered[1]), gathered[2].reshape(-1, d)
    else:
        w_out, w_up, w_down = rest
    o_gdn = b2(o_gdn)
    xs, bc, dtx, acsx, acs, ypre_ssd = _ssd_prep(proj, scw, scb, sp, seq, tm)
    ar = rows_of(acs, 16, SSD_HEADS)
    y_ssd, ssd_st = _ssd_chunk_fwd(b3(xs), b3(bc), b3(dtx), b3(acsx), b3(acs), ar, bsz, seq, sb)
    y_ssd = b2(y_ssd)
    mixin = _gate_norm(o_gdn, y_ssd, xs, proj, gnw, snw, dvec, tm)
    mix, x1, h2 = _out_mid(mixin, w_out, x2, post_mix, pre_ffn)
    u_pre = _matmul("mm_up", h2, w_up, "nn", F32)
    act, u = _ffn_act(u_pre, fcw, fcb, seq, tm_wide)
    dy, df, loss_lanes, d_post_ffn = _down_final(act, w_down, x1, tgt2, post_ffn)

    g_down = _matmul("mm_dw_down", act, df, "tn", BF16, tm=1408, tk=2048)
    dact = _matmul("mm_dact", df, w_down, "nt", F32, tn=1408)
    du_pre, d_fcw, d_fcb = _ffn_bwd(u, u_pre, dact, fcw, seq, tm_wide)
    g_up = _matmul("mm_dw_up", h2, du_pre, "tn", BF16, tk=2048)
    dx1, dmix, d_post_mix, d_pre_ffn = _dh2_mid_bwd(du_pre, w_up, x1, mix, dy, post_mix, pre_ffn)
    g_out = _matmul("mm_dw_out", mixin, dmix, "tn", BF16, tk=2048)
    do_gdn, dza, dy_ssd, dxs_d, dzs, d_gnw, d_snw, d_dd = _dmixin_gate_norm_bwd(dmix, w_out, o_gdn, y_ssd, xs, proj, gnw, snw, dvec)
    dyx, dybc, ddt, dacs, d_scb_x, d_scb_bc = _ssd_chunk_bwd(
        b3(xs), b3(bc), b3(dtx), b3(acsx), b3(acs), ar, b3(dy_ssd), ssd_st, b3(ypre_ssd), b3(dxs_d), bsz, seq, sb)
    dyx, dybc, ddt, dacs = b2(dyx), b2(dybc), b2(ddt), b2(dacs)
    d_scb = jnp.concatenate([d_scb_x, d_scb_bc], axis=1)
    riders = [_rows_to_shards(g_out), _cols_to_shards(g_up), _rows_to_shards(g_down)] if rest_is_sharded else []
    dgdn, received = _gdn_chunk_bwd(qn, kn, vv, gs, gr, b3(do_gdn), gdn_st, gdn_ti, b3(ypre_gdn), bsz, seq, min(256, seq), riders)
    if rest_is_sharded:
        g_out, g_up, g_down = received
    dyq, dyk, dyv, dgs = (b2(a) for a in dgdn)
    dsm, d_gp, d_sp = _gates_bwd(proj, dgs, ddt, dacs, gp, sp, tm)
    dproj, d_gcw, d_scw = _assemble_dproj((dyq, dyk, dyv), dza, dzs, (dyx, dybc), dsm, proj, gcw, scw, seq, tm)
    g_in = _matmul("mm_dw_in", h1, dproj, "tn", BF16, tk=2048)
    if rest_is_sharded:
        (dx, d_pre_mix), (g_in,) = _dh1_first_bwd(dproj, wp_in, x2, dx1, pre_mix, [_cols_to_shards(_unpermute_in(g_in))])
    else:
        dx, d_pre_mix = _dh1_first_bwd(dproj, wp_in, x2, dx1, pre_mix, [])

    small = dict(pre_mix_norm=d_pre_mix, ssd_norm_w=d_snw, post_mix_norm=d_post_mix, pre_ffn_norm=d_pre_ffn,
                 post_ffn_norm=d_post_ffn, dd_lanes=d_dd, loss_lanes=loss_lanes, gdn_gates=d_gp, ssd_gates=d_sp,
                 gdn_norm_w=d_gnw, gdn_conv_w=d_gcw[0:4], ssd_conv_w=d_scw[0:4], ssd_conv_b=d_scb,
                 ffn_conv_w=d_fcw[0:3], ffn_conv_b=d_fcb)
    return dx.reshape(bsz, seq, d), g_in, g_out, g_up, g_down, small


def kernel(x, pre_mix_norm, w_in, gdn_conv_w, gdn_a_log, gdn_dt_bias, gdn_norm_w, ssd_conv_w, ssd_conv_b, ssd_a_log, ssd_dt_bias, ssd_d, ssd_norm_w, w_out, post_mix_norm, pre_ffn_norm, w_up, ffn_conv_w, ffn_conv_b, w_down, post_ffn_norm, loss_target, m_pre_mix_norm, m_w_in, m_gdn_conv_w, m_gdn_a_log, m_gdn_dt_bias, m_gdn_norm_w, m_ssd_conv_w, m_ssd_conv_b, m_ssd_a_log, m_ssd_dt_bias, m_ssd_d, m_ssd_norm_w, m_w_out, m_post_mix_norm, m_pre_ffn_norm, m_w_up, m_ffn_conv_w, m_ffn_conv_b, m_w_down, m_post_ffn_norm, v_pre_mix_norm, v_w_in, v_gdn_conv_w, v_gdn_a_log, v_gdn_dt_bias, v_gdn_norm_w, v_ssd_conv_w, v_ssd_conv_b, v_ssd_a_log, v_ssd_dt_bias, v_ssd_d, v_ssd_norm_w, v_w_out, v_post_mix_norm, v_pre_ffn_norm, v_w_up, v_ffn_conv_w, v_ffn_conv_b, v_w_down, v_post_ffn_norm):
    names = ["pre_mix_norm", "w_in", "gdn_conv_w", "gdn_a_log", "gdn_dt_bias", "gdn_norm_w", "ssd_conv_w", "ssd_conv_b",
             "ssd_a_log", "ssd_dt_bias", "ssd_d", "ssd_norm_w", "w_out", "post_mix_norm", "pre_ffn_norm", "w_up",
             "ffn_conv_w", "ffn_conv_b", "w_down", "post_ffn_norm"]
    w_args = [pre_mix_norm, w_in, gdn_conv_w, gdn_a_log, gdn_dt_bias, gdn_norm_w, ssd_conv_w, ssd_conv_b, ssd_a_log, ssd_dt_bias, ssd_d, ssd_norm_w, w_out, post_mix_norm, pre_ffn_norm, w_up, ffn_conv_w, ffn_conv_b, w_down, post_ffn_norm]
    m_args = [m_pre_mix_norm, m_w_in, m_gdn_conv_w, m_gdn_a_log, m_gdn_dt_bias, m_gdn_norm_w, m_ssd_conv_w, m_ssd_conv_b, m_ssd_a_log, m_ssd_dt_bias, m_ssd_d, m_ssd_norm_w, m_w_out, m_post_mix_norm, m_pre_ffn_norm, m_w_up, m_ffn_conv_w, m_ffn_conv_b, m_w_down, m_post_ffn_norm]
    v_args = [v_pre_mix_norm, v_w_in, v_gdn_conv_w, v_gdn_a_log, v_gdn_dt_bias, v_gdn_norm_w, v_ssd_conv_w, v_ssd_conv_b, v_ssd_a_log, v_ssd_dt_bias, v_ssd_d, v_ssd_norm_w, v_w_out, v_post_mix_norm, v_pre_ffn_norm, v_w_up, v_ffn_conv_w, v_ffn_conv_b, v_w_down, v_post_ffn_norm]
    w = {k: a[0] for k, a in zip(names, w_args)}
    m = {k: a[0] for k, a in zip(names, m_args)}
    v = {k: a[0] for k, a in zip(names, v_args)}
    idx = 4 * lax.axis_index("x") + 2 * lax.axis_index("y") + lax.axis_index("c")
    big = ("w_in", "w_out", "w_up", "w_down")
    conv = ("gdn_conv_w", "ssd_conv_w", "ffn_conv_w")

    conv_local = jnp.concatenate([jnp.pad(w[k], ((0, 4 - w[k].shape[0]), (0, 0))) for k in conv], axis=1)
    g_in, g_conv = _gather_two_level("gather_weights", [w["w_in"].astype(BF16), conv_local])
    wp_in = _permute_in(_cols_from_shards(g_in))
    p = {k: w[k] for k in names if k not in big and k not in conv}
    off = 0
    for k in conv:
        cw = w[k].shape[1]
        p[k] = jnp.transpose(g_conv[:, :w[k].shape[0], off:off + cw], (1, 0, 2)).reshape(w[k].shape[0], N_DEV * cw)
        off += cw

    rest = tuple(w[k].astype(BF16) for k in ("w_out", "w_up", "w_down"))
    dx, p_in, p_out, p_up, p_down, small = _local_step(x, loss_target, wp_in, rest, p, True)

    gate_row = jnp.concatenate([small["gdn_gates"][0], small["gdn_gates"][1], small["ssd_gates"][0], small["ssd_gates"][1],
                                small["gdn_norm_w"][0], jnp.zeros((D_MODEL - 5 * LANES,), F32)]).reshape(1, D_MODEL)
    pack = _pack([small["pre_mix_norm"], small["ssd_norm_w"], small["post_mix_norm"], small["pre_ffn_norm"],
                  small["post_ffn_norm"], small["dd_lanes"], small["loss_lanes"], gate_row,
                  small["gdn_conv_w"], small["ssd_conv_w"], jnp.pad(small["ssd_conv_b"], ((0, 0), (0, 512))),
                  jnp.pad(small["ffn_conv_w"].reshape(-1), (0, 17 * D_MODEL - 3 * 2 * D_FF)),
                  jnp.pad(small["ffn_conv_b"], ((0, 0), (0, 512)))], SMALL_ROWS)
    (pack_all,) = _gather_two_level("gather_small", [pack])
    ssum, extra = _small_sum(pack_all)

    grads, deltas, new_m, new_v = {}, {}, {}, {}
    for k, parts in (("w_in", p_in), ("w_out", p_out), ("w_up", p_up), ("w_down", p_down)):
        grads[k], deltas[k], new_m[k], new_v[k] = _adam_big("adam_" + k, parts, w[k], m[k], v[k], 256)

    flat = ssum.reshape(-1)
    gate = ssum[7]
    sg = dict(pre_mix_norm=ssum[0], ssd_norm_w=ssum[1], post_mix_norm=ssum[2], pre_ffn_norm=ssum[3], post_ffn_norm=ssum[4],
              gdn_a_log=gate[8:16], gdn_dt_bias=gate[LANES + 8:LANES + 16], ssd_a_log=gate[2 * LANES + 16:2 * LANES + 32],
              ssd_dt_bias=gate[3 * LANES + 16:3 * LANES + 32], gdn_norm_w=gate[4 * LANES:5 * LANES], ssd_d=extra[0, 0:SSD_HEADS])
    o = 8 * D_MODEL
    full_gcw = flat[o:o + 4 * 3072].reshape(4, 3072)
    o += 12 * D_MODEL
    full_scw = flat[o:o + 4 * 1536].reshape(4, 1536)
    o += 6 * D_MODEL
    sg["ssd_conv_b"] = flat[o:o + 1536]
    o += 2 * D_MODEL
    full_fcw = flat[o:o + 3 * 2 * D_FF].reshape(3, 2 * D_FF)
    o += 17 * D_MODEL
    sg["ffn_conv_b"] = flat[o:o + 2 * D_FF]
    for k, full in (("gdn_conv_w", full_gcw), ("ssd_conv_w", full_scw), ("ffn_conv_w", full_fcw)):
        cw = w[k].shape[1]
        sg[k] = lax.dynamic_slice_in_dim(full, idx * cw, cw, axis=1)
    small_names = [k for k in names if k not in big]
    rows = 24
    gpk = _pack([sg[k] for k in small_names], rows)
    dpk, mpk, vpk = _adam_small(gpk, _pack([w[k] for k in small_names], rows), _pack([m[k] for k in small_names], rows),
                                _pack([v[k] for k in small_names], rows))
    shapes = [w[k].shape for k in small_names]
    for k, g_, d_, m_, v_ in zip(small_names, _unpack(gpk, shapes), _unpack(dpk, shapes), _unpack(mpk, shapes), _unpack(vpk, shapes)):
        grads[k], deltas[k], new_m[k], new_v[k] = g_, d_, m_, v_

    loss = extra[1, 0]
    lead = lambda a: a[None]
    return (loss, dx, *[lead(grads[k]) for k in names], *[lead(deltas[k]) for k in names],
            *[lead(new_m[k]) for k in names], *[lead(new_v[k]) for k in names])
```

```python
import functools

import jax
import jax.numpy as jnp
from jax import lax
from jax.experimental import pallas as pl
from jax.experimental.pallas import tpu as pltpu

F32 = jnp.float32
BF16 = jnp.bfloat16
MXU_DTYPE = jnp.bfloat16
HIGHEST = lax.Precision.HIGHEST
VMEM_LIMIT_V7X = 48 * 1024 * 1024
SUBLANES = 8
LANES = 128

D_MODEL = 1024
GDN_HEADS = 8
GDN_DK = 128
SSD_HEADS = 16
SSD_P = 64
SSD_GROUPS = 2
SSD_HPG = 8
SSD_N = 128
CHUNK = 128
D_FF = 2816
EPS = 1e-6
N_DEV = 8
PROJ_W = 7168
SMALL_CB = 52
D_IN = 6688

ADAM_LR = 0.001
ADAM_B1 = 0.9
ADAM_B2 = 0.999
ADAM_EPS = 1e-08
ADAM_WD = 0.01
ADAM_STEP = 10

NN = (((1,), (0,)), ((), ()))
NT = (((1,), (1,)), ((), ()))
TN = (((0,), (0,)), ((), ()))


def _pcall(body, **kw):
    return pl.pallas_call(body, **kw)


def _mm(a, b, dims=NN):
    return lax.dot_general(a.astype(MXU_DTYPE), b.astype(MXU_DTYPE), dims, preferred_element_type=F32)


def _mmx(a, b, dims=NN):
    return lax.dot_general(a, b, dims, precision=HIGHEST, preferred_element_type=F32)


def _split(a):
    hi = a.astype(MXU_DTYPE)
    return hi, (a - hi.astype(F32)).astype(MXU_DTYPE)


def _mm3(a, b, dims=NN):
    (ah, al), (bh, bl) = _split(a), _split(b)
    dot = lambda p, q: lax.dot_general(p, q, dims, preferred_element_type=F32)
    return dot(ah, bh) + (dot(ah, bl) + dot(al, bh))


def _mmsel(a, sel, dims=NN, terms=2):
    s = sel.astype(MXU_DTYPE)
    out = None
    for _ in range(terms):
        part = a.astype(MXU_DTYPE)
        a = a - part.astype(F32)
        prod = lax.dot_general(part, s, dims, preferred_element_type=F32)
        out = prod if out is None else out + prod
    return out


def _sigmoid(x):
    return 0.5 * jnp.tanh(0.5 * x) + 0.5


def _softplus(x):
    return jnp.maximum(x, 0.0) + jnp.log(1.0 + jnp.exp(-jnp.abs(x)))


def _dsilu(x, s):
    return s * (1.0 + x * (1.0 - s))


def _rowsum(x):
    return jnp.sum(x, axis=1, keepdims=True)


def _colsum(x):
    return jnp.sum(x, axis=0, keepdims=True)


def _pick(dim, pref):
    if dim <= pref:
        return dim
    best = None
    t = LANES
    while t <= pref:
        if dim % t == 0:
            best = t
        t += LANES
    return dim if best is None else best


def _params(sem):
    return pltpu.CompilerParams(dimension_semantics=sem, vmem_limit_bytes=VMEM_LIMIT_V7X)


def _matmul(name, a, b, mode, out_dtype, tm=1024, tn=1024, tk=1024):
    if mode == "nn":
        (m, k), (_, n) = a.shape, b.shape
    elif mode == "nt":
        (m, k), (n, _) = a.shape, b.shape
    else:
        (k, m), (_, n) = a.shape, b.shape
    tm, tn, tk = _pick(m, tm), _pick(n, tn), _pick(k, tk)
    nk = k // tk
    if mode == "tn":
        a_spec = pl.BlockSpec((tk, tm), lambda i, j, kk: (kk, i))
    else:
        a_spec = pl.BlockSpec((tm, tk), lambda i, j, kk: (i, kk))
    if mode == "nt":
        b_spec = pl.BlockSpec((tn, tk), lambda i, j, kk: (j, kk))
    else:
        b_spec = pl.BlockSpec((tk, tn), lambda i, j, kk: (kk, j))
    dims = {"nn": NN, "nt": NT, "tn": TN}[mode]

    def body(a_ref, b_ref, o_ref, *acc):
        if nk == 1:
            o_ref[...] = _mm(a_ref[...], b_ref[...], dims).astype(out_dtype)
            return
        kk = pl.program_id(2)

        @pl.when(kk == 0)
        def _():
            acc[0][...] = jnp.zeros_like(acc[0])

        acc[0][...] += _mm(a_ref[...], b_ref[...], dims)

        @pl.when(kk == nk - 1)
        def _():
            o_ref[...] = acc[0][...].astype(out_dtype)

    return _pcall(
        body, name=name, grid=(m // tm, n // tn, nk),
        in_specs=[a_spec, b_spec],
        out_specs=pl.BlockSpec((tm, tn), lambda i, j, kk: (i, j)),
        out_shape=jax.ShapeDtypeStruct((m, n), out_dtype),
        scratch_shapes=[pltpu.VMEM((tm, tn), F32)] if nk > 1 else [],
        compiler_params=_params(("parallel", "parallel", "arbitrary")),
    )(a, b)


def _matmul_rows(name, a, b, mode, epilogue, row_ins, full_ins, outs, accs=(), tm=512, tk=1024, scatter_riders=()):
    if mode == "nn":
        (m, k), (_, n) = a.shape, b.shape
    else:
        (m, k), (n, _) = a.shape, b.shape
    tm, tk = _pick(m, tm), _pick(k, tk)
    nk = k // tk
    a_spec = pl.BlockSpec((tm, tk), lambda i, kk: (i, kk))
    b_spec = pl.BlockSpec((n, tk), lambda i, kk: (0, kk)) if mode == "nt" else pl.BlockSpec((tk, n), lambda i, kk: (kk, 0))
    dims = NT if mode == "nt" else NN
    n_row, n_full, n_out, n_acc = len(row_ins), len(full_ins), len(outs), len(accs)

    def body(a_ref, b_ref, *rest):
        ins = rest[:n_row + n_full]
        out_refs = rest[n_row + n_full:n_row + n_full + n_out]
        acc_refs = rest[n_row + n_full + n_out:n_row + n_full + n_out + n_acc]
        prod_scr = rest[-1]
        i, kk = pl.program_id(0), pl.program_id(1)

        if n_acc:
            @pl.when((i == 0) & (kk == 0))
            def _():
                for r in acc_refs:
                    r[...] = jnp.zeros_like(r)

        if nk == 1:
            epilogue(_mm(a_ref[...], b_ref[...], dims), *ins, *out_refs, *acc_refs)
            return

        @pl.when(kk == 0)
        def _():
            prod_scr[...] = jnp.zeros_like(prod_scr)

        prod_scr[...] += _mm(a_ref[...], b_ref[...], dims)

        @pl.when(kk == nk - 1)
        def _():
            epilogue(prod_scr[...], *ins, *out_refs, *acc_refs)

    grid = (m // tm, nk)
    riders = list(scatter_riders)
    n_in = 2 + n_row + n_full
    any_spec, rider_shapes, rider_sems, wrap = _riding_exchange(riders, True, n_in, n_out + n_acc, grid)
    row_ins = [r if isinstance(r, tuple) else (r, r.shape[1], 0) for r in row_ins]
    in_specs = [a_spec, b_spec] + [pl.BlockSpec((tm, w), lambda i, kk, cb=cb: (i, cb)) for _, w, cb in row_ins]
    in_specs += [pl.BlockSpec(f.shape, lambda i, kk, nd=f.ndim: (0,) * nd) for f in full_ins]
    row_ins = [r for r, _, _ in row_ins]
    out_specs = [pl.BlockSpec((tm, w), lambda i, kk: (i, 0)) for w, _ in outs]
    out_specs += [pl.BlockSpec(s, lambda i, kk: (0, 0)) for s in accs]
    out_shape = [jax.ShapeDtypeStruct((m, w), dt) for w, dt in outs] + [jax.ShapeDtypeStruct(s, F32) for s in accs]
    res = _pcall(
        wrap(body), name=name, grid=grid,
        in_specs=in_specs + any_spec, out_specs=out_specs + any_spec, out_shape=out_shape + rider_shapes,
        scratch_shapes=[pltpu.VMEM((tm, n), F32)] + rider_sems,
        compiler_params=_params(("arbitrary", "arbitrary")),
    )(a, b, *row_ins, *full_ins, *riders)
    return (res[:n_out + n_acc], res[n_out + n_acc:]) if riders else res


def _rowwise(name, body, n_rows, tm, ins, outs, accs=()):
    arrays, in_specs = [], []
    last8 = n_rows // SUBLANES - 1
    per = tm // SUBLANES
    for spec in ins:
        kind, arr = spec[0], spec[1]
        if kind == "full":
            in_specs.append(pl.BlockSpec(arr.shape, lambda i, nd=arr.ndim: (0,) * nd))
        else:
            w, cb = spec[2], spec[3]
            if kind == "row":
                in_specs.append(pl.BlockSpec((tm, w), lambda i, cb=cb: (i, cb)))
            elif kind == "prev":
                in_specs.append(pl.BlockSpec((SUBLANES, w), lambda i, cb=cb: (jnp.maximum(i * per - 1, 0), cb)))
            else:
                in_specs.append(pl.BlockSpec((SUBLANES, w), lambda i, cb=cb: (jnp.minimum((i + 1) * per, last8), cb)))
        arrays.append(arr)
    out_shape = [jax.ShapeDtypeStruct((n_rows, w), dt) for (w, dt) in outs]
    out_shape += [jax.ShapeDtypeStruct(s, F32) for s in accs]
    out_specs = [pl.BlockSpec((tm, w), lambda i: (i, 0)) for (w, _) in outs]
    out_specs += [pl.BlockSpec(s, lambda i: (0, 0)) for s in accs]
    n_io = len(ins) + len(outs)

    def kern(*refs):
        i = pl.program_id(0)
        if accs:
            @pl.when(i == 0)
            def _():
                for r in refs[n_io:]:
                    r[...] = jnp.zeros_like(r)
        body(i, *refs)

    res = _pcall(
        kern, name=name, grid=(n_rows // tm,), in_specs=in_specs, out_specs=out_specs, out_shape=out_shape,
        compiler_params=_params(("arbitrary",)),
    )(*arrays)
    return res


def _shift_down(x, halo, j):
    r = pltpu.roll(x, j, 0)
    hr = pltpu.roll(halo, j, 0)
    rows = lax.broadcasted_iota(jnp.int32, (SUBLANES, x.shape[1]), 0)
    top = jnp.where(rows < j, hr, r[0:SUBLANES])
    return jnp.concatenate([top, r[SUBLANES:]], axis=0)


def _shift_up(x, halo, j):
    tm = x.shape[0]
    r = pltpu.roll(x, tm - j, 0)
    hr = pltpu.roll(halo, SUBLANES - j, 0)
    rows = lax.broadcasted_iota(jnp.int32, (SUBLANES, x.shape[1]), 0)
    bot = jnp.where(rows >= SUBLANES - j, hr, r[tm - SUBLANES:])
    return jnp.concatenate([r[:tm - SUBLANES], bot], axis=0)


def _conv_taps(x, halo, kw):
    return [x if kw - 1 - k == 0 else _shift_down(x, halo, kw - 1 - k) for k in range(kw)]


def _conv(taps, w):
    y = taps[0] * w[0:1]
    for k in range(1, len(taps)):
        y = y + taps[k] * w[k:k + 1]
    return y


def _rms(x, width):
    r = lax.rsqrt(jnp.sum(x * x, axis=-1, keepdims=True) * (1.0 / width) + EPS)
    return x * r, r


def _rms_bwd(xh, r, dxh, width):
    return r * (dxh - xh * (jnp.sum(dxh * xh, axis=-1, keepdims=True) * (1.0 / width)))


def _seq_flags(i, seq, tm):
    nps = seq // tm
    pos = i % nps
    return jnp.where(pos == 0, 0.0, 1.0), jnp.where(pos == nps - 1, 0.0, 1.0)


def _norm_proj(x, w, wp, tm=1024, tn=1024):
    t, d = x.shape
    n = wp.shape[1]
    tm, tn = _pick(t, tm), _pick(n, tn)

    def body(x_ref, w_ref, b_ref, h_ref, o_ref, h_scr):
        @pl.when(pl.program_id(1) == 0)
        def _():
            xh, _ = _rms(x_ref[...], d)
            h = (xh * w_ref[...]).astype(BF16)
            h_scr[...] = h
            h_ref[...] = h

        o_ref[...] = _mm(h_scr[...], b_ref[...])

    return _pcall(
        body, name="mm_norm_proj", grid=(t // tm, n // tn),
        in_specs=[pl.BlockSpec((tm, d), lambda i, j: (i, 0)), pl.BlockSpec((1, d), lambda i, j: (0, 0)),
                  pl.BlockSpec((d, tn), lambda i, j: (0, j))],
        out_specs=[pl.BlockSpec((tm, d), lambda i, j: (i, 0)), pl.BlockSpec((tm, tn), lambda i, j: (i, j))],
        out_shape=[jax.ShapeDtypeStruct((t, d), BF16), jax.ShapeDtypeStruct((t, n), F32)],
        scratch_shapes=[pltpu.VMEM((tm, d), BF16)],
        compiler_params=_params(("parallel", "arbitrary")),
    )(x, w, wp)


def _gdn_prep(proj, cw, gp, seq, tm):
    t = proj.shape[0]
    d = D_MODEL

    def body(i, q_ref, qh_ref, k_ref, kh_ref, v_ref, vh_ref, sm_ref, cw_ref, gp_ref, qn_ref, kn_ref, vv_ref, gs_ref, ypre_ref):
        keep, _ = _seq_flags(i, seq, tm)
        for x_ref, h_ref, o_ref, off, scale in ((q_ref, qh_ref, qn_ref, 0, GDN_DK ** -0.5),
                                               (k_ref, kh_ref, kn_ref, d, 1.0), (v_ref, vh_ref, vv_ref, 2 * d, None)):
            y = _conv(_conv_taps(x_ref[...], h_ref[...] * keep, 4), cw_ref[:, off:off + d])
            ypre_ref[:, off:off + d] = y
            a = y * _sigmoid(y)
            if scale is None:
                o_ref[...] = a
            else:
                for hh in range(GDN_HEADS):
                    s = a[:, hh * GDN_DK:(hh + 1) * GDN_DK]
                    n = lax.rsqrt(_rowsum(s * s) + EPS)
                    o_ref[:, hh * GDN_DK:(hh + 1) * GDN_DK] = s * (n * scale)
        sm = sm_ref[...]
        lane = lax.broadcasted_iota(jnp.int32, sm.shape, 1)
        beta = _sigmoid(sm)
        g = jnp.where((lane >= 8) & (lane < 16), -jnp.exp(gp_ref[0:1, :]) * _softplus(sm + gp_ref[1:2, :]), 0.0)
        gs_ref[...] = jnp.where(lane < 8, beta, _mmx(_block_tri(tm, False), g))

    ins = []
    for cb in range(3):
        ins += [("row", proj, d, cb), ("prev", proj, d, cb)]
    ins += [("row", proj, LANES, SMALL_CB), ("full", cw), ("full", gp)]
    return _rowwise("gdn_prep", body, t, tm, ins, [(d, F32), (d, F32), (d, F32), (LANES, F32), (3 * d, F32)])


def _block_tri(tm, upper):
    ri = lax.broadcasted_iota(jnp.int32, (tm, tm), 0)
    ci = lax.broadcasted_iota(jnp.int32, (tm, tm), 1)
    tri = (ri <= ci) if upper else (ri >= ci)
    return (tri & ((ri // CHUNK) == (ci // CHUNK))).astype(F32)


def _chunk_consts():
    row = lax.broadcasted_iota(jnp.int32, (CHUNK, CHUNK), 0)
    col = lax.broadcasted_iota(jnp.int32, (CHUNK, CHUNK), 1)
    return dict(
        tril=row >= col, strict=row > col, eye=(row == col).astype(F32),
        lane=lax.broadcasted_iota(jnp.int32, (CHUNK, LANES), 1),
        row1=lax.broadcasted_iota(jnp.int32, (CHUNK, 1), 0),
        ones=jnp.ones((CHUNK, LANES), F32))


def _hmap(fn, *lists):
    return [fn(*a) for a in zip(*lists)]


def _tri_inv(nmats, eye):
    levels = CHUNK.bit_length() - 2
    x = [eye - n for n in nmats]
    p = _hmap(_mm3, nmats, nmats)
    for lvl in range(levels):
        x = _hmap(lambda xi, pi: xi + _mm3(xi, pi), x, p)
        if lvl < levels - 1:
            p = _hmap(_mm3, p, p)
    return x


def _gdn_gates(gs, gc_row, h, c):
    beta = _rowsum(jnp.where(c["lane"] == h, gs, 0.0))
    gc = _rowsum(jnp.where(c["lane"] == h + 8, gs, 0.0))
    dc = jnp.exp(jnp.where(c["tril"], gc - gc_row, -1e30))
    gl = gc[CHUNK - 1:CHUNK, :]
    return beta, dc, jnp.exp(gc), jnp.exp(gl), jnp.exp(gl - gc)


GDN_HB = GDN_HEADS


def _gdn_specs(seq, sb, hb, backward):
    assert hb == GDN_HEADS
    nsb = seq // sb
    ncb = sb // CHUNK
    order = (lambda j: nsb - 1 - j) if backward else (lambda j: j)
    specs = dict(
        wide=lambda: pl.BlockSpec((1, sb, hb * GDN_DK), lambda b, h, j: (b, order(j), h)),
        gs=lambda: pl.BlockSpec((1, sb, LANES), lambda b, h, j: (b, order(j), 0)),
        gr=pl.BlockSpec((1, ncb, GDN_HEADS, CHUNK), lambda b, h, j: (b, order(j), 0, 0)),
        st=pl.BlockSpec((1, hb, ncb * GDN_DK, GDN_DK), lambda b, h, j: (b, h, order(j), 0)),
        ti=pl.BlockSpec((1, hb, sb, CHUNK), lambda b, h, j: (b, h, order(j), 0)))
    return nsb, ncb, specs


def _riding_exchange(arrays, scatter, n_in, n_out, grid):
    n = len(arrays)
    if n == 0:
        return [], [], [], lambda body: body
    any_spec = [pl.BlockSpec(memory_space=pl.ANY)] * n

    def wrap(body):
        def wrapped(*refs):
            ins = refs[n_in:n_in + n]
            outs = refs[n_in + n + n_out:n_in + 2 * n + n_out]
            sems = refs[len(refs) - 3:]
            pid = [pl.program_id(a) for a in range(len(grid))]
            first = functools.reduce(lambda a, b: a & b, [p == 0 for p in pid])
            last = functools.reduce(lambda a, b: a & b, [p == g - 1 for p, g in zip(pid, grid)])

            @pl.when(first)
            def _():
                _exchange_phase(ins, outs, sems, scatter, start=True)

            body(*refs[:n_in], *refs[n_in + n:n_in + n + n_out], *refs[n_in + 2 * n + n_out:len(refs) - 3])

            @pl.when(last)
            def _():
                _exchange_phase(ins, outs, sems, scatter, start=False)

        return wrapped

    return any_spec, _exchange_out_shapes(arrays, scatter), _exchange_sems(n), wrap


def _gdn_chunk_fwd(qn, kn, vv, gs, gr, bsz, seq, sb, riders):
    hb = GDN_HB
    nsb, ncb, sp = _gdn_specs(seq, sb, hb, False)
    grid = (bsz, GDN_HEADS // hb, nsb)
    any_spec, rider_shapes, rider_sems, wrap = _riding_exchange(riders, False, 5, 3, grid)

    def body(q_ref, k_ref, v_ref, gs_ref, gr_ref, o_ref, st_ref, ti_ref, s_scr):
        hg = pl.program_id(1)

        @pl.when(pl.program_id(2) == 0)
        def _():
            s_scr[...] = jnp.zeros_like(s_scr)

        c = _chunk_consts()

        def chunk(n, carry):
            r = pl.ds(pl.multiple_of(n * CHUNK, CHUNK), CHUNK)
            rs = pl.ds(pl.multiple_of(n * GDN_DK, GDN_DK), GDN_DK)
            gsv = gs_ref[0, r, :]
            heads = list(range(hb))
            sls = [slice(ih * GDN_DK, (ih + 1) * GDN_DK) for ih in heads]
            q = [q_ref[0, r, sl] for sl in sls]
            k = [k_ref[0, r, sl] for sl in sls]
            v = [v_ref[0, r, sl] for sl in sls]
            beta, dc, eg, egl, ekd = zip(*[
                _gdn_gates(gsv, gr_ref[0, n, pl.ds(ih, 1), :], ih, c) for ih in heads])
            kb = _hmap(lambda a, b: a * b, k, beta)
            amat = _hmap(lambda a, b, d_: jnp.where(c["strict"], _mm(a, b, NT) * d_, 0.0), kb, k, dc)
            tinv = _tri_inv(amat, c["eye"])
            u = _hmap(lambda t_, a, b: _mm3(t_, a * b), tinv, v, beta)
            w = _hmap(lambda t_, a, b: _mm3(t_, a * b), tinv, kb, eg)
            qk = _hmap(lambda a, b, d_: _mm(a, b, NT) * d_, q, k, dc)
            s = [s_scr[ih] for ih in heads]
            v_new = _hmap(lambda a, b, s_: a - _mm(b, s_), u, w, s)
            o = _hmap(lambda a, e, s_, qk_, vn: _mm(a * e, s_) + _mm(qk_, vn), q, eg, s, qk, v_new)
            s_new = _hmap(lambda s_, e, a, f, vn: s_ * e + _mm(a * f, vn, TN), s, egl, k, ekd, v_new)
            for ih in heads:
                o_ref[0, r, sls[ih]] = o[ih]
                st_ref[0, ih, rs, :] = s[ih]
                ti_ref[0, ih, r, :] = tinv[ih]
                s_scr[ih] = s_new[ih]
            return carry

        lax.fori_loop(0, ncb, chunk, 0)

    t3 = (bsz, seq, D_MODEL)
    res = _pcall(
        wrap(body), name="gdn_chunk_fwd", grid=grid,
        in_specs=[sp["wide"](), sp["wide"](), sp["wide"](), sp["gs"](), sp["gr"]] + any_spec,
        out_specs=[sp["wide"](), sp["st"], sp["ti"]] + any_spec,
        out_shape=[jax.ShapeDtypeStruct(t3, F32),
                   jax.ShapeDtypeStruct((bsz, GDN_HEADS, (seq // CHUNK) * GDN_DK, GDN_DK), F32),
                   jax.ShapeDtypeStruct((bsz, GDN_HEADS, seq, CHUNK), F32)] + rider_shapes,
        scratch_shapes=[pltpu.VMEM((hb, GDN_DK, GDN_DK), F32)] + rider_sems,
        compiler_params=_params(("arbitrary", "arbitrary", "arbitrary")),
    )(qn, kn, vv, gs, gr, *riders)
    return res[:3], res[3:]


def _ssd_prep(proj, cw, cb, sp, seq, tm):
    t = proj.shape[0]
    d = D_MODEL
    ssd_w = SSD_HEADS * SSD_P

    def body(i, x_ref, xh_ref, bc_ref, bch_ref, sm_ref, cw_ref, cb_ref, sp_ref, xs_ref, bco_ref, dtx_ref, acsx_ref, acs_ref, ypre_ref):
        keep, _ = _seq_flags(i, seq, tm)
        y = _conv(_conv_taps(x_ref[...], xh_ref[...] * keep, 4), cw_ref[:, 0:d]) + cb_ref[:, 0:d]
        ypre_ref[:, 0:d] = y
        xs_ref[...] = y * _sigmoid(y)
        y = _conv(_conv_taps(bc_ref[...], bch_ref[...] * keep, 4), cw_ref[:, d:d + 512]) + cb_ref[:, d:d + 512]
        ypre_ref[:, d:d + 512] = y
        bco_ref[...] = y * _sigmoid(y)
        sm = sm_ref[...]
        lane = lax.broadcasted_iota(jnp.int32, sm.shape, 1)
        valid = (lane >= 16) & (lane < 32)
        dt = jnp.where(valid, _softplus(sm + sp_ref[1:2, :]), 0.0)
        adt = dt * (-jnp.exp(sp_ref[0:1, :]))
        acs = _mmx(_block_tri(tm, False), adt)
        l64 = lax.broadcasted_iota(jnp.int32, (LANES, ssd_w), 0)
        d64 = lax.broadcasted_iota(jnp.int32, (LANES, ssd_w), 1)
        e64 = (l64 - 16 == d64 // SSD_P).astype(F32)
        dtx_ref[...] = _mmsel(dt, e64, terms=3)
        acsx_ref[...] = _mmsel(acs, e64, terms=3)
        acs_ref[...] = acs

    ins = [("row", proj, d, 5), ("prev", proj, d, 5), ("row", proj, 512, 12), ("prev", proj, 512, 12),
           ("row", proj, LANES, SMALL_CB), ("full", cw), ("full", cb), ("full", sp)]
    return _rowwise("ssd_prep", body, t, tm, ins,
                    [(d, F32), (512, F32), (ssd_w, F32), (ssd_w, F32), (LANES, F32), (d + 512, F32)])


SSD_GW = SSD_HPG * SSD_P


def _ssd_head(acs, ar_ref, n, head, cbm, c):
    col = _rowsum(jnp.where(c["lane"] == head + 16, acs, 0.0))
    lm = jnp.exp(jnp.where(c["tril"], col - ar_ref[0, n, pl.ds(head, 1), :], -1e30))
    return lm, cbm * lm


def _ssd_specs(seq, sb):
    nsb = seq // sb
    ncb = sb // CHUNK
    def specs(order):
        return dict(
            wide=lambda: pl.BlockSpec((1, sb, SSD_HEADS * SSD_P), lambda b, j: (b, order(j), 0)),
            bc=lambda: pl.BlockSpec((1, sb, 2 * SSD_GROUPS * SSD_N), lambda b, j: (b, order(j), 0)),
            half=lambda: pl.BlockSpec((1, sb, SSD_GROUPS * SSD_N), lambda b, j: (b, order(j), 0)),
            small=lambda: pl.BlockSpec((1, sb, LANES), lambda b, j: (b, order(j), 0)),
            ar=pl.BlockSpec((1, ncb, SSD_HEADS, CHUNK), lambda b, j: (b, order(j), 0, 0)),
            st=pl.BlockSpec((1, ncb * SSD_N, SSD_HEADS * SSD_P), lambda b, j: (b, order(j), 0)))
    return nsb, ncb, specs(lambda j: j), specs(lambda j: nsb - 1 - j)


def _ssd_chunk_fwd(xs, bc, dtx, acsx, acs, ar, bsz, seq, sb):
    nsb, ncb, sp, _ = _ssd_specs(seq, sb)

    def body(x_ref, dtx_ref, ax_ref, bc_ref, acs_ref, ar_ref, y_ref, sts_ref, st_scr):
        @pl.when(pl.program_id(1) == 0)
        def _():
            st_scr[...] = jnp.zeros_like(st_scr)

        c = _chunk_consts()
        lane5 = lax.broadcasted_iota(jnp.int32, (CHUNK, SSD_GW), 1) // SSD_P

        def chunk(n, carry):
            r = pl.ds(pl.multiple_of(n * CHUNK, CHUNK), CHUNK)
            rs = pl.ds(pl.multiple_of(n * SSD_N, SSD_N), SSD_N)
            acsv = acs_ref[0, r, :]
            for g in range(SSD_GROUPS):
                gl = slice(g * SSD_GW, (g + 1) * SSD_GW)
                x, dt, ax = x_ref[0, r, gl], dtx_ref[0, r, gl], ax_ref[0, r, gl]
                bm = bc_ref[0, r, g * SSD_N:(g + 1) * SSD_N]
                cm = bc_ref[0, r, (SSD_GROUPS + g) * SSD_N:(SSD_GROUPS + g + 1) * SSD_N]
                xdt = x * dt
                cbm = _mm(cm, bm, NT)
                al = ax[CHUNK - 1:CHUNK, :]
                st = st_scr[:, gl]
                y = _mm(cm, st) * jnp.exp(ax)
                for hh in range(SSD_HPG):
                    _, gm = _ssd_head(acsv, ar_ref, n, g * SSD_HPG + hh, cbm, c)
                    y = y + _mm(gm, jnp.where(lane5 == hh, xdt, 0.0))
                y_ref[0, r, gl] = y
                sts_ref[0, rs, gl] = st
                st_scr[:, gl] = st * jnp.exp(al) + _mm(bm, xdt * jnp.exp(al - ax), TN)
            return carry

        lax.fori_loop(0, ncb, chunk, 0)

    return _pcall(
        body, name="ssd_chunk_fwd", grid=(bsz, nsb),
        in_specs=[sp["wide"](), sp["wide"](), sp["wide"](), sp["bc"](), sp["small"](), sp["ar"]],
        out_specs=[sp["wide"](), sp["st"]],
        out_shape=[jax.ShapeDtypeStruct((bsz, seq, SSD_HEADS * SSD_P), F32),
                   jax.ShapeDtypeStruct((bsz, (seq // CHUNK) * SSD_N, SSD_HEADS * SSD_P), F32)],
        scratch_shapes=[pltpu.VMEM((SSD_N, SSD_HEADS * SSD_P), F32)],
        compiler_params=_params(("parallel", "arbitrary")),
    )(xs, dtx, acsx, bc, acs, ar)


def _gate_norm(o_gdn, y_ssd, xs, proj, gnw, snw, dvec, tm):
    t = o_gdn.shape[0]
    d = D_MODEL

    def body(i, o_ref, za_ref, y_ref, xs_ref, zs_ref, gnw_ref, snw_ref, dv_ref, out_ref):
        for hh in range(GDN_HEADS):
            sl = slice(hh * GDN_DK, (hh + 1) * GDN_DK)
            oh, _ = _rms(o_ref[:, sl], GDN_DK)
            z = za_ref[:, sl]
            out_ref[:, sl] = (oh * gnw_ref[...] * (z * _sigmoid(z))).astype(BF16)
        zs = zs_ref[...]
        yg = (y_ref[...] + dv_ref[...] * xs_ref[...]) * (zs * _sigmoid(zs))
        for g in range(SSD_GROUPS):
            sl = slice(g * 512, (g + 1) * 512)
            yh, _ = _rms(yg[:, sl], 512)
            out_ref[:, d + g * 512:d + (g + 1) * 512] = (yh * snw_ref[:, sl]).astype(BF16)

    ins = [("row", o_gdn, d, 0), ("row", proj, d, 3), ("row", y_ssd, d, 0), ("row", xs, d, 0), ("row", proj, d, 4),
           ("full", gnw), ("full", snw), ("full", dvec)]
    return _rowwise("gate_norm", body, t, tm, ins, [(2 * d, BF16)])[0]


def _out_mid(mixin, w_out, x, pmw, pfw):
    d = D_MODEL

    def epilogue(mix, x_ref, pmw_ref, pfw_ref, mix_ref, x1_ref, h2_ref):
        mix_ref[...] = mix
        mh, _ = _rms(mix, d)
        x1 = x_ref[...] + mh * pmw_ref[...]
        x1_ref[...] = x1
        xh, _ = _rms(x1, d)
        h2_ref[...] = (xh * pfw_ref[...]).astype(BF16)

    return _matmul_rows("mm_out_mid", mixin, w_out, "nn", epilogue, [x], [pmw, pfw], [(d, F32), (d, F32), (d, BF16)],
                        tk=2 * d)


def _ffn_act(u_pre, cw, cb, seq, tm):
    t = u_pre.shape[0]

    def body(i, ug_ref, ugh_ref, uu_ref, uuh_ref, cw_ref, cb_ref, act_ref, u_ref):
        keep, _ = _seq_flags(i, seq, tm)
        gate = _conv(_conv_taps(ug_ref[...], ugh_ref[...] * keep, 3), cw_ref[:, 0:D_FF]) + cb_ref[:, 0:D_FF]
        up = _conv(_conv_taps(uu_ref[...], uuh_ref[...] * keep, 3), cw_ref[:, D_FF:2 * D_FF]) + cb_ref[:, D_FF:2 * D_FF]
        u_ref[:, 0:D_FF] = gate
        u_ref[:, D_FF:2 * D_FF] = up
        act_ref[...] = (gate * _sigmoid(gate) * up).astype(BF16)

    ins = [("row", u_pre, D_FF, 0), ("prev", u_pre, D_FF, 0), ("row", u_pre, D_FF, 1), ("prev", u_pre, D_FF, 1),
           ("full", cw), ("full", cb)]
    return _rowwise("ffn_act", body, t, tm, ins, [(D_FF, BF16), (2 * D_FF, F32)])


def _down_final(act, w_down, x1, tgt, w):
    d = D_MODEL

    def epilogue(f, x1_ref, t_ref, w_ref, dy_ref, df_ref, loss_ref, dw_ref):
        fh, r = _rms(f, d)
        e = x1_ref[...] + fh * w_ref[...] - t_ref[...]
        loss_ref[...] += _colsum(e * e) * (0.5 / d)
        dy = e * (1.0 / d)
        dy_ref[...] = dy
        dw_ref[...] += _colsum(dy * fh)
        df_ref[...] = _rms_bwd(fh, r, dy * w_ref[...], d).astype(BF16)

    return _matmul_rows("mm_down_final", act, w_down, "nn", epilogue, [x1, tgt], [w], [(d, F32), (d, BF16)],
                        accs=[(1, d), (1, d)], tk=D_FF)


def _ffn_bwd(u, u_pre, dact, cw, seq, tm):
    t = u.shape[0]

    def body(i, g_ref, gn_ref, up_ref, upn_ref, xg_ref, xu_ref, da_ref, dan_ref, cw_ref, dpre_ref, dcw_ref, dcb_ref):
        _, keep_next = _seq_flags(i, seq, tm)
        ext = lambda a_ref, n_ref: jnp.concatenate([a_ref[...], n_ref[...]], axis=0)
        rows = tm + SUBLANES
        gate, up = ext(g_ref, gn_ref), ext(up_ref, upn_ref)
        sg = _sigmoid(gate)
        da = jnp.concatenate([da_ref[...], dan_ref[...] * keep_next], axis=0)
        for off, grad, x_ref in ((0, da * up * _dsilu(gate, sg), xg_ref), (D_FF, da * gate * sg, xu_ref)):
            x = x_ref[...]
            own = grad[0:tm]
            acc = own * cw_ref[2:3, off:off + D_FF]
            dcb_ref[:, off:off + D_FF] += _colsum(own)
            dcw_ref[2:3, off:off + D_FF] += _colsum(own * x)
            for j in (1, 2):
                ahead = pltpu.roll(grad, rows - j, 0)[0:tm]
                acc = acc + ahead * cw_ref[2 - j:3 - j, off:off + D_FF]
                dcw_ref[2 - j:3 - j, off:off + D_FF] += _colsum(ahead * x)
            dpre_ref[:, off:off + D_FF] = acc.astype(BF16)

    ins = []
    for cb_ in range(2):
        ins += [("row", u, D_FF, cb_), ("next", u, D_FF, cb_)]
    ins += [("row", u_pre, D_FF, 0), ("row", u_pre, D_FF, 1), ("row", dact, D_FF, 0), ("next", dact, D_FF, 0), ("full", cw)]
    return _rowwise("ffn_bwd", body, t, tm, ins, [(2 * D_FF, BF16)], accs=[(SUBLANES, 2 * D_FF), (1, 2 * D_FF)])


def _assemble_dproj(dpre_qkv, dza, dzs, dpre_xbc, dsm, proj, gcw, scw, seq, tm):
    t = dza.shape[0]
    d = D_MODEL

    def body(i, dq_ref, dqn_ref, dk_ref, dkn_ref, dv_ref, dvn_ref, dx_ref, dxn_ref, dbc_ref, dbcn_ref, dza_ref, dzs_ref,
             dsm_ref, xq_ref, xk_ref, xv_ref, xx_ref, xbc_ref, gcw_ref, scw_ref, o_ref, dgcw_ref, dscw_ref):
        _, keep = _seq_flags(i, seq, tm)
        pieces = [(g_ref, n_ref, gcw_ref, dgcw_ref, x_ref, 0, c0) for g_ref, n_ref, x_ref, c0 in (
            (dq_ref, dqn_ref, xq_ref, 0), (dk_ref, dkn_ref, xk_ref, d), (dv_ref, dvn_ref, xv_ref, 2 * d))]
        pieces += [(g_ref, n_ref, scw_ref, dscw_ref, x_ref, 5 * d, c0) for g_ref, n_ref, x_ref, c0 in (
            (dx_ref, dxn_ref, xx_ref, 0), (dbc_ref, dbcn_ref, xbc_ref, d))]
        for d_ref, n_ref, cw_ref, dcw_ref, x_ref, base, c0 in pieces:
            w = x_ref.shape[1]
            x = x_ref[...]
            g = d_ref[...]
            halo = n_ref[...] * keep
            acc = g * cw_ref[3:4, c0:c0 + w]
            dcw_ref[3:4, c0:c0 + w] += _colsum(g * x)
            for j in range(1, 4):
                ahead = _shift_up(g, halo, j)
                acc = acc + ahead * cw_ref[3 - j:4 - j, c0:c0 + w]
                dcw_ref[3 - j:4 - j, c0:c0 + w] += _colsum(ahead * x)
            o_ref[:, base + c0:base + c0 + w] = acc.astype(BF16)
        o_ref[:, 3 * d:4 * d] = dza_ref[...]
        o_ref[:, 4 * d:5 * d] = dzs_ref[...]
        o_ref[:, 6 * d + 512:6 * d + 512 + LANES] = dsm_ref[...]
        o_ref[:, 6 * d + 512 + LANES:PROJ_W] = jnp.zeros((tm, PROJ_W - (6 * d + 512 + LANES)), BF16)

    ins = []
    for g in tuple(dpre_qkv) + tuple(dpre_xbc):
        ins += [("row", g, g.shape[1], 0), ("next", g, g.shape[1], 0)]
    ins += [("row", dza, d, 0), ("row", dzs, d, 0), ("row", dsm, LANES, 0),
           ("row", proj, d, 0), ("row", proj, d, 1), ("row", proj, d, 2), ("row", proj, d, 5), ("row", proj, 512, 12),
           ("full", gcw), ("full", scw)]
    return _rowwise("assemble_dproj", body, t, tm, ins, [(PROJ_W, BF16)], accs=[(SUBLANES, 3 * d), (SUBLANES, d + 512)])


def _dh2_mid_bwd(du_pre, w_up, x1, mix, dy, pmw, pfw):
    d = D_MODEL

    def epilogue(dh2, x1_ref, mix_ref, dy_ref, pmw_ref, pfw_ref, dx1_ref, dmix_ref, dpm_ref, dpf_ref):
        xh, r2 = _rms(x1_ref[...], d)
        dpf_ref[...] += _colsum(dh2 * xh)
        dx1 = dy_ref[...] + _rms_bwd(xh, r2, dh2 * pfw_ref[...], d)
        dx1_ref[...] = dx1
        mh, r = _rms(mix_ref[...], d)
        dpm_ref[...] += _colsum(dx1 * mh)
        dmix_ref[...] = _rms_bwd(mh, r, dx1 * pmw_ref[...], d).astype(BF16)

    return _matmul_rows("mm_dh2_mid_bwd", du_pre, w_up, "nt", epilogue, [x1, mix, dy], [pmw, pfw],
                        [(d, F32), (d, BF16)], accs=[(1, d), (1, d)], tk=D_FF)


def _dmixin_gate_norm_bwd(dmix, w_out, o_gdn, y_ssd, xs, proj, gnw, snw, dvec):
    d = D_MODEL

    def epilogue(dmixin, o_ref, za_ref, y_ref, xs_ref, zs_ref, gnw_ref, snw_ref, dv_ref,
                 do_ref, dza_ref, dy_ref, dxs_ref, dzs_ref, dgnw_ref, dsnw_ref, dd_ref):
        for hh in range(GDN_HEADS):
            sl = slice(hh * GDN_DK, (hh + 1) * GDN_DK)
            oh, r = _rms(o_ref[:, sl], GDN_DK)
            z = za_ref[:, sl]
            sz = _sigmoid(z)
            dm = dmixin[:, sl]
            don = dm * (z * sz)
            dza_ref[:, sl] = (dm * oh * gnw_ref[...] * _dsilu(z, sz)).astype(BF16)
            dgnw_ref[...] += _colsum(don * oh)
            do_ref[:, sl] = _rms_bwd(oh, r, don * gnw_ref[...], GDN_DK)
        zs = zs_ref[...]
        sz = _sigmoid(zs)
        sil = zs * sz
        x = xs_ref[...]
        y0 = y_ref[...] + dv_ref[...] * x
        yg = y0 * sil
        dms = dmixin[:, d:2 * d]
        for g in range(SSD_GROUPS):
            sl = slice(g * 512, (g + 1) * 512)
            yh, r = _rms(yg[:, sl], 512)
            dsnw_ref[:, sl] += _colsum(dms[:, sl] * yh)
            dyg = _rms_bwd(yh, r, dms[:, sl] * snw_ref[:, sl], 512)
            dy0 = dyg * sil[:, sl]
            dzs_ref[:, sl] = (dyg * y0[:, sl] * _dsilu(zs[:, sl], sz[:, sl])).astype(BF16)
            dy_ref[:, sl] = dy0
            dxs_ref[:, sl] = dy0 * dv_ref[:, sl]
            dd_ref[:, sl] += _colsum(dy0 * x[:, sl])

    row_ins = [o_gdn, (proj, d, 3), y_ssd, xs, (proj, d, 4)]
    return _matmul_rows("mm_dmixin_gate_norm_bwd", dmix, w_out, "nt", epilogue, row_ins, [gnw, snw, dvec],
                        [(d, F32), (d, BF16), (d, F32), (d, F32), (d, BF16)], accs=[(1, GDN_DK), (1, d), (1, d)], tm=256)


def _ssd_chunk_bwd(xs, bc, dtx, acsx, acs, ar, dy, sts, ypre, dxs_d, bsz, seq, sb):
    nsb, ncb, _, sp = _ssd_specs(seq, sb)
    bc_w = 2 * SSD_GROUPS * SSD_N
    x_w = SSD_HEADS * SSD_P

    def body(x_ref, dtx_ref, ax_ref, bc_ref, acs_ref, ar_ref, dy_ref, sts_ref, yx_ref, ybc_ref, dxd_ref,
             dx_ref, dbc_ref, ddt_ref, dacs_ref, dbx_ref, dbbc_ref, dst_scr):
        @pl.when(pl.program_id(1) == 0)
        def _():
            dst_scr[...] = jnp.zeros_like(dst_scr)

        @pl.when((pl.program_id(0) == 0) & (pl.program_id(1) == 0))
        def _():
            dbx_ref[...] = jnp.zeros_like(dbx_ref)
            dbbc_ref[...] = jnp.zeros_like(dbbc_ref)

        def to_conv_out(grad, y):
            return grad * _dsilu(y, _sigmoid(y))

        c = _chunk_consts()
        lane5 = lax.broadcasted_iota(jnp.int32, (CHUNK, SSD_GW), 1) // SSD_P
        row5 = lax.broadcasted_iota(jnp.int32, (CHUNK, SSD_GW), 0)
        sel_in = lax.broadcasted_iota(jnp.int32, (SSD_GW, LANES), 0) // SSD_P
        sel_out = lax.broadcasted_iota(jnp.int32, (SSD_GW, LANES), 1)

        def chunk(nn, carry):
            n = ncb - 1 - nn
            r = pl.ds(pl.multiple_of(n * CHUNK, CHUNK), CHUNK)
            rs = pl.ds(pl.multiple_of(n * SSD_N, SSD_N), SSD_N)
            acsv = acs_ref[0, r, :]
            ddt = jnp.zeros((CHUNK, LANES), F32)
            dacs = jnp.zeros((CHUNK, LANES), F32)
            for g in range(SSD_GROUPS):
                gl = slice(g * SSD_GW, (g + 1) * SSD_GW)
                x, dt, ax, dyv = x_ref[0, r, gl], dtx_ref[0, r, gl], ax_ref[0, r, gl], dy_ref[0, r, gl]
                bm = bc_ref[0, r, g * SSD_N:(g + 1) * SSD_N]
                cm = bc_ref[0, r, (SSD_GROUPS + g) * SSD_N:(SSD_GROUPS + g + 1) * SSD_N]
                st = sts_ref[0, rs, gl]
                dst = dst_scr[:, gl]
                rsel = (sel_in + (16 + g * SSD_HPG) == sel_out).astype(F32)
                xdt = x * dt
                cbm = _mm(cm, bm, NT)
                al = ax[CHUNK - 1:CHUNK, :]
                ex, el = jnp.exp(ax), jnp.exp(al)
                dec = jnp.exp(al - ax)
                xd = xdt * dec
                dye = dyv * ex
                dxd = _mm(bm, dst)
                dxdt = dec * dxd
                dcm = _mm(dye, st, NT)
                dbm = _mm(xd, dst, NT)
                z = dye * _mm(cm, st) - dxd * xd
                zl = _colsum(dst * st) * el + _colsum(dxd * xd)
                z = z + jnp.where(row5 == CHUNK - 1, zl, 0.0)
                dcb = jnp.zeros((CHUNK, CHUNK), F32)
                for hh in range(SSD_HPG):
                    head = g * SSD_HPG + hh
                    lm, gm = _ssd_head(acsv, ar_ref, n, head, cbm, c)
                    dym = jnp.where(lane5 == hh, dyv, 0.0)
                    dxdt = dxdt + _mm(gm, dym, TN)
                    dg = _mm(dym, xdt, NT)
                    dcb = dcb + dg * lm
                    pm = dg * gm
                    dacs = dacs + jnp.where(c["lane"] == head + 16, _rowsum(pm) - _mmsel(pm, c["ones"], TN), 0.0)
                for sl, grad in ((slice((SSD_GROUPS + g) * SSD_N, (SSD_GROUPS + g + 1) * SSD_N), dcm + _mm(dcb, bm)),
                                 (slice(g * SSD_N, (g + 1) * SSD_N), dbm + _mm(dcb, cm, TN))):
                    dpre = to_conv_out(grad, ybc_ref[0, r, sl])
                    dbc_ref[0, r, sl] = dpre
                    dbbc_ref[:, sl] += _colsum(dpre)
                dacs = dacs + _mmsel(z, rsel)
                ddt = ddt + _mmsel(dxdt * x, rsel)
                dpre = to_conv_out(dxdt * dt + dxd_ref[0, r, gl], yx_ref[0, r, gl])
                dx_ref[0, r, gl] = dpre
                dbx_ref[:, gl] += _colsum(dpre)
                dst_scr[:, gl] = dst * el + _mm(cm, dye, TN)
            ddt_ref[0, r, :] = ddt
            dacs_ref[0, r, :] = dacs
            return carry

        lax.fori_loop(0, ncb, chunk, 0)

    nsb_rev = lambda j: nsb - 1 - j
    return _pcall(
        body, name="ssd_chunk_bwd", grid=(bsz, nsb),
        in_specs=[sp["wide"](), sp["wide"](), sp["wide"](), sp["bc"](), sp["small"](), sp["ar"], sp["wide"](), sp["st"],
                  sp["wide"](), pl.BlockSpec((1, sb, bc_w), lambda b, j: (b, nsb_rev(j), x_w // bc_w)), sp["wide"]()],
        out_specs=[sp["wide"](), sp["bc"](), sp["small"](), sp["small"](),
                   pl.BlockSpec((1, x_w), lambda b, j: (0, 0)), pl.BlockSpec((1, bc_w), lambda b, j: (0, 0))],
        out_shape=[jax.ShapeDtypeStruct((bsz, seq, x_w), F32), jax.ShapeDtypeStruct((bsz, seq, bc_w), F32),
                   jax.ShapeDtypeStruct((bsz, seq, LANES), F32), jax.ShapeDtypeStruct((bsz, seq, LANES), F32),
                   jax.ShapeDtypeStruct((1, x_w), F32), jax.ShapeDtypeStruct((1, bc_w), F32)],
        scratch_shapes=[pltpu.VMEM((SSD_N, x_w), F32)],
        compiler_params=_params(("arbitrary", "arbitrary")),
    )(xs, dtx, acsx, bc, acs, ar, dy, sts, ypre, ypre, dxs_d)


def _through_norm_silu(g, y, scale):
    sy = _sigmoid(y)
    ds_ = _dsilu(y, sy)
    if scale is None:
        return g * ds_
    a = y * sy
    n = lax.rsqrt(_rowsum(a * a) + EPS)
    ah = a * n
    return (scale * n) * (g - ah * _rowsum(g * ah)) * ds_


def _gdn_chunk_bwd(qn, kn, vv, gs, gr, do, sts, tis, ypre, bsz, seq, sb, riders):
    hb = GDN_HB
    nsb, ncb, sp = _gdn_specs(seq, sb, hb, True)
    grid = (bsz, GDN_HEADS // hb, nsb)
    any_spec, rider_shapes, rider_sems, wrap = _riding_exchange(riders, True, 11, 4, grid)
    ypre_spec = lambda cb: pl.BlockSpec((1, sb, hb * GDN_DK), lambda b, h, j: (b, nsb - 1 - j, cb))

    def body(q_ref, k_ref, v_ref, gs_ref, gr_ref, do_ref, st_ref, ti_ref, yq_ref, yk_ref, yv_ref,
             dq_ref, dk_ref, dv_ref, dgs_ref, ds_scr):
        @pl.when(pl.program_id(2) == 0)
        def _():
            ds_scr[...] = jnp.zeros_like(ds_scr)

        c = _chunk_consts()

        def chunk(nn, carry):
            n = ncb - 1 - nn
            r = pl.ds(pl.multiple_of(n * CHUNK, CHUNK), CHUNK)
            rs = pl.ds(pl.multiple_of(n * GDN_DK, GDN_DK), GDN_DK)
            gsv = gs_ref[0, r, :]
            heads = list(range(hb))
            sls = [slice(ih * GDN_DK, (ih + 1) * GDN_DK) for ih in heads]
            q = [q_ref[0, r, sl] for sl in sls]
            k = [k_ref[0, r, sl] for sl in sls]
            v = [v_ref[0, r, sl] for sl in sls]
            do_ = [do_ref[0, r, sl] for sl in sls]
            s = [st_ref[0, ih, rs, :] for ih in heads]
            tinv = [ti_ref[0, ih, r, :] for ih in heads]
            dsn = [ds_scr[ih] for ih in heads]
            beta, dc, eg, egl, ekd = zip(*[
                _gdn_gates(gsv, gr_ref[0, n, pl.ds(ih, 1), :], ih, c) for ih in heads])
            mul = lambda a, b: a * b
            kb = _hmap(mul, k, beta)
            rhs_w = _hmap(mul, kb, eg)
            u = _hmap(lambda t_, a, b: _mm3(t_, a * b), tinv, v, beta)
            w = _hmap(_mm3, tinv, rhs_w)
            amat = _hmap(lambda a, b, d_: jnp.where(c["strict"], _mm(a, b, NT) * d_, 0.0), kb, k, dc)
            qk = _hmap(lambda a, b, d_: _mm(a, b, NT) * d_, q, k, dc)
            qd = _hmap(mul, q, eg)
            kd = _hmap(mul, k, ekd)
            v_new = _hmap(lambda a, b, s_: a - _mm(b, s_), u, w, s)
            dv_new = _hmap(lambda qk_, d_, kd_, dn: _mm(qk_, d_, TN) + _mm(kd_, dn), qk, do_, kd, dsn)
            dqk = _hmap(lambda d_, vn: _mm(d_, vn, NT), do_, v_new)
            dqd = _hmap(lambda d_, s_: _mm(d_, s_, NT), do_, s)
            ds_new = _hmap(lambda qd_, d_, dn, e, w_, dvn: _mm(qd_, d_, TN) + dn * e - _mm(w_, dvn, TN),
                           qd, do_, dsn, egl, w, dv_new)
            dkd = _hmap(lambda vn, dn: _mm(vn, dn, NT), v_new, dsn)
            dgl = _hmap(lambda s_, dn, e: _colsum(_rowsum(s_ * dn)) * e, s, dsn, egl)
            dw = _hmap(lambda dvn, s_: -_mm(dvn, s_, NT), dv_new, s)
            dru = _hmap(lambda t_, a: _mm3(t_, a, TN), tinv, dv_new)
            drw = _hmap(lambda t_, a: _mm3(t_, a, TN), tinv, dw)
            da = _hmap(lambda a, u_, b, w_: jnp.where(c["strict"], -(_mm(a, u_, NT) + _mm(b, w_, NT)), 0.0), dru, u, drw, w)
            m = _hmap(mul, da, dc)
            dkb = _hmap(lambda a, e, m_, k_: a * e + _mm(m_, k_), drw, eg, m, k)
            mq = _hmap(mul, dqk, dc)
            dq = _hmap(lambda mq_, k_, a, e: _mm(mq_, k_) + a * e, mq, k, dqd, eg)
            dk = _hmap(lambda m_, kb_, mq_, q_, a, e, b, be: _mm(m_, kb_, TN) + _mm(mq_, q_, TN) + a * e + b * be,
                       m, kb, mq, q, dkd, ekd, dkb, beta)
            dbeta = _hmap(lambda a, v_, b, k_: _rowsum(a * v_) + _rowsum(b * k_), dru, v, dkb, k)
            pq = _hmap(lambda a, am, b, qk_: a * am + b * qk_, da, amat, dqk, qk)
            ekk = _hmap(lambda a, b: _rowsum(a * b), dkd, kd)
            dgc = _hmap(lambda pq_, a, rw, b, qd_, e, gl_: (
                _rowsum(pq_) - _mmsel(pq_, c["ones"], TN) + (_rowsum(a * rw) + _rowsum(b * qd_) - e)
                + jnp.where(c["row1"] == CHUNK - 1, _colsum(e) + gl_, 0.0)), pq, drw, rhs_w, dqd, qd, ekk, dgl)
            dv = _hmap(mul, dru, beta)
            dyq = _hmap(lambda g_, sl: _through_norm_silu(g_, yq_ref[0, r, sl], GDN_DK ** -0.5), dq, sls)
            dyk = _hmap(lambda g_, sl: _through_norm_silu(g_, yk_ref[0, r, sl], 1.0), dk, sls)
            dyv = _hmap(lambda g_, sl: _through_norm_silu(g_, yv_ref[0, r, sl], None), dv, sls)
            dgs = jnp.zeros((CHUNK, LANES), F32)
            for ih in heads:
                ds_scr[ih] = ds_new[ih]
                dq_ref[0, r, sls[ih]] = dyq[ih]
                dk_ref[0, r, sls[ih]] = dyk[ih]
                dv_ref[0, r, sls[ih]] = dyv[ih]
                dgs = dgs + jnp.where(c["lane"] == ih, dbeta[ih], jnp.where(c["lane"] == ih + 8, dgc[ih], 0.0))
            dgs_ref[0, r, :] = dgs
            return carry

        lax.fori_loop(0, ncb, chunk, 0)

    res = _pcall(
        wrap(body), name="gdn_chunk_bwd", grid=grid,
        in_specs=[sp["wide"](), sp["wide"](), sp["wide"](), sp["gs"](), sp["gr"], sp["wide"](), sp["st"], sp["ti"],
                  ypre_spec(0), ypre_spec(1), ypre_spec(2)] + any_spec,
        out_specs=[sp["wide"](), sp["wide"](), sp["wide"](), sp["gs"]()] + any_spec,
        out_shape=[jax.ShapeDtypeStruct((bsz, seq, D_MODEL), F32)] * 3 + [jax.ShapeDtypeStruct((bsz, seq, LANES), F32)]
        + rider_shapes,
        scratch_shapes=[pltpu.VMEM((hb, GDN_DK, GDN_DK), F32)] + rider_sems,
        compiler_params=_params(("arbitrary", "arbitrary", "arbitrary")),
    )(qn, kn, vv, gs, gr, do, sts, tis, ypre, ypre, ypre, *riders)
    return res[:4], res[4:]


def _gates_bwd(proj, dgs, ddt, dacs, gp, sp, tm):
    t = proj.shape[0]

    def body(i, sm_ref, dgs_ref, ddt_ref, dacs_ref, gp_ref, sp_ref, dsm_ref, dgp_ref, dsp_ref):
        sm = sm_ref[...]
        lane = lax.broadcasted_iota(jnp.int32, sm.shape, 1)
        is_g = (lane >= 8) & (lane < 16)
        is_dt = (lane >= 16) & (lane < 32)
        dgs = dgs_ref[...]
        back = _mmx(_block_tri(tm, True), jnp.where(is_g, dgs, 0.0) + dacs_ref[...])
        beta = _sigmoid(sm)
        bias = gp_ref[1:2, :] + sp_ref[1:2, :]
        xb = sm + bias
        soft, dsoft = _softplus(xb), _sigmoid(xb)
        g_neg = -jnp.exp(gp_ref[0:1, :])
        a_neg = -jnp.exp(sp_ref[0:1, :])
        dg = jnp.where(is_g, back * g_neg, 0.0)
        dxb_g = dg * dsoft
        dxb_dt = jnp.where(is_dt, (ddt_ref[...] + back * a_neg) * dsoft, 0.0)
        dsm_ref[...] = (jnp.where(lane < 8, dgs * beta * (1.0 - beta), dxb_g) + dxb_dt).astype(BF16)
        dgp_ref[1:2, :] += _colsum(dxb_g)
        dgp_ref[0:1, :] += _colsum(dg * soft)
        dsp_ref[1:2, :] += _colsum(dxb_dt)
        dsp_ref[0:1, :] += jnp.where(is_dt[0:1, :], _colsum(back * soft) * a_neg, 0.0)

    ins = [("row", proj, LANES, SMALL_CB), ("row", dgs, LANES, 0), ("row", ddt, LANES, 0), ("row", dacs, LANES, 0),
           ("full", gp), ("full", sp)]
    return _rowwise("gates_bwd", body, t, tm, ins, [(LANES, BF16)], accs=[(SUBLANES, LANES), (SUBLANES, LANES)])


def _dh1_first_bwd(dproj, wp_in, x, dx1, w, scatter_riders):
    d = D_MODEL

    def epilogue(dh, x_ref, dx1_ref, w_ref, dx_ref, dw_ref):
        xh, r = _rms(x_ref[...], d)
        dw_ref[...] += _colsum(dh * xh)
        dx_ref[...] = dx1_ref[...] + _rms_bwd(xh, r, dh * w_ref[...], d)

    return _matmul_rows("mm_dh1_first_bwd", dproj, wp_in, "nt", epilogue, [x, dx1], [w], [(d, F32)], accs=[(1, d)],
                        tk=PROJ_W // 2, scatter_riders=scatter_riders)


def _gather_two_level(name, arrays):
    n = len(arrays)
    n_sem = 7

    def body(*refs):
        ins, outs = refs[:n], refs[n:2 * n]
        send_sems, recv_sems, loc_sems = refs[2 * n:]
        x, y, c = lax.axis_index("x"), lax.axis_index("y"), lax.axis_index("c")
        slot = lambda px, py, pc: 4 * px + 2 * py + pc
        sibling = (x, y, 1 - c)
        chips = [(1 - x, y), (x, 1 - y), (1 - x, 1 - y)]

        def copy(t, k, src, block, to):
            return pltpu.make_async_remote_copy(
                src_ref=src, dst_ref=outs[t].at[block], send_sem=send_sems.at[t, k], recv_sem=recv_sems.at[t, k],
                device_id=to, device_id_type=pl.DeviceIdType.MESH)

        own, first, passed = [], [], []
        for t in range(n):
            own.append(pltpu.make_async_copy(ins[t], outs[t].at[slot(x, y, c)], loc_sems.at[t]))
            first.append(copy(t, 0, ins[t], slot(x, y, c), sibling))
            first += [copy(t, 1 + j, ins[t], slot(x, y, c), (px, py, c)) for j, (px, py) in enumerate(chips)]
        for cp in own + first:
            cp.start()
        for t in range(n):
            for j, (px, py) in enumerate(chips):
                copy(t, 1 + j, ins[t], slot(px, py, c), (px, py, c)).wait_recv()
                fwd = copy(t, 4 + j, outs[t].at[slot(px, py, c)], slot(px, py, c), sibling)
                fwd.start()
                passed.append(fwd)
        for t in range(n):
            copy(t, 0, ins[t], slot(x, y, 1 - c), sibling).wait_recv()
            for j, (px, py) in enumerate(chips):
                copy(t, 4 + j, ins[t], slot(px, py, 1 - c), sibling).wait_recv()
        for cp in first + passed:
            cp.wait_send()
        for cp in own:
            cp.wait()

    return _pcall(
        body, name=name,
        in_specs=[pl.BlockSpec(memory_space=pl.ANY)] * n,
        out_specs=[pl.BlockSpec(memory_space=pl.ANY)] * n,
        out_shape=_exchange_out_shapes(arrays, False),
        scratch_shapes=[pltpu.SemaphoreType.DMA((n, n_sem)), pltpu.SemaphoreType.DMA((n, n_sem)), pltpu.SemaphoreType.DMA((n,))],
    )(*arrays)


def _exchange_out_shapes(arrays, scatter):
    return [jax.ShapeDtypeStruct(a.shape if scatter else (N_DEV,) + a.shape, a.dtype) for a in arrays]


def _exchange_sems(n):
    return [pltpu.SemaphoreType.DMA((n, N_DEV - 1)), pltpu.SemaphoreType.DMA((n, N_DEV - 1)), pltpu.SemaphoreType.DMA((n,))]


def _exchange_phase(ins, outs, sems, scatter, start):
    send_sems, recv_sems, loc_sems = sems
    x, y, c = lax.axis_index("x"), lax.axis_index("y"), lax.axis_index("c")
    me = 4 * x + 2 * y + c
    for t in range(len(ins)):
        loc = pltpu.make_async_copy(ins[t].at[me] if scatter else ins[t], outs[t].at[me], loc_sems.at[t])
        if start:
            loc.start()
        else:
            loc.wait()
        for k in range(N_DEV - 1):
            bx, by, bc = ((k + 1) >> 2) & 1, ((k + 1) >> 1) & 1, (k + 1) & 1
            px = 1 - x if bx else x
            py = 1 - y if by else y
            pc = 1 - c if bc else c
            peer = 4 * px + 2 * py + pc
            src = ins[t].at[peer] if scatter else ins[t]
            copy = lambda dst: pltpu.make_async_remote_copy(
                src_ref=src, dst_ref=dst, send_sem=send_sems.at[t, k], recv_sem=recv_sems.at[t, k],
                device_id=(px, py, pc), device_id_type=pl.DeviceIdType.MESH)
            if start:
                copy(outs[t].at[me]).start()
            else:
                copy(outs[t].at[me]).wait_send()
                copy(outs[t].at[peer]).wait_recv()


def _adam_math(w, g, m, v):
    m = ADAM_B1 * m + (1.0 - ADAM_B1) * g
    v = ADAM_B2 * v + (1.0 - ADAM_B2) * (g * g)
    m_hat = m / (1.0 - ADAM_B1 ** ADAM_STEP)
    v_hat = v / (1.0 - ADAM_B2 ** ADAM_STEP)
    delta = -ADAM_LR * (m_hat / (jnp.sqrt(v_hat) + ADAM_EPS) + ADAM_WD * w)
    return delta, m, v


def _adam_big(name, parts, w, m, v, tm):
    r, c = w.shape
    tm = tm if r % tm == 0 else r

    def body(p_ref, w_ref, m_ref, v_ref, g_ref, d_ref, nm_ref, nv_ref):
        g = p_ref[0].astype(F32)
        for s in range(1, N_DEV):
            g = g + p_ref[s].astype(F32)
        g_ref[...] = g
        d_ref[...], nm_ref[...], nv_ref[...] = _adam_math(w_ref[...], g, m_ref[...], v_ref[...])

    blk = lambda: pl.BlockSpec((tm, c), lambda i: (i, 0))
    return _pcall(
        body, name=name, grid=(r // tm,),
        in_specs=[pl.BlockSpec((N_DEV, tm, c), lambda i: (0, i, 0)), blk(), blk(), blk()],
        out_specs=[blk(), blk(), blk(), blk()],
        out_shape=[jax.ShapeDtypeStruct((r, c), F32)] * 4,
        compiler_params=_params(("parallel",)),
    )(parts, w, m, v)


SMALL_ROWS = 56
ROW_DD, ROW_LOSS = 5, 6


def _small_sum(gathered):
    def body(g_ref, o_ref, x_ref):
        s = g_ref[0]
        for dev in range(1, N_DEV):
            s = s + g_ref[dev]
        o_ref[...] = s
        ri = lax.broadcasted_iota(jnp.int32, (D_MODEL, LANES), 0)
        ro = lax.broadcasted_iota(jnp.int32, (D_MODEL, LANES), 1)
        heads = _mmx(jnp.broadcast_to(s[ROW_DD:ROW_DD + 1, :], (SUBLANES, D_MODEL)), (ri // SSD_P == ro).astype(F32))
        loss = _rowsum(jnp.broadcast_to(s[ROW_LOSS:ROW_LOSS + 1, :], (SUBLANES, D_MODEL)))
        row = lax.broadcasted_iota(jnp.int32, (SUBLANES, LANES), 0)
        x_ref[...] = jnp.where(row == 0, heads, jnp.broadcast_to(loss, (SUBLANES, LANES)))

    return _pcall(
        body, name="small_sum",
        out_shape=[jax.ShapeDtypeStruct((SMALL_ROWS, D_MODEL), F32), jax.ShapeDtypeStruct((SUBLANES, LANES), F32)],
        compiler_params=_params(None),
    )(gathered)


def _adam_small(g, w, m, v):
    def body(g_ref, w_ref, m_ref, v_ref, d_ref, nm_ref, nv_ref):
        d_ref[...], nm_ref[...], nv_ref[...] = _adam_math(w_ref[...], g_ref[...], m_ref[...], v_ref[...])

    return _pcall(body, name="adam_small", out_shape=[jax.ShapeDtypeStruct(g.shape, F32)] * 3,
                  compiler_params=_params(None))(g, w, m, v)


def _pack(pieces, rows):
    flat = jnp.concatenate([p.reshape(-1).astype(F32) for p in pieces])
    return jnp.pad(flat, (0, rows * D_MODEL - flat.shape[0])).reshape(rows, D_MODEL)


def _unpack(packed, shapes):
    flat = packed.reshape(-1)
    out, off = [], 0
    for shp in shapes:
        size = 1
        for s in shp:
            size *= s
        out.append(flat[off:off + size].reshape(shp))
        off += size
    return out


def _permute_in(w):
    pad = jnp.zeros((w.shape[0], PROJ_W - D_IN), w.dtype)
    return jnp.concatenate([w[:, 0:4096], w[:, 4112:6672], w[:, 4096:4112], w[:, 6672:6688], pad], axis=1)


def _unpermute_in(g):
    return jnp.concatenate([g[:, 0:4096], g[:, 6656:6672], g[:, 4096:6656], g[:, 6672:6688]], axis=1)


def _lane_row(vec, start):
    return jnp.zeros((LANES,), F32).at[start:start + vec.shape[0]].set(vec)


def _cols_from_shards(g):
    return jnp.transpose(g, (1, 0, 2)).reshape(g.shape[1], N_DEV * g.shape[2])


def _cols_to_shards(a):
    return jnp.transpose(a.astype(BF16).reshape(a.shape[0], N_DEV, a.shape[1] // N_DEV), (1, 0, 2))


def _rows_to_shards(a):
    return a.astype(BF16).reshape(N_DEV, a.shape[0] // N_DEV, a.shape[1])


def _local_step(x, tgt, wp_in, rest, p, rest_is_sharded):
    bsz, seq, d = x.shape
    t = bsz * seq
    x2 = x.reshape(t, d)
    tgt2 = tgt.reshape(t, d)
    tm = min(256, seq)
    tm_big = min(512, seq)
    tm_wide = min(256, seq)
    sb = min(512, seq)

    gp = jnp.zeros((SUBLANES, LANES), F32).at[0].set(_lane_row(p["gdn_a_log"], 8)).at[1].set(_lane_row(p["gdn_dt_bias"], 8))
    sp = jnp.zeros((SUBLANES, LANES), F32).at[0].set(_lane_row(p["ssd_a_log"], 16)).at[1].set(_lane_row(p["ssd_dt_bias"], 16))
    dvec = jnp.repeat(p["ssd_d"], SSD_P).reshape(1, d)
    row = lambda v: v.reshape(1, -1)
    pre_mix, post_mix, pre_ffn, post_ffn = (row(p[k]) for k in ("pre_mix_norm", "post_mix_norm", "pre_ffn_norm", "post_ffn_norm"))
    gnw, snw = row(p["gdn_norm_w"]), row(p["ssd_norm_w"])
    gcw, scw, scb, fcw, fcb = p["gdn_conv_w"], p["ssd_conv_w"], row(p["ssd_conv_b"]), p["ffn_conv_w"], row(p["ffn_conv_b"])

    h1, proj = _norm_proj(x2, pre_mix, wp_in)
    b3 = lambda a: a.reshape(bsz, seq, a.shape[-1])
    b2 = lambda a: a.reshape(t, a.shape[-1])
    rows_of = lambda a, lo, n: jnp.transpose(a[:, lo:lo + n].reshape(bsz, seq // CHUNK, CHUNK, n), (0, 1, 3, 2))
    qn, kn, vv, gs, ypre_gdn = _gdn_prep(proj, gcw, gp, seq, tm_big)
    gr = rows_of(gs, 8, GDN_HEADS)
    qn, kn, vv, gs = b3(qn), b3(kn), b3(vv), b3(gs)
    (o_gdn, gdn_st, gdn_ti), gathered = _gdn_chunk_fwd(qn, kn, vv, gs, gr, bsz, seq, sb, list(rest) if rest_is_sharded else [])
    if rest_is_sharded:
        w_out, w_up, w_down = gathered[0].reshape(-1, d), _cols_from_shards(gathered[1]), gathered[2].reshape(-1, d)
    else:
        w_out, w_up, w_down = rest
    o_gdn = b2(o_gdn)
    xs, bc, dtx, acsx, acs, ypre_ssd = _ssd_prep(proj, scw, scb, sp, seq, tm_big)
    ar = rows_of(acs, 16, SSD_HEADS)
    y_ssd, ssd_st = _ssd_chunk_fwd(b3(xs), b3(bc), b3(dtx), b3(acsx), b3(acs), ar, bsz, seq, sb)
    y_ssd = b2(y_ssd)
    mixin = _gate_norm(o_gdn, y_ssd, xs, proj, gnw, snw, dvec, tm_big)
    mix, x1, h2 = _out_mid(mixin, w_out, x2, post_mix, pre_ffn)
    u_pre = _matmul("mm_up", h2, w_up, "nn", F32)
    act, u = _ffn_act(u_pre, fcw, fcb, seq, tm_wide)
    dy, df, loss_lanes, d_post_ffn = _down_final(act, w_down, x1, tgt2, post_ffn)

    g_down = _matmul("mm_dw_down", act, df, "tn", BF16, tm=1408, tk=2048)
    dact = _matmul("mm_dact", df, w_down, "nt", F32, tn=1408)
    du_pre, d_fcw, d_fcb = _ffn_bwd(u, u_pre, dact, fcw, seq, tm_wide)
    g_up = _matmul("mm_dw_up", h2, du_pre, "tn", BF16, tk=2048)
    dx1, dmix, d_post_mix, d_pre_ffn = _dh2_mid_bwd(du_pre, w_up, x1, mix, dy, post_mix, pre_ffn)
    g_out = _matmul("mm_dw_out", mixin, dmix, "tn", BF16, tk=2048)
    do_gdn, dza, dy_ssd, dxs_d, dzs, d_gnw, d_snw, d_dd = _dmixin_gate_norm_bwd(dmix, w_out, o_gdn, y_ssd, xs, proj, gnw, snw, dvec)
    dyx, dybc, ddt, dacs, d_scb_x, d_scb_bc = _ssd_chunk_bwd(
        b3(xs), b3(bc), b3(dtx), b3(acsx), b3(acs), ar, b3(dy_ssd), ssd_st, b3(ypre_ssd), b3(dxs_d), bsz, seq, sb)
    dyx, dybc, ddt, dacs = b2(dyx), b2(dybc), b2(ddt), b2(dacs)
    d_scb = jnp.concatenate([d_scb_x, d_scb_bc], axis=1)
    riders = [_rows_to_shards(g_out), _cols_to_shards(g_up), _rows_to_shards(g_down)] if rest_is_sharded else []
    dgdn, received = _gdn_chunk_bwd(qn, kn, vv, gs, gr, b3(do_gdn), gdn_st, gdn_ti, b3(ypre_gdn), bsz, seq, min(256, seq), riders)
    if rest_is_sharded:
        g_out, g_up, g_down = received
    dyq, dyk, dyv, dgs = (b2(a) for a in dgdn)
    dsm, d_gp, d_sp = _gates_bwd(proj, dgs, ddt, dacs, gp, sp, tm)
    dproj, d_gcw, d_scw = _assemble_dproj((dyq, dyk, dyv), dza, dzs, (dyx, dybc), dsm, proj, gcw, scw, seq, tm)
    g_in = _matmul("mm_dw_in", h1, dproj, "tn", BF16, tk=2048)
    if rest_is_sharded:
        (dx, d_pre_mix), (g_in,) = _dh1_first_bwd(dproj, wp_in, x2, dx1, pre_mix, [_cols_to_shards(_unpermute_in(g_in))])
    else:
        dx, d_pre_mix = _dh1_first_bwd(dproj, wp_in, x2, dx1, pre_mix, [])

    small = dict(pre_mix_norm=d_pre_mix, ssd_norm_w=d_snw, post_mix_norm=d_post_mix, pre_ffn_norm=d_pre_ffn,
                 post_ffn_norm=d_post_ffn, dd_lanes=d_dd, loss_lanes=loss_lanes, gdn_gates=d_gp, ssd_gates=d_sp,
                 gdn_norm_w=d_gnw, gdn_conv_w=d_gcw[0:4], ssd_conv_w=d_scw[0:4], ssd_conv_b=d_scb,
                 ffn_conv_w=d_fcw[0:3], ffn_conv_b=d_fcb)
    return dx.reshape(bsz, seq, d), g_in, g_out, g_up, g_down, small


def kernel(x, pre_mix_norm, w_in, gdn_conv_w, gdn_a_log, gdn_dt_bias, gdn_norm_w, ssd_conv_w, ssd_conv_b, ssd_a_log, ssd_dt_bias, ssd_d, ssd_norm_w, w_out, post_mix_norm, pre_ffn_norm, w_up, ffn_conv_w, ffn_conv_b, w_down, post_ffn_norm, loss_target, m_pre_mix_norm, m_w_in, m_gdn_conv_w, m_gdn_a_log, m_gdn_dt_bias, m_gdn_norm_w, m_ssd_conv_w, m_ssd_conv_b, m_ssd_a_log, m_ssd_dt_bias, m_ssd_d, m_ssd_norm_w, m_w_out, m_post_mix_norm, m_pre_ffn_norm, m_w_up, m_ffn_conv_w, m_ffn_conv_b, m_w_down, m_post_ffn_norm, v_pre_mix_norm, v_w_in, v_gdn_conv_w, v_gdn_a_log, v_gdn_dt_bias, v_gdn_norm_w, v_ssd_conv_w, v_ssd_conv_b, v_ssd_a_log, v_ssd_dt_bias, v_ssd_d, v_ssd_norm_w, v_w_out, v_post_mix_norm, v_pre_ffn_norm, v_w_up, v_ffn_conv_w, v_ffn_conv_b, v_w_down, v_post_ffn_norm):
    names = ["pre_mix_norm", "w_in", "gdn_conv_w", "gdn_a_log", "gdn_dt_bias", "gdn_norm_w", "ssd_conv_w", "ssd_conv_b",
             "ssd_a_log", "ssd_dt_bias", "ssd_d", "ssd_norm_w", "w_out", "post_mix_norm", "pre_ffn_norm", "w_up",
             "ffn_conv_w", "ffn_conv_b", "w_down", "post_ffn_norm"]
    w_args = [pre_mix_norm, w_in, gdn_conv_w, gdn_a_log, gdn_dt_bias, gdn_norm_w, ssd_conv_w, ssd_conv_b, ssd_a_log, ssd_dt_bias, ssd_d, ssd_norm_w, w_out, post_mix_norm, pre_ffn_norm, w_up, ffn_conv_w, ffn_conv_b, w_down, post_ffn_norm]
    m_args = [m_pre_mix_norm, m_w_in, m_gdn_conv_w, m_gdn_a_log, m_gdn_dt_bias, m_gdn_norm_w, m_ssd_conv_w, m_ssd_conv_b, m_ssd_a_log, m_ssd_dt_bias, m_ssd_d, m_ssd_norm_w, m_w_out, m_post_mix_norm, m_pre_ffn_norm, m_w_up, m_ffn_conv_w, m_ffn_conv_b, m_w_down, m_post_ffn_norm]
    v_args = [v_pre_mix_norm, v_w_in, v_gdn_conv_w, v_gdn_a_log, v_gdn_dt_bias, v_gdn_norm_w, v_ssd_conv_w, v_ssd_conv_b, v_ssd_a_log, v_ssd_dt_bias, v_ssd_d, v_ssd_norm_w, v_w_out, v_post_mix_norm, v_pre_ffn_norm, v_w_up, v_ffn_conv_w, v_ffn_conv_b, v_w_down, v_post_ffn_norm]
    w = {k: a[0] for k, a in zip(names, w_args)}
    m = {k: a[0] for k, a in zip(names, m_args)}
    v = {k: a[0] for k, a in zip(names, v_args)}
    idx = 4 * lax.axis_index("x") + 2 * lax.axis_index("y") + lax.axis_index("c")
    big = ("w_in", "w_out", "w_up", "w_down")
    conv = ("gdn_conv_w", "ssd_conv_w", "ffn_conv_w")

    conv_local = jnp.concatenate([jnp.pad(w[k], ((0, 4 - w[k].shape[0]), (0, 0))) for k in conv], axis=1)
    g_in, g_conv = _gather_two_level("gather_weights", [w["w_in"].astype(BF16), conv_local])
    wp_in = _permute_in(_cols_from_shards(g_in))
    p = {k: w[k] for k in names if k not in big and k not in conv}
    off = 0
    for k in conv:
        cw = w[k].shape[1]
        p[k] = jnp.transpose(g_conv[:, :w[k].shape[0], off:off + cw], (1, 0, 2)).reshape(w[k].shape[0], N_DEV * cw)
        off += cw

    rest = tuple(w[k].astype(BF16) for k in ("w_out", "w_up", "w_down"))
    dx, p_in, p_out, p_up, p_down, small = _local_step(x, loss_target, wp_in, rest, p, True)

    gate_row = jnp.concatenate([small["gdn_gates"][0], small["gdn_gates"][1], small["ssd_gates"][0], small["ssd_gates"][1],
                                small["gdn_norm_w"][0], jnp.zeros((D_MODEL - 5 * LANES,), F32)]).reshape(1, D_MODEL)
    pack = _pack([small["pre_mix_norm"], small["ssd_norm_w"], small["post_mix_norm"], small["pre_ffn_norm"],
                  small["post_ffn_norm"], small["dd_lanes"], small["loss_lanes"], gate_row,
                  small["gdn_conv_w"], small["ssd_conv_w"], jnp.pad(small["ssd_conv_b"], ((0, 0), (0, 512))),
                  jnp.pad(small["ffn_conv_w"].reshape(-1), (0, 17 * D_MODEL - 3 * 2 * D_FF)),
                  jnp.pad(small["ffn_conv_b"], ((0, 0), (0, 512)))], SMALL_ROWS)
    (pack_all,) = _gather_two_level("gather_small", [pack])
    ssum, extra = _small_sum(pack_all)

    grads, deltas, new_m, new_v = {}, {}, {}, {}
    for k, parts in (("w_in", p_in), ("w_out", p_out), ("w_up", p_up), ("w_down", p_down)):
        grads[k], deltas[k], new_m[k], new_v[k] = _adam_big("adam_" + k, parts, w[k], m[k], v[k], 256)

    flat = ssum.reshape(-1)
    gate = ssum[7]
    sg = dict(pre_mix_norm=ssum[0], ssd_norm_w=ssum[1], post_mix_norm=ssum[2], pre_ffn_norm=ssum[3], post_ffn_norm=ssum[4],
              gdn_a_log=gate[8:16], gdn_dt_bias=gate[LANES + 8:LANES + 16], ssd_a_log=gate[2 * LANES + 16:2 * LANES + 32],
              ssd_dt_bias=gate[3 * LANES + 16:3 * LANES + 32], gdn_norm_w=gate[4 * LANES:5 * LANES], ssd_d=extra[0, 0:SSD_HEADS])
    o = 8 * D_MODEL
    full_gcw = flat[o:o + 4 * 3072].reshape(4, 3072)
    o += 12 * D_MODEL
    full_scw = flat[o:o + 4 * 1536].reshape(4, 1536)
    o += 6 * D_MODEL
    sg["ssd_conv_b"] = flat[o:o + 1536]
    o += 2 * D_MODEL
    full_fcw = flat[o:o + 3 * 2 * D_FF].reshape(3, 2 * D_FF)
    o += 17 * D_MODEL
    sg["ffn_conv_b"] = flat[o:o + 2 * D_FF]
    for k, full in (("gdn_conv_w", full_gcw), ("ssd_conv_w", full_scw), ("ffn_conv_w", full_fcw)):
        cw = w[k].shape[1]
        sg[k] = lax.dynamic_slice_in_dim(full, idx * cw, cw, axis=1)
    small_names = [k for k in names if k not in big]
    rows = 24
    gpk = _pack([sg[k] for k in small_names], rows)
    dpk, mpk, vpk = _adam_small(gpk, _pack([w[k] for k in small_names], rows), _pack([m[k] for k in small_names], rows),
                                _pack([v[k] for k in small_names], rows))
    shapes = [w[k].shape for k in small_names]
    for k, g_, d_, m_, v_ in zip(small_names, _unpack(gpk, shapes), _unpack(dpk, shapes), _unpack(mpk, shapes), _unpack(vpk, shapes)):
        grads[k], deltas[k], new_m[k], new_v[k] = g_, d_, m_, v_

    loss = extra[1, 0]
    lead = lambda a: a[None]
    return (loss, dx, *[lead(grads[k]) for k in names], *[lead(deltas[k]) for k in names],
            *[lead(new_m[k]) for k in names], *[lead(new_v[k]) for k in names])
```

```python
import functools

import jax
import jax.numpy as jnp
from jax import lax
from jax.experimental import pallas as pl
from jax.experimental.pallas import tpu as pltpu

F32 = jnp.float32
BF16 = jnp.bfloat16
MXU_DTYPE = jnp.bfloat16
HIGHEST = lax.Precision.HIGHEST
VMEM_LIMIT_V7X = 48 * 1024 * 1024
SUBLANES = 8
LANES = 128

D_MODEL = 1024
GDN_HEADS = 8
GDN_DK = 128
SSD_HEADS = 16
SSD_P = 64
SSD_GROUPS = 2
SSD_HPG = 8
SSD_N = 128
CHUNK = 128
D_FF = 2816
EPS = 1e-6
N_DEV = 8
PROJ_W = 7168
SMALL_CB = 52
D_IN = 6688

ADAM_LR = 0.001
ADAM_B1 = 0.9
ADAM_B2 = 0.999
ADAM_EPS = 1e-08
ADAM_WD = 0.01
ADAM_STEP = 10

NN = (((1,), (0,)), ((), ()))
NT = (((1,), (1,)), ((), ()))
TN = (((0,), (0,)), ((), ()))


def _pcall(body, **kw):
    return pl.pallas_call(body, **kw)


def _mm(a, b, dims=NN):
    return lax.dot_general(a.astype(MXU_DTYPE), b.astype(MXU_DTYPE), dims, preferred_element_type=F32)


def _mmx(a, b, dims=NN):
    return lax.dot_general(a, b, dims, precision=HIGHEST, preferred_element_type=F32)


def _split(a):
    hi = a.astype(MXU_DTYPE)
    return hi, (a - hi.astype(F32)).astype(MXU_DTYPE)


def _mm3(a, b, dims=NN):
    (ah, al), (bh, bl) = _split(a), _split(b)
    dot = lambda p, q: lax.dot_general(p, q, dims, preferred_element_type=F32)
    return dot(ah, bh) + (dot(ah, bl) + dot(al, bh))


def _mmsel(a, sel, dims=NN, terms=2):
    s = sel.astype(MXU_DTYPE)
    out = None
    for _ in range(terms):
        part = a.astype(MXU_DTYPE)
        a = a - part.astype(F32)
        prod = lax.dot_general(part, s, dims, preferred_element_type=F32)
        out = prod if out is None else out + prod
    return out


def _sigmoid(x):
    return 0.5 * jnp.tanh(0.5 * x) + 0.5


def _softplus(x):
    return jnp.maximum(x, 0.0) + jnp.log(1.0 + jnp.exp(-jnp.abs(x)))


def _dsilu(x, s):
    return s * (1.0 + x * (1.0 - s))


def _rowsum(x):
    return jnp.sum(x, axis=1, keepdims=True)


def _colsum(x):
    return jnp.sum(x, axis=0, keepdims=True)


def _pick(dim, pref):
    if dim <= pref:
        return dim
    best = None
    t = LANES
    while t <= pref:
        if dim % t == 0:
            best = t
        t += LANES
    return dim if best is None else best


def _params(sem):
    return pltpu.CompilerParams(dimension_semantics=sem, vmem_limit_bytes=VMEM_LIMIT_V7X)


def _matmul(name, a, b, mode, out_dtype, tm=1024, tn=1024, tk=1024):
    if mode == "nn":
        (m, k), (_, n) = a.shape, b.shape
    elif mode == "nt":
        (m, k), (n, _) = a.shape, b.shape
    else:
        (k, m), (_, n) = a.shape, b.shape
    tm, tn, tk = _pick(m, tm), _pick(n, tn), _pick(k, tk)
    nk = k // tk
    if mode == "tn":
        a_spec = pl.BlockSpec((tk, tm), lambda i, j, kk: (kk, i))
    else:
        a_spec = pl.BlockSpec((tm, tk), lambda i, j, kk: (i, kk))
    if mode == "nt":
        b_spec = pl.BlockSpec((tn, tk), lambda i, j, kk: (j, kk))
    else:
        b_spec = pl.BlockSpec((tk, tn), lambda i, j, kk: (kk, j))
    dims = {"nn": NN, "nt": NT, "tn": TN}[mode]

    def body(a_ref, b_ref, o_ref, *acc):
        if nk == 1:
            o_ref[...] = _mm(a_ref[...], b_ref[...], dims).astype(out_dtype)
            return
        kk = pl.program_id(2)

        @pl.when(kk == 0)
        def _():
            acc[0][...] = jnp.zeros_like(acc[0])

        acc[0][...] += _mm(a_ref[...], b_ref[...], dims)

        @pl.when(kk == nk - 1)
        def _():
            o_ref[...] = acc[0][...].astype(out_dtype)

    return _pcall(
        body, name=name, grid=(m // tm, n // tn, nk),
        in_specs=[a_spec, b_spec],
        out_specs=pl.BlockSpec((tm, tn), lambda i, j, kk: (i, j)),
        out_shape=jax.ShapeDtypeStruct((m, n), out_dtype),
        scratch_shapes=[pltpu.VMEM((tm, tn), F32)] if nk > 1 else [],
        compiler_params=_params(("parallel", "parallel", "arbitrary")),
    )(a, b)


def _matmul_rows(name, a, b, mode, epilogue, row_ins, full_ins, outs, accs=(), tm=512, tk=1024, scatter_riders=()):
    if mode == "nn":
        (m, k), (_, n) = a.shape, b.shape
    else:
        (m, k), (n, _) = a.shape, b.shape
    tm, tk = _pick(m, tm), _pick(k, tk)
    nk = k // tk
    a_spec = pl.BlockSpec((tm, tk), lambda i, kk: (i, kk))
    b_spec = pl.BlockSpec((n, tk), lambda i, kk: (0, kk)) if mode == "nt" else pl.BlockSpec((tk, n), lambda i, kk: (kk, 0))
    dims = NT if mode == "nt" else NN
    n_row, n_full, n_out, n_acc = len(row_ins), len(full_ins), len(outs), len(accs)

    def body(a_ref, b_ref, *rest):
        ins = rest[:n_row + n_full]
        out_refs = rest[n_row + n_full:n_row + n_full + n_out]
        acc_refs = rest[n_row + n_full + n_out:n_row + n_full + n_out + n_acc]
        prod_scr = rest[-1]
        i, kk = pl.program_id(0), pl.program_id(1)

        if n_acc:
            @pl.when((i == 0) & (kk == 0))
            def _():
                for r in acc_refs:
                    r[...] = jnp.zeros_like(r)

        if nk == 1:
            epilogue(_mm(a_ref[...], b_ref[...], dims), *ins, *out_refs, *acc_refs)
            return

        @pl.when(kk == 0)
        def _():
            prod_scr[...] = jnp.zeros_like(prod_scr)

        prod_scr[...] += _mm(a_ref[...], b_ref[...], dims)

        @pl.when(kk == nk - 1)
        def _():
            epilogue(prod_scr[...], *ins, *out_refs, *acc_refs)

    grid = (m // tm, nk)
    riders = list(scatter_riders)
    n_in = 2 + n_row + n_full
    any_spec, rider_shapes, rider_sems, wrap = _riding_exchange(riders, True, n_in, n_out + n_acc, grid)
    row_ins = [r if isinstance(r, tuple) else (r, r.shape[1], 0) for r in row_ins]
    in_specs = [a_spec, b_spec] + [pl.BlockSpec((tm, w), lambda i, kk, cb=cb: (i, cb)) for _, w, cb in row_ins]
    in_specs += [pl.BlockSpec(f.shape, lambda i, kk, nd=f.ndim: (0,) * nd) for f in full_ins]
    row_ins = [r for r, _, _ in row_ins]
    out_specs = [pl.BlockSpec((tm, w), lambda i, kk: (i, 0)) for w, _ in outs]
    out_specs += [pl.BlockSpec(s, lambda i, kk: (0, 0)) for s in accs]
    out_shape = [jax.ShapeDtypeStruct((m, w), dt) for w, dt in outs] + [jax.ShapeDtypeStruct(s, F32) for s in accs]
    res = _pcall(
        wrap(body), name=name, grid=grid,
        in_specs=in_specs + any_spec, out_specs=out_specs + any_spec, out_shape=out_shape + rider_shapes,
        scratch_shapes=[pltpu.VMEM((tm, n), F32)] + rider_sems,
        compiler_params=_params(("arbitrary", "arbitrary")),
    )(a, b, *row_ins, *full_ins, *riders)
    return (res[:n_out + n_acc], res[n_out + n_acc:]) if riders else res


def _rowwise(name, body, n_rows, tm, ins, outs, accs=()):
    arrays, in_specs = [], []
    last8 = n_rows // SUBLANES - 1
    per = tm // SUBLANES
    for spec in ins:
        kind, arr = spec[0], spec[1]
        if kind == "full":
            in_specs.append(pl.BlockSpec(arr.shape, lambda i, nd=arr.ndim: (0,) * nd))
        else:
            w, cb = spec[2], spec[3]
            if kind == "row":
                in_specs.append(pl.BlockSpec((tm, w), lambda i, cb=cb: (i, cb)))
            elif kind == "prev":
                in_specs.append(pl.BlockSpec((SUBLANES, w), lambda i, cb=cb: (jnp.maximum(i * per - 1, 0), cb)))
            else:
                in_specs.append(pl.BlockSpec((SUBLANES, w), lambda i, cb=cb: (jnp.minimum((i + 1) * per, last8), cb)))
        arrays.append(arr)
    out_shape = [jax.ShapeDtypeStruct((n_rows, w), dt) for (w, dt) in outs]
    out_shape += [jax.ShapeDtypeStruct(s, F32) for s in accs]
    out_specs = [pl.BlockSpec((tm, w), lambda i: (i, 0)) for (w, _) in outs]
    out_specs += [pl.BlockSpec(s, lambda i: (0, 0)) for s in accs]
    n_io = len(ins) + len(outs)

    def kern(*refs):
        i = pl.program_id(0)
        if accs:
            @pl.when(i == 0)
            def _():
                for r in refs[n_io:]:
                    r[...] = jnp.zeros_like(r)
        body(i, *refs)

    res = _pcall(
        kern, name=name, grid=(n_rows // tm,), in_specs=in_specs, out_specs=out_specs, out_shape=out_shape,
        compiler_params=_params(("arbitrary",)),
    )(*arrays)
    return res


def _shift_down(x, halo, j):
    r = pltpu.roll(x, j, 0)
    hr = pltpu.roll(halo, j, 0)
    rows = lax.broadcasted_iota(jnp.int32, (SUBLANES, x.shape[1]), 0)
    top = jnp.where(rows < j, hr, r[0:SUBLANES])
    return jnp.concatenate([top, r[SUBLANES:]], axis=0)


def _shift_up(x, halo, j):
    tm = x.shape[0]
    r = pltpu.roll(x, tm - j, 0)
    hr = pltpu.roll(halo, SUBLANES - j, 0)
    rows = lax.broadcasted_iota(jnp.int32, (SUBLANES, x.shape[1]), 0)
    bot = jnp.where(rows >= SUBLANES - j, hr, r[tm - SUBLANES:])
    return jnp.concatenate([r[:tm - SUBLANES], bot], axis=0)


def _conv_taps(x, halo, kw):
    return [x if kw - 1 - k == 0 else _shift_down(x, halo, kw - 1 - k) for k in range(kw)]


def _conv(taps, w):
    y = taps[0] * w[0:1]
    for k in range(1, len(taps)):
        y = y + taps[k] * w[k:k + 1]
    return y


def _rms(x, width):
    r = lax.rsqrt(jnp.sum(x * x, axis=-1, keepdims=True) * (1.0 / width) + EPS)
    return x * r, r


def _rms_bwd(xh, r, dxh, width):
    return r * (dxh - xh * (jnp.sum(dxh * xh, axis=-1, keepdims=True) * (1.0 / width)))


def _seq_flags(i, seq, tm):
    nps = seq // tm
    pos = i % nps
    return jnp.where(pos == 0, 0.0, 1.0), jnp.where(pos == nps - 1, 0.0, 1.0)


def _norm_proj(x, w, wp, tm=1024, tn=1024):
    t, d = x.shape
    n = wp.shape[1]
    tm, tn = _pick(t, tm), _pick(n, tn)

    def body(x_ref, w_ref, b_ref, h_ref, o_ref, h_scr):
        @pl.when(pl.program_id(1) == 0)
        def _():
            xh, _ = _rms(x_ref[...], d)
            h = (xh * w_ref[...]).astype(BF16)
            h_scr[...] = h
            h_ref[...] = h

        o_ref[...] = _mm(h_scr[...], b_ref[...])

    return _pcall(
        body, name="mm_norm_proj", grid=(t // tm, n // tn),
        in_specs=[pl.BlockSpec((tm, d), lambda i, j: (i, 0)), pl.BlockSpec((1, d), lambda i, j: (0, 0)),
                  pl.BlockSpec((d, tn), lambda i, j: (0, j))],
        out_specs=[pl.BlockSpec((tm, d), lambda i, j: (i, 0)), pl.BlockSpec((tm, tn), lambda i, j: (i, j))],
        out_shape=[jax.ShapeDtypeStruct((t, d), BF16), jax.ShapeDtypeStruct((t, n), F32)],
        scratch_shapes=[pltpu.VMEM((tm, d), BF16)],
        compiler_params=_params(("parallel", "arbitrary")),
    )(x, w, wp)


def _gdn_prep(proj, cw, gp, seq, tm):
    t = proj.shape[0]
    d = D_MODEL

    def body(i, q_ref, qh_ref, k_ref, kh_ref, v_ref, vh_ref, sm_ref, cw_ref, gp_ref, gs_ref, ypre_ref):
        keep, _ = _seq_flags(i, seq, tm)
        for x_ref, h_ref, off in ((q_ref, qh_ref, 0), (k_ref, kh_ref, d), (v_ref, vh_ref, 2 * d)):
            ypre_ref[:, off:off + d] = _conv(_conv_taps(x_ref[...], h_ref[...] * keep, 4), cw_ref[:, off:off + d])
        sm = sm_ref[...]
        lane = lax.broadcasted_iota(jnp.int32, sm.shape, 1)
        beta = _sigmoid(sm)
        g = jnp.where((lane >= 8) & (lane < 16), -jnp.exp(gp_ref[0:1, :]) * _softplus(sm + gp_ref[1:2, :]), 0.0)
        gs_ref[...] = jnp.where(lane < 8, beta, _mmx(_block_tri(tm, False), g))

    ins = []
    for cb in range(3):
        ins += [("row", proj, d, cb), ("prev", proj, d, cb)]
    ins += [("row", proj, LANES, SMALL_CB), ("full", cw), ("full", gp)]
    return _rowwise("gdn_prep", body, t, tm, ins, [(LANES, F32), (3 * d, F32)])


def _block_tri(tm, upper):
    ri = lax.broadcasted_iota(jnp.int32, (tm, tm), 0)
    ci = lax.broadcasted_iota(jnp.int32, (tm, tm), 1)
    tri = (ri <= ci) if upper else (ri >= ci)
    return (tri & ((ri // CHUNK) == (ci // CHUNK))).astype(F32)


def _chunk_consts():
    row = lax.broadcasted_iota(jnp.int32, (CHUNK, CHUNK), 0)
    col = lax.broadcasted_iota(jnp.int32, (CHUNK, CHUNK), 1)
    return dict(
        tril=row >= col, strict=row > col, eye=(row == col).astype(F32),
        lane=lax.broadcasted_iota(jnp.int32, (CHUNK, LANES), 1),
        row1=lax.broadcasted_iota(jnp.int32, (CHUNK, 1), 0),
        ones=jnp.ones((CHUNK, LANES), F32))


def _hmap(fn, *lists):
    return [fn(*a) for a in zip(*lists)]


def _tri_inv(nmats, eye):
    levels = CHUNK.bit_length() - 2
    x = [eye - n for n in nmats]
    p = _hmap(_mm3, nmats, nmats)
    for lvl in range(levels):
        x = _hmap(lambda xi, pi: xi + _mm3(xi, pi), x, p)
        if lvl < levels - 1:
            p = _hmap(_mm3, p, p)
    return x


def _act_norm(y, scale):
    sy = _sigmoid(y)
    a = y * sy
    if scale is None:
        return a, (sy, None, None)
    n = lax.rsqrt(_rowsum(a * a) + EPS)
    ah = a * n
    return (ah if scale == 1.0 else ah * scale), (sy, n, ah)


def _act_norm_bwd(g, y, scale, pieces):
    sy, n, ah = pieces
    ds_ = _dsilu(y, sy)
    if scale is None:
        return g * ds_
    return (scale * n) * (g - ah * _rowsum(g * ah)) * ds_


def _gdn_gates(gs, gc_row, h, c):
    beta = _rowsum(jnp.where(c["lane"] == h, gs, 0.0))
    gc = _rowsum(jnp.where(c["lane"] == h + 8, gs, 0.0))
    dc = jnp.exp(jnp.where(c["tril"], gc - gc_row, -1e30))
    gl = gc[CHUNK - 1:CHUNK, :]
    return beta, dc, jnp.exp(gc), jnp.exp(gl), jnp.exp(gl - gc)


GDN_HB = GDN_HEADS


def _gdn_specs(seq, sb, hb, backward):
    assert hb == GDN_HEADS
    nsb = seq // sb
    ncb = sb // CHUNK
    order = (lambda j: nsb - 1 - j) if backward else (lambda j: j)
    specs = dict(
        wide=lambda: pl.BlockSpec((1, sb, hb * GDN_DK), lambda b, h, j: (b, order(j), h)),
        gs=lambda: pl.BlockSpec((1, sb, LANES), lambda b, h, j: (b, order(j), 0)),
        ypre=lambda part: pl.BlockSpec((1, sb, hb * GDN_DK), lambda b, h, j: (b, order(j), part)),
        gr=pl.BlockSpec((1, ncb, GDN_HEADS, CHUNK), lambda b, h, j: (b, order(j), 0, 0)),
        st=pl.BlockSpec((1, hb, ncb * GDN_DK, GDN_DK), lambda b, h, j: (b, h, order(j), 0)),
        ti=pl.BlockSpec((1, hb, sb, CHUNK), lambda b, h, j: (b, h, order(j), 0)))
    return nsb, ncb, specs


def _riding_exchange(arrays, scatter, n_in, n_out, grid):
    n = len(arrays)
    if n == 0:
        return [], [], [], lambda body: body
    any_spec = [pl.BlockSpec(memory_space=pl.ANY)] * n

    def wrap(body):
        def wrapped(*refs):
            ins = refs[n_in:n_in + n]
            outs = refs[n_in + n + n_out:n_in + 2 * n + n_out]
            sems = refs[len(refs) - 3:]
            pid = [pl.program_id(a) for a in range(len(grid))]
            first = functools.reduce(lambda a, b: a & b, [p == 0 for p in pid])
            last = functools.reduce(lambda a, b: a & b, [p == g - 1 for p, g in zip(pid, grid)])

            @pl.when(first)
            def _():
                _exchange_phase(ins, outs, sems, scatter, start=True)

            body(*refs[:n_in], *refs[n_in + n:n_in + n + n_out], *refs[n_in + 2 * n + n_out:len(refs) - 3])

            @pl.when(last)
            def _():
                _exchange_phase(ins, outs, sems, scatter, start=False)

        return wrapped

    return any_spec, _exchange_out_shapes(arrays, scatter), _exchange_sems(n), wrap


def _gdn_chunk_fwd(ypre, gs, gr, bsz, seq, sb, riders):
    hb = GDN_HB
    nsb, ncb, sp = _gdn_specs(seq, sb, hb, False)
    grid = (bsz, GDN_HEADS // hb, nsb)
    any_spec, rider_shapes, rider_sems, wrap = _riding_exchange(riders, False, 5, 3, grid)

    def body(yq_ref, yk_ref, yv_ref, gs_ref, gr_ref, o_ref, st_ref, ti_ref, s_scr):
        @pl.when(pl.program_id(2) == 0)
        def _():
            s_scr[...] = jnp.zeros_like(s_scr)

        c = _chunk_consts()

        def chunk(n, carry):
            r = pl.ds(pl.multiple_of(n * CHUNK, CHUNK), CHUNK)
            rs = pl.ds(pl.multiple_of(n * GDN_DK, GDN_DK), GDN_DK)
            gsv = gs_ref[0, r, :]
            heads = list(range(hb))
            sls = [slice(ih * GDN_DK, (ih + 1) * GDN_DK) for ih in heads]
            q = [_act_norm(yq_ref[0, r, sl], GDN_DK ** -0.5)[0] for sl in sls]
            k = [_act_norm(yk_ref[0, r, sl], 1.0)[0] for sl in sls]
            v = [_act_norm(yv_ref[0, r, sl], None)[0] for sl in sls]
            beta, dc, eg, egl, ekd = zip(*[
                _gdn_gates(gsv, gr_ref[0, n, pl.ds(ih, 1), :], ih, c) for ih in heads])
            kb = _hmap(lambda a, b: a * b, k, beta)
            amat = _hmap(lambda a, b, d_: jnp.where(c["strict"], _mm(a, b, NT) * d_, 0.0), kb, k, dc)
            tinv = _tri_inv(amat, c["eye"])
            u = _hmap(lambda t_, a, b: _mm3(t_, a * b), tinv, v, beta)
            w = _hmap(lambda t_, a, b: _mm3(t_, a * b), tinv, kb, eg)
            qk = _hmap(lambda a, b, d_: _mm(a, b, NT) * d_, q, k, dc)
            s = [s_scr[ih] for ih in heads]
            v_new = _hmap(lambda a, b, s_: a - _mm(b, s_), u, w, s)
            o = _hmap(lambda a, e, s_, qk_, vn: _mm(a * e, s_) + _mm(qk_, vn), q, eg, s, qk, v_new)
            s_new = _hmap(lambda s_, e, a, f, vn: s_ * e + _mm(a * f, vn, TN), s, egl, k, ekd, v_new)
            for ih in heads:
                o_ref[0, r, sls[ih]] = o[ih]
                st_ref[0, ih, rs, :] = s[ih]
                ti_ref[0, ih, r, :] = tinv[ih]
                s_scr[ih] = s_new[ih]
            return carry

        lax.fori_loop(0, ncb, chunk, 0)

    t3 = (bsz, seq, D_MODEL)
    res = _pcall(
        wrap(body), name="gdn_chunk_fwd", grid=grid,
        in_specs=[sp["ypre"](0), sp["ypre"](1), sp["ypre"](2), sp["gs"](), sp["gr"]] + any_spec,
        out_specs=[sp["wide"](), sp["st"], sp["ti"]] + any_spec,
        out_shape=[jax.ShapeDtypeStruct(t3, F32),
                   jax.ShapeDtypeStruct((bsz, GDN_HEADS, (seq // CHUNK) * GDN_DK, GDN_DK), F32),
                   jax.ShapeDtypeStruct((bsz, GDN_HEADS, seq, CHUNK), F32)] + rider_shapes,
        scratch_shapes=[pltpu.VMEM((hb, GDN_DK, GDN_DK), F32)] + rider_sems,
        compiler_params=_params(("arbitrary", "arbitrary", "arbitrary")),
    )(ypre, ypre, ypre, gs, gr, *riders)
    return res[:3], res[3:]


def _ssd_prep(proj, cw, cb, sp, seq, tm):
    t = proj.shape[0]
    d = D_MODEL
    ssd_w = SSD_HEADS * SSD_P

    def body(i, x_ref, xh_ref, bc_ref, bch_ref, sm_ref, cw_ref, cb_ref, sp_ref, xs_ref, bco_ref, dtx_ref, acsx_ref, acs_ref, ypre_ref):
        keep, _ = _seq_flags(i, seq, tm)
        y = _conv(_conv_taps(x_ref[...], xh_ref[...] * keep, 4), cw_ref[:, 0:d]) + cb_ref[:, 0:d]
        ypre_ref[:, 0:d] = y
        xs_ref[...] = y * _sigmoid(y)
        y = _conv(_conv_taps(bc_ref[...], bch_ref[...] * keep, 4), cw_ref[:, d:d + 512]) + cb_ref[:, d:d + 512]
        ypre_ref[:, d:d + 512] = y
        bco_ref[...] = y * _sigmoid(y)
        sm = sm_ref[...]
        lane = lax.broadcasted_iota(jnp.int32, sm.shape, 1)
        valid = (lane >= 16) & (lane < 32)
        dt = jnp.where(valid, _softplus(sm + sp_ref[1:2, :]), 0.0)
        adt = dt * (-jnp.exp(sp_ref[0:1, :]))
        acs = _mmx(_block_tri(tm, False), adt)
        l64 = lax.broadcasted_iota(jnp.int32, (LANES, ssd_w), 0)
        d64 = lax.broadcasted_iota(jnp.int32, (LANES, ssd_w), 1)
        e64 = (l64 - 16 == d64 // SSD_P).astype(F32)
        dtx_ref[...] = _mmsel(dt, e64, terms=3)
        acsx_ref[...] = _mmsel(acs, e64, terms=3)
        acs_ref[...] = acs

    ins = [("row", proj, d, 5), ("prev", proj, d, 5), ("row", proj, 512, 12), ("prev", proj, 512, 12),
           ("row", proj, LANES, SMALL_CB), ("full", cw), ("full", cb), ("full", sp)]
    return _rowwise("ssd_prep", body, t, tm, ins,
                    [(d, F32), (512, F32), (ssd_w, F32), (ssd_w, F32), (LANES, F32), (d + 512, F32)])


SSD_GW = SSD_HPG * SSD_P


def _ssd_head(acs, ar_ref, n, head, cbm, c):
    col = _rowsum(jnp.where(c["lane"] == head + 16, acs, 0.0))
    lm = jnp.exp(jnp.where(c["tril"], col - ar_ref[0, n, pl.ds(head, 1), :], -1e30))
    return lm, cbm * lm


def _ssd_specs(seq, sb):
    nsb = seq // sb
    ncb = sb // CHUNK
    def specs(order):
        return dict(
            wide=lambda: pl.BlockSpec((1, sb, SSD_HEADS * SSD_P), lambda b, j: (b, order(j), 0)),
            bc=lambda: pl.BlockSpec((1, sb, 2 * SSD_GROUPS * SSD_N), lambda b, j: (b, order(j), 0)),
            half=lambda: pl.BlockSpec((1, sb, SSD_GROUPS * SSD_N), lambda b, j: (b, order(j), 0)),
            small=lambda: pl.BlockSpec((1, sb, LANES), lambda b, j: (b, order(j), 0)),
            ar=pl.BlockSpec((1, ncb, SSD_HEADS, CHUNK), lambda b, j: (b, order(j), 0, 0)),
            st=pl.BlockSpec((1, ncb * SSD_N, SSD_HEADS * SSD_P), lambda b, j: (b, order(j), 0)))
    return nsb, ncb, specs(lambda j: j), specs(lambda j: nsb - 1 - j)


def _ssd_chunk_fwd(xs, bc, dtx, acsx, acs, ar, bsz, seq, sb):
    nsb, ncb, sp, _ = _ssd_specs(seq, sb)

    def body(x_ref, dtx_ref, ax_ref, bc_ref, acs_ref, ar_ref, y_ref, sts_ref, st_scr):
        @pl.when(pl.program_id(1) == 0)
        def _():
            st_scr[...] = jnp.zeros_like(st_scr)

        c = _chunk_consts()
        lane5 = lax.broadcasted_iota(jnp.int32, (CHUNK, SSD_GW), 1) // SSD_P

        def chunk(n, carry):
            r = pl.ds(pl.multiple_of(n * CHUNK, CHUNK), CHUNK)
            rs = pl.ds(pl.multiple_of(n * SSD_N, SSD_N), SSD_N)
            acsv = acs_ref[0, r, :]
            for g in range(SSD_GROUPS):
                gl = slice(g * SSD_GW, (g + 1) * SSD_GW)
                x, dt, ax = x_ref[0, r, gl], dtx_ref[0, r, gl], ax_ref[0, r, gl]
                bm = bc_ref[0, r, g * SSD_N:(g + 1) * SSD_N]
                cm = bc_ref[0, r, (SSD_GROUPS + g) * SSD_N:(SSD_GROUPS + g + 1) * SSD_N]
                xdt = x * dt
                cbm = _mm(cm, bm, NT)
                al = ax[CHUNK - 1:CHUNK, :]
                st = st_scr[:, gl]
                y = _mm(cm, st) * jnp.exp(ax)
                for hh in range(SSD_HPG):
                    _, gm = _ssd_head(acsv, ar_ref, n, g * SSD_HPG + hh, cbm, c)
                    y = y + _mm(gm, jnp.where(lane5 == hh, xdt, 0.0))
                y_ref[0, r, gl] = y
                sts_ref[0, rs, gl] = st
                st_scr[:, gl] = st * jnp.exp(al) + _mm(bm, xdt * jnp.exp(al - ax), TN)
            return carry

        lax.fori_loop(0, ncb, chunk, 0)

    return _pcall(
        body, name="ssd_chunk_fwd", grid=(bsz, nsb),
        in_specs=[sp["wide"](), sp["wide"](), sp["wide"](), sp["bc"](), sp["small"](), sp["ar"]],
        out_specs=[sp["wide"](), sp["st"]],
        out_shape=[jax.ShapeDtypeStruct((bsz, seq, SSD_HEADS * SSD_P), F32),
                   jax.ShapeDtypeStruct((bsz, (seq // CHUNK) * SSD_N, SSD_HEADS * SSD_P), F32)],
        scratch_shapes=[pltpu.VMEM((SSD_N, SSD_HEADS * SSD_P), F32)],
        compiler_params=_params(("parallel", "arbitrary")),
    )(xs, dtx, acsx, bc, acs, ar)


def _gate_norm(o_gdn, y_ssd, xs, proj, gnw, snw, dvec, tm):
    t = o_gdn.shape[0]
    d = D_MODEL

    def body(i, o_ref, za_ref, y_ref, xs_ref, zs_ref, gnw_ref, snw_ref, dv_ref, out_ref):
        for hh in range(GDN_HEADS):
            sl = slice(hh * GDN_DK, (hh + 1) * GDN_DK)
            oh, _ = _rms(o_ref[:, sl], GDN_DK)
            z = za_ref[:, sl]
            out_ref[:, sl] = (oh * gnw_ref[...] * (z * _sigmoid(z))).astype(BF16)
        zs = zs_ref[...]
        yg = (y_ref[...] + dv_ref[...] * xs_ref[...]) * (zs * _sigmoid(zs))
        for g in range(SSD_GROUPS):
            sl = slice(g * 512, (g + 1) * 512)
            yh, _ = _rms(yg[:, sl], 512)
            out_ref[:, d + g * 512:d + (g + 1) * 512] = (yh * snw_ref[:, sl]).astype(BF16)

    ins = [("row", o_gdn, d, 0), ("row", proj, d, 3), ("row", y_ssd, d, 0), ("row", xs, d, 0), ("row", proj, d, 4),
           ("full", gnw), ("full", snw), ("full", dvec)]
    return _rowwise("gate_norm", body, t, tm, ins, [(2 * d, BF16)])[0]


def _out_mid(mixin, w_out, x, pmw, pfw):
    d = D_MODEL

    def epilogue(mix, x_ref, pmw_ref, pfw_ref, mix_ref, x1_ref, h2_ref):
        mix_ref[...] = mix
        mh, _ = _rms(mix, d)
        x1 = x_ref[...] + mh * pmw_ref[...]
        x1_ref[...] = x1
        xh, _ = _rms(x1, d)
        h2_ref[...] = (xh * pfw_ref[...]).astype(BF16)

    return _matmul_rows("mm_out_mid", mixin, w_out, "nn", epilogue, [x], [pmw, pfw], [(d, F32), (d, F32), (d, BF16)],
                        tk=2 * d)


def _ffn_act(u_pre, cw, cb, seq, tm):
    t = u_pre.shape[0]

    def body(i, ug_ref, ugh_ref, uu_ref, uuh_ref, cw_ref, cb_ref, act_ref, u_ref):
        keep, _ = _seq_flags(i, seq, tm)
        gate = _conv(_conv_taps(ug_ref[...], ugh_ref[...] * keep, 3), cw_ref[:, 0:D_FF]) + cb_ref[:, 0:D_FF]
        up = _conv(_conv_taps(uu_ref[...], uuh_ref[...] * keep, 3), cw_ref[:, D_FF:2 * D_FF]) + cb_ref[:, D_FF:2 * D_FF]
        u_ref[:, 0:D_FF] = gate
        u_ref[:, D_FF:2 * D_FF] = up
        act_ref[...] = (gate * _sigmoid(gate) * up).astype(BF16)

    ins = [("row", u_pre, D_FF, 0), ("prev", u_pre, D_FF, 0), ("row", u_pre, D_FF, 1), ("prev", u_pre, D_FF, 1),
           ("full", cw), ("full", cb)]
    return _rowwise("ffn_act", body, t, tm, ins, [(D_FF, BF16), (2 * D_FF, F32)])


def _down_final(act, w_down, x1, tgt, w):
    d = D_MODEL

    def epilogue(f, x1_ref, t_ref, w_ref, dy_ref, df_ref, loss_ref, dw_ref):
        fh, r = _rms(f, d)
        e = x1_ref[...] + fh * w_ref[...] - t_ref[...]
        loss_ref[...] += _colsum(e * e) * (0.5 / d)
        dy = e * (1.0 / d)
        dy_ref[...] = dy
        dw_ref[...] += _colsum(dy * fh)
        df_ref[...] = _rms_bwd(fh, r, dy * w_ref[...], d).astype(BF16)

    return _matmul_rows("mm_down_final", act, w_down, "nn", epilogue, [x1, tgt], [w], [(d, F32), (d, BF16)],
                        accs=[(1, d), (1, d)], tk=D_FF)


def _ffn_bwd(u, u_pre, dact, cw, seq, tm):
    t = u.shape[0]

    def body(i, g_ref, gn_ref, up_ref, upn_ref, xg_ref, xu_ref, da_ref, dan_ref, cw_ref, dpre_ref, dcw_ref, dcb_ref):
        _, keep_next = _seq_flags(i, seq, tm)
        ext = lambda a_ref, n_ref: jnp.concatenate([a_ref[...], n_ref[...]], axis=0)
        rows = tm + SUBLANES
        gate, up = ext(g_ref, gn_ref), ext(up_ref, upn_ref)
        sg = _sigmoid(gate)
        da = jnp.concatenate([da_ref[...], dan_ref[...] * keep_next], axis=0)
        for off, grad, x_ref in ((0, da * up * _dsilu(gate, sg), xg_ref), (D_FF, da * gate * sg, xu_ref)):
            x = x_ref[...]
            own = grad[0:tm]
            acc = own * cw_ref[2:3, off:off + D_FF]
            dcb_ref[:, off:off + D_FF] += _colsum(own)
            dcw_ref[2:3, off:off + D_FF] += _colsum(own * x)
            for j in (1, 2):
                ahead = pltpu.roll(grad, rows - j, 0)[0:tm]
                acc = acc + ahead * cw_ref[2 - j:3 - j, off:off + D_FF]
                dcw_ref[2 - j:3 - j, off:off + D_FF] += _colsum(ahead * x)
            dpre_ref[:, off:off + D_FF] = acc.astype(BF16)

    ins = []
    for cb_ in range(2):
        ins += [("row", u, D_FF, cb_), ("next", u, D_FF, cb_)]
    ins += [("row", u_pre, D_FF, 0), ("row", u_pre, D_FF, 1), ("row", dact, D_FF, 0), ("next", dact, D_FF, 0), ("full", cw)]
    return _rowwise("ffn_bwd", body, t, tm, ins, [(2 * D_FF, BF16)], accs=[(SUBLANES, 2 * D_FF), (1, 2 * D_FF)])


def _assemble_dproj(dpre_qkv, dza, dzs, dpre_xbc, dsm, proj, gcw, scw, seq, tm):
    t = dza.shape[0]
    d = D_MODEL

    def body(i, dq_ref, dqn_ref, dk_ref, dkn_ref, dv_ref, dvn_ref, dx_ref, dxn_ref, dbc_ref, dbcn_ref, dza_ref, dzs_ref,
             dsm_ref, xq_ref, xk_ref, xv_ref, xx_ref, xbc_ref, gcw_ref, scw_ref, o_ref, dgcw_ref, dscw_ref):
        _, keep = _seq_flags(i, seq, tm)
        pieces = [(g_ref, n_ref, gcw_ref, dgcw_ref, x_ref, 0, c0) for g_ref, n_ref, x_ref, c0 in (
            (dq_ref, dqn_ref, xq_ref, 0), (dk_ref, dkn_ref, xk_ref, d), (dv_ref, dvn_ref, xv_ref, 2 * d))]
        pieces += [(g_ref, n_ref, scw_ref, dscw_ref, x_ref, 5 * d, c0) for g_ref, n_ref, x_ref, c0 in (
            (dx_ref, dxn_ref, xx_ref, 0), (dbc_ref, dbcn_ref, xbc_ref, d))]
        for d_ref, n_ref, cw_ref, dcw_ref, x_ref, base, c0 in pieces:
            w = x_ref.shape[1]
            x = x_ref[...]
            g = d_ref[...]
            halo = n_ref[...] * keep
            acc = g * cw_ref[3:4, c0:c0 + w]
            dcw_ref[3:4, c0:c0 + w] += _colsum(g * x)
            for j in range(1, 4):
                ahead = _shift_up(g, halo, j)
                acc = acc + ahead * cw_ref[3 - j:4 - j, c0:c0 + w]
                dcw_ref[3 - j:4 - j, c0:c0 + w] += _colsum(ahead * x)
            o_ref[:, base + c0:base + c0 + w] = acc.astype(BF16)
        o_ref[:, 3 * d:4 * d] = dza_ref[...]
        o_ref[:, 4 * d:5 * d] = dzs_ref[...]
        o_ref[:, 6 * d + 512:6 * d + 512 + LANES] = dsm_ref[...]
        o_ref[:, 6 * d + 512 + LANES:PROJ_W] = jnp.zeros((tm, PROJ_W - (6 * d + 512 + LANES)), BF16)

    ins = []
    for g in tuple(dpre_qkv) + tuple(dpre_xbc):
        ins += [("row", g, g.shape[1], 0), ("next", g, g.shape[1], 0)]
    ins += [("row", dza, d, 0), ("row", dzs, d, 0), ("row", dsm, LANES, 0),
           ("row", proj, d, 0), ("row", proj, d, 1), ("row", proj, d, 2), ("row", proj, d, 5), ("row", proj, 512, 12),
           ("full", gcw), ("full", scw)]
    return _rowwise("assemble_dproj", body, t, tm, ins, [(PROJ_W, BF16)], accs=[(SUBLANES, 3 * d), (SUBLANES, d + 512)])


def _dh2_mid_bwd(du_pre, w_up, x1, mix, dy, pmw, pfw):
    d = D_MODEL

    def epilogue(dh2, x1_ref, mix_ref, dy_ref, pmw_ref, pfw_ref, dx1_ref, dmix_ref, dpm_ref, dpf_ref):
        xh, r2 = _rms(x1_ref[...], d)
        dpf_ref[...] += _colsum(dh2 * xh)
        dx1 = dy_ref[...] + _rms_bwd(xh, r2, dh2 * pfw_ref[...], d)
        dx1_ref[...] = dx1
        mh, r = _rms(mix_ref[...], d)
        dpm_ref[...] += _colsum(dx1 * mh)
        dmix_ref[...] = _rms_bwd(mh, r, dx1 * pmw_ref[...], d).astype(BF16)

    return _matmul_rows("mm_dh2_mid_bwd", du_pre, w_up, "nt", epilogue, [x1, mix, dy], [pmw, pfw],
                        [(d, F32), (d, BF16)], accs=[(1, d), (1, d)], tk=D_FF)


def _dmixin_gate_norm_bwd(dmix, w_out, o_gdn, y_ssd, xs, proj, gnw, snw, dvec):
    d = D_MODEL

    def epilogue(dmixin, o_ref, za_ref, y_ref, xs_ref, zs_ref, gnw_ref, snw_ref, dv_ref,
                 do_ref, dza_ref, dy_ref, dxs_ref, dzs_ref, dgnw_ref, dsnw_ref, dd_ref):
        for hh in range(GDN_HEADS):
            sl = slice(hh * GDN_DK, (hh + 1) * GDN_DK)
            oh, r = _rms(o_ref[:, sl], GDN_DK)
            z = za_ref[:, sl]
            sz = _sigmoid(z)
            dm = dmixin[:, sl]
            don = dm * (z * sz)
            dza_ref[:, sl] = (dm * oh * gnw_ref[...] * _dsilu(z, sz)).astype(BF16)
            dgnw_ref[...] += _colsum(don * oh)
            do_ref[:, sl] = _rms_bwd(oh, r, don * gnw_ref[...], GDN_DK)
        zs = zs_ref[...]
        sz = _sigmoid(zs)
        sil = zs * sz
        x = xs_ref[...]
        y0 = y_ref[...] + dv_ref[...] * x
        yg = y0 * sil
        dms = dmixin[:, d:2 * d]
        for g in range(SSD_GROUPS):
            sl = slice(g * 512, (g + 1) * 512)
            yh, r = _rms(yg[:, sl], 512)
            dsnw_ref[:, sl] += _colsum(dms[:, sl] * yh)
            dyg = _rms_bwd(yh, r, dms[:, sl] * snw_ref[:, sl], 512)
            dy0 = dyg * sil[:, sl]
            dzs_ref[:, sl] = (dyg * y0[:, sl] * _dsilu(zs[:, sl], sz[:, sl])).astype(BF16)
            dy_ref[:, sl] = dy0
            dxs_ref[:, sl] = dy0 * dv_ref[:, sl]
            dd_ref[:, sl] += _colsum(dy0 * x[:, sl])

    row_ins = [o_gdn, (proj, d, 3), y_ssd, xs, (proj, d, 4)]
    return _matmul_rows("mm_dmixin_gate_norm_bwd", dmix, w_out, "nt", epilogue, row_ins, [gnw, snw, dvec],
                        [(d, F32), (d, BF16), (d, F32), (d, F32), (d, BF16)], accs=[(1, GDN_DK), (1, d), (1, d)], tm=256)


def _ssd_chunk_bwd(xs, bc, dtx, acsx, acs, ar, dy, sts, ypre, dxs_d, bsz, seq, sb):
    nsb, ncb, _, sp = _ssd_specs(seq, sb)
    bc_w = 2 * SSD_GROUPS * SSD_N
    x_w = SSD_HEADS * SSD_P

    def body(x_ref, dtx_ref, ax_ref, bc_ref, acs_ref, ar_ref, dy_ref, sts_ref, yx_ref, ybc_ref, dxd_ref,
             dx_ref, dbc_ref, ddt_ref, dacs_ref, dbx_ref, dbbc_ref, dst_scr):
        @pl.when(pl.program_id(1) == 0)
        def _():
            dst_scr[...] = jnp.zeros_like(dst_scr)

        @pl.when((pl.program_id(0) == 0) & (pl.program_id(1) == 0))
        def _():
            dbx_ref[...] = jnp.zeros_like(dbx_ref)
            dbbc_ref[...] = jnp.zeros_like(dbbc_ref)

        def to_conv_out(grad, y):
            return grad * _dsilu(y, _sigmoid(y))

        c = _chunk_consts()
        lane5 = lax.broadcasted_iota(jnp.int32, (CHUNK, SSD_GW), 1) // SSD_P
        row5 = lax.broadcasted_iota(jnp.int32, (CHUNK, SSD_GW), 0)
        sel_in = lax.broadcasted_iota(jnp.int32, (SSD_GW, LANES), 0) // SSD_P
        sel_out = lax.broadcasted_iota(jnp.int32, (SSD_GW, LANES), 1)

        def chunk(nn, carry):
            n = ncb - 1 - nn
            r = pl.ds(pl.multiple_of(n * CHUNK, CHUNK), CHUNK)
            rs = pl.ds(pl.multiple_of(n * SSD_N, SSD_N), SSD_N)
            acsv = acs_ref[0, r, :]
            ddt = jnp.zeros((CHUNK, LANES), F32)
            dacs = jnp.zeros((CHUNK, LANES), F32)
            for g in range(SSD_GROUPS):
                gl = slice(g * SSD_GW, (g + 1) * SSD_GW)
                x, dt, ax, dyv = x_ref[0, r, gl], dtx_ref[0, r, gl], ax_ref[0, r, gl], dy_ref[0, r, gl]
                bm = bc_ref[0, r, g * SSD_N:(g + 1) * SSD_N]
                cm = bc_ref[0, r, (SSD_GROUPS + g) * SSD_N:(SSD_GROUPS + g + 1) * SSD_N]
                st = sts_ref[0, rs, gl]
                dst = dst_scr[:, gl]
                rsel = (sel_in + (16 + g * SSD_HPG) == sel_out).astype(F32)
                xdt = x * dt
                cbm = _mm(cm, bm, NT)
                al = ax[CHUNK - 1:CHUNK, :]
                ex, el = jnp.exp(ax), jnp.exp(al)
                dec = jnp.exp(al - ax)
                xd = xdt * dec
                dye = dyv * ex
                dxd = _mm(bm, dst)
                dxdt = dec * dxd
                dcm = _mm(dye, st, NT)
                dbm = _mm(xd, dst, NT)
                z = dye * _mm(cm, st) - dxd * xd
                zl = _colsum(dst * st) * el + _colsum(dxd * xd)
                z = z + jnp.where(row5 == CHUNK - 1, zl, 0.0)
                dcb = jnp.zeros((CHUNK, CHUNK), F32)
                for hh in range(SSD_HPG):
                    head = g * SSD_HPG + hh
                    lm, gm = _ssd_head(acsv, ar_ref, n, head, cbm, c)
                    dym = jnp.where(lane5 == hh, dyv, 0.0)
                    dxdt = dxdt + _mm(gm, dym, TN)
                    dg = _mm(dym, xdt, NT)
                    dcb = dcb + dg * lm
                    pm = dg * gm
                    dacs = dacs + jnp.where(c["lane"] == head + 16, _rowsum(pm) - _mmsel(pm, c["ones"], TN), 0.0)
                for sl, grad in ((slice((SSD_GROUPS + g) * SSD_N, (SSD_GROUPS + g + 1) * SSD_N), dcm + _mm(dcb, bm)),
                                 (slice(g * SSD_N, (g + 1) * SSD_N), dbm + _mm(dcb, cm, TN))):
                    dpre = to_conv_out(grad, ybc_ref[0, r, sl])
                    dbc_ref[0, r, sl] = dpre
                    dbbc_ref[:, sl] += _colsum(dpre)
                dacs = dacs + _mmsel(z, rsel)
                ddt = ddt + _mmsel(dxdt * x, rsel)
                dpre = to_conv_out(dxdt * dt + dxd_ref[0, r, gl], yx_ref[0, r, gl])
                dx_ref[0, r, gl] = dpre
                dbx_ref[:, gl] += _colsum(dpre)
                dst_scr[:, gl] = dst * el + _mm(cm, dye, TN)
            ddt_ref[0, r, :] = ddt
            dacs_ref[0, r, :] = dacs
            return carry

        lax.fori_loop(0, ncb, chunk, 0)

    nsb_rev = lambda j: nsb - 1 - j
    return _pcall(
        body, name="ssd_chunk_bwd", grid=(bsz, nsb),
        in_specs=[sp["wide"](), sp["wide"](), sp["wide"](), sp["bc"](), sp["small"](), sp["ar"], sp["wide"](), sp["st"],
                  sp["wide"](), pl.BlockSpec((1, sb, bc_w), lambda b, j: (b, nsb_rev(j), x_w // bc_w)), sp["wide"]()],
        out_specs=[sp["wide"](), sp["bc"](), sp["small"](), sp["small"](),
                   pl.BlockSpec((1, x_w), lambda b, j: (0, 0)), pl.BlockSpec((1, bc_w), lambda b, j: (0, 0))],
        out_shape=[jax.ShapeDtypeStruct((bsz, seq, x_w), F32), jax.ShapeDtypeStruct((bsz, seq, bc_w), F32),
                   jax.ShapeDtypeStruct((bsz, seq, LANES), F32), jax.ShapeDtypeStruct((bsz, seq, LANES), F32),
                   jax.ShapeDtypeStruct((1, x_w), F32), jax.ShapeDtypeStruct((1, bc_w), F32)],
        scratch_shapes=[pltpu.VMEM((SSD_N, x_w), F32)],
        compiler_params=_params(("arbitrary", "arbitrary")),
    )(xs, dtx, acsx, bc, acs, ar, dy, sts, ypre, ypre, dxs_d)


def _gdn_chunk_bwd(ypre, gs, gr, do, sts, tis, bsz, seq, sb, riders):
    hb = GDN_HB
    nsb, ncb, sp = _gdn_specs(seq, sb, hb, True)
    grid = (bsz, GDN_HEADS // hb, nsb)
    any_spec, rider_shapes, rider_sems, wrap = _riding_exchange(riders, True, 8, 4, grid)

    def body(yq_ref, yk_ref, yv_ref, gs_ref, gr_ref, do_ref, st_ref, ti_ref, dq_ref, dk_ref, dv_ref, dgs_ref, ds_scr):
        @pl.when(pl.program_id(2) == 0)
        def _():
            ds_scr[...] = jnp.zeros_like(ds_scr)

        c = _chunk_consts()

        def chunk(nn, carry):
            n = ncb - 1 - nn
            r = pl.ds(pl.multiple_of(n * CHUNK, CHUNK), CHUNK)
            rs = pl.ds(pl.multiple_of(n * GDN_DK, GDN_DK), GDN_DK)
            gsv = gs_ref[0, r, :]
            heads = list(range(hb))
            sls = [slice(ih * GDN_DK, (ih + 1) * GDN_DK) for ih in heads]
            yq = [yq_ref[0, r, sl] for sl in sls]
            yk = [yk_ref[0, r, sl] for sl in sls]
            yv = [yv_ref[0, r, sl] for sl in sls]
            q, q_pieces = zip(*[_act_norm(y, GDN_DK ** -0.5) for y in yq])
            k, k_pieces = zip(*[_act_norm(y, 1.0) for y in yk])
            v, v_pieces = zip(*[_act_norm(y, None) for y in yv])
            do_ = [do_ref[0, r, sl] for sl in sls]
            s = [st_ref[0, ih, rs, :] for ih in heads]
            tinv = [ti_ref[0, ih, r, :] for ih in heads]
            dsn = [ds_scr[ih] for ih in heads]
            beta, dc, eg, egl, ekd = zip(*[
                _gdn_gates(gsv, gr_ref[0, n, pl.ds(ih, 1), :], ih, c) for ih in heads])
            mul = lambda a, b: a * b
            kb = _hmap(mul, k, beta)
            rhs_w = _hmap(mul, kb, eg)
            u = _hmap(lambda t_, a, b: _mm3(t_, a * b), tinv, v, beta)
            w = _hmap(_mm3, tinv, rhs_w)
            amat = _hmap(lambda a, b, d_: jnp.where(c["strict"], _mm(a, b, NT) * d_, 0.0), kb, k, dc)
            qk = _hmap(lambda a, b, d_: _mm(a, b, NT) * d_, q, k, dc)
            qd = _hmap(mul, q, eg)
            kd = _hmap(mul, k, ekd)
            v_new = _hmap(lambda a, b, s_: a - _mm(b, s_), u, w, s)
            dv_new = _hmap(lambda qk_, d_, kd_, dn: _mm(qk_, d_, TN) + _mm(kd_, dn), qk, do_, kd, dsn)
            dqk = _hmap(lambda d_, vn: _mm(d_, vn, NT), do_, v_new)
            dqd = _hmap(lambda d_, s_: _mm(d_, s_, NT), do_, s)
            ds_new = _hmap(lambda qd_, d_, dn, e, w_, dvn: _mm(qd_, d_, TN) + dn * e - _mm(w_, dvn, TN),
                           qd, do_, dsn, egl, w, dv_new)
            dkd = _hmap(lambda vn, dn: _mm(vn, dn, NT), v_new, dsn)
            dgl = _hmap(lambda s_, dn, e: _colsum(_rowsum(s_ * dn)) * e, s, dsn, egl)
            dw = _hmap(lambda dvn, s_: -_mm(dvn, s_, NT), dv_new, s)
            dru = _hmap(lambda t_, a: _mm3(t_, a, TN), tinv, dv_new)
            drw = _hmap(lambda t_, a: _mm3(t_, a, TN), tinv, dw)
            da = _hmap(lambda a, u_, b, w_: jnp.where(c["strict"], -(_mm(a, u_, NT) + _mm(b, w_, NT)), 0.0), dru, u, drw, w)
            m = _hmap(mul, da, dc)
            dkb = _hmap(lambda a, e, m_, k_: a * e + _mm(m_, k_), drw, eg, m, k)
            mq = _hmap(mul, dqk, dc)
            dq = _hmap(lambda mq_, k_, a, e: _mm(mq_, k_) + a * e, mq, k, dqd, eg)
            dk = _hmap(lambda m_, kb_, mq_, q_, a, e, b, be: _mm(m_, kb_, TN) + _mm(mq_, q_, TN) + a * e + b * be,
                       m, kb, mq, q, dkd, ekd, dkb, beta)
            dbeta = _hmap(lambda a, v_, b, k_: _rowsum(a * v_) + _rowsum(b * k_), dru, v, dkb, k)
            pq = _hmap(lambda a, am, b, qk_: a * am + b * qk_, da, amat, dqk, qk)
            ekk = _hmap(lambda a, b: _rowsum(a * b), dkd, kd)
            dgc = _hmap(lambda pq_, a, rw, b, qd_, e, gl_: (
                _rowsum(pq_) - _mmsel(pq_, c["ones"], TN) + (_rowsum(a * rw) + _rowsum(b * qd_) - e)
                + jnp.where(c["row1"] == CHUNK - 1, _colsum(e) + gl_, 0.0)), pq, drw, rhs_w, dqd, qd, ekk, dgl)
            dv = _hmap(mul, dru, beta)
            dyq = _hmap(lambda g_, y, pc: _act_norm_bwd(g_, y, GDN_DK ** -0.5, pc), dq, yq, q_pieces)
            dyk = _hmap(lambda g_, y, pc: _act_norm_bwd(g_, y, 1.0, pc), dk, yk, k_pieces)
            dyv = _hmap(lambda g_, y, pc: _act_norm_bwd(g_, y, None, pc), dv, yv, v_pieces)
            dgs = jnp.zeros((CHUNK, LANES), F32)
            for ih in heads:
                ds_scr[ih] = ds_new[ih]
                dq_ref[0, r, sls[ih]] = dyq[ih]
                dk_ref[0, r, sls[ih]] = dyk[ih]
                dv_ref[0, r, sls[ih]] = dyv[ih]
                dgs = dgs + jnp.where(c["lane"] == ih, dbeta[ih], jnp.where(c["lane"] == ih + 8, dgc[ih], 0.0))
            dgs_ref[0, r, :] = dgs
            return carry

        lax.fori_loop(0, ncb, chunk, 0)

    res = _pcall(
        wrap(body), name="gdn_chunk_bwd", grid=grid,
        in_specs=[sp["ypre"](0), sp["ypre"](1), sp["ypre"](2), sp["gs"](), sp["gr"], sp["wide"](), sp["st"], sp["ti"]]
        + any_spec,
        out_specs=[sp["wide"](), sp["wide"](), sp["wide"](), sp["gs"]()] + any_spec,
        out_shape=[jax.ShapeDtypeStruct((bsz, seq, D_MODEL), F32)] * 3 + [jax.ShapeDtypeStruct((bsz, seq, LANES), F32)]
        + rider_shapes,
        scratch_shapes=[pltpu.VMEM((hb, GDN_DK, GDN_DK), F32)] + rider_sems,
        compiler_params=_params(("arbitrary", "arbitrary", "arbitrary")),
    )(ypre, ypre, ypre, gs, gr, do, sts, tis, *riders)
    return res[:4], res[4:]


def _gates_bwd(proj, dgs, ddt, dacs, gp, sp, tm):
    t = proj.shape[0]

    def body(i, sm_ref, dgs_ref, ddt_ref, dacs_ref, gp_ref, sp_ref, dsm_ref, dgp_ref, dsp_ref):
        sm = sm_ref[...]
        lane = lax.broadcasted_iota(jnp.int32, sm.shape, 1)
        is_g = (lane >= 8) & (lane < 16)
        is_dt = (lane >= 16) & (lane < 32)
        dgs = dgs_ref[...]
        back = _mmx(_block_tri(tm, True), jnp.where(is_g, dgs, 0.0) + dacs_ref[...])
        beta = _sigmoid(sm)
        bias = gp_ref[1:2, :] + sp_ref[1:2, :]
        xb = sm + bias
        soft, dsoft = _softplus(xb), _sigmoid(xb)
        g_neg = -jnp.exp(gp_ref[0:1, :])
        a_neg = -jnp.exp(sp_ref[0:1, :])
        dg = jnp.where(is_g, back * g_neg, 0.0)
        dxb_g = dg * dsoft
        dxb_dt = jnp.where(is_dt, (ddt_ref[...] + back * a_neg) * dsoft, 0.0)
        dsm_ref[...] = (jnp.where(lane < 8, dgs * beta * (1.0 - beta), dxb_g) + dxb_dt).astype(BF16)
        dgp_ref[1:2, :] += _colsum(dxb_g)
        dgp_ref[0:1, :] += _colsum(dg * soft)
        dsp_ref[1:2, :] += _colsum(dxb_dt)
        dsp_ref[0:1, :] += jnp.where(is_dt[0:1, :], _colsum(back * soft) * a_neg, 0.0)

    ins = [("row", proj, LANES, SMALL_CB), ("row", dgs, LANES, 0), ("row", ddt, LANES, 0), ("row", dacs, LANES, 0),
           ("full", gp), ("full", sp)]
    return _rowwise("gates_bwd", body, t, tm, ins, [(LANES, BF16)], accs=[(SUBLANES, LANES), (SUBLANES, LANES)])


def _dh1_first_bwd(dproj, wp_in, x, dx1, w, scatter_riders):
    d = D_MODEL

    def epilogue(dh, x_ref, dx1_ref, w_ref, dx_ref, dw_ref):
        xh, r = _rms(x_ref[...], d)
        dw_ref[...] += _colsum(dh * xh)
        dx_ref[...] = dx1_ref[...] + _rms_bwd(xh, r, dh * w_ref[...], d)

    return _matmul_rows("mm_dh1_first_bwd", dproj, wp_in, "nt", epilogue, [x, dx1], [w], [(d, F32)], accs=[(1, d)],
                        tk=PROJ_W // 2, scatter_riders=scatter_riders)


def _gather_two_level(name, arrays):
    n = len(arrays)
    n_sem = 7

    def body(*refs):
        ins, outs = refs[:n], refs[n:2 * n]
        send_sems, recv_sems, loc_sems = refs[2 * n:]
        x, y, c = lax.axis_index("x"), lax.axis_index("y"), lax.axis_index("c")
        slot = lambda px, py, pc: 4 * px + 2 * py + pc
        sibling = (x, y, 1 - c)
        chips = [(1 - x, y), (x, 1 - y), (1 - x, 1 - y)]

        def copy(t, k, src, block, to):
            return pltpu.make_async_remote_copy(
                src_ref=src, dst_ref=outs[t].at[block], send_sem=send_sems.at[t, k], recv_sem=recv_sems.at[t, k],
                device_id=to, device_id_type=pl.DeviceIdType.MESH)

        own, first, passed = [], [], []
        for t in range(n):
            own.append(pltpu.make_async_copy(ins[t], outs[t].at[slot(x, y, c)], loc_sems.at[t]))
            first.append(copy(t, 0, ins[t], slot(x, y, c), sibling))
            first += [copy(t, 1 + j, ins[t], slot(x, y, c), (px, py, c)) for j, (px, py) in enumerate(chips)]
        for cp in own + first:
            cp.start()
        for t in range(n):
            for j, (px, py) in enumerate(chips):
                copy(t, 1 + j, ins[t], slot(px, py, c), (px, py, c)).wait_recv()
                fwd = copy(t, 4 + j, outs[t].at[slot(px, py, c)], slot(px, py, c), sibling)
                fwd.start()
                passed.append(fwd)
        for t in range(n):
            copy(t, 0, ins[t], slot(x, y, 1 - c), sibling).wait_recv()
            for j, (px, py) in enumerate(chips):
                copy(t, 4 + j, ins[t], slot(px, py, 1 - c), sibling).wait_recv()
        for cp in first + passed:
            cp.wait_send()
        for cp in own:
            cp.wait()

    return _pcall(
        body, name=name,
        in_specs=[pl.BlockSpec(memory_space=pl.ANY)] * n,
        out_specs=[pl.BlockSpec(memory_space=pl.ANY)] * n,
        out_shape=_exchange_out_shapes(arrays, False),
        scratch_shapes=[pltpu.SemaphoreType.DMA((n, n_sem)), pltpu.SemaphoreType.DMA((n, n_sem)), pltpu.SemaphoreType.DMA((n,))],
    )(*arrays)


def _exchange_out_shapes(arrays, scatter):
    return [jax.ShapeDtypeStruct(a.shape if scatter else (N_DEV,) + a.shape, a.dtype) for a in arrays]


def _exchange_sems(n):
    return [pltpu.SemaphoreType.DMA((n, N_DEV - 1)), pltpu.SemaphoreType.DMA((n, N_DEV - 1)), pltpu.SemaphoreType.DMA((n,))]


def _exchange_phase(ins, outs, sems, scatter, start):
    send_sems, recv_sems, loc_sems = sems
    x, y, c = lax.axis_index("x"), lax.axis_index("y"), lax.axis_index("c")
    me = 4 * x + 2 * y + c
    for t in range(len(ins)):
        loc = pltpu.make_async_copy(ins[t].at[me] if scatter else ins[t], outs[t].at[me], loc_sems.at[t])
        if start:
            loc.start()
        else:
            loc.wait()
        for k in range(N_DEV - 1):
            bx, by, bc = ((k + 1) >> 2) & 1, ((k + 1) >> 1) & 1, (k + 1) & 1
            px = 1 - x if bx else x
            py = 1 - y if by else y
            pc = 1 - c if bc else c
            peer = 4 * px + 2 * py + pc
            src = ins[t].at[peer] if scatter else ins[t]
            copy = lambda dst: pltpu.make_async_remote_copy(
                src_ref=src, dst_ref=dst, send_sem=send_sems.at[t, k], recv_sem=recv_sems.at[t, k],
                device_id=(px, py, pc), device_id_type=pl.DeviceIdType.MESH)
            if start:
                copy(outs[t].at[me]).start()
            else:
                copy(outs[t].at[me]).wait_send()
                copy(outs[t].at[peer]).wait_recv()


def _adam_math(w, g, m, v):
    m = ADAM_B1 * m + (1.0 - ADAM_B1) * g
    v = ADAM_B2 * v + (1.0 - ADAM_B2) * (g * g)
    m_hat = m / (1.0 - ADAM_B1 ** ADAM_STEP)
    v_hat = v / (1.0 - ADAM_B2 ** ADAM_STEP)
    delta = -ADAM_LR * (m_hat / (jnp.sqrt(v_hat) + ADAM_EPS) + ADAM_WD * w)
    return delta, m, v


def _adam_big(name, parts, w, m, v, tm):
    r, c = w.shape
    tm = tm if r % tm == 0 else r

    def body(p_ref, w_ref, m_ref, v_ref, g_ref, d_ref, nm_ref, nv_ref):
        g = p_ref[0].astype(F32)
        for s in range(1, N_DEV):
            g = g + p_ref[s].astype(F32)
        g_ref[...] = g
        d_ref[...], nm_ref[...], nv_ref[...] = _adam_math(w_ref[...], g, m_ref[...], v_ref[...])

    blk = lambda: pl.BlockSpec((tm, c), lambda i: (i, 0))
    return _pcall(
        body, name=name, grid=(r // tm,),
        in_specs=[pl.BlockSpec((N_DEV, tm, c), lambda i: (0, i, 0)), blk(), blk(), blk()],
        out_specs=[blk(), blk(), blk(), blk()],
        out_shape=[jax.ShapeDtypeStruct((r, c), F32)] * 4,
        compiler_params=_params(("parallel",)),
    )(parts, w, m, v)


SMALL_ROWS = 56
ROW_DD, ROW_LOSS = 5, 6


def _small_sum(gathered):
    def body(g_ref, o_ref, x_ref):
        s = g_ref[0]
        for dev in range(1, N_DEV):
            s = s + g_ref[dev]
        o_ref[...] = s
        ri = lax.broadcasted_iota(jnp.int32, (D_MODEL, LANES), 0)
        ro = lax.broadcasted_iota(jnp.int32, (D_MODEL, LANES), 1)
        heads = _mmx(jnp.broadcast_to(s[ROW_DD:ROW_DD + 1, :], (SUBLANES, D_MODEL)), (ri // SSD_P == ro).astype(F32))
        loss = _rowsum(jnp.broadcast_to(s[ROW_LOSS:ROW_LOSS + 1, :], (SUBLANES, D_MODEL)))
        row = lax.broadcasted_iota(jnp.int32, (SUBLANES, LANES), 0)
        x_ref[...] = jnp.where(row == 0, heads, jnp.broadcast_to(loss, (SUBLANES, LANES)))

    return _pcall(
        body, name="small_sum",
        out_shape=[jax.ShapeDtypeStruct((SMALL_ROWS, D_MODEL), F32), jax.ShapeDtypeStruct((SUBLANES, LANES), F32)],
        compiler_params=_params(None),
    )(gathered)


def _adam_small(g, w, m, v):
    def body(g_ref, w_ref, m_ref, v_ref, d_ref, nm_ref, nv_ref):
        d_ref[...], nm_ref[...], nv_ref[...] = _adam_math(w_ref[...], g_ref[...], m_ref[...], v_ref[...])

    return _pcall(body, name="adam_small", out_shape=[jax.ShapeDtypeStruct(g.shape, F32)] * 3,
                  compiler_params=_params(None))(g, w, m, v)


def _pack(pieces, rows):
    flat = jnp.concatenate([p.reshape(-1).astype(F32) for p in pieces])
    return jnp.pad(flat, (0, rows * D_MODEL - flat.shape[0])).reshape(rows, D_MODEL)


def _unpack(packed, shapes):
    flat = packed.reshape(-1)
    out, off = [], 0
    for shp in shapes:
        size = 1
        for s in shp:
            size *= s
        out.append(flat[off:off + size].reshape(shp))
        off += size
    return out


def _permute_in(w):
    pad = jnp.zeros((w.shape[0], PROJ_W - D_IN), w.dtype)
    return jnp.concatenate([w[:, 0:4096], w[:, 4112:6672], w[:, 4096:4112], w[:, 6672:6688], pad], axis=1)


def _unpermute_in(g):
    return jnp.concatenate([g[:, 0:4096], g[:, 6656:6672], g[:, 4096:6656], g[:, 6672:6688]], axis=1)


def _lane_row(vec, start):
    return jnp.zeros((LANES,), F32).at[start:start + vec.shape[0]].set(vec)


def _cols_from_shards(g):
    return jnp.transpose(g, (1, 0, 2)).reshape(g.shape[1], N_DEV * g.shape[2])


def _cols_to_shards(a):
    return jnp.transpose(a.astype(BF16).reshape(a.shape[0], N_DEV, a.shape[1] // N_DEV), (1, 0, 2))


def _rows_to_shards(a):
    return a.astype(BF16).reshape(N_DEV, a.shape[0] // N_DEV, a.shape[1])


def _local_step(x, tgt, wp_in, rest, p, rest_is_sharded):
    bsz, seq, d = x.shape
    t = bsz * seq
    x2 = x.reshape(t, d)
    tgt2 = tgt.reshape(t, d)
    tm = min(256, seq)
    tm_big = min(512, seq)
    tm_wide = min(256, seq)
    sb = min(512, seq)

    gp = jnp.zeros((SUBLANES, LANES), F32).at[0].set(_lane_row(p["gdn_a_log"], 8)).at[1].set(_lane_row(p["gdn_dt_bias"], 8))
    sp = jnp.zeros((SUBLANES, LANES), F32).at[0].set(_lane_row(p["ssd_a_log"], 16)).at[1].set(_lane_row(p["ssd_dt_bias"], 16))
    dvec = jnp.repeat(p["ssd_d"], SSD_P).reshape(1, d)
    row = lambda v: v.reshape(1, -1)
    pre_mix, post_mix, pre_ffn, post_ffn = (row(p[k]) for k in ("pre_mix_norm", "post_mix_norm", "pre_ffn_norm", "post_ffn_norm"))
    gnw, snw = row(p["gdn_norm_w"]), row(p["ssd_norm_w"])
    gcw, scw, scb, fcw, fcb = p["gdn_conv_w"], p["ssd_conv_w"], row(p["ssd_conv_b"]), p["ffn_conv_w"], row(p["ffn_conv_b"])

    h1, proj = _norm_proj(x2, pre_mix, wp_in)
    b3 = lambda a: a.reshape(bsz, seq, a.shape[-1])
    b2 = lambda a: a.reshape(t, a.shape[-1])
    rows_of = lambda a, lo, n: jnp.transpose(a[:, lo:lo + n].reshape(bsz, seq // CHUNK, CHUNK, n), (0, 1, 3, 2))
    gs, ypre_gdn = _gdn_prep(proj, gcw, gp, seq, tm_big)
    gr = rows_of(gs, 8, GDN_HEADS)
    gs, ypre_gdn = b3(gs), b3(ypre_gdn)
    (o_gdn, gdn_st, gdn_ti), gathered = _gdn_chunk_fwd(ypre_gdn, gs, gr, bsz, seq, sb, list(rest) if rest_is_sharded else [])
    if rest_is_sharded:
        w_out, w_up, w_down = gathered[0].reshape(-1, d), _cols_from_shards(gathered[1]), gathered[2].reshape(-1, d)
    else:
        w_out, w_up, w_down = rest
    o_gdn = b2(o_gdn)
    xs, bc, dtx, acsx, acs, ypre_ssd = _ssd_prep(proj, scw, scb, sp, seq, tm)
    ar = rows_of(acs, 16, SSD_HEADS)
    y_ssd, ssd_st = _ssd_chunk_fwd(b3(xs), b3(bc), b3(dtx), b3(acsx), b3(acs), ar, bsz, seq, sb)
    y_ssd = b2(y_ssd)
    mixin = _gate_norm(o_gdn, y_ssd, xs, proj, gnw, snw, dvec, tm_big)
    mix, x1, h2 = _out_mid(mixin, w_out, x2, post_mix, pre_ffn)
    u_pre = _matmul("mm_up", h2, w_up, "nn", F32)
    act, u = _ffn_act(u_pre, fcw, fcb, seq, tm_wide)
    dy, df, loss_lanes, d_post_ffn = _down_final(act, w_down, x1, tgt2, post_ffn)

    g_down = _matmul("mm_dw_down", act, df, "tn", BF16, tm=1408, tk=2048)
    dact = _matmul("mm_dact", df, w_down, "nt", F32, tn=1408)
    du_pre, d_fcw, d_fcb = _ffn_bwd(u, u_pre, dact, fcw, seq, tm_wide)
    g_up = _matmul("mm_dw_up", h2, du_pre, "tn", BF16, tk=2048)
    dx1, dmix, d_post_mix, d_pre_ffn = _dh2_mid_bwd(du_pre, w_up, x1, mix, dy, post_mix, pre_ffn)
    g_out = _matmul("mm_dw_out", mixin, dmix, "tn", BF16, tk=2048)
    do_gdn, dza, dy_ssd, dxs_d, dzs, d_gnw, d_snw, d_dd = _dmixin_gate_norm_bwd(dmix, w_out, o_gdn, y_ssd, xs, proj, gnw, snw, dvec)
    dyx, dybc, ddt, dacs, d_scb_x, d_scb_bc = _ssd_chunk_bwd(
        b3(xs), b3(bc), b3(dtx), b3(acsx), b3(acs), ar, b3(dy_ssd), ssd_st, b3(ypre_ssd), b3(dxs_d), bsz, seq, sb)
    dyx, dybc, ddt, dacs = b2(dyx), b2(dybc), b2(ddt), b2(dacs)
    d_scb = jnp.concatenate([d_scb_x, d_scb_bc], axis=1)
    riders = [_rows_to_shards(g_out), _cols_to_shards(g_up), _rows_to_shards(g_down)] if rest_is_sharded else []
    dgdn, received = _gdn_chunk_bwd(ypre_gdn, gs, gr, b3(do_gdn), gdn_st, gdn_ti, bsz, seq, min(256, seq), riders)
    if rest_is_sharded:
        g_out, g_up, g_down = received
    dyq, dyk, dyv, dgs = (b2(a) for a in dgdn)
    dsm, d_gp, d_sp = _gates_bwd(proj, dgs, ddt, dacs, gp, sp, tm)
    dproj, d_gcw, d_scw = _assemble_dproj((dyq, dyk, dyv), dza, dzs, (dyx, dybc), dsm, proj, gcw, scw, seq, tm)
    g_in = _matmul("mm_dw_in", h1, dproj, "tn", BF16, tk=2048)
    if rest_is_sharded:
        (dx, d_pre_mix), (g_in,) = _dh1_first_bwd(dproj, wp_in, x2, dx1, pre_mix, [_cols_to_shards(_unpermute_in(g_in))])
    else:
        dx, d_pre_mix = _dh1_first_bwd(dproj, wp_in, x2, dx1, pre_mix, [])

    small = dict(pre_mix_norm=d_pre_mix, ssd_norm_w=d_snw, post_mix_norm=d_post_mix, pre_ffn_norm=d_pre_ffn,
                 post_ffn_norm=d_post_ffn, dd_lanes=d_dd, loss_lanes=loss_lanes, gdn_gates=d_gp, ssd_gates=d_sp,
                 gdn_norm_w=d_gnw, gdn_conv_w=d_gcw[0:4], ssd_conv_w=d_scw[0:4], ssd_conv_b=d_scb,
                 ffn_conv_w=d_fcw[0:3], ffn_conv_b=d_fcb)
    return dx.reshape(bsz, seq, d), g_in, g_out, g_up, g_down, small


def kernel(x, pre_mix_norm, w_in, gdn_conv_w, gdn_a_log, gdn_dt_bias, gdn_norm_w, ssd_conv_w, ssd_conv_b, ssd_a_log, ssd_dt_bias, ssd_d, ssd_norm_w, w_out, post_mix_norm, pre_ffn_norm, w_up, ffn_conv_w, ffn_conv_b, w_down, post_ffn_norm, loss_target, m_pre_mix_norm, m_w_in, m_gdn_conv_w, m_gdn_a_log, m_gdn_dt_bias, m_gdn_norm_w, m_ssd_conv_w, m_ssd_conv_b, m_ssd_a_log, m_ssd_dt_bias, m_ssd_d, m_ssd_norm_w, m_w_out, m_post_mix_norm, m_pre_ffn_norm, m_w_up, m_ffn_conv_w, m_ffn_conv_b, m_w_down, m_post_ffn_norm, v_pre_mix_norm, v_w_in, v_gdn_conv_w, v_gdn_a_log, v_gdn_dt_bias, v_gdn_norm_w, v_ssd_conv_w, v_ssd_conv_b, v_ssd_a_log, v_ssd_dt_bias, v_ssd_d, v_ssd_norm_w, v_w_out, v_post_mix_norm, v_pre_ffn_norm, v_w_up, v_ffn_conv_w, v_ffn_conv_b, v_w_down, v_post_ffn_norm):
    names = ["pre_mix_norm", "w_in", "gdn_conv_w", "gdn_a_log", "gdn_dt_bias", "gdn_norm_w", "ssd_conv_w", "ssd_conv_b",
             "ssd_a_log", "ssd_dt_bias", "ssd_d", "ssd_norm_w", "w_out", "post_mix_norm", "pre_ffn_norm", "w_up",
             "ffn_conv_w", "ffn_conv_b", "w_down", "post_ffn_norm"]
    w_args = [pre_mix_norm, w_in, gdn_conv_w, gdn_a_log, gdn_dt_bias, gdn_norm_w, ssd_conv_w, ssd_conv_b, ssd_a_log, ssd_dt_bias, ssd_d, ssd_norm_w, w_out, post_mix_norm, pre_ffn_norm, w_up, ffn_conv_w, ffn_conv_b, w_down, post_ffn_norm]
    m_args = [m_pre_mix_norm, m_w_in, m_gdn_conv_w, m_gdn_a_log, m_gdn_dt_bias, m_gdn_norm_w, m_ssd_conv_w, m_ssd_conv_b, m_ssd_a_log, m_ssd_dt_bias, m_ssd_d, m_ssd_norm_w, m_w_out, m_post_mix_norm, m_pre_ffn_norm, m_w_up, m_ffn_conv_w, m_ffn_conv_b, m_w_down, m_post_ffn_norm]
    v_args = [v_pre_mix_norm, v_w_in, v_gdn_conv_w, v_gdn_a_log, v_gdn_dt_bias, v_gdn_norm_w, v_ssd_conv_w, v_ssd_conv_b, v_ssd_a_log, v_ssd_dt_bias, v_ssd_d, v_ssd_norm_w, v_w_out, v_post_mix_norm, v_pre_ffn_norm, v_w_up, v_ffn_conv_w, v_ffn_conv_b, v_w_down, v_post_ffn_norm]
    w = {k: a[0] for k, a in zip(names, w_args)}
    m = {k: a[0] for k, a in zip(names, m_args)}
    v = {k: a[0] for k, a in zip(names, v_args)}
    idx = 4 * lax.axis_index("x") + 2 * lax.axis_index("y") + lax.axis_index("c")
    big = ("w_in", "w_out", "w_up", "w_down")
    conv = ("gdn_conv_w", "ssd_conv_w", "ffn_conv_w")

    conv_local = jnp.concatenate([jnp.pad(w[k], ((0, 4 - w[k].shape[0]), (0, 0))) for k in conv], axis=1)
    g_in, g_conv = _gather_two_level("gather_weights", [w["w_in"].astype(BF16), conv_local])
    wp_in = _permute_in(_cols_from_shards(g_in))
    p = {k: w[k] for k in names if k not in big and k not in conv}
    off = 0
    for k in conv:
        cw = w[k].shape[1]
        p[k] = jnp.transpose(g_conv[:, :w[k].shape[0], off:off + cw], (1, 0, 2)).reshape(w[k].shape[0], N_DEV * cw)
        off += cw

    rest = tuple(w[k].astype(BF16) for k in ("w_out", "w_up", "w_down"))
    dx, p_in, p_out, p_up, p_down, small = _local_step(x, loss_target, wp_in, rest, p, True)

    gate_row = jnp.concatenate([small["gdn_gates"][0], small["gdn_gates"][1], small["ssd_gates"][0], small["ssd_gates"][1],
                                small["gdn_norm_w"][0], jnp.zeros((D_MODEL - 5 * LANES,), F32)]).reshape(1, D_MODEL)
    pack = _pack([small["pre_mix_norm"], small["ssd_norm_w"], small["post_mix_norm"], small["pre_ffn_norm"],
                  small["post_ffn_norm"], small["dd_lanes"], small["loss_lanes"], gate_row,
                  small["gdn_conv_w"], small["ssd_conv_w"], jnp.pad(small["ssd_conv_b"], ((0, 0), (0, 512))),
                  jnp.pad(small["ffn_conv_w"].reshape(-1), (0, 17 * D_MODEL - 3 * 2 * D_FF)),
                  jnp.pad(small["ffn_conv_b"], ((0, 0), (0, 512)))], SMALL_ROWS)
    (pack_all,) = _gather_two_level("gather_small", [pack])
    ssum, extra = _small_sum(pack_all)

    grads, deltas, new_m, new_v = {}, {}, {}, {}
    for k, parts in (("w_in", p_in), ("w_out", p_out), ("w_up", p_up), ("w_down", p_down)):
        grads[k], deltas[k], new_m[k], new_v[k] = _adam_big("adam_" + k, parts, w[k], m[k], v[k], 256)

    flat = ssum.reshape(-1)
    gate = ssum[7]
    sg = dict(pre_mix_norm=ssum[0], ssd_norm_w=ssum[1], post_mix_norm=ssum[2], pre_ffn_norm=ssum[3], post_ffn_norm=ssum[4],
              gdn_a_log=gate[8:16], gdn_dt_bias=gate[LANES + 8:LANES + 16], ssd_a_log=gate[2 * LANES + 16:2 * LANES + 32],
              ssd_dt_bias=gate[3 * LANES + 16:3 * LANES + 32], gdn_norm_w=gate[4 * LANES:5 * LANES], ssd_d=extra[0, 0:SSD_HEADS])
    o = 8 * D_MODEL
    full_gcw = flat[o:o + 4 * 3072].reshape(4, 3072)
    o += 12 * D_MODEL
    full_scw = flat[o:o + 4 * 1536].reshape(4, 1536)
    o += 6 * D_MODEL
    sg["ssd_conv_b"] = flat[o:o + 1536]
    o += 2 * D_MODEL
    full_fcw = flat[o:o + 3 * 2 * D_FF].reshape(3, 2 * D_FF)
    o += 17 * D_MODEL
    sg["ffn_conv_b"] = flat[o:o + 2 * D_FF]
    for k, full in (("gdn_conv_w", full_gcw), ("ssd_conv_w", full_scw), ("ffn_conv_w", full_fcw)):
        cw = w[k].shape[1]
        sg[k] = lax.dynamic_slice_in_dim(full, idx * cw, cw, axis=1)
    small_names = [k for k in names if k not in big]
    rows = 24
    gpk = _pack([sg[k] for k in small_names], rows)
    dpk, mpk, vpk = _adam_small(gpk, _pack([w[k] for k in small_names], rows), _pack([m[k] for k in small_names], rows),
                                _pack([v[k] for k in small_names], rows))
    shapes = [w[k].shape for k in small_names]
    for k, g_, d_, m_, v_ in zip(small_names, _unpack(gpk, shapes), _unpack(dpk, shapes), _unpack(mpk, shapes), _unpack(vpk, shapes)):
        grads[k], deltas[k], new_m[k], new_v[k] = g_, d_, m_, v_

    loss = extra[1, 0]
    lead = lambda a: a[None]
    return (loss, dx, *[lead(grads[k]) for k in names], *[lead(deltas[k]) for k in names],
            *[lead(new_m[k]) for k in names], *[lead(new_v[k]) for k in names])
```

```python
import functools

import jax
import jax.numpy as jnp
from jax import lax
from jax.experimental import pallas as pl
from jax.experimental.pallas import tpu as pltpu

F32 = jnp.float32
BF16 = jnp.bfloat16
MXU_DTYPE = jnp.bfloat16
HIGHEST = lax.Precision.HIGHEST
VMEM_LIMIT_V7X = 48 * 1024 * 1024
SUBLANES = 8
LANES = 128

D_MODEL = 1024
GDN_HEADS = 8
GDN_DK = 128
SSD_HEADS = 16
SSD_P = 64
SSD_GROUPS = 2
SSD_HPG = 8
SSD_N = 128
CHUNK = 128
D_FF = 2816
EPS = 1e-6
N_DEV = 8
PROJ_W = 7168
SMALL_CB = 52
D_IN = 6688

ADAM_LR = 0.001
ADAM_B1 = 0.9
ADAM_B2 = 0.999
ADAM_EPS = 1e-08
ADAM_WD = 0.01
ADAM_STEP = 10

NN = (((1,), (0,)), ((), ()))
NT = (((1,), (1,)), ((), ()))
TN = (((0,), (0,)), ((), ()))


def _pcall(body, **kw):
    return pl.pallas_call(body, **kw)


def _mm(a, b, dims=NN):
    return lax.dot_general(a.astype(MXU_DTYPE), b.astype(MXU_DTYPE), dims, preferred_element_type=F32)


def _mmx(a, b, dims=NN):
    return lax.dot_general(a, b, dims, precision=HIGHEST, preferred_element_type=F32)


def _split(a):
    hi = a.astype(MXU_DTYPE)
    return hi, (a - hi.astype(F32)).astype(MXU_DTYPE)


def _mm3(a, b, dims=NN):
    (ah, al), (bh, bl) = _split(a), _split(b)
    dot = lambda p, q: lax.dot_general(p, q, dims, preferred_element_type=F32)
    return dot(ah, bh) + (dot(ah, bl) + dot(al, bh))


def _mmsel(a, sel, dims=NN, terms=2):
    s = sel.astype(MXU_DTYPE)
    out = None
    for _ in range(terms):
        part = a.astype(MXU_DTYPE)
        a = a - part.astype(F32)
        prod = lax.dot_general(part, s, dims, preferred_element_type=F32)
        out = prod if out is None else out + prod
    return out


def _sigmoid(x):
    return 0.5 * jnp.tanh(0.5 * x) + 0.5


def _softplus(x):
    return jnp.maximum(x, 0.0) + jnp.log(1.0 + jnp.exp(-jnp.abs(x)))


def _dsilu(x, s):
    return s * (1.0 + x * (1.0 - s))


def _rowsum(x):
    return jnp.sum(x, axis=1, keepdims=True)


def _colsum(x):
    return jnp.sum(x, axis=0, keepdims=True)


def _pick(dim, pref):
    if dim <= pref:
        return dim
    best = None
    t = LANES
    while t <= pref:
        if dim % t == 0:
            best = t
        t += LANES
    return dim if best is None else best


def _params(sem):
    return pltpu.CompilerParams(dimension_semantics=sem, vmem_limit_bytes=VMEM_LIMIT_V7X)


def _matmul(name, a, b, mode, out_dtype, tm=1024, tn=1024, tk=1024):
    if mode == "nn":
        (m, k), (_, n) = a.shape, b.shape
    elif mode == "nt":
        (m, k), (n, _) = a.shape, b.shape
    else:
        (k, m), (_, n) = a.shape, b.shape
    tm, tn, tk = _pick(m, tm), _pick(n, tn), _pick(k, tk)
    nk = k // tk
    if mode == "tn":
        a_spec = pl.BlockSpec((tk, tm), lambda i, j, kk: (kk, i))
    else:
        a_spec = pl.BlockSpec((tm, tk), lambda i, j, kk: (i, kk))
    if mode == "nt":
        b_spec = pl.BlockSpec((tn, tk), lambda i, j, kk: (j, kk))
    else:
        b_spec = pl.BlockSpec((tk, tn), lambda i, j, kk: (kk, j))
    dims = {"nn": NN, "nt": NT, "tn": TN}[mode]

    def body(a_ref, b_ref, o_ref, *acc):
        if nk == 1:
            o_ref[...] = _mm(a_ref[...], b_ref[...], dims).astype(out_dtype)
            return
        kk = pl.program_id(2)

        @pl.when(kk == 0)
        def _():
            acc[0][...] = jnp.zeros_like(acc[0])

        acc[0][...] += _mm(a_ref[...], b_ref[...], dims)

        @pl.when(kk == nk - 1)
        def _():
            o_ref[...] = acc[0][...].astype(out_dtype)

    return _pcall(
        body, name=name, grid=(m // tm, n // tn, nk),
        in_specs=[a_spec, b_spec],
        out_specs=pl.BlockSpec((tm, tn), lambda i, j, kk: (i, j)),
        out_shape=jax.ShapeDtypeStruct((m, n), out_dtype),
        scratch_shapes=[pltpu.VMEM((tm, tn), F32)] if nk > 1 else [],
        compiler_params=_params(("parallel", "parallel", "arbitrary")),
    )(a, b)


def _matmul_rows(name, a, b, mode, epilogue, row_ins, full_ins, outs, accs=(), tm=512, tk=1024, scatter_riders=()):
    if mode == "nn":
        (m, k), (_, n) = a.shape, b.shape
    else:
        (m, k), (n, _) = a.shape, b.shape
    tm, tk = _pick(m, tm), _pick(k, tk)
    nk = k // tk
    a_spec = pl.BlockSpec((tm, tk), lambda i, kk: (i, kk))
    b_spec = pl.BlockSpec((n, tk), lambda i, kk: (0, kk)) if mode == "nt" else pl.BlockSpec((tk, n), lambda i, kk: (kk, 0))
    dims = NT if mode == "nt" else NN
    n_row, n_full, n_out, n_acc = len(row_ins), len(full_ins), len(outs), len(accs)

    def body(a_ref, b_ref, *rest):
        ins = rest[:n_row + n_full]
        out_refs = rest[n_row + n_full:n_row + n_full + n_out]
        acc_refs = rest[n_row + n_full + n_out:n_row + n_full + n_out + n_acc]
        prod_scr = rest[-1]
        i, kk = pl.program_id(0), pl.program_id(1)

        if n_acc:
            @pl.when((i == 0) & (kk == 0))
            def _():
                for r in acc_refs:
                    r[...] = jnp.zeros_like(r)

        if nk == 1:
            epilogue(_mm(a_ref[...], b_ref[...], dims), *ins, *out_refs, *acc_refs)
            return

        @pl.when(kk == 0)
        def _():
            prod_scr[...] = jnp.zeros_like(prod_scr)

        prod_scr[...] += _mm(a_ref[...], b_ref[...], dims)

        @pl.when(kk == nk - 1)
        def _():
            epilogue(prod_scr[...], *ins, *out_refs, *acc_refs)

    grid = (m // tm, nk)
    riders = list(scatter_riders)
    n_in = 2 + n_row + n_full
    any_spec, rider_shapes, rider_sems, wrap = _riding_exchange(riders, True, n_in, n_out + n_acc, grid)
    row_ins = [r if isinstance(r, tuple) else (r, r.shape[1], 0) for r in row_ins]
    in_specs = [a_spec, b_spec] + [pl.BlockSpec((tm, w), lambda i, kk, cb=cb: (i, cb)) for _, w, cb in row_ins]
    in_specs += [pl.BlockSpec(f.shape, lambda i, kk, nd=f.ndim: (0,) * nd) for f in full_ins]
    row_ins = [r for r, _, _ in row_ins]
    out_specs = [pl.BlockSpec((tm, w), lambda i, kk: (i, 0)) for w, _ in outs]
    out_specs += [pl.BlockSpec(s, lambda i, kk: (0, 0)) for s in accs]
    out_shape = [jax.ShapeDtypeStruct((m, w), dt) for w, dt in outs] + [jax.ShapeDtypeStruct(s, F32) for s in accs]
    res = _pcall(
        wrap(body), name=name, grid=grid,
        in_specs=in_specs + any_spec, out_specs=out_specs + any_spec, out_shape=out_shape + rider_shapes,
        scratch_shapes=[pltpu.VMEM((tm, n), F32)] + rider_sems,
        compiler_params=_params(("arbitrary", "arbitrary")),
    )(a, b, *row_ins, *full_ins, *riders)
    return (res[:n_out + n_acc], res[n_out + n_acc:]) if riders else res


def _rowwise(name, body, n_rows, tm, ins, outs, accs=()):
    arrays, in_specs = [], []
    last8 = n_rows // SUBLANES - 1
    per = tm // SUBLANES
    for spec in ins:
        kind, arr = spec[0], spec[1]
        if kind == "full":
            in_specs.append(pl.BlockSpec(arr.shape, lambda i, nd=arr.ndim: (0,) * nd))
        else:
            w, cb = spec[2], spec[3]
            if kind == "row":
                in_specs.append(pl.BlockSpec((tm, w), lambda i, cb=cb: (i, cb)))
            elif kind == "prev":
                in_specs.append(pl.BlockSpec((SUBLANES, w), lambda i, cb=cb: (jnp.maximum(i * per - 1, 0), cb)))
            else:
                in_specs.append(pl.BlockSpec((SUBLANES, w), lambda i, cb=cb: (jnp.minimum((i + 1) * per, last8), cb)))
        arrays.append(arr)
    out_shape = [jax.ShapeDtypeStruct((n_rows, w), dt) for (w, dt) in outs]
    out_shape += [jax.ShapeDtypeStruct(s, F32) for s in accs]
    out_specs = [pl.BlockSpec((tm, w), lambda i: (i, 0)) for (w, _) in outs]
    out_specs += [pl.BlockSpec(s, lambda i: (0, 0)) for s in accs]
    n_io = len(ins) + len(outs)

    def kern(*refs):
        i = pl.program_id(0)
        if accs:
            @pl.when(i == 0)
            def _():
                for r in refs[n_io:]:
                    r[...] = jnp.zeros_like(r)
        body(i, *refs)

    res = _pcall(
        kern, name=name, grid=(n_rows // tm,), in_specs=in_specs, out_specs=out_specs, out_shape=out_shape,
        compiler_params=_params(("arbitrary",)),
    )(*arrays)
    return res


def _shift_down(x, halo, j):
    r = pltpu.roll(x, j, 0)
    hr = pltpu.roll(halo, j, 0)
    rows = lax.broadcasted_iota(jnp.int32, (SUBLANES, x.shape[1]), 0)
    top = jnp.where(rows < j, hr, r[0:SUBLANES])
    return jnp.concatenate([top, r[SUBLANES:]], axis=0)


def _shift_up(x, halo, j):
    tm = x.shape[0]
    r = pltpu.roll(x, tm - j, 0)
    hr = pltpu.roll(halo, SUBLANES - j, 0)
    rows = lax.broadcasted_iota(jnp.int32, (SUBLANES, x.shape[1]), 0)
    bot = jnp.where(rows >= SUBLANES - j, hr, r[tm - SUBLANES:])
    return jnp.concatenate([r[:tm - SUBLANES], bot], axis=0)


def _conv_taps(x, halo, kw):
    return [x if kw - 1 - k == 0 else _shift_down(x, halo, kw - 1 - k) for k in range(kw)]


def _conv(taps, w):
    y = taps[0] * w[0:1]
    for k in range(1, len(taps)):
        y = y + taps[k] * w[k:k + 1]
    return y


def _rms(x, width):
    r = lax.rsqrt(jnp.sum(x * x, axis=-1, keepdims=True) * (1.0 / width) + EPS)
    return x * r, r


def _rms_bwd(xh, r, dxh, width):
    return r * (dxh - xh * (jnp.sum(dxh * xh, axis=-1, keepdims=True) * (1.0 / width)))


def _seq_flags(i, seq, tm):
    nps = seq // tm
    pos = i % nps
    return jnp.where(pos == 0, 0.0, 1.0), jnp.where(pos == nps - 1, 0.0, 1.0)


def _norm_proj(x, w, wp, tm=1024, tn=1024):
    t, d = x.shape
    n = wp.shape[1]
    tm, tn = _pick(t, tm), _pick(n, tn)

    def body(x_ref, w_ref, b_ref, h_ref, o_ref, h_scr):
        @pl.when(pl.program_id(1) == 0)
        def _():
            xh, _ = _rms(x_ref[...], d)
            h = (xh * w_ref[...]).astype(BF16)
            h_scr[...] = h
            h_ref[...] = h

        o_ref[...] = _mm(h_scr[...], b_ref[...])

    return _pcall(
        body, name="mm_norm_proj", grid=(t // tm, n // tn),
        in_specs=[pl.BlockSpec((tm, d), lambda i, j: (i, 0)), pl.BlockSpec((1, d), lambda i, j: (0, 0)),
                  pl.BlockSpec((d, tn), lambda i, j: (0, j))],
        out_specs=[pl.BlockSpec((tm, d), lambda i, j: (i, 0)), pl.BlockSpec((tm, tn), lambda i, j: (i, j))],
        out_shape=[jax.ShapeDtypeStruct((t, d), BF16), jax.ShapeDtypeStruct((t, n), F32)],
        scratch_shapes=[pltpu.VMEM((tm, d), BF16)],
        compiler_params=_params(("parallel", "arbitrary")),
    )(x, w, wp)


def _gdn_prep(proj, cw, gp, seq, tm):
    t = proj.shape[0]
    d = D_MODEL

    def body(i, q_ref, qh_ref, k_ref, kh_ref, v_ref, vh_ref, sm_ref, cw_ref, gp_ref, qn_ref, kn_ref, vv_ref, gs_ref, ypre_ref):
        keep, _ = _seq_flags(i, seq, tm)
        for x_ref, h_ref, o_ref, off, scale in ((q_ref, qh_ref, qn_ref, 0, GDN_DK ** -0.5),
                                               (k_ref, kh_ref, kn_ref, d, 1.0), (v_ref, vh_ref, vv_ref, 2 * d, None)):
            y = _conv(_conv_taps(x_ref[...], h_ref[...] * keep, 4), cw_ref[:, off:off + d])
            ypre_ref[:, off:off + d] = y
            a = y * _sigmoid(y)
            if scale is None:
                o_ref[...] = a
            else:
                for hh in range(GDN_HEADS):
                    s = a[:, hh * GDN_DK:(hh + 1) * GDN_DK]
                    n = lax.rsqrt(_rowsum(s * s) + EPS)
                    o_ref[:, hh * GDN_DK:(hh + 1) * GDN_DK] = s * (n * scale)
        sm = sm_ref[...]
        lane = lax.broadcasted_iota(jnp.int32, sm.shape, 1)
        beta = _sigmoid(sm)
        g = jnp.where((lane >= 8) & (lane < 16), -jnp.exp(gp_ref[0:1, :]) * _softplus(sm + gp_ref[1:2, :]), 0.0)
        gs_ref[...] = jnp.where(lane < 8, beta, _mmx(_block_tri(tm, False), g))

    ins = []
    for cb in range(3):
        ins += [("row", proj, d, cb), ("prev", proj, d, cb)]
    ins += [("row", proj, LANES, SMALL_CB), ("full", cw), ("full", gp)]
    return _rowwise("gdn_prep", body, t, tm, ins, [(d, F32), (d, F32), (d, F32), (LANES, F32), (3 * d, F32)])


def _block_tri(tm, upper):
    ri = lax.broadcasted_iota(jnp.int32, (tm, tm), 0)
    ci = lax.broadcasted_iota(jnp.int32, (tm, tm), 1)
    tri = (ri <= ci) if upper else (ri >= ci)
    return (tri & ((ri // CHUNK) == (ci // CHUNK))).astype(F32)


def _chunk_consts():
    row = lax.broadcasted_iota(jnp.int32, (CHUNK, CHUNK), 0)
    col = lax.broadcasted_iota(jnp.int32, (CHUNK, CHUNK), 1)
    return dict(
        tril=row >= col, strict=row > col, eye=(row == col).astype(F32),
        lane=lax.broadcasted_iota(jnp.int32, (CHUNK, LANES), 1),
        row1=lax.broadcasted_iota(jnp.int32, (CHUNK, 1), 0),
        ones=jnp.ones((CHUNK, LANES), F32))


def _hmap(fn, *lists):
    return [fn(*a) for a in zip(*lists)]


def _tri_inv(nmats, eye):
    levels = CHUNK.bit_length() - 2
    x = [eye - n for n in nmats]
    p = _hmap(_mm3, nmats, nmats)
    for lvl in range(levels):
        x = _hmap(lambda xi, pi: xi + _mm3(xi, pi), x, p)
        if lvl < levels - 1:
            p = _hmap(_mm3, p, p)
    return x


def _gdn_gates(gs, gc_row, h, c):
    beta = _rowsum(jnp.where(c["lane"] == h, gs, 0.0))
    gc = _rowsum(jnp.where(c["lane"] == h + 8, gs, 0.0))
    dc = jnp.exp(jnp.where(c["tril"], gc - gc_row, -1e30))
    gl = gc[CHUNK - 1:CHUNK, :]
    return beta, dc, jnp.exp(gc), jnp.exp(gl), jnp.exp(gl - gc)


GDN_HB = GDN_HEADS


def _gdn_specs(seq, sb, hb, backward):
    assert hb == GDN_HEADS
    nsb = seq // sb
    ncb = sb // CHUNK
    order = (lambda j: nsb - 1 - j) if backward else (lambda j: j)
    specs = dict(
        wide=lambda: pl.BlockSpec((1, sb, hb * GDN_DK), lambda b, h, j: (b, order(j), h)),
        gs=lambda: pl.BlockSpec((1, sb, LANES), lambda b, h, j: (b, order(j), 0)),
        gr=pl.BlockSpec((1, ncb, GDN_HEADS, CHUNK), lambda b, h, j: (b, order(j), 0, 0)),
        st=pl.BlockSpec((1, hb, ncb * GDN_DK, GDN_DK), lambda b, h, j: (b, h, order(j), 0)),
        ti=pl.BlockSpec((1, hb, sb, CHUNK), lambda b, h, j: (b, h, order(j), 0)))
    return nsb, ncb, specs


def _riding_exchange(arrays, scatter, n_in, n_out, grid):
    n = len(arrays)
    if n == 0:
        return [], [], [], lambda body: body
    any_spec = [pl.BlockSpec(memory_space=pl.ANY)] * n

    def wrap(body):
        def wrapped(*refs):
            ins = refs[n_in:n_in + n]
            outs = refs[n_in + n + n_out:n_in + 2 * n + n_out]
            sems = refs[len(refs) - 3:]
            pid = [pl.program_id(a) for a in range(len(grid))]
            first = functools.reduce(lambda a, b: a & b, [p == 0 for p in pid])
            last = functools.reduce(lambda a, b: a & b, [p == g - 1 for p, g in zip(pid, grid)])

            @pl.when(first)
            def _():
                _exchange_phase(ins, outs, sems, scatter, start=True)

            body(*refs[:n_in], *refs[n_in + n:n_in + n + n_out], *refs[n_in + 2 * n + n_out:len(refs) - 3])

            @pl.when(last)
            def _():
                _exchange_phase(ins, outs, sems, scatter, start=False)

        return wrapped

    return any_spec, _exchange_out_shapes(arrays, scatter), _exchange_sems(n), wrap


def _gdn_chunk_fwd(qn, kn, vv, gs, gr, bsz, seq, sb, riders):
    hb = GDN_HB
    nsb, ncb, sp = _gdn_specs(seq, sb, hb, False)
    grid = (bsz, GDN_HEADS // hb, nsb)
    any_spec, rider_shapes, rider_sems, wrap = _riding_exchange(riders, False, 5, 3, grid)

    def body(q_ref, k_ref, v_ref, gs_ref, gr_ref, o_ref, st_ref, ti_ref, s_scr):
        @pl.when(pl.program_id(2) == 0)
        def _():
            s_scr[...] = jnp.zeros_like(s_scr)

        c = _chunk_consts()

        def chunk(n, carry):
            r = pl.ds(pl.multiple_of(n * CHUNK, CHUNK), CHUNK)
            rs = pl.ds(pl.multiple_of(n * GDN_DK, GDN_DK), GDN_DK)
            gsv = gs_ref[0, r, :]
            heads = list(range(hb))
            sls = [slice(ih * GDN_DK, (ih + 1) * GDN_DK) for ih in heads]
            q = [q_ref[0, r, sl] for sl in sls]
            k = [k_ref[0, r, sl] for sl in sls]
            v = [v_ref[0, r, sl] for sl in sls]
            beta, dc, eg, egl, ekd = zip(*[
                _gdn_gates(gsv, gr_ref[0, n, pl.ds(ih, 1), :], ih, c) for ih in heads])
            kb = _hmap(lambda a, b: a * b, k, beta)
            amat = _hmap(lambda a, b, d_: jnp.where(c["strict"], _mm(a, b, NT) * d_, 0.0), kb, k, dc)
            tinv = _tri_inv(amat, c["eye"])
            u = _hmap(lambda t_, a, b: _mm3(t_, a * b), tinv, v, beta)
            w = _hmap(lambda t_, a, b: _mm3(t_, a * b), tinv, kb, eg)
            qk = _hmap(lambda a, b, d_: _mm(a, b, NT) * d_, q, k, dc)
            s = [s_scr[ih] for ih in heads]
            v_new = _hmap(lambda a, b, s_: a - _mm(b, s_), u, w, s)
            o = _hmap(lambda a, e, s_, qk_, vn: _mm(a * e, s_) + _mm(qk_, vn), q, eg, s, qk, v_new)
            s_new = _hmap(lambda s_, e, a, f, vn: s_ * e + _mm(a * f, vn, TN), s, egl, k, ekd, v_new)
            for ih in heads:
                o_ref[0, r, sls[ih]] = o[ih]
                st_ref[0, ih, rs, :] = s[ih]
                ti_ref[0, ih, r, :] = tinv[ih]
                s_scr[ih] = s_new[ih]
            return carry

        lax.fori_loop(0, ncb, chunk, 0)

    t3 = (bsz, seq, D_MODEL)
    res = _pcall(
        wrap(body), name="gdn_chunk_fwd", grid=grid,
        in_specs=[sp["wide"](), sp["wide"](), sp["wide"](), sp["gs"](), sp["gr"]] + any_spec,
        out_specs=[sp["wide"](), sp["st"], sp["ti"]] + any_spec,
        out_shape=[jax.ShapeDtypeStruct(t3, F32),
                   jax.ShapeDtypeStruct((bsz, GDN_HEADS, (seq // CHUNK) * GDN_DK, GDN_DK), F32),
                   jax.ShapeDtypeStruct((bsz, GDN_HEADS, seq, CHUNK), F32)] + rider_shapes,
        scratch_shapes=[pltpu.VMEM((hb, GDN_DK, GDN_DK), F32)] + rider_sems,
        compiler_params=_params(("arbitrary", "arbitrary", "arbitrary")),
    )(qn, kn, vv, gs, gr, *riders)
    return res[:3], res[3:]


def _ssd_prep(proj, cw, cb, sp, seq, tm):
    t = proj.shape[0]
    d = D_MODEL
    ssd_w = SSD_HEADS * SSD_P

    def body(i, x_ref, xh_ref, bc_ref, bch_ref, sm_ref, cw_ref, cb_ref, sp_ref, xs_ref, bco_ref, dtx_ref, acsx_ref, acs_ref, ypre_ref):
        keep, _ = _seq_flags(i, seq, tm)
        y = _conv(_conv_taps(x_ref[...], xh_ref[...] * keep, 4), cw_ref[:, 0:d]) + cb_ref[:, 0:d]
        ypre_ref[:, 0:d] = y
        xs_ref[...] = y * _sigmoid(y)
        y = _conv(_conv_taps(bc_ref[...], bch_ref[...] * keep, 4), cw_ref[:, d:d + 512]) + cb_ref[:, d:d + 512]
        ypre_ref[:, d:d + 512] = y
        bco_ref[...] = y * _sigmoid(y)
        sm = sm_ref[...]
        lane = lax.broadcasted_iota(jnp.int32, sm.shape, 1)
        valid = (lane >= 16) & (lane < 32)
        dt = jnp.where(valid, _softplus(sm + sp_ref[1:2, :]), 0.0)
        adt = dt * (-jnp.exp(sp_ref[0:1, :]))
        acs = _mmx(_block_tri(tm, False), adt)
        l64 = lax.broadcasted_iota(jnp.int32, (LANES, ssd_w), 0)
        d64 = lax.broadcasted_iota(jnp.int32, (LANES, ssd_w), 1)
        e64 = (l64 - 16 == d64 // SSD_P).astype(F32)
        dtx_ref[...] = _mmsel(dt, e64, terms=3)
        acsx_ref[...] = _mmsel(acs, e64, terms=3)
        acs_ref[...] = acs

    ins = [("row", proj, d, 5), ("prev", proj, d, 5), ("row", proj, 512, 12), ("prev", proj, 512, 12),
           ("row", proj, LANES, SMALL_CB), ("full", cw), ("full", cb), ("full", sp)]
    return _rowwise("ssd_prep", body, t, tm, ins,
                    [(d, F32), (512, F32), (ssd_w, F32), (ssd_w, F32), (LANES, F32), (d + 512, F32)])


SSD_GW = SSD_HPG * SSD_P


def _ssd_head(acs, ar_ref, n, head, cbm, c):
    col = _rowsum(jnp.where(c["lane"] == head + 16, acs, 0.0))
    lm = jnp.exp(jnp.where(c["tril"], col - ar_ref[0, n, pl.ds(head, 1), :], -1e30))
    return lm, cbm * lm


def _ssd_specs(seq, sb):
    nsb = seq // sb
    ncb = sb // CHUNK
    def specs(order):
        return dict(
            wide=lambda: pl.BlockSpec((1, sb, SSD_HEADS * SSD_P), lambda b, j: (b, order(j), 0)),
            bc=lambda: pl.BlockSpec((1, sb, 2 * SSD_GROUPS * SSD_N), lambda b, j: (b, order(j), 0)),
            small=lambda: pl.BlockSpec((1, sb, LANES), lambda b, j: (b, order(j), 0)),
            ar=pl.BlockSpec((1, ncb, SSD_HEADS, CHUNK), lambda b, j: (b, order(j), 0, 0)),
            st=pl.BlockSpec((1, ncb * SSD_N, SSD_HEADS * SSD_P), lambda b, j: (b, order(j), 0)))
    return nsb, ncb, specs(lambda j: j), specs(lambda j: nsb - 1 - j)


def _ssd_chunk_fwd(xs, bc, dtx, acsx, acs, ar, bsz, seq, sb):
    nsb, ncb, sp, _ = _ssd_specs(seq, sb)

    def body(x_ref, dtx_ref, ax_ref, bc_ref, acs_ref, ar_ref, y_ref, sts_ref, st_scr):
        @pl.when(pl.program_id(1) == 0)
        def _():
            st_scr[...] = jnp.zeros_like(st_scr)

        c = _chunk_consts()
        lane5 = lax.broadcasted_iota(jnp.int32, (CHUNK, SSD_GW), 1) // SSD_P

        def chunk(n, carry):
            r = pl.ds(pl.multiple_of(n * CHUNK, CHUNK), CHUNK)
            rs = pl.ds(pl.multiple_of(n * SSD_N, SSD_N), SSD_N)
            acsv = acs_ref[0, r, :]
            for g in range(SSD_GROUPS):
                gl = slice(g * SSD_GW, (g + 1) * SSD_GW)
                x, dt, ax = x_ref[0, r, gl], dtx_ref[0, r, gl], ax_ref[0, r, gl]
                bm = bc_ref[0, r, g * SSD_N:(g + 1) * SSD_N]
                cm = bc_ref[0, r, (SSD_GROUPS + g) * SSD_N:(SSD_GROUPS + g + 1) * SSD_N]
                xdt = x * dt
                cbm = _mm(cm, bm, NT)
                al = ax[CHUNK - 1:CHUNK, :]
                st = st_scr[:, gl]
                y = _mm(cm, st) * jnp.exp(ax)
                for hh in range(SSD_HPG):
                    _, gm = _ssd_head(acsv, ar_ref, n, g * SSD_HPG + hh, cbm, c)
                    y = y + _mm(gm, jnp.where(lane5 == hh, xdt, 0.0))
                y_ref[0, r, gl] = y
                sts_ref[0, rs, gl] = st
                st_scr[:, gl] = st * jnp.exp(al) + _mm(bm, xdt * jnp.exp(al - ax), TN)
            return carry

        lax.fori_loop(0, ncb, chunk, 0)

    return _pcall(
        body, name="ssd_chunk_fwd", grid=(bsz, nsb),
        in_specs=[sp["wide"](), sp["wide"](), sp["wide"](), sp["bc"](), sp["small"](), sp["ar"]],
        out_specs=[sp["wide"](), sp["st"]],
        out_shape=[jax.ShapeDtypeStruct((bsz, seq, SSD_HEADS * SSD_P), F32),
                   jax.ShapeDtypeStruct((bsz, (seq // CHUNK) * SSD_N, SSD_HEADS * SSD_P), F32)],
        scratch_shapes=[pltpu.VMEM((SSD_N, SSD_HEADS * SSD_P), F32)],
        compiler_params=_params(("parallel", "arbitrary")),
    )(xs, dtx, acsx, bc, acs, ar)


def _gate_norm(o_gdn, y_ssd, xs, proj, gnw, snw, dvec, tm):
    t = o_gdn.shape[0]
    d = D_MODEL

    def body(i, o_ref, za_ref, y_ref, xs_ref, zs_ref, gnw_ref, snw_ref, dv_ref, out_ref):
        for hh in range(GDN_HEADS):
            sl = slice(hh * GDN_DK, (hh + 1) * GDN_DK)
            oh, _ = _rms(o_ref[:, sl], GDN_DK)
            z = za_ref[:, sl]
            out_ref[:, sl] = (oh * gnw_ref[...] * (z * _sigmoid(z))).astype(BF16)
        zs = zs_ref[...]
        yg = (y_ref[...] + dv_ref[...] * xs_ref[...]) * (zs * _sigmoid(zs))
        for g in range(SSD_GROUPS):
            sl = slice(g * 512, (g + 1) * 512)
            yh, _ = _rms(yg[:, sl], 512)
            out_ref[:, d + g * 512:d + (g + 1) * 512] = (yh * snw_ref[:, sl]).astype(BF16)

    ins = [("row", o_gdn, d, 0), ("row", proj, d, 3), ("row", y_ssd, d, 0), ("row", xs, d, 0), ("row", proj, d, 4),
           ("full", gnw), ("full", snw), ("full", dvec)]
    return _rowwise("gate_norm", body, t, tm, ins, [(2 * d, BF16)])[0]


def _out_mid(mixin, w_out, x, pmw, pfw):
    d = D_MODEL

    def epilogue(mix, x_ref, pmw_ref, pfw_ref, mix_ref, x1_ref, h2_ref):
        mix_ref[...] = mix
        mh, _ = _rms(mix, d)
        x1 = x_ref[...] + mh * pmw_ref[...]
        x1_ref[...] = x1
        xh, _ = _rms(x1, d)
        h2_ref[...] = (xh * pfw_ref[...]).astype(BF16)

    return _matmul_rows("mm_out_mid", mixin, w_out, "nn", epilogue, [x], [pmw, pfw], [(d, F32), (d, F32), (d, BF16)],
                        tk=2 * d)


def _ffn_act(u_pre, cw, cb, seq, tm):
    t = u_pre.shape[0]

    def body(i, ug_ref, ugh_ref, uu_ref, uuh_ref, cw_ref, cb_ref, act_ref, u_ref):
        keep, _ = _seq_flags(i, seq, tm)
        gate = _conv(_conv_taps(ug_ref[...], ugh_ref[...] * keep, 3), cw_ref[:, 0:D_FF]) + cb_ref[:, 0:D_FF]
        up = _conv(_conv_taps(uu_ref[...], uuh_ref[...] * keep, 3), cw_ref[:, D_FF:2 * D_FF]) + cb_ref[:, D_FF:2 * D_FF]
        u_ref[:, 0:D_FF] = gate
        u_ref[:, D_FF:2 * D_FF] = up
        act_ref[...] = (gate * _sigmoid(gate) * up).astype(BF16)

    ins = [("row", u_pre, D_FF, 0), ("prev", u_pre, D_FF, 0), ("row", u_pre, D_FF, 1), ("prev", u_pre, D_FF, 1),
           ("full", cw), ("full", cb)]
    return _rowwise("ffn_act", body, t, tm, ins, [(D_FF, BF16), (2 * D_FF, F32)])


def _down_final(act, w_down, x1, tgt, w):
    d = D_MODEL

    def epilogue(f, x1_ref, t_ref, w_ref, dy_ref, df_ref, loss_ref, dw_ref):
        fh, r = _rms(f, d)
        e = x1_ref[...] + fh * w_ref[...] - t_ref[...]
        loss_ref[...] += _colsum(e * e) * (0.5 / d)
        dy = e * (1.0 / d)
        dy_ref[...] = dy
        dw_ref[...] += _colsum(dy * fh)
        df_ref[...] = _rms_bwd(fh, r, dy * w_ref[...], d).astype(BF16)

    return _matmul_rows("mm_down_final", act, w_down, "nn", epilogue, [x1, tgt], [w], [(d, F32), (d, BF16)],
                        accs=[(1, d), (1, d)], tk=D_FF)


def _ffn_bwd(u, u_pre, dact, cw, seq, tm):
    t = u.shape[0]

    def body(i, g_ref, gn_ref, up_ref, upn_ref, xg_ref, xu_ref, da_ref, dan_ref, cw_ref, dpre_ref, dcw_ref, dcb_ref):
        _, keep_next = _seq_flags(i, seq, tm)
        ext = lambda a_ref, n_ref: jnp.concatenate([a_ref[...], n_ref[...]], axis=0)
        rows = tm + SUBLANES
        gate, up = ext(g_ref, gn_ref), ext(up_ref, upn_ref)
        sg = _sigmoid(gate)
        da = jnp.concatenate([da_ref[...], dan_ref[...] * keep_next], axis=0)
        for off, grad, x_ref in ((0, da * up * _dsilu(gate, sg), xg_ref), (D_FF, da * gate * sg, xu_ref)):
            x = x_ref[...]
            own = grad[0:tm]
            acc = own * cw_ref[2:3, off:off + D_FF]
            dcb_ref[:, off:off + D_FF] += _colsum(own)
            dcw_ref[2:3, off:off + D_FF] += _colsum(own * x)
            for j in (1, 2):
                ahead = pltpu.roll(grad, rows - j, 0)[0:tm]
                acc = acc + ahead * cw_ref[2 - j:3 - j, off:off + D_FF]
                dcw_ref[2 - j:3 - j, off:off + D_FF] += _colsum(ahead * x)
            dpre_ref[:, off:off + D_FF] = acc.astype(BF16)

    ins = []
    for cb_ in range(2):
        ins += [("row", u, D_FF, cb_), ("next", u, D_FF, cb_)]
    ins += [("row", u_pre, D_FF, 0), ("row", u_pre, D_FF, 1), ("row", dact, D_FF, 0), ("next", dact, D_FF, 0), ("full", cw)]
    return _rowwise("ffn_bwd", body, t, tm, ins, [(2 * D_FF, BF16)], accs=[(SUBLANES, 2 * D_FF), (1, 2 * D_FF)])


def _assemble_dproj(dpre_qkv, dza, dzs, dpre_xbc, dsm, proj, gcw, scw, seq, tm):
    t = dza.shape[0]
    d = D_MODEL

    def body(i, dq_ref, dqn_ref, dk_ref, dkn_ref, dv_ref, dvn_ref, dx_ref, dxn_ref, dbc_ref, dbcn_ref, dza_ref, dzs_ref,
             dsm_ref, xq_ref, xk_ref, xv_ref, xx_ref, xbc_ref, gcw_ref, scw_ref, o_ref, dgcw_ref, dscw_ref):
        _, keep = _seq_flags(i, seq, tm)
        pieces = [(g_ref, n_ref, gcw_ref, dgcw_ref, x_ref, 0, c0) for g_ref, n_ref, x_ref, c0 in (
            (dq_ref, dqn_ref, xq_ref, 0), (dk_ref, dkn_ref, xk_ref, d), (dv_ref, dvn_ref, xv_ref, 2 * d))]
        pieces += [(g_ref, n_ref, scw_ref, dscw_ref, x_ref, 5 * d, c0) for g_ref, n_ref, x_ref, c0 in (
            (dx_ref, dxn_ref, xx_ref, 0), (dbc_ref, dbcn_ref, xbc_ref, d))]
        for d_ref, n_ref, cw_ref, dcw_ref, x_ref, base, c0 in pieces:
            w = x_ref.shape[1]
            x = x_ref[...]
            g = d_ref[...]
            halo = n_ref[...] * keep
            acc = g * cw_ref[3:4, c0:c0 + w]
            dcw_ref[3:4, c0:c0 + w] += _colsum(g * x)
            for j in range(1, 4):
                ahead = _shift_up(g, halo, j)
                acc = acc + ahead * cw_ref[3 - j:4 - j, c0:c0 + w]
                dcw_ref[3 - j:4 - j, c0:c0 + w] += _colsum(ahead * x)
            o_ref[:, base + c0:base + c0 + w] = acc.astype(BF16)
        o_ref[:, 3 * d:4 * d] = dza_ref[...]
        o_ref[:, 4 * d:5 * d] = dzs_ref[...]
        o_ref[:, 6 * d + 512:6 * d + 512 + LANES] = dsm_ref[...]
        o_ref[:, 6 * d + 512 + LANES:PROJ_W] = jnp.zeros((tm, PROJ_W - (6 * d + 512 + LANES)), BF16)

    ins = []
    for g in tuple(dpre_qkv) + tuple(dpre_xbc):
        ins += [("row", g, g.shape[1], 0), ("next", g, g.shape[1], 0)]
    ins += [("row", dza, d, 0), ("row", dzs, d, 0), ("row", dsm, LANES, 0),
           ("row", proj, d, 0), ("row", proj, d, 1), ("row", proj, d, 2), ("row", proj, d, 5), ("row", proj, 512, 12),
           ("full", gcw), ("full", scw)]
    return _rowwise("assemble_dproj", body, t, tm, ins, [(PROJ_W, BF16)], accs=[(SUBLANES, 3 * d), (SUBLANES, d + 512)])


def _dh2_mid_bwd(du_pre, w_up, x1, mix, dy, pmw, pfw):
    d = D_MODEL

    def epilogue(dh2, x1_ref, mix_ref, dy_ref, pmw_ref, pfw_ref, dx1_ref, dmix_ref, dpm_ref, dpf_ref):
        xh, r2 = _rms(x1_ref[...], d)
        dpf_ref[...] += _colsum(dh2 * xh)
        dx1 = dy_ref[...] + _rms_bwd(xh, r2, dh2 * pfw_ref[...], d)
        dx1_ref[...] = dx1
        mh, r = _rms(mix_ref[...], d)
        dpm_ref[...] += _colsum(dx1 * mh)
        dmix_ref[...] = _rms_bwd(mh, r, dx1 * pmw_ref[...], d).astype(BF16)

    return _matmul_rows("mm_dh2_mid_bwd", du_pre, w_up, "nt", epilogue, [x1, mix, dy], [pmw, pfw],
                        [(d, F32), (d, BF16)], accs=[(1, d), (1, d)], tk=D_FF)


def _dmixin_gate_norm_bwd(dmix, w_out, o_gdn, y_ssd, xs, proj, gnw, snw, dvec):
    d = D_MODEL

    def epilogue(dmixin, o_ref, za_ref, y_ref, xs_ref, zs_ref, gnw_ref, snw_ref, dv_ref,
                 do_ref, dza_ref, dy_ref, dxs_ref, dzs_ref, dgnw_ref, dsnw_ref, dd_ref):
        for hh in range(GDN_HEADS):
            sl = slice(hh * GDN_DK, (hh + 1) * GDN_DK)
            oh, r = _rms(o_ref[:, sl], GDN_DK)
            z = za_ref[:, sl]
            sz = _sigmoid(z)
            dm = dmixin[:, sl]
            don = dm * (z * sz)
            dza_ref[:, sl] = (dm * oh * gnw_ref[...] * _dsilu(z, sz)).astype(BF16)
            dgnw_ref[...] += _colsum(don * oh)
            do_ref[:, sl] = _rms_bwd(oh, r, don * gnw_ref[...], GDN_DK)
        zs = zs_ref[...]
        sz = _sigmoid(zs)
        sil = zs * sz
        x = xs_ref[...]
        y0 = y_ref[...] + dv_ref[...] * x
        yg = y0 * sil
        dms = dmixin[:, d:2 * d]
        for g in range(SSD_GROUPS):
            sl = slice(g * 512, (g + 1) * 512)
            yh, r = _rms(yg[:, sl], 512)
            dsnw_ref[:, sl] += _colsum(dms[:, sl] * yh)
            dyg = _rms_bwd(yh, r, dms[:, sl] * snw_ref[:, sl], 512)
            dy0 = dyg * sil[:, sl]
            dzs_ref[:, sl] = (dyg * y0[:, sl] * _dsilu(zs[:, sl], sz[:, sl])).astype(BF16)
            dy_ref[:, sl] = dy0
            dxs_ref[:, sl] = dy0 * dv_ref[:, sl]
            dd_ref[:, sl] += _colsum(dy0 * x[:, sl])

    row_ins = [o_gdn, (proj, d, 3), y_ssd, xs, (proj, d, 4)]
    return _matmul_rows("mm_dmixin_gate_norm_bwd", dmix, w_out, "nt", epilogue, row_ins, [gnw, snw, dvec],
                        [(d, F32), (d, BF16), (d, F32), (d, F32), (d, BF16)], accs=[(1, GDN_DK), (1, d), (1, d)], tm=256)


def _ssd_chunk_bwd(xs, bc, dtx, acsx, acs, ar, dy, sts, ypre, dxs_d, bsz, seq, sb):
    nsb, ncb, _, sp = _ssd_specs(seq, sb)
    bc_w = 2 * SSD_GROUPS * SSD_N
    x_w = SSD_HEADS * SSD_P

    def body(x_ref, dtx_ref, ax_ref, bc_ref, acs_ref, ar_ref, dy_ref, sts_ref, yx_ref, ybc_ref, dxd_ref,
             dx_ref, dbc_ref, ddt_ref, dacs_ref, dbx_ref, dbbc_ref, dst_scr):
        @pl.when(pl.program_id(1) == 0)
        def _():
            dst_scr[...] = jnp.zeros_like(dst_scr)

        @pl.when((pl.program_id(0) == 0) & (pl.program_id(1) == 0))
        def _():
            dbx_ref[...] = jnp.zeros_like(dbx_ref)
            dbbc_ref[...] = jnp.zeros_like(dbbc_ref)

        def to_conv_out(grad, y):
            return grad * _dsilu(y, _sigmoid(y))

        c = _chunk_consts()
        lane5 = lax.broadcasted_iota(jnp.int32, (CHUNK, SSD_GW), 1) // SSD_P
        row5 = lax.broadcasted_iota(jnp.int32, (CHUNK, SSD_GW), 0)
        sel_in = lax.broadcasted_iota(jnp.int32, (SSD_GW, LANES), 0) // SSD_P
        sel_out = lax.broadcasted_iota(jnp.int32, (SSD_GW, LANES), 1)

        def chunk(nn, carry):
            n = ncb - 1 - nn
            r = pl.ds(pl.multiple_of(n * CHUNK, CHUNK), CHUNK)
            rs = pl.ds(pl.multiple_of(n * SSD_N, SSD_N), SSD_N)
            acsv = acs_ref[0, r, :]
            ddt = jnp.zeros((CHUNK, LANES), F32)
            dacs = jnp.zeros((CHUNK, LANES), F32)
            for g in range(SSD_GROUPS):
                gl = slice(g * SSD_GW, (g + 1) * SSD_GW)
                x, dt, ax, dyv = x_ref[0, r, gl], dtx_ref[0, r, gl], ax_ref[0, r, gl], dy_ref[0, r, gl]
                bm = bc_ref[0, r, g * SSD_N:(g + 1) * SSD_N]
                cm = bc_ref[0, r, (SSD_GROUPS + g) * SSD_N:(SSD_GROUPS + g + 1) * SSD_N]
                st = sts_ref[0, rs, gl]
                dst = dst_scr[:, gl]
                rsel = (sel_in + (16 + g * SSD_HPG) == sel_out).astype(F32)
                xdt = x * dt
                cbm = _mm(cm, bm, NT)
                al = ax[CHUNK - 1:CHUNK, :]
                ex, el = jnp.exp(ax), jnp.exp(al)
                dec = jnp.exp(al - ax)
                xd = xdt * dec
                dye = dyv * ex
                dxd = _mm(bm, dst)
                dxdt = dec * dxd
                dcm = _mm(dye, st, NT)
                dbm = _mm(xd, dst, NT)
                z = dye * _mm(cm, st) - dxd * xd
                zl = _colsum(dst * st) * el + _colsum(dxd * xd)
                z = z + jnp.where(row5 == CHUNK - 1, zl, 0.0)
                dcb = jnp.zeros((CHUNK, CHUNK), F32)
                for hh in range(SSD_HPG):
                    head = g * SSD_HPG + hh
                    lm, gm = _ssd_head(acsv, ar_ref, n, head, cbm, c)
                    dym = jnp.where(lane5 == hh, dyv, 0.0)
                    dxdt = dxdt + _mm(gm, dym, TN)
                    dg = _mm(dym, xdt, NT)
                    dcb = dcb + dg * lm
                    pm = dg * gm
                    dacs = dacs + jnp.where(c["lane"] == head + 16, _rowsum(pm) - _mmsel(pm, c["ones"], TN), 0.0)
                for sl, grad in ((slice((SSD_GROUPS + g) * SSD_N, (SSD_GROUPS + g + 1) * SSD_N), dcm + _mm(dcb, bm)),
                                 (slice(g * SSD_N, (g + 1) * SSD_N), dbm + _mm(dcb, cm, TN))):
                    dpre = to_conv_out(grad, ybc_ref[0, r, sl])
                    dbc_ref[0, r, sl] = dpre
                    dbbc_ref[:, sl] += _colsum(dpre)
                dacs = dacs + _mmsel(z, rsel)
                ddt = ddt + _mmsel(dxdt * x, rsel)
                dpre = to_conv_out(dxdt * dt + dxd_ref[0, r, gl], yx_ref[0, r, gl])
                dx_ref[0, r, gl] = dpre
                dbx_ref[:, gl] += _colsum(dpre)
                dst_scr[:, gl] = dst * el + _mm(cm, dye, TN)
            ddt_ref[0, r, :] = ddt
            dacs_ref[0, r, :] = dacs
            return carry

        lax.fori_loop(0, ncb, chunk, 0)

    nsb_rev = lambda j: nsb - 1 - j
    return _pcall(
        body, name="ssd_chunk_bwd", grid=(bsz, nsb),
        in_specs=[sp["wide"](), sp["wide"](), sp["wide"](), sp["bc"](), sp["small"](), sp["ar"], sp["wide"](), sp["st"],
                  sp["wide"](), pl.BlockSpec((1, sb, bc_w), lambda b, j: (b, nsb_rev(j), x_w // bc_w)), sp["wide"]()],
        out_specs=[sp["wide"](), sp["bc"](), sp["small"](), sp["small"](),
                   pl.BlockSpec((1, x_w), lambda b, j: (0, 0)), pl.BlockSpec((1, bc_w), lambda b, j: (0, 0))],
        out_shape=[jax.ShapeDtypeStruct((bsz, seq, x_w), F32), jax.ShapeDtypeStruct((bsz, seq, bc_w), F32),
                   jax.ShapeDtypeStruct((bsz, seq, LANES), F32), jax.ShapeDtypeStruct((bsz, seq, LANES), F32),
                   jax.ShapeDtypeStruct((1, x_w), F32), jax.ShapeDtypeStruct((1, bc_w), F32)],
        scratch_shapes=[pltpu.VMEM((SSD_N, x_w), F32)],
        compiler_params=_params(("arbitrary", "arbitrary")),
    )(xs, dtx, acsx, bc, acs, ar, dy, sts, ypre, ypre, dxs_d)


def _through_norm_silu(g, y, scale):
    sy = _sigmoid(y)
    ds_ = _dsilu(y, sy)
    if scale is None:
        return g * ds_
    a = y * sy
    n = lax.rsqrt(_rowsum(a * a) + EPS)
    ah = a * n
    return (scale * n) * (g - ah * _rowsum(g * ah)) * ds_


def _gdn_chunk_bwd(qn, kn, vv, gs, gr, do, sts, tis, ypre, bsz, seq, sb, riders):
    hb = GDN_HB
    nsb, ncb, sp = _gdn_specs(seq, sb, hb, True)
    grid = (bsz, GDN_HEADS // hb, nsb)
    any_spec, rider_shapes, rider_sems, wrap = _riding_exchange(riders, True, 11, 4, grid)
    ypre_spec = lambda cb: pl.BlockSpec((1, sb, hb * GDN_DK), lambda b, h, j: (b, nsb - 1 - j, cb))

    def body(q_ref, k_ref, v_ref, gs_ref, gr_ref, do_ref, st_ref, ti_ref, yq_ref, yk_ref, yv_ref,
             dq_ref, dk_ref, dv_ref, dgs_ref, ds_scr):
        @pl.when(pl.program_id(2) == 0)
        def _():
            ds_scr[...] = jnp.zeros_like(ds_scr)

        c = _chunk_consts()

        def chunk(nn, carry):
            n = ncb - 1 - nn
            r = pl.ds(pl.multiple_of(n * CHUNK, CHUNK), CHUNK)
            rs = pl.ds(pl.multiple_of(n * GDN_DK, GDN_DK), GDN_DK)
            gsv = gs_ref[0, r, :]
            heads = list(range(hb))
            sls = [slice(ih * GDN_DK, (ih + 1) * GDN_DK) for ih in heads]
            q = [q_ref[0, r, sl] for sl in sls]
            k = [k_ref[0, r, sl] for sl in sls]
            v = [v_ref[0, r, sl] for sl in sls]
            do_ = [do_ref[0, r, sl] for sl in sls]
            s = [st_ref[0, ih, rs, :] for ih in heads]
            tinv = [ti_ref[0, ih, r, :] for ih in heads]
            dsn = [ds_scr[ih] for ih in heads]
            beta, dc, eg, egl, ekd = zip(*[
                _gdn_gates(gsv, gr_ref[0, n, pl.ds(ih, 1), :], ih, c) for ih in heads])
            mul = lambda a, b: a * b
            kb = _hmap(mul, k, beta)
            rhs_w = _hmap(mul, kb, eg)
            u = _hmap(lambda t_, a, b: _mm3(t_, a * b), tinv, v, beta)
            w = _hmap(_mm3, tinv, rhs_w)
            amat = _hmap(lambda a, b, d_: jnp.where(c["strict"], _mm(a, b, NT) * d_, 0.0), kb, k, dc)
            qk = _hmap(lambda a, b, d_: _mm(a, b, NT) * d_, q, k, dc)
            qd = _hmap(mul, q, eg)
            kd = _hmap(mul, k, ekd)
            v_new = _hmap(lambda a, b, s_: a - _mm(b, s_), u, w, s)
            dv_new = _hmap(lambda qk_, d_, kd_, dn: _mm(qk_, d_, TN) + _mm(kd_, dn), qk, do_, kd, dsn)
            dqk = _hmap(lambda d_, vn: _mm(d_, vn, NT), do_, v_new)
            dqd = _hmap(lambda d_, s_: _mm(d_, s_, NT), do_, s)
            ds_new = _hmap(lambda qd_, d_, dn, e, w_, dvn: _mm(qd_, d_, TN) + dn * e - _mm(w_, dvn, TN),
                           qd, do_, dsn, egl, w, dv_new)
            dkd = _hmap(lambda vn, dn: _mm(vn, dn, NT), v_new, dsn)
            dgl = _hmap(lambda s_, dn, e: _colsum(_rowsum(s_ * dn)) * e, s, dsn, egl)
            dw = _hmap(lambda dvn, s_: -_mm(dvn, s_, NT), dv_new, s)
            dru = _hmap(lambda t_, a: _mm3(t_, a, TN), tinv, dv_new)
            drw = _hmap(lambda t_, a: _mm3(t_, a, TN), tinv, dw)
            da = _hmap(lambda a, u_, b, w_: jnp.where(c["strict"], -(_mm(a, u_, NT) + _mm(b, w_, NT)), 0.0), dru, u, drw, w)
            m = _hmap(mul, da, dc)
            dkb = _hmap(lambda a, e, m_, k_: a * e + _mm(m_, k_), drw, eg, m, k)
            mq = _hmap(mul, dqk, dc)
            dq = _hmap(lambda mq_, k_, a, e: _mm(mq_, k_) + a * e, mq, k, dqd, eg)
            dk = _hmap(lambda m_, kb_, mq_, q_, a, e, b, be: _mm(m_, kb_, TN) + _mm(mq_, q_, TN) + a * e + b * be,
                       m, kb, mq, q, dkd, ekd, dkb, beta)
            dbeta = _hmap(lambda a, v_, b, k_: _rowsum(a * v_) + _rowsum(b * k_), dru, v, dkb, k)
            pq = _hmap(lambda a, am, b, qk_: a * am + b * qk_, da, amat, dqk, qk)
            ekk = _hmap(lambda a, b: _rowsum(a * b), dkd, kd)
            dgc = _hmap(lambda pq_, a, rw, b, qd_, e, gl_: (
                _rowsum(pq_) - _mmsel(pq_, c["ones"], TN) + (_rowsum(a * rw) + _rowsum(b * qd_) - e)
                + jnp.where(c["row1"] == CHUNK - 1, _colsum(e) + gl_, 0.0)), pq, drw, rhs_w, dqd, qd, ekk, dgl)
            dv = _hmap(mul, dru, beta)
            dyq = _hmap(lambda g_, sl: _through_norm_silu(g_, yq_ref[0, r, sl], GDN_DK ** -0.5), dq, sls)
            dyk = _hmap(lambda g_, sl: _through_norm_silu(g_, yk_ref[0, r, sl], 1.0), dk, sls)
            dyv = _hmap(lambda g_, sl: _through_norm_silu(g_, yv_ref[0, r, sl], None), dv, sls)
            dgs = jnp.zeros((CHUNK, LANES), F32)
            for ih in heads:
                ds_scr[ih] = ds_new[ih]
                dq_ref[0, r, sls[ih]] = dyq[ih]
                dk_ref[0, r, sls[ih]] = dyk[ih]
                dv_ref[0, r, sls[ih]] = dyv[ih]
                dgs = dgs + jnp.where(c["lane"] == ih, dbeta[ih], jnp.where(c["lane"] == ih + 8, dgc[ih], 0.0))
            dgs_ref[0, r, :] = dgs
            return carry

        lax.fori_loop(0, ncb, chunk, 0)

    res = _pcall(
        wrap(body), name="gdn_chunk_bwd", grid=grid,
        in_specs=[sp["wide"](), sp["wide"](), sp["wide"](), sp["gs"](), sp["gr"], sp["wide"](), sp["st"], sp["ti"],
                  ypre_spec(0), ypre_spec(1), ypre_spec(2)] + any_spec,
        out_specs=[sp["wide"](), sp["wide"](), sp["wide"](), sp["gs"]()] + any_spec,
        out_shape=[jax.ShapeDtypeStruct((bsz, seq, D_MODEL), F32)] * 3 + [jax.ShapeDtypeStruct((bsz, seq, LANES), F32)]
        + rider_shapes,
        scratch_shapes=[pltpu.VMEM((hb, GDN_DK, GDN_DK), F32)] + rider_sems,
        compiler_params=_params(("arbitrary", "arbitrary", "arbitrary")),
    )(qn, kn, vv, gs, gr, do, sts, tis, ypre, ypre, ypre, *riders)
    return res[:4], res[4:]


def _gates_bwd(proj, dgs, ddt, dacs, gp, sp, tm):
    t = proj.shape[0]

    def body(i, sm_ref, dgs_ref, ddt_ref, dacs_ref, gp_ref, sp_ref, dsm_ref, dgp_ref, dsp_ref):
        sm = sm_ref[...]
        lane = lax.broadcasted_iota(jnp.int32, sm.shape, 1)
        is_g = (lane >= 8) & (lane < 16)
        is_dt = (lane >= 16) & (lane < 32)
        dgs = dgs_ref[...]
        back = _mmx(_block_tri(tm, True), jnp.where(is_g, dgs, 0.0) + dacs_ref[...])
        beta = _sigmoid(sm)
        bias = gp_ref[1:2, :] + sp_ref[1:2, :]
        xb = sm + bias
        soft, dsoft = _softplus(xb), _sigmoid(xb)
        g_neg = -jnp.exp(gp_ref[0:1, :])
        a_neg = -jnp.exp(sp_ref[0:1, :])
        dg = jnp.where(is_g, back * g_neg, 0.0)
        dxb_g = dg * dsoft
        dxb_dt = jnp.where(is_dt, (ddt_ref[...] + back * a_neg) * dsoft, 0.0)
        dsm_ref[...] = (jnp.where(lane < 8, dgs * beta * (1.0 - beta), dxb_g) + dxb_dt).astype(BF16)
        dgp_ref[1:2, :] += _colsum(dxb_g)
        dgp_ref[0:1, :] += _colsum(dg * soft)
        dsp_ref[1:2, :] += _colsum(dxb_dt)
        dsp_ref[0:1, :] += jnp.where(is_dt[0:1, :], _colsum(back * soft) * a_neg, 0.0)

    ins = [("row", proj, LANES, SMALL_CB), ("row", dgs, LANES, 0), ("row", ddt, LANES, 0), ("row", dacs, LANES, 0),
           ("full", gp), ("full", sp)]
    return _rowwise("gates_bwd", body, t, tm, ins, [(LANES, BF16)], accs=[(SUBLANES, LANES), (SUBLANES, LANES)])


def _dh1_first_bwd(dproj, wp_in, x, dx1, w, scatter_riders):
    d = D_MODEL

    def epilogue(dh, x_ref, dx1_ref, w_ref, dx_ref, dw_ref):
        xh, r = _rms(x_ref[...], d)
        dw_ref[...] += _colsum(dh * xh)
        dx_ref[...] = dx1_ref[...] + _rms_bwd(xh, r, dh * w_ref[...], d)

    return _matmul_rows("mm_dh1_first_bwd", dproj, wp_in, "nt", epilogue, [x, dx1], [w], [(d, F32)], accs=[(1, d)],
                        tk=PROJ_W // 2, scatter_riders=scatter_riders)


def _gather_two_level(name, arrays):
    n = len(arrays)
    n_sem = 7

    def body(*refs):
        ins, outs = refs[:n], refs[n:2 * n]
        send_sems, recv_sems, loc_sems = refs[2 * n:]
        x, y, c = lax.axis_index("x"), lax.axis_index("y"), lax.axis_index("c")
        slot = lambda px, py, pc: 4 * px + 2 * py + pc
        sibling = (x, y, 1 - c)
        chips = [(1 - x, y), (x, 1 - y), (1 - x, 1 - y)]

        def copy(t, k, src, block, to):
            return pltpu.make_async_remote_copy(
                src_ref=src, dst_ref=outs[t].at[block], send_sem=send_sems.at[t, k], recv_sem=recv_sems.at[t, k],
                device_id=to, device_id_type=pl.DeviceIdType.MESH)

        own, first, passed = [], [], []
        for t in range(n):
            own.append(pltpu.make_async_copy(ins[t], outs[t].at[slot(x, y, c)], loc_sems.at[t]))
            first.append(copy(t, 0, ins[t], slot(x, y, c), sibling))
            first += [copy(t, 1 + j, ins[t], slot(x, y, c), (px, py, c)) for j, (px, py) in enumerate(chips)]
        for cp in own + first:
            cp.start()
        for t in range(n):
            for j, (px, py) in enumerate(chips):
                copy(t, 1 + j, ins[t], slot(px, py, c), (px, py, c)).wait_recv()
                fwd = copy(t, 4 + j, outs[t].at[slot(px, py, c)], slot(px, py, c), sibling)
                fwd.start()
                passed.append(fwd)
        for t in range(n):
            copy(t, 0, ins[t], slot(x, y, 1 - c), sibling).wait_recv()
            for j, (px, py) in enumerate(chips):
                copy(t, 4 + j, ins[t], slot(px, py, 1 - c), sibling).wait_recv()
        for cp in first + passed:
            cp.wait_send()
        for cp in own:
            cp.wait()

    return _pcall(
        body, name=name,
        in_specs=[pl.BlockSpec(memory_space=pl.ANY)] * n,
        out_specs=[pl.BlockSpec(memory_space=pl.ANY)] * n,
        out_shape=_exchange_out_shapes(arrays, False),
        scratch_shapes=[pltpu.SemaphoreType.DMA((n, n_sem)), pltpu.SemaphoreType.DMA((n, n_sem)), pltpu.SemaphoreType.DMA((n,))],
    )(*arrays)


def _exchange_out_shapes(arrays, scatter):
    return [jax.ShapeDtypeStruct(a.shape if scatter else (N_DEV,) + a.shape, a.dtype) for a in arrays]


def _exchange_sems(n):
    return [pltpu.SemaphoreType.DMA((n, N_DEV - 1)), pltpu.SemaphoreType.DMA((n, N_DEV - 1)), pltpu.SemaphoreType.DMA((n,))]


def _exchange_phase(ins, outs, sems, scatter, start):
    send_sems, recv_sems, loc_sems = sems
    x, y, c = lax.axis_index("x"), lax.axis_index("y"), lax.axis_index("c")
    me = 4 * x + 2 * y + c
    for t in range(len(ins)):
        loc = pltpu.make_async_copy(ins[t].at[me] if scatter else ins[t], outs[t].at[me], loc_sems.at[t])
        if start:
            loc.start()
        else:
            loc.wait()
        for k in range(N_DEV - 1):
            bx, by, bc = ((k + 1) >> 2) & 1, ((k + 1) >> 1) & 1, (k + 1) & 1
            px = 1 - x if bx else x
            py = 1 - y if by else y
            pc = 1 - c if bc else c
            peer = 4 * px + 2 * py + pc
            src = ins[t].at[peer] if scatter else ins[t]
            copy = lambda dst: pltpu.make_async_remote_copy(
                src_ref=src, dst_ref=dst, send_sem=send_sems.at[t, k], recv_sem=recv_sems.at[t, k],
                device_id=(px, py, pc), device_id_type=pl.DeviceIdType.MESH)
            if start:
                copy(outs[t].at[me]).start()
            else:
                copy(outs[t].at[me]).wait_send()
                copy(outs[t].at[peer]).wait_recv()


def _adam_math(w, g, m, v):
    m = ADAM_B1 * m + (1.0 - ADAM_B1) * g
    v = ADAM_B2 * v + (1.0 - ADAM_B2) * (g * g)
    m_hat = m / (1.0 - ADAM_B1 ** ADAM_STEP)
    v_hat = v / (1.0 - ADAM_B2 ** ADAM_STEP)
    delta = -ADAM_LR * (m_hat / (jnp.sqrt(v_hat) + ADAM_EPS) + ADAM_WD * w)
    return delta, m, v


def _adam_big(name, parts, w, m, v, tm):
    r, c = w.shape
    tm = tm if r % tm == 0 else r

    def body(p_ref, w_ref, m_ref, v_ref, g_ref, d_ref, nm_ref, nv_ref):
        g = p_ref[0].astype(F32)
        for s in range(1, N_DEV):
            g = g + p_ref[s].astype(F32)
        g_ref[...] = g
        d_ref[...], nm_ref[...], nv_ref[...] = _adam_math(w_ref[...], g, m_ref[...], v_ref[...])

    blk = lambda: pl.BlockSpec((tm, c), lambda i: (i, 0))
    return _pcall(
        body, name=name, grid=(r // tm,),
        in_specs=[pl.BlockSpec((N_DEV, tm, c), lambda i: (0, i, 0)), blk(), blk(), blk()],
        out_specs=[blk(), blk(), blk(), blk()],
        out_shape=[jax.ShapeDtypeStruct((r, c), F32)] * 4,
        compiler_params=_params(("parallel",)),
    )(parts, w, m, v)


SMALL_ROWS = 56
ROW_DD, ROW_LOSS = 5, 6


def _small_sum(gathered):
    def body(g_ref, o_ref, x_ref):
        s = g_ref[0]
        for dev in range(1, N_DEV):
            s = s + g_ref[dev]
        o_ref[...] = s
        ri = lax.broadcasted_iota(jnp.int32, (D_MODEL, LANES), 0)
        ro = lax.broadcasted_iota(jnp.int32, (D_MODEL, LANES), 1)
        heads = _mmx(jnp.broadcast_to(s[ROW_DD:ROW_DD + 1, :], (SUBLANES, D_MODEL)), (ri // SSD_P == ro).astype(F32))
        loss = _rowsum(jnp.broadcast_to(s[ROW_LOSS:ROW_LOSS + 1, :], (SUBLANES, D_MODEL)))
        row = lax.broadcasted_iota(jnp.int32, (SUBLANES, LANES), 0)
        x_ref[...] = jnp.where(row == 0, heads, jnp.broadcast_to(loss, (SUBLANES, LANES)))

    return _pcall(
        body, name="small_sum",
        out_shape=[jax.ShapeDtypeStruct((SMALL_ROWS, D_MODEL), F32), jax.ShapeDtypeStruct((SUBLANES, LANES), F32)],
        compiler_params=_params(None),
    )(gathered)


def _adam_small(g, w, m, v):
    def body(g_ref, w_ref, m_ref, v_ref, d_ref, nm_ref, nv_ref):
        d_ref[...], nm_ref[...], nv_ref[...] = _adam_math(w_ref[...], g_ref[...], m_ref[...], v_ref[...])

    return _pcall(body, name="adam_small", out_shape=[jax.ShapeDtypeStruct(g.shape, F32)] * 3,
                  compiler_params=_params(None))(g, w, m, v)


def _pack(pieces, rows):
    flat = jnp.concatenate([p.reshape(-1).astype(F32) for p in pieces])
    return jnp.pad(flat, (0, rows * D_MODEL - flat.shape[0])).reshape(rows, D_MODEL)


def _unpack(packed, shapes):
    flat = packed.reshape(-1)
    out, off = [], 0
    for shp in shapes:
        size = 1
        for s in shp:
            size *= s
        out.append(flat[off:off + size].reshape(shp))
        off += size
    return out


def _permute_in(w):
    pad = jnp.zeros((w.shape[0], PROJ_W - D_IN), w.dtype)
    return jnp.concatenate([w[:, 0:4096], w[:, 4112:6672], w[:, 4096:4112], w[:, 6672:6688], pad], axis=1)


def _unpermute_in(g):
    return jnp.concatenate([g[:, 0:4096], g[:, 6656:6672], g[:, 4096:6656], g[:, 6672:6688]], axis=1)


def _lane_row(vec, start):
    return jnp.zeros((LANES,), F32).at[start:start + vec.shape[0]].set(vec)


def _cols_from_shards(g):
    return jnp.transpose(g, (1, 0, 2)).reshape(g.shape[1], N_DEV * g.shape[2])


def _cols_to_shards(a):
    return jnp.transpose(a.astype(BF16).reshape(a.shape[0], N_DEV, a.shape[1] // N_DEV), (1, 0, 2))


def _rows_to_shards(a):
    return a.astype(BF16).reshape(N_DEV, a.shape[0] // N_DEV, a.shape[1])


def _local_step(x, tgt, wp_in, rest, p, rest_is_sharded):
    bsz, seq, d = x.shape
    t = bsz * seq
    x2 = x.reshape(t, d)
    tgt2 = tgt.reshape(t, d)
    tm = min(256, seq)
    tm_big = min(512, seq)
    tm_wide = min(256, seq)
    sb = min(512, seq)

    gp = jnp.zeros((SUBLANES, LANES), F32).at[0].set(_lane_row(p["gdn_a_log"], 8)).at[1].set(_lane_row(p["gdn_dt_bias"], 8))
    sp = jnp.zeros((SUBLANES, LANES), F32).at[0].set(_lane_row(p["ssd_a_log"], 16)).at[1].set(_lane_row(p["ssd_dt_bias"], 16))
    dvec = jnp.repeat(p["ssd_d"], SSD_P).reshape(1, d)
    row = lambda v: v.reshape(1, -1)
    pre_mix, post_mix, pre_ffn, post_ffn = (row(p[k]) for k in ("pre_mix_norm", "post_mix_norm", "pre_ffn_norm", "post_ffn_norm"))
    gnw, snw = row(p["gdn_norm_w"]), row(p["ssd_norm_w"])
    gcw, scw, scb, fcw, fcb = p["gdn_conv_w"], p["ssd_conv_w"], row(p["ssd_conv_b"]), p["ffn_conv_w"], row(p["ffn_conv_b"])

    h1, proj = _norm_proj(x2, pre_mix, wp_in)
    b3 = lambda a: a.reshape(bsz, seq, a.shape[-1])
    b2 = lambda a: a.reshape(t, a.shape[-1])
    rows_of = lambda a, lo, n: jnp.transpose(a[:, lo:lo + n].reshape(bsz, seq // CHUNK, CHUNK, n), (0, 1, 3, 2))
    qn, kn, vv, gs, ypre_gdn = _gdn_prep(proj, gcw, gp, seq, tm_big)
    gr = rows_of(gs, 8, GDN_HEADS)
    qn, kn, vv, gs = b3(qn), b3(kn), b3(vv), b3(gs)
    (o_gdn, gdn_st, gdn_ti), gathered = _gdn_chunk_fwd(qn, kn, vv, gs, gr, bsz, seq, sb, list(rest) if rest_is_sharded else [])
    if rest_is_sharded:
        w_out, w_up, w_down = gathered[0].reshape(-1, d), _cols_from_shards(gathered[1]), gathered[2].reshape(-1, d)
    else:
        w_out, w_up, w_down = rest
    o_gdn = b2(o_gdn)
    xs, bc, dtx, acsx, acs, ypre_ssd = _ssd_prep(proj, scw, scb, sp, seq, tm)
    ar = rows_of(acs, 16, SSD_HEADS)
    y_ssd, ssd_st = _ssd_chunk_fwd(b3(xs), b3(bc), b3(dtx), b3(acsx), b3(acs), ar, bsz, seq, sb)
    y_ssd = b2(y_ssd)
    mixin = _gate_norm(o_gdn, y_ssd, xs, proj, gnw, snw, dvec, tm_big)
    mix, x1, h2 = _out_mid(mixin, w_out, x2, post_mix, pre_ffn)
    u_pre = _matmul("mm_up", h2, w_up, "nn", F32)
    act, u = _ffn_act(u_pre, fcw, fcb, seq, tm_wide)
    dy, df, loss_lanes, d_post_ffn = _down_final(act, w_down, x1, tgt2, post_ffn)

    g_down = _matmul("mm_dw_down", act, df, "tn", BF16, tm=1408, tk=2048)
    dact = _matmul("mm_dact", df, w_down, "nt", F32, tn=1408)
    du_pre, d_fcw, d_fcb = _ffn_bwd(u, u_pre, dact, fcw, seq, tm_wide)
    g_up = _matmul("mm_dw_up", h2, du_pre, "tn", BF16, tk=2048)
    dx1, dmix, d_post_mix, d_pre_ffn = _dh2_mid_bwd(du_pre, w_up, x1, mix, dy, post_mix, pre_ffn)
    g_out = _matmul("mm_dw_out", mixin, dmix, "tn", BF16, tk=2048)
    do_gdn, dza, dy_ssd, dxs_d, dzs, d_gnw, d_snw, d_dd = _dmixin_gate_norm_bwd(dmix, w_out, o_gdn, y_ssd, xs, proj, gnw, snw, dvec)
    dyx, dybc, ddt, dacs, d_scb_x, d_scb_bc = _ssd_chunk_bwd(
        b3(xs), b3(bc), b3(dtx), b3(acsx), b3(acs), ar, b3(dy_ssd), ssd_st, b3(ypre_ssd), b3(dxs_d), bsz, seq, sb)
    dyx, dybc, ddt, dacs = b2(dyx), b2(dybc), b2(ddt), b2(dacs)
    d_scb = jnp.concatenate([d_scb_x, d_scb_bc], axis=1)
    riders = [_rows_to_shards(g_out), _cols_to_shards(g_up), _rows_to_shards(g_down)] if rest_is_sharded else []
    dgdn, received = _gdn_chunk_bwd(qn, kn, vv, gs, gr, b3(do_gdn), gdn_st, gdn_ti, b3(ypre_gdn), bsz, seq, min(256, seq), riders)
    if rest_is_sharded:
        g_out, g_up, g_down = received
    dyq, dyk, dyv, dgs = (b2(a) for a in dgdn)
    dsm, d_gp, d_sp = _gates_bwd(proj, dgs, ddt, dacs, gp, sp, tm)
    dproj, d_gcw, d_scw = _assemble_dproj((dyq, dyk, dyv), dza, dzs, (dyx, dybc), dsm, proj, gcw, scw, seq, tm)
    g_in = _matmul("mm_dw_in", h1, dproj, "tn", BF16, tk=2048)
    if rest_is_sharded:
        (dx, d_pre_mix), (g_in,) = _dh1_first_bwd(dproj, wp_in, x2, dx1, pre_mix, [_cols_to_shards(_unpermute_in(g_in))])
    else:
        dx, d_pre_mix = _dh1_first_bwd(dproj, wp_in, x2, dx1, pre_mix, [])

    small = dict(pre_mix_norm=d_pre_mix, ssd_norm_w=d_snw, post_mix_norm=d_post_mix, pre_ffn_norm=d_pre_ffn,
                 post_ffn_norm=d_post_ffn, dd_lanes=d_dd, loss_lanes=loss_lanes, gdn_gates=d_gp, ssd_gates=d_sp,
                 gdn_norm_w=d_gnw, gdn_conv_w=d_gcw[0:4], ssd_conv_w=d_scw[0:4], ssd_conv_b=d_scb,
                 ffn_conv_w=d_fcw[0:3], ffn_conv_b=d_fcb)
    return dx.reshape(bsz, seq, d), g_in, g_out, g_up, g_down, small


def kernel(x, pre_mix_norm, w_in, gdn_conv_w, gdn_a_log, gdn_dt_bias, gdn_norm_w, ssd_conv_w, ssd_conv_b, ssd_a_log, ssd_dt_bias, ssd_d, ssd_norm_w, w_out, post_mix_norm, pre_ffn_norm, w_up, ffn_conv_w, ffn_conv_b, w_down, post_ffn_norm, loss_target, m_pre_mix_norm, m_w_in, m_gdn_conv_w, m_gdn_a_log, m_gdn_dt_bias, m_gdn_norm_w, m_ssd_conv_w, m_ssd_conv_b, m_ssd_a_log, m_ssd_dt_bias, m_ssd_d, m_ssd_norm_w, m_w_out, m_post_mix_norm, m_pre_ffn_norm, m_w_up, m_ffn_conv_w, m_ffn_conv_b, m_w_down, m_post_ffn_norm, v_pre_mix_norm, v_w_in, v_gdn_conv_w, v_gdn_a_log, v_gdn_dt_bias, v_gdn_norm_w, v_ssd_conv_w, v_ssd_conv_b, v_ssd_a_log, v_ssd_dt_bias, v_ssd_d, v_ssd_norm_w, v_w_out, v_post_mix_norm, v_pre_ffn_norm, v_w_up, v_ffn_conv_w, v_ffn_conv_b, v_w_down, v_post_ffn_norm):
    names = ["pre_mix_norm", "w_in", "gdn_conv_w", "gdn_a_log", "gdn_dt_bias", "gdn_norm_w", "ssd_conv_w", "ssd_conv_b",
             "ssd_a_log", "ssd_dt_bias", "ssd_d", "ssd_norm_w", "w_out", "post_mix_norm", "pre_ffn_norm", "w_up",
             "ffn_conv_w", "ffn_conv_b", "w_down", "post_ffn_norm"]
    w_args = [pre_mix_norm, w_in, gdn_conv_w, gdn_a_log, gdn_dt_bias, gdn_norm_w, ssd_conv_w, ssd_conv_b, ssd_a_log, ssd_dt_bias, ssd_d, ssd_norm_w, w_out, post_mix_norm, pre_ffn_norm, w_up, ffn_conv_w, ffn_conv_b, w_down, post_ffn_norm]
    m_args = [m_pre_mix_norm, m_w_in, m_gdn_conv_w, m_gdn_a_log, m_gdn_dt_bias, m_gdn_norm_w, m_ssd_conv_w, m_ssd_conv_b, m_ssd_a_log, m_ssd_dt_bias, m_ssd_d, m_ssd_norm_w, m_w_out, m_post_mix_norm, m_pre_ffn_norm, m_w_up, m_ffn_conv_w, m_ffn_conv_b, m_w_down, m_post_ffn_norm]
    v_args = [v_pre_mix_norm, v_w_in, v_gdn_conv_w, v_gdn_a_log, v_gdn_dt_bias, v_gdn_norm_w, v_ssd_conv_w, v_ssd_conv_b, v_ssd_a_log, v_ssd_dt_bias, v_ssd_d, v_ssd_norm_w, v_w_out, v_post_mix_norm, v_pre_ffn_norm, v_w_up, v_ffn_conv_w, v_ffn_conv_b, v_w_down, v_post_ffn_norm]
    w = {k: a[0] for k, a in zip(names, w_args)}
    m = {k: a[0] for k, a in zip(names, m_args)}
    v = {k: a[0] for k, a in zip(names, v_args)}
    idx = 4 * lax.axis_index("x") + 2 * lax.axis_index("y") + lax.axis_index("c")
    big = ("w_in", "w_out", "w_up", "w_down")
    conv = ("gdn_conv_w", "ssd_conv_w", "ffn_conv_w")

    conv_local = jnp.concatenate([jnp.pad(w[k], ((0, 4 - w[k].shape[0]), (0, 0))) for k in conv], axis=1)
    g_in, g_conv = _gather_two_level("gather_weights", [w["w_in"].astype(BF16), conv_local])
    wp_in = _permute_in(_cols_from_shards(g_in))
    p = {k: w[k] for k in names if k not in big and k not in conv}
    off = 0
    for k in conv:
        cw = w[k].shape[1]
        p[k] = jnp.transpose(g_conv[:, :w[k].shape[0], off:off + cw], (1, 0, 2)).reshape(w[k].shape[0], N_DEV * cw)
        off += cw

    rest = tuple(w[k].astype(BF16) for k in ("w_out", "w_up", "w_down"))
    dx, p_in, p_out, p_up, p_down, small = _local_step(x, loss_target, wp_in, rest, p, True)

    gate_row = jnp.concatenate([small["gdn_gates"][0], small["gdn_gates"][1], small["ssd_gates"][0], small["ssd_gates"][1],
                                small["gdn_norm_w"][0], jnp.zeros((D_MODEL - 5 * LANES,), F32)]).reshape(1, D_MODEL)
    pack = _pack([small["pre_mix_norm"], small["ssd_norm_w"], small["post_mix_norm"], small["pre_ffn_norm"],
                  small["post_ffn_norm"], small["dd_lanes"], small["loss_lanes"], gate_row,
                  small["gdn_conv_w"], small["ssd_conv_w"], jnp.pad(small["ssd_conv_b"], ((0, 0), (0, 512))),
                  jnp.pad(small["ffn_conv_w"].reshape(-1), (0, 17 * D_MODEL - 3 * 2 * D_FF)),
                  jnp.pad(small["ffn_conv_b"], ((0, 0), (0, 512)))], SMALL_ROWS)
    (pack_all,) = _gather_two_level("gather_small", [pack])
    ssum, extra = _small_sum(pack_all)

    grads, deltas, new_m, new_v = {}, {}, {}, {}
    for k, parts in (("w_in", p_in), ("w_out", p_out), ("w_up", p_up), ("w_down", p_down)):
        grads[k], deltas[k], new_m[k], new_v[k] = _adam_big("adam_" + k, parts, w[k], m[k], v[k], 256)

    flat = ssum.reshape(-1)
    gate = ssum[7]
    sg = dict(pre_mix_norm=ssum[0], ssd_norm_w=ssum[1], post_mix_norm=ssum[2], pre_ffn_norm=ssum[3], post_ffn_norm=ssum[4],
              gdn_a_log=gate[8:16], gdn_dt_bias=gate[LANES + 8:LANES + 16], ssd_a_log=gate[2 * LANES + 16:2 * LANES + 32],
              ssd_dt_bias=gate[3 * LANES + 16:3 * LANES + 32], gdn_norm_w=gate[4 * LANES:5 * LANES], ssd_d=extra[0, 0:SSD_HEADS])
    o = 8 * D_MODEL
    full_gcw = flat[o:o + 4 * 3072].reshape(4, 3072)
    o += 12 * D_MODEL
    full_scw = flat[o:o + 4 * 1536].reshape(4, 1536)
    o += 6 * D_MODEL
    sg["ssd_conv_b"] = flat[o:o + 1536]
    o += 2 * D_MODEL
    full_fcw = flat[o:o + 3 * 2 * D_FF].reshape(3, 2 * D_FF)
    o += 17 * D_MODEL
    sg["ffn_conv_b"] = flat[o:o + 2 * D_FF]
    for k, full in (("gdn_conv_w", full_gcw), ("ssd_conv_w", full_scw), ("ffn_conv_w", full_fcw)):
        cw = w[k].shape[1]
        sg[k] = lax.dynamic_slice_in_dim(full, idx * cw, cw, axis=1)
    small_names = [k for k in names if k not in big]
    rows = 24
    gpk = _pack([sg[k] for k in small_names], rows)
    dpk, mpk, vpk = _adam_small(gpk, _pack([w[k] for k in small_names], rows), _pack([m[k] for k in small_names], rows),
                                _pack([v[k] for k in small_names], rows))
    shapes = [w[k].shape for k in small_names]
    for k, g_, d_, m_, v_ in zip(small_names, _unpack(gpk, shapes), _unpack(dpk, shapes), _unpack(mpk, shapes), _unpack(vpk, shapes)):
        grads[k], deltas[k], new_m[k], new_v[k] = g_, d_, m_, v_

    loss = extra[1, 0]
    lead = lambda a: a[None]
    return (loss, dx, *[lead(grads[k]) for k in names], *[lead(deltas[k]) for k in names],
            *[lead(new_m[k]) for k in names], *[lead(new_v[k]) for k in names])
```

```python
import functools

import jax
import jax.numpy as jnp
from jax import lax
from jax.experimental import pallas as pl
from jax.experimental.pallas import tpu as pltpu

F32 = jnp.float32
BF16 = jnp.bfloat16
MXU_DTYPE = jnp.bfloat16
HIGHEST = lax.Precision.HIGHEST
VMEM_LIMIT_V7X = 48 * 1024 * 1024
SUBLANES = 8
LANES = 128

D_MODEL = 1024
GDN_HEADS = 8
GDN_DK = 128
SSD_HEADS = 16
SSD_P = 64
SSD_GROUPS = 2
SSD_HPG = 8
SSD_N = 128
CHUNK = 128
D_FF = 2816
EPS = 1e-6
N_DEV = 8
PROJ_W = 7168
SMALL_CB = 52
D_IN = 6688

ADAM_LR = 0.001
ADAM_B1 = 0.9
ADAM_B2 = 0.999
ADAM_EPS = 1e-08
ADAM_WD = 0.01
ADAM_STEP = 10

NN = (((1,), (0,)), ((), ()))
NT = (((1,), (1,)), ((), ()))
TN = (((0,), (0,)), ((), ()))


def _pcall(body, **kw):
    return pl.pallas_call(body, **kw)


def _mm(a, b, dims=NN):
    return lax.dot_general(a.astype(MXU_DTYPE), b.astype(MXU_DTYPE), dims, preferred_element_type=F32)


def _mmx(a, b, dims=NN):
    return lax.dot_general(a, b, dims, precision=HIGHEST, preferred_element_type=F32)


def _split(a):
    hi = a.astype(MXU_DTYPE)
    return hi, (a - hi.astype(F32)).astype(MXU_DTYPE)


def _mm3(a, b, dims=NN):
    (ah, al), (bh, bl) = _split(a), _split(b)
    dot = lambda p, q: lax.dot_general(p, q, dims, preferred_element_type=F32)
    return dot(ah, bh) + (dot(ah, bl) + dot(al, bh))


def _mmsel(a, sel, dims=NN, terms=2):
    s = sel.astype(MXU_DTYPE)
    out = None
    for _ in range(terms):
        part = a.astype(MXU_DTYPE)
        a = a - part.astype(F32)
        prod = lax.dot_general(part, s, dims, preferred_element_type=F32)
        out = prod if out is None else out + prod
    return out


def _sigmoid(x):
    return 0.5 * jnp.tanh(0.5 * x) + 0.5


def _softplus(x):
    return jnp.maximum(x, 0.0) + jnp.log(1.0 + jnp.exp(-jnp.abs(x)))


def _dsilu(x, s):
    return s * (1.0 + x * (1.0 - s))


def _rowsum(x):
    return jnp.sum(x, axis=1, keepdims=True)


def _colsum(x):
    return jnp.sum(x, axis=0, keepdims=True)


def _pick(dim, pref):
    if dim <= pref:
        return dim
    best = None
    t = LANES
    while t <= pref:
        if dim % t == 0:
            best = t
        t += LANES
    return dim if best is None else best


def _params(sem):
    return pltpu.CompilerParams(dimension_semantics=sem, vmem_limit_bytes=VMEM_LIMIT_V7X)


def _matmul(name, a, b, mode, out_dtype, tm=1024, tn=1024, tk=1024):
    if mode == "nn":
        (m, k), (_, n) = a.shape, b.shape
    elif mode == "nt":
        (m, k), (n, _) = a.shape, b.shape
    else:
        (k, m), (_, n) = a.shape, b.shape
    tm, tn, tk = _pick(m, tm), _pick(n, tn), _pick(k, tk)
    nk = k // tk
    if mode == "tn":
        a_spec = pl.BlockSpec((tk, tm), lambda i, j, kk: (kk, i))
    else:
        a_spec = pl.BlockSpec((tm, tk), lambda i, j, kk: (i, kk))
    if mode == "nt":
        b_spec = pl.BlockSpec((tn, tk), lambda i, j, kk: (j, kk))
    else:
        b_spec = pl.BlockSpec((tk, tn), lambda i, j, kk: (kk, j))
    dims = {"nn": NN, "nt": NT, "tn": TN}[mode]

    def body(a_ref, b_ref, o_ref, *acc):
        if nk == 1:
            o_ref[...] = _mm(a_ref[...], b_ref[...], dims).astype(out_dtype)
            return
        kk = pl.program_id(2)

        @pl.when(kk == 0)
        def _():
            acc[0][...] = jnp.zeros_like(acc[0])

        acc[0][...] += _mm(a_ref[...], b_ref[...], dims)

        @pl.when(kk == nk - 1)
        def _():
            o_ref[...] = acc[0][...].astype(out_dtype)

    return _pcall(
        body, name=name, grid=(m // tm, n // tn, nk),
        in_specs=[a_spec, b_spec],
        out_specs=pl.BlockSpec((tm, tn), lambda i, j, kk: (i, j)),
        out_shape=jax.ShapeDtypeStruct((m, n), out_dtype),
        scratch_shapes=[pltpu.VMEM((tm, tn), F32)] if nk > 1 else [],
        compiler_params=_params(("parallel", "parallel", "arbitrary")),
    )(a, b)


def _matmul_rows(name, a, b, mode, epilogue, row_ins, full_ins, outs, accs=(), tm=512, tk=1024, scatter_riders=()):
    if mode == "nn":
        (m, k), (_, n) = a.shape, b.shape
    else:
        (m, k), (n, _) = a.shape, b.shape
    tm, tk = _pick(m, tm), _pick(k, tk)
    nk = k // tk
    a_spec = pl.BlockSpec((tm, tk), lambda i, kk: (i, kk))
    b_spec = pl.BlockSpec((n, tk), lambda i, kk: (0, kk)) if mode == "nt" else pl.BlockSpec((tk, n), lambda i, kk: (kk, 0))
    dims = NT if mode == "nt" else NN
    n_row, n_full, n_out, n_acc = len(row_ins), len(full_ins), len(outs), len(accs)

    def body(a_ref, b_ref, *rest):
        ins = rest[:n_row + n_full]
        out_refs = rest[n_row + n_full:n_row + n_full + n_out]
        acc_refs = rest[n_row + n_full + n_out:n_row + n_full + n_out + n_acc]
        prod_scr = rest[-1]
        i, kk = pl.program_id(0), pl.program_id(1)

        if n_acc:
            @pl.when((i == 0) & (kk == 0))
            def _():
                for r in acc_refs:
                    r[...] = jnp.zeros_like(r)

        if nk == 1:
            epilogue(_mm(a_ref[...], b_ref[...], dims), *ins, *out_refs, *acc_refs)
            return

        @pl.when(kk == 0)
        def _():
            prod_scr[...] = jnp.zeros_like(prod_scr)

        prod_scr[...] += _mm(a_ref[...], b_ref[...], dims)

        @pl.when(kk == nk - 1)
        def _():
            epilogue(prod_scr[...], *ins, *out_refs, *acc_refs)

    grid = (m // tm, nk)
    riders = list(scatter_riders)
    n_in = 2 + n_row + n_full
    any_spec, rider_shapes, rider_sems, wrap = _riding_exchange(riders, True, n_in, n_out + n_acc, grid)
    row_ins = [r if isinstance(r, tuple) else (r, r.shape[1], 0) for r in row_ins]
    in_specs = [a_spec, b_spec] + [pl.BlockSpec((tm, w), lambda i, kk, cb=cb: (i, cb)) for _, w, cb in row_ins]
    in_specs += [pl.BlockSpec(f.shape, lambda i, kk, nd=f.ndim: (0,) * nd) for f in full_ins]
    row_ins = [r for r, _, _ in row_ins]
    out_specs = [pl.BlockSpec((tm, w), lambda i, kk: (i, 0)) for w, _ in outs]
    out_specs += [pl.BlockSpec(s, lambda i, kk: (0, 0)) for s in accs]
    out_shape = [jax.ShapeDtypeStruct((m, w), dt) for w, dt in outs] + [jax.ShapeDtypeStruct(s, F32) for s in accs]
    res = _pcall(
        wrap(body), name=name, grid=grid,
        in_specs=in_specs + any_spec, out_specs=out_specs + any_spec, out_shape=out_shape + rider_shapes,
        scratch_shapes=[pltpu.VMEM((tm, n), F32)] + rider_sems,
        compiler_params=_params(("arbitrary", "arbitrary")),
    )(a, b, *row_ins, *full_ins, *riders)
    return (res[:n_out + n_acc], res[n_out + n_acc:]) if riders else res


def _rowwise(name, body, n_rows, tm, ins, outs, accs=()):
    arrays, in_specs = [], []
    last8 = n_rows // SUBLANES - 1
    per = tm // SUBLANES
    for spec in ins:
        kind, arr = spec[0], spec[1]
        if kind == "full":
            in_specs.append(pl.BlockSpec(arr.shape, lambda i, nd=arr.ndim: (0,) * nd))
        else:
            w, cb = spec[2], spec[3]
            if kind == "row":
                in_specs.append(pl.BlockSpec((tm, w), lambda i, cb=cb: (i, cb)))
            elif kind == "prev":
                in_specs.append(pl.BlockSpec((SUBLANES, w), lambda i, cb=cb: (jnp.maximum(i * per - 1, 0), cb)))
            else:
                in_specs.append(pl.BlockSpec((SUBLANES, w), lambda i, cb=cb: (jnp.minimum((i + 1) * per, last8), cb)))
        arrays.append(arr)
    out_shape = [jax.ShapeDtypeStruct((n_rows, w), dt) for (w, dt) in outs]
    out_shape += [jax.ShapeDtypeStruct(s, F32) for s in accs]
    out_specs = [pl.BlockSpec((tm, w), lambda i: (i, 0)) for (w, _) in outs]
    out_specs += [pl.BlockSpec(s, lambda i: (0, 0)) for s in accs]
    n_io = len(ins) + len(outs)

    def kern(*refs):
        i = pl.program_id(0)
        if accs:
            @pl.when(i == 0)
            def _():
                for r in refs[n_io:]:
                    r[...] = jnp.zeros_like(r)
        body(i, *refs)

    res = _pcall(
        kern, name=name, grid=(n_rows // tm,), in_specs=in_specs, out_specs=out_specs, out_shape=out_shape,
        compiler_params=_params(("arbitrary",)),
    )(*arrays)
    return res


def _shift_down(x, halo, j):
    r = pltpu.roll(x, j, 0)
    hr = pltpu.roll(halo, j, 0)
    rows = lax.broadcasted_iota(jnp.int32, (SUBLANES, x.shape[1]), 0)
    top = jnp.where(rows < j, hr, r[0:SUBLANES])
    return jnp.concatenate([top, r[SUBLANES:]], axis=0)


def _shift_up(x, halo, j):
    tm = x.shape[0]
    r = pltpu.roll(x, tm - j, 0)
    hr = pltpu.roll(halo, SUBLANES - j, 0)
    rows = lax.broadcasted_iota(jnp.int32, (SUBLANES, x.shape[1]), 0)
    bot = jnp.where(rows >= SUBLANES - j, hr, r[tm - SUBLANES:])
    return jnp.concatenate([r[:tm - SUBLANES], bot], axis=0)


def _conv_taps(x, halo, kw):
    return [x if kw - 1 - k == 0 else _shift_down(x, halo, kw - 1 - k) for k in range(kw)]


def _conv(taps, w):
    y = taps[0] * w[0:1]
    for k in range(1, len(taps)):
        y = y + taps[k] * w[k:k + 1]
    return y


def _rms(x, width):
    r = lax.rsqrt(jnp.sum(x * x, axis=-1, keepdims=True) * (1.0 / width) + EPS)
    return x * r, r


def _rms_bwd(xh, r, dxh, width):
    return r * (dxh - xh * (jnp.sum(dxh * xh, axis=-1, keepdims=True) * (1.0 / width)))


def _seq_flags(i, seq, tm):
    nps = seq // tm
    pos = i % nps
    return jnp.where(pos == 0, 0.0, 1.0), jnp.where(pos == nps - 1, 0.0, 1.0)


def _norm_proj(x, w, wp, tm=1024, tn=1024):
    t, d = x.shape
    n = wp.shape[1]
    tm, tn = _pick(t, tm), _pick(n, tn)

    def body(x_ref, w_ref, b_ref, h_ref, o_ref, h_scr):
        @pl.when(pl.program_id(1) == 0)
        def _():
            xh, _ = _rms(x_ref[...], d)
            h = (xh * w_ref[...]).astype(BF16)
            h_scr[...] = h
            h_ref[...] = h

        o_ref[...] = _mm(h_scr[...], b_ref[...])

    return _pcall(
        body, name="mm_norm_proj", grid=(t // tm, n // tn),
        in_specs=[pl.BlockSpec((tm, d), lambda i, j: (i, 0)), pl.BlockSpec((1, d), lambda i, j: (0, 0)),
                  pl.BlockSpec((d, tn), lambda i, j: (0, j))],
        out_specs=[pl.BlockSpec((tm, d), lambda i, j: (i, 0)), pl.BlockSpec((tm, tn), lambda i, j: (i, j))],
        out_shape=[jax.ShapeDtypeStruct((t, d), BF16), jax.ShapeDtypeStruct((t, n), F32)],
        scratch_shapes=[pltpu.VMEM((tm, d), BF16)],
        compiler_params=_params(("parallel", "arbitrary")),
    )(x, w, wp)


def _gdn_prep(proj, cw, gp, seq, tm):
    t = proj.shape[0]
    d = D_MODEL

    def body(i, q_ref, qh_ref, k_ref, kh_ref, v_ref, vh_ref, sm_ref, cw_ref, gp_ref, qn_ref, kn_ref, vv_ref, gs_ref, ypre_ref):
        keep, _ = _seq_flags(i, seq, tm)
        for x_ref, h_ref, o_ref, off, scale in ((q_ref, qh_ref, qn_ref, 0, GDN_DK ** -0.5),
                                               (k_ref, kh_ref, kn_ref, d, 1.0), (v_ref, vh_ref, vv_ref, 2 * d, None)):
            y = _conv(_conv_taps(x_ref[...], h_ref[...] * keep, 4), cw_ref[:, off:off + d])
            ypre_ref[:, off:off + d] = y
            a = y * _sigmoid(y)
            if scale is None:
                o_ref[...] = a
            else:
                for hh in range(GDN_HEADS):
                    s = a[:, hh * GDN_DK:(hh + 1) * GDN_DK]
                    n = lax.rsqrt(_rowsum(s * s) + EPS)
                    o_ref[:, hh * GDN_DK:(hh + 1) * GDN_DK] = s * (n * scale)
        sm = sm_ref[...]
        lane = lax.broadcasted_iota(jnp.int32, sm.shape, 1)
        beta = _sigmoid(sm)
        g = jnp.where((lane >= 8) & (lane < 16), -jnp.exp(gp_ref[0:1, :]) * _softplus(sm + gp_ref[1:2, :]), 0.0)
        gs_ref[...] = jnp.where(lane < 8, beta, _mmx(_block_tri(tm, False), g))

    ins = []
    for cb in range(3):
        ins += [("row", proj, d, cb), ("prev", proj, d, cb)]
    ins += [("row", proj, LANES, SMALL_CB), ("full", cw), ("full", gp)]
    return _rowwise("gdn_prep", body, t, tm, ins, [(d, F32), (d, F32), (d, F32), (LANES, F32), (3 * d, F32)])


def _block_tri(tm, upper):
    ri = lax.broadcasted_iota(jnp.int32, (tm, tm), 0)
    ci = lax.broadcasted_iota(jnp.int32, (tm, tm), 1)
    tri = (ri <= ci) if upper else (ri >= ci)
    return (tri & ((ri // CHUNK) == (ci // CHUNK))).astype(F32)


def _chunk_consts():
    row = lax.broadcasted_iota(jnp.int32, (CHUNK, CHUNK), 0)
    col = lax.broadcasted_iota(jnp.int32, (CHUNK, CHUNK), 1)
    return dict(
        tril=row >= col, strict=row > col, eye=(row == col).astype(F32),
        lane=lax.broadcasted_iota(jnp.int32, (CHUNK, LANES), 1),
        row1=lax.broadcasted_iota(jnp.int32, (CHUNK, 1), 0),
        ones=jnp.ones((CHUNK, LANES), F32))


def _hmap(fn, *lists):
    return [fn(*a) for a in zip(*lists)]


def _tri_inv(nmats, eye):
    levels = CHUNK.bit_length() - 2
    x = [eye - n for n in nmats]
    p = _hmap(_mm3, nmats, nmats)
    for lvl in range(levels):
        x = _hmap(lambda xi, pi: xi + _mm3(xi, pi), x, p)
        if lvl < levels - 1:
            p = _hmap(_mm3, p, p)
    return x


def _gdn_gates(gs, gc_row, h, c):
    beta = _rowsum(jnp.where(c["lane"] == h, gs, 0.0))
    gc = _rowsum(jnp.where(c["lane"] == h + 8, gs, 0.0))
    dc = jnp.exp(jnp.where(c["tril"], gc - gc_row, -1e30))
    gl = gc[CHUNK - 1:CHUNK, :]
    return beta, dc, jnp.exp(gc), jnp.exp(gl), jnp.exp(gl - gc)


GDN_HB = GDN_HEADS


def _gdn_specs(seq, sb, hb, backward):
    assert hb == GDN_HEADS
    nsb = seq // sb
    ncb = sb // CHUNK
    order = (lambda j: nsb - 1 - j) if backward else (lambda j: j)
    specs = dict(
        wide=lambda: pl.BlockSpec((1, sb, hb * GDN_DK), lambda b, h, j: (b, order(j), h)),
        gs=lambda: pl.BlockSpec((1, sb, LANES), lambda b, h, j: (b, order(j), 0)),
        gr=pl.BlockSpec((1, ncb, GDN_HEADS, CHUNK), lambda b, h, j: (b, order(j), 0, 0)),
        st=pl.BlockSpec((1, hb, ncb * GDN_DK, GDN_DK), lambda b, h, j: (b, h, order(j), 0)),
        ti=pl.BlockSpec((1, hb, sb, CHUNK), lambda b, h, j: (b, h, order(j), 0)))
    return nsb, ncb, specs


def _riding_exchange(arrays, scatter, n_in, n_out, grid):
    n = len(arrays)
    if n == 0:
        return [], [], [], lambda body: body
    any_spec = [pl.BlockSpec(memory_space=pl.ANY)] * n

    def wrap(body):
        def wrapped(*refs):
            ins = refs[n_in:n_in + n]
            outs = refs[n_in + n + n_out:n_in + 2 * n + n_out]
            sems = refs[len(refs) - 3:]
            pid = [pl.program_id(a) for a in range(len(grid))]
            first = functools.reduce(lambda a, b: a & b, [p == 0 for p in pid])
            last = functools.reduce(lambda a, b: a & b, [p == g - 1 for p, g in zip(pid, grid)])

            @pl.when(first)
            def _():
                _exchange_phase(ins, outs, sems, scatter, start=True)

            body(*refs[:n_in], *refs[n_in + n:n_in + n + n_out], *refs[n_in + 2 * n + n_out:len(refs) - 3])

            @pl.when(last)
            def _():
                _exchange_phase(ins, outs, sems, scatter, start=False)

        return wrapped

    return any_spec, _exchange_out_shapes(arrays, scatter), _exchange_sems(n), wrap


def _gdn_chunk_fwd(qn, kn, vv, gs, gr, bsz, seq, sb, riders):
    hb = GDN_HB
    nsb, ncb, sp = _gdn_specs(seq, sb, hb, False)
    grid = (bsz, GDN_HEADS // hb, nsb)
    any_spec, rider_shapes, rider_sems, wrap = _riding_exchange(riders, False, 5, 3, grid)

    def body(q_ref, k_ref, v_ref, gs_ref, gr_ref, o_ref, st_ref, ti_ref, s_scr):
        @pl.when(pl.program_id(2) == 0)
        def _():
            s_scr[...] = jnp.zeros_like(s_scr)

        c = _chunk_consts()

        def chunk(n, carry):
            r = pl.ds(pl.multiple_of(n * CHUNK, CHUNK), CHUNK)
            rs = pl.ds(pl.multiple_of(n * GDN_DK, GDN_DK), GDN_DK)
            gsv = gs_ref[0, r, :]
            heads = list(range(hb))
            sls = [slice(ih * GDN_DK, (ih + 1) * GDN_DK) for ih in heads]
            q = [q_ref[0, r, sl] for sl in sls]
            k = [k_ref[0, r, sl] for sl in sls]
            v = [v_ref[0, r, sl] for sl in sls]
            beta, dc, eg, egl, ekd = zip(*[
                _gdn_gates(gsv, gr_ref[0, n, pl.ds(ih, 1), :], ih, c) for ih in heads])
            kb = _hmap(lambda a, b: a * b, k, beta)
            amat = _hmap(lambda a, b, d_: jnp.where(c["strict"], _mm(a, b, NT) * d_, 0.0), kb, k, dc)
            tinv = _tri_inv(amat, c["eye"])
            u = _hmap(lambda t_, a, b: _mm3(t_, a * b), tinv, v, beta)
            w = _hmap(lambda t_, a, b: _mm3(t_, a * b), tinv, kb, eg)
            qk = _hmap(lambda a, b, d_: _mm(a, b, NT) * d_, q, k, dc)
            s = [s_scr[ih] for ih in heads]
            v_new = _hmap(lambda a, b, s_: a - _mm(b, s_), u, w, s)
            o = _hmap(lambda a, e, s_, qk_, vn: _mm(a * e, s_) + _mm(qk_, vn), q, eg, s, qk, v_new)
            s_new = _hmap(lambda s_, e, a, f, vn: s_ * e + _mm(a * f, vn, TN), s, egl, k, ekd, v_new)
            for ih in heads:
                o_ref[0, r, sls[ih]] = o[ih]
                st_ref[0, ih, rs, :] = s[ih]
                ti_ref[0, ih, r, :] = tinv[ih]
                s_scr[ih] = s_new[ih]
            return carry

        lax.fori_loop(0, ncb, chunk, 0)

    t3 = (bsz, seq, D_MODEL)
    res = _pcall(
        wrap(body), name="gdn_chunk_fwd", grid=grid,
        in_specs=[sp["wide"](), sp["wide"](), sp["wide"](), sp["gs"](), sp["gr"]] + any_spec,
        out_specs=[sp["wide"](), sp["st"], sp["ti"]] + any_spec,
        out_shape=[jax.ShapeDtypeStruct(t3, F32),
                   jax.ShapeDtypeStruct((bsz, GDN_HEADS, (seq // CHUNK) * GDN_DK, GDN_DK), F32),
                   jax.ShapeDtypeStruct((bsz, GDN_HEADS, seq, CHUNK), F32)] + rider_shapes,
        scratch_shapes=[pltpu.VMEM((hb, GDN_DK, GDN_DK), F32)] + rider_sems,
        compiler_params=_params(("arbitrary", "arbitrary", "arbitrary")),
    )(qn, kn, vv, gs, gr, *riders)
    return res[:3], res[3:]


def _ssd_prep(proj, cw, cb, sp, seq, tm):
    t = proj.shape[0]
    d = D_MODEL
    ssd_w = SSD_HEADS * SSD_P

    def body(i, x_ref, xh_ref, bc_ref, bch_ref, sm_ref, cw_ref, cb_ref, sp_ref, xs_ref, bco_ref, dtx_ref, acsx_ref, acs_ref, ypre_ref):
        keep, _ = _seq_flags(i, seq, tm)
        y = _conv(_conv_taps(x_ref[...], xh_ref[...] * keep, 4), cw_ref[:, 0:d]) + cb_ref[:, 0:d]
        ypre_ref[:, 0:d] = y
        xs_ref[...] = y * _sigmoid(y)
        y = _conv(_conv_taps(bc_ref[...], bch_ref[...] * keep, 4), cw_ref[:, d:d + 512]) + cb_ref[:, d:d + 512]
        ypre_ref[:, d:d + 512] = y
        bco_ref[...] = y * _sigmoid(y)
        sm = sm_ref[...]
        lane = lax.broadcasted_iota(jnp.int32, sm.shape, 1)
        valid = (lane >= 16) & (lane < 32)
        dt = jnp.where(valid, _softplus(sm + sp_ref[1:2, :]), 0.0)
        adt = dt * (-jnp.exp(sp_ref[0:1, :]))
        acs = _mmx(_block_tri(tm, False), adt)
        l64 = lax.broadcasted_iota(jnp.int32, (LANES, ssd_w), 0)
        d64 = lax.broadcasted_iota(jnp.int32, (LANES, ssd_w), 1)
        e64 = (l64 - 16 == d64 // SSD_P).astype(F32)
        dtx_ref[...] = _mmsel(dt, e64, terms=3)
        acsx_ref[...] = _mmsel(acs, e64, terms=3)
        acs_ref[...] = acs

    ins = [("row", proj, d, 5), ("prev", proj, d, 5), ("row", proj, 512, 12), ("prev", proj, 512, 12),
           ("row", proj, LANES, SMALL_CB), ("full", cw), ("full", cb), ("full", sp)]
    return _rowwise("ssd_prep", body, t, tm, ins,
                    [(d, F32), (512, F32), (ssd_w, F32), (ssd_w, F32), (LANES, F32), (d + 512, F32)])


SSD_GW = SSD_HPG * SSD_P


def _ssd_head(acs, ar_ref, n, head, cbm, c):
    col = _rowsum(jnp.where(c["lane"] == head + 16, acs, 0.0))
    lm = jnp.exp(jnp.where(c["tril"], col - ar_ref[0, n, pl.ds(head, 1), :], -1e30))
    return lm, cbm * lm


def _ssd_specs(seq, sb):
    nsb = seq // sb
    ncb = sb // CHUNK
    def specs(order):
        return dict(
            wide=lambda: pl.BlockSpec((1, sb, SSD_HEADS * SSD_P), lambda b, j: (b, order(j), 0)),
            bc=lambda: pl.BlockSpec((1, sb, 2 * SSD_GROUPS * SSD_N), lambda b, j: (b, order(j), 0)),
            small=lambda: pl.BlockSpec((1, sb, LANES), lambda b, j: (b, order(j), 0)),
            ar=pl.BlockSpec((1, ncb, SSD_HEADS, CHUNK), lambda b, j: (b, order(j), 0, 0)),
            st=pl.BlockSpec((1, ncb * SSD_N, SSD_HEADS * SSD_P), lambda b, j: (b, order(j), 0)))
    return nsb, ncb, specs(lambda j: j), specs(lambda j: nsb - 1 - j)


def _ssd_chunk_fwd(xs, bc, dtx, acsx, acs, ar, bsz, seq, sb):
    nsb, ncb, sp, _ = _ssd_specs(seq, sb)

    def body(x_ref, dtx_ref, ax_ref, bc_ref, acs_ref, ar_ref, y_ref, sts_ref, st_scr):
        @pl.when(pl.program_id(1) == 0)
        def _():
            st_scr[...] = jnp.zeros_like(st_scr)

        c = _chunk_consts()
        lane5 = lax.broadcasted_iota(jnp.int32, (CHUNK, SSD_GW), 1) // SSD_P

        def chunk(n, carry):
            r = pl.ds(pl.multiple_of(n * CHUNK, CHUNK), CHUNK)
            rs = pl.ds(pl.multiple_of(n * SSD_N, SSD_N), SSD_N)
            acsv = acs_ref[0, r, :]
            for g in range(SSD_GROUPS):
                gl = slice(g * SSD_GW, (g + 1) * SSD_GW)
                x, dt, ax = x_ref[0, r, gl], dtx_ref[0, r, gl], ax_ref[0, r, gl]
                bm = bc_ref[0, r, g * SSD_N:(g + 1) * SSD_N]
                cm = bc_ref[0, r, (SSD_GROUPS + g) * SSD_N:(SSD_GROUPS + g + 1) * SSD_N]
                xdt = x * dt
                cbm = _mm(cm, bm, NT)
                al = ax[CHUNK - 1:CHUNK, :]
                st = st_scr[:, gl]
                y = _mm(cm, st) * jnp.exp(ax)
                for hh in range(SSD_HPG):
                    _, gm = _ssd_head(acsv, ar_ref, n, g * SSD_HPG + hh, cbm, c)
                    y = y + _mm(gm, jnp.where(lane5 == hh, xdt, 0.0))
                y_ref[0, r, gl] = y
                sts_ref[0, rs, gl] = st
                st_scr[:, gl] = st * jnp.exp(al) + _mm(bm, xdt * jnp.exp(al - ax), TN)
            return carry

        lax.fori_loop(0, ncb, chunk, 0)

    return _pcall(
        body, name="ssd_chunk_fwd", grid=(bsz, nsb),
        in_specs=[sp["wide"](), sp["wide"](), sp["wide"](), sp["bc"](), sp["small"](), sp["ar"]],
        out_specs=[sp["wide"](), sp["st"]],
        out_shape=[jax.ShapeDtypeStruct((bsz, seq, SSD_HEADS * SSD_P), F32),
                   jax.ShapeDtypeStruct((bsz, (seq // CHUNK) * SSD_N, SSD_HEADS * SSD_P), F32)],
        scratch_shapes=[pltpu.VMEM((SSD_N, SSD_HEADS * SSD_P), F32)],
        compiler_params=_params(("parallel", "arbitrary")),
    )(xs, dtx, acsx, bc, acs, ar)


def _gate_norm(o_gdn, y_ssd, xs, proj, gnw, snw, dvec, tm):
    t = o_gdn.shape[0]
    d = D_MODEL

    def body(i, o_ref, za_ref, y_ref, xs_ref, zs_ref, gnw_ref, snw_ref, dv_ref, out_ref):
        for hh in range(GDN_HEADS):
            sl = slice(hh * GDN_DK, (hh + 1) * GDN_DK)
            oh, _ = _rms(o_ref[:, sl], GDN_DK)
            z = za_ref[:, sl]
            out_ref[:, sl] = (oh * gnw_ref[...] * (z * _sigmoid(z))).astype(BF16)
        zs = zs_ref[...]
        yg = (y_ref[...] + dv_ref[...] * xs_ref[...]) * (zs * _sigmoid(zs))
        for g in range(SSD_GROUPS):
            sl = slice(g * 512, (g + 1) * 512)
            yh, _ = _rms(yg[:, sl], 512)
            out_ref[:, d + g * 512:d + (g + 1) * 512] = (yh * snw_ref[:, sl]).astype(BF16)

    ins = [("row", o_gdn, d, 0), ("row", proj, d, 3), ("row", y_ssd, d, 0), ("row", xs, d, 0), ("row", proj, d, 4),
           ("full", gnw), ("full", snw), ("full", dvec)]
    return _rowwise("gate_norm", body, t, tm, ins, [(2 * d, BF16)])[0]


def _out_mid(mixin, w_out, x, pmw, pfw):
    d = D_MODEL

    def epilogue(mix, x_ref, pmw_ref, pfw_ref, mix_ref, x1_ref, h2_ref):
        mix_ref[...] = mix
        mh, _ = _rms(mix, d)
        x1 = x_ref[...] + mh * pmw_ref[...]
        x1_ref[...] = x1
        xh, _ = _rms(x1, d)
        h2_ref[...] = (xh * pfw_ref[...]).astype(BF16)

    return _matmul_rows("mm_out_mid", mixin, w_out, "nn", epilogue, [x], [pmw, pfw], [(d, F32), (d, F32), (d, BF16)],
                        tk=2 * d)


def _ffn_act(u_pre, cw, cb, seq, tm):
    t = u_pre.shape[0]

    def body(i, ug_ref, ugh_ref, uu_ref, uuh_ref, cw_ref, cb_ref, act_ref, u_ref):
        keep, _ = _seq_flags(i, seq, tm)
        gate = _conv(_conv_taps(ug_ref[...], ugh_ref[...] * keep, 3), cw_ref[:, 0:D_FF]) + cb_ref[:, 0:D_FF]
        up = _conv(_conv_taps(uu_ref[...], uuh_ref[...] * keep, 3), cw_ref[:, D_FF:2 * D_FF]) + cb_ref[:, D_FF:2 * D_FF]
        u_ref[:, 0:D_FF] = gate
        u_ref[:, D_FF:2 * D_FF] = up
        act_ref[...] = (gate * _sigmoid(gate) * up).astype(BF16)

    ins = [("row", u_pre, D_FF, 0), ("prev", u_pre, D_FF, 0), ("row", u_pre, D_FF, 1), ("prev", u_pre, D_FF, 1),
           ("full", cw), ("full", cb)]
    return _rowwise("ffn_act", body, t, tm, ins, [(D_FF, BF16), (2 * D_FF, F32)])


def _down_final(act, w_down, x1, tgt, w):
    d = D_MODEL

    def epilogue(f, x1_ref, t_ref, w_ref, dy_ref, df_ref, loss_ref, dw_ref):
        fh, r = _rms(f, d)
        e = x1_ref[...] + fh * w_ref[...] - t_ref[...]
        loss_ref[...] += _colsum(e * e) * (0.5 / d)
        dy = e * (1.0 / d)
        dy_ref[...] = dy
        dw_ref[...] += _colsum(dy * fh)
        df_ref[...] = _rms_bwd(fh, r, dy * w_ref[...], d).astype(BF16)

    return _matmul_rows("mm_down_final", act, w_down, "nn", epilogue, [x1, tgt], [w], [(d, F32), (d, BF16)],
                        accs=[(1, d), (1, d)], tk=D_FF)


def _ffn_bwd(u, u_pre, dact, cw, seq, tm):
    t = u.shape[0]

    def body(i, g_ref, gn_ref, up_ref, upn_ref, xg_ref, xu_ref, da_ref, dan_ref, cw_ref, dpre_ref, dcw_ref, dcb_ref):
        _, keep_next = _seq_flags(i, seq, tm)
        ext = lambda a_ref, n_ref: jnp.concatenate([a_ref[...], n_ref[...]], axis=0)
        rows = tm + SUBLANES
        gate, up = ext(g_ref, gn_ref), ext(up_ref, upn_ref)
        sg = _sigmoid(gate)
        da = jnp.concatenate([da_ref[...], dan_ref[...] * keep_next], axis=0)
        for off, grad, x_ref in ((0, da * up * _dsilu(gate, sg), xg_ref), (D_FF, da * gate * sg, xu_ref)):
            x = x_ref[...]
            own = grad[0:tm]
            acc = own * cw_ref[2:3, off:off + D_FF]
            dcb_ref[:, off:off + D_FF] += _colsum(own)
            dcw_ref[2:3, off:off + D_FF] += _colsum(own * x)
            for j in (1, 2):
                ahead = pltpu.roll(grad, rows - j, 0)[0:tm]
                acc = acc + ahead * cw_ref[2 - j:3 - j, off:off + D_FF]
                dcw_ref[2 - j:3 - j, off:off + D_FF] += _colsum(ahead * x)
            dpre_ref[:, off:off + D_FF] = acc.astype(BF16)

    ins = []
    for cb_ in range(2):
        ins += [("row", u, D_FF, cb_), ("next", u, D_FF, cb_)]
    ins += [("row", u_pre, D_FF, 0), ("row", u_pre, D_FF, 1), ("row", dact, D_FF, 0), ("next", dact, D_FF, 0), ("full", cw)]
    return _rowwise("ffn_bwd", body, t, tm, ins, [(2 * D_FF, BF16)], accs=[(SUBLANES, 2 * D_FF), (1, 2 * D_FF)])


def _assemble_dproj(dpre_qkv, dza, dzs, dpre_xbc, dsm, proj, gcw, scw, seq, tm):
    t = dza.shape[0]
    d = D_MODEL

    def body(i, dq_ref, dqn_ref, dk_ref, dkn_ref, dv_ref, dvn_ref, dx_ref, dxn_ref, dbc_ref, dbcn_ref, dza_ref, dzs_ref,
             dsm_ref, xq_ref, xk_ref, xv_ref, xx_ref, xbc_ref, gcw_ref, scw_ref, o_ref, dgcw_ref, dscw_ref):
        _, keep = _seq_flags(i, seq, tm)
        pieces = [(g_ref, n_ref, gcw_ref, dgcw_ref, x_ref, 0, c0) for g_ref, n_ref, x_ref, c0 in (
            (dq_ref, dqn_ref, xq_ref, 0), (dk_ref, dkn_ref, xk_ref, d), (dv_ref, dvn_ref, xv_ref, 2 * d))]
        pieces += [(g_ref, n_ref, scw_ref, dscw_ref, x_ref, 5 * d, c0) for g_ref, n_ref, x_ref, c0 in (
            (dx_ref, dxn_ref, xx_ref, 0), (dbc_ref, dbcn_ref, xbc_ref, d))]
        for d_ref, n_ref, cw_ref, dcw_ref, x_ref, base, c0 in pieces:
            w = x_ref.shape[1]
            x = x_ref[...]
            g = d_ref[...]
            halo = n_ref[...] * keep
            acc = g * cw_ref[3:4, c0:c0 + w]
            dcw_ref[3:4, c0:c0 + w] += _colsum(g * x)
            for j in range(1, 4):
                ahead = _shift_up(g, halo, j)
                acc = acc + ahead * cw_ref[3 - j:4 - j, c0:c0 + w]
                dcw_ref[3 - j:4 - j, c0:c0 + w] += _colsum(ahead * x)
            o_ref[:, base + c0:base + c0 + w] = acc.astype(BF16)
        o_ref[:, 3 * d:4 * d] = dza_ref[...]
        o_ref[:, 4 * d:5 * d] = dzs_ref[...]
        o_ref[:, 6 * d + 512:6 * d + 512 + LANES] = dsm_ref[...]
        o_ref[:, 6 * d + 512 + LANES:PROJ_W] = jnp.zeros((tm, PROJ_W - (6 * d + 512 + LANES)), BF16)

    ins = []
    for g in tuple(dpre_qkv) + tuple(dpre_xbc):
        ins += [("row", g, g.shape[1], 0), ("next", g, g.shape[1], 0)]
    ins += [("row", dza, d, 0), ("row", dzs, d, 0), ("row", dsm, LANES, 0),
           ("row", proj, d, 0), ("row", proj, d, 1), ("row", proj, d, 2), ("row", proj, d, 5), ("row", proj, 512, 12),
           ("full", gcw), ("full", scw)]
    return _rowwise("assemble_dproj", body, t, tm, ins, [(PROJ_W, BF16)], accs=[(SUBLANES, 3 * d), (SUBLANES, d + 512)])


def _dh2_mid_bwd(du_pre, w_up, x1, mix, dy, pmw, pfw):
    d = D_MODEL

    def epilogue(dh2, x1_ref, mix_ref, dy_ref, pmw_ref, pfw_ref, dx1_ref, dmix_ref, dpm_ref, dpf_ref):
        xh, r2 = _rms(x1_ref[...], d)
        dpf_ref[...] += _colsum(dh2 * xh)
        dx1 = dy_ref[...] + _rms_bwd(xh, r2, dh2 * pfw_ref[...], d)
        dx1_ref[...] = dx1
        mh, r = _rms(mix_ref[...], d)
        dpm_ref[...] += _colsum(dx1 * mh)
        dmix_ref[...] = _rms_bwd(mh, r, dx1 * pmw_ref[...], d).astype(BF16)

    return _matmul_rows("mm_dh2_mid_bwd", du_pre, w_up, "nt", epilogue, [x1, mix, dy], [pmw, pfw],
                        [(d, F32), (d, BF16)], accs=[(1, d), (1, d)], tk=D_FF)


def _dmixin_gate_norm_bwd(dmix, w_out, o_gdn, y_ssd, xs, proj, gnw, snw, dvec):
    d = D_MODEL

    def epilogue(dmixin, o_ref, za_ref, y_ref, xs_ref, zs_ref, gnw_ref, snw_ref, dv_ref,
                 do_ref, dza_ref, dy_ref, dxs_ref, dzs_ref, dgnw_ref, dsnw_ref, dd_ref):
        for hh in range(GDN_HEADS):
            sl = slice(hh * GDN_DK, (hh + 1) * GDN_DK)
            oh, r = _rms(o_ref[:, sl], GDN_DK)
            z = za_ref[:, sl]
            sz = _sigmoid(z)
            dm = dmixin[:, sl]
            don = dm * (z * sz)
            dza_ref[:, sl] = (dm * oh * gnw_ref[...] * _dsilu(z, sz)).astype(BF16)
            dgnw_ref[...] += _colsum(don * oh)
            do_ref[:, sl] = _rms_bwd(oh, r, don * gnw_ref[...], GDN_DK)
        zs = zs_ref[...]
        sz = _sigmoid(zs)
        sil = zs * sz
        x = xs_ref[...]
        y0 = y_ref[...] + dv_ref[...] * x
        yg = y0 * sil
        dms = dmixin[:, d:2 * d]
        for g in range(SSD_GROUPS):
            sl = slice(g * 512, (g + 1) * 512)
            yh, r = _rms(yg[:, sl], 512)
            dsnw_ref[:, sl] += _colsum(dms[:, sl] * yh)
            dyg = _rms_bwd(yh, r, dms[:, sl] * snw_ref[:, sl], 512)
            dy0 = dyg * sil[:, sl]
            dzs_ref[:, sl] = (dyg * y0[:, sl] * _dsilu(zs[:, sl], sz[:, sl])).astype(BF16)
            dy_ref[:, sl] = dy0
            dxs_ref[:, sl] = dy0 * dv_ref[:, sl]
            dd_ref[:, sl] += _colsum(dy0 * x[:, sl])

    row_ins = [o_gdn, (proj, d, 3), y_ssd, xs, (proj, d, 4)]
    return _matmul_rows("mm_dmixin_gate_norm_bwd", dmix, w_out, "nt", epilogue, row_ins, [gnw, snw, dvec],
                        [(d, F32), (d, BF16), (d, F32), (d, F32), (d, BF16)], accs=[(1, GDN_DK), (1, d), (1, d)], tm=256)


def _ssd_chunk_bwd(xs, bc, dtx, acsx, acs, ar, dy, sts, ypre, dxs_d, bsz, seq, sb):
    nsb, ncb, _, sp = _ssd_specs(seq, sb)
    bc_w = 2 * SSD_GROUPS * SSD_N
    x_w = SSD_HEADS * SSD_P

    def body(x_ref, dtx_ref, ax_ref, bc_ref, acs_ref, ar_ref, dy_ref, sts_ref, yx_ref, ybc_ref, dxd_ref,
             dx_ref, dbc_ref, ddt_ref, dacs_ref, dbx_ref, dbbc_ref, dst_scr):
        @pl.when(pl.program_id(1) == 0)
        def _():
            dst_scr[...] = jnp.zeros_like(dst_scr)

        @pl.when((pl.program_id(0) == 0) & (pl.program_id(1) == 0))
        def _():
            dbx_ref[...] = jnp.zeros_like(dbx_ref)
            dbbc_ref[...] = jnp.zeros_like(dbbc_ref)

        def to_conv_out(grad, y):
            return grad * _dsilu(y, _sigmoid(y))

        c = _chunk_consts()
        lane5 = lax.broadcasted_iota(jnp.int32, (CHUNK, SSD_GW), 1) // SSD_P
        row5 = lax.broadcasted_iota(jnp.int32, (CHUNK, SSD_GW), 0)
        sel_in = lax.broadcasted_iota(jnp.int32, (SSD_GW, LANES), 0) // SSD_P
        sel_out = lax.broadcasted_iota(jnp.int32, (SSD_GW, LANES), 1)

        def chunk(nn, carry):
            n = ncb - 1 - nn
            r = pl.ds(pl.multiple_of(n * CHUNK, CHUNK), CHUNK)
            rs = pl.ds(pl.multiple_of(n * SSD_N, SSD_N), SSD_N)
            acsv = acs_ref[0, r, :]
            ddt = jnp.zeros((CHUNK, LANES), F32)
            dacs = jnp.zeros((CHUNK, LANES), F32)
            for g in range(SSD_GROUPS):
                gl = slice(g * SSD_GW, (g + 1) * SSD_GW)
                x, dt, ax, dyv = x_ref[0, r, gl], dtx_ref[0, r, gl], ax_ref[0, r, gl], dy_ref[0, r, gl]
                bm = bc_ref[0, r, g * SSD_N:(g + 1) * SSD_N]
                cm = bc_ref[0, r, (SSD_GROUPS + g) * SSD_N:(SSD_GROUPS + g + 1) * SSD_N]
                st = sts_ref[0, rs, gl]
                dst = dst_scr[:, gl]
                rsel = (sel_in + (16 + g * SSD_HPG) == sel_out).astype(F32)
                xdt = x * dt
                cbm = _mm(cm, bm, NT)
                al = ax[CHUNK - 1:CHUNK, :]
                ex, el = jnp.exp(ax), jnp.exp(al)
                dec = jnp.exp(al - ax)
                xd = xdt * dec
                dye = dyv * ex
                dxd = _mm(bm, dst)
                dxdt = dec * dxd
                dcm = _mm(dye, st, NT)
                dbm = _mm(xd, dst, NT)
                z = dye * _mm(cm, st) - dxd * xd
                zl = _colsum(dst * st) * el + _colsum(dxd * xd)
                z = z + jnp.where(row5 == CHUNK - 1, zl, 0.0)
                dcb = jnp.zeros((CHUNK, CHUNK), F32)
                for hh in range(SSD_HPG):
                    head = g * SSD_HPG + hh
                    lm, gm = _ssd_head(acsv, ar_ref, n, head, cbm, c)
                    dym = jnp.where(lane5 == hh, dyv, 0.0)
                    dxdt = dxdt + _mm(gm, dym, TN)
                    dg = _mm(dym, xdt, NT)
                    dcb = dcb + dg * lm
                    pm = dg * gm
                    dacs = dacs + jnp.where(c["lane"] == head + 16, _rowsum(pm) - _mmsel(pm, c["ones"], TN), 0.0)
                for sl, grad in ((slice((SSD_GROUPS + g) * SSD_N, (SSD_GROUPS + g + 1) * SSD_N), dcm + _mm(dcb, bm)),
                                 (slice(g * SSD_N, (g + 1) * SSD_N), dbm + _mm(dcb, cm, TN))):
                    dpre = to_conv_out(grad, ybc_ref[0, r, sl])
                    dbc_ref[0, r, sl] = dpre
                    dbbc_ref[:, sl] += _colsum(dpre)
                dacs = dacs + _mmsel(z, rsel)
                ddt = ddt + _mmsel(dxdt * x, rsel)
                dpre = to_conv_out(dxdt * dt + dxd_ref[0, r, gl], yx_ref[0, r, gl])
                dx_ref[0, r, gl] = dpre
                dbx_ref[:, gl] += _colsum(dpre)
                dst_scr[:, gl] = dst * el + _mm(cm, dye, TN)
            ddt_ref[0, r, :] = ddt
            dacs_ref[0, r, :] = dacs
            return carry

        lax.fori_loop(0, ncb, chunk, 0)

    nsb_rev = lambda j: nsb - 1 - j
    return _pcall(
        body, name="ssd_chunk_bwd", grid=(bsz, nsb),
        in_specs=[sp["wide"](), sp["wide"](), sp["wide"](), sp["bc"](), sp["small"](), sp["ar"], sp["wide"](), sp["st"],
                  sp["wide"](), pl.BlockSpec((1, sb, bc_w), lambda b, j: (b, nsb_rev(j), x_w // bc_w)), sp["wide"]()],
        out_specs=[sp["wide"](), sp["bc"](), sp["small"](), sp["small"](),
                   pl.BlockSpec((1, x_w), lambda b, j: (0, 0)), pl.BlockSpec((1, bc_w), lambda b, j: (0, 0))],
        out_shape=[jax.ShapeDtypeStruct((bsz, seq, x_w), F32), jax.ShapeDtypeStruct((bsz, seq, bc_w), F32),
                   jax.ShapeDtypeStruct((bsz, seq, LANES), F32), jax.ShapeDtypeStruct((bsz, seq, LANES), F32),
                   jax.ShapeDtypeStruct((1, x_w), F32), jax.ShapeDtypeStruct((1, bc_w), F32)],
        scratch_shapes=[pltpu.VMEM((SSD_N, x_w), F32)],
        compiler_params=_params(("arbitrary", "arbitrary")),
    )(xs, dtx, acsx, bc, acs, ar, dy, sts, ypre, ypre, dxs_d)


def _through_norm_silu(g, y, scale):
    sy = _sigmoid(y)
    ds_ = _dsilu(y, sy)
    if scale is None:
        return g * ds_
    a = y * sy
    n = lax.rsqrt(_rowsum(a * a) + EPS)
    ah = a * n
    return (scale * n) * (g - ah * _rowsum(g * ah)) * ds_


def _gdn_chunk_bwd(qn, kn, vv, gs, gr, do, sts, tis, ypre, bsz, seq, sb, riders):
    hb = GDN_HB
    nsb, ncb, sp = _gdn_specs(seq, sb, hb, True)
    grid = (bsz, GDN_HEADS // hb, nsb)
    any_spec, rider_shapes, rider_sems, wrap = _riding_exchange(riders, True, 11, 4, grid)
    ypre_spec = lambda cb: pl.BlockSpec((1, sb, hb * GDN_DK), lambda b, h, j: (b, nsb - 1 - j, cb))

    def body(q_ref, k_ref, v_ref, gs_ref, gr_ref, do_ref, st_ref, ti_ref, yq_ref, yk_ref, yv_ref,
             dq_ref, dk_ref, dv_ref, dgs_ref, ds_scr):
        @pl.when(pl.program_id(2) == 0)
        def _():
            ds_scr[...] = jnp.zeros_like(ds_scr)

        c = _chunk_consts()

        def chunk(nn, carry):
            n = ncb - 1 - nn
            r = pl.ds(pl.multiple_of(n * CHUNK, CHUNK), CHUNK)
            rs = pl.ds(pl.multiple_of(n * GDN_DK, GDN_DK), GDN_DK)
            gsv = gs_ref[0, r, :]
            heads = list(range(hb))
            sls = [slice(ih * GDN_DK, (ih + 1) * GDN_DK) for ih in heads]
            q = [q_ref[0, r, sl] for sl in sls]
            k = [k_ref[0, r, sl] for sl in sls]
            v = [v_ref[0, r, sl] for sl in sls]
            do_ = [do_ref[0, r, sl] for sl in sls]
            s = [st_ref[0, ih, rs, :] for ih in heads]
            tinv = [ti_ref[0, ih, r, :] for ih in heads]
            dsn = [ds_scr[ih] for ih in heads]
            beta, dc, eg, egl, ekd = zip(*[
                _gdn_gates(gsv, gr_ref[0, n, pl.ds(ih, 1), :], ih, c) for ih in heads])
            mul = lambda a, b: a * b
            kb = _hmap(mul, k, beta)
            rhs_w = _hmap(mul, kb, eg)
            u = _hmap(lambda t_, a, b: _mm3(t_, a * b), tinv, v, beta)
            w = _hmap(_mm3, tinv, rhs_w)
            amat = _hmap(lambda a, b, d_: jnp.where(c["strict"], _mm(a, b, NT) * d_, 0.0), kb, k, dc)
            qk = _hmap(lambda a, b, d_: _mm(a, b, NT) * d_, q, k, dc)
            qd = _hmap(mul, q, eg)
            kd = _hmap(mul, k, ekd)
            v_new = _hmap(lambda a, b, s_: a - _mm(b, s_), u, w, s)
            dv_new = _hmap(lambda qk_, d_, kd_, dn: _mm(qk_, d_, TN) + _mm(kd_, dn), qk, do_, kd, dsn)
            dqk = _hmap(lambda d_, vn: _mm(d_, vn, NT), do_, v_new)
            dqd = _hmap(lambda d_, s_: _mm(d_, s_, NT), do_, s)
            ds_new = _hmap(lambda qd_, d_, dn, e, w_, dvn: _mm(qd_, d_, TN) + dn * e - _mm(w_, dvn, TN),
                           qd, do_, dsn, egl, w, dv_new)
            dkd = _hmap(lambda vn, dn: _mm(vn, dn, NT), v_new, dsn)
            dgl = _hmap(lambda s_, dn, e: _colsum(_rowsum(s_ * dn)) * e, s, dsn, egl)
            dw = _hmap(lambda dvn, s_: -_mm(dvn, s_, NT), dv_new, s)
            dru = _hmap(lambda t_, a: _mm3(t_, a, TN), tinv, dv_new)
            drw = _hmap(lambda t_, a: _mm3(t_, a, TN), tinv, dw)
            da = _hmap(lambda a, u_, b, w_: jnp.where(c["strict"], -(_mm(a, u_, NT) + _mm(b, w_, NT)), 0.0), dru, u, drw, w)
            m = _hmap(mul, da, dc)
            dkb = _hmap(lambda a, e, m_, k_: a * e + _mm(m_, k_), drw, eg, m, k)
            mq = _hmap(mul, dqk, dc)
            dq = _hmap(lambda mq_, k_, a, e: _mm(mq_, k_) + a * e, mq, k, dqd, eg)
            dk = _hmap(lambda m_, kb_, mq_, q_, a, e, b, be: _mm(m_, kb_, TN) + _mm(mq_, q_, TN) + a * e + b * be,
                       m, kb, mq, q, dkd, ekd, dkb, beta)
            dbeta = _hmap(lambda a, v_, b, k_: _rowsum(a * v_) + _rowsum(b * k_), dru, v, dkb, k)
            pq = _hmap(lambda a, am, b, qk_: a * am + b * qk_, da, amat, dqk, qk)
            ekk = _hmap(lambda a, b: _rowsum(a * b), dkd, kd)
            dgc = _hmap(lambda pq_, a, rw, b, qd_, e, gl_: (
                _rowsum(pq_) - _mmsel(pq_, c["ones"], TN) + (_rowsum(a * rw) + _rowsum(b * qd_) - e)
                + jnp.where(c["row1"] == CHUNK - 1, _colsum(e) + gl_, 0.0)), pq, drw, rhs_w, dqd, qd, ekk, dgl)
            dv = _hmap(mul, dru, beta)
            dyq = _hmap(lambda g_, sl: _through_norm_silu(g_, yq_ref[0, r, sl], GDN_DK ** -0.5), dq, sls)
            dyk = _hmap(lambda g_, sl: _through_norm_silu(g_, yk_ref[0, r, sl], 1.0), dk, sls)
            dyv = _hmap(lambda g_, sl: _through_norm_silu(g_, yv_ref[0, r, sl], None), dv, sls)
            dgs = jnp.zeros((CHUNK, LANES), F32)
            for ih in heads:
                ds_scr[ih] = ds_new[ih]
                dq_ref[0, r, sls[ih]] = dyq[ih]
                dk_ref[0, r, sls[ih]] = dyk[ih]
                dv_ref[0, r, sls[ih]] = dyv[ih]
                dgs = dgs + jnp.where(c["lane"] == ih, dbeta[ih], jnp.where(c["lane"] == ih + 8, dgc[ih], 0.0))
            dgs_ref[0, r, :] = dgs
            return carry

        lax.fori_loop(0, ncb, chunk, 0)

    res = _pcall(
        wrap(body), name="gdn_chunk_bwd", grid=grid,
        in_specs=[sp["wide"](), sp["wide"](), sp["wide"](), sp["gs"](), sp["gr"], sp["wide"](), sp["st"], sp["ti"],
                  ypre_spec(0), ypre_spec(1), ypre_spec(2)] + any_spec,
        out_specs=[sp["wide"](), sp["wide"](), sp["wide"](), sp["gs"]()] + any_spec,
        out_shape=[jax.ShapeDtypeStruct((bsz, seq, D_MODEL), F32)] * 3 + [jax.ShapeDtypeStruct((bsz, seq, LANES), F32)]
        + rider_shapes,
        scratch_shapes=[pltpu.VMEM((hb, GDN_DK, GDN_DK), F32)] + rider_sems,
        compiler_params=_params(("arbitrary", "arbitrary", "arbitrary")),
    )(qn, kn, vv, gs, gr, do, sts, tis, ypre, ypre, ypre, *riders)
    return res[:4], res[4:]


def _gates_bwd(proj, dgs, ddt, dacs, gp, sp, tm):
    t = proj.shape[0]

    def body(i, sm_ref, dgs_ref, ddt_ref, dacs_ref, gp_ref, sp_ref, dsm_ref, dgp_ref, dsp_ref):
        sm = sm_ref[...]
        lane = lax.broadcasted_iota(jnp.int32, sm.shape, 1)
        is_g = (lane >= 8) & (lane < 16)
        is_dt = (lane >= 16) & (lane < 32)
        dgs = dgs_ref[...]
        back = _mmx(_block_tri(tm, True), jnp.where(is_g, dgs, 0.0) + dacs_ref[...])
        beta = _sigmoid(sm)
        bias = gp_ref[1:2, :] + sp_ref[1:2, :]
        xb = sm + bias
        soft, dsoft = _softplus(xb), _sigmoid(xb)
        g_neg = -jnp.exp(gp_ref[0:1, :])
        a_neg = -jnp.exp(sp_ref[0:1, :])
        dg = jnp.where(is_g, back * g_neg, 0.0)
        dxb_g = dg * dsoft
        dxb_dt = jnp.where(is_dt, (ddt_ref[...] + back * a_neg) * dsoft, 0.0)
        dsm_ref[...] = (jnp.where(lane < 8, dgs * beta * (1.0 - beta), dxb_g) + dxb_dt).astype(BF16)
        dgp_ref[1:2, :] += _colsum(dxb_g)
        dgp_ref[0:1, :] += _colsum(dg * soft)
        dsp_ref[1:2, :] += _colsum(dxb_dt)
        dsp_ref[0:1, :] += jnp.where(is_dt[0:1, :], _colsum(back * soft) * a_neg, 0.0)

    ins = [("row", proj, LANES, SMALL_CB), ("row", dgs, LANES, 0), ("row", ddt, LANES, 0), ("row", dacs, LANES, 0),
           ("full", gp), ("full", sp)]
    return _rowwise("gates_bwd", body, t, tm, ins, [(LANES, BF16)], accs=[(SUBLANES, LANES), (SUBLANES, LANES)])


def _dh1_first_bwd(dproj, wp_in, x, dx1, w, scatter_riders):
    d = D_MODEL

    def epilogue(dh, x_ref, dx1_ref, w_ref, dx_ref, dw_ref):
        xh, r = _rms(x_ref[...], d)
        dw_ref[...] += _colsum(dh * xh)
        dx_ref[...] = dx1_ref[...] + _rms_bwd(xh, r, dh * w_ref[...], d)

    return _matmul_rows("mm_dh1_first_bwd", dproj, wp_in, "nt", epilogue, [x, dx1], [w], [(d, F32)], accs=[(1, d)],
                        tk=PROJ_W // 2, scatter_riders=scatter_riders)


def _gather_two_level(name, arrays):
    n = len(arrays)
    n_sem = 7

    def body(*refs):
        ins, outs = refs[:n], refs[n:2 * n]
        send_sems, recv_sems, loc_sems = refs[2 * n:]
        x, y, c = lax.axis_index("x"), lax.axis_index("y"), lax.axis_index("c")
        slot = lambda px, py, pc: 4 * px + 2 * py + pc
        sibling = (x, y, 1 - c)
        chips = [(1 - x, y), (x, 1 - y), (1 - x, 1 - y)]

        def copy(t, k, src, block, to):
            return pltpu.make_async_remote_copy(
                src_ref=src, dst_ref=outs[t].at[block], send_sem=send_sems.at[t, k], recv_sem=recv_sems.at[t, k],
                device_id=to, device_id_type=pl.DeviceIdType.MESH)

        own, first, passed = [], [], []
        for t in range(n):
            own.append(pltpu.make_async_copy(ins[t], outs[t].at[slot(x, y, c)], loc_sems.at[t]))
            first.append(copy(t, 0, ins[t], slot(x, y, c), sibling))
            first += [copy(t, 1 + j, ins[t], slot(x, y, c), (px, py, c)) for j, (px, py) in enumerate(chips)]
        for cp in own + first:
            cp.start()
        for t in range(n):
            for j, (px, py) in enumerate(chips):
                copy(t, 1 + j, ins[t], slot(px, py, c), (px, py, c)).wait_recv()
                fwd = copy(t, 4 + j, outs[t].at[slot(px, py, c)], slot(px, py, c), sibling)
                fwd.start()
                passed.append(fwd)
        for t in range(n):
            copy(t, 0, ins[t], slot(x, y, 1 - c), sibling).wait_recv()
            for j, (px, py) in enumerate(chips):
                copy(t, 4 + j, ins[t], slot(px, py, 1 - c), sibling).wait_recv()
        for cp in first + passed:
            cp.wait_send()
        for cp in own:
            cp.wait()

    return _pcall(
        body, name=name,
        in_specs=[pl.BlockSpec(memory_space=pl.ANY)] * n,
        out_specs=[pl.BlockSpec(memory_space=pl.ANY)] * n,
        out_shape=_exchange_out_shapes(arrays, False),
        scratch_shapes=[pltpu.SemaphoreType.DMA((n, n_sem)), pltpu.SemaphoreType.DMA((n, n_sem)), pltpu.SemaphoreType.DMA((n,))],
    )(*arrays)


def _exchange_out_shapes(arrays, scatter):
    return [jax.ShapeDtypeStruct(a.shape if scatter else (N_DEV,) + a.shape, a.dtype) for a in arrays]


def _exchange_sems(n):
    return [pltpu.SemaphoreType.DMA((n, N_DEV - 1)), pltpu.SemaphoreType.DMA((n, N_DEV - 1)), pltpu.SemaphoreType.DMA((n,))]


def _exchange_phase(ins, outs, sems, scatter, start):
    send_sems, recv_sems, loc_sems = sems
    x, y, c = lax.axis_index("x"), lax.axis_index("y"), lax.axis_index("c")
    me = 4 * x + 2 * y + c
    for t in range(len(ins)):
        loc = pltpu.make_async_copy(ins[t].at[me] if scatter else ins[t], outs[t].at[me], loc_sems.at[t])
        if start:
            loc.start()
        else:
            loc.wait()
        for k in range(N_DEV - 1):
            bx, by, bc = ((k + 1) >> 2) & 1, ((k + 1) >> 1) & 1, (k + 1) & 1
            px = 1 - x if bx else x
            py = 1 - y if by else y
            pc = 1 - c if bc else c
            peer = 4 * px + 2 * py + pc
            src = ins[t].at[peer] if scatter else ins[t]
            copy = lambda dst: pltpu.make_async_remote_copy(
                src_ref=src, dst_ref=dst, send_sem=send_sems.at[t, k], recv_sem=recv_sems.at[t, k],
                device_id=(px, py, pc), device_id_type=pl.DeviceIdType.MESH)
            if start:
                copy(outs[t].at[me]).start()
            else:
                copy(outs[t].at[me]).wait_send()
                copy(outs[t].at[peer]).wait_recv()


def _adam_math(w, g, m, v):
    m = ADAM_B1 * m + (1.0 - ADAM_B1) * g
    v = ADAM_B2 * v + (1.0 - ADAM_B2) * (g * g)
    m_hat = m / (1.0 - ADAM_B1 ** ADAM_STEP)
    v_hat = v / (1.0 - ADAM_B2 ** ADAM_STEP)
    delta = -ADAM_LR * (m_hat / (jnp.sqrt(v_hat) + ADAM_EPS) + ADAM_WD * w)
    return delta, m, v


def _adam_big(name, parts, w, m, v, tm):
    r, c = w.shape
    tm = tm if r % tm == 0 else r

    def body(p_ref, w_ref, m_ref, v_ref, g_ref, d_ref, nm_ref, nv_ref):
        g = p_ref[0].astype(F32)
        for s in range(1, N_DEV):
            g = g + p_ref[s].astype(F32)
        g_ref[...] = g
        d_ref[...], nm_ref[...], nv_ref[...] = _adam_math(w_ref[...], g, m_ref[...], v_ref[...])

    blk = lambda: pl.BlockSpec((tm, c), lambda i: (i, 0))
    return _pcall(
        body, name=name, grid=(r // tm,),
        in_specs=[pl.BlockSpec((N_DEV, tm, c), lambda i: (0, i, 0)), blk(), blk(), blk()],
        out_specs=[blk(), blk(), blk(), blk()],
        out_shape=[jax.ShapeDtypeStruct((r, c), F32)] * 4,
        compiler_params=_params(("parallel",)),
    )(parts, w, m, v)


SMALL_ROWS = 56
ROW_DD, ROW_LOSS = 5, 6


def _small_sum(gathered):
    def body(g_ref, o_ref, x_ref):
        s = g_ref[0]
        for dev in range(1, N_DEV):
            s = s + g_ref[dev]
        o_ref[...] = s
        ri = lax.broadcasted_iota(jnp.int32, (D_MODEL, LANES), 0)
        ro = lax.broadcasted_iota(jnp.int32, (D_MODEL, LANES), 1)
        heads = _mmx(jnp.broadcast_to(s[ROW_DD:ROW_DD + 1, :], (SUBLANES, D_MODEL)), (ri // SSD_P == ro).astype(F32))
        loss = _rowsum(jnp.broadcast_to(s[ROW_LOSS:ROW_LOSS + 1, :], (SUBLANES, D_MODEL)))
        row = lax.broadcasted_iota(jnp.int32, (SUBLANES, LANES), 0)
        x_ref[...] = jnp.where(row == 0, heads, jnp.broadcast_to(loss, (SUBLANES, LANES)))

    return _pcall(
        body, name="small_sum",
        out_shape=[jax.ShapeDtypeStruct((SMALL_ROWS, D_MODEL), F32), jax.ShapeDtypeStruct((SUBLANES, LANES), F32)],
        compiler_params=_params(None),
    )(gathered)


def _adam_small(g, w, m, v):
    def body(g_ref, w_ref, m_ref, v_ref, d_ref, nm_ref, nv_ref):
        d_ref[...], nm_ref[...], nv_ref[...] = _adam_math(w_ref[...], g_ref[...], m_ref[...], v_ref[...])

    return _pcall(body, name="adam_small", out_shape=[jax.ShapeDtypeStruct(g.shape, F32)] * 3,
                  compiler_params=_params(None))(g, w, m, v)


def _pack(pieces, rows):
    flat = jnp.concatenate([p.reshape(-1).astype(F32) for p in pieces])
    return jnp.pad(flat, (0, rows * D_MODEL - flat.shape[0])).reshape(rows, D_MODEL)


def _unpack(packed, shapes):
    flat = packed.reshape(-1)
    out, off = [], 0
    for shp in shapes:
        size = 1
        for s in shp:
            size *= s
        out.append(flat[off:off + size].reshape(shp))
        off += size
    return out


def _permute_in(w):
    pad = jnp.zeros((w.shape[0], PROJ_W - D_IN), w.dtype)
    return jnp.concatenate([w[:, 0:4096], w[:, 4112:6672], w[:, 4096:4112], w[:, 6672:6688], pad], axis=1)


def _unpermute_in(g):
    return jnp.concatenate([g[:, 0:4096], g[:, 6656:6672], g[:, 4096:6656], g[:, 6672:6688]], axis=1)


def _lane_row(vec, start):
    return jnp.zeros((LANES,), F32).at[start:start + vec.shape[0]].set(vec)


def _cols_from_shards(g):
    return jnp.transpose(g, (1, 0, 2)).reshape(g.shape[1], N_DEV * g.shape[2])


def _cols_to_shards(a):
    return jnp.transpose(a.astype(BF16).reshape(a.shape[0], N_DEV, a.shape[1] // N_DEV), (1, 0, 2))


def _rows_to_shards(a):
    return a.astype(BF16).reshape(N_DEV, a.shape[0] // N_DEV, a.shape[1])


def _local_step(x, tgt, wp_in, rest, p, rest_is_sharded):
    bsz, seq, d = x.shape
    t = bsz * seq
    x2 = x.reshape(t, d)
    tgt2 = tgt.reshape(t, d)
    tm = min(256, seq)
    tm_big = min(512, seq)
    tm_wide = min(256, seq)
    sb = min(512, seq)

    gp = jnp.zeros((SUBLANES, LANES), F32).at[0].set(_lane_row(p["gdn_a_log"], 8)).at[1].set(_lane_row(p["gdn_dt_bias"], 8))
    sp = jnp.zeros((SUBLANES, LANES), F32).at[0].set(_lane_row(p["ssd_a_log"], 16)).at[1].set(_lane_row(p["ssd_dt_bias"], 16))
    dvec = jnp.repeat(p["ssd_d"], SSD_P).reshape(1, d)
    row = lambda v: v.reshape(1, -1)
    pre_mix, post_mix, pre_ffn, post_ffn = (row(p[k]) for k in ("pre_mix_norm", "post_mix_norm", "pre_ffn_norm", "post_ffn_norm"))
    gnw, snw = row(p["gdn_norm_w"]), row(p["ssd_norm_w"])
    gcw, scw, scb, fcw, fcb = p["gdn_conv_w"], p["ssd_conv_w"], row(p["ssd_conv_b"]), p["ffn_conv_w"], row(p["ffn_conv_b"])

    h1, proj = _norm_proj(x2, pre_mix, wp_in)
    b3 = lambda a: a.reshape(bsz, seq, a.shape[-1])
    b2 = lambda a: a.reshape(t, a.shape[-1])
    rows_of = lambda a, lo, n: jnp.transpose(a[:, lo:lo + n].reshape(bsz, seq // CHUNK, CHUNK, n), (0, 1, 3, 2))
    qn, kn, vv, gs, ypre_gdn = _gdn_prep(proj, gcw, gp, seq, tm_big)
    gr = rows_of(gs, 8, GDN_HEADS)
    qn, kn, vv, gs = b3(qn), b3(kn), b3(vv), b3(gs)
    (o_gdn, gdn_st, gdn_ti), gathered = _gdn_chunk_fwd(qn, kn, vv, gs, gr, bsz, seq, sb, list(rest) if rest_is_sharded else [])
    if rest_is_sharded:
        w_out, w_up, w_down = gathered[0].reshape(-1, d), _cols_from_shards(gathered[1]), gathered[2].reshape(-1, d)
    else:
        w_out, w_up, w_down = rest
    o_gdn = b2(o_gdn)
    xs, bc, dtx, acsx, acs, ypre_ssd = _ssd_prep(proj, scw, scb, sp, seq, tm)
    ar = rows_of(acs, 16, SSD_HEADS)
    y_ssd, ssd_st = _ssd_chunk_fwd(b3(xs), b3(bc), b3(dtx), b3(acsx), b3(acs), ar, bsz, seq, sb)
    y_ssd = b2(y_ssd)
    mixin = _gate_norm(o_gdn, y_ssd, xs, proj, gnw, snw, dvec, tm_big)
    mix, x1, h2 = _out_mid(mixin, w_out, x2, post_mix, pre_ffn)
    u_pre = _matmul("mm_up", h2, w_up, "nn", F32)
    act, u = _ffn_act(u_pre, fcw, fcb, seq, tm_wide)
    dy, df, loss_lanes, d_post_ffn = _down_final(act, w_down, x1, tgt2, post_ffn)

    g_down = _matmul("mm_dw_down", act, df, "tn", BF16, tm=1408, tk=2048)
    dact = _matmul("mm_dact", df, w_down, "nt", F32, tn=1408)
    du_pre, d_fcw, d_fcb = _ffn_bwd(u, u_pre, dact, fcw, seq, tm_wide)
    g_up = _matmul("mm_dw_up", h2, du_pre, "tn", BF16, tn=512, tk=4096)
    dx1, dmix, d_post_mix, d_pre_ffn = _dh2_mid_bwd(du_pre, w_up, x1, mix, dy, post_mix, pre_ffn)
    g_out = _matmul("mm_dw_out", mixin, dmix, "tn", BF16, tk=2048)
    do_gdn, dza, dy_ssd, dxs_d, dzs, d_gnw, d_snw, d_dd = _dmixin_gate_norm_bwd(dmix, w_out, o_gdn, y_ssd, xs, proj, gnw, snw, dvec)
    dyx, dybc, ddt, dacs, d_scb_x, d_scb_bc = _ssd_chunk_bwd(
        b3(xs), b3(bc), b3(dtx), b3(acsx), b3(acs), ar, b3(dy_ssd), ssd_st, b3(ypre_ssd), b3(dxs_d), bsz, seq, sb)
    dyx, dybc, ddt, dacs = b2(dyx), b2(dybc), b2(ddt), b2(dacs)
    d_scb = jnp.concatenate([d_scb_x, d_scb_bc], axis=1)
    riders = [_rows_to_shards(g_out), _cols_to_shards(g_up), _rows_to_shards(g_down)] if rest_is_sharded else []
    dgdn, received = _gdn_chunk_bwd(qn, kn, vv, gs, gr, b3(do_gdn), gdn_st, gdn_ti, b3(ypre_gdn), bsz, seq, min(256, seq), riders)
    if rest_is_sharded:
        g_out, g_up, g_down = received
    dyq, dyk, dyv, dgs = (b2(a) for a in dgdn)
    dsm, d_gp, d_sp = _gates_bwd(proj, dgs, ddt, dacs, gp, sp, tm)
    dproj, d_gcw, d_scw = _assemble_dproj((dyq, dyk, dyv), dza, dzs, (dyx, dybc), dsm, proj, gcw, scw, seq, tm)
    g_in = _matmul("mm_dw_in", h1, dproj, "tn", BF16, tk=4096)
    if rest_is_sharded:
        (dx, d_pre_mix), (g_in,) = _dh1_first_bwd(dproj, wp_in, x2, dx1, pre_mix, [_cols_to_shards(_unpermute_in(g_in))])
    else:
        dx, d_pre_mix = _dh1_first_bwd(dproj, wp_in, x2, dx1, pre_mix, [])

    small = dict(pre_mix_norm=d_pre_mix, ssd_norm_w=d_snw, post_mix_norm=d_post_mix, pre_ffn_norm=d_pre_ffn,
                 post_ffn_norm=d_post_ffn, dd_lanes=d_dd, loss_lanes=loss_lanes, gdn_gates=d_gp, ssd_gates=d_sp,
                 gdn_norm_w=d_gnw, gdn_conv_w=d_gcw[0:4], ssd_conv_w=d_scw[0:4], ssd_conv_b=d_scb,
                 ffn_conv_w=d_fcw[0:3], ffn_conv_b=d_fcb)
    return dx.reshape(bsz, seq, d), g_in, g_out, g_up, g_down, small


def kernel(x, pre_mix_norm, w_in, gdn_conv_w, gdn_a_log, gdn_dt_bias, gdn_norm_w, ssd_conv_w, ssd_conv_b, ssd_a_log, ssd_dt_bias, ssd_d, ssd_norm_w, w_out, post_mix_norm, pre_ffn_norm, w_up, ffn_conv_w, ffn_conv_b, w_down, post_ffn_norm, loss_target, m_pre_mix_norm, m_w_in, m_gdn_conv_w, m_gdn_a_log, m_gdn_dt_bias, m_gdn_norm_w, m_ssd_conv_w, m_ssd_conv_b, m_ssd_a_log, m_ssd_dt_bias, m_ssd_d, m_ssd_norm_w, m_w_out, m_post_mix_norm, m_pre_ffn_norm, m_w_up, m_ffn_conv_w, m_ffn_conv_b, m_w_down, m_post_ffn_norm, v_pre_mix_norm, v_w_in, v_gdn_conv_w, v_gdn_a_log, v_gdn_dt_bias, v_gdn_norm_w, v_ssd_conv_w, v_ssd_conv_b, v_ssd_a_log, v_ssd_dt_bias, v_ssd_d, v_ssd_norm_w, v_w_out, v_post_mix_norm, v_pre_ffn_norm, v_w_up, v_ffn_conv_w, v_ffn_conv_b, v_w_down, v_post_ffn_norm):
    names = ["pre_mix_norm", "w_in", "gdn_conv_w", "gdn_a_log", "gdn_dt_bias", "gdn_norm_w", "ssd_conv_w", "ssd_conv_b",
             "ssd_a_log", "ssd_dt_bias", "ssd_d", "ssd_norm_w", "w_out", "post_mix_norm", "pre_ffn_norm", "w_up",
             "ffn_conv_w", "ffn_conv_b", "w_down", "post_ffn_norm"]
    w_args = [pre_mix_norm, w_in, gdn_conv_w, gdn_a_log, gdn_dt_bias, gdn_norm_w, ssd_conv_w, ssd_conv_b, ssd_a_log, ssd_dt_bias, ssd_d, ssd_norm_w, w_out, post_mix_norm, pre_ffn_norm, w_up, ffn_conv_w, ffn_conv_b, w_down, post_ffn_norm]
    m_args = [m_pre_mix_norm, m_w_in, m_gdn_conv_w, m_gdn_a_log, m_gdn_dt_bias, m_gdn_norm_w, m_ssd_conv_w, m_ssd_conv_b, m_ssd_a_log, m_ssd_dt_bias, m_ssd_d, m_ssd_norm_w, m_w_out, m_post_mix_norm, m_pre_ffn_norm, m_w_up, m_ffn_conv_w, m_ffn_conv_b, m_w_down, m_post_ffn_norm]
    v_args = [v_pre_mix_norm, v_w_in, v_gdn_conv_w, v_gdn_a_log, v_gdn_dt_bias, v_gdn_norm_w, v_ssd_conv_w, v_ssd_conv_b, v_ssd_a_log, v_ssd_dt_bias, v_ssd_d, v_ssd_norm_w, v_w_out, v_post_mix_norm, v_pre_ffn_norm, v_w_up, v_ffn_conv_w, v_ffn_conv_b, v_w_down, v_post_ffn_norm]
    w = {k: a[0] for k, a in zip(names, w_args)}
    m = {k: a[0] for k, a in zip(names, m_args)}
    v = {k: a[0] for k, a in zip(names, v_args)}
    idx = 4 * lax.axis_index("x") + 2 * lax.axis_index("y") + lax.axis_index("c")
    big = ("w_in", "w_out", "w_up", "w_down")
    conv = ("gdn_conv_w", "ssd_conv_w", "ffn_conv_w")

    conv_local = jnp.concatenate([jnp.pad(w[k], ((0, 4 - w[k].shape[0]), (0, 0))) for k in conv], axis=1)
    g_in, g_conv = _gather_two_level("gather_weights", [w["w_in"].astype(BF16), conv_local])
    wp_in = _permute_in(_cols_from_shards(g_in))
    p = {k: w[k] for k in names if k not in big and k not in conv}
    off = 0
    for k in conv:
        cw = w[k].shape[1]
        p[k] = jnp.transpose(g_conv[:, :w[k].shape[0], off:off + cw], (1, 0, 2)).reshape(w[k].shape[0], N_DEV * cw)
        off += cw

    rest = tuple(w[k].astype(BF16) for k in ("w_out", "w_up", "w_down"))
    dx, p_in, p_out, p_up, p_down, small = _local_step(x, loss_target, wp_in, rest, p, True)

    gate_row = jnp.concatenate([small["gdn_gates"][0], small["gdn_gates"][1], small["ssd_gates"][0], small["ssd_gates"][1],
                                small["gdn_norm_w"][0], jnp.zeros((D_MODEL - 5 * LANES,), F32)]).reshape(1, D_MODEL)
    pack = _pack([small["pre_mix_norm"], small["ssd_norm_w"], small["post_mix_norm"], small["pre_ffn_norm"],
                  small["post_ffn_norm"], small["dd_lanes"], small["loss_lanes"], gate_row,
                  small["gdn_conv_w"], small["ssd_conv_w"], jnp.pad(small["ssd_conv_b"], ((0, 0), (0, 512))),
                  jnp.pad(small["ffn_conv_w"].reshape(-1), (0, 17 * D_MODEL - 3 * 2 * D_FF)),
                  jnp.pad(small["ffn_conv_b"], ((0, 0), (0, 512)))], SMALL_ROWS)
    (pack_all,) = _gather_two_level("gather_small", [pack])
    ssum, extra = _small_sum(pack_all)

    grads, deltas, new_m, new_v = {}, {}, {}, {}
    for k, parts in (("w_in", p_in), ("w_out", p_out), ("w_up", p_up), ("w_down", p_down)):
        grads[k], deltas[k], new_m[k], new_v[k] = _adam_big("adam_" + k, parts, w[k], m[k], v[k], 256)

    flat = ssum.reshape(-1)
    gate = ssum[7]
    sg = dict(pre_mix_norm=ssum[0], ssd_norm_w=ssum[1], post_mix_norm=ssum[2], pre_ffn_norm=ssum[3], post_ffn_norm=ssum[4],
              gdn_a_log=gate[8:16], gdn_dt_bias=gate[LANES + 8:LANES + 16], ssd_a_log=gate[2 * LANES + 16:2 * LANES + 32],
              ssd_dt_bias=gate[3 * LANES + 16:3 * LANES + 32], gdn_norm_w=gate[4 * LANES:5 * LANES], ssd_d=extra[0, 0:SSD_HEADS])
    o = 8 * D_MODEL
    full_gcw = flat[o:o + 4 * 3072].reshape(4, 3072)
    o += 12 * D_MODEL
    full_scw = flat[o:o + 4 * 1536].reshape(4, 1536)
    o += 6 * D_MODEL
    sg["ssd_conv_b"] = flat[o:o + 1536]
    o += 2 * D_MODEL
    full_fcw = flat[o:o + 3 * 2 * D_FF].reshape(3, 2 * D_FF)
    o += 17 * D_MODEL
    sg["ffn_conv_b"] = flat[o:o + 2 * D_FF]
    for k, full in (("gdn_conv_w", full_gcw), ("ssd_conv_w", full_scw), ("ffn_conv_w", full_fcw)):
        cw = w[k].shape[1]
        sg[k] = lax.dynamic_slice_in_dim(full, idx * cw, cw, axis=1)
    small_names = [k for k in names if k not in big]
    rows = 24
    gpk = _pack([sg[k] for k in small_names], rows)
    dpk, mpk, vpk = _adam_small(gpk, _pack([w[k] for k in small_names], rows), _pack([m[k] for k in small_names], rows),
                                _pack([v[k] for k in small_names], rows))
    shapes = [w[k].shape for k in small_names]
    for k, g_, d_, m_, v_ in zip(small_names, _unpack(gpk, shapes), _unpack(dpk, shapes), _unpack(mpk, shapes), _unpack(vpk, shapes)):
        grads[k], deltas[k], new_m[k], new_v[k] = g_, d_, m_, v_

    loss = extra[1, 0]
    lead = lambda a: a[None]
    return (loss, dx, *[lead(grads[k]) for k in names], *[lead(deltas[k]) for k in names],
            *[lead(new_m[k]) for k in names], *[lead(new_v[k]) for k in names])
```

```python
import functools

import jax
import jax.numpy as jnp
from jax import lax
from jax.experimental import pallas as pl
from jax.experimental.pallas import tpu as pltpu

F32 = jnp.float32
BF16 = jnp.bfloat16
MXU_DTYPE = jnp.bfloat16
HIGHEST = lax.Precision.HIGHEST
VMEM_LIMIT_V7X = 48 * 1024 * 1024
SUBLANES = 8
LANES = 128

D_MODEL = 1024
GDN_HEADS = 8
GDN_DK = 128
SSD_HEADS = 16
SSD_P = 64
SSD_GROUPS = 2
SSD_HPG = 8
SSD_N = 128
CHUNK = 128
D_FF = 2816
EPS = 1e-6
N_DEV = 8
PROJ_W = 7168
SMALL_CB = 52
D_IN = 6688

ADAM_LR = 0.001
ADAM_B1 = 0.9
ADAM_B2 = 0.999
ADAM_EPS = 1e-08
ADAM_WD = 0.01
ADAM_STEP = 10

NN = (((1,), (0,)), ((), ()))
NT = (((1,), (1,)), ((), ()))
TN = (((0,), (0,)), ((), ()))


def _pcall(body, **kw):
    return pl.pallas_call(body, **kw)


def _mm(a, b, dims=NN):
    return lax.dot_general(a.astype(MXU_DTYPE), b.astype(MXU_DTYPE), dims, preferred_element_type=F32)


def _mmx(a, b, dims=NN):
    return lax.dot_general(a, b, dims, precision=HIGHEST, preferred_element_type=F32)


def _split(a):
    hi = a.astype(MXU_DTYPE)
    return hi, (a - hi.astype(F32)).astype(MXU_DTYPE)


def _mm3(a, b, dims=NN):
    (ah, al), (bh, bl) = _split(a), _split(b)
    dot = lambda p, q: lax.dot_general(p, q, dims, preferred_element_type=F32)
    return dot(ah, bh) + (dot(ah, bl) + dot(al, bh))


def _mmsel(a, sel, dims=NN, terms=2):
    s = sel.astype(MXU_DTYPE)
    out = None
    for _ in range(terms):
        part = a.astype(MXU_DTYPE)
        a = a - part.astype(F32)
        prod = lax.dot_general(part, s, dims, preferred_element_type=F32)
        out = prod if out is None else out + prod
    return out


def _sigmoid(x):
    return 0.5 * jnp.tanh(0.5 * x) + 0.5


def _softplus(x):
    return jnp.maximum(x, 0.0) + jnp.log(1.0 + jnp.exp(-jnp.abs(x)))


def _dsilu(x, s):
    return s * (1.0 + x * (1.0 - s))


def _rowsum(x):
    return jnp.sum(x, axis=1, keepdims=True)


def _colsum(x):
    return jnp.sum(x, axis=0, keepdims=True)


def _pick(dim, pref):
    if dim <= pref:
        return dim
    best = None
    t = LANES
    while t <= pref:
        if dim % t == 0:
            best = t
        t += LANES
    return dim if best is None else best


def _params(sem):
    return pltpu.CompilerParams(dimension_semantics=sem, vmem_limit_bytes=VMEM_LIMIT_V7X)


def _matmul(name, a, b, mode, out_dtype, tm=1024, tn=1024, tk=1024):
    if mode == "nn":
        (m, k), (_, n) = a.shape, b.shape
    elif mode == "nt":
        (m, k), (n, _) = a.shape, b.shape
    else:
        (k, m), (_, n) = a.shape, b.shape
    tm, tn, tk = _pick(m, tm), _pick(n, tn), _pick(k, tk)
    nk = k // tk
    if mode == "tn":
        a_spec = pl.BlockSpec((tk, tm), lambda i, j, kk: (kk, i))
    else:
        a_spec = pl.BlockSpec((tm, tk), lambda i, j, kk: (i, kk))
    if mode == "nt":
        b_spec = pl.BlockSpec((tn, tk), lambda i, j, kk: (j, kk))
    else:
        b_spec = pl.BlockSpec((tk, tn), lambda i, j, kk: (kk, j))
    dims = {"nn": NN, "nt": NT, "tn": TN}[mode]

    def body(a_ref, b_ref, o_ref, *acc):
        if nk == 1:
            o_ref[...] = _mm(a_ref[...], b_ref[...], dims).astype(out_dtype)
            return
        kk = pl.program_id(2)

        @pl.when(kk == 0)
        def _():
            acc[0][...] = jnp.zeros_like(acc[0])

        acc[0][...] += _mm(a_ref[...], b_ref[...], dims)

        @pl.when(kk == nk - 1)
        def _():
            o_ref[...] = acc[0][...].astype(out_dtype)

    return _pcall(
        body, name=name, grid=(m // tm, n // tn, nk),
        in_specs=[a_spec, b_spec],
        out_specs=pl.BlockSpec((tm, tn), lambda i, j, kk: (i, j)),
        out_shape=jax.ShapeDtypeStruct((m, n), out_dtype),
        scratch_shapes=[pltpu.VMEM((tm, tn), F32)] if nk > 1 else [],
        compiler_params=_params(("parallel", "parallel", "arbitrary")),
    )(a, b)


def _matmul_rows(name, a, b, mode, epilogue, row_ins, full_ins, outs, accs=(), tm=512, tk=1024, scatter_riders=()):
    if mode == "nn":
        (m, k), (_, n) = a.shape, b.shape
    else:
        (m, k), (n, _) = a.shape, b.shape
    tm, tk = _pick(m, tm), _pick(k, tk)
    nk = k // tk
    a_spec = pl.BlockSpec((tm, tk), lambda i, kk: (i, kk))
    b_spec = pl.BlockSpec((n, tk), lambda i, kk: (0, kk)) if mode == "nt" else pl.BlockSpec((tk, n), lambda i, kk: (kk, 0))
    dims = NT if mode == "nt" else NN
    n_row, n_full, n_out, n_acc = len(row_ins), len(full_ins), len(outs), len(accs)

    def body(a_ref, b_ref, *rest):
        ins = rest[:n_row + n_full]
        out_refs = rest[n_row + n_full:n_row + n_full + n_out]
        acc_refs = rest[n_row + n_full + n_out:n_row + n_full + n_out + n_acc]
        prod_scr = rest[-1]
        i, kk = pl.program_id(0), pl.program_id(1)

        if n_acc:
            @pl.when((i == 0) & (kk == 0))
            def _():
                for r in acc_refs:
                    r[...] = jnp.zeros_like(r)

        if nk == 1:
            epilogue(_mm(a_ref[...], b_ref[...], dims), *ins, *out_refs, *acc_refs)
            return

        @pl.when(kk == 0)
        def _():
            prod_scr[...] = jnp.zeros_like(prod_scr)

        prod_scr[...] += _mm(a_ref[...], b_ref[...], dims)

        @pl.when(kk == nk - 1)
        def _():
            epilogue(prod_scr[...], *ins, *out_refs, *acc_refs)

    grid = (m // tm, nk)
    riders = list(scatter_riders)
    n_in = 2 + n_row + n_full
    any_spec, rider_shapes, rider_sems, wrap = _riding_exchange(riders, True, n_in, n_out + n_acc, grid)
    row_ins = [r if isinstance(r, tuple) else (r, r.shape[1], 0) for r in row_ins]
    in_specs = [a_spec, b_spec] + [pl.BlockSpec((tm, w), lambda i, kk, cb=cb: (i, cb)) for _, w, cb in row_ins]
    in_specs += [pl.BlockSpec(f.shape, lambda i, kk, nd=f.ndim: (0,) * nd) for f in full_ins]
    row_ins = [r for r, _, _ in row_ins]
    out_specs = [pl.BlockSpec((tm, w), lambda i, kk: (i, 0)) for w, _ in outs]
    out_specs += [pl.BlockSpec(s, lambda i, kk: (0, 0)) for s in accs]
    out_shape = [jax.ShapeDtypeStruct((m, w), dt) for w, dt in outs] + [jax.ShapeDtypeStruct(s, F32) for s in accs]
    res = _pcall(
        wrap(body), name=name, grid=grid,
        in_specs=in_specs + any_spec, out_specs=out_specs + any_spec, out_shape=out_shape + rider_shapes,
        scratch_shapes=[pltpu.VMEM((tm, n), F32)] + rider_sems,
        compiler_params=_params(("arbitrary", "arbitrary")),
    )(a, b, *row_ins, *full_ins, *riders)
    return (res[:n_out + n_acc], res[n_out + n_acc:]) if riders else res


def _rowwise(name, body, n_rows, tm, ins, outs, accs=()):
    arrays, in_specs = [], []
    last8 = n_rows // SUBLANES - 1
    per = tm // SUBLANES
    for spec in ins:
        kind, arr = spec[0], spec[1]
        if kind == "full":
            in_specs.append(pl.BlockSpec(arr.shape, lambda i, nd=arr.ndim: (0,) * nd))
        else:
            w, cb = spec[2], spec[3]
            if kind == "row":
                in_specs.append(pl.BlockSpec((tm, w), lambda i, cb=cb: (i, cb)))
            elif kind == "prev":
                in_specs.append(pl.BlockSpec((SUBLANES, w), lambda i, cb=cb: (jnp.maximum(i * per - 1, 0), cb)))
            else:
                in_specs.append(pl.BlockSpec((SUBLANES, w), lambda i, cb=cb: (jnp.minimum((i + 1) * per, last8), cb)))
        arrays.append(arr)
    out_shape = [jax.ShapeDtypeStruct((n_rows, w), dt) for (w, dt) in outs]
    out_shape += [jax.ShapeDtypeStruct(s, F32) for s in accs]
    out_specs = [pl.BlockSpec((tm, w), lambda i: (i, 0)) for (w, _) in outs]
    out_specs += [pl.BlockSpec(s, lambda i: (0, 0)) for s in accs]
    n_io = len(ins) + len(outs)

    def kern(*refs):
        i = pl.program_id(0)
        if accs:
            @pl.when(i == 0)
            def _():
                for r in refs[n_io:]:
                    r[...] = jnp.zeros_like(r)
        body(i, *refs)

    res = _pcall(
        kern, name=name, grid=(n_rows // tm,), in_specs=in_specs, out_specs=out_specs, out_shape=out_shape,
        compiler_params=_params(("arbitrary",)),
    )(*arrays)
    return res


def _shift_down(x, halo, j):
    r = pltpu.roll(x, j, 0)
    hr = pltpu.roll(halo, j, 0)
    rows = lax.broadcasted_iota(jnp.int32, (SUBLANES, x.shape[1]), 0)
    top = jnp.where(rows < j, hr, r[0:SUBLANES])
    return jnp.concatenate([top, r[SUBLANES:]], axis=0)


def _shift_up(x, halo, j):
    tm = x.shape[0]
    r = pltpu.roll(x, tm - j, 0)
    hr = pltpu.roll(halo, SUBLANES - j, 0)
    rows = lax.broadcasted_iota(jnp.int32, (SUBLANES, x.shape[1]), 0)
    bot = jnp.where(rows >= SUBLANES - j, hr, r[tm - SUBLANES:])
    return jnp.concatenate([r[:tm - SUBLANES], bot], axis=0)


def _conv_taps(x, halo, kw):
    return [x if kw - 1 - k == 0 else _shift_down(x, halo, kw - 1 - k) for k in range(kw)]


def _conv(taps, w):
    y = taps[0] * w[0:1]
    for k in range(1, len(taps)):
        y = y + taps[k] * w[k:k + 1]
    return y


def _rms(x, width):
    r = lax.rsqrt(jnp.sum(x * x, axis=-1, keepdims=True) * (1.0 / width) + EPS)
    return x * r, r


def _rms_bwd(xh, r, dxh, width):
    return r * (dxh - xh * (jnp.sum(dxh * xh, axis=-1, keepdims=True) * (1.0 / width)))


def _seq_flags(i, seq, tm):
    nps = seq // tm
    pos = i % nps
    return jnp.where(pos == 0, 0.0, 1.0), jnp.where(pos == nps - 1, 0.0, 1.0)


def _norm_proj(x, w, wp, tm=1024, tn=1024):
    t, d = x.shape
    n = wp.shape[1]
    tm, tn = _pick(t, tm), _pick(n, tn)

    def body(x_ref, w_ref, b_ref, h_ref, o_ref, h_scr):
        @pl.when(pl.program_id(1) == 0)
        def _():
            xh, _ = _rms(x_ref[...], d)
            h = (xh * w_ref[...]).astype(BF16)
            h_scr[...] = h
            h_ref[...] = h

        o_ref[...] = _mm(h_scr[...], b_ref[...])

    return _pcall(
        body, name="mm_norm_proj", grid=(t // tm, n // tn),
        in_specs=[pl.BlockSpec((tm, d), lambda i, j: (i, 0)), pl.BlockSpec((1, d), lambda i, j: (0, 0)),
                  pl.BlockSpec((d, tn), lambda i, j: (0, j))],
        out_specs=[pl.BlockSpec((tm, d), lambda i, j: (i, 0)), pl.BlockSpec((tm, tn), lambda i, j: (i, j))],
        out_shape=[jax.ShapeDtypeStruct((t, d), BF16), jax.ShapeDtypeStruct((t, n), F32)],
        scratch_shapes=[pltpu.VMEM((tm, d), BF16)],
        compiler_params=_params(("parallel", "arbitrary")),
    )(x, w, wp)


def _gdn_prep(proj, cw, gp, seq, tm):
    t = proj.shape[0]
    d = D_MODEL

    def body(i, q_ref, qh_ref, k_ref, kh_ref, v_ref, vh_ref, sm_ref, cw_ref, gp_ref, qn_ref, kn_ref, vv_ref, gs_ref, ypre_ref):
        keep, _ = _seq_flags(i, seq, tm)
        for x_ref, h_ref, o_ref, off, scale in ((q_ref, qh_ref, qn_ref, 0, GDN_DK ** -0.5),
                                               (k_ref, kh_ref, kn_ref, d, 1.0), (v_ref, vh_ref, vv_ref, 2 * d, None)):
            y = _conv(_conv_taps(x_ref[...], h_ref[...] * keep, 4), cw_ref[:, off:off + d])
            ypre_ref[:, off:off + d] = y
            a = y * _sigmoid(y)
            if scale is None:
                o_ref[...] = a
            else:
                for hh in range(GDN_HEADS):
                    s = a[:, hh * GDN_DK:(hh + 1) * GDN_DK]
                    n = lax.rsqrt(_rowsum(s * s) + EPS)
                    o_ref[:, hh * GDN_DK:(hh + 1) * GDN_DK] = s * (n * scale)
        sm = sm_ref[...]
        lane = lax.broadcasted_iota(jnp.int32, sm.shape, 1)
        beta = _sigmoid(sm)
        g = jnp.where((lane >= 8) & (lane < 16), -jnp.exp(gp_ref[0:1, :]) * _softplus(sm + gp_ref[1:2, :]), 0.0)
        gs_ref[...] = jnp.where(lane < 8, beta, _chunk_cumsum(g, False))

    ins = []
    for cb in range(3):
        ins += [("row", proj, d, cb), ("prev", proj, d, cb)]
    ins += [("row", proj, LANES, SMALL_CB), ("full", cw), ("full", gp)]
    return _rowwise("gdn_prep", body, t, tm, ins, [(d, F32), (d, F32), (d, F32), (LANES, F32), (3 * d, F32)])


def _block_tri(tm, upper):
    ri = lax.broadcasted_iota(jnp.int32, (tm, tm), 0)
    ci = lax.broadcasted_iota(jnp.int32, (tm, tm), 1)
    tri = (ri <= ci) if upper else (ri >= ci)
    return (tri & ((ri // CHUNK) == (ci // CHUNK))).astype(F32)


def _chunk_cumsum(x, upper):
    tri = _block_tri(x.shape[0], upper).astype(MXU_DTYPE)
    out = None
    for _ in range(3):
        part = x.astype(MXU_DTYPE)
        x = x - part.astype(F32)
        prod = lax.dot_general(tri, part, NN, preferred_element_type=F32)
        out = prod if out is None else out + prod
    return out


def _chunk_consts():
    row = lax.broadcasted_iota(jnp.int32, (CHUNK, CHUNK), 0)
    col = lax.broadcasted_iota(jnp.int32, (CHUNK, CHUNK), 1)
    return dict(
        tril=row >= col, strict=row > col, eye=(row == col).astype(F32),
        lane=lax.broadcasted_iota(jnp.int32, (CHUNK, LANES), 1),
        row1=lax.broadcasted_iota(jnp.int32, (CHUNK, 1), 0),
        ones=jnp.ones((CHUNK, LANES), F32))


def _hmap(fn, *lists):
    return [fn(*a) for a in zip(*lists)]


def _tri_inv(nmats, eye):
    levels = CHUNK.bit_length() - 2
    x = [eye - n for n in nmats]
    p = _hmap(_mm3, nmats, nmats)
    for lvl in range(levels):
        x = _hmap(lambda xi, pi: xi + _mm3(xi, pi), x, p)
        if lvl < levels - 1:
            p = _hmap(_mm3, p, p)
    return x


def _gdn_gates(gs, gc_row, h, c):
    beta = _rowsum(jnp.where(c["lane"] == h, gs, 0.0))
    gc = _rowsum(jnp.where(c["lane"] == h + 8, gs, 0.0))
    dc = jnp.exp(jnp.where(c["tril"], gc - gc_row, -1e30))
    gl = gc[CHUNK - 1:CHUNK, :]
    return beta, dc, jnp.exp(gc), jnp.exp(gl), jnp.exp(gl - gc)


GDN_HB = GDN_HEADS


def _gdn_specs(seq, sb, hb, backward):
    assert hb == GDN_HEADS
    nsb = seq // sb
    ncb = sb // CHUNK
    order = (lambda j: nsb - 1 - j) if backward else (lambda j: j)
    specs = dict(
        wide=lambda: pl.BlockSpec((1, sb, hb * GDN_DK), lambda b, h, j: (b, order(j), h)),
        gs=lambda: pl.BlockSpec((1, sb, LANES), lambda b, h, j: (b, order(j), 0)),
        gr=pl.BlockSpec((1, ncb, GDN_HEADS, CHUNK), lambda b, h, j: (b, order(j), 0, 0)),
        st=pl.BlockSpec((1, hb, ncb * GDN_DK, GDN_DK), lambda b, h, j: (b, h, order(j), 0)),
        ti=pl.BlockSpec((1, hb, sb, CHUNK), lambda b, h, j: (b, h, order(j), 0)))
    return nsb, ncb, specs


def _riding_exchange(arrays, scatter, n_in, n_out, grid):
    n = len(arrays)
    if n == 0:
        return [], [], [], lambda body: body
    any_spec = [pl.BlockSpec(memory_space=pl.ANY)] * n

    def wrap(body):
        def wrapped(*refs):
            ins = refs[n_in:n_in + n]
            outs = refs[n_in + n + n_out:n_in + 2 * n + n_out]
            sems = refs[len(refs) - 3:]
            pid = [pl.program_id(a) for a in range(len(grid))]
            first = functools.reduce(lambda a, b: a & b, [p == 0 for p in pid])
            last = functools.reduce(lambda a, b: a & b, [p == g - 1 for p, g in zip(pid, grid)])

            @pl.when(first)
            def _():
                _exchange_phase(ins, outs, sems, scatter, start=True)

            body(*refs[:n_in], *refs[n_in + n:n_in + n + n_out], *refs[n_in + 2 * n + n_out:len(refs) - 3])

            @pl.when(last)
            def _():
                _exchange_phase(ins, outs, sems, scatter, start=False)

        return wrapped

    return any_spec, _exchange_out_shapes(arrays, scatter), _exchange_sems(n), wrap


def _gdn_chunk_fwd(qn, kn, vv, gs, gr, bsz, seq, sb, riders):
    hb = GDN_HB
    nsb, ncb, sp = _gdn_specs(seq, sb, hb, False)
    grid = (bsz, GDN_HEADS // hb, nsb)
    any_spec, rider_shapes, rider_sems, wrap = _riding_exchange(riders, False, 5, 3, grid)

    def body(q_ref, k_ref, v_ref, gs_ref, gr_ref, o_ref, st_ref, ti_ref, s_scr):
        @pl.when(pl.program_id(2) == 0)
        def _():
            s_scr[...] = jnp.zeros_like(s_scr)

        c = _chunk_consts()

        def chunk(n, carry):
            r = pl.ds(pl.multiple_of(n * CHUNK, CHUNK), CHUNK)
            rs = pl.ds(pl.multiple_of(n * GDN_DK, GDN_DK), GDN_DK)
            gsv = gs_ref[0, r, :]
            heads = list(range(hb))
            sls = [slice(ih * GDN_DK, (ih + 1) * GDN_DK) for ih in heads]
            q = [q_ref[0, r, sl] for sl in sls]
            k = [k_ref[0, r, sl] for sl in sls]
            v = [v_ref[0, r, sl] for sl in sls]
            beta, dc, eg, egl, ekd = zip(*[
                _gdn_gates(gsv, gr_ref[0, n, pl.ds(ih, 1), :], ih, c) for ih in heads])
            kb = _hmap(lambda a, b: a * b, k, beta)
            amat = _hmap(lambda a, b, d_: jnp.where(c["strict"], _mm(a, b, NT) * d_, 0.0), kb, k, dc)
            tinv = _tri_inv(amat, c["eye"])
            u = _hmap(lambda t_, a, b: _mm3(t_, a * b), tinv, v, beta)
            w = _hmap(lambda t_, a, b: _mm3(t_, a * b), tinv, kb, eg)
            qk = _hmap(lambda a, b, d_: _mm(a, b, NT) * d_, q, k, dc)
            s = [s_scr[ih] for ih in heads]
            v_new = _hmap(lambda a, b, s_: a - _mm(b, s_), u, w, s)
            o = _hmap(lambda a, e, s_, qk_, vn: _mm(a * e, s_) + _mm(qk_, vn), q, eg, s, qk, v_new)
            s_new = _hmap(lambda s_, e, a, f, vn: s_ * e + _mm(a * f, vn, TN), s, egl, k, ekd, v_new)
            for ih in heads:
                o_ref[0, r, sls[ih]] = o[ih]
                st_ref[0, ih, rs, :] = s[ih]
                ti_ref[0, ih, r, :] = tinv[ih]
                s_scr[ih] = s_new[ih]
            return carry

        lax.fori_loop(0, ncb, chunk, 0)

    t3 = (bsz, seq, D_MODEL)
    res = _pcall(
        wrap(body), name="gdn_chunk_fwd", grid=grid,
        in_specs=[sp["wide"](), sp["wide"](), sp["wide"](), sp["gs"](), sp["gr"]] + any_spec,
        out_specs=[sp["wide"](), sp["st"], sp["ti"]] + any_spec,
        out_shape=[jax.ShapeDtypeStruct(t3, F32),
                   jax.ShapeDtypeStruct((bsz, GDN_HEADS, (seq // CHUNK) * GDN_DK, GDN_DK), F32),
                   jax.ShapeDtypeStruct((bsz, GDN_HEADS, seq, CHUNK), F32)] + rider_shapes,
        scratch_shapes=[pltpu.VMEM((hb, GDN_DK, GDN_DK), F32)] + rider_sems,
        compiler_params=_params(("arbitrary", "arbitrary", "arbitrary")),
    )(qn, kn, vv, gs, gr, *riders)
    return res[:3], res[3:]


def _ssd_prep(proj, cw, cb, sp, seq, tm):
    t = proj.shape[0]
    d = D_MODEL
    ssd_w = SSD_HEADS * SSD_P

    def body(i, x_ref, xh_ref, bc_ref, bch_ref, sm_ref, cw_ref, cb_ref, sp_ref, xs_ref, bco_ref, dtx_ref, acsx_ref, acs_ref, ypre_ref):
        keep, _ = _seq_flags(i, seq, tm)
        y = _conv(_conv_taps(x_ref[...], xh_ref[...] * keep, 4), cw_ref[:, 0:d]) + cb_ref[:, 0:d]
        ypre_ref[:, 0:d] = y
        xs_ref[...] = y * _sigmoid(y)
        y = _conv(_conv_taps(bc_ref[...], bch_ref[...] * keep, 4), cw_ref[:, d:d + 512]) + cb_ref[:, d:d + 512]
        ypre_ref[:, d:d + 512] = y
        bco_ref[...] = y * _sigmoid(y)
        sm = sm_ref[...]
        lane = lax.broadcasted_iota(jnp.int32, sm.shape, 1)
        valid = (lane >= 16) & (lane < 32)
        dt = jnp.where(valid, _softplus(sm + sp_ref[1:2, :]), 0.0)
        adt = dt * (-jnp.exp(sp_ref[0:1, :]))
        acs = _chunk_cumsum(adt, False)
        l64 = lax.broadcasted_iota(jnp.int32, (LANES, ssd_w), 0)
        d64 = lax.broadcasted_iota(jnp.int32, (LANES, ssd_w), 1)
        e64 = (l64 - 16 == d64 // SSD_P).astype(F32)
        dtx_ref[...] = _mmsel(dt, e64, terms=3)
        acsx_ref[...] = _mmsel(acs, e64, terms=3)
        acs_ref[...] = acs

    ins = [("row", proj, d, 5), ("prev", proj, d, 5), ("row", proj, 512, 12), ("prev", proj, 512, 12),
           ("row", proj, LANES, SMALL_CB), ("full", cw), ("full", cb), ("full", sp)]
    return _rowwise("ssd_prep", body, t, tm, ins,
                    [(d, F32), (512, F32), (ssd_w, F32), (ssd_w, F32), (LANES, F32), (d + 512, F32)])


SSD_GW = SSD_HPG * SSD_P


def _ssd_head(acs, ar_ref, n, head, cbm, c):
    col = _rowsum(jnp.where(c["lane"] == head + 16, acs, 0.0))
    lm = jnp.exp(jnp.where(c["tril"], col - ar_ref[0, n, pl.ds(head, 1), :], -1e30))
    return lm, cbm * lm


def _ssd_specs(seq, sb):
    nsb = seq // sb
    ncb = sb // CHUNK
    def specs(order):
        return dict(
            wide=lambda: pl.BlockSpec((1, sb, SSD_HEADS * SSD_P), lambda b, j: (b, order(j), 0)),
            bc=lambda: pl.BlockSpec((1, sb, 2 * SSD_GROUPS * SSD_N), lambda b, j: (b, order(j), 0)),
            small=lambda: pl.BlockSpec((1, sb, LANES), lambda b, j: (b, order(j), 0)),
            ar=pl.BlockSpec((1, ncb, SSD_HEADS, CHUNK), lambda b, j: (b, order(j), 0, 0)),
            st=pl.BlockSpec((1, ncb * SSD_N, SSD_HEADS * SSD_P), lambda b, j: (b, order(j), 0)))
    return nsb, ncb, specs(lambda j: j), specs(lambda j: nsb - 1 - j)


def _ssd_chunk_fwd(xs, bc, dtx, acsx, acs, ar, bsz, seq, sb):
    nsb, ncb, sp, _ = _ssd_specs(seq, sb)

    def body(x_ref, dtx_ref, ax_ref, bc_ref, acs_ref, ar_ref, y_ref, sts_ref, st_scr):
        @pl.when(pl.program_id(1) == 0)
        def _():
            st_scr[...] = jnp.zeros_like(st_scr)

        c = _chunk_consts()
        lane5 = lax.broadcasted_iota(jnp.int32, (CHUNK, SSD_GW), 1) // SSD_P

        def chunk(n, carry):
            r = pl.ds(pl.multiple_of(n * CHUNK, CHUNK), CHUNK)
            rs = pl.ds(pl.multiple_of(n * SSD_N, SSD_N), SSD_N)
            acsv = acs_ref[0, r, :]
            for g in range(SSD_GROUPS):
                gl = slice(g * SSD_GW, (g + 1) * SSD_GW)
                x, dt, ax = x_ref[0, r, gl], dtx_ref[0, r, gl], ax_ref[0, r, gl]
                bm = bc_ref[0, r, g * SSD_N:(g + 1) * SSD_N]
                cm = bc_ref[0, r, (SSD_GROUPS + g) * SSD_N:(SSD_GROUPS + g + 1) * SSD_N]
                xdt = x * dt
                cbm = _mm(cm, bm, NT)
                al = ax[CHUNK - 1:CHUNK, :]
                st = st_scr[:, gl]
                y = _mm(cm, st) * jnp.exp(ax)
                for hh in range(SSD_HPG):
                    _, gm = _ssd_head(acsv, ar_ref, n, g * SSD_HPG + hh, cbm, c)
                    y = y + _mm(gm, jnp.where(lane5 == hh, xdt, 0.0))
                y_ref[0, r, gl] = y
                sts_ref[0, rs, gl] = st
                st_scr[:, gl] = st * jnp.exp(al) + _mm(bm, xdt * jnp.exp(al - ax), TN)
            return carry

        lax.fori_loop(0, ncb, chunk, 0)

    return _pcall(
        body, name="ssd_chunk_fwd", grid=(bsz, nsb),
        in_specs=[sp["wide"](), sp["wide"](), sp["wide"](), sp["bc"](), sp["small"](), sp["ar"]],
        out_specs=[sp["wide"](), sp["st"]],
        out_shape=[jax.ShapeDtypeStruct((bsz, seq, SSD_HEADS * SSD_P), F32),
                   jax.ShapeDtypeStruct((bsz, (seq // CHUNK) * SSD_N, SSD_HEADS * SSD_P), F32)],
        scratch_shapes=[pltpu.VMEM((SSD_N, SSD_HEADS * SSD_P), F32)],
        compiler_params=_params(("parallel", "arbitrary")),
    )(xs, dtx, acsx, bc, acs, ar)


def _gate_norm(o_gdn, y_ssd, xs, proj, gnw, snw, dvec, tm):
    t = o_gdn.shape[0]
    d = D_MODEL

    def body(i, o_ref, za_ref, y_ref, xs_ref, zs_ref, gnw_ref, snw_ref, dv_ref, out_ref):
        for hh in range(GDN_HEADS):
            sl = slice(hh * GDN_DK, (hh + 1) * GDN_DK)
            oh, _ = _rms(o_ref[:, sl], GDN_DK)
            z = za_ref[:, sl]
            out_ref[:, sl] = (oh * gnw_ref[...] * (z * _sigmoid(z))).astype(BF16)
        zs = zs_ref[...]
        yg = (y_ref[...] + dv_ref[...] * xs_ref[...]) * (zs * _sigmoid(zs))
        for g in range(SSD_GROUPS):
            sl = slice(g * 512, (g + 1) * 512)
            yh, _ = _rms(yg[:, sl], 512)
            out_ref[:, d + g * 512:d + (g + 1) * 512] = (yh * snw_ref[:, sl]).astype(BF16)

    ins = [("row", o_gdn, d, 0), ("row", proj, d, 3), ("row", y_ssd, d, 0), ("row", xs, d, 0), ("row", proj, d, 4),
           ("full", gnw), ("full", snw), ("full", dvec)]
    return _rowwise("gate_norm", body, t, tm, ins, [(2 * d, BF16)])[0]


def _out_mid(mixin, w_out, x, pmw, pfw):
    d = D_MODEL

    def epilogue(mix, x_ref, pmw_ref, pfw_ref, mix_ref, x1_ref, h2_ref):
        mix_ref[...] = mix
        mh, _ = _rms(mix, d)
        x1 = x_ref[...] + mh * pmw_ref[...]
        x1_ref[...] = x1
        xh, _ = _rms(x1, d)
        h2_ref[...] = (xh * pfw_ref[...]).astype(BF16)

    return _matmul_rows("mm_out_mid", mixin, w_out, "nn", epilogue, [x], [pmw, pfw], [(d, F32), (d, F32), (d, BF16)],
                        tk=2 * d)


def _ffn_act(u_pre, cw, cb, seq, tm):
    t = u_pre.shape[0]

    def body(i, ug_ref, ugh_ref, uu_ref, uuh_ref, cw_ref, cb_ref, act_ref, u_ref):
        keep, _ = _seq_flags(i, seq, tm)
        gate = _conv(_conv_taps(ug_ref[...], ugh_ref[...] * keep, 3), cw_ref[:, 0:D_FF]) + cb_ref[:, 0:D_FF]
        up = _conv(_conv_taps(uu_ref[...], uuh_ref[...] * keep, 3), cw_ref[:, D_FF:2 * D_FF]) + cb_ref[:, D_FF:2 * D_FF]
        u_ref[:, 0:D_FF] = gate
        u_ref[:, D_FF:2 * D_FF] = up
        act_ref[...] = (gate * _sigmoid(gate) * up).astype(BF16)

    ins = [("row", u_pre, D_FF, 0), ("prev", u_pre, D_FF, 0), ("row", u_pre, D_FF, 1), ("prev", u_pre, D_FF, 1),
           ("full", cw), ("full", cb)]
    return _rowwise("ffn_act", body, t, tm, ins, [(D_FF, BF16), (2 * D_FF, F32)])


def _down_final(act, w_down, x1, tgt, w):
    d = D_MODEL

    def epilogue(f, x1_ref, t_ref, w_ref, dy_ref, df_ref, loss_ref, dw_ref):
        fh, r = _rms(f, d)
        e = x1_ref[...] + fh * w_ref[...] - t_ref[...]
        loss_ref[...] += _colsum(e * e) * (0.5 / d)
        dy = e * (1.0 / d)
        dy_ref[...] = dy
        dw_ref[...] += _colsum(dy * fh)
        df_ref[...] = _rms_bwd(fh, r, dy * w_ref[...], d).astype(BF16)

    return _matmul_rows("mm_down_final", act, w_down, "nn", epilogue, [x1, tgt], [w], [(d, F32), (d, BF16)],
                        accs=[(1, d), (1, d)], tk=D_FF)


def _ffn_bwd(u, u_pre, dact, cw, seq, tm):
    t = u.shape[0]

    def body(i, g_ref, gn_ref, up_ref, upn_ref, xg_ref, xu_ref, da_ref, dan_ref, cw_ref, dpre_ref, dcw_ref, dcb_ref):
        _, keep_next = _seq_flags(i, seq, tm)
        ext = lambda a_ref, n_ref: jnp.concatenate([a_ref[...], n_ref[...]], axis=0)
        rows = tm + SUBLANES
        gate, up = ext(g_ref, gn_ref), ext(up_ref, upn_ref)
        sg = _sigmoid(gate)
        da = jnp.concatenate([da_ref[...], dan_ref[...] * keep_next], axis=0)
        for off, grad, x_ref in ((0, da * up * _dsilu(gate, sg), xg_ref), (D_FF, da * gate * sg, xu_ref)):
            x = x_ref[...]
            own = grad[0:tm]
            acc = own * cw_ref[2:3, off:off + D_FF]
            dcb_ref[:, off:off + D_FF] += _colsum(own)
            dcw_ref[2:3, off:off + D_FF] += _colsum(own * x)
            for j in (1, 2):
                ahead = pltpu.roll(grad, rows - j, 0)[0:tm]
                acc = acc + ahead * cw_ref[2 - j:3 - j, off:off + D_FF]
                dcw_ref[2 - j:3 - j, off:off + D_FF] += _colsum(ahead * x)
            dpre_ref[:, off:off + D_FF] = acc.astype(BF16)

    ins = []
    for cb_ in range(2):
        ins += [("row", u, D_FF, cb_), ("next", u, D_FF, cb_)]
    ins += [("row", u_pre, D_FF, 0), ("row", u_pre, D_FF, 1), ("row", dact, D_FF, 0), ("next", dact, D_FF, 0), ("full", cw)]
    return _rowwise("ffn_bwd", body, t, tm, ins, [(2 * D_FF, BF16)], accs=[(SUBLANES, 2 * D_FF), (1, 2 * D_FF)])


def _assemble_dproj(dpre_qkv, dza, dzs, dpre_xbc, dsm, proj, gcw, scw, seq, tm):
    t = dza.shape[0]
    d = D_MODEL

    def body(i, dq_ref, dqn_ref, dk_ref, dkn_ref, dv_ref, dvn_ref, dx_ref, dxn_ref, dbc_ref, dbcn_ref, dza_ref, dzs_ref,
             dsm_ref, xq_ref, xk_ref, xv_ref, xx_ref, xbc_ref, gcw_ref, scw_ref, o_ref, dgcw_ref, dscw_ref):
        _, keep = _seq_flags(i, seq, tm)
        pieces = [(g_ref, n_ref, gcw_ref, dgcw_ref, x_ref, 0, c0) for g_ref, n_ref, x_ref, c0 in (
            (dq_ref, dqn_ref, xq_ref, 0), (dk_ref, dkn_ref, xk_ref, d), (dv_ref, dvn_ref, xv_ref, 2 * d))]
        pieces += [(g_ref, n_ref, scw_ref, dscw_ref, x_ref, 5 * d, c0) for g_ref, n_ref, x_ref, c0 in (
            (dx_ref, dxn_ref, xx_ref, 0), (dbc_ref, dbcn_ref, xbc_ref, d))]
        for d_ref, n_ref, cw_ref, dcw_ref, x_ref, base, c0 in pieces:
            w = x_ref.shape[1]
            x = x_ref[...]
            g = d_ref[...]
            halo = n_ref[...] * keep
            acc = g * cw_ref[3:4, c0:c0 + w]
            dcw_ref[3:4, c0:c0 + w] += _colsum(g * x)
            for j in range(1, 4):
                ahead = _shift_up(g, halo, j)
                acc = acc + ahead * cw_ref[3 - j:4 - j, c0:c0 + w]
                dcw_ref[3 - j:4 - j, c0:c0 + w] += _colsum(ahead * x)
            o_ref[:, base + c0:base + c0 + w] = acc.astype(BF16)
        o_ref[:, 3 * d:4 * d] = dza_ref[...]
        o_ref[:, 4 * d:5 * d] = dzs_ref[...]
        o_ref[:, 6 * d + 512:6 * d + 512 + LANES] = dsm_ref[...]
        o_ref[:, 6 * d + 512 + LANES:PROJ_W] = jnp.zeros((tm, PROJ_W - (6 * d + 512 + LANES)), BF16)

    ins = []
    for g in tuple(dpre_qkv) + tuple(dpre_xbc):
        ins += [("row", g, g.shape[1], 0), ("next", g, g.shape[1], 0)]
    ins += [("row", dza, d, 0), ("row", dzs, d, 0), ("row", dsm, LANES, 0),
           ("row", proj, d, 0), ("row", proj, d, 1), ("row", proj, d, 2), ("row", proj, d, 5), ("row", proj, 512, 12),
           ("full", gcw), ("full", scw)]
    return _rowwise("assemble_dproj", body, t, tm, ins, [(PROJ_W, BF16)], accs=[(SUBLANES, 3 * d), (SUBLANES, d + 512)])


def _dh2_mid_bwd(du_pre, w_up, x1, mix, dy, pmw, pfw):
    d = D_MODEL

    def epilogue(dh2, x1_ref, mix_ref, dy_ref, pmw_ref, pfw_ref, dx1_ref, dmix_ref, dpm_ref, dpf_ref):
        xh, r2 = _rms(x1_ref[...], d)
        dpf_ref[...] += _colsum(dh2 * xh)
        dx1 = dy_ref[...] + _rms_bwd(xh, r2, dh2 * pfw_ref[...], d)
        dx1_ref[...] = dx1
        mh, r = _rms(mix_ref[...], d)
        dpm_ref[...] += _colsum(dx1 * mh)
        dmix_ref[...] = _rms_bwd(mh, r, dx1 * pmw_ref[...], d).astype(BF16)

    return _matmul_rows("mm_dh2_mid_bwd", du_pre, w_up, "nt", epilogue, [x1, mix, dy], [pmw, pfw],
                        [(d, F32), (d, BF16)], accs=[(1, d), (1, d)], tk=D_FF)


def _dmixin_gate_norm_bwd(dmix, w_out, o_gdn, y_ssd, xs, proj, gnw, snw, dvec):
    d = D_MODEL

    def epilogue(dmixin, o_ref, za_ref, y_ref, xs_ref, zs_ref, gnw_ref, snw_ref, dv_ref,
                 do_ref, dza_ref, dy_ref, dxs_ref, dzs_ref, dgnw_ref, dsnw_ref, dd_ref):
        for hh in range(GDN_HEADS):
            sl = slice(hh * GDN_DK, (hh + 1) * GDN_DK)
            oh, r = _rms(o_ref[:, sl], GDN_DK)
            z = za_ref[:, sl]
            sz = _sigmoid(z)
            dm = dmixin[:, sl]
            don = dm * (z * sz)
            dza_ref[:, sl] = (dm * oh * gnw_ref[...] * _dsilu(z, sz)).astype(BF16)
            dgnw_ref[...] += _colsum(don * oh)
            do_ref[:, sl] = _rms_bwd(oh, r, don * gnw_ref[...], GDN_DK)
        zs = zs_ref[...]
        sz = _sigmoid(zs)
        sil = zs * sz
        x = xs_ref[...]
        y0 = y_ref[...] + dv_ref[...] * x
        yg = y0 * sil
        dms = dmixin[:, d:2 * d]
        for g in range(SSD_GROUPS):
            sl = slice(g * 512, (g + 1) * 512)
            yh, r = _rms(yg[:, sl], 512)
            dsnw_ref[:, sl] += _colsum(dms[:, sl] * yh)
            dyg = _rms_bwd(yh, r, dms[:, sl] * snw_ref[:, sl], 512)
            dy0 = dyg * sil[:, sl]
            dzs_ref[:, sl] = (dyg * y0[:, sl] * _dsilu(zs[:, sl], sz[:, sl])).astype(BF16)
            dy_ref[:, sl] = dy0
            dxs_ref[:, sl] = dy0 * dv_ref[:, sl]
            dd_ref[:, sl] += _colsum(dy0 * x[:, sl])

    row_ins = [o_gdn, (proj, d, 3), y_ssd, xs, (proj, d, 4)]
    return _matmul_rows("mm_dmixin_gate_norm_bwd", dmix, w_out, "nt", epilogue, row_ins, [gnw, snw, dvec],
                        [(d, F32), (d, BF16), (d, F32), (d, F32), (d, BF16)], accs=[(1, GDN_DK), (1, d), (1, d)], tm=256)


def _ssd_chunk_bwd(xs, bc, dtx, acsx, acs, ar, dy, sts, ypre, dxs_d, bsz, seq, sb):
    nsb, ncb, _, sp = _ssd_specs(seq, sb)
    bc_w = 2 * SSD_GROUPS * SSD_N
    x_w = SSD_HEADS * SSD_P

    def body(x_ref, dtx_ref, ax_ref, bc_ref, acs_ref, ar_ref, dy_ref, sts_ref, yx_ref, ybc_ref, dxd_ref,
             dx_ref, dbc_ref, ddt_ref, dacs_ref, dbx_ref, dbbc_ref, dst_scr):
        @pl.when(pl.program_id(1) == 0)
        def _():
            dst_scr[...] = jnp.zeros_like(dst_scr)

        @pl.when((pl.program_id(0) == 0) & (pl.program_id(1) == 0))
        def _():
            dbx_ref[...] = jnp.zeros_like(dbx_ref)
            dbbc_ref[...] = jnp.zeros_like(dbbc_ref)

        def to_conv_out(grad, y):
            return grad * _dsilu(y, _sigmoid(y))

        c = _chunk_consts()
        lane5 = lax.broadcasted_iota(jnp.int32, (CHUNK, SSD_GW), 1) // SSD_P
        row5 = lax.broadcasted_iota(jnp.int32, (CHUNK, SSD_GW), 0)
        sel_in = lax.broadcasted_iota(jnp.int32, (SSD_GW, LANES), 0) // SSD_P
        sel_out = lax.broadcasted_iota(jnp.int32, (SSD_GW, LANES), 1)

        def chunk(nn, carry):
            n = ncb - 1 - nn
            r = pl.ds(pl.multiple_of(n * CHUNK, CHUNK), CHUNK)
            rs = pl.ds(pl.multiple_of(n * SSD_N, SSD_N), SSD_N)
            acsv = acs_ref[0, r, :]
            ddt = jnp.zeros((CHUNK, LANES), F32)
            dacs = jnp.zeros((CHUNK, LANES), F32)
            for g in range(SSD_GROUPS):
                gl = slice(g * SSD_GW, (g + 1) * SSD_GW)
                x, dt, ax, dyv = x_ref[0, r, gl], dtx_ref[0, r, gl], ax_ref[0, r, gl], dy_ref[0, r, gl]
                bm = bc_ref[0, r, g * SSD_N:(g + 1) * SSD_N]
                cm = bc_ref[0, r, (SSD_GROUPS + g) * SSD_N:(SSD_GROUPS + g + 1) * SSD_N]
                st = sts_ref[0, rs, gl]
                dst = dst_scr[:, gl]
                rsel = (sel_in + (16 + g * SSD_HPG) == sel_out).astype(F32)
                xdt = x * dt
                cbm = _mm(cm, bm, NT)
                al = ax[CHUNK - 1:CHUNK, :]
                ex, el = jnp.exp(ax), jnp.exp(al)
                dec = jnp.exp(al - ax)
                xd = xdt * dec
                dye = dyv * ex
                dxd = _mm(bm, dst)
                dxdt = dec * dxd
                dcm = _mm(dye, st, NT)
                dbm = _mm(xd, dst, NT)
                z = dye * _mm(cm, st) - dxd * xd
                zl = _colsum(dst * st) * el + _colsum(dxd * xd)
                z = z + jnp.where(row5 == CHUNK - 1, zl, 0.0)
                dcb = jnp.zeros((CHUNK, CHUNK), F32)
                for hh in range(SSD_HPG):
                    head = g * SSD_HPG + hh
                    lm, gm = _ssd_head(acsv, ar_ref, n, head, cbm, c)
                    dym = jnp.where(lane5 == hh, dyv, 0.0)
                    dxdt = dxdt + _mm(gm, dym, TN)
                    dg = _mm(dym, xdt, NT)
                    dcb = dcb + dg * lm
                    pm = dg * gm
                    dacs = dacs + jnp.where(c["lane"] == head + 16, _rowsum(pm) - _mmsel(pm, c["ones"], TN), 0.0)
                for sl, grad in ((slice((SSD_GROUPS + g) * SSD_N, (SSD_GROUPS + g + 1) * SSD_N), dcm + _mm(dcb, bm)),
                                 (slice(g * SSD_N, (g + 1) * SSD_N), dbm + _mm(dcb, cm, TN))):
                    dpre = to_conv_out(grad, ybc_ref[0, r, sl])
                    dbc_ref[0, r, sl] = dpre
                    dbbc_ref[:, sl] += _colsum(dpre)
                dacs = dacs + _mmsel(z, rsel)
                ddt = ddt + _mmsel(dxdt * x, rsel)
                dpre = to_conv_out(dxdt * dt + dxd_ref[0, r, gl], yx_ref[0, r, gl])
                dx_ref[0, r, gl] = dpre
                dbx_ref[:, gl] += _colsum(dpre)
                dst_scr[:, gl] = dst * el + _mm(cm, dye, TN)
            ddt_ref[0, r, :] = ddt
            dacs_ref[0, r, :] = dacs
            return carry

        lax.fori_loop(0, ncb, chunk, 0)

    nsb_rev = lambda j: nsb - 1 - j
    return _pcall(
        body, name="ssd_chunk_bwd", grid=(bsz, nsb),
        in_specs=[sp["wide"](), sp["wide"](), sp["wide"](), sp["bc"](), sp["small"](), sp["ar"], sp["wide"](), sp["st"],
                  sp["wide"](), pl.BlockSpec((1, sb, bc_w), lambda b, j: (b, nsb_rev(j), x_w // bc_w)), sp["wide"]()],
        out_specs=[sp["wide"](), sp["bc"](), sp["small"](), sp["small"](),
                   pl.BlockSpec((1, x_w), lambda b, j: (0, 0)), pl.BlockSpec((1, bc_w), lambda b, j: (0, 0))],
        out_shape=[jax.ShapeDtypeStruct((bsz, seq, x_w), F32), jax.ShapeDtypeStruct((bsz, seq, bc_w), F32),
                   jax.ShapeDtypeStruct((bsz, seq, LANES), F32), jax.ShapeDtypeStruct((bsz, seq, LANES), F32),
                   jax.ShapeDtypeStruct((1, x_w), F32), jax.ShapeDtypeStruct((1, bc_w), F32)],
        scratch_shapes=[pltpu.VMEM((SSD_N, x_w), F32)],
        compiler_params=_params(("arbitrary", "arbitrary")),
    )(xs, dtx, acsx, bc, acs, ar, dy, sts, ypre, ypre, dxs_d)


def _through_norm_silu(g, y, scale):
    sy = _sigmoid(y)
    ds_ = _dsilu(y, sy)
    if scale is None:
        return g * ds_
    a = y * sy
    n = lax.rsqrt(_rowsum(a * a) + EPS)
    ah = a * n
    return (scale * n) * (g - ah * _rowsum(g * ah)) * ds_


def _gdn_chunk_bwd(qn, kn, vv, gs, gr, do, sts, tis, ypre, bsz, seq, sb, riders):
    hb = GDN_HB
    nsb, ncb, sp = _gdn_specs(seq, sb, hb, True)
    grid = (bsz, GDN_HEADS // hb, nsb)
    any_spec, rider_shapes, rider_sems, wrap = _riding_exchange(riders, True, 11, 4, grid)
    ypre_spec = lambda cb: pl.BlockSpec((1, sb, hb * GDN_DK), lambda b, h, j: (b, nsb - 1 - j, cb))

    def body(q_ref, k_ref, v_ref, gs_ref, gr_ref, do_ref, st_ref, ti_ref, yq_ref, yk_ref, yv_ref,
             dq_ref, dk_ref, dv_ref, dgs_ref, ds_scr):
        @pl.when(pl.program_id(2) == 0)
        def _():
            ds_scr[...] = jnp.zeros_like(ds_scr)

        c = _chunk_consts()

        def chunk(nn, carry):
            n = ncb - 1 - nn
            r = pl.ds(pl.multiple_of(n * CHUNK, CHUNK), CHUNK)
            rs = pl.ds(pl.multiple_of(n * GDN_DK, GDN_DK), GDN_DK)
            gsv = gs_ref[0, r, :]
            heads = list(range(hb))
            sls = [slice(ih * GDN_DK, (ih + 1) * GDN_DK) for ih in heads]
            q = [q_ref[0, r, sl] for sl in sls]
            k = [k_ref[0, r, sl] for sl in sls]
            v = [v_ref[0, r, sl] for sl in sls]
            do_ = [do_ref[0, r, sl] for sl in sls]
            s = [st_ref[0, ih, rs, :] for ih in heads]
            tinv = [ti_ref[0, ih, r, :] for ih in heads]
            dsn = [ds_scr[ih] for ih in heads]
            beta, dc, eg, egl, ekd = zip(*[
                _gdn_gates(gsv, gr_ref[0, n, pl.ds(ih, 1), :], ih, c) for ih in heads])
            mul = lambda a, b: a * b
            kb = _hmap(mul, k, beta)
            rhs_w = _hmap(mul, kb, eg)
            u = _hmap(lambda t_, a, b: _mm3(t_, a * b), tinv, v, beta)
            w = _hmap(_mm3, tinv, rhs_w)
            amat = _hmap(lambda a, b, d_: jnp.where(c["strict"], _mm(a, b, NT) * d_, 0.0), kb, k, dc)
            qk = _hmap(lambda a, b, d_: _mm(a, b, NT) * d_, q, k, dc)
            qd = _hmap(mul, q, eg)
            kd = _hmap(mul, k, ekd)
            v_new = _hmap(lambda a, b, s_: a - _mm(b, s_), u, w, s)
            dv_new = _hmap(lambda qk_, d_, kd_, dn: _mm(qk_, d_, TN) + _mm(kd_, dn), qk, do_, kd, dsn)
            dqk = _hmap(lambda d_, vn: _mm(d_, vn, NT), do_, v_new)
            dqd = _hmap(lambda d_, s_: _mm(d_, s_, NT), do_, s)
            ds_new = _hmap(lambda qd_, d_, dn, e, w_, dvn: _mm(qd_, d_, TN) + dn * e - _mm(w_, dvn, TN),
                           qd, do_, dsn, egl, w, dv_new)
            dkd = _hmap(lambda vn, dn: _mm(vn, dn, NT), v_new, dsn)
            dgl = _hmap(lambda s_, dn, e: _colsum(_rowsum(s_ * dn)) * e, s, dsn, egl)
            dw = _hmap(lambda dvn, s_: -_mm(dvn, s_, NT), dv_new, s)
            dru = _hmap(lambda t_, a: _mm3(t_, a, TN), tinv, dv_new)
            drw = _hmap(lambda t_, a: _mm3(t_, a, TN), tinv, dw)
            da = _hmap(lambda a, u_, b, w_: jnp.where(c["strict"], -(_mm(a, u_, NT) + _mm(b, w_, NT)), 0.0), dru, u, drw, w)
            m = _hmap(mul, da, dc)
            dkb = _hmap(lambda a, e, m_, k_: a * e + _mm(m_, k_), drw, eg, m, k)
            mq = _hmap(mul, dqk, dc)
            dq = _hmap(lambda mq_, k_, a, e: _mm(mq_, k_) + a * e, mq, k, dqd, eg)
            dk = _hmap(lambda m_, kb_, mq_, q_, a, e, b, be: _mm(m_, kb_, TN) + _mm(mq_, q_, TN) + a * e + b * be,
                       m, kb, mq, q, dkd, ekd, dkb, beta)
            dbeta = _hmap(lambda a, v_, b, k_: _rowsum(a * v_) + _rowsum(b * k_), dru, v, dkb, k)
            pq = _hmap(lambda a, am, b, qk_: a * am + b * qk_, da, amat, dqk, qk)
            ekk = _hmap(lambda a, b: _rowsum(a * b), dkd, kd)
            dgc = _hmap(lambda pq_, a, rw, b, qd_, e, gl_: (
                _rowsum(pq_) - _mmsel(pq_, c["ones"], TN) + (_rowsum(a * rw) + _rowsum(b * qd_) - e)
                + jnp.where(c["row1"] == CHUNK - 1, _colsum(e) + gl_, 0.0)), pq, drw, rhs_w, dqd, qd, ekk, dgl)
            dv = _hmap(mul, dru, beta)
            dyq = _hmap(lambda g_, sl: _through_norm_silu(g_, yq_ref[0, r, sl], GDN_DK ** -0.5), dq, sls)
            dyk = _hmap(lambda g_, sl: _through_norm_silu(g_, yk_ref[0, r, sl], 1.0), dk, sls)
            dyv = _hmap(lambda g_, sl: _through_norm_silu(g_, yv_ref[0, r, sl], None), dv, sls)
            dgs = jnp.zeros((CHUNK, LANES), F32)
            for ih in heads:
                ds_scr[ih] = ds_new[ih]
                dq_ref[0, r, sls[ih]] = dyq[ih]
                dk_ref[0, r, sls[ih]] = dyk[ih]
                dv_ref[0, r, sls[ih]] = dyv[ih]
                dgs = dgs + jnp.where(c["lane"] == ih, dbeta[ih], jnp.where(c["lane"] == ih + 8, dgc[ih], 0.0))
            dgs_ref[0, r, :] = dgs
            return carry

        lax.fori_loop(0, ncb, chunk, 0)

    res = _pcall(
        wrap(body), name="gdn_chunk_bwd", grid=grid,
        in_specs=[sp["wide"](), sp["wide"](), sp["wide"](), sp["gs"](), sp["gr"], sp["wide"](), sp["st"], sp["ti"],
                  ypre_spec(0), ypre_spec(1), ypre_spec(2)] + any_spec,
        out_specs=[sp["wide"](), sp["wide"](), sp["wide"](), sp["gs"]()] + any_spec,
        out_shape=[jax.ShapeDtypeStruct((bsz, seq, D_MODEL), F32)] * 3 + [jax.ShapeDtypeStruct((bsz, seq, LANES), F32)]
        + rider_shapes,
        scratch_shapes=[pltpu.VMEM((hb, GDN_DK, GDN_DK), F32)] + rider_sems,
        compiler_params=_params(("arbitrary", "arbitrary", "arbitrary")),
    )(qn, kn, vv, gs, gr, do, sts, tis, ypre, ypre, ypre, *riders)
    return res[:4], res[4:]


def _gates_bwd(proj, dgs, ddt, dacs, gp, sp, tm):
    t = proj.shape[0]

    def body(i, sm_ref, dgs_ref, ddt_ref, dacs_ref, gp_ref, sp_ref, dsm_ref, dgp_ref, dsp_ref):
        sm = sm_ref[...]
        lane = lax.broadcasted_iota(jnp.int32, sm.shape, 1)
        is_g = (lane >= 8) & (lane < 16)
        is_dt = (lane >= 16) & (lane < 32)
        dgs = dgs_ref[...]
        back = _chunk_cumsum(jnp.where(is_g, dgs, 0.0) + dacs_ref[...], True)
        beta = _sigmoid(sm)
        bias = gp_ref[1:2, :] + sp_ref[1:2, :]
        xb = sm + bias
        soft, dsoft = _softplus(xb), _sigmoid(xb)
        g_neg = -jnp.exp(gp_ref[0:1, :])
        a_neg = -jnp.exp(sp_ref[0:1, :])
        dg = jnp.where(is_g, back * g_neg, 0.0)
        dxb_g = dg * dsoft
        dxb_dt = jnp.where(is_dt, (ddt_ref[...] + back * a_neg) * dsoft, 0.0)
        dsm_ref[...] = (jnp.where(lane < 8, dgs * beta * (1.0 - beta), dxb_g) + dxb_dt).astype(BF16)
        dgp_ref[1:2, :] += _colsum(dxb_g)
        dgp_ref[0:1, :] += _colsum(dg * soft)
        dsp_ref[1:2, :] += _colsum(dxb_dt)
        dsp_ref[0:1, :] += jnp.where(is_dt[0:1, :], _colsum(back * soft) * a_neg, 0.0)

    ins = [("row", proj, LANES, SMALL_CB), ("row", dgs, LANES, 0), ("row", ddt, LANES, 0), ("row", dacs, LANES, 0),
           ("full", gp), ("full", sp)]
    return _rowwise("gates_bwd", body, t, tm, ins, [(LANES, BF16)], accs=[(SUBLANES, LANES), (SUBLANES, LANES)])


def _dh1_first_bwd(dproj, wp_in, x, dx1, w, scatter_riders):
    d = D_MODEL

    def epilogue(dh, x_ref, dx1_ref, w_ref, dx_ref, dw_ref):
        xh, r = _rms(x_ref[...], d)
        dw_ref[...] += _colsum(dh * xh)
        dx_ref[...] = dx1_ref[...] + _rms_bwd(xh, r, dh * w_ref[...], d)

    return _matmul_rows("mm_dh1_first_bwd", dproj, wp_in, "nt", epilogue, [x, dx1], [w], [(d, F32)], accs=[(1, d)],
                        tk=PROJ_W // 2, scatter_riders=scatter_riders)


def _gather_two_level(name, arrays):
    n = len(arrays)
    n_sem = 7

    def body(*refs):
        ins, outs = refs[:n], refs[n:2 * n]
        send_sems, recv_sems, loc_sems = refs[2 * n:]
        x, y, c = lax.axis_index("x"), lax.axis_index("y"), lax.axis_index("c")
        slot = lambda px, py, pc: 4 * px + 2 * py + pc
        sibling = (x, y, 1 - c)
        chips = [(1 - x, y), (x, 1 - y), (1 - x, 1 - y)]

        def copy(t, k, src, block, to):
            return pltpu.make_async_remote_copy(
                src_ref=src, dst_ref=outs[t].at[block], send_sem=send_sems.at[t, k], recv_sem=recv_sems.at[t, k],
                device_id=to, device_id_type=pl.DeviceIdType.MESH)

        own, first, passed = [], [], []
        for t in range(n):
            own.append(pltpu.make_async_copy(ins[t], outs[t].at[slot(x, y, c)], loc_sems.at[t]))
            first.append(copy(t, 0, ins[t], slot(x, y, c), sibling))
            first += [copy(t, 1 + j, ins[t], slot(x, y, c), (px, py, c)) for j, (px, py) in enumerate(chips)]
        for cp in own + first:
            cp.start()
        for t in range(n):
            for j, (px, py) in enumerate(chips):
                copy(t, 1 + j, ins[t], slot(px, py, c), (px, py, c)).wait_recv()
                fwd = copy(t, 4 + j, outs[t].at[slot(px, py, c)], slot(px, py, c), sibling)
                fwd.start()
                passed.append(fwd)
        for t in range(n):
            copy(t, 0, ins[t], slot(x, y, 1 - c), sibling).wait_recv()
            for j, (px, py) in enumerate(chips):
                copy(t, 4 + j, ins[t], slot(px, py, 1 - c), sibling).wait_recv()
        for cp in first + passed:
            cp.wait_send()
        for cp in own:
            cp.wait()

    return _pcall(
        body, name=name,
        in_specs=[pl.BlockSpec(memory_space=pl.ANY)] * n,
        out_specs=[pl.BlockSpec(memory_space=pl.ANY)] * n,
        out_shape=_exchange_out_shapes(arrays, False),
        scratch_shapes=[pltpu.SemaphoreType.DMA((n, n_sem)), pltpu.SemaphoreType.DMA((n, n_sem)), pltpu.SemaphoreType.DMA((n,))],
    )(*arrays)


def _exchange_out_shapes(arrays, scatter):
    return [jax.ShapeDtypeStruct(a.shape if scatter else (N_DEV,) + a.shape, a.dtype) for a in arrays]


def _exchange_sems(n):
    return [pltpu.SemaphoreType.DMA((n, N_DEV - 1)), pltpu.SemaphoreType.DMA((n, N_DEV - 1)), pltpu.SemaphoreType.DMA((n,))]


def _exchange_phase(ins, outs, sems, scatter, start):
    send_sems, recv_sems, loc_sems = sems
    x, y, c = lax.axis_index("x"), lax.axis_index("y"), lax.axis_index("c")
    me = 4 * x + 2 * y + c
    for t in range(len(ins)):
        loc = pltpu.make_async_copy(ins[t].at[me] if scatter else ins[t], outs[t].at[me], loc_sems.at[t])
        if start:
            loc.start()
        else:
            loc.wait()
        for k in range(N_DEV - 1):
            bx, by, bc = ((k + 1) >> 2) & 1, ((k + 1) >> 1) & 1, (k + 1) & 1
            px = 1 - x if bx else x
            py = 1 - y if by else y
            pc = 1 - c if bc else c
            peer = 4 * px + 2 * py + pc
            src = ins[t].at[peer] if scatter else ins[t]
            copy = lambda dst: pltpu.make_async_remote_copy(
                src_ref=src, dst_ref=dst, send_sem=send_sems.at[t, k], recv_sem=recv_sems.at[t, k],
                device_id=(px, py, pc), device_id_type=pl.DeviceIdType.MESH)
            if start:
                copy(outs[t].at[me]).start()
            else:
                copy(outs[t].at[me]).wait_send()
                copy(outs[t].at[peer]).wait_recv()


def _adam_math(w, g, m, v):
    m = ADAM_B1 * m + (1.0 - ADAM_B1) * g
    v = ADAM_B2 * v + (1.0 - ADAM_B2) * (g * g)
    m_hat = m / (1.0 - ADAM_B1 ** ADAM_STEP)
    v_hat = v / (1.0 - ADAM_B2 ** ADAM_STEP)
    delta = -ADAM_LR * (m_hat / (jnp.sqrt(v_hat) + ADAM_EPS) + ADAM_WD * w)
    return delta, m, v


def _adam_big(name, parts, w, m, v, tm):
    r, c = w.shape
    tm = tm if r % tm == 0 else r

    def body(p_ref, w_ref, m_ref, v_ref, g_ref, d_ref, nm_ref, nv_ref):
        g = p_ref[0].astype(F32)
        for s in range(1, N_DEV):
            g = g + p_ref[s].astype(F32)
        g_ref[...] = g
        d_ref[...], nm_ref[...], nv_ref[...] = _adam_math(w_ref[...], g, m_ref[...], v_ref[...])

    blk = lambda: pl.BlockSpec((tm, c), lambda i: (i, 0))
    return _pcall(
        body, name=name, grid=(r // tm,),
        in_specs=[pl.BlockSpec((N_DEV, tm, c), lambda i: (0, i, 0)), blk(), blk(), blk()],
        out_specs=[blk(), blk(), blk(), blk()],
        out_shape=[jax.ShapeDtypeStruct((r, c), F32)] * 4,
        compiler_params=_params(("parallel",)),
    )(parts, w, m, v)


SMALL_ROWS = 56
ROW_DD, ROW_LOSS = 5, 6


def _small_sum(gathered):
    def body(g_ref, o_ref, x_ref):
        s = g_ref[0]
        for dev in range(1, N_DEV):
            s = s + g_ref[dev]
        o_ref[...] = s
        ri = lax.broadcasted_iota(jnp.int32, (D_MODEL, LANES), 0)
        ro = lax.broadcasted_iota(jnp.int32, (D_MODEL, LANES), 1)
        heads = _mmx(jnp.broadcast_to(s[ROW_DD:ROW_DD + 1, :], (SUBLANES, D_MODEL)), (ri // SSD_P == ro).astype(F32))
        loss = _rowsum(jnp.broadcast_to(s[ROW_LOSS:ROW_LOSS + 1, :], (SUBLANES, D_MODEL)))
        row = lax.broadcasted_iota(jnp.int32, (SUBLANES, LANES), 0)
        x_ref[...] = jnp.where(row == 0, heads, jnp.broadcast_to(loss, (SUBLANES, LANES)))

    return _pcall(
        body, name="small_sum",
        out_shape=[jax.ShapeDtypeStruct((SMALL_ROWS, D_MODEL), F32), jax.ShapeDtypeStruct((SUBLANES, LANES), F32)],
        compiler_params=_params(None),
    )(gathered)


def _adam_small(g, w, m, v):
    def body(g_ref, w_ref, m_ref, v_ref, d_ref, nm_ref, nv_ref):
        d_ref[...], nm_ref[...], nv_ref[...] = _adam_math(w_ref[...], g_ref[...], m_ref[...], v_ref[...])

    return _pcall(body, name="adam_small", out_shape=[jax.ShapeDtypeStruct(g.shape, F32)] * 3,
                  compiler_params=_params(None))(g, w, m, v)


def _pack(pieces, rows):
    flat = jnp.concatenate([p.reshape(-1).astype(F32) for p in pieces])
    return jnp.pad(flat, (0, rows * D_MODEL - flat.shape[0])).reshape(rows, D_MODEL)


def _unpack(packed, shapes):
    flat = packed.reshape(-1)
    out, off = [], 0
    for shp in shapes:
        size = 1
        for s in shp:
            size *= s
        out.append(flat[off:off + size].reshape(shp))
        off += size
    return out


def _permute_in(w):
    pad = jnp.zeros((w.shape[0], PROJ_W - D_IN), w.dtype)
    return jnp.concatenate([w[:, 0:4096], w[:, 4112:6672], w[:, 4096:4112], w[:, 6672:6688], pad], axis=1)


def _unpermute_in(g):
    return jnp.concatenate([g[:, 0:4096], g[:, 6656:6672], g[:, 4096:6656], g[:, 6672:6688]], axis=1)


def _lane_row(vec, start):
    return jnp.zeros((LANES,), F32).at[start:start + vec.shape[0]].set(vec)


def _cols_from_shards(g):
    return jnp.transpose(g, (1, 0, 2)).reshape(g.shape[1], N_DEV * g.shape[2])


def _cols_to_shards(a):
    return jnp.transpose(a.astype(BF16).reshape(a.shape[0], N_DEV, a.shape[1] // N_DEV), (1, 0, 2))


def _rows_to_shards(a):
    return a.astype(BF16).reshape(N_DEV, a.shape[0] // N_DEV, a.shape[1])


def _local_step(x, tgt, wp_in, rest, p, rest_is_sharded):
    bsz, seq, d = x.shape
    t = bsz * seq
    x2 = x.reshape(t, d)
    tgt2 = tgt.reshape(t, d)
    tm = min(256, seq)
    tm_big = min(512, seq)
    tm_wide = min(256, seq)
    sb = min(512, seq)

    gp = jnp.zeros((SUBLANES, LANES), F32).at[0].set(_lane_row(p["gdn_a_log"], 8)).at[1].set(_lane_row(p["gdn_dt_bias"], 8))
    sp = jnp.zeros((SUBLANES, LANES), F32).at[0].set(_lane_row(p["ssd_a_log"], 16)).at[1].set(_lane_row(p["ssd_dt_bias"], 16))
    dvec = jnp.repeat(p["ssd_d"], SSD_P).reshape(1, d)
    row = lambda v: v.reshape(1, -1)
    pre_mix, post_mix, pre_ffn, post_ffn = (row(p[k]) for k in ("pre_mix_norm", "post_mix_norm", "pre_ffn_norm", "post_ffn_norm"))
    gnw, snw = row(p["gdn_norm_w"]), row(p["ssd_norm_w"])
    gcw, scw, scb, fcw, fcb = p["gdn_conv_w"], p["ssd_conv_w"], row(p["ssd_conv_b"]), p["ffn_conv_w"], row(p["ffn_conv_b"])

    h1, proj = _norm_proj(x2, pre_mix, wp_in)
    b3 = lambda a: a.reshape(bsz, seq, a.shape[-1])
    b2 = lambda a: a.reshape(t, a.shape[-1])
    rows_of = lambda a, lo, n: jnp.transpose(a[:, lo:lo + n].reshape(bsz, seq // CHUNK, CHUNK, n), (0, 1, 3, 2))
    qn, kn, vv, gs, ypre_gdn = _gdn_prep(proj, gcw, gp, seq, tm_big)
    gr = rows_of(gs, 8, GDN_HEADS)
    qn, kn, vv, gs = b3(qn), b3(kn), b3(vv), b3(gs)
    (o_gdn, gdn_st, gdn_ti), gathered = _gdn_chunk_fwd(qn, kn, vv, gs, gr, bsz, seq, sb, list(rest) if rest_is_sharded else [])
    if rest_is_sharded:
        w_out, w_up, w_down = gathered[0].reshape(-1, d), _cols_from_shards(gathered[1]), gathered[2].reshape(-1, d)
    else:
        w_out, w_up, w_down = rest
    o_gdn = b2(o_gdn)
    xs, bc, dtx, acsx, acs, ypre_ssd = _ssd_prep(proj, scw, scb, sp, seq, tm)
    ar = rows_of(acs, 16, SSD_HEADS)
    y_ssd, ssd_st = _ssd_chunk_fwd(b3(xs), b3(bc), b3(dtx), b3(acsx), b3(acs), ar, bsz, seq, sb)
    y_ssd = b2(y_ssd)
    mixin = _gate_norm(o_gdn, y_ssd, xs, proj, gnw, snw, dvec, tm_big)
    mix, x1, h2 = _out_mid(mixin, w_out, x2, post_mix, pre_ffn)
    u_pre = _matmul("mm_up", h2, w_up, "nn", F32)
    act, u = _ffn_act(u_pre, fcw, fcb, seq, tm_wide)
    dy, df, loss_lanes, d_post_ffn = _down_final(act, w_down, x1, tgt2, post_ffn)

    g_down = _matmul("mm_dw_down", act, df, "tn", BF16, tm=1408, tk=2048)
    dact = _matmul("mm_dact", df, w_down, "nt", F32, tn=1408)
    du_pre, d_fcw, d_fcb = _ffn_bwd(u, u_pre, dact, fcw, seq, tm_wide)
    g_up = _matmul("mm_dw_up", h2, du_pre, "tn", BF16, tn=512, tk=4096)
    dx1, dmix, d_post_mix, d_pre_ffn = _dh2_mid_bwd(du_pre, w_up, x1, mix, dy, post_mix, pre_ffn)
    g_out = _matmul("mm_dw_out", mixin, dmix, "tn", BF16, tk=2048)
    do_gdn, dza, dy_ssd, dxs_d, dzs, d_gnw, d_snw, d_dd = _dmixin_gate_norm_bwd(dmix, w_out, o_gdn, y_ssd, xs, proj, gnw, snw, dvec)
    dyx, dybc, ddt, dacs, d_scb_x, d_scb_bc = _ssd_chunk_bwd(
        b3(xs), b3(bc), b3(dtx), b3(acsx), b3(acs), ar, b3(dy_ssd), ssd_st, b3(ypre_ssd), b3(dxs_d), bsz, seq, sb)
    dyx, dybc, ddt, dacs = b2(dyx), b2(dybc), b2(ddt), b2(dacs)
    d_scb = jnp.concatenate([d_scb_x, d_scb_bc], axis=1)
    riders = [_rows_to_shards(g_out), _cols_to_shards(g_up), _rows_to_shards(g_down)] if rest_is_sharded else []
    dgdn, received = _gdn_chunk_bwd(qn, kn, vv, gs, gr, b3(do_gdn), gdn_st, gdn_ti, b3(ypre_gdn), bsz, seq, min(256, seq), riders)
    if rest_is_sharded:
        g_out, g_up, g_down = received
    dyq, dyk, dyv, dgs = (b2(a) for a in dgdn)
    dsm, d_gp, d_sp = _gates_bwd(proj, dgs, ddt, dacs, gp, sp, tm)
    dproj, d_gcw, d_scw = _assemble_dproj((dyq, dyk, dyv), dza, dzs, (dyx, dybc), dsm, proj, gcw, scw, seq, tm)
    g_in = _matmul("mm_dw_in", h1, dproj, "tn", BF16, tk=4096)
    if rest_is_sharded:
        (dx, d_pre_mix), (g_in,) = _dh1_first_bwd(dproj, wp_in, x2, dx1, pre_mix, [_cols_to_shards(_unpermute_in(g_in))])
    else:
        dx, d_pre_mix = _dh1_first_bwd(dproj, wp_in, x2, dx1, pre_mix, [])

    small = dict(pre_mix_norm=d_pre_mix, ssd_norm_w=d_snw, post_mix_norm=d_post_mix, pre_ffn_norm=d_pre_ffn,
                 post_ffn_norm=d_post_ffn, dd_lanes=d_dd, loss_lanes=loss_lanes, gdn_gates=d_gp, ssd_gates=d_sp,
                 gdn_norm_w=d_gnw, gdn_conv_w=d_gcw[0:4], ssd_conv_w=d_scw[0:4], ssd_conv_b=d_scb,
                 ffn_conv_w=d_fcw[0:3], ffn_conv_b=d_fcb)
    return dx.reshape(bsz, seq, d), g_in, g_out, g_up, g_down, small


def kernel(x, pre_mix_norm, w_in, gdn_conv_w, gdn_a_log, gdn_dt_bias, gdn_norm_w, ssd_conv_w, ssd_conv_b, ssd_a_log, ssd_dt_bias, ssd_d, ssd_norm_w, w_out, post_mix_norm, pre_ffn_norm, w_up, ffn_conv_w, ffn_conv_b, w_down, post_ffn_norm, loss_target, m_pre_mix_norm, m_w_in, m_gdn_conv_w, m_gdn_a_log, m_gdn_dt_bias, m_gdn_norm_w, m_ssd_conv_w, m_ssd_conv_b, m_ssd_a_log, m_ssd_dt_bias, m_ssd_d, m_ssd_norm_w, m_w_out, m_post_mix_norm, m_pre_ffn_norm, m_w_up, m_ffn_conv_w, m_ffn_conv_b, m_w_down, m_post_ffn_norm, v_pre_mix_norm, v_w_in, v_gdn_conv_w, v_gdn_a_log, v_gdn_dt_bias, v_gdn_norm_w, v_ssd_conv_w, v_ssd_conv_b, v_ssd_a_log, v_ssd_dt_bias, v_ssd_d, v_ssd_norm_w, v_w_out, v_post_mix_norm, v_pre_ffn_norm, v_w_up, v_ffn_conv_w, v_ffn_conv_b, v_w_down, v_post_ffn_norm):
    names = ["pre_mix_norm", "w_in", "gdn_conv_w", "gdn_a_log", "gdn_dt_bias", "gdn_norm_w", "ssd_conv_w", "ssd_conv_b",
             "ssd_a_log", "ssd_dt_bias", "ssd_d", "ssd_norm_w", "w_out", "post_mix_norm", "pre_ffn_norm", "w_up",
             "ffn_conv_w", "ffn_conv_b", "w_down", "post_ffn_norm"]
    w_args = [pre_mix_norm, w_in, gdn_conv_w, gdn_a_log, gdn_dt_bias, gdn_norm_w, ssd_conv_w, ssd_conv_b, ssd_a_log, ssd_dt_bias, ssd_d, ssd_norm_w, w_out, post_mix_norm, pre_ffn_norm, w_up, ffn_conv_w, ffn_conv_b, w_down, post_ffn_norm]
    m_args = [m_pre_mix_norm, m_w_in, m_gdn_conv_w, m_gdn_a_log, m_gdn_dt_bias, m_gdn_norm_w, m_ssd_conv_w, m_ssd_conv_b, m_ssd_a_log, m_ssd_dt_bias, m_ssd_d, m_ssd_norm_w, m_w_out, m_post_mix_norm, m_pre_ffn_norm, m_w_up, m_ffn_conv_w, m_ffn_conv_b, m_w_down, m_post_ffn_norm]
    v_args = [v_pre_mix_norm, v_w_in, v_gdn_conv_w, v_gdn_a_log, v_gdn_dt_bias, v_gdn_norm_w, v_ssd_conv_w, v_ssd_conv_b, v_ssd_a_log, v_ssd_dt_bias, v_ssd_d, v_ssd_norm_w, v_w_out, v_post_mix_norm, v_pre_ffn_norm, v_w_up, v_ffn_conv_w, v_ffn_conv_b, v_w_down, v_post_ffn_norm]
    w = {k: a[0] for k, a in zip(names, w_args)}
    m = {k: a[0] for k, a in zip(names, m_args)}
    v = {k: a[0] for k, a in zip(names, v_args)}
    idx = 4 * lax.axis_index("x") + 2 * lax.axis_index("y") + lax.axis_index("c")
    big = ("w_in", "w_out", "w_up", "w_down")
    conv = ("gdn_conv_w", "ssd_conv_w", "ffn_conv_w")

    conv_local = jnp.concatenate([jnp.pad(w[k], ((0, 4 - w[k].shape[0]), (0, 0))) for k in conv], axis=1)
    g_in, g_conv = _gather_two_level("gather_weights", [w["w_in"].astype(BF16), conv_local])
    wp_in = _permute_in(_cols_from_shards(g_in))
    p = {k: w[k] for k in names if k not in big and k not in conv}
    off = 0
    for k in conv:
        cw = w[k].shape[1]
        p[k] = jnp.transpose(g_conv[:, :w[k].shape[0], off:off + cw], (1, 0, 2)).reshape(w[k].shape[0], N_DEV * cw)
        off += cw

    rest = tuple(w[k].astype(BF16) for k in ("w_out", "w_up", "w_down"))
    dx, p_in, p_out, p_up, p_down, small = _local_step(x, loss_target, wp_in, rest, p, True)

    gate_row = jnp.concatenate([small["gdn_gates"][0], small["gdn_gates"][1], small["ssd_gates"][0], small["ssd_gates"][1],
                                small["gdn_norm_w"][0], jnp.zeros((D_MODEL - 5 * LANES,), F32)]).reshape(1, D_MODEL)
    pack = _pack([small["pre_mix_norm"], small["ssd_norm_w"], small["post_mix_norm"], small["pre_ffn_norm"],
                  small["post_ffn_norm"], small["dd_lanes"], small["loss_lanes"], gate_row,
                  small["gdn_conv_w"], small["ssd_conv_w"], jnp.pad(small["ssd_conv_b"], ((0, 0), (0, 512))),
                  jnp.pad(small["ffn_conv_w"].reshape(-1), (0, 17 * D_MODEL - 3 * 2 * D_FF)),
                  jnp.pad(small["ffn_conv_b"], ((0, 0), (0, 512)))], SMALL_ROWS)
    (pack_all,) = _gather_two_level("gather_small", [pack])
    ssum, extra = _small_sum(pack_all)

    grads, deltas, new_m, new_v = {}, {}, {}, {}
    for k, parts in (("w_in", p_in), ("w_out", p_out), ("w_up", p_up), ("w_down", p_down)):
        grads[k], deltas[k], new_m[k], new_v[k] = _adam_big("adam_" + k, parts, w[k], m[k], v[k], 256)

    flat = ssum.reshape(-1)
    gate = ssum[7]
    sg = dict(pre_mix_norm=ssum[0], ssd_norm_w=ssum[1], post_mix_norm=ssum[2], pre_ffn_norm=ssum[3], post_ffn_norm=ssum[4],
              gdn_a_log=gate[8:16], gdn_dt_bias=gate[LANES + 8:LANES + 16], ssd_a_log=gate[2 * LANES + 16:2 * LANES + 32],
              ssd_dt_bias=gate[3 * LANES + 16:3 * LANES + 32], gdn_norm_w=gate[4 * LANES:5 * LANES], ssd_d=extra[0, 0:SSD_HEADS])
    o = 8 * D_MODEL
    full_gcw = flat[o:o + 4 * 3072].reshape(4, 3072)
    o += 12 * D_MODEL
    full_scw = flat[o:o + 4 * 1536].reshape(4, 1536)
    o += 6 * D_MODEL
    sg["ssd_conv_b"] = flat[o:o + 1536]
    o += 2 * D_MODEL
    full_fcw = flat[o:o + 3 * 2 * D_FF].reshape(3, 2 * D_FF)
    o += 17 * D_MODEL
    sg["ffn_conv_b"] = flat[o:o + 2 * D_FF]
    for k, full in (("gdn_conv_w", full_gcw), ("ssd_conv_w", full_scw), ("ffn_conv_w", full_fcw)):
        cw = w[k].shape[1]
        sg[k] = lax.dynamic_slice_in_dim(full, idx * cw, cw, axis=1)
    small_names = [k for k in names if k not in big]
    rows = 24
    gpk = _pack([sg[k] for k in small_names], rows)
    dpk, mpk, vpk = _adam_small(gpk, _pack([w[k] for k in small_names], rows), _pack([m[k] for k in small_names], rows),
                                _pack([v[k] for k in small_names], rows))
    shapes = [w[k].shape for k in small_names]
    for k, g_, d_, m_, v_ in zip(small_names, _unpack(gpk, shapes), _unpack(dpk, shapes), _unpack(mpk, shapes), _unpack(vpk, shapes)):
        grads[k], deltas[k], new_m[k], new_v[k] = g_, d_, m_, v_

    loss = extra[1, 0]
    lead = lambda a: a[None]
    return (loss, dx, *[lead(grads[k]) for k in names], *[lead(deltas[k]) for k in names],
            *[lead(new_m[k]) for k in names], *[lead(new_v[k]) for k in names])
```

```python
import functools

import jax
import jax.numpy as jnp
from jax import lax
from jax.experimental import pallas as pl
from jax.experimental.pallas import tpu as pltpu

F32 = jnp.float32
BF16 = jnp.bfloat16
MXU_DTYPE = jnp.bfloat16
HIGHEST = lax.Precision.HIGHEST
VMEM_LIMIT_V7X = 48 * 1024 * 1024
SUBLANES = 8
LANES = 128

D_MODEL = 1024
GDN_HEADS = 8
GDN_DK = 128
SSD_HEADS = 16
SSD_P = 64
SSD_GROUPS = 2
SSD_HPG = 8
SSD_N = 128
CHUNK = 128
D_FF = 2816
EPS = 1e-6
N_DEV = 8
PROJ_W = 7168
SMALL_CB = 52
D_IN = 6688

ADAM_LR = 0.001
ADAM_B1 = 0.9
ADAM_B2 = 0.999
ADAM_EPS = 1e-08
ADAM_WD = 0.01
ADAM_STEP = 10

NN = (((1,), (0,)), ((), ()))
NT = (((1,), (1,)), ((), ()))
TN = (((0,), (0,)), ((), ()))


def _pcall(body, **kw):
    return pl.pallas_call(body, **kw)


def _mm(a, b, dims=NN):
    return lax.dot_general(a.astype(MXU_DTYPE), b.astype(MXU_DTYPE), dims, preferred_element_type=F32)


def _mmx(a, b, dims=NN):
    return lax.dot_general(a, b, dims, precision=HIGHEST, preferred_element_type=F32)


def _split(a):
    hi = a.astype(MXU_DTYPE)
    return hi, (a - hi.astype(F32)).astype(MXU_DTYPE)


def _mm3(a, b, dims=NN):
    (ah, al), (bh, bl) = _split(a), _split(b)
    dot = lambda p, q: lax.dot_general(p, q, dims, preferred_element_type=F32)
    return dot(ah, bh) + (dot(ah, bl) + dot(al, bh))


def _mmsel(a, sel, dims=NN, terms=2):
    s = sel.astype(MXU_DTYPE)
    out = None
    for _ in range(terms):
        part = a.astype(MXU_DTYPE)
        a = a - part.astype(F32)
        prod = lax.dot_general(part, s, dims, preferred_element_type=F32)
        out = prod if out is None else out + prod
    return out


def _sigmoid(x):
    return 0.5 * jnp.tanh(0.5 * x) + 0.5


def _softplus(x):
    return jnp.maximum(x, 0.0) + jnp.log(1.0 + jnp.exp(-jnp.abs(x)))


def _dsilu(x, s):
    return s * (1.0 + x * (1.0 - s))


def _rowsum(x):
    return jnp.sum(x, axis=1, keepdims=True)


def _colsum(x):
    return jnp.sum(x, axis=0, keepdims=True)


def _pick(dim, pref):
    if dim <= pref:
        return dim
    best = None
    t = LANES
    while t <= pref:
        if dim % t == 0:
            best = t
        t += LANES
    return dim if best is None else best


def _params(sem):
    return pltpu.CompilerParams(dimension_semantics=sem, vmem_limit_bytes=VMEM_LIMIT_V7X)


def _matmul(name, a, b, mode, out_dtype, tm=1024, tn=1024, tk=1024):
    if mode == "nn":
        (m, k), (_, n) = a.shape, b.shape
    elif mode == "nt":
        (m, k), (n, _) = a.shape, b.shape
    else:
        (k, m), (_, n) = a.shape, b.shape
    tm, tn, tk = _pick(m, tm), _pick(n, tn), _pick(k, tk)
    nk = k // tk
    if mode == "tn":
        a_spec = pl.BlockSpec((tk, tm), lambda i, j, kk: (kk, i))
    else:
        a_spec = pl.BlockSpec((tm, tk), lambda i, j, kk: (i, kk))
    if mode == "nt":
        b_spec = pl.BlockSpec((tn, tk), lambda i, j, kk: (j, kk))
    else:
        b_spec = pl.BlockSpec((tk, tn), lambda i, j, kk: (kk, j))
    dims = {"nn": NN, "nt": NT, "tn": TN}[mode]

    def body(a_ref, b_ref, o_ref, *acc):
        if nk == 1:
            o_ref[...] = _mm(a_ref[...], b_ref[...], dims).astype(out_dtype)
            return
        kk = pl.program_id(2)

        @pl.when(kk == 0)
        def _():
            acc[0][...] = jnp.zeros_like(acc[0])

        acc[0][...] += _mm(a_ref[...], b_ref[...], dims)

        @pl.when(kk == nk - 1)
        def _():
            o_ref[...] = acc[0][...].astype(out_dtype)

    return _pcall(
        body, name=name, grid=(m // tm, n // tn, nk),
        in_specs=[a_spec, b_spec],
        out_specs=pl.BlockSpec((tm, tn), lambda i, j, kk: (i, j)),
        out_shape=jax.ShapeDtypeStruct((m, n), out_dtype),
        scratch_shapes=[pltpu.VMEM((tm, tn), F32)] if nk > 1 else [],
        compiler_params=_params(("parallel", "parallel", "arbitrary")),
    )(a, b)


def _matmul_rows(name, a, b, mode, epilogue, row_ins, full_ins, outs, accs=(), tm=512, tk=1024, scatter_riders=()):
    if mode == "nn":
        (m, k), (_, n) = a.shape, b.shape
    else:
        (m, k), (n, _) = a.shape, b.shape
    tm, tk = _pick(m, tm), _pick(k, tk)
    nk = k // tk
    a_spec = pl.BlockSpec((tm, tk), lambda i, kk: (i, kk))
    b_spec = pl.BlockSpec((n, tk), lambda i, kk: (0, kk)) if mode == "nt" else pl.BlockSpec((tk, n), lambda i, kk: (kk, 0))
    dims = NT if mode == "nt" else NN
    n_row, n_full, n_out, n_acc = len(row_ins), len(full_ins), len(outs), len(accs)

    def body(a_ref, b_ref, *rest):
        ins = rest[:n_row + n_full]
        out_refs = rest[n_row + n_full:n_row + n_full + n_out]
        acc_refs = rest[n_row + n_full + n_out:n_row + n_full + n_out + n_acc]
        prod_scr = rest[-1]
        i, kk = pl.program_id(0), pl.program_id(1)

        if n_acc:
            @pl.when((i == 0) & (kk == 0))
            def _():
                for r in acc_refs:
                    r[...] = jnp.zeros_like(r)

        if nk == 1:
            epilogue(_mm(a_ref[...], b_ref[...], dims), *ins, *out_refs, *acc_refs)
            return

        @pl.when(kk == 0)
        def _():
            prod_scr[...] = jnp.zeros_like(prod_scr)

        prod_scr[...] += _mm(a_ref[...], b_ref[...], dims)

        @pl.when(kk == nk - 1)
        def _():
            epilogue(prod_scr[...], *ins, *out_refs, *acc_refs)

    grid = (m // tm, nk)
    riders = list(scatter_riders)
    n_in = 2 + n_row + n_full
    any_spec, rider_shapes, rider_sems, wrap = _riding_exchange(riders, True, n_in, n_out + n_acc, grid)
    row_ins = [r if isinstance(r, tuple) else (r, r.shape[1], 0) for r in row_ins]
    in_specs = [a_spec, b_spec] + [pl.BlockSpec((tm, w), lambda i, kk, cb=cb: (i, cb)) for _, w, cb in row_ins]
    in_specs += [pl.BlockSpec(f.shape, lambda i, kk, nd=f.ndim: (0,) * nd) for f in full_ins]
    row_ins = [r for r, _, _ in row_ins]
    out_specs = [pl.BlockSpec((tm, w), lambda i, kk: (i, 0)) for w, _ in outs]
    out_specs += [pl.BlockSpec(s, lambda i, kk: (0, 0)) for s in accs]
    out_shape = [jax.ShapeDtypeStruct((m, w), dt) for w, dt in outs] + [jax.ShapeDtypeStruct(s, F32) for s in accs]
    res = _pcall(
        wrap(body), name=name, grid=grid,
        in_specs=in_specs + any_spec, out_specs=out_specs + any_spec, out_shape=out_shape + rider_shapes,
        scratch_shapes=[pltpu.VMEM((tm, n), F32)] + rider_sems,
        compiler_params=_params(("arbitrary", "arbitrary")),
    )(a, b, *row_ins, *full_ins, *riders)
    return (res[:n_out + n_acc], res[n_out + n_acc:]) if riders else res


def _rowwise(name, body, n_rows, tm, ins, outs, accs=()):
    arrays, in_specs = [], []
    last8 = n_rows // SUBLANES - 1
    per = tm // SUBLANES
    for spec in ins:
        kind, arr = spec[0], spec[1]
        if kind == "full":
            in_specs.append(pl.BlockSpec(arr.shape, lambda i, nd=arr.ndim: (0,) * nd))
        else:
            w, cb = spec[2], spec[3]
            if kind == "row":
                in_specs.append(pl.BlockSpec((tm, w), lambda i, cb=cb: (i, cb)))
            elif kind == "prev":
                in_specs.append(pl.BlockSpec((SUBLANES, w), lambda i, cb=cb: (jnp.maximum(i * per - 1, 0), cb)))
            else:
                in_specs.append(pl.BlockSpec((SUBLANES, w), lambda i, cb=cb: (jnp.minimum((i + 1) * per, last8), cb)))
        arrays.append(arr)
    out_shape = [jax.ShapeDtypeStruct((n_rows, w), dt) for (w, dt) in outs]
    out_shape += [jax.ShapeDtypeStruct(s, F32) for s in accs]
    out_specs = [pl.BlockSpec((tm, w), lambda i: (i, 0)) for (w, _) in outs]
    out_specs += [pl.BlockSpec(s, lambda i: (0, 0)) for s in accs]
    n_io = len(ins) + len(outs)

    def kern(*refs):
        i = pl.program_id(0)
        if accs:
            @pl.when(i == 0)
            def _():
                for r in refs[n_io:]:
                    r[...] = jnp.zeros_like(r)
        body(i, *refs)

    res = _pcall(
        kern, name=name, grid=(n_rows // tm,), in_specs=in_specs, out_specs=out_specs, out_shape=out_shape,
        compiler_params=_params(("arbitrary",)),
    )(*arrays)
    return res


def _shift_down(x, halo, j):
    r = pltpu.roll(x, j, 0)
    hr = pltpu.roll(halo, j, 0)
    rows = lax.broadcasted_iota(jnp.int32, (SUBLANES, x.shape[1]), 0)
    top = jnp.where(rows < j, hr, r[0:SUBLANES])
    return jnp.concatenate([top, r[SUBLANES:]], axis=0)


def _shift_up(x, halo, j):
    tm = x.shape[0]
    r = pltpu.roll(x, tm - j, 0)
    hr = pltpu.roll(halo, SUBLANES - j, 0)
    rows = lax.broadcasted_iota(jnp.int32, (SUBLANES, x.shape[1]), 0)
    bot = jnp.where(rows >= SUBLANES - j, hr, r[tm - SUBLANES:])
    return jnp.concatenate([r[:tm - SUBLANES], bot], axis=0)


def _conv_taps(x, halo, kw):
    return [x if kw - 1 - k == 0 else _shift_down(x, halo, kw - 1 - k) for k in range(kw)]


def _conv(taps, w):
    y = taps[0] * w[0:1]
    for k in range(1, len(taps)):
        y = y + taps[k] * w[k:k + 1]
    return y


def _rms(x, width):
    r = lax.rsqrt(jnp.sum(x * x, axis=-1, keepdims=True) * (1.0 / width) + EPS)
    return x * r, r


def _rms_bwd(xh, r, dxh, width):
    return r * (dxh - xh * (jnp.sum(dxh * xh, axis=-1, keepdims=True) * (1.0 / width)))


def _seq_flags(i, seq, tm):
    nps = seq // tm
    pos = i % nps
    return jnp.where(pos == 0, 0.0, 1.0), jnp.where(pos == nps - 1, 0.0, 1.0)


def _norm_proj(x, w, wp, tm=1024, tn=1024):
    t, d = x.shape
    n = wp.shape[1]
    tm, tn = _pick(t, tm), _pick(n, tn)

    def body(x_ref, w_ref, b_ref, h_ref, o_ref, h_scr):
        @pl.when(pl.program_id(1) == 0)
        def _():
            xh, _ = _rms(x_ref[...], d)
            h = (xh * w_ref[...]).astype(BF16)
            h_scr[...] = h
            h_ref[...] = h

        o_ref[...] = _mm(h_scr[...], b_ref[...])

    return _pcall(
        body, name="mm_norm_proj", grid=(t // tm, n // tn),
        in_specs=[pl.BlockSpec((tm, d), lambda i, j: (i, 0)), pl.BlockSpec((1, d), lambda i, j: (0, 0)),
                  pl.BlockSpec((d, tn), lambda i, j: (0, j))],
        out_specs=[pl.BlockSpec((tm, d), lambda i, j: (i, 0)), pl.BlockSpec((tm, tn), lambda i, j: (i, j))],
        out_shape=[jax.ShapeDtypeStruct((t, d), BF16), jax.ShapeDtypeStruct((t, n), F32)],
        scratch_shapes=[pltpu.VMEM((tm, d), BF16)],
        compiler_params=_params(("parallel", "arbitrary")),
    )(x, w, wp)


def _gdn_prep(proj, cw, gp, seq, tm):
    t = proj.shape[0]
    d = D_MODEL

    def body(i, q_ref, qh_ref, k_ref, kh_ref, v_ref, vh_ref, sm_ref, cw_ref, gp_ref, qn_ref, kn_ref, vv_ref, gs_ref, ypre_ref):
        keep, _ = _seq_flags(i, seq, tm)
        for x_ref, h_ref, o_ref, off, scale in ((q_ref, qh_ref, qn_ref, 0, GDN_DK ** -0.5),
                                               (k_ref, kh_ref, kn_ref, d, 1.0), (v_ref, vh_ref, vv_ref, 2 * d, None)):
            y = _conv(_conv_taps(x_ref[...], h_ref[...] * keep, 4), cw_ref[:, off:off + d])
            ypre_ref[:, off:off + d] = y
            a = y * _sigmoid(y)
            if scale is None:
                o_ref[...] = a
            else:
                for hh in range(GDN_HEADS):
                    s = a[:, hh * GDN_DK:(hh + 1) * GDN_DK]
                    n = lax.rsqrt(_rowsum(s * s) + EPS)
                    o_ref[:, hh * GDN_DK:(hh + 1) * GDN_DK] = s * (n * scale)
        sm = sm_ref[...]
        lane = lax.broadcasted_iota(jnp.int32, sm.shape, 1)
        beta = _sigmoid(sm)
        g = jnp.where((lane >= 8) & (lane < 16), -jnp.exp(gp_ref[0:1, :]) * _softplus(sm + gp_ref[1:2, :]), 0.0)
        gs_ref[...] = jnp.where(lane < 8, beta, _chunk_cumsum(g, False))

    ins = []
    for cb in range(3):
        ins += [("row", proj, d, cb), ("prev", proj, d, cb)]
    ins += [("row", proj, LANES, SMALL_CB), ("full", cw), ("full", gp)]
    return _rowwise("gdn_prep", body, t, tm, ins, [(d, F32), (d, F32), (d, F32), (LANES, F32), (3 * d, F32)])


def _block_tri(tm, upper):
    ri = lax.broadcasted_iota(jnp.int32, (tm, tm), 0)
    ci = lax.broadcasted_iota(jnp.int32, (tm, tm), 1)
    tri = (ri <= ci) if upper else (ri >= ci)
    return (tri & ((ri // CHUNK) == (ci // CHUNK))).astype(F32)


def _chunk_cumsum(x, upper):
    tri = _block_tri(x.shape[0], upper).astype(MXU_DTYPE)
    out = None
    for _ in range(3):
        part = x.astype(MXU_DTYPE)
        x = x - part.astype(F32)
        prod = lax.dot_general(tri, part, NN, preferred_element_type=F32)
        out = prod if out is None else out + prod
    return out


def _chunk_consts():
    row = lax.broadcasted_iota(jnp.int32, (CHUNK, CHUNK), 0)
    col = lax.broadcasted_iota(jnp.int32, (CHUNK, CHUNK), 1)
    return dict(
        tril=row >= col, strict=row > col, eye=(row == col).astype(F32),
        lane=lax.broadcasted_iota(jnp.int32, (CHUNK, LANES), 1),
        row1=lax.broadcasted_iota(jnp.int32, (CHUNK, 1), 0),
        ones=jnp.ones((CHUNK, LANES), F32))


def _hmap(fn, *lists):
    return [fn(*a) for a in zip(*lists)]


def _tri_inv(nmats, eye):
    levels = CHUNK.bit_length() - 2
    x = [eye - n for n in nmats]
    p = _hmap(_mm3, nmats, nmats)
    for lvl in range(levels):
        x = _hmap(lambda xi, pi: xi + _mm3(xi, pi), x, p)
        if lvl < levels - 1:
            p = _hmap(_mm3, p, p)
    return x


def _gdn_gates(gs, gc_row, h, c):
    beta = _rowsum(jnp.where(c["lane"] == h, gs, 0.0))
    gc = _rowsum(jnp.where(c["lane"] == h + 8, gs, 0.0))
    dc = jnp.exp(jnp.where(c["tril"], gc - gc_row, -1e30))
    gl = gc[CHUNK - 1:CHUNK, :]
    return beta, dc, jnp.exp(gc), jnp.exp(gl), jnp.exp(gl - gc)


GDN_HB = GDN_HEADS


def _gdn_specs(seq, sb, hb, backward):
    assert hb == GDN_HEADS
    nsb = seq // sb
    ncb = sb // CHUNK
    order = (lambda j: nsb - 1 - j) if backward else (lambda j: j)
    specs = dict(
        wide=lambda: pl.BlockSpec((1, sb, hb * GDN_DK), lambda b, h, j: (b, order(j), h)),
        gs=lambda: pl.BlockSpec((1, sb, LANES), lambda b, h, j: (b, order(j), 0)),
        gr=pl.BlockSpec((1, ncb, GDN_HEADS, CHUNK), lambda b, h, j: (b, order(j), 0, 0)),
        st=pl.BlockSpec((1, hb, ncb * GDN_DK, GDN_DK), lambda b, h, j: (b, h, order(j), 0)),
        ti=pl.BlockSpec((1, hb, sb, CHUNK), lambda b, h, j: (b, h, order(j), 0)))
    return nsb, ncb, specs


def _riding_exchange(arrays, scatter, n_in, n_out, grid):
    n = len(arrays)
    if n == 0:
        return [], [], [], lambda body: body
    any_spec = [pl.BlockSpec(memory_space=pl.ANY)] * n

    def wrap(body):
        def wrapped(*refs):
            ins = refs[n_in:n_in + n]
            outs = refs[n_in + n + n_out:n_in + 2 * n + n_out]
            sems = refs[len(refs) - 3:]
            pid = [pl.program_id(a) for a in range(len(grid))]
            first = functools.reduce(lambda a, b: a & b, [p == 0 for p in pid])
            last = functools.reduce(lambda a, b: a & b, [p == g - 1 for p, g in zip(pid, grid)])

            @pl.when(first)
            def _():
                _exchange_phase(ins, outs, sems, scatter, start=True)

            body(*refs[:n_in], *refs[n_in + n:n_in + n + n_out], *refs[n_in + 2 * n + n_out:len(refs) - 3])

            @pl.when(last)
            def _():
                _exchange_phase(ins, outs, sems, scatter, start=False)

        return wrapped

    return any_spec, _exchange_out_shapes(arrays, scatter), _exchange_sems(n), wrap


def _gdn_chunk_fwd(qn, kn, vv, gs, gr, bsz, seq, sb, riders):
    hb = GDN_HB
    nsb, ncb, sp = _gdn_specs(seq, sb, hb, False)
    grid = (bsz, GDN_HEADS // hb, nsb)
    any_spec, rider_shapes, rider_sems, wrap = _riding_exchange(riders, False, 5, 3, grid)

    def body(q_ref, k_ref, v_ref, gs_ref, gr_ref, o_ref, st_ref, ti_ref, s_scr):
        @pl.when(pl.program_id(2) == 0)
        def _():
            s_scr[...] = jnp.zeros_like(s_scr)

        c = _chunk_consts()

        def chunk(n, carry):
            r = pl.ds(pl.multiple_of(n * CHUNK, CHUNK), CHUNK)
            rs = pl.ds(pl.multiple_of(n * GDN_DK, GDN_DK), GDN_DK)
            gsv = gs_ref[0, r, :]
            heads = list(range(hb))
            sls = [slice(ih * GDN_DK, (ih + 1) * GDN_DK) for ih in heads]
            q = [q_ref[0, r, sl] for sl in sls]
            k = [k_ref[0, r, sl] for sl in sls]
            v = [v_ref[0, r, sl] for sl in sls]
            beta, dc, eg, egl, ekd = zip(*[
                _gdn_gates(gsv, gr_ref[0, n, pl.ds(ih, 1), :], ih, c) for ih in heads])
            kb = _hmap(lambda a, b: a * b, k, beta)
            amat = _hmap(lambda a, b, d_: jnp.where(c["strict"], _mm(a, b, NT) * d_, 0.0), kb, k, dc)
            tinv = _tri_inv(amat, c["eye"])
            u = _hmap(lambda t_, a, b: _mm3(t_, a * b), tinv, v, beta)
            w = _hmap(lambda t_, a, b: _mm3(t_, a * b), tinv, kb, eg)
            qk = _hmap(lambda a, b, d_: _mm(a, b, NT) * d_, q, k, dc)
            s = [s_scr[ih] for ih in heads]
            v_new = _hmap(lambda a, b, s_: a - _mm(b, s_), u, w, s)
            o = _hmap(lambda a, e, s_, qk_, vn: _mm(a * e, s_) + _mm(qk_, vn), q, eg, s, qk, v_new)
            s_new = _hmap(lambda s_, e, a, f, vn: s_ * e + _mm(a * f, vn, TN), s, egl, k, ekd, v_new)
            for ih in heads:
                o_ref[0, r, sls[ih]] = o[ih]
                st_ref[0, ih, rs, :] = s[ih]
                ti_ref[0, ih, r, :] = tinv[ih]
                s_scr[ih] = s_new[ih]
            return carry

        lax.fori_loop(0, ncb, chunk, 0)

    t3 = (bsz, seq, D_MODEL)
    res = _pcall(
        wrap(body), name="gdn_chunk_fwd", grid=grid,
        in_specs=[sp["wide"](), sp["wide"](), sp["wide"](), sp["gs"](), sp["gr"]] + any_spec,
        out_specs=[sp["wide"](), sp["st"], sp["ti"]] + any_spec,
        out_shape=[jax.ShapeDtypeStruct(t3, F32),
                   jax.ShapeDtypeStruct((bsz, GDN_HEADS, (seq // CHUNK) * GDN_DK, GDN_DK), F32),
                   jax.ShapeDtypeStruct((bsz, GDN_HEADS, seq, CHUNK), F32)] + rider_shapes,
        scratch_shapes=[pltpu.VMEM((hb, GDN_DK, GDN_DK), F32)] + rider_sems,
        compiler_params=_params(("arbitrary", "arbitrary", "arbitrary")),
    )(qn, kn, vv, gs, gr, *riders)
    return res[:3], res[3:]


def _ssd_prep(proj, cw, cb, sp, seq, tm):
    t = proj.shape[0]
    d = D_MODEL
    ssd_w = SSD_HEADS * SSD_P

    def body(i, x_ref, xh_ref, bc_ref, bch_ref, sm_ref, cw_ref, cb_ref, sp_ref, xs_ref, bco_ref, dtx_ref, acsx_ref, acs_ref, ypre_ref):
        keep, _ = _seq_flags(i, seq, tm)
        y = _conv(_conv_taps(x_ref[...], xh_ref[...] * keep, 4), cw_ref[:, 0:d]) + cb_ref[:, 0:d]
        ypre_ref[:, 0:d] = y
        xs_ref[...] = y * _sigmoid(y)
        y = _conv(_conv_taps(bc_ref[...], bch_ref[...] * keep, 4), cw_ref[:, d:d + 512]) + cb_ref[:, d:d + 512]
        ypre_ref[:, d:d + 512] = y
        bco_ref[...] = y * _sigmoid(y)
        sm = sm_ref[...]
        lane = lax.broadcasted_iota(jnp.int32, sm.shape, 1)
        valid = (lane >= 16) & (lane < 32)
        dt = jnp.where(valid, _softplus(sm + sp_ref[1:2, :]), 0.0)
        adt = dt * (-jnp.exp(sp_ref[0:1, :]))
        acs = _chunk_cumsum(adt, False)
        l64 = lax.broadcasted_iota(jnp.int32, (LANES, ssd_w), 0)
        d64 = lax.broadcasted_iota(jnp.int32, (LANES, ssd_w), 1)
        e64 = (l64 - 16 == d64 // SSD_P).astype(F32)
        dtx_ref[...] = _mmsel(dt, e64, terms=3)
        acsx_ref[...] = _mmsel(acs, e64, terms=3)
        acs_ref[...] = acs

    ins = [("row", proj, d, 5), ("prev", proj, d, 5), ("row", proj, 512, 12), ("prev", proj, 512, 12),
           ("row", proj, LANES, SMALL_CB), ("full", cw), ("full", cb), ("full", sp)]
    return _rowwise("ssd_prep", body, t, tm, ins,
                    [(d, F32), (512, F32), (ssd_w, F32), (ssd_w, F32), (LANES, F32), (d + 512, F32)])


SSD_GW = SSD_HPG * SSD_P


def _ssd_head(acs, ar_ref, n, head, cbm, c):
    col = _rowsum(jnp.where(c["lane"] == head + 16, acs, 0.0))
    lm = jnp.exp(jnp.where(c["tril"], col - ar_ref[0, n, pl.ds(head, 1), :], -1e30))
    return lm, cbm * lm


def _ssd_specs(seq, sb):
    nsb = seq // sb
    ncb = sb // CHUNK
    def specs(order):
        return dict(
            wide=lambda: pl.BlockSpec((1, sb, SSD_HEADS * SSD_P), lambda b, j: (b, order(j), 0)),
            bc=lambda: pl.BlockSpec((1, sb, 2 * SSD_GROUPS * SSD_N), lambda b, j: (b, order(j), 0)),
            small=lambda: pl.BlockSpec((1, sb, LANES), lambda b, j: (b, order(j), 0)),
            ar=pl.BlockSpec((1, ncb, SSD_HEADS, CHUNK), lambda b, j: (b, order(j), 0, 0)),
            st=pl.BlockSpec((1, ncb * SSD_N, SSD_HEADS * SSD_P), lambda b, j: (b, order(j), 0)))
    return nsb, ncb, specs(lambda j: j), specs(lambda j: nsb - 1 - j)


def _ssd_chunk_fwd(xs, bc, dtx, acsx, acs, ar, bsz, seq, sb):
    nsb, ncb, sp, _ = _ssd_specs(seq, sb)

    def body(x_ref, dtx_ref, ax_ref, bc_ref, acs_ref, ar_ref, y_ref, sts_ref, st_scr):
        @pl.when(pl.program_id(1) == 0)
        def _():
            st_scr[...] = jnp.zeros_like(st_scr)

        c = _chunk_consts()
        lane5 = lax.broadcasted_iota(jnp.int32, (CHUNK, SSD_GW), 1) // SSD_P

        def chunk(n, carry):
            r = pl.ds(pl.multiple_of(n * CHUNK, CHUNK), CHUNK)
            rs = pl.ds(pl.multiple_of(n * SSD_N, SSD_N), SSD_N)
            acsv = acs_ref[0, r, :]
            for g in range(SSD_GROUPS):
                gl = slice(g * SSD_GW, (g + 1) * SSD_GW)
                x, dt, ax = x_ref[0, r, gl], dtx_ref[0, r, gl], ax_ref[0, r, gl]
                bm = bc_ref[0, r, g * SSD_N:(g + 1) * SSD_N]
                cm = bc_ref[0, r, (SSD_GROUPS + g) * SSD_N:(SSD_GROUPS + g + 1) * SSD_N]
                xdt = x * dt
                cbm = _mm(cm, bm, NT)
                al = ax[CHUNK - 1:CHUNK, :]
                st = st_scr[:, gl]
                y = _mm(cm, st) * jnp.exp(ax)
                for hh in range(SSD_HPG):
                    _, gm = _ssd_head(acsv, ar_ref, n, g * SSD_HPG + hh, cbm, c)
                    y = y + _mm(gm, jnp.where(lane5 == hh, xdt, 0.0))
                y_ref[0, r, gl] = y
                sts_ref[0, rs, gl] = st
                st_scr[:, gl] = st * jnp.exp(al) + _mm(bm, xdt * jnp.exp(al - ax), TN)
            return carry

        lax.fori_loop(0, ncb, chunk, 0)

    return _pcall(
        body, name="ssd_chunk_fwd", grid=(bsz, nsb),
        in_specs=[sp["wide"](), sp["wide"](), sp["wide"](), sp["bc"](), sp["small"](), sp["ar"]],
        out_specs=[sp["wide"](), sp["st"]],
        out_shape=[jax.ShapeDtypeStruct((bsz, seq, SSD_HEADS * SSD_P), F32),
                   jax.ShapeDtypeStruct((bsz, (seq // CHUNK) * SSD_N, SSD_HEADS * SSD_P), F32)],
        scratch_shapes=[pltpu.VMEM((SSD_N, SSD_HEADS * SSD_P), F32)],
        compiler_params=_params(("parallel", "arbitrary")),
    )(xs, dtx, acsx, bc, acs, ar)


def _gate_norm(o_gdn, y_ssd, xs, proj, gnw, snw, dvec, tm):
    t = o_gdn.shape[0]
    d = D_MODEL

    def body(i, o_ref, za_ref, y_ref, xs_ref, zs_ref, gnw_ref, snw_ref, dv_ref, out_ref):
        for hh in range(GDN_HEADS):
            sl = slice(hh * GDN_DK, (hh + 1) * GDN_DK)
            oh, _ = _rms(o_ref[:, sl], GDN_DK)
            z = za_ref[:, sl]
            out_ref[:, sl] = (oh * gnw_ref[...] * (z * _sigmoid(z))).astype(BF16)
        zs = zs_ref[...]
        yg = (y_ref[...] + dv_ref[...] * xs_ref[...]) * (zs * _sigmoid(zs))
        for g in range(SSD_GROUPS):
            sl = slice(g * 512, (g + 1) * 512)
            yh, _ = _rms(yg[:, sl], 512)
            out_ref[:, d + g * 512:d + (g + 1) * 512] = (yh * snw_ref[:, sl]).astype(BF16)

    ins = [("row", o_gdn, d, 0), ("row", proj, d, 3), ("row", y_ssd, d, 0), ("row", xs, d, 0), ("row", proj, d, 4),
           ("full", gnw), ("full", snw), ("full", dvec)]
    return _rowwise("gate_norm", body, t, tm, ins, [(2 * d, BF16)])[0]


def _out_mid(mixin, w_out, x, pmw, pfw):
    d = D_MODEL

    def epilogue(mix, x_ref, pmw_ref, pfw_ref, mix_ref, x1_ref, h2_ref):
        mix_ref[...] = mix
        mh, _ = _rms(mix, d)
        x1 = x_ref[...] + mh * pmw_ref[...]
        x1_ref[...] = x1
        xh, _ = _rms(x1, d)
        h2_ref[...] = (xh * pfw_ref[...]).astype(BF16)

    return _matmul_rows("mm_out_mid", mixin, w_out, "nn", epilogue, [x], [pmw, pfw], [(d, F32), (d, F32), (d, BF16)],
                        tk=2 * d)


def _ffn_act(u_pre, cw, cb, seq, tm):
    t = u_pre.shape[0]

    def body(i, ug_ref, ugh_ref, uu_ref, uuh_ref, cw_ref, cb_ref, act_ref, u_ref):
        keep, _ = _seq_flags(i, seq, tm)
        gate = _conv(_conv_taps(ug_ref[...], ugh_ref[...] * keep, 3), cw_ref[:, 0:D_FF]) + cb_ref[:, 0:D_FF]
        up = _conv(_conv_taps(uu_ref[...], uuh_ref[...] * keep, 3), cw_ref[:, D_FF:2 * D_FF]) + cb_ref[:, D_FF:2 * D_FF]
        u_ref[:, 0:D_FF] = gate
        u_ref[:, D_FF:2 * D_FF] = up
        act_ref[...] = (gate * _sigmoid(gate) * up).astype(BF16)

    ins = [("row", u_pre, D_FF, 0), ("prev", u_pre, D_FF, 0), ("row", u_pre, D_FF, 1), ("prev", u_pre, D_FF, 1),
           ("full", cw), ("full", cb)]
    return _rowwise("ffn_act", body, t, tm, ins, [(D_FF, BF16), (2 * D_FF, F32)])


def _down_final(act, w_down, x1, tgt, w):
    d = D_MODEL

    def epilogue(f, x1_ref, t_ref, w_ref, dy_ref, df_ref, loss_ref, dw_ref):
        fh, r = _rms(f, d)
        e = x1_ref[...] + fh * w_ref[...] - t_ref[...]
        loss_ref[...] += _colsum(e * e) * (0.5 / d)
        dy = e * (1.0 / d)
        dy_ref[...] = dy
        dw_ref[...] += _colsum(dy * fh)
        df_ref[...] = _rms_bwd(fh, r, dy * w_ref[...], d).astype(BF16)

    return _matmul_rows("mm_down_final", act, w_down, "nn", epilogue, [x1, tgt], [w], [(d, F32), (d, BF16)],
                        accs=[(1, d), (1, d)], tk=D_FF)


def _ffn_bwd(u, u_pre, dact, cw, seq, tm):
    t = u.shape[0]

    def body(i, g_ref, gn_ref, up_ref, upn_ref, xg_ref, xu_ref, da_ref, dan_ref, cw_ref, dpre_ref, dcw_ref, dcb_ref):
        _, keep_next = _seq_flags(i, seq, tm)
        ext = lambda a_ref, n_ref: jnp.concatenate([a_ref[...], n_ref[...]], axis=0)
        rows = tm + SUBLANES
        gate, up = ext(g_ref, gn_ref), ext(up_ref, upn_ref)
        sg = _sigmoid(gate)
        da = jnp.concatenate([da_ref[...], dan_ref[...] * keep_next], axis=0)
        for off, grad, x_ref in ((0, da * up * _dsilu(gate, sg), xg_ref), (D_FF, da * gate * sg, xu_ref)):
            x = x_ref[...]
            own = grad[0:tm]
            acc = own * cw_ref[2:3, off:off + D_FF]
            dcb_ref[:, off:off + D_FF] += _colsum(own)
            dcw_ref[2:3, off:off + D_FF] += _colsum(own * x)
            for j in (1, 2):
                ahead = pltpu.roll(grad, rows - j, 0)[0:tm]
                acc = acc + ahead * cw_ref[2 - j:3 - j, off:off + D_FF]
                dcw_ref[2 - j:3 - j, off:off + D_FF] += _colsum(ahead * x)
            dpre_ref[:, off:off + D_FF] = acc.astype(BF16)

    ins = []
    for cb_ in range(2):
        ins += [("row", u, D_FF, cb_), ("next", u, D_FF, cb_)]
    ins += [("row", u_pre, D_FF, 0), ("row", u_pre, D_FF, 1), ("row", dact, D_FF, 0), ("next", dact, D_FF, 0), ("full", cw)]
    return _rowwise("ffn_bwd", body, t, tm, ins, [(2 * D_FF, BF16)], accs=[(SUBLANES, 2 * D_FF), (1, 2 * D_FF)])


def _assemble_dproj(dpre_qkv, dza, dzs, dpre_xbc, dsm, proj, gcw, scw, seq, tm):
    t = dza.shape[0]
    d = D_MODEL

    def body(i, dq_ref, dqn_ref, dk_ref, dkn_ref, dv_ref, dvn_ref, dx_ref, dxn_ref, dbc_ref, dbcn_ref, dza_ref, dzs_ref,
             dsm_ref, xq_ref, xk_ref, xv_ref, xx_ref, xbc_ref, gcw_ref, scw_ref, o_ref, dgcw_ref, dscw_ref):
        _, keep = _seq_flags(i, seq, tm)
        pieces = [(g_ref, n_ref, gcw_ref, dgcw_ref, x_ref, 0, c0) for g_ref, n_ref, x_ref, c0 in (
            (dq_ref, dqn_ref, xq_ref, 0), (dk_ref, dkn_ref, xk_ref, d), (dv_ref, dvn_ref, xv_ref, 2 * d))]
        pieces += [(g_ref, n_ref, scw_ref, dscw_ref, x_ref, 5 * d, c0) for g_ref, n_ref, x_ref, c0 in (
            (dx_ref, dxn_ref, xx_ref, 0), (dbc_ref, dbcn_ref, xbc_ref, d))]
        for d_ref, n_ref, cw_ref, dcw_ref, x_ref, base, c0 in pieces:
            w = x_ref.shape[1]
            x = x_ref[...]
            g = d_ref[...]
            halo = n_ref[...] * keep
            acc = g * cw_ref[3:4, c0:c0 + w]
            dcw_ref[3:4, c0:c0 + w] += _colsum(g * x)
            for j in range(1, 4):
                ahead = _shift_up(g, halo, j)
                acc = acc + ahead * cw_ref[3 - j:4 - j, c0:c0 + w]
                dcw_ref[3 - j:4 - j, c0:c0 + w] += _colsum(ahead * x)
            o_ref[:, base + c0:base + c0 + w] = acc.astype(BF16)
        o_ref[:, 3 * d:4 * d] = dza_ref[...]
        o_ref[:, 4 * d:5 * d] = dzs_ref[...]
        o_ref[:, 6 * d + 512:6 * d + 512 + LANES] = dsm_ref[...]
        o_ref[:, 6 * d + 512 + LANES:PROJ_W] = jnp.zeros((tm, PROJ_W - (6 * d + 512 + LANES)), BF16)

    ins = []
    for g in tuple(dpre_qkv) + tuple(dpre_xbc):
        ins += [("row", g, g.shape[1], 0), ("next", g, g.shape[1], 0)]
    ins += [("row", dza, d, 0), ("row", dzs, d, 0), ("row", dsm, LANES, 0),
           ("row", proj, d, 0), ("row", proj, d, 1), ("row", proj, d, 2), ("row", proj, d, 5), ("row", proj, 512, 12),
           ("full", gcw), ("full", scw)]
    return _rowwise("assemble_dproj", body, t, tm, ins, [(PROJ_W, BF16)], accs=[(SUBLANES, 3 * d), (SUBLANES, d + 512)])


def _dh2_mid_bwd(du_pre, w_up, x1, mix, dy, pmw, pfw):
    d = D_MODEL

    def epilogue(dh2, x1_ref, mix_ref, dy_ref, pmw_ref, pfw_ref, dx1_ref, dmix_ref, dpm_ref, dpf_ref):
        xh, r2 = _rms(x1_ref[...], d)
        dpf_ref[...] += _colsum(dh2 * xh)
        dx1 = dy_ref[...] + _rms_bwd(xh, r2, dh2 * pfw_ref[...], d)
        dx1_ref[...] = dx1
        mh, r = _rms(mix_ref[...], d)
        dpm_ref[...] += _colsum(dx1 * mh)
        dmix_ref[...] = _rms_bwd(mh, r, dx1 * pmw_ref[...], d).astype(BF16)

    return _matmul_rows("mm_dh2_mid_bwd", du_pre, w_up, "nt", epilogue, [x1, mix, dy], [pmw, pfw],
                        [(d, F32), (d, BF16)], accs=[(1, d), (1, d)], tk=D_FF)


def _dmixin_gate_norm_bwd(dmix, w_out, o_gdn, y_ssd, xs, proj, gnw, snw, dvec):
    d = D_MODEL

    def epilogue(dmixin, o_ref, za_ref, y_ref, xs_ref, zs_ref, gnw_ref, snw_ref, dv_ref,
                 do_ref, dza_ref, dy_ref, dxs_ref, dzs_ref, dgnw_ref, dsnw_ref, dd_ref):
        for hh in range(GDN_HEADS):
            sl = slice(hh * GDN_DK, (hh + 1) * GDN_DK)
            oh, r = _rms(o_ref[:, sl], GDN_DK)
            z = za_ref[:, sl]
            sz = _sigmoid(z)
            dm = dmixin[:, sl]
            don = dm * (z * sz)
            dza_ref[:, sl] = (dm * oh * gnw_ref[...] * _dsilu(z, sz)).astype(BF16)
            dgnw_ref[...] += _colsum(don * oh)
            do_ref[:, sl] = _rms_bwd(oh, r, don * gnw_ref[...], GDN_DK)
        zs = zs_ref[...]
        sz = _sigmoid(zs)
        sil = zs * sz
        x = xs_ref[...]
        y0 = y_ref[...] + dv_ref[...] * x
        yg = y0 * sil
        dms = dmixin[:, d:2 * d]
        for g in range(SSD_GROUPS):
            sl = slice(g * 512, (g + 1) * 512)
            yh, r = _rms(yg[:, sl], 512)
            dsnw_ref[:, sl] += _colsum(dms[:, sl] * yh)
            dyg = _rms_bwd(yh, r, dms[:, sl] * snw_ref[:, sl], 512)
            dy0 = dyg * sil[:, sl]
            dzs_ref[:, sl] = (dyg * y0[:, sl] * _dsilu(zs[:, sl], sz[:, sl])).astype(BF16)
            dy_ref[:, sl] = dy0
            dxs_ref[:, sl] = dy0 * dv_ref[:, sl]
            dd_ref[:, sl] += _colsum(dy0 * x[:, sl])

    row_ins = [o_gdn, (proj, d, 3), y_ssd, xs, (proj, d, 4)]
    return _matmul_rows("mm_dmixin_gate_norm_bwd", dmix, w_out, "nt", epilogue, row_ins, [gnw, snw, dvec],
                        [(d, F32), (d, BF16), (d, F32), (d, F32), (d, BF16)], accs=[(1, GDN_DK), (1, d), (1, d)], tm=256)


def _ssd_chunk_bwd(xs, bc, dtx, acsx, acs, ar, dy, sts, ypre, dxs_d, bsz, seq, sb):
    nsb, ncb, _, sp = _ssd_specs(seq, sb)
    bc_w = 2 * SSD_GROUPS * SSD_N
    x_w = SSD_HEADS * SSD_P

    def body(x_ref, dtx_ref, ax_ref, bc_ref, acs_ref, ar_ref, dy_ref, sts_ref, yx_ref, ybc_ref, dxd_ref,
             dx_ref, dbc_ref, ddt_ref, dacs_ref, dbx_ref, dbbc_ref, dst_scr):
        @pl.when(pl.program_id(1) == 0)
        def _():
            dst_scr[...] = jnp.zeros_like(dst_scr)

        @pl.when((pl.program_id(0) == 0) & (pl.program_id(1) == 0))
        def _():
            dbx_ref[...] = jnp.zeros_like(dbx_ref)
            dbbc_ref[...] = jnp.zeros_like(dbbc_ref)

        def to_conv_out(grad, y):
            return grad * _dsilu(y, _sigmoid(y))

        c = _chunk_consts()
        lane5 = lax.broadcasted_iota(jnp.int32, (CHUNK, SSD_GW), 1) // SSD_P
        row5 = lax.broadcasted_iota(jnp.int32, (CHUNK, SSD_GW), 0)
        sel_in = lax.broadcasted_iota(jnp.int32, (SSD_GW, LANES), 0) // SSD_P
        sel_out = lax.broadcasted_iota(jnp.int32, (SSD_GW, LANES), 1)

        def chunk(nn, carry):
            n = ncb - 1 - nn
            r = pl.ds(pl.multiple_of(n * CHUNK, CHUNK), CHUNK)
            rs = pl.ds(pl.multiple_of(n * SSD_N, SSD_N), SSD_N)
            acsv = acs_ref[0, r, :]
            ddt = jnp.zeros((CHUNK, LANES), F32)
            dacs = jnp.zeros((CHUNK, LANES), F32)
            for g in range(SSD_GROUPS):
                gl = slice(g * SSD_GW, (g + 1) * SSD_GW)
                x, dt, ax, dyv = x_ref[0, r, gl], dtx_ref[0, r, gl], ax_ref[0, r, gl], dy_ref[0, r, gl]
                bm = bc_ref[0, r, g * SSD_N:(g + 1) * SSD_N]
                cm = bc_ref[0, r, (SSD_GROUPS + g) * SSD_N:(SSD_GROUPS + g + 1) * SSD_N]
                st = sts_ref[0, rs, gl]
                dst = dst_scr[:, gl]
                rsel = (sel_in + (16 + g * SSD_HPG) == sel_out).astype(F32)
                xdt = x * dt
                cbm = _mm(cm, bm, NT)
                al = ax[CHUNK - 1:CHUNK, :]
                ex, el = jnp.exp(ax), jnp.exp(al)
                dec = jnp.exp(al - ax)
                xd = xdt * dec
                dye = dyv * ex
                dxd = _mm(bm, dst)
                dxdt = dec * dxd
                dcm = _mm(dye, st, NT)
                dbm = _mm(xd, dst, NT)
                z = dye * _mm(cm, st) - dxd * xd
                zl = _colsum(dst * st) * el + _colsum(dxd * xd)
                z = z + jnp.where(row5 == CHUNK - 1, zl, 0.0)
                dcb = jnp.zeros((CHUNK, CHUNK), F32)
                for hh in range(SSD_HPG):
                    head = g * SSD_HPG + hh
                    lm, gm = _ssd_head(acsv, ar_ref, n, head, cbm, c)
                    dym = jnp.where(lane5 == hh, dyv, 0.0)
                    dxdt = dxdt + _mm(gm, dym, TN)
                    dg = _mm(dym, xdt, NT)
                    dcb = dcb + dg * lm
                    pm = dg * gm
                    dacs = dacs + jnp.where(c["lane"] == head + 16, _rowsum(pm) - _mmsel(pm, c["ones"], TN), 0.0)
                for sl, grad in ((slice((SSD_GROUPS + g) * SSD_N, (SSD_GROUPS + g + 1) * SSD_N), dcm + _mm(dcb, bm)),
                                 (slice(g * SSD_N, (g + 1) * SSD_N), dbm + _mm(dcb, cm, TN))):
                    dpre = to_conv_out(grad, ybc_ref[0, r, sl])
                    dbc_ref[0, r, sl] = dpre
                    dbbc_ref[:, sl] += _colsum(dpre)
                dacs = dacs + _mmsel(z, rsel)
                ddt = ddt + _mmsel(dxdt * x, rsel)
                dpre = to_conv_out(dxdt * dt + dxd_ref[0, r, gl], yx_ref[0, r, gl])
                dx_ref[0, r, gl] = dpre
                dbx_ref[:, gl] += _colsum(dpre)
                dst_scr[:, gl] = dst * el + _mm(cm, dye, TN)
            ddt_ref[0, r, :] = ddt
            dacs_ref[0, r, :] = dacs
            return carry

        lax.fori_loop(0, ncb, chunk, 0)

    nsb_rev = lambda j: nsb - 1 - j
    return _pcall(
        body, name="ssd_chunk_bwd", grid=(bsz, nsb),
        in_specs=[sp["wide"](), sp["wide"](), sp["wide"](), sp["bc"](), sp["small"](), sp["ar"], sp["wide"](), sp["st"],
                  sp["wide"](), pl.BlockSpec((1, sb, bc_w), lambda b, j: (b, nsb_rev(j), x_w // bc_w)), sp["wide"]()],
        out_specs=[sp["wide"](), sp["bc"](), sp["small"](), sp["small"](),
                   pl.BlockSpec((1, x_w), lambda b, j: (0, 0)), pl.BlockSpec((1, bc_w), lambda b, j: (0, 0))],
        out_shape=[jax.ShapeDtypeStruct((bsz, seq, x_w), F32), jax.ShapeDtypeStruct((bsz, seq, bc_w), F32),
                   jax.ShapeDtypeStruct((bsz, seq, LANES), F32), jax.ShapeDtypeStruct((bsz, seq, LANES), F32),
                   jax.ShapeDtypeStruct((1, x_w), F32), jax.ShapeDtypeStruct((1, bc_w), F32)],
        scratch_shapes=[pltpu.VMEM((SSD_N, x_w), F32)],
        compiler_params=_params(("arbitrary", "arbitrary")),
    )(xs, dtx, acsx, bc, acs, ar, dy, sts, ypre, ypre, dxs_d)


def _through_norm_silu(g, y, scale):
    sy = _sigmoid(y)
    ds_ = _dsilu(y, sy)
    if scale is None:
        return g * ds_
    a = y * sy
    n = lax.rsqrt(_rowsum(a * a) + EPS)
    ah = a * n
    return (scale * n) * (g - ah * _rowsum(g * ah)) * ds_


def _gdn_chunk_bwd(qn, kn, vv, gs, gr, do, sts, tis, ypre, bsz, seq, sb, riders):
    hb = GDN_HB
    nsb, ncb, sp = _gdn_specs(seq, sb, hb, True)
    grid = (bsz, GDN_HEADS // hb, nsb)
    any_spec, rider_shapes, rider_sems, wrap = _riding_exchange(riders, True, 11, 4, grid)
    ypre_spec = lambda cb: pl.BlockSpec((1, sb, hb * GDN_DK), lambda b, h, j: (b, nsb - 1 - j, cb))

    def body(q_ref, k_ref, v_ref, gs_ref, gr_ref, do_ref, st_ref, ti_ref, yq_ref, yk_ref, yv_ref,
             dq_ref, dk_ref, dv_ref, dgs_ref, ds_scr):
        @pl.when(pl.program_id(2) == 0)
        def _():
            ds_scr[...] = jnp.zeros_like(ds_scr)

        c = _chunk_consts()

        def chunk(nn, carry):
            n = ncb - 1 - nn
            r = pl.ds(pl.multiple_of(n * CHUNK, CHUNK), CHUNK)
            rs = pl.ds(pl.multiple_of(n * GDN_DK, GDN_DK), GDN_DK)
            gsv = gs_ref[0, r, :]
            heads = list(range(hb))
            sls = [slice(ih * GDN_DK, (ih + 1) * GDN_DK) for ih in heads]
            q = [q_ref[0, r, sl] for sl in sls]
            k = [k_ref[0, r, sl] for sl in sls]
            v = [v_ref[0, r, sl] for sl in sls]
            do_ = [do_ref[0, r, sl] for sl in sls]
            s = [st_ref[0, ih, rs, :] for ih in heads]
            tinv = [ti_ref[0, ih, r, :] for ih in heads]
            dsn = [ds_scr[ih] for ih in heads]
            beta, dc, eg, egl, ekd = zip(*[
                _gdn_gates(gsv, gr_ref[0, n, pl.ds(ih, 1), :], ih, c) for ih in heads])
            mul = lambda a, b: a * b
            kb = _hmap(mul, k, beta)
            rhs_w = _hmap(mul, kb, eg)
            u = _hmap(lambda t_, a, b: _mm3(t_, a * b), tinv, v, beta)
            w = _hmap(_mm3, tinv, rhs_w)
            amat = _hmap(lambda a, b, d_: jnp.where(c["strict"], _mm(a, b, NT) * d_, 0.0), kb, k, dc)
            qk = _hmap(lambda a, b, d_: _mm(a, b, NT) * d_, q, k, dc)
            qd = _hmap(mul, q, eg)
            kd = _hmap(mul, k, ekd)
            v_new = _hmap(lambda a, b, s_: a - _mm(b, s_), u, w, s)
            dv_new = _hmap(lambda qk_, d_, kd_, dn: _mm(qk_, d_, TN) + _mm(kd_, dn), qk, do_, kd, dsn)
            dqk = _hmap(lambda d_, vn: _mm(d_, vn, NT), do_, v_new)
            dqd = _hmap(lambda d_, s_: _mm(d_, s_, NT), do_, s)
            ds_new = _hmap(lambda qd_, d_, dn, e, w_, dvn: _mm(qd_, d_, TN) + dn * e - _mm(w_, dvn, TN),
                           qd, do_, dsn, egl, w, dv_new)
            dkd = _hmap(lambda vn, dn: _mm(vn, dn, NT), v_new, dsn)
            dgl = _hmap(lambda s_, dn, e: _colsum(_rowsum(s_ * dn)) * e, s, dsn, egl)
            dw = _hmap(lambda dvn, s_: -_mm(dvn, s_, NT), dv_new, s)
            dru = _hmap(lambda t_, a: _mm3(t_, a, TN), tinv, dv_new)
            drw = _hmap(lambda t_, a: _mm3(t_, a, TN), tinv, dw)
            da = _hmap(lambda a, u_, b, w_: jnp.where(c["strict"], -(_mm(a, u_, NT) + _mm(b, w_, NT)), 0.0), dru, u, drw, w)
            m = _hmap(mul, da, dc)
            dkb = _hmap(lambda a, e, m_, k_: a * e + _mm(m_, k_), drw, eg, m, k)
            mq = _hmap(mul, dqk, dc)
            dq = _hmap(lambda mq_, k_, a, e: _mm(mq_, k_) + a * e, mq, k, dqd, eg)
            dk = _hmap(lambda m_, kb_, mq_, q_, a, e, b, be: _mm(m_, kb_, TN) + _mm(mq_, q_, TN) + a * e + b * be,
                       m, kb, mq, q, dkd, ekd, dkb, beta)
            dbeta = _hmap(lambda a, v_, b, k_: _rowsum(a * v_) + _rowsum(b * k_), dru, v, dkb, k)
            pq = _hmap(lambda a, am, b, qk_: a * am + b * qk_, da, amat, dqk, qk)
            ekk = _hmap(lambda a, b: _rowsum(a * b), dkd, kd)
            dgc = _hmap(lambda pq_, a, rw, b, qd_, e, gl_: (
                _rowsum(pq_) - _mmsel(pq_, c["ones"], TN) + (_rowsum(a * rw) + _rowsum(b * qd_) - e)
                + jnp.where(c["row1"] == CHUNK - 1, _colsum(e) + gl_, 0.0)), pq, drw, rhs_w, dqd, qd, ekk, dgl)
            dv = _hmap(mul, dru, beta)
            dyq = _hmap(lambda g_, sl: _through_norm_silu(g_, yq_ref[0, r, sl], GDN_DK ** -0.5), dq, sls)
            dyk = _hmap(lambda g_, sl: _through_norm_silu(g_, yk_ref[0, r, sl], 1.0), dk, sls)
            dyv = _hmap(lambda g_, sl: _through_norm_silu(g_, yv_ref[0, r, sl], None), dv, sls)
            dgs = jnp.zeros((CHUNK, LANES), F32)
            for ih in heads:
                ds_scr[ih] = ds_new[ih]
                dq_ref[0, r, sls[ih]] = dyq[ih]
                dk_ref[0, r, sls[ih]] = dyk[ih]
                dv_ref[0, r, sls[ih]] = dyv[ih]
                dgs = dgs + jnp.where(c["lane"] == ih, dbeta[ih], jnp.where(c["lane"] == ih + 8, dgc[ih], 0.0))
            dgs_ref[0, r, :] = dgs
            return carry

        lax.fori_loop(0, ncb, chunk, 0)

    res = _pcall(
        wrap(body), name="gdn_chunk_bwd", grid=grid,
        in_specs=[sp["wide"](), sp["wide"](), sp["wide"](), sp["gs"](), sp["gr"], sp["wide"](), sp["st"], sp["ti"],
                  ypre_spec(0), ypre_spec(1), ypre_spec(2)] + any_spec,
        out_specs=[sp["wide"](), sp["wide"](), sp["wide"](), sp["gs"]()] + any_spec,
        out_shape=[jax.ShapeDtypeStruct((bsz, seq, D_MODEL), F32)] * 3 + [jax.ShapeDtypeStruct((bsz, seq, LANES), F32)]
        + rider_shapes,
        scratch_shapes=[pltpu.VMEM((hb, GDN_DK, GDN_DK), F32)] + rider_sems,
        compiler_params=_params(("arbitrary", "arbitrary", "arbitrary")),
    )(qn, kn, vv, gs, gr, do, sts, tis, ypre, ypre, ypre, *riders)
    return res[:4], res[4:]


def _gates_bwd(proj, dgs, ddt, dacs, gp, sp, tm):
    t = proj.shape[0]

    def body(i, sm_ref, dgs_ref, ddt_ref, dacs_ref, gp_ref, sp_ref, dsm_ref, dgp_ref, dsp_ref):
        sm = sm_ref[...]
        lane = lax.broadcasted_iota(jnp.int32, sm.shape, 1)
        is_g = (lane >= 8) & (lane < 16)
        is_dt = (lane >= 16) & (lane < 32)
        dgs = dgs_ref[...]
        back = _chunk_cumsum(jnp.where(is_g, dgs, 0.0) + dacs_ref[...], True)
        beta = _sigmoid(sm)
        bias = gp_ref[1:2, :] + sp_ref[1:2, :]
        xb = sm + bias
        soft, dsoft = _softplus(xb), _sigmoid(xb)
        g_neg = -jnp.exp(gp_ref[0:1, :])
        a_neg = -jnp.exp(sp_ref[0:1, :])
        dg = jnp.where(is_g, back * g_neg, 0.0)
        dxb_g = dg * dsoft
        dxb_dt = jnp.where(is_dt, (ddt_ref[...] + back * a_neg) * dsoft, 0.0)
        dsm_ref[...] = (jnp.where(lane < 8, dgs * beta * (1.0 - beta), dxb_g) + dxb_dt).astype(BF16)
        dgp_ref[1:2, :] += _colsum(dxb_g)
        dgp_ref[0:1, :] += _colsum(dg * soft)
        dsp_ref[1:2, :] += _colsum(dxb_dt)
        dsp_ref[0:1, :] += jnp.where(is_dt[0:1, :], _colsum(back * soft) * a_neg, 0.0)

    ins = [("row", proj, LANES, SMALL_CB), ("row", dgs, LANES, 0), ("row", ddt, LANES, 0), ("row", dacs, LANES, 0),
           ("full", gp), ("full", sp)]
    return _rowwise("gates_bwd", body, t, tm, ins, [(LANES, BF16)], accs=[(SUBLANES, LANES), (SUBLANES, LANES)])


def _dh1_first_bwd(dproj, wp_in, x, dx1, w, scatter_riders):
    d = D_MODEL

    def epilogue(dh, x_ref, dx1_ref, w_ref, dx_ref, dw_ref):
        xh, r = _rms(x_ref[...], d)
        dw_ref[...] += _colsum(dh * xh)
        dx_ref[...] = dx1_ref[...] + _rms_bwd(xh, r, dh * w_ref[...], d)

    return _matmul_rows("mm_dh1_first_bwd", dproj, wp_in, "nt", epilogue, [x, dx1], [w], [(d, F32)], accs=[(1, d)],
                        tk=PROJ_W // 2, scatter_riders=scatter_riders)


def _gather_two_level(name, arrays):
    n = len(arrays)
    n_sem = 7

    def body(*refs):
        ins, outs = refs[:n], refs[n:2 * n]
        send_sems, recv_sems, loc_sems = refs[2 * n:]
        x, y, c = lax.axis_index("x"), lax.axis_index("y"), lax.axis_index("c")
        slot = lambda px, py, pc: 4 * px + 2 * py + pc
        sibling = (x, y, 1 - c)
        chips = [(1 - x, y), (x, 1 - y), (1 - x, 1 - y)]

        def copy(t, k, src, block, to):
            return pltpu.make_async_remote_copy(
                src_ref=src, dst_ref=outs[t].at[block], send_sem=send_sems.at[t, k], recv_sem=recv_sems.at[t, k],
                device_id=to, device_id_type=pl.DeviceIdType.MESH)

        own, first, passed = [], [], []
        for t in range(n):
            own.append(pltpu.make_async_copy(ins[t], outs[t].at[slot(x, y, c)], loc_sems.at[t]))
            first.append(copy(t, 0, ins[t], slot(x, y, c), sibling))
            first += [copy(t, 1 + j, ins[t], slot(x, y, c), (px, py, c)) for j, (px, py) in enumerate(chips)]
        for cp in own + first:
            cp.start()
        for t in range(n):
            for j, (px, py) in enumerate(chips):
                copy(t, 1 + j, ins[t], slot(px, py, c), (px, py, c)).wait_recv()
                fwd = copy(t, 4 + j, outs[t].at[slot(px, py, c)], slot(px, py, c), sibling)
                fwd.start()
                passed.append(fwd)
        for t in range(n):
            copy(t, 0, ins[t], slot(x, y, 1 - c), sibling).wait_recv()
            for j, (px, py) in enumerate(chips):
                copy(t, 4 + j, ins[t], slot(px, py, 1 - c), sibling).wait_recv()
        for cp in first + passed:
            cp.wait_send()
        for cp in own:
            cp.wait()

    return _pcall(
        body, name=name,
        in_specs=[pl.BlockSpec(memory_space=pl.ANY)] * n,
        out_specs=[pl.BlockSpec(memory_space=pl.ANY)] * n,
        out_shape=_exchange_out_shapes(arrays, False),
        scratch_shapes=[pltpu.SemaphoreType.DMA((n, n_sem)), pltpu.SemaphoreType.DMA((n, n_sem)), pltpu.SemaphoreType.DMA((n,))],
    )(*arrays)


def _exchange_out_shapes(arrays, scatter):
    return [jax.ShapeDtypeStruct(a.shape if scatter else (N_DEV,) + a.shape, a.dtype) for a in arrays]


def _exchange_sems(n):
    return [pltpu.SemaphoreType.DMA((n, N_DEV - 1)), pltpu.SemaphoreType.DMA((n, N_DEV - 1)), pltpu.SemaphoreType.DMA((n,))]


def _exchange_phase(ins, outs, sems, scatter, start):
    send_sems, recv_sems, loc_sems = sems
    x, y, c = lax.axis_index("x"), lax.axis_index("y"), lax.axis_index("c")
    me = 4 * x + 2 * y + c
    for t in range(len(ins)):
        loc = pltpu.make_async_copy(ins[t].at[me] if scatter else ins[t], outs[t].at[me], loc_sems.at[t])
        if start:
            loc.start()
        else:
            loc.wait()
        for k in range(N_DEV - 1):
            bx, by, bc = ((k + 1) >> 2) & 1, ((k + 1) >> 1) & 1, (k + 1) & 1
            px = 1 - x if bx else x
            py = 1 - y if by else y
            pc = 1 - c if bc else c
            peer = 4 * px + 2 * py + pc
            src = ins[t].at[peer] if scatter else ins[t]
            copy = lambda dst: pltpu.make_async_remote_copy(
                src_ref=src, dst_ref=dst, send_sem=send_sems.at[t, k], recv_sem=recv_sems.at[t, k],
                device_id=(px, py, pc), device_id_type=pl.DeviceIdType.MESH)
            if start:
                copy(outs[t].at[me]).start()
            else:
                copy(outs[t].at[me]).wait_send()
                copy(outs[t].at[peer]).wait_recv()


def _adam_math(w, g, m, v):
    m = ADAM_B1 * m + (1.0 - ADAM_B1) * g
    v = ADAM_B2 * v + (1.0 - ADAM_B2) * (g * g)
    m_hat = m / (1.0 - ADAM_B1 ** ADAM_STEP)
    v_hat = v / (1.0 - ADAM_B2 ** ADAM_STEP)
    delta = -ADAM_LR * (m_hat / (jnp.sqrt(v_hat) + ADAM_EPS) + ADAM_WD * w)
    return delta, m, v


def _adam_big(name, parts, w, m, v, tm):
    r, c = w.shape
    tm = tm if r % tm == 0 else r

    def body(p_ref, w_ref, m_ref, v_ref, g_ref, d_ref, nm_ref, nv_ref):
        g = p_ref[0].astype(F32)
        for s in range(1, N_DEV):
            g = g + p_ref[s].astype(F32)
        g_ref[...] = g
        d_ref[...], nm_ref[...], nv_ref[...] = _adam_math(w_ref[...], g, m_ref[...], v_ref[...])

    blk = lambda: pl.BlockSpec((tm, c), lambda i: (i, 0))
    return _pcall(
        body, name=name, grid=(r // tm,),
        in_specs=[pl.BlockSpec((N_DEV, tm, c), lambda i: (0, i, 0)), blk(), blk(), blk()],
        out_specs=[blk(), blk(), blk(), blk()],
        out_shape=[jax.ShapeDtypeStruct((r, c), F32)] * 4,
        compiler_params=_params(("parallel",)),
    )(parts, w, m, v)


SMALL_ROWS = 56
ROW_DD, ROW_LOSS = 5, 6


def _small_sum(gathered):
    def body(g_ref, o_ref, x_ref):
        s = g_ref[0]
        for dev in range(1, N_DEV):
            s = s + g_ref[dev]
        o_ref[...] = s
        ri = lax.broadcasted_iota(jnp.int32, (D_MODEL, LANES), 0)
        ro = lax.broadcasted_iota(jnp.int32, (D_MODEL, LANES), 1)
        heads = _mmx(jnp.broadcast_to(s[ROW_DD:ROW_DD + 1, :], (SUBLANES, D_MODEL)), (ri // SSD_P == ro).astype(F32))
        loss = _rowsum(jnp.broadcast_to(s[ROW_LOSS:ROW_LOSS + 1, :], (SUBLANES, D_MODEL)))
        row = lax.broadcasted_iota(jnp.int32, (SUBLANES, LANES), 0)
        x_ref[...] = jnp.where(row == 0, heads, jnp.broadcast_to(loss, (SUBLANES, LANES)))

    return _pcall(
        body, name="small_sum",
        out_shape=[jax.ShapeDtypeStruct((SMALL_ROWS, D_MODEL), F32), jax.ShapeDtypeStruct((SUBLANES, LANES), F32)],
        compiler_params=_params(None),
    )(gathered)


def _adam_small(g, w, m, v):
    def body(g_ref, w_ref, m_ref, v_ref, d_ref, nm_ref, nv_ref):
        d_ref[...], nm_ref[...], nv_ref[...] = _adam_math(w_ref[...], g_ref[...], m_ref[...], v_ref[...])

    return _pcall(body, name="adam_small", out_shape=[jax.ShapeDtypeStruct(g.shape, F32)] * 3,
                  compiler_params=_params(None))(g, w, m, v)


def _pack(pieces, rows):
    flat = jnp.concatenate([p.reshape(-1).astype(F32) for p in pieces])
    return jnp.pad(flat, (0, rows * D_MODEL - flat.shape[0])).reshape(rows, D_MODEL)


def _unpack(packed, shapes):
    flat = packed.reshape(-1)
    out, off = [], 0
    for shp in shapes:
        size = 1
        for s in shp:
            size *= s
        out.append(flat[off:off + size].reshape(shp))
        off += size
    return out


def _permute_in(w):
    pad = jnp.zeros((w.shape[0], PROJ_W - D_IN), w.dtype)
    return jnp.concatenate([w[:, 0:4096], w[:, 4112:6672], w[:, 4096:4112], w[:, 6672:6688], pad], axis=1)


def _unpermute_in(g):
    return jnp.concatenate([g[:, 0:4096], g[:, 6656:6672], g[:, 4096:6656], g[:, 6672:6688]], axis=1)


def _lane_row(vec, start):
    return jnp.zeros((LANES,), F32).at[start:start + vec.shape[0]].set(vec)


def _cols_from_shards(g):
    return jnp.transpose(g, (1, 0, 2)).reshape(g.shape[1], N_DEV * g.shape[2])


def _cols_to_shards(a):
    return jnp.transpose(a.astype(BF16).reshape(a.shape[0], N_DEV, a.shape[1] // N_DEV), (1, 0, 2))


def _rows_to_shards(a):
    return a.astype(BF16).reshape(N_DEV, a.shape[0] // N_DEV, a.shape[1])


def _local_step(x, tgt, wp_in, rest, p, rest_is_sharded):
    bsz, seq, d = x.shape
    t = bsz * seq
    x2 = x.reshape(t, d)
    tgt2 = tgt.reshape(t, d)
    tm = min(256, seq)
    tm_big = min(512, seq)
    tm_wide = min(256, seq)
    sb = min(512, seq)

    gp = jnp.zeros((SUBLANES, LANES), F32).at[0].set(_lane_row(p["gdn_a_log"], 8)).at[1].set(_lane_row(p["gdn_dt_bias"], 8))
    sp = jnp.zeros((SUBLANES, LANES), F32).at[0].set(_lane_row(p["ssd_a_log"], 16)).at[1].set(_lane_row(p["ssd_dt_bias"], 16))
    dvec = jnp.repeat(p["ssd_d"], SSD_P).reshape(1, d)
    row = lambda v: v.reshape(1, -1)
    pre_mix, post_mix, pre_ffn, post_ffn = (row(p[k]) for k in ("pre_mix_norm", "post_mix_norm", "pre_ffn_norm", "post_ffn_norm"))
    gnw, snw = row(p["gdn_norm_w"]), row(p["ssd_norm_w"])
    gcw, scw, scb, fcw, fcb = p["gdn_conv_w"], p["ssd_conv_w"], row(p["ssd_conv_b"]), p["ffn_conv_w"], row(p["ffn_conv_b"])

    h1, proj = _norm_proj(x2, pre_mix, wp_in, tm=2048, tn=512)
    b3 = lambda a: a.reshape(bsz, seq, a.shape[-1])
    b2 = lambda a: a.reshape(t, a.shape[-1])
    rows_of = lambda a, lo, n: jnp.transpose(a[:, lo:lo + n].reshape(bsz, seq // CHUNK, CHUNK, n), (0, 1, 3, 2))
    qn, kn, vv, gs, ypre_gdn = _gdn_prep(proj, gcw, gp, seq, tm_big)
    gr = rows_of(gs, 8, GDN_HEADS)
    qn, kn, vv, gs = b3(qn), b3(kn), b3(vv), b3(gs)
    (o_gdn, gdn_st, gdn_ti), gathered = _gdn_chunk_fwd(qn, kn, vv, gs, gr, bsz, seq, sb, list(rest) if rest_is_sharded else [])
    if rest_is_sharded:
        w_out, w_up, w_down = gathered[0].reshape(-1, d), _cols_from_shards(gathered[1]), gathered[2].reshape(-1, d)
    else:
        w_out, w_up, w_down = rest
    o_gdn = b2(o_gdn)
    xs, bc, dtx, acsx, acs, ypre_ssd = _ssd_prep(proj, scw, scb, sp, seq, tm)
    ar = rows_of(acs, 16, SSD_HEADS)
    y_ssd, ssd_st = _ssd_chunk_fwd(b3(xs), b3(bc), b3(dtx), b3(acsx), b3(acs), ar, bsz, seq, sb)
    y_ssd = b2(y_ssd)
    mixin = _gate_norm(o_gdn, y_ssd, xs, proj, gnw, snw, dvec, tm_big)
    mix, x1, h2 = _out_mid(mixin, w_out, x2, post_mix, pre_ffn)
    u_pre = _matmul("mm_up", h2, w_up, "nn", F32, tm=2048)
    act, u = _ffn_act(u_pre, fcw, fcb, seq, tm_wide)
    dy, df, loss_lanes, d_post_ffn = _down_final(act, w_down, x1, tgt2, post_ffn)

    g_down = _matmul("mm_dw_down", act, df, "tn", BF16, tm=1408, tk=2048)
    dact = _matmul("mm_dact", df, w_down, "nt", F32, tm=2048, tn=1408)
    du_pre, d_fcw, d_fcb = _ffn_bwd(u, u_pre, dact, fcw, seq, tm_wide)
    g_up = _matmul("mm_dw_up", h2, du_pre, "tn", BF16, tn=512, tk=4096)
    dx1, dmix, d_post_mix, d_pre_ffn = _dh2_mid_bwd(du_pre, w_up, x1, mix, dy, post_mix, pre_ffn)
    g_out = _matmul("mm_dw_out", mixin, dmix, "tn", BF16, tk=2048)
    do_gdn, dza, dy_ssd, dxs_d, dzs, d_gnw, d_snw, d_dd = _dmixin_gate_norm_bwd(dmix, w_out, o_gdn, y_ssd, xs, proj, gnw, snw, dvec)
    dyx, dybc, ddt, dacs, d_scb_x, d_scb_bc = _ssd_chunk_bwd(
        b3(xs), b3(bc), b3(dtx), b3(acsx), b3(acs), ar, b3(dy_ssd), ssd_st, b3(ypre_ssd), b3(dxs_d), bsz, seq, sb)
    dyx, dybc, ddt, dacs = b2(dyx), b2(dybc), b2(ddt), b2(dacs)
    d_scb = jnp.concatenate([d_scb_x, d_scb_bc], axis=1)
    riders = [_rows_to_shards(g_out), _cols_to_shards(g_up), _rows_to_shards(g_down)] if rest_is_sharded else []
    dgdn, received = _gdn_chunk_bwd(qn, kn, vv, gs, gr, b3(do_gdn), gdn_st, gdn_ti, b3(ypre_gdn), bsz, seq, min(256, seq), riders)
    if rest_is_sharded:
        g_out, g_up, g_down = received
    dyq, dyk, dyv, dgs = (b2(a) for a in dgdn)
    dsm, d_gp, d_sp = _gates_bwd(proj, dgs, ddt, dacs, gp, sp, tm)
    dproj, d_gcw, d_scw = _assemble_dproj((dyq, dyk, dyv), dza, dzs, (dyx, dybc), dsm, proj, gcw, scw, seq, tm)
    g_in = _matmul("mm_dw_in", h1, dproj, "tn", BF16, tk=4096)
    if rest_is_sharded:
        (dx, d_pre_mix), (g_in,) = _dh1_first_bwd(dproj, wp_in, x2, dx1, pre_mix, [_cols_to_shards(_unpermute_in(g_in))])
    else:
        dx, d_pre_mix = _dh1_first_bwd(dproj, wp_in, x2, dx1, pre_mix, [])

    small = dict(pre_mix_norm=d_pre_mix, ssd_norm_w=d_snw, post_mix_norm=d_post_mix, pre_ffn_norm=d_pre_ffn,
                 post_ffn_norm=d_post_ffn, dd_lanes=d_dd, loss_lanes=loss_lanes, gdn_gates=d_gp, ssd_gates=d_sp,
                 gdn_norm_w=d_gnw, gdn_conv_w=d_gcw[0:4], ssd_conv_w=d_scw[0:4], ssd_conv_b=d_scb,
                 ffn_conv_w=d_fcw[0:3], ffn_conv_b=d_fcb)
    return dx.reshape(bsz, seq, d), g_in, g_out, g_up, g_down, small


def kernel(x, pre_mix_norm, w_in, gdn_conv_w, gdn_a_log, gdn_dt_bias, gdn_norm_w, ssd_conv_w, ssd_conv_b, ssd_a_log, ssd_dt_bias, ssd_d, ssd_norm_w, w_out, post_mix_norm, pre_ffn_norm, w_up, ffn_conv_w, ffn_conv_b, w_down, post_ffn_norm, loss_target, m_pre_mix_norm, m_w_in, m_gdn_conv_w, m_gdn_a_log, m_gdn_dt_bias, m_gdn_norm_w, m_ssd_conv_w, m_ssd_conv_b, m_ssd_a_log, m_ssd_dt_bias, m_ssd_d, m_ssd_norm_w, m_w_out, m_post_mix_norm, m_pre_ffn_norm, m_w_up, m_ffn_conv_w, m_ffn_conv_b, m_w_down, m_post_ffn_norm, v_pre_mix_norm, v_w_in, v_gdn_conv_w, v_gdn_a_log, v_gdn_dt_bias, v_gdn_norm_w, v_ssd_conv_w, v_ssd_conv_b, v_ssd_a_log, v_ssd_dt_bias, v_ssd_d, v_ssd_norm_w, v_w_out, v_post_mix_norm, v_pre_ffn_norm, v_w_up, v_ffn_conv_w, v_ffn_conv_b, v_w_down, v_post_ffn_norm):
    names = ["pre_mix_norm", "w_in", "gdn_conv_w", "gdn_a_log", "gdn_dt_bias", "gdn_norm_w", "ssd_conv_w", "ssd_conv_b",
             "ssd_a_log", "ssd_dt_bias", "ssd_d", "ssd_norm_w", "w_out", "post_mix_norm", "pre_ffn_norm", "w_up",
             "ffn_conv_w", "ffn_conv_b", "w_down", "post_ffn_norm"]
    w_args = [pre_mix_norm, w_in, gdn_conv_w, gdn_a_log, gdn_dt_bias, gdn_norm_w, ssd_conv_w, ssd_conv_b, ssd_a_log, ssd_dt_bias, ssd_d, ssd_norm_w, w_out, post_mix_norm, pre_ffn_norm, w_up, ffn_conv_w, ffn_conv_b, w_down, post_ffn_norm]
    m_args = [m_pre_mix_norm, m_w_in, m_gdn_conv_w, m_gdn_a_log, m_gdn_dt_bias, m_gdn_norm_w, m_ssd_conv_w, m_ssd_conv_b, m_ssd_a_log, m_ssd_dt_bias, m_ssd_d, m_ssd_norm_w, m_w_out, m_post_mix_norm, m_pre_ffn_norm, m_w_up, m_ffn_conv_w, m_ffn_conv_b, m_w_down, m_post_ffn_norm]
    v_args = [v_pre_mix_norm, v_w_in, v_gdn_conv_w, v_gdn_a_log, v_gdn_dt_bias, v_gdn_norm_w, v_ssd_conv_w, v_ssd_conv_b, v_ssd_a_log, v_ssd_dt_bias, v_ssd_d, v_ssd_norm_w, v_w_out, v_post_mix_norm, v_pre_ffn_norm, v_w_up, v_ffn_conv_w, v_ffn_conv_b, v_w_down, v_post_ffn_norm]
    w = {k: a[0] for k, a in zip(names, w_args)}
    m = {k: a[0] for k, a in zip(names, m_args)}
    v = {k: a[0] for k, a in zip(names, v_args)}
    idx = 4 * lax.axis_index("x") + 2 * lax.axis_index("y") + lax.axis_index("c")
    big = ("w_in", "w_out", "w_up", "w_down")
    conv = ("gdn_conv_w", "ssd_conv_w", "ffn_conv_w")

    conv_local = jnp.concatenate([jnp.pad(w[k], ((0, 4 - w[k].shape[0]), (0, 0))) for k in conv], axis=1)
    g_in, g_conv = _gather_two_level("gather_weights", [w["w_in"].astype(BF16), conv_local])
    wp_in = _permute_in(_cols_from_shards(g_in))
    p = {k: w[k] for k in names if k not in big and k not in conv}
    off = 0
    for k in conv:
        cw = w[k].shape[1]
        p[k] = jnp.transpose(g_conv[:, :w[k].shape[0], off:off + cw], (1, 0, 2)).reshape(w[k].shape[0], N_DEV * cw)
        off += cw

    rest = tuple(w[k].astype(BF16) for k in ("w_out", "w_up", "w_down"))
    dx, p_in, p_out, p_up, p_down, small = _local_step(x, loss_target, wp_in, rest, p, True)

    gate_row = jnp.concatenate([small["gdn_gates"][0], small["gdn_gates"][1], small["ssd_gates"][0], small["ssd_gates"][1],
                                small["gdn_norm_w"][0], jnp.zeros((D_MODEL - 5 * LANES,), F32)]).reshape(1, D_MODEL)
    pack = _pack([small["pre_mix_norm"], small["ssd_norm_w"], small["post_mix_norm"], small["pre_ffn_norm"],
                  small["post_ffn_norm"], small["dd_lanes"], small["loss_lanes"], gate_row,
                  small["gdn_conv_w"], small["ssd_conv_w"], jnp.pad(small["ssd_conv_b"], ((0, 0), (0, 512))),
                  jnp.pad(small["ffn_conv_w"].reshape(-1), (0, 17 * D_MODEL - 3 * 2 * D_FF)),
                  jnp.pad(small["ffn_conv_b"], ((0, 0), (0, 512)))], SMALL_ROWS)
    (pack_all,) = _gather_two_level("gather_small", [pack])
    ssum, extra = _small_sum(pack_all)

    grads, deltas, new_m, new_v = {}, {}, {}, {}
    for k, parts in (("w_in", p_in), ("w_out", p_out), ("w_up", p_up), ("w_down", p_down)):
        grads[k], deltas[k], new_m[k], new_v[k] = _adam_big("adam_" + k, parts, w[k], m[k], v[k], 256)

    flat = ssum.reshape(-1)
    gate = ssum[7]
    sg = dict(pre_mix_norm=ssum[0], ssd_norm_w=ssum[1], post_mix_norm=ssum[2], pre_ffn_norm=ssum[3], post_ffn_norm=ssum[4],
              gdn_a_log=gate[8:16], gdn_dt_bias=gate[LANES + 8:LANES + 16], ssd_a_log=gate[2 * LANES + 16:2 * LANES + 32],
              ssd_dt_bias=gate[3 * LANES + 16:3 * LANES + 32], gdn_norm_w=gate[4 * LANES:5 * LANES], ssd_d=extra[0, 0:SSD_HEADS])
    o = 8 * D_MODEL
    full_gcw = flat[o:o + 4 * 3072].reshape(4, 3072)
    o += 12 * D_MODEL
    full_scw = flat[o:o + 4 * 1536].reshape(4, 1536)
    o += 6 * D_MODEL
    sg["ssd_conv_b"] = flat[o:o + 1536]
    o += 2 * D_MODEL
    full_fcw = flat[o:o + 3 * 2 * D_FF].reshape(3, 2 * D_FF)
    o += 17 * D_MODEL
    sg["ffn_conv_b"] = flat[o:o + 2 * D_FF]
    for k, full in (("gdn_conv_w", full_gcw), ("ssd_conv_w", full_scw), ("ffn_conv_w", full_fcw)):
        cw = w[k].shape[1]
        sg[k] = lax.dynamic_slice_in_dim(full, idx * cw, cw, axis=1)
    small_names = [k for k in names if k not in big]
    rows = 24
    gpk = _pack([sg[k] for k in small_names], rows)
    dpk, mpk, vpk = _adam_small(gpk, _pack([w[k] for k in small_names], rows), _pack([m[k] for k in small_names], rows),
                                _pack([v[k] for k in small_names], rows))
    shapes = [w[k].shape for k in small_names]
    for k, g_, d_, m_, v_ in zip(small_names, _unpack(gpk, shapes), _unpack(dpk, shapes), _unpack(mpk, shapes), _unpack(vpk, shapes)):
        grads[k], deltas[k], new_m[k], new_v[k] = g_, d_, m_, v_

    loss = extra[1, 0]
    lead = lambda a: a[None]
    return (loss, dx, *[lead(grads[k]) for k in names], *[lead(deltas[k]) for k in names],
            *[lead(new_m[k]) for k in names], *[lead(new_v[k]) for k in names])
```
